```python
import math
import jax, jax.numpy as jnp
from jax import lax
import numpy as np

D_MODEL = 1024
BATCH = 8
SEQ = 4096
DEPTH = 2

MLA_HEADS = 8
MLA_Q_LORA = 256
MLA_KV_LORA = 256
MLA_NOPE = 64
MLA_ROPE = 32
MLA_V = 64
MLA_SCALE = (MLA_NOPE + MLA_ROPE) ** -0.5
ROPE_BASE = 10000.0
Q_BLOCK = 128
MAX_POS_OFFSET = 1024

SGU_GROUPS = 4
SGU_GROUP_DIM = 128
SGU_DIM = SGU_GROUPS * SGU_GROUP_DIM
SGU_CHUNK = 128

EVEN_IN = MLA_Q_LORA + MLA_KV_LORA + MLA_ROPE + 2 * SGU_DIM
EVEN_SPLITS = (MLA_Q_LORA,
               MLA_Q_LORA + MLA_KV_LORA,
               MLA_Q_LORA + MLA_KV_LORA + MLA_ROPE,
               MLA_Q_LORA + MLA_KV_LORA + MLA_ROPE + SGU_DIM)
EVEN_MIX = MLA_HEADS * MLA_V + SGU_DIM

HG_HEADS = 8
HG_DK = 128
HG_DV = D_MODEL // HG_HEADS
HG_KEY_DIM = HG_HEADS * HG_DK
HG_VAL_DIM = HG_HEADS * HG_DV
HG_CHUNK = 64
ODD_IN = 2 * HG_KEY_DIM + 2 * HG_VAL_DIM
ODD_SPLITS = (HG_KEY_DIM, 2 * HG_KEY_DIM, 2 * HG_KEY_DIM + HG_VAL_DIM)

D_FF = 4 * D_MODEL

N_EVEN = (DEPTH + 1) // 2
N_ODD = DEPTH // 2
DN_ALPHA = (2 * DEPTH) ** 0.25
DN_BETA = (8 * DEPTH) ** -0.25
NORM_EPS = 1e-5

kernel_name = 'hybrid_mla_sgu_hgrn2_deepnorm'


def layer_norm(x, g, b):
    xf = x.astype(jnp.float32)
    mu = jnp.mean(xf, -1, keepdims=True)
    var = jnp.mean(jnp.square(xf - mu), -1, keepdims=True)
    y = (xf - mu) * lax.rsqrt(var + NORM_EPS)
    return (y * g.astype(jnp.float32) + b.astype(jnp.float32)).astype(x.dtype)


def rms_norm(x, g):
    xf = x.astype(jnp.float32)
    y = xf * lax.rsqrt(jnp.mean(jnp.square(xf), -1, keepdims=True) + NORM_EPS)
    return (y * g.astype(jnp.float32)).astype(x.dtype)


def apply_rope(x, cos, sin):
    half = MLA_ROPE // 2
    xf = x.astype(jnp.float32)
    x1, x2 = xf[..., :half], xf[..., half:]
    out = jnp.concatenate([x1 * cos - x2 * sin, x2 * cos + x1 * sin], axis=-1)
    return out.astype(x.dtype)


def mla(c_q, c_kv, k_rope, positions, g_q, g_kv, w_qb, w_kvb):
    B, S, _ = c_q.shape
    q = (rms_norm(c_q, g_q) @ w_qb).reshape(B, S, MLA_HEADS, MLA_NOPE + MLA_ROPE)
    q_nope, q_rope = q[..., :MLA_NOPE], q[..., MLA_NOPE:]
    kv = (rms_norm(c_kv, g_kv) @ w_kvb).reshape(B, S, MLA_HEADS, MLA_NOPE + MLA_V)
    k_nope, v = kv[..., :MLA_NOPE], kv[..., MLA_NOPE:]
    half = MLA_ROPE // 2
    inv_freq = ROPE_BASE ** (-jnp.arange(half, dtype=jnp.float32) / half)
    ang = positions.astype(jnp.float32)[..., None] * inv_freq
    cos, sin = jnp.cos(ang), jnp.sin(ang)
    q_rope = apply_rope(q_rope, cos[:, :, None, :], sin[:, :, None, :])
    k_rope = apply_rope(k_rope, cos, sin)
    nb = S // Q_BLOCK
    qn_b = q_nope.reshape(B, nb, Q_BLOCK, MLA_HEADS, MLA_NOPE).transpose(1, 0, 2, 3, 4)
    qr_b = q_rope.reshape(B, nb, Q_BLOCK, MLA_HEADS, MLA_ROPE).transpose(1, 0, 2, 3, 4)
    key_idx = jnp.arange(S)

    def block(args):
        qn, qr, bi = args
        s = (jnp.einsum('bqhd,bkhd->bhqk', qn, k_nope)
             + jnp.einsum('bqhr,bkr->bhqk', qr, k_rope)).astype(jnp.float32) * MLA_SCALE
        q_idx = bi * Q_BLOCK + jnp.arange(Q_BLOCK)
        mask = key_idx[None, :] <= q_idx[:, None]
        s = jnp.where(mask[None, None], s, -jnp.inf)
        p = jax.nn.softmax(s, axis=-1).astype(v.dtype)
        return jnp.einsum('bhqk,bkhd->bqhd', p, v)

    out = lax.map(block, (qn_b, qr_b, jnp.arange(nb)))
    return out.transpose(1, 0, 2, 3, 4).reshape(B, S, MLA_HEADS * MLA_V)


def sgu(u, v, ln_g, ln_b, w_s, b_s):
    B, S, _ = u.shape
    u = jax.nn.gelu(u)
    v = layer_norm(jax.nn.gelu(v), ln_g, ln_b)
    nc = S // SGU_CHUNK
    vc = v.reshape(B, nc, SGU_CHUNK, SGU_GROUPS, SGU_GROUP_DIM)
    causal = jnp.tril(jnp.ones((SGU_CHUNK, SGU_CHUNK), dtype=bool))
    w = jnp.where(causal[None], w_s, jnp.zeros_like(w_s))
    mixed = jnp.einsum('gts,bnsgc->bntgc', w, vc) + b_s.T[:, :, None]
    return u * mixed.reshape(B, S, SGU_DIM)


def hgrn2(q, f, i, g, lb, g_norm):
    B, S, _ = q.shape
    nc = S // HG_CHUNK
    qf = jax.nn.silu(q.astype(jnp.float32))
    gate = lb + (1.0 - lb) * jax.nn.sigmoid(f.astype(jnp.float32))
    k = 1.0 - gate
    log_g = jnp.log(gate)
    vf = i.astype(jnp.float32)

    def chunks(t, d):
        return t.reshape(B, nc, HG_CHUNK, HG_HEADS, d).transpose(1, 0, 3, 2, 4)

    xs = (chunks(qf, HG_DK), chunks(k, HG_DK), chunks(vf, HG_DV), chunks(log_g, HG_DK))
    tri = jnp.tril(jnp.ones((HG_CHUNK, HG_CHUNK), dtype=bool))[:, :, None]

    def step(state, inp):
        qc, kc, vc, lg = inp
        bcum = jnp.cumsum(lg, axis=2)
        diff = bcum[:, :, :, None, :] - bcum[:, :, None, :, :]
        decay = jnp.exp(jnp.where(tri, diff, -jnp.inf))
        attn = jnp.einsum('bhtd,bhsd,bhtsd->bhts', qc, kc, decay)
        o = (jnp.einsum('bhts,bhsv->bhtv', attn, vc)
             + jnp.einsum('bhtd,bhdv->bhtv', qc * jnp.exp(bcum), state))
        b_last = bcum[:, :, -1:, :]
        k_dec = kc * jnp.exp(b_last - bcum)
        new_state = (jnp.exp(b_last[:, :, 0, :])[..., None] * state
                     + jnp.einsum('bhsd,bhsv->bhdv', k_dec, vc))
        return new_state, o

    state0 = jnp.zeros((B, HG_HEADS, HG_DK, HG_DV), jnp.float32)
    _, o = lax.scan(step, state0, xs)
    o = o.transpose(1, 0, 3, 2, 4).reshape(B, S, HG_HEADS, HG_DV)
    o = o * lax.rsqrt(jnp.mean(jnp.square(o), -1, keepdims=True) + NORM_EPS)
    o = o * g_norm.astype(jnp.float32).reshape(HG_HEADS, HG_DV)
    o = o * jax.nn.silu(g.astype(jnp.float32).reshape(B, S, HG_HEADS, HG_DV))
    return o.reshape(B, S, HG_VAL_DIM).astype(i.dtype)


def _fwd_setup_inputs(seed: int = 0) -> dict:
    key = jax.random.key(seed)
    ks = jax.random.split(key, 24)

    def nrm(k, shape, scale):
        return jax.random.normal(k, shape, jnp.float32) * scale

    def gain(k, shape):
        return 1.0 + 0.01 * jax.random.normal(k, shape, jnp.float32)

    x = jax.random.normal(ks[0], (BATCH, SEQ, D_MODEL), jnp.float32)
    offs = jax.random.randint(ks[1], (BATCH, 1), 0, MAX_POS_OFFSET, dtype=jnp.int32)
    positions = (offs + jnp.arange(SEQ, dtype=jnp.int32)[None, :]).astype(jnp.int32)
    return {
        'x': x,
        'positions': positions,
        'w_in_e': nrm(ks[2], (N_EVEN, D_MODEL, EVEN_IN), D_MODEL ** -0.5),
        'mla_gq': gain(ks[3], (N_EVEN, MLA_Q_LORA)),
        'mla_gkv': gain(ks[4], (N_EVEN, MLA_KV_LORA)),
        'w_qb': nrm(ks[5], (N_EVEN, MLA_Q_LORA, MLA_HEADS * (MLA_NOPE + MLA_ROPE)), MLA_Q_LORA ** -0.5),
        'w_kvb': nrm(ks[6], (N_EVEN, MLA_KV_LORA, MLA_HEADS * (MLA_NOPE + MLA_V)), MLA_KV_LORA ** -0.5),
        'sgu_ln_g': gain(ks[7], (N_EVEN, SGU_DIM)),
        'sgu_ln_b': nrm(ks[8], (N_EVEN, SGU_DIM), 0.01),
        'sgu_w': nrm(ks[9], (N_EVEN, SGU_GROUPS, SGU_CHUNK, SGU_CHUNK), SGU_CHUNK ** -0.5),
        'sgu_b': gain(ks[10], (N_EVEN, SGU_GROUPS, SGU_CHUNK)),
        'w_out_e': nrm(ks[11], (N_EVEN, EVEN_MIX, D_MODEL), DN_BETA * EVEN_MIX ** -0.5),
        'w_in_o': nrm(ks[12], (N_ODD, D_MODEL, ODD_IN), D_MODEL ** -0.5),
        'hg_lb': nrm(ks[13], (DEPTH, HG_KEY_DIM), 0.1),
        'hg_gnorm': gain(ks[14], (N_ODD, HG_VAL_DIM)),
        'w_out_o': nrm(ks[15], (N_ODD, HG_VAL_DIM, D_MODEL), DN_BETA * HG_VAL_DIM ** -0.5),
        'ln1_g': gain(ks[16], (DEPTH, D_MODEL)),
        'ln1_b': nrm(ks[17], (DEPTH, D_MODEL), 0.01),
        'w_ff1': nrm(ks[18], (DEPTH, D_MODEL, D_FF), DN_BETA * D_MODEL ** -0.5),
        'w_ff2': nrm(ks[19], (DEPTH, D_FF, D_MODEL), DN_BETA * D_FF ** -0.5),
        'ln2_g': gain(ks[20], (DEPTH, D_MODEL)),
        'ln2_b': nrm(ks[21], (DEPTH, D_MODEL), 0.01),
    }


def _fwd_reference(x, positions, w_in_e, mla_gq, mla_gkv, w_qb, w_kvb, sgu_ln_g, sgu_ln_b,
              sgu_w, sgu_b, w_out_e, w_in_o, hg_lb, hg_gnorm, w_out_o,
              ln1_g, ln1_b, w_ff1, w_ff2, ln2_g, ln2_b):
    lb_sm = jax.nn.softmax(hg_lb.astype(jnp.float32), axis=0)
    lb_all = jnp.cumsum(lb_sm, axis=0) - lb_sm[0:1]
    h = x
    for l in range(DEPTH):
        if l % 2 == 0:
            e = l // 2
            z = h @ w_in_e[e]
            c_q, c_kv, k_r, u, v = jnp.split(z, EVEN_SPLITS, axis=-1)
            a_out = mla(c_q, c_kv, k_r, positions, mla_gq[e], mla_gkv[e], w_qb[e], w_kvb[e])
            b_out = sgu(u, v, sgu_ln_g[e], sgu_ln_b[e], sgu_w[e], sgu_b[e])
            mix = jnp.concatenate([a_out, b_out], axis=-1) @ w_out_e[e]
        else:
            o = l // 2
            z = h @ w_in_o[o]
            q, f, i, g = jnp.split(z, ODD_SPLITS, axis=-1)
            mix = hgrn2(q, f, i, g, lb_all[l], hg_gnorm[o]) @ w_out_o[o]
        h = layer_norm(DN_ALPHA * h + mix, ln1_g[l], ln1_b[l])
        ff = jnp.square(jax.nn.relu(h @ w_ff1[l])) @ w_ff2[l]
        h = layer_norm(DN_ALPHA * h + ff, ln2_g[l], ln2_b[l])
    return h


import jax as _jax
import jax.numpy as _jnp

TWIN_FORMAT = 'train_step'
FWD_PARAMS = ['x', 'positions', 'w_in_e', 'mla_gq', 'mla_gkv', 'w_qb', 'w_kvb', 'sgu_ln_g', 'sgu_ln_b', 'sgu_w', 'sgu_b', 'w_out_e', 'w_in_o', 'hg_lb', 'hg_gnorm', 'w_out_o', 'ln1_g', 'ln1_b', 'w_ff1', 'w_ff2', 'ln2_g', 'ln2_b']
TWIN_WEIGHTS = ['w_in_e', 'mla_gq', 'mla_gkv', 'w_qb', 'w_kvb', 'sgu_ln_g', 'sgu_ln_b', 'sgu_w', 'sgu_b', 'w_out_e', 'w_in_o', 'hg_lb', 'hg_gnorm', 'w_out_o', 'ln1_g', 'ln1_b', 'w_ff1', 'w_ff2', 'ln2_g', 'ln2_b']
TWIN_DIFF_INPUT = 'x'
TWIN_INPUTS = ['x', 'positions', 'w_in_e', 'mla_gq', 'mla_gkv', 'w_qb', 'w_kvb', 'sgu_ln_g', 'sgu_ln_b', 'sgu_w', 'sgu_b', 'w_out_e', 'w_in_o', 'hg_lb', 'hg_gnorm', 'w_out_o', 'ln1_g', 'ln1_b', 'w_ff1', 'w_ff2', 'ln2_g', 'ln2_b', 'loss_target', 'm_w_in_e', 'm_mla_gq', 'm_mla_gkv', 'm_w_qb', 'm_w_kvb', 'm_sgu_ln_g', 'm_sgu_ln_b', 'm_sgu_w', 'm_sgu_b', 'm_w_out_e', 'm_w_in_o', 'm_hg_lb', 'm_hg_gnorm', 'm_w_out_o', 'm_ln1_g', 'm_ln1_b', 'm_w_ff1', 'm_w_ff2', 'm_ln2_g', 'm_ln2_b', 'v_w_in_e', 'v_mla_gq', 'v_mla_gkv', 'v_w_qb', 'v_w_kvb', 'v_sgu_ln_g', 'v_sgu_ln_b', 'v_sgu_w', 'v_sgu_b', 'v_w_out_e', 'v_w_in_o', 'v_hg_lb', 'v_hg_gnorm', 'v_w_out_o', 'v_ln1_g', 'v_ln1_b', 'v_w_ff1', 'v_w_ff2', 'v_ln2_g', 'v_ln2_b']
TWIN_OUTPUTS = ['loss', 'grad_x', 'grad_w_in_e', 'grad_mla_gq', 'grad_mla_gkv', 'grad_w_qb', 'grad_w_kvb', 'grad_sgu_ln_g', 'grad_sgu_ln_b', 'grad_sgu_w', 'grad_sgu_b', 'grad_w_out_e', 'grad_w_in_o', 'grad_hg_lb', 'grad_hg_gnorm', 'grad_w_out_o', 'grad_ln1_g', 'grad_ln1_b', 'grad_w_ff1', 'grad_w_ff2', 'grad_ln2_g', 'grad_ln2_b', 'delta_w_in_e', 'delta_mla_gq', 'delta_mla_gkv', 'delta_w_qb', 'delta_w_kvb', 'delta_sgu_ln_g', 'delta_sgu_ln_b', 'delta_sgu_w', 'delta_sgu_b', 'delta_w_out_e', 'delta_w_in_o', 'delta_hg_lb', 'delta_hg_gnorm', 'delta_w_out_o', 'delta_ln1_g', 'delta_ln1_b', 'delta_w_ff1', 'delta_w_ff2', 'delta_ln2_g', 'delta_ln2_b', 'new_m_w_in_e', 'new_m_mla_gq', 'new_m_mla_gkv', 'new_m_w_qb', 'new_m_w_kvb', 'new_m_sgu_ln_g', 'new_m_sgu_ln_b', 'new_m_sgu_w', 'new_m_sgu_b', 'new_m_w_out_e', 'new_m_w_in_o', 'new_m_hg_lb', 'new_m_hg_gnorm', 'new_m_w_out_o', 'new_m_ln1_g', 'new_m_ln1_b', 'new_m_w_ff1', 'new_m_w_ff2', 'new_m_ln2_g', 'new_m_ln2_b', 'new_v_w_in_e', 'new_v_mla_gq', 'new_v_mla_gkv', 'new_v_w_qb', 'new_v_w_kvb', 'new_v_sgu_ln_g', 'new_v_sgu_ln_b', 'new_v_sgu_w', 'new_v_sgu_b', 'new_v_w_out_e', 'new_v_w_in_o', 'new_v_hg_lb', 'new_v_hg_gnorm', 'new_v_w_out_o', 'new_v_ln1_g', 'new_v_ln1_b', 'new_v_w_ff1', 'new_v_w_ff2', 'new_v_ln2_g', 'new_v_ln2_b']
TWIN_LEAF_KINDS = {'loss': 'loss', 'grad_x': 'grad_x', 'grad_w_in_e': 'grad_w', 'grad_mla_gq': 'grad_w', 'grad_mla_gkv': 'grad_w', 'grad_w_qb': 'grad_w', 'grad_w_kvb': 'grad_w', 'grad_sgu_ln_g': 'grad_w', 'grad_sgu_ln_b': 'grad_w', 'grad_sgu_w': 'grad_w', 'grad_sgu_b': 'grad_w', 'grad_w_out_e': 'grad_w', 'grad_w_in_o': 'grad_w', 'grad_hg_lb': 'grad_w', 'grad_hg_gnorm': 'grad_w', 'grad_w_out_o': 'grad_w', 'grad_ln1_g': 'grad_w', 'grad_ln1_b': 'grad_w', 'grad_w_ff1': 'grad_w', 'grad_w_ff2': 'grad_w', 'grad_ln2_g': 'grad_w', 'grad_ln2_b': 'grad_w', 'delta_w_in_e': 'delta_w', 'delta_mla_gq': 'delta_w', 'delta_mla_gkv': 'delta_w', 'delta_w_qb': 'delta_w', 'delta_w_kvb': 'delta_w', 'delta_sgu_ln_g': 'delta_w', 'delta_sgu_ln_b': 'delta_w', 'delta_sgu_w': 'delta_w', 'delta_sgu_b': 'delta_w', 'delta_w_out_e': 'delta_w', 'delta_w_in_o': 'delta_w', 'delta_hg_lb': 'delta_w', 'delta_hg_gnorm': 'delta_w', 'delta_w_out_o': 'delta_w', 'delta_ln1_g': 'delta_w', 'delta_ln1_b': 'delta_w', 'delta_w_ff1': 'delta_w', 'delta_w_ff2': 'delta_w', 'delta_ln2_g': 'delta_w', 'delta_ln2_b': 'delta_w', 'new_m_w_in_e': 'new_m', 'new_m_mla_gq': 'new_m', 'new_m_mla_gkv': 'new_m', 'new_m_w_qb': 'new_m', 'new_m_w_kvb': 'new_m', 'new_m_sgu_ln_g': 'new_m', 'new_m_sgu_ln_b': 'new_m', 'new_m_sgu_w': 'new_m', 'new_m_sgu_b': 'new_m', 'new_m_w_out_e': 'new_m', 'new_m_w_in_o': 'new_m', 'new_m_hg_lb': 'new_m', 'new_m_hg_gnorm': 'new_m', 'new_m_w_out_o': 'new_m', 'new_m_ln1_g': 'new_m', 'new_m_ln1_b': 'new_m', 'new_m_w_ff1': 'new_m', 'new_m_w_ff2': 'new_m', 'new_m_ln2_g': 'new_m', 'new_m_ln2_b': 'new_m', 'new_v_w_in_e': 'new_v', 'new_v_mla_gq': 'new_v', 'new_v_mla_gkv': 'new_v', 'new_v_w_qb': 'new_v', 'new_v_w_kvb': 'new_v', 'new_v_sgu_ln_g': 'new_v', 'new_v_sgu_ln_b': 'new_v', 'new_v_sgu_w': 'new_v', 'new_v_sgu_b': 'new_v', 'new_v_w_out_e': 'new_v', 'new_v_w_in_o': 'new_v', 'new_v_hg_lb': 'new_v', 'new_v_hg_gnorm': 'new_v', 'new_v_w_out_o': 'new_v', 'new_v_ln1_g': 'new_v', 'new_v_ln1_b': 'new_v', 'new_v_w_ff1': 'new_v', 'new_v_w_ff2': 'new_v', 'new_v_ln2_g': 'new_v', 'new_v_ln2_b': 'new_v'}


def _forward(args):
    return _fwd_reference(*[args[k] for k in FWD_PARAMS])


def _output_shape():
    out = _jax.eval_shape(lambda: _forward(_fwd_setup_inputs(0)))
    return out.shape, out.dtype

N_MICROBATCH = 1
ADAM_LR = 0.001
ADAM_B1 = 0.9
ADAM_B2 = 0.999
ADAM_EPS = 1e-08
ADAM_WD = 0.01
ADAM_STEP = 10
PER_EXAMPLE_BATCH_AXIS = {'x': 0, 'positions': 0, 'loss_target': 0}
SHARED_INPUTS = []
_WEIGHT_DTYPES = {'w_in_e': _jnp.float32, 'mla_gq': _jnp.float32, 'mla_gkv': _jnp.float32, 'w_qb': _jnp.float32, 'w_kvb': _jnp.float32, 'sgu_ln_g': _jnp.float32, 'sgu_ln_b': _jnp.float32, 'sgu_w': _jnp.float32, 'sgu_b': _jnp.float32, 'w_out_e': _jnp.float32, 'w_in_o': _jnp.float32, 'hg_lb': _jnp.float32, 'hg_gnorm': _jnp.float32, 'w_out_o': _jnp.float32, 'ln1_g': _jnp.float32, 'ln1_b': _jnp.float32, 'w_ff1': _jnp.float32, 'w_ff2': _jnp.float32, 'ln2_g': _jnp.float32, 'ln2_b': _jnp.float32}
MOMENT_SCALE = {'w_in_e': 3.732855e-02, 'mla_gq': 1.653564e-02, 'mla_gkv': 2.665319e-02, 'w_qb': 9.224245e-03, 'w_kvb': 1.227680e-02, 'sgu_ln_g': 3.331855e-02, 'sgu_ln_b': 2.922950e-02, 'sgu_w': 2.917732e-02, 'sgu_b': 4.021184e-02, 'w_out_e': 7.495156e-02, 'w_in_o': 2.666164e-02, 'hg_lb': 3.351487e-03, 'hg_gnorm': 3.780272e-02, 'w_out_o': 7.344965e-02, 'ln1_g': 4.420005e-01, 'ln1_b': 2.607922e-01, 'w_ff1': 2.224955e-02, 'w_ff2': 4.163974e-02, 'ln2_g': 2.261426e+01, 'ln2_b': 1.934939e+00}


def _to_microbatches(a, axis):
    t = _jnp.moveaxis(a, axis, 0)
    t = t.reshape((N_MICROBATCH, t.shape[0] // N_MICROBATCH) + t.shape[1:])
    return _jnp.moveaxis(t, 1, axis + 1)


def setup_inputs(seed: int = 0) -> dict:
    inp = _fwd_setup_inputs(seed)
    key = _jax.random.fold_in(_jax.random.key(seed), 7919)
    shape, _ = _output_shape()
    out = dict(inp)
    out["loss_target"] = _jax.random.normal(_jax.random.fold_in(key, 0), shape, _jnp.float32)
    for i, name in enumerate(TWIN_WEIGHTS):
        w = inp[name].astype(_jnp.float32)
        if MOMENT_SCALE is None:
            s = _jnp.sqrt(_jnp.mean(_jnp.square(w)) + 1e-30)
        else:
            s = MOMENT_SCALE[name]
        km, kv = _jax.random.split(_jax.random.fold_in(key, i + 1))
        out[name] = w
        out["m_" + name] = s * _jax.random.normal(km, w.shape, _jnp.float32)
        out["v_" + name] = (s * s) * _jax.random.uniform(kv, w.shape, _jnp.float32, 0.5, 1.5)
    if N_MICROBATCH > 1:
        for name, axis in PER_EXAMPLE_BATCH_AXIS.items():
            out[name] = _to_microbatches(out[name], axis)
    return {'x': out['x'], 'positions': out['positions'], 'w_in_e': out['w_in_e'], 'mla_gq': out['mla_gq'], 'mla_gkv': out['mla_gkv'], 'w_qb': out['w_qb'], 'w_kvb': out['w_kvb'], 'sgu_ln_g': out['sgu_ln_g'], 'sgu_ln_b': out['sgu_ln_b'], 'sgu_w': out['sgu_w'], 'sgu_b': out['sgu_b'], 'w_out_e': out['w_out_e'], 'w_in_o': out['w_in_o'], 'hg_lb': out['hg_lb'], 'hg_gnorm': out['hg_gnorm'], 'w_out_o': out['w_out_o'], 'ln1_g': out['ln1_g'], 'ln1_b': out['ln1_b'], 'w_ff1': out['w_ff1'], 'w_ff2': out['w_ff2'], 'ln2_g': out['ln2_g'], 'ln2_b': out['ln2_b'], 'loss_target': out['loss_target'], 'm_w_in_e': out['m_w_in_e'], 'm_mla_gq': out['m_mla_gq'], 'm_mla_gkv': out['m_mla_gkv'], 'm_w_qb': out['m_w_qb'], 'm_w_kvb': out['m_w_kvb'], 'm_sgu_ln_g': out['m_sgu_ln_g'], 'm_sgu_ln_b': out['m_sgu_ln_b'], 'm_sgu_w': out['m_sgu_w'], 'm_sgu_b': out['m_sgu_b'], 'm_w_out_e': out['m_w_out_e'], 'm_w_in_o': out['m_w_in_o'], 'm_hg_lb': out['m_hg_lb'], 'm_hg_gnorm': out['m_hg_gnorm'], 'm_w_out_o': out['m_w_out_o'], 'm_ln1_g': out['m_ln1_g'], 'm_ln1_b': out['m_ln1_b'], 'm_w_ff1': out['m_w_ff1'], 'm_w_ff2': out['m_w_ff2'], 'm_ln2_g': out['m_ln2_g'], 'm_ln2_b': out['m_ln2_b'], 'v_w_in_e': out['v_w_in_e'], 'v_mla_gq': out['v_mla_gq'], 'v_mla_gkv': out['v_mla_gkv'], 'v_w_qb': out['v_w_qb'], 'v_w_kvb': out['v_w_kvb'], 'v_sgu_ln_g': out['v_sgu_ln_g'], 'v_sgu_ln_b': out['v_sgu_ln_b'], 'v_sgu_w': out['v_sgu_w'], 'v_sgu_b': out['v_sgu_b'], 'v_w_out_e': out['v_w_out_e'], 'v_w_in_o': out['v_w_in_o'], 'v_hg_lb': out['v_hg_lb'], 'v_hg_gnorm': out['v_hg_gnorm'], 'v_w_out_o': out['v_w_out_o'], 'v_ln1_g': out['v_ln1_g'], 'v_ln1_b': out['v_ln1_b'], 'v_w_ff1': out['v_w_ff1'], 'v_w_ff2': out['v_w_ff2'], 'v_ln2_g': out['v_ln2_g'], 'v_ln2_b': out['v_ln2_b']}


def _loss(weights, diff, rest, loss_target):
    with _jax.named_scope("forward"):
        args = {**rest, TWIN_DIFF_INPUT: diff, **{k: w.astype(_WEIGHT_DTYPES[k]) for k, w in weights.items()}}
        y = _forward(args)
    with _jax.named_scope("loss_head"):
        err = _jnp.square(y.astype(_jnp.float32) - loss_target)
        return 0.5 * _jnp.sum(_jnp.mean(err, axis=-1)) if err.ndim else 0.5 * err


def _adamw(w, g, m, v):
    m = ADAM_B1 * m + (1.0 - ADAM_B1) * g
    v = ADAM_B2 * v + (1.0 - ADAM_B2) * _jnp.square(g)
    m_hat = m / (1.0 - ADAM_B1 ** ADAM_STEP)
    v_hat = v / (1.0 - ADAM_B2 ** ADAM_STEP)
    delta = -ADAM_LR * (m_hat / (_jnp.sqrt(v_hat) + ADAM_EPS) + ADAM_WD * w)
    return delta, m, v


def reference(x, positions, w_in_e, mla_gq, mla_gkv, w_qb, w_kvb, sgu_ln_g, sgu_ln_b, sgu_w, sgu_b, w_out_e, w_in_o, hg_lb, hg_gnorm, w_out_o, ln1_g, ln1_b, w_ff1, w_ff2, ln2_g, ln2_b, loss_target, m_w_in_e, m_mla_gq, m_mla_gkv, m_w_qb, m_w_kvb, m_sgu_ln_g, m_sgu_ln_b, m_sgu_w, m_sgu_b, m_w_out_e, m_w_in_o, m_hg_lb, m_hg_gnorm, m_w_out_o, m_ln1_g, m_ln1_b, m_w_ff1, m_w_ff2, m_ln2_g, m_ln2_b, v_w_in_e, v_mla_gq, v_mla_gkv, v_w_qb, v_w_kvb, v_sgu_ln_g, v_sgu_ln_b, v_sgu_w, v_sgu_b, v_w_out_e, v_w_in_o, v_hg_lb, v_hg_gnorm, v_w_out_o, v_ln1_g, v_ln1_b, v_w_ff1, v_w_ff2, v_ln2_g, v_ln2_b):
    given = dict(x=x, positions=positions, w_in_e=w_in_e, mla_gq=mla_gq, mla_gkv=mla_gkv, w_qb=w_qb, w_kvb=w_kvb, sgu_ln_g=sgu_ln_g, sgu_ln_b=sgu_ln_b, sgu_w=sgu_w, sgu_b=sgu_b, w_out_e=w_out_e, w_in_o=w_in_o, hg_lb=hg_lb, hg_gnorm=hg_gnorm, w_out_o=w_out_o, ln1_g=ln1_g, ln1_b=ln1_b, w_ff1=w_ff1, w_ff2=w_ff2, ln2_g=ln2_g, ln2_b=ln2_b, loss_target=loss_target, m_w_in_e=m_w_in_e, m_mla_gq=m_mla_gq, m_mla_gkv=m_mla_gkv, m_w_qb=m_w_qb, m_w_kvb=m_w_kvb, m_sgu_ln_g=m_sgu_ln_g, m_sgu_ln_b=m_sgu_ln_b, m_sgu_w=m_sgu_w, m_sgu_b=m_sgu_b, m_w_out_e=m_w_out_e, m_w_in_o=m_w_in_o, m_hg_lb=m_hg_lb, m_hg_gnorm=m_hg_gnorm, m_w_out_o=m_w_out_o, m_ln1_g=m_ln1_g, m_ln1_b=m_ln1_b, m_w_ff1=m_w_ff1, m_w_ff2=m_w_ff2, m_ln2_g=m_ln2_g, m_ln2_b=m_ln2_b, v_w_in_e=v_w_in_e, v_mla_gq=v_mla_gq, v_mla_gkv=v_mla_gkv, v_w_qb=v_w_qb, v_w_kvb=v_w_kvb, v_sgu_ln_g=v_sgu_ln_g, v_sgu_ln_b=v_sgu_ln_b, v_sgu_w=v_sgu_w, v_sgu_b=v_sgu_b, v_w_out_e=v_w_out_e, v_w_in_o=v_w_in_o, v_hg_lb=v_hg_lb, v_hg_gnorm=v_hg_gnorm, v_w_out_o=v_w_out_o, v_ln1_g=v_ln1_g, v_ln1_b=v_ln1_b, v_w_ff1=v_w_ff1, v_w_ff2=v_w_ff2, v_ln2_g=v_ln2_g, v_ln2_b=v_ln2_b)
    weights = {n: given[n] for n in TWIN_WEIGHTS}
    shared = {n: given[n] for n in SHARED_INPUTS}
    per_example = {n: given[n] for n in ['x', 'positions']}
    grad_fn = _jax.value_and_grad(_loss, argnums=(0, 1))

    def one_microbatch(ex, loss_target):
        ex = dict(ex)
        diff = ex.pop(TWIN_DIFF_INPUT)
        return grad_fn(weights, diff, {**shared, **ex}, loss_target)

    if N_MICROBATCH == 1:
        loss, (grad_w, grad_x) = one_microbatch(per_example, given["loss_target"])
    else:
        def body(carry, xs):
            loss_sum, grad_sum = carry
            l_k, (gw_k, gx_k) = one_microbatch(xs[0], xs[1])
            with _jax.named_scope("update"):
                return (loss_sum + l_k, _jax.tree.map(_jnp.add, grad_sum, gw_k)), gx_k

        init = (_jnp.zeros((), _jnp.float32), _jax.tree.map(_jnp.zeros_like, weights))
        (loss, grad_w), grad_x = _jax.lax.scan(body, init, (per_example, given["loss_target"]))
    with _jax.named_scope("update"):
        delta_w, new_m, new_v = {}, {}, {}
        for n in TWIN_WEIGHTS:
            delta_w[n], new_m[n], new_v[n] = _adamw(weights[n], grad_w[n], given["m_" + n], given["v_" + n])
    return (loss, grad_x, *[grad_w[n] for n in TWIN_WEIGHTS], *[delta_w[n] for n in TWIN_WEIGHTS],
            *[new_m[n] for n in TWIN_WEIGHTS], *[new_v[n] for n in TWIN_WEIGHTS])
```

```python
import functools
import math

import jax
import jax.numpy as jnp
from jax import lax
from jax.experimental import pallas as pl
from jax.experimental.pallas import tpu as pltpu

F32 = jnp.float32
BF16 = jnp.bfloat16
MESH_IDS = pl.DeviceIdType.MESH

D = 1024
DEPTH = 2
HEADS = 8
NOPE, ROPE, VDIM = 64, 32, 64
QK_SCALE = (NOPE + ROPE) ** -0.5
ROPE_BASE = 10000.0
SGU_G, SGU_C = 4, 128
HG_CHUNK = 64
ALPHA = (2 * DEPTH) ** 0.25
EPS = 1e-5
LR, B1, B2, ADAM_EPS, WD, STEP = 0.001, 0.9, 0.999, 1e-08, 0.01, 10
GELU_C = math.sqrt(2.0 / math.pi)
GELU_A = 0.044715
HI = lax.Precision.HIGHEST
MB = 1024 * 1024
ROW_BLOCK = 512

NT_DIMS = (((1,), (1,)), ((), ()))
TN_DIMS = (((0,), (0,)), ((), ()))

SHARDED = (
    ("w_in_e", (1, 1024, 392), 2), ("w_qb", (1, 256, 192), 2), ("w_kvb", (1, 256, 256), 2),
    ("w_out_e", (1, 256, 1024), 1), ("w_in_o", (1, 1024, 1024), 2), ("w_out_o", (1, 256, 1024), 1),
    ("w_ff1", (2, 1024, 1024), 2), ("w_ff2", (2, 1024, 1024), 1), ("hg_gnorm", (1, 256), 1),
)
PACK_ROWS = 6144
HALF_ROWS = PACK_ROWS // 2
SMALL = (("mla_gq", (1, 256)), ("mla_gkv", (1, 256)), ("sgu_ln_g", (1, 512)), ("sgu_ln_b", (1, 512)),
         ("sgu_w", (1, 4, 128, 128)), ("sgu_b", (1, 4, 128)), ("hg_lb", (2, 1024)), ("hg_gnorm", (1, 1024)),
         ("ln1_g", (2, 1024)), ("ln1_b", (2, 1024)), ("ln2_g", (2, 1024)), ("ln2_b", (2, 1024)))
SMALL_ROWS = 80


def _params(vmem_mb, n_axes=0):
    kw = dict(vmem_limit_bytes=vmem_mb * MB)
    if n_axes:
        kw["dimension_semantics"] = ("arbitrary",) * n_axes
    return pltpu.CompilerParams(**kw)


def _fold8(x):
    return x.reshape(x.shape[0] // 8, 8, x.shape[1]).sum(axis=0)


def _ln_stats(r):
    mu = jnp.mean(r, -1, keepdims=True)
    xc = r - mu
    rstd = lax.rsqrt(jnp.mean(xc * xc, -1, keepdims=True) + EPS)
    return xc * rstd, rstd


def _sigmoid(x):
    return 1.0 / (1.0 + jnp.exp(-x))


def _gelu(x):
    return 0.5 * x * (1.0 + jnp.tanh(GELU_C * (x + GELU_A * x * x * x)))


def _gelu_grad(x):
    t = jnp.tanh(GELU_C * (x + GELU_A * x * x * x))
    return 0.5 * (1.0 + t) + 0.5 * x * (1.0 - t * t) * GELU_C * (1.0 + 3.0 * GELU_A * x * x)


def _matmul(a, b, *, name, ta=False, tb=False, out_dtype=F32, tm=512, tn=1024, tk=1024,
            a_sq=False, mul=None, add=None, add_scale=1.0):
    (K, M) = a.shape if ta else a.shape[::-1]
    (N, K2) = b.shape if tb else b.shape[::-1]
    assert K == K2, (a.shape, b.shape)
    tm, tn, tk = min(tm, M), min(tn, N), min(tk, K)
    assert M % tm == 0 and N % tn == 0 and K % tk == 0
    nk = K // tk
    a_spec = pl.BlockSpec((tk, tm), lambda i, j, k: (k, i)) if ta else pl.BlockSpec((tm, tk), lambda i, j, k: (i, k))
    b_spec = pl.BlockSpec((tn, tk), lambda i, j, k: (j, k)) if tb else pl.BlockSpec((tk, tn), lambda i, j, k: (k, j))
    o_spec = pl.BlockSpec((tm, tn), lambda i, j, k: (i, j))
    dims = (((0 if ta else 1,), (1 if tb else 0,)), ((), ()))
    extra = [e for e in (mul, add) if e is not None]

    def body(*refs):
        a_ref, b_ref = refs[0], refs[1]
        rest = list(refs[2:])
        mul_ref = rest.pop(0) if mul is not None else None
        add_ref = rest.pop(0) if add is not None else None
        o_ref = rest.pop(0)
        av = a_ref[...]
        if a_sq:
            av = av * av
        p = lax.dot_general(av, b_ref[...], dims, preferred_element_type=F32)

        def finish(r):
            if mul_ref is not None:
                r = r * (2.0 * mul_ref[...].astype(F32))
            if add_ref is not None:
                r = r + add_scale * add_ref[...]
            o_ref[...] = r.astype(out_dtype)

        if nk == 1:
            finish(p)
        else:
            acc_ref = rest.pop(0)
            k = pl.program_id(2)

            @pl.when(k == 0)
            def _():
                acc_ref[...] = p

            @pl.when(k > 0)
            def _():
                acc_ref[...] += p

            @pl.when(k == nk - 1)
            def _():
                finish(acc_ref[...])

    return pl.pallas_call(
        body, name=name, grid=(M // tm, N // tn, nk),
        in_specs=[a_spec, b_spec] + [o_spec] * len(extra), out_specs=o_spec,
        out_shape=jax.ShapeDtypeStruct((M, N), out_dtype),
        scratch_shapes=[pltpu.VMEM((tm, tn), F32)] if nk > 1 else [],
        compiler_params=_params(48, 3),
    )(a, b, *extra)


def _proj_ln(a_b, w_b, h_prev, g, b, *, name):
    T = a_b.shape[0]
    tm = min(ROW_BLOCK, T)
    row = pl.BlockSpec((tm, D), lambda i: (i, 0))
    vec = pl.BlockSpec((1, D), lambda i: (0, 0))

    def body(a_ref, w_ref, h_ref, g_ref, b_ref, r_ref, ho_ref, hb_ref):
        mix = jnp.dot(a_ref[...], w_ref[...], preferred_element_type=F32)
        r = ALPHA * h_ref[...] + mix
        xhat, _ = _ln_stats(r)
        y = xhat * g_ref[...] + b_ref[...]
        r_ref[...] = r
        ho_ref[...] = y
        hb_ref[...] = y.astype(BF16)

    return pl.pallas_call(
        body, name=name, grid=(T // tm,),
        in_specs=[row, pl.BlockSpec((D, D), lambda i: (0, 0)), row, vec, vec],
        out_specs=[row, row, row],
        out_shape=[jax.ShapeDtypeStruct((T, D), F32), jax.ShapeDtypeStruct((T, D), F32),
                   jax.ShapeDtypeStruct((T, D), BF16)],
        compiler_params=_params(40, 1),
    )(a_b, w_b, h_prev, g, b)


def _ffn_ln(h_b, w1, w2, h, g, b, *, name):
    T = h_b.shape[0]
    F = w1.shape[1]
    tm, tf = min(ROW_BLOCK, T), 1024
    nf = F // tf
    row = pl.BlockSpec((tm, D), lambda i, j: (i, 0))
    vec = pl.BlockSpec((1, D), lambda i, j: (0, 0))

    def body(hb_ref, w1_ref, w2_ref, h_ref, g_ref, b_ref, ra_ref, r_ref, ho_ref, hbo_ref, acc_ref):
        j = pl.program_id(1)
        a = jnp.dot(hb_ref[...], w1_ref[...], preferred_element_type=F32)
        ra = jnp.maximum(a, 0.0)
        ra_ref[...] = ra.astype(BF16)
        p = jnp.dot((ra * ra).astype(BF16), w2_ref[...], preferred_element_type=F32)

        @pl.when(j == 0)
        def _():
            acc_ref[...] = p

        @pl.when(j > 0)
        def _():
            acc_ref[...] += p

        @pl.when(j == nf - 1)
        def _():
            r = ALPHA * h_ref[...] + acc_ref[...]
            xhat, _ = _ln_stats(r)
            y = xhat * g_ref[...] + b_ref[...]
            r_ref[...] = r
            ho_ref[...] = y
            hbo_ref[...] = y.astype(BF16)

    return pl.pallas_call(
        body, name=name, grid=(T // tm, nf),
        in_specs=[row, pl.BlockSpec((D, tf), lambda i, j: (0, j)), pl.BlockSpec((tf, D), lambda i, j: (j, 0)),
                  row, vec, vec],
        out_specs=[pl.BlockSpec((tm, tf), lambda i, j: (i, j)), row, row, row],
        out_shape=[jax.ShapeDtypeStruct((T, F), BF16), jax.ShapeDtypeStruct((T, D), F32),
                   jax.ShapeDtypeStruct((T, D), F32), jax.ShapeDtypeStruct((T, D), BF16)],
        scratch_shapes=[pltpu.VMEM((tm, D), F32)],
        compiler_params=_params(48, 2),
    )(h_b, w1, w2, h, g, b)


def _loss_dy(y, tgt):
    T = y.shape[0]
    tm = min(ROW_BLOCK, T)
    row = pl.BlockSpec((tm, D), lambda i: (i, 0))

    def body(y_ref, t_ref, dy_ref, ls_ref):
        e = y_ref[...] - t_ref[...]
        dy_ref[...] = e * (1.0 / D)

        @pl.when(pl.program_id(0) == 0)
        def _():
            ls_ref[...] = jnp.zeros_like(ls_ref)

        ls_ref[...] += _fold8(e * e)

    return pl.pallas_call(
        body, name="loss_dy", grid=(T // tm,), in_specs=[row, row],
        out_specs=[row, pl.BlockSpec((8, D), lambda i: (0, 0))],
        out_shape=[jax.ShapeDtypeStruct((T, D), F32), jax.ShapeDtypeStruct((8, D), F32)],
        compiler_params=_params(32, 1),
    )(y, tgt)


def _ln_bwd(dy, r, g, *, name):
    T = dy.shape[0]
    tm = min(ROW_BLOCK, T)
    row = pl.BlockSpec((tm, D), lambda i: (i, 0))
    acc = pl.BlockSpec((8, D), lambda i: (0, 0))

    def body(dy_ref, r_ref, g_ref, dr_ref, drb_ref, dg_ref, db_ref):
        @pl.when(pl.program_id(0) == 0)
        def _():
            dg_ref[...] = jnp.zeros_like(dg_ref)
            db_ref[...] = jnp.zeros_like(db_ref)

        dy_ = dy_ref[...]
        xhat, rstd = _ln_stats(r_ref[...])
        dxh = dy_ * g_ref[...]
        m1 = jnp.mean(dxh, -1, keepdims=True)
        m2 = jnp.mean(dxh * xhat, -1, keepdims=True)
        dr = rstd * (dxh - m1 - xhat * m2)
        dr_ref[...] = dr
        drb_ref[...] = dr.astype(BF16)
        dg_ref[...] += _fold8(dy_ * xhat)
        db_ref[...] += _fold8(dy_)

    return pl.pallas_call(
        body, name=name, grid=(T // tm,),
        in_specs=[row, row, pl.BlockSpec((1, D), lambda i: (0, 0))],
        out_specs=[row, row, acc, acc],
        out_shape=[jax.ShapeDtypeStruct((T, D), F32), jax.ShapeDtypeStruct((T, D), BF16),
                   jax.ShapeDtypeStruct((8, D), F32), jax.ShapeDtypeStruct((8, D), F32)],
        compiler_params=_params(40, 1),
    )(dy, r, g)


def _rope(x, c, s1, s2):
    return x * c + pltpu.roll(x, 112, 1) * s1 + pltpu.roll(x, 16, 1) * s2


def _rope_t(dy, c, s1, s2):
    return dy * c + pltpu.roll(dy * s1, 16, 1) + pltpu.roll(dy * s2, 112, 1)


def _rms(x, g):
    rstd = lax.rsqrt(jnp.mean(x * x, -1, keepdims=True) + EPS)
    xhat = x * rstd
    return xhat * g, xhat, rstd


def _mla_prep(z0, gq, gkv, wq, wk, wv, rc, rs1, rs2):
    T = z0.shape[0]
    tm = min(ROW_BLOCK, T)
    HW = HEADS * 128

    def body(cq_ref, ckv_ref, kr_ref, gq_ref, gkv_ref, wq_ref, wk_ref, wv_ref, c_ref, s1_ref, s2_ref,
             q_ref, k_ref, v_ref):
        nq = _rms(cq_ref[...], gq_ref[...])[0].astype(BF16)
        nkv = _rms(ckv_ref[...], gkv_ref[...])[0].astype(BF16)
        q = jnp.dot(nq, wq_ref[...], preferred_element_type=F32)
        k = jnp.dot(nkv, wk_ref[...], preferred_element_type=F32)
        v = jnp.dot(nkv, wv_ref[...], preferred_element_type=F32)
        c, s1, s2 = c_ref[...], s1_ref[...], s2_ref[...]
        kr = _rope(pltpu.roll(kr_ref[...], 64, 1), c, s1, s2)
        for h in range(HEADS):
            sl = slice(h * 128, (h + 1) * 128)
            q_ref[:, sl] = (_rope(q[:, sl], c, s1, s2) * QK_SCALE).astype(BF16)
            k_ref[:, sl] = (k[:, sl] + kr).astype(BF16)
        v_ref[...] = v.astype(BF16)

    full = lambda shape: pl.BlockSpec(shape, lambda i: (0, 0))
    tab = pl.BlockSpec((tm, 128), lambda i: (i, 0))
    return pl.pallas_call(
        body, name="mla_prep", grid=(T // tm,),
        in_specs=[pl.BlockSpec((tm, 256), lambda i: (i, 0)), pl.BlockSpec((tm, 256), lambda i: (i, 1)),
                  pl.BlockSpec((tm, 128), lambda i: (i, 12)), full((1, 256)), full((1, 256)),
                  full((256, HW)), full((256, HW)), full((256, 512)), tab, tab, tab],
        out_specs=[pl.BlockSpec((tm, HW), lambda i: (i, 0)), pl.BlockSpec((tm, HW), lambda i: (i, 0)),
                   pl.BlockSpec((tm, 512), lambda i: (i, 0))],
        out_shape=[jax.ShapeDtypeStruct((T, HW), BF16), jax.ShapeDtypeStruct((T, HW), BF16),
                   jax.ShapeDtypeStruct((T, 512), BF16)],
        compiler_params=_params(40, 1),
    )(z0, z0, z0, gq, gkv, wq, wk, wv, rc, rs1, rs2)


def _flash_fwd(q, k, v):
    T = q.shape[0]
    bq = min(ROW_BLOCK, T)
    nq = T // bq

    def body(q_ref, k_ref, v_ref, o_ref, lse_ref, m_sc, l_sc, acc_sc):
        i, j = pl.program_id(1), pl.program_id(2)
        first = lax.broadcasted_iota(jnp.int32, (bq, 128), 1) < 64

        @pl.when(j == 0)
        def _():
            m_sc[...] = jnp.full_like(m_sc, -jnp.inf)
            l_sc[...] = jnp.zeros_like(l_sc)
            acc_sc[...] = jnp.zeros_like(acc_sc)

        def step(masked):
            vp = v_ref[...]
            acc = acc_sc[...]
            for h in range(2):
                sl = slice(h * 128, (h + 1) * 128)
                s = lax.dot_general(q_ref[:, sl], k_ref[:, sl], NT_DIMS, preferred_element_type=F32)
                if masked:
                    rows = lax.broadcasted_iota(jnp.int32, (bq, bq), 0)
                    cols = lax.broadcasted_iota(jnp.int32, (bq, bq), 1)
                    s = jnp.where(cols <= rows, s, -jnp.inf)
                m_prev = m_sc[h, :, 0:1]
                m_new = jnp.maximum(m_prev, jnp.max(s, axis=1, keepdims=True))
                alpha = jnp.exp(m_prev - m_new)
                p = jnp.exp(s - m_new)
                l_new = alpha * l_sc[h, :, 0:1] + jnp.sum(p, axis=1, keepdims=True)
                pv = jnp.dot(p.astype(BF16), vp, preferred_element_type=F32)
                mine = first if h == 0 else jnp.logical_not(first)
                acc = jnp.where(mine, acc * alpha + pv, acc)
                m_sc[h] = jnp.broadcast_to(m_new, (bq, 128))
                l_sc[h] = jnp.broadcast_to(l_new, (bq, 128))
            acc_sc[...] = acc

        @pl.when(j < i)
        def _():
            step(False)

        @pl.when(j == i)
        def _():
            step(True)
            l0, l1 = l_sc[0], l_sc[1]
            o_ref[...] = (acc_sc[...] / jnp.where(first, l0, l1)).astype(BF16)
            lse_ref[...] = jnp.where(first, m_sc[0] + jnp.log(l0), m_sc[1] + jnp.log(l1))

    kv = lambda hp, i, j: (jnp.minimum(i, j), hp)
    return pl.pallas_call(
        body, name="flash_fwd", grid=(4, nq, nq),
        in_specs=[pl.BlockSpec((bq, 256), lambda hp, i, j: (i, hp)), pl.BlockSpec((bq, 256), kv),
                  pl.BlockSpec((bq, 128), kv)],
        out_specs=[pl.BlockSpec((bq, 128), lambda hp, i, j: (i, hp)), pl.BlockSpec((bq, 128), lambda hp, i, j: (i, hp))],
        out_shape=[jax.ShapeDtypeStruct((T, 512), BF16), jax.ShapeDtypeStruct((T, 512), F32)],
        scratch_shapes=[pltpu.VMEM((2, bq, 128), F32), pltpu.VMEM((2, bq, 128), F32), pltpu.VMEM((bq, 128), F32)],
        compiler_params=_params(32, 3),
    )(q, k, v)


def _attn_delta(dmix, o):
    T = o.shape[0]
    tm = min(ROW_BLOCK, T)
    blk = pl.BlockSpec((tm, 512), lambda i: (i, 0))

    def body(do_ref, o_ref, delta_ref, dob_ref):
        first = lax.broadcasted_iota(jnp.int32, (tm, 128), 1) < 64
        for hp in range(4):
            sl = slice(hp * 128, (hp + 1) * 128)
            prod = do_ref[:, sl] * o_ref[:, sl].astype(F32)
            d0 = jnp.sum(jnp.where(first, prod, 0.0), axis=1, keepdims=True)
            d1 = jnp.sum(jnp.where(first, 0.0, prod), axis=1, keepdims=True)
            delta_ref[:, sl] = jnp.where(first, d0, d1)
        dob_ref[...] = do_ref[...].astype(BF16)

    return pl.pallas_call(
        body, name="attn_delta", grid=(T // tm,), in_specs=[blk, blk], out_specs=[blk, blk],
        out_shape=[jax.ShapeDtypeStruct((T, 512), F32), jax.ShapeDtypeStruct((T, 512), BF16)],
        compiler_params=_params(32, 1),
    )(dmix, o)


def _flash_bwd(q, k, v, do_b, lse, delta):
    T = q.shape[0]
    bq = min(ROW_BLOCK, T)
    nq = T // bq

    def body(q_ref, k_ref, v_ref, do_ref, lse_ref, dl_ref, dq_hbm, dk_ref, dv_ref, dq_sc, dk_sc, dv_sc, sem):
        hp, j, i = pl.program_id(0), pl.program_id(1), pl.program_id(2)
        first = lax.broadcasted_iota(jnp.int32, (bq, 128), 1) < 64

        @pl.when((j == 0) & (i == 0))
        def _():
            dq_sc[...] = jnp.zeros_like(dq_sc)

        @pl.when(i == j)
        def _():
            dk_sc[...] = jnp.zeros_like(dk_sc)
            dv_sc[...] = jnp.zeros_like(dv_sc)

        def step(masked):
            vp = v_ref[...]
            do = do_ref[...]
            for h in range(2):
                sl = slice(h * 128, (h + 1) * 128)
                qh, kh = q_ref[:, sl], k_ref[:, sl]
                s = lax.dot_general(qh, kh, NT_DIMS, preferred_element_type=F32)
                p = jnp.exp(s - lse_ref[:, h * 64:h * 64 + 1])
                if masked:
                    rows = lax.broadcasted_iota(jnp.int32, (bq, bq), 0)
                    cols = lax.broadcasted_iota(jnp.int32, (bq, bq), 1)
                    p = jnp.where(cols <= rows, p, 0.0)
                mine = first if h == 0 else jnp.logical_not(first)
                do_h = jnp.where(mine, do, jnp.zeros_like(do))
                dv_sc[...] += lax.dot_general(p.astype(BF16), do_h, TN_DIMS, preferred_element_type=F32)
                dp = lax.dot_general(do_h, vp, NT_DIMS, preferred_element_type=F32)
                ds = (p * (dp - dl_ref[:, h * 64:h * 64 + 1])).astype(BF16)
                dq_sc[i, :, sl] += jnp.dot(ds, kh, preferred_element_type=F32)
                dk_sc[:, sl] += lax.dot_general(ds, qh, TN_DIMS, preferred_element_type=F32)

        @pl.when(i > j)
        def _():
            step(False)

        @pl.when(i == j)
        def _():
            step(True)

        @pl.when(i == nq - 1)
        def _():
            dk_ref[...] = dk_sc[...]
            dv_ref[...] = dv_sc[...]

        @pl.when((j == nq - 1) & (i == nq - 1))
        def _():
            cp = pltpu.make_async_copy(dq_sc, dq_hbm.at[hp], sem)
            cp.start()
            cp.wait()

    qi = lambda hp, j, i: (jnp.maximum(i, j), hp)
    kj = lambda hp, j, i: (j, hp)
    return pl.pallas_call(
        body, name="flash_bwd", grid=(4, nq, nq),
        in_specs=[pl.BlockSpec((bq, 256), qi), pl.BlockSpec((bq, 256), kj), pl.BlockSpec((bq, 128), kj),
                  pl.BlockSpec((bq, 128), qi), pl.BlockSpec((bq, 128), qi), pl.BlockSpec((bq, 128), qi)],
        out_specs=[pl.BlockSpec(memory_space=pl.ANY), pl.BlockSpec((bq, 256), kj), pl.BlockSpec((bq, 128), kj)],
        out_shape=[jax.ShapeDtypeStruct((4, nq, bq, 256), F32), jax.ShapeDtypeStruct((T, 1024), F32),
                   jax.ShapeDtypeStruct((T, 512), F32)],
        scratch_shapes=[pltpu.VMEM((nq, bq, 256), F32), pltpu.VMEM((bq, 256), F32), pltpu.VMEM((bq, 128), F32),
                        pltpu.SemaphoreType.DMA],
        compiler_params=_params(40, 3),
    )(q, k, v, do_b, lse, delta)


def _mla_bwd(z0, dq4, dk, dv, gq, gkv, wq, wk, wv, rc, rs1, rs2):
    T = z0.shape[0]
    tm = dq4.shape[2]
    HW = HEADS * 128

    def body(cq_ref, ckv_ref, dq_ref, dk_ref, dv_ref, gq_ref, gkv_ref, wq_ref, wk_ref, wv_ref, c_ref, s1_ref, s2_ref,
             dc_ref, dkr_ref, dwq_ref, dwk_ref, dwv_ref, dgq_ref, dgkv_ref):
        @pl.when(pl.program_id(0) == 0)
        def _():
            for ref in (dwq_ref, dwk_ref, dwv_ref, dgq_ref, dgkv_ref):
                ref[...] = jnp.zeros_like(ref)

        c, s1, s2 = c_ref[...], s1_ref[...], s2_ref[...]
        lane = lax.broadcasted_iota(jnp.int32, (tm, 128), 1)
        nq, xq, rq = _rms(cq_ref[...], gq_ref[...])
        nkv, xkv, rkv = _rms(ckv_ref[...], gkv_ref[...])
        nq_b, nkv_b = nq.astype(BF16), nkv.astype(BF16)

        dq_parts, dk_parts = [], []
        dkr = jnp.zeros((tm, 128), F32)
        for h in range(HEADS):
            blk = dq_ref[h // 2, :, (h % 2) * 128:(h % 2 + 1) * 128] * QK_SCALE
            dq_parts.append(_rope_t(blk, c, s1, s2).astype(BF16))
            kb = dk_ref[:, h * 128:(h + 1) * 128]
            dk_parts.append(jnp.where(lane < NOPE, kb, 0.0).astype(BF16))
            dkr = dkr + kb
        dq_b = jnp.concatenate(dq_parts, axis=1)
        dk_b = jnp.concatenate(dk_parts, axis=1)
        dv_b = dv_ref[...].astype(BF16)

        dwq_ref[...] += lax.dot_general(nq_b, dq_b, TN_DIMS, preferred_element_type=F32)
        dwk_ref[...] += lax.dot_general(nkv_b, dk_b, TN_DIMS, preferred_element_type=F32)
        dwv_ref[...] += lax.dot_general(nkv_b, dv_b, TN_DIMS, preferred_element_type=F32)
        dnq = lax.dot_general(dq_b, wq_ref[...], NT_DIMS, preferred_element_type=F32)
        dnkv = (lax.dot_general(dk_b, wk_ref[...], NT_DIMS, preferred_element_type=F32)
                + lax.dot_general(dv_b, wv_ref[...], NT_DIMS, preferred_element_type=F32))

        def rms_bwd(dn, xhat, rstd, g):
            dxh = dn * g
            return rstd * (dxh - xhat * jnp.mean(dxh * xhat, -1, keepdims=True))

        dc_ref[:, :256] = rms_bwd(dnq, xq, rq, gq_ref[...]).astype(BF16)
        dc_ref[:, 256:] = rms_bwd(dnkv, xkv, rkv, gkv_ref[...]).astype(BF16)
        dgq_ref[...] += _fold8(dnq * xq)
        dgkv_ref[...] += _fold8(dnkv * xkv)
        dkr = pltpu.roll(_rope_t(dkr, c, s1, s2), 64, 1)
        dkr_ref[...] = jnp.where(lane < ROPE, dkr, 0.0).astype(BF16)

    full = lambda shape: pl.BlockSpec(shape, lambda i: (0,) * len(shape))
    tab = pl.BlockSpec((tm, 128), lambda i: (i, 0))
    return pl.pallas_call(
        body, name="mla_bwd", grid=(T // tm,),
        in_specs=[pl.BlockSpec((tm, 256), lambda i: (i, 0)), pl.BlockSpec((tm, 256), lambda i: (i, 1)),
                  pl.BlockSpec((4, None, tm, 256), lambda i: (0, i, 0, 0)),
                  pl.BlockSpec((tm, HW), lambda i: (i, 0)), pl.BlockSpec((tm, 512), lambda i: (i, 0)),
                  full((1, 256)), full((1, 256)), full((256, HW)), full((256, HW)), full((256, 512)), tab, tab, tab],
        out_specs=[pl.BlockSpec((tm, 512), lambda i: (i, 0)), tab, full((256, HW)), full((256, HW)),
                   full((256, 512)), full((8, 256)), full((8, 256))],
        out_shape=[jax.ShapeDtypeStruct((T, 512), BF16), jax.ShapeDtypeStruct((T, 128), BF16),
                   jax.ShapeDtypeStruct((256, HW), F32), jax.ShapeDtypeStruct((256, HW), F32),
                   jax.ShapeDtypeStruct((256, 512), F32), jax.ShapeDtypeStruct((8, 256), F32),
                   jax.ShapeDtypeStruct((8, 256), F32)],
        compiler_params=_params(48, 1),
    )(z0, z0, dq4, dk, dv, gq, gkv, wq, wk, wv, rc, rs1, rs2)


def _sgu_fwd(z0, ln_g, ln_b, w, b_t):
    T = z0.shape[0]
    tm = min(ROW_BLOCK, T)
    W = SGU_G * SGU_C

    def body(u_ref, v_ref, g_ref, b_ref, w_ref, bt_ref, o_ref):
        ug = _gelu(u_ref[...])
        xhat, _ = _ln_stats(_gelu(v_ref[...]))
        vn = (xhat * g_ref[...] + b_ref[...]).astype(BF16)
        tril = lax.broadcasted_iota(jnp.int32, (SGU_C, SGU_C), 0) >= lax.broadcasted_iota(jnp.int32, (SGU_C, SGU_C), 1)
        for g in range(SGU_G):
            cs = slice(g * SGU_C, (g + 1) * SGU_C)
            wg = jnp.where(tril, w_ref[g], 0.0).astype(BF16)
            bcol = bt_ref[:, g:g + 1]
            for c in range(tm // SGU_C):
                rs = slice(c * SGU_C, (c + 1) * SGU_C)
                mixed = jnp.dot(wg, vn[rs, cs], preferred_element_type=F32) + bcol
                o_ref[rs, cs] = (ug[rs, cs] * mixed).astype(BF16)

    full = lambda shape: pl.BlockSpec(shape, lambda i: (0,) * len(shape))
    return pl.pallas_call(
        body, name="sgu_fwd", grid=(T // tm,),
        in_specs=[pl.BlockSpec((tm, W), lambda i: (i, 1)), pl.BlockSpec((tm, W), lambda i: (i, 2)),
                  full((1, W)), full((1, W)), full((SGU_G, SGU_C, SGU_C)), full((SGU_C, SGU_G))],
        out_specs=pl.BlockSpec((tm, W), lambda i: (i, 0)),
        out_shape=jax.ShapeDtypeStruct((T, W), BF16),
        compiler_params=_params(32, 1),
    )(z0, z0, ln_g, ln_b, w, b_t)


def _sgu_bwd(z0, dmix, ln_g, ln_b, w, b_t):
    T = z0.shape[0]
    tm = min(ROW_BLOCK, T)
    W = SGU_G * SGU_C

    def body(u_ref, v_ref, do_ref, g_ref, b_ref, w_ref, bt_ref, duv_ref, dw_ref, db_ref, dlg_ref, dlb_ref):
        @pl.when(pl.program_id(0) == 0)
        def _():
            for ref in (dw_ref, db_ref, dlg_ref, dlb_ref):
                ref[...] = jnp.zeros_like(ref)

        u, v, dout = u_ref[...], v_ref[...], do_ref[...]
        ug = _gelu(u)
        xhat, rstd = _ln_stats(_gelu(v))
        vn = (xhat * g_ref[...] + b_ref[...]).astype(BF16)
        dmixed = dout * ug
        dmixed_b = dmixed.astype(BF16)
        tril = lax.broadcasted_iota(jnp.int32, (SGU_C, SGU_C), 0) >= lax.broadcasted_iota(jnp.int32, (SGU_C, SGU_C), 1)
        lane = lax.broadcasted_iota(jnp.int32, (SGU_C, SGU_C), 1)
        dvn_cols = []
        for g in range(SGU_G):
            cs = slice(g * SGU_C, (g + 1) * SGU_C)
            wg = jnp.where(tril, w_ref[g], 0.0).astype(BF16)
            bcol = bt_ref[:, g:g + 1]
            dw_g = jnp.zeros((SGU_C, SGU_C), F32)
            db_g = jnp.zeros((SGU_C, 1), F32)
            dvn_rows = []
            for c in range(tm // SGU_C):
                rs = slice(c * SGU_C, (c + 1) * SGU_C)
                mixed = jnp.dot(wg, vn[rs, cs], preferred_element_type=F32) + bcol
                duv_ref[rs, cs] = (dout[rs, cs] * mixed * _gelu_grad(u[rs, cs])).astype(BF16)
                dm = dmixed_b[rs, cs]
                dw_g = dw_g + lax.dot_general(dm, vn[rs, cs], NT_DIMS, preferred_element_type=F32)
                db_g = db_g + jnp.sum(dmixed[rs, cs], axis=1, keepdims=True)
                dvn_rows.append(lax.dot_general(wg, dm, TN_DIMS, preferred_element_type=F32))
            dw_ref[g] += jnp.where(tril, dw_g, 0.0)
            db_ref[...] += jnp.where(lane == g, db_g, 0.0)
            dvn_cols.append(jnp.concatenate(dvn_rows, axis=0))
        dvn = jnp.concatenate(dvn_cols, axis=1)
        dxh = dvn * g_ref[...]
        m1 = jnp.mean(dxh, -1, keepdims=True)
        m2 = jnp.mean(dxh * xhat, -1, keepdims=True)
        dvg = rstd * (dxh - m1 - xhat * m2)
        duv_ref[:, W:] = (dvg * _gelu_grad(v)).astype(BF16)
        dlg_ref[...] += _fold8(dvn * xhat)
        dlb_ref[...] += _fold8(dvn)

    full = lambda shape: pl.BlockSpec(shape, lambda i: (0,) * len(shape))
    return pl.pallas_call(
        body, name="sgu_bwd", grid=(T // tm,),
        in_specs=[pl.BlockSpec((tm, W), lambda i: (i, 1)), pl.BlockSpec((tm, W), lambda i: (i, 2)),
                  pl.BlockSpec((tm, W), lambda i: (i, 1)),
                  full((1, W)), full((1, W)), full((SGU_G, SGU_C, SGU_C)), full((SGU_C, SGU_G))],
        out_specs=[pl.BlockSpec((tm, 2 * W), lambda i: (i, 0)), full((SGU_G, SGU_C, SGU_C)), full((SGU_C, SGU_C)),
                   full((8, W)), full((8, W))],
        out_shape=[jax.ShapeDtypeStruct((T, 2 * W), BF16), jax.ShapeDtypeStruct((SGU_G, SGU_C, SGU_C), F32),
                   jax.ShapeDtypeStruct((SGU_C, SGU_C), F32), jax.ShapeDtypeStruct((8, W), F32),
                   jax.ShapeDtypeStruct((8, W), F32)],
        compiler_params=_params(40, 1),
    )(z0, z0, dmix, ln_g, ln_b, w, b_t)


def _hg_lower_bound(lb_ref):
    a0, a1 = lb_ref[0:1, :], lb_ref[1:2, :]
    m = jnp.maximum(a0, a1)
    e0, e1 = jnp.exp(a0 - m), jnp.exp(a1 - m)
    return e1 / (e0 + e1)


def _hg_chunk(qc, fc, lb):
    C = HG_CHUNK
    rows = lax.broadcasted_iota(jnp.int32, (C, C), 0)
    cols = lax.broadcasted_iota(jnp.int32, (C, C), 1)
    rowid = lax.broadcasted_iota(jnp.int32, (C, 128), 0)
    sq, sg = _sigmoid(qc), _sigmoid(fc)
    qf = qc * sq
    gate = lb + (1.0 - lb) * sg
    kk = 1.0 - gate
    lg = jnp.log(gate)
    bcum = jnp.dot((rows >= cols).astype(F32), lg, precision=HI, preferred_element_type=F32)
    b_mid = jnp.sum(jnp.where(rowid < C // 2, lg, 0.0), axis=0, keepdims=True)
    b_last = jnp.sum(lg, axis=0, keepdims=True)
    eq, ek, e, eh = jnp.exp(bcum - b_mid), jnp.exp(b_mid - bcum), jnp.exp(bcum), jnp.exp(b_last - bcum)
    qt, kt, qe, khat = qf * eq, kk * ek, qf * e, kk * eh
    a = lax.dot_general(qt.astype(BF16), kt.astype(BF16), NT_DIMS, preferred_element_type=F32)
    a = jnp.where(rows >= cols, a, 0.0)
    return dict(sq=sq, sg=sg, gate=gate, kk=kk, eq=eq, ek=ek, e=e, eh=eh, qt=qt, kt=kt, qe=qe, khat=khat, a=a,
                e_last=jnp.exp(b_last), tril=rows >= cols, rowid=rowid)


def _hgrn_fwd(z4, hg_lb, gnorm):
    T = z4.shape[1]
    tb = min(ROW_BLOCK, T)
    C = HG_CHUNK
    ncb = tb // C

    def body(q_ref, f_ref, i_ref, g_ref, lb_ref, gn_ref, y_ref, o_ref, st_ref, st_sc):
        @pl.when(pl.program_id(1) == 0)
        def _():
            st_sc[...] = jnp.zeros_like(st_sc)

        lb = _hg_lower_bound(lb_ref)
        gn = gn_ref[...]

        def chunk(c, carry):
            rs = pl.ds(pl.multiple_of(c * C, C), C)
            v_b = i_ref[rs, :].astype(BF16)
            gc = g_ref[rs, :]
            x = _hg_chunk(q_ref[rs, :], f_ref[rs, :], lb)
            st = st_sc[...]
            st_ref[c] = st
            o = (jnp.dot(x["a"].astype(BF16), v_b, preferred_element_type=F32)
                 + lax.dot_general(x["qe"].astype(BF16), st.astype(BF16), NT_DIMS, preferred_element_type=F32))
            st_sc[...] = st * x["e_last"] + lax.dot_general(v_b, x["khat"].astype(BF16), TN_DIMS,
                                                            preferred_element_type=F32)
            o_ref[rs, :] = o
            n = o * lax.rsqrt(jnp.mean(o * o, -1, keepdims=True) + EPS)
            y_ref[rs, :] = (n * gn * (gc * _sigmoid(gc))).astype(BF16)
            return carry

        lax.fori_loop(0, ncb, chunk, 0)

    zb = lambda k: pl.BlockSpec((None, tb, 128), lambda h, t: (k, t, h))
    out = pl.BlockSpec((tb, 128), lambda h, t: (t, h))
    return pl.pallas_call(
        body, name="hgrn_fwd", grid=(HEADS, T // tb),
        in_specs=[zb(0), zb(1), zb(2), zb(3), pl.BlockSpec((2, 128), lambda h, t: (0, h)),
                  pl.BlockSpec((1, 128), lambda h, t: (0, h))],
        out_specs=[out, out, pl.BlockSpec((None, ncb, 128, 128), lambda h, t: (h, t, 0, 0))],
        out_shape=[jax.ShapeDtypeStruct((T, D), BF16), jax.ShapeDtypeStruct((T, D), F32),
                   jax.ShapeDtypeStruct((HEADS, T // C, 128, 128), F32)],
        scratch_shapes=[pltpu.VMEM((128, 128), F32)],
        compiler_params=_params(32, 2),
    )(z4, z4, z4, z4, hg_lb, gnorm)


def _hgrn_bwd(z4, o_raw, dy, states, hg_lb, gnorm):
    T = z4.shape[1]
    tb = min(ROW_BLOCK, T)
    C = HG_CHUNK
    ncb = tb // C
    nt = T // tb

    def body(q_ref, f_ref, i_ref, g_ref, o_ref, dy_ref, st_ref, lb_ref, gn_ref, dz_ref, dlb_ref, dgn_ref, dst_sc):
        @pl.when(pl.program_id(1) == 0)
        def _():
            dst_sc[...] = jnp.zeros_like(dst_sc)
            dlb_ref[...] = jnp.zeros_like(dlb_ref)
            dgn_ref[...] = jnp.zeros_like(dgn_ref)

        lb = _hg_lower_bound(lb_ref)
        gn = gn_ref[...]

        def chunk(cc, carry):
            c = ncb - 1 - cc
            rs = pl.ds(pl.multiple_of(c * C, C), C)
            qc, gc = q_ref[rs, :], g_ref[rs, :]
            v_b = i_ref[rs, :].astype(BF16)
            x = _hg_chunk(qc, f_ref[rs, :], lb)
            st, dst = st_ref[c], dst_sc[...]
            st_b, dst_b = st.astype(BF16), dst.astype(BF16)
            o, dyc = o_ref[rs, :], dy_ref[rs, :]
            sgg = _sigmoid(gc)
            sil = gc * sgg
            rstd = lax.rsqrt(jnp.mean(o * o, -1, keepdims=True) + EPS)
            n = o * rstd
            dgn_ref[...] += _fold8(dyc * n * sil)
            dn = dyc * gn * sil
            do = rstd * (dn - n * jnp.mean(dn * n, -1, keepdims=True))
            dg = dyc * n * gn * (sgg * (1.0 + gc * (1.0 - sgg)))
            do_b = do.astype(BF16)
            da = jnp.where(x["tril"], lax.dot_general(do_b, v_b, NT_DIMS, preferred_element_type=F32), 0.0).astype(BF16)
            qt_b, kt_b, qe_b, khat_b = (x[n_].astype(BF16) for n_ in ("qt", "kt", "qe", "khat"))
            dv = (lax.dot_general(x["a"].astype(BF16), do_b, TN_DIMS, preferred_element_type=F32)
                  + lax.dot_general(khat_b, dst_b, NT_DIMS, preferred_element_type=F32))
            dqt = jnp.dot(da, kt_b, preferred_element_type=F32)
            dqe = jnp.dot(do_b, st_b, preferred_element_type=F32)
            dkt = lax.dot_general(da, qt_b, TN_DIMS, preferred_element_type=F32)
            dkhat = jnp.dot(v_b, dst_b, preferred_element_type=F32)
            dst_sc[...] = lax.dot_general(do_b, qe_b, TN_DIMS, preferred_element_type=F32) + dst * x["e_last"]
            de_last = jnp.sum(st * dst, axis=0, keepdims=True)
            dqf = dqt * x["eq"] + dqe * x["e"]
            dkk = dkt * x["ek"] + dkhat * x["eh"]
            dkh_kh = dkhat * x["khat"]
            db = dqt * qt_b.astype(F32) - dkt * kt_b.astype(F32) + dqe * x["qe"] - dkh_kh
            db_last = jnp.sum(dkh_kh, axis=0, keepdims=True) + de_last * x["e_last"]
            db = db + jnp.where(x["rowid"] == C - 1, db_last, 0.0)
            rows = lax.broadcasted_iota(jnp.int32, (C, C), 0)
            cols = lax.broadcasted_iota(jnp.int32, (C, C), 1)
            dlg = jnp.dot((rows <= cols).astype(F32), db, precision=HI, preferred_element_type=F32)
            dgate = dlg / x["gate"] - dkk
            sg, sq = x["sg"], x["sq"]
            dlb_ref[...] += _fold8(dgate * (1.0 - sg)) * (lb * (1.0 - lb))
            dz_ref[0, rs, :] = (dqf * (sq * (1.0 + qc * (1.0 - sq)))).astype(BF16)
            dz_ref[1, rs, :] = (dgate * (1.0 - lb) * sg * (1.0 - sg)).astype(BF16)
            dz_ref[2, rs, :] = dv.astype(BF16)
            dz_ref[3, rs, :] = dg.astype(BF16)
            return carry

        lax.fori_loop(0, ncb, chunk, 0)

    zb = lambda k: pl.BlockSpec((None, tb, 128), lambda h, t: (k, nt - 1 - t, h))
    blk = pl.BlockSpec((tb, 128), lambda h, t: (nt - 1 - t, h))
    acc = pl.BlockSpec((8, 128), lambda h, t: (0, h))
    return pl.pallas_call(
        body, name="hgrn_bwd", grid=(HEADS, nt),
        in_specs=[zb(0), zb(1), zb(2), zb(3), blk, blk,
                  pl.BlockSpec((None, ncb, 128, 128), lambda h, t: (h, nt - 1 - t, 0, 0)),
                  pl.BlockSpec((2, 128), lambda h, t: (0, h)), pl.BlockSpec((1, 128), lambda h, t: (0, h))],
        out_specs=[pl.BlockSpec((4, tb, 128), lambda h, t: (0, nt - 1 - t, h)), acc, acc],
        out_shape=[jax.ShapeDtypeStruct((4, T, D), BF16), jax.ShapeDtypeStruct((8, D), F32),
                   jax.ShapeDtypeStruct((8, D), F32)],
        scratch_shapes=[pltpu.VMEM((128, 128), F32)],
        compiler_params=_params(32, 2),
    )(z4, z4, z4, z4, o_raw, dy, states, hg_lb, gnorm)


def _adamw(w, g, m, v, *, name):
    R, L = w.shape
    tr = R if R <= 512 else 512
    assert R % tr == 0
    blk = pl.BlockSpec((tr, L), lambda i: (i, 0))
    c1, c2 = 1.0 - B1 ** STEP, 1.0 - B2 ** STEP

    def body(w_ref, g_ref, m_ref, v_ref, d_ref, mo_ref, vo_ref):
        g_ = g_ref[...]
        m_ = B1 * m_ref[...] + (1.0 - B1) * g_
        v_ = B2 * v_ref[...] + (1.0 - B2) * (g_ * g_)
        d_ref[...] = -LR * ((m_ / c1) / (jnp.sqrt(v_ / c2) + ADAM_EPS) + WD * w_ref[...])
        mo_ref[...] = m_
        vo_ref[...] = v_

    sds = jax.ShapeDtypeStruct((R, L), F32)
    return pl.pallas_call(
        body, name=name, grid=(R // tr,), in_specs=[blk] * 4, out_specs=[blk] * 3, out_shape=[sds] * 3,
        compiler_params=_params(32, 1),
    )(w, g, m, v)


def _add_pairs(a, b):
    n, R, L = a.shape
    tr = 512
    blk = pl.BlockSpec((None, tr, L), lambda k, i: (k, i, 0))

    def body(a_ref, b_ref, o_ref):
        o_ref[...] = (a_ref[...].astype(F32) + b_ref[...].astype(F32)).astype(BF16)

    return pl.pallas_call(
        body, name="grad_pair_add", grid=(n, R // tr), in_specs=[blk, blk], out_specs=blk,
        out_shape=jax.ShapeDtypeStruct((n, R, L), BF16), compiler_params=_params(16, 2),
    )(a, b)


def _sum_chips(own, recv):
    R, L = own.shape
    tr = 512

    def body(o_ref, r_ref, out_ref):
        out_ref[...] = ((o_ref[...].astype(F32) + r_ref[0].astype(F32)) + r_ref[1].astype(F32)) + r_ref[2].astype(F32)

    return pl.pallas_call(
        body, name="grad_chip_sum", grid=(R // tr,),
        in_specs=[pl.BlockSpec((tr, L), lambda i: (i, 0)), pl.BlockSpec((3, tr, L), lambda i: (0, i, 0))],
        out_specs=pl.BlockSpec((tr, L), lambda i: (i, 0)), out_shape=jax.ShapeDtypeStruct((R, L), F32),
        compiler_params=_params(32, 1),
    )(own, recv)


def _mesh_pos():
    return lax.axis_index("x"), lax.axis_index("y"), lax.axis_index("c")


_ANY = pl.BlockSpec(memory_space=pl.ANY)


def _gather_shards(shard):
    _, R, L = shard.shape

    def body(x_ref, out_ref, send_sems, recv_sems, local_sem):
        x, y, c = _mesh_pos()
        k = 2 * x + y
        chips = [(1 - x, y), (x, 1 - y), (1 - x, 1 - y)]

        def copy(idx, chip, half, to, src=None):
            dst = out_ref.at[chip, half]
            return pltpu.make_async_remote_copy(
                src_ref=dst if src is None else src, dst_ref=dst, send_sem=send_sems.at[idx],
                recv_sem=recv_sems.at[idx], device_id=to, device_id_type=MESH_IDS)

        mine = pltpu.make_async_copy(x_ref, out_ref.at[k], local_sem)
        mine.start()
        first = [copy(j, k, c, (cx, cy, c), src=x_ref.at[c]) for j, (cx, cy) in enumerate(chips)]
        for cp in first:
            cp.start()
        passed = [copy(3 + j, 2 * cx + cy, c, (x, y, 1 - c)) for j, (cx, cy) in enumerate(chips)]
        for j, (cx, cy) in enumerate(chips):
            copy(j, 2 * cx + cy, c, (x, y, c)).wait_recv()
            passed[j].start()
        for j, (cx, cy) in enumerate(chips):
            copy(3 + j, 2 * cx + cy, 1 - c, (x, y, c)).wait_recv()
        for cp in first + passed:
            cp.wait_send()
        mine.wait()

    return pl.pallas_call(
        body, name="gather_shards", in_specs=[_ANY], out_specs=_ANY,
        out_shape=jax.ShapeDtypeStruct((4, 2, R, L), shard.dtype),
        scratch_shapes=[pltpu.SemaphoreType.DMA((6,)), pltpu.SemaphoreType.DMA((6,)), pltpu.SemaphoreType.DMA],
    )(shard)


def _pair_swap(send, *, name):
    def body(s_ref, r_ref, send_sem, recv_sem):
        x, y, c = _mesh_pos()
        cp = pltpu.make_async_remote_copy(src_ref=s_ref, dst_ref=r_ref, send_sem=send_sem, recv_sem=recv_sem,
                                          device_id=(x, y, 1 - c), device_id_type=MESH_IDS)
        cp.start()
        cp.wait()

    return pl.pallas_call(
        body, name=name, in_specs=[_ANY], out_specs=_ANY, out_shape=jax.ShapeDtypeStruct(send.shape, send.dtype),
        scratch_shapes=[pltpu.SemaphoreType.DMA, pltpu.SemaphoreType.DMA],
    )(send)


def _pair_gather(own, *, name):
    R, L = own.shape

    def body(s_ref, out_ref, send_sem, recv_sem, local_sem):
        x, y, c = _mesh_pos()
        mine = pltpu.make_async_copy(s_ref, out_ref.at[c], local_sem)
        mine.start()
        cp = pltpu.make_async_remote_copy(src_ref=s_ref, dst_ref=out_ref.at[c], send_sem=send_sem, recv_sem=recv_sem,
                                          device_id=(x, y, 1 - c), device_id_type=MESH_IDS)
        cp.start()
        cp.wait_send()
        pltpu.make_async_remote_copy(src_ref=s_ref, dst_ref=out_ref.at[1 - c], send_sem=send_sem, recv_sem=recv_sem,
                                     device_id=(x, y, 1 - c), device_id_type=MESH_IDS).wait_recv()
        mine.wait()

    return pl.pallas_call(
        body, name=name, in_specs=[_ANY], out_specs=_ANY, out_shape=jax.ShapeDtypeStruct((2, R, L), own.dtype),
        scratch_shapes=[pltpu.SemaphoreType.DMA, pltpu.SemaphoreType.DMA, pltpu.SemaphoreType.DMA],
    )(own)


def _chip_scatter(parts):
    _, R, L = parts.shape

    def body(p_ref, r_ref, send_sems, recv_sems):
        x, y, c = _mesh_pos()
        chips = [(1 - x, y), (x, 1 - y), (1 - x, 1 - y)]
        cps = [pltpu.make_async_remote_copy(src_ref=p_ref.at[2 * cx + cy], dst_ref=r_ref.at[j], send_sem=send_sems.at[j],
                                            recv_sem=recv_sems.at[j], device_id=(cx, cy, c), device_id_type=MESH_IDS)
               for j, (cx, cy) in enumerate(chips)]
        for cp in cps:
            cp.start()
        for cp in cps:
            cp.wait()

    return pl.pallas_call(
        body, name="grad_chip_scatter", in_specs=[_ANY], out_specs=_ANY,
        out_shape=jax.ShapeDtypeStruct((3, R, L), parts.dtype),
        scratch_shapes=[pltpu.SemaphoreType.DMA((3,)), pltpu.SemaphoreType.DMA((3,))],
    )(parts)


def _small_allreduce(vec):
    R, L = vec.shape

    def body(v_ref, sum_ref, all_ref, send_sems, recv_sems):
        x, y, c = _mesh_pos()
        me = 4 * x + 2 * y + c
        all_ref[me] = v_ref[...]
        cps = []
        for r in range(1, 8):
            peer = (x ^ (r >> 2), y ^ ((r >> 1) & 1), c ^ (r & 1))
            cps.append(pltpu.make_async_remote_copy(
                src_ref=v_ref, dst_ref=all_ref.at[me], send_sem=send_sems.at[r - 1], recv_sem=recv_sems.at[r - 1],
                device_id=peer, device_id_type=MESH_IDS))
        for cp in cps:
            cp.start()
        for r in range(1, 8):
            src = 4 * (x ^ (r >> 2)) + 2 * (y ^ ((r >> 1) & 1)) + (c ^ (r & 1))
            pltpu.make_async_remote_copy(
                src_ref=v_ref, dst_ref=all_ref.at[src], send_sem=send_sems.at[r - 1], recv_sem=recv_sems.at[r - 1],
                device_id=(x, y, c), device_id_type=MESH_IDS).wait_recv()
        for cp in cps:
            cp.wait_send()
        s = all_ref[0]
        for d in range(1, 8):
            s = s + all_ref[d]
        sum_ref[...] = s

    vm = pl.BlockSpec(memory_space=pltpu.VMEM)
    return pl.pallas_call(
        body, name="small_allreduce", in_specs=[vm], out_specs=[vm, vm],
        out_shape=[jax.ShapeDtypeStruct((R, L), F32), jax.ShapeDtypeStruct((8, R, L), F32)],
        scratch_shapes=[pltpu.SemaphoreType.DMA((7,)), pltpu.SemaphoreType.DMA((7,))],
        compiler_params=_params(16),
    )(vec)[0]


BIG = tuple(e for e in SHARDED if e[0] != "hg_gnorm")


def _pack_shards(shards, dtype, gnorm=None):
    parts = [shards[n].reshape(-1).astype(dtype) for n, _, _ in BIG]
    if gnorm is not None:
        parts.append(lax.bitcast_convert_type(gnorm.reshape(-1), BF16).reshape(-1))
    flat = jnp.concatenate(parts)
    flat = jnp.pad(flat, (0, PACK_ROWS * 1024 - flat.shape[0]))
    return flat.reshape(2, HALF_ROWS, 1024)


def _unpack_shards(packed, with_gnorm=False):
    flat = packed.reshape(-1)
    out, off = {}, 0
    for n, shape, _ in BIG:
        size = math.prod(shape)
        out[n] = flat[off:off + size].reshape(shape)
        off += size
    if with_gnorm:
        out["hg_gnorm"] = lax.bitcast_convert_type(flat[off:off + 512].reshape(256, 2), F32).reshape(1, 256)
    return out


def _pack_small(vals):
    flat = jnp.concatenate([vals[n].reshape(-1).astype(F32) for n, _ in SMALL])
    return jnp.pad(flat, (0, SMALL_ROWS * 1024 - flat.shape[0])).reshape(SMALL_ROWS, 1024)


def _unpack_small(packed):
    flat = packed.reshape(-1)
    out, off = {}, 0
    for n, shape in SMALL:
        size = math.prod(shape)
        out[n] = flat[off:off + size].reshape(shape)
        off += size
    return out


def _rope_tables(positions):
    half = ROPE // 2
    inv_freq = ROPE_BASE ** (-jnp.arange(half, dtype=F32) / half)
    ang = positions.astype(F32).reshape(-1, 1) * inv_freq
    cos, sin = jnp.cos(ang), jnp.sin(ang)
    T = ang.shape[0]
    one, z16, z32 = jnp.ones((T, NOPE), F32), jnp.zeros((T, half), F32), jnp.zeros((T, 32), F32)
    z64 = jnp.zeros((T, NOPE), F32)
    c = jnp.concatenate([one, cos, cos, z32], axis=1)
    s1 = jnp.concatenate([z64, -sin, z16, z32], axis=1)
    s2 = jnp.concatenate([z64, z16, sin, z32], axis=1)
    return c, s1, s2


def _local_step(x, positions, tgt, W, P):
    row = lambda a: a.reshape(1, -1)
    rc, rs1, rs2 = _rope_tables(positions)

    w_in_e = W["w_in_e"][0]
    w_in = jnp.concatenate([w_in_e[:, :512], w_in_e[:, 544:1568], w_in_e[:, 512:544], jnp.zeros((D, 96), BF16)], axis=1)
    wq = jnp.pad(W["w_qb"][0].reshape(256, HEADS, NOPE + ROPE), ((0, 0), (0, 0), (0, 32))).reshape(256, HEADS * 128)
    kvb = W["w_kvb"][0].reshape(256, HEADS, NOPE + VDIM)
    wk = jnp.pad(kvb[:, :, :NOPE], ((0, 0), (0, 0), (0, 64))).reshape(256, HEADS * 128)
    wv = kvb[:, :, NOPE:].reshape(256, HEADS * VDIM)
    w_out_e, w_out_o = W["w_out_e"][0], W["w_out_o"][0]
    w_in_o = W["w_in_o"][0]
    sgu_w = P["sgu_w"][0]
    sgu_bt = P["sgu_b"][0].T
    gq, gkv = P["mla_gq"], P["mla_gkv"]
    gnorm = P["hg_gnorm"]

    x_b = x.astype(BF16)
    z0 = _matmul(x_b, w_in, name="in_proj_e", tn=1664)
    q, k, v = _mla_prep(z0, gq, gkv, wq, wk, wv, rc, rs1, rs2)
    a_out, lse = _flash_fwd(q, k, v)
    b_out = _sgu_fwd(z0, P["sgu_ln_g"], P["sgu_ln_b"], sgu_w, sgu_bt)
    mix0 = jnp.concatenate([a_out, b_out], axis=1)
    r1, h1, h1b = _proj_ln(mix0, w_out_e, x, row(P["ln1_g"][0]), row(P["ln1_b"][0]), name="out_proj_ln_e")
    ra0, r2, h2, h2b = _ffn_ln(h1b, W["w_ff1"][0], W["w_ff2"][0], h1, row(P["ln2_g"][0]), row(P["ln2_b"][0]), name="ffn_ln_0")
    z4 = jnp.stack([_matmul(h2b, w_in_o[:, s * D:(s + 1) * D], name=f"in_proj_o{s}") for s in range(4)])
    y1, o_raw, states = _hgrn_fwd(z4, P["hg_lb"], gnorm)
    r3, h3, h3b = _proj_ln(y1, w_out_o, h2, row(P["ln1_g"][1]), row(P["ln1_b"][1]), name="out_proj_ln_o")
    ra1, r4, h4, _ = _ffn_ln(h3b, W["w_ff1"][1], W["w_ff2"][1], h3, row(P["ln2_g"][1]), row(P["ln2_b"][1]), name="ffn_ln_1")
    dy, loss_parts = _loss_dy(h4, tgt)

    gw, gs = {}, {}
    ln1_g, ln1_b, ln2_g, ln2_b = [None, None], [None, None], [None, None], [None, None]
    ff1, ff2 = [None, None], [None, None]

    def ffn_bwd(l, dh, r_out, ra, r_mid_in_b, g2):
        dr, dr_b, dg, db = _ln_bwd(dh, r_out, row(g2), name=f"ln2_bwd_{l}")
        ln2_g[l], ln2_b[l] = dg.sum(0), db.sum(0)
        da = _matmul(dr_b, W["w_ff2"][l], tb=True, mul=ra, out_dtype=BF16, name=f"ffn_da_{l}")
        ff2[l] = _matmul(ra, dr_b, ta=True, a_sq=True, name=f"ffn_dw2_{l}", tm=1024, tn=1024, tk=512)
        ff1[l] = _matmul(r_mid_in_b, da, ta=True, name=f"ffn_dw1_{l}", tm=1024, tn=1024, tk=512)
        return _matmul(da, W["w_ff1"][l], tb=True, add=dr, add_scale=ALPHA, name=f"ffn_dh_{l}")

    dh3 = ffn_bwd(1, dy, r4, ra1, h3b, P["ln2_g"][1])
    dr3, dr3_b, dg, db = _ln_bwd(dh3, r3, row(P["ln1_g"][1]), name="ln1_bwd_1")
    ln1_g[1], ln1_b[1] = dg.sum(0), db.sum(0)
    gw["w_out_o"] = _matmul(y1, dr3_b, ta=True, name="dw_out_o", tm=1024, tn=1024, tk=512)[None]
    dmix1 = _matmul(dr3_b, w_out_o, tb=True, name="dmix_o")
    dz4, dlb, dgn = _hgrn_bwd(z4, o_raw, dmix1, states, P["hg_lb"], gnorm)
    gw["w_in_o"] = jnp.concatenate(
        [_matmul(h2b, dz4[s], ta=True, name=f"dw_in_o{s}", tm=1024, tn=1024, tk=512) for s in range(4)], axis=1)[None]
    dh2 = dr3
    for s in range(4):
        dh2 = _matmul(dz4[s], w_in_o[:, s * D:(s + 1) * D], tb=True, add=dh2, add_scale=ALPHA if s == 0 else 1.0,
                      name=f"dh_in_o{s}")
    d_lb1 = dlb.sum(0)
    gs["hg_lb"] = jnp.stack([-d_lb1, d_lb1])
    gs["hg_gnorm"] = dgn.sum(0)[None]

    dh1 = ffn_bwd(0, dh2, r2, ra0, h1b, P["ln2_g"][0])
    dr1, dr1_b, dg, db = _ln_bwd(dh1, r1, row(P["ln1_g"][0]), name="ln1_bwd_0")
    ln1_g[0], ln1_b[0] = dg.sum(0), db.sum(0)
    gw["w_out_e"] = _matmul(mix0, dr1_b, ta=True, name="dw_out_e", tm=1024, tn=1024, tk=512)[None]
    dmix0 = _matmul(dr1_b, w_out_e, tb=True, name="dmix_e")
    delta, do_b = _attn_delta(dmix0, a_out)
    dq4, dk, dv = _flash_bwd(q, k, v, do_b, lse, delta)
    dc, dkr, dwq, dwk, dwv, dgq, dgkv = _mla_bwd(z0, dq4, dk, dv, gq, gkv, wq, wk, wv, rc, rs1, rs2)
    duv, dsw, dsb, dslg, dslb = _sgu_bwd(z0, dmix0, P["sgu_ln_g"], P["sgu_ln_b"], sgu_w, sgu_bt)
    dz0 = jnp.concatenate([dc, duv, dkr], axis=1)
    dw_in = _matmul(x_b, dz0, ta=True, name="dw_in_e", tm=1024, tn=1664, tk=512)
    gw["w_in_e"] = jnp.concatenate([dw_in[:, :512], dw_in[:, 1536:1568], dw_in[:, 512:1536]], axis=1)[None]
    grad_x = _matmul(dz0, w_in, tb=True, add=dr1, add_scale=ALPHA, name="dx", tk=1664)

    gw["w_qb"] = dwq.reshape(256, HEADS, 128)[:, :, :NOPE + ROPE].reshape(1, 256, HEADS * (NOPE + ROPE))
    gw["w_kvb"] = jnp.concatenate([dwk.reshape(256, HEADS, 128)[:, :, :NOPE], dwv.reshape(256, HEADS, VDIM)],
                                  axis=2).reshape(1, 256, HEADS * (NOPE + VDIM))
    gw["w_ff1"], gw["w_ff2"] = jnp.stack(ff1), jnp.stack(ff2)
    gs["mla_gq"], gs["mla_gkv"] = dgq.sum(0)[None], dgkv.sum(0)[None]
    gs["sgu_ln_g"], gs["sgu_ln_b"] = dslg.sum(0)[None], dslb.sum(0)[None]
    gs["sgu_w"], gs["sgu_b"] = dsw[None], dsb[:, :SGU_G].T[None]
    gs["ln1_g"], gs["ln1_b"] = jnp.stack(ln1_g), jnp.stack(ln1_b)
    gs["ln2_g"], gs["ln2_b"] = jnp.stack(ln2_g), jnp.stack(ln2_b)
    return loss_parts, grad_x, gw, gs


WEIGHTS = ['w_in_e', 'mla_gq', 'mla_gkv', 'w_qb', 'w_kvb', 'sgu_ln_g', 'sgu_ln_b', 'sgu_w', 'sgu_b', 'w_out_e',
           'w_in_o', 'hg_lb', 'hg_gnorm', 'w_out_o', 'ln1_g', 'ln1_b', 'w_ff1', 'w_ff2', 'ln2_g', 'ln2_b']


def kernel(x, positions, w_in_e, mla_gq, mla_gkv, w_qb, w_kvb, sgu_ln_g, sgu_ln_b, sgu_w, sgu_b, w_out_e, w_in_o, hg_lb, hg_gnorm, w_out_o, ln1_g, ln1_b, w_ff1, w_ff2, ln2_g, ln2_b, loss_target, m_w_in_e, m_mla_gq, m_mla_gkv, m_w_qb, m_w_kvb, m_sgu_ln_g, m_sgu_ln_b, m_sgu_w, m_sgu_b, m_w_out_e, m_w_in_o, m_hg_lb, m_hg_gnorm, m_w_out_o, m_ln1_g, m_ln1_b, m_w_ff1, m_w_ff2, m_ln2_g, m_ln2_b, v_w_in_e, v_mla_gq, v_mla_gkv, v_w_qb, v_w_kvb, v_sgu_ln_g, v_sgu_ln_b, v_sgu_w, v_sgu_b, v_w_out_e, v_w_in_o, v_hg_lb, v_hg_gnorm, v_w_out_o, v_ln1_g, v_ln1_b, v_w_ff1, v_w_ff2, v_ln2_g, v_ln2_b):
    args = dict(locals())
    w = {n: args[n] for n in WEIGHTS}
    m = {n: args["m_" + n] for n in WEIGHTS}
    v = {n: args["v_" + n] for n in WEIGHTS}
    cx, cy, cc = _mesh_pos()
    chip = 2 * cx + cy

    gathered = _gather_shards(_pack_shards(w, BF16, gnorm=hg_gnorm))
    per_chip = [_unpack_shards(gathered[j], with_gnorm=True) for j in range(4)]
    full = {n: jnp.concatenate([per_chip[j][n] for j in range(4)], axis=ax) for n, _, ax in SHARDED}
    small = {n: w[n] for n, _ in SMALL if n != "hg_gnorm"}
    small["hg_gnorm"] = full["hg_gnorm"]

    loss_parts, grad_x, gw, gs = _local_step(x[0], positions[0], loss_target[0], full, small)

    loss = lax.psum((0.5 / D) * jnp.sum(loss_parts), ("x", "y", "c"))

    by_chip = []
    for j in range(4):
        by_chip.append(_pack_shards({n: jnp.split(gw[n], 4, axis=ax)[j] for n, _, ax in BIG}, BF16))
    halves = jnp.stack(by_chip, axis=1)
    mine_half = lax.dynamic_index_in_dim(halves, cc, 0, keepdims=False)
    other_half = lax.dynamic_index_in_dim(halves, 1 - cc, 0, keepdims=False)
    pair_sum = _add_pairs(mine_half, _pair_swap(other_half, name="grad_pair_swap"))
    from_chips = _chip_scatter(pair_sum)
    own = lax.dynamic_index_in_dim(pair_sum, chip, 0, keepdims=False)
    reduced = _pair_gather(_sum_chips(own, from_chips), name="grad_pair_gather")
    g_shard = _unpack_shards(reduced)

    g_small = _unpack_small(_small_allreduce(_pack_small(gs)))
    g_gnorm = lax.dynamic_slice_in_dim(g_small["hg_gnorm"], chip * 256, 256, axis=1)

    grads = {n: g_shard[n] for n, _, _ in BIG}
    grads.update({n: g_small[n] for n, _ in SMALL if n != "hg_gnorm"})
    grads["hg_gnorm"] = g_gnorm

    delta, new_m, new_v = {}, {}, {}
    big = [n for n, _, _ in BIG]
    for n in big:
        shp = w[n].shape
        two_d = (-1, shp[-1])
        d_, m_, v_ = _adamw(w[n].reshape(two_d), grads[n].reshape(two_d), m[n].reshape(two_d), v[n].reshape(two_d),
                            name=f"adamw_{n}")
        delta[n], new_m[n], new_v[n] = d_.reshape(shp), m_.reshape(shp), v_.reshape(shp)
    rest = [n for n in WEIGHTS if n not in big]

    def pack_rest(d):
        flat = jnp.concatenate([d[n].reshape(-1) for n in rest])
        return jnp.pad(flat, (0, SMALL_ROWS * 1024 - flat.shape[0])).reshape(SMALL_ROWS, 1024)

    outs = _adamw(pack_rest(w), pack_rest(grads), pack_rest(m), pack_rest(v), name="adamw_small")
    for dst, packed in zip((delta, new_m, new_v), outs):
        flat, off = packed.reshape(-1), 0
        for n in rest:
            size = math.prod(w[n].shape)
            dst[n] = flat[off:off + size].reshape(w[n].shape)
            off += size

    return (loss, grad_x[None], *[grads[n] for n in WEIGHTS], *[delta[n] for n in WEIGHTS],
            *[new_m[n] for n in WEIGHTS], *[new_v[n] for n in WEIGHTS])
```

```python
import functools
import math

import jax
import jax.numpy as jnp
from jax import lax
from jax.experimental import pallas as pl
from jax.experimental.pallas import tpu as pltpu

F32 = jnp.float32
BF16 = jnp.bfloat16
MESH_IDS = pl.DeviceIdType.MESH

D = 1024
DEPTH = 2
HEADS = 8
NOPE, ROPE, VDIM = 64, 32, 64
QK_SCALE = (NOPE + ROPE) ** -0.5
ROPE_BASE = 10000.0
SGU_G, SGU_C = 4, 128
HG_CHUNK = 64
ALPHA = (2 * DEPTH) ** 0.25
EPS = 1e-5
LR, B1, B2, ADAM_EPS, WD, STEP = 0.001, 0.9, 0.999, 1e-08, 0.01, 10
GELU_C = math.sqrt(2.0 / math.pi)
GELU_A = 0.044715
HI = lax.Precision.HIGHEST
MB = 1024 * 1024
ROW_BLOCK = 512

NT_DIMS = (((1,), (1,)), ((), ()))
TN_DIMS = (((0,), (0,)), ((), ()))

SHARDED = (
    ("w_in_e", (1, 1024, 392), 2), ("w_qb", (1, 256, 192), 2), ("w_kvb", (1, 256, 256), 2),
    ("w_out_e", (1, 256, 1024), 1), ("w_in_o", (1, 1024, 1024), 2), ("w_out_o", (1, 256, 1024), 1),
    ("w_ff1", (2, 1024, 1024), 2), ("w_ff2", (2, 1024, 1024), 1), ("hg_gnorm", (1, 256), 1),
)
PACK_ROWS = 6144
HALF_ROWS = PACK_ROWS // 2
SMALL = (("mla_gq", (1, 256)), ("mla_gkv", (1, 256)), ("sgu_ln_g", (1, 512)), ("sgu_ln_b", (1, 512)),
         ("sgu_w", (1, 4, 128, 128)), ("sgu_b", (1, 4, 128)), ("hg_lb", (2, 1024)), ("hg_gnorm", (1, 1024)),
         ("ln1_g", (2, 1024)), ("ln1_b", (2, 1024)), ("ln2_g", (2, 1024)), ("ln2_b", (2, 1024)))
SMALL_ROWS = 80


def _params(vmem_mb, n_axes=0):
    kw = dict(vmem_limit_bytes=vmem_mb * MB)
    if n_axes:
        kw["dimension_semantics"] = ("arbitrary",) * n_axes
    return pltpu.CompilerParams(**kw)


_ANY = pl.BlockSpec(memory_space=pl.ANY)


def _mesh_pos():
    return lax.axis_index("x"), lax.axis_index("y"), lax.axis_index("c")


class _Plan:
    def __init__(self, ins, outs, n_remote, n_local, start, wait, aliases=None):
        self.ins, self.outs, self.n_remote, self.n_local = list(ins), list(outs), n_remote, n_local
        self.start, self.wait, self.aliases = start, wait, dict(aliases or {})


def _join_plans(plans):
    ins, outs, aliases, parts = [], [], {}, []
    nr = nl = 0
    for p in plans:
        parts.append((p, len(ins), len(outs), nr, nl))
        aliases.update({len(ins) + i: len(outs) + o for i, o in p.aliases.items()})
        ins += p.ins
        outs += p.outs
        nr += p.n_remote
        nl += p.n_local

    def run(which):
        def go(in_refs, out_refs, send, recv, loc):
            for p, i0, o0, r0, l0 in parts:
                getattr(p, which)(in_refs[i0:i0 + len(p.ins)], out_refs[o0:o0 + len(p.outs)],
                                  lambda i, r0=r0: send(r0 + i), lambda i, r0=r0: recv(r0 + i),
                                  lambda i, l0=l0: loc(l0 + i))
        return go

    return _Plan(ins, outs, nr, nl, run("start"), run("wait"), aliases)


def _plan_io(plan, n_in, n_out):
    if plan is None:
        return [], [], [], [], {}
    sems = [pltpu.SemaphoreType.DMA((max(plan.n_remote, 1),)), pltpu.SemaphoreType.DMA((max(plan.n_remote, 1),)),
            pltpu.SemaphoreType.DMA((max(plan.n_local, 1),))]
    aliases = {n_in + i: n_out + o for i, o in plan.aliases.items()}
    return plan.ins, [_ANY] * len(plan.outs), plan.outs, sems, aliases


def _split_refs(refs, n_in, n_out, n_scr, plan):
    p_in, p_out = (len(plan.ins), len(plan.outs)) if plan is not None else (0, 0)
    refs = list(refs)
    ins, refs = refs[:n_in], refs[n_in:]
    pins, refs = refs[:p_in], refs[p_in:]
    outs, refs = refs[:n_out], refs[n_out:]
    pouts, refs = refs[:p_out], refs[p_out:]
    scr, psem = refs[:n_scr], refs[n_scr:]
    psem = tuple((lambda i, s=s: s.at[i]) for s in psem)
    return ins, outs, scr, (pins, pouts, psem)


def _grid_edge(grid, last):
    cond = None
    for ax, n in enumerate(grid):
        c = pl.program_id(ax) == (n - 1 if last else 0)
        cond = c if cond is None else cond & c
    return cond


def _plan_start(plan, pctx, grid):
    if plan is not None:
        pins, pouts, psem = pctx
        pl.when(_grid_edge(grid, False))(lambda: plan.start(pins, pouts, *psem))


def _plan_wait(plan, pctx, grid):
    if plan is not None:
        pins, pouts, psem = pctx
        pl.when(_grid_edge(grid, True))(lambda: plan.wait(pins, pouts, *psem))


def _run_plan(plan, *, name):
    def body(*refs):
        _, _, _, (pins, pouts, psem) = _split_refs(refs, 0, 0, 0, plan)
        plan.start(pins, pouts, *psem)
        plan.wait(pins, pouts, *psem)

    p_in, p_ospec, p_oshape, p_scr, p_alias = _plan_io(plan, 0, 0)
    return pl.pallas_call(body, name=name, in_specs=[_ANY] * len(p_in), out_specs=p_ospec, out_shape=p_oshape,
                          scratch_shapes=p_scr, input_output_aliases=p_alias)(*p_in)


def _fold8(x):
    return x.reshape(x.shape[0] // 8, 8, x.shape[1]).sum(axis=0)


def _ln_stats(r):
    mu = jnp.mean(r, -1, keepdims=True)
    xc = r - mu
    rstd = lax.rsqrt(jnp.mean(xc * xc, -1, keepdims=True) + EPS)
    return xc * rstd, rstd


def _sigmoid(x):
    return 1.0 / (1.0 + jnp.exp(-x))


def _gelu(x):
    return 0.5 * x * (1.0 + jnp.tanh(GELU_C * (x + GELU_A * x * x * x)))


def _gelu_grad(x):
    t = jnp.tanh(GELU_C * (x + GELU_A * x * x * x))
    return 0.5 * (1.0 + t) + 0.5 * x * (1.0 - t * t) * GELU_C * (1.0 + 3.0 * GELU_A * x * x)


def _matmul(a, b, *, name, M, N, K, ta=False, tb=False, out_dtype=F32, tm=512, tn=1024, tk=1024,
            a_spec=None, b_spec=None, b_merge=None, out_shape=None, o_spec=None, into=None,
            a_sq=False, mul=None, add=None, add_scale=1.0, plan=None):
    tm, tn, tk = min(tm, M), min(tn, N), min(tk, K)
    assert M % tm == 0 and N % tn == 0 and K % tk == 0
    grid = (M // tm, N // tn, K // tk)
    nk = grid[2]
    if a_spec is None:
        a_spec = pl.BlockSpec((tk, tm), lambda i, j, k: (k, i)) if ta else pl.BlockSpec((tm, tk), lambda i, j, k: (i, k))
    if b_spec is None:
        b_spec = pl.BlockSpec((tn, tk), lambda i, j, k: (j, k)) if tb else pl.BlockSpec((tk, tn), lambda i, j, k: (k, j))
    if o_spec is None:
        o_spec = pl.BlockSpec((tm, tn), lambda i, j, k: (i, j))
        out_shape = jax.ShapeDtypeStruct((M, N), out_dtype)
    e_spec = pl.BlockSpec((tm, tn), lambda i, j, k: (i, j))
    dims = (((0 if ta else 1,), (1 if tb else 0,)), ((), ()))
    extra = [e for e in (mul, add, into) if e is not None]
    n_in = 2 + len(extra)

    def body(*refs):
        ins, outs, scr, pctx = _split_refs(refs, n_in, 1, 1 if nk > 1 else 0, plan)
        a_ref, b_ref = ins[0], ins[1]
        rest = list(ins[2:])
        mul_ref = rest.pop(0) if mul is not None else None
        add_ref = rest.pop(0) if add is not None else None
        o_ref = outs[0]
        _plan_start(plan, pctx, grid)
        av = a_ref[...]
        if a_sq:
            av = av * av
        bv = b_ref[...]
        if b_merge is not None:
            bv = bv.reshape(b_merge)
        p = lax.dot_general(av, bv, dims, preferred_element_type=F32)

        def finish(r):
            if mul_ref is not None:
                r = r * (2.0 * mul_ref[...].astype(F32))
            if add_ref is not None:
                r = r + add_scale * add_ref[...]
            o_ref[...] = r.astype(o_ref.dtype)

        if nk == 1:
            finish(p)
        else:
            acc_ref = scr[0]
            k = pl.program_id(2)

            @pl.when(k == 0)
            def _():
                acc_ref[...] = p

            @pl.when(k > 0)
            def _():
                acc_ref[...] += p

            @pl.when(k == nk - 1)
            def _():
                finish(acc_ref[...])

        _plan_wait(plan, pctx, grid)

    p_in, p_ospec, p_oshape, p_scr, p_alias = _plan_io(plan, n_in, 1)
    aliases = dict(p_alias)
    if into is not None:
        aliases[n_in - 1] = 0
    return pl.pallas_call(
        body, name=name, grid=grid,
        in_specs=[a_spec, b_spec] + [e_spec] * (len(extra) - (into is not None)) + [_ANY] * (into is not None)
        + [_ANY] * len(p_in),
        out_specs=[o_spec] + p_ospec, out_shape=[out_shape] + p_oshape,
        scratch_shapes=([pltpu.VMEM((tm, tn), F32)] if nk > 1 else []) + p_scr,
        input_output_aliases=aliases, compiler_params=_params(48, 3),
    )(a, b, *extra, *p_in)


def _rows4_spec(rowblk, n_axes):
    return pl.BlockSpec((4, 256, D), lambda *_: (0, rowblk, 0))


def _proj_ln(a_b, w, h_prev, g, b, *, name, w_rowblk=None, plan=None):
    T = a_b.shape[0]
    tm = min(ROW_BLOCK, T)
    grid = (T // tm,)
    row = pl.BlockSpec((tm, D), lambda i: (i, 0))
    vec = pl.BlockSpec((1, D), lambda i: (0, 0))
    w_spec = pl.BlockSpec((D, D), lambda i: (0, 0)) if w_rowblk is None else _rows4_spec(w_rowblk, 1)

    def body(*refs):
        (a_ref, w_ref, h_ref, g_ref, b_ref), (r_ref, ho_ref, hb_ref), _, pctx = _split_refs(refs, 5, 3, 0, plan)
        _plan_start(plan, pctx, grid)
        mix = jnp.dot(a_ref[...], w_ref[...].reshape(D, D), preferred_element_type=F32)
        r = ALPHA * h_ref[...] + mix
        xhat, _ = _ln_stats(r)
        y = xhat * g_ref[...] + b_ref[...]
        r_ref[...] = r
        ho_ref[...] = y
        hb_ref[...] = y.astype(BF16)
        _plan_wait(plan, pctx, grid)

    p_in, p_ospec, p_oshape, p_scr, p_alias = _plan_io(plan, 5, 3)
    return pl.pallas_call(
        body, name=name, grid=grid,
        in_specs=[row, w_spec, row, vec, vec] + [_ANY] * len(p_in),
        out_specs=[row, row, row] + p_ospec,
        out_shape=[jax.ShapeDtypeStruct((T, D), F32), jax.ShapeDtypeStruct((T, D), F32),
                   jax.ShapeDtypeStruct((T, D), BF16)] + p_oshape,
        scratch_shapes=p_scr, input_output_aliases=p_alias, compiler_params=_params(40, 1),
    )(a_b, w, h_prev, g, b, *p_in)


def _ffn_ln(h_b, wbuf, h, g, b, *, name):
    T = h_b.shape[0]
    tm, tf = min(ROW_BLOCK, T), 1024
    nf = 4
    F = nf * tf
    row = pl.BlockSpec((tm, D), lambda i, j: (i, 0))
    vec = pl.BlockSpec((1, D), lambda i, j: (0, 0))

    def body(hb_ref, w1_ref, w2_ref, h_ref, g_ref, b_ref, ra_ref, r_ref, ho_ref, hbo_ref, acc_ref):
        j = pl.program_id(1)
        a = jnp.dot(hb_ref[...], w1_ref[...], preferred_element_type=F32)
        ra = jnp.maximum(a, 0.0)
        ra_ref[...] = ra.astype(BF16)
        p = jnp.dot((ra * ra).astype(BF16), w2_ref[...], preferred_element_type=F32)

        @pl.when(j == 0)
        def _():
            acc_ref[...] = p

        @pl.when(j > 0)
        def _():
            acc_ref[...] += p

        @pl.when(j == nf - 1)
        def _():
            r = ALPHA * h_ref[...] + acc_ref[...]
            xhat, _ = _ln_stats(r)
            y = xhat * g_ref[...] + b_ref[...]
            r_ref[...] = r
            ho_ref[...] = y
            hbo_ref[...] = y.astype(BF16)

    return pl.pallas_call(
        body, name=name, grid=(T // tm, nf),
        in_specs=[row, pl.BlockSpec((None, D, tf), lambda i, j: (j, 0, 0)),
                  pl.BlockSpec((None, tf, D), lambda i, j: (j, 1, 0)), row, vec, vec],
        out_specs=[pl.BlockSpec((tm, tf), lambda i, j: (i, j)), row, row, row],
        out_shape=[jax.ShapeDtypeStruct((T, F), BF16), jax.ShapeDtypeStruct((T, D), F32),
                   jax.ShapeDtypeStruct((T, D), F32), jax.ShapeDtypeStruct((T, D), BF16)],
        scratch_shapes=[pltpu.VMEM((tm, D), F32)],
        compiler_params=_params(48, 2),
    )(h_b, wbuf, wbuf, h, g, b)


def _loss_dy(y, tgt):
    T = y.shape[0]
    tm = min(ROW_BLOCK, T)
    row = pl.BlockSpec((tm, D), lambda i: (i, 0))

    def body(y_ref, t_ref, dy_ref, ls_ref):
        e = y_ref[...] - t_ref[...]
        dy_ref[...] = e * (1.0 / D)

        @pl.when(pl.program_id(0) == 0)
        def _():
            ls_ref[...] = jnp.zeros_like(ls_ref)

        ls_ref[...] += _fold8(e * e)

    return pl.pallas_call(
        body, name="loss_dy", grid=(T // tm,), in_specs=[row, row],
        out_specs=[row, pl.BlockSpec((8, D), lambda i: (0, 0))],
        out_shape=[jax.ShapeDtypeStruct((T, D), F32), jax.ShapeDtypeStruct((8, D), F32)],
        compiler_params=_params(32, 1),
    )(y, tgt)


def _ln_bwd(dy, r, g, *, name):
    T = dy.shape[0]
    tm = min(ROW_BLOCK, T)
    row = pl.BlockSpec((tm, D), lambda i: (i, 0))
    acc = pl.BlockSpec((8, D), lambda i: (0, 0))

    def body(dy_ref, r_ref, g_ref, dr_ref, drb_ref, dg_ref, db_ref):
        @pl.when(pl.program_id(0) == 0)
        def _():
            dg_ref[...] = jnp.zeros_like(dg_ref)
            db_ref[...] = jnp.zeros_like(db_ref)

        dy_ = dy_ref[...]
        xhat, rstd = _ln_stats(r_ref[...])
        dxh = dy_ * g_ref[...]
        m1 = jnp.mean(dxh, -1, keepdims=True)
        m2 = jnp.mean(dxh * xhat, -1, keepdims=True)
        dr = rstd * (dxh - m1 - xhat * m2)
        dr_ref[...] = dr
        drb_ref[...] = dr.astype(BF16)
        dg_ref[...] += _fold8(dy_ * xhat)
        db_ref[...] += _fold8(dy_)

    return pl.pallas_call(
        body, name=name, grid=(T // tm,),
        in_specs=[row, row, pl.BlockSpec((1, D), lambda i: (0, 0))],
        out_specs=[row, row, acc, acc],
        out_shape=[jax.ShapeDtypeStruct((T, D), F32), jax.ShapeDtypeStruct((T, D), BF16),
                   jax.ShapeDtypeStruct((8, D), F32), jax.ShapeDtypeStruct((8, D), F32)],
        compiler_params=_params(40, 1),
    )(dy, r, g)


def _rope(x, c, s1, s2):
    return x * c + pltpu.roll(x, 112, 1) * s1 + pltpu.roll(x, 16, 1) * s2


def _rope_t(dy, c, s1, s2):
    return dy * c + pltpu.roll(dy * s1, 16, 1) + pltpu.roll(dy * s2, 112, 1)


def _rms(x, g):
    rstd = lax.rsqrt(jnp.mean(x * x, -1, keepdims=True) + EPS)
    xhat = x * rstd
    return xhat * g, xhat, rstd


def _mla_prep(z0, gq, gkv, wq, wk, wv, rc, rs1, rs2):
    T = z0.shape[0]
    tm = min(ROW_BLOCK, T)
    HW = HEADS * 128

    def body(cq_ref, ckv_ref, kr_ref, gq_ref, gkv_ref, wq_ref, wk_ref, wv_ref, c_ref, s1_ref, s2_ref,
             q_ref, k_ref, v_ref):
        nq = _rms(cq_ref[...], gq_ref[...])[0].astype(BF16)
        nkv = _rms(ckv_ref[...], gkv_ref[...])[0].astype(BF16)
        q = jnp.dot(nq, wq_ref[...], preferred_element_type=F32)
        k = jnp.dot(nkv, wk_ref[...], preferred_element_type=F32)
        v = jnp.dot(nkv, wv_ref[...], preferred_element_type=F32)
        c, s1, s2 = c_ref[...], s1_ref[...], s2_ref[...]
        kr = _rope(pltpu.roll(kr_ref[...], 64, 1), c, s1, s2)
        for h in range(HEADS):
            sl = slice(h * 128, (h + 1) * 128)
            q_ref[:, sl] = (_rope(q[:, sl], c, s1, s2) * QK_SCALE).astype(BF16)
            k_ref[:, sl] = (k[:, sl] + kr).astype(BF16)
        v_ref[...] = v.astype(BF16)

    full = lambda shape: pl.BlockSpec(shape, lambda i: (0, 0))
    tab = pl.BlockSpec((tm, 128), lambda i: (i, 0))
    return pl.pallas_call(
        body, name="mla_prep", grid=(T // tm,),
        in_specs=[pl.BlockSpec((tm, 256), lambda i: (i, 0)), pl.BlockSpec((tm, 256), lambda i: (i, 1)),
                  pl.BlockSpec((tm, 128), lambda i: (i, 12)), full((1, 256)), full((1, 256)),
                  full((256, HW)), full((256, HW)), full((256, 512)), tab, tab, tab],
        out_specs=[pl.BlockSpec((tm, HW), lambda i: (i, 0)), pl.BlockSpec((tm, HW), lambda i: (i, 0)),
                   pl.BlockSpec((tm, 512), lambda i: (i, 0))],
        out_shape=[jax.ShapeDtypeStruct((T, HW), BF16), jax.ShapeDtypeStruct((T, HW), BF16),
                   jax.ShapeDtypeStruct((T, 512), BF16)],
        compiler_params=_params(40, 1),
    )(z0, z0, z0, gq, gkv, wq, wk, wv, rc, rs1, rs2)


def _flash_fwd(q, k, v, plan=None):
    T = q.shape[0]
    bq = min(ROW_BLOCK, T)
    nq = T // bq
    grid = (4, nq, nq)

    def body(*refs):
        (q_ref, k_ref, v_ref), (o_ref, lse_ref), (m_sc, l_sc, acc_sc), pctx = _split_refs(refs, 3, 2, 3, plan)
        _plan_start(plan, pctx, grid)
        i, j = pl.program_id(1), pl.program_id(2)
        first = lax.broadcasted_iota(jnp.int32, (bq, 128), 1) < 64

        @pl.when(j == 0)
        def _():
            m_sc[...] = jnp.full_like(m_sc, -jnp.inf)
            l_sc[...] = jnp.zeros_like(l_sc)
            acc_sc[...] = jnp.zeros_like(acc_sc)

        def step(masked):
            vp = v_ref[...]
            acc = acc_sc[...]
            for h in range(2):
                sl = slice(h * 128, (h + 1) * 128)
                s = lax.dot_general(q_ref[:, sl], k_ref[:, sl], NT_DIMS, preferred_element_type=F32)
                if masked:
                    rows = lax.broadcasted_iota(jnp.int32, (bq, bq), 0)
                    cols = lax.broadcasted_iota(jnp.int32, (bq, bq), 1)
                    s = jnp.where(cols <= rows, s, -jnp.inf)
                m_prev = m_sc[h, :, 0:1]
                m_new = jnp.maximum(m_prev, jnp.max(s, axis=1, keepdims=True))
                alpha = jnp.exp(m_prev - m_new)
                p = jnp.exp(s - m_new)
                l_new = alpha * l_sc[h, :, 0:1] + jnp.sum(p, axis=1, keepdims=True)
                pv = jnp.dot(p.astype(BF16), vp, preferred_element_type=F32)
                mine = first if h == 0 else jnp.logical_not(first)
                acc = jnp.where(mine, acc * alpha + pv, acc)
                m_sc[h] = jnp.broadcast_to(m_new, (bq, 128))
                l_sc[h] = jnp.broadcast_to(l_new, (bq, 128))
            acc_sc[...] = acc

        @pl.when(j < i)
        def _():
            step(False)

        @pl.when(j == i)
        def _():
            step(True)
            l0, l1 = l_sc[0], l_sc[1]
            o_ref[...] = (acc_sc[...] / jnp.where(first, l0, l1)).astype(BF16)
            lse_ref[...] = jnp.where(first, m_sc[0] + jnp.log(l0), m_sc[1] + jnp.log(l1))

        _plan_wait(plan, pctx, grid)

    kv = lambda hp, i, j: (jnp.minimum(i, j), hp)
    p_in, p_ospec, p_oshape, p_scr, p_alias = _plan_io(plan, 3, 2)
    return pl.pallas_call(
        body, name="flash_fwd", grid=grid,
        in_specs=[pl.BlockSpec((bq, 256), lambda hp, i, j: (i, hp)), pl.BlockSpec((bq, 256), kv),
                  pl.BlockSpec((bq, 128), kv)] + [_ANY] * len(p_in),
        out_specs=[pl.BlockSpec((bq, 128), lambda hp, i, j: (i, hp)),
                   pl.BlockSpec((bq, 128), lambda hp, i, j: (i, hp))] + p_ospec,
        out_shape=[jax.ShapeDtypeStruct((T, 512), BF16), jax.ShapeDtypeStruct((T, 512), F32)] + p_oshape,
        scratch_shapes=[pltpu.VMEM((2, bq, 128), F32), pltpu.VMEM((2, bq, 128), F32), pltpu.VMEM((bq, 128), F32)] + p_scr,
        input_output_aliases=p_alias, compiler_params=_params(32, 3),
    )(q, k, v, *p_in)


def _attn_delta(dmix, o):
    T = o.shape[0]
    tm = min(ROW_BLOCK, T)
    blk = pl.BlockSpec((tm, 512), lambda i: (i, 0))

    def body(do_ref, o_ref, delta_ref, dob_ref):
        first = lax.broadcasted_iota(jnp.int32, (tm, 128), 1) < 64
        for hp in range(4):
            sl = slice(hp * 128, (hp + 1) * 128)
            prod = do_ref[:, sl] * o_ref[:, sl].astype(F32)
            d0 = jnp.sum(jnp.where(first, prod, 0.0), axis=1, keepdims=True)
            d1 = jnp.sum(jnp.where(first, 0.0, prod), axis=1, keepdims=True)
            delta_ref[:, sl] = jnp.where(first, d0, d1)
        dob_ref[...] = do_ref[...].astype(BF16)

    return pl.pallas_call(
        body, name="attn_delta", grid=(T // tm,), in_specs=[blk, blk], out_specs=[blk, blk],
        out_shape=[jax.ShapeDtypeStruct((T, 512), F32), jax.ShapeDtypeStruct((T, 512), BF16)],
        compiler_params=_params(32, 1),
    )(dmix, o)


def _flash_bwd(q, k, v, do_b, lse, delta, plan=None):
    T = q.shape[0]
    bq = min(ROW_BLOCK, T)
    nq = T // bq
    grid = (4, nq, nq)

    def body(*refs):
        ((q_ref, k_ref, v_ref, do_ref, lse_ref, dl_ref), (dq_hbm, dk_ref, dv_ref), (dq_sc, dk_sc, dv_sc, sem),
         pctx) = _split_refs(refs, 6, 3, 4, plan)
        _plan_start(plan, pctx, grid)
        hp, j, i = pl.program_id(0), pl.program_id(1), pl.program_id(2)
        first = lax.broadcasted_iota(jnp.int32, (bq, 128), 1) < 64

        @pl.when((j == 0) & (i == 0))
        def _():
            dq_sc[...] = jnp.zeros_like(dq_sc)

        @pl.when(i == j)
        def _():
            dk_sc[...] = jnp.zeros_like(dk_sc)
            dv_sc[...] = jnp.zeros_like(dv_sc)

        def step(masked):
            vp = v_ref[...]
            do = do_ref[...]
            for h in range(2):
                sl = slice(h * 128, (h + 1) * 128)
                qh, kh = q_ref[:, sl], k_ref[:, sl]
                s = lax.dot_general(qh, kh, NT_DIMS, preferred_element_type=F32)
                p = jnp.exp(s - lse_ref[:, h * 64:h * 64 + 1])
                if masked:
                    rows = lax.broadcasted_iota(jnp.int32, (bq, bq), 0)
                    cols = lax.broadcasted_iota(jnp.int32, (bq, bq), 1)
                    p = jnp.where(cols <= rows, p, 0.0)
                mine = first if h == 0 else jnp.logical_not(first)
                do_h = jnp.where(mine, do, jnp.zeros_like(do))
                dv_sc[...] += lax.dot_general(p.astype(BF16), do_h, TN_DIMS, preferred_element_type=F32)
                dp = lax.dot_general(do_h, vp, NT_DIMS, preferred_element_type=F32)
                ds = (p * (dp - dl_ref[:, h * 64:h * 64 + 1])).astype(BF16)
                dq_sc[i, :, sl] += jnp.dot(ds, kh, preferred_element_type=F32)
                dk_sc[:, sl] += lax.dot_general(ds, qh, TN_DIMS, preferred_element_type=F32)

        @pl.when(i > j)
        def _():
            step(False)

        @pl.when(i == j)
        def _():
            step(True)

        @pl.when(i == nq - 1)
        def _():
            dk_ref[...] = dk_sc[...]
            dv_ref[...] = dv_sc[...]

        @pl.when((j == nq - 1) & (i == nq - 1))
        def _():
            cp = pltpu.make_async_copy(dq_sc, dq_hbm.at[hp], sem)
            cp.start()
            cp.wait()

        _plan_wait(plan, pctx, grid)

    qi = lambda hp, j, i: (jnp.maximum(i, j), hp)
    kj = lambda hp, j, i: (j, hp)
    p_in, p_ospec, p_oshape, p_scr, p_alias = _plan_io(plan, 6, 3)
    return pl.pallas_call(
        body, name="flash_bwd", grid=grid,
        in_specs=[pl.BlockSpec((bq, 256), qi), pl.BlockSpec((bq, 256), kj), pl.BlockSpec((bq, 128), kj),
                  pl.BlockSpec((bq, 128), qi), pl.BlockSpec((bq, 128), qi), pl.BlockSpec((bq, 128), qi)]
        + [_ANY] * len(p_in),
        out_specs=[_ANY, pl.BlockSpec((bq, 256), kj), pl.BlockSpec((bq, 128), kj)] + p_ospec,
        out_shape=[jax.ShapeDtypeStruct((4, nq, bq, 256), F32), jax.ShapeDtypeStruct((T, 1024), F32),
                   jax.ShapeDtypeStruct((T, 512), F32)] + p_oshape,
        scratch_shapes=[pltpu.VMEM((nq, bq, 256), F32), pltpu.VMEM((bq, 256), F32), pltpu.VMEM((bq, 128), F32),
                        pltpu.SemaphoreType.DMA] + p_scr,
        input_output_aliases=p_alias, compiler_params=_params(40, 3),
    )(q, k, v, do_b, lse, delta, *p_in)


def _mla_bwd(z0, dq4, dk, dv, gq, gkv, wq, wk, wv, rc, rs1, rs2, plan=None):
    T = z0.shape[0]
    tm = dq4.shape[2]
    HW = HEADS * 128
    grid = (T // tm,)

    def body(*refs):
        ((cq_ref, ckv_ref, dq_ref, dk_ref, dv_ref, gq_ref, gkv_ref, wq_ref, wk_ref, wv_ref, c_ref, s1_ref, s2_ref),
         (dc_ref, dkr_ref, dwq_ref, dwk_ref, dwv_ref, dgq_ref, dgkv_ref), _, pctx) = _split_refs(refs, 13, 7, 0, plan)
        _plan_start(plan, pctx, grid)

        @pl.when(pl.program_id(0) == 0)
        def _():
            for ref in (dwq_ref, dwk_ref, dwv_ref, dgq_ref, dgkv_ref):
                ref[...] = jnp.zeros_like(ref)

        c, s1, s2 = c_ref[...], s1_ref[...], s2_ref[...]
        lane = lax.broadcasted_iota(jnp.int32, (tm, 128), 1)
        nq, xq, rq = _rms(cq_ref[...], gq_ref[...])
        nkv, xkv, rkv = _rms(ckv_ref[...], gkv_ref[...])
        nq_b, nkv_b = nq.astype(BF16), nkv.astype(BF16)

        dq_parts, dk_parts = [], []
        dkr = jnp.zeros((tm, 128), F32)
        for h in range(HEADS):
            blk = dq_ref[h // 2, :, (h % 2) * 128:(h % 2 + 1) * 128] * QK_SCALE
            dq_parts.append(_rope_t(blk, c, s1, s2).astype(BF16))
            kb = dk_ref[:, h * 128:(h + 1) * 128]
            dk_parts.append(jnp.where(lane < NOPE, kb, 0.0).astype(BF16))
            dkr = dkr + kb
        dq_b = jnp.concatenate(dq_parts, axis=1)
        dk_b = jnp.concatenate(dk_parts, axis=1)
        dv_b = dv_ref[...].astype(BF16)

        dwq_ref[...] += lax.dot_general(nq_b, dq_b, TN_DIMS, preferred_element_type=F32)
        dwk_ref[...] += lax.dot_general(nkv_b, dk_b, TN_DIMS, preferred_element_type=F32)
        dwv_ref[...] += lax.dot_general(nkv_b, dv_b, TN_DIMS, preferred_element_type=F32)
        dnq = lax.dot_general(dq_b, wq_ref[...], NT_DIMS, preferred_element_type=F32)
        dnkv = (lax.dot_general(dk_b, wk_ref[...], NT_DIMS, preferred_element_type=F32)
                + lax.dot_general(dv_b, wv_ref[...], NT_DIMS, preferred_element_type=F32))

        def rms_bwd(dn, xhat, rstd, g):
            dxh = dn * g
            return rstd * (dxh - xhat * jnp.mean(dxh * xhat, -1, keepdims=True))

        dc_ref[:, :256] = rms_bwd(dnq, xq, rq, gq_ref[...]).astype(BF16)
        dc_ref[:, 256:] = rms_bwd(dnkv, xkv, rkv, gkv_ref[...]).astype(BF16)
        dgq_ref[...] += _fold8(dnq * xq)
        dgkv_ref[...] += _fold8(dnkv * xkv)
        dkr = pltpu.roll(_rope_t(dkr, c, s1, s2), 64, 1)
        dkr_ref[...] = jnp.where(lane < ROPE, dkr, 0.0).astype(BF16)
        _plan_wait(plan, pctx, grid)

    full = lambda shape: pl.BlockSpec(shape, lambda i: (0,) * len(shape))
    tab = pl.BlockSpec((tm, 128), lambda i: (i, 0))
    p_in, p_ospec, p_oshape, p_scr, p_alias = _plan_io(plan, 13, 7)
    return pl.pallas_call(
        body, name="mla_bwd", grid=grid,
        in_specs=[pl.BlockSpec((tm, 256), lambda i: (i, 0)), pl.BlockSpec((tm, 256), lambda i: (i, 1)),
                  pl.BlockSpec((4, None, tm, 256), lambda i: (0, i, 0, 0)),
                  pl.BlockSpec((tm, HW), lambda i: (i, 0)), pl.BlockSpec((tm, 512), lambda i: (i, 0)),
                  full((1, 256)), full((1, 256)), full((256, HW)), full((256, HW)), full((256, 512)), tab, tab, tab]
        + [_ANY] * len(p_in),
        out_specs=[pl.BlockSpec((tm, 512), lambda i: (i, 0)), tab, full((256, HW)), full((256, HW)),
                   full((256, 512)), full((8, 256)), full((8, 256))] + p_ospec,
        out_shape=[jax.ShapeDtypeStruct((T, 512), BF16), jax.ShapeDtypeStruct((T, 128), BF16),
                   jax.ShapeDtypeStruct((256, HW), F32), jax.ShapeDtypeStruct((256, HW), F32),
                   jax.ShapeDtypeStruct((256, 512), F32), jax.ShapeDtypeStruct((8, 256), F32),
                   jax.ShapeDtypeStruct((8, 256), F32)] + p_oshape,
        scratch_shapes=p_scr, input_output_aliases=p_alias, compiler_params=_params(48, 1),
    )(z0, z0, dq4, dk, dv, gq, gkv, wq, wk, wv, rc, rs1, rs2, *p_in)


def _sgu_fwd(z0, a_out, ln_g, ln_b, w, b_t):
    T = z0.shape[0]
    tm = min(ROW_BLOCK, T)
    W = SGU_G * SGU_C

    def body(u_ref, v_ref, a_ref, g_ref, b_ref, w_ref, bt_ref, o_ref):
        o_ref[:, :W] = a_ref[...]
        ug = _gelu(u_ref[...])
        xhat, _ = _ln_stats(_gelu(v_ref[...]))
        vn = (xhat * g_ref[...] + b_ref[...]).astype(BF16)
        tril = lax.broadcasted_iota(jnp.int32, (SGU_C, SGU_C), 0) >= lax.broadcasted_iota(jnp.int32, (SGU_C, SGU_C), 1)
        for g in range(SGU_G):
            cs = slice(g * SGU_C, (g + 1) * SGU_C)
            wg = jnp.where(tril, w_ref[g], 0.0).astype(BF16)
            bcol = bt_ref[:, g:g + 1]
            for c in range(tm // SGU_C):
                rs = slice(c * SGU_C, (c + 1) * SGU_C)
                mixed = jnp.dot(wg, vn[rs, cs], preferred_element_type=F32) + bcol
                o_ref[rs, W + g * SGU_C:W + (g + 1) * SGU_C] = (ug[rs, cs] * mixed).astype(BF16)

    full = lambda shape: pl.BlockSpec(shape, lambda i: (0,) * len(shape))
    return pl.pallas_call(
        body, name="sgu_fwd", grid=(T // tm,),
        in_specs=[pl.BlockSpec((tm, W), lambda i: (i, 1)), pl.BlockSpec((tm, W), lambda i: (i, 2)),
                  pl.BlockSpec((tm, W), lambda i: (i, 0)),
                  full((1, W)), full((1, W)), full((SGU_G, SGU_C, SGU_C)), full((SGU_C, SGU_G))],
        out_specs=pl.BlockSpec((tm, 2 * W), lambda i: (i, 0)),
        out_shape=jax.ShapeDtypeStruct((T, 2 * W), BF16),
        compiler_params=_params(32, 1),
    )(z0, z0, a_out, ln_g, ln_b, w, b_t)


def _sgu_bwd(z0, dmix, dc, dkr, ln_g, ln_b, w, b_t):
    T = z0.shape[0]
    tm = min(ROW_BLOCK, T)
    W = SGU_G * SGU_C

    def body(u_ref, v_ref, do_ref, dc_ref, dkr_ref, g_ref, b_ref, w_ref, bt_ref, dz_ref, dw_ref, db_ref, dlg_ref,
             dlb_ref):
        @pl.when(pl.program_id(0) == 0)
        def _():
            for ref in (dw_ref, db_ref, dlg_ref, dlb_ref):
                ref[...] = jnp.zeros_like(ref)

        dz_ref[:, :W] = dc_ref[...]
        dz_ref[:, 3 * W:] = dkr_ref[...]

        u, v, dout = u_ref[...], v_ref[...], do_ref[...]
        ug = _gelu(u)
        xhat, rstd = _ln_stats(_gelu(v))
        vn = (xhat * g_ref[...] + b_ref[...]).astype(BF16)
        dmixed = dout * ug
        dmixed_b = dmixed.astype(BF16)
        tril = lax.broadcasted_iota(jnp.int32, (SGU_C, SGU_C), 0) >= lax.broadcasted_iota(jnp.int32, (SGU_C, SGU_C), 1)
        lane = lax.broadcasted_iota(jnp.int32, (SGU_C, SGU_C), 1)
        dvn_cols = []
        for g in range(SGU_G):
            cs = slice(g * SGU_C, (g + 1) * SGU_C)
            wg = jnp.where(tril, w_ref[g], 0.0).astype(BF16)
            bcol = bt_ref[:, g:g + 1]
            dw_g = jnp.zeros((SGU_C, SGU_C), F32)
            db_g = jnp.zeros((SGU_C, 1), F32)
            dvn_rows = []
            for c in range(tm // SGU_C):
                rs = slice(c * SGU_C, (c + 1) * SGU_C)
                mixed = jnp.dot(wg, vn[rs, cs], preferred_element_type=F32) + bcol
                dz_ref[rs, W + g * SGU_C:W + (g + 1) * SGU_C] = (dout[rs, cs] * mixed * _gelu_grad(u[rs, cs])).astype(BF16)
                dm = dmixed_b[rs, cs]
                dw_g = dw_g + lax.dot_general(dm, vn[rs, cs], NT_DIMS, preferred_element_type=F32)
                db_g = db_g + jnp.sum(dmixed[rs, cs], axis=1, keepdims=True)
                dvn_rows.append(lax.dot_general(wg, dm, TN_DIMS, preferred_element_type=F32))
            dw_ref[g] += jnp.where(tril, dw_g, 0.0)
            db_ref[...] += jnp.where(lane == g, db_g, 0.0)
            dvn_cols.append(jnp.concatenate(dvn_rows, axis=0))
        dvn = jnp.concatenate(dvn_cols, axis=1)
        dxh = dvn * g_ref[...]
        m1 = jnp.mean(dxh, -1, keepdims=True)
        m2 = jnp.mean(dxh * xhat, -1, keepdims=True)
        dvg = rstd * (dxh - m1 - xhat * m2)
        dz_ref[:, 2 * W:3 * W] = (dvg * _gelu_grad(v)).astype(BF16)
        dlg_ref[...] += _fold8(dvn * xhat)
        dlb_ref[...] += _fold8(dvn)

    full = lambda shape: pl.BlockSpec(shape, lambda i: (0,) * len(shape))
    return pl.pallas_call(
        body, name="sgu_bwd", grid=(T // tm,),
        in_specs=[pl.BlockSpec((tm, W), lambda i: (i, 1)), pl.BlockSpec((tm, W), lambda i: (i, 2)),
                  pl.BlockSpec((tm, W), lambda i: (i, 1)), pl.BlockSpec((tm, W), lambda i: (i, 0)),
                  pl.BlockSpec((tm, 128), lambda i: (i, 0)),
                  full((1, W)), full((1, W)), full((SGU_G, SGU_C, SGU_C)), full((SGU_C, SGU_G))],
        out_specs=[pl.BlockSpec((tm, 3 * W + 128), lambda i: (i, 0)), full((SGU_G, SGU_C, SGU_C)),
                   full((SGU_C, SGU_C)), full((8, W)), full((8, W))],
        out_shape=[jax.ShapeDtypeStruct((T, 3 * W + 128), BF16), jax.ShapeDtypeStruct((SGU_G, SGU_C, SGU_C), F32),
                   jax.ShapeDtypeStruct((SGU_C, SGU_C), F32), jax.ShapeDtypeStruct((8, W), F32),
                   jax.ShapeDtypeStruct((8, W), F32)],
        compiler_params=_params(40, 1),
    )(z0, z0, dmix, dc, dkr, ln_g, ln_b, w, b_t)


def _hg_lower_bound(lb_ref):
    a0, a1 = lb_ref[0:1, :], lb_ref[1:2, :]
    m = jnp.maximum(a0, a1)
    e0, e1 = jnp.exp(a0 - m), jnp.exp(a1 - m)
    return e1 / (e0 + e1)


def _hg_chunk(qc, fc, lb):
    C = HG_CHUNK
    rows = lax.broadcasted_iota(jnp.int32, (C, C), 0)
    cols = lax.broadcasted_iota(jnp.int32, (C, C), 1)
    rowid = lax.broadcasted_iota(jnp.int32, (C, 128), 0)
    sq, sg = _sigmoid(qc), _sigmoid(fc)
    qf = qc * sq
    gate = lb + (1.0 - lb) * sg
    kk = 1.0 - gate
    lg = jnp.log(gate)
    bcum = jnp.dot((rows >= cols).astype(F32), lg, precision=HI, preferred_element_type=F32)
    b_mid = jnp.sum(jnp.where(rowid < C // 2, lg, 0.0), axis=0, keepdims=True)
    b_last = jnp.sum(lg, axis=0, keepdims=True)
    eq, ek, e, eh = jnp.exp(bcum - b_mid), jnp.exp(b_mid - bcum), jnp.exp(bcum), jnp.exp(b_last - bcum)
    qt, kt, qe, khat = qf * eq, kk * ek, qf * e, kk * eh
    a = lax.dot_general(qt.astype(BF16), kt.astype(BF16), NT_DIMS, preferred_element_type=F32)
    a = jnp.where(rows >= cols, a, 0.0)
    return dict(sq=sq, sg=sg, gate=gate, kk=kk, eq=eq, ek=ek, e=e, eh=eh, qt=qt, kt=kt, qe=qe, khat=khat, a=a,
                e_last=jnp.exp(b_last), tril=rows >= cols, rowid=rowid)


def _hgrn_fwd(z4, hg_lb, gnorm):
    T = z4.shape[1]
    tb = min(ROW_BLOCK, T)
    C = HG_CHUNK
    ncb = tb // C

    def body(q_ref, f_ref, i_ref, g_ref, lb_ref, gn_ref, y_ref, o_ref, st_ref, st_sc):
        @pl.when(pl.program_id(1) == 0)
        def _():
            st_sc[...] = jnp.zeros_like(st_sc)

        lb = _hg_lower_bound(lb_ref)
        gn = gn_ref[...]

        def chunk(c, carry):
            rs = pl.ds(pl.multiple_of(c * C, C), C)
            v_b = i_ref[rs, :].astype(BF16)
            gc = g_ref[rs, :]
            x = _hg_chunk(q_ref[rs, :], f_ref[rs, :], lb)
            st = st_sc[...]
            st_ref[c] = st
            o = (jnp.dot(x["a"].astype(BF16), v_b, preferred_element_type=F32)
                 + lax.dot_general(x["qe"].astype(BF16), st.astype(BF16), NT_DIMS, preferred_element_type=F32))
            st_sc[...] = st * x["e_last"] + lax.dot_general(v_b, x["khat"].astype(BF16), TN_DIMS,
                                                            preferred_element_type=F32)
            o_ref[rs, :] = o
            n = o * lax.rsqrt(jnp.mean(o * o, -1, keepdims=True) + EPS)
            y_ref[rs, :] = (n * gn * (gc * _sigmoid(gc))).astype(BF16)
            return carry

        lax.fori_loop(0, ncb, chunk, 0)

    zb = lambda k: pl.BlockSpec((None, tb, 128), lambda h, t: (k, t, h))
    out = pl.BlockSpec((tb, 128), lambda h, t: (t, h))
    return pl.pallas_call(
        body, name="hgrn_fwd", grid=(HEADS, T // tb),
        in_specs=[zb(0), zb(1), zb(2), zb(3), pl.BlockSpec((2, 128), lambda h, t: (0, h)),
                  pl.BlockSpec((1, 128), lambda h, t: (0, h))],
        out_specs=[out, out, pl.BlockSpec((None, ncb, 128, 128), lambda h, t: (h, t, 0, 0))],
        out_shape=[jax.ShapeDtypeStruct((T, D), BF16), jax.ShapeDtypeStruct((T, D), F32),
                   jax.ShapeDtypeStruct((HEADS, T // C, 128, 128), F32)],
        scratch_shapes=[pltpu.VMEM((128, 128), F32)],
        compiler_params=_params(32, 2),
    )(z4, z4, z4, z4, hg_lb, gnorm)


def _hgrn_bwd(z4, o_raw, dy, states, hg_lb, gnorm):
    T = z4.shape[1]
    tb = min(ROW_BLOCK, T)
    C = HG_CHUNK
    ncb = tb // C
    nt = T // tb

    def body(q_ref, f_ref, i_ref, g_ref, o_ref, dy_ref, st_ref, lb_ref, gn_ref, dz_ref, dlb_ref, dgn_ref, dst_sc):
        @pl.when(pl.program_id(1) == 0)
        def _():
            dst_sc[...] = jnp.zeros_like(dst_sc)
            dlb_ref[...] = jnp.zeros_like(dlb_ref)
            dgn_ref[...] = jnp.zeros_like(dgn_ref)

        lb = _hg_lower_bound(lb_ref)
        gn = gn_ref[...]

        def chunk(cc, carry):
            c = ncb - 1 - cc
            rs = pl.ds(pl.multiple_of(c * C, C), C)
            qc, gc = q_ref[rs, :], g_ref[rs, :]
            v_b = i_ref[rs, :].astype(BF16)
            x = _hg_chunk(qc, f_ref[rs, :], lb)
            st, dst = st_ref[c], dst_sc[...]
            st_b, dst_b = st.astype(BF16), dst.astype(BF16)
            o, dyc = o_ref[rs, :], dy_ref[rs, :]
            sgg = _sigmoid(gc)
            sil = gc * sgg
            rstd = lax.rsqrt(jnp.mean(o * o, -1, keepdims=True) + EPS)
            n = o * rstd
            dgn_ref[...] += _fold8(dyc * n * sil)
            dn = dyc * gn * sil
            do = rstd * (dn - n * jnp.mean(dn * n, -1, keepdims=True))
            dg = dyc * n * gn * (sgg * (1.0 + gc * (1.0 - sgg)))
            do_b = do.astype(BF16)
            da = jnp.where(x["tril"], lax.dot_general(do_b, v_b, NT_DIMS, preferred_element_type=F32), 0.0).astype(BF16)
            qt_b, kt_b, qe_b, khat_b = (x[n_].astype(BF16) for n_ in ("qt", "kt", "qe", "khat"))
            dv = (lax.dot_general(x["a"].astype(BF16), do_b, TN_DIMS, preferred_element_type=F32)
                  + lax.dot_general(khat_b, dst_b, NT_DIMS, preferred_element_type=F32))
            dqt = jnp.dot(da, kt_b, preferred_element_type=F32)
            dqe = jnp.dot(do_b, st_b, preferred_element_type=F32)
            dkt = lax.dot_general(da, qt_b, TN_DIMS, preferred_element_type=F32)
            dkhat = jnp.dot(v_b, dst_b, preferred_element_type=F32)
            dst_sc[...] = lax.dot_general(do_b, qe_b, TN_DIMS, preferred_element_type=F32) + dst * x["e_last"]
            de_last = jnp.sum(st * dst, axis=0, keepdims=True)
            dqf = dqt * x["eq"] + dqe * x["e"]
            dkk = dkt * x["ek"] + dkhat * x["eh"]
            dkh_kh = dkhat * x["khat"]
            db = dqt * qt_b.astype(F32) - dkt * kt_b.astype(F32) + dqe * x["qe"] - dkh_kh
            db_last = jnp.sum(dkh_kh, axis=0, keepdims=True) + de_last * x["e_last"]
            db = db + jnp.where(x["rowid"] == C - 1, db_last, 0.0)
            rows = lax.broadcasted_iota(jnp.int32, (C, C), 0)
            cols = lax.broadcasted_iota(jnp.int32, (C, C), 1)
            dlg = jnp.dot((rows <= cols).astype(F32), db, precision=HI, preferred_element_type=F32)
            dgate = dlg / x["gate"] - dkk
            sg, sq = x["sg"], x["sq"]
            dlb_ref[...] += _fold8(dgate * (1.0 - sg)) * (lb * (1.0 - lb))
            dz_ref[0, rs, :] = (dqf * (sq * (1.0 + qc * (1.0 - sq)))).astype(BF16)
            dz_ref[1, rs, :] = (dgate * (1.0 - lb) * sg * (1.0 - sg)).astype(BF16)
            dz_ref[2, rs, :] = dv.astype(BF16)
            dz_ref[3, rs, :] = dg.astype(BF16)
            return carry

        lax.fori_loop(0, ncb, chunk, 0)

    zb = lambda k: pl.BlockSpec((None, tb, 128), lambda h, t: (k, nt - 1 - t, h))
    blk = pl.BlockSpec((tb, 128), lambda h, t: (nt - 1 - t, h))
    acc = pl.BlockSpec((8, 128), lambda h, t: (0, h))
    return pl.pallas_call(
        body, name="hgrn_bwd", grid=(HEADS, nt),
        in_specs=[zb(0), zb(1), zb(2), zb(3), blk, blk,
                  pl.BlockSpec((None, ncb, 128, 128), lambda h, t: (h, nt - 1 - t, 0, 0)),
                  pl.BlockSpec((2, 128), lambda h, t: (0, h)), pl.BlockSpec((1, 128), lambda h, t: (0, h))],
        out_specs=[pl.BlockSpec((4, tb, 128), lambda h, t: (0, nt - 1 - t, h)), acc, acc],
        out_shape=[jax.ShapeDtypeStruct((4, T, D), BF16), jax.ShapeDtypeStruct((8, D), F32),
                   jax.ShapeDtypeStruct((8, D), F32)],
        scratch_shapes=[pltpu.VMEM((128, 128), F32)],
        compiler_params=_params(32, 2),
    )(z4, z4, z4, z4, o_raw, dy, states, hg_lb, gnorm)


def _adamw(w, g, m, v, *, name):
    R, L = w.shape
    tr = R if R <= 512 else 512
    assert R % tr == 0
    blk = pl.BlockSpec((tr, L), lambda i: (i, 0))
    c1, c2 = 1.0 - B1 ** STEP, 1.0 - B2 ** STEP

    def body(w_ref, g_ref, m_ref, v_ref, d_ref, mo_ref, vo_ref):
        g_ = g_ref[...]
        m_ = B1 * m_ref[...] + (1.0 - B1) * g_
        v_ = B2 * v_ref[...] + (1.0 - B2) * (g_ * g_)
        d_ref[...] = -LR * ((m_ / c1) / (jnp.sqrt(v_ / c2) + ADAM_EPS) + WD * w_ref[...])
        mo_ref[...] = m_
        vo_ref[...] = v_

    sds = jax.ShapeDtypeStruct((R, L), F32)
    return pl.pallas_call(
        body, name=name, grid=(R // tr,), in_specs=[blk] * 4, out_specs=[blk] * 3, out_shape=[sds] * 3,
        compiler_params=_params(32, 1),
    )(w, g, m, v)


def _adamw_rows(w, m, v, gbufs, row0, *, name):
    L, R, C = w.shape
    tr = 256
    assert R % tr == 0 and row0 % tr == 0 and len(gbufs) == L
    blk = pl.BlockSpec((None, tr, C), lambda l, i: (l, i, 0))
    gblk = pl.BlockSpec((tr, C), lambda l, i: (row0 // tr + i, 0))
    c1, c2 = 1.0 - B1 ** STEP, 1.0 - B2 ** STEP

    def body(*refs):
        w_ref, m_ref, v_ref = refs[:3]
        g_refs = refs[3:3 + L]
        go_ref, d_ref, mo_ref, vo_ref = refs[3 + L:]
        g_ = g_refs[0][...]
        for l in range(1, L):
            g_ = jnp.where(pl.program_id(0) == l, g_refs[l][...], g_)
        m_ = B1 * m_ref[...] + (1.0 - B1) * g_
        v_ = B2 * v_ref[...] + (1.0 - B2) * (g_ * g_)
        go_ref[...] = g_
        d_ref[...] = -LR * ((m_ / c1) / (jnp.sqrt(v_ / c2) + ADAM_EPS) + WD * w_ref[...])
        mo_ref[...] = m_
        vo_ref[...] = v_

    sds = jax.ShapeDtypeStruct((L, R, C), F32)
    return pl.pallas_call(
        body, name=name, grid=(L, R // tr), in_specs=[blk] * 3 + [gblk] * L, out_specs=[blk] * 4, out_shape=[sds] * 4,
        compiler_params=_params(32, 2),
    )(w, m, v, *gbufs)


def _add_pairs(a, b, *, name):
    n, R, L = a.shape
    tr = 128
    blk = pl.BlockSpec((n, tr, L), lambda i: (0, i, 0))

    def body(a_ref, b_ref, o_ref):
        o_ref[...] = (a_ref[...].astype(F32) + b_ref[...].astype(F32)).astype(BF16)

    return pl.pallas_call(
        body, name=name, grid=(R // tr,), in_specs=[blk, blk], out_specs=blk,
        out_shape=jax.ShapeDtypeStruct((n, R, L), BF16), compiler_params=_params(16, 1),
    )(a, b)


def _sum_chips(parts, *, name):
    _, R, L = parts.shape
    tr = 128

    def body(r_ref, out_ref):
        out_ref[...] = ((r_ref[3].astype(F32) + r_ref[0].astype(F32)) + r_ref[1].astype(F32)) + r_ref[2].astype(F32)

    return pl.pallas_call(
        body, name=name, grid=(R // tr,),
        in_specs=[pl.BlockSpec((4, tr, L), lambda i: (0, i, 0))],
        out_specs=pl.BlockSpec((tr, L), lambda i: (i, 0)), out_shape=jax.ShapeDtypeStruct((R, L), F32),
        compiler_params=_params(32, 1),
    )(parts)


def _gather_shards(shard):
    _, R, L = shard.shape

    def body(x_ref, out_ref, send_sems, recv_sems, local_sem):
        x, y, c = _mesh_pos()
        k = 2 * x + y
        chips = [(1 - x, y), (x, 1 - y), (1 - x, 1 - y)]

        def copy(idx, chip, half, to, src=None):
            dst = out_ref.at[chip, half]
            return pltpu.make_async_remote_copy(
                src_ref=dst if src is None else src, dst_ref=dst, send_sem=send_sems.at[idx],
                recv_sem=recv_sems.at[idx], device_id=to, device_id_type=MESH_IDS)

        mine = pltpu.make_async_copy(x_ref, out_ref.at[k], local_sem)
        mine.start()
        first = [copy(j, k, c, (cx, cy, c), src=x_ref.at[c]) for j, (cx, cy) in enumerate(chips)]
        for cp in first:
            cp.start()
        passed = [copy(3 + j, 2 * cx + cy, c, (x, y, 1 - c)) for j, (cx, cy) in enumerate(chips)]
        for j, (cx, cy) in enumerate(chips):
            copy(j, 2 * cx + cy, c, (x, y, c)).wait_recv()
            passed[j].start()
        for j, (cx, cy) in enumerate(chips):
            copy(3 + j, 2 * cx + cy, 1 - c, (x, y, c)).wait_recv()
        for cp in first + passed:
            cp.wait_send()
        mine.wait()

    return pl.pallas_call(
        body, name="gather_shards", in_specs=[_ANY], out_specs=_ANY,
        out_shape=jax.ShapeDtypeStruct((4, 2, R, L), shard.dtype),
        scratch_shapes=[pltpu.SemaphoreType.DMA((6,)), pltpu.SemaphoreType.DMA((6,)), pltpu.SemaphoreType.DMA],
    )(shard)


def _remote(src, dst, send_sem, recv_sem, to):
    return pltpu.make_async_remote_copy(src_ref=src, dst_ref=dst, send_sem=send_sem, recv_sem=recv_sem,
                                        device_id=to, device_id_type=MESH_IDS)


def _rows(ref, lead, start, size):
    return ref.at[tuple(pl.ds(0, n) for n in ref.shape[:lead]) + (pl.ds(start, size),)]


def _other_chips():
    x, y, _ = _mesh_pos()
    return [(1 - x, y), (x, 1 - y), (1 - x, 1 - y)]


def _plan_gather_ici(shards):
    n = len(shards)

    def copies(ins, outs, send, recv):
        x, y, c = _mesh_pos()
        res = []
        for b in range(n):
            half = shards[b].shape[0] // 2
            for j, (cx, cy) in enumerate(_other_chips()):
                res.append((_remote(_rows(ins[b], 0, c * half, half), _rows(outs[b].at[2 * x + y], 0, c * half, half),
                                    send(3 * b + j), recv(3 * b + j), (cx, cy, c)),
                            _remote(_rows(ins[b], 0, c * half, half), _rows(outs[b].at[2 * cx + cy], 0, c * half, half),
                                    send(3 * b + j), recv(3 * b + j), (x, y, c))))
        return res

    def local(ins, outs, loc):
        x, y, _ = _mesh_pos()
        return [pltpu.make_async_copy(ins[b], outs[b].at[2 * x + y], loc(b)) for b in range(n)]

    def start(ins, outs, send, recv, loc):
        for cp in local(ins, outs, loc):
            cp.start()
        for out_cp, _ in copies(ins, outs, send, recv):
            out_cp.start()

    def wait(ins, outs, send, recv, loc):
        for out_cp, in_cp in copies(ins, outs, send, recv):
            in_cp.wait_recv()
            out_cp.wait_send()
        for cp in local(ins, outs, loc):
            cp.wait()

    outs = [jax.ShapeDtypeStruct((4,) + s.shape, s.dtype) for s in shards]
    return _Plan(shards, outs, 3 * n, n, start, wait)


def _plan_gather_forward(bufs):
    n = len(bufs)

    def copies(outs, send, recv):
        x, y, c = _mesh_pos()
        res = []
        for b in range(n):
            half = bufs[b].shape[1] // 2
            for j, (cx, cy) in enumerate(_other_chips()):
                slot = outs[b].at[2 * cx + cy]
                res.append((_remote(_rows(slot, 0, c * half, half), _rows(slot, 0, c * half, half),
                                    send(3 * b + j), recv(3 * b + j), (x, y, 1 - c)),
                            _remote(_rows(slot, 0, c * half, half), _rows(slot, 0, (1 - c) * half, half),
                                    send(3 * b + j), recv(3 * b + j), (x, y, c))))
        return res

    def start(ins, outs, send, recv, loc):
        for out_cp, _ in copies(outs, send, recv):
            out_cp.start()

    def wait(ins, outs, send, recv, loc):
        for out_cp, in_cp in copies(outs, send, recv):
            in_cp.wait_recv()
            out_cp.wait_send()

    outs = [jax.ShapeDtypeStruct(b.shape, b.dtype) for b in bufs]
    return _Plan(bufs, outs, 3 * n, 0, start, wait, aliases={b: b for b in range(n)})


def _plan_pair_swap(g):
    half = g.shape[1] // 2

    def parts(ins, outs, send, recv, loc):
        x, y, c = _mesh_pos()
        cp = _remote(_rows(ins[0], 1, (1 - c) * half, half), outs[0], send(0), recv(0), (x, y, 1 - c))
        own = pltpu.make_async_copy(_rows(ins[0], 1, c * half, half), outs[1], loc(0))
        return cp, own

    def start(*a):
        cp, own = parts(*a)
        own.start()
        cp.start()

    def wait(*a):
        cp, own = parts(*a)
        cp.wait()
        own.wait()

    sds = jax.ShapeDtypeStruct((4, half, g.shape[2]), g.dtype)
    return _Plan([g], [sds, sds], 1, 1, start, wait)


def _plan_pair_gather(own):
    def parts(ins, outs, send, recv, loc):
        x, y, c = _mesh_pos()
        mine = pltpu.make_async_copy(ins[0], outs[0].at[c], loc(0))
        out_cp = _remote(ins[0], outs[0].at[c], send(0), recv(0), (x, y, 1 - c))
        in_cp = _remote(ins[0], outs[0].at[1 - c], send(0), recv(0), (x, y, c))
        return mine, out_cp, in_cp

    def start(*a):
        mine, out_cp, _ = parts(*a)
        mine.start()
        out_cp.start()

    def wait(*a):
        mine, out_cp, in_cp = parts(*a)
        in_cp.wait_recv()
        out_cp.wait_send()
        mine.wait()

    return _Plan([own], [jax.ShapeDtypeStruct((2,) + own.shape, own.dtype)], 1, 1, start, wait)


def _plan_chip_scatter(p):
    def parts(ins, outs, send, recv, loc):
        x, y, c = _mesh_pos()
        cps = [_remote(ins[0].at[2 * cx + cy], outs[0].at[j], send(j), recv(j), (cx, cy, c))
               for j, (cx, cy) in enumerate(_other_chips())]
        return cps, pltpu.make_async_copy(ins[0].at[2 * x + y], outs[0].at[3], loc(0))

    def start(*a):
        cps, own = parts(*a)
        own.start()
        for cp in cps:
            cp.start()

    def wait(*a):
        cps, own = parts(*a)
        for cp in cps:
            cp.wait()
        own.wait()

    return _Plan([p], [jax.ShapeDtypeStruct(p.shape, p.dtype)], 3, 1, start, wait)


def _small_allreduce(vec):
    R, L = vec.shape

    def body(v_ref, sum_ref, all_ref, send_sems, recv_sems):
        x, y, c = _mesh_pos()
        me = 4 * x + 2 * y + c
        all_ref[me] = v_ref[...]
        cps = []
        for r in range(1, 8):
            peer = (x ^ (r >> 2), y ^ ((r >> 1) & 1), c ^ (r & 1))
            cps.append(pltpu.make_async_remote_copy(
                src_ref=v_ref, dst_ref=all_ref.at[me], send_sem=send_sems.at[r - 1], recv_sem=recv_sems.at[r - 1],
                device_id=peer, device_id_type=MESH_IDS))
        for cp in cps:
            cp.start()
        for r in range(1, 8):
            src = 4 * (x ^ (r >> 2)) + 2 * (y ^ ((r >> 1) & 1)) + (c ^ (r & 1))
            pltpu.make_async_remote_copy(
                src_ref=v_ref, dst_ref=all_ref.at[src], send_sem=send_sems.at[r - 1], recv_sem=recv_sems.at[r - 1],
                device_id=(x, y, c), device_id_type=MESH_IDS).wait_recv()
        for cp in cps:
            cp.wait_send()
        s = all_ref[0]
        for d in range(1, 8):
            s = s + all_ref[d]
        sum_ref[...] = s

    vm = pl.BlockSpec(memory_space=pltpu.VMEM)
    return pl.pallas_call(
        body, name="small_allreduce", in_specs=[vm], out_specs=[vm, vm],
        out_shape=[jax.ShapeDtypeStruct((R, L), F32), jax.ShapeDtypeStruct((8, R, L), F32)],
        scratch_shapes=[pltpu.SemaphoreType.DMA((7,)), pltpu.SemaphoreType.DMA((7,))],
        compiler_params=_params(16),
    )(vec)[0]


ROWS_L1, ROWS_L0, ROWS_ODD = 3328, 2048, 768
ODD_PARTS = (("w_out_e", (256, 1024)), ("w_in_e", (1024, 392)), ("w_qb", (256, 192)), ("w_kvb", (256, 256)))


def _odd_rows(parts, dtype, gnorm=None):
    rows = [parts[n].reshape(-1, 1024).astype(dtype) for n, _ in ODD_PARTS]
    used = sum(r.shape[0] for r in rows)
    if gnorm is not None:
        bits = lax.bitcast_convert_type(gnorm.reshape(-1), BF16).reshape(1, 512)
        rows.append(jnp.pad(bits, ((0, 0), (0, 512))))
        used += 1
    rows.append(jnp.zeros((ROWS_ODD - used, 1024), dtype))
    return jnp.concatenate(rows, axis=0)


def _odd_unrows(buf, with_gnorm=False):
    out, off = {}, 0
    for n, shape in ODD_PARTS:
        nr = math.prod(shape) // 1024
        out[n] = buf[off:off + nr].reshape(shape)
        off += nr
    if with_gnorm:
        out["hg_gnorm"] = lax.bitcast_convert_type(buf[off, :512].reshape(256, 2), F32).reshape(1, 256)
    return out


def _pack_small(vals):
    flat = jnp.concatenate([vals[n].reshape(-1).astype(F32) for n, _ in SMALL])
    return jnp.pad(flat, (0, SMALL_ROWS * 1024 - flat.shape[0])).reshape(SMALL_ROWS, 1024)


def _unpack_small(packed):
    flat = packed.reshape(-1)
    out, off = {}, 0
    for n, shape in SMALL:
        size = math.prod(shape)
        out[n] = flat[off:off + size].reshape(shape)
        off += size
    return out


def _rope_tables(positions):
    half = ROPE // 2
    inv_freq = ROPE_BASE ** (-jnp.arange(half, dtype=F32) / half)
    ang = positions.astype(F32).reshape(-1, 1) * inv_freq
    cos, sin = jnp.cos(ang), jnp.sin(ang)
    T = ang.shape[0]
    one, z16, z32 = jnp.ones((T, NOPE), F32), jnp.zeros((T, half), F32), jnp.zeros((T, 32), F32)
    z64 = jnp.zeros((T, NOPE), F32)
    c = jnp.concatenate([one, cos, cos, z32], axis=1)
    s1 = jnp.concatenate([z64, -sin, z16, z32], axis=1)
    s2 = jnp.concatenate([z64, z16, sin, z32], axis=1)
    return c, s1, s2


def _local_step(x, positions, tgt, odd, bufs, P, exchange):
    T = x.shape[0]
    row = lambda a: a.reshape(1, -1)
    rc, rs1, rs2 = _rope_tables(positions)
    blk = lambda f: pl.BlockSpec((None, D, D), f)

    w_in_e = odd["w_in_e"]
    w_in = jnp.concatenate([w_in_e[:, :512], w_in_e[:, 544:1568], w_in_e[:, 512:544], jnp.zeros((D, 96), BF16)], axis=1)
    wq = jnp.pad(odd["w_qb"].reshape(256, HEADS, NOPE + ROPE), ((0, 0), (0, 0), (0, 32))).reshape(256, HEADS * 128)
    kvb = odd["w_kvb"].reshape(256, HEADS, NOPE + VDIM)
    wk = jnp.pad(kvb[:, :, :NOPE], ((0, 0), (0, 0), (0, 64))).reshape(256, HEADS * 128)
    wv = kvb[:, :, NOPE:].reshape(256, HEADS * VDIM)
    w_out_e = odd["w_out_e"]
    sgu_w = P["sgu_w"][0]
    sgu_bt = P["sgu_b"][0].T
    gq, gkv = P["mla_gq"], P["mla_gkv"]
    gnorm = P["hg_gnorm"]

    x_b = x.astype(BF16)
    z0 = _matmul(x_b, w_in, name="in_proj_e", M=T, N=1664, K=D, tn=1664)[0]
    q, k, v = _mla_prep(z0, gq, gkv, wq, wk, wv, rc, rs1, rs2)
    if exchange:
        a_out, lse, wg0, wg1 = _flash_fwd(q, k, v, plan=_plan_gather_ici(list(bufs)))
    else:
        a_out, lse = _flash_fwd(q, k, v)
        wg0, wg1 = bufs
    mix0 = _sgu_fwd(z0, a_out, P["sgu_ln_g"], P["sgu_ln_b"], sgu_w, sgu_bt)
    res = _proj_ln(mix0, w_out_e, x, row(P["ln1_g"][0]), row(P["ln1_b"][0]), name="out_proj_ln_e",
                   plan=_plan_gather_forward([wg0, wg1]) if exchange else None)
    r1, h1, h1b = res[:3]
    if exchange:
        wg0, wg1 = res[3:]
    ra0, r2, h2, h2b = _ffn_ln(h1b, wg0, h1, row(P["ln2_g"][0]), row(P["ln2_b"][0]), name="ffn_ln_0")
    z4 = _matmul(h2b, wg1, name="in_proj_o", M=T, N=4 * D, K=D, b_spec=blk(lambda i, j, k: (j, 2, 0)),
                 out_shape=jax.ShapeDtypeStruct((4, T, D), F32),
                 o_spec=pl.BlockSpec((None, min(512, T), D), lambda i, j, k: (j, i, 0)))[0]
    y1, o_raw, states = _hgrn_fwd(z4, P["hg_lb"], gnorm)
    r3, h3, h3b = _proj_ln(y1, wg1, h2, row(P["ln1_g"][1]), row(P["ln1_b"][1]), name="out_proj_ln_o", w_rowblk=12)
    ra1, r4, h4, _ = _ffn_ln(h3b, wg1, h3, row(P["ln2_g"][1]), row(P["ln2_b"][1]), name="ffn_ln_1")
    dy, loss_parts = _loss_dy(h4, tgt)

    gs = {}
    ln1_g, ln1_b, ln2_g, ln2_b = [None, None], [None, None], [None, None], [None, None]

    def ffn_bwd(l, dh, r_out, ra, h_mid_b, g2, wg, rows, plan=None):
        dr, dr_b, dg, db = _ln_bwd(dh, r_out, row(g2), name=f"ln2_bwd_{l}")
        ln2_g[l], ln2_b[l] = dg.sum(0), db.sum(0)
        da, *extra = _matmul(dr_b, wg, tb=True, mul=ra, out_dtype=BF16, name=f"ffn_da_{l}", M=T, N=4 * D, K=D,
                             b_spec=blk(lambda i, j, k: (j, 1, 0)), plan=plan)
        gbuf = _matmul(ra, dr_b, ta=True, a_sq=True, name=f"ffn_dw2_{l}", M=4 * D, N=D, K=T, tm=1024, tk=512,
                       out_shape=jax.ShapeDtypeStruct((4, rows, D), BF16), o_spec=blk(lambda i, j, k: (i, 1, 0)))[0]
        gbuf = _matmul(h_mid_b, da, ta=True, name=f"ffn_dw1_{l}", M=D, N=4 * D, K=T, tm=1024, tk=512, into=gbuf,
                       out_shape=jax.ShapeDtypeStruct((4, rows, D), BF16), o_spec=blk(lambda i, j, k: (j, 0, 0)))[0]
        dh_mid = _matmul(da, wg, tb=True, add=dr, add_scale=ALPHA, name=f"ffn_dh_{l}", M=T, N=D, K=4 * D,
                         b_spec=blk(lambda i, j, k: (k, 0, 0)))[0]
        return dh_mid, gbuf, extra

    dh3, g1, _ = ffn_bwd(1, dy, r4, ra1, h3b, P["ln2_g"][1], wg1, ROWS_L1)
    dr3, dr3_b, dg, db = _ln_bwd(dh3, r3, row(P["ln1_g"][1]), name="ln1_bwd_1")
    ln1_g[1], ln1_b[1] = dg.sum(0), db.sum(0)
    g1_sds = jax.ShapeDtypeStruct((4, ROWS_L1, D), BF16)
    g1 = _matmul(y1, dr3_b, ta=True, name="dw_out_o", M=D, N=D, K=T, tm=256, tk=512, into=g1, out_shape=g1_sds,
                 o_spec=pl.BlockSpec((None, 256, D), lambda i, j, k: (i, 12, 0)))[0]
    dmix1 = _matmul(dr3_b, wg1, tb=True, name="dmix_o", M=T, N=D, K=D, b_spec=_rows4_spec(12, 3), b_merge=(D, D))[0]
    dz4, dlb, dgn = _hgrn_bwd(z4, o_raw, dmix1, states, P["hg_lb"], gnorm)
    g1 = _matmul(h2b, dz4, ta=True, name="dw_in_o", M=D, N=4 * D, K=T, tm=1024, tk=512, into=g1, out_shape=g1_sds,
                 b_spec=pl.BlockSpec((None, min(512, T), D), lambda i, j, k: (j, k, 0)),
                 o_spec=blk(lambda i, j, k: (j, 2, 0)))[0]
    dh2 = _matmul(dz4, wg1, tb=True, add=dr3, add_scale=ALPHA, name="dh_in_o", M=T, N=D, K=4 * D,
                  a_spec=pl.BlockSpec((None, min(512, T), D), lambda i, j, k: (k, i, 0)),
                  b_spec=blk(lambda i, j, k: (k, 2, 0)))[0]
    d_lb1 = dlb.sum(0)
    gs["hg_lb"] = jnp.stack([-d_lb1, d_lb1])
    gs["hg_gnorm"] = dgn.sum(0)[None]

    dh1, g0, swapped1 = ffn_bwd(0, dh2, r2, ra0, h1b, P["ln2_g"][0], wg0, ROWS_L0,
                                plan=_plan_pair_swap(g1) if exchange else None)
    dr1, dr1_b, dg, db = _ln_bwd(dh1, r1, row(P["ln1_g"][0]), name="ln1_bwd_0")
    ln1_g[0], ln1_b[0] = dg.sum(0), db.sum(0)
    godd = {"w_out_e": _matmul(mix0, dr1_b, ta=True, name="dw_out_e", M=D, N=D, K=T, tm=1024, tk=512)[0]}
    dmix0, *swapped0 = _matmul(dr1_b, w_out_e, tb=True, name="dmix_e", M=T, N=D, K=D,
                               plan=_plan_pair_swap(g0) if exchange else None)
    delta, do_b = _attn_delta(dmix0, a_out)
    if exchange:
        pair1 = _add_pairs(swapped1[1], swapped1[0], name="grad_pair_add_1")
        pair0 = _add_pairs(swapped0[1], swapped0[0], name="grad_pair_add_0")
        dq4, dk, dv, parts0, parts1 = _flash_bwd(
            q, k, v, do_b, lse, delta, plan=_join_plans([_plan_chip_scatter(pair0), _plan_chip_scatter(pair1)]))
        half0 = _sum_chips(parts0, name="grad_chip_sum_0")
        half1 = _sum_chips(parts1, name="grad_chip_sum_1")
        dc, dkr, dwq, dwk, dwv, dgq, dgkv, g0, g1 = _mla_bwd(
            z0, dq4, dk, dv, gq, gkv, wq, wk, wv, rc, rs1, rs2,
            plan=_join_plans([_plan_pair_gather(half0), _plan_pair_gather(half1)]))
        g0, g1 = g0.reshape(ROWS_L0, D), g1.reshape(ROWS_L1, D)
    else:
        dq4, dk, dv = _flash_bwd(q, k, v, do_b, lse, delta)
        dc, dkr, dwq, dwk, dwv, dgq, dgkv = _mla_bwd(z0, dq4, dk, dv, gq, gkv, wq, wk, wv, rc, rs1, rs2)
    dz0, dsw, dsb, dslg, dslb = _sgu_bwd(z0, dmix0, dc, dkr, P["sgu_ln_g"], P["sgu_ln_b"], sgu_w, sgu_bt)
    dw_in = _matmul(x_b, dz0, ta=True, name="dw_in_e", M=D, N=1664, K=T, tm=1024, tn=1664, tk=512)[0]
    godd["w_in_e"] = jnp.concatenate([dw_in[:, :512], dw_in[:, 1536:1568], dw_in[:, 512:1536]], axis=1)
    grad_x = _matmul(dz0, w_in, tb=True, add=dr1, add_scale=ALPHA, name="dx", M=T, N=D, K=1664, tk=1664)[0]

    godd["w_qb"] = dwq.reshape(256, HEADS, 128)[:, :, :NOPE + ROPE].reshape(256, HEADS * (NOPE + ROPE))
    godd["w_kvb"] = jnp.concatenate([dwk.reshape(256, HEADS, 128)[:, :, :NOPE], dwv.reshape(256, HEADS, VDIM)],
                                    axis=2).reshape(256, HEADS * (NOPE + VDIM))
    gs["mla_gq"], gs["mla_gkv"] = dgq.sum(0)[None], dgkv.sum(0)[None]
    gs["sgu_ln_g"], gs["sgu_ln_b"] = dslg.sum(0)[None], dslb.sum(0)[None]
    gs["sgu_w"], gs["sgu_b"] = dsw[None], dsb[:, :SGU_G].T[None]
    gs["ln1_g"], gs["ln1_b"] = jnp.stack(ln1_g), jnp.stack(ln1_b)
    gs["ln2_g"], gs["ln2_b"] = jnp.stack(ln2_g), jnp.stack(ln2_b)
    return loss_parts, grad_x, g0, g1, godd, gs


WEIGHTS = ['w_in_e', 'mla_gq', 'mla_gkv', 'w_qb', 'w_kvb', 'sgu_ln_g', 'sgu_ln_b', 'sgu_w', 'sgu_b', 'w_out_e',
           'w_in_o', 'hg_lb', 'hg_gnorm', 'w_out_o', 'ln1_g', 'ln1_b', 'w_ff1', 'w_ff2', 'ln2_g', 'ln2_b']


def kernel(x, positions, w_in_e, mla_gq, mla_gkv, w_qb, w_kvb, sgu_ln_g, sgu_ln_b, sgu_w, sgu_b, w_out_e, w_in_o, hg_lb, hg_gnorm, w_out_o, ln1_g, ln1_b, w_ff1, w_ff2, ln2_g, ln2_b, loss_target, m_w_in_e, m_mla_gq, m_mla_gkv, m_w_qb, m_w_kvb, m_sgu_ln_g, m_sgu_ln_b, m_sgu_w, m_sgu_b, m_w_out_e, m_w_in_o, m_hg_lb, m_hg_gnorm, m_w_out_o, m_ln1_g, m_ln1_b, m_w_ff1, m_w_ff2, m_ln2_g, m_ln2_b, v_w_in_e, v_mla_gq, v_mla_gkv, v_w_qb, v_w_kvb, v_sgu_ln_g, v_sgu_ln_b, v_sgu_w, v_sgu_b, v_w_out_e, v_w_in_o, v_hg_lb, v_hg_gnorm, v_w_out_o, v_ln1_g, v_ln1_b, v_w_ff1, v_w_ff2, v_ln2_g, v_ln2_b):
    args = dict(locals())
    w = {n: args[n] for n in WEIGHTS}
    m = {n: args["m_" + n] for n in WEIGHTS}
    v = {n: args["v_" + n] for n in WEIGHTS}
    cx, cy, cc = _mesh_pos()
    chip = 2 * cx + cy

    odd_shard = _odd_rows({"w_out_e": w_out_e[0], "w_in_e": w_in_e[0], "w_qb": w_qb[0], "w_kvb": w_kvb[0]}, BF16,
                          gnorm=hg_gnorm)
    gathered = _gather_shards(odd_shard.reshape(2, ROWS_ODD // 2, 1024)).reshape(4, ROWS_ODD, 1024)
    per_chip = [_odd_unrows(gathered[j], with_gnorm=True) for j in range(4)]
    odd = {"w_out_e": jnp.concatenate([p["w_out_e"] for p in per_chip], axis=0)}
    for n in ("w_in_e", "w_qb", "w_kvb"):
        odd[n] = jnp.concatenate([p[n] for p in per_chip], axis=1)
    small = {n: w[n] for n, _ in SMALL if n != "hg_gnorm"}
    small["hg_gnorm"] = jnp.concatenate([p["hg_gnorm"] for p in per_chip], axis=1)
    rows_l0 = jnp.concatenate([w_ff1[0], w_ff2[0]], axis=0).astype(BF16)
    rows_l1 = jnp.concatenate([w_ff1[1], w_ff2[1], w_in_o[0], w_out_o[0]], axis=0).astype(BF16)

    loss_parts, grad_x, g_l0, g_l1, godd, gs = _local_step(x[0], positions[0], loss_target[0], odd, (rows_l0, rows_l1),
                                                            small, True)

    loss = lax.psum((0.5 / D) * jnp.sum(loss_parts), ("x", "y", "c"))

    by_chip = [_odd_rows({"w_out_e": jnp.split(godd["w_out_e"], 4, axis=0)[j],
                          **{n: jnp.split(godd[n], 4, axis=1)[j] for n in ("w_in_e", "w_qb", "w_kvb")}}, BF16)
               for j in range(4)]
    theirs, mine = _run_plan(_plan_pair_swap(jnp.stack(by_chip)), name="odd_pair_swap")
    parts = _run_plan(_plan_chip_scatter(_add_pairs(mine, theirs, name="odd_pair_add")), name="odd_chip_scatter")[0]
    g_odd = _run_plan(_plan_pair_gather(_sum_chips(parts, name="odd_chip_sum")), name="odd_pair_gather")[0]
    g_odd = _odd_unrows(g_odd.reshape(ROWS_ODD, 1024))

    g_small = _unpack_small(_small_allreduce(_pack_small(gs)))
    g_gnorm = lax.dynamic_slice_in_dim(g_small["hg_gnorm"], chip * 256, 256, axis=1)

    grads = {n: g_small[n] for n, _ in SMALL if n != "hg_gnorm"}
    grads["hg_gnorm"] = g_gnorm

    delta, new_m, new_v = {}, {}, {}
    for n, bufs_, row0 in (("w_ff1", [g_l0, g_l1], 0), ("w_ff2", [g_l0, g_l1], 1024), ("w_in_o", [g_l1], 2048),
                           ("w_out_o", [g_l1], 3072)):
        grads[n], delta[n], new_m[n], new_v[n] = _adamw_rows(w[n], m[n], v[n], bufs_, row0, name=f"adamw_{n}")
    for n, _ in ODD_PARTS:
        grads[n] = g_odd[n][None]
        d_, m_, v_ = _adamw(w[n][0], g_odd[n], m[n][0], v[n][0], name=f"adamw_{n}")
        delta[n], new_m[n], new_v[n] = d_[None], m_[None], v_[None]
    rest = [n for n in WEIGHTS if n not in delta]

    def pack_rest(d):
        flat = jnp.concatenate([d[n].reshape(-1) for n in rest])
        return jnp.pad(flat, (0, SMALL_ROWS * 1024 - flat.shape[0])).reshape(SMALL_ROWS, 1024)

    outs = _adamw(pack_rest(w), pack_rest(grads), pack_rest(m), pack_rest(v), name="adamw_small")
    for dst, packed in zip((delta, new_m, new_v), outs):
        flat, off = packed.reshape(-1), 0
        for n in rest:
            size = math.prod(w[n].shape)
            dst[n] = flat[off:off + size].reshape(w[n].shape)
            off += size

    return (loss, grad_x[None], *[grads[n] for n in WEIGHTS], *[delta[n] for n in WEIGHTS],
            *[new_m[n] for n in WEIGHTS], *[new_v[n] for n in WEIGHTS])
```

```python
import functools
import math

import jax
import jax.numpy as jnp
from jax import lax
from jax.experimental import pallas as pl
from jax.experimental.pallas import tpu as pltpu

F32 = jnp.float32
BF16 = jnp.bfloat16
MESH_IDS = pl.DeviceIdType.MESH

D = 1024
DEPTH = 2
HEADS = 8
NOPE, ROPE, VDIM = 64, 32, 64
QK_SCALE = (NOPE + ROPE) ** -0.5
ROPE_BASE = 10000.0
SGU_G, SGU_C = 4, 128
HG_CHUNK = 64
ALPHA = (2 * DEPTH) ** 0.25
EPS = 1e-5
LR, B1, B2, ADAM_EPS, WD, STEP = 0.001, 0.9, 0.999, 1e-08, 0.01, 10
GELU_C = math.sqrt(2.0 / math.pi)
GELU_A = 0.044715
HI = lax.Precision.HIGHEST
MB = 1024 * 1024
ROW_BLOCK = 512

NT_DIMS = (((1,), (1,)), ((), ()))
TN_DIMS = (((0,), (0,)), ((), ()))

SHARDED = (
    ("w_in_e", (1, 1024, 392), 2), ("w_qb", (1, 256, 192), 2), ("w_kvb", (1, 256, 256), 2),
    ("w_out_e", (1, 256, 1024), 1), ("w_in_o", (1, 1024, 1024), 2), ("w_out_o", (1, 256, 1024), 1),
    ("w_ff1", (2, 1024, 1024), 2), ("w_ff2", (2, 1024, 1024), 1), ("hg_gnorm", (1, 256), 1),
)
PACK_ROWS = 6144
HALF_ROWS = PACK_ROWS // 2
SMALL = (("mla_gq", (1, 256)), ("mla_gkv", (1, 256)), ("sgu_ln_g", (1, 512)), ("sgu_ln_b", (1, 512)),
         ("sgu_w", (1, 4, 128, 128)), ("sgu_b", (1, 4, 128)), ("hg_lb", (2, 1024)), ("hg_gnorm", (1, 1024)),
         ("ln1_g", (2, 1024)), ("ln1_b", (2, 1024)), ("ln2_g", (2, 1024)), ("ln2_b", (2, 1024)))
SMALL_ROWS = 80


def _params(vmem_mb, n_axes=0):
    kw = dict(vmem_limit_bytes=vmem_mb * MB)
    if n_axes:
        kw["dimension_semantics"] = ("arbitrary",) * n_axes
    return pltpu.CompilerParams(**kw)


_ANY = pl.BlockSpec(memory_space=pl.ANY)


def _mesh_pos():
    return lax.axis_index("x"), lax.axis_index("y"), lax.axis_index("c")


class _Plan:
    def __init__(self, ins, outs, n_remote, n_local, start, wait, aliases=None):
        self.ins, self.outs, self.n_remote, self.n_local = list(ins), list(outs), n_remote, n_local
        self.start, self.wait, self.aliases = start, wait, dict(aliases or {})


def _join_plans(plans):
    ins, outs, aliases, parts = [], [], {}, []
    nr = nl = 0
    for p in plans:
        parts.append((p, len(ins), len(outs), nr, nl))
        aliases.update({len(ins) + i: len(outs) + o for i, o in p.aliases.items()})
        ins += p.ins
        outs += p.outs
        nr += p.n_remote
        nl += p.n_local

    def run(which):
        def go(in_refs, out_refs, send, recv, loc):
            for p, i0, o0, r0, l0 in parts:
                getattr(p, which)(in_refs[i0:i0 + len(p.ins)], out_refs[o0:o0 + len(p.outs)],
                                  lambda i, r0=r0: send(r0 + i), lambda i, r0=r0: recv(r0 + i),
                                  lambda i, l0=l0: loc(l0 + i))
        return go

    return _Plan(ins, outs, nr, nl, run("start"), run("wait"), aliases)


def _plan_io(plan, n_in, n_out):
    if plan is None:
        return [], [], [], [], {}
    sems = [pltpu.SemaphoreType.DMA((max(plan.n_remote, 1),)), pltpu.SemaphoreType.DMA((max(plan.n_remote, 1),)),
            pltpu.SemaphoreType.DMA((max(plan.n_local, 1),))]
    aliases = {n_in + i: n_out + o for i, o in plan.aliases.items()}
    return plan.ins, [_ANY] * len(plan.outs), plan.outs, sems, aliases


def _split_refs(refs, n_in, n_out, n_scr, plan):
    p_in, p_out = (len(plan.ins), len(plan.outs)) if plan is not None else (0, 0)
    refs = list(refs)
    ins, refs = refs[:n_in], refs[n_in:]
    pins, refs = refs[:p_in], refs[p_in:]
    outs, refs = refs[:n_out], refs[n_out:]
    pouts, refs = refs[:p_out], refs[p_out:]
    scr, psem = refs[:n_scr], refs[n_scr:]
    psem = tuple((lambda i, s=s: s.at[i]) for s in psem)
    return ins, outs, scr, (pins, pouts, psem)


def _grid_edge(grid, last):
    cond = None
    for ax, n in enumerate(grid):
        c = pl.program_id(ax) == (n - 1 if last else 0)
        cond = c if cond is None else cond & c
    return cond


def _plan_start(plan, pctx, grid):
    if plan is not None:
        pins, pouts, psem = pctx
        pl.when(_grid_edge(grid, False))(lambda: plan.start(pins, pouts, *psem))


def _plan_wait(plan, pctx, grid):
    if plan is not None:
        pins, pouts, psem = pctx
        pl.when(_grid_edge(grid, True))(lambda: plan.wait(pins, pouts, *psem))


def _run_plan(plan, *, name):
    def body(*refs):
        _, _, _, (pins, pouts, psem) = _split_refs(refs, 0, 0, 0, plan)
        plan.start(pins, pouts, *psem)
        plan.wait(pins, pouts, *psem)

    p_in, p_ospec, p_oshape, p_scr, p_alias = _plan_io(plan, 0, 0)
    return pl.pallas_call(body, name=name, in_specs=[_ANY] * len(p_in), out_specs=p_ospec, out_shape=p_oshape,
                          scratch_shapes=p_scr, input_output_aliases=p_alias)(*p_in)


def _fold8(x):
    return x.reshape(x.shape[0] // 8, 8, x.shape[1]).sum(axis=0)


def _ln_stats(r):
    mu = jnp.mean(r, -1, keepdims=True)
    xc = r - mu
    rstd = lax.rsqrt(jnp.mean(xc * xc, -1, keepdims=True) + EPS)
    return xc * rstd, rstd


def _sigmoid(x):
    return 1.0 / (1.0 + jnp.exp(-x))


def _gelu(x):
    return 0.5 * x * (1.0 + jnp.tanh(GELU_C * (x + GELU_A * x * x * x)))


def _gelu_grad(x):
    t = jnp.tanh(GELU_C * (x + GELU_A * x * x * x))
    return 0.5 * (1.0 + t) + 0.5 * x * (1.0 - t * t) * GELU_C * (1.0 + 3.0 * GELU_A * x * x)


def _matmul(a, b, *, name, M, N, K, ta=False, tb=False, out_dtype=F32, tm=512, tn=1024, tk=1024,
            a_spec=None, b_spec=None, b_merge=None, out_shape=None, o_spec=None, into=None,
            a_sq=False, mul=None, add=None, add_scale=1.0, plan=None):
    tm, tn, tk = min(tm, M), min(tn, N), min(tk, K)
    assert M % tm == 0 and N % tn == 0 and K % tk == 0
    grid = (M // tm, N // tn, K // tk)
    nk = grid[2]
    if a_spec is None:
        a_spec = pl.BlockSpec((tk, tm), lambda i, j, k: (k, i)) if ta else pl.BlockSpec((tm, tk), lambda i, j, k: (i, k))
    if b_spec is None:
        b_spec = pl.BlockSpec((tn, tk), lambda i, j, k: (j, k)) if tb else pl.BlockSpec((tk, tn), lambda i, j, k: (k, j))
    if o_spec is None:
        o_spec = pl.BlockSpec((tm, tn), lambda i, j, k: (i, j))
        out_shape = jax.ShapeDtypeStruct((M, N), out_dtype)
    e_spec = pl.BlockSpec((tm, tn), lambda i, j, k: (i, j))
    dims = (((0 if ta else 1,), (1 if tb else 0,)), ((), ()))
    extra = [e for e in (mul, add, into) if e is not None]
    n_in = 2 + len(extra)

    def body(*refs):
        ins, outs, scr, pctx = _split_refs(refs, n_in, 1, 1 if nk > 1 else 0, plan)
        a_ref, b_ref = ins[0], ins[1]
        rest = list(ins[2:])
        mul_ref = rest.pop(0) if mul is not None else None
        add_ref = rest.pop(0) if add is not None else None
        o_ref = outs[0]
        _plan_start(plan, pctx, grid)
        av = a_ref[...]
        if a_sq:
            av = av * av
        bv = b_ref[...]
        if b_merge is not None:
            bv = bv.reshape(b_merge)
        p = lax.dot_general(av, bv, dims, preferred_element_type=F32)

        def finish(r):
            if mul_ref is not None:
                r = r * (2.0 * mul_ref[...].astype(F32))
            if add_ref is not None:
                r = r + add_scale * add_ref[...]
            o_ref[...] = r.astype(o_ref.dtype)

        if nk == 1:
            finish(p)
        else:
            acc_ref = scr[0]
            k = pl.program_id(2)

            @pl.when(k == 0)
            def _():
                acc_ref[...] = p

            @pl.when(k > 0)
            def _():
                acc_ref[...] += p

            @pl.when(k == nk - 1)
            def _():
                finish(acc_ref[...])

        _plan_wait(plan, pctx, grid)

    p_in, p_ospec, p_oshape, p_scr, p_alias = _plan_io(plan, n_in, 1)
    aliases = dict(p_alias)
    if into is not None:
        aliases[n_in - 1] = 0
    return pl.pallas_call(
        body, name=name, grid=grid,
        in_specs=[a_spec, b_spec] + [e_spec] * (len(extra) - (into is not None)) + [_ANY] * (into is not None)
        + [_ANY] * len(p_in),
        out_specs=[o_spec] + p_ospec, out_shape=[out_shape] + p_oshape,
        scratch_shapes=([pltpu.VMEM((tm, tn), F32)] if nk > 1 else []) + p_scr,
        input_output_aliases=aliases, compiler_params=_params(48, 3),
    )(a, b, *extra, *p_in)


def _rows4_spec(rowblk, n_axes):
    return pl.BlockSpec((4, 256, D), lambda *_: (0, rowblk, 0))


def _proj_ln(a_b, w, h_prev, g, b, *, name, w_rowblk=None, plan=None):
    T = a_b.shape[0]
    tm = min(ROW_BLOCK, T)
    grid = (T // tm,)
    row = pl.BlockSpec((tm, D), lambda i: (i, 0))
    vec = pl.BlockSpec((1, D), lambda i: (0, 0))
    w_spec = pl.BlockSpec((D, D), lambda i: (0, 0)) if w_rowblk is None else _rows4_spec(w_rowblk, 1)

    def body(*refs):
        (a_ref, w_ref, h_ref, g_ref, b_ref), (r_ref, ho_ref, hb_ref), _, pctx = _split_refs(refs, 5, 3, 0, plan)
        _plan_start(plan, pctx, grid)
        mix = jnp.dot(a_ref[...], w_ref[...].reshape(D, D), preferred_element_type=F32)
        r = ALPHA * h_ref[...] + mix
        xhat, _ = _ln_stats(r)
        y = xhat * g_ref[...] + b_ref[...]
        r_ref[...] = r
        ho_ref[...] = y
        hb_ref[...] = y.astype(BF16)
        _plan_wait(plan, pctx, grid)

    p_in, p_ospec, p_oshape, p_scr, p_alias = _plan_io(plan, 5, 3)
    return pl.pallas_call(
        body, name=name, grid=grid,
        in_specs=[row, w_spec, row, vec, vec] + [_ANY] * len(p_in),
        out_specs=[row, row, row] + p_ospec,
        out_shape=[jax.ShapeDtypeStruct((T, D), F32), jax.ShapeDtypeStruct((T, D), F32),
                   jax.ShapeDtypeStruct((T, D), BF16)] + p_oshape,
        scratch_shapes=p_scr, input_output_aliases=p_alias, compiler_params=_params(40, 1),
    )(a_b, w, h_prev, g, b, *p_in)


def _ffn_ln(h_b, wbuf, h, g, b, *, name):
    T = h_b.shape[0]
    tm, tf = min(ROW_BLOCK, T), 1024
    nf = 4
    F = nf * tf
    row = pl.BlockSpec((tm, D), lambda i, j: (i, 0))
    vec = pl.BlockSpec((1, D), lambda i, j: (0, 0))

    def body(hb_ref, w1_ref, w2_ref, h_ref, g_ref, b_ref, ra_ref, r_ref, ho_ref, hbo_ref, acc_ref):
        j = pl.program_id(1)
        a = jnp.dot(hb_ref[...], w1_ref[...], preferred_element_type=F32)
        ra = jnp.maximum(a, 0.0)
        ra_ref[...] = ra.astype(BF16)
        p = jnp.dot((ra * ra).astype(BF16), w2_ref[...], preferred_element_type=F32)

        @pl.when(j == 0)
        def _():
            acc_ref[...] = p

        @pl.when(j > 0)
        def _():
            acc_ref[...] += p

        @pl.when(j == nf - 1)
        def _():
            r = ALPHA * h_ref[...] + acc_ref[...]
            xhat, _ = _ln_stats(r)
            y = xhat * g_ref[...] + b_ref[...]
            r_ref[...] = r
            ho_ref[...] = y
            hbo_ref[...] = y.astype(BF16)

    return pl.pallas_call(
        body, name=name, grid=(T // tm, nf),
        in_specs=[row, pl.BlockSpec((None, D, tf), lambda i, j: (j, 0, 0)),
                  pl.BlockSpec((None, tf, D), lambda i, j: (j, 1, 0)), row, vec, vec],
        out_specs=[pl.BlockSpec((tm, tf), lambda i, j: (i, j)), row, row, row],
        out_shape=[jax.ShapeDtypeStruct((T, F), BF16), jax.ShapeDtypeStruct((T, D), F32),
                   jax.ShapeDtypeStruct((T, D), F32), jax.ShapeDtypeStruct((T, D), BF16)],
        scratch_shapes=[pltpu.VMEM((tm, D), F32)],
        compiler_params=_params(48, 2),
    )(h_b, wbuf, wbuf, h, g, b)


def _loss_dy(y, tgt):
    T = y.shape[0]
    tm = min(ROW_BLOCK, T)
    row = pl.BlockSpec((tm, D), lambda i: (i, 0))

    def body(y_ref, t_ref, dy_ref, ls_ref):
        e = y_ref[...] - t_ref[...]
        dy_ref[...] = e * (1.0 / D)

        @pl.when(pl.program_id(0) == 0)
        def _():
            ls_ref[...] = jnp.zeros_like(ls_ref)

        ls_ref[...] += _fold8(e * e)

    return pl.pallas_call(
        body, name="loss_dy", grid=(T // tm,), in_specs=[row, row],
        out_specs=[row, pl.BlockSpec((8, D), lambda i: (0, 0))],
        out_shape=[jax.ShapeDtypeStruct((T, D), F32), jax.ShapeDtypeStruct((8, D), F32)],
        compiler_params=_params(32, 1),
    )(y, tgt)


def _ln_bwd(dy, r, g, *, name):
    T = dy.shape[0]
    tm = min(ROW_BLOCK, T)
    row = pl.BlockSpec((tm, D), lambda i: (i, 0))
    acc = pl.BlockSpec((8, D), lambda i: (0, 0))

    def body(dy_ref, r_ref, g_ref, dr_ref, drb_ref, dg_ref, db_ref):
        @pl.when(pl.program_id(0) == 0)
        def _():
            dg_ref[...] = jnp.zeros_like(dg_ref)
            db_ref[...] = jnp.zeros_like(db_ref)

        dy_ = dy_ref[...]
        xhat, rstd = _ln_stats(r_ref[...])
        dxh = dy_ * g_ref[...]
        m1 = jnp.mean(dxh, -1, keepdims=True)
        m2 = jnp.mean(dxh * xhat, -1, keepdims=True)
        dr = rstd * (dxh - m1 - xhat * m2)
        dr_ref[...] = dr
        drb_ref[...] = dr.astype(BF16)
        dg_ref[...] += _fold8(dy_ * xhat)
        db_ref[...] += _fold8(dy_)

    return pl.pallas_call(
        body, name=name, grid=(T // tm,),
        in_specs=[row, row, pl.BlockSpec((1, D), lambda i: (0, 0))],
        out_specs=[row, row, acc, acc],
        out_shape=[jax.ShapeDtypeStruct((T, D), F32), jax.ShapeDtypeStruct((T, D), BF16),
                   jax.ShapeDtypeStruct((8, D), F32), jax.ShapeDtypeStruct((8, D), F32)],
        compiler_params=_params(40, 1),
    )(dy, r, g)


def _rope(x, c, s1, s2):
    return x * c + pltpu.roll(x, 112, 1) * s1 + pltpu.roll(x, 16, 1) * s2


def _rope_t(dy, c, s1, s2):
    return dy * c + pltpu.roll(dy * s1, 16, 1) + pltpu.roll(dy * s2, 112, 1)


def _rms(x, g):
    rstd = lax.rsqrt(jnp.mean(x * x, -1, keepdims=True) + EPS)
    xhat = x * rstd
    return xhat * g, xhat, rstd


def _mla_prep(z0, gq, gkv, wq, wk, wv, rc, rs1, rs2):
    T = z0.shape[0]
    tm = min(ROW_BLOCK, T)
    HW = HEADS * 128

    def body(cq_ref, ckv_ref, kr_ref, gq_ref, gkv_ref, wq_ref, wk_ref, wv_ref, c_ref, s1_ref, s2_ref,
             q_ref, k_ref, v_ref):
        nq = _rms(cq_ref[...], gq_ref[...])[0].astype(BF16)
        nkv = _rms(ckv_ref[...], gkv_ref[...])[0].astype(BF16)
        q = jnp.dot(nq, wq_ref[...], preferred_element_type=F32)
        k = jnp.dot(nkv, wk_ref[...], preferred_element_type=F32)
        v = jnp.dot(nkv, wv_ref[...], preferred_element_type=F32)
        c, s1, s2 = c_ref[...], s1_ref[...], s2_ref[...]
        kr = _rope(pltpu.roll(kr_ref[...], 64, 1), c, s1, s2)
        for h in range(HEADS):
            sl = slice(h * 128, (h + 1) * 128)
            q_ref[:, sl] = (_rope(q[:, sl], c, s1, s2) * QK_SCALE).astype(BF16)
            k_ref[:, sl] = (k[:, sl] + kr).astype(BF16)
        v_ref[...] = v.astype(BF16)

    full = lambda shape: pl.BlockSpec(shape, lambda i: (0, 0))
    tab = pl.BlockSpec((tm, 128), lambda i: (i, 0))
    return pl.pallas_call(
        body, name="mla_prep", grid=(T // tm,),
        in_specs=[pl.BlockSpec((tm, 256), lambda i: (i, 0)), pl.BlockSpec((tm, 256), lambda i: (i, 1)),
                  pl.BlockSpec((tm, 128), lambda i: (i, 12)), full((1, 256)), full((1, 256)),
                  full((256, HW)), full((256, HW)), full((256, 512)), tab, tab, tab],
        out_specs=[pl.BlockSpec((tm, HW), lambda i: (i, 0)), pl.BlockSpec((tm, HW), lambda i: (i, 0)),
                   pl.BlockSpec((tm, 512), lambda i: (i, 0))],
        out_shape=[jax.ShapeDtypeStruct((T, HW), BF16), jax.ShapeDtypeStruct((T, HW), BF16),
                   jax.ShapeDtypeStruct((T, 512), BF16)],
        compiler_params=_params(40, 1),
    )(z0, z0, z0, gq, gkv, wq, wk, wv, rc, rs1, rs2)


def _flash_fwd(q, k, v, plan=None):
    T = q.shape[0]
    bq = min(ROW_BLOCK, T)
    nq = T // bq
    grid = (4, nq, nq)

    def body(*refs):
        (q_ref, k_ref, v_ref), (o_ref, lse_ref), (m_sc, l_sc, acc_sc), pctx = _split_refs(refs, 3, 2, 3, plan)
        _plan_start(plan, pctx, grid)
        i, j = pl.program_id(1), pl.program_id(2)
        first = lax.broadcasted_iota(jnp.int32, (bq, 128), 1) < 64

        @pl.when(j == 0)
        def _():
            m_sc[...] = jnp.full_like(m_sc, -jnp.inf)
            l_sc[...] = jnp.zeros_like(l_sc)
            acc_sc[...] = jnp.zeros_like(acc_sc)

        def step(masked):
            vp = v_ref[...]
            acc = acc_sc[...]
            for h in range(2):
                sl = slice(h * 128, (h + 1) * 128)
                s = lax.dot_general(q_ref[:, sl], k_ref[:, sl], NT_DIMS, preferred_element_type=F32)
                if masked:
                    rows = lax.broadcasted_iota(jnp.int32, (bq, bq), 0)
                    cols = lax.broadcasted_iota(jnp.int32, (bq, bq), 1)
                    s = jnp.where(cols <= rows, s, -jnp.inf)
                m_prev = m_sc[h, :, 0:1]
                m_new = jnp.maximum(m_prev, jnp.max(s, axis=1, keepdims=True))
                alpha = jnp.exp(m_prev - m_new)
                p = jnp.exp(s - m_new)
                l_new = alpha * l_sc[h, :, 0:1] + jnp.sum(p, axis=1, keepdims=True)
                pv = jnp.dot(p.astype(BF16), vp, preferred_element_type=F32)
                mine = first if h == 0 else jnp.logical_not(first)
                acc = jnp.where(mine, acc * alpha + pv, acc)
                m_sc[h] = jnp.broadcast_to(m_new, (bq, 128))
                l_sc[h] = jnp.broadcast_to(l_new, (bq, 128))
            acc_sc[...] = acc

        @pl.when(j < i)
        def _():
            step(False)

        @pl.when(j == i)
        def _():
            step(True)
            l0, l1 = l_sc[0], l_sc[1]
            o_ref[...] = (acc_sc[...] / jnp.where(first, l0, l1)).astype(BF16)
            lse_ref[...] = jnp.where(first, m_sc[0] + jnp.log(l0), m_sc[1] + jnp.log(l1))

        _plan_wait(plan, pctx, grid)

    kv = lambda hp, i, j: (jnp.minimum(i, j), hp)
    p_in, p_ospec, p_oshape, p_scr, p_alias = _plan_io(plan, 3, 2)
    return pl.pallas_call(
        body, name="flash_fwd", grid=grid,
        in_specs=[pl.BlockSpec((bq, 256), lambda hp, i, j: (i, hp)), pl.BlockSpec((bq, 256), kv),
                  pl.BlockSpec((bq, 128), kv)] + [_ANY] * len(p_in),
        out_specs=[pl.BlockSpec((bq, 128), lambda hp, i, j: (i, hp)),
                   pl.BlockSpec((bq, 128), lambda hp, i, j: (i, hp))] + p_ospec,
        out_shape=[jax.ShapeDtypeStruct((T, 512), BF16), jax.ShapeDtypeStruct((T, 512), F32)] + p_oshape,
        scratch_shapes=[pltpu.VMEM((2, bq, 128), F32), pltpu.VMEM((2, bq, 128), F32), pltpu.VMEM((bq, 128), F32)] + p_scr,
        input_output_aliases=p_alias, compiler_params=_params(32, 3),
    )(q, k, v, *p_in)


def _attn_delta(dmix, o):
    T = o.shape[0]
    tm = min(ROW_BLOCK, T)
    blk = pl.BlockSpec((tm, 512), lambda i: (i, 0))

    def body(do_ref, o_ref, delta_ref, dob_ref):
        first = lax.broadcasted_iota(jnp.int32, (tm, 128), 1) < 64
        for hp in range(4):
            sl = slice(hp * 128, (hp + 1) * 128)
            prod = do_ref[:, sl] * o_ref[:, sl].astype(F32)
            d0 = jnp.sum(jnp.where(first, prod, 0.0), axis=1, keepdims=True)
            d1 = jnp.sum(jnp.where(first, 0.0, prod), axis=1, keepdims=True)
            delta_ref[:, sl] = jnp.where(first, d0, d1)
        dob_ref[...] = do_ref[...].astype(BF16)

    return pl.pallas_call(
        body, name="attn_delta", grid=(T // tm,), in_specs=[blk, blk], out_specs=[blk, blk],
        out_shape=[jax.ShapeDtypeStruct((T, 512), F32), jax.ShapeDtypeStruct((T, 512), BF16)],
        compiler_params=_params(32, 1),
    )(dmix, o)


def _flash_bwd(q, k, v, do_b, lse, delta, plan=None):
    T = q.shape[0]
    bq = min(ROW_BLOCK, T)
    nq = T // bq
    grid = (4, nq, nq)

    def body(*refs):
        ((q_ref, k_ref, v_ref, do_ref, lse_ref, dl_ref), (dq_hbm, dk_ref, dv_ref), (dq_sc, dk_sc, dv_sc, sem),
         pctx) = _split_refs(refs, 6, 3, 4, plan)
        _plan_start(plan, pctx, grid)
        hp, j, i = pl.program_id(0), pl.program_id(1), pl.program_id(2)
        first = lax.broadcasted_iota(jnp.int32, (bq, 128), 1) < 64

        @pl.when((j == 0) & (i == 0))
        def _():
            dq_sc[...] = jnp.zeros_like(dq_sc)

        @pl.when(i == j)
        def _():
            dk_sc[...] = jnp.zeros_like(dk_sc)
            dv_sc[...] = jnp.zeros_like(dv_sc)

        def step(masked):
            vp = v_ref[...]
            do = do_ref[...]
            for h in range(2):
                sl = slice(h * 128, (h + 1) * 128)
                qh, kh = q_ref[:, sl], k_ref[:, sl]
                s = lax.dot_general(qh, kh, NT_DIMS, preferred_element_type=F32)
                p = jnp.exp(s - lse_ref[:, h * 64:h * 64 + 1])
                if masked:
                    rows = lax.broadcasted_iota(jnp.int32, (bq, bq), 0)
                    cols = lax.broadcasted_iota(jnp.int32, (bq, bq), 1)
                    p = jnp.where(cols <= rows, p, 0.0)
                mine = first if h == 0 else jnp.logical_not(first)
                do_h = jnp.where(mine, do, jnp.zeros_like(do))
                dv_sc[...] += lax.dot_general(p.astype(BF16), do_h, TN_DIMS, preferred_element_type=F32)
                dp = lax.dot_general(do_h, vp, NT_DIMS, preferred_element_type=F32)
                ds = (p * (dp - dl_ref[:, h * 64:h * 64 + 1])).astype(BF16)
                dq_sc[i, :, sl] += jnp.dot(ds, kh, preferred_element_type=F32)
                dk_sc[:, sl] += lax.dot_general(ds, qh, TN_DIMS, preferred_element_type=F32)

        @pl.when(i > j)
        def _():
            step(False)

        @pl.when(i == j)
        def _():
            step(True)

        @pl.when(i == nq - 1)
        def _():
            dk_ref[...] = dk_sc[...]
            dv_ref[...] = dv_sc[...]

        @pl.when((j == nq - 1) & (i == nq - 1))
        def _():
            cp = pltpu.make_async_copy(dq_sc, dq_hbm.at[hp], sem)
            cp.start()
            cp.wait()

        _plan_wait(plan, pctx, grid)

    qi = lambda hp, j, i: (jnp.maximum(i, j), hp)
    kj = lambda hp, j, i: (j, hp)
    p_in, p_ospec, p_oshape, p_scr, p_alias = _plan_io(plan, 6, 3)
    return pl.pallas_call(
        body, name="flash_bwd", grid=grid,
        in_specs=[pl.BlockSpec((bq, 256), qi), pl.BlockSpec((bq, 256), kj), pl.BlockSpec((bq, 128), kj),
                  pl.BlockSpec((bq, 128), qi), pl.BlockSpec((bq, 128), qi), pl.BlockSpec((bq, 128), qi)]
        + [_ANY] * len(p_in),
        out_specs=[_ANY, pl.BlockSpec((bq, 256), kj), pl.BlockSpec((bq, 128), kj)] + p_ospec,
        out_shape=[jax.ShapeDtypeStruct((4, nq, bq, 256), F32), jax.ShapeDtypeStruct((T, 1024), F32),
                   jax.ShapeDtypeStruct((T, 512), F32)] + p_oshape,
        scratch_shapes=[pltpu.VMEM((nq, bq, 256), F32), pltpu.VMEM((bq, 256), F32), pltpu.VMEM((bq, 128), F32),
                        pltpu.SemaphoreType.DMA] + p_scr,
        input_output_aliases=p_alias, compiler_params=_params(40, 3),
    )(q, k, v, do_b, lse, delta, *p_in)


def _mla_bwd(z0, dq4, dk, dv, gq, gkv, wq, wk, wv, rc, rs1, rs2, plan=None):
    T = z0.shape[0]
    tm = dq4.shape[2]
    HW = HEADS * 128
    grid = (T // tm,)

    def body(*refs):
        ((cq_ref, ckv_ref, dq_ref, dk_ref, dv_ref, gq_ref, gkv_ref, wq_ref, wk_ref, wv_ref, c_ref, s1_ref, s2_ref),
         (dc_ref, dkr_ref, dwq_ref, dwk_ref, dwv_ref, dgq_ref, dgkv_ref), _, pctx) = _split_refs(refs, 13, 7, 0, plan)
        _plan_start(plan, pctx, grid)

        @pl.when(pl.program_id(0) == 0)
        def _():
            for ref in (dwq_ref, dwk_ref, dwv_ref, dgq_ref, dgkv_ref):
                ref[...] = jnp.zeros_like(ref)

        c, s1, s2 = c_ref[...], s1_ref[...], s2_ref[...]
        lane = lax.broadcasted_iota(jnp.int32, (tm, 128), 1)
        nq, xq, rq = _rms(cq_ref[...], gq_ref[...])
        nkv, xkv, rkv = _rms(ckv_ref[...], gkv_ref[...])
        nq_b, nkv_b = nq.astype(BF16), nkv.astype(BF16)

        dq_parts, dk_parts = [], []
        dkr = jnp.zeros((tm, 128), F32)
        for h in range(HEADS):
            blk = dq_ref[h // 2, :, (h % 2) * 128:(h % 2 + 1) * 128] * QK_SCALE
            dq_parts.append(_rope_t(blk, c, s1, s2).astype(BF16))
            kb = dk_ref[:, h * 128:(h + 1) * 128]
            dk_parts.append(jnp.where(lane < NOPE, kb, 0.0).astype(BF16))
            dkr = dkr + kb
        dq_b = jnp.concatenate(dq_parts, axis=1)
        dk_b = jnp.concatenate(dk_parts, axis=1)
        dv_b = dv_ref[...].astype(BF16)

        dwq_ref[...] += lax.dot_general(nq_b, dq_b, TN_DIMS, preferred_element_type=F32)
        dwk_ref[...] += lax.dot_general(nkv_b, dk_b, TN_DIMS, preferred_element_type=F32)
        dwv_ref[...] += lax.dot_general(nkv_b, dv_b, TN_DIMS, preferred_element_type=F32)
        dnq = lax.dot_general(dq_b, wq_ref[...], NT_DIMS, preferred_element_type=F32)
        dnkv = (lax.dot_general(dk_b, wk_ref[...], NT_DIMS, preferred_element_type=F32)
                + lax.dot_general(dv_b, wv_ref[...], NT_DIMS, preferred_element_type=F32))

        def rms_bwd(dn, xhat, rstd, g):
            dxh = dn * g
            return rstd * (dxh - xhat * jnp.mean(dxh * xhat, -1, keepdims=True))

        dc_ref[:, :256] = rms_bwd(dnq, xq, rq, gq_ref[...]).astype(BF16)
        dc_ref[:, 256:] = rms_bwd(dnkv, xkv, rkv, gkv_ref[...]).astype(BF16)
        dgq_ref[...] += _fold8(dnq * xq)
        dgkv_ref[...] += _fold8(dnkv * xkv)
        dkr = pltpu.roll(_rope_t(dkr, c, s1, s2), 64, 1)
        dkr_ref[...] = jnp.where(lane < ROPE, dkr, 0.0).astype(BF16)
        _plan_wait(plan, pctx, grid)

    full = lambda shape: pl.BlockSpec(shape, lambda i: (0,) * len(shape))
    tab = pl.BlockSpec((tm, 128), lambda i: (i, 0))
    p_in, p_ospec, p_oshape, p_scr, p_alias = _plan_io(plan, 13, 7)
    return pl.pallas_call(
        body, name="mla_bwd", grid=grid,
        in_specs=[pl.BlockSpec((tm, 256), lambda i: (i, 0)), pl.BlockSpec((tm, 256), lambda i: (i, 1)),
                  pl.BlockSpec((4, None, tm, 256), lambda i: (0, i, 0, 0)),
                  pl.BlockSpec((tm, HW), lambda i: (i, 0)), pl.BlockSpec((tm, 512), lambda i: (i, 0)),
                  full((1, 256)), full((1, 256)), full((256, HW)), full((256, HW)), full((256, 512)), tab, tab, tab]
        + [_ANY] * len(p_in),
        out_specs=[pl.BlockSpec((tm, 512), lambda i: (i, 0)), tab, full((256, HW)), full((256, HW)),
                   full((256, 512)), full((8, 256)), full((8, 256))] + p_ospec,
        out_shape=[jax.ShapeDtypeStruct((T, 512), BF16), jax.ShapeDtypeStruct((T, 128), BF16),
                   jax.ShapeDtypeStruct((256, HW), F32), jax.ShapeDtypeStruct((256, HW), F32),
                   jax.ShapeDtypeStruct((256, 512), F32), jax.ShapeDtypeStruct((8, 256), F32),
                   jax.ShapeDtypeStruct((8, 256), F32)] + p_oshape,
        scratch_shapes=p_scr, input_output_aliases=p_alias, compiler_params=_params(48, 1),
    )(z0, z0, dq4, dk, dv, gq, gkv, wq, wk, wv, rc, rs1, rs2, *p_in)


def _sgu_fwd(z0, a_out, ln_g, ln_b, w, b_t):
    T = z0.shape[0]
    tm = min(ROW_BLOCK, T)
    W = SGU_G * SGU_C

    def body(u_ref, v_ref, a_ref, g_ref, b_ref, w_ref, bt_ref, o_ref):
        o_ref[:, :W] = a_ref[...]
        ug = _gelu(u_ref[...])
        xhat, _ = _ln_stats(_gelu(v_ref[...]))
        vn = (xhat * g_ref[...] + b_ref[...]).astype(BF16)
        tril = lax.broadcasted_iota(jnp.int32, (SGU_C, SGU_C), 0) >= lax.broadcasted_iota(jnp.int32, (SGU_C, SGU_C), 1)
        for g in range(SGU_G):
            cs = slice(g * SGU_C, (g + 1) * SGU_C)
            wg = jnp.where(tril, w_ref[g], 0.0).astype(BF16)
            bcol = bt_ref[:, g:g + 1]
            for c in range(tm // SGU_C):
                rs = slice(c * SGU_C, (c + 1) * SGU_C)
                mixed = jnp.dot(wg, vn[rs, cs], preferred_element_type=F32) + bcol
                o_ref[rs, W + g * SGU_C:W + (g + 1) * SGU_C] = (ug[rs, cs] * mixed).astype(BF16)

    full = lambda shape: pl.BlockSpec(shape, lambda i: (0,) * len(shape))
    return pl.pallas_call(
        body, name="sgu_fwd", grid=(T // tm,),
        in_specs=[pl.BlockSpec((tm, W), lambda i: (i, 1)), pl.BlockSpec((tm, W), lambda i: (i, 2)),
                  pl.BlockSpec((tm, W), lambda i: (i, 0)),
                  full((1, W)), full((1, W)), full((SGU_G, SGU_C, SGU_C)), full((SGU_C, SGU_G))],
        out_specs=pl.BlockSpec((tm, 2 * W), lambda i: (i, 0)),
        out_shape=jax.ShapeDtypeStruct((T, 2 * W), BF16),
        compiler_params=_params(32, 1),
    )(z0, z0, a_out, ln_g, ln_b, w, b_t)


def _sgu_bwd(z0, dmix, dc, dkr, ln_g, ln_b, w, b_t):
    T = z0.shape[0]
    tm = min(ROW_BLOCK, T)
    W = SGU_G * SGU_C

    def body(u_ref, v_ref, do_ref, dc_ref, dkr_ref, g_ref, b_ref, w_ref, bt_ref, dz_ref, dw_ref, db_ref, dlg_ref,
             dlb_ref):
        @pl.when(pl.program_id(0) == 0)
        def _():
            for ref in (dw_ref, db_ref, dlg_ref, dlb_ref):
                ref[...] = jnp.zeros_like(ref)

        dz_ref[:, :W] = dc_ref[...]
        dz_ref[:, 3 * W:] = dkr_ref[...]

        u, v, dout = u_ref[...], v_ref[...], do_ref[...]
        ug = _gelu(u)
        xhat, rstd = _ln_stats(_gelu(v))
        vn = (xhat * g_ref[...] + b_ref[...]).astype(BF16)
        dmixed = dout * ug
        dmixed_b = dmixed.astype(BF16)
        tril = lax.broadcasted_iota(jnp.int32, (SGU_C, SGU_C), 0) >= lax.broadcasted_iota(jnp.int32, (SGU_C, SGU_C), 1)
        lane = lax.broadcasted_iota(jnp.int32, (SGU_C, SGU_C), 1)
        dvn_cols = []
        for g in range(SGU_G):
            cs = slice(g * SGU_C, (g + 1) * SGU_C)
            wg = jnp.where(tril, w_ref[g], 0.0).astype(BF16)
            bcol = bt_ref[:, g:g + 1]
            dw_g = jnp.zeros((SGU_C, SGU_C), F32)
            db_g = jnp.zeros((SGU_C, 1), F32)
            dvn_rows = []
            for c in range(tm // SGU_C):
                rs = slice(c * SGU_C, (c + 1) * SGU_C)
                mixed = jnp.dot(wg, vn[rs, cs], preferred_element_type=F32) + bcol
                dz_ref[rs, W + g * SGU_C:W + (g + 1) * SGU_C] = (dout[rs, cs] * mixed * _gelu_grad(u[rs, cs])).astype(BF16)
                dm = dmixed_b[rs, cs]
                dw_g = dw_g + lax.dot_general(dm, vn[rs, cs], NT_DIMS, preferred_element_type=F32)
                db_g = db_g + jnp.sum(dmixed[rs, cs], axis=1, keepdims=True)
                dvn_rows.append(lax.dot_general(wg, dm, TN_DIMS, preferred_element_type=F32))
            dw_ref[g] += jnp.where(tril, dw_g, 0.0)
            db_ref[...] += jnp.where(lane == g, db_g, 0.0)
            dvn_cols.append(jnp.concatenate(dvn_rows, axis=0))
        dvn = jnp.concatenate(dvn_cols, axis=1)
        dxh = dvn * g_ref[...]
        m1 = jnp.mean(dxh, -1, keepdims=True)
        m2 = jnp.mean(dxh * xhat, -1, keepdims=True)
        dvg = rstd * (dxh - m1 - xhat * m2)
        dz_ref[:, 2 * W:3 * W] = (dvg * _gelu_grad(v)).astype(BF16)
        dlg_ref[...] += _fold8(dvn * xhat)
        dlb_ref[...] += _fold8(dvn)

    full = lambda shape: pl.BlockSpec(shape, lambda i: (0,) * len(shape))
    return pl.pallas_call(
        body, name="sgu_bwd", grid=(T // tm,),
        in_specs=[pl.BlockSpec((tm, W), lambda i: (i, 1)), pl.BlockSpec((tm, W), lambda i: (i, 2)),
                  pl.BlockSpec((tm, W), lambda i: (i, 1)), pl.BlockSpec((tm, W), lambda i: (i, 0)),
                  pl.BlockSpec((tm, 128), lambda i: (i, 0)),
                  full((1, W)), full((1, W)), full((SGU_G, SGU_C, SGU_C)), full((SGU_C, SGU_G))],
        out_specs=[pl.BlockSpec((tm, 3 * W + 128), lambda i: (i, 0)), full((SGU_G, SGU_C, SGU_C)),
                   full((SGU_C, SGU_C)), full((8, W)), full((8, W))],
        out_shape=[jax.ShapeDtypeStruct((T, 3 * W + 128), BF16), jax.ShapeDtypeStruct((SGU_G, SGU_C, SGU_C), F32),
                   jax.ShapeDtypeStruct((SGU_C, SGU_C), F32), jax.ShapeDtypeStruct((8, W), F32),
                   jax.ShapeDtypeStruct((8, W), F32)],
        compiler_params=_params(40, 1),
    )(z0, z0, dmix, dc, dkr, ln_g, ln_b, w, b_t)


def _hg_lower_bound(lb_ref):
    a0, a1 = lb_ref[0:1, :], lb_ref[1:2, :]
    m = jnp.maximum(a0, a1)
    e0, e1 = jnp.exp(a0 - m), jnp.exp(a1 - m)
    return e1 / (e0 + e1)


def _hg_chunk(qc, fc, lb):
    C = HG_CHUNK
    rows = lax.broadcasted_iota(jnp.int32, (C, C), 0)
    cols = lax.broadcasted_iota(jnp.int32, (C, C), 1)
    rowid = lax.broadcasted_iota(jnp.int32, (C, 128), 0)
    sq, sg = _sigmoid(qc), _sigmoid(fc)
    qf = qc * sq
    gate = lb + (1.0 - lb) * sg
    kk = 1.0 - gate
    lg = jnp.log(gate)
    bcum = jnp.dot((rows >= cols).astype(F32), lg, precision=HI, preferred_element_type=F32)
    b_mid = jnp.sum(jnp.where(rowid < C // 2, lg, 0.0), axis=0, keepdims=True)
    b_last = jnp.sum(lg, axis=0, keepdims=True)
    eq, ek, e, eh = jnp.exp(bcum - b_mid), jnp.exp(b_mid - bcum), jnp.exp(bcum), jnp.exp(b_last - bcum)
    qt, kt, qe, khat = qf * eq, kk * ek, qf * e, kk * eh
    a = lax.dot_general(qt.astype(BF16), kt.astype(BF16), NT_DIMS, preferred_element_type=F32)
    a = jnp.where(rows >= cols, a, 0.0)
    return dict(sq=sq, sg=sg, gate=gate, kk=kk, eq=eq, ek=ek, e=e, eh=eh, qt=qt, kt=kt, qe=qe, khat=khat, a=a,
                e_last=jnp.exp(b_last), tril=rows >= cols, rowid=rowid)


def _hgrn_fwd(z4, hg_lb, gnorm):
    T = z4.shape[1]
    tb = min(ROW_BLOCK, T)
    C = HG_CHUNK
    ncb = tb // C

    def body(q_ref, f_ref, i_ref, g_ref, lb_ref, gn_ref, y_ref, o_ref, st_ref, st_sc):
        @pl.when(pl.program_id(1) == 0)
        def _():
            st_sc[...] = jnp.zeros_like(st_sc)

        lb = _hg_lower_bound(lb_ref)
        gn = gn_ref[...]

        def chunk(c, carry):
            rs = pl.ds(pl.multiple_of(c * C, C), C)
            v_b = i_ref[rs, :].astype(BF16)
            gc = g_ref[rs, :]
            x = _hg_chunk(q_ref[rs, :], f_ref[rs, :], lb)
            st = st_sc[...]
            st_ref[c] = st
            o = (jnp.dot(x["a"].astype(BF16), v_b, preferred_element_type=F32)
                 + lax.dot_general(x["qe"].astype(BF16), st.astype(BF16), NT_DIMS, preferred_element_type=F32))
            st_sc[...] = st * x["e_last"] + lax.dot_general(v_b, x["khat"].astype(BF16), TN_DIMS,
                                                            preferred_element_type=F32)
            o_ref[rs, :] = o
            n = o * lax.rsqrt(jnp.mean(o * o, -1, keepdims=True) + EPS)
            y_ref[rs, :] = (n * gn * (gc * _sigmoid(gc))).astype(BF16)
            return carry

        lax.fori_loop(0, ncb, chunk, 0)

    zb = lambda k: pl.BlockSpec((None, tb, 128), lambda h, t: (k, t, h))
    out = pl.BlockSpec((tb, 128), lambda h, t: (t, h))
    return pl.pallas_call(
        body, name="hgrn_fwd", grid=(HEADS, T // tb),
        in_specs=[zb(0), zb(1), zb(2), zb(3), pl.BlockSpec((2, 128), lambda h, t: (0, h)),
                  pl.BlockSpec((1, 128), lambda h, t: (0, h))],
        out_specs=[out, out, pl.BlockSpec((None, ncb, 128, 128), lambda h, t: (h, t, 0, 0))],
        out_shape=[jax.ShapeDtypeStruct((T, D), BF16), jax.ShapeDtypeStruct((T, D), F32),
                   jax.ShapeDtypeStruct((HEADS, T // C, 128, 128), F32)],
        scratch_shapes=[pltpu.VMEM((128, 128), F32)],
        compiler_params=_params(32, 2),
    )(z4, z4, z4, z4, hg_lb, gnorm)


def _hgrn_bwd(z4, o_raw, dy, states, hg_lb, gnorm):
    T = z4.shape[1]
    tb = min(ROW_BLOCK, T)
    C = HG_CHUNK
    ncb = tb // C
    nt = T // tb

    def body(q_ref, f_ref, i_ref, g_ref, o_ref, dy_ref, st_ref, lb_ref, gn_ref, dz_ref, dlb_ref, dgn_ref, dst_sc):
        @pl.when(pl.program_id(1) == 0)
        def _():
            dst_sc[...] = jnp.zeros_like(dst_sc)
            dlb_ref[...] = jnp.zeros_like(dlb_ref)
            dgn_ref[...] = jnp.zeros_like(dgn_ref)

        lb = _hg_lower_bound(lb_ref)
        gn = gn_ref[...]

        def chunk(cc, carry):
            c = ncb - 1 - cc
            rs = pl.ds(pl.multiple_of(c * C, C), C)
            qc, gc = q_ref[rs, :], g_ref[rs, :]
            v_b = i_ref[rs, :].astype(BF16)
            x = _hg_chunk(qc, f_ref[rs, :], lb)
            st, dst = st_ref[c], dst_sc[...]
            st_b, dst_b = st.astype(BF16), dst.astype(BF16)
            o, dyc = o_ref[rs, :], dy_ref[rs, :]
            sgg = _sigmoid(gc)
            sil = gc * sgg
            rstd = lax.rsqrt(jnp.mean(o * o, -1, keepdims=True) + EPS)
            n = o * rstd
            dgn_ref[...] += _fold8(dyc * n * sil)
            dn = dyc * gn * sil
            do = rstd * (dn - n * jnp.mean(dn * n, -1, keepdims=True))
            dg = dyc * n * gn * (sgg * (1.0 + gc * (1.0 - sgg)))
            do_b = do.astype(BF16)
            da = jnp.where(x["tril"], lax.dot_general(do_b, v_b, NT_DIMS, preferred_element_type=F32), 0.0).astype(BF16)
            qt_b, kt_b, qe_b, khat_b = (x[n_].astype(BF16) for n_ in ("qt", "kt", "qe", "khat"))
            dv = (lax.dot_general(x["a"].astype(BF16), do_b, TN_DIMS, preferred_element_type=F32)
                  + lax.dot_general(khat_b, dst_b, NT_DIMS, preferred_element_type=F32))
            dqt = jnp.dot(da, kt_b, preferred_element_type=F32)
            dqe = jnp.dot(do_b, st_b, preferred_element_type=F32)
            dkt = lax.dot_general(da, qt_b, TN_DIMS, preferred_element_type=F32)
            dkhat = jnp.dot(v_b, dst_b, preferred_element_type=F32)
            dst_sc[...] = lax.dot_general(do_b, qe_b, TN_DIMS, preferred_element_type=F32) + dst * x["e_last"]
            de_last = jnp.sum(st * dst, axis=0, keepdims=True)
            dqf = dqt * x["eq"] + dqe * x["e"]
            dkk = dkt * x["ek"] + dkhat * x["eh"]
            dkh_kh = dkhat * x["khat"]
            db = dqt * qt_b.astype(F32) - dkt * kt_b.astype(F32) + dqe * x["qe"] - dkh_kh
            db_last = jnp.sum(dkh_kh, axis=0, keepdims=True) + de_last * x["e_last"]
            db = db + jnp.where(x["rowid"] == C - 1, db_last, 0.0)
            rows = lax.broadcasted_iota(jnp.int32, (C, C), 0)
            cols = lax.broadcasted_iota(jnp.int32, (C, C), 1)
            dlg = jnp.dot((rows <= cols).astype(F32), db, precision=HI, preferred_element_type=F32)
            dgate = dlg / x["gate"] - dkk
            sg, sq = x["sg"], x["sq"]
            dlb_ref[...] += _fold8(dgate * (1.0 - sg)) * (lb * (1.0 - lb))
            dz_ref[0, rs, :] = (dqf * (sq * (1.0 + qc * (1.0 - sq)))).astype(BF16)
            dz_ref[1, rs, :] = (dgate * (1.0 - lb) * sg * (1.0 - sg)).astype(BF16)
            dz_ref[2, rs, :] = dv.astype(BF16)
            dz_ref[3, rs, :] = dg.astype(BF16)
            return carry

        lax.fori_loop(0, ncb, chunk, 0)

    zb = lambda k: pl.BlockSpec((None, tb, 128), lambda h, t: (k, nt - 1 - t, h))
    blk = pl.BlockSpec((tb, 128), lambda h, t: (nt - 1 - t, h))
    acc = pl.BlockSpec((8, 128), lambda h, t: (0, h))
    return pl.pallas_call(
        body, name="hgrn_bwd", grid=(HEADS, nt),
        in_specs=[zb(0), zb(1), zb(2), zb(3), blk, blk,
                  pl.BlockSpec((None, ncb, 128, 128), lambda h, t: (h, nt - 1 - t, 0, 0)),
                  pl.BlockSpec((2, 128), lambda h, t: (0, h)), pl.BlockSpec((1, 128), lambda h, t: (0, h))],
        out_specs=[pl.BlockSpec((4, tb, 128), lambda h, t: (0, nt - 1 - t, h)), acc, acc],
        out_shape=[jax.ShapeDtypeStruct((4, T, D), BF16), jax.ShapeDtypeStruct((8, D), F32),
                   jax.ShapeDtypeStruct((8, D), F32)],
        scratch_shapes=[pltpu.VMEM((128, 128), F32)],
        compiler_params=_params(32, 2),
    )(z4, z4, z4, z4, o_raw, dy, states, hg_lb, gnorm)


def _adamw(w, g, m, v, *, name):
    R, L = w.shape
    tr = R if R <= 512 else 512
    assert R % tr == 0
    blk = pl.BlockSpec((tr, L), lambda i: (i, 0))
    c1, c2 = 1.0 - B1 ** STEP, 1.0 - B2 ** STEP

    def body(w_ref, g_ref, m_ref, v_ref, d_ref, mo_ref, vo_ref):
        g_ = g_ref[...]
        m_ = B1 * m_ref[...] + (1.0 - B1) * g_
        v_ = B2 * v_ref[...] + (1.0 - B2) * (g_ * g_)
        d_ref[...] = -LR * ((m_ / c1) / (jnp.sqrt(v_ / c2) + ADAM_EPS) + WD * w_ref[...])
        mo_ref[...] = m_
        vo_ref[...] = v_

    sds = jax.ShapeDtypeStruct((R, L), F32)
    return pl.pallas_call(
        body, name=name, grid=(R // tr,), in_specs=[blk] * 4, out_specs=[blk] * 3, out_shape=[sds] * 3,
        compiler_params=_params(32, 1),
    )(w, g, m, v)


def _adamw_rows(w, m, v, gbufs, row0, *, name):
    L, R, C = w.shape
    tr = 256
    assert R % tr == 0 and row0 % tr == 0 and len(gbufs) == L
    blk = pl.BlockSpec((None, tr, C), lambda l, i: (l, i, 0))
    gblk = pl.BlockSpec((tr, C), lambda l, i: (row0 // tr + i, 0))
    c1, c2 = 1.0 - B1 ** STEP, 1.0 - B2 ** STEP

    def body(*refs):
        w_ref, m_ref, v_ref = refs[:3]
        g_refs = refs[3:3 + L]
        go_ref, d_ref, mo_ref, vo_ref = refs[3 + L:]
        g_ = g_refs[0][...]
        for l in range(1, L):
            g_ = jnp.where(pl.program_id(0) == l, g_refs[l][...], g_)
        m_ = B1 * m_ref[...] + (1.0 - B1) * g_
        v_ = B2 * v_ref[...] + (1.0 - B2) * (g_ * g_)
        go_ref[...] = g_
        d_ref[...] = -LR * ((m_ / c1) / (jnp.sqrt(v_ / c2) + ADAM_EPS) + WD * w_ref[...])
        mo_ref[...] = m_
        vo_ref[...] = v_

    sds = jax.ShapeDtypeStruct((L, R, C), F32)
    return pl.pallas_call(
        body, name=name, grid=(L, R // tr), in_specs=[blk] * 3 + [gblk] * L, out_specs=[blk] * 4, out_shape=[sds] * 4,
        compiler_params=_params(32, 2),
    )(w, m, v, *gbufs)


def _add_pairs(g, theirs, ids, *, name):
    n, R, L = theirs.shape
    tr = 128
    nb = R // tr

    def body(ids_ref, a_ref, b_ref, o_ref):
        o_ref[...] = (a_ref[...].astype(F32) + b_ref[...].astype(F32)).astype(BF16)

    blk = pl.BlockSpec((n, tr, L), lambda i, ids: (0, i, 0))
    return pl.pallas_call(
        body, name=name, out_shape=jax.ShapeDtypeStruct((n, R, L), BF16),
        grid_spec=pltpu.PrefetchScalarGridSpec(
            num_scalar_prefetch=1, grid=(nb,),
            in_specs=[pl.BlockSpec((n, tr, L), lambda i, ids: (0, ids[1] * nb + i, 0)), blk], out_specs=blk),
        compiler_params=_params(16, 1),
    )(ids, g, theirs)


def _sum_chips(pair, parts, ids, *, name):
    _, R, L = parts.shape
    tr = 128

    def body(ids_ref, o_ref, r_ref, out_ref):
        out_ref[...] = ((o_ref[...].astype(F32) + r_ref[0].astype(F32)) + r_ref[1].astype(F32)) + r_ref[2].astype(F32)

    return pl.pallas_call(
        body, name=name, out_shape=jax.ShapeDtypeStruct((2, R, L), F32),
        grid_spec=pltpu.PrefetchScalarGridSpec(
            num_scalar_prefetch=1, grid=(R // tr,),
            in_specs=[pl.BlockSpec((None, tr, L), lambda i, ids: (ids[0], i, 0)),
                      pl.BlockSpec((3, tr, L), lambda i, ids: (0, i, 0))],
            out_specs=pl.BlockSpec((None, tr, L), lambda i, ids: (ids[1], i, 0))),
        compiler_params=_params(32, 1),
    )(ids, pair, parts)


def _mesh_ids():
    x, y, c = _mesh_pos()
    return jnp.stack([2 * x + y, c]).astype(jnp.int32)


def _place_shard(rows, ids, *, name):
    R, L = rows.shape
    tr = 256

    def body(ids_ref, in_ref, out_ref):
        out_ref[...] = in_ref[...].astype(BF16)

    return pl.pallas_call(
        body, name=name, out_shape=jax.ShapeDtypeStruct((4, R, L), BF16),
        grid_spec=pltpu.PrefetchScalarGridSpec(
            num_scalar_prefetch=1, grid=(R // tr,), in_specs=[pl.BlockSpec((tr, L), lambda i, ids: (i, 0))],
            out_specs=pl.BlockSpec((None, tr, L), lambda i, ids: (ids[0], i, 0))),
        compiler_params=_params(16, 1),
    )(ids, rows)


def _remote(src, dst, send_sem, recv_sem, to):
    return pltpu.make_async_remote_copy(src_ref=src, dst_ref=dst, send_sem=send_sem, recv_sem=recv_sem,
                                        device_id=to, device_id_type=MESH_IDS)


def _rows(ref, lead, start, size):
    return ref.at[tuple(pl.ds(0, n) for n in ref.shape[:lead]) + (pl.ds(start, size),)]


def _other_chips():
    x, y, _ = _mesh_pos()
    return [(1 - x, y), (x, 1 - y), (1 - x, 1 - y)]


def _plan_gather_ici(bufs):
    n = len(bufs)

    def copies(outs, send, recv):
        x, y, c = _mesh_pos()
        res = []
        for b in range(n):
            half = bufs[b].shape[1] // 2
            mine = _rows(outs[b].at[2 * x + y], 0, c * half, half)
            for j, (cx, cy) in enumerate(_other_chips()):
                res.append((_remote(mine, mine, send(3 * b + j), recv(3 * b + j), (cx, cy, c)),
                            _remote(mine, _rows(outs[b].at[2 * cx + cy], 0, c * half, half),
                                    send(3 * b + j), recv(3 * b + j), (x, y, c))))
        return res

    def start(ins, outs, send, recv, loc):
        for out_cp, _ in copies(outs, send, recv):
            out_cp.start()

    def wait(ins, outs, send, recv, loc):
        for out_cp, in_cp in copies(outs, send, recv):
            in_cp.wait_recv()
            out_cp.wait_send()

    outs = [jax.ShapeDtypeStruct(b.shape, b.dtype) for b in bufs]
    return _Plan(bufs, outs, 3 * n, 0, start, wait, aliases={b: b for b in range(n)})


def _plan_gather_forward(bufs):
    n = len(bufs)

    def copies(outs, send, recv):
        x, y, c = _mesh_pos()
        res = []
        for b in range(n):
            half = bufs[b].shape[1] // 2
            for j, (cx, cy) in enumerate(_other_chips()):
                slot = outs[b].at[2 * cx + cy]
                res.append((_remote(_rows(slot, 0, c * half, half), _rows(slot, 0, c * half, half),
                                    send(3 * b + j), recv(3 * b + j), (x, y, 1 - c)),
                            _remote(_rows(slot, 0, c * half, half), _rows(slot, 0, (1 - c) * half, half),
                                    send(3 * b + j), recv(3 * b + j), (x, y, c))))
        return res

    def start(ins, outs, send, recv, loc):
        for out_cp, _ in copies(outs, send, recv):
            out_cp.start()

    def wait(ins, outs, send, recv, loc):
        for out_cp, in_cp in copies(outs, send, recv):
            in_cp.wait_recv()
            out_cp.wait_send()

    outs = [jax.ShapeDtypeStruct(b.shape, b.dtype) for b in bufs]
    return _Plan(bufs, outs, 3 * n, 0, start, wait, aliases={b: b for b in range(n)})


def _plan_pair_swap(g):
    half = g.shape[1] // 2

    def copy(ins, outs, send, recv, loc):
        x, y, c = _mesh_pos()
        return _remote(_rows(ins[0], 1, (1 - c) * half, half), outs[0], send(0), recv(0), (x, y, 1 - c))

    return _Plan([g], [jax.ShapeDtypeStruct((4, half, g.shape[2]), g.dtype)], 1, 0,
                 lambda *a: copy(*a).start(), lambda *a: copy(*a).wait())


def _plan_pair_gather(buf):
    def copies(ins, outs, send, recv, loc):
        x, y, c = _mesh_pos()
        return (_remote(outs[0].at[c], outs[0].at[c], send(0), recv(0), (x, y, 1 - c)),
                _remote(outs[0].at[c], outs[0].at[1 - c], send(0), recv(0), (x, y, c)))

    def wait(*a):
        out_cp, in_cp = copies(*a)
        in_cp.wait_recv()
        out_cp.wait_send()

    return _Plan([buf], [jax.ShapeDtypeStruct(buf.shape, buf.dtype)], 1, 0, lambda *a: copies(*a)[0].start(), wait,
                 aliases={0: 0})


def _plan_chip_scatter(p):
    def copies(ins, outs, send, recv, loc):
        _, _, c = _mesh_pos()
        return [_remote(ins[0].at[2 * cx + cy], outs[0].at[j], send(j), recv(j), (cx, cy, c))
                for j, (cx, cy) in enumerate(_other_chips())]

    def start(*a):
        for cp in copies(*a):
            cp.start()

    def wait(*a):
        for cp in copies(*a):
            cp.wait()

    return _Plan([p], [jax.ShapeDtypeStruct((3,) + p.shape[1:], p.dtype)], 3, 0, start, wait)


def _small_allreduce(vec):
    R, L = vec.shape

    def body(v_ref, sum_ref, all_ref, send_sems, recv_sems):
        x, y, c = _mesh_pos()
        me = 4 * x + 2 * y + c
        all_ref[me] = v_ref[...]
        cps = []
        for r in range(1, 8):
            peer = (x ^ (r >> 2), y ^ ((r >> 1) & 1), c ^ (r & 1))
            cps.append(pltpu.make_async_remote_copy(
                src_ref=v_ref, dst_ref=all_ref.at[me], send_sem=send_sems.at[r - 1], recv_sem=recv_sems.at[r - 1],
                device_id=peer, device_id_type=MESH_IDS))
        for cp in cps:
            cp.start()
        for r in range(1, 8):
            src = 4 * (x ^ (r >> 2)) + 2 * (y ^ ((r >> 1) & 1)) + (c ^ (r & 1))
            pltpu.make_async_remote_copy(
                src_ref=v_ref, dst_ref=all_ref.at[src], send_sem=send_sems.at[r - 1], recv_sem=recv_sems.at[r - 1],
                device_id=(x, y, c), device_id_type=MESH_IDS).wait_recv()
        for cp in cps:
            cp.wait_send()
        s = all_ref[0]
        for d in range(1, 8):
            s = s + all_ref[d]
        sum_ref[...] = s

    vm = pl.BlockSpec(memory_space=pltpu.VMEM)
    return pl.pallas_call(
        body, name="small_allreduce", in_specs=[vm], out_specs=[vm, vm],
        out_shape=[jax.ShapeDtypeStruct((R, L), F32), jax.ShapeDtypeStruct((8, R, L), F32)],
        scratch_shapes=[pltpu.SemaphoreType.DMA((7,)), pltpu.SemaphoreType.DMA((7,))],
        compiler_params=_params(16),
    )(vec)[0]


ROWS_L1, ROWS_L0, ROWS_ODD = 3328, 2048, 768
ODD_PARTS = (("w_out_e", (256, 1024)), ("w_in_e", (1024, 392)), ("w_qb", (256, 192)), ("w_kvb", (256, 256)))


def _odd_rows(parts, dtype, gnorm=None):
    rows = [parts[n].reshape(-1, 1024).astype(dtype) for n, _ in ODD_PARTS]
    used = sum(r.shape[0] for r in rows)
    if gnorm is not None:
        bits = lax.bitcast_convert_type(gnorm.reshape(-1), BF16).reshape(1, 512)
        rows.append(jnp.pad(bits, ((0, 0), (0, 512))))
        used += 1
    rows.append(jnp.zeros((ROWS_ODD - used, 1024), dtype))
    return jnp.concatenate(rows, axis=0)


def _odd_unrows(buf, with_gnorm=False):
    out, off = {}, 0
    for n, shape in ODD_PARTS:
        nr = math.prod(shape) // 1024
        out[n] = buf[off:off + nr].reshape(shape)
        off += nr
    if with_gnorm:
        out["hg_gnorm"] = lax.bitcast_convert_type(buf[off, :512].reshape(256, 2), F32).reshape(1, 256)
    return out


def _pack_small(vals):
    flat = jnp.concatenate([vals[n].reshape(-1).astype(F32) for n, _ in SMALL])
    return jnp.pad(flat, (0, SMALL_ROWS * 1024 - flat.shape[0])).reshape(SMALL_ROWS, 1024)


def _unpack_small(packed):
    flat = packed.reshape(-1)
    out, off = {}, 0
    for n, shape in SMALL:
        size = math.prod(shape)
        out[n] = flat[off:off + size].reshape(shape)
        off += size
    return out


def _rope_tables(positions):
    half = ROPE // 2
    inv_freq = ROPE_BASE ** (-jnp.arange(half, dtype=F32) / half)
    ang = positions.astype(F32).reshape(-1, 1) * inv_freq
    cos, sin = jnp.cos(ang), jnp.sin(ang)
    T = ang.shape[0]
    one, z16, z32 = jnp.ones((T, NOPE), F32), jnp.zeros((T, half), F32), jnp.zeros((T, 32), F32)
    z64 = jnp.zeros((T, NOPE), F32)
    c = jnp.concatenate([one, cos, cos, z32], axis=1)
    s1 = jnp.concatenate([z64, -sin, z16, z32], axis=1)
    s2 = jnp.concatenate([z64, z16, sin, z32], axis=1)
    return c, s1, s2


def _local_step(x, positions, tgt, odd, bufs, P, exchange):
    T = x.shape[0]
    row = lambda a: a.reshape(1, -1)
    rc, rs1, rs2 = _rope_tables(positions)
    blk = lambda f: pl.BlockSpec((None, D, D), f)

    w_in_e = odd["w_in_e"]
    w_in = jnp.concatenate([w_in_e[:, :512], w_in_e[:, 544:1568], w_in_e[:, 512:544], jnp.zeros((D, 96), BF16)], axis=1)
    wq = jnp.pad(odd["w_qb"].reshape(256, HEADS, NOPE + ROPE), ((0, 0), (0, 0), (0, 32))).reshape(256, HEADS * 128)
    kvb = odd["w_kvb"].reshape(256, HEADS, NOPE + VDIM)
    wk = jnp.pad(kvb[:, :, :NOPE], ((0, 0), (0, 0), (0, 64))).reshape(256, HEADS * 128)
    wv = kvb[:, :, NOPE:].reshape(256, HEADS * VDIM)
    w_out_e = odd["w_out_e"]
    sgu_w = P["sgu_w"][0]
    sgu_bt = P["sgu_b"][0].T
    gq, gkv = P["mla_gq"], P["mla_gkv"]
    gnorm = P["hg_gnorm"]

    x_b = x.astype(BF16)
    z0 = _matmul(x_b, w_in, name="in_proj_e", M=T, N=1664, K=D, tn=1664)[0]
    q, k, v = _mla_prep(z0, gq, gkv, wq, wk, wv, rc, rs1, rs2)
    if exchange:
        ids = _mesh_ids()
        placed = [_place_shard(b, ids, name=f"place_shard_{l}") for l, b in enumerate(bufs)]
        a_out, lse, wg0, wg1 = _flash_fwd(q, k, v, plan=_plan_gather_ici(placed))
    else:
        a_out, lse = _flash_fwd(q, k, v)
        wg0, wg1 = bufs
    mix0 = _sgu_fwd(z0, a_out, P["sgu_ln_g"], P["sgu_ln_b"], sgu_w, sgu_bt)
    res = _proj_ln(mix0, w_out_e, x, row(P["ln1_g"][0]), row(P["ln1_b"][0]), name="out_proj_ln_e",
                   plan=_plan_gather_forward([wg0, wg1]) if exchange else None)
    r1, h1, h1b = res[:3]
    if exchange:
        wg0, wg1 = res[3:]
    ra0, r2, h2, h2b = _ffn_ln(h1b, wg0, h1, row(P["ln2_g"][0]), row(P["ln2_b"][0]), name="ffn_ln_0")
    z4 = _matmul(h2b, wg1, name="in_proj_o", M=T, N=4 * D, K=D, b_spec=blk(lambda i, j, k: (j, 2, 0)),
                 out_shape=jax.ShapeDtypeStruct((4, T, D), F32),
                 o_spec=pl.BlockSpec((None, min(512, T), D), lambda i, j, k: (j, i, 0)))[0]
    y1, o_raw, states = _hgrn_fwd(z4, P["hg_lb"], gnorm)
    r3, h3, h3b = _proj_ln(y1, wg1, h2, row(P["ln1_g"][1]), row(P["ln1_b"][1]), name="out_proj_ln_o", w_rowblk=12)
    ra1, r4, h4, _ = _ffn_ln(h3b, wg1, h3, row(P["ln2_g"][1]), row(P["ln2_b"][1]), name="ffn_ln_1")
    dy, loss_parts = _loss_dy(h4, tgt)

    gs = {}
    ln1_g, ln1_b, ln2_g, ln2_b = [None, None], [None, None], [None, None], [None, None]

    def ffn_bwd(l, dh, r_out, ra, h_mid_b, g2, wg, rows, plan=None):
        dr, dr_b, dg, db = _ln_bwd(dh, r_out, row(g2), name=f"ln2_bwd_{l}")
        ln2_g[l], ln2_b[l] = dg.sum(0), db.sum(0)
        da, *extra = _matmul(dr_b, wg, tb=True, mul=ra, out_dtype=BF16, name=f"ffn_da_{l}", M=T, N=4 * D, K=D,
                             b_spec=blk(lambda i, j, k: (j, 1, 0)), plan=plan)
        gbuf = _matmul(ra, dr_b, ta=True, a_sq=True, name=f"ffn_dw2_{l}", M=4 * D, N=D, K=T, tm=1024, tk=512,
                       out_shape=jax.ShapeDtypeStruct((4, rows, D), BF16), o_spec=blk(lambda i, j, k: (i, 1, 0)))[0]
        gbuf = _matmul(h_mid_b, da, ta=True, name=f"ffn_dw1_{l}", M=D, N=4 * D, K=T, tm=1024, tk=512, into=gbuf,
                       out_shape=jax.ShapeDtypeStruct((4, rows, D), BF16), o_spec=blk(lambda i, j, k: (j, 0, 0)))[0]
        dh_mid = _matmul(da, wg, tb=True, add=dr, add_scale=ALPHA, name=f"ffn_dh_{l}", M=T, N=D, K=4 * D,
                         b_spec=blk(lambda i, j, k: (k, 0, 0)))[0]
        return dh_mid, gbuf, extra

    dh3, g1, _ = ffn_bwd(1, dy, r4, ra1, h3b, P["ln2_g"][1], wg1, ROWS_L1)
    dr3, dr3_b, dg, db = _ln_bwd(dh3, r3, row(P["ln1_g"][1]), name="ln1_bwd_1")
    ln1_g[1], ln1_b[1] = dg.sum(0), db.sum(0)
    g1_sds = jax.ShapeDtypeStruct((4, ROWS_L1, D), BF16)
    g1 = _matmul(y1, dr3_b, ta=True, name="dw_out_o", M=D, N=D, K=T, tm=256, tk=512, into=g1, out_shape=g1_sds,
                 o_spec=pl.BlockSpec((None, 256, D), lambda i, j, k: (i, 12, 0)))[0]
    dmix1 = _matmul(dr3_b, wg1, tb=True, name="dmix_o", M=T, N=D, K=D, b_spec=_rows4_spec(12, 3), b_merge=(D, D))[0]
    dz4, dlb, dgn = _hgrn_bwd(z4, o_raw, dmix1, states, P["hg_lb"], gnorm)
    g1 = _matmul(h2b, dz4, ta=True, name="dw_in_o", M=D, N=4 * D, K=T, tm=1024, tk=512, into=g1, out_shape=g1_sds,
                 b_spec=pl.BlockSpec((None, min(512, T), D), lambda i, j, k: (j, k, 0)),
                 o_spec=blk(lambda i, j, k: (j, 2, 0)))[0]
    dh2 = _matmul(dz4, wg1, tb=True, add=dr3, add_scale=ALPHA, name="dh_in_o", M=T, N=D, K=4 * D,
                  a_spec=pl.BlockSpec((None, min(512, T), D), lambda i, j, k: (k, i, 0)),
                  b_spec=blk(lambda i, j, k: (k, 2, 0)))[0]
    d_lb1 = dlb.sum(0)
    gs["hg_lb"] = jnp.stack([-d_lb1, d_lb1])
    gs["hg_gnorm"] = dgn.sum(0)[None]

    dh1, g0, swapped1 = ffn_bwd(0, dh2, r2, ra0, h1b, P["ln2_g"][0], wg0, ROWS_L0,
                                plan=_plan_pair_swap(g1) if exchange else None)
    dr1, dr1_b, dg, db = _ln_bwd(dh1, r1, row(P["ln1_g"][0]), name="ln1_bwd_0")
    ln1_g[0], ln1_b[0] = dg.sum(0), db.sum(0)
    godd = {"w_out_e": _matmul(mix0, dr1_b, ta=True, name="dw_out_e", M=D, N=D, K=T, tm=1024, tk=512)[0]}
    dmix0, *swapped0 = _matmul(dr1_b, w_out_e, tb=True, name="dmix_e", M=T, N=D, K=D,
                               plan=_plan_pair_swap(g0) if exchange else None)
    delta, do_b = _attn_delta(dmix0, a_out)
    if exchange:
        pair1 = _add_pairs(g1, swapped1[0], ids, name="grad_pair_add_1")
        pair0 = _add_pairs(g0, swapped0[0], ids, name="grad_pair_add_0")
        dq4, dk, dv, parts0, parts1 = _flash_bwd(
            q, k, v, do_b, lse, delta, plan=_join_plans([_plan_chip_scatter(pair0), _plan_chip_scatter(pair1)]))
        half0 = _sum_chips(pair0, parts0, ids, name="grad_chip_sum_0")
        half1 = _sum_chips(pair1, parts1, ids, name="grad_chip_sum_1")
        dc, dkr, dwq, dwk, dwv, dgq, dgkv, g0, g1 = _mla_bwd(
            z0, dq4, dk, dv, gq, gkv, wq, wk, wv, rc, rs1, rs2,
            plan=_join_plans([_plan_pair_gather(half0), _plan_pair_gather(half1)]))
        g0, g1 = g0.reshape(ROWS_L0, D), g1.reshape(ROWS_L1, D)
    else:
        dq4, dk, dv = _flash_bwd(q, k, v, do_b, lse, delta)
        dc, dkr, dwq, dwk, dwv, dgq, dgkv = _mla_bwd(z0, dq4, dk, dv, gq, gkv, wq, wk, wv, rc, rs1, rs2)
    dz0, dsw, dsb, dslg, dslb = _sgu_bwd(z0, dmix0, dc, dkr, P["sgu_ln_g"], P["sgu_ln_b"], sgu_w, sgu_bt)
    dw_in = _matmul(x_b, dz0, ta=True, name="dw_in_e", M=D, N=1664, K=T, tm=1024, tn=1664, tk=512)[0]
    godd["w_in_e"] = jnp.concatenate([dw_in[:, :512], dw_in[:, 1536:1568], dw_in[:, 512:1536]], axis=1)
    grad_x = _matmul(dz0, w_in, tb=True, add=dr1, add_scale=ALPHA, name="dx", M=T, N=D, K=1664, tk=1664)[0]

    godd["w_qb"] = dwq.reshape(256, HEADS, 128)[:, :, :NOPE + ROPE].reshape(256, HEADS * (NOPE + ROPE))
    godd["w_kvb"] = jnp.concatenate([dwk.reshape(256, HEADS, 128)[:, :, :NOPE], dwv.reshape(256, HEADS, VDIM)],
                                    axis=2).reshape(256, HEADS * (NOPE + VDIM))
    gs["mla_gq"], gs["mla_gkv"] = dgq.sum(0)[None], dgkv.sum(0)[None]
    gs["sgu_ln_g"], gs["sgu_ln_b"] = dslg.sum(0)[None], dslb.sum(0)[None]
    gs["sgu_w"], gs["sgu_b"] = dsw[None], dsb[:, :SGU_G].T[None]
    gs["ln1_g"], gs["ln1_b"] = jnp.stack(ln1_g), jnp.stack(ln1_b)
    gs["ln2_g"], gs["ln2_b"] = jnp.stack(ln2_g), jnp.stack(ln2_b)
    return loss_parts, grad_x, g0, g1, godd, gs


WEIGHTS = ['w_in_e', 'mla_gq', 'mla_gkv', 'w_qb', 'w_kvb', 'sgu_ln_g', 'sgu_ln_b', 'sgu_w', 'sgu_b', 'w_out_e',
           'w_in_o', 'hg_lb', 'hg_gnorm', 'w_out_o', 'ln1_g', 'ln1_b', 'w_ff1', 'w_ff2', 'ln2_g', 'ln2_b']


def kernel(x, positions, w_in_e, mla_gq, mla_gkv, w_qb, w_kvb, sgu_ln_g, sgu_ln_b, sgu_w, sgu_b, w_out_e, w_in_o, hg_lb, hg_gnorm, w_out_o, ln1_g, ln1_b, w_ff1, w_ff2, ln2_g, ln2_b, loss_target, m_w_in_e, m_mla_gq, m_mla_gkv, m_w_qb, m_w_kvb, m_sgu_ln_g, m_sgu_ln_b, m_sgu_w, m_sgu_b, m_w_out_e, m_w_in_o, m_hg_lb, m_hg_gnorm, m_w_out_o, m_ln1_g, m_ln1_b, m_w_ff1, m_w_ff2, m_ln2_g, m_ln2_b, v_w_in_e, v_mla_gq, v_mla_gkv, v_w_qb, v_w_kvb, v_sgu_ln_g, v_sgu_ln_b, v_sgu_w, v_sgu_b, v_w_out_e, v_w_in_o, v_hg_lb, v_hg_gnorm, v_w_out_o, v_ln1_g, v_ln1_b, v_w_ff1, v_w_ff2, v_ln2_g, v_ln2_b):
    args = dict(locals())
    w = {n: args[n] for n in WEIGHTS}
    m = {n: args["m_" + n] for n in WEIGHTS}
    v = {n: args["v_" + n] for n in WEIGHTS}
    cx, cy, cc = _mesh_pos()
    chip = 2 * cx + cy

    odd_shard = _odd_rows({"w_out_e": w_out_e[0], "w_in_e": w_in_e[0], "w_qb": w_qb[0], "w_kvb": w_kvb[0]}, BF16,
                          gnorm=hg_gnorm)
    ids = _mesh_ids()
    gathered = _run_plan(_plan_gather_ici([_place_shard(odd_shard, ids, name="place_shard_odd")]), name="odd_gather")[0]
    gathered = _run_plan(_plan_gather_forward([gathered]), name="odd_gather_forward")[0]
    per_chip = [_odd_unrows(gathered[j], with_gnorm=True) for j in range(4)]
    odd = {"w_out_e": jnp.concatenate([p["w_out_e"] for p in per_chip], axis=0)}
    for n in ("w_in_e", "w_qb", "w_kvb"):
        odd[n] = jnp.concatenate([p[n] for p in per_chip], axis=1)
    small = {n: w[n] for n, _ in SMALL if n != "hg_gnorm"}
    small["hg_gnorm"] = jnp.concatenate([p["hg_gnorm"] for p in per_chip], axis=1)
    rows_l0 = jnp.concatenate([w_ff1[0], w_ff2[0]], axis=0).astype(BF16)
    rows_l1 = jnp.concatenate([w_ff1[1], w_ff2[1], w_in_o[0], w_out_o[0]], axis=0).astype(BF16)

    loss_parts, grad_x, g_l0, g_l1, godd, gs = _local_step(x[0], positions[0], loss_target[0], odd, (rows_l0, rows_l1),
                                                            small, True)

    loss = lax.psum((0.5 / D) * jnp.sum(loss_parts), ("x", "y", "c"))

    by_chip = [_odd_rows({"w_out_e": jnp.split(godd["w_out_e"], 4, axis=0)[j],
                          **{n: jnp.split(godd[n], 4, axis=1)[j] for n in ("w_in_e", "w_qb", "w_kvb")}}, BF16)
               for j in range(4)]
    godd_buf = jnp.stack(by_chip)
    theirs = _run_plan(_plan_pair_swap(godd_buf), name="odd_pair_swap")[0]
    pair = _add_pairs(godd_buf, theirs, ids, name="odd_pair_add")
    parts = _run_plan(_plan_chip_scatter(pair), name="odd_chip_scatter")[0]
    g_odd = _run_plan(_plan_pair_gather(_sum_chips(pair, parts, ids, name="odd_chip_sum")), name="odd_pair_gather")[0]
    g_odd = _odd_unrows(g_odd.reshape(ROWS_ODD, 1024))

    g_small = _unpack_small(_small_allreduce(_pack_small(gs)))
    g_gnorm = lax.dynamic_slice_in_dim(g_small["hg_gnorm"], chip * 256, 256, axis=1)

    grads = {n: g_small[n] for n, _ in SMALL if n != "hg_gnorm"}
    grads["hg_gnorm"] = g_gnorm

    delta, new_m, new_v = {}, {}, {}
    for n, bufs_, row0 in (("w_ff1", [g_l0, g_l1], 0), ("w_ff2", [g_l0, g_l1], 1024), ("w_in_o", [g_l1], 2048),
                           ("w_out_o", [g_l1], 3072)):
        grads[n], delta[n], new_m[n], new_v[n] = _adamw_rows(w[n], m[n], v[n], bufs_, row0, name=f"adamw_{n}")
    for n, _ in ODD_PARTS:
        grads[n] = g_odd[n][None]
        d_, m_, v_ = _adamw(w[n][0], g_odd[n], m[n][0], v[n][0], name=f"adamw_{n}")
        delta[n], new_m[n], new_v[n] = d_[None], m_[None], v_[None]
    rest = [n for n in WEIGHTS if n not in delta]

    def pack_rest(d):
        flat = jnp.concatenate([d[n].reshape(-1) for n in rest])
        return jnp.pad(flat, (0, SMALL_ROWS * 1024 - flat.shape[0])).reshape(SMALL_ROWS, 1024)

    outs = _adamw(pack_rest(w), pack_rest(grads), pack_rest(m), pack_rest(v), name="adamw_small")
    for dst, packed in zip((delta, new_m, new_v), outs):
        flat, off = packed.reshape(-1), 0
        for n in rest:
            size = math.prod(w[n].shape)
            dst[n] = flat[off:off + size].reshape(w[n].shape)
            off += size

    return (loss, grad_x[None], *[grads[n] for n in WEIGHTS], *[delta[n] for n in WEIGHTS],
            *[new_m[n] for n in WEIGHTS], *[new_v[n] for n in WEIGHTS])
```

```python
import functools
import math

import jax
import jax.numpy as jnp
from jax import lax
from jax.experimental import pallas as pl
from jax.experimental.pallas import tpu as pltpu

F32 = jnp.float32
BF16 = jnp.bfloat16
MESH_IDS = pl.DeviceIdType.MESH

D = 1024
DEPTH = 2
HEADS = 8
NOPE, ROPE, VDIM = 64, 32, 64
QK_SCALE = (NOPE + ROPE) ** -0.5
ROPE_BASE = 10000.0
SGU_G, SGU_C = 4, 128
HG_CHUNK = 64
HG_HEADS_PER_STEP = 2
ALPHA = (2 * DEPTH) ** 0.25
EPS = 1e-5
LR, B1, B2, ADAM_EPS, WD, STEP = 0.001, 0.9, 0.999, 1e-08, 0.01, 10
GELU_C = math.sqrt(2.0 / math.pi)
GELU_A = 0.044715
HI = lax.Precision.HIGHEST
MB = 1024 * 1024
ROW_BLOCK = 512

NT_DIMS = (((1,), (1,)), ((), ()))
TN_DIMS = (((0,), (0,)), ((), ()))

SHARDED = (
    ("w_in_e", (1, 1024, 392), 2), ("w_qb", (1, 256, 192), 2), ("w_kvb", (1, 256, 256), 2),
    ("w_out_e", (1, 256, 1024), 1), ("w_in_o", (1, 1024, 1024), 2), ("w_out_o", (1, 256, 1024), 1),
    ("w_ff1", (2, 1024, 1024), 2), ("w_ff2", (2, 1024, 1024), 1), ("hg_gnorm", (1, 256), 1),
)
PACK_ROWS = 6144
HALF_ROWS = PACK_ROWS // 2
SMALL = (("mla_gq", (1, 256)), ("mla_gkv", (1, 256)), ("sgu_ln_g", (1, 512)), ("sgu_ln_b", (1, 512)),
         ("sgu_w", (1, 4, 128, 128)), ("sgu_b", (1, 4, 128)), ("hg_lb", (2, 1024)), ("hg_gnorm", (1, 1024)),
         ("ln1_g", (2, 1024)), ("ln1_b", (2, 1024)), ("ln2_g", (2, 1024)), ("ln2_b", (2, 1024)))
SMALL_ROWS = 80


def _params(vmem_mb, n_axes=0):
    kw = dict(vmem_limit_bytes=vmem_mb * MB)
    if n_axes:
        kw["dimension_semantics"] = ("arbitrary",) * n_axes
    return pltpu.CompilerParams(**kw)


_ANY = pl.BlockSpec(memory_space=pl.ANY)


def _mesh_pos():
    return lax.axis_index("x"), lax.axis_index("y"), lax.axis_index("c")


class _Plan:
    def __init__(self, ins, outs, n_remote, n_local, start, wait, aliases=None):
        self.ins, self.outs, self.n_remote, self.n_local = list(ins), list(outs), n_remote, n_local
        self.start, self.wait, self.aliases = start, wait, dict(aliases or {})


def _join_plans(plans):
    ins, outs, aliases, parts = [], [], {}, []
    nr = nl = 0
    for p in plans:
        parts.append((p, len(ins), len(outs), nr, nl))
        aliases.update({len(ins) + i: len(outs) + o for i, o in p.aliases.items()})
        ins += p.ins
        outs += p.outs
        nr += p.n_remote
        nl += p.n_local

    def run(which):
        def go(in_refs, out_refs, send, recv, loc):
            for p, i0, o0, r0, l0 in parts:
                getattr(p, which)(in_refs[i0:i0 + len(p.ins)], out_refs[o0:o0 + len(p.outs)],
                                  lambda i, r0=r0: send(r0 + i), lambda i, r0=r0: recv(r0 + i),
                                  lambda i, l0=l0: loc(l0 + i))
        return go

    return _Plan(ins, outs, nr, nl, run("start"), run("wait"), aliases)


def _plan_io(plan, n_in, n_out):
    if plan is None:
        return [], [], [], [], {}
    sems = [pltpu.SemaphoreType.DMA((max(plan.n_remote, 1),)), pltpu.SemaphoreType.DMA((max(plan.n_remote, 1),)),
            pltpu.SemaphoreType.DMA((max(plan.n_local, 1),))]
    aliases = {n_in + i: n_out + o for i, o in plan.aliases.items()}
    return plan.ins, [_ANY] * len(plan.outs), plan.outs, sems, aliases


def _split_refs(refs, n_in, n_out, n_scr, plan):
    p_in, p_out = (len(plan.ins), len(plan.outs)) if plan is not None else (0, 0)
    refs = list(refs)
    ins, refs = refs[:n_in], refs[n_in:]
    pins, refs = refs[:p_in], refs[p_in:]
    outs, refs = refs[:n_out], refs[n_out:]
    pouts, refs = refs[:p_out], refs[p_out:]
    scr, psem = refs[:n_scr], refs[n_scr:]
    psem = tuple((lambda i, s=s: s.at[i]) for s in psem)
    return ins, outs, scr, (pins, pouts, psem)


def _grid_edge(grid, last):
    cond = None
    for ax, n in enumerate(grid):
        c = pl.program_id(ax) == (n - 1 if last else 0)
        cond = c if cond is None else cond & c
    return cond


def _plan_start(plan, pctx, grid):
    if plan is not None:
        pins, pouts, psem = pctx
        pl.when(_grid_edge(grid, False))(lambda: plan.start(pins, pouts, *psem))


def _plan_wait(plan, pctx, grid):
    if plan is not None:
        pins, pouts, psem = pctx
        pl.when(_grid_edge(grid, True))(lambda: plan.wait(pins, pouts, *psem))


def _run_plan(plan, *, name):
    def body(*refs):
        _, _, _, (pins, pouts, psem) = _split_refs(refs, 0, 0, 0, plan)
        plan.start(pins, pouts, *psem)
        plan.wait(pins, pouts, *psem)

    p_in, p_ospec, p_oshape, p_scr, p_alias = _plan_io(plan, 0, 0)
    return pl.pallas_call(body, name=name, in_specs=[_ANY] * len(p_in), out_specs=p_ospec, out_shape=p_oshape,
                          scratch_shapes=p_scr, input_output_aliases=p_alias)(*p_in)


def _fold8(x):
    return x.reshape(x.shape[0] // 8, 8, x.shape[1]).sum(axis=0)


def _ln_stats(r):
    mu = jnp.mean(r, -1, keepdims=True)
    xc = r - mu
    rstd = lax.rsqrt(jnp.mean(xc * xc, -1, keepdims=True) + EPS)
    return xc * rstd, rstd


def _sigmoid(x):
    return 1.0 / (1.0 + jnp.exp(-x))


def _gelu(x):
    return 0.5 * x * (1.0 + jnp.tanh(GELU_C * (x + GELU_A * x * x * x)))


def _gelu_grad(x):
    t = jnp.tanh(GELU_C * (x + GELU_A * x * x * x))
    return 0.5 * (1.0 + t) + 0.5 * x * (1.0 - t * t) * GELU_C * (1.0 + 3.0 * GELU_A * x * x)


def _matmul(a, b, *, name, M, N, K, ta=False, tb=False, out_dtype=F32, tm=512, tn=1024, tk=1024,
            a_spec=None, b_spec=None, b_merge=None, out_shape=None, o_spec=None, into=None,
            a_sq=False, mul=None, add=None, add_scale=1.0, plan=None):
    tm, tn, tk = min(tm, M), min(tn, N), min(tk, K)
    assert M % tm == 0 and N % tn == 0 and K % tk == 0
    grid = (M // tm, N // tn, K // tk)
    nk = grid[2]
    if a_spec is None:
        a_spec = pl.BlockSpec((tk, tm), lambda i, j, k: (k, i)) if ta else pl.BlockSpec((tm, tk), lambda i, j, k: (i, k))
    if b_spec is None:
        b_spec = pl.BlockSpec((tn, tk), lambda i, j, k: (j, k)) if tb else pl.BlockSpec((tk, tn), lambda i, j, k: (k, j))
    if o_spec is None:
        o_spec = pl.BlockSpec((tm, tn), lambda i, j, k: (i, j))
        out_shape = jax.ShapeDtypeStruct((M, N), out_dtype)
    e_spec = pl.BlockSpec((tm, tn), lambda i, j, k: (i, j))
    dims = (((0 if ta else 1,), (1 if tb else 0,)), ((), ()))
    extra = [e for e in (mul, add, into) if e is not None]
    n_in = 2 + len(extra)

    def body(*refs):
        ins, outs, scr, pctx = _split_refs(refs, n_in, 1, 1 if nk > 1 else 0, plan)
        a_ref, b_ref = ins[0], ins[1]
        rest = list(ins[2:])
        mul_ref = rest.pop(0) if mul is not None else None
        add_ref = rest.pop(0) if add is not None else None
        o_ref = outs[0]
        _plan_start(plan, pctx, grid)
        av = a_ref[...]
        if a_sq:
            av = av * av
        bv = b_ref[...]
        if b_merge is not None:
            bv = bv.reshape(b_merge)
        p = lax.dot_general(av, bv, dims, preferred_element_type=F32)

        def finish(r):
            if mul_ref is not None:
                r = r * (2.0 * mul_ref[...].astype(F32))
            if add_ref is not None:
                r = r + add_scale * add_ref[...]
            o_ref[...] = r.astype(o_ref.dtype)

        if nk == 1:
            finish(p)
        else:
            acc_ref = scr[0]
            k = pl.program_id(2)

            @pl.when(k == 0)
            def _():
                acc_ref[...] = p

            @pl.when(k > 0)
            def _():
                acc_ref[...] += p

            @pl.when(k == nk - 1)
            def _():
                finish(acc_ref[...])

        _plan_wait(plan, pctx, grid)

    p_in, p_ospec, p_oshape, p_scr, p_alias = _plan_io(plan, n_in, 1)
    aliases = dict(p_alias)
    if into is not None:
        aliases[n_in - 1] = 0
    return pl.pallas_call(
        body, name=name, grid=grid,
        in_specs=[a_spec, b_spec] + [e_spec] * (len(extra) - (into is not None)) + [_ANY] * (into is not None)
        + [_ANY] * len(p_in),
        out_specs=[o_spec] + p_ospec, out_shape=[out_shape] + p_oshape,
        scratch_shapes=([pltpu.VMEM((tm, tn), F32)] if nk > 1 else []) + p_scr,
        input_output_aliases=aliases, compiler_params=_params(48, 3),
    )(a, b, *extra, *p_in)


def _rows4_spec(rowblk, n_axes):
    return pl.BlockSpec((4, 256, D), lambda *_: (0, rowblk, 0))


def _proj_ln(a_b, w, h_prev, g, b, *, name, w_rowblk=None, plan=None):
    T = a_b.shape[0]
    tm = min(ROW_BLOCK, T)
    grid = (T // tm,)
    row = pl.BlockSpec((tm, D), lambda i: (i, 0))
    vec = pl.BlockSpec((1, D), lambda i: (0, 0))
    w_spec = pl.BlockSpec((D, D), lambda i: (0, 0)) if w_rowblk is None else _rows4_spec(w_rowblk, 1)

    def body(*refs):
        (a_ref, w_ref, h_ref, g_ref, b_ref), (r_ref, ho_ref, hb_ref), _, pctx = _split_refs(refs, 5, 3, 0, plan)
        _plan_start(plan, pctx, grid)
        mix = jnp.dot(a_ref[...], w_ref[...].reshape(D, D), preferred_element_type=F32)
        r = ALPHA * h_ref[...] + mix
        xhat, _ = _ln_stats(r)
        y = xhat * g_ref[...] + b_ref[...]
        r_ref[...] = r
        ho_ref[...] = y
        hb_ref[...] = y.astype(BF16)
        _plan_wait(plan, pctx, grid)

    p_in, p_ospec, p_oshape, p_scr, p_alias = _plan_io(plan, 5, 3)
    return pl.pallas_call(
        body, name=name, grid=grid,
        in_specs=[row, w_spec, row, vec, vec] + [_ANY] * len(p_in),
        out_specs=[row, row, row] + p_ospec,
        out_shape=[jax.ShapeDtypeStruct((T, D), F32), jax.ShapeDtypeStruct((T, D), F32),
                   jax.ShapeDtypeStruct((T, D), BF16)] + p_oshape,
        scratch_shapes=p_scr, input_output_aliases=p_alias, compiler_params=_params(40, 1),
    )(a_b, w, h_prev, g, b, *p_in)


def _ffn_ln(h_b, wbuf, h, g, b, *, name):
    T = h_b.shape[0]
    tm, tf = min(ROW_BLOCK, T), 1024
    nf = 4
    F = nf * tf
    row = pl.BlockSpec((tm, D), lambda i, j: (i, 0))
    vec = pl.BlockSpec((1, D), lambda i, j: (0, 0))

    def body(hb_ref, w1_ref, w2_ref, h_ref, g_ref, b_ref, ra_ref, r_ref, ho_ref, hbo_ref, acc_ref):
        j = pl.program_id(1)
        a = jnp.dot(hb_ref[...], w1_ref[...], preferred_element_type=F32)
        ra = jnp.maximum(a, 0.0)
        ra_ref[...] = ra.astype(BF16)
        p = jnp.dot((ra * ra).astype(BF16), w2_ref[...], preferred_element_type=F32)

        @pl.when(j == 0)
        def _():
            acc_ref[...] = p

        @pl.when(j > 0)
        def _():
            acc_ref[...] += p

        @pl.when(j == nf - 1)
        def _():
            r = ALPHA * h_ref[...] + acc_ref[...]
            xhat, _ = _ln_stats(r)
            y = xhat * g_ref[...] + b_ref[...]
            r_ref[...] = r
            ho_ref[...] = y
            hbo_ref[...] = y.astype(BF16)

    return pl.pallas_call(
        body, name=name, grid=(T // tm, nf),
        in_specs=[row, pl.BlockSpec((None, D, tf), lambda i, j: (j, 0, 0)),
                  pl.BlockSpec((None, tf, D), lambda i, j: (j, 1, 0)), row, vec, vec],
        out_specs=[pl.BlockSpec((tm, tf), lambda i, j: (i, j)), row, row, row],
        out_shape=[jax.ShapeDtypeStruct((T, F), BF16), jax.ShapeDtypeStruct((T, D), F32),
                   jax.ShapeDtypeStruct((T, D), F32), jax.ShapeDtypeStruct((T, D), BF16)],
        scratch_shapes=[pltpu.VMEM((tm, D), F32)],
        compiler_params=_params(48, 2),
    )(h_b, wbuf, wbuf, h, g, b)


def _loss_dy(y, tgt):
    T = y.shape[0]
    tm = min(ROW_BLOCK, T)
    row = pl.BlockSpec((tm, D), lambda i: (i, 0))

    def body(y_ref, t_ref, dy_ref, ls_ref):
        e = y_ref[...] - t_ref[...]
        dy_ref[...] = e * (1.0 / D)

        @pl.when(pl.program_id(0) == 0)
        def _():
            ls_ref[...] = jnp.zeros_like(ls_ref)

        ls_ref[...] += _fold8(e * e)

    return pl.pallas_call(
        body, name="loss_dy", grid=(T // tm,), in_specs=[row, row],
        out_specs=[row, pl.BlockSpec((8, D), lambda i: (0, 0))],
        out_shape=[jax.ShapeDtypeStruct((T, D), F32), jax.ShapeDtypeStruct((8, D), F32)],
        compiler_params=_params(32, 1),
    )(y, tgt)


def _ln_bwd(dy, r, g, *, name):
    T = dy.shape[0]
    tm = min(ROW_BLOCK, T)
    row = pl.BlockSpec((tm, D), lambda i: (i, 0))
    acc = pl.BlockSpec((8, D), lambda i: (0, 0))

    def body(dy_ref, r_ref, g_ref, dr_ref, drb_ref, dg_ref, db_ref):
        @pl.when(pl.program_id(0) == 0)
        def _():
            dg_ref[...] = jnp.zeros_like(dg_ref)
            db_ref[...] = jnp.zeros_like(db_ref)

        dy_ = dy_ref[...]
        xhat, rstd = _ln_stats(r_ref[...])
        dxh = dy_ * g_ref[...]
        m1 = jnp.mean(dxh, -1, keepdims=True)
        m2 = jnp.mean(dxh * xhat, -1, keepdims=True)
        dr = rstd * (dxh - m1 - xhat * m2)
        dr_ref[...] = dr
        drb_ref[...] = dr.astype(BF16)
        dg_ref[...] += _fold8(dy_ * xhat)
        db_ref[...] += _fold8(dy_)

    return pl.pallas_call(
        body, name=name, grid=(T // tm,),
        in_specs=[row, row, pl.BlockSpec((1, D), lambda i: (0, 0))],
        out_specs=[row, row, acc, acc],
        out_shape=[jax.ShapeDtypeStruct((T, D), F32), jax.ShapeDtypeStruct((T, D), BF16),
                   jax.ShapeDtypeStruct((8, D), F32), jax.ShapeDtypeStruct((8, D), F32)],
        compiler_params=_params(40, 1),
    )(dy, r, g)


def _rope(x, c, s1, s2):
    return x * c + pltpu.roll(x, 112, 1) * s1 + pltpu.roll(x, 16, 1) * s2


def _rope_t(dy, c, s1, s2):
    return dy * c + pltpu.roll(dy * s1, 16, 1) + pltpu.roll(dy * s2, 112, 1)


def _rms(x, g):
    rstd = lax.rsqrt(jnp.mean(x * x, -1, keepdims=True) + EPS)
    xhat = x * rstd
    return xhat * g, xhat, rstd


def _mla_prep(z0, gq, gkv, wq, wk, wv, rc, rs1, rs2):
    T = z0.shape[0]
    tm = min(ROW_BLOCK, T)
    HW = HEADS * 128

    def body(cq_ref, ckv_ref, kr_ref, gq_ref, gkv_ref, wq_ref, wk_ref, wv_ref, c_ref, s1_ref, s2_ref,
             q_ref, k_ref, v_ref):
        nq = _rms(cq_ref[...], gq_ref[...])[0].astype(BF16)
        nkv = _rms(ckv_ref[...], gkv_ref[...])[0].astype(BF16)
        q = jnp.dot(nq, wq_ref[...], preferred_element_type=F32)
        k = jnp.dot(nkv, wk_ref[...], preferred_element_type=F32)
        v = jnp.dot(nkv, wv_ref[...], preferred_element_type=F32)
        c, s1, s2 = c_ref[...], s1_ref[...], s2_ref[...]
        kr = _rope(pltpu.roll(kr_ref[...], 64, 1), c, s1, s2)
        for h in range(HEADS):
            sl = slice(h * 128, (h + 1) * 128)
            q_ref[:, sl] = (_rope(q[:, sl], c, s1, s2) * QK_SCALE).astype(BF16)
            k_ref[:, sl] = (k[:, sl] + kr).astype(BF16)
        v_ref[...] = v.astype(BF16)

    full = lambda shape: pl.BlockSpec(shape, lambda i: (0, 0))
    tab = pl.BlockSpec((tm, 128), lambda i: (i, 0))
    return pl.pallas_call(
        body, name="mla_prep", grid=(T // tm,),
        in_specs=[pl.BlockSpec((tm, 256), lambda i: (i, 0)), pl.BlockSpec((tm, 256), lambda i: (i, 1)),
                  pl.BlockSpec((tm, 128), lambda i: (i, 12)), full((1, 256)), full((1, 256)),
                  full((256, HW)), full((256, HW)), full((256, 512)), tab, tab, tab],
        out_specs=[pl.BlockSpec((tm, HW), lambda i: (i, 0)), pl.BlockSpec((tm, HW), lambda i: (i, 0)),
                   pl.BlockSpec((tm, 512), lambda i: (i, 0))],
        out_shape=[jax.ShapeDtypeStruct((T, HW), BF16), jax.ShapeDtypeStruct((T, HW), BF16),
                   jax.ShapeDtypeStruct((T, 512), BF16)],
        compiler_params=_params(40, 1),
    )(z0, z0, z0, gq, gkv, wq, wk, wv, rc, rs1, rs2)


def _flash_fwd(q, k, v, plan=None):
    T = q.shape[0]
    bq = min(2 * ROW_BLOCK, T)
    nq = T // bq
    grid = (4, nq, nq)

    def body(*refs):
        (q_ref, k_ref, v_ref), (o_ref, lse_ref), (m_sc, l_sc, acc_sc), pctx = _split_refs(refs, 3, 2, 3, plan)
        _plan_start(plan, pctx, grid)
        i, j = pl.program_id(1), pl.program_id(2)
        first = lax.broadcasted_iota(jnp.int32, (bq, 128), 1) < 64

        @pl.when(j == 0)
        def _():
            m_sc[...] = jnp.full_like(m_sc, -jnp.inf)
            l_sc[...] = jnp.zeros_like(l_sc)
            acc_sc[...] = jnp.zeros_like(acc_sc)

        def step(masked):
            vp = v_ref[...]
            acc = acc_sc[...]
            for h in range(2):
                sl = slice(h * 128, (h + 1) * 128)
                s = lax.dot_general(q_ref[:, sl], k_ref[:, sl], NT_DIMS, preferred_element_type=F32)
                if masked:
                    rows = lax.broadcasted_iota(jnp.int32, (bq, bq), 0)
                    cols = lax.broadcasted_iota(jnp.int32, (bq, bq), 1)
                    s = jnp.where(cols <= rows, s, -jnp.inf)
                m_prev = m_sc[h, :, 0:1]
                m_new = jnp.maximum(m_prev, jnp.max(s, axis=1, keepdims=True))
                alpha = jnp.exp(m_prev - m_new)
                p = jnp.exp(s - m_new)
                l_new = alpha * l_sc[h, :, 0:1] + jnp.sum(p, axis=1, keepdims=True)
                pv = jnp.dot(p.astype(BF16), vp, preferred_element_type=F32)
                mine = first if h == 0 else jnp.logical_not(first)
                acc = jnp.where(mine, acc * alpha + pv, acc)
                m_sc[h] = jnp.broadcast_to(m_new, (bq, 128))
                l_sc[h] = jnp.broadcast_to(l_new, (bq, 128))
            acc_sc[...] = acc

        @pl.when(j < i)
        def _():
            step(False)

        @pl.when(j == i)
        def _():
            step(True)
            l0, l1 = l_sc[0], l_sc[1]
            o_ref[...] = (acc_sc[...] / jnp.where(first, l0, l1)).astype(BF16)
            lse_ref[...] = jnp.where(first, m_sc[0] + jnp.log(l0), m_sc[1] + jnp.log(l1))

        _plan_wait(plan, pctx, grid)

    kv = lambda hp, i, j: (jnp.minimum(i, j), hp)
    p_in, p_ospec, p_oshape, p_scr, p_alias = _plan_io(plan, 3, 2)
    return pl.pallas_call(
        body, name="flash_fwd", grid=grid,
        in_specs=[pl.BlockSpec((bq, 256), lambda hp, i, j: (i, hp)), pl.BlockSpec((bq, 256), kv),
                  pl.BlockSpec((bq, 128), kv)] + [_ANY] * len(p_in),
        out_specs=[pl.BlockSpec((bq, 128), lambda hp, i, j: (i, hp)),
                   pl.BlockSpec((bq, 128), lambda hp, i, j: (i, hp))] + p_ospec,
        out_shape=[jax.ShapeDtypeStruct((T, 512), BF16), jax.ShapeDtypeStruct((T, 512), F32)] + p_oshape,
        scratch_shapes=[pltpu.VMEM((2, bq, 128), F32), pltpu.VMEM((2, bq, 128), F32), pltpu.VMEM((bq, 128), F32)] + p_scr,
        input_output_aliases=p_alias, compiler_params=_params(56, 3),
    )(q, k, v, *p_in)


def _attn_delta(dmix, o):
    T = o.shape[0]
    tm = min(ROW_BLOCK, T)
    blk = pl.BlockSpec((tm, 512), lambda i: (i, 0))

    def body(do_ref, o_ref, delta_ref, dob_ref):
        first = lax.broadcasted_iota(jnp.int32, (tm, 128), 1) < 64
        for hp in range(4):
            sl = slice(hp * 128, (hp + 1) * 128)
            prod = do_ref[:, sl] * o_ref[:, sl].astype(F32)
            d0 = jnp.sum(jnp.where(first, prod, 0.0), axis=1, keepdims=True)
            d1 = jnp.sum(jnp.where(first, 0.0, prod), axis=1, keepdims=True)
            delta_ref[:, sl] = jnp.where(first, d0, d1)
        dob_ref[...] = do_ref[...].astype(BF16)

    return pl.pallas_call(
        body, name="attn_delta", grid=(T // tm,), in_specs=[blk, blk], out_specs=[blk, blk],
        out_shape=[jax.ShapeDtypeStruct((T, 512), F32), jax.ShapeDtypeStruct((T, 512), BF16)],
        compiler_params=_params(32, 1),
    )(dmix, o)


def _flash_bwd(q, k, v, do_b, lse, delta, plan=None):
    T = q.shape[0]
    bq = min(ROW_BLOCK, T)
    nq = T // bq
    grid = (4, nq, nq)

    def body(*refs):
        ((q_ref, k_ref, v_ref, do_ref, lse_ref, dl_ref), (dq_hbm, dk_ref, dv_ref), (dq_sc, dk_sc, dv_sc, sem),
         pctx) = _split_refs(refs, 6, 3, 4, plan)
        _plan_start(plan, pctx, grid)
        hp, j, i = pl.program_id(0), pl.program_id(1), pl.program_id(2)
        first = lax.broadcasted_iota(jnp.int32, (bq, 128), 1) < 64

        @pl.when((j == 0) & (i == 0))
        def _():
            dq_sc[...] = jnp.zeros_like(dq_sc)

        @pl.when(i == j)
        def _():
            dk_sc[...] = jnp.zeros_like(dk_sc)
            dv_sc[...] = jnp.zeros_like(dv_sc)

        def step(masked):
            vp = v_ref[...]
            do = do_ref[...]
            for h in range(2):
                sl = slice(h * 128, (h + 1) * 128)
                qh, kh = q_ref[:, sl], k_ref[:, sl]
                s = lax.dot_general(qh, kh, NT_DIMS, preferred_element_type=F32)
                p = jnp.exp(s - lse_ref[:, h * 64:h * 64 + 1])
                if masked:
                    rows = lax.broadcasted_iota(jnp.int32, (bq, bq), 0)
                    cols = lax.broadcasted_iota(jnp.int32, (bq, bq), 1)
                    p = jnp.where(cols <= rows, p, 0.0)
                mine = first if h == 0 else jnp.logical_not(first)
                do_h = jnp.where(mine, do, jnp.zeros_like(do))
                dv_sc[...] += lax.dot_general(p.astype(BF16), do_h, TN_DIMS, preferred_element_type=F32)
                dp = lax.dot_general(do_h, vp, NT_DIMS, preferred_element_type=F32)
                ds = (p * (dp - dl_ref[:, h * 64:h * 64 + 1])).astype(BF16)
                dq_sc[i, :, sl] += jnp.dot(ds, kh, preferred_element_type=F32)
                dk_sc[:, sl] += lax.dot_general(ds, qh, TN_DIMS, preferred_element_type=F32)

        @pl.when(i > j)
        def _():
            step(False)

        @pl.when(i == j)
        def _():
            step(True)

        @pl.when(i == nq - 1)
        def _():
            dk_ref[...] = dk_sc[...]
            dv_ref[...] = dv_sc[...]

        @pl.when((j == nq - 1) & (i == nq - 1))
        def _():
            cp = pltpu.make_async_copy(dq_sc, dq_hbm.at[hp], sem)
            cp.start()
            cp.wait()

        _plan_wait(plan, pctx, grid)

    qi = lambda hp, j, i: (jnp.maximum(i, j), hp)
    kj = lambda hp, j, i: (j, hp)
    p_in, p_ospec, p_oshape, p_scr, p_alias = _plan_io(plan, 6, 3)
    return pl.pallas_call(
        body, name="flash_bwd", grid=grid,
        in_specs=[pl.BlockSpec((bq, 256), qi), pl.BlockSpec((bq, 256), kj), pl.BlockSpec((bq, 128), kj),
                  pl.BlockSpec((bq, 128), qi), pl.BlockSpec((bq, 128), qi), pl.BlockSpec((bq, 128), qi)]
        + [_ANY] * len(p_in),
        out_specs=[_ANY, pl.BlockSpec((bq, 256), kj), pl.BlockSpec((bq, 128), kj)] + p_ospec,
        out_shape=[jax.ShapeDtypeStruct((4, nq, bq, 256), F32), jax.ShapeDtypeStruct((T, 1024), F32),
                   jax.ShapeDtypeStruct((T, 512), F32)] + p_oshape,
        scratch_shapes=[pltpu.VMEM((nq, bq, 256), F32), pltpu.VMEM((bq, 256), F32), pltpu.VMEM((bq, 128), F32),
                        pltpu.SemaphoreType.DMA] + p_scr,
        input_output_aliases=p_alias, compiler_params=_params(40, 3),
    )(q, k, v, do_b, lse, delta, *p_in)


def _mla_bwd(z0, dq4, dk, dv, gq, gkv, wq, wk, wv, rc, rs1, rs2, plan=None):
    T = z0.shape[0]
    tm = dq4.shape[2]
    HW = HEADS * 128
    grid = (T // tm,)

    def body(*refs):
        ((cq_ref, ckv_ref, dq_ref, dk_ref, dv_ref, gq_ref, gkv_ref, wq_ref, wk_ref, wv_ref, c_ref, s1_ref, s2_ref),
         (dc_ref, dkr_ref, dwq_ref, dwk_ref, dwv_ref, dgq_ref, dgkv_ref), _, pctx) = _split_refs(refs, 13, 7, 0, plan)
        _plan_start(plan, pctx, grid)

        @pl.when(pl.program_id(0) == 0)
        def _():
            for ref in (dwq_ref, dwk_ref, dwv_ref, dgq_ref, dgkv_ref):
                ref[...] = jnp.zeros_like(ref)

        c, s1, s2 = c_ref[...], s1_ref[...], s2_ref[...]
        lane = lax.broadcasted_iota(jnp.int32, (tm, 128), 1)
        nq, xq, rq = _rms(cq_ref[...], gq_ref[...])
        nkv, xkv, rkv = _rms(ckv_ref[...], gkv_ref[...])
        nq_b, nkv_b = nq.astype(BF16), nkv.astype(BF16)

        dq_parts, dk_parts = [], []
        dkr = jnp.zeros((tm, 128), F32)
        for h in range(HEADS):
            blk = dq_ref[h // 2, :, (h % 2) * 128:(h % 2 + 1) * 128] * QK_SCALE
            dq_parts.append(_rope_t(blk, c, s1, s2).astype(BF16))
            kb = dk_ref[:, h * 128:(h + 1) * 128]
            dk_parts.append(jnp.where(lane < NOPE, kb, 0.0).astype(BF16))
            dkr = dkr + kb
        dq_b = jnp.concatenate(dq_parts, axis=1)
        dk_b = jnp.concatenate(dk_parts, axis=1)
        dv_b = dv_ref[...].astype(BF16)

        dwq_ref[...] += lax.dot_general(nq_b, dq_b, TN_DIMS, preferred_element_type=F32)
        dwk_ref[...] += lax.dot_general(nkv_b, dk_b, TN_DIMS, preferred_element_type=F32)
        dwv_ref[...] += lax.dot_general(nkv_b, dv_b, TN_DIMS, preferred_element_type=F32)
        dnq = lax.dot_general(dq_b, wq_ref[...], NT_DIMS, preferred_element_type=F32)
        dnkv = (lax.dot_general(dk_b, wk_ref[...], NT_DIMS, preferred_element_type=F32)
                + lax.dot_general(dv_b, wv_ref[...], NT_DIMS, preferred_element_type=F32))

        def rms_bwd(dn, xhat, rstd, g):
            dxh = dn * g
            return rstd * (dxh - xhat * jnp.mean(dxh * xhat, -1, keepdims=True))

        dc_ref[:, :256] = rms_bwd(dnq, xq, rq, gq_ref[...]).astype(BF16)
        dc_ref[:, 256:] = rms_bwd(dnkv, xkv, rkv, gkv_ref[...]).astype(BF16)
        dgq_ref[...] += _fold8(dnq * xq)
        dgkv_ref[...] += _fold8(dnkv * xkv)
        dkr = pltpu.roll(_rope_t(dkr, c, s1, s2), 64, 1)
        dkr_ref[...] = jnp.where(lane < ROPE, dkr, 0.0).astype(BF16)
        _plan_wait(plan, pctx, grid)

    full = lambda shape: pl.BlockSpec(shape, lambda i: (0,) * len(shape))
    tab = pl.BlockSpec((tm, 128), lambda i: (i, 0))
    p_in, p_ospec, p_oshape, p_scr, p_alias = _plan_io(plan, 13, 7)
    return pl.pallas_call(
        body, name="mla_bwd", grid=grid,
        in_specs=[pl.BlockSpec((tm, 256), lambda i: (i, 0)), pl.BlockSpec((tm, 256), lambda i: (i, 1)),
                  pl.BlockSpec((4, None, tm, 256), lambda i: (0, i, 0, 0)),
                  pl.BlockSpec((tm, HW), lambda i: (i, 0)), pl.BlockSpec((tm, 512), lambda i: (i, 0)),
                  full((1, 256)), full((1, 256)), full((256, HW)), full((256, HW)), full((256, 512)), tab, tab, tab]
        + [_ANY] * len(p_in),
        out_specs=[pl.BlockSpec((tm, 512), lambda i: (i, 0)), tab, full((256, HW)), full((256, HW)),
                   full((256, 512)), full((8, 256)), full((8, 256))] + p_ospec,
        out_shape=[jax.ShapeDtypeStruct((T, 512), BF16), jax.ShapeDtypeStruct((T, 128), BF16),
                   jax.ShapeDtypeStruct((256, HW), F32), jax.ShapeDtypeStruct((256, HW), F32),
                   jax.ShapeDtypeStruct((256, 512), F32), jax.ShapeDtypeStruct((8, 256), F32),
                   jax.ShapeDtypeStruct((8, 256), F32)] + p_oshape,
        scratch_shapes=p_scr, input_output_aliases=p_alias, compiler_params=_params(48, 1),
    )(z0, z0, dq4, dk, dv, gq, gkv, wq, wk, wv, rc, rs1, rs2, *p_in)


def _sgu_fwd(z0, a_out, ln_g, ln_b, w, b_t):
    T = z0.shape[0]
    tm = min(ROW_BLOCK, T)
    W = SGU_G * SGU_C

    def body(u_ref, v_ref, a_ref, g_ref, b_ref, w_ref, bt_ref, o_ref):
        o_ref[:, :W] = a_ref[...]
        ug = _gelu(u_ref[...])
        xhat, _ = _ln_stats(_gelu(v_ref[...]))
        vn = (xhat * g_ref[...] + b_ref[...]).astype(BF16)
        tril = lax.broadcasted_iota(jnp.int32, (SGU_C, SGU_C), 0) >= lax.broadcasted_iota(jnp.int32, (SGU_C, SGU_C), 1)
        for g in range(SGU_G):
            cs = slice(g * SGU_C, (g + 1) * SGU_C)
            wg = jnp.where(tril, w_ref[g], 0.0).astype(BF16)
            bcol = bt_ref[:, g:g + 1]
            for c in range(tm // SGU_C):
                rs = slice(c * SGU_C, (c + 1) * SGU_C)
                mixed = jnp.dot(wg, vn[rs, cs], preferred_element_type=F32) + bcol
                o_ref[rs, W + g * SGU_C:W + (g + 1) * SGU_C] = (ug[rs, cs] * mixed).astype(BF16)

    full = lambda shape: pl.BlockSpec(shape, lambda i: (0,) * len(shape))
    return pl.pallas_call(
        body, name="sgu_fwd", grid=(T // tm,),
        in_specs=[pl.BlockSpec((tm, W), lambda i: (i, 1)), pl.BlockSpec((tm, W), lambda i: (i, 2)),
                  pl.BlockSpec((tm, W), lambda i: (i, 0)),
                  full((1, W)), full((1, W)), full((SGU_G, SGU_C, SGU_C)), full((SGU_C, SGU_G))],
        out_specs=pl.BlockSpec((tm, 2 * W), lambda i: (i, 0)),
        out_shape=jax.ShapeDtypeStruct((T, 2 * W), BF16),
        compiler_params=_params(32, 1),
    )(z0, z0, a_out, ln_g, ln_b, w, b_t)


def _sgu_bwd(z0, dmix, dc, dkr, ln_g, ln_b, w, b_t):
    T = z0.shape[0]
    tm = min(ROW_BLOCK, T)
    W = SGU_G * SGU_C

    def body(u_ref, v_ref, do_ref, dc_ref, dkr_ref, g_ref, b_ref, w_ref, bt_ref, dz_ref, dw_ref, db_ref, dlg_ref,
             dlb_ref):
        @pl.when(pl.program_id(0) == 0)
        def _():
            for ref in (dw_ref, db_ref, dlg_ref, dlb_ref):
                ref[...] = jnp.zeros_like(ref)

        dz_ref[:, :W] = dc_ref[...]
        dz_ref[:, 3 * W:] = dkr_ref[...]

        u, v, dout = u_ref[...], v_ref[...], do_ref[...]
        ug = _gelu(u)
        xhat, rstd = _ln_stats(_gelu(v))
        vn = (xhat * g_ref[...] + b_ref[...]).astype(BF16)
        dmixed = dout * ug
        dmixed_b = dmixed.astype(BF16)
        tril = lax.broadcasted_iota(jnp.int32, (SGU_C, SGU_C), 0) >= lax.broadcasted_iota(jnp.int32, (SGU_C, SGU_C), 1)
        lane = lax.broadcasted_iota(jnp.int32, (SGU_C, SGU_C), 1)
        dvn_cols = []
        for g in range(SGU_G):
            cs = slice(g * SGU_C, (g + 1) * SGU_C)
            wg = jnp.where(tril, w_ref[g], 0.0).astype(BF16)
            bcol = bt_ref[:, g:g + 1]
            dw_g = jnp.zeros((SGU_C, SGU_C), F32)
            db_g = jnp.zeros((SGU_C, 1), F32)
            dvn_rows = []
            for c in range(tm // SGU_C):
                rs = slice(c * SGU_C, (c + 1) * SGU_C)
                mixed = jnp.dot(wg, vn[rs, cs], preferred_element_type=F32) + bcol
                dz_ref[rs, W + g * SGU_C:W + (g + 1) * SGU_C] = (dout[rs, cs] * mixed * _gelu_grad(u[rs, cs])).astype(BF16)
                dm = dmixed_b[rs, cs]
                dw_g = dw_g + lax.dot_general(dm, vn[rs, cs], NT_DIMS, preferred_element_type=F32)
                db_g = db_g + jnp.sum(dmixed[rs, cs], axis=1, keepdims=True)
                dvn_rows.append(lax.dot_general(wg, dm, TN_DIMS, preferred_element_type=F32))
            dw_ref[g] += jnp.where(tril, dw_g, 0.0)
            db_ref[...] += jnp.where(lane == g, db_g, 0.0)
            dvn_cols.append(jnp.concatenate(dvn_rows, axis=0))
        dvn = jnp.concatenate(dvn_cols, axis=1)
        dxh = dvn * g_ref[...]
        m1 = jnp.mean(dxh, -1, keepdims=True)
        m2 = jnp.mean(dxh * xhat, -1, keepdims=True)
        dvg = rstd * (dxh - m1 - xhat * m2)
        dz_ref[:, 2 * W:3 * W] = (dvg * _gelu_grad(v)).astype(BF16)
        dlg_ref[...] += _fold8(dvn * xhat)
        dlb_ref[...] += _fold8(dvn)

    full = lambda shape: pl.BlockSpec(shape, lambda i: (0,) * len(shape))
    return pl.pallas_call(
        body, name="sgu_bwd", grid=(T // tm,),
        in_specs=[pl.BlockSpec((tm, W), lambda i: (i, 1)), pl.BlockSpec((tm, W), lambda i: (i, 2)),
                  pl.BlockSpec((tm, W), lambda i: (i, 1)), pl.BlockSpec((tm, W), lambda i: (i, 0)),
                  pl.BlockSpec((tm, 128), lambda i: (i, 0)),
                  full((1, W)), full((1, W)), full((SGU_G, SGU_C, SGU_C)), full((SGU_C, SGU_G))],
        out_specs=[pl.BlockSpec((tm, 3 * W + 128), lambda i: (i, 0)), full((SGU_G, SGU_C, SGU_C)),
                   full((SGU_C, SGU_C)), full((8, W)), full((8, W))],
        out_shape=[jax.ShapeDtypeStruct((T, 3 * W + 128), BF16), jax.ShapeDtypeStruct((SGU_G, SGU_C, SGU_C), F32),
                   jax.ShapeDtypeStruct((SGU_C, SGU_C), F32), jax.ShapeDtypeStruct((8, W), F32),
                   jax.ShapeDtypeStruct((8, W), F32)],
        compiler_params=_params(40, 1),
    )(z0, z0, dmix, dc, dkr, ln_g, ln_b, w, b_t)


def _hg_lower_bound(lb_ref):
    a0, a1 = lb_ref[0:1, :], lb_ref[1:2, :]
    m = jnp.maximum(a0, a1)
    e0, e1 = jnp.exp(a0 - m), jnp.exp(a1 - m)
    return e1 / (e0 + e1)


def _hg_chunk(qc, fc, lb):
    C = HG_CHUNK
    rows = lax.broadcasted_iota(jnp.int32, (C, C), 0)
    cols = lax.broadcasted_iota(jnp.int32, (C, C), 1)
    rowid = lax.broadcasted_iota(jnp.int32, (C, 128), 0)
    sq, sg = _sigmoid(qc), _sigmoid(fc)
    qf = qc * sq
    gate = lb + (1.0 - lb) * sg
    kk = 1.0 - gate
    lg = jnp.log(gate)
    bcum = jnp.dot((rows >= cols).astype(F32), lg, precision=HI, preferred_element_type=F32)
    b_mid = jnp.sum(jnp.where(rowid < C // 2, lg, 0.0), axis=0, keepdims=True)
    b_last = jnp.sum(lg, axis=0, keepdims=True)
    eq, ek, e, eh = jnp.exp(bcum - b_mid), jnp.exp(b_mid - bcum), jnp.exp(bcum), jnp.exp(b_last - bcum)
    qt, kt, qe, khat = qf * eq, kk * ek, qf * e, kk * eh
    a = lax.dot_general(qt.astype(BF16), kt.astype(BF16), NT_DIMS, preferred_element_type=F32)
    a = jnp.where(rows >= cols, a, 0.0)
    return dict(sq=sq, sg=sg, gate=gate, kk=kk, eq=eq, ek=ek, e=e, eh=eh, qt=qt, kt=kt, qe=qe, khat=khat, a=a,
                e_last=jnp.exp(b_last), tril=rows >= cols, rowid=rowid)


def _hgrn_fwd(z4, hg_lb, gnorm):
    T = z4.shape[1]
    tb = min(ROW_BLOCK, T)
    C = HG_CHUNK
    ncb = tb // C
    HPB = HG_HEADS_PER_STEP

    def body(q_ref, f_ref, i_ref, g_ref, lb_ref, gn_ref, y_ref, o_ref, st_ref, st_sc):
        @pl.when(pl.program_id(1) == 0)
        def _():
            st_sc[...] = jnp.zeros_like(st_sc)

        def chunk(c, carry):
            rs = pl.ds(pl.multiple_of(c * C, C), C)
            for hh in range(HPB):
                hs = slice(hh * 128, (hh + 1) * 128)
                lb = _hg_lower_bound(lb_ref.at[:, hs])
                v_b = i_ref[rs, hs].astype(BF16)
                gc = g_ref[rs, hs]
                x = _hg_chunk(q_ref[rs, hs], f_ref[rs, hs], lb)
                st = st_sc[hh]
                st_ref[hh, c] = st
                o = (jnp.dot(x["a"].astype(BF16), v_b, preferred_element_type=F32)
                     + lax.dot_general(x["qe"].astype(BF16), st.astype(BF16), NT_DIMS, preferred_element_type=F32))
                st_sc[hh] = st * x["e_last"] + lax.dot_general(v_b, x["khat"].astype(BF16), TN_DIMS,
                                                               preferred_element_type=F32)
                o_ref[rs, hs] = o
                n = o * lax.rsqrt(jnp.mean(o * o, -1, keepdims=True) + EPS)
                y_ref[rs, hs] = (n * gn_ref[:, hs] * (gc * _sigmoid(gc))).astype(BF16)
            return carry

        lax.fori_loop(0, ncb, chunk, 0)

    W = 128 * HPB
    zb = lambda k: pl.BlockSpec((None, tb, W), lambda h, t: (k, t, h))
    out = pl.BlockSpec((tb, W), lambda h, t: (t, h))
    return pl.pallas_call(
        body, name="hgrn_fwd", grid=(HEADS // HPB, T // tb),
        in_specs=[zb(0), zb(1), zb(2), zb(3), pl.BlockSpec((2, W), lambda h, t: (0, h)),
                  pl.BlockSpec((1, W), lambda h, t: (0, h))],
        out_specs=[out, out, pl.BlockSpec((HPB, ncb, 128, 128), lambda h, t: (h, t, 0, 0))],
        out_shape=[jax.ShapeDtypeStruct((T, D), BF16), jax.ShapeDtypeStruct((T, D), F32),
                   jax.ShapeDtypeStruct((HEADS, T // C, 128, 128), F32)],
        scratch_shapes=[pltpu.VMEM((HPB, 128, 128), F32)],
        compiler_params=_params(32, 2),
    )(z4, z4, z4, z4, hg_lb, gnorm)


def _hgrn_bwd(z4, o_raw, dy, states, hg_lb, gnorm):
    T = z4.shape[1]
    tb = min(ROW_BLOCK, T)
    C = HG_CHUNK
    ncb = tb // C
    nt = T // tb
    HPB = HG_HEADS_PER_STEP

    def body(q_ref, f_ref, i_ref, g_ref, o_ref, dy_ref, st_ref, lb_ref, gn_ref, dz_ref, dlb_ref, dgn_ref, dst_sc):
        @pl.when(pl.program_id(1) == 0)
        def _():
            dst_sc[...] = jnp.zeros_like(dst_sc)
            dlb_ref[...] = jnp.zeros_like(dlb_ref)
            dgn_ref[...] = jnp.zeros_like(dgn_ref)

        def chunk(cc, carry):
            for hh in range(HPB):
                one_head(ncb - 1 - cc, hh, slice(hh * 128, (hh + 1) * 128))
            return carry

        def one_head(c, hh, hs):
            rs = pl.ds(pl.multiple_of(c * C, C), C)
            lb = _hg_lower_bound(lb_ref.at[:, hs])
            gn = gn_ref[:, hs]
            qc, gc = q_ref[rs, hs], g_ref[rs, hs]
            v_b = i_ref[rs, hs].astype(BF16)
            x = _hg_chunk(qc, f_ref[rs, hs], lb)
            st, dst = st_ref[hh, c], dst_sc[hh]
            st_b, dst_b = st.astype(BF16), dst.astype(BF16)
            o, dyc = o_ref[rs, hs], dy_ref[rs, hs]
            sgg = _sigmoid(gc)
            sil = gc * sgg
            rstd = lax.rsqrt(jnp.mean(o * o, -1, keepdims=True) + EPS)
            n = o * rstd
            dgn_ref[:, hs] += _fold8(dyc * n * sil)
            dn = dyc * gn * sil
            do = rstd * (dn - n * jnp.mean(dn * n, -1, keepdims=True))
            dg = dyc * n * gn * (sgg * (1.0 + gc * (1.0 - sgg)))
            do_b = do.astype(BF16)
            da = jnp.where(x["tril"], lax.dot_general(do_b, v_b, NT_DIMS, preferred_element_type=F32), 0.0).astype(BF16)
            qt_b, kt_b, qe_b, khat_b = (x[n_].astype(BF16) for n_ in ("qt", "kt", "qe", "khat"))
            dv = (lax.dot_general(x["a"].astype(BF16), do_b, TN_DIMS, preferred_element_type=F32)
                  + lax.dot_general(khat_b, dst_b, NT_DIMS, preferred_element_type=F32))
            dqt = jnp.dot(da, kt_b, preferred_element_type=F32)
            dqe = jnp.dot(do_b, st_b, preferred_element_type=F32)
            dkt = lax.dot_general(da, qt_b, TN_DIMS, preferred_element_type=F32)
            dkhat = jnp.dot(v_b, dst_b, preferred_element_type=F32)
            dst_sc[hh] = lax.dot_general(do_b, qe_b, TN_DIMS, preferred_element_type=F32) + dst * x["e_last"]
            de_last = jnp.sum(st * dst, axis=0, keepdims=True)
            dqf = dqt * x["eq"] + dqe * x["e"]
            dkk = dkt * x["ek"] + dkhat * x["eh"]
            dkh_kh = dkhat * x["khat"]
            db = dqt * qt_b.astype(F32) - dkt * kt_b.astype(F32) + dqe * x["qe"] - dkh_kh
            db_last = jnp.sum(dkh_kh, axis=0, keepdims=True) + de_last * x["e_last"]
            db = db + jnp.where(x["rowid"] == C - 1, db_last, 0.0)
            rows = lax.broadcasted_iota(jnp.int32, (C, C), 0)
            cols = lax.broadcasted_iota(jnp.int32, (C, C), 1)
            dlg = jnp.dot((rows <= cols).astype(F32), db, precision=HI, preferred_element_type=F32)
            dgate = dlg / x["gate"] - dkk
            sg, sq = x["sg"], x["sq"]
            dlb_ref[:, hs] += _fold8(dgate * (1.0 - sg)) * (lb * (1.0 - lb))
            dz_ref[0, rs, hs] = (dqf * (sq * (1.0 + qc * (1.0 - sq)))).astype(BF16)
            dz_ref[1, rs, hs] = (dgate * (1.0 - lb) * sg * (1.0 - sg)).astype(BF16)
            dz_ref[2, rs, hs] = dv.astype(BF16)
            dz_ref[3, rs, hs] = dg.astype(BF16)

        lax.fori_loop(0, ncb, chunk, 0)

    W = 128 * HPB
    zb = lambda k: pl.BlockSpec((None, tb, W), lambda h, t: (k, nt - 1 - t, h))
    blk = pl.BlockSpec((tb, W), lambda h, t: (nt - 1 - t, h))
    acc = pl.BlockSpec((8, W), lambda h, t: (0, h))
    return pl.pallas_call(
        body, name="hgrn_bwd", grid=(HEADS // HPB, nt),
        in_specs=[zb(0), zb(1), zb(2), zb(3), blk, blk,
                  pl.BlockSpec((HPB, ncb, 128, 128), lambda h, t: (h, nt - 1 - t, 0, 0)),
                  pl.BlockSpec((2, W), lambda h, t: (0, h)), pl.BlockSpec((1, W), lambda h, t: (0, h))],
        out_specs=[pl.BlockSpec((4, tb, W), lambda h, t: (0, nt - 1 - t, h)), acc, acc],
        out_shape=[jax.ShapeDtypeStruct((4, T, D), BF16), jax.ShapeDtypeStruct((8, D), F32),
                   jax.ShapeDtypeStruct((8, D), F32)],
        scratch_shapes=[pltpu.VMEM((HPB, 128, 128), F32)],
        compiler_params=_params(32, 2),
    )(z4, z4, z4, z4, o_raw, dy, states, hg_lb, gnorm)


def _adamw(w, g, m, v, *, name):
    R, L = w.shape
    tr = R if R <= 512 else 512
    assert R % tr == 0
    blk = pl.BlockSpec((tr, L), lambda i: (i, 0))
    c1, c2 = 1.0 - B1 ** STEP, 1.0 - B2 ** STEP

    def body(w_ref, g_ref, m_ref, v_ref, d_ref, mo_ref, vo_ref):
        g_ = g_ref[...]
        m_ = B1 * m_ref[...] + (1.0 - B1) * g_
        v_ = B2 * v_ref[...] + (1.0 - B2) * (g_ * g_)
        d_ref[...] = -LR * ((m_ / c1) / (jnp.sqrt(v_ / c2) + ADAM_EPS) + WD * w_ref[...])
        mo_ref[...] = m_
        vo_ref[...] = v_

    sds = jax.ShapeDtypeStruct((R, L), F32)
    return pl.pallas_call(
        body, name=name, grid=(R // tr,), in_specs=[blk] * 4, out_specs=[blk] * 3, out_shape=[sds] * 3,
        compiler_params=_params(32, 1),
    )(w, g, m, v)


def _adamw_rows(w, m, v, gbufs, row0, *, name):
    L, R, C = w.shape
    tr = 256
    assert R % tr == 0 and row0 % tr == 0 and len(gbufs) == L
    blk = pl.BlockSpec((None, tr, C), lambda l, i: (l, i, 0))
    gblk = pl.BlockSpec((tr, C), lambda l, i: (row0 // tr + i, 0))
    c1, c2 = 1.0 - B1 ** STEP, 1.0 - B2 ** STEP

    def body(*refs):
        w_ref, m_ref, v_ref = refs[:3]
        g_refs = refs[3:3 + L]
        go_ref, d_ref, mo_ref, vo_ref = refs[3 + L:]
        g_ = g_refs[0][...]
        for l in range(1, L):
            g_ = jnp.where(pl.program_id(0) == l, g_refs[l][...], g_)
        m_ = B1 * m_ref[...] + (1.0 - B1) * g_
        v_ = B2 * v_ref[...] + (1.0 - B2) * (g_ * g_)
        go_ref[...] = g_
        d_ref[...] = -LR * ((m_ / c1) / (jnp.sqrt(v_ / c2) + ADAM_EPS) + WD * w_ref[...])
        mo_ref[...] = m_
        vo_ref[...] = v_

    sds = jax.ShapeDtypeStruct((L, R, C), F32)
    return pl.pallas_call(
        body, name=name, grid=(L, R // tr), in_specs=[blk] * 3 + [gblk] * L, out_specs=[blk] * 4, out_shape=[sds] * 4,
        compiler_params=_params(32, 2),
    )(w, m, v, *gbufs)


def _add_pairs(g, theirs, ids, *, name):
    n, R, L = theirs.shape
    tr = 128
    nb = R // tr

    def body(ids_ref, a_ref, b_ref, o_ref):
        o_ref[...] = (a_ref[...].astype(F32) + b_ref[...].astype(F32)).astype(BF16)

    blk = pl.BlockSpec((n, tr, L), lambda i, ids: (0, i, 0))
    return pl.pallas_call(
        body, name=name, out_shape=jax.ShapeDtypeStruct((n, R, L), BF16),
        grid_spec=pltpu.PrefetchScalarGridSpec(
            num_scalar_prefetch=1, grid=(nb,),
            in_specs=[pl.BlockSpec((n, tr, L), lambda i, ids: (0, ids[1] * nb + i, 0)), blk], out_specs=blk),
        compiler_params=_params(16, 1),
    )(ids, g, theirs)


def _sum_chips(pair, parts, ids, *, name):
    _, R, L = parts.shape
    tr = 128

    def body(ids_ref, o_ref, r_ref, out_ref):
        out_ref[...] = ((o_ref[...].astype(F32) + r_ref[0].astype(F32)) + r_ref[1].astype(F32)) + r_ref[2].astype(F32)

    return pl.pallas_call(
        body, name=name, out_shape=jax.ShapeDtypeStruct((2, R, L), F32),
        grid_spec=pltpu.PrefetchScalarGridSpec(
            num_scalar_prefetch=1, grid=(R // tr,),
            in_specs=[pl.BlockSpec((None, tr, L), lambda i, ids: (ids[0], i, 0)),
                      pl.BlockSpec((3, tr, L), lambda i, ids: (0, i, 0))],
            out_specs=pl.BlockSpec((None, tr, L), lambda i, ids: (ids[1], i, 0))),
        compiler_params=_params(32, 1),
    )(ids, pair, parts)


def _mesh_ids():
    x, y, c = _mesh_pos()
    return jnp.stack([2 * x + y, c]).astype(jnp.int32)


def _place_shard(rows, ids, *, name):
    R, L = rows.shape
    tr = 256

    def body(ids_ref, in_ref, out_ref):
        out_ref[...] = in_ref[...].astype(BF16)

    return pl.pallas_call(
        body, name=name, out_shape=jax.ShapeDtypeStruct((4, R, L), BF16),
        grid_spec=pltpu.PrefetchScalarGridSpec(
            num_scalar_prefetch=1, grid=(R // tr,), in_specs=[pl.BlockSpec((tr, L), lambda i, ids: (i, 0))],
            out_specs=pl.BlockSpec((None, tr, L), lambda i, ids: (ids[0], i, 0))),
        compiler_params=_params(16, 1),
    )(ids, rows)


def _remote(src, dst, send_sem, recv_sem, to):
    return pltpu.make_async_remote_copy(src_ref=src, dst_ref=dst, send_sem=send_sem, recv_sem=recv_sem,
                                        device_id=to, device_id_type=MESH_IDS)


def _rows(ref, lead, start, size):
    return ref.at[tuple(pl.ds(0, n) for n in ref.shape[:lead]) + (pl.ds(start, size),)]


def _other_chips():
    x, y, _ = _mesh_pos()
    return [(1 - x, y), (x, 1 - y), (1 - x, 1 - y)]


def _plan_gather_ici(bufs):
    n = len(bufs)

    def copies(outs, send, recv):
        x, y, c = _mesh_pos()
        res = []
        for b in range(n):
            half = bufs[b].shape[1] // 2
            mine = _rows(outs[b].at[2 * x + y], 0, c * half, half)
            for j, (cx, cy) in enumerate(_other_chips()):
                res.append((_remote(mine, mine, send(3 * b + j), recv(3 * b + j), (cx, cy, c)),
                            _remote(mine, _rows(outs[b].at[2 * cx + cy], 0, c * half, half),
                                    send(3 * b + j), recv(3 * b + j), (x, y, c))))
        return res

    def start(ins, outs, send, recv, loc):
        for out_cp, _ in copies(outs, send, recv):
            out_cp.start()

    def wait(ins, outs, send, recv, loc):
        for out_cp, in_cp in copies(outs, send, recv):
            in_cp.wait_recv()
            out_cp.wait_send()

    outs = [jax.ShapeDtypeStruct(b.shape, b.dtype) for b in bufs]
    return _Plan(bufs, outs, 3 * n, 0, start, wait, aliases={b: b for b in range(n)})


def _plan_gather_forward(bufs):
    n = len(bufs)

    def copies(outs, send, recv):
        x, y, c = _mesh_pos()
        res = []
        for b in range(n):
            half = bufs[b].shape[1] // 2
            for j, (cx, cy) in enumerate(_other_chips()):
                slot = outs[b].at[2 * cx + cy]
                res.append((_remote(_rows(slot, 0, c * half, half), _rows(slot, 0, c * half, half),
                                    send(3 * b + j), recv(3 * b + j), (x, y, 1 - c)),
                            _remote(_rows(slot, 0, c * half, half), _rows(slot, 0, (1 - c) * half, half),
                                    send(3 * b + j), recv(3 * b + j), (x, y, c))))
        return res

    def start(ins, outs, send, recv, loc):
        for out_cp, _ in copies(outs, send, recv):
            out_cp.start()

    def wait(ins, outs, send, recv, loc):
        for out_cp, in_cp in copies(outs, send, recv):
            in_cp.wait_recv()
            out_cp.wait_send()

    outs = [jax.ShapeDtypeStruct(b.shape, b.dtype) for b in bufs]
    return _Plan(bufs, outs, 3 * n, 0, start, wait, aliases={b: b for b in range(n)})


def _plan_pair_swap(g):
    half = g.shape[1] // 2

    def copy(ins, outs, send, recv, loc):
        x, y, c = _mesh_pos()
        return _remote(_rows(ins[0], 1, (1 - c) * half, half), outs[0], send(0), recv(0), (x, y, 1 - c))

    return _Plan([g], [jax.ShapeDtypeStruct((4, half, g.shape[2]), g.dtype)], 1, 0,
                 lambda *a: copy(*a).start(), lambda *a: copy(*a).wait())


def _plan_pair_gather(buf):
    def copies(ins, outs, send, recv, loc):
        x, y, c = _mesh_pos()
        return (_remote(outs[0].at[c], outs[0].at[c], send(0), recv(0), (x, y, 1 - c)),
                _remote(outs[0].at[c], outs[0].at[1 - c], send(0), recv(0), (x, y, c)))

    def wait(*a):
        out_cp, in_cp = copies(*a)
        in_cp.wait_recv()
        out_cp.wait_send()

    return _Plan([buf], [jax.ShapeDtypeStruct(buf.shape, buf.dtype)], 1, 0, lambda *a: copies(*a)[0].start(), wait,
                 aliases={0: 0})


def _plan_chip_scatter(p):
    def copies(ins, outs, send, recv, loc):
        _, _, c = _mesh_pos()
        return [_remote(ins[0].at[2 * cx + cy], outs[0].at[j], send(j), recv(j), (cx, cy, c))
                for j, (cx, cy) in enumerate(_other_chips())]

    def start(*a):
        for cp in copies(*a):
            cp.start()

    def wait(*a):
        for cp in copies(*a):
            cp.wait()

    return _Plan([p], [jax.ShapeDtypeStruct((3,) + p.shape[1:], p.dtype)], 3, 0, start, wait)


def _small_allreduce(vec):
    R, L = vec.shape

    def body(v_ref, sum_ref, all_ref, send_sems, recv_sems):
        x, y, c = _mesh_pos()
        me = 4 * x + 2 * y + c
        all_ref[me] = v_ref[...]
        cps = []
        for r in range(1, 8):
            peer = (x ^ (r >> 2), y ^ ((r >> 1) & 1), c ^ (r & 1))
            cps.append(pltpu.make_async_remote_copy(
                src_ref=v_ref, dst_ref=all_ref.at[me], send_sem=send_sems.at[r - 1], recv_sem=recv_sems.at[r - 1],
                device_id=peer, device_id_type=MESH_IDS))
        for cp in cps:
            cp.start()
        for r in range(1, 8):
            src = 4 * (x ^ (r >> 2)) + 2 * (y ^ ((r >> 1) & 1)) + (c ^ (r & 1))
            pltpu.make_async_remote_copy(
                src_ref=v_ref, dst_ref=all_ref.at[src], send_sem=send_sems.at[r - 1], recv_sem=recv_sems.at[r - 1],
                device_id=(x, y, c), device_id_type=MESH_IDS).wait_recv()
        for cp in cps:
            cp.wait_send()
        s = all_ref[0]
        for d in range(1, 8):
            s = s + all_ref[d]
        sum_ref[...] = s

    vm = pl.BlockSpec(memory_space=pltpu.VMEM)
    return pl.pallas_call(
        body, name="small_allreduce", in_specs=[vm], out_specs=[vm, vm],
        out_shape=[jax.ShapeDtypeStruct((R, L), F32), jax.ShapeDtypeStruct((8, R, L), F32)],
        scratch_shapes=[pltpu.SemaphoreType.DMA((7,)), pltpu.SemaphoreType.DMA((7,))],
        compiler_params=_params(16),
    )(vec)[0]


ROWS_L1, ROWS_L0, ROWS_ODD = 3328, 2048, 768
ODD_PARTS = (("w_out_e", (256, 1024)), ("w_in_e", (1024, 392)), ("w_qb", (256, 192)), ("w_kvb", (256, 256)))


def _odd_rows(parts, dtype, gnorm=None):
    rows = [parts[n].reshape(-1, 1024).astype(dtype) for n, _ in ODD_PARTS]
    used = sum(r.shape[0] for r in rows)
    if gnorm is not None:
        bits = lax.bitcast_convert_type(gnorm.reshape(-1), BF16).reshape(1, 512)
        rows.append(jnp.pad(bits, ((0, 0), (0, 512))))
        used += 1
    rows.append(jnp.zeros((ROWS_ODD - used, 1024), dtype))
    return jnp.concatenate(rows, axis=0)


def _odd_unrows(buf, with_gnorm=False):
    out, off = {}, 0
    for n, shape in ODD_PARTS:
        nr = math.prod(shape) // 1024
        out[n] = buf[off:off + nr].reshape(shape)
        off += nr
    if with_gnorm:
        out["hg_gnorm"] = lax.bitcast_convert_type(buf[off, :512].reshape(256, 2), F32).reshape(1, 256)
    return out


def _pack_small(vals):
    flat = jnp.concatenate([vals[n].reshape(-1).astype(F32) for n, _ in SMALL])
    return jnp.pad(flat, (0, SMALL_ROWS * 1024 - flat.shape[0])).reshape(SMALL_ROWS, 1024)


def _unpack_small(packed):
    flat = packed.reshape(-1)
    out, off = {}, 0
    for n, shape in SMALL:
        size = math.prod(shape)
        out[n] = flat[off:off + size].reshape(shape)
        off += size
    return out


def _rope_tables(positions):
    half = ROPE // 2
    inv_freq = ROPE_BASE ** (-jnp.arange(half, dtype=F32) / half)
    ang = positions.astype(F32).reshape(-1, 1) * inv_freq
    cos, sin = jnp.cos(ang), jnp.sin(ang)
    T = ang.shape[0]
    one, z16, z32 = jnp.ones((T, NOPE), F32), jnp.zeros((T, half), F32), jnp.zeros((T, 32), F32)
    z64 = jnp.zeros((T, NOPE), F32)
    c = jnp.concatenate([one, cos, cos, z32], axis=1)
    s1 = jnp.concatenate([z64, -sin, z16, z32], axis=1)
    s2 = jnp.concatenate([z64, z16, sin, z32], axis=1)
    return c, s1, s2


def _local_step(x, positions, tgt, odd, bufs, P, exchange):
    T = x.shape[0]
    row = lambda a: a.reshape(1, -1)
    rc, rs1, rs2 = _rope_tables(positions)
    blk = lambda f: pl.BlockSpec((None, D, D), f)

    w_in_e = odd["w_in_e"]
    w_in = jnp.concatenate([w_in_e[:, :512], w_in_e[:, 544:1568], w_in_e[:, 512:544], jnp.zeros((D, 96), BF16)], axis=1)
    wq = jnp.pad(odd["w_qb"].reshape(256, HEADS, NOPE + ROPE), ((0, 0), (0, 0), (0, 32))).reshape(256, HEADS * 128)
    kvb = odd["w_kvb"].reshape(256, HEADS, NOPE + VDIM)
    wk = jnp.pad(kvb[:, :, :NOPE], ((0, 0), (0, 0), (0, 64))).reshape(256, HEADS * 128)
    wv = kvb[:, :, NOPE:].reshape(256, HEADS * VDIM)
    w_out_e = odd["w_out_e"]
    sgu_w = P["sgu_w"][0]
    sgu_bt = P["sgu_b"][0].T
    gq, gkv = P["mla_gq"], P["mla_gkv"]
    gnorm = P["hg_gnorm"]

    x_b = x.astype(BF16)
    z0 = _matmul(x_b, w_in, name="in_proj_e", M=T, N=1664, K=D, tn=1664)[0]
    q, k, v = _mla_prep(z0, gq, gkv, wq, wk, wv, rc, rs1, rs2)
    if exchange:
        ids = _mesh_ids()
        placed = [_place_shard(b, ids, name=f"place_shard_{l}") for l, b in enumerate(bufs)]
        a_out, lse, wg0, wg1 = _flash_fwd(q, k, v, plan=_plan_gather_ici(placed))
    else:
        a_out, lse = _flash_fwd(q, k, v)
        wg0, wg1 = bufs
    mix0 = _sgu_fwd(z0, a_out, P["sgu_ln_g"], P["sgu_ln_b"], sgu_w, sgu_bt)
    res = _proj_ln(mix0, w_out_e, x, row(P["ln1_g"][0]), row(P["ln1_b"][0]), name="out_proj_ln_e",
                   plan=_plan_gather_forward([wg0, wg1]) if exchange else None)
    r1, h1, h1b = res[:3]
    if exchange:
        wg0, wg1 = res[3:]
    ra0, r2, h2, h2b = _ffn_ln(h1b, wg0, h1, row(P["ln2_g"][0]), row(P["ln2_b"][0]), name="ffn_ln_0")
    z4 = _matmul(h2b, wg1, name="in_proj_o", M=T, N=4 * D, K=D, b_spec=blk(lambda i, j, k: (j, 2, 0)),
                 out_shape=jax.ShapeDtypeStruct((4, T, D), F32),
                 o_spec=pl.BlockSpec((None, min(512, T), D), lambda i, j, k: (j, i, 0)))[0]
    y1, o_raw, states = _hgrn_fwd(z4, P["hg_lb"], gnorm)
    r3, h3, h3b = _proj_ln(y1, wg1, h2, row(P["ln1_g"][1]), row(P["ln1_b"][1]), name="out_proj_ln_o", w_rowblk=12)
    ra1, r4, h4, _ = _ffn_ln(h3b, wg1, h3, row(P["ln2_g"][1]), row(P["ln2_b"][1]), name="ffn_ln_1")
    dy, loss_parts = _loss_dy(h4, tgt)

    gs = {}
    ln1_g, ln1_b, ln2_g, ln2_b = [None, None], [None, None], [None, None], [None, None]

    def ffn_bwd(l, dh, r_out, ra, h_mid_b, g2, wg, rows, plan=None):
        dr, dr_b, dg, db = _ln_bwd(dh, r_out, row(g2), name=f"ln2_bwd_{l}")
        ln2_g[l], ln2_b[l] = dg.sum(0), db.sum(0)
        da, *extra = _matmul(dr_b, wg, tb=True, mul=ra, out_dtype=BF16, name=f"ffn_da_{l}", M=T, N=4 * D, K=D,
                             b_spec=blk(lambda i, j, k: (j, 1, 0)), plan=plan)
        gbuf = _matmul(ra, dr_b, ta=True, a_sq=True, name=f"ffn_dw2_{l}", M=4 * D, N=D, K=T, tm=1024, tk=512,
                       out_shape=jax.ShapeDtypeStruct((4, rows, D), BF16), o_spec=blk(lambda i, j, k: (i, 1, 0)))[0]
        gbuf = _matmul(h_mid_b, da, ta=True, name=f"ffn_dw1_{l}", M=D, N=4 * D, K=T, tm=1024, tk=512, into=gbuf,
                       out_shape=jax.ShapeDtypeStruct((4, rows, D), BF16), o_spec=blk(lambda i, j, k: (j, 0, 0)))[0]
        dh_mid = _matmul(da, wg, tb=True, add=dr, add_scale=ALPHA, name=f"ffn_dh_{l}", M=T, N=D, K=4 * D,
                         b_spec=blk(lambda i, j, k: (k, 0, 0)))[0]
        return dh_mid, gbuf, extra

    dh3, g1, _ = ffn_bwd(1, dy, r4, ra1, h3b, P["ln2_g"][1], wg1, ROWS_L1)
    dr3, dr3_b, dg, db = _ln_bwd(dh3, r3, row(P["ln1_g"][1]), name="ln1_bwd_1")
    ln1_g[1], ln1_b[1] = dg.sum(0), db.sum(0)
    g1_sds = jax.ShapeDtypeStruct((4, ROWS_L1, D), BF16)
    g1 = _matmul(y1, dr3_b, ta=True, name="dw_out_o", M=D, N=D, K=T, tm=256, tk=512, into=g1, out_shape=g1_sds,
                 o_spec=pl.BlockSpec((None, 256, D), lambda i, j, k: (i, 12, 0)))[0]
    dmix1 = _matmul(dr3_b, wg1, tb=True, name="dmix_o", M=T, N=D, K=D, b_spec=_rows4_spec(12, 3), b_merge=(D, D))[0]
    dz4, dlb, dgn = _hgrn_bwd(z4, o_raw, dmix1, states, P["hg_lb"], gnorm)
    g1 = _matmul(h2b, dz4, ta=True, name="dw_in_o", M=D, N=4 * D, K=T, tm=1024, tk=512, into=g1, out_shape=g1_sds,
                 b_spec=pl.BlockSpec((None, min(512, T), D), lambda i, j, k: (j, k, 0)),
                 o_spec=blk(lambda i, j, k: (j, 2, 0)))[0]
    dh2 = _matmul(dz4, wg1, tb=True, add=dr3, add_scale=ALPHA, name="dh_in_o", M=T, N=D, K=4 * D,
                  a_spec=pl.BlockSpec((None, min(512, T), D), lambda i, j, k: (k, i, 0)),
                  b_spec=blk(lambda i, j, k: (k, 2, 0)))[0]
    d_lb1 = dlb.sum(0)
    gs["hg_lb"] = jnp.stack([-d_lb1, d_lb1])
    gs["hg_gnorm"] = dgn.sum(0)[None]

    dh1, g0, swapped1 = ffn_bwd(0, dh2, r2, ra0, h1b, P["ln2_g"][0], wg0, ROWS_L0,
                                plan=_plan_pair_swap(g1) if exchange else None)
    dr1, dr1_b, dg, db = _ln_bwd(dh1, r1, row(P["ln1_g"][0]), name="ln1_bwd_0")
    ln1_g[0], ln1_b[0] = dg.sum(0), db.sum(0)
    godd = {"w_out_e": _matmul(mix0, dr1_b, ta=True, name="dw_out_e", M=D, N=D, K=T, tm=1024, tk=512)[0]}
    dmix0, *swapped0 = _matmul(dr1_b, w_out_e, tb=True, name="dmix_e", M=T, N=D, K=D,
                               plan=_plan_pair_swap(g0) if exchange else None)
    delta, do_b = _attn_delta(dmix0, a_out)
    if exchange:
        pair1 = _add_pairs(g1, swapped1[0], ids, name="grad_pair_add_1")
        pair0 = _add_pairs(g0, swapped0[0], ids, name="grad_pair_add_0")
        dq4, dk, dv, parts0, parts1 = _flash_bwd(
            q, k, v, do_b, lse, delta, plan=_join_plans([_plan_chip_scatter(pair0), _plan_chip_scatter(pair1)]))
        half0 = _sum_chips(pair0, parts0, ids, name="grad_chip_sum_0")
        half1 = _sum_chips(pair1, parts1, ids, name="grad_chip_sum_1")
        dc, dkr, dwq, dwk, dwv, dgq, dgkv, g0, g1 = _mla_bwd(
            z0, dq4, dk, dv, gq, gkv, wq, wk, wv, rc, rs1, rs2,
            plan=_join_plans([_plan_pair_gather(half0), _plan_pair_gather(half1)]))
        g0, g1 = g0.reshape(ROWS_L0, D), g1.reshape(ROWS_L1, D)
    else:
        dq4, dk, dv = _flash_bwd(q, k, v, do_b, lse, delta)
        dc, dkr, dwq, dwk, dwv, dgq, dgkv = _mla_bwd(z0, dq4, dk, dv, gq, gkv, wq, wk, wv, rc, rs1, rs2)
    dz0, dsw, dsb, dslg, dslb = _sgu_bwd(z0, dmix0, dc, dkr, P["sgu_ln_g"], P["sgu_ln_b"], sgu_w, sgu_bt)
    dw_in = _matmul(x_b, dz0, ta=True, name="dw_in_e", M=D, N=1664, K=T, tm=1024, tn=1664, tk=512)[0]
    godd["w_in_e"] = jnp.concatenate([dw_in[:, :512], dw_in[:, 1536:1568], dw_in[:, 512:1536]], axis=1)
    grad_x = _matmul(dz0, w_in, tb=True, add=dr1, add_scale=ALPHA, name="dx", M=T, N=D, K=1664, tk=1664)[0]

    godd["w_qb"] = dwq.reshape(256, HEADS, 128)[:, :, :NOPE + ROPE].reshape(256, HEADS * (NOPE + ROPE))
    godd["w_kvb"] = jnp.concatenate([dwk.reshape(256, HEADS, 128)[:, :, :NOPE], dwv.reshape(256, HEADS, VDIM)],
                                    axis=2).reshape(256, HEADS * (NOPE + VDIM))
    gs["mla_gq"], gs["mla_gkv"] = dgq.sum(0)[None], dgkv.sum(0)[None]
    gs["sgu_ln_g"], gs["sgu_ln_b"] = dslg.sum(0)[None], dslb.sum(0)[None]
    gs["sgu_w"], gs["sgu_b"] = dsw[None], dsb[:, :SGU_G].T[None]
    gs["ln1_g"], gs["ln1_b"] = jnp.stack(ln1_g), jnp.stack(ln1_b)
    gs["ln2_g"], gs["ln2_b"] = jnp.stack(ln2_g), jnp.stack(ln2_b)
    return loss_parts, grad_x, g0, g1, godd, gs


WEIGHTS = ['w_in_e', 'mla_gq', 'mla_gkv', 'w_qb', 'w_kvb', 'sgu_ln_g', 'sgu_ln_b', 'sgu_w', 'sgu_b', 'w_out_e',
           'w_in_o', 'hg_lb', 'hg_gnorm', 'w_out_o', 'ln1_g', 'ln1_b', 'w_ff1', 'w_ff2', 'ln2_g', 'ln2_b']


def kernel(x, positions, w_in_e, mla_gq, mla_gkv, w_qb, w_kvb, sgu_ln_g, sgu_ln_b, sgu_w, sgu_b, w_out_e, w_in_o, hg_lb, hg_gnorm, w_out_o, ln1_g, ln1_b, w_ff1, w_ff2, ln2_g, ln2_b, loss_target, m_w_in_e, m_mla_gq, m_mla_gkv, m_w_qb, m_w_kvb, m_sgu_ln_g, m_sgu_ln_b, m_sgu_w, m_sgu_b, m_w_out_e, m_w_in_o, m_hg_lb, m_hg_gnorm, m_w_out_o, m_ln1_g, m_ln1_b, m_w_ff1, m_w_ff2, m_ln2_g, m_ln2_b, v_w_in_e, v_mla_gq, v_mla_gkv, v_w_qb, v_w_kvb, v_sgu_ln_g, v_sgu_ln_b, v_sgu_w, v_sgu_b, v_w_out_e, v_w_in_o, v_hg_lb, v_hg_gnorm, v_w_out_o, v_ln1_g, v_ln1_b, v_w_ff1, v_w_ff2, v_ln2_g, v_ln2_b):
    args = dict(locals())
    w = {n: args[n] for n in WEIGHTS}
    m = {n: args["m_" + n] for n in WEIGHTS}
    v = {n: args["v_" + n] for n in WEIGHTS}
    cx, cy, cc = _mesh_pos()
    chip = 2 * cx + cy

    odd_shard = _odd_rows({"w_out_e": w_out_e[0], "w_in_e": w_in_e[0], "w_qb": w_qb[0], "w_kvb": w_kvb[0]}, BF16,
                          gnorm=hg_gnorm)
    ids = _mesh_ids()
    gathered = _run_plan(_plan_gather_ici([_place_shard(odd_shard, ids, name="place_shard_odd")]), name="odd_gather")[0]
    gathered = _run_plan(_plan_gather_forward([gathered]), name="odd_gather_forward")[0]
    per_chip = [_odd_unrows(gathered[j], with_gnorm=True) for j in range(4)]
    odd = {"w_out_e": jnp.concatenate([p["w_out_e"] for p in per_chip], axis=0)}
    for n in ("w_in_e", "w_qb", "w_kvb"):
        odd[n] = jnp.concatenate([p[n] for p in per_chip], axis=1)
    small = {n: w[n] for n, _ in SMALL if n != "hg_gnorm"}
    small["hg_gnorm"] = jnp.concatenate([p["hg_gnorm"] for p in per_chip], axis=1)
    rows_l0 = jnp.concatenate([w_ff1[0], w_ff2[0]], axis=0).astype(BF16)
    rows_l1 = jnp.concatenate([w_ff1[1], w_ff2[1], w_in_o[0], w_out_o[0]], axis=0).astype(BF16)

    loss_parts, grad_x, g_l0, g_l1, godd, gs = _local_step(x[0], positions[0], loss_target[0], odd, (rows_l0, rows_l1),
                                                            small, True)

    loss = lax.psum((0.5 / D) * jnp.sum(loss_parts), ("x", "y", "c"))

    by_chip = [_odd_rows({"w_out_e": jnp.split(godd["w_out_e"], 4, axis=0)[j],
                          **{n: jnp.split(godd[n], 4, axis=1)[j] for n in ("w_in_e", "w_qb", "w_kvb")}}, BF16)
               for j in range(4)]
    godd_buf = jnp.stack(by_chip)
    theirs = _run_plan(_plan_pair_swap(godd_buf), name="odd_pair_swap")[0]
    pair = _add_pairs(godd_buf, theirs, ids, name="odd_pair_add")
    parts = _run_plan(_plan_chip_scatter(pair), name="odd_chip_scatter")[0]
    g_odd = _run_plan(_plan_pair_gather(_sum_chips(pair, parts, ids, name="odd_chip_sum")), name="odd_pair_gather")[0]
    g_odd = _odd_unrows(g_odd.reshape(ROWS_ODD, 1024))

    g_small = _unpack_small(_small_allreduce(_pack_small(gs)))
    g_gnorm = lax.dynamic_slice_in_dim(g_small["hg_gnorm"], chip * 256, 256, axis=1)

    grads = {n: g_small[n] for n, _ in SMALL if n != "hg_gnorm"}
    grads["hg_gnorm"] = g_gnorm

    delta, new_m, new_v = {}, {}, {}
    for n, bufs_, row0 in (("w_ff1", [g_l0, g_l1], 0), ("w_ff2", [g_l0, g_l1], 1024), ("w_in_o", [g_l1], 2048),
                           ("w_out_o", [g_l1], 3072)):
        grads[n], delta[n], new_m[n], new_v[n] = _adamw_rows(w[n], m[n], v[n], bufs_, row0, name=f"adamw_{n}")
    for n, _ in ODD_PARTS:
        grads[n] = g_odd[n][None]
        d_, m_, v_ = _adamw(w[n][0], g_odd[n], m[n][0], v[n][0], name=f"adamw_{n}")
        delta[n], new_m[n], new_v[n] = d_[None], m_[None], v_[None]
    rest = [n for n in WEIGHTS if n not in delta]

    def pack_rest(d):
        flat = jnp.concatenate([d[n].reshape(-1) for n in rest])
        return jnp.pad(flat, (0, SMALL_ROWS * 1024 - flat.shape[0])).reshape(SMALL_ROWS, 1024)

    outs = _adamw(pack_rest(w), pack_rest(grads), pack_rest(m), pack_rest(v), name="adamw_small")
    for dst, packed in zip((delta, new_m, new_v), outs):
        flat, off = packed.reshape(-1), 0
        for n in rest:
            size = math.prod(w[n].shape)
            dst[n] = flat[off:off + size].reshape(w[n].shape)
            off += size

    return (loss, grad_x[None], *[grads[n] for n in WEIGHTS], *[delta[n] for n in WEIGHTS],
            *[new_m[n] for n in WEIGHTS], *[new_v[n] for n in WEIGHTS])
```

```python
import functools
import math

import jax
import jax.numpy as jnp
from jax import lax
from jax.experimental import pallas as pl
from jax.experimental.pallas import tpu as pltpu

F32 = jnp.float32
BF16 = jnp.bfloat16
MESH_IDS = pl.DeviceIdType.MESH

D = 1024
DEPTH = 2
HEADS = 8
NOPE, ROPE, VDIM = 64, 32, 64
QK_SCALE = (NOPE + ROPE) ** -0.5
ROPE_BASE = 10000.0
SGU_G, SGU_C = 4, 128
HG_CHUNK = 64
HG_HEADS_PER_STEP = 4
ALPHA = (2 * DEPTH) ** 0.25
EPS = 1e-5
LR, B1, B2, ADAM_EPS, WD, STEP = 0.001, 0.9, 0.999, 1e-08, 0.01, 10
GELU_C = math.sqrt(2.0 / math.pi)
GELU_A = 0.044715
HI = lax.Precision.HIGHEST
MB = 1024 * 1024
ROW_BLOCK = 512

NT_DIMS = (((1,), (1,)), ((), ()))
TN_DIMS = (((0,), (0,)), ((), ()))

SHARDED = (
    ("w_in_e", (1, 1024, 392), 2), ("w_qb", (1, 256, 192), 2), ("w_kvb", (1, 256, 256), 2),
    ("w_out_e", (1, 256, 1024), 1), ("w_in_o", (1, 1024, 1024), 2), ("w_out_o", (1, 256, 1024), 1),
    ("w_ff1", (2, 1024, 1024), 2), ("w_ff2", (2, 1024, 1024), 1), ("hg_gnorm", (1, 256), 1),
)
PACK_ROWS = 6144
HALF_ROWS = PACK_ROWS // 2
SMALL = (("mla_gq", (1, 256)), ("mla_gkv", (1, 256)), ("sgu_ln_g", (1, 512)), ("sgu_ln_b", (1, 512)),
         ("sgu_w", (1, 4, 128, 128)), ("sgu_b", (1, 4, 128)), ("hg_lb", (2, 1024)), ("hg_gnorm", (1, 1024)),
         ("ln1_g", (2, 1024)), ("ln1_b", (2, 1024)), ("ln2_g", (2, 1024)), ("ln2_b", (2, 1024)))
SMALL_ROWS = 80


def _params(vmem_mb, n_axes=0):
    kw = dict(vmem_limit_bytes=vmem_mb * MB)
    if n_axes:
        kw["dimension_semantics"] = ("arbitrary",) * n_axes
    return pltpu.CompilerParams(**kw)


_ANY = pl.BlockSpec(memory_space=pl.ANY)


def _mesh_pos():
    return lax.axis_index("x"), lax.axis_index("y"), lax.axis_index("c")


class _Plan:
    def __init__(self, ins, outs, n_remote, n_local, start, wait, aliases=None):
        self.ins, self.outs, self.n_remote, self.n_local = list(ins), list(outs), n_remote, n_local
        self.start, self.wait, self.aliases = start, wait, dict(aliases or {})


def _join_plans(plans):
    ins, outs, aliases, parts = [], [], {}, []
    nr = nl = 0
    for p in plans:
        parts.append((p, len(ins), len(outs), nr, nl))
        aliases.update({len(ins) + i: len(outs) + o for i, o in p.aliases.items()})
        ins += p.ins
        outs += p.outs
        nr += p.n_remote
        nl += p.n_local

    def run(which):
        def go(in_refs, out_refs, send, recv, loc):
            for p, i0, o0, r0, l0 in parts:
                getattr(p, which)(in_refs[i0:i0 + len(p.ins)], out_refs[o0:o0 + len(p.outs)],
                                  lambda i, r0=r0: send(r0 + i), lambda i, r0=r0: recv(r0 + i),
                                  lambda i, l0=l0: loc(l0 + i))
        return go

    return _Plan(ins, outs, nr, nl, run("start"), run("wait"), aliases)


def _plan_io(plan, n_in, n_out):
    if plan is None:
        return [], [], [], [], {}
    sems = [pltpu.SemaphoreType.DMA((max(plan.n_remote, 1),)), pltpu.SemaphoreType.DMA((max(plan.n_remote, 1),)),
            pltpu.SemaphoreType.DMA((max(plan.n_local, 1),))]
    aliases = {n_in + i: n_out + o for i, o in plan.aliases.items()}
    return plan.ins, [_ANY] * len(plan.outs), plan.outs, sems, aliases


def _split_refs(refs, n_in, n_out, n_scr, plan):
    p_in, p_out = (len(plan.ins), len(plan.outs)) if plan is not None else (0, 0)
    refs = list(refs)
    ins, refs = refs[:n_in], refs[n_in:]
    pins, refs = refs[:p_in], refs[p_in:]
    outs, refs = refs[:n_out], refs[n_out:]
    pouts, refs = refs[:p_out], refs[p_out:]
    scr, psem = refs[:n_scr], refs[n_scr:]
    psem = tuple((lambda i, s=s: s.at[i]) for s in psem)
    return ins, outs, scr, (pins, pouts, psem)


def _grid_edge(grid, last):
    cond = None
    for ax, n in enumerate(grid):
        c = pl.program_id(ax) == (n - 1 if last else 0)
        cond = c if cond is None else cond & c
    return cond


def _plan_start(plan, pctx, grid):
    if plan is not None:
        pins, pouts, psem = pctx
        pl.when(_grid_edge(grid, False))(lambda: plan.start(pins, pouts, *psem))


def _plan_wait(plan, pctx, grid):
    if plan is not None:
        pins, pouts, psem = pctx
        pl.when(_grid_edge(grid, True))(lambda: plan.wait(pins, pouts, *psem))


def _run_plan(plan, *, name):
    def body(*refs):
        _, _, _, (pins, pouts, psem) = _split_refs(refs, 0, 0, 0, plan)
        plan.start(pins, pouts, *psem)
        plan.wait(pins, pouts, *psem)

    p_in, p_ospec, p_oshape, p_scr, p_alias = _plan_io(plan, 0, 0)
    return pl.pallas_call(body, name=name, in_specs=[_ANY] * len(p_in), out_specs=p_ospec, out_shape=p_oshape,
                          scratch_shapes=p_scr, input_output_aliases=p_alias)(*p_in)


def _fold8(x):
    return x.reshape(x.shape[0] // 8, 8, x.shape[1]).sum(axis=0)


def _ln_stats(r):
    mu = jnp.mean(r, -1, keepdims=True)
    xc = r - mu
    rstd = lax.rsqrt(jnp.mean(xc * xc, -1, keepdims=True) + EPS)
    return xc * rstd, rstd


def _sigmoid(x):
    return 1.0 / (1.0 + jnp.exp(-x))


def _gelu(x):
    return 0.5 * x * (1.0 + jnp.tanh(GELU_C * (x + GELU_A * x * x * x)))


def _gelu_grad(x):
    t = jnp.tanh(GELU_C * (x + GELU_A * x * x * x))
    return 0.5 * (1.0 + t) + 0.5 * x * (1.0 - t * t) * GELU_C * (1.0 + 3.0 * GELU_A * x * x)


def _matmul(a, b, *, name, M, N, K, ta=False, tb=False, out_dtype=F32, tm=512, tn=1024, tk=1024,
            a_spec=None, b_spec=None, b_merge=None, out_shape=None, o_spec=None, into=None,
            a_sq=False, mul=None, add=None, add_scale=1.0, plan=None):
    tm, tn, tk = min(tm, M), min(tn, N), min(tk, K)
    assert M % tm == 0 and N % tn == 0 and K % tk == 0
    grid = (M // tm, N // tn, K // tk)
    nk = grid[2]
    if a_spec is None:
        a_spec = pl.BlockSpec((tk, tm), lambda i, j, k: (k, i)) if ta else pl.BlockSpec((tm, tk), lambda i, j, k: (i, k))
    if b_spec is None:
        b_spec = pl.BlockSpec((tn, tk), lambda i, j, k: (j, k)) if tb else pl.BlockSpec((tk, tn), lambda i, j, k: (k, j))
    if o_spec is None:
        o_spec = pl.BlockSpec((tm, tn), lambda i, j, k: (i, j))
        out_shape = jax.ShapeDtypeStruct((M, N), out_dtype)
    e_spec = pl.BlockSpec((tm, tn), lambda i, j, k: (i, j))
    dims = (((0 if ta else 1,), (1 if tb else 0,)), ((), ()))
    extra = [e for e in (mul, add, into) if e is not None]
    n_in = 2 + len(extra)

    def body(*refs):
        ins, outs, scr, pctx = _split_refs(refs, n_in, 1, 1 if nk > 1 else 0, plan)
        a_ref, b_ref = ins[0], ins[1]
        rest = list(ins[2:])
        mul_ref = rest.pop(0) if mul is not None else None
        add_ref = rest.pop(0) if add is not None else None
        o_ref = outs[0]
        _plan_start(plan, pctx, grid)
        av = a_ref[...]
        if a_sq:
            av = av * av
        bv = b_ref[...]
        if b_merge is not None:
            bv = bv.reshape(b_merge)
        p = lax.dot_general(av, bv, dims, preferred_element_type=F32)

        def finish(r):
            if mul_ref is not None:
                r = r * (2.0 * mul_ref[...].astype(F32))
            if add_ref is not None:
                r = r + add_scale * add_ref[...]
            o_ref[...] = r.astype(o_ref.dtype)

        if nk == 1:
            finish(p)
        else:
            acc_ref = scr[0]
            k = pl.program_id(2)

            @pl.when(k == 0)
            def _():
                acc_ref[...] = p

            @pl.when(k > 0)
            def _():
                acc_ref[...] += p

            @pl.when(k == nk - 1)
            def _():
                finish(acc_ref[...])

        _plan_wait(plan, pctx, grid)

    p_in, p_ospec, p_oshape, p_scr, p_alias = _plan_io(plan, n_in, 1)
    aliases = dict(p_alias)
    if into is not None:
        aliases[n_in - 1] = 0
    return pl.pallas_call(
        body, name=name, grid=grid,
        in_specs=[a_spec, b_spec] + [e_spec] * (len(extra) - (into is not None)) + [_ANY] * (into is not None)
        + [_ANY] * len(p_in),
        out_specs=[o_spec] + p_ospec, out_shape=[out_shape] + p_oshape,
        scratch_shapes=([pltpu.VMEM((tm, tn), F32)] if nk > 1 else []) + p_scr,
        input_output_aliases=aliases, compiler_params=_params(48, 3),
    )(a, b, *extra, *p_in)


def _rows4_spec(rowblk, n_axes):
    return pl.BlockSpec((4, 256, D), lambda *_: (0, rowblk, 0))


def _proj_ln(a_b, w, h_prev, g, b, *, name, w_rowblk=None, plan=None):
    T = a_b.shape[0]
    tm = min(ROW_BLOCK, T)
    grid = (T // tm,)
    row = pl.BlockSpec((tm, D), lambda i: (i, 0))
    vec = pl.BlockSpec((1, D), lambda i: (0, 0))
    w_spec = pl.BlockSpec((D, D), lambda i: (0, 0)) if w_rowblk is None else _rows4_spec(w_rowblk, 1)

    def body(*refs):
        (a_ref, w_ref, h_ref, g_ref, b_ref), (r_ref, ho_ref, hb_ref), _, pctx = _split_refs(refs, 5, 3, 0, plan)
        _plan_start(plan, pctx, grid)
        mix = jnp.dot(a_ref[...], w_ref[...].reshape(D, D), preferred_element_type=F32)
        r = ALPHA * h_ref[...] + mix
        xhat, _ = _ln_stats(r)
        y = xhat * g_ref[...] + b_ref[...]
        r_ref[...] = r
        ho_ref[...] = y
        hb_ref[...] = y.astype(BF16)
        _plan_wait(plan, pctx, grid)

    p_in, p_ospec, p_oshape, p_scr, p_alias = _plan_io(plan, 5, 3)
    return pl.pallas_call(
        body, name=name, grid=grid,
        in_specs=[row, w_spec, row, vec, vec] + [_ANY] * len(p_in),
        out_specs=[row, row, row] + p_ospec,
        out_shape=[jax.ShapeDtypeStruct((T, D), F32), jax.ShapeDtypeStruct((T, D), F32),
                   jax.ShapeDtypeStruct((T, D), BF16)] + p_oshape,
        scratch_shapes=p_scr, input_output_aliases=p_alias, compiler_params=_params(40, 1),
    )(a_b, w, h_prev, g, b, *p_in)


def _ffn_ln(h_b, wbuf, h, g, b, *, name):
    T = h_b.shape[0]
    tm, tf = min(ROW_BLOCK, T), 1024
    nf = 4
    F = nf * tf
    row = pl.BlockSpec((tm, D), lambda i, j: (i, 0))
    vec = pl.BlockSpec((1, D), lambda i, j: (0, 0))

    def body(hb_ref, w1_ref, w2_ref, h_ref, g_ref, b_ref, ra_ref, r_ref, ho_ref, hbo_ref, acc_ref):
        j = pl.program_id(1)
        a = jnp.dot(hb_ref[...], w1_ref[...], preferred_element_type=F32)
        ra = jnp.maximum(a, 0.0)
        ra_ref[...] = ra.astype(BF16)
        p = jnp.dot((ra * ra).astype(BF16), w2_ref[...], preferred_element_type=F32)

        @pl.when(j == 0)
        def _():
            acc_ref[...] = p

        @pl.when(j > 0)
        def _():
            acc_ref[...] += p

        @pl.when(j == nf - 1)
        def _():
            r = ALPHA * h_ref[...] + acc_ref[...]
            xhat, _ = _ln_stats(r)
            y = xhat * g_ref[...] + b_ref[...]
            r_ref[...] = r
            ho_ref[...] = y
            hbo_ref[...] = y.astype(BF16)

    return pl.pallas_call(
        body, name=name, grid=(T // tm, nf),
        in_specs=[row, pl.BlockSpec((None, D, tf), lambda i, j: (j, 0, 0)),
                  pl.BlockSpec((None, tf, D), lambda i, j: (j, 1, 0)), row, vec, vec],
        out_specs=[pl.BlockSpec((tm, tf), lambda i, j: (i, j)), row, row, row],
        out_shape=[jax.ShapeDtypeStruct((T, F), BF16), jax.ShapeDtypeStruct((T, D), F32),
                   jax.ShapeDtypeStruct((T, D), F32), jax.ShapeDtypeStruct((T, D), BF16)],
        scratch_shapes=[pltpu.VMEM((tm, D), F32)],
        compiler_params=_params(48, 2),
    )(h_b, wbuf, wbuf, h, g, b)


def _loss_dy(y, tgt):
    T = y.shape[0]
    tm = min(ROW_BLOCK, T)
    row = pl.BlockSpec((tm, D), lambda i: (i, 0))

    def body(y_ref, t_ref, dy_ref, ls_ref):
        e = y_ref[...] - t_ref[...]
        dy_ref[...] = e * (1.0 / D)

        @pl.when(pl.program_id(0) == 0)
        def _():
            ls_ref[...] = jnp.zeros_like(ls_ref)

        ls_ref[...] += _fold8(e * e)

    return pl.pallas_call(
        body, name="loss_dy", grid=(T // tm,), in_specs=[row, row],
        out_specs=[row, pl.BlockSpec((8, D), lambda i: (0, 0))],
        out_shape=[jax.ShapeDtypeStruct((T, D), F32), jax.ShapeDtypeStruct((8, D), F32)],
        compiler_params=_params(32, 1),
    )(y, tgt)


def _ln_bwd(dy, r, g, *, name):
    T = dy.shape[0]
    tm = min(ROW_BLOCK, T)
    row = pl.BlockSpec((tm, D), lambda i: (i, 0))
    acc = pl.BlockSpec((8, D), lambda i: (0, 0))

    def body(dy_ref, r_ref, g_ref, dr_ref, drb_ref, dg_ref, db_ref):
        @pl.when(pl.program_id(0) == 0)
        def _():
            dg_ref[...] = jnp.zeros_like(dg_ref)
            db_ref[...] = jnp.zeros_like(db_ref)

        dy_ = dy_ref[...]
        xhat, rstd = _ln_stats(r_ref[...])
        dxh = dy_ * g_ref[...]
        m1 = jnp.mean(dxh, -1, keepdims=True)
        m2 = jnp.mean(dxh * xhat, -1, keepdims=True)
        dr = rstd * (dxh - m1 - xhat * m2)
        dr_ref[...] = dr
        drb_ref[...] = dr.astype(BF16)
        dg_ref[...] += _fold8(dy_ * xhat)
        db_ref[...] += _fold8(dy_)

    return pl.pallas_call(
        body, name=name, grid=(T // tm,),
        in_specs=[row, row, pl.BlockSpec((1, D), lambda i: (0, 0))],
        out_specs=[row, row, acc, acc],
        out_shape=[jax.ShapeDtypeStruct((T, D), F32), jax.ShapeDtypeStruct((T, D), BF16),
                   jax.ShapeDtypeStruct((8, D), F32), jax.ShapeDtypeStruct((8, D), F32)],
        compiler_params=_params(40, 1),
    )(dy, r, g)


def _rope(x, c, s1, s2):
    return x * c + pltpu.roll(x, 112, 1) * s1 + pltpu.roll(x, 16, 1) * s2


def _rope_t(dy, c, s1, s2):
    return dy * c + pltpu.roll(dy * s1, 16, 1) + pltpu.roll(dy * s2, 112, 1)


def _rms(x, g):
    rstd = lax.rsqrt(jnp.mean(x * x, -1, keepdims=True) + EPS)
    xhat = x * rstd
    return xhat * g, xhat, rstd


def _mla_prep(z0, gq, gkv, wq, wk, wv, rc, rs1, rs2):
    T = z0.shape[0]
    tm = min(ROW_BLOCK, T)
    HW = HEADS * 128

    def body(cq_ref, ckv_ref, kr_ref, gq_ref, gkv_ref, wq_ref, wk_ref, wv_ref, c_ref, s1_ref, s2_ref,
             q_ref, k_ref, v_ref):
        nq = _rms(cq_ref[...], gq_ref[...])[0].astype(BF16)
        nkv = _rms(ckv_ref[...], gkv_ref[...])[0].astype(BF16)
        q = jnp.dot(nq, wq_ref[...], preferred_element_type=F32)
        k = jnp.dot(nkv, wk_ref[...], preferred_element_type=F32)
        v = jnp.dot(nkv, wv_ref[...], preferred_element_type=F32)
        c, s1, s2 = c_ref[...], s1_ref[...], s2_ref[...]
        kr = _rope(pltpu.roll(kr_ref[...], 64, 1), c, s1, s2)
        for h in range(HEADS):
            sl = slice(h * 128, (h + 1) * 128)
            q_ref[:, sl] = (_rope(q[:, sl], c, s1, s2) * QK_SCALE).astype(BF16)
            k_ref[:, sl] = (k[:, sl] + kr).astype(BF16)
        v_ref[...] = v.astype(BF16)

    full = lambda shape: pl.BlockSpec(shape, lambda i: (0, 0))
    tab = pl.BlockSpec((tm, 128), lambda i: (i, 0))
    return pl.pallas_call(
        body, name="mla_prep", grid=(T // tm,),
        in_specs=[pl.BlockSpec((tm, 256), lambda i: (i, 0)), pl.BlockSpec((tm, 256), lambda i: (i, 1)),
                  pl.BlockSpec((tm, 128), lambda i: (i, 12)), full((1, 256)), full((1, 256)),
                  full((256, HW)), full((256, HW)), full((256, 512)), tab, tab, tab],
        out_specs=[pl.BlockSpec((tm, HW), lambda i: (i, 0)), pl.BlockSpec((tm, HW), lambda i: (i, 0)),
                   pl.BlockSpec((tm, 512), lambda i: (i, 0))],
        out_shape=[jax.ShapeDtypeStruct((T, HW), BF16), jax.ShapeDtypeStruct((T, HW), BF16),
                   jax.ShapeDtypeStruct((T, 512), BF16)],
        compiler_params=_params(40, 1),
    )(z0, z0, z0, gq, gkv, wq, wk, wv, rc, rs1, rs2)


def _flash_fwd(q, k, v, plan=None):
    T = q.shape[0]
    bq = min(2 * ROW_BLOCK, T)
    nq = T // bq
    grid = (4, nq, nq)

    def body(*refs):
        (q_ref, k_ref, v_ref), (o_ref, lse_ref), (m_sc, l_sc, acc_sc), pctx = _split_refs(refs, 3, 2, 3, plan)
        _plan_start(plan, pctx, grid)
        i, j = pl.program_id(1), pl.program_id(2)
        first = lax.broadcasted_iota(jnp.int32, (bq, 128), 1) < 64

        @pl.when(j == 0)
        def _():
            m_sc[...] = jnp.full_like(m_sc, -jnp.inf)
            l_sc[...] = jnp.zeros_like(l_sc)
            acc_sc[...] = jnp.zeros_like(acc_sc)

        def step(masked):
            vp = v_ref[...]
            acc = acc_sc[...]
            for h in range(2):
                sl = slice(h * 128, (h + 1) * 128)
                s = lax.dot_general(q_ref[:, sl], k_ref[:, sl], NT_DIMS, preferred_element_type=F32)
                if masked:
                    rows = lax.broadcasted_iota(jnp.int32, (bq, bq), 0)
                    cols = lax.broadcasted_iota(jnp.int32, (bq, bq), 1)
                    s = jnp.where(cols <= rows, s, -jnp.inf)
                m_prev = m_sc[h, :, 0:1]
                m_new = jnp.maximum(m_prev, jnp.max(s, axis=1, keepdims=True))
                alpha = jnp.exp(m_prev - m_new)
                p = jnp.exp(s - m_new)
                l_new = alpha * l_sc[h, :, 0:1] + jnp.sum(p, axis=1, keepdims=True)
                pv = jnp.dot(p.astype(BF16), vp, preferred_element_type=F32)
                mine = first if h == 0 else jnp.logical_not(first)
                acc = jnp.where(mine, acc * alpha + pv, acc)
                m_sc[h] = jnp.broadcast_to(m_new, (bq, 128))
                l_sc[h] = jnp.broadcast_to(l_new, (bq, 128))
            acc_sc[...] = acc

        @pl.when(j < i)
        def _():
            step(False)

        @pl.when(j == i)
        def _():
            step(True)
            l0, l1 = l_sc[0], l_sc[1]
            o_ref[...] = (acc_sc[...] / jnp.where(first, l0, l1)).astype(BF16)
            lse_ref[...] = jnp.where(first, m_sc[0] + jnp.log(l0), m_sc[1] + jnp.log(l1))

        _plan_wait(plan, pctx, grid)

    kv = lambda hp, i, j: (jnp.minimum(i, j), hp)
    p_in, p_ospec, p_oshape, p_scr, p_alias = _plan_io(plan, 3, 2)
    return pl.pallas_call(
        body, name="flash_fwd", grid=grid,
        in_specs=[pl.BlockSpec((bq, 256), lambda hp, i, j: (i, hp)), pl.BlockSpec((bq, 256), kv),
                  pl.BlockSpec((bq, 128), kv)] + [_ANY] * len(p_in),
        out_specs=[pl.BlockSpec((bq, 128), lambda hp, i, j: (i, hp)),
                   pl.BlockSpec((bq, 128), lambda hp, i, j: (i, hp))] + p_ospec,
        out_shape=[jax.ShapeDtypeStruct((T, 512), BF16), jax.ShapeDtypeStruct((T, 512), F32)] + p_oshape,
        scratch_shapes=[pltpu.VMEM((2, bq, 128), F32), pltpu.VMEM((2, bq, 128), F32), pltpu.VMEM((bq, 128), F32)] + p_scr,
        input_output_aliases=p_alias, compiler_params=_params(56, 3),
    )(q, k, v, *p_in)


def _attn_delta(dmix, o):
    T = o.shape[0]
    tm = min(ROW_BLOCK, T)
    blk = pl.BlockSpec((tm, 512), lambda i: (i, 0))

    def body(do_ref, o_ref, delta_ref, dob_ref):
        first = lax.broadcasted_iota(jnp.int32, (tm, 128), 1) < 64
        for hp in range(4):
            sl = slice(hp * 128, (hp + 1) * 128)
            prod = do_ref[:, sl] * o_ref[:, sl].astype(F32)
            d0 = jnp.sum(jnp.where(first, prod, 0.0), axis=1, keepdims=True)
            d1 = jnp.sum(jnp.where(first, 0.0, prod), axis=1, keepdims=True)
            delta_ref[:, sl] = jnp.where(first, d0, d1)
        dob_ref[...] = do_ref[...].astype(BF16)

    return pl.pallas_call(
        body, name="attn_delta", grid=(T // tm,), in_specs=[blk, blk], out_specs=[blk, blk],
        out_shape=[jax.ShapeDtypeStruct((T, 512), F32), jax.ShapeDtypeStruct((T, 512), BF16)],
        compiler_params=_params(32, 1),
    )(dmix, o)


def _flash_bwd(q, k, v, do_b, lse, delta, plan=None):
    T = q.shape[0]
    bq = min(2 * ROW_BLOCK, T)
    nq = T // bq
    grid = (4, nq, nq)

    def body(*refs):
        ((q_ref, k_ref, v_ref, do_ref, lse_ref, dl_ref), (dq_hbm, dk_ref, dv_ref), (dq_sc, dk_sc, dv_sc, sem),
         pctx) = _split_refs(refs, 6, 3, 4, plan)
        _plan_start(plan, pctx, grid)
        hp, j, i = pl.program_id(0), pl.program_id(1), pl.program_id(2)
        first = lax.broadcasted_iota(jnp.int32, (bq, 128), 1) < 64

        @pl.when((j == 0) & (i == 0))
        def _():
            dq_sc[...] = jnp.zeros_like(dq_sc)

        @pl.when(i == j)
        def _():
            dk_sc[...] = jnp.zeros_like(dk_sc)
            dv_sc[...] = jnp.zeros_like(dv_sc)

        def step(masked):
            vp = v_ref[...]
            do = do_ref[...]
            for h in range(2):
                sl = slice(h * 128, (h + 1) * 128)
                qh, kh = q_ref[:, sl], k_ref[:, sl]
                s = lax.dot_general(qh, kh, NT_DIMS, preferred_element_type=F32)
                p = jnp.exp(s - lse_ref[:, h * 64:h * 64 + 1])
                if masked:
                    rows = lax.broadcasted_iota(jnp.int32, (bq, bq), 0)
                    cols = lax.broadcasted_iota(jnp.int32, (bq, bq), 1)
                    p = jnp.where(cols <= rows, p, 0.0)
                mine = first if h == 0 else jnp.logical_not(first)
                do_h = jnp.where(mine, do, jnp.zeros_like(do))
                dv_sc[...] += lax.dot_general(p.astype(BF16), do_h, TN_DIMS, preferred_element_type=F32)
                dp = lax.dot_general(do_h, vp, NT_DIMS, preferred_element_type=F32)
                ds = (p * (dp - dl_ref[:, h * 64:h * 64 + 1])).astype(BF16)
                dq_sc[i, :, sl] += jnp.dot(ds, kh, preferred_element_type=F32)
                dk_sc[:, sl] += lax.dot_general(ds, qh, TN_DIMS, preferred_element_type=F32)

        @pl.when(i > j)
        def _():
            step(False)

        @pl.when(i == j)
        def _():
            step(True)

        @pl.when(i == nq - 1)
        def _():
            dk_ref[...] = dk_sc[...]
            dv_ref[...] = dv_sc[...]

        @pl.when((j == nq - 1) & (i == nq - 1))
        def _():
            cp = pltpu.make_async_copy(dq_sc, dq_hbm.at[hp], sem)
            cp.start()
            cp.wait()

        _plan_wait(plan, pctx, grid)

    qi = lambda hp, j, i: (jnp.maximum(i, j), hp)
    kj = lambda hp, j, i: (j, hp)
    p_in, p_ospec, p_oshape, p_scr, p_alias = _plan_io(plan, 6, 3)
    return pl.pallas_call(
        body, name="flash_bwd", grid=grid,
        in_specs=[pl.BlockSpec((bq, 256), qi), pl.BlockSpec((bq, 256), kj), pl.BlockSpec((bq, 128), kj),
                  pl.BlockSpec((bq, 128), qi), pl.BlockSpec((bq, 128), qi), pl.BlockSpec((bq, 128), qi)]
        + [_ANY] * len(p_in),
        out_specs=[_ANY, pl.BlockSpec((bq, 256), kj), pl.BlockSpec((bq, 128), kj)] + p_ospec,
        out_shape=[jax.ShapeDtypeStruct((4, nq, bq, 256), F32), jax.ShapeDtypeStruct((T, 1024), F32),
                   jax.ShapeDtypeStruct((T, 512), F32)] + p_oshape,
        scratch_shapes=[pltpu.VMEM((nq, bq, 256), F32), pltpu.VMEM((bq, 256), F32), pltpu.VMEM((bq, 128), F32),
                        pltpu.SemaphoreType.DMA] + p_scr,
        input_output_aliases=p_alias, compiler_params=_params(56, 3),
    )(q, k, v, do_b, lse, delta, *p_in)


def _mla_bwd(z0, dq4, dk, dv, gq, gkv, wq, wk, wv, rc, rs1, rs2, plan=None):
    T = z0.shape[0]
    tm = min(ROW_BLOCK, T)
    HW = HEADS * 128
    grid = (T // tm,)
    dq4 = dq4.reshape(4, T, 256)

    def body(*refs):
        ((cq_ref, ckv_ref, dq_ref, dk_ref, dv_ref, gq_ref, gkv_ref, wq_ref, wk_ref, wv_ref, c_ref, s1_ref, s2_ref),
         (dc_ref, dkr_ref, dwq_ref, dwk_ref, dwv_ref, dgq_ref, dgkv_ref), _, pctx) = _split_refs(refs, 13, 7, 0, plan)
        _plan_start(plan, pctx, grid)

        @pl.when(pl.program_id(0) == 0)
        def _():
            for ref in (dwq_ref, dwk_ref, dwv_ref, dgq_ref, dgkv_ref):
                ref[...] = jnp.zeros_like(ref)

        c, s1, s2 = c_ref[...], s1_ref[...], s2_ref[...]
        lane = lax.broadcasted_iota(jnp.int32, (tm, 128), 1)
        nq, xq, rq = _rms(cq_ref[...], gq_ref[...])
        nkv, xkv, rkv = _rms(ckv_ref[...], gkv_ref[...])
        nq_b, nkv_b = nq.astype(BF16), nkv.astype(BF16)

        dq_parts, dk_parts = [], []
        dkr = jnp.zeros((tm, 128), F32)
        for h in range(HEADS):
            blk = dq_ref[h // 2, :, (h % 2) * 128:(h % 2 + 1) * 128] * QK_SCALE
            dq_parts.append(_rope_t(blk, c, s1, s2).astype(BF16))
            kb = dk_ref[:, h * 128:(h + 1) * 128]
            dk_parts.append(jnp.where(lane < NOPE, kb, 0.0).astype(BF16))
            dkr = dkr + kb
        dq_b = jnp.concatenate(dq_parts, axis=1)
        dk_b = jnp.concatenate(dk_parts, axis=1)
        dv_b = dv_ref[...].astype(BF16)

        dwq_ref[...] += lax.dot_general(nq_b, dq_b, TN_DIMS, preferred_element_type=F32)
        dwk_ref[...] += lax.dot_general(nkv_b, dk_b, TN_DIMS, preferred_element_type=F32)
        dwv_ref[...] += lax.dot_general(nkv_b, dv_b, TN_DIMS, preferred_element_type=F32)
        dnq = lax.dot_general(dq_b, wq_ref[...], NT_DIMS, preferred_element_type=F32)
        dnkv = (lax.dot_general(dk_b, wk_ref[...], NT_DIMS, preferred_element_type=F32)
                + lax.dot_general(dv_b, wv_ref[...], NT_DIMS, preferred_element_type=F32))

        def rms_bwd(dn, xhat, rstd, g):
            dxh = dn * g
            return rstd * (dxh - xhat * jnp.mean(dxh * xhat, -1, keepdims=True))

        dc_ref[:, :256] = rms_bwd(dnq, xq, rq, gq_ref[...]).astype(BF16)
        dc_ref[:, 256:] = rms_bwd(dnkv, xkv, rkv, gkv_ref[...]).astype(BF16)
        dgq_ref[...] += _fold8(dnq * xq)
        dgkv_ref[...] += _fold8(dnkv * xkv)
        dkr = pltpu.roll(_rope_t(dkr, c, s1, s2), 64, 1)
        dkr_ref[...] = jnp.where(lane < ROPE, dkr, 0.0).astype(BF16)
        _plan_wait(plan, pctx, grid)

    full = lambda shape: pl.BlockSpec(shape, lambda i: (0,) * len(shape))
    tab = pl.BlockSpec((tm, 128), lambda i: (i, 0))
    p_in, p_ospec, p_oshape, p_scr, p_alias = _plan_io(plan, 13, 7)
    return pl.pallas_call(
        body, name="mla_bwd", grid=grid,
        in_specs=[pl.BlockSpec((tm, 256), lambda i: (i, 0)), pl.BlockSpec((tm, 256), lambda i: (i, 1)),
                  pl.BlockSpec((4, tm, 256), lambda i: (0, i, 0)),
                  pl.BlockSpec((tm, HW), lambda i: (i, 0)), pl.BlockSpec((tm, 512), lambda i: (i, 0)),
                  full((1, 256)), full((1, 256)), full((256, HW)), full((256, HW)), full((256, 512)), tab, tab, tab]
        + [_ANY] * len(p_in),
        out_specs=[pl.BlockSpec((tm, 512), lambda i: (i, 0)), tab, full((256, HW)), full((256, HW)),
                   full((256, 512)), full((8, 256)), full((8, 256))] + p_ospec,
        out_shape=[jax.ShapeDtypeStruct((T, 512), BF16), jax.ShapeDtypeStruct((T, 128), BF16),
                   jax.ShapeDtypeStruct((256, HW), F32), jax.ShapeDtypeStruct((256, HW), F32),
                   jax.ShapeDtypeStruct((256, 512), F32), jax.ShapeDtypeStruct((8, 256), F32),
                   jax.ShapeDtypeStruct((8, 256), F32)] + p_oshape,
        scratch_shapes=p_scr, input_output_aliases=p_alias, compiler_params=_params(48, 1),
    )(z0, z0, dq4, dk, dv, gq, gkv, wq, wk, wv, rc, rs1, rs2, *p_in)


def _sgu_fwd(z0, a_out, ln_g, ln_b, w, b_t):
    T = z0.shape[0]
    tm = min(ROW_BLOCK, T)
    W = SGU_G * SGU_C

    def body(u_ref, v_ref, a_ref, g_ref, b_ref, w_ref, bt_ref, o_ref):
        o_ref[:, :W] = a_ref[...]
        ug = _gelu(u_ref[...])
        xhat, _ = _ln_stats(_gelu(v_ref[...]))
        vn = (xhat * g_ref[...] + b_ref[...]).astype(BF16)
        tril = lax.broadcasted_iota(jnp.int32, (SGU_C, SGU_C), 0) >= lax.broadcasted_iota(jnp.int32, (SGU_C, SGU_C), 1)
        for g in range(SGU_G):
            cs = slice(g * SGU_C, (g + 1) * SGU_C)
            wg = jnp.where(tril, w_ref[g], 0.0).astype(BF16)
            bcol = bt_ref[:, g:g + 1]
            for c in range(tm // SGU_C):
                rs = slice(c * SGU_C, (c + 1) * SGU_C)
                mixed = jnp.dot(wg, vn[rs, cs], preferred_element_type=F32) + bcol
                o_ref[rs, W + g * SGU_C:W + (g + 1) * SGU_C] = (ug[rs, cs] * mixed).astype(BF16)

    full = lambda shape: pl.BlockSpec(shape, lambda i: (0,) * len(shape))
    return pl.pallas_call(
        body, name="sgu_fwd", grid=(T // tm,),
        in_specs=[pl.BlockSpec((tm, W), lambda i: (i, 1)), pl.BlockSpec((tm, W), lambda i: (i, 2)),
                  pl.BlockSpec((tm, W), lambda i: (i, 0)),
                  full((1, W)), full((1, W)), full((SGU_G, SGU_C, SGU_C)), full((SGU_C, SGU_G))],
        out_specs=pl.BlockSpec((tm, 2 * W), lambda i: (i, 0)),
        out_shape=jax.ShapeDtypeStruct((T, 2 * W), BF16),
        compiler_params=_params(32, 1),
    )(z0, z0, a_out, ln_g, ln_b, w, b_t)


def _sgu_bwd(z0, dmix, dc, dkr, ln_g, ln_b, w, b_t):
    T = z0.shape[0]
    tm = min(ROW_BLOCK, T)
    W = SGU_G * SGU_C

    def body(u_ref, v_ref, do_ref, dc_ref, dkr_ref, g_ref, b_ref, w_ref, bt_ref, dz_ref, dw_ref, db_ref, dlg_ref,
             dlb_ref):
        @pl.when(pl.program_id(0) == 0)
        def _():
            for ref in (dw_ref, db_ref, dlg_ref, dlb_ref):
                ref[...] = jnp.zeros_like(ref)

        dz_ref[:, :W] = dc_ref[...]
        dz_ref[:, 3 * W:] = dkr_ref[...]

        u, v, dout = u_ref[...], v_ref[...], do_ref[...]
        ug = _gelu(u)
        xhat, rstd = _ln_stats(_gelu(v))
        vn = (xhat * g_ref[...] + b_ref[...]).astype(BF16)
        dmixed = dout * ug
        dmixed_b = dmixed.astype(BF16)
        tril = lax.broadcasted_iota(jnp.int32, (SGU_C, SGU_C), 0) >= lax.broadcasted_iota(jnp.int32, (SGU_C, SGU_C), 1)
        lane = lax.broadcasted_iota(jnp.int32, (SGU_C, SGU_C), 1)
        dvn_cols = []
        for g in range(SGU_G):
            cs = slice(g * SGU_C, (g + 1) * SGU_C)
            wg = jnp.where(tril, w_ref[g], 0.0).astype(BF16)
            bcol = bt_ref[:, g:g + 1]
            dw_g = jnp.zeros((SGU_C, SGU_C), F32)
            db_g = jnp.zeros((SGU_C, 1), F32)
            dvn_rows = []
            for c in range(tm // SGU_C):
                rs = slice(c * SGU_C, (c + 1) * SGU_C)
                mixed = jnp.dot(wg, vn[rs, cs], preferred_element_type=F32) + bcol
                dz_ref[rs, W + g * SGU_C:W + (g + 1) * SGU_C] = (dout[rs, cs] * mixed * _gelu_grad(u[rs, cs])).astype(BF16)
                dm = dmixed_b[rs, cs]
                dw_g = dw_g + lax.dot_general(dm, vn[rs, cs], NT_DIMS, preferred_element_type=F32)
                db_g = db_g + jnp.sum(dmixed[rs, cs], axis=1, keepdims=True)
                dvn_rows.append(lax.dot_general(wg, dm, TN_DIMS, preferred_element_type=F32))
            dw_ref[g] += jnp.where(tril, dw_g, 0.0)
            db_ref[...] += jnp.where(lane == g, db_g, 0.0)
            dvn_cols.append(jnp.concatenate(dvn_rows, axis=0))
        dvn = jnp.concatenate(dvn_cols, axis=1)
        dxh = dvn * g_ref[...]
        m1 = jnp.mean(dxh, -1, keepdims=True)
        m2 = jnp.mean(dxh * xhat, -1, keepdims=True)
        dvg = rstd * (dxh - m1 - xhat * m2)
        dz_ref[:, 2 * W:3 * W] = (dvg * _gelu_grad(v)).astype(BF16)
        dlg_ref[...] += _fold8(dvn * xhat)
        dlb_ref[...] += _fold8(dvn)

    full = lambda shape: pl.BlockSpec(shape, lambda i: (0,) * len(shape))
    return pl.pallas_call(
        body, name="sgu_bwd", grid=(T // tm,),
        in_specs=[pl.BlockSpec((tm, W), lambda i: (i, 1)), pl.BlockSpec((tm, W), lambda i: (i, 2)),
                  pl.BlockSpec((tm, W), lambda i: (i, 1)), pl.BlockSpec((tm, W), lambda i: (i, 0)),
                  pl.BlockSpec((tm, 128), lambda i: (i, 0)),
                  full((1, W)), full((1, W)), full((SGU_G, SGU_C, SGU_C)), full((SGU_C, SGU_G))],
        out_specs=[pl.BlockSpec((tm, 3 * W + 128), lambda i: (i, 0)), full((SGU_G, SGU_C, SGU_C)),
                   full((SGU_C, SGU_C)), full((8, W)), full((8, W))],
        out_shape=[jax.ShapeDtypeStruct((T, 3 * W + 128), BF16), jax.ShapeDtypeStruct((SGU_G, SGU_C, SGU_C), F32),
                   jax.ShapeDtypeStruct((SGU_C, SGU_C), F32), jax.ShapeDtypeStruct((8, W), F32),
                   jax.ShapeDtypeStruct((8, W), F32)],
        compiler_params=_params(40, 1),
    )(z0, z0, dmix, dc, dkr, ln_g, ln_b, w, b_t)


def _hg_lower_bound(lb_ref):
    a0, a1 = lb_ref[0:1, :], lb_ref[1:2, :]
    m = jnp.maximum(a0, a1)
    e0, e1 = jnp.exp(a0 - m), jnp.exp(a1 - m)
    return e1 / (e0 + e1)


def _hg_chunk(qc, fc, lb):
    C = HG_CHUNK
    rows = lax.broadcasted_iota(jnp.int32, (C, C), 0)
    cols = lax.broadcasted_iota(jnp.int32, (C, C), 1)
    rowid = lax.broadcasted_iota(jnp.int32, (C, 128), 0)
    sq, sg = _sigmoid(qc), _sigmoid(fc)
    qf = qc * sq
    gate = lb + (1.0 - lb) * sg
    kk = 1.0 - gate
    lg = jnp.log(gate)
    bcum = jnp.dot((rows >= cols).astype(F32), lg, precision=HI, preferred_element_type=F32)
    b_mid = jnp.sum(jnp.where(rowid < C // 2, lg, 0.0), axis=0, keepdims=True)
    b_last = jnp.sum(lg, axis=0, keepdims=True)
    eq, ek, e, eh = jnp.exp(bcum - b_mid), jnp.exp(b_mid - bcum), jnp.exp(bcum), jnp.exp(b_last - bcum)
    qt, kt, qe, khat = qf * eq, kk * ek, qf * e, kk * eh
    a = lax.dot_general(qt.astype(BF16), kt.astype(BF16), NT_DIMS, preferred_element_type=F32)
    a = jnp.where(rows >= cols, a, 0.0)
    return dict(sq=sq, sg=sg, gate=gate, kk=kk, eq=eq, ek=ek, e=e, eh=eh, qt=qt, kt=kt, qe=qe, khat=khat, a=a,
                e_last=jnp.exp(b_last), tril=rows >= cols, rowid=rowid)


def _hgrn_fwd(z4, hg_lb, gnorm):
    T = z4.shape[1]
    tb = min(ROW_BLOCK, T)
    C = HG_CHUNK
    ncb = tb // C
    HPB = HG_HEADS_PER_STEP

    def body(q_ref, f_ref, i_ref, g_ref, lb_ref, gn_ref, y_ref, o_ref, st_ref, st_sc):
        @pl.when(pl.program_id(1) == 0)
        def _():
            st_sc[...] = jnp.zeros_like(st_sc)

        def chunk(c, carry):
            rs = pl.ds(pl.multiple_of(c * C, C), C)
            for hh in range(HPB):
                hs = slice(hh * 128, (hh + 1) * 128)
                lb = _hg_lower_bound(lb_ref.at[:, hs])
                v_b = i_ref[rs, hs].astype(BF16)
                gc = g_ref[rs, hs]
                x = _hg_chunk(q_ref[rs, hs], f_ref[rs, hs], lb)
                st = st_sc[hh]
                st_ref[hh, c] = st
                o = (jnp.dot(x["a"].astype(BF16), v_b, preferred_element_type=F32)
                     + lax.dot_general(x["qe"].astype(BF16), st.astype(BF16), NT_DIMS, preferred_element_type=F32))
                st_sc[hh] = st * x["e_last"] + lax.dot_general(v_b, x["khat"].astype(BF16), TN_DIMS,
                                                               preferred_element_type=F32)
                o_ref[rs, hs] = o
                n = o * lax.rsqrt(jnp.mean(o * o, -1, keepdims=True) + EPS)
                y_ref[rs, hs] = (n * gn_ref[:, hs] * (gc * _sigmoid(gc))).astype(BF16)
            return carry

        lax.fori_loop(0, ncb, chunk, 0)

    W = 128 * HPB
    zb = lambda k: pl.BlockSpec((None, tb, W), lambda h, t: (k, t, h))
    out = pl.BlockSpec((tb, W), lambda h, t: (t, h))
    return pl.pallas_call(
        body, name="hgrn_fwd", grid=(HEADS // HPB, T // tb),
        in_specs=[zb(0), zb(1), zb(2), zb(3), pl.BlockSpec((2, W), lambda h, t: (0, h)),
                  pl.BlockSpec((1, W), lambda h, t: (0, h))],
        out_specs=[out, out, pl.BlockSpec((HPB, ncb, 128, 128), lambda h, t: (h, t, 0, 0))],
        out_shape=[jax.ShapeDtypeStruct((T, D), BF16), jax.ShapeDtypeStruct((T, D), F32),
                   jax.ShapeDtypeStruct((HEADS, T // C, 128, 128), F32)],
        scratch_shapes=[pltpu.VMEM((HPB, 128, 128), F32)],
        compiler_params=_params(32, 2),
    )(z4, z4, z4, z4, hg_lb, gnorm)


def _hgrn_bwd(z4, o_raw, dy, states, hg_lb, gnorm):
    T = z4.shape[1]
    tb = min(ROW_BLOCK, T)
    C = HG_CHUNK
    ncb = tb // C
    nt = T // tb
    HPB = HG_HEADS_PER_STEP

    def body(q_ref, f_ref, i_ref, g_ref, o_ref, dy_ref, st_ref, lb_ref, gn_ref, dz_ref, dlb_ref, dgn_ref, dst_sc):
        @pl.when(pl.program_id(1) == 0)
        def _():
            dst_sc[...] = jnp.zeros_like(dst_sc)
            dlb_ref[...] = jnp.zeros_like(dlb_ref)
            dgn_ref[...] = jnp.zeros_like(dgn_ref)

        def chunk(cc, carry):
            for hh in range(HPB):
                one_head(ncb - 1 - cc, hh, slice(hh * 128, (hh + 1) * 128))
            return carry

        def one_head(c, hh, hs):
            rs = pl.ds(pl.multiple_of(c * C, C), C)
            lb = _hg_lower_bound(lb_ref.at[:, hs])
            gn = gn_ref[:, hs]
            qc, gc = q_ref[rs, hs], g_ref[rs, hs]
            v_b = i_ref[rs, hs].astype(BF16)
            x = _hg_chunk(qc, f_ref[rs, hs], lb)
            st, dst = st_ref[hh, c], dst_sc[hh]
            st_b, dst_b = st.astype(BF16), dst.astype(BF16)
            o, dyc = o_ref[rs, hs], dy_ref[rs, hs]
            sgg = _sigmoid(gc)
            sil = gc * sgg
            rstd = lax.rsqrt(jnp.mean(o * o, -1, keepdims=True) + EPS)
            n = o * rstd
            dgn_ref[:, hs] += _fold8(dyc * n * sil)
            dn = dyc * gn * sil
            do = rstd * (dn - n * jnp.mean(dn * n, -1, keepdims=True))
            dg = dyc * n * gn * (sgg * (1.0 + gc * (1.0 - sgg)))
            do_b = do.astype(BF16)
            da = jnp.where(x["tril"], lax.dot_general(do_b, v_b, NT_DIMS, preferred_element_type=F32), 0.0).astype(BF16)
            qt_b, kt_b, qe_b, khat_b = (x[n_].astype(BF16) for n_ in ("qt", "kt", "qe", "khat"))
            dv = (lax.dot_general(x["a"].astype(BF16), do_b, TN_DIMS, preferred_element_type=F32)
                  + lax.dot_general(khat_b, dst_b, NT_DIMS, preferred_element_type=F32))
            dqt = jnp.dot(da, kt_b, preferred_element_type=F32)
            dqe = jnp.dot(do_b, st_b, preferred_element_type=F32)
            dkt = lax.dot_general(da, qt_b, TN_DIMS, preferred_element_type=F32)
            dkhat = jnp.dot(v_b, dst_b, preferred_element_type=F32)
            dst_sc[hh] = lax.dot_general(do_b, qe_b, TN_DIMS, preferred_element_type=F32) + dst * x["e_last"]
            de_last = jnp.sum(st * dst, axis=0, keepdims=True)
            dqf = dqt * x["eq"] + dqe * x["e"]
            dkk = dkt * x["ek"] + dkhat * x["eh"]
            dkh_kh = dkhat * x["khat"]
            db = dqt * qt_b.astype(F32) - dkt * kt_b.astype(F32) + dqe * x["qe"] - dkh_kh
            db_last = jnp.sum(dkh_kh, axis=0, keepdims=True) + de_last * x["e_last"]
            db = db + jnp.where(x["rowid"] == C - 1, db_last, 0.0)
            rows = lax.broadcasted_iota(jnp.int32, (C, C), 0)
            cols = lax.broadcasted_iota(jnp.int32, (C, C), 1)
            dlg = jnp.dot((rows <= cols).astype(F32), db, precision=HI, preferred_element_type=F32)
            dgate = dlg / x["gate"] - dkk
            sg, sq = x["sg"], x["sq"]
            dlb_ref[:, hs] += _fold8(dgate * (1.0 - sg)) * (lb * (1.0 - lb))
            dz_ref[0, rs, hs] = (dqf * (sq * (1.0 + qc * (1.0 - sq)))).astype(BF16)
            dz_ref[1, rs, hs] = (dgate * (1.0 - lb) * sg * (1.0 - sg)).astype(BF16)
            dz_ref[2, rs, hs] = dv.astype(BF16)
            dz_ref[3, rs, hs] = dg.astype(BF16)

        lax.fori_loop(0, ncb, chunk, 0)

    W = 128 * HPB
    zb = lambda k: pl.BlockSpec((None, tb, W), lambda h, t: (k, nt - 1 - t, h))
    blk = pl.BlockSpec((tb, W), lambda h, t: (nt - 1 - t, h))
    acc = pl.BlockSpec((8, W), lambda h, t: (0, h))
    return pl.pallas_call(
        body, name="hgrn_bwd", grid=(HEADS // HPB, nt),
        in_specs=[zb(0), zb(1), zb(2), zb(3), blk, blk,
                  pl.BlockSpec((HPB, ncb, 128, 128), lambda h, t: (h, nt - 1 - t, 0, 0)),
                  pl.BlockSpec((2, W), lambda h, t: (0, h)), pl.BlockSpec((1, W), lambda h, t: (0, h))],
        out_specs=[pl.BlockSpec((4, tb, W), lambda h, t: (0, nt - 1 - t, h)), acc, acc],
        out_shape=[jax.ShapeDtypeStruct((4, T, D), BF16), jax.ShapeDtypeStruct((8, D), F32),
                   jax.ShapeDtypeStruct((8, D), F32)],
        scratch_shapes=[pltpu.VMEM((HPB, 128, 128), F32)],
        compiler_params=_params(32, 2),
    )(z4, z4, z4, z4, o_raw, dy, states, hg_lb, gnorm)


def _adamw(w, g, m, v, *, name):
    R, L = w.shape
    tr = R if R <= 512 else 512
    assert R % tr == 0
    blk = pl.BlockSpec((tr, L), lambda i: (i, 0))
    c1, c2 = 1.0 - B1 ** STEP, 1.0 - B2 ** STEP

    def body(w_ref, g_ref, m_ref, v_ref, d_ref, mo_ref, vo_ref):
        g_ = g_ref[...]
        m_ = B1 * m_ref[...] + (1.0 - B1) * g_
        v_ = B2 * v_ref[...] + (1.0 - B2) * (g_ * g_)
        d_ref[...] = -LR * ((m_ / c1) / (jnp.sqrt(v_ / c2) + ADAM_EPS) + WD * w_ref[...])
        mo_ref[...] = m_
        vo_ref[...] = v_

    sds = jax.ShapeDtypeStruct((R, L), F32)
    return pl.pallas_call(
        body, name=name, grid=(R // tr,), in_specs=[blk] * 4, out_specs=[blk] * 3, out_shape=[sds] * 3,
        compiler_params=_params(32, 1),
    )(w, g, m, v)


def _adamw_rows(w, m, v, gbufs, row0, *, name):
    L, R, C = w.shape
    tr = 256
    assert R % tr == 0 and row0 % tr == 0 and len(gbufs) == L
    blk = pl.BlockSpec((None, tr, C), lambda l, i: (l, i, 0))
    gblk = pl.BlockSpec((tr, C), lambda l, i: (row0 // tr + i, 0))
    c1, c2 = 1.0 - B1 ** STEP, 1.0 - B2 ** STEP

    def body(*refs):
        w_ref, m_ref, v_ref = refs[:3]
        g_refs = refs[3:3 + L]
        go_ref, d_ref, mo_ref, vo_ref = refs[3 + L:]
        g_ = g_refs[0][...]
        for l in range(1, L):
            g_ = jnp.where(pl.program_id(0) == l, g_refs[l][...], g_)
        m_ = B1 * m_ref[...] + (1.0 - B1) * g_
        v_ = B2 * v_ref[...] + (1.0 - B2) * (g_ * g_)
        go_ref[...] = g_
        d_ref[...] = -LR * ((m_ / c1) / (jnp.sqrt(v_ / c2) + ADAM_EPS) + WD * w_ref[...])
        mo_ref[...] = m_
        vo_ref[...] = v_

    sds = jax.ShapeDtypeStruct((L, R, C), F32)
    return pl.pallas_call(
        body, name=name, grid=(L, R // tr), in_specs=[blk] * 3 + [gblk] * L, out_specs=[blk] * 4, out_shape=[sds] * 4,
        compiler_params=_params(32, 2),
    )(w, m, v, *gbufs)


def _add_pairs(g, theirs, ids, *, name):
    n, R, L = theirs.shape
    tr = 128
    nb = R // tr

    def body(ids_ref, a_ref, b_ref, o_ref):
        o_ref[...] = (a_ref[...].astype(F32) + b_ref[...].astype(F32)).astype(BF16)

    blk = pl.BlockSpec((n, tr, L), lambda i, ids: (0, i, 0))
    return pl.pallas_call(
        body, name=name, out_shape=jax.ShapeDtypeStruct((n, R, L), BF16),
        grid_spec=pltpu.PrefetchScalarGridSpec(
            num_scalar_prefetch=1, grid=(nb,),
            in_specs=[pl.BlockSpec((n, tr, L), lambda i, ids: (0, ids[1] * nb + i, 0)), blk], out_specs=blk),
        compiler_params=_params(16, 1),
    )(ids, g, theirs)


def _sum_chips(pair, parts, ids, *, name):
    _, R, L = parts.shape
    tr = 128

    def body(ids_ref, o_ref, r_ref, out_ref):
        out_ref[...] = ((o_ref[...].astype(F32) + r_ref[0].astype(F32)) + r_ref[1].astype(F32)) + r_ref[2].astype(F32)

    return pl.pallas_call(
        body, name=name, out_shape=jax.ShapeDtypeStruct((2, R, L), F32),
        grid_spec=pltpu.PrefetchScalarGridSpec(
            num_scalar_prefetch=1, grid=(R // tr,),
            in_specs=[pl.BlockSpec((None, tr, L), lambda i, ids: (ids[0], i, 0)),
                      pl.BlockSpec((3, tr, L), lambda i, ids: (0, i, 0))],
            out_specs=pl.BlockSpec((None, tr, L), lambda i, ids: (ids[1], i, 0))),
        compiler_params=_params(32, 1),
    )(ids, pair, parts)


def _mesh_ids():
    x, y, c = _mesh_pos()
    return jnp.stack([2 * x + y, c]).astype(jnp.int32)


def _place_shard(rows, ids, *, name):
    R, L = rows.shape
    tr = 256

    def body(ids_ref, in_ref, out_ref):
        out_ref[...] = in_ref[...].astype(BF16)

    return pl.pallas_call(
        body, name=name, out_shape=jax.ShapeDtypeStruct((4, R, L), BF16),
        grid_spec=pltpu.PrefetchScalarGridSpec(
            num_scalar_prefetch=1, grid=(R // tr,), in_specs=[pl.BlockSpec((tr, L), lambda i, ids: (i, 0))],
            out_specs=pl.BlockSpec((None, tr, L), lambda i, ids: (ids[0], i, 0))),
        compiler_params=_params(16, 1),
    )(ids, rows)


def _remote(src, dst, send_sem, recv_sem, to):
    return pltpu.make_async_remote_copy(src_ref=src, dst_ref=dst, send_sem=send_sem, recv_sem=recv_sem,
                                        device_id=to, device_id_type=MESH_IDS)


def _rows(ref, lead, start, size):
    return ref.at[tuple(pl.ds(0, n) for n in ref.shape[:lead]) + (pl.ds(start, size),)]


def _other_chips():
    x, y, _ = _mesh_pos()
    return [(1 - x, y), (x, 1 - y), (1 - x, 1 - y)]


def _plan_gather_ici(bufs):
    n = len(bufs)

    def copies(outs, send, recv):
        x, y, c = _mesh_pos()
        res = []
        for b in range(n):
            half = bufs[b].shape[1] // 2
            mine = _rows(outs[b].at[2 * x + y], 0, c * half, half)
            for j, (cx, cy) in enumerate(_other_chips()):
                res.append((_remote(mine, mine, send(3 * b + j), recv(3 * b + j), (cx, cy, c)),
                            _remote(mine, _rows(outs[b].at[2 * cx + cy], 0, c * half, half),
                                    send(3 * b + j), recv(3 * b + j), (x, y, c))))
        return res

    def start(ins, outs, send, recv, loc):
        for out_cp, _ in copies(outs, send, recv):
            out_cp.start()

    def wait(ins, outs, send, recv, loc):
        for out_cp, in_cp in copies(outs, send, recv):
            in_cp.wait_recv()
            out_cp.wait_send()

    outs = [jax.ShapeDtypeStruct(b.shape, b.dtype) for b in bufs]
    return _Plan(bufs, outs, 3 * n, 0, start, wait, aliases={b: b for b in range(n)})


def _plan_gather_forward(bufs):
    n = len(bufs)

    def copies(outs, send, recv):
        x, y, c = _mesh_pos()
        res = []
        for b in range(n):
            half = bufs[b].shape[1] // 2
            for j, (cx, cy) in enumerate(_other_chips()):
                slot = outs[b].at[2 * cx + cy]
                res.append((_remote(_rows(slot, 0, c * half, half), _rows(slot, 0, c * half, half),
                                    send(3 * b + j), recv(3 * b + j), (x, y, 1 - c)),
                            _remote(_rows(slot, 0, c * half, half), _rows(slot, 0, (1 - c) * half, half),
                                    send(3 * b + j), recv(3 * b + j), (x, y, c))))
        return res

    def start(ins, outs, send, recv, loc):
        for out_cp, _ in copies(outs, send, recv):
            out_cp.start()

    def wait(ins, outs, send, recv, loc):
        for out_cp, in_cp in copies(outs, send, recv):
            in_cp.wait_recv()
            out_cp.wait_send()

    outs = [jax.ShapeDtypeStruct(b.shape, b.dtype) for b in bufs]
    return _Plan(bufs, outs, 3 * n, 0, start, wait, aliases={b: b for b in range(n)})


def _plan_pair_swap(g):
    half = g.shape[1] // 2

    def copy(ins, outs, send, recv, loc):
        x, y, c = _mesh_pos()
        return _remote(_rows(ins[0], 1, (1 - c) * half, half), outs[0], send(0), recv(0), (x, y, 1 - c))

    return _Plan([g], [jax.ShapeDtypeStruct((4, half, g.shape[2]), g.dtype)], 1, 0,
                 lambda *a: copy(*a).start(), lambda *a: copy(*a).wait())


def _plan_pair_gather(buf):
    def copies(ins, outs, send, recv, loc):
        x, y, c = _mesh_pos()
        return (_remote(outs[0].at[c], outs[0].at[c], send(0), recv(0), (x, y, 1 - c)),
                _remote(outs[0].at[c], outs[0].at[1 - c], send(0), recv(0), (x, y, c)))

    def wait(*a):
        out_cp, in_cp = copies(*a)
        in_cp.wait_recv()
        out_cp.wait_send()

    return _Plan([buf], [jax.ShapeDtypeStruct(buf.shape, buf.dtype)], 1, 0, lambda *a: copies(*a)[0].start(), wait,
                 aliases={0: 0})


def _plan_chip_scatter(p):
    def copies(ins, outs, send, recv, loc):
        _, _, c = _mesh_pos()
        return [_remote(ins[0].at[2 * cx + cy], outs[0].at[j], send(j), recv(j), (cx, cy, c))
                for j, (cx, cy) in enumerate(_other_chips())]

    def start(*a):
        for cp in copies(*a):
            cp.start()

    def wait(*a):
        for cp in copies(*a):
            cp.wait()

    return _Plan([p], [jax.ShapeDtypeStruct((3,) + p.shape[1:], p.dtype)], 3, 0, start, wait)


def _small_allreduce(vec):
    R, L = vec.shape

    def body(v_ref, sum_ref, all_ref, send_sems, recv_sems):
        x, y, c = _mesh_pos()
        me = 4 * x + 2 * y + c
        all_ref[me] = v_ref[...]
        cps = []
        for r in range(1, 8):
            peer = (x ^ (r >> 2), y ^ ((r >> 1) & 1), c ^ (r & 1))
            cps.append(pltpu.make_async_remote_copy(
                src_ref=v_ref, dst_ref=all_ref.at[me], send_sem=send_sems.at[r - 1], recv_sem=recv_sems.at[r - 1],
                device_id=peer, device_id_type=MESH_IDS))
        for cp in cps:
            cp.start()
        for r in range(1, 8):
            src = 4 * (x ^ (r >> 2)) + 2 * (y ^ ((r >> 1) & 1)) + (c ^ (r & 1))
            pltpu.make_async_remote_copy(
                src_ref=v_ref, dst_ref=all_ref.at[src], send_sem=send_sems.at[r - 1], recv_sem=recv_sems.at[r - 1],
                device_id=(x, y, c), device_id_type=MESH_IDS).wait_recv()
        for cp in cps:
            cp.wait_send()
        s = all_ref[0]
        for d in range(1, 8):
            s = s + all_ref[d]
        sum_ref[...] = s

    vm = pl.BlockSpec(memory_space=pltpu.VMEM)
    return pl.pallas_call(
        body, name="small_allreduce", in_specs=[vm], out_specs=[vm, vm],
        out_shape=[jax.ShapeDtypeStruct((R, L), F32), jax.ShapeDtypeStruct((8, R, L), F32)],
        scratch_shapes=[pltpu.SemaphoreType.DMA((7,)), pltpu.SemaphoreType.DMA((7,))],
        compiler_params=_params(16),
    )(vec)[0]


ROWS_L1, ROWS_L0, ROWS_ODD = 3328, 2048, 768
ODD_PARTS = (("w_out_e", (256, 1024)), ("w_in_e", (1024, 392)), ("w_qb", (256, 192)), ("w_kvb", (256, 256)))


def _odd_rows(parts, dtype, gnorm=None):
    rows = [parts[n].reshape(-1, 1024).astype(dtype) for n, _ in ODD_PARTS]
    used = sum(r.shape[0] for r in rows)
    if gnorm is not None:
        bits = lax.bitcast_convert_type(gnorm.reshape(-1), BF16).reshape(1, 512)
        rows.append(jnp.pad(bits, ((0, 0), (0, 512))))
        used += 1
    rows.append(jnp.zeros((ROWS_ODD - used, 1024), dtype))
    return jnp.concatenate(rows, axis=0)


def _odd_unrows(buf, with_gnorm=False):
    out, off = {}, 0
    for n, shape in ODD_PARTS:
        nr = math.prod(shape) // 1024
        out[n] = buf[off:off + nr].reshape(shape)
        off += nr
    if with_gnorm:
        out["hg_gnorm"] = lax.bitcast_convert_type(buf[off, :512].reshape(256, 2), F32).reshape(1, 256)
    return out


def _pack_small(vals):
    flat = jnp.concatenate([vals[n].reshape(-1).astype(F32) for n, _ in SMALL])
    return jnp.pad(flat, (0, SMALL_ROWS * 1024 - flat.shape[0])).reshape(SMALL_ROWS, 1024)


def _unpack_small(packed):
    flat = packed.reshape(-1)
    out, off = {}, 0
    for n, shape in SMALL:
        size = math.prod(shape)
        out[n] = flat[off:off + size].reshape(shape)
        off += size
    return out


def _rope_tables(positions):
    half = ROPE // 2
    inv_freq = ROPE_BASE ** (-jnp.arange(half, dtype=F32) / half)
    ang = positions.astype(F32).reshape(-1, 1) * inv_freq
    cos, sin = jnp.cos(ang), jnp.sin(ang)
    T = ang.shape[0]
    one, z16, z32 = jnp.ones((T, NOPE), F32), jnp.zeros((T, half), F32), jnp.zeros((T, 32), F32)
    z64 = jnp.zeros((T, NOPE), F32)
    c = jnp.concatenate([one, cos, cos, z32], axis=1)
    s1 = jnp.concatenate([z64, -sin, z16, z32], axis=1)
    s2 = jnp.concatenate([z64, z16, sin, z32], axis=1)
    return c, s1, s2


def _local_step(x, positions, tgt, odd, bufs, P, exchange):
    T = x.shape[0]
    row = lambda a: a.reshape(1, -1)
    rc, rs1, rs2 = _rope_tables(positions)
    blk = lambda f: pl.BlockSpec((None, D, D), f)

    w_in_e = odd["w_in_e"]
    w_in = jnp.concatenate([w_in_e[:, :512], w_in_e[:, 544:1568], w_in_e[:, 512:544], jnp.zeros((D, 96), BF16)], axis=1)
    wq = jnp.pad(odd["w_qb"].reshape(256, HEADS, NOPE + ROPE), ((0, 0), (0, 0), (0, 32))).reshape(256, HEADS * 128)
    kvb = odd["w_kvb"].reshape(256, HEADS, NOPE + VDIM)
    wk = jnp.pad(kvb[:, :, :NOPE], ((0, 0), (0, 0), (0, 64))).reshape(256, HEADS * 128)
    wv = kvb[:, :, NOPE:].reshape(256, HEADS * VDIM)
    w_out_e = odd["w_out_e"]
    sgu_w = P["sgu_w"][0]
    sgu_bt = P["sgu_b"][0].T
    gq, gkv = P["mla_gq"], P["mla_gkv"]
    gnorm = P["hg_gnorm"]

    x_b = x.astype(BF16)
    z0 = _matmul(x_b, w_in, name="in_proj_e", M=T, N=1664, K=D, tn=1664)[0]
    q, k, v = _mla_prep(z0, gq, gkv, wq, wk, wv, rc, rs1, rs2)
    if exchange:
        ids = _mesh_ids()
        placed = [_place_shard(b, ids, name=f"place_shard_{l}") for l, b in enumerate(bufs)]
        a_out, lse, wg0, wg1 = _flash_fwd(q, k, v, plan=_plan_gather_ici(placed))
    else:
        a_out, lse = _flash_fwd(q, k, v)
        wg0, wg1 = bufs
    mix0 = _sgu_fwd(z0, a_out, P["sgu_ln_g"], P["sgu_ln_b"], sgu_w, sgu_bt)
    res = _proj_ln(mix0, w_out_e, x, row(P["ln1_g"][0]), row(P["ln1_b"][0]), name="out_proj_ln_e",
                   plan=_plan_gather_forward([wg0, wg1]) if exchange else None)
    r1, h1, h1b = res[:3]
    if exchange:
        wg0, wg1 = res[3:]
    ra0, r2, h2, h2b = _ffn_ln(h1b, wg0, h1, row(P["ln2_g"][0]), row(P["ln2_b"][0]), name="ffn_ln_0")
    z4 = _matmul(h2b, wg1, name="in_proj_o", M=T, N=4 * D, K=D, b_spec=blk(lambda i, j, k: (j, 2, 0)),
                 out_shape=jax.ShapeDtypeStruct((4, T, D), F32),
                 o_spec=pl.BlockSpec((None, min(512, T), D), lambda i, j, k: (j, i, 0)))[0]
    y1, o_raw, states = _hgrn_fwd(z4, P["hg_lb"], gnorm)
    r3, h3, h3b = _proj_ln(y1, wg1, h2, row(P["ln1_g"][1]), row(P["ln1_b"][1]), name="out_proj_ln_o", w_rowblk=12)
    ra1, r4, h4, _ = _ffn_ln(h3b, wg1, h3, row(P["ln2_g"][1]), row(P["ln2_b"][1]), name="ffn_ln_1")
    dy, loss_parts = _loss_dy(h4, tgt)

    gs = {}
    ln1_g, ln1_b, ln2_g, ln2_b = [None, None], [None, None], [None, None], [None, None]

    def ffn_bwd(l, dh, r_out, ra, h_mid_b, g2, wg, rows, plan=None):
        dr, dr_b, dg, db = _ln_bwd(dh, r_out, row(g2), name=f"ln2_bwd_{l}")
        ln2_g[l], ln2_b[l] = dg.sum(0), db.sum(0)
        da, *extra = _matmul(dr_b, wg, tb=True, mul=ra, out_dtype=BF16, name=f"ffn_da_{l}", M=T, N=4 * D, K=D,
                             b_spec=blk(lambda i, j, k: (j, 1, 0)), plan=plan)
        gbuf = _matmul(ra, dr_b, ta=True, a_sq=True, name=f"ffn_dw2_{l}", M=4 * D, N=D, K=T, tm=1024, tk=512,
                       out_shape=jax.ShapeDtypeStruct((4, rows, D), BF16), o_spec=blk(lambda i, j, k: (i, 1, 0)))[0]
        gbuf = _matmul(h_mid_b, da, ta=True, name=f"ffn_dw1_{l}", M=D, N=4 * D, K=T, tm=1024, tk=512, into=gbuf,
                       out_shape=jax.ShapeDtypeStruct((4, rows, D), BF16), o_spec=blk(lambda i, j, k: (j, 0, 0)))[0]
        dh_mid = _matmul(da, wg, tb=True, add=dr, add_scale=ALPHA, name=f"ffn_dh_{l}", M=T, N=D, K=4 * D,
                         b_spec=blk(lambda i, j, k: (k, 0, 0)))[0]
        return dh_mid, gbuf, extra

    dh3, g1, _ = ffn_bwd(1, dy, r4, ra1, h3b, P["ln2_g"][1], wg1, ROWS_L1)
    dr3, dr3_b, dg, db = _ln_bwd(dh3, r3, row(P["ln1_g"][1]), name="ln1_bwd_1")
    ln1_g[1], ln1_b[1] = dg.sum(0), db.sum(0)
    g1_sds = jax.ShapeDtypeStruct((4, ROWS_L1, D), BF16)
    g1 = _matmul(y1, dr3_b, ta=True, name="dw_out_o", M=D, N=D, K=T, tm=256, tk=512, into=g1, out_shape=g1_sds,
                 o_spec=pl.BlockSpec((None, 256, D), lambda i, j, k: (i, 12, 0)))[0]
    dmix1 = _matmul(dr3_b, wg1, tb=True, name="dmix_o", M=T, N=D, K=D, b_spec=_rows4_spec(12, 3), b_merge=(D, D))[0]
    dz4, dlb, dgn = _hgrn_bwd(z4, o_raw, dmix1, states, P["hg_lb"], gnorm)
    g1 = _matmul(h2b, dz4, ta=True, name="dw_in_o", M=D, N=4 * D, K=T, tm=1024, tk=512, into=g1, out_shape=g1_sds,
                 b_spec=pl.BlockSpec((None, min(512, T), D), lambda i, j, k: (j, k, 0)),
                 o_spec=blk(lambda i, j, k: (j, 2, 0)))[0]
    dh2 = _matmul(dz4, wg1, tb=True, add=dr3, add_scale=ALPHA, name="dh_in_o", M=T, N=D, K=4 * D,
                  a_spec=pl.BlockSpec((None, min(512, T), D), lambda i, j, k: (k, i, 0)),
                  b_spec=blk(lambda i, j, k: (k, 2, 0)))[0]
    d_lb1 = dlb.sum(0)
    gs["hg_lb"] = jnp.stack([-d_lb1, d_lb1])
    gs["hg_gnorm"] = dgn.sum(0)[None]

    dh1, g0, swapped1 = ffn_bwd(0, dh2, r2, ra0, h1b, P["ln2_g"][0], wg0, ROWS_L0,
                                plan=_plan_pair_swap(g1) if exchange else None)
    dr1, dr1_b, dg, db = _ln_bwd(dh1, r1, row(P["ln1_g"][0]), name="ln1_bwd_0")
    ln1_g[0], ln1_b[0] = dg.sum(0), db.sum(0)
    godd = {"w_out_e": _matmul(mix0, dr1_b, ta=True, name="dw_out_e", M=D, N=D, K=T, tm=1024, tk=512)[0]}
    dmix0, *swapped0 = _matmul(dr1_b, w_out_e, tb=True, name="dmix_e", M=T, N=D, K=D,
                               plan=_plan_pair_swap(g0) if exchange else None)
    delta, do_b = _attn_delta(dmix0, a_out)
    if exchange:
        pair1 = _add_pairs(g1, swapped1[0], ids, name="grad_pair_add_1")
        pair0 = _add_pairs(g0, swapped0[0], ids, name="grad_pair_add_0")
        dq4, dk, dv, parts0, parts1 = _flash_bwd(
            q, k, v, do_b, lse, delta, plan=_join_plans([_plan_chip_scatter(pair0), _plan_chip_scatter(pair1)]))
        half0 = _sum_chips(pair0, parts0, ids, name="grad_chip_sum_0")
        half1 = _sum_chips(pair1, parts1, ids, name="grad_chip_sum_1")
        dc, dkr, dwq, dwk, dwv, dgq, dgkv, g0, g1 = _mla_bwd(
            z0, dq4, dk, dv, gq, gkv, wq, wk, wv, rc, rs1, rs2,
            plan=_join_plans([_plan_pair_gather(half0), _plan_pair_gather(half1)]))
        g0, g1 = g0.reshape(ROWS_L0, D), g1.reshape(ROWS_L1, D)
    else:
        dq4, dk, dv = _flash_bwd(q, k, v, do_b, lse, delta)
        dc, dkr, dwq, dwk, dwv, dgq, dgkv = _mla_bwd(z0, dq4, dk, dv, gq, gkv, wq, wk, wv, rc, rs1, rs2)
    dz0, dsw, dsb, dslg, dslb = _sgu_bwd(z0, dmix0, dc, dkr, P["sgu_ln_g"], P["sgu_ln_b"], sgu_w, sgu_bt)
    dw_in = _matmul(x_b, dz0, ta=True, name="dw_in_e", M=D, N=1664, K=T, tm=1024, tn=1664, tk=512)[0]
    godd["w_in_e"] = jnp.concatenate([dw_in[:, :512], dw_in[:, 1536:1568], dw_in[:, 512:1536]], axis=1)
    grad_x = _matmul(dz0, w_in, tb=True, add=dr1, add_scale=ALPHA, name="dx", M=T, N=D, K=1664, tk=1664)[0]

    godd["w_qb"] = dwq.reshape(256, HEADS, 128)[:, :, :NOPE + ROPE].reshape(256, HEADS * (NOPE + ROPE))
    godd["w_kvb"] = jnp.concatenate([dwk.reshape(256, HEADS, 128)[:, :, :NOPE], dwv.reshape(256, HEADS, VDIM)],
                                    axis=2).reshape(256, HEADS * (NOPE + VDIM))
    gs["mla_gq"], gs["mla_gkv"] = dgq.sum(0)[None], dgkv.sum(0)[None]
    gs["sgu_ln_g"], gs["sgu_ln_b"] = dslg.sum(0)[None], dslb.sum(0)[None]
    gs["sgu_w"], gs["sgu_b"] = dsw[None], dsb[:, :SGU_G].T[None]
    gs["ln1_g"], gs["ln1_b"] = jnp.stack(ln1_g), jnp.stack(ln1_b)
    gs["ln2_g"], gs["ln2_b"] = jnp.stack(ln2_g), jnp.stack(ln2_b)
    return loss_parts, grad_x, g0, g1, godd, gs


WEIGHTS = ['w_in_e', 'mla_gq', 'mla_gkv', 'w_qb', 'w_kvb', 'sgu_ln_g', 'sgu_ln_b', 'sgu_w', 'sgu_b', 'w_out_e',
           'w_in_o', 'hg_lb', 'hg_gnorm', 'w_out_o', 'ln1_g', 'ln1_b', 'w_ff1', 'w_ff2', 'ln2_g', 'ln2_b']


def kernel(x, positions, w_in_e, mla_gq, mla_gkv, w_qb, w_kvb, sgu_ln_g, sgu_ln_b, sgu_w, sgu_b, w_out_e, w_in_o, hg_lb, hg_gnorm, w_out_o, ln1_g, ln1_b, w_ff1, w_ff2, ln2_g, ln2_b, loss_target, m_w_in_e, m_mla_gq, m_mla_gkv, m_w_qb, m_w_kvb, m_sgu_ln_g, m_sgu_ln_b, m_sgu_w, m_sgu_b, m_w_out_e, m_w_in_o, m_hg_lb, m_hg_gnorm, m_w_out_o, m_ln1_g, m_ln1_b, m_w_ff1, m_w_ff2, m_ln2_g, m_ln2_b, v_w_in_e, v_mla_gq, v_mla_gkv, v_w_qb, v_w_kvb, v_sgu_ln_g, v_sgu_ln_b, v_sgu_w, v_sgu_b, v_w_out_e, v_w_in_o, v_hg_lb, v_hg_gnorm, v_w_out_o, v_ln1_g, v_ln1_b, v_w_ff1, v_w_ff2, v_ln2_g, v_ln2_b):
    args = dict(locals())
    w = {n: args[n] for n in WEIGHTS}
    m = {n: args["m_" + n] for n in WEIGHTS}
    v = {n: args["v_" + n] for n in WEIGHTS}
    cx, cy, cc = _mesh_pos()
    chip = 2 * cx + cy

    odd_shard = _odd_rows({"w_out_e": w_out_e[0], "w_in_e": w_in_e[0], "w_qb": w_qb[0], "w_kvb": w_kvb[0]}, BF16,
                          gnorm=hg_gnorm)
    ids = _mesh_ids()
    gathered = _run_plan(_plan_gather_ici([_place_shard(odd_shard, ids, name="place_shard_odd")]), name="odd_gather")[0]
    gathered = _run_plan(_plan_gather_forward([gathered]), name="odd_gather_forward")[0]
    per_chip = [_odd_unrows(gathered[j], with_gnorm=True) for j in range(4)]
    odd = {"w_out_e": jnp.concatenate([p["w_out_e"] for p in per_chip], axis=0)}
    for n in ("w_in_e", "w_qb", "w_kvb"):
        odd[n] = jnp.concatenate([p[n] for p in per_chip], axis=1)
    small = {n: w[n] for n, _ in SMALL if n != "hg_gnorm"}
    small["hg_gnorm"] = jnp.concatenate([p["hg_gnorm"] for p in per_chip], axis=1)
    rows_l0 = jnp.concatenate([w_ff1[0], w_ff2[0]], axis=0).astype(BF16)
    rows_l1 = jnp.concatenate([w_ff1[1], w_ff2[1], w_in_o[0], w_out_o[0]], axis=0).astype(BF16)

    loss_parts, grad_x, g_l0, g_l1, godd, gs = _local_step(x[0], positions[0], loss_target[0], odd, (rows_l0, rows_l1),
                                                            small, True)

    loss = lax.psum((0.5 / D) * jnp.sum(loss_parts), ("x", "y", "c"))

    by_chip = [_odd_rows({"w_out_e": jnp.split(godd["w_out_e"], 4, axis=0)[j],
                          **{n: jnp.split(godd[n], 4, axis=1)[j] for n in ("w_in_e", "w_qb", "w_kvb")}}, BF16)
               for j in range(4)]
    godd_buf = jnp.stack(by_chip)
    theirs = _run_plan(_plan_pair_swap(godd_buf), name="odd_pair_swap")[0]
    pair = _add_pairs(godd_buf, theirs, ids, name="odd_pair_add")
    parts = _run_plan(_plan_chip_scatter(pair), name="odd_chip_scatter")[0]
    g_odd = _run_plan(_plan_pair_gather(_sum_chips(pair, parts, ids, name="odd_chip_sum")), name="odd_pair_gather")[0]
    g_odd = _odd_unrows(g_odd.reshape(ROWS_ODD, 1024))

    g_small = _unpack_small(_small_allreduce(_pack_small(gs)))
    g_gnorm = lax.dynamic_slice_in_dim(g_small["hg_gnorm"], chip * 256, 256, axis=1)

    grads = {n: g_small[n] for n, _ in SMALL if n != "hg_gnorm"}
    grads["hg_gnorm"] = g_gnorm

    delta, new_m, new_v = {}, {}, {}
    for n, bufs_, row0 in (("w_ff1", [g_l0, g_l1], 0), ("w_ff2", [g_l0, g_l1], 1024), ("w_in_o", [g_l1], 2048),
                           ("w_out_o", [g_l1], 3072)):
        grads[n], delta[n], new_m[n], new_v[n] = _adamw_rows(w[n], m[n], v[n], bufs_, row0, name=f"adamw_{n}")
    for n, _ in ODD_PARTS:
        grads[n] = g_odd[n][None]
        d_, m_, v_ = _adamw(w[n][0], g_odd[n], m[n][0], v[n][0], name=f"adamw_{n}")
        delta[n], new_m[n], new_v[n] = d_[None], m_[None], v_[None]
    rest = [n for n in WEIGHTS if n not in delta]

    def pack_rest(d):
        flat = jnp.concatenate([d[n].reshape(-1) for n in rest])
        return jnp.pad(flat, (0, SMALL_ROWS * 1024 - flat.shape[0])).reshape(SMALL_ROWS, 1024)

    outs = _adamw(pack_rest(w), pack_rest(grads), pack_rest(m), pack_rest(v), name="adamw_small")
    for dst, packed in zip((delta, new_m, new_v), outs):
        flat, off = packed.reshape(-1), 0
        for n in rest:
            size = math.prod(w[n].shape)
            dst[n] = flat[off:off + size].reshape(w[n].shape)
            off += size

    return (loss, grad_x[None], *[grads[n] for n in WEIGHTS], *[delta[n] for n in WEIGHTS],
            *[new_m[n] for n in WEIGHTS], *[new_v[n] for n in WEIGHTS])
```

```python
import functools
import math

import jax
import jax.numpy as jnp
from jax import lax
from jax.experimental import pallas as pl
from jax.experimental.pallas import tpu as pltpu

F32 = jnp.float32
BF16 = jnp.bfloat16
MESH_IDS = pl.DeviceIdType.MESH

D = 1024
DEPTH = 2
HEADS = 8
NOPE, ROPE, VDIM = 64, 32, 64
QK_SCALE = (NOPE + ROPE) ** -0.5
ROPE_BASE = 10000.0
SGU_G, SGU_C = 4, 128
HG_CHUNK = 64
HG_HEADS_PER_STEP = 4
ALPHA = (2 * DEPTH) ** 0.25
EPS = 1e-5
LR, B1, B2, ADAM_EPS, WD, STEP = 0.001, 0.9, 0.999, 1e-08, 0.01, 10
GELU_C = math.sqrt(2.0 / math.pi)
GELU_A = 0.044715
HI = lax.Precision.HIGHEST
MB = 1024 * 1024
ROW_BLOCK = 512

NT_DIMS = (((1,), (1,)), ((), ()))
TN_DIMS = (((0,), (0,)), ((), ()))

SHARDED = (
    ("w_in_e", (1, 1024, 392), 2), ("w_qb", (1, 256, 192), 2), ("w_kvb", (1, 256, 256), 2),
    ("w_out_e", (1, 256, 1024), 1), ("w_in_o", (1, 1024, 1024), 2), ("w_out_o", (1, 256, 1024), 1),
    ("w_ff1", (2, 1024, 1024), 2), ("w_ff2", (2, 1024, 1024), 1), ("hg_gnorm", (1, 256), 1),
)
PACK_ROWS = 6144
HALF_ROWS = PACK_ROWS // 2
SMALL = (("mla_gq", (1, 256)), ("mla_gkv", (1, 256)), ("sgu_ln_g", (1, 512)), ("sgu_ln_b", (1, 512)),
         ("sgu_w", (1, 4, 128, 128)), ("sgu_b", (1, 4, 128)), ("hg_lb", (2, 1024)), ("hg_gnorm", (1, 1024)),
         ("ln1_g", (2, 1024)), ("ln1_b", (2, 1024)), ("ln2_g", (2, 1024)), ("ln2_b", (2, 1024)))
SMALL_ROWS = 80


def _params(vmem_mb, n_axes=0):
    kw = dict(vmem_limit_bytes=vmem_mb * MB)
    if n_axes:
        kw["dimension_semantics"] = ("arbitrary",) * n_axes
    return pltpu.CompilerParams(**kw)


_ANY = pl.BlockSpec(memory_space=pl.ANY)


def _mesh_pos():
    return lax.axis_index("x"), lax.axis_index("y"), lax.axis_index("c")


class _Plan:
    def __init__(self, ins, outs, n_remote, n_local, start, wait, aliases=None):
        self.ins, self.outs, self.n_remote, self.n_local = list(ins), list(outs), n_remote, n_local
        self.start, self.wait, self.aliases = start, wait, dict(aliases or {})


def _join_plans(plans):
    ins, outs, aliases, parts = [], [], {}, []
    nr = nl = 0
    for p in plans:
        parts.append((p, len(ins), len(outs), nr, nl))
        aliases.update({len(ins) + i: len(outs) + o for i, o in p.aliases.items()})
        ins += p.ins
        outs += p.outs
        nr += p.n_remote
        nl += p.n_local

    def run(which):
        def go(in_refs, out_refs, send, recv, loc):
            for p, i0, o0, r0, l0 in parts:
                getattr(p, which)(in_refs[i0:i0 + len(p.ins)], out_refs[o0:o0 + len(p.outs)],
                                  lambda i, r0=r0: send(r0 + i), lambda i, r0=r0: recv(r0 + i),
                                  lambda i, l0=l0: loc(l0 + i))
        return go

    return _Plan(ins, outs, nr, nl, run("start"), run("wait"), aliases)


def _plan_io(plan, n_in, n_out):
    if plan is None:
        return [], [], [], [], {}
    sems = [pltpu.SemaphoreType.DMA((max(plan.n_remote, 1),)), pltpu.SemaphoreType.DMA((max(plan.n_remote, 1),)),
            pltpu.SemaphoreType.DMA((max(plan.n_local, 1),))]
    aliases = {n_in + i: n_out + o for i, o in plan.aliases.items()}
    return plan.ins, [_ANY] * len(plan.outs), plan.outs, sems, aliases


def _split_refs(refs, n_in, n_out, n_scr, plan):
    p_in, p_out = (len(plan.ins), len(plan.outs)) if plan is not None else (0, 0)
    refs = list(refs)
    ins, refs = refs[:n_in], refs[n_in:]
    pins, refs = refs[:p_in], refs[p_in:]
    outs, refs = refs[:n_out], refs[n_out:]
    pouts, refs = refs[:p_out], refs[p_out:]
    scr, psem = refs[:n_scr], refs[n_scr:]
    psem = tuple((lambda i, s=s: s.at[i]) for s in psem)
    return ins, outs, scr, (pins, pouts, psem)


def _grid_edge(grid, last):
    cond = None
    for ax, n in enumerate(grid):
        c = pl.program_id(ax) == (n - 1 if last else 0)
        cond = c if cond is None else cond & c
    return cond


def _plan_start(plan, pctx, grid):
    if plan is not None:
        pins, pouts, psem = pctx
        pl.when(_grid_edge(grid, False))(lambda: plan.start(pins, pouts, *psem))


def _plan_wait(plan, pctx, grid):
    if plan is not None:
        pins, pouts, psem = pctx
        pl.when(_grid_edge(grid, True))(lambda: plan.wait(pins, pouts, *psem))


def _run_plan(plan, *, name):
    def body(*refs):
        _, _, _, (pins, pouts, psem) = _split_refs(refs, 0, 0, 0, plan)
        plan.start(pins, pouts, *psem)
        plan.wait(pins, pouts, *psem)

    p_in, p_ospec, p_oshape, p_scr, p_alias = _plan_io(plan, 0, 0)
    return pl.pallas_call(body, name=name, in_specs=[_ANY] * len(p_in), out_specs=p_ospec, out_shape=p_oshape,
                          scratch_shapes=p_scr, input_output_aliases=p_alias)(*p_in)


def _fold8(x):
    return x.reshape(x.shape[0] // 8, 8, x.shape[1]).sum(axis=0)


def _ln_stats(r):
    mu = jnp.mean(r, -1, keepdims=True)
    xc = r - mu
    rstd = lax.rsqrt(jnp.mean(xc * xc, -1, keepdims=True) + EPS)
    return xc * rstd, rstd


def _sigmoid(x):
    return 1.0 / (1.0 + jnp.exp(-x))


def _gelu(x):
    return 0.5 * x * (1.0 + jnp.tanh(GELU_C * (x + GELU_A * x * x * x)))


def _gelu_grad(x):
    t = jnp.tanh(GELU_C * (x + GELU_A * x * x * x))
    return 0.5 * (1.0 + t) + 0.5 * x * (1.0 - t * t) * GELU_C * (1.0 + 3.0 * GELU_A * x * x)


MM_ROWS = 1024
DW_TOKENS = 2048


def _matmul(a, b, *, name, M, N, K, ta=False, tb=False, out_dtype=F32, tm=MM_ROWS, tn=1024, tk=1024,
            a_spec=None, b_spec=None, b_merge=None, out_shape=None, o_spec=None, into=None,
            a_sq=False, mul=None, add=None, add_scale=1.0, plan=None):
    tm, tn, tk = min(tm, M), min(tn, N), min(tk, K)
    assert M % tm == 0 and N % tn == 0 and K % tk == 0
    grid = (M // tm, N // tn, K // tk)
    nk = grid[2]
    if a_spec is None:
        a_spec = pl.BlockSpec((tk, tm), lambda i, j, k: (k, i)) if ta else pl.BlockSpec((tm, tk), lambda i, j, k: (i, k))
    if b_spec is None:
        b_spec = pl.BlockSpec((tn, tk), lambda i, j, k: (j, k)) if tb else pl.BlockSpec((tk, tn), lambda i, j, k: (k, j))
    if o_spec is None:
        o_spec = pl.BlockSpec((tm, tn), lambda i, j, k: (i, j))
        out_shape = jax.ShapeDtypeStruct((M, N), out_dtype)
    e_spec = pl.BlockSpec((tm, tn), lambda i, j, k: (i, j))
    dims = (((0 if ta else 1,), (1 if tb else 0,)), ((), ()))
    extra = [e for e in (mul, add, into) if e is not None]
    n_in = 2 + len(extra)

    def body(*refs):
        ins, outs, scr, pctx = _split_refs(refs, n_in, 1, 1 if nk > 1 else 0, plan)
        a_ref, b_ref = ins[0], ins[1]
        rest = list(ins[2:])
        mul_ref = rest.pop(0) if mul is not None else None
        add_ref = rest.pop(0) if add is not None else None
        o_ref = outs[0]
        _plan_start(plan, pctx, grid)
        av = a_ref[...]
        if a_sq:
            av = av * av
        bv = b_ref[...]
        if b_merge is not None:
            bv = bv.reshape(b_merge)
        p = lax.dot_general(av, bv, dims, preferred_element_type=F32)

        def finish(r):
            if mul_ref is not None:
                r = r * (2.0 * mul_ref[...].astype(F32))
            if add_ref is not None:
                r = r + add_scale * add_ref[...]
            o_ref[...] = r.astype(o_ref.dtype)

        if nk == 1:
            finish(p)
        else:
            acc_ref = scr[0]
            k = pl.program_id(2)

            @pl.when(k == 0)
            def _():
                acc_ref[...] = p

            @pl.when(k > 0)
            def _():
                acc_ref[...] += p

            @pl.when(k == nk - 1)
            def _():
                finish(acc_ref[...])

        _plan_wait(plan, pctx, grid)

    p_in, p_ospec, p_oshape, p_scr, p_alias = _plan_io(plan, n_in, 1)
    aliases = dict(p_alias)
    if into is not None:
        aliases[n_in - 1] = 0
    return pl.pallas_call(
        body, name=name, grid=grid,
        in_specs=[a_spec, b_spec] + [e_spec] * (len(extra) - (into is not None)) + [_ANY] * (into is not None)
        + [_ANY] * len(p_in),
        out_specs=[o_spec] + p_ospec, out_shape=[out_shape] + p_oshape,
        scratch_shapes=([pltpu.VMEM((tm, tn), F32)] if nk > 1 else []) + p_scr,
        input_output_aliases=aliases, compiler_params=_params(48, 3),
    )(a, b, *extra, *p_in)


def _rows4_spec(rowblk, n_axes):
    return pl.BlockSpec((4, 256, D), lambda *_: (0, rowblk, 0))


def _proj_ln(a_b, w, h_prev, g, b, *, name, w_rowblk=None, plan=None):
    T = a_b.shape[0]
    tm = min(ROW_BLOCK, T)
    grid = (T // tm,)
    row = pl.BlockSpec((tm, D), lambda i: (i, 0))
    vec = pl.BlockSpec((1, D), lambda i: (0, 0))
    w_spec = pl.BlockSpec((D, D), lambda i: (0, 0)) if w_rowblk is None else _rows4_spec(w_rowblk, 1)

    def body(*refs):
        (a_ref, w_ref, h_ref, g_ref, b_ref), (r_ref, ho_ref, hb_ref), _, pctx = _split_refs(refs, 5, 3, 0, plan)
        _plan_start(plan, pctx, grid)
        mix = jnp.dot(a_ref[...], w_ref[...].reshape(D, D), preferred_element_type=F32)
        r = ALPHA * h_ref[...] + mix
        xhat, _ = _ln_stats(r)
        y = xhat * g_ref[...] + b_ref[...]
        r_ref[...] = r
        ho_ref[...] = y
        hb_ref[...] = y.astype(BF16)
        _plan_wait(plan, pctx, grid)

    p_in, p_ospec, p_oshape, p_scr, p_alias = _plan_io(plan, 5, 3)
    return pl.pallas_call(
        body, name=name, grid=grid,
        in_specs=[row, w_spec, row, vec, vec] + [_ANY] * len(p_in),
        out_specs=[row, row, row] + p_ospec,
        out_shape=[jax.ShapeDtypeStruct((T, D), F32), jax.ShapeDtypeStruct((T, D), F32),
                   jax.ShapeDtypeStruct((T, D), BF16)] + p_oshape,
        scratch_shapes=p_scr, input_output_aliases=p_alias, compiler_params=_params(40, 1),
    )(a_b, w, h_prev, g, b, *p_in)


def _ffn_ln(h_b, wbuf, h, g, b, *, name):
    T = h_b.shape[0]
    tm, tf = min(ROW_BLOCK, T), 1024
    nf = 4
    F = nf * tf
    row = pl.BlockSpec((tm, D), lambda i, j: (i, 0))
    vec = pl.BlockSpec((1, D), lambda i, j: (0, 0))

    def body(hb_ref, w1_ref, w2_ref, h_ref, g_ref, b_ref, ra_ref, r_ref, ho_ref, hbo_ref, acc_ref):
        j = pl.program_id(1)
        a = jnp.dot(hb_ref[...], w1_ref[...], preferred_element_type=F32)
        ra = jnp.maximum(a, 0.0)
        ra_ref[...] = ra.astype(BF16)
        p = jnp.dot((ra * ra).astype(BF16), w2_ref[...], preferred_element_type=F32)

        @pl.when(j == 0)
        def _():
            acc_ref[...] = p

        @pl.when(j > 0)
        def _():
            acc_ref[...] += p

        @pl.when(j == nf - 1)
        def _():
            r = ALPHA * h_ref[...] + acc_ref[...]
            xhat, _ = _ln_stats(r)
            y = xhat * g_ref[...] + b_ref[...]
            r_ref[...] = r
            ho_ref[...] = y
            hbo_ref[...] = y.astype(BF16)

    return pl.pallas_call(
        body, name=name, grid=(T // tm, nf),
        in_specs=[row, pl.BlockSpec((None, D, tf), lambda i, j: (j, 0, 0)),
                  pl.BlockSpec((None, tf, D), lambda i, j: (j, 1, 0)), row, vec, vec],
        out_specs=[pl.BlockSpec((tm, tf), lambda i, j: (i, j)), row, row, row],
        out_shape=[jax.ShapeDtypeStruct((T, F), BF16), jax.ShapeDtypeStruct((T, D), F32),
                   jax.ShapeDtypeStruct((T, D), F32), jax.ShapeDtypeStruct((T, D), BF16)],
        scratch_shapes=[pltpu.VMEM((tm, D), F32)],
        compiler_params=_params(48, 2),
    )(h_b, wbuf, wbuf, h, g, b)


def _loss_dy(y, tgt):
    T = y.shape[0]
    tm = min(ROW_BLOCK, T)
    row = pl.BlockSpec((tm, D), lambda i: (i, 0))

    def body(y_ref, t_ref, dy_ref, ls_ref):
        e = y_ref[...] - t_ref[...]
        dy_ref[...] = e * (1.0 / D)

        @pl.when(pl.program_id(0) == 0)
        def _():
            ls_ref[...] = jnp.zeros_like(ls_ref)

        ls_ref[...] += _fold8(e * e)

    return pl.pallas_call(
        body, name="loss_dy", grid=(T // tm,), in_specs=[row, row],
        out_specs=[row, pl.BlockSpec((8, D), lambda i: (0, 0))],
        out_shape=[jax.ShapeDtypeStruct((T, D), F32), jax.ShapeDtypeStruct((8, D), F32)],
        compiler_params=_params(32, 1),
    )(y, tgt)


def _ln_bwd(dy, r, g, *, name):
    T = dy.shape[0]
    tm = min(ROW_BLOCK, T)
    row = pl.BlockSpec((tm, D), lambda i: (i, 0))
    acc = pl.BlockSpec((8, D), lambda i: (0, 0))

    def body(dy_ref, r_ref, g_ref, dr_ref, drb_ref, dg_ref, db_ref):
        @pl.when(pl.program_id(0) == 0)
        def _():
            dg_ref[...] = jnp.zeros_like(dg_ref)
            db_ref[...] = jnp.zeros_like(db_ref)

        dy_ = dy_ref[...]
        xhat, rstd = _ln_stats(r_ref[...])
        dxh = dy_ * g_ref[...]
        m1 = jnp.mean(dxh, -1, keepdims=True)
        m2 = jnp.mean(dxh * xhat, -1, keepdims=True)
        dr = rstd * (dxh - m1 - xhat * m2)
        dr_ref[...] = dr
        drb_ref[...] = dr.astype(BF16)
        dg_ref[...] += _fold8(dy_ * xhat)
        db_ref[...] += _fold8(dy_)

    return pl.pallas_call(
        body, name=name, grid=(T // tm,),
        in_specs=[row, row, pl.BlockSpec((1, D), lambda i: (0, 0))],
        out_specs=[row, row, acc, acc],
        out_shape=[jax.ShapeDtypeStruct((T, D), F32), jax.ShapeDtypeStruct((T, D), BF16),
                   jax.ShapeDtypeStruct((8, D), F32), jax.ShapeDtypeStruct((8, D), F32)],
        compiler_params=_params(40, 1),
    )(dy, r, g)


def _rope(x, c, s1, s2):
    return x * c + pltpu.roll(x, 112, 1) * s1 + pltpu.roll(x, 16, 1) * s2


def _rope_t(dy, c, s1, s2):
    return dy * c + pltpu.roll(dy * s1, 16, 1) + pltpu.roll(dy * s2, 112, 1)


def _rms(x, g):
    rstd = lax.rsqrt(jnp.mean(x * x, -1, keepdims=True) + EPS)
    xhat = x * rstd
    return xhat * g, xhat, rstd


def _mla_prep(z0, gq, gkv, wq, wk, wv, rc, rs1, rs2):
    T = z0.shape[0]
    tm = min(ROW_BLOCK, T)
    HW = HEADS * 128

    def body(cq_ref, ckv_ref, kr_ref, gq_ref, gkv_ref, wq_ref, wk_ref, wv_ref, c_ref, s1_ref, s2_ref,
             q_ref, k_ref, v_ref):
        nq = _rms(cq_ref[...], gq_ref[...])[0].astype(BF16)
        nkv = _rms(ckv_ref[...], gkv_ref[...])[0].astype(BF16)
        q = jnp.dot(nq, wq_ref[...], preferred_element_type=F32)
        k = jnp.dot(nkv, wk_ref[...], preferred_element_type=F32)
        v = jnp.dot(nkv, wv_ref[...], preferred_element_type=F32)
        c, s1, s2 = c_ref[...], s1_ref[...], s2_ref[...]
        kr = _rope(pltpu.roll(kr_ref[...], 64, 1), c, s1, s2)
        for h in range(HEADS):
            sl = slice(h * 128, (h + 1) * 128)
            q_ref[:, sl] = (_rope(q[:, sl], c, s1, s2) * QK_SCALE).astype(BF16)
            k_ref[:, sl] = (k[:, sl] + kr).astype(BF16)
        v_ref[...] = v.astype(BF16)

    full = lambda shape: pl.BlockSpec(shape, lambda i: (0, 0))
    tab = pl.BlockSpec((tm, 128), lambda i: (i, 0))
    return pl.pallas_call(
        body, name="mla_prep", grid=(T // tm,),
        in_specs=[pl.BlockSpec((tm, 256), lambda i: (i, 0)), pl.BlockSpec((tm, 256), lambda i: (i, 1)),
                  pl.BlockSpec((tm, 128), lambda i: (i, 12)), full((1, 256)), full((1, 256)),
                  full((256, HW)), full((256, HW)), full((256, 512)), tab, tab, tab],
        out_specs=[pl.BlockSpec((tm, HW), lambda i: (i, 0)), pl.BlockSpec((tm, HW), lambda i: (i, 0)),
                   pl.BlockSpec((tm, 512), lambda i: (i, 0))],
        out_shape=[jax.ShapeDtypeStruct((T, HW), BF16), jax.ShapeDtypeStruct((T, HW), BF16),
                   jax.ShapeDtypeStruct((T, 512), BF16)],
        compiler_params=_params(40, 1),
    )(z0, z0, z0, gq, gkv, wq, wk, wv, rc, rs1, rs2)


def _flash_fwd(q, k, v, plan=None):
    T = q.shape[0]
    bq = min(2 * ROW_BLOCK, T)
    nq = T // bq
    grid = (4, nq, nq)

    def body(*refs):
        (q_ref, k_ref, v_ref), (o_ref, lse_ref), (m_sc, l_sc, acc_sc), pctx = _split_refs(refs, 3, 2, 3, plan)
        _plan_start(plan, pctx, grid)
        i, j = pl.program_id(1), pl.program_id(2)
        first = lax.broadcasted_iota(jnp.int32, (bq, 128), 1) < 64

        @pl.when(j == 0)
        def _():
            m_sc[...] = jnp.full_like(m_sc, -jnp.inf)
            l_sc[...] = jnp.zeros_like(l_sc)
            acc_sc[...] = jnp.zeros_like(acc_sc)

        def step(masked):
            vp = v_ref[...]
            acc = acc_sc[...]
            for h in range(2):
                sl = slice(h * 128, (h + 1) * 128)
                s = lax.dot_general(q_ref[:, sl], k_ref[:, sl], NT_DIMS, preferred_element_type=F32)
                if masked:
                    rows = lax.broadcasted_iota(jnp.int32, (bq, bq), 0)
                    cols = lax.broadcasted_iota(jnp.int32, (bq, bq), 1)
                    s = jnp.where(cols <= rows, s, -jnp.inf)
                m_prev = m_sc[h, :, 0:1]
                m_new = jnp.maximum(m_prev, jnp.max(s, axis=1, keepdims=True))
                alpha = jnp.exp(m_prev - m_new)
                p = jnp.exp(s - m_new)
                l_new = alpha * l_sc[h, :, 0:1] + jnp.sum(p, axis=1, keepdims=True)
                pv = jnp.dot(p.astype(BF16), vp, preferred_element_type=F32)
                mine = first if h == 0 else jnp.logical_not(first)
                acc = jnp.where(mine, acc * alpha + pv, acc)
                m_sc[h] = jnp.broadcast_to(m_new, (bq, 128))
                l_sc[h] = jnp.broadcast_to(l_new, (bq, 128))
            acc_sc[...] = acc

        @pl.when(j < i)
        def _():
            step(False)

        @pl.when(j == i)
        def _():
            step(True)
            l0, l1 = l_sc[0], l_sc[1]
            o_ref[...] = (acc_sc[...] / jnp.where(first, l0, l1)).astype(BF16)
            lse_ref[...] = jnp.where(first, m_sc[0] + jnp.log(l0), m_sc[1] + jnp.log(l1))

        _plan_wait(plan, pctx, grid)

    kv = lambda hp, i, j: (jnp.minimum(i, j), hp)
    p_in, p_ospec, p_oshape, p_scr, p_alias = _plan_io(plan, 3, 2)
    return pl.pallas_call(
        body, name="flash_fwd", grid=grid,
        in_specs=[pl.BlockSpec((bq, 256), lambda hp, i, j: (i, hp)), pl.BlockSpec((bq, 256), kv),
                  pl.BlockSpec((bq, 128), kv)] + [_ANY] * len(p_in),
        out_specs=[pl.BlockSpec((bq, 128), lambda hp, i, j: (i, hp)),
                   pl.BlockSpec((bq, 128), lambda hp, i, j: (i, hp))] + p_ospec,
        out_shape=[jax.ShapeDtypeStruct((T, 512), BF16), jax.ShapeDtypeStruct((T, 512), F32)] + p_oshape,
        scratch_shapes=[pltpu.VMEM((2, bq, 128), F32), pltpu.VMEM((2, bq, 128), F32), pltpu.VMEM((bq, 128), F32)] + p_scr,
        input_output_aliases=p_alias, compiler_params=_params(56, 3),
    )(q, k, v, *p_in)


def _attn_delta(dmix, o):
    T = o.shape[0]
    tm = min(ROW_BLOCK, T)
    blk = pl.BlockSpec((tm, 512), lambda i: (i, 0))

    def body(do_ref, o_ref, delta_ref, dob_ref):
        first = lax.broadcasted_iota(jnp.int32, (tm, 128), 1) < 64
        for hp in range(4):
            sl = slice(hp * 128, (hp + 1) * 128)
            prod = do_ref[:, sl] * o_ref[:, sl].astype(F32)
            d0 = jnp.sum(jnp.where(first, prod, 0.0), axis=1, keepdims=True)
            d1 = jnp.sum(jnp.where(first, 0.0, prod), axis=1, keepdims=True)
            delta_ref[:, sl] = jnp.where(first, d0, d1)
        dob_ref[...] = do_ref[...].astype(BF16)

    return pl.pallas_call(
        body, name="attn_delta", grid=(T // tm,), in_specs=[blk, blk], out_specs=[blk, blk],
        out_shape=[jax.ShapeDtypeStruct((T, 512), F32), jax.ShapeDtypeStruct((T, 512), BF16)],
        compiler_params=_params(32, 1),
    )(dmix, o)


def _flash_bwd(q, k, v, do_b, lse, delta, plan=None):
    T = q.shape[0]
    bq = min(2 * ROW_BLOCK, T)
    nq = T // bq
    grid = (4, nq, nq)

    def body(*refs):
        ((q_ref, k_ref, v_ref, do_ref, lse_ref, dl_ref), (dq_hbm, dk_ref, dv_ref), (dq_sc, dk_sc, dv_sc, sem),
         pctx) = _split_refs(refs, 6, 3, 4, plan)
        _plan_start(plan, pctx, grid)
        hp, j, i = pl.program_id(0), pl.program_id(1), pl.program_id(2)
        first = lax.broadcasted_iota(jnp.int32, (bq, 128), 1) < 64

        @pl.when((j == 0) & (i == 0))
        def _():
            dq_sc[...] = jnp.zeros_like(dq_sc)

        @pl.when(i == j)
        def _():
            dk_sc[...] = jnp.zeros_like(dk_sc)
            dv_sc[...] = jnp.zeros_like(dv_sc)

        def step(masked):
            vp = v_ref[...]
            do = do_ref[...]
            for h in range(2):
                sl = slice(h * 128, (h + 1) * 128)
                qh, kh = q_ref[:, sl], k_ref[:, sl]
                s = lax.dot_general(qh, kh, NT_DIMS, preferred_element_type=F32)
                p = jnp.exp(s - lse_ref[:, h * 64:h * 64 + 1])
                if masked:
                    rows = lax.broadcasted_iota(jnp.int32, (bq, bq), 0)
                    cols = lax.broadcasted_iota(jnp.int32, (bq, bq), 1)
                    p = jnp.where(cols <= rows, p, 0.0)
                mine = first if h == 0 else jnp.logical_not(first)
                do_h = jnp.where(mine, do, jnp.zeros_like(do))
                dv_sc[...] += lax.dot_general(p.astype(BF16), do_h, TN_DIMS, preferred_element_type=F32)
                dp = lax.dot_general(do_h, vp, NT_DIMS, preferred_element_type=F32)
                ds = (p * (dp - dl_ref[:, h * 64:h * 64 + 1])).astype(BF16)
                dq_sc[i, :, sl] += jnp.dot(ds, kh, preferred_element_type=F32)
                dk_sc[:, sl] += lax.dot_general(ds, qh, TN_DIMS, preferred_element_type=F32)

        @pl.when(i > j)
        def _():
            step(False)

        @pl.when(i == j)
        def _():
            step(True)

        @pl.when(i == nq - 1)
        def _():
            dk_ref[...] = dk_sc[...]
            dv_ref[...] = dv_sc[...]

        @pl.when((j == nq - 1) & (i == nq - 1))
        def _():
            cp = pltpu.make_async_copy(dq_sc, dq_hbm.at[hp], sem)
            cp.start()
            cp.wait()

        _plan_wait(plan, pctx, grid)

    qi = lambda hp, j, i: (jnp.maximum(i, j), hp)
    kj = lambda hp, j, i: (j, hp)
    p_in, p_ospec, p_oshape, p_scr, p_alias = _plan_io(plan, 6, 3)
    return pl.pallas_call(
        body, name="flash_bwd", grid=grid,
        in_specs=[pl.BlockSpec((bq, 256), qi), pl.BlockSpec((bq, 256), kj), pl.BlockSpec((bq, 128), kj),
                  pl.BlockSpec((bq, 128), qi), pl.BlockSpec((bq, 128), qi), pl.BlockSpec((bq, 128), qi)]
        + [_ANY] * len(p_in),
        out_specs=[_ANY, pl.BlockSpec((bq, 256), kj), pl.BlockSpec((bq, 128), kj)] + p_ospec,
        out_shape=[jax.ShapeDtypeStruct((4, nq, bq, 256), F32), jax.ShapeDtypeStruct((T, 1024), F32),
                   jax.ShapeDtypeStruct((T, 512), F32)] + p_oshape,
        scratch_shapes=[pltpu.VMEM((nq, bq, 256), F32), pltpu.VMEM((bq, 256), F32), pltpu.VMEM((bq, 128), F32),
                        pltpu.SemaphoreType.DMA] + p_scr,
        input_output_aliases=p_alias, compiler_params=_params(56, 3),
    )(q, k, v, do_b, lse, delta, *p_in)


def _mla_bwd(z0, dq4, dk, dv, gq, gkv, wq, wk, wv, rc, rs1, rs2, plan=None):
    T = z0.shape[0]
    tm = min(ROW_BLOCK, T)
    HW = HEADS * 128
    grid = (T // tm,)
    dq4 = dq4.reshape(4, T, 256)

    def body(*refs):
        ((cq_ref, ckv_ref, dq_ref, dk_ref, dv_ref, gq_ref, gkv_ref, wq_ref, wk_ref, wv_ref, c_ref, s1_ref, s2_ref),
         (dc_ref, dkr_ref, dwq_ref, dwk_ref, dwv_ref, dgq_ref, dgkv_ref), _, pctx) = _split_refs(refs, 13, 7, 0, plan)
        _plan_start(plan, pctx, grid)

        @pl.when(pl.program_id(0) == 0)
        def _():
            for ref in (dwq_ref, dwk_ref, dwv_ref, dgq_ref, dgkv_ref):
                ref[...] = jnp.zeros_like(ref)

        c, s1, s2 = c_ref[...], s1_ref[...], s2_ref[...]
        lane = lax.broadcasted_iota(jnp.int32, (tm, 128), 1)
        nq, xq, rq = _rms(cq_ref[...], gq_ref[...])
        nkv, xkv, rkv = _rms(ckv_ref[...], gkv_ref[...])
        nq_b, nkv_b = nq.astype(BF16), nkv.astype(BF16)

        dq_parts, dk_parts = [], []
        dkr = jnp.zeros((tm, 128), F32)
        for h in range(HEADS):
            blk = dq_ref[h // 2, :, (h % 2) * 128:(h % 2 + 1) * 128] * QK_SCALE
            dq_parts.append(_rope_t(blk, c, s1, s2).astype(BF16))
            kb = dk_ref[:, h * 128:(h + 1) * 128]
            dk_parts.append(jnp.where(lane < NOPE, kb, 0.0).astype(BF16))
            dkr = dkr + kb
        dq_b = jnp.concatenate(dq_parts, axis=1)
        dk_b = jnp.concatenate(dk_parts, axis=1)
        dv_b = dv_ref[...].astype(BF16)

        dwq_ref[...] += lax.dot_general(nq_b, dq_b, TN_DIMS, preferred_element_type=F32)
        dwk_ref[...] += lax.dot_general(nkv_b, dk_b, TN_DIMS, preferred_element_type=F32)
        dwv_ref[...] += lax.dot_general(nkv_b, dv_b, TN_DIMS, preferred_element_type=F32)
        dnq = lax.dot_general(dq_b, wq_ref[...], NT_DIMS, preferred_element_type=F32)
        dnkv = (lax.dot_general(dk_b, wk_ref[...], NT_DIMS, preferred_element_type=F32)
                + lax.dot_general(dv_b, wv_ref[...], NT_DIMS, preferred_element_type=F32))

        def rms_bwd(dn, xhat, rstd, g):
            dxh = dn * g
            return rstd * (dxh - xhat * jnp.mean(dxh * xhat, -1, keepdims=True))

        dc_ref[:, :256] = rms_bwd(dnq, xq, rq, gq_ref[...]).astype(BF16)
        dc_ref[:, 256:] = rms_bwd(dnkv, xkv, rkv, gkv_ref[...]).astype(BF16)
        dgq_ref[...] += _fold8(dnq * xq)
        dgkv_ref[...] += _fold8(dnkv * xkv)
        dkr = pltpu.roll(_rope_t(dkr, c, s1, s2), 64, 1)
        dkr_ref[...] = jnp.where(lane < ROPE, dkr, 0.0).astype(BF16)
        _plan_wait(plan, pctx, grid)

    full = lambda shape: pl.BlockSpec(shape, lambda i: (0,) * len(shape))
    tab = pl.BlockSpec((tm, 128), lambda i: (i, 0))
    p_in, p_ospec, p_oshape, p_scr, p_alias = _plan_io(plan, 13, 7)
    return pl.pallas_call(
        body, name="mla_bwd", grid=grid,
        in_specs=[pl.BlockSpec((tm, 256), lambda i: (i, 0)), pl.BlockSpec((tm, 256), lambda i: (i, 1)),
                  pl.BlockSpec((4, tm, 256), lambda i: (0, i, 0)),
                  pl.BlockSpec((tm, HW), lambda i: (i, 0)), pl.BlockSpec((tm, 512), lambda i: (i, 0)),
                  full((1, 256)), full((1, 256)), full((256, HW)), full((256, HW)), full((256, 512)), tab, tab, tab]
        + [_ANY] * len(p_in),
        out_specs=[pl.BlockSpec((tm, 512), lambda i: (i, 0)), tab, full((256, HW)), full((256, HW)),
                   full((256, 512)), full((8, 256)), full((8, 256))] + p_ospec,
        out_shape=[jax.ShapeDtypeStruct((T, 512), BF16), jax.ShapeDtypeStruct((T, 128), BF16),
                   jax.ShapeDtypeStruct((256, HW), F32), jax.ShapeDtypeStruct((256, HW), F32),
                   jax.ShapeDtypeStruct((256, 512), F32), jax.ShapeDtypeStruct((8, 256), F32),
                   jax.ShapeDtypeStruct((8, 256), F32)] + p_oshape,
        scratch_shapes=p_scr, input_output_aliases=p_alias, compiler_params=_params(48, 1),
    )(z0, z0, dq4, dk, dv, gq, gkv, wq, wk, wv, rc, rs1, rs2, *p_in)


def _sgu_fwd(z0, a_out, ln_g, ln_b, w, b_t):
    T = z0.shape[0]
    tm = min(ROW_BLOCK, T)
    W = SGU_G * SGU_C

    def body(u_ref, v_ref, a_ref, g_ref, b_ref, w_ref, bt_ref, o_ref):
        o_ref[:, :W] = a_ref[...]
        ug = _gelu(u_ref[...])
        xhat, _ = _ln_stats(_gelu(v_ref[...]))
        vn = (xhat * g_ref[...] + b_ref[...]).astype(BF16)
        tril = lax.broadcasted_iota(jnp.int32, (SGU_C, SGU_C), 0) >= lax.broadcasted_iota(jnp.int32, (SGU_C, SGU_C), 1)
        for g in range(SGU_G):
            cs = slice(g * SGU_C, (g + 1) * SGU_C)
            wg = jnp.where(tril, w_ref[g], 0.0).astype(BF16)
            bcol = bt_ref[:, g:g + 1]
            for c in range(tm // SGU_C):
                rs = slice(c * SGU_C, (c + 1) * SGU_C)
                mixed = jnp.dot(wg, vn[rs, cs], preferred_element_type=F32) + bcol
                o_ref[rs, W + g * SGU_C:W + (g + 1) * SGU_C] = (ug[rs, cs] * mixed).astype(BF16)

    full = lambda shape: pl.BlockSpec(shape, lambda i: (0,) * len(shape))
    return pl.pallas_call(
        body, name="sgu_fwd", grid=(T // tm,),
        in_specs=[pl.BlockSpec((tm, W), lambda i: (i, 1)), pl.BlockSpec((tm, W), lambda i: (i, 2)),
                  pl.BlockSpec((tm, W), lambda i: (i, 0)),
                  full((1, W)), full((1, W)), full((SGU_G, SGU_C, SGU_C)), full((SGU_C, SGU_G))],
        out_specs=pl.BlockSpec((tm, 2 * W), lambda i: (i, 0)),
        out_shape=jax.ShapeDtypeStruct((T, 2 * W), BF16),
        compiler_params=_params(32, 1),
    )(z0, z0, a_out, ln_g, ln_b, w, b_t)


def _sgu_bwd(z0, dmix, dc, dkr, ln_g, ln_b, w, b_t):
    T = z0.shape[0]
    tm = min(ROW_BLOCK, T)
    W = SGU_G * SGU_C

    def body(u_ref, v_ref, do_ref, dc_ref, dkr_ref, g_ref, b_ref, w_ref, bt_ref, dz_ref, dw_ref, db_ref, dlg_ref,
             dlb_ref):
        @pl.when(pl.program_id(0) == 0)
        def _():
            for ref in (dw_ref, db_ref, dlg_ref, dlb_ref):
                ref[...] = jnp.zeros_like(ref)

        dz_ref[:, :W] = dc_ref[...]
        dz_ref[:, 3 * W:] = dkr_ref[...]

        u, v, dout = u_ref[...], v_ref[...], do_ref[...]
        ug = _gelu(u)
        xhat, rstd = _ln_stats(_gelu(v))
        vn = (xhat * g_ref[...] + b_ref[...]).astype(BF16)
        dmixed = dout * ug
        dmixed_b = dmixed.astype(BF16)
        tril = lax.broadcasted_iota(jnp.int32, (SGU_C, SGU_C), 0) >= lax.broadcasted_iota(jnp.int32, (SGU_C, SGU_C), 1)
        lane = lax.broadcasted_iota(jnp.int32, (SGU_C, SGU_C), 1)
        dvn_cols = []
        for g in range(SGU_G):
            cs = slice(g * SGU_C, (g + 1) * SGU_C)
            wg = jnp.where(tril, w_ref[g], 0.0).astype(BF16)
            bcol = bt_ref[:, g:g + 1]
            dw_g = jnp.zeros((SGU_C, SGU_C), F32)
            db_g = jnp.zeros((SGU_C, 1), F32)
            dvn_rows = []
            for c in range(tm // SGU_C):
                rs = slice(c * SGU_C, (c + 1) * SGU_C)
                mixed = jnp.dot(wg, vn[rs, cs], preferred_element_type=F32) + bcol
                dz_ref[rs, W + g * SGU_C:W + (g + 1) * SGU_C] = (dout[rs, cs] * mixed * _gelu_grad(u[rs, cs])).astype(BF16)
                dm = dmixed_b[rs, cs]
                dw_g = dw_g + lax.dot_general(dm, vn[rs, cs], NT_DIMS, preferred_element_type=F32)
                db_g = db_g + jnp.sum(dmixed[rs, cs], axis=1, keepdims=True)
                dvn_rows.append(lax.dot_general(wg, dm, TN_DIMS, preferred_element_type=F32))
            dw_ref[g] += jnp.where(tril, dw_g, 0.0)
            db_ref[...] += jnp.where(lane == g, db_g, 0.0)
            dvn_cols.append(jnp.concatenate(dvn_rows, axis=0))
        dvn = jnp.concatenate(dvn_cols, axis=1)
        dxh = dvn * g_ref[...]
        m1 = jnp.mean(dxh, -1, keepdims=True)
        m2 = jnp.mean(dxh * xhat, -1, keepdims=True)
        dvg = rstd * (dxh - m1 - xhat * m2)
        dz_ref[:, 2 * W:3 * W] = (dvg * _gelu_grad(v)).astype(BF16)
        dlg_ref[...] += _fold8(dvn * xhat)
        dlb_ref[...] += _fold8(dvn)

    full = lambda shape: pl.BlockSpec(shape, lambda i: (0,) * len(shape))
    return pl.pallas_call(
        body, name="sgu_bwd", grid=(T // tm,),
        in_specs=[pl.BlockSpec((tm, W), lambda i: (i, 1)), pl.BlockSpec((tm, W), lambda i: (i, 2)),
                  pl.BlockSpec((tm, W), lambda i: (i, 1)), pl.BlockSpec((tm, W), lambda i: (i, 0)),
                  pl.BlockSpec((tm, 128), lambda i: (i, 0)),
                  full((1, W)), full((1, W)), full((SGU_G, SGU_C, SGU_C)), full((SGU_C, SGU_G))],
        out_specs=[pl.BlockSpec((tm, 3 * W + 128), lambda i: (i, 0)), full((SGU_G, SGU_C, SGU_C)),
                   full((SGU_C, SGU_C)), full((8, W)), full((8, W))],
        out_shape=[jax.ShapeDtypeStruct((T, 3 * W + 128), BF16), jax.ShapeDtypeStruct((SGU_G, SGU_C, SGU_C), F32),
                   jax.ShapeDtypeStruct((SGU_C, SGU_C), F32), jax.ShapeDtypeStruct((8, W), F32),
                   jax.ShapeDtypeStruct((8, W), F32)],
        compiler_params=_params(40, 1),
    )(z0, z0, dmix, dc, dkr, ln_g, ln_b, w, b_t)


def _hg_lower_bound(lb_ref):
    a0, a1 = lb_ref[0:1, :], lb_ref[1:2, :]
    m = jnp.maximum(a0, a1)
    e0, e1 = jnp.exp(a0 - m), jnp.exp(a1 - m)
    return e1 / (e0 + e1)


def _running_sum(x, reverse=False):
    n = x.shape[0]
    row = lax.broadcasted_iota(jnp.int32, x.shape, 0)
    s = 1
    while s < n:
        if reverse:
            x = x + jnp.where(row < n - s, pltpu.roll(x, n - s, 0), 0.0)
        else:
            x = x + jnp.where(row >= s, pltpu.roll(x, s, 0), 0.0)
        s *= 2
    return x


def _hg_chunk(qc, fc, lb):
    C = HG_CHUNK
    rows = lax.broadcasted_iota(jnp.int32, (C, C), 0)
    cols = lax.broadcasted_iota(jnp.int32, (C, C), 1)
    rowid = lax.broadcasted_iota(jnp.int32, (C, 128), 0)
    sq, sg = _sigmoid(qc), _sigmoid(fc)
    qf = qc * sq
    gate = lb + (1.0 - lb) * sg
    kk = 1.0 - gate
    lg = jnp.log(gate)
    bcum = _running_sum(lg)
    b_mid = jnp.sum(jnp.where(rowid < C // 2, lg, 0.0), axis=0, keepdims=True)
    b_last = jnp.sum(lg, axis=0, keepdims=True)
    eq, ek, e, eh = jnp.exp(bcum - b_mid), jnp.exp(b_mid - bcum), jnp.exp(bcum), jnp.exp(b_last - bcum)
    qt, kt, qe, khat = qf * eq, kk * ek, qf * e, kk * eh
    a = lax.dot_general(qt.astype(BF16), kt.astype(BF16), NT_DIMS, preferred_element_type=F32)
    a = jnp.where(rows >= cols, a, 0.0)
    return dict(sq=sq, sg=sg, gate=gate, kk=kk, eq=eq, ek=ek, e=e, eh=eh, qt=qt, kt=kt, qe=qe, khat=khat, a=a,
                e_last=jnp.exp(b_last), tril=rows >= cols, rowid=rowid)


def _hgrn_fwd(z4, hg_lb, gnorm):
    T = z4.shape[1]
    tb = min(ROW_BLOCK, T)
    C = HG_CHUNK
    ncb = tb // C
    HPB = HG_HEADS_PER_STEP

    def body(q_ref, f_ref, i_ref, g_ref, lb_ref, gn_ref, y_ref, o_ref, st_ref, st_sc):
        @pl.when(pl.program_id(1) == 0)
        def _():
            st_sc[...] = jnp.zeros_like(st_sc)

        def chunk(c, carry):
            rs = pl.ds(pl.multiple_of(c * C, C), C)
            for hh in range(HPB):
                hs = slice(hh * 128, (hh + 1) * 128)
                lb = _hg_lower_bound(lb_ref.at[:, hs])
                v_b = i_ref[rs, hs].astype(BF16)
                gc = g_ref[rs, hs]
                x = _hg_chunk(q_ref[rs, hs], f_ref[rs, hs], lb)
                st = st_sc[hh]
                st_ref[hh, c] = st
                o = (jnp.dot(x["a"].astype(BF16), v_b, preferred_element_type=F32)
                     + lax.dot_general(x["qe"].astype(BF16), st.astype(BF16), NT_DIMS, preferred_element_type=F32))
                st_sc[hh] = st * x["e_last"] + lax.dot_general(v_b, x["khat"].astype(BF16), TN_DIMS,
                                                               preferred_element_type=F32)
                o_ref[rs, hs] = o
                n = o * lax.rsqrt(jnp.mean(o * o, -1, keepdims=True) + EPS)
                y_ref[rs, hs] = (n * gn_ref[:, hs] * (gc * _sigmoid(gc))).astype(BF16)
            return carry

        lax.fori_loop(0, ncb, chunk, 0)

    W = 128 * HPB
    zb = lambda k: pl.BlockSpec((None, tb, W), lambda h, t: (k, t, h))
    out = pl.BlockSpec((tb, W), lambda h, t: (t, h))
    return pl.pallas_call(
        body, name="hgrn_fwd", grid=(HEADS // HPB, T // tb),
        in_specs=[zb(0), zb(1), zb(2), zb(3), pl.BlockSpec((2, W), lambda h, t: (0, h)),
                  pl.BlockSpec((1, W), lambda h, t: (0, h))],
        out_specs=[out, out, pl.BlockSpec((HPB, ncb, 128, 128), lambda h, t: (h, t, 0, 0))],
        out_shape=[jax.ShapeDtypeStruct((T, D), BF16), jax.ShapeDtypeStruct((T, D), F32),
                   jax.ShapeDtypeStruct((HEADS, T // C, 128, 128), F32)],
        scratch_shapes=[pltpu.VMEM((HPB, 128, 128), F32)],
        compiler_params=_params(32, 2),
    )(z4, z4, z4, z4, hg_lb, gnorm)


def _hgrn_bwd(z4, o_raw, dy, states, hg_lb, gnorm):
    T = z4.shape[1]
    tb = min(ROW_BLOCK, T)
    C = HG_CHUNK
    ncb = tb // C
    nt = T // tb
    HPB = HG_HEADS_PER_STEP

    def body(q_ref, f_ref, i_ref, g_ref, o_ref, dy_ref, st_ref, lb_ref, gn_ref, dz_ref, dlb_ref, dgn_ref, dst_sc):
        @pl.when(pl.program_id(1) == 0)
        def _():
            dst_sc[...] = jnp.zeros_like(dst_sc)
            dlb_ref[...] = jnp.zeros_like(dlb_ref)
            dgn_ref[...] = jnp.zeros_like(dgn_ref)

        def chunk(cc, carry):
            for hh in range(HPB):
                one_head(ncb - 1 - cc, hh, slice(hh * 128, (hh + 1) * 128))
            return carry

        def one_head(c, hh, hs):
            rs = pl.ds(pl.multiple_of(c * C, C), C)
            lb = _hg_lower_bound(lb_ref.at[:, hs])
            gn = gn_ref[:, hs]
            qc, gc = q_ref[rs, hs], g_ref[rs, hs]
            v_b = i_ref[rs, hs].astype(BF16)
            x = _hg_chunk(qc, f_ref[rs, hs], lb)
            st, dst = st_ref[hh, c], dst_sc[hh]
            st_b, dst_b = st.astype(BF16), dst.astype(BF16)
            o, dyc = o_ref[rs, hs], dy_ref[rs, hs]
            sgg = _sigmoid(gc)
            sil = gc * sgg
            rstd = lax.rsqrt(jnp.mean(o * o, -1, keepdims=True) + EPS)
            n = o * rstd
            dgn_ref[:, hs] += _fold8(dyc * n * sil)
            dn = dyc * gn * sil
            do = rstd * (dn - n * jnp.mean(dn * n, -1, keepdims=True))
            dg = dyc * n * gn * (sgg * (1.0 + gc * (1.0 - sgg)))
            do_b = do.astype(BF16)
            da = jnp.where(x["tril"], lax.dot_general(do_b, v_b, NT_DIMS, preferred_element_type=F32), 0.0).astype(BF16)
            qt_b, kt_b, qe_b, khat_b = (x[n_].astype(BF16) for n_ in ("qt", "kt", "qe", "khat"))
            dv = (lax.dot_general(x["a"].astype(BF16), do_b, TN_DIMS, preferred_element_type=F32)
                  + lax.dot_general(khat_b, dst_b, NT_DIMS, preferred_element_type=F32))
            dqt = jnp.dot(da, kt_b, preferred_element_type=F32)
            dqe = jnp.dot(do_b, st_b, preferred_element_type=F32)
            dkt = lax.dot_general(da, qt_b, TN_DIMS, preferred_element_type=F32)
            dkhat = jnp.dot(v_b, dst_b, preferred_element_type=F32)
            dst_sc[hh] = lax.dot_general(do_b, qe_b, TN_DIMS, preferred_element_type=F32) + dst * x["e_last"]
            de_last = jnp.sum(st * dst, axis=0, keepdims=True)
            dqf = dqt * x["eq"] + dqe * x["e"]
            dkk = dkt * x["ek"] + dkhat * x["eh"]
            dkh_kh = dkhat * x["khat"]
            db = dqt * qt_b.astype(F32) - dkt * kt_b.astype(F32) + dqe * x["qe"] - dkh_kh
            db_last = jnp.sum(dkh_kh, axis=0, keepdims=True) + de_last * x["e_last"]
            db = db + jnp.where(x["rowid"] == C - 1, db_last, 0.0)
            dlg = _running_sum(db, reverse=True)
            dgate = dlg / x["gate"] - dkk
            sg, sq = x["sg"], x["sq"]
            dlb_ref[:, hs] += _fold8(dgate * (1.0 - sg)) * (lb * (1.0 - lb))
            dz_ref[0, rs, hs] = (dqf * (sq * (1.0 + qc * (1.0 - sq)))).astype(BF16)
            dz_ref[1, rs, hs] = (dgate * (1.0 - lb) * sg * (1.0 - sg)).astype(BF16)
            dz_ref[2, rs, hs] = dv.astype(BF16)
            dz_ref[3, rs, hs] = dg.astype(BF16)

        lax.fori_loop(0, ncb, chunk, 0)

    W = 128 * HPB
    zb = lambda k: pl.BlockSpec((None, tb, W), lambda h, t: (k, nt - 1 - t, h))
    blk = pl.BlockSpec((tb, W), lambda h, t: (nt - 1 - t, h))
    acc = pl.BlockSpec((8, W), lambda h, t: (0, h))
    return pl.pallas_call(
        body, name="hgrn_bwd", grid=(HEADS // HPB, nt),
        in_specs=[zb(0), zb(1), zb(2), zb(3), blk, blk,
                  pl.BlockSpec((HPB, ncb, 128, 128), lambda h, t: (h, nt - 1 - t, 0, 0)),
                  pl.BlockSpec((2, W), lambda h, t: (0, h)), pl.BlockSpec((1, W), lambda h, t: (0, h))],
        out_specs=[pl.BlockSpec((4, tb, W), lambda h, t: (0, nt - 1 - t, h)), acc, acc],
        out_shape=[jax.ShapeDtypeStruct((4, T, D), BF16), jax.ShapeDtypeStruct((8, D), F32),
                   jax.ShapeDtypeStruct((8, D), F32)],
        scratch_shapes=[pltpu.VMEM((HPB, 128, 128), F32)],
        compiler_params=_params(32, 2),
    )(z4, z4, z4, z4, o_raw, dy, states, hg_lb, gnorm)


def _adamw(w, g, m, v, *, name):
    R, L = w.shape
    tr = R if R <= 512 else 512
    assert R % tr == 0
    blk = pl.BlockSpec((tr, L), lambda i: (i, 0))
    c1, c2 = 1.0 - B1 ** STEP, 1.0 - B2 ** STEP

    def body(w_ref, g_ref, m_ref, v_ref, d_ref, mo_ref, vo_ref):
        g_ = g_ref[...]
        m_ = B1 * m_ref[...] + (1.0 - B1) * g_
        v_ = B2 * v_ref[...] + (1.0 - B2) * (g_ * g_)
        d_ref[...] = -LR * ((m_ / c1) / (jnp.sqrt(v_ / c2) + ADAM_EPS) + WD * w_ref[...])
        mo_ref[...] = m_
        vo_ref[...] = v_

    sds = jax.ShapeDtypeStruct((R, L), F32)
    return pl.pallas_call(
        body, name=name, grid=(R // tr,), in_specs=[blk] * 4, out_specs=[blk] * 3, out_shape=[sds] * 3,
        compiler_params=_params(32, 1),
    )(w, g, m, v)


def _adamw_rows(w, m, v, gbufs, row0, *, name, plan=None):
    L, R, C = w.shape
    tr = 256
    assert R % tr == 0 and row0 % tr == 0 and len(gbufs) == L
    grid = (L, R // tr)
    blk = pl.BlockSpec((None, tr, C), lambda l, i: (l, i, 0))
    gblk = pl.BlockSpec((tr, C), lambda l, i: (row0 // tr + i, 0))
    c1, c2 = 1.0 - B1 ** STEP, 1.0 - B2 ** STEP

    def body(*refs):
        ins, (go_ref, d_ref, mo_ref, vo_ref), _, pctx = _split_refs(refs, 3 + L, 4, 0, plan)
        w_ref, m_ref, v_ref = ins[:3]
        g_refs = ins[3:]
        _plan_start(plan, pctx, grid)
        g_ = g_refs[0][...]
        for l in range(1, L):
            g_ = jnp.where(pl.program_id(0) == l, g_refs[l][...], g_)
        m_ = B1 * m_ref[...] + (1.0 - B1) * g_
        v_ = B2 * v_ref[...] + (1.0 - B2) * (g_ * g_)
        go_ref[...] = g_
        d_ref[...] = -LR * ((m_ / c1) / (jnp.sqrt(v_ / c2) + ADAM_EPS) + WD * w_ref[...])
        mo_ref[...] = m_
        vo_ref[...] = v_
        _plan_wait(plan, pctx, grid)

    sds = jax.ShapeDtypeStruct((L, R, C), F32)
    p_in, p_ospec, p_oshape, p_scr, p_alias = _plan_io(plan, 3 + L, 4)
    return pl.pallas_call(
        body, name=name, grid=grid, in_specs=[blk] * 3 + [gblk] * L + [_ANY] * len(p_in),
        out_specs=[blk] * 4 + p_ospec, out_shape=[sds] * 4 + p_oshape, scratch_shapes=p_scr,
        input_output_aliases=p_alias, compiler_params=_params(32, 2),
    )(w, m, v, *gbufs, *p_in)


def _add_pairs(g, theirs, ids, *, name):
    n, R, L = theirs.shape
    tr = 128
    nb = R // tr

    def body(ids_ref, a_ref, b_ref, o_ref):
        o_ref[...] = (a_ref[...].astype(F32) + b_ref[...].astype(F32)).astype(BF16)

    blk = pl.BlockSpec((n, tr, L), lambda i, ids: (0, i, 0))
    return pl.pallas_call(
        body, name=name, out_shape=jax.ShapeDtypeStruct((n, R, L), BF16),
        grid_spec=pltpu.PrefetchScalarGridSpec(
            num_scalar_prefetch=1, grid=(nb,),
            in_specs=[pl.BlockSpec((n, tr, L), lambda i, ids: (0, ids[1] * nb + i, 0)), blk], out_specs=blk),
        compiler_params=_params(16, 1),
    )(ids, g, theirs)


def _sum_chips(pair, parts, ids, *, name):
    _, R, L = parts.shape
    tr = 128

    def body(ids_ref, o_ref, r_ref, out_ref):
        out_ref[...] = ((o_ref[...].astype(F32) + r_ref[0].astype(F32)) + r_ref[1].astype(F32)) + r_ref[2].astype(F32)

    return pl.pallas_call(
        body, name=name, out_shape=jax.ShapeDtypeStruct((2, R, L), F32),
        grid_spec=pltpu.PrefetchScalarGridSpec(
            num_scalar_prefetch=1, grid=(R // tr,),
            in_specs=[pl.BlockSpec((None, tr, L), lambda i, ids: (ids[0], i, 0)),
                      pl.BlockSpec((3, tr, L), lambda i, ids: (0, i, 0))],
            out_specs=pl.BlockSpec((None, tr, L), lambda i, ids: (ids[1], i, 0))),
        compiler_params=_params(32, 1),
    )(ids, pair, parts)


def _mesh_ids():
    x, y, c = _mesh_pos()
    return jnp.stack([2 * x + y, c]).astype(jnp.int32)


def _place_shard(rows, ids, *, name):
    R, L = rows.shape
    tr = 256

    def body(ids_ref, in_ref, out_ref):
        out_ref[...] = in_ref[...].astype(BF16)

    return pl.pallas_call(
        body, name=name, out_shape=jax.ShapeDtypeStruct((4, R, L), BF16),
        grid_spec=pltpu.PrefetchScalarGridSpec(
            num_scalar_prefetch=1, grid=(R // tr,), in_specs=[pl.BlockSpec((tr, L), lambda i, ids: (i, 0))],
            out_specs=pl.BlockSpec((None, tr, L), lambda i, ids: (ids[0], i, 0))),
        compiler_params=_params(16, 1),
    )(ids, rows)


def _remote(src, dst, send_sem, recv_sem, to):
    return pltpu.make_async_remote_copy(src_ref=src, dst_ref=dst, send_sem=send_sem, recv_sem=recv_sem,
                                        device_id=to, device_id_type=MESH_IDS)


def _rows(ref, lead, start, size):
    return ref.at[tuple(pl.ds(0, n) for n in ref.shape[:lead]) + (pl.ds(start, size),)]


def _other_chips():
    x, y, _ = _mesh_pos()
    return [(1 - x, y), (x, 1 - y), (1 - x, 1 - y)]


def _plan_gather_ici(bufs):
    n = len(bufs)

    def copies(outs, send, recv):
        x, y, c = _mesh_pos()
        res = []
        for b in range(n):
            half = bufs[b].shape[1] // 2
            mine = _rows(outs[b].at[2 * x + y], 0, c * half, half)
            for j, (cx, cy) in enumerate(_other_chips()):
                res.append((_remote(mine, mine, send(3 * b + j), recv(3 * b + j), (cx, cy, c)),
                            _remote(mine, _rows(outs[b].at[2 * cx + cy], 0, c * half, half),
                                    send(3 * b + j), recv(3 * b + j), (x, y, c))))
        return res

    def start(ins, outs, send, recv, loc):
        for out_cp, _ in copies(outs, send, recv):
            out_cp.start()

    def wait(ins, outs, send, recv, loc):
        for out_cp, in_cp in copies(outs, send, recv):
            in_cp.wait_recv()
            out_cp.wait_send()

    outs = [jax.ShapeDtypeStruct(b.shape, b.dtype) for b in bufs]
    return _Plan(bufs, outs, 3 * n, 0, start, wait, aliases={b: b for b in range(n)})


def _plan_gather_forward(bufs):
    n = len(bufs)

    def copies(outs, send, recv):
        x, y, c = _mesh_pos()
        res = []
        for b in range(n):
            half = bufs[b].shape[1] // 2
            for j, (cx, cy) in enumerate(_other_chips()):
                slot = outs[b].at[2 * cx + cy]
                res.append((_remote(_rows(slot, 0, c * half, half), _rows(slot, 0, c * half, half),
                                    send(3 * b + j), recv(3 * b + j), (x, y, 1 - c)),
                            _remote(_rows(slot, 0, c * half, half), _rows(slot, 0, (1 - c) * half, half),
                                    send(3 * b + j), recv(3 * b + j), (x, y, c))))
        return res

    def start(ins, outs, send, recv, loc):
        for out_cp, _ in copies(outs, send, recv):
            out_cp.start()

    def wait(ins, outs, send, recv, loc):
        for out_cp, in_cp in copies(outs, send, recv):
            in_cp.wait_recv()
            out_cp.wait_send()

    outs = [jax.ShapeDtypeStruct(b.shape, b.dtype) for b in bufs]
    return _Plan(bufs, outs, 3 * n, 0, start, wait, aliases={b: b for b in range(n)})


def _plan_pair_swap(g):
    half = g.shape[1] // 2

    def copy(ins, outs, send, recv, loc):
        x, y, c = _mesh_pos()
        return _remote(_rows(ins[0], 1, (1 - c) * half, half), outs[0], send(0), recv(0), (x, y, 1 - c))

    return _Plan([g], [jax.ShapeDtypeStruct((4, half, g.shape[2]), g.dtype)], 1, 0,
                 lambda *a: copy(*a).start(), lambda *a: copy(*a).wait())


def _plan_pair_gather(buf):
    def copies(ins, outs, send, recv, loc):
        x, y, c = _mesh_pos()
        return (_remote(outs[0].at[c], outs[0].at[c], send(0), recv(0), (x, y, 1 - c)),
                _remote(outs[0].at[c], outs[0].at[1 - c], send(0), recv(0), (x, y, c)))

    def wait(*a):
        out_cp, in_cp = copies(*a)
        in_cp.wait_recv()
        out_cp.wait_send()

    return _Plan([buf], [jax.ShapeDtypeStruct(buf.shape, buf.dtype)], 1, 0, lambda *a: copies(*a)[0].start(), wait,
                 aliases={0: 0})


def _plan_chip_scatter(p):
    def copies(ins, outs, send, recv, loc):
        _, _, c = _mesh_pos()
        return [_remote(ins[0].at[2 * cx + cy], outs[0].at[j], send(j), recv(j), (cx, cy, c))
                for j, (cx, cy) in enumerate(_other_chips())]

    def start(*a):
        for cp in copies(*a):
            cp.start()

    def wait(*a):
        for cp in copies(*a):
            cp.wait()

    return _Plan([p], [jax.ShapeDtypeStruct((3,) + p.shape[1:], p.dtype)], 3, 0, start, wait)


def _plan_exchange_all(vec):
    def copies(ins, outs, send, recv, loc):
        x, y, c = _mesh_pos()
        return [_remote(ins[0], outs[0].at[r - 1], send(r - 1), recv(r - 1), (x ^ (r >> 2), y ^ ((r >> 1) & 1), c ^ (r & 1)))
                for r in range(1, 8)]

    def start(*a):
        for cp in copies(*a):
            cp.start()

    def wait(*a):
        for cp in copies(*a):
            cp.wait()

    return _Plan([vec], [jax.ShapeDtypeStruct((7,) + vec.shape, vec.dtype)], 7, 0, start, wait)


def _sum_devices(vec, others, ids):
    R, L = vec.shape

    def body(ids_ref, v_ref, o_ref, out_ref):
        me = 2 * ids_ref[0] + ids_ref[1]
        total = None
        for d in range(8):
            rel = d ^ me
            term = jnp.where(rel == 0, v_ref[...], o_ref[jnp.maximum(rel - 1, 0)])
            total = term if total is None else total + term
        out_ref[...] = total

    return pl.pallas_call(
        body, name="small_grad_sum", out_shape=jax.ShapeDtypeStruct((R, L), F32),
        grid_spec=pltpu.PrefetchScalarGridSpec(
            num_scalar_prefetch=1, grid=(1,), in_specs=[pl.BlockSpec((R, L), lambda i, ids: (0, 0)),
                                                        pl.BlockSpec((7, R, L), lambda i, ids: (0, 0, 0))],
            out_specs=pl.BlockSpec((R, L), lambda i, ids: (0, 0))),
        compiler_params=_params(16, 1),
    )(ids, vec, others)


ROWS_L1, ROWS_L0, ROWS_ODD = 3328, 2048, 768
ODD_PARTS = (("w_out_e", (256, 1024)), ("w_in_e", (1024, 392)), ("w_qb", (256, 192)), ("w_kvb", (256, 256)))


def _odd_rows(parts, dtype, gnorm=None):
    rows = [parts[n].reshape(-1, 1024).astype(dtype) for n, _ in ODD_PARTS]
    used = sum(r.shape[0] for r in rows)
    if gnorm is not None:
        bits = lax.bitcast_convert_type(gnorm.reshape(-1), BF16).reshape(1, 512)
        rows.append(jnp.pad(bits, ((0, 0), (0, 512))))
        used += 1
    rows.append(jnp.zeros((ROWS_ODD - used, 1024), dtype))
    return jnp.concatenate(rows, axis=0)


def _odd_unrows(buf, with_gnorm=False):
    out, off = {}, 0
    for n, shape in ODD_PARTS:
        nr = math.prod(shape) // 1024
        out[n] = buf[off:off + nr].reshape(shape)
        off += nr
    if with_gnorm:
        out["hg_gnorm"] = lax.bitcast_convert_type(buf[off, :512].reshape(256, 2), F32).reshape(1, 256)
    return out


def _pack_small(vals):
    flat = jnp.concatenate([vals[n].reshape(-1).astype(F32) for n, _ in SMALL])
    return jnp.pad(flat, (0, SMALL_ROWS * 1024 - flat.shape[0])).reshape(SMALL_ROWS, 1024)


def _unpack_small(packed):
    flat = packed.reshape(-1)
    out, off = {}, 0
    for n, shape in SMALL:
        size = math.prod(shape)
        out[n] = flat[off:off + size].reshape(shape)
        off += size
    return out


def _rope_tables(positions):
    half = ROPE // 2
    inv_freq = ROPE_BASE ** (-jnp.arange(half, dtype=F32) / half)
    ang = positions.astype(F32).reshape(-1, 1) * inv_freq
    cos, sin = jnp.cos(ang), jnp.sin(ang)
    T = ang.shape[0]
    one, z16, z32 = jnp.ones((T, NOPE), F32), jnp.zeros((T, half), F32), jnp.zeros((T, 32), F32)
    z64 = jnp.zeros((T, NOPE), F32)
    c = jnp.concatenate([one, cos, cos, z32], axis=1)
    s1 = jnp.concatenate([z64, -sin, z16, z32], axis=1)
    s2 = jnp.concatenate([z64, z16, sin, z32], axis=1)
    return c, s1, s2


def _local_step(x, positions, tgt, odd, bufs, P, exchange):
    T = x.shape[0]
    row = lambda a: a.reshape(1, -1)
    rc, rs1, rs2 = _rope_tables(positions)
    blk = lambda f: pl.BlockSpec((None, D, D), f)

    w_in_e = odd["w_in_e"]
    w_in = jnp.concatenate([w_in_e[:, :512], w_in_e[:, 544:1568], w_in_e[:, 512:544], jnp.zeros((D, 96), BF16)], axis=1)
    wq = jnp.pad(odd["w_qb"].reshape(256, HEADS, NOPE + ROPE), ((0, 0), (0, 0), (0, 32))).reshape(256, HEADS * 128)
    kvb = odd["w_kvb"].reshape(256, HEADS, NOPE + VDIM)
    wk = jnp.pad(kvb[:, :, :NOPE], ((0, 0), (0, 0), (0, 64))).reshape(256, HEADS * 128)
    wv = kvb[:, :, NOPE:].reshape(256, HEADS * VDIM)
    w_out_e = odd["w_out_e"]
    sgu_w = P["sgu_w"][0]
    sgu_bt = P["sgu_b"][0].T
    gq, gkv = P["mla_gq"], P["mla_gkv"]
    gnorm = P["hg_gnorm"]

    x_b = x.astype(BF16)
    z0 = _matmul(x_b, w_in, name="in_proj_e", M=T, N=1664, K=D, tn=1664)[0]
    q, k, v = _mla_prep(z0, gq, gkv, wq, wk, wv, rc, rs1, rs2)
    if exchange:
        ids = _mesh_ids()
        placed = [_place_shard(b, ids, name=f"place_shard_{l}") for l, b in enumerate(bufs)]
        a_out, lse, wg0, wg1 = _flash_fwd(q, k, v, plan=_plan_gather_ici(placed))
    else:
        a_out, lse = _flash_fwd(q, k, v)
        wg0, wg1 = bufs
    mix0 = _sgu_fwd(z0, a_out, P["sgu_ln_g"], P["sgu_ln_b"], sgu_w, sgu_bt)
    res = _proj_ln(mix0, w_out_e, x, row(P["ln1_g"][0]), row(P["ln1_b"][0]), name="out_proj_ln_e",
                   plan=_plan_gather_forward([wg0, wg1]) if exchange else None)
    r1, h1, h1b = res[:3]
    if exchange:
        wg0, wg1 = res[3:]
    ra0, r2, h2, h2b = _ffn_ln(h1b, wg0, h1, row(P["ln2_g"][0]), row(P["ln2_b"][0]), name="ffn_ln_0")
    z4 = _matmul(h2b, wg1, name="in_proj_o", M=T, N=4 * D, K=D, b_spec=blk(lambda i, j, k: (j, 2, 0)),
                 out_shape=jax.ShapeDtypeStruct((4, T, D), F32),
                 o_spec=pl.BlockSpec((None, min(MM_ROWS, T), D), lambda i, j, k: (j, i, 0)))[0]
    y1, o_raw, states = _hgrn_fwd(z4, P["hg_lb"], gnorm)
    r3, h3, h3b = _proj_ln(y1, wg1, h2, row(P["ln1_g"][1]), row(P["ln1_b"][1]), name="out_proj_ln_o", w_rowblk=12)
    ra1, r4, h4, _ = _ffn_ln(h3b, wg1, h3, row(P["ln2_g"][1]), row(P["ln2_b"][1]), name="ffn_ln_1")
    dy, loss_parts = _loss_dy(h4, tgt)

    gs = {}
    ln1_g, ln1_b, ln2_g, ln2_b = [None, None], [None, None], [None, None], [None, None]

    def ffn_bwd(l, dh, r_out, ra, h_mid_b, g2, wg, rows, plan=None):
        dr, dr_b, dg, db = _ln_bwd(dh, r_out, row(g2), name=f"ln2_bwd_{l}")
        ln2_g[l], ln2_b[l] = dg.sum(0), db.sum(0)
        da, *extra = _matmul(dr_b, wg, tb=True, mul=ra, out_dtype=BF16, name=f"ffn_da_{l}", M=T, N=4 * D, K=D,
                             b_spec=blk(lambda i, j, k: (j, 1, 0)), plan=plan)
        gbuf = _matmul(ra, dr_b, ta=True, a_sq=True, name=f"ffn_dw2_{l}", M=4 * D, N=D, K=T, tm=1024, tk=DW_TOKENS,
                       out_shape=jax.ShapeDtypeStruct((4, rows, D), BF16), o_spec=blk(lambda i, j, k: (i, 1, 0)))[0]
        gbuf = _matmul(h_mid_b, da, ta=True, name=f"ffn_dw1_{l}", M=D, N=4 * D, K=T, tm=1024, tk=DW_TOKENS, into=gbuf,
                       out_shape=jax.ShapeDtypeStruct((4, rows, D), BF16), o_spec=blk(lambda i, j, k: (j, 0, 0)))[0]
        dh_mid = _matmul(da, wg, tb=True, add=dr, add_scale=ALPHA, name=f"ffn_dh_{l}", M=T, N=D, K=4 * D,
                         b_spec=blk(lambda i, j, k: (k, 0, 0)))[0]
        return dh_mid, gbuf, extra

    dh3, g1, _ = ffn_bwd(1, dy, r4, ra1, h3b, P["ln2_g"][1], wg1, ROWS_L1)
    dr3, dr3_b, dg, db = _ln_bwd(dh3, r3, row(P["ln1_g"][1]), name="ln1_bwd_1")
    ln1_g[1], ln1_b[1] = dg.sum(0), db.sum(0)
    g1_sds = jax.ShapeDtypeStruct((4, ROWS_L1, D), BF16)
    g1 = _matmul(y1, dr3_b, ta=True, name="dw_out_o", M=D, N=D, K=T, tm=256, tk=DW_TOKENS, into=g1, out_shape=g1_sds,
                 o_spec=pl.BlockSpec((None, 256, D), lambda i, j, k: (i, 12, 0)))[0]
    dmix1 = _matmul(dr3_b, wg1, tb=True, name="dmix_o", M=T, N=D, K=D, b_spec=_rows4_spec(12, 3), b_merge=(D, D))[0]
    dz4, dlb, dgn = _hgrn_bwd(z4, o_raw, dmix1, states, P["hg_lb"], gnorm)
    g1 = _matmul(h2b, dz4, ta=True, name="dw_in_o", M=D, N=4 * D, K=T, tm=1024, tk=DW_TOKENS, into=g1, out_shape=g1_sds,
                 b_spec=pl.BlockSpec((None, min(DW_TOKENS, T), D), lambda i, j, k: (j, k, 0)),
                 o_spec=blk(lambda i, j, k: (j, 2, 0)))[0]
    dh2 = _matmul(dz4, wg1, tb=True, add=dr3, add_scale=ALPHA, name="dh_in_o", M=T, N=D, K=4 * D,
                  a_spec=pl.BlockSpec((None, min(MM_ROWS, T), D), lambda i, j, k: (k, i, 0)),
                  b_spec=blk(lambda i, j, k: (k, 2, 0)))[0]
    d_lb1 = dlb.sum(0)
    gs["hg_lb"] = jnp.stack([-d_lb1, d_lb1])
    gs["hg_gnorm"] = dgn.sum(0)[None]

    dh1, g0, swapped1 = ffn_bwd(0, dh2, r2, ra0, h1b, P["ln2_g"][0], wg0, ROWS_L0,
                                plan=_plan_pair_swap(g1) if exchange else None)
    dr1, dr1_b, dg, db = _ln_bwd(dh1, r1, row(P["ln1_g"][0]), name="ln1_bwd_0")
    ln1_g[0], ln1_b[0] = dg.sum(0), db.sum(0)
    godd = {"w_out_e": _matmul(mix0, dr1_b, ta=True, name="dw_out_e", M=D, N=D, K=T, tm=1024, tk=DW_TOKENS)[0]}
    dmix0, *swapped0 = _matmul(dr1_b, w_out_e, tb=True, name="dmix_e", M=T, N=D, K=D,
                               plan=_plan_pair_swap(g0) if exchange else None)
    delta, do_b = _attn_delta(dmix0, a_out)
    if exchange:
        pair1 = _add_pairs(g1, swapped1[0], ids, name="grad_pair_add_1")
        pair0 = _add_pairs(g0, swapped0[0], ids, name="grad_pair_add_0")
        dq4, dk, dv, parts0, parts1 = _flash_bwd(
            q, k, v, do_b, lse, delta, plan=_join_plans([_plan_chip_scatter(pair0), _plan_chip_scatter(pair1)]))
        half0 = _sum_chips(pair0, parts0, ids, name="grad_chip_sum_0")
        half1 = _sum_chips(pair1, parts1, ids, name="grad_chip_sum_1")
        dc, dkr, dwq, dwk, dwv, dgq, dgkv, g0, g1 = _mla_bwd(
            z0, dq4, dk, dv, gq, gkv, wq, wk, wv, rc, rs1, rs2,
            plan=_join_plans([_plan_pair_gather(half0), _plan_pair_gather(half1)]))
        g0, g1 = g0.reshape(ROWS_L0, D), g1.reshape(ROWS_L1, D)
    else:
        dq4, dk, dv = _flash_bwd(q, k, v, do_b, lse, delta)
        dc, dkr, dwq, dwk, dwv, dgq, dgkv = _mla_bwd(z0, dq4, dk, dv, gq, gkv, wq, wk, wv, rc, rs1, rs2)
    dz0, dsw, dsb, dslg, dslb = _sgu_bwd(z0, dmix0, dc, dkr, P["sgu_ln_g"], P["sgu_ln_b"], sgu_w, sgu_bt)
    dw_in = _matmul(x_b, dz0, ta=True, name="dw_in_e", M=D, N=1664, K=T, tm=1024, tn=1664, tk=DW_TOKENS // 2)[0]
    godd["w_in_e"] = jnp.concatenate([dw_in[:, :512], dw_in[:, 1536:1568], dw_in[:, 512:1536]], axis=1)
    grad_x = _matmul(dz0, w_in, tb=True, add=dr1, add_scale=ALPHA, name="dx", M=T, N=D, K=1664, tk=1664)[0]

    godd["w_qb"] = dwq.reshape(256, HEADS, 128)[:, :, :NOPE + ROPE].reshape(256, HEADS * (NOPE + ROPE))
    godd["w_kvb"] = jnp.concatenate([dwk.reshape(256, HEADS, 128)[:, :, :NOPE], dwv.reshape(256, HEADS, VDIM)],
                                    axis=2).reshape(256, HEADS * (NOPE + VDIM))
    gs["mla_gq"], gs["mla_gkv"] = dgq.sum(0)[None], dgkv.sum(0)[None]
    gs["sgu_ln_g"], gs["sgu_ln_b"] = dslg.sum(0)[None], dslb.sum(0)[None]
    gs["sgu_w"], gs["sgu_b"] = dsw[None], dsb[:, :SGU_G].T[None]
    gs["ln1_g"], gs["ln1_b"] = jnp.stack(ln1_g), jnp.stack(ln1_b)
    gs["ln2_g"], gs["ln2_b"] = jnp.stack(ln2_g), jnp.stack(ln2_b)
    return loss_parts, grad_x, g0, g1, godd, gs


WEIGHTS = ['w_in_e', 'mla_gq', 'mla_gkv', 'w_qb', 'w_kvb', 'sgu_ln_g', 'sgu_ln_b', 'sgu_w', 'sgu_b', 'w_out_e',
           'w_in_o', 'hg_lb', 'hg_gnorm', 'w_out_o', 'ln1_g', 'ln1_b', 'w_ff1', 'w_ff2', 'ln2_g', 'ln2_b']


def kernel(x, positions, w_in_e, mla_gq, mla_gkv, w_qb, w_kvb, sgu_ln_g, sgu_ln_b, sgu_w, sgu_b, w_out_e, w_in_o, hg_lb, hg_gnorm, w_out_o, ln1_g, ln1_b, w_ff1, w_ff2, ln2_g, ln2_b, loss_target, m_w_in_e, m_mla_gq, m_mla_gkv, m_w_qb, m_w_kvb, m_sgu_ln_g, m_sgu_ln_b, m_sgu_w, m_sgu_b, m_w_out_e, m_w_in_o, m_hg_lb, m_hg_gnorm, m_w_out_o, m_ln1_g, m_ln1_b, m_w_ff1, m_w_ff2, m_ln2_g, m_ln2_b, v_w_in_e, v_mla_gq, v_mla_gkv, v_w_qb, v_w_kvb, v_sgu_ln_g, v_sgu_ln_b, v_sgu_w, v_sgu_b, v_w_out_e, v_w_in_o, v_hg_lb, v_hg_gnorm, v_w_out_o, v_ln1_g, v_ln1_b, v_w_ff1, v_w_ff2, v_ln2_g, v_ln2_b):
    args = dict(locals())
    w = {n: args[n] for n in WEIGHTS}
    m = {n: args["m_" + n] for n in WEIGHTS}
    v = {n: args["v_" + n] for n in WEIGHTS}
    cx, cy, cc = _mesh_pos()
    chip = 2 * cx + cy

    odd_shard = _odd_rows({"w_out_e": w_out_e[0], "w_in_e": w_in_e[0], "w_qb": w_qb[0], "w_kvb": w_kvb[0]}, BF16,
                          gnorm=hg_gnorm)
    ids = _mesh_ids()
    gathered = _run_plan(_plan_gather_ici([_place_shard(odd_shard, ids, name="place_shard_odd")]), name="odd_gather")[0]
    gathered = _run_plan(_plan_gather_forward([gathered]), name="odd_gather_forward")[0]
    per_chip = [_odd_unrows(gathered[j], with_gnorm=True) for j in range(4)]
    odd = {"w_out_e": jnp.concatenate([p["w_out_e"] for p in per_chip], axis=0)}
    for n in ("w_in_e", "w_qb", "w_kvb"):
        odd[n] = jnp.concatenate([p[n] for p in per_chip], axis=1)
    small = {n: w[n] for n, _ in SMALL if n != "hg_gnorm"}
    small["hg_gnorm"] = jnp.concatenate([p["hg_gnorm"] for p in per_chip], axis=1)
    rows_l0 = jnp.concatenate([w_ff1[0], w_ff2[0]], axis=0).astype(BF16)
    rows_l1 = jnp.concatenate([w_ff1[1], w_ff2[1], w_in_o[0], w_out_o[0]], axis=0).astype(BF16)

    loss_parts, grad_x, g_l0, g_l1, godd, gs = _local_step(x[0], positions[0], loss_target[0], odd, (rows_l0, rows_l1),
                                                            small, True)

    loss = lax.psum((0.5 / D) * jnp.sum(loss_parts), ("x", "y", "c"))

    by_chip = [_odd_rows({"w_out_e": jnp.split(godd["w_out_e"], 4, axis=0)[j],
                          **{n: jnp.split(godd[n], 4, axis=1)[j] for n in ("w_in_e", "w_qb", "w_kvb")}}, BF16)
               for j in range(4)]
    godd_buf = jnp.stack(by_chip)
    small_vec = _pack_small(gs)
    grads, delta, new_m, new_v = {}, {}, {}, {}

    def adamw_big(n, bufs_, row0, plan):
        grads[n], delta[n], new_m[n], new_v[n], *extra = _adamw_rows(w[n], m[n], v[n], bufs_, row0, name=f"adamw_{n}",
                                                                     plan=plan)
        return extra

    theirs, = adamw_big("w_in_o", [g_l1], 2048, _plan_pair_swap(godd_buf))
    pair = _add_pairs(godd_buf, theirs, ids, name="odd_pair_add")
    parts, = adamw_big("w_ff2", [g_l0, g_l1], 1024, _plan_chip_scatter(pair))
    others, = adamw_big("w_ff1", [g_l0, g_l1], 0, _plan_exchange_all(small_vec))
    adamw_big("w_out_o", [g_l1], 3072, None)
    g_odd = _run_plan(_plan_pair_gather(_sum_chips(pair, parts, ids, name="odd_chip_sum")), name="odd_pair_gather")[0]
    g_odd = _odd_unrows(g_odd.reshape(ROWS_ODD, 1024))
    g_small = _unpack_small(_sum_devices(small_vec, others, ids))
    grads.update({n: g_small[n] for n, _ in SMALL if n != "hg_gnorm"})
    grads["hg_gnorm"] = lax.dynamic_slice_in_dim(g_small["hg_gnorm"], chip * 256, 256, axis=1)

    for n, _ in ODD_PARTS:
        grads[n] = g_odd[n][None]
        d_, m_, v_ = _adamw(w[n][0], g_odd[n], m[n][0], v[n][0], name=f"adamw_{n}")
        delta[n], new_m[n], new_v[n] = d_[None], m_[None], v_[None]
    rest = [n for n in WEIGHTS if n not in delta]

    def pack_rest(d):
        flat = jnp.concatenate([d[n].reshape(-1) for n in rest])
        return jnp.pad(flat, (0, SMALL_ROWS * 1024 - flat.shape[0])).reshape(SMALL_ROWS, 1024)

    outs = _adamw(pack_rest(w), pack_rest(grads), pack_rest(m), pack_rest(v), name="adamw_small")
    for dst, packed in zip((delta, new_m, new_v), outs):
        flat, off = packed.reshape(-1), 0
        for n in rest:
            size = math.prod(w[n].shape)
            dst[n] = flat[off:off + size].reshape(w[n].shape)
            off += size

    return (loss, grad_x[None], *[grads[n] for n in WEIGHTS], *[delta[n] for n in WEIGHTS],
            *[new_m[n] for n in WEIGHTS], *[new_v[n] for n in WEIGHTS])
```

```python
import functools
import math

import jax
import jax.numpy as jnp
from jax import lax
from jax.experimental import pallas as pl
from jax.experimental.pallas import tpu as pltpu

F32 = jnp.float32
BF16 = jnp.bfloat16
MESH_IDS = pl.DeviceIdType.MESH

D = 1024
DEPTH = 2
HEADS = 8
NOPE, ROPE, VDIM = 64, 32, 64
QK_SCALE = (NOPE + ROPE) ** -0.5
ROPE_BASE = 10000.0
SGU_G, SGU_C = 4, 128
HG_CHUNK = 64
HG_HEADS_PER_STEP = 4
ALPHA = (2 * DEPTH) ** 0.25
EPS = 1e-5
LR, B1, B2, ADAM_EPS, WD, STEP = 0.001, 0.9, 0.999, 1e-08, 0.01, 10
GELU_C = math.sqrt(2.0 / math.pi)
GELU_A = 0.044715
HI = lax.Precision.HIGHEST
MB = 1024 * 1024
ROW_BLOCK = 512

NT_DIMS = (((1,), (1,)), ((), ()))
TN_DIMS = (((0,), (0,)), ((), ()))

SHARDED = (
    ("w_in_e", (1, 1024, 392), 2), ("w_qb", (1, 256, 192), 2), ("w_kvb", (1, 256, 256), 2),
    ("w_out_e", (1, 256, 1024), 1), ("w_in_o", (1, 1024, 1024), 2), ("w_out_o", (1, 256, 1024), 1),
    ("w_ff1", (2, 1024, 1024), 2), ("w_ff2", (2, 1024, 1024), 1), ("hg_gnorm", (1, 256), 1),
)
PACK_ROWS = 6144
HALF_ROWS = PACK_ROWS // 2
SMALL = (("mla_gq", (1, 256)), ("mla_gkv", (1, 256)), ("sgu_ln_g", (1, 512)), ("sgu_ln_b", (1, 512)),
         ("sgu_w", (1, 4, 128, 128)), ("sgu_b", (1, 4, 128)), ("hg_lb", (2, 1024)), ("hg_gnorm", (1, 1024)),
         ("ln1_g", (2, 1024)), ("ln1_b", (2, 1024)), ("ln2_g", (2, 1024)), ("ln2_b", (2, 1024)))
SMALL_ROWS = 80


def _params(vmem_mb, n_axes=0):
    kw = dict(vmem_limit_bytes=vmem_mb * MB)
    if n_axes:
        kw["dimension_semantics"] = ("arbitrary",) * n_axes
    return pltpu.CompilerParams(**kw)


_ANY = pl.BlockSpec(memory_space=pltpu.HBM)


def _mesh_pos():
    return lax.axis_index("x"), lax.axis_index("y"), lax.axis_index("c")


class _Plan:
    def __init__(self, ins, outs, n_remote, n_local, start, wait, aliases=None):
        self.ins, self.outs, self.n_remote, self.n_local = list(ins), list(outs), n_remote, n_local
        self.start, self.wait, self.aliases = start, wait, dict(aliases or {})


def _join_plans(plans):
    ins, outs, aliases, parts = [], [], {}, []
    nr = nl = 0
    for p in plans:
        parts.append((p, len(ins), len(outs), nr, nl))
        aliases.update({len(ins) + i: len(outs) + o for i, o in p.aliases.items()})
        ins += p.ins
        outs += p.outs
        nr += p.n_remote
        nl += p.n_local

    def run(which):
        def go(in_refs, out_refs, send, recv, loc):
            for p, i0, o0, r0, l0 in parts:
                getattr(p, which)(in_refs[i0:i0 + len(p.ins)], out_refs[o0:o0 + len(p.outs)],
                                  lambda i, r0=r0: send(r0 + i), lambda i, r0=r0: recv(r0 + i),
                                  lambda i, l0=l0: loc(l0 + i))
        return go

    return _Plan(ins, outs, nr, nl, run("start"), run("wait"), aliases)


def _plan_io(plan, n_in, n_out):
    if plan is None:
        return [], [], [], [], {}
    sems = [pltpu.SemaphoreType.DMA((max(plan.n_remote, 1),)), pltpu.SemaphoreType.DMA((max(plan.n_remote, 1),)),
            pltpu.SemaphoreType.DMA((max(plan.n_local, 1),))]
    aliases = {n_in + i: n_out + o for i, o in plan.aliases.items()}
    return plan.ins, [_ANY] * len(plan.outs), plan.outs, sems, aliases


def _split_refs(refs, n_in, n_out, n_scr, plan):
    p_in, p_out = (len(plan.ins), len(plan.outs)) if plan is not None else (0, 0)
    refs = list(refs)
    ins, refs = refs[:n_in], refs[n_in:]
    pins, refs = refs[:p_in], refs[p_in:]
    outs, refs = refs[:n_out], refs[n_out:]
    pouts, refs = refs[:p_out], refs[p_out:]
    scr, psem = refs[:n_scr], refs[n_scr:]
    psem = tuple((lambda i, s=s: s.at[i]) for s in psem)
    return ins, outs, scr, (pins, pouts, psem)


def _grid_edge(grid, last):
    cond = None
    for ax, n in enumerate(grid):
        c = pl.program_id(ax) == (n - 1 if last else 0)
        cond = c if cond is None else cond & c
    return cond


def _plan_start(plan, pctx, grid):
    if plan is not None:
        pins, pouts, psem = pctx
        pl.when(_grid_edge(grid, False))(lambda: plan.start(pins, pouts, *psem))


def _plan_wait(plan, pctx, grid):
    if plan is not None:
        pins, pouts, psem = pctx
        pl.when(_grid_edge(grid, True))(lambda: plan.wait(pins, pouts, *psem))


def _run_plan(plan, *, name):
    def body(*refs):
        _, _, _, (pins, pouts, psem) = _split_refs(refs, 0, 0, 0, plan)
        plan.start(pins, pouts, *psem)
        plan.wait(pins, pouts, *psem)

    p_in, p_ospec, p_oshape, p_scr, p_alias = _plan_io(plan, 0, 0)
    return pl.pallas_call(body, name=name, in_specs=[_ANY] * len(p_in), out_specs=p_ospec, out_shape=p_oshape,
                          scratch_shapes=p_scr, input_output_aliases=p_alias)(*p_in)


def _fold8(x):
    return x.reshape(x.shape[0] // 8, 8, x.shape[1]).sum(axis=0)


def _ln_stats(r):
    mu = jnp.mean(r, -1, keepdims=True)
    xc = r - mu
    rstd = lax.rsqrt(jnp.mean(xc * xc, -1, keepdims=True) + EPS)
    return xc * rstd, rstd


def _sigmoid(x):
    return 1.0 / (1.0 + jnp.exp(-x))


def _gelu(x):
    return 0.5 * x * (1.0 + jnp.tanh(GELU_C * (x + GELU_A * x * x * x)))


def _gelu_grad(x):
    t = jnp.tanh(GELU_C * (x + GELU_A * x * x * x))
    return 0.5 * (1.0 + t) + 0.5 * x * (1.0 - t * t) * GELU_C * (1.0 + 3.0 * GELU_A * x * x)


MM_ROWS = 1024
DW_TOKENS = 2048


def _matmul(a, b, *, name, M, N, K, ta=False, tb=False, out_dtype=F32, tm=MM_ROWS, tn=1024, tk=1024,
            a_spec=None, b_spec=None, b_merge=None, out_shape=None, o_spec=None, into=None,
            a_sq=False, mul=None, add=None, add_scale=1.0, plan=None):
    tm, tn, tk = min(tm, M), min(tn, N), min(tk, K)
    assert M % tm == 0 and N % tn == 0 and K % tk == 0
    grid = (M // tm, N // tn, K // tk)
    nk = grid[2]
    if a_spec is None:
        a_spec = pl.BlockSpec((tk, tm), lambda i, j, k: (k, i)) if ta else pl.BlockSpec((tm, tk), lambda i, j, k: (i, k))
    if b_spec is None:
        b_spec = pl.BlockSpec((tn, tk), lambda i, j, k: (j, k)) if tb else pl.BlockSpec((tk, tn), lambda i, j, k: (k, j))
    if o_spec is None:
        o_spec = pl.BlockSpec((tm, tn), lambda i, j, k: (i, j))
        out_shape = jax.ShapeDtypeStruct((M, N), out_dtype)
    e_spec = pl.BlockSpec((tm, tn), lambda i, j, k: (i, j))
    dims = (((0 if ta else 1,), (1 if tb else 0,)), ((), ()))
    extra = [e for e in (mul, add, into) if e is not None]
    n_in = 2 + len(extra)

    def body(*refs):
        ins, outs, scr, pctx = _split_refs(refs, n_in, 1, 1 if nk > 1 else 0, plan)
        a_ref, b_ref = ins[0], ins[1]
        rest = list(ins[2:])
        mul_ref = rest.pop(0) if mul is not None else None
        add_ref = rest.pop(0) if add is not None else None
        o_ref = outs[0]
        _plan_start(plan, pctx, grid)
        av = a_ref[...]
        if a_sq:
            av = av * av
        bv = b_ref[...]
        if b_merge is not None:
            bv = bv.reshape(b_merge)
        p = lax.dot_general(av, bv, dims, preferred_element_type=F32)

        def finish(r):
            if mul_ref is not None:
                r = r * (2.0 * mul_ref[...].astype(F32))
            if add_ref is not None:
                r = r + add_scale * add_ref[...]
            o_ref[...] = r.astype(o_ref.dtype)

        if nk == 1:
            finish(p)
        else:
            acc_ref = scr[0]
            k = pl.program_id(2)

            @pl.when(k == 0)
            def _():
                acc_ref[...] = p

            @pl.when(k > 0)
            def _():
                acc_ref[...] += p

            @pl.when(k == nk - 1)
            def _():
                finish(acc_ref[...])

        _plan_wait(plan, pctx, grid)

    p_in, p_ospec, p_oshape, p_scr, p_alias = _plan_io(plan, n_in, 1)
    aliases = dict(p_alias)
    if into is not None:
        aliases[n_in - 1] = 0
    return pl.pallas_call(
        body, name=name, grid=grid,
        in_specs=[a_spec, b_spec] + [e_spec] * (len(extra) - (into is not None)) + [_ANY] * (into is not None)
        + [_ANY] * len(p_in),
        out_specs=[o_spec] + p_ospec, out_shape=[out_shape] + p_oshape,
        scratch_shapes=([pltpu.VMEM((tm, tn), F32)] if nk > 1 else []) + p_scr,
        input_output_aliases=aliases, compiler_params=_params(48, 3),
    )(a, b, *extra, *p_in)


def _rows4_spec(rowblk, n_axes):
    return pl.BlockSpec((4, 256, D), lambda *_: (0, rowblk, 0))


def _proj_ln(a_b, w, h_prev, g, b, *, name, w_rowblk=None, plan=None):
    T = a_b.shape[0]
    tm = min(ROW_BLOCK, T)
    grid = (T // tm,)
    row = pl.BlockSpec((tm, D), lambda i: (i, 0))
    vec = pl.BlockSpec((1, D), lambda i: (0, 0))
    w_spec = pl.BlockSpec((D, D), lambda i: (0, 0)) if w_rowblk is None else _rows4_spec(w_rowblk, 1)

    def body(*refs):
        (a_ref, w_ref, h_ref, g_ref, b_ref), (r_ref, ho_ref, hb_ref), _, pctx = _split_refs(refs, 5, 3, 0, plan)
        _plan_start(plan, pctx, grid)
        mix = jnp.dot(a_ref[...], w_ref[...].reshape(D, D), preferred_element_type=F32)
        r = ALPHA * h_ref[...] + mix
        xhat, _ = _ln_stats(r)
        y = xhat * g_ref[...] + b_ref[...]
        r_ref[...] = r
        ho_ref[...] = y
        hb_ref[...] = y.astype(BF16)
        _plan_wait(plan, pctx, grid)

    p_in, p_ospec, p_oshape, p_scr, p_alias = _plan_io(plan, 5, 3)
    return pl.pallas_call(
        body, name=name, grid=grid,
        in_specs=[row, w_spec, row, vec, vec] + [_ANY] * len(p_in),
        out_specs=[row, row, row] + p_ospec,
        out_shape=[jax.ShapeDtypeStruct((T, D), F32), jax.ShapeDtypeStruct((T, D), F32),
                   jax.ShapeDtypeStruct((T, D), BF16)] + p_oshape,
        scratch_shapes=p_scr, input_output_aliases=p_alias, compiler_params=_params(40, 1),
    )(a_b, w, h_prev, g, b, *p_in)


def _ffn_ln(h_b, wbuf, h, g, b, *, name):
    T = h_b.shape[0]
    tm, tf = min(ROW_BLOCK, T), 1024
    nf = 4
    F = nf * tf
    row = pl.BlockSpec((tm, D), lambda i, j: (i, 0))
    vec = pl.BlockSpec((1, D), lambda i, j: (0, 0))

    def body(hb_ref, w1_ref, w2_ref, h_ref, g_ref, b_ref, ra_ref, r_ref, ho_ref, hbo_ref, acc_ref):
        j = pl.program_id(1)
        a = jnp.dot(hb_ref[...], w1_ref[...], preferred_element_type=F32)
        ra = jnp.maximum(a, 0.0)
        ra_ref[...] = ra.astype(BF16)
        p = jnp.dot((ra * ra).astype(BF16), w2_ref[...], preferred_element_type=F32)

        @pl.when(j == 0)
        def _():
            acc_ref[...] = p

        @pl.when(j > 0)
        def _():
            acc_ref[...] += p

        @pl.when(j == nf - 1)
        def _():
            r = ALPHA * h_ref[...] + acc_ref[...]
            xhat, _ = _ln_stats(r)
            y = xhat * g_ref[...] + b_ref[...]
            r_ref[...] = r
            ho_ref[...] = y
            hbo_ref[...] = y.astype(BF16)

    return pl.pallas_call(
        body, name=name, grid=(T // tm, nf),
        in_specs=[row, pl.BlockSpec((None, D, tf), lambda i, j: (j, 0, 0)),
                  pl.BlockSpec((None, tf, D), lambda i, j: (j, 1, 0)), row, vec, vec],
        out_specs=[pl.BlockSpec((tm, tf), lambda i, j: (i, j)), row, row, row],
        out_shape=[jax.ShapeDtypeStruct((T, F), BF16), jax.ShapeDtypeStruct((T, D), F32),
                   jax.ShapeDtypeStruct((T, D), F32), jax.ShapeDtypeStruct((T, D), BF16)],
        scratch_shapes=[pltpu.VMEM((tm, D), F32)],
        compiler_params=_params(48, 2),
    )(h_b, wbuf, wbuf, h, g, b)


def _loss_dy(y, tgt):
    T = y.shape[0]
    tm = min(ROW_BLOCK, T)
    row = pl.BlockSpec((tm, D), lambda i: (i, 0))

    def body(y_ref, t_ref, dy_ref, ls_ref):
        e = y_ref[...] - t_ref[...]
        dy_ref[...] = e * (1.0 / D)

        @pl.when(pl.program_id(0) == 0)
        def _():
            ls_ref[...] = jnp.zeros_like(ls_ref)

        ls_ref[...] += _fold8(e * e)

    return pl.pallas_call(
        body, name="loss_dy", grid=(T // tm,), in_specs=[row, row],
        out_specs=[row, pl.BlockSpec((8, D), lambda i: (0, 0))],
        out_shape=[jax.ShapeDtypeStruct((T, D), F32), jax.ShapeDtypeStruct((8, D), F32)],
        compiler_params=_params(32, 1),
    )(y, tgt)


def _ln_bwd(dy, r, g, *, name):
    T = dy.shape[0]
    tm = min(ROW_BLOCK, T)
    row = pl.BlockSpec((tm, D), lambda i: (i, 0))
    acc = pl.BlockSpec((8, D), lambda i: (0, 0))

    def body(dy_ref, r_ref, g_ref, dr_ref, drb_ref, dg_ref, db_ref):
        @pl.when(pl.program_id(0) == 0)
        def _():
            dg_ref[...] = jnp.zeros_like(dg_ref)
            db_ref[...] = jnp.zeros_like(db_ref)

        dy_ = dy_ref[...]
        xhat, rstd = _ln_stats(r_ref[...])
        dxh = dy_ * g_ref[...]
        m1 = jnp.mean(dxh, -1, keepdims=True)
        m2 = jnp.mean(dxh * xhat, -1, keepdims=True)
        dr = rstd * (dxh - m1 - xhat * m2)
        dr_ref[...] = dr
        drb_ref[...] = dr.astype(BF16)
        dg_ref[...] += _fold8(dy_ * xhat)
        db_ref[...] += _fold8(dy_)

    return pl.pallas_call(
        body, name=name, grid=(T // tm,),
        in_specs=[row, row, pl.BlockSpec((1, D), lambda i: (0, 0))],
        out_specs=[row, row, acc, acc],
        out_shape=[jax.ShapeDtypeStruct((T, D), F32), jax.ShapeDtypeStruct((T, D), BF16),
                   jax.ShapeDtypeStruct((8, D), F32), jax.ShapeDtypeStruct((8, D), F32)],
        compiler_params=_params(40, 1),
    )(dy, r, g)


def _rope(x, c, s1, s2):
    return x * c + pltpu.roll(x, 112, 1) * s1 + pltpu.roll(x, 16, 1) * s2


def _rope_t(dy, c, s1, s2):
    return dy * c + pltpu.roll(dy * s1, 16, 1) + pltpu.roll(dy * s2, 112, 1)


def _rms(x, g):
    rstd = lax.rsqrt(jnp.mean(x * x, -1, keepdims=True) + EPS)
    xhat = x * rstd
    return xhat * g, xhat, rstd


def _mla_prep(z0, gq, gkv, wq, wk, wv, rc, rs1, rs2):
    T = z0.shape[0]
    tm = min(ROW_BLOCK, T)
    HW = HEADS * 128

    def body(cq_ref, ckv_ref, kr_ref, gq_ref, gkv_ref, wq_ref, wk_ref, wv_ref, c_ref, s1_ref, s2_ref,
             q_ref, k_ref, v_ref):
        nq = _rms(cq_ref[...], gq_ref[...])[0].astype(BF16)
        nkv = _rms(ckv_ref[...], gkv_ref[...])[0].astype(BF16)
        q = jnp.dot(nq, wq_ref[...], preferred_element_type=F32)
        k = jnp.dot(nkv, wk_ref[...], preferred_element_type=F32)
        v = jnp.dot(nkv, wv_ref[...], preferred_element_type=F32)
        c, s1, s2 = c_ref[...], s1_ref[...], s2_ref[...]
        kr = _rope(pltpu.roll(kr_ref[...], 64, 1), c, s1, s2)
        for h in range(HEADS):
            sl = slice(h * 128, (h + 1) * 128)
            q_ref[:, sl] = (_rope(q[:, sl], c, s1, s2) * QK_SCALE).astype(BF16)
            k_ref[:, sl] = (k[:, sl] + kr).astype(BF16)
        v_ref[...] = v.astype(BF16)

    full = lambda shape: pl.BlockSpec(shape, lambda i: (0, 0))
    tab = pl.BlockSpec((tm, 128), lambda i: (i, 0))
    return pl.pallas_call(
        body, name="mla_prep", grid=(T // tm,),
        in_specs=[pl.BlockSpec((tm, 256), lambda i: (i, 0)), pl.BlockSpec((tm, 256), lambda i: (i, 1)),
                  pl.BlockSpec((tm, 128), lambda i: (i, 12)), full((1, 256)), full((1, 256)),
                  full((256, HW)), full((256, HW)), full((256, 512)), tab, tab, tab],
        out_specs=[pl.BlockSpec((tm, HW), lambda i: (i, 0)), pl.BlockSpec((tm, HW), lambda i: (i, 0)),
                   pl.BlockSpec((tm, 512), lambda i: (i, 0))],
        out_shape=[jax.ShapeDtypeStruct((T, HW), BF16), jax.ShapeDtypeStruct((T, HW), BF16),
                   jax.ShapeDtypeStruct((T, 512), BF16)],
        compiler_params=_params(40, 1),
    )(z0, z0, z0, gq, gkv, wq, wk, wv, rc, rs1, rs2)


def _flash_fwd(q, k, v, plan=None):
    T = q.shape[0]
    bq = min(2 * ROW_BLOCK, T)
    nq = T // bq
    grid = (4, nq, nq)

    def body(*refs):
        (q_ref, k_ref, v_ref), (o_ref, lse_ref), (m_sc, l_sc, acc_sc), pctx = _split_refs(refs, 3, 2, 3, plan)
        _plan_start(plan, pctx, grid)
        i, j = pl.program_id(1), pl.program_id(2)
        first = lax.broadcasted_iota(jnp.int32, (bq, 128), 1) < 64

        @pl.when(j == 0)
        def _():
            m_sc[...] = jnp.full_like(m_sc, -jnp.inf)
            l_sc[...] = jnp.zeros_like(l_sc)
            acc_sc[...] = jnp.zeros_like(acc_sc)

        def step(masked):
            vp = v_ref[...]
            acc = acc_sc[...]
            for h in range(2):
                sl = slice(h * 128, (h + 1) * 128)
                s = lax.dot_general(q_ref[:, sl], k_ref[:, sl], NT_DIMS, preferred_element_type=F32)
                if masked:
                    rows = lax.broadcasted_iota(jnp.int32, (bq, bq), 0)
                    cols = lax.broadcasted_iota(jnp.int32, (bq, bq), 1)
                    s = jnp.where(cols <= rows, s, -jnp.inf)
                m_prev = m_sc[h, :, 0:1]
                m_new = jnp.maximum(m_prev, jnp.max(s, axis=1, keepdims=True))
                alpha = jnp.exp(m_prev - m_new)
                p = jnp.exp(s - m_new)
                l_new = alpha * l_sc[h, :, 0:1] + jnp.sum(p, axis=1, keepdims=True)
                pv = jnp.dot(p.astype(BF16), vp, preferred_element_type=F32)
                mine = first if h == 0 else jnp.logical_not(first)
                acc = jnp.where(mine, acc * alpha + pv, acc)
                m_sc[h] = jnp.broadcast_to(m_new, (bq, 128))
                l_sc[h] = jnp.broadcast_to(l_new, (bq, 128))
            acc_sc[...] = acc

        @pl.when(j < i)
        def _():
            step(False)

        @pl.when(j == i)
        def _():
            step(True)
            l0, l1 = l_sc[0], l_sc[1]
            o_ref[...] = (acc_sc[...] / jnp.where(first, l0, l1)).astype(BF16)
            lse_ref[...] = jnp.where(first, m_sc[0] + jnp.log(l0), m_sc[1] + jnp.log(l1))

        _plan_wait(plan, pctx, grid)

    kv = lambda hp, i, j: (jnp.minimum(i, j), hp)
    p_in, p_ospec, p_oshape, p_scr, p_alias = _plan_io(plan, 3, 2)
    return pl.pallas_call(
        body, name="flash_fwd", grid=grid,
        in_specs=[pl.BlockSpec((bq, 256), lambda hp, i, j: (i, hp)), pl.BlockSpec((bq, 256), kv),
                  pl.BlockSpec((bq, 128), kv)] + [_ANY] * len(p_in),
        out_specs=[pl.BlockSpec((bq, 128), lambda hp, i, j: (i, hp)),
                   pl.BlockSpec((bq, 128), lambda hp, i, j: (i, hp))] + p_ospec,
        out_shape=[jax.ShapeDtypeStruct((T, 512), BF16), jax.ShapeDtypeStruct((T, 512), F32)] + p_oshape,
        scratch_shapes=[pltpu.VMEM((2, bq, 128), F32), pltpu.VMEM((2, bq, 128), F32), pltpu.VMEM((bq, 128), F32)] + p_scr,
        input_output_aliases=p_alias, compiler_params=_params(56, 3),
    )(q, k, v, *p_in)


def _attn_delta(dmix, o):
    T = o.shape[0]
    tm = min(ROW_BLOCK, T)
    blk = pl.BlockSpec((tm, 512), lambda i: (i, 0))

    def body(do_ref, o_ref, delta_ref, dob_ref):
        first = lax.broadcasted_iota(jnp.int32, (tm, 128), 1) < 64
        for hp in range(4):
            sl = slice(hp * 128, (hp + 1) * 128)
            prod = do_ref[:, sl] * o_ref[:, sl].astype(F32)
            d0 = jnp.sum(jnp.where(first, prod, 0.0), axis=1, keepdims=True)
            d1 = jnp.sum(jnp.where(first, 0.0, prod), axis=1, keepdims=True)
            delta_ref[:, sl] = jnp.where(first, d0, d1)
        dob_ref[...] = do_ref[...].astype(BF16)

    return pl.pallas_call(
        body, name="attn_delta", grid=(T // tm,), in_specs=[blk, blk], out_specs=[blk, blk],
        out_shape=[jax.ShapeDtypeStruct((T, 512), F32), jax.ShapeDtypeStruct((T, 512), BF16)],
        compiler_params=_params(32, 1),
    )(dmix, o)


def _flash_bwd(q, k, v, do_b, lse, delta, plan=None):
    T = q.shape[0]
    bq = min(2 * ROW_BLOCK, T)
    nq = T // bq
    grid = (4, nq, nq)

    def body(*refs):
        ((q_ref, k_ref, v_ref, do_ref, lse_ref, dl_ref), (dq_hbm, dk_ref, dv_ref), (dq_sc, dk_sc, dv_sc, sem),
         pctx) = _split_refs(refs, 6, 3, 4, plan)
        _plan_start(plan, pctx, grid)
        hp, j, i = pl.program_id(0), pl.program_id(1), pl.program_id(2)
        first = lax.broadcasted_iota(jnp.int32, (bq, 128), 1) < 64

        @pl.when((j == 0) & (i == 0))
        def _():
            dq_sc[...] = jnp.zeros_like(dq_sc)

        @pl.when(i == j)
        def _():
            dk_sc[...] = jnp.zeros_like(dk_sc)
            dv_sc[...] = jnp.zeros_like(dv_sc)

        def step(masked):
            vp = v_ref[...]
            do = do_ref[...]
            for h in range(2):
                sl = slice(h * 128, (h + 1) * 128)
                qh, kh = q_ref[:, sl], k_ref[:, sl]
                s = lax.dot_general(qh, kh, NT_DIMS, preferred_element_type=F32)
                p = jnp.exp(s - lse_ref[:, h * 64:h * 64 + 1])
                if masked:
                    rows = lax.broadcasted_iota(jnp.int32, (bq, bq), 0)
                    cols = lax.broadcasted_iota(jnp.int32, (bq, bq), 1)
                    p = jnp.where(cols <= rows, p, 0.0)
                mine = first if h == 0 else jnp.logical_not(first)
                do_h = jnp.where(mine, do, jnp.zeros_like(do))
                dv_sc[...] += lax.dot_general(p.astype(BF16), do_h, TN_DIMS, preferred_element_type=F32)
                dp = lax.dot_general(do_h, vp, NT_DIMS, preferred_element_type=F32)
                ds = (p * (dp - dl_ref[:, h * 64:h * 64 + 1])).astype(BF16)
                dq_sc[i, :, sl] += jnp.dot(ds, kh, preferred_element_type=F32)
                dk_sc[:, sl] += lax.dot_general(ds, qh, TN_DIMS, preferred_element_type=F32)

        @pl.when(i > j)
        def _():
            step(False)

        @pl.when(i == j)
        def _():
            step(True)

        @pl.when(i == nq - 1)
        def _():
            dk_ref[...] = dk_sc[...]
            dv_ref[...] = dv_sc[...]

        @pl.when((j == nq - 1) & (i == nq - 1))
        def _():
            cp = pltpu.make_async_copy(dq_sc, dq_hbm.at[hp], sem)
            cp.start()
            cp.wait()

        _plan_wait(plan, pctx, grid)

    qi = lambda hp, j, i: (jnp.maximum(i, j), hp)
    kj = lambda hp, j, i: (j, hp)
    p_in, p_ospec, p_oshape, p_scr, p_alias = _plan_io(plan, 6, 3)
    return pl.pallas_call(
        body, name="flash_bwd", grid=grid,
        in_specs=[pl.BlockSpec((bq, 256), qi), pl.BlockSpec((bq, 256), kj), pl.BlockSpec((bq, 128), kj),
                  pl.BlockSpec((bq, 128), qi), pl.BlockSpec((bq, 128), qi), pl.BlockSpec((bq, 128), qi)]
        + [_ANY] * len(p_in),
        out_specs=[_ANY, pl.BlockSpec((bq, 256), kj), pl.BlockSpec((bq, 128), kj)] + p_ospec,
        out_shape=[jax.ShapeDtypeStruct((4, nq, bq, 256), F32), jax.ShapeDtypeStruct((T, 1024), F32),
                   jax.ShapeDtypeStruct((T, 512), F32)] + p_oshape,
        scratch_shapes=[pltpu.VMEM((nq, bq, 256), F32), pltpu.VMEM((bq, 256), F32), pltpu.VMEM((bq, 128), F32),
                        pltpu.SemaphoreType.DMA] + p_scr,
        input_output_aliases=p_alias, compiler_params=_params(56, 3),
    )(q, k, v, do_b, lse, delta, *p_in)


def _mla_bwd(z0, dq4, dk, dv, gq, gkv, wq, wk, wv, rc, rs1, rs2, plan=None):
    T = z0.shape[0]
    tm = min(ROW_BLOCK, T)
    HW = HEADS * 128
    grid = (T // tm,)
    dq4 = dq4.reshape(4, T, 256)

    def body(*refs):
        ((cq_ref, ckv_ref, dq_ref, dk_ref, dv_ref, gq_ref, gkv_ref, wq_ref, wk_ref, wv_ref, c_ref, s1_ref, s2_ref),
         (dc_ref, dkr_ref, dwq_ref, dwk_ref, dwv_ref, dgq_ref, dgkv_ref), _, pctx) = _split_refs(refs, 13, 7, 0, plan)
        _plan_start(plan, pctx, grid)

        @pl.when(pl.program_id(0) == 0)
        def _():
            for ref in (dwq_ref, dwk_ref, dwv_ref, dgq_ref, dgkv_ref):
                ref[...] = jnp.zeros_like(ref)

        c, s1, s2 = c_ref[...], s1_ref[...], s2_ref[...]
        lane = lax.broadcasted_iota(jnp.int32, (tm, 128), 1)
        nq, xq, rq = _rms(cq_ref[...], gq_ref[...])
        nkv, xkv, rkv = _rms(ckv_ref[...], gkv_ref[...])
        nq_b, nkv_b = nq.astype(BF16), nkv.astype(BF16)

        dq_parts, dk_parts = [], []
        dkr = jnp.zeros((tm, 128), F32)
        for h in range(HEADS):
            blk = dq_ref[h // 2, :, (h % 2) * 128:(h % 2 + 1) * 128] * QK_SCALE
            dq_parts.append(_rope_t(blk, c, s1, s2).astype(BF16))
            kb = dk_ref[:, h * 128:(h + 1) * 128]
            dk_parts.append(jnp.where(lane < NOPE, kb, 0.0).astype(BF16))
            dkr = dkr + kb
        dq_b = jnp.concatenate(dq_parts, axis=1)
        dk_b = jnp.concatenate(dk_parts, axis=1)
        dv_b = dv_ref[...].astype(BF16)

        dwq_ref[...] += lax.dot_general(nq_b, dq_b, TN_DIMS, preferred_element_type=F32)
        dwk_ref[...] += lax.dot_general(nkv_b, dk_b, TN_DIMS, preferred_element_type=F32)
        dwv_ref[...] += lax.dot_general(nkv_b, dv_b, TN_DIMS, preferred_element_type=F32)
        dnq = lax.dot_general(dq_b, wq_ref[...], NT_DIMS, preferred_element_type=F32)
        dnkv = (lax.dot_general(dk_b, wk_ref[...], NT_DIMS, preferred_element_type=F32)
                + lax.dot_general(dv_b, wv_ref[...], NT_DIMS, preferred_element_type=F32))

        def rms_bwd(dn, xhat, rstd, g):
            dxh = dn * g
            return rstd * (dxh - xhat * jnp.mean(dxh * xhat, -1, keepdims=True))

        dc_ref[:, :256] = rms_bwd(dnq, xq, rq, gq_ref[...]).astype(BF16)
        dc_ref[:, 256:] = rms_bwd(dnkv, xkv, rkv, gkv_ref[...]).astype(BF16)
        dgq_ref[...] += _fold8(dnq * xq)
        dgkv_ref[...] += _fold8(dnkv * xkv)
        dkr = pltpu.roll(_rope_t(dkr, c, s1, s2), 64, 1)
        dkr_ref[...] = jnp.where(lane < ROPE, dkr, 0.0).astype(BF16)
        _plan_wait(plan, pctx, grid)

    full = lambda shape: pl.BlockSpec(shape, lambda i: (0,) * len(shape))
    tab = pl.BlockSpec((tm, 128), lambda i: (i, 0))
    p_in, p_ospec, p_oshape, p_scr, p_alias = _plan_io(plan, 13, 7)
    return pl.pallas_call(
        body, name="mla_bwd", grid=grid,
        in_specs=[pl.BlockSpec((tm, 256), lambda i: (i, 0)), pl.BlockSpec((tm, 256), lambda i: (i, 1)),
                  pl.BlockSpec((4, tm, 256), lambda i: (0, i, 0)),
                  pl.BlockSpec((tm, HW), lambda i: (i, 0)), pl.BlockSpec((tm, 512), lambda i: (i, 0)),
                  full((1, 256)), full((1, 256)), full((256, HW)), full((256, HW)), full((256, 512)), tab, tab, tab]
        + [_ANY] * len(p_in),
        out_specs=[pl.BlockSpec((tm, 512), lambda i: (i, 0)), tab, full((256, HW)), full((256, HW)),
                   full((256, 512)), full((8, 256)), full((8, 256))] + p_ospec,
        out_shape=[jax.ShapeDtypeStruct((T, 512), BF16), jax.ShapeDtypeStruct((T, 128), BF16),
                   jax.ShapeDtypeStruct((256, HW), F32), jax.ShapeDtypeStruct((256, HW), F32),
                   jax.ShapeDtypeStruct((256, 512), F32), jax.ShapeDtypeStruct((8, 256), F32),
                   jax.ShapeDtypeStruct((8, 256), F32)] + p_oshape,
        scratch_shapes=p_scr, input_output_aliases=p_alias, compiler_params=_params(48, 1),
    )(z0, z0, dq4, dk, dv, gq, gkv, wq, wk, wv, rc, rs1, rs2, *p_in)


def _sgu_fwd(z0, a_out, ln_g, ln_b, w, b_t):
    T = z0.shape[0]
    tm = min(ROW_BLOCK, T)
    W = SGU_G * SGU_C

    def body(u_ref, v_ref, a_ref, g_ref, b_ref, w_ref, bt_ref, o_ref):
        o_ref[:, :W] = a_ref[...]
        ug = _gelu(u_ref[...])
        xhat, _ = _ln_stats(_gelu(v_ref[...]))
        vn = (xhat * g_ref[...] + b_ref[...]).astype(BF16)
        tril = lax.broadcasted_iota(jnp.int32, (SGU_C, SGU_C), 0) >= lax.broadcasted_iota(jnp.int32, (SGU_C, SGU_C), 1)
        for g in range(SGU_G):
            cs = slice(g * SGU_C, (g + 1) * SGU_C)
            wg = jnp.where(tril, w_ref[g], 0.0).astype(BF16)
            bcol = bt_ref[:, g:g + 1]
            for c in range(tm // SGU_C):
                rs = slice(c * SGU_C, (c + 1) * SGU_C)
                mixed = jnp.dot(wg, vn[rs, cs], preferred_element_type=F32) + bcol
                o_ref[rs, W + g * SGU_C:W + (g + 1) * SGU_C] = (ug[rs, cs] * mixed).astype(BF16)

    full = lambda shape: pl.BlockSpec(shape, lambda i: (0,) * len(shape))
    return pl.pallas_call(
        body, name="sgu_fwd", grid=(T // tm,),
        in_specs=[pl.BlockSpec((tm, W), lambda i: (i, 1)), pl.BlockSpec((tm, W), lambda i: (i, 2)),
                  pl.BlockSpec((tm, W), lambda i: (i, 0)),
                  full((1, W)), full((1, W)), full((SGU_G, SGU_C, SGU_C)), full((SGU_C, SGU_G))],
        out_specs=pl.BlockSpec((tm, 2 * W), lambda i: (i, 0)),
        out_shape=jax.ShapeDtypeStruct((T, 2 * W), BF16),
        compiler_params=_params(32, 1),
    )(z0, z0, a_out, ln_g, ln_b, w, b_t)


def _sgu_bwd(z0, dmix, dc, dkr, ln_g, ln_b, w, b_t):
    T = z0.shape[0]
    tm = min(ROW_BLOCK, T)
    W = SGU_G * SGU_C

    def body(u_ref, v_ref, do_ref, dc_ref, dkr_ref, g_ref, b_ref, w_ref, bt_ref, dz_ref, dw_ref, db_ref, dlg_ref,
             dlb_ref):
        @pl.when(pl.program_id(0) == 0)
        def _():
            for ref in (dw_ref, db_ref, dlg_ref, dlb_ref):
                ref[...] = jnp.zeros_like(ref)

        dz_ref[:, :W] = dc_ref[...]
        dz_ref[:, 3 * W:] = dkr_ref[...]

        u, v, dout = u_ref[...], v_ref[...], do_ref[...]
        ug = _gelu(u)
        xhat, rstd = _ln_stats(_gelu(v))
        vn = (xhat * g_ref[...] + b_ref[...]).astype(BF16)
        dmixed = dout * ug
        dmixed_b = dmixed.astype(BF16)
        tril = lax.broadcasted_iota(jnp.int32, (SGU_C, SGU_C), 0) >= lax.broadcasted_iota(jnp.int32, (SGU_C, SGU_C), 1)
        lane = lax.broadcasted_iota(jnp.int32, (SGU_C, SGU_C), 1)
        dvn_cols = []
        for g in range(SGU_G):
            cs = slice(g * SGU_C, (g + 1) * SGU_C)
            wg = jnp.where(tril, w_ref[g], 0.0).astype(BF16)
            bcol = bt_ref[:, g:g + 1]
            dw_g = jnp.zeros((SGU_C, SGU_C), F32)
            db_g = jnp.zeros((SGU_C, 1), F32)
            dvn_rows = []
            for c in range(tm // SGU_C):
                rs = slice(c * SGU_C, (c + 1) * SGU_C)
                mixed = jnp.dot(wg, vn[rs, cs], preferred_element_type=F32) + bcol
                dz_ref[rs, W + g * SGU_C:W + (g + 1) * SGU_C] = (dout[rs, cs] * mixed * _gelu_grad(u[rs, cs])).astype(BF16)
                dm = dmixed_b[rs, cs]
                dw_g = dw_g + lax.dot_general(dm, vn[rs, cs], NT_DIMS, preferred_element_type=F32)
                db_g = db_g + jnp.sum(dmixed[rs, cs], axis=1, keepdims=True)
                dvn_rows.append(lax.dot_general(wg, dm, TN_DIMS, preferred_element_type=F32))
            dw_ref[g] += jnp.where(tril, dw_g, 0.0)
            db_ref[...] += jnp.where(lane == g, db_g, 0.0)
            dvn_cols.append(jnp.concatenate(dvn_rows, axis=0))
        dvn = jnp.concatenate(dvn_cols, axis=1)
        dxh = dvn * g_ref[...]
        m1 = jnp.mean(dxh, -1, keepdims=True)
        m2 = jnp.mean(dxh * xhat, -1, keepdims=True)
        dvg = rstd * (dxh - m1 - xhat * m2)
        dz_ref[:, 2 * W:3 * W] = (dvg * _gelu_grad(v)).astype(BF16)
        dlg_ref[...] += _fold8(dvn * xhat)
        dlb_ref[...] += _fold8(dvn)

    full = lambda shape: pl.BlockSpec(shape, lambda i: (0,) * len(shape))
    return pl.pallas_call(
        body, name="sgu_bwd", grid=(T // tm,),
        in_specs=[pl.BlockSpec((tm, W), lambda i: (i, 1)), pl.BlockSpec((tm, W), lambda i: (i, 2)),
                  pl.BlockSpec((tm, W), lambda i: (i, 1)), pl.BlockSpec((tm, W), lambda i: (i, 0)),
                  pl.BlockSpec((tm, 128), lambda i: (i, 0)),
                  full((1, W)), full((1, W)), full((SGU_G, SGU_C, SGU_C)), full((SGU_C, SGU_G))],
        out_specs=[pl.BlockSpec((tm, 3 * W + 128), lambda i: (i, 0)), full((SGU_G, SGU_C, SGU_C)),
                   full((SGU_C, SGU_C)), full((8, W)), full((8, W))],
        out_shape=[jax.ShapeDtypeStruct((T, 3 * W + 128), BF16), jax.ShapeDtypeStruct((SGU_G, SGU_C, SGU_C), F32),
                   jax.ShapeDtypeStruct((SGU_C, SGU_C), F32), jax.ShapeDtypeStruct((8, W), F32),
                   jax.ShapeDtypeStruct((8, W), F32)],
        compiler_params=_params(40, 1),
    )(z0, z0, dmix, dc, dkr, ln_g, ln_b, w, b_t)


def _hg_lower_bound(lb_ref):
    a0, a1 = lb_ref[0:1, :], lb_ref[1:2, :]
    m = jnp.maximum(a0, a1)
    e0, e1 = jnp.exp(a0 - m), jnp.exp(a1 - m)
    return e1 / (e0 + e1)


def _running_sum(x, reverse=False):
    n = x.shape[0]
    row = lax.broadcasted_iota(jnp.int32, x.shape, 0)
    s = 1
    while s < n:
        if reverse:
            x = x + jnp.where(row < n - s, pltpu.roll(x, n - s, 0), 0.0)
        else:
            x = x + jnp.where(row >= s, pltpu.roll(x, s, 0), 0.0)
        s *= 2
    return x


def _hg_chunk(qc, fc, lb):
    C = HG_CHUNK
    rows = lax.broadcasted_iota(jnp.int32, (C, C), 0)
    cols = lax.broadcasted_iota(jnp.int32, (C, C), 1)
    rowid = lax.broadcasted_iota(jnp.int32, (C, 128), 0)
    sq, sg = _sigmoid(qc), _sigmoid(fc)
    qf = qc * sq
    gate = lb + (1.0 - lb) * sg
    kk = 1.0 - gate
    lg = jnp.log(gate)
    bcum = _running_sum(lg)
    b_mid = jnp.sum(jnp.where(rowid < C // 2, lg, 0.0), axis=0, keepdims=True)
    b_last = jnp.sum(lg, axis=0, keepdims=True)
    eq, ek, e, eh = jnp.exp(bcum - b_mid), jnp.exp(b_mid - bcum), jnp.exp(bcum), jnp.exp(b_last - bcum)
    qt, kt, qe, khat = qf * eq, kk * ek, qf * e, kk * eh
    a = lax.dot_general(qt.astype(BF16), kt.astype(BF16), NT_DIMS, preferred_element_type=F32)
    a = jnp.where(rows >= cols, a, 0.0)
    return dict(sq=sq, sg=sg, gate=gate, kk=kk, eq=eq, ek=ek, e=e, eh=eh, qt=qt, kt=kt, qe=qe, khat=khat, a=a,
                e_last=jnp.exp(b_last), tril=rows >= cols, rowid=rowid)


def _hgrn_fwd(z4, hg_lb, gnorm):
    T = z4.shape[1]
    tb = min(ROW_BLOCK, T)
    C = HG_CHUNK
    ncb = tb // C
    HPB = HG_HEADS_PER_STEP

    def body(q_ref, f_ref, i_ref, g_ref, lb_ref, gn_ref, y_ref, o_ref, st_ref, st_sc):
        @pl.when(pl.program_id(1) == 0)
        def _():
            st_sc[...] = jnp.zeros_like(st_sc)

        def chunk(c, carry):
            rs = pl.ds(pl.multiple_of(c * C, C), C)
            for hh in range(HPB):
                hs = slice(hh * 128, (hh + 1) * 128)
                lb = _hg_lower_bound(lb_ref.at[:, hs])
                v_b = i_ref[rs, hs].astype(BF16)
                gc = g_ref[rs, hs]
                x = _hg_chunk(q_ref[rs, hs], f_ref[rs, hs], lb)
                st = st_sc[hh]
                st_ref[hh, c] = st
                o = (jnp.dot(x["a"].astype(BF16), v_b, preferred_element_type=F32)
                     + lax.dot_general(x["qe"].astype(BF16), st.astype(BF16), NT_DIMS, preferred_element_type=F32))
                st_sc[hh] = st * x["e_last"] + lax.dot_general(v_b, x["khat"].astype(BF16), TN_DIMS,
                                                               preferred_element_type=F32)
                o_ref[rs, hs] = o
                n = o * lax.rsqrt(jnp.mean(o * o, -1, keepdims=True) + EPS)
                y_ref[rs, hs] = (n * gn_ref[:, hs] * (gc * _sigmoid(gc))).astype(BF16)
            return carry

        lax.fori_loop(0, ncb, chunk, 0)

    W = 128 * HPB
    zb = lambda k: pl.BlockSpec((None, tb, W), lambda h, t: (k, t, h))
    out = pl.BlockSpec((tb, W), lambda h, t: (t, h))
    return pl.pallas_call(
        body, name="hgrn_fwd", grid=(HEADS // HPB, T // tb),
        in_specs=[zb(0), zb(1), zb(2), zb(3), pl.BlockSpec((2, W), lambda h, t: (0, h)),
                  pl.BlockSpec((1, W), lambda h, t: (0, h))],
        out_specs=[out, out, pl.BlockSpec((HPB, ncb, 128, 128), lambda h, t: (h, t, 0, 0))],
        out_shape=[jax.ShapeDtypeStruct((T, D), BF16), jax.ShapeDtypeStruct((T, D), F32),
                   jax.ShapeDtypeStruct((HEADS, T // C, 128, 128), F32)],
        scratch_shapes=[pltpu.VMEM((HPB, 128, 128), F32)],
        compiler_params=_params(32, 2),
    )(z4, z4, z4, z4, hg_lb, gnorm)


def _hgrn_bwd(z4, o_raw, dy, states, hg_lb, gnorm):
    T = z4.shape[1]
    tb = min(ROW_BLOCK, T)
    C = HG_CHUNK
    ncb = tb // C
    nt = T // tb
    HPB = HG_HEADS_PER_STEP

    def body(q_ref, f_ref, i_ref, g_ref, o_ref, dy_ref, st_ref, lb_ref, gn_ref, dz_ref, dlb_ref, dgn_ref, dst_sc):
        @pl.when(pl.program_id(1) == 0)
        def _():
            dst_sc[...] = jnp.zeros_like(dst_sc)
            dlb_ref[...] = jnp.zeros_like(dlb_ref)
            dgn_ref[...] = jnp.zeros_like(dgn_ref)

        def chunk(cc, carry):
            for hh in range(HPB):
                one_head(ncb - 1 - cc, hh, slice(hh * 128, (hh + 1) * 128))
            return carry

        def one_head(c, hh, hs):
            rs = pl.ds(pl.multiple_of(c * C, C), C)
            lb = _hg_lower_bound(lb_ref.at[:, hs])
            gn = gn_ref[:, hs]
            qc, gc = q_ref[rs, hs], g_ref[rs, hs]
            v_b = i_ref[rs, hs].astype(BF16)
            x = _hg_chunk(qc, f_ref[rs, hs], lb)
            st, dst = st_ref[hh, c], dst_sc[hh]
            st_b, dst_b = st.astype(BF16), dst.astype(BF16)
            o, dyc = o_ref[rs, hs], dy_ref[rs, hs]
            sgg = _sigmoid(gc)
            sil = gc * sgg
            rstd = lax.rsqrt(jnp.mean(o * o, -1, keepdims=True) + EPS)
            n = o * rstd
            dgn_ref[:, hs] += _fold8(dyc * n * sil)
            dn = dyc * gn * sil
            do = rstd * (dn - n * jnp.mean(dn * n, -1, keepdims=True))
            dg = dyc * n * gn * (sgg * (1.0 + gc * (1.0 - sgg)))
            do_b = do.astype(BF16)
            da = jnp.where(x["tril"], lax.dot_general(do_b, v_b, NT_DIMS, preferred_element_type=F32), 0.0).astype(BF16)
            qt_b, kt_b, qe_b, khat_b = (x[n_].astype(BF16) for n_ in ("qt", "kt", "qe", "khat"))
            dv = (lax.dot_general(x["a"].astype(BF16), do_b, TN_DIMS, preferred_element_type=F32)
                  + lax.dot_general(khat_b, dst_b, NT_DIMS, preferred_element_type=F32))
            dqt = jnp.dot(da, kt_b, preferred_element_type=F32)
            dqe = jnp.dot(do_b, st_b, preferred_element_type=F32)
            dkt = lax.dot_general(da, qt_b, TN_DIMS, preferred_element_type=F32)
            dkhat = jnp.dot(v_b, dst_b, preferred_element_type=F32)
            dst_sc[hh] = lax.dot_general(do_b, qe_b, TN_DIMS, preferred_element_type=F32) + dst * x["e_last"]
            de_last = jnp.sum(st * dst, axis=0, keepdims=True)
            dqf = dqt * x["eq"] + dqe * x["e"]
            dkk = dkt * x["ek"] + dkhat * x["eh"]
            dkh_kh = dkhat * x["khat"]
            db = dqt * qt_b.astype(F32) - dkt * kt_b.astype(F32) + dqe * x["qe"] - dkh_kh
            db_last = jnp.sum(dkh_kh, axis=0, keepdims=True) + de_last * x["e_last"]
            db = db + jnp.where(x["rowid"] == C - 1, db_last, 0.0)
            dlg = _running_sum(db, reverse=True)
            dgate = dlg / x["gate"] - dkk
            sg, sq = x["sg"], x["sq"]
            dlb_ref[:, hs] += _fold8(dgate * (1.0 - sg)) * (lb * (1.0 - lb))
            dz_ref[0, rs, hs] = (dqf * (sq * (1.0 + qc * (1.0 - sq)))).astype(BF16)
            dz_ref[1, rs, hs] = (dgate * (1.0 - lb) * sg * (1.0 - sg)).astype(BF16)
            dz_ref[2, rs, hs] = dv.astype(BF16)
            dz_ref[3, rs, hs] = dg.astype(BF16)

        lax.fori_loop(0, ncb, chunk, 0)

    W = 128 * HPB
    zb = lambda k: pl.BlockSpec((None, tb, W), lambda h, t: (k, nt - 1 - t, h))
    blk = pl.BlockSpec((tb, W), lambda h, t: (nt - 1 - t, h))
    acc = pl.BlockSpec((8, W), lambda h, t: (0, h))
    return pl.pallas_call(
        body, name="hgrn_bwd", grid=(HEADS // HPB, nt),
        in_specs=[zb(0), zb(1), zb(2), zb(3), blk, blk,
                  pl.BlockSpec((HPB, ncb, 128, 128), lambda h, t: (h, nt - 1 - t, 0, 0)),
                  pl.BlockSpec((2, W), lambda h, t: (0, h)), pl.BlockSpec((1, W), lambda h, t: (0, h))],
        out_specs=[pl.BlockSpec((4, tb, W), lambda h, t: (0, nt - 1 - t, h)), acc, acc],
        out_shape=[jax.ShapeDtypeStruct((4, T, D), BF16), jax.ShapeDtypeStruct((8, D), F32),
                   jax.ShapeDtypeStruct((8, D), F32)],
        scratch_shapes=[pltpu.VMEM((HPB, 128, 128), F32)],
        compiler_params=_params(32, 2),
    )(z4, z4, z4, z4, o_raw, dy, states, hg_lb, gnorm)


def _adamw(w, g, m, v, *, name):
    R, L = w.shape
    tr = R if R <= 512 else 512
    assert R % tr == 0
    blk = pl.BlockSpec((tr, L), lambda i: (i, 0))
    c1, c2 = 1.0 - B1 ** STEP, 1.0 - B2 ** STEP

    def body(w_ref, g_ref, m_ref, v_ref, d_ref, mo_ref, vo_ref):
        g_ = g_ref[...]
        m_ = B1 * m_ref[...] + (1.0 - B1) * g_
        v_ = B2 * v_ref[...] + (1.0 - B2) * (g_ * g_)
        d_ref[...] = -LR * ((m_ / c1) / (jnp.sqrt(v_ / c2) + ADAM_EPS) + WD * w_ref[...])
        mo_ref[...] = m_
        vo_ref[...] = v_

    sds = jax.ShapeDtypeStruct((R, L), F32)
    return pl.pallas_call(
        body, name=name, grid=(R // tr,), in_specs=[blk] * 4, out_specs=[blk] * 3, out_shape=[sds] * 3,
        compiler_params=_params(32, 1),
    )(w, g, m, v)


def _adamw_rows(w, m, v, gbufs, row0, *, name, plan=None):
    L, R, C = w.shape
    tr = 256
    assert R % tr == 0 and row0 % tr == 0 and len(gbufs) == L
    grid = (L, R // tr)
    blk = pl.BlockSpec((None, tr, C), lambda l, i: (l, i, 0))
    gblk = pl.BlockSpec((tr, C), lambda l, i: (row0 // tr + i, 0))
    c1, c2 = 1.0 - B1 ** STEP, 1.0 - B2 ** STEP

    def body(*refs):
        ins, (go_ref, d_ref, mo_ref, vo_ref), _, pctx = _split_refs(refs, 3 + L, 4, 0, plan)
        w_ref, m_ref, v_ref = ins[:3]
        g_refs = ins[3:]
        _plan_start(plan, pctx, grid)
        g_ = g_refs[0][...]
        for l in range(1, L):
            g_ = jnp.where(pl.program_id(0) == l, g_refs[l][...], g_)
        m_ = B1 * m_ref[...] + (1.0 - B1) * g_
        v_ = B2 * v_ref[...] + (1.0 - B2) * (g_ * g_)
        go_ref[...] = g_
        d_ref[...] = -LR * ((m_ / c1) / (jnp.sqrt(v_ / c2) + ADAM_EPS) + WD * w_ref[...])
        mo_ref[...] = m_
        vo_ref[...] = v_
        _plan_wait(plan, pctx, grid)

    sds = jax.ShapeDtypeStruct((L, R, C), F32)
    p_in, p_ospec, p_oshape, p_scr, p_alias = _plan_io(plan, 3 + L, 4)
    return pl.pallas_call(
        body, name=name, grid=grid, in_specs=[blk] * 3 + [gblk] * L + [_ANY] * len(p_in),
        out_specs=[blk] * 4 + p_ospec, out_shape=[sds] * 4 + p_oshape, scratch_shapes=p_scr,
        input_output_aliases=p_alias, compiler_params=_params(32, 2),
    )(w, m, v, *gbufs, *p_in)


def _add_pairs(g, theirs, ids, *, name):
    n, R, L = theirs.shape
    tr = 128
    nb = R // tr

    def body(ids_ref, a_ref, b_ref, o_ref):
        o_ref[...] = (a_ref[...].astype(F32) + b_ref[...].astype(F32)).astype(BF16)

    blk = pl.BlockSpec((n, tr, L), lambda i, ids: (0, i, 0))
    return pl.pallas_call(
        body, name=name, out_shape=jax.ShapeDtypeStruct((n, R, L), BF16),
        grid_spec=pltpu.PrefetchScalarGridSpec(
            num_scalar_prefetch=1, grid=(nb,),
            in_specs=[pl.BlockSpec((n, tr, L), lambda i, ids: (0, ids[1] * nb + i, 0)), blk], out_specs=blk),
        compiler_params=_params(16, 1),
    )(ids, g, theirs)


def _sum_chips(pair, parts, ids, *, name):
    _, R, L = parts.shape
    tr = 128

    def body(ids_ref, o_ref, r_ref, out_ref):
        out_ref[...] = ((o_ref[...].astype(F32) + r_ref[0].astype(F32)) + r_ref[1].astype(F32)) + r_ref[2].astype(F32)

    return pl.pallas_call(
        body, name=name, out_shape=jax.ShapeDtypeStruct((2, R, L), F32),
        grid_spec=pltpu.PrefetchScalarGridSpec(
            num_scalar_prefetch=1, grid=(R // tr,),
            in_specs=[pl.BlockSpec((None, tr, L), lambda i, ids: (ids[0], i, 0)),
                      pl.BlockSpec((3, tr, L), lambda i, ids: (0, i, 0))],
            out_specs=pl.BlockSpec((None, tr, L), lambda i, ids: (ids[1], i, 0))),
        compiler_params=_params(32, 1),
    )(ids, pair, parts)


def _mesh_ids():
    x, y, c = _mesh_pos()
    return jnp.stack([2 * x + y, c]).astype(jnp.int32)


def _place_shard(rows, ids, *, name):
    R, L = rows.shape
    tr = 256

    def body(ids_ref, in_ref, out_ref):
        out_ref[...] = in_ref[...].astype(BF16)

    return pl.pallas_call(
        body, name=name, out_shape=jax.ShapeDtypeStruct((4, R, L), BF16),
        grid_spec=pltpu.PrefetchScalarGridSpec(
            num_scalar_prefetch=1, grid=(R // tr,), in_specs=[pl.BlockSpec((tr, L), lambda i, ids: (i, 0))],
            out_specs=pl.BlockSpec((None, tr, L), lambda i, ids: (ids[0], i, 0))),
        compiler_params=_params(16, 1),
    )(ids, rows)


def _remote(src, dst, send_sem, recv_sem, to):
    return pltpu.make_async_remote_copy(src_ref=src, dst_ref=dst, send_sem=send_sem, recv_sem=recv_sem,
                                        device_id=to, device_id_type=MESH_IDS)


def _rows(ref, lead, start, size):
    return ref.at[tuple(pl.ds(0, n) for n in ref.shape[:lead]) + (pl.ds(start, size),)]


def _other_chips():
    x, y, _ = _mesh_pos()
    return [(1 - x, y), (x, 1 - y), (1 - x, 1 - y)]


def _plan_gather_ici(bufs):
    n = len(bufs)

    def copies(outs, send, recv):
        x, y, c = _mesh_pos()
        res = []
        for b in range(n):
            half = bufs[b].shape[1] // 2
            mine = _rows(outs[b].at[2 * x + y], 0, c * half, half)
            for j, (cx, cy) in enumerate(_other_chips()):
                res.append((_remote(mine, mine, send(3 * b + j), recv(3 * b + j), (cx, cy, c)),
                            _remote(mine, _rows(outs[b].at[2 * cx + cy], 0, c * half, half),
                                    send(3 * b + j), recv(3 * b + j), (x, y, c))))
        return res

    def start(ins, outs, send, recv, loc):
        for out_cp, _ in copies(outs, send, recv):
            out_cp.start()

    def wait(ins, outs, send, recv, loc):
        for out_cp, in_cp in copies(outs, send, recv):
            in_cp.wait_recv()
            out_cp.wait_send()

    outs = [jax.ShapeDtypeStruct(b.shape, b.dtype) for b in bufs]
    return _Plan(bufs, outs, 3 * n, 0, start, wait, aliases={b: b for b in range(n)})


def _plan_gather_forward(bufs):
    n = len(bufs)

    def copies(outs, send, recv):
        x, y, c = _mesh_pos()
        res = []
        for b in range(n):
            half = bufs[b].shape[1] // 2
            for j, (cx, cy) in enumerate(_other_chips()):
                slot = outs[b].at[2 * cx + cy]
                res.append((_remote(_rows(slot, 0, c * half, half), _rows(slot, 0, c * half, half),
                                    send(3 * b + j), recv(3 * b + j), (x, y, 1 - c)),
                            _remote(_rows(slot, 0, c * half, half), _rows(slot, 0, (1 - c) * half, half),
                                    send(3 * b + j), recv(3 * b + j), (x, y, c))))
        return res

    def start(ins, outs, send, recv, loc):
        for out_cp, _ in copies(outs, send, recv):
            out_cp.start()

    def wait(ins, outs, send, recv, loc):
        for out_cp, in_cp in copies(outs, send, recv):
            in_cp.wait_recv()
            out_cp.wait_send()

    outs = [jax.ShapeDtypeStruct(b.shape, b.dtype) for b in bufs]
    return _Plan(bufs, outs, 3 * n, 0, start, wait, aliases={b: b for b in range(n)})


def _plan_pair_swap(g):
    half = g.shape[1] // 2

    def copy(ins, outs, send, recv, loc):
        x, y, c = _mesh_pos()
        return _remote(_rows(ins[0], 1, (1 - c) * half, half), outs[0], send(0), recv(0), (x, y, 1 - c))

    return _Plan([g], [jax.ShapeDtypeStruct((4, half, g.shape[2]), g.dtype)], 1, 0,
                 lambda *a: copy(*a).start(), lambda *a: copy(*a).wait())


def _plan_pair_gather(buf):
    def copies(ins, outs, send, recv, loc):
        x, y, c = _mesh_pos()
        return (_remote(outs[0].at[c], outs[0].at[c], send(0), recv(0), (x, y, 1 - c)),
                _remote(outs[0].at[c], outs[0].at[1 - c], send(0), recv(0), (x, y, c)))

    def wait(*a):
        out_cp, in_cp = copies(*a)
        in_cp.wait_recv()
        out_cp.wait_send()

    return _Plan([buf], [jax.ShapeDtypeStruct(buf.shape, buf.dtype)], 1, 0, lambda *a: copies(*a)[0].start(), wait,
                 aliases={0: 0})


def _plan_chip_scatter(p):
    def copies(ins, outs, send, recv, loc):
        _, _, c = _mesh_pos()
        return [_remote(ins[0].at[2 * cx + cy], outs[0].at[j], send(j), recv(j), (cx, cy, c))
                for j, (cx, cy) in enumerate(_other_chips())]

    def start(*a):
        for cp in copies(*a):
            cp.start()

    def wait(*a):
        for cp in copies(*a):
            cp.wait()

    return _Plan([p], [jax.ShapeDtypeStruct((3,) + p.shape[1:], p.dtype)], 3, 0, start, wait)


def _plan_exchange_all(vec):
    def copies(ins, outs, send, recv, loc):
        x, y, c = _mesh_pos()
        return [_remote(ins[0], outs[0].at[r - 1], send(r - 1), recv(r - 1), (x ^ (r >> 2), y ^ ((r >> 1) & 1), c ^ (r & 1)))
                for r in range(1, 8)]

    def start(*a):
        for cp in copies(*a):
            cp.start()

    def wait(*a):
        for cp in copies(*a):
            cp.wait()

    return _Plan([vec], [jax.ShapeDtypeStruct((7,) + vec.shape, vec.dtype)], 7, 0, start, wait)


def _sum_devices(vec, others, ids):
    R, L = vec.shape

    def body(ids_ref, v_ref, o_ref, out_ref):
        me = 2 * ids_ref[0] + ids_ref[1]
        total = None
        for d in range(8):
            rel = d ^ me
            term = jnp.where(rel == 0, v_ref[...], o_ref[jnp.maximum(rel - 1, 0)])
            total = term if total is None else total + term
        out_ref[...] = total

    return pl.pallas_call(
        body, name="small_grad_sum", out_shape=jax.ShapeDtypeStruct((R, L), F32),
        grid_spec=pltpu.PrefetchScalarGridSpec(
            num_scalar_prefetch=1, grid=(1,), in_specs=[pl.BlockSpec((R, L), lambda i, ids: (0, 0)),
                                                        pl.BlockSpec((7, R, L), lambda i, ids: (0, 0, 0))],
            out_specs=pl.BlockSpec((R, L), lambda i, ids: (0, 0))),
        compiler_params=_params(16, 1),
    )(ids, vec, others)


ROWS_L1, ROWS_L0, ROWS_ODD = 3328, 2048, 768
ODD_PARTS = (("w_out_e", (256, 1024)), ("w_in_e", (1024, 392)), ("w_qb", (256, 192)), ("w_kvb", (256, 256)))


def _odd_rows(parts, dtype, gnorm=None):
    rows = [parts[n].reshape(-1, 1024).astype(dtype) for n, _ in ODD_PARTS]
    used = sum(r.shape[0] for r in rows)
    if gnorm is not None:
        bits = lax.bitcast_convert_type(gnorm.reshape(-1), BF16).reshape(1, 512)
        rows.append(jnp.pad(bits, ((0, 0), (0, 512))))
        used += 1
    rows.append(jnp.zeros((ROWS_ODD - used, 1024), dtype))
    return jnp.concatenate(rows, axis=0)


def _odd_unrows(buf, with_gnorm=False):
    out, off = {}, 0
    for n, shape in ODD_PARTS:
        nr = math.prod(shape) // 1024
        out[n] = buf[off:off + nr].reshape(shape)
        off += nr
    if with_gnorm:
        out["hg_gnorm"] = lax.bitcast_convert_type(buf[off, :512].reshape(256, 2), F32).reshape(1, 256)
    return out


def _pack_small(vals):
    flat = jnp.concatenate([vals[n].reshape(-1).astype(F32) for n, _ in SMALL])
    return jnp.pad(flat, (0, SMALL_ROWS * 1024 - flat.shape[0])).reshape(SMALL_ROWS, 1024)


def _unpack_small(packed):
    flat = packed.reshape(-1)
    out, off = {}, 0
    for n, shape in SMALL:
        size = math.prod(shape)
        out[n] = flat[off:off + size].reshape(shape)
        off += size
    return out


def _rope_tables(positions):
    half = ROPE // 2
    inv_freq = ROPE_BASE ** (-jnp.arange(half, dtype=F32) / half)
    ang = positions.astype(F32).reshape(-1, 1) * inv_freq
    cos, sin = jnp.cos(ang), jnp.sin(ang)
    T = ang.shape[0]
    one, z16, z32 = jnp.ones((T, NOPE), F32), jnp.zeros((T, half), F32), jnp.zeros((T, 32), F32)
    z64 = jnp.zeros((T, NOPE), F32)
    c = jnp.concatenate([one, cos, cos, z32], axis=1)
    s1 = jnp.concatenate([z64, -sin, z16, z32], axis=1)
    s2 = jnp.concatenate([z64, z16, sin, z32], axis=1)
    return c, s1, s2


def _local_step(x, positions, tgt, odd, bufs, P, exchange):
    T = x.shape[0]
    row = lambda a: a.reshape(1, -1)
    rc, rs1, rs2 = _rope_tables(positions)
    blk = lambda f: pl.BlockSpec((None, D, D), f)

    w_in_e = odd["w_in_e"]
    w_in = jnp.concatenate([w_in_e[:, :512], w_in_e[:, 544:1568], w_in_e[:, 512:544], jnp.zeros((D, 96), BF16)], axis=1)
    wq = jnp.pad(odd["w_qb"].reshape(256, HEADS, NOPE + ROPE), ((0, 0), (0, 0), (0, 32))).reshape(256, HEADS * 128)
    kvb = odd["w_kvb"].reshape(256, HEADS, NOPE + VDIM)
    wk = jnp.pad(kvb[:, :, :NOPE], ((0, 0), (0, 0), (0, 64))).reshape(256, HEADS * 128)
    wv = kvb[:, :, NOPE:].reshape(256, HEADS * VDIM)
    w_out_e = odd["w_out_e"]
    sgu_w = P["sgu_w"][0]
    sgu_bt = P["sgu_b"][0].T
    gq, gkv = P["mla_gq"], P["mla_gkv"]
    gnorm = P["hg_gnorm"]

    x_b = x.astype(BF16)
    z0 = _matmul(x_b, w_in, name="in_proj_e", M=T, N=1664, K=D, tn=1664)[0]
    q, k, v = _mla_prep(z0, gq, gkv, wq, wk, wv, rc, rs1, rs2)
    if exchange:
        ids = _mesh_ids()
        placed = [_place_shard(b, ids, name=f"place_shard_{l}") for l, b in enumerate(bufs)]
        a_out, lse, wg0, wg1 = _flash_fwd(q, k, v, plan=_plan_gather_ici(placed))
    else:
        a_out, lse = _flash_fwd(q, k, v)
        wg0, wg1 = bufs
    mix0 = _sgu_fwd(z0, a_out, P["sgu_ln_g"], P["sgu_ln_b"], sgu_w, sgu_bt)
    res = _proj_ln(mix0, w_out_e, x, row(P["ln1_g"][0]), row(P["ln1_b"][0]), name="out_proj_ln_e",
                   plan=_plan_gather_forward([wg0, wg1]) if exchange else None)
    r1, h1, h1b = res[:3]
    if exchange:
        wg0, wg1 = res[3:]
    ra0, r2, h2, h2b = _ffn_ln(h1b, wg0, h1, row(P["ln2_g"][0]), row(P["ln2_b"][0]), name="ffn_ln_0")
    z4 = _matmul(h2b, wg1, name="in_proj_o", M=T, N=4 * D, K=D, b_spec=blk(lambda i, j, k: (j, 2, 0)),
                 out_shape=jax.ShapeDtypeStruct((4, T, D), F32),
                 o_spec=pl.BlockSpec((None, min(MM_ROWS, T), D), lambda i, j, k: (j, i, 0)))[0]
    y1, o_raw, states = _hgrn_fwd(z4, P["hg_lb"], gnorm)
    r3, h3, h3b = _proj_ln(y1, wg1, h2, row(P["ln1_g"][1]), row(P["ln1_b"][1]), name="out_proj_ln_o", w_rowblk=12)
    ra1, r4, h4, _ = _ffn_ln(h3b, wg1, h3, row(P["ln2_g"][1]), row(P["ln2_b"][1]), name="ffn_ln_1")
    dy, loss_parts = _loss_dy(h4, tgt)

    gs = {}
    ln1_g, ln1_b, ln2_g, ln2_b = [None, None], [None, None], [None, None], [None, None]

    def ffn_bwd(l, dh, r_out, ra, h_mid_b, g2, wg, rows, plan=None):
        dr, dr_b, dg, db = _ln_bwd(dh, r_out, row(g2), name=f"ln2_bwd_{l}")
        ln2_g[l], ln2_b[l] = dg.sum(0), db.sum(0)
        da, *extra = _matmul(dr_b, wg, tb=True, mul=ra, out_dtype=BF16, name=f"ffn_da_{l}", M=T, N=4 * D, K=D,
                             b_spec=blk(lambda i, j, k: (j, 1, 0)), plan=plan)
        gbuf = _matmul(ra, dr_b, ta=True, a_sq=True, name=f"ffn_dw2_{l}", M=4 * D, N=D, K=T, tm=1024, tk=DW_TOKENS,
                       out_shape=jax.ShapeDtypeStruct((4, rows, D), BF16), o_spec=blk(lambda i, j, k: (i, 1, 0)))[0]
        gbuf = _matmul(h_mid_b, da, ta=True, name=f"ffn_dw1_{l}", M=D, N=4 * D, K=T, tm=1024, tk=DW_TOKENS, into=gbuf,
                       out_shape=jax.ShapeDtypeStruct((4, rows, D), BF16), o_spec=blk(lambda i, j, k: (j, 0, 0)))[0]
        dh_mid = _matmul(da, wg, tb=True, add=dr, add_scale=ALPHA, name=f"ffn_dh_{l}", M=T, N=D, K=4 * D,
                         b_spec=blk(lambda i, j, k: (k, 0, 0)))[0]
        return dh_mid, gbuf, extra

    dh3, g1, _ = ffn_bwd(1, dy, r4, ra1, h3b, P["ln2_g"][1], wg1, ROWS_L1)
    dr3, dr3_b, dg, db = _ln_bwd(dh3, r3, row(P["ln1_g"][1]), name="ln1_bwd_1")
    ln1_g[1], ln1_b[1] = dg.sum(0), db.sum(0)
    g1_sds = jax.ShapeDtypeStruct((4, ROWS_L1, D), BF16)
    g1 = _matmul(y1, dr3_b, ta=True, name="dw_out_o", M=D, N=D, K=T, tm=256, tk=DW_TOKENS, into=g1, out_shape=g1_sds,
                 o_spec=pl.BlockSpec((None, 256, D), lambda i, j, k: (i, 12, 0)))[0]
    dmix1 = _matmul(dr3_b, wg1, tb=True, name="dmix_o", M=T, N=D, K=D, b_spec=_rows4_spec(12, 3), b_merge=(D, D))[0]
    dz4, dlb, dgn = _hgrn_bwd(z4, o_raw, dmix1, states, P["hg_lb"], gnorm)
    g1 = _matmul(h2b, dz4, ta=True, name="dw_in_o", M=D, N=4 * D, K=T, tm=1024, tk=DW_TOKENS, into=g1, out_shape=g1_sds,
                 b_spec=pl.BlockSpec((None, min(DW_TOKENS, T), D), lambda i, j, k: (j, k, 0)),
                 o_spec=blk(lambda i, j, k: (j, 2, 0)))[0]
    dh2 = _matmul(dz4, wg1, tb=True, add=dr3, add_scale=ALPHA, name="dh_in_o", M=T, N=D, K=4 * D,
                  a_spec=pl.BlockSpec((None, min(MM_ROWS, T), D), lambda i, j, k: (k, i, 0)),
                  b_spec=blk(lambda i, j, k: (k, 2, 0)))[0]
    d_lb1 = dlb.sum(0)
    gs["hg_lb"] = jnp.stack([-d_lb1, d_lb1])
    gs["hg_gnorm"] = dgn.sum(0)[None]

    dh1, g0, swapped1 = ffn_bwd(0, dh2, r2, ra0, h1b, P["ln2_g"][0], wg0, ROWS_L0,
                                plan=_plan_pair_swap(g1) if exchange else None)
    dr1, dr1_b, dg, db = _ln_bwd(dh1, r1, row(P["ln1_g"][0]), name="ln1_bwd_0")
    ln1_g[0], ln1_b[0] = dg.sum(0), db.sum(0)
    godd = {"w_out_e": _matmul(mix0, dr1_b, ta=True, name="dw_out_e", M=D, N=D, K=T, tm=1024, tk=DW_TOKENS)[0]}
    dmix0, *swapped0 = _matmul(dr1_b, w_out_e, tb=True, name="dmix_e", M=T, N=D, K=D,
                               plan=_plan_pair_swap(g0) if exchange else None)
    delta, do_b = _attn_delta(dmix0, a_out)
    if exchange:
        pair1 = _add_pairs(g1, swapped1[0], ids, name="grad_pair_add_1")
        pair0 = _add_pairs(g0, swapped0[0], ids, name="grad_pair_add_0")
        dq4, dk, dv, parts0, parts1 = _flash_bwd(
            q, k, v, do_b, lse, delta, plan=_join_plans([_plan_chip_scatter(pair0), _plan_chip_scatter(pair1)]))
        half0 = _sum_chips(pair0, parts0, ids, name="grad_chip_sum_0")
        half1 = _sum_chips(pair1, parts1, ids, name="grad_chip_sum_1")
        dc, dkr, dwq, dwk, dwv, dgq, dgkv, g0, g1 = _mla_bwd(
            z0, dq4, dk, dv, gq, gkv, wq, wk, wv, rc, rs1, rs2,
            plan=_join_plans([_plan_pair_gather(half0), _plan_pair_gather(half1)]))
        g0, g1 = g0.reshape(ROWS_L0, D), g1.reshape(ROWS_L1, D)
    else:
        dq4, dk, dv = _flash_bwd(q, k, v, do_b, lse, delta)
        dc, dkr, dwq, dwk, dwv, dgq, dgkv = _mla_bwd(z0, dq4, dk, dv, gq, gkv, wq, wk, wv, rc, rs1, rs2)
    dz0, dsw, dsb, dslg, dslb = _sgu_bwd(z0, dmix0, dc, dkr, P["sgu_ln_g"], P["sgu_ln_b"], sgu_w, sgu_bt)
    gs["mla_gq"], gs["mla_gkv"] = dgq.sum(0)[None], dgkv.sum(0)[None]
    gs["sgu_ln_g"], gs["sgu_ln_b"] = dslg.sum(0)[None], dslb.sum(0)[None]
    gs["sgu_w"], gs["sgu_b"] = dsw[None], dsb[:, :SGU_G].T[None]
    gs["ln1_g"], gs["ln1_b"] = jnp.stack(ln1_g), jnp.stack(ln1_b)
    gs["ln2_g"], gs["ln2_b"] = jnp.stack(ln2_g), jnp.stack(ln2_b)
    small_vec = _pack_small(gs)
    dw_in, *small_others = _matmul(x_b, dz0, ta=True, name="dw_in_e", M=D, N=1664, K=T, tm=1024, tn=1664,
                                   tk=DW_TOKENS // 2, plan=_plan_exchange_all(small_vec) if exchange else None)
    godd["w_in_e"] = jnp.concatenate([dw_in[:, :512], dw_in[:, 1536:1568], dw_in[:, 512:1536]], axis=1)
    grad_x = _matmul(dz0, w_in, tb=True, add=dr1, add_scale=ALPHA, name="dx", M=T, N=D, K=1664, tk=1664)[0]

    godd["w_qb"] = dwq.reshape(256, HEADS, 128)[:, :, :NOPE + ROPE].reshape(256, HEADS * (NOPE + ROPE))
    godd["w_kvb"] = jnp.concatenate([dwk.reshape(256, HEADS, 128)[:, :, :NOPE], dwv.reshape(256, HEADS, VDIM)],
                                    axis=2).reshape(256, HEADS * (NOPE + VDIM))
    small = (small_vec, small_others[0]) if exchange else gs
    return loss_parts, grad_x, g0, g1, godd, small


WEIGHTS = ['w_in_e', 'mla_gq', 'mla_gkv', 'w_qb', 'w_kvb', 'sgu_ln_g', 'sgu_ln_b', 'sgu_w', 'sgu_b', 'w_out_e',
           'w_in_o', 'hg_lb', 'hg_gnorm', 'w_out_o', 'ln1_g', 'ln1_b', 'w_ff1', 'w_ff2', 'ln2_g', 'ln2_b']


def kernel(x, positions, w_in_e, mla_gq, mla_gkv, w_qb, w_kvb, sgu_ln_g, sgu_ln_b, sgu_w, sgu_b, w_out_e, w_in_o, hg_lb, hg_gnorm, w_out_o, ln1_g, ln1_b, w_ff1, w_ff2, ln2_g, ln2_b, loss_target, m_w_in_e, m_mla_gq, m_mla_gkv, m_w_qb, m_w_kvb, m_sgu_ln_g, m_sgu_ln_b, m_sgu_w, m_sgu_b, m_w_out_e, m_w_in_o, m_hg_lb, m_hg_gnorm, m_w_out_o, m_ln1_g, m_ln1_b, m_w_ff1, m_w_ff2, m_ln2_g, m_ln2_b, v_w_in_e, v_mla_gq, v_mla_gkv, v_w_qb, v_w_kvb, v_sgu_ln_g, v_sgu_ln_b, v_sgu_w, v_sgu_b, v_w_out_e, v_w_in_o, v_hg_lb, v_hg_gnorm, v_w_out_o, v_ln1_g, v_ln1_b, v_w_ff1, v_w_ff2, v_ln2_g, v_ln2_b):
    args = dict(locals())
    w = {n: args[n] for n in WEIGHTS}
    m = {n: args["m_" + n] for n in WEIGHTS}
    v = {n: args["v_" + n] for n in WEIGHTS}
    cx, cy, cc = _mesh_pos()
    chip = 2 * cx + cy

    odd_shard = _odd_rows({"w_out_e": w_out_e[0], "w_in_e": w_in_e[0], "w_qb": w_qb[0], "w_kvb": w_kvb[0]}, BF16,
                          gnorm=hg_gnorm)
    ids = _mesh_ids()
    gathered = _run_plan(_plan_gather_ici([_place_shard(odd_shard, ids, name="place_shard_odd")]), name="odd_gather")[0]
    gathered = _run_plan(_plan_gather_forward([gathered]), name="odd_gather_forward")[0]
    per_chip = [_odd_unrows(gathered[j], with_gnorm=True) for j in range(4)]
    odd = {"w_out_e": jnp.concatenate([p["w_out_e"] for p in per_chip], axis=0)}
    for n in ("w_in_e", "w_qb", "w_kvb"):
        odd[n] = jnp.concatenate([p[n] for p in per_chip], axis=1)
    small = {n: w[n] for n, _ in SMALL if n != "hg_gnorm"}
    small["hg_gnorm"] = jnp.concatenate([p["hg_gnorm"] for p in per_chip], axis=1)
    rows_l0 = jnp.concatenate([w_ff1[0], w_ff2[0]], axis=0).astype(BF16)
    rows_l1 = jnp.concatenate([w_ff1[1], w_ff2[1], w_in_o[0], w_out_o[0]], axis=0).astype(BF16)

    loss_parts, grad_x, g_l0, g_l1, godd, (small_vec, small_others) = _local_step(
        x[0], positions[0], loss_target[0], odd, (rows_l0, rows_l1), small, True)

    loss = lax.psum((0.5 / D) * jnp.sum(loss_parts), ("x", "y", "c"))

    by_chip = [_odd_rows({"w_out_e": jnp.split(godd["w_out_e"], 4, axis=0)[j],
                          **{n: jnp.split(godd[n], 4, axis=1)[j] for n in ("w_in_e", "w_qb", "w_kvb")}}, BF16)
               for j in range(4)]
    godd_buf = jnp.stack(by_chip)
    theirs = _run_plan(_plan_pair_swap(godd_buf), name="odd_pair_swap")[0]
    pair = _add_pairs(godd_buf, theirs, ids, name="odd_pair_add")
    parts = _run_plan(_plan_chip_scatter(pair), name="odd_chip_scatter")[0]
    g_odd = _run_plan(_plan_pair_gather(_sum_chips(pair, parts, ids, name="odd_chip_sum")), name="odd_pair_gather")[0]
    g_odd = _odd_unrows(g_odd.reshape(ROWS_ODD, 1024))

    g_small = _unpack_small(_sum_devices(small_vec, small_others, ids))
    grads = {n: g_small[n] for n, _ in SMALL if n != "hg_gnorm"}
    grads["hg_gnorm"] = lax.dynamic_slice_in_dim(g_small["hg_gnorm"], chip * 256, 256, axis=1)

    delta, new_m, new_v = {}, {}, {}
    for n, bufs_, row0 in (("w_ff1", [g_l0, g_l1], 0), ("w_ff2", [g_l0, g_l1], 1024), ("w_in_o", [g_l1], 2048),
                           ("w_out_o", [g_l1], 3072)):
        grads[n], delta[n], new_m[n], new_v[n] = _adamw_rows(w[n], m[n], v[n], bufs_, row0, name=f"adamw_{n}")
    for n, _ in ODD_PARTS:
        grads[n] = g_odd[n][None]
        d_, m_, v_ = _adamw(w[n][0], g_odd[n], m[n][0], v[n][0], name=f"adamw_{n}")
        delta[n], new_m[n], new_v[n] = d_[None], m_[None], v_[None]
    rest = [n for n in WEIGHTS if n not in delta]

    def pack_rest(d):
        flat = jnp.concatenate([d[n].reshape(-1) for n in rest])
        return jnp.pad(flat, (0, SMALL_ROWS * 1024 - flat.shape[0])).reshape(SMALL_ROWS, 1024)

    outs = _adamw(pack_rest(w), pack_rest(grads), pack_rest(m), pack_rest(v), name="adamw_small")
    for dst, packed in zip((delta, new_m, new_v), outs):
        flat, off = packed.reshape(-1), 0
        for n in rest:
            size = math.prod(w[n].shape)
            dst[n] = flat[off:off + size].reshape(w[n].shape)
            off += size

    return (loss, grad_x[None], *[grads[n] for n in WEIGHTS], *[delta[n] for n in WEIGHTS],
            *[new_m[n] for n in WEIGHTS], *[new_v[n] for n in WEIGHTS])
```

```python
import functools
import math

import jax
import jax.numpy as jnp
from jax import lax
from jax.experimental import pallas as pl
from jax.experimental.pallas import tpu as pltpu

F32 = jnp.float32
BF16 = jnp.bfloat16
MESH_IDS = pl.DeviceIdType.MESH

D = 1024
DEPTH = 2
HEADS = 8
NOPE, ROPE, VDIM = 64, 32, 64
QK_SCALE = (NOPE + ROPE) ** -0.5
ROPE_BASE = 10000.0
SGU_G, SGU_C = 4, 128
HG_CHUNK = 64
HG_HEADS_PER_STEP = 4
ALPHA = (2 * DEPTH) ** 0.25
EPS = 1e-5
LR, B1, B2, ADAM_EPS, WD, STEP = 0.001, 0.9, 0.999, 1e-08, 0.01, 10
GELU_C = math.sqrt(2.0 / math.pi)
GELU_A = 0.044715
HI = lax.Precision.HIGHEST
MB = 1024 * 1024
ROW_BLOCK = 512

NT_DIMS = (((1,), (1,)), ((), ()))
TN_DIMS = (((0,), (0,)), ((), ()))

SHARDED = (
    ("w_in_e", (1, 1024, 392), 2), ("w_qb", (1, 256, 192), 2), ("w_kvb", (1, 256, 256), 2),
    ("w_out_e", (1, 256, 1024), 1), ("w_in_o", (1, 1024, 1024), 2), ("w_out_o", (1, 256, 1024), 1),
    ("w_ff1", (2, 1024, 1024), 2), ("w_ff2", (2, 1024, 1024), 1), ("hg_gnorm", (1, 256), 1),
)
PACK_ROWS = 6144
HALF_ROWS = PACK_ROWS // 2
SMALL = (("mla_gq", (1, 256)), ("mla_gkv", (1, 256)), ("sgu_ln_g", (1, 512)), ("sgu_ln_b", (1, 512)),
         ("sgu_w", (1, 4, 128, 128)), ("sgu_b", (1, 4, 128)), ("hg_lb", (2, 1024)), ("hg_gnorm", (1, 1024)),
         ("ln1_g", (2, 1024)), ("ln1_b", (2, 1024)), ("ln2_g", (2, 1024)), ("ln2_b", (2, 1024)))
SMALL_ROWS = 80


def _params(vmem_mb, n_axes=0):
    kw = dict(vmem_limit_bytes=vmem_mb * MB)
    if n_axes:
        kw["dimension_semantics"] = ("arbitrary",) * n_axes
    return pltpu.CompilerParams(**kw)


_ANY = pl.BlockSpec(memory_space=pltpu.HBM)


def _mesh_pos():
    return lax.axis_index("x"), lax.axis_index("y"), lax.axis_index("c")


class _Plan:
    def __init__(self, ins, outs, n_remote, n_local, start, wait, aliases=None):
        self.ins, self.outs, self.n_remote, self.n_local = list(ins), list(outs), n_remote, n_local
        self.start, self.wait, self.aliases = start, wait, dict(aliases or {})


def _join_plans(plans):
    ins, outs, aliases, parts = [], [], {}, []
    nr = nl = 0
    for p in plans:
        parts.append((p, len(ins), len(outs), nr, nl))
        aliases.update({len(ins) + i: len(outs) + o for i, o in p.aliases.items()})
        ins += p.ins
        outs += p.outs
        nr += p.n_remote
        nl += p.n_local

    def run(which):
        def go(in_refs, out_refs, send, recv, loc):
            for p, i0, o0, r0, l0 in parts:
                getattr(p, which)(in_refs[i0:i0 + len(p.ins)], out_refs[o0:o0 + len(p.outs)],
                                  lambda i, r0=r0: send(r0 + i), lambda i, r0=r0: recv(r0 + i),
                                  lambda i, l0=l0: loc(l0 + i))
        return go

    return _Plan(ins, outs, nr, nl, run("start"), run("wait"), aliases)


def _plan_io(plan, n_in, n_out):
    if plan is None:
        return [], [], [], [], {}
    sems = [pltpu.SemaphoreType.DMA((max(plan.n_remote, 1),)), pltpu.SemaphoreType.DMA((max(plan.n_remote, 1),)),
            pltpu.SemaphoreType.DMA((max(plan.n_local, 1),))]
    aliases = {n_in + i: n_out + o for i, o in plan.aliases.items()}
    return plan.ins, [_ANY] * len(plan.outs), plan.outs, sems, aliases


def _split_refs(refs, n_in, n_out, n_scr, plan):
    p_in, p_out = (len(plan.ins), len(plan.outs)) if plan is not None else (0, 0)
    refs = list(refs)
    ins, refs = refs[:n_in], refs[n_in:]
    pins, refs = refs[:p_in], refs[p_in:]
    outs, refs = refs[:n_out], refs[n_out:]
    pouts, refs = refs[:p_out], refs[p_out:]
    scr, psem = refs[:n_scr], refs[n_scr:]
    psem = tuple((lambda i, s=s: s.at[i]) for s in psem)
    return ins, outs, scr, (pins, pouts, psem)


def _grid_edge(grid, last):
    cond = None
    for ax, n in enumerate(grid):
        c = pl.program_id(ax) == (n - 1 if last else 0)
        cond = c if cond is None else cond & c
    return cond


def _plan_start(plan, pctx, grid):
    if plan is not None:
        pins, pouts, psem = pctx
        pl.when(_grid_edge(grid, False))(lambda: plan.start(pins, pouts, *psem))


def _plan_wait(plan, pctx, grid):
    if plan is not None:
        pins, pouts, psem = pctx
        pl.when(_grid_edge(grid, True))(lambda: plan.wait(pins, pouts, *psem))


def _run_plan(plan, *, name):
    def body(*refs):
        _, _, _, (pins, pouts, psem) = _split_refs(refs, 0, 0, 0, plan)
        plan.start(pins, pouts, *psem)
        plan.wait(pins, pouts, *psem)

    p_in, p_ospec, p_oshape, p_scr, p_alias = _plan_io(plan, 0, 0)
    return pl.pallas_call(body, name=name, in_specs=[_ANY] * len(p_in), out_specs=p_ospec, out_shape=p_oshape,
                          scratch_shapes=p_scr, input_output_aliases=p_alias)(*p_in)


def _fold8(x):
    return x.reshape(x.shape[0] // 8, 8, x.shape[1]).sum(axis=0)


def _ln_stats(r):
    mu = jnp.mean(r, -1, keepdims=True)
    xc = r - mu
    rstd = lax.rsqrt(jnp.mean(xc * xc, -1, keepdims=True) + EPS)
    return xc * rstd, rstd


def _sigmoid(x):
    return 1.0 / (1.0 + jnp.exp(-x))


def _gelu(x):
    return 0.5 * x * (1.0 + jnp.tanh(GELU_C * (x + GELU_A * x * x * x)))


def _gelu_grad(x):
    t = jnp.tanh(GELU_C * (x + GELU_A * x * x * x))
    return 0.5 * (1.0 + t) + 0.5 * x * (1.0 - t * t) * GELU_C * (1.0 + 3.0 * GELU_A * x * x)


MM_ROWS = 1024
DW_TOKENS = 2048


def _matmul(a, b, *, name, M, N, K, ta=False, tb=False, out_dtype=F32, tm=MM_ROWS, tn=1024, tk=1024,
            a_spec=None, b_spec=None, b_merge=None, out_shape=None, o_spec=None, into=None,
            a_sq=False, mul=None, add=None, add_scale=1.0, plan=None):
    tm, tn, tk = min(tm, M), min(tn, N), min(tk, K)
    assert M % tm == 0 and N % tn == 0 and K % tk == 0
    grid = (M // tm, N // tn, K // tk)
    nk = grid[2]
    if a_spec is None:
        a_spec = pl.BlockSpec((tk, tm), lambda i, j, k: (k, i)) if ta else pl.BlockSpec((tm, tk), lambda i, j, k: (i, k))
    if b_spec is None:
        b_spec = pl.BlockSpec((tn, tk), lambda i, j, k: (j, k)) if tb else pl.BlockSpec((tk, tn), lambda i, j, k: (k, j))
    if o_spec is None:
        o_spec = pl.BlockSpec((tm, tn), lambda i, j, k: (i, j))
        out_shape = jax.ShapeDtypeStruct((M, N), out_dtype)
    e_spec = pl.BlockSpec((tm, tn), lambda i, j, k: (i, j))
    dims = (((0 if ta else 1,), (1 if tb else 0,)), ((), ()))
    extra = [e for e in (mul, add, into) if e is not None]
    n_in = 2 + len(extra)

    def body(*refs):
        ins, outs, scr, pctx = _split_refs(refs, n_in, 1, 1 if nk > 1 else 0, plan)
        a_ref, b_ref = ins[0], ins[1]
        rest = list(ins[2:])
        mul_ref = rest.pop(0) if mul is not None else None
        add_ref = rest.pop(0) if add is not None else None
        o_ref = outs[0]
        _plan_start(plan, pctx, grid)
        av = a_ref[...].astype(BF16)
        if a_sq:
            av = av * av
        bv = b_ref[...]
        if b_merge is not None:
            bv = bv.reshape(b_merge)
        p = lax.dot_general(av, bv, dims, preferred_element_type=F32)

        def finish(r):
            if mul_ref is not None:
                r = r * (2.0 * mul_ref[...].astype(F32))
            if add_ref is not None:
                r = r + add_scale * add_ref[...]
            o_ref[...] = r.astype(o_ref.dtype)

        if nk == 1:
            finish(p)
        else:
            acc_ref = scr[0]
            k = pl.program_id(2)

            @pl.when(k == 0)
            def _():
                acc_ref[...] = p

            @pl.when(k > 0)
            def _():
                acc_ref[...] += p

            @pl.when(k == nk - 1)
            def _():
                finish(acc_ref[...])

        _plan_wait(plan, pctx, grid)

    p_in, p_ospec, p_oshape, p_scr, p_alias = _plan_io(plan, n_in, 1)
    aliases = dict(p_alias)
    if into is not None:
        aliases[n_in - 1] = 0
    return pl.pallas_call(
        body, name=name, grid=grid,
        in_specs=[a_spec, b_spec] + [e_spec] * (len(extra) - (into is not None)) + [_ANY] * (into is not None)
        + [_ANY] * len(p_in),
        out_specs=[o_spec] + p_ospec, out_shape=[out_shape] + p_oshape,
        scratch_shapes=([pltpu.VMEM((tm, tn), F32)] if nk > 1 else []) + p_scr,
        input_output_aliases=aliases, compiler_params=_params(48, 3),
    )(a, b, *extra, *p_in)


def _rows4_spec(rowblk, n_axes):
    return pl.BlockSpec((4, 256, D), lambda *_: (0, rowblk, 0))


def _proj_ln(a_b, w, h_prev, g, b, *, name, w_rowblk=None, plan=None):
    T = a_b.shape[0]
    tm = min(ROW_BLOCK, T)
    grid = (T // tm,)
    row = pl.BlockSpec((tm, D), lambda i: (i, 0))
    vec = pl.BlockSpec((1, D), lambda i: (0, 0))
    w_spec = pl.BlockSpec((D, D), lambda i: (0, 0)) if w_rowblk is None else _rows4_spec(w_rowblk, 1)

    def body(*refs):
        (a_ref, w_ref, h_ref, g_ref, b_ref), (r_ref, ho_ref, hb_ref), _, pctx = _split_refs(refs, 5, 3, 0, plan)
        _plan_start(plan, pctx, grid)
        mix = jnp.dot(a_ref[...], w_ref[...].reshape(D, D), preferred_element_type=F32)
        r = ALPHA * h_ref[...] + mix
        xhat, _ = _ln_stats(r)
        y = xhat * g_ref[...] + b_ref[...]
        r_ref[...] = r
        ho_ref[...] = y
        hb_ref[...] = y.astype(BF16)
        _plan_wait(plan, pctx, grid)

    p_in, p_ospec, p_oshape, p_scr, p_alias = _plan_io(plan, 5, 3)
    return pl.pallas_call(
        body, name=name, grid=grid,
        in_specs=[row, w_spec, row, vec, vec] + [_ANY] * len(p_in),
        out_specs=[row, row, row] + p_ospec,
        out_shape=[jax.ShapeDtypeStruct((T, D), F32), jax.ShapeDtypeStruct((T, D), F32),
                   jax.ShapeDtypeStruct((T, D), BF16)] + p_oshape,
        scratch_shapes=p_scr, input_output_aliases=p_alias, compiler_params=_params(40, 1),
    )(a_b, w, h_prev, g, b, *p_in)


def _ffn_ln(h_b, wbuf, h, g, b, *, name):
    T = h_b.shape[0]
    tm, tf = min(ROW_BLOCK, T), 1024
    nf = 4
    F = nf * tf
    row = pl.BlockSpec((tm, D), lambda i, j: (i, 0))
    vec = pl.BlockSpec((1, D), lambda i, j: (0, 0))

    def body(hb_ref, w1_ref, w2_ref, h_ref, g_ref, b_ref, ra_ref, r_ref, ho_ref, hbo_ref, acc_ref):
        j = pl.program_id(1)
        a = jnp.dot(hb_ref[...], w1_ref[...], preferred_element_type=F32)
        ra = jnp.maximum(a, 0.0)
        ra_ref[...] = ra.astype(BF16)
        p = jnp.dot((ra * ra).astype(BF16), w2_ref[...], preferred_element_type=F32)

        @pl.when(j == 0)
        def _():
            acc_ref[...] = p

        @pl.when(j > 0)
        def _():
            acc_ref[...] += p

        @pl.when(j == nf - 1)
        def _():
            r = ALPHA * h_ref[...] + acc_ref[...]
            xhat, _ = _ln_stats(r)
            y = xhat * g_ref[...] + b_ref[...]
            r_ref[...] = r
            ho_ref[...] = y
            hbo_ref[...] = y.astype(BF16)

    return pl.pallas_call(
        body, name=name, grid=(T // tm, nf),
        in_specs=[row, pl.BlockSpec((None, D, tf), lambda i, j: (j, 0, 0)),
                  pl.BlockSpec((None, tf, D), lambda i, j: (j, 1, 0)), row, vec, vec],
        out_specs=[pl.BlockSpec((tm, tf), lambda i, j: (i, j)), row, row, row],
        out_shape=[jax.ShapeDtypeStruct((T, F), BF16), jax.ShapeDtypeStruct((T, D), F32),
                   jax.ShapeDtypeStruct((T, D), F32), jax.ShapeDtypeStruct((T, D), BF16)],
        scratch_shapes=[pltpu.VMEM((tm, D), F32)],
        compiler_params=_params(48, 2),
    )(h_b, wbuf, wbuf, h, g, b)


def _loss_dy(y, tgt):
    T = y.shape[0]
    tm = min(ROW_BLOCK, T)
    row = pl.BlockSpec((tm, D), lambda i: (i, 0))

    def body(y_ref, t_ref, dy_ref, ls_ref):
        e = y_ref[...] - t_ref[...]
        dy_ref[...] = e * (1.0 / D)

        @pl.when(pl.program_id(0) == 0)
        def _():
            ls_ref[...] = jnp.zeros_like(ls_ref)

        ls_ref[...] += _fold8(e * e)

    return pl.pallas_call(
        body, name="loss_dy", grid=(T // tm,), in_specs=[row, row],
        out_specs=[row, pl.BlockSpec((8, D), lambda i: (0, 0))],
        out_shape=[jax.ShapeDtypeStruct((T, D), F32), jax.ShapeDtypeStruct((8, D), F32)],
        compiler_params=_params(32, 1),
    )(y, tgt)


def _ln_bwd(dy, r, g, *, name):
    T = dy.shape[0]
    tm = min(ROW_BLOCK, T)
    row = pl.BlockSpec((tm, D), lambda i: (i, 0))
    acc = pl.BlockSpec((8, D), lambda i: (0, 0))

    def body(dy_ref, r_ref, g_ref, dr_ref, drb_ref, dg_ref, db_ref):
        @pl.when(pl.program_id(0) == 0)
        def _():
            dg_ref[...] = jnp.zeros_like(dg_ref)
            db_ref[...] = jnp.zeros_like(db_ref)

        dy_ = dy_ref[...]
        xhat, rstd = _ln_stats(r_ref[...])
        dxh = dy_ * g_ref[...]
        m1 = jnp.mean(dxh, -1, keepdims=True)
        m2 = jnp.mean(dxh * xhat, -1, keepdims=True)
        dr = rstd * (dxh - m1 - xhat * m2)
        dr_ref[...] = dr
        drb_ref[...] = dr.astype(BF16)
        dg_ref[...] += _fold8(dy_ * xhat)
        db_ref[...] += _fold8(dy_)

    return pl.pallas_call(
        body, name=name, grid=(T // tm,),
        in_specs=[row, row, pl.BlockSpec((1, D), lambda i: (0, 0))],
        out_specs=[row, row, acc, acc],
        out_shape=[jax.ShapeDtypeStruct((T, D), F32), jax.ShapeDtypeStruct((T, D), BF16),
                   jax.ShapeDtypeStruct((8, D), F32), jax.ShapeDtypeStruct((8, D), F32)],
        compiler_params=_params(40, 1),
    )(dy, r, g)


def _rope(x, c, s1, s2):
    return x * c + pltpu.roll(x, 112, 1) * s1 + pltpu.roll(x, 16, 1) * s2


def _rope_t(dy, c, s1, s2):
    return dy * c + pltpu.roll(dy * s1, 16, 1) + pltpu.roll(dy * s2, 112, 1)


def _rms(x, g):
    rstd = lax.rsqrt(jnp.mean(x * x, -1, keepdims=True) + EPS)
    xhat = x * rstd
    return xhat * g, xhat, rstd


def _mla_prep(z0, gq, gkv, wq, wk, wv, rc, rs1, rs2):
    T = z0.shape[0]
    tm = min(ROW_BLOCK, T)
    HW = HEADS * 128

    def body(cq_ref, ckv_ref, kr_ref, gq_ref, gkv_ref, wq_ref, wk_ref, wv_ref, c_ref, s1_ref, s2_ref,
             q_ref, k_ref, v_ref):
        nq = _rms(cq_ref[...], gq_ref[...])[0].astype(BF16)
        nkv = _rms(ckv_ref[...], gkv_ref[...])[0].astype(BF16)
        q = jnp.dot(nq, wq_ref[...], preferred_element_type=F32)
        k = jnp.dot(nkv, wk_ref[...], preferred_element_type=F32)
        v = jnp.dot(nkv, wv_ref[...], preferred_element_type=F32)
        c, s1, s2 = c_ref[...], s1_ref[...], s2_ref[...]
        kr = _rope(pltpu.roll(kr_ref[...], 64, 1), c, s1, s2)
        for h in range(HEADS):
            sl = slice(h * 128, (h + 1) * 128)
            q_ref[:, sl] = (_rope(q[:, sl], c, s1, s2) * QK_SCALE).astype(BF16)
            k_ref[:, sl] = (k[:, sl] + kr).astype(BF16)
        v_ref[...] = v.astype(BF16)

    full = lambda shape: pl.BlockSpec(shape, lambda i: (0, 0))
    tab = pl.BlockSpec((tm, 128), lambda i: (i, 0))
    return pl.pallas_call(
        body, name="mla_prep", grid=(T // tm,),
        in_specs=[pl.BlockSpec((tm, 256), lambda i: (i, 0)), pl.BlockSpec((tm, 256), lambda i: (i, 1)),
                  pl.BlockSpec((tm, 128), lambda i: (i, 12)), full((1, 256)), full((1, 256)),
                  full((256, HW)), full((256, HW)), full((256, 512)), tab, tab, tab],
        out_specs=[pl.BlockSpec((tm, HW), lambda i: (i, 0)), pl.BlockSpec((tm, HW), lambda i: (i, 0)),
                   pl.BlockSpec((tm, 512), lambda i: (i, 0))],
        out_shape=[jax.ShapeDtypeStruct((T, HW), BF16), jax.ShapeDtypeStruct((T, HW), BF16),
                   jax.ShapeDtypeStruct((T, 512), BF16)],
        compiler_params=_params(40, 1),
    )(z0, z0, z0, gq, gkv, wq, wk, wv, rc, rs1, rs2)


def _flash_fwd(q, k, v, plan=None):
    T = q.shape[0]
    bq = min(2 * ROW_BLOCK, T)
    nq = T // bq
    grid = (4, nq, nq)

    def body(*refs):
        (q_ref, k_ref, v_ref), (o_ref, lse_ref), (m_sc, acc_sc), pctx = _split_refs(refs, 3, 2, 2, plan)
        _plan_start(plan, pctx, grid)
        i, j = pl.program_id(1), pl.program_id(2)
        first = lax.broadcasted_iota(jnp.int32, (bq, 128), 1) < 64

        @pl.when(j == 0)
        def _():
            m_sc[...] = jnp.full_like(m_sc, -jnp.inf)
            acc_sc[...] = jnp.zeros_like(acc_sc)

        def step(masked):
            vp = v_ref[...]
            for h in range(2):
                sl = slice(h * 128, (h + 1) * 128)
                s = lax.dot_general(q_ref[:, sl], k_ref[:, sl], NT_DIMS, preferred_element_type=F32)
                if masked:
                    rows = lax.broadcasted_iota(jnp.int32, (bq, bq), 0)
                    cols = lax.broadcasted_iota(jnp.int32, (bq, bq), 1)
                    s = jnp.where(cols <= rows, s, -jnp.inf)
                m_prev = m_sc[h, :, 0:1]
                m_new = jnp.maximum(m_prev, jnp.max(s, axis=1, keepdims=True))
                alpha = jnp.exp(m_prev - m_new)
                p = jnp.exp(s - m_new).astype(BF16)
                mine = first if h == 0 else jnp.logical_not(first)
                vh = jnp.where(mine, vp, jnp.ones_like(vp))
                acc_sc[h] = acc_sc[h] * alpha + jnp.dot(p, vh, preferred_element_type=F32)
                m_sc[h] = jnp.broadcast_to(m_new, (bq, 128))

        @pl.when(j < i)
        def _():
            step(False)

        @pl.when(j == i)
        def _():
            step(True)
            a0, a1 = acc_sc[0], acc_sc[1]
            l0, l1 = pltpu.roll(a0, 64, 1), pltpu.roll(a1, 64, 1)
            o_ref[...] = jnp.where(first, a0 / l0, a1 / l1).astype(BF16)
            lse_ref[...] = jnp.where(first, m_sc[0] + jnp.log(l0), m_sc[1] + jnp.log(l1))

        _plan_wait(plan, pctx, grid)

    kv = lambda hp, i, j: (jnp.minimum(i, j), hp)
    p_in, p_ospec, p_oshape, p_scr, p_alias = _plan_io(plan, 3, 2)
    return pl.pallas_call(
        body, name="flash_fwd", grid=grid,
        in_specs=[pl.BlockSpec((bq, 256), lambda hp, i, j: (i, hp)), pl.BlockSpec((bq, 256), kv),
                  pl.BlockSpec((bq, 128), kv)] + [_ANY] * len(p_in),
        out_specs=[pl.BlockSpec((bq, 128), lambda hp, i, j: (i, hp)),
                   pl.BlockSpec((bq, 128), lambda hp, i, j: (i, hp))] + p_ospec,
        out_shape=[jax.ShapeDtypeStruct((T, 512), BF16), jax.ShapeDtypeStruct((T, 512), F32)] + p_oshape,
        scratch_shapes=[pltpu.VMEM((2, bq, 128), F32), pltpu.VMEM((2, bq, 128), F32)] + p_scr,
        input_output_aliases=p_alias, compiler_params=_params(56, 3),
    )(q, k, v, *p_in)


def _attn_delta(dmix, o):
    T = o.shape[0]
    tm = min(ROW_BLOCK, T)
    blk = pl.BlockSpec((tm, 512), lambda i: (i, 0))

    def body(do_ref, o_ref, delta_ref, dob_ref):
        first = lax.broadcasted_iota(jnp.int32, (tm, 128), 1) < 64
        for hp in range(4):
            sl = slice(hp * 128, (hp + 1) * 128)
            prod = do_ref[:, sl] * o_ref[:, sl].astype(F32)
            d0 = jnp.sum(jnp.where(first, prod, 0.0), axis=1, keepdims=True)
            d1 = jnp.sum(jnp.where(first, 0.0, prod), axis=1, keepdims=True)
            delta_ref[:, sl] = jnp.where(first, d0, d1)
        dob_ref[...] = do_ref[...].astype(BF16)

    return pl.pallas_call(
        body, name="attn_delta", grid=(T // tm,), in_specs=[blk, blk], out_specs=[blk, blk],
        out_shape=[jax.ShapeDtypeStruct((T, 512), F32), jax.ShapeDtypeStruct((T, 512), BF16)],
        compiler_params=_params(32, 1),
    )(dmix, o)


def _flash_bwd(q, k, v, do_b, lse, delta, plan=None):
    T = q.shape[0]
    bq = min(2 * ROW_BLOCK, T)
    nq = T // bq
    grid = (4, nq, nq)

    def body(*refs):
        ((q_ref, k_ref, v_ref, do_ref, lse_ref, dl_ref), (dq_hbm, dk_ref, dv_ref), (dq_sc, dk_sc, dv_sc, sem),
         pctx) = _split_refs(refs, 6, 3, 4, plan)
        _plan_start(plan, pctx, grid)
        hp, j, i = pl.program_id(0), pl.program_id(1), pl.program_id(2)
        first = lax.broadcasted_iota(jnp.int32, (bq, 128), 1) < 64

        @pl.when((j == 0) & (i == 0))
        def _():
            dq_sc[...] = jnp.zeros_like(dq_sc)

        @pl.when(i == j)
        def _():
            dk_sc[...] = jnp.zeros_like(dk_sc)
            dv_sc[...] = jnp.zeros_like(dv_sc)

        def step(masked):
            vp = v_ref[...]
            do = do_ref[...]
            for h in range(2):
                sl = slice(h * 128, (h + 1) * 128)
                qh, kh = q_ref[:, sl], k_ref[:, sl]
                s = lax.dot_general(qh, kh, NT_DIMS, preferred_element_type=F32)
                p = jnp.exp(s - lse_ref[:, h * 64:h * 64 + 1])
                if masked:
                    rows = lax.broadcasted_iota(jnp.int32, (bq, bq), 0)
                    cols = lax.broadcasted_iota(jnp.int32, (bq, bq), 1)
                    p = jnp.where(cols <= rows, p, 0.0)
                mine = first if h == 0 else jnp.logical_not(first)
                do_h = jnp.where(mine, do, jnp.zeros_like(do))
                dv_sc[...] += lax.dot_general(p.astype(BF16), do_h, TN_DIMS, preferred_element_type=F32)
                dp = lax.dot_general(do_h, vp, NT_DIMS, preferred_element_type=F32)
                ds = (p * (dp - dl_ref[:, h * 64:h * 64 + 1])).astype(BF16)
                dq_sc[i, :, sl] += jnp.dot(ds, kh, preferred_element_type=F32)
                dk_sc[:, sl] += lax.dot_general(ds, qh, TN_DIMS, preferred_element_type=F32)

        @pl.when(i > j)
        def _():
            step(False)

        @pl.when(i == j)
        def _():
            step(True)

        @pl.when(i == nq - 1)
        def _():
            dk_ref[...] = dk_sc[...]
            dv_ref[...] = dv_sc[...]

        @pl.when((j == nq - 1) & (i == nq - 1))
        def _():
            cp = pltpu.make_async_copy(dq_sc, dq_hbm.at[hp], sem)
            cp.start()
            cp.wait()

        _plan_wait(plan, pctx, grid)

    qi = lambda hp, j, i: (jnp.maximum(i, j), hp)
    kj = lambda hp, j, i: (j, hp)
    p_in, p_ospec, p_oshape, p_scr, p_alias = _plan_io(plan, 6, 3)
    return pl.pallas_call(
        body, name="flash_bwd", grid=grid,
        in_specs=[pl.BlockSpec((bq, 256), qi), pl.BlockSpec((bq, 256), kj), pl.BlockSpec((bq, 128), kj),
                  pl.BlockSpec((bq, 128), qi), pl.BlockSpec((bq, 128), qi), pl.BlockSpec((bq, 128), qi)]
        + [_ANY] * len(p_in),
        out_specs=[_ANY, pl.BlockSpec((bq, 256), kj), pl.BlockSpec((bq, 128), kj)] + p_ospec,
        out_shape=[jax.ShapeDtypeStruct((4, nq, bq, 256), F32), jax.ShapeDtypeStruct((T, 1024), F32),
                   jax.ShapeDtypeStruct((T, 512), F32)] + p_oshape,
        scratch_shapes=[pltpu.VMEM((nq, bq, 256), F32), pltpu.VMEM((bq, 256), F32), pltpu.VMEM((bq, 128), F32),
                        pltpu.SemaphoreType.DMA] + p_scr,
        input_output_aliases=p_alias, compiler_params=_params(56, 3),
    )(q, k, v, do_b, lse, delta, *p_in)


def _mla_bwd(z0, dq4, dk, dv, gq, gkv, wq, wk, wv, rc, rs1, rs2, plan=None):
    T = z0.shape[0]
    tm = min(ROW_BLOCK, T)
    HW = HEADS * 128
    grid = (T // tm,)
    dq4 = dq4.reshape(4, T, 256)

    def body(*refs):
        ((cq_ref, ckv_ref, dq_ref, dk_ref, dv_ref, gq_ref, gkv_ref, wq_ref, wk_ref, wv_ref, c_ref, s1_ref, s2_ref),
         (dc_ref, dkr_ref, dwq_ref, dwk_ref, dwv_ref, dgq_ref, dgkv_ref), _, pctx) = _split_refs(refs, 13, 7, 0, plan)
        _plan_start(plan, pctx, grid)

        @pl.when(pl.program_id(0) == 0)
        def _():
            for ref in (dwq_ref, dwk_ref, dwv_ref, dgq_ref, dgkv_ref):
                ref[...] = jnp.zeros_like(ref)

        c, s1, s2 = c_ref[...], s1_ref[...], s2_ref[...]
        lane = lax.broadcasted_iota(jnp.int32, (tm, 128), 1)
        nq, xq, rq = _rms(cq_ref[...], gq_ref[...])
        nkv, xkv, rkv = _rms(ckv_ref[...], gkv_ref[...])
        nq_b, nkv_b = nq.astype(BF16), nkv.astype(BF16)

        dq_parts, dk_parts = [], []
        dkr = jnp.zeros((tm, 128), F32)
        for h in range(HEADS):
            blk = dq_ref[h // 2, :, (h % 2) * 128:(h % 2 + 1) * 128] * QK_SCALE
            dq_parts.append(_rope_t(blk, c, s1, s2).astype(BF16))
            kb = dk_ref[:, h * 128:(h + 1) * 128]
            dk_parts.append(jnp.where(lane < NOPE, kb, 0.0).astype(BF16))
            dkr = dkr + kb
        dq_b = jnp.concatenate(dq_parts, axis=1)
        dk_b = jnp.concatenate(dk_parts, axis=1)
        dv_b = dv_ref[...].astype(BF16)

        dwq_ref[...] += lax.dot_general(nq_b, dq_b, TN_DIMS, preferred_element_type=F32)
        dwk_ref[...] += lax.dot_general(nkv_b, dk_b, TN_DIMS, preferred_element_type=F32)
        dwv_ref[...] += lax.dot_general(nkv_b, dv_b, TN_DIMS, preferred_element_type=F32)
        dnq = lax.dot_general(dq_b, wq_ref[...], NT_DIMS, preferred_element_type=F32)
        dnkv = (lax.dot_general(dk_b, wk_ref[...], NT_DIMS, preferred_element_type=F32)
                + lax.dot_general(dv_b, wv_ref[...], NT_DIMS, preferred_element_type=F32))

        def rms_bwd(dn, xhat, rstd, g):
            dxh = dn * g
            return rstd * (dxh - xhat * jnp.mean(dxh * xhat, -1, keepdims=True))

        dc_ref[:, :256] = rms_bwd(dnq, xq, rq, gq_ref[...]).astype(BF16)
        dc_ref[:, 256:] = rms_bwd(dnkv, xkv, rkv, gkv_ref[...]).astype(BF16)
        dgq_ref[...] += _fold8(dnq * xq)
        dgkv_ref[...] += _fold8(dnkv * xkv)
        dkr = pltpu.roll(_rope_t(dkr, c, s1, s2), 64, 1)
        dkr_ref[...] = jnp.where(lane < ROPE, dkr, 0.0).astype(BF16)
        _plan_wait(plan, pctx, grid)

    full = lambda shape: pl.BlockSpec(shape, lambda i: (0,) * len(shape))
    tab = pl.BlockSpec((tm, 128), lambda i: (i, 0))
    p_in, p_ospec, p_oshape, p_scr, p_alias = _plan_io(plan, 13, 7)
    return pl.pallas_call(
        body, name="mla_bwd", grid=grid,
        in_specs=[pl.BlockSpec((tm, 256), lambda i: (i, 0)), pl.BlockSpec((tm, 256), lambda i: (i, 1)),
                  pl.BlockSpec((4, tm, 256), lambda i: (0, i, 0)),
                  pl.BlockSpec((tm, HW), lambda i: (i, 0)), pl.BlockSpec((tm, 512), lambda i: (i, 0)),
                  full((1, 256)), full((1, 256)), full((256, HW)), full((256, HW)), full((256, 512)), tab, tab, tab]
        + [_ANY] * len(p_in),
        out_specs=[pl.BlockSpec((tm, 512), lambda i: (i, 0)), tab, full((256, HW)), full((256, HW)),
                   full((256, 512)), full((8, 256)), full((8, 256))] + p_ospec,
        out_shape=[jax.ShapeDtypeStruct((T, 512), BF16), jax.ShapeDtypeStruct((T, 128), BF16),
                   jax.ShapeDtypeStruct((256, HW), F32), jax.ShapeDtypeStruct((256, HW), F32),
                   jax.ShapeDtypeStruct((256, 512), F32), jax.ShapeDtypeStruct((8, 256), F32),
                   jax.ShapeDtypeStruct((8, 256), F32)] + p_oshape,
        scratch_shapes=p_scr, input_output_aliases=p_alias, compiler_params=_params(48, 1),
    )(z0, z0, dq4, dk, dv, gq, gkv, wq, wk, wv, rc, rs1, rs2, *p_in)


def _sgu_fwd(z0, a_out, ln_g, ln_b, w, b_t):
    T = z0.shape[0]
    tm = min(ROW_BLOCK, T)
    W = SGU_G * SGU_C

    def body(u_ref, v_ref, a_ref, g_ref, b_ref, w_ref, bt_ref, o_ref):
        o_ref[:, :W] = a_ref[...]
        ug = _gelu(u_ref[...])
        xhat, _ = _ln_stats(_gelu(v_ref[...]))
        vn = (xhat * g_ref[...] + b_ref[...]).astype(BF16)
        tril = lax.broadcasted_iota(jnp.int32, (SGU_C, SGU_C), 0) >= lax.broadcasted_iota(jnp.int32, (SGU_C, SGU_C), 1)
        for g in range(SGU_G):
            cs = slice(g * SGU_C, (g + 1) * SGU_C)
            wg = jnp.where(tril, w_ref[g], 0.0).astype(BF16)
            bcol = bt_ref[:, g:g + 1]
            for c in range(tm // SGU_C):
                rs = slice(c * SGU_C, (c + 1) * SGU_C)
                mixed = jnp.dot(wg, vn[rs, cs], preferred_element_type=F32) + bcol
                o_ref[rs, W + g * SGU_C:W + (g + 1) * SGU_C] = (ug[rs, cs] * mixed).astype(BF16)

    full = lambda shape: pl.BlockSpec(shape, lambda i: (0,) * len(shape))
    return pl.pallas_call(
        body, name="sgu_fwd", grid=(T // tm,),
        in_specs=[pl.BlockSpec((tm, W), lambda i: (i, 1)), pl.BlockSpec((tm, W), lambda i: (i, 2)),
                  pl.BlockSpec((tm, W), lambda i: (i, 0)),
                  full((1, W)), full((1, W)), full((SGU_G, SGU_C, SGU_C)), full((SGU_C, SGU_G))],
        out_specs=pl.BlockSpec((tm, 2 * W), lambda i: (i, 0)),
        out_shape=jax.ShapeDtypeStruct((T, 2 * W), BF16),
        compiler_params=_params(32, 1),
    )(z0, z0, a_out, ln_g, ln_b, w, b_t)


def _sgu_bwd(z0, dmix, dc, dkr, ln_g, ln_b, w, b_t):
    T = z0.shape[0]
    tm = min(ROW_BLOCK, T)
    W = SGU_G * SGU_C

    def body(u_ref, v_ref, do_ref, dc_ref, dkr_ref, g_ref, b_ref, w_ref, bt_ref, dz_ref, dw_ref, db_ref, dlg_ref,
             dlb_ref):
        @pl.when(pl.program_id(0) == 0)
        def _():
            for ref in (dw_ref, db_ref, dlg_ref, dlb_ref):
                ref[...] = jnp.zeros_like(ref)

        dz_ref[:, :W] = dc_ref[...]
        dz_ref[:, 3 * W:] = dkr_ref[...]

        u, v, dout = u_ref[...], v_ref[...], do_ref[...]
        ug = _gelu(u)
        xhat, rstd = _ln_stats(_gelu(v))
        vn = (xhat * g_ref[...] + b_ref[...]).astype(BF16)
        dmixed = dout * ug
        dmixed_b = dmixed.astype(BF16)
        tril = lax.broadcasted_iota(jnp.int32, (SGU_C, SGU_C), 0) >= lax.broadcasted_iota(jnp.int32, (SGU_C, SGU_C), 1)
        lane = lax.broadcasted_iota(jnp.int32, (SGU_C, SGU_C), 1)
        dvn_cols = []
        for g in range(SGU_G):
            cs = slice(g * SGU_C, (g + 1) * SGU_C)
            wg = jnp.where(tril, w_ref[g], 0.0).astype(BF16)
            bcol = bt_ref[:, g:g + 1]
            dw_g = jnp.zeros((SGU_C, SGU_C), F32)
            db_g = jnp.zeros((SGU_C, 1), F32)
            dvn_rows = []
            for c in range(tm // SGU_C):
                rs = slice(c * SGU_C, (c + 1) * SGU_C)
                mixed = jnp.dot(wg, vn[rs, cs], preferred_element_type=F32) + bcol
                dz_ref[rs, W + g * SGU_C:W + (g + 1) * SGU_C] = (dout[rs, cs] * mixed * _gelu_grad(u[rs, cs])).astype(BF16)
                dm = dmixed_b[rs, cs]
                dw_g = dw_g + lax.dot_general(dm, vn[rs, cs], NT_DIMS, preferred_element_type=F32)
                db_g = db_g + jnp.sum(dmixed[rs, cs], axis=1, keepdims=True)
                dvn_rows.append(lax.dot_general(wg, dm, TN_DIMS, preferred_element_type=F32))
            dw_ref[g] += jnp.where(tril, dw_g, 0.0)
            db_ref[...] += jnp.where(lane == g, db_g, 0.0)
            dvn_cols.append(jnp.concatenate(dvn_rows, axis=0))
        dvn = jnp.concatenate(dvn_cols, axis=1)
        dxh = dvn * g_ref[...]
        m1 = jnp.mean(dxh, -1, keepdims=True)
        m2 = jnp.mean(dxh * xhat, -1, keepdims=True)
        dvg = rstd * (dxh - m1 - xhat * m2)
        dz_ref[:, 2 * W:3 * W] = (dvg * _gelu_grad(v)).astype(BF16)
        dlg_ref[...] += _fold8(dvn * xhat)
        dlb_ref[...] += _fold8(dvn)

    full = lambda shape: pl.BlockSpec(shape, lambda i: (0,) * len(shape))
    return pl.pallas_call(
        body, name="sgu_bwd", grid=(T // tm,),
        in_specs=[pl.BlockSpec((tm, W), lambda i: (i, 1)), pl.BlockSpec((tm, W), lambda i: (i, 2)),
                  pl.BlockSpec((tm, W), lambda i: (i, 1)), pl.BlockSpec((tm, W), lambda i: (i, 0)),
                  pl.BlockSpec((tm, 128), lambda i: (i, 0)),
                  full((1, W)), full((1, W)), full((SGU_G, SGU_C, SGU_C)), full((SGU_C, SGU_G))],
        out_specs=[pl.BlockSpec((tm, 3 * W + 128), lambda i: (i, 0)), full((SGU_G, SGU_C, SGU_C)),
                   full((SGU_C, SGU_C)), full((8, W)), full((8, W))],
        out_shape=[jax.ShapeDtypeStruct((T, 3 * W + 128), BF16), jax.ShapeDtypeStruct((SGU_G, SGU_C, SGU_C), F32),
                   jax.ShapeDtypeStruct((SGU_C, SGU_C), F32), jax.ShapeDtypeStruct((8, W), F32),
                   jax.ShapeDtypeStruct((8, W), F32)],
        compiler_params=_params(40, 1),
    )(z0, z0, dmix, dc, dkr, ln_g, ln_b, w, b_t)


def _hg_lower_bound(lb_ref):
    a0, a1 = lb_ref[0:1, :], lb_ref[1:2, :]
    m = jnp.maximum(a0, a1)
    e0, e1 = jnp.exp(a0 - m), jnp.exp(a1 - m)
    return e1 / (e0 + e1)


def _running_sum(x, reverse=False):
    n = x.shape[0]
    row = lax.broadcasted_iota(jnp.int32, x.shape, 0)
    s = 1
    while s < n:
        if reverse:
            x = x + jnp.where(row < n - s, pltpu.roll(x, n - s, 0), 0.0)
        else:
            x = x + jnp.where(row >= s, pltpu.roll(x, s, 0), 0.0)
        s *= 2
    return x


def _hg_chunk(qc, fc, lb):
    C = HG_CHUNK
    rows = lax.broadcasted_iota(jnp.int32, (C, C), 0)
    cols = lax.broadcasted_iota(jnp.int32, (C, C), 1)
    rowid = lax.broadcasted_iota(jnp.int32, (C, 128), 0)
    sq, sg = _sigmoid(qc), _sigmoid(fc)
    qf = qc * sq
    gate = lb + (1.0 - lb) * sg
    kk = 1.0 - gate
    lg = jnp.log(gate)
    bcum = _running_sum(lg)
    b_mid = jnp.sum(jnp.where(rowid < C // 2, lg, 0.0), axis=0, keepdims=True)
    b_last = jnp.sum(lg, axis=0, keepdims=True)
    eq, ek, e, eh = jnp.exp(bcum - b_mid), jnp.exp(b_mid - bcum), jnp.exp(bcum), jnp.exp(b_last - bcum)
    qt, kt, qe, khat = qf * eq, kk * ek, qf * e, kk * eh
    a = lax.dot_general(qt.astype(BF16), kt.astype(BF16), NT_DIMS, preferred_element_type=F32)
    a = jnp.where(rows >= cols, a, 0.0)
    return dict(sq=sq, sg=sg, gate=gate, kk=kk, eq=eq, ek=ek, e=e, eh=eh, qt=qt, kt=kt, qe=qe, khat=khat, a=a,
                e_last=jnp.exp(b_last), tril=rows >= cols, rowid=rowid)


def _hgrn_fwd(z4, hg_lb, gnorm):
    T = z4.shape[1]
    tb = min(ROW_BLOCK, T)
    C = HG_CHUNK
    ncb = tb // C
    HPB = HG_HEADS_PER_STEP

    def body(q_ref, f_ref, i_ref, g_ref, lb_ref, gn_ref, y_ref, o_ref, st_ref, st_sc):
        @pl.when(pl.program_id(1) == 0)
        def _():
            st_sc[...] = jnp.zeros_like(st_sc)

        def chunk(c, carry):
            rs = pl.ds(pl.multiple_of(c * C, C), C)
            for hh in range(HPB):
                hs = slice(hh * 128, (hh + 1) * 128)
                lb = _hg_lower_bound(lb_ref.at[:, hs])
                v_b = i_ref[rs, hs].astype(BF16)
                gc = g_ref[rs, hs]
                x = _hg_chunk(q_ref[rs, hs], f_ref[rs, hs], lb)
                st = st_sc[hh]
                st_ref[hh, c] = st
                o = (jnp.dot(x["a"].astype(BF16), v_b, preferred_element_type=F32)
                     + lax.dot_general(x["qe"].astype(BF16), st.astype(BF16), NT_DIMS, preferred_element_type=F32))
                st_sc[hh] = st * x["e_last"] + lax.dot_general(v_b, x["khat"].astype(BF16), TN_DIMS,
                                                               preferred_element_type=F32)
                o_ref[rs, hs] = o
                n = o * lax.rsqrt(jnp.mean(o * o, -1, keepdims=True) + EPS)
                y_ref[rs, hs] = (n * gn_ref[:, hs] * (gc * _sigmoid(gc))).astype(BF16)
            return carry

        lax.fori_loop(0, ncb, chunk, 0)

    W = 128 * HPB
    zb = lambda k: pl.BlockSpec((None, tb, W), lambda h, t: (k, t, h))
    out = pl.BlockSpec((tb, W), lambda h, t: (t, h))
    return pl.pallas_call(
        body, name="hgrn_fwd", grid=(HEADS // HPB, T // tb),
        in_specs=[zb(0), zb(1), zb(2), zb(3), pl.BlockSpec((2, W), lambda h, t: (0, h)),
                  pl.BlockSpec((1, W), lambda h, t: (0, h))],
        out_specs=[out, out, pl.BlockSpec((HPB, ncb, 128, 128), lambda h, t: (h, t, 0, 0))],
        out_shape=[jax.ShapeDtypeStruct((T, D), BF16), jax.ShapeDtypeStruct((T, D), F32),
                   jax.ShapeDtypeStruct((HEADS, T // C, 128, 128), F32)],
        scratch_shapes=[pltpu.VMEM((HPB, 128, 128), F32)],
        compiler_params=_params(32, 2),
    )(z4, z4, z4, z4, hg_lb, gnorm)


def _hgrn_bwd(z4, o_raw, dy, states, hg_lb, gnorm):
    T = z4.shape[1]
    tb = min(ROW_BLOCK, T)
    C = HG_CHUNK
    ncb = tb // C
    nt = T // tb
    HPB = HG_HEADS_PER_STEP

    def body(q_ref, f_ref, i_ref, g_ref, o_ref, dy_ref, st_ref, lb_ref, gn_ref, dz_ref, dlb_ref, dgn_ref, dst_sc):
        @pl.when(pl.program_id(1) == 0)
        def _():
            dst_sc[...] = jnp.zeros_like(dst_sc)
            dlb_ref[...] = jnp.zeros_like(dlb_ref)
            dgn_ref[...] = jnp.zeros_like(dgn_ref)

        def chunk(cc, carry):
            for hh in range(HPB):
                one_head(ncb - 1 - cc, hh, slice(hh * 128, (hh + 1) * 128))
            return carry

        def one_head(c, hh, hs):
            rs = pl.ds(pl.multiple_of(c * C, C), C)
            lb = _hg_lower_bound(lb_ref.at[:, hs])
            gn = gn_ref[:, hs]
            qc, gc = q_ref[rs, hs], g_ref[rs, hs]
            v_b = i_ref[rs, hs].astype(BF16)
            x = _hg_chunk(qc, f_ref[rs, hs], lb)
            st, dst = st_ref[hh, c], dst_sc[hh]
            st_b, dst_b = st.astype(BF16), dst.astype(BF16)
            o, dyc = o_ref[rs, hs], dy_ref[rs, hs]
            sgg = _sigmoid(gc)
            sil = gc * sgg
            rstd = lax.rsqrt(jnp.mean(o * o, -1, keepdims=True) + EPS)
            n = o * rstd
            dgn_ref[:, hs] += _fold8(dyc * n * sil)
            dn = dyc * gn * sil
            do = rstd * (dn - n * jnp.mean(dn * n, -1, keepdims=True))
            dg = dyc * n * gn * (sgg * (1.0 + gc * (1.0 - sgg)))
            do_b = do.astype(BF16)
            da = jnp.where(x["tril"], lax.dot_general(do_b, v_b, NT_DIMS, preferred_element_type=F32), 0.0).astype(BF16)
            qt_b, kt_b, qe_b, khat_b = (x[n_].astype(BF16) for n_ in ("qt", "kt", "qe", "khat"))
            dv = (lax.dot_general(x["a"].astype(BF16), do_b, TN_DIMS, preferred_element_type=F32)
                  + lax.dot_general(khat_b, dst_b, NT_DIMS, preferred_element_type=F32))
            dqt = jnp.dot(da, kt_b, preferred_element_type=F32)
            dqe = jnp.dot(do_b, st_b, preferred_element_type=F32)
            dkt = lax.dot_general(da, qt_b, TN_DIMS, preferred_element_type=F32)
            dkhat = jnp.dot(v_b, dst_b, preferred_element_type=F32)
            dst_sc[hh] = lax.dot_general(do_b, qe_b, TN_DIMS, preferred_element_type=F32) + dst * x["e_last"]
            de_last = jnp.sum(st * dst, axis=0, keepdims=True)
            dqf = dqt * x["eq"] + dqe * x["e"]
            dkk = dkt * x["ek"] + dkhat * x["eh"]
            dkh_kh = dkhat * x["khat"]
            db = dqt * qt_b.astype(F32) - dkt * kt_b.astype(F32) + dqe * x["qe"] - dkh_kh
            db_last = jnp.sum(dkh_kh, axis=0, keepdims=True) + de_last * x["e_last"]
            db = db + jnp.where(x["rowid"] == C - 1, db_last, 0.0)
            dlg = _running_sum(db, reverse=True)
            dgate = dlg / x["gate"] - dkk
            sg, sq = x["sg"], x["sq"]
            dlb_ref[:, hs] += _fold8(dgate * (1.0 - sg)) * (lb * (1.0 - lb))
            dz_ref[0, rs, hs] = (dqf * (sq * (1.0 + qc * (1.0 - sq)))).astype(BF16)
            dz_ref[1, rs, hs] = (dgate * (1.0 - lb) * sg * (1.0 - sg)).astype(BF16)
            dz_ref[2, rs, hs] = dv.astype(BF16)
            dz_ref[3, rs, hs] = dg.astype(BF16)

        lax.fori_loop(0, ncb, chunk, 0)

    W = 128 * HPB
    zb = lambda k: pl.BlockSpec((None, tb, W), lambda h, t: (k, nt - 1 - t, h))
    blk = pl.BlockSpec((tb, W), lambda h, t: (nt - 1 - t, h))
    acc = pl.BlockSpec((8, W), lambda h, t: (0, h))
    return pl.pallas_call(
        body, name="hgrn_bwd", grid=(HEADS // HPB, nt),
        in_specs=[zb(0), zb(1), zb(2), zb(3), blk, blk,
                  pl.BlockSpec((HPB, ncb, 128, 128), lambda h, t: (h, nt - 1 - t, 0, 0)),
                  pl.BlockSpec((2, W), lambda h, t: (0, h)), pl.BlockSpec((1, W), lambda h, t: (0, h))],
        out_specs=[pl.BlockSpec((4, tb, W), lambda h, t: (0, nt - 1 - t, h)), acc, acc],
        out_shape=[jax.ShapeDtypeStruct((4, T, D), BF16), jax.ShapeDtypeStruct((8, D), F32),
                   jax.ShapeDtypeStruct((8, D), F32)],
        scratch_shapes=[pltpu.VMEM((HPB, 128, 128), F32)],
        compiler_params=_params(32, 2),
    )(z4, z4, z4, z4, o_raw, dy, states, hg_lb, gnorm)


def _adamw(w, g, m, v, *, name):
    R, L = w.shape
    tr = R if R <= 512 else 512
    assert R % tr == 0
    blk = pl.BlockSpec((tr, L), lambda i: (i, 0))
    c1, c2 = 1.0 - B1 ** STEP, 1.0 - B2 ** STEP

    def body(w_ref, g_ref, m_ref, v_ref, d_ref, mo_ref, vo_ref):
        g_ = g_ref[...]
        m_ = B1 * m_ref[...] + (1.0 - B1) * g_
        v_ = B2 * v_ref[...] + (1.0 - B2) * (g_ * g_)
        d_ref[...] = -LR * ((m_ / c1) / (jnp.sqrt(v_ / c2) + ADAM_EPS) + WD * w_ref[...])
        mo_ref[...] = m_
        vo_ref[...] = v_

    sds = jax.ShapeDtypeStruct((R, L), F32)
    return pl.pallas_call(
        body, name=name, grid=(R // tr,), in_specs=[blk] * 4, out_specs=[blk] * 3, out_shape=[sds] * 3,
        compiler_params=_params(32, 1),
    )(w, g, m, v)


def _adamw_rows(w, m, v, gbufs, row0, *, name, plan=None):
    L, R, C = w.shape
    tr = 256
    assert R % tr == 0 and row0 % tr == 0 and len(gbufs) == L
    grid = (L, R // tr)
    blk = pl.BlockSpec((None, tr, C), lambda l, i: (l, i, 0))
    gblk = pl.BlockSpec((tr, C), lambda l, i: (row0 // tr + i, 0))
    c1, c2 = 1.0 - B1 ** STEP, 1.0 - B2 ** STEP

    def body(*refs):
        ins, (go_ref, d_ref, mo_ref, vo_ref), _, pctx = _split_refs(refs, 3 + L, 4, 0, plan)
        w_ref, m_ref, v_ref = ins[:3]
        g_refs = ins[3:]
        _plan_start(plan, pctx, grid)
        g_ = g_refs[0][...]
        for l in range(1, L):
            g_ = jnp.where(pl.program_id(0) == l, g_refs[l][...], g_)
        m_ = B1 * m_ref[...] + (1.0 - B1) * g_
        v_ = B2 * v_ref[...] + (1.0 - B2) * (g_ * g_)
        go_ref[...] = g_
        d_ref[...] = -LR * ((m_ / c1) / (jnp.sqrt(v_ / c2) + ADAM_EPS) + WD * w_ref[...])
        mo_ref[...] = m_
        vo_ref[...] = v_
        _plan_wait(plan, pctx, grid)

    sds = jax.ShapeDtypeStruct((L, R, C), F32)
    p_in, p_ospec, p_oshape, p_scr, p_alias = _plan_io(plan, 3 + L, 4)
    return pl.pallas_call(
        body, name=name, grid=grid, in_specs=[blk] * 3 + [gblk] * L + [_ANY] * len(p_in),
        out_specs=[blk] * 4 + p_ospec, out_shape=[sds] * 4 + p_oshape, scratch_shapes=p_scr,
        input_output_aliases=p_alias, compiler_params=_params(32, 2),
    )(w, m, v, *gbufs, *p_in)


def _add_pairs(g, theirs, ids, *, name):
    n, R, L = theirs.shape
    tr = 128
    nb = R // tr

    def body(ids_ref, a_ref, b_ref, o_ref):
        o_ref[...] = (a_ref[...].astype(F32) + b_ref[...].astype(F32)).astype(BF16)

    blk = pl.BlockSpec((n, tr, L), lambda i, ids: (0, i, 0))
    return pl.pallas_call(
        body, name=name, out_shape=jax.ShapeDtypeStruct((n, R, L), BF16),
        grid_spec=pltpu.PrefetchScalarGridSpec(
            num_scalar_prefetch=1, grid=(nb,),
            in_specs=[pl.BlockSpec((n, tr, L), lambda i, ids: (0, ids[1] * nb + i, 0)), blk], out_specs=blk),
        compiler_params=_params(16, 1),
    )(ids, g, theirs)


def _sum_chips(pair, parts, ids, *, name):
    _, R, L = parts.shape
    tr = 128

    def body(ids_ref, o_ref, r_ref, out_ref):
        out_ref[...] = ((o_ref[...].astype(F32) + r_ref[0].astype(F32)) + r_ref[1].astype(F32)) + r_ref[2].astype(F32)

    return pl.pallas_call(
        body, name=name, out_shape=jax.ShapeDtypeStruct((2, R, L), F32),
        grid_spec=pltpu.PrefetchScalarGridSpec(
            num_scalar_prefetch=1, grid=(R // tr,),
            in_specs=[pl.BlockSpec((None, tr, L), lambda i, ids: (ids[0], i, 0)),
                      pl.BlockSpec((3, tr, L), lambda i, ids: (0, i, 0))],
            out_specs=pl.BlockSpec((None, tr, L), lambda i, ids: (ids[1], i, 0))),
        compiler_params=_params(32, 1),
    )(ids, pair, parts)


def _mesh_ids():
    x, y, c = _mesh_pos()
    return jnp.stack([2 * x + y, c]).astype(jnp.int32)


def _place_shard(rows, ids, *, name):
    R, L = rows.shape
    tr = 256

    def body(ids_ref, in_ref, out_ref):
        out_ref[...] = in_ref[...].astype(BF16)

    return pl.pallas_call(
        body, name=name, out_shape=jax.ShapeDtypeStruct((4, R, L), BF16),
        grid_spec=pltpu.PrefetchScalarGridSpec(
            num_scalar_prefetch=1, grid=(R // tr,), in_specs=[pl.BlockSpec((tr, L), lambda i, ids: (i, 0))],
            out_specs=pl.BlockSpec((None, tr, L), lambda i, ids: (ids[0], i, 0))),
        compiler_params=_params(16, 1),
    )(ids, rows)


def _remote(src, dst, send_sem, recv_sem, to):
    return pltpu.make_async_remote_copy(src_ref=src, dst_ref=dst, send_sem=send_sem, recv_sem=recv_sem,
                                        device_id=to, device_id_type=MESH_IDS)


def _rows(ref, lead, start, size):
    return ref.at[tuple(pl.ds(0, n) for n in ref.shape[:lead]) + (pl.ds(start, size),)]


def _other_chips():
    x, y, _ = _mesh_pos()
    return [(1 - x, y), (x, 1 - y), (1 - x, 1 - y)]


def _plan_gather_ici(bufs):
    n = len(bufs)

    def copies(outs, send, recv):
        x, y, c = _mesh_pos()
        res = []
        for b in range(n):
            half = bufs[b].shape[1] // 2
            mine = _rows(outs[b].at[2 * x + y], 0, c * half, half)
            for j, (cx, cy) in enumerate(_other_chips()):
                res.append((_remote(mine, mine, send(3 * b + j), recv(3 * b + j), (cx, cy, c)),
                            _remote(mine, _rows(outs[b].at[2 * cx + cy], 0, c * half, half),
                                    send(3 * b + j), recv(3 * b + j), (x, y, c))))
        return res

    def start(ins, outs, send, recv, loc):
        for out_cp, _ in copies(outs, send, recv):
            out_cp.start()

    def wait(ins, outs, send, recv, loc):
        for out_cp, in_cp in copies(outs, send, recv):
            in_cp.wait_recv()
            out_cp.wait_send()

    outs = [jax.ShapeDtypeStruct(b.shape, b.dtype) for b in bufs]
    return _Plan(bufs, outs, 3 * n, 0, start, wait, aliases={b: b for b in range(n)})


def _plan_gather_forward(bufs):
    n = len(bufs)

    def copies(outs, send, recv):
        x, y, c = _mesh_pos()
        res = []
        for b in range(n):
            half = bufs[b].shape[1] // 2
            for j, (cx, cy) in enumerate(_other_chips()):
                slot = outs[b].at[2 * cx + cy]
                res.append((_remote(_rows(slot, 0, c * half, half), _rows(slot, 0, c * half, half),
                                    send(3 * b + j), recv(3 * b + j), (x, y, 1 - c)),
                            _remote(_rows(slot, 0, c * half, half), _rows(slot, 0, (1 - c) * half, half),
                                    send(3 * b + j), recv(3 * b + j), (x, y, c))))
        return res

    def start(ins, outs, send, recv, loc):
        for out_cp, _ in copies(outs, send, recv):
            out_cp.start()

    def wait(ins, outs, send, recv, loc):
        for out_cp, in_cp in copies(outs, send, recv):
            in_cp.wait_recv()
            out_cp.wait_send()

    outs = [jax.ShapeDtypeStruct(b.shape, b.dtype) for b in bufs]
    return _Plan(bufs, outs, 3 * n, 0, start, wait, aliases={b: b for b in range(n)})


def _plan_pair_swap(g):
    half = g.shape[1] // 2

    def copy(ins, outs, send, recv, loc):
        x, y, c = _mesh_pos()
        return _remote(_rows(ins[0], 1, (1 - c) * half, half), outs[0], send(0), recv(0), (x, y, 1 - c))

    return _Plan([g], [jax.ShapeDtypeStruct((4, half, g.shape[2]), g.dtype)], 1, 0,
                 lambda *a: copy(*a).start(), lambda *a: copy(*a).wait())


def _plan_pair_gather(buf):
    def copies(ins, outs, send, recv, loc):
        x, y, c = _mesh_pos()
        return (_remote(outs[0].at[c], outs[0].at[c], send(0), recv(0), (x, y, 1 - c)),
                _remote(outs[0].at[c], outs[0].at[1 - c], send(0), recv(0), (x, y, c)))

    def wait(*a):
        out_cp, in_cp = copies(*a)
        in_cp.wait_recv()
        out_cp.wait_send()

    return _Plan([buf], [jax.ShapeDtypeStruct(buf.shape, buf.dtype)], 1, 0, lambda *a: copies(*a)[0].start(), wait,
                 aliases={0: 0})


def _plan_chip_scatter(p):
    def copies(ins, outs, send, recv, loc):
        _, _, c = _mesh_pos()
        return [_remote(ins[0].at[2 * cx + cy], outs[0].at[j], send(j), recv(j), (cx, cy, c))
                for j, (cx, cy) in enumerate(_other_chips())]

    def start(*a):
        for cp in copies(*a):
            cp.start()

    def wait(*a):
        for cp in copies(*a):
            cp.wait()

    return _Plan([p], [jax.ShapeDtypeStruct((3,) + p.shape[1:], p.dtype)], 3, 0, start, wait)


def _plan_exchange_all(vec):
    def copies(ins, outs, send, recv, loc):
        x, y, c = _mesh_pos()
        return [_remote(ins[0], outs[0].at[r - 1], send(r - 1), recv(r - 1), (x ^ (r >> 2), y ^ ((r >> 1) & 1), c ^ (r & 1)))
                for r in range(1, 8)]

    def start(*a):
        for cp in copies(*a):
            cp.start()

    def wait(*a):
        for cp in copies(*a):
            cp.wait()

    return _Plan([vec], [jax.ShapeDtypeStruct((7,) + vec.shape, vec.dtype)], 7, 0, start, wait)


def _sum_devices(vec, others, ids):
    R, L = vec.shape

    def body(ids_ref, v_ref, o_ref, out_ref):
        me = 2 * ids_ref[0] + ids_ref[1]
        total = None
        for d in range(8):
            rel = d ^ me
            term = jnp.where(rel == 0, v_ref[...], o_ref[jnp.maximum(rel - 1, 0)])
            total = term if total is None else total + term
        out_ref[...] = total

    return pl.pallas_call(
        body, name="small_grad_sum", out_shape=jax.ShapeDtypeStruct((R, L), F32),
        grid_spec=pltpu.PrefetchScalarGridSpec(
            num_scalar_prefetch=1, grid=(1,), in_specs=[pl.BlockSpec((R, L), lambda i, ids: (0, 0)),
                                                        pl.BlockSpec((7, R, L), lambda i, ids: (0, 0, 0))],
            out_specs=pl.BlockSpec((R, L), lambda i, ids: (0, 0))),
        compiler_params=_params(16, 1),
    )(ids, vec, others)


ROWS_L1, ROWS_L0, ROWS_ODD = 3328, 2048, 768
ODD_PARTS = (("w_out_e", (256, 1024)), ("w_in_e", (1024, 392)), ("w_qb", (256, 192)), ("w_kvb", (256, 256)))


def _odd_rows(parts, dtype, gnorm=None):
    rows = [parts[n].reshape(-1, 1024).astype(dtype) for n, _ in ODD_PARTS]
    used = sum(r.shape[0] for r in rows)
    if gnorm is not None:
        bits = lax.bitcast_convert_type(gnorm.reshape(-1), BF16).reshape(1, 512)
        rows.append(jnp.pad(bits, ((0, 0), (0, 512))))
        used += 1
    rows.append(jnp.zeros((ROWS_ODD - used, 1024), dtype))
    return jnp.concatenate(rows, axis=0)


def _odd_unrows(buf, with_gnorm=False):
    out, off = {}, 0
    for n, shape in ODD_PARTS:
        nr = math.prod(shape) // 1024
        out[n] = buf[off:off + nr].reshape(shape)
        off += nr
    if with_gnorm:
        out["hg_gnorm"] = lax.bitcast_convert_type(buf[off, :512].reshape(256, 2), F32).reshape(1, 256)
    return out


def _pack_small(vals, last):
    flat = jnp.concatenate([vals[n].reshape(-1).astype(F32) for n, _ in SMALL])
    pad = jnp.zeros((SMALL_ROWS * 1024 - flat.shape[0] - 1,), F32)
    return jnp.concatenate([flat, pad, last.reshape(1)]).reshape(SMALL_ROWS, 1024)


def _unpack_small(packed):
    flat = packed.reshape(-1)
    out, off = {}, 0
    for n, shape in SMALL:
        size = math.prod(shape)
        out[n] = flat[off:off + size].reshape(shape)
        off += size
    return out


def _rope_tables(positions):
    half = ROPE // 2
    inv_freq = ROPE_BASE ** (-jnp.arange(half, dtype=F32) / half)
    ang = positions.astype(F32).reshape(-1, 1) * inv_freq
    cos, sin = jnp.cos(ang), jnp.sin(ang)
    T = ang.shape[0]
    one, z16, z32 = jnp.ones((T, NOPE), F32), jnp.zeros((T, half), F32), jnp.zeros((T, 32), F32)
    z64 = jnp.zeros((T, NOPE), F32)
    c = jnp.concatenate([one, cos, cos, z32], axis=1)
    s1 = jnp.concatenate([z64, -sin, z16, z32], axis=1)
    s2 = jnp.concatenate([z64, z16, sin, z32], axis=1)
    return c, s1, s2


def _local_step(x, positions, tgt, odd, bufs, P, exchange):
    T = x.shape[0]
    row = lambda a: a.reshape(1, -1)
    rc, rs1, rs2 = _rope_tables(positions)
    blk = lambda f: pl.BlockSpec((None, D, D), f)

    w_in_e = odd["w_in_e"]
    w_in = jnp.concatenate([w_in_e[:, :512], w_in_e[:, 544:1568], w_in_e[:, 512:544], jnp.zeros((D, 96), BF16)], axis=1)
    wq = jnp.pad(odd["w_qb"].reshape(256, HEADS, NOPE + ROPE), ((0, 0), (0, 0), (0, 32))).reshape(256, HEADS * 128)
    kvb = odd["w_kvb"].reshape(256, HEADS, NOPE + VDIM)
    wk = jnp.pad(kvb[:, :, :NOPE], ((0, 0), (0, 0), (0, 64))).reshape(256, HEADS * 128)
    wv = kvb[:, :, NOPE:].reshape(256, HEADS * VDIM)
    w_out_e = odd["w_out_e"]
    sgu_w = P["sgu_w"][0]
    sgu_bt = P["sgu_b"][0].T
    gq, gkv = P["mla_gq"], P["mla_gkv"]
    gnorm = P["hg_gnorm"]

    z0 = _matmul(x, w_in, name="in_proj_e", M=T, N=1664, K=D, tn=1664)[0]
    q, k, v = _mla_prep(z0, gq, gkv, wq, wk, wv, rc, rs1, rs2)
    if exchange:
        ids = _mesh_ids()
        placed = [_place_shard(b, ids, name=f"place_shard_{l}") for l, b in enumerate(bufs)]
        a_out, lse, wg0, wg1 = _flash_fwd(q, k, v, plan=_plan_gather_ici(placed))
    else:
        a_out, lse = _flash_fwd(q, k, v)
        wg0, wg1 = bufs
    mix0 = _sgu_fwd(z0, a_out, P["sgu_ln_g"], P["sgu_ln_b"], sgu_w, sgu_bt)
    res = _proj_ln(mix0, w_out_e, x, row(P["ln1_g"][0]), row(P["ln1_b"][0]), name="out_proj_ln_e",
                   plan=_plan_gather_forward([wg0, wg1]) if exchange else None)
    r1, h1, h1b = res[:3]
    if exchange:
        wg0, wg1 = res[3:]
    ra0, r2, h2, h2b = _ffn_ln(h1b, wg0, h1, row(P["ln2_g"][0]), row(P["ln2_b"][0]), name="ffn_ln_0")
    z4 = _matmul(h2b, wg1, name="in_proj_o", M=T, N=4 * D, K=D, b_spec=blk(lambda i, j, k: (j, 2, 0)),
                 out_shape=jax.ShapeDtypeStruct((4, T, D), F32),
                 o_spec=pl.BlockSpec((None, min(MM_ROWS, T), D), lambda i, j, k: (j, i, 0)))[0]
    y1, o_raw, states = _hgrn_fwd(z4, P["hg_lb"], gnorm)
    r3, h3, h3b = _proj_ln(y1, wg1, h2, row(P["ln1_g"][1]), row(P["ln1_b"][1]), name="out_proj_ln_o", w_rowblk=12)
    ra1, r4, h4, _ = _ffn_ln(h3b, wg1, h3, row(P["ln2_g"][1]), row(P["ln2_b"][1]), name="ffn_ln_1")
    dy, loss_parts = _loss_dy(h4, tgt)

    gs = {}
    ln1_g, ln1_b, ln2_g, ln2_b = [None, None], [None, None], [None, None], [None, None]

    def ffn_bwd(l, dh, r_out, ra, h_mid_b, g2, wg, rows, plan=None):
        dr, dr_b, dg, db = _ln_bwd(dh, r_out, row(g2), name=f"ln2_bwd_{l}")
        ln2_g[l], ln2_b[l] = dg.sum(0), db.sum(0)
        da, *extra = _matmul(dr_b, wg, tb=True, mul=ra, out_dtype=BF16, name=f"ffn_da_{l}", M=T, N=4 * D, K=D,
                             b_spec=blk(lambda i, j, k: (j, 1, 0)), plan=plan)
        gbuf = _matmul(ra, dr_b, ta=True, a_sq=True, name=f"ffn_dw2_{l}", M=4 * D, N=D, K=T, tm=1024, tk=DW_TOKENS,
                       out_shape=jax.ShapeDtypeStruct((4, rows, D), BF16), o_spec=blk(lambda i, j, k: (i, 1, 0)))[0]
        gbuf = _matmul(h_mid_b, da, ta=True, name=f"ffn_dw1_{l}", M=D, N=4 * D, K=T, tm=1024, tk=DW_TOKENS, into=gbuf,
                       out_shape=jax.ShapeDtypeStruct((4, rows, D), BF16), o_spec=blk(lambda i, j, k: (j, 0, 0)))[0]
        dh_mid = _matmul(da, wg, tb=True, add=dr, add_scale=ALPHA, name=f"ffn_dh_{l}", M=T, N=D, K=4 * D,
                         b_spec=blk(lambda i, j, k: (k, 0, 0)))[0]
        return dh_mid, gbuf, extra

    dh3, g1, _ = ffn_bwd(1, dy, r4, ra1, h3b, P["ln2_g"][1], wg1, ROWS_L1)
    dr3, dr3_b, dg, db = _ln_bwd(dh3, r3, row(P["ln1_g"][1]), name="ln1_bwd_1")
    ln1_g[1], ln1_b[1] = dg.sum(0), db.sum(0)
    g1_sds = jax.ShapeDtypeStruct((4, ROWS_L1, D), BF16)
    g1 = _matmul(y1, dr3_b, ta=True, name="dw_out_o", M=D, N=D, K=T, tm=256, tk=DW_TOKENS, into=g1, out_shape=g1_sds,
                 o_spec=pl.BlockSpec((None, 256, D), lambda i, j, k: (i, 12, 0)))[0]
    dmix1 = _matmul(dr3_b, wg1, tb=True, name="dmix_o", M=T, N=D, K=D, b_spec=_rows4_spec(12, 3), b_merge=(D, D))[0]
    dz4, dlb, dgn = _hgrn_bwd(z4, o_raw, dmix1, states, P["hg_lb"], gnorm)
    g1 = _matmul(h2b, dz4, ta=True, name="dw_in_o", M=D, N=4 * D, K=T, tm=1024, tk=DW_TOKENS, into=g1, out_shape=g1_sds,
                 b_spec=pl.BlockSpec((None, min(DW_TOKENS, T), D), lambda i, j, k: (j, k, 0)),
                 o_spec=blk(lambda i, j, k: (j, 2, 0)))[0]
    dh2 = _matmul(dz4, wg1, tb=True, add=dr3, add_scale=ALPHA, name="dh_in_o", M=T, N=D, K=4 * D,
                  a_spec=pl.BlockSpec((None, min(MM_ROWS, T), D), lambda i, j, k: (k, i, 0)),
                  b_spec=blk(lambda i, j, k: (k, 2, 0)))[0]
    d_lb1 = dlb.sum(0)
    gs["hg_lb"] = jnp.stack([-d_lb1, d_lb1])
    gs["hg_gnorm"] = dgn.sum(0)[None]

    dh1, g0, swapped1 = ffn_bwd(0, dh2, r2, ra0, h1b, P["ln2_g"][0], wg0, ROWS_L0,
                                plan=_plan_pair_swap(g1) if exchange else None)
    dr1, dr1_b, dg, db = _ln_bwd(dh1, r1, row(P["ln1_g"][0]), name="ln1_bwd_0")
    ln1_g[0], ln1_b[0] = dg.sum(0), db.sum(0)
    godd = {"w_out_e": _matmul(mix0, dr1_b, ta=True, name="dw_out_e", M=D, N=D, K=T, tm=1024, tk=DW_TOKENS)[0]}
    dmix0, *swapped0 = _matmul(dr1_b, w_out_e, tb=True, name="dmix_e", M=T, N=D, K=D,
                               plan=_plan_pair_swap(g0) if exchange else None)
    delta, do_b = _attn_delta(dmix0, a_out)
    if exchange:
        pair1 = _add_pairs(g1, swapped1[0], ids, name="grad_pair_add_1")
        pair0 = _add_pairs(g0, swapped0[0], ids, name="grad_pair_add_0")
        dq4, dk, dv, parts0, parts1 = _flash_bwd(
            q, k, v, do_b, lse, delta, plan=_join_plans([_plan_chip_scatter(pair0), _plan_chip_scatter(pair1)]))
        half0 = _sum_chips(pair0, parts0, ids, name="grad_chip_sum_0")
        half1 = _sum_chips(pair1, parts1, ids, name="grad_chip_sum_1")
        dc, dkr, dwq, dwk, dwv, dgq, dgkv, g0, g1 = _mla_bwd(
            z0, dq4, dk, dv, gq, gkv, wq, wk, wv, rc, rs1, rs2,
            plan=_join_plans([_plan_pair_gather(half0), _plan_pair_gather(half1)]))
        g0, g1 = g0.reshape(ROWS_L0, D), g1.reshape(ROWS_L1, D)
    else:
        dq4, dk, dv = _flash_bwd(q, k, v, do_b, lse, delta)
        dc, dkr, dwq, dwk, dwv, dgq, dgkv = _mla_bwd(z0, dq4, dk, dv, gq, gkv, wq, wk, wv, rc, rs1, rs2)
    dz0, dsw, dsb, dslg, dslb = _sgu_bwd(z0, dmix0, dc, dkr, P["sgu_ln_g"], P["sgu_ln_b"], sgu_w, sgu_bt)
    gs["mla_gq"], gs["mla_gkv"] = dgq.sum(0)[None], dgkv.sum(0)[None]
    gs["sgu_ln_g"], gs["sgu_ln_b"] = dslg.sum(0)[None], dslb.sum(0)[None]
    gs["sgu_w"], gs["sgu_b"] = dsw[None], dsb[:, :SGU_G].T[None]
    gs["ln1_g"], gs["ln1_b"] = jnp.stack(ln1_g), jnp.stack(ln1_b)
    gs["ln2_g"], gs["ln2_b"] = jnp.stack(ln2_g), jnp.stack(ln2_b)
    small_vec = _pack_small(gs, (0.5 / D) * jnp.sum(loss_parts))
    dw_in, *small_others = _matmul(x, dz0, ta=True, name="dw_in_e", M=D, N=1664, K=T, tm=1024, tn=1664,
                                   tk=DW_TOKENS // 2, plan=_plan_exchange_all(small_vec) if exchange else None)
    godd["w_in_e"] = jnp.concatenate([dw_in[:, :512], dw_in[:, 1536:1568], dw_in[:, 512:1536]], axis=1)
    grad_x = _matmul(dz0, w_in, tb=True, add=dr1, add_scale=ALPHA, name="dx", M=T, N=D, K=1664, tk=1664)[0]

    godd["w_qb"] = dwq.reshape(256, HEADS, 128)[:, :, :NOPE + ROPE].reshape(256, HEADS * (NOPE + ROPE))
    godd["w_kvb"] = jnp.concatenate([dwk.reshape(256, HEADS, 128)[:, :, :NOPE], dwv.reshape(256, HEADS, VDIM)],
                                    axis=2).reshape(256, HEADS * (NOPE + VDIM))
    small = (small_vec, small_others[0]) if exchange else gs
    return loss_parts, grad_x, g0, g1, godd, small


WEIGHTS = ['w_in_e', 'mla_gq', 'mla_gkv', 'w_qb', 'w_kvb', 'sgu_ln_g', 'sgu_ln_b', 'sgu_w', 'sgu_b', 'w_out_e',
           'w_in_o', 'hg_lb', 'hg_gnorm', 'w_out_o', 'ln1_g', 'ln1_b', 'w_ff1', 'w_ff2', 'ln2_g', 'ln2_b']


def kernel(x, positions, w_in_e, mla_gq, mla_gkv, w_qb, w_kvb, sgu_ln_g, sgu_ln_b, sgu_w, sgu_b, w_out_e, w_in_o, hg_lb, hg_gnorm, w_out_o, ln1_g, ln1_b, w_ff1, w_ff2, ln2_g, ln2_b, loss_target, m_w_in_e, m_mla_gq, m_mla_gkv, m_w_qb, m_w_kvb, m_sgu_ln_g, m_sgu_ln_b, m_sgu_w, m_sgu_b, m_w_out_e, m_w_in_o, m_hg_lb, m_hg_gnorm, m_w_out_o, m_ln1_g, m_ln1_b, m_w_ff1, m_w_ff2, m_ln2_g, m_ln2_b, v_w_in_e, v_mla_gq, v_mla_gkv, v_w_qb, v_w_kvb, v_sgu_ln_g, v_sgu_ln_b, v_sgu_w, v_sgu_b, v_w_out_e, v_w_in_o, v_hg_lb, v_hg_gnorm, v_w_out_o, v_ln1_g, v_ln1_b, v_w_ff1, v_w_ff2, v_ln2_g, v_ln2_b):
    args = dict(locals())
    w = {n: args[n] for n in WEIGHTS}
    m = {n: args["m_" + n] for n in WEIGHTS}
    v = {n: args["v_" + n] for n in WEIGHTS}
    cx, cy, cc = _mesh_pos()
    chip = 2 * cx + cy

    odd_shard = _odd_rows({"w_out_e": w_out_e[0], "w_in_e": w_in_e[0], "w_qb": w_qb[0], "w_kvb": w_kvb[0]}, BF16,
                          gnorm=hg_gnorm)
    ids = _mesh_ids()
    gathered = _run_plan(_plan_gather_ici([_place_shard(odd_shard, ids, name="place_shard_odd")]), name="odd_gather")[0]
    gathered = _run_plan(_plan_gather_forward([gathered]), name="odd_gather_forward")[0]
    per_chip = [_odd_unrows(gathered[j], with_gnorm=True) for j in range(4)]
    odd = {"w_out_e": jnp.concatenate([p["w_out_e"] for p in per_chip], axis=0)}
    for n in ("w_in_e", "w_qb", "w_kvb"):
        odd[n] = jnp.concatenate([p[n] for p in per_chip], axis=1)
    small = {n: w[n] for n, _ in SMALL if n != "hg_gnorm"}
    small["hg_gnorm"] = jnp.concatenate([p["hg_gnorm"] for p in per_chip], axis=1)
    rows_l0 = jnp.concatenate([w_ff1[0], w_ff2[0]], axis=0).astype(BF16)
    rows_l1 = jnp.concatenate([w_ff1[1], w_ff2[1], w_in_o[0], w_out_o[0]], axis=0).astype(BF16)

    loss_parts, grad_x, g_l0, g_l1, godd, (small_vec, small_others) = _local_step(
        x[0], positions[0], loss_target[0], odd, (rows_l0, rows_l1), small, True)

    by_chip = [_odd_rows({"w_out_e": jnp.split(godd["w_out_e"], 4, axis=0)[j],
                          **{n: jnp.split(godd[n], 4, axis=1)[j] for n in ("w_in_e", "w_qb", "w_kvb")}}, BF16)
               for j in range(4)]
    godd_buf = jnp.stack(by_chip)
    theirs = _run_plan(_plan_pair_swap(godd_buf), name="odd_pair_swap")[0]
    pair = _add_pairs(godd_buf, theirs, ids, name="odd_pair_add")
    parts = _run_plan(_plan_chip_scatter(pair), name="odd_chip_scatter")[0]
    g_odd = _run_plan(_plan_pair_gather(_sum_chips(pair, parts, ids, name="odd_chip_sum")), name="odd_pair_gather")[0]
    g_odd = _odd_unrows(g_odd.reshape(ROWS_ODD, 1024))

    small_sum = _sum_devices(small_vec, small_others, ids)
    loss = small_sum[-1, -1]
    g_small = _unpack_small(small_sum)
    grads = {n: g_small[n] for n, _ in SMALL if n != "hg_gnorm"}
    grads["hg_gnorm"] = lax.dynamic_slice_in_dim(g_small["hg_gnorm"], chip * 256, 256, axis=1)

    delta, new_m, new_v = {}, {}, {}
    for n, bufs_, row0 in (("w_ff1", [g_l0, g_l1], 0), ("w_ff2", [g_l0, g_l1], 1024), ("w_in_o", [g_l1], 2048),
                           ("w_out_o", [g_l1], 3072)):
        grads[n], delta[n], new_m[n], new_v[n] = _adamw_rows(w[n], m[n], v[n], bufs_, row0, name=f"adamw_{n}")
    for n, _ in ODD_PARTS:
        grads[n] = g_odd[n][None]
        d_, m_, v_ = _adamw(w[n][0], g_odd[n], m[n][0], v[n][0], name=f"adamw_{n}")
        delta[n], new_m[n], new_v[n] = d_[None], m_[None], v_[None]
    rest = [n for n in WEIGHTS if n not in delta]

    def pack_rest(d):
        flat = jnp.concatenate([d[n].reshape(-1) for n in rest])
        return jnp.pad(flat, (0, SMALL_ROWS * 1024 - flat.shape[0])).reshape(SMALL_ROWS, 1024)

    outs = _adamw(pack_rest(w), pack_rest(grads), pack_rest(m), pack_rest(v), name="adamw_small")
    for dst, packed in zip((delta, new_m, new_v), outs):
        flat, off = packed.reshape(-1), 0
        for n in rest:
            size = math.prod(w[n].shape)
            dst[n] = flat[off:off + size].reshape(w[n].shape)
            off += size

    return (loss, grad_x[None], *[grads[n] for n in WEIGHTS], *[delta[n] for n in WEIGHTS],
            *[new_m[n] for n in WEIGHTS], *[new_v[n] for n in WEIGHTS])
```

```python
import functools
import math

import jax
import jax.numpy as jnp
from jax import lax
from jax.experimental import pallas as pl
from jax.experimental.pallas import tpu as pltpu

F32 = jnp.float32
BF16 = jnp.bfloat16
MESH_IDS = pl.DeviceIdType.MESH

D = 1024
DEPTH = 2
HEADS = 8
NOPE, ROPE, VDIM = 64, 32, 64
QK_SCALE = (NOPE + ROPE) ** -0.5
ROPE_BASE = 10000.0
SGU_G, SGU_C = 4, 128
HG_CHUNK = 64
HG_HEADS_PER_STEP = 4
ALPHA = (2 * DEPTH) ** 0.25
EPS = 1e-5
LR, B1, B2, ADAM_EPS, WD, STEP = 0.001, 0.9, 0.999, 1e-08, 0.01, 10
GELU_C = math.sqrt(2.0 / math.pi)
GELU_A = 0.044715
HI = lax.Precision.HIGHEST
MB = 1024 * 1024
ROW_BLOCK = 512

NT_DIMS = (((1,), (1,)), ((), ()))
TN_DIMS = (((0,), (0,)), ((), ()))

SHARDED = (
    ("w_in_e", (1, 1024, 392), 2), ("w_qb", (1, 256, 192), 2), ("w_kvb", (1, 256, 256), 2),
    ("w_out_e", (1, 256, 1024), 1), ("w_in_o", (1, 1024, 1024), 2), ("w_out_o", (1, 256, 1024), 1),
    ("w_ff1", (2, 1024, 1024), 2), ("w_ff2", (2, 1024, 1024), 1), ("hg_gnorm", (1, 256), 1),
)
PACK_ROWS = 6144
HALF_ROWS = PACK_ROWS // 2
SMALL = (("mla_gq", (1, 256)), ("mla_gkv", (1, 256)), ("sgu_ln_g", (1, 512)), ("sgu_ln_b", (1, 512)),
         ("sgu_w", (1, 4, 128, 128)), ("sgu_b", (1, 4, 128)), ("hg_lb", (2, 1024)), ("hg_gnorm", (1, 1024)),
         ("ln1_g", (2, 1024)), ("ln1_b", (2, 1024)), ("ln2_g", (2, 1024)), ("ln2_b", (2, 1024)))
SMALL_ROWS = 80


def _params(vmem_mb, n_axes=0):
    kw = dict(vmem_limit_bytes=vmem_mb * MB)
    if n_axes:
        kw["dimension_semantics"] = ("arbitrary",) * n_axes
    return pltpu.CompilerParams(**kw)


_ANY = pl.BlockSpec(memory_space=pltpu.HBM)


def _mesh_pos():
    return lax.axis_index("x"), lax.axis_index("y"), lax.axis_index("c")


class _Plan:
    def __init__(self, ins, outs, n_remote, n_local, start, wait, aliases=None):
        self.ins, self.outs, self.n_remote, self.n_local = list(ins), list(outs), n_remote, n_local
        self.start, self.wait, self.aliases = start, wait, dict(aliases or {})


def _join_plans(plans):
    ins, outs, aliases, parts = [], [], {}, []
    nr = nl = 0
    for p in plans:
        parts.append((p, len(ins), len(outs), nr, nl))
        aliases.update({len(ins) + i: len(outs) + o for i, o in p.aliases.items()})
        ins += p.ins
        outs += p.outs
        nr += p.n_remote
        nl += p.n_local

    def run(which):
        def go(in_refs, out_refs, send, recv, loc):
            for p, i0, o0, r0, l0 in parts:
                getattr(p, which)(in_refs[i0:i0 + len(p.ins)], out_refs[o0:o0 + len(p.outs)],
                                  lambda i, r0=r0: send(r0 + i), lambda i, r0=r0: recv(r0 + i),
                                  lambda i, l0=l0: loc(l0 + i))
        return go

    return _Plan(ins, outs, nr, nl, run("start"), run("wait"), aliases)


def _plan_io(plan, n_in, n_out):
    if plan is None:
        return [], [], [], [], {}
    sems = [pltpu.SemaphoreType.DMA((max(plan.n_remote, 1),)), pltpu.SemaphoreType.DMA((max(plan.n_remote, 1),)),
            pltpu.SemaphoreType.DMA((max(plan.n_local, 1),))]
    aliases = {n_in + i: n_out + o for i, o in plan.aliases.items()}
    return plan.ins, [_ANY] * len(plan.outs), plan.outs, sems, aliases


def _split_refs(refs, n_in, n_out, n_scr, plan):
    p_in, p_out = (len(plan.ins), len(plan.outs)) if plan is not None else (0, 0)
    refs = list(refs)
    ins, refs = refs[:n_in], refs[n_in:]
    pins, refs = refs[:p_in], refs[p_in:]
    outs, refs = refs[:n_out], refs[n_out:]
    pouts, refs = refs[:p_out], refs[p_out:]
    scr, psem = refs[:n_scr], refs[n_scr:]
    psem = tuple((lambda i, s=s: s.at[i]) for s in psem)
    return ins, outs, scr, (pins, pouts, psem)


def _grid_edge(grid, last):
    cond = None
    for ax, n in enumerate(grid):
        c = pl.program_id(ax) == (n - 1 if last else 0)
        cond = c if cond is None else cond & c
    return cond


def _plan_start(plan, pctx, grid):
    if plan is not None:
        pins, pouts, psem = pctx
        pl.when(_grid_edge(grid, False))(lambda: plan.start(pins, pouts, *psem))


def _plan_wait(plan, pctx, grid):
    if plan is not None:
        pins, pouts, psem = pctx
        pl.when(_grid_edge(grid, True))(lambda: plan.wait(pins, pouts, *psem))


def _run_plan(plan, *, name):
    def body(*refs):
        _, _, _, (pins, pouts, psem) = _split_refs(refs, 0, 0, 0, plan)
        plan.start(pins, pouts, *psem)
        plan.wait(pins, pouts, *psem)

    p_in, p_ospec, p_oshape, p_scr, p_alias = _plan_io(plan, 0, 0)
    return pl.pallas_call(body, name=name, in_specs=[_ANY] * len(p_in), out_specs=p_ospec, out_shape=p_oshape,
                          scratch_shapes=p_scr, input_output_aliases=p_alias)(*p_in)


def _fold8(x):
    return x.reshape(x.shape[0] // 8, 8, x.shape[1]).sum(axis=0)


def _ln_stats(r):
    mu = jnp.mean(r, -1, keepdims=True)
    xc = r - mu
    rstd = lax.rsqrt(jnp.mean(xc * xc, -1, keepdims=True) + EPS)
    return xc * rstd, rstd


def _sigmoid(x):
    return 1.0 / (1.0 + jnp.exp(-x))


def _gelu(x):
    return 0.5 * x * (1.0 + jnp.tanh(GELU_C * (x + GELU_A * x * x * x)))


def _gelu_grad(x):
    t = jnp.tanh(GELU_C * (x + GELU_A * x * x * x))
    return 0.5 * (1.0 + t) + 0.5 * x * (1.0 - t * t) * GELU_C * (1.0 + 3.0 * GELU_A * x * x)


MM_ROWS = 1024
DW_TOKENS = 2048


def _matmul(a, b, *, name, M, N, K, ta=False, tb=False, out_dtype=F32, tm=MM_ROWS, tn=1024, tk=1024,
            a_spec=None, b_spec=None, b_merge=None, out_shape=None, o_spec=None, into=None,
            a_sq=False, mul=None, add=None, add_scale=1.0, plan=None):
    tm, tn, tk = min(tm, M), min(tn, N), min(tk, K)
    assert M % tm == 0 and N % tn == 0 and K % tk == 0
    grid = (M // tm, N // tn, K // tk)
    nk = grid[2]
    if a_spec is None:
        a_spec = pl.BlockSpec((tk, tm), lambda i, j, k: (k, i)) if ta else pl.BlockSpec((tm, tk), lambda i, j, k: (i, k))
    if b_spec is None:
        b_spec = pl.BlockSpec((tn, tk), lambda i, j, k: (j, k)) if tb else pl.BlockSpec((tk, tn), lambda i, j, k: (k, j))
    if o_spec is None:
        o_spec = pl.BlockSpec((tm, tn), lambda i, j, k: (i, j))
        out_shape = jax.ShapeDtypeStruct((M, N), out_dtype)
    e_spec = pl.BlockSpec((tm, tn), lambda i, j, k: (i, j))
    dims = (((0 if ta else 1,), (1 if tb else 0,)), ((), ()))
    extra = [e for e in (mul, add, into) if e is not None]
    n_in = 2 + len(extra)

    def body(*refs):
        ins, outs, scr, pctx = _split_refs(refs, n_in, 1, 1 if nk > 1 else 0, plan)
        a_ref, b_ref = ins[0], ins[1]
        rest = list(ins[2:])
        mul_ref = rest.pop(0) if mul is not None else None
        add_ref = rest.pop(0) if add is not None else None
        o_ref = outs[0]
        _plan_start(plan, pctx, grid)
        av = a_ref[...].astype(BF16)
        if a_sq:
            av = av * av
        bv = b_ref[...]
        if b_merge is not None:
            bv = bv.reshape(b_merge)
        p = lax.dot_general(av, bv, dims, preferred_element_type=F32)

        def finish(r):
            if mul_ref is not None:
                r = r * (2.0 * mul_ref[...].astype(F32))
            if add_ref is not None:
                r = r + add_scale * add_ref[...]
            o_ref[...] = r.astype(o_ref.dtype)

        if nk == 1:
            finish(p)
        else:
            acc_ref = scr[0]
            k = pl.program_id(2)

            @pl.when(k == 0)
            def _():
                acc_ref[...] = p

            @pl.when(k > 0)
            def _():
                acc_ref[...] += p

            @pl.when(k == nk - 1)
            def _():
                finish(acc_ref[...])

        _plan_wait(plan, pctx, grid)

    p_in, p_ospec, p_oshape, p_scr, p_alias = _plan_io(plan, n_in, 1)
    aliases = dict(p_alias)
    if into is not None:
        aliases[n_in - 1] = 0
    return pl.pallas_call(
        body, name=name, grid=grid,
        in_specs=[a_spec, b_spec] + [e_spec] * (len(extra) - (into is not None)) + [_ANY] * (into is not None)
        + [_ANY] * len(p_in),
        out_specs=[o_spec] + p_ospec, out_shape=[out_shape] + p_oshape,
        scratch_shapes=([pltpu.VMEM((tm, tn), F32)] if nk > 1 else []) + p_scr,
        input_output_aliases=aliases, compiler_params=_params(48, 3),
    )(a, b, *extra, *p_in)


def _rows4_spec(rowblk, n_axes):
    return pl.BlockSpec((4, 256, D), lambda *_: (0, rowblk, 0))


def _proj_ln(a_b, w, h_prev, g, b, *, name, w_rowblk=None, plan=None):
    T = a_b.shape[0]
    tm = min(ROW_BLOCK, T)
    grid = (T // tm,)
    row = pl.BlockSpec((tm, D), lambda i: (i, 0))
    vec = pl.BlockSpec((1, D), lambda i: (0, 0))
    w_spec = pl.BlockSpec((D, D), lambda i: (0, 0)) if w_rowblk is None else _rows4_spec(w_rowblk, 1)

    def body(*refs):
        (a_ref, w_ref, h_ref, g_ref, b_ref), (r_ref, ho_ref, hb_ref), _, pctx = _split_refs(refs, 5, 3, 0, plan)
        _plan_start(plan, pctx, grid)
        mix = jnp.dot(a_ref[...], w_ref[...].reshape(D, D), preferred_element_type=F32)
        r = ALPHA * h_ref[...] + mix
        xhat, _ = _ln_stats(r)
        y = xhat * g_ref[...] + b_ref[...]
        r_ref[...] = r
        ho_ref[...] = y
        hb_ref[...] = y.astype(BF16)
        _plan_wait(plan, pctx, grid)

    p_in, p_ospec, p_oshape, p_scr, p_alias = _plan_io(plan, 5, 3)
    return pl.pallas_call(
        body, name=name, grid=grid,
        in_specs=[row, w_spec, row, vec, vec] + [_ANY] * len(p_in),
        out_specs=[row, row, row] + p_ospec,
        out_shape=[jax.ShapeDtypeStruct((T, D), F32), jax.ShapeDtypeStruct((T, D), F32),
                   jax.ShapeDtypeStruct((T, D), BF16)] + p_oshape,
        scratch_shapes=p_scr, input_output_aliases=p_alias, compiler_params=_params(40, 1),
    )(a_b, w, h_prev, g, b, *p_in)


def _ffn_ln(h_b, wbuf, h, g, b, *, name, plan=None):
    T = h_b.shape[0]
    tm, tf = min(ROW_BLOCK, T), 1024
    nf = 4
    F = nf * tf
    grid = (T // tm, nf)
    row = pl.BlockSpec((tm, D), lambda i, j: (i, 0))
    vec = pl.BlockSpec((1, D), lambda i, j: (0, 0))

    def body(*refs):
        ((hb_ref, w1_ref, w2_ref, h_ref, g_ref, b_ref), (ra_ref, r_ref, ho_ref, hbo_ref), (acc_ref,),
         pctx) = _split_refs(refs, 6, 4, 1, plan)
        _plan_start(plan, pctx, grid)
        j = pl.program_id(1)
        a = jnp.dot(hb_ref[...], w1_ref[...], preferred_element_type=F32)
        ra = jnp.maximum(a, 0.0)
        ra_ref[...] = ra.astype(BF16)
        p = jnp.dot((ra * ra).astype(BF16), w2_ref[...], preferred_element_type=F32)

        @pl.when(j == 0)
        def _():
            acc_ref[...] = p

        @pl.when(j > 0)
        def _():
            acc_ref[...] += p

        @pl.when(j == nf - 1)
        def _():
            r = ALPHA * h_ref[...] + acc_ref[...]
            xhat, _ = _ln_stats(r)
            y = xhat * g_ref[...] + b_ref[...]
            r_ref[...] = r
            ho_ref[...] = y
            hbo_ref[...] = y.astype(BF16)

        _plan_wait(plan, pctx, grid)

    p_in, p_ospec, p_oshape, p_scr, p_alias = _plan_io(plan, 6, 4)
    return pl.pallas_call(
        body, name=name, grid=grid,
        in_specs=[row, pl.BlockSpec((None, D, tf), lambda i, j: (j, 0, 0)),
                  pl.BlockSpec((None, tf, D), lambda i, j: (j, 1, 0)), row, vec, vec] + [_ANY] * len(p_in),
        out_specs=[pl.BlockSpec((tm, tf), lambda i, j: (i, j)), row, row, row] + p_ospec,
        out_shape=[jax.ShapeDtypeStruct((T, F), BF16), jax.ShapeDtypeStruct((T, D), F32),
                   jax.ShapeDtypeStruct((T, D), F32), jax.ShapeDtypeStruct((T, D), BF16)] + p_oshape,
        scratch_shapes=[pltpu.VMEM((tm, D), F32)] + p_scr,
        input_output_aliases=p_alias, compiler_params=_params(48, 2),
    )(h_b, wbuf, wbuf, h, g, b, *p_in)


def _loss_dy(y, tgt):
    T = y.shape[0]
    tm = min(ROW_BLOCK, T)
    row = pl.BlockSpec((tm, D), lambda i: (i, 0))

    def body(y_ref, t_ref, dy_ref, ls_ref):
        e = y_ref[...] - t_ref[...]
        dy_ref[...] = e * (1.0 / D)

        @pl.when(pl.program_id(0) == 0)
        def _():
            ls_ref[...] = jnp.zeros_like(ls_ref)

        ls_ref[...] += _fold8(e * e)

    return pl.pallas_call(
        body, name="loss_dy", grid=(T // tm,), in_specs=[row, row],
        out_specs=[row, pl.BlockSpec((8, D), lambda i: (0, 0))],
        out_shape=[jax.ShapeDtypeStruct((T, D), F32), jax.ShapeDtypeStruct((8, D), F32)],
        compiler_params=_params(32, 1),
    )(y, tgt)


def _ln_bwd(dy, r, g, *, name):
    T = dy.shape[0]
    tm = min(ROW_BLOCK, T)
    row = pl.BlockSpec((tm, D), lambda i: (i, 0))
    acc = pl.BlockSpec((8, D), lambda i: (0, 0))

    def body(dy_ref, r_ref, g_ref, dr_ref, drb_ref, dg_ref, db_ref):
        @pl.when(pl.program_id(0) == 0)
        def _():
            dg_ref[...] = jnp.zeros_like(dg_ref)
            db_ref[...] = jnp.zeros_like(db_ref)

        dy_ = dy_ref[...]
        xhat, rstd = _ln_stats(r_ref[...])
        dxh = dy_ * g_ref[...]
        m1 = jnp.mean(dxh, -1, keepdims=True)
        m2 = jnp.mean(dxh * xhat, -1, keepdims=True)
        dr = rstd * (dxh - m1 - xhat * m2)
        dr_ref[...] = dr
        drb_ref[...] = dr.astype(BF16)
        dg_ref[...] += _fold8(dy_ * xhat)
        db_ref[...] += _fold8(dy_)

    return pl.pallas_call(
        body, name=name, grid=(T // tm,),
        in_specs=[row, row, pl.BlockSpec((1, D), lambda i: (0, 0))],
        out_specs=[row, row, acc, acc],
        out_shape=[jax.ShapeDtypeStruct((T, D), F32), jax.ShapeDtypeStruct((T, D), BF16),
                   jax.ShapeDtypeStruct((8, D), F32), jax.ShapeDtypeStruct((8, D), F32)],
        compiler_params=_params(40, 1),
    )(dy, r, g)


def _rope(x, c, s1, s2):
    return x * c + pltpu.roll(x, 112, 1) * s1 + pltpu.roll(x, 16, 1) * s2


def _rope_t(dy, c, s1, s2):
    return dy * c + pltpu.roll(dy * s1, 16, 1) + pltpu.roll(dy * s2, 112, 1)


def _rms(x, g):
    rstd = lax.rsqrt(jnp.mean(x * x, -1, keepdims=True) + EPS)
    xhat = x * rstd
    return xhat * g, xhat, rstd


def _mla_prep(z0, gq, gkv, wq, wk, wv, rc, rs1, rs2):
    T = z0.shape[0]
    tm = min(ROW_BLOCK, T)
    HW = HEADS * 128

    def body(cq_ref, ckv_ref, kr_ref, gq_ref, gkv_ref, wq_ref, wk_ref, wv_ref, c_ref, s1_ref, s2_ref,
             q_ref, k_ref, v_ref):
        nq = _rms(cq_ref[...], gq_ref[...])[0].astype(BF16)
        nkv = _rms(ckv_ref[...], gkv_ref[...])[0].astype(BF16)
        q = jnp.dot(nq, wq_ref[...], preferred_element_type=F32)
        k = jnp.dot(nkv, wk_ref[...], preferred_element_type=F32)
        v = jnp.dot(nkv, wv_ref[...], preferred_element_type=F32)
        c, s1, s2 = c_ref[...], s1_ref[...], s2_ref[...]
        kr = _rope(pltpu.roll(kr_ref[...], 64, 1), c, s1, s2)
        for h in range(HEADS):
            sl = slice(h * 128, (h + 1) * 128)
            q_ref[:, sl] = (_rope(q[:, sl], c, s1, s2) * QK_SCALE).astype(BF16)
            k_ref[:, sl] = (k[:, sl] + kr).astype(BF16)
        v_ref[...] = v.astype(BF16)

    full = lambda shape: pl.BlockSpec(shape, lambda i: (0, 0))
    tab = pl.BlockSpec((tm, 128), lambda i: (i, 0))
    return pl.pallas_call(
        body, name="mla_prep", grid=(T // tm,),
        in_specs=[pl.BlockSpec((tm, 256), lambda i: (i, 0)), pl.BlockSpec((tm, 256), lambda i: (i, 1)),
                  pl.BlockSpec((tm, 128), lambda i: (i, 12)), full((1, 256)), full((1, 256)),
                  full((256, HW)), full((256, HW)), full((256, 512)), tab, tab, tab],
        out_specs=[pl.BlockSpec((tm, HW), lambda i: (i, 0)), pl.BlockSpec((tm, HW), lambda i: (i, 0)),
                   pl.BlockSpec((tm, 512), lambda i: (i, 0))],
        out_shape=[jax.ShapeDtypeStruct((T, HW), BF16), jax.ShapeDtypeStruct((T, HW), BF16),
                   jax.ShapeDtypeStruct((T, 512), BF16)],
        compiler_params=_params(40, 1),
    )(z0, z0, z0, gq, gkv, wq, wk, wv, rc, rs1, rs2)


def _flash_fwd(q, k, v, plan=None):
    T = q.shape[0]
    bq = min(2 * ROW_BLOCK, T)
    nq = T // bq
    grid = (4, nq, nq)

    def body(*refs):
        (q_ref, k_ref, v_ref), (o_ref, lse_ref), (m_sc, acc_sc), pctx = _split_refs(refs, 3, 2, 2, plan)
        _plan_start(plan, pctx, grid)
        i, j = pl.program_id(1), pl.program_id(2)
        first = lax.broadcasted_iota(jnp.int32, (bq, 128), 1) < 64

        @pl.when(j == 0)
        def _():
            m_sc[...] = jnp.full_like(m_sc, -jnp.inf)
            acc_sc[...] = jnp.zeros_like(acc_sc)

        def step(masked):
            vp = v_ref[...]
            for h in range(2):
                sl = slice(h * 128, (h + 1) * 128)
                s = lax.dot_general(q_ref[:, sl], k_ref[:, sl], NT_DIMS, preferred_element_type=F32)
                if masked:
                    rows = lax.broadcasted_iota(jnp.int32, (bq, bq), 0)
                    cols = lax.broadcasted_iota(jnp.int32, (bq, bq), 1)
                    s = jnp.where(cols <= rows, s, -jnp.inf)
                m_prev = m_sc[h, :, 0:1]
                m_new = jnp.maximum(m_prev, jnp.max(s, axis=1, keepdims=True))
                alpha = jnp.exp(m_prev - m_new)
                p = jnp.exp(s - m_new).astype(BF16)
                mine = first if h == 0 else jnp.logical_not(first)
                vh = jnp.where(mine, vp, jnp.ones_like(vp))
                acc_sc[h] = acc_sc[h] * alpha + jnp.dot(p, vh, preferred_element_type=F32)
                m_sc[h] = jnp.broadcast_to(m_new, (bq, 128))

        @pl.when(j < i)
        def _():
            step(False)

        @pl.when(j == i)
        def _():
            step(True)
            a0, a1 = acc_sc[0], acc_sc[1]
            l0, l1 = pltpu.roll(a0, 64, 1), pltpu.roll(a1, 64, 1)
            o_ref[...] = jnp.where(first, a0 / l0, a1 / l1).astype(BF16)
            lse_ref[...] = jnp.where(first, m_sc[0] + jnp.log(l0), m_sc[1] + jnp.log(l1))

        _plan_wait(plan, pctx, grid)

    kv = lambda hp, i, j: (jnp.minimum(i, j), hp)
    p_in, p_ospec, p_oshape, p_scr, p_alias = _plan_io(plan, 3, 2)
    return pl.pallas_call(
        body, name="flash_fwd", grid=grid,
        in_specs=[pl.BlockSpec((bq, 256), lambda hp, i, j: (i, hp)), pl.BlockSpec((bq, 256), kv),
                  pl.BlockSpec((bq, 128), kv)] + [_ANY] * len(p_in),
        out_specs=[pl.BlockSpec((bq, 128), lambda hp, i, j: (i, hp)),
                   pl.BlockSpec((bq, 128), lambda hp, i, j: (i, hp))] + p_ospec,
        out_shape=[jax.ShapeDtypeStruct((T, 512), BF16), jax.ShapeDtypeStruct((T, 512), F32)] + p_oshape,
        scratch_shapes=[pltpu.VMEM((2, bq, 128), F32), pltpu.VMEM((2, bq, 128), F32)] + p_scr,
        input_output_aliases=p_alias, compiler_params=_params(56, 3),
    )(q, k, v, *p_in)


def _attn_delta(dmix, o):
    T = o.shape[0]
    tm = min(ROW_BLOCK, T)
    blk = pl.BlockSpec((tm, 512), lambda i: (i, 0))

    def body(do_ref, o_ref, delta_ref, dob_ref):
        first = lax.broadcasted_iota(jnp.int32, (tm, 128), 1) < 64
        for hp in range(4):
            sl = slice(hp * 128, (hp + 1) * 128)
            prod = do_ref[:, sl] * o_ref[:, sl].astype(F32)
            d0 = jnp.sum(jnp.where(first, prod, 0.0), axis=1, keepdims=True)
            d1 = jnp.sum(jnp.where(first, 0.0, prod), axis=1, keepdims=True)
            delta_ref[:, sl] = jnp.where(first, d0, d1)
        dob_ref[...] = do_ref[...].astype(BF16)

    return pl.pallas_call(
        body, name="attn_delta", grid=(T // tm,), in_specs=[blk, blk], out_specs=[blk, blk],
        out_shape=[jax.ShapeDtypeStruct((T, 512), F32), jax.ShapeDtypeStruct((T, 512), BF16)],
        compiler_params=_params(32, 1),
    )(dmix, o)


def _flash_bwd(q, k, v, do_b, lse, delta, plan=None):
    T = q.shape[0]
    bq = min(2 * ROW_BLOCK, T)
    nq = T // bq
    grid = (4, nq, nq)

    def body(*refs):
        ((q_ref, k_ref, v_ref, do_ref, lse_ref, dl_ref), (dq_hbm, dk_ref, dv_ref), (dq_sc, dk_sc, dv_sc, sem),
         pctx) = _split_refs(refs, 6, 3, 4, plan)
        _plan_start(plan, pctx, grid)
        hp, j, i = pl.program_id(0), pl.program_id(1), pl.program_id(2)
        first = lax.broadcasted_iota(jnp.int32, (bq, 128), 1) < 64

        @pl.when((j == 0) & (i == 0))
        def _():
            dq_sc[...] = jnp.zeros_like(dq_sc)

        @pl.when(i == j)
        def _():
            dk_sc[...] = jnp.zeros_like(dk_sc)
            dv_sc[...] = jnp.zeros_like(dv_sc)

        def step(masked):
            vp = v_ref[...]
            do = do_ref[...]
            for h in range(2):
                sl = slice(h * 128, (h + 1) * 128)
                qh, kh = q_ref[:, sl], k_ref[:, sl]
                s = lax.dot_general(qh, kh, NT_DIMS, preferred_element_type=F32)
                p = jnp.exp(s - lse_ref[:, h * 64:h * 64 + 1])
                if masked:
                    rows = lax.broadcasted_iota(jnp.int32, (bq, bq), 0)
                    cols = lax.broadcasted_iota(jnp.int32, (bq, bq), 1)
                    p = jnp.where(cols <= rows, p, 0.0)
                mine = first if h == 0 else jnp.logical_not(first)
                do_h = jnp.where(mine, do, jnp.zeros_like(do))
                dv_sc[...] += lax.dot_general(p.astype(BF16), do_h, TN_DIMS, preferred_element_type=F32)
                dp = lax.dot_general(do_h, vp, NT_DIMS, preferred_element_type=F32)
                ds = (p * (dp - dl_ref[:, h * 64:h * 64 + 1])).astype(BF16)
                dq_sc[i, :, sl] += jnp.dot(ds, kh, preferred_element_type=F32)
                dk_sc[:, sl] += lax.dot_general(ds, qh, TN_DIMS, preferred_element_type=F32)

        @pl.when(i > j)
        def _():
            step(False)

        @pl.when(i == j)
        def _():
            step(True)

        @pl.when(i == nq - 1)
        def _():
            dk_ref[...] = dk_sc[...]
            dv_ref[...] = dv_sc[...]

        @pl.when((j == nq - 1) & (i == nq - 1))
        def _():
            cp = pltpu.make_async_copy(dq_sc, dq_hbm.at[hp], sem)
            cp.start()
            cp.wait()

        _plan_wait(plan, pctx, grid)

    qi = lambda hp, j, i: (jnp.maximum(i, j), hp)
    kj = lambda hp, j, i: (j, hp)
    p_in, p_ospec, p_oshape, p_scr, p_alias = _plan_io(plan, 6, 3)
    return pl.pallas_call(
        body, name="flash_bwd", grid=grid,
        in_specs=[pl.BlockSpec((bq, 256), qi), pl.BlockSpec((bq, 256), kj), pl.BlockSpec((bq, 128), kj),
                  pl.BlockSpec((bq, 128), qi), pl.BlockSpec((bq, 128), qi), pl.BlockSpec((bq, 128), qi)]
        + [_ANY] * len(p_in),
        out_specs=[_ANY, pl.BlockSpec((bq, 256), kj), pl.BlockSpec((bq, 128), kj)] + p_ospec,
        out_shape=[jax.ShapeDtypeStruct((4, nq, bq, 256), F32), jax.ShapeDtypeStruct((T, 1024), F32),
                   jax.ShapeDtypeStruct((T, 512), F32)] + p_oshape,
        scratch_shapes=[pltpu.VMEM((nq, bq, 256), F32), pltpu.VMEM((bq, 256), F32), pltpu.VMEM((bq, 128), F32),
                        pltpu.SemaphoreType.DMA] + p_scr,
        input_output_aliases=p_alias, compiler_params=_params(56, 3),
    )(q, k, v, do_b, lse, delta, *p_in)


def _mla_bwd(z0, dq4, dk, dv, gq, gkv, wq, wk, wv, rc, rs1, rs2, plan=None):
    T = z0.shape[0]
    tm = min(ROW_BLOCK, T)
    HW = HEADS * 128
    grid = (T // tm,)
    dq4 = dq4.reshape(4, T, 256)

    def body(*refs):
        ((cq_ref, ckv_ref, dq_ref, dk_ref, dv_ref, gq_ref, gkv_ref, wq_ref, wk_ref, wv_ref, c_ref, s1_ref, s2_ref),
         (dc_ref, dkr_ref, dwq_ref, dwk_ref, dwv_ref, dgq_ref, dgkv_ref), _, pctx) = _split_refs(refs, 13, 7, 0, plan)
        _plan_start(plan, pctx, grid)

        @pl.when(pl.program_id(0) == 0)
        def _():
            for ref in (dwq_ref, dwk_ref, dwv_ref, dgq_ref, dgkv_ref):
                ref[...] = jnp.zeros_like(ref)

        c, s1, s2 = c_ref[...], s1_ref[...], s2_ref[...]
        lane = lax.broadcasted_iota(jnp.int32, (tm, 128), 1)
        nq, xq, rq = _rms(cq_ref[...], gq_ref[...])
        nkv, xkv, rkv = _rms(ckv_ref[...], gkv_ref[...])
        nq_b, nkv_b = nq.astype(BF16), nkv.astype(BF16)

        dq_parts, dk_parts = [], []
        dkr = jnp.zeros((tm, 128), F32)
        for h in range(HEADS):
            blk = dq_ref[h // 2, :, (h % 2) * 128:(h % 2 + 1) * 128] * QK_SCALE
            dq_parts.append(_rope_t(blk, c, s1, s2).astype(BF16))
            kb = dk_ref[:, h * 128:(h + 1) * 128]
            dk_parts.append(jnp.where(lane < NOPE, kb, 0.0).astype(BF16))
            dkr = dkr + kb
        dq_b = jnp.concatenate(dq_parts, axis=1)
        dk_b = jnp.concatenate(dk_parts, axis=1)
        dv_b = dv_ref[...].astype(BF16)

        dwq_ref[...] += lax.dot_general(nq_b, dq_b, TN_DIMS, preferred_element_type=F32)
        dwk_ref[...] += lax.dot_general(nkv_b, dk_b, TN_DIMS, preferred_element_type=F32)
        dwv_ref[...] += lax.dot_general(nkv_b, dv_b, TN_DIMS, preferred_element_type=F32)
        dnq = lax.dot_general(dq_b, wq_ref[...], NT_DIMS, preferred_element_type=F32)
        dnkv = (lax.dot_general(dk_b, wk_ref[...], NT_DIMS, preferred_element_type=F32)
                + lax.dot_general(dv_b, wv_ref[...], NT_DIMS, preferred_element_type=F32))

        def rms_bwd(dn, xhat, rstd, g):
            dxh = dn * g
            return rstd * (dxh - xhat * jnp.mean(dxh * xhat, -1, keepdims=True))

        dc_ref[:, :256] = rms_bwd(dnq, xq, rq, gq_ref[...]).astype(BF16)
        dc_ref[:, 256:] = rms_bwd(dnkv, xkv, rkv, gkv_ref[...]).astype(BF16)
        dgq_ref[...] += _fold8(dnq * xq)
        dgkv_ref[...] += _fold8(dnkv * xkv)
        dkr = pltpu.roll(_rope_t(dkr, c, s1, s2), 64, 1)
        dkr_ref[...] = jnp.where(lane < ROPE, dkr, 0.0).astype(BF16)
        _plan_wait(plan, pctx, grid)

    full = lambda shape: pl.BlockSpec(shape, lambda i: (0,) * len(shape))
    tab = pl.BlockSpec((tm, 128), lambda i: (i, 0))
    p_in, p_ospec, p_oshape, p_scr, p_alias = _plan_io(plan, 13, 7)
    return pl.pallas_call(
        body, name="mla_bwd", grid=grid,
        in_specs=[pl.BlockSpec((tm, 256), lambda i: (i, 0)), pl.BlockSpec((tm, 256), lambda i: (i, 1)),
                  pl.BlockSpec((4, tm, 256), lambda i: (0, i, 0)),
                  pl.BlockSpec((tm, HW), lambda i: (i, 0)), pl.BlockSpec((tm, 512), lambda i: (i, 0)),
                  full((1, 256)), full((1, 256)), full((256, HW)), full((256, HW)), full((256, 512)), tab, tab, tab]
        + [_ANY] * len(p_in),
        out_specs=[pl.BlockSpec((tm, 512), lambda i: (i, 0)), tab, full((256, HW)), full((256, HW)),
                   full((256, 512)), full((8, 256)), full((8, 256))] + p_ospec,
        out_shape=[jax.ShapeDtypeStruct((T, 512), BF16), jax.ShapeDtypeStruct((T, 128), BF16),
                   jax.ShapeDtypeStruct((256, HW), F32), jax.ShapeDtypeStruct((256, HW), F32),
                   jax.ShapeDtypeStruct((256, 512), F32), jax.ShapeDtypeStruct((8, 256), F32),
                   jax.ShapeDtypeStruct((8, 256), F32)] + p_oshape,
        scratch_shapes=p_scr, input_output_aliases=p_alias, compiler_params=_params(48, 1),
    )(z0, z0, dq4, dk, dv, gq, gkv, wq, wk, wv, rc, rs1, rs2, *p_in)


def _sgu_fwd(z0, a_out, ln_g, ln_b, w, b_t):
    T = z0.shape[0]
    tm = min(ROW_BLOCK, T)
    W = SGU_G * SGU_C

    def body(u_ref, v_ref, a_ref, g_ref, b_ref, w_ref, bt_ref, o_ref):
        o_ref[:, :W] = a_ref[...]
        ug = _gelu(u_ref[...])
        xhat, _ = _ln_stats(_gelu(v_ref[...]))
        vn = (xhat * g_ref[...] + b_ref[...]).astype(BF16)
        tril = lax.broadcasted_iota(jnp.int32, (SGU_C, SGU_C), 0) >= lax.broadcasted_iota(jnp.int32, (SGU_C, SGU_C), 1)
        for g in range(SGU_G):
            cs = slice(g * SGU_C, (g + 1) * SGU_C)
            wg = jnp.where(tril, w_ref[g], 0.0).astype(BF16)
            bcol = bt_ref[:, g:g + 1]
            for c in range(tm // SGU_C):
                rs = slice(c * SGU_C, (c + 1) * SGU_C)
                mixed = jnp.dot(wg, vn[rs, cs], preferred_element_type=F32) + bcol
                o_ref[rs, W + g * SGU_C:W + (g + 1) * SGU_C] = (ug[rs, cs] * mixed).astype(BF16)

    full = lambda shape: pl.BlockSpec(shape, lambda i: (0,) * len(shape))
    return pl.pallas_call(
        body, name="sgu_fwd", grid=(T // tm,),
        in_specs=[pl.BlockSpec((tm, W), lambda i: (i, 1)), pl.BlockSpec((tm, W), lambda i: (i, 2)),
                  pl.BlockSpec((tm, W), lambda i: (i, 0)),
                  full((1, W)), full((1, W)), full((SGU_G, SGU_C, SGU_C)), full((SGU_C, SGU_G))],
        out_specs=pl.BlockSpec((tm, 2 * W), lambda i: (i, 0)),
        out_shape=jax.ShapeDtypeStruct((T, 2 * W), BF16),
        compiler_params=_params(32, 1),
    )(z0, z0, a_out, ln_g, ln_b, w, b_t)


def _sgu_bwd(z0, dmix, dc, dkr, ln_g, ln_b, w, b_t):
    T = z0.shape[0]
    tm = min(ROW_BLOCK, T)
    W = SGU_G * SGU_C

    def body(u_ref, v_ref, do_ref, dc_ref, dkr_ref, g_ref, b_ref, w_ref, bt_ref, dz_ref, dw_ref, db_ref, dlg_ref,
             dlb_ref):
        @pl.when(pl.program_id(0) == 0)
        def _():
            for ref in (dw_ref, db_ref, dlg_ref, dlb_ref):
                ref[...] = jnp.zeros_like(ref)

        dz_ref[:, :W] = dc_ref[...]
        dz_ref[:, 3 * W:] = dkr_ref[...]

        u, v, dout = u_ref[...], v_ref[...], do_ref[...]
        ug = _gelu(u)
        xhat, rstd = _ln_stats(_gelu(v))
        vn = (xhat * g_ref[...] + b_ref[...]).astype(BF16)
        dmixed = dout * ug
        dmixed_b = dmixed.astype(BF16)
        tril = lax.broadcasted_iota(jnp.int32, (SGU_C, SGU_C), 0) >= lax.broadcasted_iota(jnp.int32, (SGU_C, SGU_C), 1)
        lane = lax.broadcasted_iota(jnp.int32, (SGU_C, SGU_C), 1)
        dvn_cols = []
        for g in range(SGU_G):
            cs = slice(g * SGU_C, (g + 1) * SGU_C)
            wg = jnp.where(tril, w_ref[g], 0.0).astype(BF16)
            bcol = bt_ref[:, g:g + 1]
            dw_g = jnp.zeros((SGU_C, SGU_C), F32)
            db_g = jnp.zeros((SGU_C, 1), F32)
            dvn_rows = []
            for c in range(tm // SGU_C):
                rs = slice(c * SGU_C, (c + 1) * SGU_C)
                mixed = jnp.dot(wg, vn[rs, cs], preferred_element_type=F32) + bcol
                dz_ref[rs, W + g * SGU_C:W + (g + 1) * SGU_C] = (dout[rs, cs] * mixed * _gelu_grad(u[rs, cs])).astype(BF16)
                dm = dmixed_b[rs, cs]
                dw_g = dw_g + lax.dot_general(dm, vn[rs, cs], NT_DIMS, preferred_element_type=F32)
                db_g = db_g + jnp.sum(dmixed[rs, cs], axis=1, keepdims=True)
                dvn_rows.append(lax.dot_general(wg, dm, TN_DIMS, preferred_element_type=F32))
            dw_ref[g] += jnp.where(tril, dw_g, 0.0)
            db_ref[...] += jnp.where(lane == g, db_g, 0.0)
            dvn_cols.append(jnp.concatenate(dvn_rows, axis=0))
        dvn = jnp.concatenate(dvn_cols, axis=1)
        dxh = dvn * g_ref[...]
        m1 = jnp.mean(dxh, -1, keepdims=True)
        m2 = jnp.mean(dxh * xhat, -1, keepdims=True)
        dvg = rstd * (dxh - m1 - xhat * m2)
        dz_ref[:, 2 * W:3 * W] = (dvg * _gelu_grad(v)).astype(BF16)
        dlg_ref[...] += _fold8(dvn * xhat)
        dlb_ref[...] += _fold8(dvn)

    full = lambda shape: pl.BlockSpec(shape, lambda i: (0,) * len(shape))
    return pl.pallas_call(
        body, name="sgu_bwd", grid=(T // tm,),
        in_specs=[pl.BlockSpec((tm, W), lambda i: (i, 1)), pl.BlockSpec((tm, W), lambda i: (i, 2)),
                  pl.BlockSpec((tm, W), lambda i: (i, 1)), pl.BlockSpec((tm, W), lambda i: (i, 0)),
                  pl.BlockSpec((tm, 128), lambda i: (i, 0)),
                  full((1, W)), full((1, W)), full((SGU_G, SGU_C, SGU_C)), full((SGU_C, SGU_G))],
        out_specs=[pl.BlockSpec((tm, 3 * W + 128), lambda i: (i, 0)), full((SGU_G, SGU_C, SGU_C)),
                   full((SGU_C, SGU_C)), full((8, W)), full((8, W))],
        out_shape=[jax.ShapeDtypeStruct((T, 3 * W + 128), BF16), jax.ShapeDtypeStruct((SGU_G, SGU_C, SGU_C), F32),
                   jax.ShapeDtypeStruct((SGU_C, SGU_C), F32), jax.ShapeDtypeStruct((8, W), F32),
                   jax.ShapeDtypeStruct((8, W), F32)],
        compiler_params=_params(40, 1),
    )(z0, z0, dmix, dc, dkr, ln_g, ln_b, w, b_t)


def _hg_lower_bound(lb_ref):
    a0, a1 = lb_ref[0:1, :], lb_ref[1:2, :]
    m = jnp.maximum(a0, a1)
    e0, e1 = jnp.exp(a0 - m), jnp.exp(a1 - m)
    return e1 / (e0 + e1)


def _running_sum(x, reverse=False):
    n = x.shape[0]
    row = lax.broadcasted_iota(jnp.int32, x.shape, 0)
    s = 1
    while s < n:
        if reverse:
            x = x + jnp.where(row < n - s, pltpu.roll(x, n - s, 0), 0.0)
        else:
            x = x + jnp.where(row >= s, pltpu.roll(x, s, 0), 0.0)
        s *= 2
    return x


def _hg_chunk(qc, fc, lb):
    C = HG_CHUNK
    rows = lax.broadcasted_iota(jnp.int32, (C, C), 0)
    cols = lax.broadcasted_iota(jnp.int32, (C, C), 1)
    rowid = lax.broadcasted_iota(jnp.int32, (C, 128), 0)
    sq, sg = _sigmoid(qc), _sigmoid(fc)
    qf = qc * sq
    gate = lb + (1.0 - lb) * sg
    kk = 1.0 - gate
    lg = jnp.log(gate)
    bcum = _running_sum(lg)
    b_mid = jnp.sum(jnp.where(rowid < C // 2, lg, 0.0), axis=0, keepdims=True)
    b_last = jnp.sum(lg, axis=0, keepdims=True)
    eq, ek, e, eh = jnp.exp(bcum - b_mid), jnp.exp(b_mid - bcum), jnp.exp(bcum), jnp.exp(b_last - bcum)
    qt, kt, qe, khat = qf * eq, kk * ek, qf * e, kk * eh
    a = lax.dot_general(qt.astype(BF16), kt.astype(BF16), NT_DIMS, preferred_element_type=F32)
    a = jnp.where(rows >= cols, a, 0.0)
    return dict(sq=sq, sg=sg, gate=gate, kk=kk, eq=eq, ek=ek, e=e, eh=eh, qt=qt, kt=kt, qe=qe, khat=khat, a=a,
                e_last=jnp.exp(b_last), tril=rows >= cols, rowid=rowid)


def _hgrn_fwd(z4, hg_lb, gnorm):
    T = z4.shape[1]
    tb = min(ROW_BLOCK, T)
    C = HG_CHUNK
    ncb = tb // C
    HPB = HG_HEADS_PER_STEP

    def body(q_ref, f_ref, i_ref, g_ref, lb_ref, gn_ref, y_ref, o_ref, st_ref, st_sc):
        @pl.when(pl.program_id(1) == 0)
        def _():
            st_sc[...] = jnp.zeros_like(st_sc)

        def chunk(c, carry):
            rs = pl.ds(pl.multiple_of(c * C, C), C)
            for hh in range(HPB):
                hs = slice(hh * 128, (hh + 1) * 128)
                lb = _hg_lower_bound(lb_ref.at[:, hs])
                v_b = i_ref[rs, hs].astype(BF16)
                gc = g_ref[rs, hs]
                x = _hg_chunk(q_ref[rs, hs], f_ref[rs, hs], lb)
                st = st_sc[hh]
                st_ref[hh, c] = st
                o = (jnp.dot(x["a"].astype(BF16), v_b, preferred_element_type=F32)
                     + lax.dot_general(x["qe"].astype(BF16), st.astype(BF16), NT_DIMS, preferred_element_type=F32))
                st_sc[hh] = st * x["e_last"] + lax.dot_general(v_b, x["khat"].astype(BF16), TN_DIMS,
                                                               preferred_element_type=F32)
                o_ref[rs, hs] = o
                n = o * lax.rsqrt(jnp.mean(o * o, -1, keepdims=True) + EPS)
                y_ref[rs, hs] = (n * gn_ref[:, hs] * (gc * _sigmoid(gc))).astype(BF16)
            return carry

        lax.fori_loop(0, ncb, chunk, 0)

    W = 128 * HPB
    zb = lambda k: pl.BlockSpec((None, tb, W), lambda h, t: (k, t, h))
    out = pl.BlockSpec((tb, W), lambda h, t: (t, h))
    return pl.pallas_call(
        body, name="hgrn_fwd", grid=(HEADS // HPB, T // tb),
        in_specs=[zb(0), zb(1), zb(2), zb(3), pl.BlockSpec((2, W), lambda h, t: (0, h)),
                  pl.BlockSpec((1, W), lambda h, t: (0, h))],
        out_specs=[out, out, pl.BlockSpec((HPB, ncb, 128, 128), lambda h, t: (h, t, 0, 0))],
        out_shape=[jax.ShapeDtypeStruct((T, D), BF16), jax.ShapeDtypeStruct((T, D), F32),
                   jax.ShapeDtypeStruct((HEADS, T // C, 128, 128), F32)],
        scratch_shapes=[pltpu.VMEM((HPB, 128, 128), F32)],
        compiler_params=_params(32, 2),
    )(z4, z4, z4, z4, hg_lb, gnorm)


def _hgrn_bwd(z4, o_raw, dy, states, hg_lb, gnorm):
    T = z4.shape[1]
    tb = min(ROW_BLOCK, T)
    C = HG_CHUNK
    ncb = tb // C
    nt = T // tb
    HPB = HG_HEADS_PER_STEP

    def body(q_ref, f_ref, i_ref, g_ref, o_ref, dy_ref, st_ref, lb_ref, gn_ref, dz_ref, dlb_ref, dgn_ref, dst_sc):
        @pl.when(pl.program_id(1) == 0)
        def _():
            dst_sc[...] = jnp.zeros_like(dst_sc)
            dlb_ref[...] = jnp.zeros_like(dlb_ref)
            dgn_ref[...] = jnp.zeros_like(dgn_ref)

        def chunk(cc, carry):
            for hh in range(HPB):
                one_head(ncb - 1 - cc, hh, slice(hh * 128, (hh + 1) * 128))
            return carry

        def one_head(c, hh, hs):
            rs = pl.ds(pl.multiple_of(c * C, C), C)
            lb = _hg_lower_bound(lb_ref.at[:, hs])
            gn = gn_ref[:, hs]
            qc, gc = q_ref[rs, hs], g_ref[rs, hs]
            v_b = i_ref[rs, hs].astype(BF16)
            x = _hg_chunk(qc, f_ref[rs, hs], lb)
            st, dst = st_ref[hh, c], dst_sc[hh]
            st_b, dst_b = st.astype(BF16), dst.astype(BF16)
            o, dyc = o_ref[rs, hs], dy_ref[rs, hs]
            sgg = _sigmoid(gc)
            sil = gc * sgg
            rstd = lax.rsqrt(jnp.mean(o * o, -1, keepdims=True) + EPS)
            n = o * rstd
            dgn_ref[:, hs] += _fold8(dyc * n * sil)
            dn = dyc * gn * sil
            do = rstd * (dn - n * jnp.mean(dn * n, -1, keepdims=True))
            dg = dyc * n * gn * (sgg * (1.0 + gc * (1.0 - sgg)))
            do_b = do.astype(BF16)
            da = jnp.where(x["tril"], lax.dot_general(do_b, v_b, NT_DIMS, preferred_element_type=F32), 0.0).astype(BF16)
            qt_b, kt_b, qe_b, khat_b = (x[n_].astype(BF16) for n_ in ("qt", "kt", "qe", "khat"))
            dv = (lax.dot_general(x["a"].astype(BF16), do_b, TN_DIMS, preferred_element_type=F32)
                  + lax.dot_general(khat_b, dst_b, NT_DIMS, preferred_element_type=F32))
            dqt = jnp.dot(da, kt_b, preferred_element_type=F32)
            dqe = jnp.dot(do_b, st_b, preferred_element_type=F32)
            dkt = lax.dot_general(da, qt_b, TN_DIMS, preferred_element_type=F32)
            dkhat = jnp.dot(v_b, dst_b, preferred_element_type=F32)
            dst_sc[hh] = lax.dot_general(do_b, qe_b, TN_DIMS, preferred_element_type=F32) + dst * x["e_last"]
            de_last = jnp.sum(st * dst, axis=0, keepdims=True)
            dqf = dqt * x["eq"] + dqe * x["e"]
            dkk = dkt * x["ek"] + dkhat * x["eh"]
            dkh_kh = dkhat * x["khat"]
            db = dqt * qt_b.astype(F32) - dkt * kt_b.astype(F32) + dqe * x["qe"] - dkh_kh
            db_last = jnp.sum(dkh_kh, axis=0, keepdims=True) + de_last * x["e_last"]
            db = db + jnp.where(x["rowid"] == C - 1, db_last, 0.0)
            dlg = _running_sum(db, reverse=True)
            dgate = dlg / x["gate"] - dkk
            sg, sq = x["sg"], x["sq"]
            dlb_ref[:, hs] += _fold8(dgate * (1.0 - sg)) * (lb * (1.0 - lb))
            dz_ref[0, rs, hs] = (dqf * (sq * (1.0 + qc * (1.0 - sq)))).astype(BF16)
            dz_ref[1, rs, hs] = (dgate * (1.0 - lb) * sg * (1.0 - sg)).astype(BF16)
            dz_ref[2, rs, hs] = dv.astype(BF16)
            dz_ref[3, rs, hs] = dg.astype(BF16)

        lax.fori_loop(0, ncb, chunk, 0)

    W = 128 * HPB
    zb = lambda k: pl.BlockSpec((None, tb, W), lambda h, t: (k, nt - 1 - t, h))
    blk = pl.BlockSpec((tb, W), lambda h, t: (nt - 1 - t, h))
    acc = pl.BlockSpec((8, W), lambda h, t: (0, h))
    return pl.pallas_call(
        body, name="hgrn_bwd", grid=(HEADS // HPB, nt),
        in_specs=[zb(0), zb(1), zb(2), zb(3), blk, blk,
                  pl.BlockSpec((HPB, ncb, 128, 128), lambda h, t: (h, nt - 1 - t, 0, 0)),
                  pl.BlockSpec((2, W), lambda h, t: (0, h)), pl.BlockSpec((1, W), lambda h, t: (0, h))],
        out_specs=[pl.BlockSpec((4, tb, W), lambda h, t: (0, nt - 1 - t, h)), acc, acc],
        out_shape=[jax.ShapeDtypeStruct((4, T, D), BF16), jax.ShapeDtypeStruct((8, D), F32),
                   jax.ShapeDtypeStruct((8, D), F32)],
        scratch_shapes=[pltpu.VMEM((HPB, 128, 128), F32)],
        compiler_params=_params(32, 2),
    )(z4, z4, z4, z4, o_raw, dy, states, hg_lb, gnorm)


def _adamw(w, g, m, v, *, name):
    R, L = w.shape
    tr = R if R <= 512 else 512
    assert R % tr == 0
    blk = pl.BlockSpec((tr, L), lambda i: (i, 0))
    c1, c2 = 1.0 - B1 ** STEP, 1.0 - B2 ** STEP

    def body(w_ref, g_ref, m_ref, v_ref, d_ref, mo_ref, vo_ref):
        g_ = g_ref[...]
        m_ = B1 * m_ref[...] + (1.0 - B1) * g_
        v_ = B2 * v_ref[...] + (1.0 - B2) * (g_ * g_)
        d_ref[...] = -LR * ((m_ / c1) / (jnp.sqrt(v_ / c2) + ADAM_EPS) + WD * w_ref[...])
        mo_ref[...] = m_
        vo_ref[...] = v_

    sds = jax.ShapeDtypeStruct((R, L), F32)
    return pl.pallas_call(
        body, name=name, grid=(R // tr,), in_specs=[blk] * 4, out_specs=[blk] * 3, out_shape=[sds] * 3,
        compiler_params=_params(32, 1),
    )(w, g, m, v)


def _adamw_rows(w, m, v, gbufs, row0, *, name, plan=None):
    L, R, C = w.shape
    tr = 256
    assert R % tr == 0 and row0 % tr == 0 and len(gbufs) == L
    grid = (L, R // tr)
    blk = pl.BlockSpec((None, tr, C), lambda l, i: (l, i, 0))
    gblk = pl.BlockSpec((tr, C), lambda l, i: (row0 // tr + i, 0))
    c1, c2 = 1.0 - B1 ** STEP, 1.0 - B2 ** STEP

    def body(*refs):
        ins, (go_ref, d_ref, mo_ref, vo_ref), _, pctx = _split_refs(refs, 3 + L, 4, 0, plan)
        w_ref, m_ref, v_ref = ins[:3]
        g_refs = ins[3:]
        _plan_start(plan, pctx, grid)
        g_ = g_refs[0][...]
        for l in range(1, L):
            g_ = jnp.where(pl.program_id(0) == l, g_refs[l][...], g_)
        m_ = B1 * m_ref[...] + (1.0 - B1) * g_
        v_ = B2 * v_ref[...] + (1.0 - B2) * (g_ * g_)
        go_ref[...] = g_
        d_ref[...] = -LR * ((m_ / c1) / (jnp.sqrt(v_ / c2) + ADAM_EPS) + WD * w_ref[...])
        mo_ref[...] = m_
        vo_ref[...] = v_
        _plan_wait(plan, pctx, grid)

    sds = jax.ShapeDtypeStruct((L, R, C), F32)
    p_in, p_ospec, p_oshape, p_scr, p_alias = _plan_io(plan, 3 + L, 4)
    return pl.pallas_call(
        body, name=name, grid=grid, in_specs=[blk] * 3 + [gblk] * L + [_ANY] * len(p_in),
        out_specs=[blk] * 4 + p_ospec, out_shape=[sds] * 4 + p_oshape, scratch_shapes=p_scr,
        input_output_aliases=p_alias, compiler_params=_params(32, 2),
    )(w, m, v, *gbufs, *p_in)


def _add_pairs(g, theirs, ids, *, name):
    n, R, L = theirs.shape
    tr = 128
    nb = R // tr

    def body(ids_ref, a_ref, b_ref, o_ref):
        o_ref[...] = (a_ref[...].astype(F32) + b_ref[...].astype(F32)).astype(BF16)

    blk = pl.BlockSpec((n, tr, L), lambda i, ids: (0, i, 0))
    return pl.pallas_call(
        body, name=name, out_shape=jax.ShapeDtypeStruct((n, R, L), BF16),
        grid_spec=pltpu.PrefetchScalarGridSpec(
            num_scalar_prefetch=1, grid=(nb,),
            in_specs=[pl.BlockSpec((n, tr, L), lambda i, ids: (0, ids[1] * nb + i, 0)), blk], out_specs=blk),
        compiler_params=_params(16, 1),
    )(ids, g, theirs)


def _sum_chips(pair, parts, ids, *, name):
    _, R, L = parts.shape
    tr = 128

    def body(ids_ref, o_ref, r_ref, out_ref):
        out_ref[...] = ((o_ref[...].astype(F32) + r_ref[0].astype(F32)) + r_ref[1].astype(F32)) + r_ref[2].astype(F32)

    return pl.pallas_call(
        body, name=name, out_shape=jax.ShapeDtypeStruct((2, R, L), F32),
        grid_spec=pltpu.PrefetchScalarGridSpec(
            num_scalar_prefetch=1, grid=(R // tr,),
            in_specs=[pl.BlockSpec((None, tr, L), lambda i, ids: (ids[0], i, 0)),
                      pl.BlockSpec((3, tr, L), lambda i, ids: (0, i, 0))],
            out_specs=pl.BlockSpec((None, tr, L), lambda i, ids: (ids[1], i, 0))),
        compiler_params=_params(32, 1),
    )(ids, pair, parts)


def _mesh_ids():
    x, y, c = _mesh_pos()
    return jnp.stack([2 * x + y, c]).astype(jnp.int32)


def _place_shard(rows, ids, *, name):
    R, L = rows.shape
    tr = 256

    def body(ids_ref, in_ref, out_ref):
        out_ref[...] = in_ref[...].astype(BF16)

    return pl.pallas_call(
        body, name=name, out_shape=jax.ShapeDtypeStruct((4, R, L), BF16),
        grid_spec=pltpu.PrefetchScalarGridSpec(
            num_scalar_prefetch=1, grid=(R // tr,), in_specs=[pl.BlockSpec((tr, L), lambda i, ids: (i, 0))],
            out_specs=pl.BlockSpec((None, tr, L), lambda i, ids: (ids[0], i, 0))),
        compiler_params=_params(16, 1),
    )(ids, rows)


def _remote(src, dst, send_sem, recv_sem, to):
    return pltpu.make_async_remote_copy(src_ref=src, dst_ref=dst, send_sem=send_sem, recv_sem=recv_sem,
                                        device_id=to, device_id_type=MESH_IDS)


def _rows(ref, lead, start, size):
    return ref.at[tuple(pl.ds(0, n) for n in ref.shape[:lead]) + (pl.ds(start, size),)]


def _other_chips():
    x, y, _ = _mesh_pos()
    return [(1 - x, y), (x, 1 - y), (1 - x, 1 - y)]


def _plan_gather_ici(bufs):
    n = len(bufs)

    def copies(outs, send, recv):
        x, y, c = _mesh_pos()
        res = []
        for b in range(n):
            half = bufs[b].shape[1] // 2
            mine = _rows(outs[b].at[2 * x + y], 0, c * half, half)
            for j, (cx, cy) in enumerate(_other_chips()):
                res.append((_remote(mine, mine, send(3 * b + j), recv(3 * b + j), (cx, cy, c)),
                            _remote(mine, _rows(outs[b].at[2 * cx + cy], 0, c * half, half),
                                    send(3 * b + j), recv(3 * b + j), (x, y, c))))
        return res

    def start(ins, outs, send, recv, loc):
        for out_cp, _ in copies(outs, send, recv):
            out_cp.start()

    def wait(ins, outs, send, recv, loc):
        for out_cp, in_cp in copies(outs, send, recv):
            in_cp.wait_recv()
            out_cp.wait_send()

    outs = [jax.ShapeDtypeStruct(b.shape, b.dtype) for b in bufs]
    return _Plan(bufs, outs, 3 * n, 0, start, wait, aliases={b: b for b in range(n)})


def _plan_gather_forward(bufs):
    n = len(bufs)

    def copies(outs, send, recv):
        x, y, c = _mesh_pos()
        res = []
        for b in range(n):
            half = bufs[b].shape[1] // 2
            for j, (cx, cy) in enumerate(_other_chips()):
                slot = outs[b].at[2 * cx + cy]
                res.append((_remote(_rows(slot, 0, c * half, half), _rows(slot, 0, c * half, half),
                                    send(3 * b + j), recv(3 * b + j), (x, y, 1 - c)),
                            _remote(_rows(slot, 0, c * half, half), _rows(slot, 0, (1 - c) * half, half),
                                    send(3 * b + j), recv(3 * b + j), (x, y, c))))
        return res

    def start(ins, outs, send, recv, loc):
        for out_cp, _ in copies(outs, send, recv):
            out_cp.start()

    def wait(ins, outs, send, recv, loc):
        for out_cp, in_cp in copies(outs, send, recv):
            in_cp.wait_recv()
            out_cp.wait_send()

    outs = [jax.ShapeDtypeStruct(b.shape, b.dtype) for b in bufs]
    return _Plan(bufs, outs, 3 * n, 0, start, wait, aliases={b: b for b in range(n)})


def _plan_pair_swap(g):
    half = g.shape[1] // 2

    def copy(ins, outs, send, recv, loc):
        x, y, c = _mesh_pos()
        return _remote(_rows(ins[0], 1, (1 - c) * half, half), outs[0], send(0), recv(0), (x, y, 1 - c))

    return _Plan([g], [jax.ShapeDtypeStruct((4, half, g.shape[2]), g.dtype)], 1, 0,
                 lambda *a: copy(*a).start(), lambda *a: copy(*a).wait())


def _plan_pair_gather(buf):
    def copies(ins, outs, send, recv, loc):
        x, y, c = _mesh_pos()
        return (_remote(outs[0].at[c], outs[0].at[c], send(0), recv(0), (x, y, 1 - c)),
                _remote(outs[0].at[c], outs[0].at[1 - c], send(0), recv(0), (x, y, c)))

    def wait(*a):
        out_cp, in_cp = copies(*a)
        in_cp.wait_recv()
        out_cp.wait_send()

    return _Plan([buf], [jax.ShapeDtypeStruct(buf.shape, buf.dtype)], 1, 0, lambda *a: copies(*a)[0].start(), wait,
                 aliases={0: 0})


def _plan_chip_scatter(p):
    def copies(ins, outs, send, recv, loc):
        _, _, c = _mesh_pos()
        return [_remote(ins[0].at[2 * cx + cy], outs[0].at[j], send(j), recv(j), (cx, cy, c))
                for j, (cx, cy) in enumerate(_other_chips())]

    def start(*a):
        for cp in copies(*a):
            cp.start()

    def wait(*a):
        for cp in copies(*a):
            cp.wait()

    return _Plan([p], [jax.ShapeDtypeStruct((3,) + p.shape[1:], p.dtype)], 3, 0, start, wait)


def _plan_exchange_all(vec):
    def copies(ins, outs, send, recv, loc):
        x, y, c = _mesh_pos()
        return [_remote(ins[0], outs[0].at[r - 1], send(r - 1), recv(r - 1), (x ^ (r >> 2), y ^ ((r >> 1) & 1), c ^ (r & 1)))
                for r in range(1, 8)]

    def start(*a):
        for cp in copies(*a):
            cp.start()

    def wait(*a):
        for cp in copies(*a):
            cp.wait()

    return _Plan([vec], [jax.ShapeDtypeStruct((7,) + vec.shape, vec.dtype)], 7, 0, start, wait)


def _sum_devices(vec, others, ids):
    R, L = vec.shape

    def body(ids_ref, v_ref, o_ref, out_ref):
        me = 2 * ids_ref[0] + ids_ref[1]
        total = None
        for d in range(8):
            rel = d ^ me
            term = jnp.where(rel == 0, v_ref[...], o_ref[jnp.maximum(rel - 1, 0)])
            total = term if total is None else total + term
        out_ref[...] = total

    return pl.pallas_call(
        body, name="small_grad_sum", out_shape=jax.ShapeDtypeStruct((R, L), F32),
        grid_spec=pltpu.PrefetchScalarGridSpec(
            num_scalar_prefetch=1, grid=(1,), in_specs=[pl.BlockSpec((R, L), lambda i, ids: (0, 0)),
                                                        pl.BlockSpec((7, R, L), lambda i, ids: (0, 0, 0))],
            out_specs=pl.BlockSpec((R, L), lambda i, ids: (0, 0))),
        compiler_params=_params(16, 1),
    )(ids, vec, others)


ROWS_L1, ROWS_L0, ROWS_ODD = 3328, 2048, 768
ODD_PARTS = (("w_out_e", (256, 1024)), ("w_in_e", (1024, 392)), ("w_qb", (256, 192)), ("w_kvb", (256, 256)))


def _odd_rows(parts, dtype, gnorm=None):
    rows = [parts[n].reshape(-1, 1024).astype(dtype) for n, _ in ODD_PARTS]
    used = sum(r.shape[0] for r in rows)
    if gnorm is not None:
        bits = lax.bitcast_convert_type(gnorm.reshape(-1), BF16).reshape(1, 512)
        rows.append(jnp.pad(bits, ((0, 0), (0, 512))))
        used += 1
    rows.append(jnp.zeros((ROWS_ODD - used, 1024), dtype))
    return jnp.concatenate(rows, axis=0)


def _odd_unrows(buf, with_gnorm=False):
    out, off = {}, 0
    for n, shape in ODD_PARTS:
        nr = math.prod(shape) // 1024
        out[n] = buf[off:off + nr].reshape(shape)
        off += nr
    if with_gnorm:
        out["hg_gnorm"] = lax.bitcast_convert_type(buf[off, :512].reshape(256, 2), F32).reshape(1, 256)
    return out


def _pack_small(vals, last):
    flat = jnp.concatenate([vals[n].reshape(-1).astype(F32) for n, _ in SMALL])
    pad = jnp.zeros((SMALL_ROWS * 1024 - flat.shape[0] - 1,), F32)
    return jnp.concatenate([flat, pad, last.reshape(1)]).reshape(SMALL_ROWS, 1024)


def _unpack_small(packed):
    flat = packed.reshape(-1)
    out, off = {}, 0
    for n, shape in SMALL:
        size = math.prod(shape)
        out[n] = flat[off:off + size].reshape(shape)
        off += size
    return out


def _rope_tables(positions):
    half = ROPE // 2
    inv_freq = ROPE_BASE ** (-jnp.arange(half, dtype=F32) / half)
    ang = positions.astype(F32).reshape(-1, 1) * inv_freq
    cos, sin = jnp.cos(ang), jnp.sin(ang)
    T = ang.shape[0]
    one, z16, z32 = jnp.ones((T, NOPE), F32), jnp.zeros((T, half), F32), jnp.zeros((T, 32), F32)
    z64 = jnp.zeros((T, NOPE), F32)
    c = jnp.concatenate([one, cos, cos, z32], axis=1)
    s1 = jnp.concatenate([z64, -sin, z16, z32], axis=1)
    s2 = jnp.concatenate([z64, z16, sin, z32], axis=1)
    return c, s1, s2


def _local_step(x, positions, tgt, odd, bufs, P, exchange):
    T = x.shape[0]
    row = lambda a: a.reshape(1, -1)
    rc, rs1, rs2 = _rope_tables(positions)
    blk = lambda f: pl.BlockSpec((None, D, D), f)

    w_in_e = odd["w_in_e"]
    w_in = jnp.concatenate([w_in_e[:, :512], w_in_e[:, 544:1568], w_in_e[:, 512:544], jnp.zeros((D, 96), BF16)], axis=1)
    wq = jnp.pad(odd["w_qb"].reshape(256, HEADS, NOPE + ROPE), ((0, 0), (0, 0), (0, 32))).reshape(256, HEADS * 128)
    kvb = odd["w_kvb"].reshape(256, HEADS, NOPE + VDIM)
    wk = jnp.pad(kvb[:, :, :NOPE], ((0, 0), (0, 0), (0, 64))).reshape(256, HEADS * 128)
    wv = kvb[:, :, NOPE:].reshape(256, HEADS * VDIM)
    w_out_e = odd["w_out_e"]
    sgu_w = P["sgu_w"][0]
    sgu_bt = P["sgu_b"][0].T
    gq, gkv = P["mla_gq"], P["mla_gkv"]
    gnorm = P["hg_gnorm"]

    z0 = _matmul(x, w_in, name="in_proj_e", M=T, N=1664, K=D, tn=1664)[0]
    q, k, v = _mla_prep(z0, gq, gkv, wq, wk, wv, rc, rs1, rs2)
    if exchange:
        ids = _mesh_ids()
        placed = [_place_shard(b, ids, name=f"place_shard_{l}") for l, b in enumerate(bufs)]
        a_out, lse, wga, wgb = _flash_fwd(q, k, v, plan=_plan_gather_ici(placed[:2]))
    else:
        a_out, lse = _flash_fwd(q, k, v)
        wga, wgb, wgc = bufs
    mix0 = _sgu_fwd(z0, a_out, P["sgu_ln_g"], P["sgu_ln_b"], sgu_w, sgu_bt)
    res = _proj_ln(mix0, w_out_e, x, row(P["ln1_g"][0]), row(P["ln1_b"][0]), name="out_proj_ln_e",
                   plan=_plan_gather_forward([wga, wgb]) if exchange else None)
    r1, h1, h1b = res[:3]
    if exchange:
        wga, wgb = res[3:]
    res = _ffn_ln(h1b, wga, h1, row(P["ln2_g"][0]), row(P["ln2_b"][0]), name="ffn_ln_0",
                  plan=_plan_gather_ici(placed[2:]) if exchange else None)
    ra0, r2, h2, h2b = res[:4]
    z4 = _matmul(h2b, wgb, name="in_proj_o", M=T, N=4 * D, K=D, b_spec=blk(lambda i, j, k: (j, 0, 0)),
                 out_shape=jax.ShapeDtypeStruct((4, T, D), F32),
                 o_spec=pl.BlockSpec((None, min(MM_ROWS, T), D), lambda i, j, k: (j, i, 0)))[0]
    y1, o_raw, states = _hgrn_fwd(z4, P["hg_lb"], gnorm)
    res2 = _proj_ln(y1, wgb, h2, row(P["ln1_g"][1]), row(P["ln1_b"][1]), name="out_proj_ln_o", w_rowblk=4,
                    plan=_plan_gather_forward([res[4]]) if exchange else None)
    r3, h3, h3b = res2[:3]
    if exchange:
        wgc = res2[3]
    ra1, r4, h4, _ = _ffn_ln(h3b, wgc, h3, row(P["ln2_g"][1]), row(P["ln2_b"][1]), name="ffn_ln_1")
    dy, loss_parts = _loss_dy(h4, tgt)

    gs = {}
    ln1_g, ln1_b, ln2_g, ln2_b = [None, None], [None, None], [None, None], [None, None]

    def ffn_bwd(l, dh, r_out, ra, h_mid_b, g2, wg, rows, plan=None):
        dr, dr_b, dg, db = _ln_bwd(dh, r_out, row(g2), name=f"ln2_bwd_{l}")
        ln2_g[l], ln2_b[l] = dg.sum(0), db.sum(0)
        da, *extra = _matmul(dr_b, wg, tb=True, mul=ra, out_dtype=BF16, name=f"ffn_da_{l}", M=T, N=4 * D, K=D,
                             b_spec=blk(lambda i, j, k: (j, 1, 0)), plan=plan)
        gbuf = _matmul(ra, dr_b, ta=True, a_sq=True, name=f"ffn_dw2_{l}", M=4 * D, N=D, K=T, tm=1024, tk=DW_TOKENS,
                       out_shape=jax.ShapeDtypeStruct((4, rows, D), BF16), o_spec=blk(lambda i, j, k: (i, 1, 0)))[0]
        gbuf = _matmul(h_mid_b, da, ta=True, name=f"ffn_dw1_{l}", M=D, N=4 * D, K=T, tm=1024, tk=DW_TOKENS, into=gbuf,
                       out_shape=jax.ShapeDtypeStruct((4, rows, D), BF16), o_spec=blk(lambda i, j, k: (j, 0, 0)))[0]
        dh_mid = _matmul(da, wg, tb=True, add=dr, add_scale=ALPHA, name=f"ffn_dh_{l}", M=T, N=D, K=4 * D,
                         b_spec=blk(lambda i, j, k: (k, 0, 0)))[0]
        return dh_mid, gbuf, extra

    dh3, g1, _ = ffn_bwd(1, dy, r4, ra1, h3b, P["ln2_g"][1], wgc, ROWS_L1)
    dr3, dr3_b, dg, db = _ln_bwd(dh3, r3, row(P["ln1_g"][1]), name="ln1_bwd_1")
    ln1_g[1], ln1_b[1] = dg.sum(0), db.sum(0)
    g1_sds = jax.ShapeDtypeStruct((4, ROWS_L1, D), BF16)
    g1 = _matmul(y1, dr3_b, ta=True, name="dw_out_o", M=D, N=D, K=T, tm=256, tk=DW_TOKENS, into=g1, out_shape=g1_sds,
                 o_spec=pl.BlockSpec((None, 256, D), lambda i, j, k: (i, 12, 0)))[0]
    dmix1 = _matmul(dr3_b, wgb, tb=True, name="dmix_o", M=T, N=D, K=D, b_spec=_rows4_spec(4, 3), b_merge=(D, D))[0]
    dz4, dlb, dgn = _hgrn_bwd(z4, o_raw, dmix1, states, P["hg_lb"], gnorm)
    g1 = _matmul(h2b, dz4, ta=True, name="dw_in_o", M=D, N=4 * D, K=T, tm=1024, tk=DW_TOKENS, into=g1, out_shape=g1_sds,
                 b_spec=pl.BlockSpec((None, min(DW_TOKENS, T), D), lambda i, j, k: (j, k, 0)),
                 o_spec=blk(lambda i, j, k: (j, 2, 0)))[0]
    dh2 = _matmul(dz4, wgb, tb=True, add=dr3, add_scale=ALPHA, name="dh_in_o", M=T, N=D, K=4 * D,
                  a_spec=pl.BlockSpec((None, min(MM_ROWS, T), D), lambda i, j, k: (k, i, 0)),
                  b_spec=blk(lambda i, j, k: (k, 0, 0)))[0]
    d_lb1 = dlb.sum(0)
    gs["hg_lb"] = jnp.stack([-d_lb1, d_lb1])
    gs["hg_gnorm"] = dgn.sum(0)[None]

    dh1, g0, swapped1 = ffn_bwd(0, dh2, r2, ra0, h1b, P["ln2_g"][0], wga, ROWS_L0,
                                plan=_plan_pair_swap(g1) if exchange else None)
    dr1, dr1_b, dg, db = _ln_bwd(dh1, r1, row(P["ln1_g"][0]), name="ln1_bwd_0")
    ln1_g[0], ln1_b[0] = dg.sum(0), db.sum(0)
    godd = {"w_out_e": _matmul(mix0, dr1_b, ta=True, name="dw_out_e", M=D, N=D, K=T, tm=1024, tk=DW_TOKENS)[0]}
    dmix0, *swapped0 = _matmul(dr1_b, w_out_e, tb=True, name="dmix_e", M=T, N=D, K=D,
                               plan=_plan_pair_swap(g0) if exchange else None)
    delta, do_b = _attn_delta(dmix0, a_out)
    if exchange:
        pair1 = _add_pairs(g1, swapped1[0], ids, name="grad_pair_add_1")
        pair0 = _add_pairs(g0, swapped0[0], ids, name="grad_pair_add_0")
        dq4, dk, dv, parts0, parts1 = _flash_bwd(
            q, k, v, do_b, lse, delta, plan=_join_plans([_plan_chip_scatter(pair0), _plan_chip_scatter(pair1)]))
        half0 = _sum_chips(pair0, parts0, ids, name="grad_chip_sum_0")
        half1 = _sum_chips(pair1, parts1, ids, name="grad_chip_sum_1")
        dc, dkr, dwq, dwk, dwv, dgq, dgkv, g0, g1 = _mla_bwd(
            z0, dq4, dk, dv, gq, gkv, wq, wk, wv, rc, rs1, rs2,
            plan=_join_plans([_plan_pair_gather(half0), _plan_pair_gather(half1)]))
        g0, g1 = g0.reshape(ROWS_L0, D), g1.reshape(ROWS_L1, D)
    else:
        dq4, dk, dv = _flash_bwd(q, k, v, do_b, lse, delta)
        dc, dkr, dwq, dwk, dwv, dgq, dgkv = _mla_bwd(z0, dq4, dk, dv, gq, gkv, wq, wk, wv, rc, rs1, rs2)
    dz0, dsw, dsb, dslg, dslb = _sgu_bwd(z0, dmix0, dc, dkr, P["sgu_ln_g"], P["sgu_ln_b"], sgu_w, sgu_bt)
    gs["mla_gq"], gs["mla_gkv"] = dgq.sum(0)[None], dgkv.sum(0)[None]
    gs["sgu_ln_g"], gs["sgu_ln_b"] = dslg.sum(0)[None], dslb.sum(0)[None]
    gs["sgu_w"], gs["sgu_b"] = dsw[None], dsb[:, :SGU_G].T[None]
    gs["ln1_g"], gs["ln1_b"] = jnp.stack(ln1_g), jnp.stack(ln1_b)
    gs["ln2_g"], gs["ln2_b"] = jnp.stack(ln2_g), jnp.stack(ln2_b)
    small_vec = _pack_small(gs, (0.5 / D) * jnp.sum(loss_parts))
    dw_in, *small_others = _matmul(x, dz0, ta=True, name="dw_in_e", M=D, N=1664, K=T, tm=1024, tn=1664,
                                   tk=DW_TOKENS // 2, plan=_plan_exchange_all(small_vec) if exchange else None)
    godd["w_in_e"] = jnp.concatenate([dw_in[:, :512], dw_in[:, 1536:1568], dw_in[:, 512:1536]], axis=1)
    grad_x = _matmul(dz0, w_in, tb=True, add=dr1, add_scale=ALPHA, name="dx", M=T, N=D, K=1664, tk=1664)[0]

    godd["w_qb"] = dwq.reshape(256, HEADS, 128)[:, :, :NOPE + ROPE].reshape(256, HEADS * (NOPE + ROPE))
    godd["w_kvb"] = jnp.concatenate([dwk.reshape(256, HEADS, 128)[:, :, :NOPE], dwv.reshape(256, HEADS, VDIM)],
                                    axis=2).reshape(256, HEADS * (NOPE + VDIM))
    small = (small_vec, small_others[0]) if exchange else gs
    return loss_parts, grad_x, g0, g1, godd, small


WEIGHTS = ['w_in_e', 'mla_gq', 'mla_gkv', 'w_qb', 'w_kvb', 'sgu_ln_g', 'sgu_ln_b', 'sgu_w', 'sgu_b', 'w_out_e',
           'w_in_o', 'hg_lb', 'hg_gnorm', 'w_out_o', 'ln1_g', 'ln1_b', 'w_ff1', 'w_ff2', 'ln2_g', 'ln2_b']


def kernel(x, positions, w_in_e, mla_gq, mla_gkv, w_qb, w_kvb, sgu_ln_g, sgu_ln_b, sgu_w, sgu_b, w_out_e, w_in_o, hg_lb, hg_gnorm, w_out_o, ln1_g, ln1_b, w_ff1, w_ff2, ln2_g, ln2_b, loss_target, m_w_in_e, m_mla_gq, m_mla_gkv, m_w_qb, m_w_kvb, m_sgu_ln_g, m_sgu_ln_b, m_sgu_w, m_sgu_b, m_w_out_e, m_w_in_o, m_hg_lb, m_hg_gnorm, m_w_out_o, m_ln1_g, m_ln1_b, m_w_ff1, m_w_ff2, m_ln2_g, m_ln2_b, v_w_in_e, v_mla_gq, v_mla_gkv, v_w_qb, v_w_kvb, v_sgu_ln_g, v_sgu_ln_b, v_sgu_w, v_sgu_b, v_w_out_e, v_w_in_o, v_hg_lb, v_hg_gnorm, v_w_out_o, v_ln1_g, v_ln1_b, v_w_ff1, v_w_ff2, v_ln2_g, v_ln2_b):
    args = dict(locals())
    w = {n: args[n] for n in WEIGHTS}
    m = {n: args["m_" + n] for n in WEIGHTS}
    v = {n: args["v_" + n] for n in WEIGHTS}
    cx, cy, cc = _mesh_pos()
    chip = 2 * cx + cy

    odd_shard = _odd_rows({"w_out_e": w_out_e[0], "w_in_e": w_in_e[0], "w_qb": w_qb[0], "w_kvb": w_kvb[0]}, BF16,
                          gnorm=hg_gnorm)
    ids = _mesh_ids()
    gathered = _run_plan(_plan_gather_ici([_place_shard(odd_shard, ids, name="place_shard_odd")]), name="odd_gather")[0]
    gathered = _run_plan(_plan_gather_forward([gathered]), name="odd_gather_forward")[0]
    per_chip = [_odd_unrows(gathered[j], with_gnorm=True) for j in range(4)]
    odd = {"w_out_e": jnp.concatenate([p["w_out_e"] for p in per_chip], axis=0)}
    for n in ("w_in_e", "w_qb", "w_kvb"):
        odd[n] = jnp.concatenate([p[n] for p in per_chip], axis=1)
    small = {n: w[n] for n, _ in SMALL if n != "hg_gnorm"}
    small["hg_gnorm"] = jnp.concatenate([p["hg_gnorm"] for p in per_chip], axis=1)
    shard_rows = (jnp.concatenate([w_ff1[0], w_ff2[0]], axis=0).astype(BF16),
                  jnp.concatenate([w_in_o[0], w_out_o[0]], axis=0).astype(BF16),
                  jnp.concatenate([w_ff1[1], w_ff2[1]], axis=0).astype(BF16))

    loss_parts, grad_x, g_l0, g_l1, godd, (small_vec, small_others) = _local_step(
        x[0], positions[0], loss_target[0], odd, shard_rows, small, True)

    by_chip = [_odd_rows({"w_out_e": jnp.split(godd["w_out_e"], 4, axis=0)[j],
                          **{n: jnp.split(godd[n], 4, axis=1)[j] for n in ("w_in_e", "w_qb", "w_kvb")}}, BF16)
               for j in range(4)]
    godd_buf = jnp.stack(by_chip)
    theirs = _run_plan(_plan_pair_swap(godd_buf), name="odd_pair_swap")[0]
    pair = _add_pairs(godd_buf, theirs, ids, name="odd_pair_add")
    parts = _run_plan(_plan_chip_scatter(pair), name="odd_chip_scatter")[0]
    g_odd = _run_plan(_plan_pair_gather(_sum_chips(pair, parts, ids, name="odd_chip_sum")), name="odd_pair_gather")[0]
    g_odd = _odd_unrows(g_odd.reshape(ROWS_ODD, 1024))

    small_sum = _sum_devices(small_vec, small_others, ids)
    loss = small_sum[-1, -1]
    g_small = _unpack_small(small_sum)
    grads = {n: g_small[n] for n, _ in SMALL if n != "hg_gnorm"}
    grads["hg_gnorm"] = lax.dynamic_slice_in_dim(g_small["hg_gnorm"], chip * 256, 256, axis=1)

    delta, new_m, new_v = {}, {}, {}
    for n, bufs_, row0 in (("w_ff1", [g_l0, g_l1], 0), ("w_ff2", [g_l0, g_l1], 1024), ("w_in_o", [g_l1], 2048),
                           ("w_out_o", [g_l1], 3072)):
        grads[n], delta[n], new_m[n], new_v[n] = _adamw_rows(w[n], m[n], v[n], bufs_, row0, name=f"adamw_{n}")
    for n, _ in ODD_PARTS:
        grads[n] = g_odd[n][None]
        d_, m_, v_ = _adamw(w[n][0], g_odd[n], m[n][0], v[n][0], name=f"adamw_{n}")
        delta[n], new_m[n], new_v[n] = d_[None], m_[None], v_[None]
    rest = [n for n in WEIGHTS if n not in delta]

    def pack_rest(d):
        flat = jnp.concatenate([d[n].reshape(-1) for n in rest])
        return jnp.pad(flat, (0, SMALL_ROWS * 1024 - flat.shape[0])).reshape(SMALL_ROWS, 1024)

    outs = _adamw(pack_rest(w), pack_rest(grads), pack_rest(m), pack_rest(v), name="adamw_small")
    for dst, packed in zip((delta, new_m, new_v), outs):
        flat, off = packed.reshape(-1), 0
        for n in rest:
            size = math.prod(w[n].shape)
            dst[n] = flat[off:off + size].reshape(w[n].shape)
            off += size

    return (loss, grad_x[None], *[grads[n] for n in WEIGHTS], *[delta[n] for n in WEIGHTS],
            *[new_m[n] for n in WEIGHTS], *[new_v[n] for n in WEIGHTS])
```

```python
import functools
import math

import jax
import jax.numpy as jnp
from jax import lax
from jax.experimental import pallas as pl
from jax.experimental.pallas import tpu as pltpu

F32 = jnp.float32
BF16 = jnp.bfloat16
MESH_IDS = pl.DeviceIdType.MESH

D = 1024
DEPTH = 2
HEADS = 8
NOPE, ROPE, VDIM = 64, 32, 64
QK_SCALE = (NOPE + ROPE) ** -0.5
ROPE_BASE = 10000.0
SGU_G, SGU_C = 4, 128
HG_CHUNK = 64
HG_HEADS_PER_STEP = 8
ALPHA = (2 * DEPTH) ** 0.25
EPS = 1e-5
LR, B1, B2, ADAM_EPS, WD, STEP = 0.001, 0.9, 0.999, 1e-08, 0.01, 10
GELU_C = math.sqrt(2.0 / math.pi)
GELU_A = 0.044715
HI = lax.Precision.HIGHEST
MB = 1024 * 1024
ROW_BLOCK = 512

NT_DIMS = (((1,), (1,)), ((), ()))
TN_DIMS = (((0,), (0,)), ((), ()))

SHARDED = (
    ("w_in_e", (1, 1024, 392), 2), ("w_qb", (1, 256, 192), 2), ("w_kvb", (1, 256, 256), 2),
    ("w_out_e", (1, 256, 1024), 1), ("w_in_o", (1, 1024, 1024), 2), ("w_out_o", (1, 256, 1024), 1),
    ("w_ff1", (2, 1024, 1024), 2), ("w_ff2", (2, 1024, 1024), 1), ("hg_gnorm", (1, 256), 1),
)
PACK_ROWS = 6144
HALF_ROWS = PACK_ROWS // 2
SMALL = (("mla_gq", (1, 256)), ("mla_gkv", (1, 256)), ("sgu_ln_g", (1, 512)), ("sgu_ln_b", (1, 512)),
         ("sgu_w", (1, 4, 128, 128)), ("sgu_b", (1, 4, 128)), ("hg_lb", (2, 1024)), ("hg_gnorm", (1, 1024)),
         ("ln1_g", (2, 1024)), ("ln1_b", (2, 1024)), ("ln2_g", (2, 1024)), ("ln2_b", (2, 1024)))
SMALL_ROWS = 80


def _params(vmem_mb, n_axes=0):
    kw = dict(vmem_limit_bytes=vmem_mb * MB)
    if n_axes:
        kw["dimension_semantics"] = ("arbitrary",) * n_axes
    return pltpu.CompilerParams(**kw)


_ANY = pl.BlockSpec(memory_space=pltpu.HBM)


def _mesh_pos():
    return lax.axis_index("x"), lax.axis_index("y"), lax.axis_index("c")


def _hbm(*arrays):
    return tuple(pltpu.with_memory_space_constraint(a, pltpu.HBM) if a.size >= 2 ** 18 else a for a in arrays)


class _Plan:
    def __init__(self, ins, outs, n_remote, n_local, start, wait, aliases=None):
        self.ins, self.outs, self.n_remote, self.n_local = list(ins), list(outs), n_remote, n_local
        self.start, self.wait, self.aliases = start, wait, dict(aliases or {})


def _join_plans(plans):
    ins, outs, aliases, parts = [], [], {}, []
    nr = nl = 0
    for p in plans:
        parts.append((p, len(ins), len(outs), nr, nl))
        aliases.update({len(ins) + i: len(outs) + o for i, o in p.aliases.items()})
        ins += p.ins
        outs += p.outs
        nr += p.n_remote
        nl += p.n_local

    def run(which):
        def go(in_refs, out_refs, send, recv, loc):
            for p, i0, o0, r0, l0 in parts:
                getattr(p, which)(in_refs[i0:i0 + len(p.ins)], out_refs[o0:o0 + len(p.outs)],
                                  lambda i, r0=r0: send(r0 + i), lambda i, r0=r0: recv(r0 + i),
                                  lambda i, l0=l0: loc(l0 + i))
        return go

    return _Plan(ins, outs, nr, nl, run("start"), run("wait"), aliases)


def _plan_io(plan, n_in, n_out):
    if plan is None:
        return [], [], [], [], {}
    sems = [pltpu.SemaphoreType.DMA((max(plan.n_remote, 1),)), pltpu.SemaphoreType.DMA((max(plan.n_remote, 1),)),
            pltpu.SemaphoreType.DMA((max(plan.n_local, 1),))]
    aliases = {n_in + i: n_out + o for i, o in plan.aliases.items()}
    return plan.ins, [_ANY] * len(plan.outs), plan.outs, sems, aliases


def _split_refs(refs, n_in, n_out, n_scr, plan):
    p_in, p_out = (len(plan.ins), len(plan.outs)) if plan is not None else (0, 0)
    refs = list(refs)
    ins, refs = refs[:n_in], refs[n_in:]
    pins, refs = refs[:p_in], refs[p_in:]
    outs, refs = refs[:n_out], refs[n_out:]
    pouts, refs = refs[:p_out], refs[p_out:]
    scr, psem = refs[:n_scr], refs[n_scr:]
    psem = tuple((lambda i, s=s: s.at[i]) for s in psem)
    return ins, outs, scr, (pins, pouts, psem)


def _grid_edge(grid, last):
    cond = None
    for ax, n in enumerate(grid):
        c = pl.program_id(ax) == (n - 1 if last else 0)
        cond = c if cond is None else cond & c
    return cond


def _plan_start(plan, pctx, grid):
    if plan is not None:
        pins, pouts, psem = pctx
        pl.when(_grid_edge(grid, False))(lambda: plan.start(pins, pouts, *psem))


def _plan_wait(plan, pctx, grid):
    if plan is not None:
        pins, pouts, psem = pctx
        pl.when(_grid_edge(grid, True))(lambda: plan.wait(pins, pouts, *psem))


def _run_plan(plan, *, name):
    def body(*refs):
        _, _, _, (pins, pouts, psem) = _split_refs(refs, 0, 0, 0, plan)
        plan.start(pins, pouts, *psem)
        plan.wait(pins, pouts, *psem)

    p_in, p_ospec, p_oshape, p_scr, p_alias = _plan_io(plan, 0, 0)
    return pl.pallas_call(body, name=name, in_specs=[_ANY] * len(p_in), out_specs=p_ospec, out_shape=p_oshape,
                          scratch_shapes=p_scr, input_output_aliases=p_alias)(*p_in)


def _fold8(x):
    return x.reshape(x.shape[0] // 8, 8, x.shape[1]).sum(axis=0)


def _ln_stats(r):
    mu = jnp.mean(r, -1, keepdims=True)
    xc = r - mu
    rstd = lax.rsqrt(jnp.mean(xc * xc, -1, keepdims=True) + EPS)
    return xc * rstd, rstd


def _sigmoid(x):
    return jax.nn.sigmoid(x)


def _gelu(x):
    return 0.5 * x * (1.0 + jnp.tanh(GELU_C * (x + GELU_A * x * x * x)))


def _gelu_grad(x):
    t = jnp.tanh(GELU_C * (x + GELU_A * x * x * x))
    return 0.5 * (1.0 + t) + 0.5 * x * (1.0 - t * t) * GELU_C * (1.0 + 3.0 * GELU_A * x * x)


MM_ROWS = 1024
DW_TOKENS = 2048


def _matmul(a, b, *, name, M, N, K, ta=False, tb=False, out_dtype=F32, tm=MM_ROWS, tn=1024, tk=1024,
            a_spec=None, b_spec=None, b_merge=None, out_shape=None, o_spec=None, into=None,
            a_sq=False, mul=None, add=None, add_scale=1.0, plan=None):
    tm, tn, tk = min(tm, M), min(tn, N), min(tk, K)
    assert M % tm == 0 and N % tn == 0 and K % tk == 0
    grid = (M // tm, N // tn, K // tk)
    nk = grid[2]
    if a_spec is None:
        a_spec = pl.BlockSpec((tk, tm), lambda i, j, k: (k, i)) if ta else pl.BlockSpec((tm, tk), lambda i, j, k: (i, k))
    if b_spec is None:
        b_spec = pl.BlockSpec((tn, tk), lambda i, j, k: (j, k)) if tb else pl.BlockSpec((tk, tn), lambda i, j, k: (k, j))
    if o_spec is None:
        o_spec = pl.BlockSpec((tm, tn), lambda i, j, k: (i, j))
        out_shape = jax.ShapeDtypeStruct((M, N), out_dtype)
    e_spec = pl.BlockSpec((tm, tn), lambda i, j, k: (i, j))
    dims = (((0 if ta else 1,), (1 if tb else 0,)), ((), ()))
    extra = [e for e in (mul, add, into) if e is not None]
    n_in = 2 + len(extra)

    def body(*refs):
        ins, outs, scr, pctx = _split_refs(refs, n_in, 1, 1 if nk > 1 else 0, plan)
        a_ref, b_ref = ins[0], ins[1]
        rest = list(ins[2:])
        mul_ref = rest.pop(0) if mul is not None else None
        add_ref = rest.pop(0) if add is not None else None
        o_ref = outs[0]
        _plan_start(plan, pctx, grid)
        av = a_ref[...].astype(BF16)
        if a_sq:
            av = av * av
        bv = b_ref[...]
        if b_merge is not None:
            bv = bv.reshape(b_merge)
        p = lax.dot_general(av, bv, dims, preferred_element_type=F32)

        def finish(r):
            if mul_ref is not None:
                r = r * (2.0 * mul_ref[...].astype(F32))
            if add_ref is not None:
                r = r + add_scale * add_ref[...]
            o_ref[...] = r.astype(o_ref.dtype)

        if nk == 1:
            finish(p)
        else:
            acc_ref = scr[0]
            k = pl.program_id(2)

            @pl.when(k == 0)
            def _():
                acc_ref[...] = p

            @pl.when(k > 0)
            def _():
                acc_ref[...] += p

            @pl.when(k == nk - 1)
            def _():
                finish(acc_ref[...])

        _plan_wait(plan, pctx, grid)

    p_in, p_ospec, p_oshape, p_scr, p_alias = _plan_io(plan, n_in, 1)
    aliases = dict(p_alias)
    if into is not None:
        aliases[n_in - 1] = 0
    return pl.pallas_call(
        body, name=name, grid=grid,
        in_specs=[a_spec, b_spec] + [e_spec] * (len(extra) - (into is not None)) + [_ANY] * (into is not None)
        + [_ANY] * len(p_in),
        out_specs=[o_spec] + p_ospec, out_shape=[out_shape] + p_oshape,
        scratch_shapes=([pltpu.VMEM((tm, tn), F32)] if nk > 1 else []) + p_scr,
        input_output_aliases=aliases, compiler_params=_params(48, 3),
    )(*_hbm(a, b, *extra), *p_in)


def _rows4_spec(rowblk, n_axes):
    return pl.BlockSpec((4, 256, D), lambda *_: (0, rowblk, 0))


def _proj_ln(a_b, w, h_prev, g, b, *, name, w_rowblk=None, plan=None):
    T = a_b.shape[0]
    tm = min(ROW_BLOCK, T)
    grid = (T // tm,)
    row = pl.BlockSpec((tm, D), lambda i: (i, 0))
    vec = pl.BlockSpec((1, D), lambda i: (0, 0))
    w_spec = pl.BlockSpec((D, D), lambda i: (0, 0)) if w_rowblk is None else _rows4_spec(w_rowblk, 1)

    def body(*refs):
        (a_ref, w_ref, h_ref, g_ref, b_ref), (r_ref, ho_ref, hb_ref), _, pctx = _split_refs(refs, 5, 3, 0, plan)
        _plan_start(plan, pctx, grid)
        mix = jnp.dot(a_ref[...], w_ref[...].reshape(D, D), preferred_element_type=F32)
        r = ALPHA * h_ref[...] + mix
        xhat, _ = _ln_stats(r)
        y = xhat * g_ref[...] + b_ref[...]
        r_ref[...] = r
        ho_ref[...] = y
        hb_ref[...] = y.astype(BF16)
        _plan_wait(plan, pctx, grid)

    p_in, p_ospec, p_oshape, p_scr, p_alias = _plan_io(plan, 5, 3)
    return pl.pallas_call(
        body, name=name, grid=grid,
        in_specs=[row, w_spec, row, vec, vec] + [_ANY] * len(p_in),
        out_specs=[row, row, row] + p_ospec,
        out_shape=[jax.ShapeDtypeStruct((T, D), F32), jax.ShapeDtypeStruct((T, D), F32),
                   jax.ShapeDtypeStruct((T, D), BF16)] + p_oshape,
        scratch_shapes=p_scr, input_output_aliases=p_alias, compiler_params=_params(40, 1),
    )(*_hbm(a_b, w, h_prev, g, b), *p_in)


def _ffn_ln(h_b, wbuf, h, g, b, *, name, plan=None):
    T = h_b.shape[0]
    tm, tf = min(ROW_BLOCK, T), 1024
    nf = 4
    F = nf * tf
    grid = (T // tm, nf)
    row = pl.BlockSpec((tm, D), lambda i, j: (i, 0))
    vec = pl.BlockSpec((1, D), lambda i, j: (0, 0))

    def body(*refs):
        ((hb_ref, w1_ref, w2_ref, h_ref, g_ref, b_ref), (ra_ref, r_ref, ho_ref, hbo_ref), (acc_ref,),
         pctx) = _split_refs(refs, 6, 4, 1, plan)
        _plan_start(plan, pctx, grid)
        j = pl.program_id(1)
        a = jnp.dot(hb_ref[...], w1_ref[...], preferred_element_type=F32)
        ra = jnp.maximum(a, 0.0)
        ra_ref[...] = ra.astype(BF16)
        p = jnp.dot((ra * ra).astype(BF16), w2_ref[...], preferred_element_type=F32)

        @pl.when(j == 0)
        def _():
            acc_ref[...] = p

        @pl.when(j > 0)
        def _():
            acc_ref[...] += p

        @pl.when(j == nf - 1)
        def _():
            r = ALPHA * h_ref[...] + acc_ref[...]
            xhat, _ = _ln_stats(r)
            y = xhat * g_ref[...] + b_ref[...]
            r_ref[...] = r
            ho_ref[...] = y
            hbo_ref[...] = y.astype(BF16)

        _plan_wait(plan, pctx, grid)

    p_in, p_ospec, p_oshape, p_scr, p_alias = _plan_io(plan, 6, 4)
    return pl.pallas_call(
        body, name=name, grid=grid,
        in_specs=[row, pl.BlockSpec((None, D, tf), lambda i, j: (j, 0, 0)),
                  pl.BlockSpec((None, tf, D), lambda i, j: (j, 1, 0)), row, vec, vec] + [_ANY] * len(p_in),
        out_specs=[pl.BlockSpec((tm, tf), lambda i, j: (i, j)), row, row, row] + p_ospec,
        out_shape=[jax.ShapeDtypeStruct((T, F), BF16), jax.ShapeDtypeStruct((T, D), F32),
                   jax.ShapeDtypeStruct((T, D), F32), jax.ShapeDtypeStruct((T, D), BF16)] + p_oshape,
        scratch_shapes=[pltpu.VMEM((tm, D), F32)] + p_scr,
        input_output_aliases=p_alias, compiler_params=_params(48, 2),
    )(*_hbm(h_b, wbuf, wbuf, h, g, b), *p_in)


def _ln_bwd(dy, r, g, *, name, tgt=None):
    T = dy.shape[0]
    tm = min(ROW_BLOCK, T)
    row = pl.BlockSpec((tm, D), lambda i: (i, 0))
    acc = pl.BlockSpec((8, D), lambda i: (0, 0))
    n_in = 3 + (tgt is not None)

    def body(*refs):
        dy_ref, r_ref, g_ref = refs[:3]
        dr_ref, drb_ref, dg_ref, db_ref = refs[n_in:n_in + 4]

        @pl.when(pl.program_id(0) == 0)
        def _():
            for ref in refs[n_in + 2:]:
                ref[...] = jnp.zeros_like(ref)

        dy_ = dy_ref[...]
        if tgt is not None:
            err = dy_ - refs[3][...]
            refs[n_in + 4][...] += _fold8(err * err)
            dy_ = err * (1.0 / D)
        xhat, rstd = _ln_stats(r_ref[...])
        dxh = dy_ * g_ref[...]
        m1 = jnp.mean(dxh, -1, keepdims=True)
        m2 = jnp.mean(dxh * xhat, -1, keepdims=True)
        dr = rstd * (dxh - m1 - xhat * m2)
        dr_ref[...] = dr
        drb_ref[...] = dr.astype(BF16)
        dg_ref[...] += _fold8(dy_ * xhat)
        db_ref[...] += _fold8(dy_)

    extra = [] if tgt is None else [tgt]
    return pl.pallas_call(
        body, name=name, grid=(T // tm,),
        in_specs=[row, row, pl.BlockSpec((1, D), lambda i: (0, 0))] + [row] * len(extra),
        out_specs=[row, row, acc, acc] + [acc] * len(extra),
        out_shape=[jax.ShapeDtypeStruct((T, D), F32), jax.ShapeDtypeStruct((T, D), BF16)]
        + [jax.ShapeDtypeStruct((8, D), F32)] * (2 + len(extra)),
        compiler_params=_params(40, 1),
    )(*_hbm(dy, r, g, *extra))


def _rope(x, c, s1, s2):
    return x * c + pltpu.roll(x, 112, 1) * s1 + pltpu.roll(x, 16, 1) * s2


def _rope_t(dy, c, s1, s2):
    return dy * c + pltpu.roll(dy * s1, 16, 1) + pltpu.roll(dy * s2, 112, 1)


def _rms(x, g):
    rstd = lax.rsqrt(jnp.mean(x * x, -1, keepdims=True) + EPS)
    xhat = x * rstd
    return xhat * g, xhat, rstd


def _mla_prep(z0, gq, gkv, wq, wk, wv, rc, rs1, rs2):
    T = z0.shape[0]
    tm = min(ROW_BLOCK, T)
    HW = HEADS * 128

    def body(cq_ref, ckv_ref, kr_ref, gq_ref, gkv_ref, wq_ref, wk_ref, wv_ref, c_ref, s1_ref, s2_ref,
             q_ref, k_ref, v_ref):
        nq = _rms(cq_ref[...], gq_ref[...])[0].astype(BF16)
        nkv = _rms(ckv_ref[...], gkv_ref[...])[0].astype(BF16)
        q = jnp.dot(nq, wq_ref[...], preferred_element_type=F32)
        k = jnp.dot(nkv, wk_ref[...], preferred_element_type=F32)
        v = jnp.dot(nkv, wv_ref[...], preferred_element_type=F32)
        c, s1, s2 = c_ref[...], s1_ref[...], s2_ref[...]
        kr = _rope(pltpu.roll(kr_ref[...], 64, 1), c, s1, s2)
        for h in range(HEADS):
            sl = slice(h * 128, (h + 1) * 128)
            q_ref[:, sl] = (_rope(q[:, sl], c, s1, s2) * QK_SCALE).astype(BF16)
            k_ref[:, sl] = (k[:, sl] + kr).astype(BF16)
        v_ref[...] = v.astype(BF16)

    full = lambda shape: pl.BlockSpec(shape, lambda i: (0, 0))
    tab = pl.BlockSpec((tm, 128), lambda i: (i, 0))
    return pl.pallas_call(
        body, name="mla_prep", grid=(T // tm,),
        in_specs=[pl.BlockSpec((tm, 256), lambda i: (i, 0)), pl.BlockSpec((tm, 256), lambda i: (i, 1)),
                  pl.BlockSpec((tm, 128), lambda i: (i, 12)), full((1, 256)), full((1, 256)),
                  full((256, HW)), full((256, HW)), full((256, 512)), tab, tab, tab],
        out_specs=[pl.BlockSpec((tm, HW), lambda i: (i, 0)), pl.BlockSpec((tm, HW), lambda i: (i, 0)),
                   pl.BlockSpec((tm, 512), lambda i: (i, 0))],
        out_shape=[jax.ShapeDtypeStruct((T, HW), BF16), jax.ShapeDtypeStruct((T, HW), BF16),
                   jax.ShapeDtypeStruct((T, 512), BF16)],
        compiler_params=_params(40, 1),
    )(z0, z0, z0, gq, gkv, wq, wk, wv, rc, rs1, rs2)


def _flash_fwd(q, k, v, plan=None):
    T = q.shape[0]
    bq = min(2 * ROW_BLOCK, T)
    nq = T // bq
    grid = (4, nq, nq)

    def body(*refs):
        (q_ref, k_ref, v_ref), (o_ref, lse_ref), (m_sc, acc_sc), pctx = _split_refs(refs, 3, 2, 2, plan)
        _plan_start(plan, pctx, grid)
        i, j = pl.program_id(1), pl.program_id(2)
        first = lax.broadcasted_iota(jnp.int32, (bq, 128), 1) < 64

        @pl.when(j == 0)
        def _():
            m_sc[...] = jnp.full_like(m_sc, -jnp.inf)
            acc_sc[...] = jnp.zeros_like(acc_sc)

        def step(masked):
            vp = v_ref[...]
            for h in range(2):
                sl = slice(h * 128, (h + 1) * 128)
                s = lax.dot_general(q_ref[:, sl], k_ref[:, sl], NT_DIMS, preferred_element_type=F32)
                if masked:
                    rows = lax.broadcasted_iota(jnp.int32, (bq, bq), 0)
                    cols = lax.broadcasted_iota(jnp.int32, (bq, bq), 1)
                    s = jnp.where(cols <= rows, s, -jnp.inf)
                m_prev = m_sc[h, :, 0:1]
                m_new = jnp.maximum(m_prev, jnp.max(s, axis=1, keepdims=True))
                alpha = jnp.exp(m_prev - m_new)
                p = jnp.exp(s - m_new).astype(BF16)
                mine = first if h == 0 else jnp.logical_not(first)
                vh = jnp.where(mine, vp, jnp.ones_like(vp))
                acc_sc[h] = acc_sc[h] * alpha + jnp.dot(p, vh, preferred_element_type=F32)
                m_sc[h] = jnp.broadcast_to(m_new, (bq, 128))

        @pl.when(j < i)
        def _():
            step(False)

        @pl.when(j == i)
        def _():
            step(True)
            a0, a1 = acc_sc[0], acc_sc[1]
            l0, l1 = pltpu.roll(a0, 64, 1), pltpu.roll(a1, 64, 1)
            o_ref[...] = jnp.where(first, a0 / l0, a1 / l1).astype(BF16)
            lse_ref[...] = jnp.where(first, m_sc[0] + jnp.log(l0), m_sc[1] + jnp.log(l1))

        _plan_wait(plan, pctx, grid)

    kv = lambda hp, i, j: (jnp.minimum(i, j), hp)
    p_in, p_ospec, p_oshape, p_scr, p_alias = _plan_io(plan, 3, 2)
    return pl.pallas_call(
        body, name="flash_fwd", grid=grid,
        in_specs=[pl.BlockSpec((bq, 256), lambda hp, i, j: (i, hp)), pl.BlockSpec((bq, 256), kv),
                  pl.BlockSpec((bq, 128), kv)] + [_ANY] * len(p_in),
        out_specs=[pl.BlockSpec((bq, 128), lambda hp, i, j: (i, hp)),
                   pl.BlockSpec((bq, 128), lambda hp, i, j: (i, hp))] + p_ospec,
        out_shape=[jax.ShapeDtypeStruct((T, 512), BF16), jax.ShapeDtypeStruct((T, 512), F32)] + p_oshape,
        scratch_shapes=[pltpu.VMEM((2, bq, 128), F32), pltpu.VMEM((2, bq, 128), F32)] + p_scr,
        input_output_aliases=p_alias, compiler_params=_params(56, 3),
    )(*_hbm(q, k, v), *p_in)


def _attn_delta(dmix, o):
    T = o.shape[0]
    tm = min(ROW_BLOCK, T)
    blk = pl.BlockSpec((tm, 512), lambda i: (i, 0))

    def body(do_ref, o_ref, delta_ref, dob_ref):
        first = lax.broadcasted_iota(jnp.int32, (tm, 128), 1) < 64
        for hp in range(4):
            sl = slice(hp * 128, (hp + 1) * 128)
            prod = do_ref[:, sl] * o_ref[:, sl].astype(F32)
            d0 = jnp.sum(jnp.where(first, prod, 0.0), axis=1, keepdims=True)
            d1 = jnp.sum(jnp.where(first, 0.0, prod), axis=1, keepdims=True)
            delta_ref[:, sl] = jnp.where(first, d0, d1)
        dob_ref[...] = do_ref[...].astype(BF16)

    return pl.pallas_call(
        body, name="attn_delta", grid=(T // tm,), in_specs=[blk, blk], out_specs=[blk, blk],
        out_shape=[jax.ShapeDtypeStruct((T, 512), F32), jax.ShapeDtypeStruct((T, 512), BF16)],
        compiler_params=_params(32, 1),
    )(dmix, o)


def _flash_bwd(q, k, v, do_b, lse, delta, plan=None):
    T = q.shape[0]
    bq = min(2 * ROW_BLOCK, T)
    nq = T // bq
    grid = (4, nq, nq)

    def body(*refs):
        ((q_ref, k_ref, v_ref, do_ref, lse_ref, dl_ref), (dq_hbm, dk_ref, dv_ref), (dq_sc, dk_sc, dv_sc, sem),
         pctx) = _split_refs(refs, 6, 3, 4, plan)
        _plan_start(plan, pctx, grid)
        hp, j, i = pl.program_id(0), pl.program_id(1), pl.program_id(2)
        first = lax.broadcasted_iota(jnp.int32, (bq, 128), 1) < 64

        @pl.when((j == 0) & (i == 0))
        def _():
            dq_sc[...] = jnp.zeros_like(dq_sc)

        @pl.when(i == j)
        def _():
            dk_sc[...] = jnp.zeros_like(dk_sc)
            dv_sc[...] = jnp.zeros_like(dv_sc)

        def step(masked):
            vp = v_ref[...]
            do = do_ref[...]
            for h in range(2):
                sl = slice(h * 128, (h + 1) * 128)
                qh, kh = q_ref[:, sl], k_ref[:, sl]
                s = lax.dot_general(qh, kh, NT_DIMS, preferred_element_type=F32)
                p = jnp.exp(s - lse_ref[:, h * 64:h * 64 + 1])
                if masked:
                    rows = lax.broadcasted_iota(jnp.int32, (bq, bq), 0)
                    cols = lax.broadcasted_iota(jnp.int32, (bq, bq), 1)
                    p = jnp.where(cols <= rows, p, 0.0)
                mine = first if h == 0 else jnp.logical_not(first)
                do_h = jnp.where(mine, do, jnp.zeros_like(do))
                dv_sc[...] += lax.dot_general(p.astype(BF16), do_h, TN_DIMS, preferred_element_type=F32)
                dp = lax.dot_general(do_h, vp, NT_DIMS, preferred_element_type=F32)
                ds = (p * (dp - dl_ref[:, h * 64:h * 64 + 1])).astype(BF16)
                dq_sc[i, :, sl] += jnp.dot(ds, kh, preferred_element_type=F32)
                dk_sc[:, sl] += lax.dot_general(ds, qh, TN_DIMS, preferred_element_type=F32)

        @pl.when(i > j)
        def _():
            step(False)

        @pl.when(i == j)
        def _():
            step(True)

        @pl.when(i == nq - 1)
        def _():
            dk_ref[...] = dk_sc[...]
            dv_ref[...] = dv_sc[...]

        @pl.when((j == nq - 1) & (i == nq - 1))
        def _():
            cp = pltpu.make_async_copy(dq_sc, dq_hbm.at[hp], sem)
            cp.start()
            cp.wait()

        _plan_wait(plan, pctx, grid)

    qi = lambda hp, j, i: (jnp.maximum(i, j), hp)
    kj = lambda hp, j, i: (j, hp)
    p_in, p_ospec, p_oshape, p_scr, p_alias = _plan_io(plan, 6, 3)
    return pl.pallas_call(
        body, name="flash_bwd", grid=grid,
        in_specs=[pl.BlockSpec((bq, 256), qi), pl.BlockSpec((bq, 256), kj), pl.BlockSpec((bq, 128), kj),
                  pl.BlockSpec((bq, 128), qi), pl.BlockSpec((bq, 128), qi), pl.BlockSpec((bq, 128), qi)]
        + [_ANY] * len(p_in),
        out_specs=[_ANY, pl.BlockSpec((bq, 256), kj), pl.BlockSpec((bq, 128), kj)] + p_ospec,
        out_shape=[jax.ShapeDtypeStruct((4, nq, bq, 256), F32), jax.ShapeDtypeStruct((T, 1024), F32),
                   jax.ShapeDtypeStruct((T, 512), F32)] + p_oshape,
        scratch_shapes=[pltpu.VMEM((nq, bq, 256), F32), pltpu.VMEM((bq, 256), F32), pltpu.VMEM((bq, 128), F32),
                        pltpu.SemaphoreType.DMA] + p_scr,
        input_output_aliases=p_alias, compiler_params=_params(56, 3),
    )(*_hbm(q, k, v, do_b, lse, delta), *p_in)


def _mla_bwd(z0, dq4, dk, dv, gq, gkv, wq, wk, wv, rc, rs1, rs2, plan=None):
    T = z0.shape[0]
    tm = min(ROW_BLOCK, T)
    HW = HEADS * 128
    grid = (T // tm,)
    dq4 = dq4.reshape(4, T, 256)

    def body(*refs):
        ((cq_ref, ckv_ref, dq_ref, dk_ref, dv_ref, gq_ref, gkv_ref, wq_ref, wk_ref, wv_ref, c_ref, s1_ref, s2_ref),
         (dc_ref, dkr_ref, dwq_ref, dwk_ref, dwv_ref, dgq_ref, dgkv_ref), _, pctx) = _split_refs(refs, 13, 7, 0, plan)
        _plan_start(plan, pctx, grid)

        @pl.when(pl.program_id(0) == 0)
        def _():
            for ref in (dwq_ref, dwk_ref, dwv_ref, dgq_ref, dgkv_ref):
                ref[...] = jnp.zeros_like(ref)

        c, s1, s2 = c_ref[...], s1_ref[...], s2_ref[...]
        lane = lax.broadcasted_iota(jnp.int32, (tm, 128), 1)
        nq, xq, rq = _rms(cq_ref[...], gq_ref[...])
        nkv, xkv, rkv = _rms(ckv_ref[...], gkv_ref[...])
        nq_b, nkv_b = nq.astype(BF16), nkv.astype(BF16)

        dq_parts, dk_parts = [], []
        dkr = jnp.zeros((tm, 128), F32)
        for h in range(HEADS):
            blk = dq_ref[h // 2, :, (h % 2) * 128:(h % 2 + 1) * 128] * QK_SCALE
            dq_parts.append(_rope_t(blk, c, s1, s2).astype(BF16))
            kb = dk_ref[:, h * 128:(h + 1) * 128]
            dk_parts.append(jnp.where(lane < NOPE, kb, 0.0).astype(BF16))
            dkr = dkr + kb
        dq_b = jnp.concatenate(dq_parts, axis=1)
        dk_b = jnp.concatenate(dk_parts, axis=1)
        dv_b = dv_ref[...].astype(BF16)

        dwq_ref[...] += lax.dot_general(nq_b, dq_b, TN_DIMS, preferred_element_type=F32)
        dwk_ref[...] += lax.dot_general(nkv_b, dk_b, TN_DIMS, preferred_element_type=F32)
        dwv_ref[...] += lax.dot_general(nkv_b, dv_b, TN_DIMS, preferred_element_type=F32)
        dnq = lax.dot_general(dq_b, wq_ref[...], NT_DIMS, preferred_element_type=F32)
        dnkv = (lax.dot_general(dk_b, wk_ref[...], NT_DIMS, preferred_element_type=F32)
                + lax.dot_general(dv_b, wv_ref[...], NT_DIMS, preferred_element_type=F32))

        def rms_bwd(dn, xhat, rstd, g):
            dxh = dn * g
            return rstd * (dxh - xhat * jnp.mean(dxh * xhat, -1, keepdims=True))

        dc_ref[:, :256] = rms_bwd(dnq, xq, rq, gq_ref[...]).astype(BF16)
        dc_ref[:, 256:] = rms_bwd(dnkv, xkv, rkv, gkv_ref[...]).astype(BF16)
        dgq_ref[...] += _fold8(dnq * xq)
        dgkv_ref[...] += _fold8(dnkv * xkv)
        dkr = pltpu.roll(_rope_t(dkr, c, s1, s2), 64, 1)
        dkr_ref[...] = jnp.where(lane < ROPE, dkr, 0.0).astype(BF16)
        _plan_wait(plan, pctx, grid)

    full = lambda shape: pl.BlockSpec(shape, lambda i: (0,) * len(shape))
    tab = pl.BlockSpec((tm, 128), lambda i: (i, 0))
    p_in, p_ospec, p_oshape, p_scr, p_alias = _plan_io(plan, 13, 7)
    return pl.pallas_call(
        body, name="mla_bwd", grid=grid,
        in_specs=[pl.BlockSpec((tm, 256), lambda i: (i, 0)), pl.BlockSpec((tm, 256), lambda i: (i, 1)),
                  pl.BlockSpec((4, tm, 256), lambda i: (0, i, 0)),
                  pl.BlockSpec((tm, HW), lambda i: (i, 0)), pl.BlockSpec((tm, 512), lambda i: (i, 0)),
                  full((1, 256)), full((1, 256)), full((256, HW)), full((256, HW)), full((256, 512)), tab, tab, tab]
        + [_ANY] * len(p_in),
        out_specs=[pl.BlockSpec((tm, 512), lambda i: (i, 0)), tab, full((256, HW)), full((256, HW)),
                   full((256, 512)), full((8, 256)), full((8, 256))] + p_ospec,
        out_shape=[jax.ShapeDtypeStruct((T, 512), BF16), jax.ShapeDtypeStruct((T, 128), BF16),
                   jax.ShapeDtypeStruct((256, HW), F32), jax.ShapeDtypeStruct((256, HW), F32),
                   jax.ShapeDtypeStruct((256, 512), F32), jax.ShapeDtypeStruct((8, 256), F32),
                   jax.ShapeDtypeStruct((8, 256), F32)] + p_oshape,
        scratch_shapes=p_scr, input_output_aliases=p_alias, compiler_params=_params(48, 1),
    )(*_hbm(z0, z0, dq4, dk, dv, gq, gkv, wq, wk, wv, rc, rs1, rs2), *p_in)


def _sgu_fwd(z0, a_out, ln_g, ln_b, w, b_t):
    T = z0.shape[0]
    tm = min(ROW_BLOCK, T)
    W = SGU_G * SGU_C

    def body(u_ref, v_ref, a_ref, g_ref, b_ref, w_ref, bt_ref, o_ref):
        o_ref[:, :W] = a_ref[...]
        ug = _gelu(u_ref[...])
        xhat, _ = _ln_stats(_gelu(v_ref[...]))
        vn = (xhat * g_ref[...] + b_ref[...]).astype(BF16)
        tril = lax.broadcasted_iota(jnp.int32, (SGU_C, SGU_C), 0) >= lax.broadcasted_iota(jnp.int32, (SGU_C, SGU_C), 1)
        for g in range(SGU_G):
            cs = slice(g * SGU_C, (g + 1) * SGU_C)
            wg = jnp.where(tril, w_ref[g], 0.0).astype(BF16)
            bcol = bt_ref[:, g:g + 1]
            for c in range(tm // SGU_C):
                rs = slice(c * SGU_C, (c + 1) * SGU_C)
                mixed = jnp.dot(wg, vn[rs, cs], preferred_element_type=F32) + bcol
                o_ref[rs, W + g * SGU_C:W + (g + 1) * SGU_C] = (ug[rs, cs] * mixed).astype(BF16)

    full = lambda shape: pl.BlockSpec(shape, lambda i: (0,) * len(shape))
    return pl.pallas_call(
        body, name="sgu_fwd", grid=(T // tm,),
        in_specs=[pl.BlockSpec((tm, W), lambda i: (i, 1)), pl.BlockSpec((tm, W), lambda i: (i, 2)),
                  pl.BlockSpec((tm, W), lambda i: (i, 0)),
                  full((1, W)), full((1, W)), full((SGU_G, SGU_C, SGU_C)), full((SGU_C, SGU_G))],
        out_specs=pl.BlockSpec((tm, 2 * W), lambda i: (i, 0)),
        out_shape=jax.ShapeDtypeStruct((T, 2 * W), BF16),
        compiler_params=_params(32, 1),
    )(z0, z0, a_out, ln_g, ln_b, w, b_t)


def _sgu_bwd(z0, dmix, dc, dkr, ln_g, ln_b, w, b_t):
    T = z0.shape[0]
    tm = min(ROW_BLOCK, T)
    W = SGU_G * SGU_C

    def body(u_ref, v_ref, do_ref, dc_ref, dkr_ref, g_ref, b_ref, w_ref, bt_ref, dz_ref, dw_ref, db_ref, dlg_ref,
             dlb_ref):
        @pl.when(pl.program_id(0) == 0)
        def _():
            for ref in (dw_ref, db_ref, dlg_ref, dlb_ref):
                ref[...] = jnp.zeros_like(ref)

        dz_ref[:, :W] = dc_ref[...]
        dz_ref[:, 3 * W:] = dkr_ref[...]

        u, v, dout = u_ref[...], v_ref[...], do_ref[...]
        ug = _gelu(u)
        xhat, rstd = _ln_stats(_gelu(v))
        vn = (xhat * g_ref[...] + b_ref[...]).astype(BF16)
        dmixed = dout * ug
        dmixed_b = dmixed.astype(BF16)
        tril = lax.broadcasted_iota(jnp.int32, (SGU_C, SGU_C), 0) >= lax.broadcasted_iota(jnp.int32, (SGU_C, SGU_C), 1)
        lane = lax.broadcasted_iota(jnp.int32, (SGU_C, SGU_C), 1)
        dvn_cols = []
        for g in range(SGU_G):
            cs = slice(g * SGU_C, (g + 1) * SGU_C)
            wg = jnp.where(tril, w_ref[g], 0.0).astype(BF16)
            bcol = bt_ref[:, g:g + 1]
            dw_g = jnp.zeros((SGU_C, SGU_C), F32)
            db_g = jnp.zeros((SGU_C, 1), F32)
            dvn_rows = []
            for c in range(tm // SGU_C):
                rs = slice(c * SGU_C, (c + 1) * SGU_C)
                mixed = jnp.dot(wg, vn[rs, cs], preferred_element_type=F32) + bcol
                dz_ref[rs, W + g * SGU_C:W + (g + 1) * SGU_C] = (dout[rs, cs] * mixed * _gelu_grad(u[rs, cs])).astype(BF16)
                dm = dmixed_b[rs, cs]
                dw_g = dw_g + lax.dot_general(dm, vn[rs, cs], NT_DIMS, preferred_element_type=F32)
                db_g = db_g + jnp.sum(dmixed[rs, cs], axis=1, keepdims=True)
                dvn_rows.append(lax.dot_general(wg, dm, TN_DIMS, preferred_element_type=F32))
            dw_ref[g] += jnp.where(tril, dw_g, 0.0)
            db_ref[...] += jnp.where(lane == g, db_g, 0.0)
            dvn_cols.append(jnp.concatenate(dvn_rows, axis=0))
        dvn = jnp.concatenate(dvn_cols, axis=1)
        dxh = dvn * g_ref[...]
        m1 = jnp.mean(dxh, -1, keepdims=True)
        m2 = jnp.mean(dxh * xhat, -1, keepdims=True)
        dvg = rstd * (dxh - m1 - xhat * m2)
        dz_ref[:, 2 * W:3 * W] = (dvg * _gelu_grad(v)).astype(BF16)
        dlg_ref[...] += _fold8(dvn * xhat)
        dlb_ref[...] += _fold8(dvn)

    full = lambda shape: pl.BlockSpec(shape, lambda i: (0,) * len(shape))
    return pl.pallas_call(
        body, name="sgu_bwd", grid=(T // tm,),
        in_specs=[pl.BlockSpec((tm, W), lambda i: (i, 1)), pl.BlockSpec((tm, W), lambda i: (i, 2)),
                  pl.BlockSpec((tm, W), lambda i: (i, 1)), pl.BlockSpec((tm, W), lambda i: (i, 0)),
                  pl.BlockSpec((tm, 128), lambda i: (i, 0)),
                  full((1, W)), full((1, W)), full((SGU_G, SGU_C, SGU_C)), full((SGU_C, SGU_G))],
        out_specs=[pl.BlockSpec((tm, 3 * W + 128), lambda i: (i, 0)), full((SGU_G, SGU_C, SGU_C)),
                   full((SGU_C, SGU_C)), full((8, W)), full((8, W))],
        out_shape=[jax.ShapeDtypeStruct((T, 3 * W + 128), BF16), jax.ShapeDtypeStruct((SGU_G, SGU_C, SGU_C), F32),
                   jax.ShapeDtypeStruct((SGU_C, SGU_C), F32), jax.ShapeDtypeStruct((8, W), F32),
                   jax.ShapeDtypeStruct((8, W), F32)],
        compiler_params=_params(40, 1),
    )(z0, z0, dmix, dc, dkr, ln_g, ln_b, w, b_t)


def _hg_lower_bound(lb_ref):
    a0, a1 = lb_ref[0:1, :], lb_ref[1:2, :]
    m = jnp.maximum(a0, a1)
    e0, e1 = jnp.exp(a0 - m), jnp.exp(a1 - m)
    return e1 / (e0 + e1)


def _running_sum(x, reverse=False):
    n = x.shape[0]
    row = lax.broadcasted_iota(jnp.int32, x.shape, 0)
    s = 1
    while s < n:
        if reverse:
            x = x + jnp.where(row < n - s, pltpu.roll(x, n - s, 0), 0.0)
        else:
            x = x + jnp.where(row >= s, pltpu.roll(x, s, 0), 0.0)
        s *= 2
    return x


def _hg_chunk(qc, fc, lb):
    C = HG_CHUNK
    rows = lax.broadcasted_iota(jnp.int32, (C, C), 0)
    cols = lax.broadcasted_iota(jnp.int32, (C, C), 1)
    rowid = lax.broadcasted_iota(jnp.int32, (C, 128), 0)
    sq, sg = _sigmoid(qc), _sigmoid(fc)
    qf = qc * sq
    gate = lb + (1.0 - lb) * sg
    kk = 1.0 - gate
    lg = jnp.log(gate)
    bcum = _running_sum(lg)
    b_mid = jnp.sum(jnp.where(rowid < C // 2, lg, 0.0), axis=0, keepdims=True)
    b_last = jnp.sum(lg, axis=0, keepdims=True)
    eq, ek, e, eh = jnp.exp(bcum - b_mid), jnp.exp(b_mid - bcum), jnp.exp(bcum), jnp.exp(b_last - bcum)
    qt, kt, qe, khat = qf * eq, kk * ek, qf * e, kk * eh
    a = lax.dot_general(qt.astype(BF16), kt.astype(BF16), NT_DIMS, preferred_element_type=F32)
    a = jnp.where(rows >= cols, a, 0.0)
    return dict(sq=sq, sg=sg, gate=gate, kk=kk, eq=eq, ek=ek, e=e, eh=eh, qt=qt, kt=kt, qe=qe, khat=khat, a=a,
                e_last=jnp.exp(b_last), tril=rows >= cols, rowid=rowid)


def _hgrn_fwd(z4, hg_lb, gnorm):
    T = z4.shape[1]
    tb = min(ROW_BLOCK, T)
    C = HG_CHUNK
    ncb = tb // C
    HPB = HG_HEADS_PER_STEP

    def body(q_ref, f_ref, i_ref, g_ref, lb_ref, gn_ref, y_ref, o_ref, st_ref, st_sc):
        @pl.when(pl.program_id(1) == 0)
        def _():
            st_sc[...] = jnp.zeros_like(st_sc)

        def chunk(c, carry):
            rs = pl.ds(pl.multiple_of(c * C, C), C)
            for hh in range(HPB):
                hs = slice(hh * 128, (hh + 1) * 128)
                lb = _hg_lower_bound(lb_ref.at[:, hs])
                v_b = i_ref[rs, hs].astype(BF16)
                gc = g_ref[rs, hs]
                x = _hg_chunk(q_ref[rs, hs], f_ref[rs, hs], lb)
                st = st_sc[hh]
                st_ref[hh, c] = st
                o = (jnp.dot(x["a"].astype(BF16), v_b, preferred_element_type=F32)
                     + lax.dot_general(x["qe"].astype(BF16), st.astype(BF16), NT_DIMS, preferred_element_type=F32))
                st_sc[hh] = st * x["e_last"] + lax.dot_general(v_b, x["khat"].astype(BF16), TN_DIMS,
                                                               preferred_element_type=F32)
                o_ref[rs, hs] = o
                n = o * lax.rsqrt(jnp.mean(o * o, -1, keepdims=True) + EPS)
                y_ref[rs, hs] = (n * gn_ref[:, hs] * (gc * _sigmoid(gc))).astype(BF16)
            return carry

        lax.fori_loop(0, ncb, chunk, 0)

    W = 128 * HPB
    zb = lambda k: pl.BlockSpec((None, tb, W), lambda h, t: (k, t, h))
    out = pl.BlockSpec((tb, W), lambda h, t: (t, h))
    return pl.pallas_call(
        body, name="hgrn_fwd", grid=(HEADS // HPB, T // tb),
        in_specs=[zb(0), zb(1), zb(2), zb(3), pl.BlockSpec((2, W), lambda h, t: (0, h)),
                  pl.BlockSpec((1, W), lambda h, t: (0, h))],
        out_specs=[out, out, pl.BlockSpec((HPB, ncb, 128, 128), lambda h, t: (h, t, 0, 0))],
        out_shape=[jax.ShapeDtypeStruct((T, D), BF16), jax.ShapeDtypeStruct((T, D), F32),
                   jax.ShapeDtypeStruct((HEADS, T // C, 128, 128), F32)],
        scratch_shapes=[pltpu.VMEM((HPB, 128, 128), F32)],
        compiler_params=_params(48, 2),
    )(*_hbm(z4, z4, z4, z4, hg_lb, gnorm))


def _hgrn_bwd(z4, o_raw, dy, states, hg_lb, gnorm):
    T = z4.shape[1]
    tb = min(ROW_BLOCK, T)
    C = HG_CHUNK
    ncb = tb // C
    nt = T // tb
    HPB = HG_HEADS_PER_STEP

    def body(q_ref, f_ref, i_ref, g_ref, o_ref, dy_ref, st_ref, lb_ref, gn_ref, dz_ref, dlb_ref, dgn_ref, dst_sc):
        @pl.when(pl.program_id(1) == 0)
        def _():
            dst_sc[...] = jnp.zeros_like(dst_sc)
            dlb_ref[...] = jnp.zeros_like(dlb_ref)
            dgn_ref[...] = jnp.zeros_like(dgn_ref)

        def chunk(cc, carry):
            for hh in range(HPB):
                one_head(ncb - 1 - cc, hh, slice(hh * 128, (hh + 1) * 128))
            return carry

        def one_head(c, hh, hs):
            rs = pl.ds(pl.multiple_of(c * C, C), C)
            lb = _hg_lower_bound(lb_ref.at[:, hs])
            gn = gn_ref[:, hs]
            qc, gc = q_ref[rs, hs], g_ref[rs, hs]
            v_b = i_ref[rs, hs].astype(BF16)
            x = _hg_chunk(qc, f_ref[rs, hs], lb)
            st, dst = st_ref[hh, c], dst_sc[hh]
            st_b, dst_b = st.astype(BF16), dst.astype(BF16)
            o, dyc = o_ref[rs, hs], dy_ref[rs, hs]
            sgg = _sigmoid(gc)
            sil = gc * sgg
            rstd = lax.rsqrt(jnp.mean(o * o, -1, keepdims=True) + EPS)
            n = o * rstd
            dgn_ref[:, hs] += _fold8(dyc * n * sil)
            dn = dyc * gn * sil
            do = rstd * (dn - n * jnp.mean(dn * n, -1, keepdims=True))
            dg = dyc * n * gn * (sgg * (1.0 + gc * (1.0 - sgg)))
            do_b = do.astype(BF16)
            da = jnp.where(x["tril"], lax.dot_general(do_b, v_b, NT_DIMS, preferred_element_type=F32), 0.0).astype(BF16)
            qt_b, kt_b, qe_b, khat_b = (x[n_].astype(BF16) for n_ in ("qt", "kt", "qe", "khat"))
            dv = (lax.dot_general(x["a"].astype(BF16), do_b, TN_DIMS, preferred_element_type=F32)
                  + lax.dot_general(khat_b, dst_b, NT_DIMS, preferred_element_type=F32))
            dqt = jnp.dot(da, kt_b, preferred_element_type=F32)
            dqe = jnp.dot(do_b, st_b, preferred_element_type=F32)
            dkt = lax.dot_general(da, qt_b, TN_DIMS, preferred_element_type=F32)
            dkhat = jnp.dot(v_b, dst_b, preferred_element_type=F32)
            dst_sc[hh] = lax.dot_general(do_b, qe_b, TN_DIMS, preferred_element_type=F32) + dst * x["e_last"]
            de_last = jnp.sum(st * dst, axis=0, keepdims=True)
            dqf = dqt * x["eq"] + dqe * x["e"]
            dkk = dkt * x["ek"] + dkhat * x["eh"]
            dkh_kh = dkhat * x["khat"]
            db = dqt * qt_b.astype(F32) - dkt * kt_b.astype(F32) + dqe * x["qe"] - dkh_kh
            db_last = jnp.sum(dkh_kh, axis=0, keepdims=True) + de_last * x["e_last"]
            db = db + jnp.where(x["rowid"] == C - 1, db_last, 0.0)
            dlg = _running_sum(db, reverse=True)
            dgate = dlg / x["gate"] - dkk
            sg, sq = x["sg"], x["sq"]
            dlb_ref[:, hs] += _fold8(dgate * (1.0 - sg)) * (lb * (1.0 - lb))
            dz_ref[0, rs, hs] = (dqf * (sq * (1.0 + qc * (1.0 - sq)))).astype(BF16)
            dz_ref[1, rs, hs] = (dgate * (1.0 - lb) * sg * (1.0 - sg)).astype(BF16)
            dz_ref[2, rs, hs] = dv.astype(BF16)
            dz_ref[3, rs, hs] = dg.astype(BF16)

        lax.fori_loop(0, ncb, chunk, 0)

    W = 128 * HPB
    zb = lambda k: pl.BlockSpec((None, tb, W), lambda h, t: (k, nt - 1 - t, h))
    blk = pl.BlockSpec((tb, W), lambda h, t: (nt - 1 - t, h))
    acc = pl.BlockSpec((8, W), lambda h, t: (0, h))
    return pl.pallas_call(
        body, name="hgrn_bwd", grid=(HEADS // HPB, nt),
        in_specs=[zb(0), zb(1), zb(2), zb(3), blk, blk,
                  pl.BlockSpec((HPB, ncb, 128, 128), lambda h, t: (h, nt - 1 - t, 0, 0)),
                  pl.BlockSpec((2, W), lambda h, t: (0, h)), pl.BlockSpec((1, W), lambda h, t: (0, h))],
        out_specs=[pl.BlockSpec((4, tb, W), lambda h, t: (0, nt - 1 - t, h)), acc, acc],
        out_shape=[jax.ShapeDtypeStruct((4, T, D), BF16), jax.ShapeDtypeStruct((8, D), F32),
                   jax.ShapeDtypeStruct((8, D), F32)],
        scratch_shapes=[pltpu.VMEM((HPB, 128, 128), F32)],
        compiler_params=_params(48, 2),
    )(*_hbm(z4, z4, z4, z4, o_raw, dy, states, hg_lb, gnorm))


def _adamw(w, g, m, v, *, name):
    R, L = w.shape
    tr = R if R <= 512 else 512
    assert R % tr == 0
    blk = pl.BlockSpec((tr, L), lambda i: (i, 0))
    c1, c2 = 1.0 - B1 ** STEP, 1.0 - B2 ** STEP

    def body(w_ref, g_ref, m_ref, v_ref, d_ref, mo_ref, vo_ref):
        g_ = g_ref[...]
        m_ = B1 * m_ref[...] + (1.0 - B1) * g_
        v_ = B2 * v_ref[...] + (1.0 - B2) * (g_ * g_)
        d_ref[...] = -LR * ((m_ / c1) / (jnp.sqrt(v_ / c2) + ADAM_EPS) + WD * w_ref[...])
        mo_ref[...] = m_
        vo_ref[...] = v_

    sds = jax.ShapeDtypeStruct((R, L), F32)
    return pl.pallas_call(
        body, name=name, grid=(R // tr,), in_specs=[blk] * 4, out_specs=[blk] * 3, out_shape=[sds] * 3,
        compiler_params=_params(32, 1),
    )(w, g, m, v)


def _adamw_rows(w, m, v, gbufs, row0, *, name, plan=None):
    L, R, C = w.shape
    tr = 256
    assert R % tr == 0 and row0 % tr == 0 and len(gbufs) == L
    grid = (L, R // tr)
    blk = pl.BlockSpec((None, tr, C), lambda l, i: (l, i, 0))
    gblk = pl.BlockSpec((tr, C), lambda l, i: (row0 // tr + i, 0))
    c1, c2 = 1.0 - B1 ** STEP, 1.0 - B2 ** STEP

    def body(*refs):
        ins, (go_ref, d_ref, mo_ref, vo_ref), _, pctx = _split_refs(refs, 3 + L, 4, 0, plan)
        w_ref, m_ref, v_ref = ins[:3]
        g_refs = ins[3:]
        _plan_start(plan, pctx, grid)
        g_ = g_refs[0][...]
        for l in range(1, L):
            g_ = jnp.where(pl.program_id(0) == l, g_refs[l][...], g_)
        m_ = B1 * m_ref[...] + (1.0 - B1) * g_
        v_ = B2 * v_ref[...] + (1.0 - B2) * (g_ * g_)
        go_ref[...] = g_
        d_ref[...] = -LR * ((m_ / c1) / (jnp.sqrt(v_ / c2) + ADAM_EPS) + WD * w_ref[...])
        mo_ref[...] = m_
        vo_ref[...] = v_
        _plan_wait(plan, pctx, grid)

    sds = jax.ShapeDtypeStruct((L, R, C), F32)
    p_in, p_ospec, p_oshape, p_scr, p_alias = _plan_io(plan, 3 + L, 4)
    return pl.pallas_call(
        body, name=name, grid=grid, in_specs=[blk] * 3 + [gblk] * L + [_ANY] * len(p_in),
        out_specs=[blk] * 4 + p_ospec, out_shape=[sds] * 4 + p_oshape, scratch_shapes=p_scr,
        input_output_aliases=p_alias, compiler_params=_params(32, 2),
    )(*_hbm(w, m, v, *gbufs), *p_in)


def _add_pairs(g, theirs, ids, *, name):
    n, R, L = theirs.shape
    tr = 128
    nb = R // tr

    def body(ids_ref, a_ref, b_ref, o_ref):
        o_ref[...] = (a_ref[...].astype(F32) + b_ref[...].astype(F32)).astype(BF16)

    blk = pl.BlockSpec((n, tr, L), lambda i, ids: (0, i, 0))
    return pl.pallas_call(
        body, name=name, out_shape=jax.ShapeDtypeStruct((n, R, L), BF16),
        grid_spec=pltpu.PrefetchScalarGridSpec(
            num_scalar_prefetch=1, grid=(nb,),
            in_specs=[pl.BlockSpec((n, tr, L), lambda i, ids: (0, ids[1] * nb + i, 0)), blk], out_specs=blk),
        compiler_params=_params(16, 1),
    )(ids, g, theirs)


def _sum_chips(pair, parts, ids, *, name):
    _, R, L = parts.shape
    tr = 128

    def body(ids_ref, o_ref, r_ref, out_ref):
        out_ref[...] = ((o_ref[...].astype(F32) + r_ref[0].astype(F32)) + r_ref[1].astype(F32)) + r_ref[2].astype(F32)

    return pl.pallas_call(
        body, name=name, out_shape=jax.ShapeDtypeStruct((2, R, L), F32),
        grid_spec=pltpu.PrefetchScalarGridSpec(
            num_scalar_prefetch=1, grid=(R // tr,),
            in_specs=[pl.BlockSpec((None, tr, L), lambda i, ids: (ids[0], i, 0)),
                      pl.BlockSpec((3, tr, L), lambda i, ids: (0, i, 0))],
            out_specs=pl.BlockSpec((None, tr, L), lambda i, ids: (ids[1], i, 0))),
        compiler_params=_params(32, 1),
    )(ids, pair, parts)


def _mesh_ids():
    x, y, c = _mesh_pos()
    return jnp.stack([2 * x + y, c]).astype(jnp.int32)


def _place_shard(rows, ids, *, name):
    R, L = rows.shape
    tr = 256

    def body(ids_ref, in_ref, out_ref):
        out_ref[...] = in_ref[...].astype(BF16)

    return pl.pallas_call(
        body, name=name, out_shape=jax.ShapeDtypeStruct((4, R, L), BF16),
        grid_spec=pltpu.PrefetchScalarGridSpec(
            num_scalar_prefetch=1, grid=(R // tr,), in_specs=[pl.BlockSpec((tr, L), lambda i, ids: (i, 0))],
            out_specs=pl.BlockSpec((None, tr, L), lambda i, ids: (ids[0], i, 0))),
        compiler_params=_params(16, 1),
    )(ids, rows)


def _remote(src, dst, send_sem, recv_sem, to):
    return pltpu.make_async_remote_copy(src_ref=src, dst_ref=dst, send_sem=send_sem, recv_sem=recv_sem,
                                        device_id=to, device_id_type=MESH_IDS)


def _rows(ref, lead, start, size):
    return ref.at[tuple(pl.ds(0, n) for n in ref.shape[:lead]) + (pl.ds(start, size),)]


def _other_chips():
    x, y, _ = _mesh_pos()
    return [(1 - x, y), (x, 1 - y), (1 - x, 1 - y)]


def _plan_gather_ici(bufs):
    n = len(bufs)

    def copies(outs, send, recv):
        x, y, c = _mesh_pos()
        res = []
        for b in range(n):
            half = bufs[b].shape[1] // 2
            mine = _rows(outs[b].at[2 * x + y], 0, c * half, half)
            for j, (cx, cy) in enumerate(_other_chips()):
                res.append((_remote(mine, mine, send(3 * b + j), recv(3 * b + j), (cx, cy, c)),
                            _remote(mine, _rows(outs[b].at[2 * cx + cy], 0, c * half, half),
                                    send(3 * b + j), recv(3 * b + j), (x, y, c))))
        return res

    def start(ins, outs, send, recv, loc):
        for out_cp, _ in copies(outs, send, recv):
            out_cp.start()

    def wait(ins, outs, send, recv, loc):
        for out_cp, in_cp in copies(outs, send, recv):
            in_cp.wait_recv()
            out_cp.wait_send()

    outs = [jax.ShapeDtypeStruct(b.shape, b.dtype) for b in bufs]
    return _Plan(bufs, outs, 3 * n, 0, start, wait, aliases={b: b for b in range(n)})


def _plan_gather_forward(bufs):
    n = len(bufs)

    def copies(outs, send, recv):
        x, y, c = _mesh_pos()
        res = []
        for b in range(n):
            half = bufs[b].shape[1] // 2
            for j, (cx, cy) in enumerate(_other_chips()):
                slot = outs[b].at[2 * cx + cy]
                res.append((_remote(_rows(slot, 0, c * half, half), _rows(slot, 0, c * half, half),
                                    send(3 * b + j), recv(3 * b + j), (x, y, 1 - c)),
                            _remote(_rows(slot, 0, c * half, half), _rows(slot, 0, (1 - c) * half, half),
                                    send(3 * b + j), recv(3 * b + j), (x, y, c))))
        return res

    def start(ins, outs, send, recv, loc):
        for out_cp, _ in copies(outs, send, recv):
            out_cp.start()

    def wait(ins, outs, send, recv, loc):
        for out_cp, in_cp in copies(outs, send, recv):
            in_cp.wait_recv()
            out_cp.wait_send()

    outs = [jax.ShapeDtypeStruct(b.shape, b.dtype) for b in bufs]
    return _Plan(bufs, outs, 3 * n, 0, start, wait, aliases={b: b for b in range(n)})


def _plan_pair_swap(g):
    half = g.shape[1] // 2

    def copy(ins, outs, send, recv, loc):
        x, y, c = _mesh_pos()
        return _remote(_rows(ins[0], 1, (1 - c) * half, half), outs[0], send(0), recv(0), (x, y, 1 - c))

    return _Plan([g], [jax.ShapeDtypeStruct((4, half, g.shape[2]), g.dtype)], 1, 0,
                 lambda *a: copy(*a).start(), lambda *a: copy(*a).wait())


def _plan_pair_gather(buf):
    def copies(ins, outs, send, recv, loc):
        x, y, c = _mesh_pos()
        return (_remote(outs[0].at[c], outs[0].at[c], send(0), recv(0), (x, y, 1 - c)),
                _remote(outs[0].at[c], outs[0].at[1 - c], send(0), recv(0), (x, y, c)))

    def wait(*a):
        out_cp, in_cp = copies(*a)
        in_cp.wait_recv()
        out_cp.wait_send()

    return _Plan([buf], [jax.ShapeDtypeStruct(buf.shape, buf.dtype)], 1, 0, lambda *a: copies(*a)[0].start(), wait,
                 aliases={0: 0})


def _plan_chip_scatter(p):
    def copies(ins, outs, send, recv, loc):
        _, _, c = _mesh_pos()
        return [_remote(ins[0].at[2 * cx + cy], outs[0].at[j], send(j), recv(j), (cx, cy, c))
                for j, (cx, cy) in enumerate(_other_chips())]

    def start(*a):
        for cp in copies(*a):
            cp.start()

    def wait(*a):
        for cp in copies(*a):
            cp.wait()

    return _Plan([p], [jax.ShapeDtypeStruct((3,) + p.shape[1:], p.dtype)], 3, 0, start, wait)


def _plan_exchange_all(vec):
    def copies(ins, outs, send, recv, loc):
        x, y, c = _mesh_pos()
        return [_remote(ins[0], outs[0].at[r - 1], send(r - 1), recv(r - 1), (x ^ (r >> 2), y ^ ((r >> 1) & 1), c ^ (r & 1)))
                for r in range(1, 8)]

    def start(*a):
        for cp in copies(*a):
            cp.start()

    def wait(*a):
        for cp in copies(*a):
            cp.wait()

    return _Plan([vec], [jax.ShapeDtypeStruct((7,) + vec.shape, vec.dtype)], 7, 0, start, wait)


def _sum_devices(vec, others, ids):
    R, L = vec.shape

    def body(ids_ref, v_ref, o_ref, out_ref):
        me = 2 * ids_ref[0] + ids_ref[1]
        total = None
        for d in range(8):
            rel = d ^ me
            term = jnp.where(rel == 0, v_ref[...], o_ref[jnp.maximum(rel - 1, 0)])
            total = term if total is None else total + term
        out_ref[...] = total

    return pl.pallas_call(
        body, name="small_grad_sum", out_shape=jax.ShapeDtypeStruct((R, L), F32),
        grid_spec=pltpu.PrefetchScalarGridSpec(
            num_scalar_prefetch=1, grid=(1,), in_specs=[pl.BlockSpec((R, L), lambda i, ids: (0, 0)),
                                                        pl.BlockSpec((7, R, L), lambda i, ids: (0, 0, 0))],
            out_specs=pl.BlockSpec((R, L), lambda i, ids: (0, 0))),
        compiler_params=_params(16, 1),
    )(ids, vec, others)


ROWS_L1, ROWS_L0, ROWS_ODD = 3328, 2048, 768
ODD_PARTS = (("w_out_e", (256, 1024)), ("w_in_e", (1024, 392)), ("w_qb", (256, 192)), ("w_kvb", (256, 256)))


def _odd_rows(parts, dtype, gnorm=None):
    rows = [parts[n].reshape(-1, 1024).astype(dtype) for n, _ in ODD_PARTS]
    used = sum(r.shape[0] for r in rows)
    if gnorm is not None:
        bits = lax.bitcast_convert_type(gnorm.reshape(-1), BF16).reshape(1, 512)
        rows.append(jnp.pad(bits, ((0, 0), (0, 512))))
        used += 1
    rows.append(jnp.zeros((ROWS_ODD - used, 1024), dtype))
    return jnp.concatenate(rows, axis=0)


def _odd_unrows(buf, with_gnorm=False):
    out, off = {}, 0
    for n, shape in ODD_PARTS:
        nr = math.prod(shape) // 1024
        out[n] = buf[off:off + nr].reshape(shape)
        off += nr
    if with_gnorm:
        out["hg_gnorm"] = lax.bitcast_convert_type(buf[off, :512].reshape(256, 2), F32).reshape(1, 256)
    return out


def _pack_small(vals, last):
    flat = jnp.concatenate([vals[n].reshape(-1).astype(F32) for n, _ in SMALL])
    pad = jnp.zeros((SMALL_ROWS * 1024 - flat.shape[0] - 1,), F32)
    return jnp.concatenate([flat, pad, last.reshape(1)]).reshape(SMALL_ROWS, 1024)


def _unpack_small(packed):
    flat = packed.reshape(-1)
    out, off = {}, 0
    for n, shape in SMALL:
        size = math.prod(shape)
        out[n] = flat[off:off + size].reshape(shape)
        off += size
    return out


def _rope_tables(positions):
    half = ROPE // 2
    inv_freq = ROPE_BASE ** (-jnp.arange(half, dtype=F32) / half)
    ang = positions.astype(F32).reshape(-1, 1) * inv_freq
    cos, sin = jnp.cos(ang), jnp.sin(ang)
    T = ang.shape[0]
    one, z16, z32 = jnp.ones((T, NOPE), F32), jnp.zeros((T, half), F32), jnp.zeros((T, 32), F32)
    z64 = jnp.zeros((T, NOPE), F32)
    c = jnp.concatenate([one, cos, cos, z32], axis=1)
    s1 = jnp.concatenate([z64, -sin, z16, z32], axis=1)
    s2 = jnp.concatenate([z64, z16, sin, z32], axis=1)
    return c, s1, s2


def _local_step(x, positions, tgt, odd, bufs, P, exchange):
    T = x.shape[0]
    row = lambda a: a.reshape(1, -1)
    rc, rs1, rs2 = _rope_tables(positions)
    blk = lambda f: pl.BlockSpec((None, D, D), f)

    w_in_e = odd["w_in_e"]
    w_in = jnp.concatenate([w_in_e[:, :512], w_in_e[:, 544:1568], w_in_e[:, 512:544], jnp.zeros((D, 96), BF16)], axis=1)
    wq = jnp.pad(odd["w_qb"].reshape(256, HEADS, NOPE + ROPE), ((0, 0), (0, 0), (0, 32))).reshape(256, HEADS * 128)
    kvb = odd["w_kvb"].reshape(256, HEADS, NOPE + VDIM)
    wk = jnp.pad(kvb[:, :, :NOPE], ((0, 0), (0, 0), (0, 64))).reshape(256, HEADS * 128)
    wv = kvb[:, :, NOPE:].reshape(256, HEADS * VDIM)
    w_out_e = odd["w_out_e"]
    sgu_w = P["sgu_w"][0]
    sgu_bt = P["sgu_b"][0].T
    gq, gkv = P["mla_gq"], P["mla_gkv"]
    gnorm = P["hg_gnorm"]

    z0 = _matmul(x, w_in, name="in_proj_e", M=T, N=1664, K=D, tn=1664)[0]
    q, k, v = _mla_prep(z0, gq, gkv, wq, wk, wv, rc, rs1, rs2)
    if exchange:
        ids = _mesh_ids()
        placed = [_place_shard(b, ids, name=f"place_shard_{l}") for l, b in enumerate(bufs)]
        a_out, lse, wga, wgb = _flash_fwd(q, k, v, plan=_plan_gather_ici(placed[:2]))
    else:
        a_out, lse = _flash_fwd(q, k, v)
        wga, wgb, wgc = bufs
    mix0 = _sgu_fwd(z0, a_out, P["sgu_ln_g"], P["sgu_ln_b"], sgu_w, sgu_bt)
    res = _proj_ln(mix0, w_out_e, x, row(P["ln1_g"][0]), row(P["ln1_b"][0]), name="out_proj_ln_e",
                   plan=_plan_gather_forward([wga, wgb]) if exchange else None)
    r1, h1, h1b = res[:3]
    if exchange:
        wga, wgb = res[3:]
    res = _ffn_ln(h1b, wga, h1, row(P["ln2_g"][0]), row(P["ln2_b"][0]), name="ffn_ln_0",
                  plan=_plan_gather_ici(placed[2:]) if exchange else None)
    ra0, r2, h2, h2b = res[:4]
    z4 = _matmul(h2b, wgb, name="in_proj_o", M=T, N=4 * D, K=D, b_spec=blk(lambda i, j, k: (j, 0, 0)),
                 out_shape=jax.ShapeDtypeStruct((4, T, D), F32),
                 o_spec=pl.BlockSpec((None, min(MM_ROWS, T), D), lambda i, j, k: (j, i, 0)))[0]
    y1, o_raw, states = _hgrn_fwd(z4, P["hg_lb"], gnorm)
    res2 = _proj_ln(y1, wgb, h2, row(P["ln1_g"][1]), row(P["ln1_b"][1]), name="out_proj_ln_o", w_rowblk=4,
                    plan=_plan_gather_forward([res[4]]) if exchange else None)
    r3, h3, h3b = res2[:3]
    if exchange:
        wgc = res2[3]
    ra1, r4, h4, _ = _ffn_ln(h3b, wgc, h3, row(P["ln2_g"][1]), row(P["ln2_b"][1]), name="ffn_ln_1")

    gs = {}
    ln1_g, ln1_b, ln2_g, ln2_b = [None, None], [None, None], [None, None], [None, None]
    sq_err_parts = []

    def ffn_bwd(l, dh, r_out, ra, h_mid_b, g2, wg, rows, plan=None, tgt=None):
        dr, dr_b, dg, db, *sq_err = _ln_bwd(dh, r_out, row(g2), name=f"ln2_bwd_{l}", tgt=tgt)
        sq_err_parts.extend(sq_err)
        ln2_g[l], ln2_b[l] = dg.sum(0), db.sum(0)
        da, *extra = _matmul(dr_b, wg, tb=True, mul=ra, out_dtype=BF16, name=f"ffn_da_{l}", M=T, N=4 * D, K=D,
                             b_spec=blk(lambda i, j, k: (j, 1, 0)), plan=plan)
        gbuf = _matmul(ra, dr_b, ta=True, a_sq=True, name=f"ffn_dw2_{l}", M=4 * D, N=D, K=T, tm=1024, tk=DW_TOKENS,
                       out_shape=jax.ShapeDtypeStruct((4, rows, D), BF16), o_spec=blk(lambda i, j, k: (i, 1, 0)))[0]
        gbuf = _matmul(h_mid_b, da, ta=True, name=f"ffn_dw1_{l}", M=D, N=4 * D, K=T, tm=1024, tk=DW_TOKENS, into=gbuf,
                       out_shape=jax.ShapeDtypeStruct((4, rows, D), BF16), o_spec=blk(lambda i, j, k: (j, 0, 0)))[0]
        dh_mid = _matmul(da, wg, tb=True, add=dr, add_scale=ALPHA, name=f"ffn_dh_{l}", M=T, N=D, K=4 * D,
                         b_spec=blk(lambda i, j, k: (k, 0, 0)))[0]
        return dh_mid, gbuf, extra

    dh3, g1, _ = ffn_bwd(1, h4, r4, ra1, h3b, P["ln2_g"][1], wgc, ROWS_L1, tgt=tgt)
    loss_parts = sq_err_parts[0]
    dr3, dr3_b, dg, db = _ln_bwd(dh3, r3, row(P["ln1_g"][1]), name="ln1_bwd_1")
    ln1_g[1], ln1_b[1] = dg.sum(0), db.sum(0)
    g1_sds = jax.ShapeDtypeStruct((4, ROWS_L1, D), BF16)
    g1 = _matmul(y1, dr3_b, ta=True, name="dw_out_o", M=D, N=D, K=T, tm=256, tk=DW_TOKENS, into=g1, out_shape=g1_sds,
                 o_spec=pl.BlockSpec((None, 256, D), lambda i, j, k: (i, 12, 0)))[0]
    dmix1 = _matmul(dr3_b, wgb, tb=True, name="dmix_o", M=T, N=D, K=D, b_spec=_rows4_spec(4, 3), b_merge=(D, D))[0]
    dz4, dlb, dgn = _hgrn_bwd(z4, o_raw, dmix1, states, P["hg_lb"], gnorm)
    g1 = _matmul(h2b, dz4, ta=True, name="dw_in_o", M=D, N=4 * D, K=T, tm=1024, tk=DW_TOKENS, into=g1, out_shape=g1_sds,
                 b_spec=pl.BlockSpec((None, min(DW_TOKENS, T), D), lambda i, j, k: (j, k, 0)),
                 o_spec=blk(lambda i, j, k: (j, 2, 0)))[0]
    dh2 = _matmul(dz4, wgb, tb=True, add=dr3, add_scale=ALPHA, name="dh_in_o", M=T, N=D, K=4 * D,
                  a_spec=pl.BlockSpec((None, min(MM_ROWS, T), D), lambda i, j, k: (k, i, 0)),
                  b_spec=blk(lambda i, j, k: (k, 0, 0)))[0]
    d_lb1 = dlb.sum(0)
    gs["hg_lb"] = jnp.stack([-d_lb1, d_lb1])
    gs["hg_gnorm"] = dgn.sum(0)[None]

    dh1, g0, swapped1 = ffn_bwd(0, dh2, r2, ra0, h1b, P["ln2_g"][0], wga, ROWS_L0,
                                plan=_plan_pair_swap(g1) if exchange else None)
    dr1, dr1_b, dg, db = _ln_bwd(dh1, r1, row(P["ln1_g"][0]), name="ln1_bwd_0")
    ln1_g[0], ln1_b[0] = dg.sum(0), db.sum(0)
    godd = {"w_out_e": _matmul(mix0, dr1_b, ta=True, name="dw_out_e", M=D, N=D, K=T, tm=1024, tk=DW_TOKENS)[0]}
    dmix0, *swapped0 = _matmul(dr1_b, w_out_e, tb=True, name="dmix_e", M=T, N=D, K=D,
                               plan=_plan_pair_swap(g0) if exchange else None)
    delta, do_b = _attn_delta(dmix0, a_out)
    if exchange:
        pair1 = _add_pairs(g1, swapped1[0], ids, name="grad_pair_add_1")
        pair0 = _add_pairs(g0, swapped0[0], ids, name="grad_pair_add_0")
        dq4, dk, dv, parts0, parts1 = _flash_bwd(
            q, k, v, do_b, lse, delta, plan=_join_plans([_plan_chip_scatter(pair0), _plan_chip_scatter(pair1)]))
        half0 = _sum_chips(pair0, parts0, ids, name="grad_chip_sum_0")
        half1 = _sum_chips(pair1, parts1, ids, name="grad_chip_sum_1")
        dc, dkr, dwq, dwk, dwv, dgq, dgkv, g0, g1 = _mla_bwd(
            z0, dq4, dk, dv, gq, gkv, wq, wk, wv, rc, rs1, rs2,
            plan=_join_plans([_plan_pair_gather(half0), _plan_pair_gather(half1)]))
        g0, g1 = g0.reshape(ROWS_L0, D), g1.reshape(ROWS_L1, D)
    else:
        dq4, dk, dv = _flash_bwd(q, k, v, do_b, lse, delta)
        dc, dkr, dwq, dwk, dwv, dgq, dgkv = _mla_bwd(z0, dq4, dk, dv, gq, gkv, wq, wk, wv, rc, rs1, rs2)
    dz0, dsw, dsb, dslg, dslb = _sgu_bwd(z0, dmix0, dc, dkr, P["sgu_ln_g"], P["sgu_ln_b"], sgu_w, sgu_bt)
    gs["mla_gq"], gs["mla_gkv"] = dgq.sum(0)[None], dgkv.sum(0)[None]
    gs["sgu_ln_g"], gs["sgu_ln_b"] = dslg.sum(0)[None], dslb.sum(0)[None]
    gs["sgu_w"], gs["sgu_b"] = dsw[None], dsb[:, :SGU_G].T[None]
    gs["ln1_g"], gs["ln1_b"] = jnp.stack(ln1_g), jnp.stack(ln1_b)
    gs["ln2_g"], gs["ln2_b"] = jnp.stack(ln2_g), jnp.stack(ln2_b)
    small_vec = _pack_small(gs, (0.5 / D) * jnp.sum(loss_parts))
    dw_in, *small_others = _matmul(x, dz0, ta=True, name="dw_in_e", M=D, N=1664, K=T, tm=1024, tn=1664,
                                   tk=DW_TOKENS // 2, plan=_plan_exchange_all(small_vec) if exchange else None)
    godd["w_in_e"] = jnp.concatenate([dw_in[:, :512], dw_in[:, 1536:1568], dw_in[:, 512:1536]], axis=1)
    grad_x = _matmul(dz0, w_in, tb=True, add=dr1, add_scale=ALPHA, name="dx", M=T, N=D, K=1664, tk=1664)[0]

    godd["w_qb"] = dwq.reshape(256, HEADS, 128)[:, :, :NOPE + ROPE].reshape(256, HEADS * (NOPE + ROPE))
    godd["w_kvb"] = jnp.concatenate([dwk.reshape(256, HEADS, 128)[:, :, :NOPE], dwv.reshape(256, HEADS, VDIM)],
                                    axis=2).reshape(256, HEADS * (NOPE + VDIM))
    small = (small_vec, small_others[0]) if exchange else gs
    return loss_parts, grad_x, g0, g1, godd, small


WEIGHTS = ['w_in_e', 'mla_gq', 'mla_gkv', 'w_qb', 'w_kvb', 'sgu_ln_g', 'sgu_ln_b', 'sgu_w', 'sgu_b', 'w_out_e',
           'w_in_o', 'hg_lb', 'hg_gnorm', 'w_out_o', 'ln1_g', 'ln1_b', 'w_ff1', 'w_ff2', 'ln2_g', 'ln2_b']


def kernel(x, positions, w_in_e, mla_gq, mla_gkv, w_qb, w_kvb, sgu_ln_g, sgu_ln_b, sgu_w, sgu_b, w_out_e, w_in_o, hg_lb, hg_gnorm, w_out_o, ln1_g, ln1_b, w_ff1, w_ff2, ln2_g, ln2_b, loss_target, m_w_in_e, m_mla_gq, m_mla_gkv, m_w_qb, m_w_kvb, m_sgu_ln_g, m_sgu_ln_b, m_sgu_w, m_sgu_b, m_w_out_e, m_w_in_o, m_hg_lb, m_hg_gnorm, m_w_out_o, m_ln1_g, m_ln1_b, m_w_ff1, m_w_ff2, m_ln2_g, m_ln2_b, v_w_in_e, v_mla_gq, v_mla_gkv, v_w_qb, v_w_kvb, v_sgu_ln_g, v_sgu_ln_b, v_sgu_w, v_sgu_b, v_w_out_e, v_w_in_o, v_hg_lb, v_hg_gnorm, v_w_out_o, v_ln1_g, v_ln1_b, v_w_ff1, v_w_ff2, v_ln2_g, v_ln2_b):
    args = dict(locals())
    w = {n: args[n] for n in WEIGHTS}
    m = {n: args["m_" + n] for n in WEIGHTS}
    v = {n: args["v_" + n] for n in WEIGHTS}
    cx, cy, cc = _mesh_pos()
    chip = 2 * cx + cy

    odd_shard = _odd_rows({"w_out_e": w_out_e[0], "w_in_e": w_in_e[0], "w_qb": w_qb[0], "w_kvb": w_kvb[0]}, BF16,
                          gnorm=hg_gnorm)
    ids = _mesh_ids()
    gathered = _run_plan(_plan_gather_ici([_place_shard(odd_shard, ids, name="place_shard_odd")]), name="odd_gather")[0]
    gathered = _run_plan(_plan_gather_forward([gathered]), name="odd_gather_forward")[0]
    per_chip = [_odd_unrows(gathered[j], with_gnorm=True) for j in range(4)]
    odd = {"w_out_e": jnp.concatenate([p["w_out_e"] for p in per_chip], axis=0)}
    for n in ("w_in_e", "w_qb", "w_kvb"):
        odd[n] = jnp.concatenate([p[n] for p in per_chip], axis=1)
    small = {n: w[n] for n, _ in SMALL if n != "hg_gnorm"}
    small["hg_gnorm"] = jnp.concatenate([p["hg_gnorm"] for p in per_chip], axis=1)
    shard_rows = (jnp.concatenate([w_ff1[0], w_ff2[0]], axis=0).astype(BF16),
                  jnp.concatenate([w_in_o[0], w_out_o[0]], axis=0).astype(BF16),
                  jnp.concatenate([w_ff1[1], w_ff2[1]], axis=0).astype(BF16))

    loss_parts, grad_x, g_l0, g_l1, godd, (small_vec, small_others) = _local_step(
        x[0], positions[0], loss_target[0], odd, shard_rows, small, True)

    by_chip = [_odd_rows({"w_out_e": jnp.split(godd["w_out_e"], 4, axis=0)[j],
                          **{n: jnp.split(godd[n], 4, axis=1)[j] for n in ("w_in_e", "w_qb", "w_kvb")}}, BF16)
               for j in range(4)]
    godd_buf = jnp.stack(by_chip)
    theirs = _run_plan(_plan_pair_swap(godd_buf), name="odd_pair_swap")[0]
    pair = _add_pairs(godd_buf, theirs, ids, name="odd_pair_add")
    parts = _run_plan(_plan_chip_scatter(pair), name="odd_chip_scatter")[0]
    g_odd = _run_plan(_plan_pair_gather(_sum_chips(pair, parts, ids, name="odd_chip_sum")), name="odd_pair_gather")[0]
    g_odd = _odd_unrows(g_odd.reshape(ROWS_ODD, 1024))

    small_sum = _sum_devices(small_vec, small_others, ids)
    loss = small_sum[-1, -1]
    g_small = _unpack_small(small_sum)
    grads = {n: g_small[n] for n, _ in SMALL if n != "hg_gnorm"}
    grads["hg_gnorm"] = lax.dynamic_slice_in_dim(g_small["hg_gnorm"], chip * 256, 256, axis=1)

    delta, new_m, new_v = {}, {}, {}
    for n, bufs_, row0 in (("w_ff1", [g_l0, g_l1], 0), ("w_ff2", [g_l0, g_l1], 1024), ("w_in_o", [g_l1], 2048),
                           ("w_out_o", [g_l1], 3072)):
        grads[n], delta[n], new_m[n], new_v[n] = _adamw_rows(w[n], m[n], v[n], bufs_, row0, name=f"adamw_{n}")
    for n, _ in ODD_PARTS:
        grads[n] = g_odd[n][None]
        d_, m_, v_ = _adamw(w[n][0], g_odd[n], m[n][0], v[n][0], name=f"adamw_{n}")
        delta[n], new_m[n], new_v[n] = d_[None], m_[None], v_[None]
    rest = [n for n in WEIGHTS if n not in delta]

    def pack_rest(d):
        flat = jnp.concatenate([d[n].reshape(-1) for n in rest])
        return jnp.pad(flat, (0, SMALL_ROWS * 1024 - flat.shape[0])).reshape(SMALL_ROWS, 1024)

    outs = _adamw(pack_rest(w), pack_rest(grads), pack_rest(m), pack_rest(v), name="adamw_small")
    for dst, packed in zip((delta, new_m, new_v), outs):
        flat, off = packed.reshape(-1), 0
        for n in rest:
            size = math.prod(w[n].shape)
            dst[n] = flat[off:off + size].reshape(w[n].shape)
            off += size

    return (loss, grad_x[None], *[grads[n] for n in WEIGHTS], *[delta[n] for n in WEIGHTS],
            *[new_m[n] for n in WEIGHTS], *[new_v[n] for n in WEIGHTS])
```

```python
import functools
import math

import jax
import jax.numpy as jnp
from jax import lax
from jax.experimental import pallas as pl
from jax.experimental.pallas import tpu as pltpu

F32 = jnp.float32
BF16 = jnp.bfloat16
MESH_IDS = pl.DeviceIdType.MESH

D = 1024
DEPTH = 2
HEADS = 8
NOPE, ROPE, VDIM = 64, 32, 64
QK_SCALE = (NOPE + ROPE) ** -0.5
ROPE_BASE = 10000.0
SGU_G, SGU_C = 4, 128
HG_CHUNK = 64
HG_HEADS_PER_STEP = 8
ALPHA = (2 * DEPTH) ** 0.25
EPS = 1e-5
LR, B1, B2, ADAM_EPS, WD, STEP = 0.001, 0.9, 0.999, 1e-08, 0.01, 10
GELU_C = math.sqrt(2.0 / math.pi)
GELU_A = 0.044715
HI = lax.Precision.HIGHEST
MB = 1024 * 1024
ROW_BLOCK = 512

NT_DIMS = (((1,), (1,)), ((), ()))
TN_DIMS = (((0,), (0,)), ((), ()))

SHARDED = (
    ("w_in_e", (1, 1024, 392), 2), ("w_qb", (1, 256, 192), 2), ("w_kvb", (1, 256, 256), 2),
    ("w_out_e", (1, 256, 1024), 1), ("w_in_o", (1, 1024, 1024), 2), ("w_out_o", (1, 256, 1024), 1),
    ("w_ff1", (2, 1024, 1024), 2), ("w_ff2", (2, 1024, 1024), 1), ("hg_gnorm", (1, 256), 1),
)
PACK_ROWS = 6144
HALF_ROWS = PACK_ROWS // 2
SMALL = (("mla_gq", (1, 256)), ("mla_gkv", (1, 256)), ("sgu_ln_g", (1, 512)), ("sgu_ln_b", (1, 512)),
         ("sgu_w", (1, 4, 128, 128)), ("sgu_b", (1, 4, 128)), ("hg_lb", (2, 1024)), ("hg_gnorm", (1, 1024)),
         ("ln1_g", (2, 1024)), ("ln1_b", (2, 1024)), ("ln2_g", (2, 1024)), ("ln2_b", (2, 1024)))
SMALL_ROWS = 80


def _params(vmem_mb, n_axes=0):
    kw = dict(vmem_limit_bytes=vmem_mb * MB)
    if n_axes:
        kw["dimension_semantics"] = ("arbitrary",) * n_axes
    return pltpu.CompilerParams(**kw)


_ANY = pl.BlockSpec(memory_space=pltpu.HBM)


def _mesh_pos():
    return lax.axis_index("x"), lax.axis_index("y"), lax.axis_index("c")


def _hbm(*arrays):
    return tuple(pltpu.with_memory_space_constraint(a, pltpu.HBM) if a.size >= 2 ** 18 else a for a in arrays)


class _Plan:
    def __init__(self, ins, outs, n_remote, n_local, start, wait, aliases=None):
        self.ins, self.outs, self.n_remote, self.n_local = list(ins), list(outs), n_remote, n_local
        self.start, self.wait, self.aliases = start, wait, dict(aliases or {})


def _join_plans(plans):
    ins, outs, aliases, parts = [], [], {}, []
    nr = nl = 0
    for p in plans:
        parts.append((p, len(ins), len(outs), nr, nl))
        aliases.update({len(ins) + i: len(outs) + o for i, o in p.aliases.items()})
        ins += p.ins
        outs += p.outs
        nr += p.n_remote
        nl += p.n_local

    def run(which):
        def go(in_refs, out_refs, send, recv, loc):
            for p, i0, o0, r0, l0 in parts:
                getattr(p, which)(in_refs[i0:i0 + len(p.ins)], out_refs[o0:o0 + len(p.outs)],
                                  lambda i, r0=r0: send(r0 + i), lambda i, r0=r0: recv(r0 + i),
                                  lambda i, l0=l0: loc(l0 + i))
        return go

    return _Plan(ins, outs, nr, nl, run("start"), run("wait"), aliases)


def _plan_io(plan, n_in, n_out):
    if plan is None:
        return [], [], [], [], {}
    sems = [pltpu.SemaphoreType.DMA((max(plan.n_remote, 1),)), pltpu.SemaphoreType.DMA((max(plan.n_remote, 1),)),
            pltpu.SemaphoreType.DMA((max(plan.n_local, 1),))]
    aliases = {n_in + i: n_out + o for i, o in plan.aliases.items()}
    return plan.ins, [_ANY] * len(plan.outs), plan.outs, sems, aliases


def _split_refs(refs, n_in, n_out, n_scr, plan):
    p_in, p_out = (len(plan.ins), len(plan.outs)) if plan is not None else (0, 0)
    refs = list(refs)
    ins, refs = refs[:n_in], refs[n_in:]
    pins, refs = refs[:p_in], refs[p_in:]
    outs, refs = refs[:n_out], refs[n_out:]
    pouts, refs = refs[:p_out], refs[p_out:]
    scr, psem = refs[:n_scr], refs[n_scr:]
    psem = tuple((lambda i, s=s: s.at[i]) for s in psem)
    return ins, outs, scr, (pins, pouts, psem)


def _grid_edge(grid, last):
    cond = None
    for ax, n in enumerate(grid):
        c = pl.program_id(ax) == (n - 1 if last else 0)
        cond = c if cond is None else cond & c
    return cond


def _plan_start(plan, pctx, grid):
    if plan is not None:
        pins, pouts, psem = pctx
        pl.when(_grid_edge(grid, False))(lambda: plan.start(pins, pouts, *psem))


def _plan_wait(plan, pctx, grid):
    if plan is not None:
        pins, pouts, psem = pctx
        pl.when(_grid_edge(grid, True))(lambda: plan.wait(pins, pouts, *psem))


def _run_plan(plan, *, name):
    def body(*refs):
        _, _, _, (pins, pouts, psem) = _split_refs(refs, 0, 0, 0, plan)
        plan.start(pins, pouts, *psem)
        plan.wait(pins, pouts, *psem)

    p_in, p_ospec, p_oshape, p_scr, p_alias = _plan_io(plan, 0, 0)
    return pl.pallas_call(body, name=name, in_specs=[_ANY] * len(p_in), out_specs=p_ospec, out_shape=p_oshape,
                          scratch_shapes=p_scr, input_output_aliases=p_alias)(*p_in)


def _fold8(x):
    return x.reshape(x.shape[0] // 8, 8, x.shape[1]).sum(axis=0)


def _ln_stats(r):
    mu = jnp.mean(r, -1, keepdims=True)
    xc = r - mu
    rstd = lax.rsqrt(jnp.mean(xc * xc, -1, keepdims=True) + EPS)
    return xc * rstd, rstd


def _sigmoid(x):
    return jax.nn.sigmoid(x)


def _gelu(x):
    return 0.5 * x * (1.0 + jnp.tanh(GELU_C * (x + GELU_A * x * x * x)))


def _gelu_grad(x):
    t = jnp.tanh(GELU_C * (x + GELU_A * x * x * x))
    return 0.5 * (1.0 + t) + 0.5 * x * (1.0 - t * t) * GELU_C * (1.0 + 3.0 * GELU_A * x * x)


MM_ROWS = 1024
DW_TOKENS = 2048


def _matmul(a, b, *, name, M, N, K, ta=False, tb=False, out_dtype=F32, tm=MM_ROWS, tn=1024, tk=1024,
            a_spec=None, b_spec=None, b_merge=None, out_shape=None, o_spec=None, into=None,
            a_sq=False, mul=None, add=None, add_scale=1.0, plan=None):
    tm, tn, tk = min(tm, M), min(tn, N), min(tk, K)
    assert M % tm == 0 and N % tn == 0 and K % tk == 0
    grid = (M // tm, N // tn, K // tk)
    nk = grid[2]
    if a_spec is None:
        a_spec = pl.BlockSpec((tk, tm), lambda i, j, k: (k, i)) if ta else pl.BlockSpec((tm, tk), lambda i, j, k: (i, k))
    if b_spec is None:
        b_spec = pl.BlockSpec((tn, tk), lambda i, j, k: (j, k)) if tb else pl.BlockSpec((tk, tn), lambda i, j, k: (k, j))
    if o_spec is None:
        o_spec = pl.BlockSpec((tm, tn), lambda i, j, k: (i, j))
        out_shape = jax.ShapeDtypeStruct((M, N), out_dtype)
    e_spec = pl.BlockSpec((tm, tn), lambda i, j, k: (i, j))
    dims = (((0 if ta else 1,), (1 if tb else 0,)), ((), ()))
    extra = [e for e in (mul, add, into) if e is not None]
    n_in = 2 + len(extra)

    def body(*refs):
        ins, outs, scr, pctx = _split_refs(refs, n_in, 1, 1 if nk > 1 else 0, plan)
        a_ref, b_ref = ins[0], ins[1]
        rest = list(ins[2:])
        mul_ref = rest.pop(0) if mul is not None else None
        add_ref = rest.pop(0) if add is not None else None
        o_ref = outs[0]
        _plan_start(plan, pctx, grid)
        av = a_ref[...].astype(BF16)
        if a_sq:
            av = av * av
        bv = b_ref[...]
        if b_merge is not None:
            bv = bv.reshape(b_merge)
        p = lax.dot_general(av, bv, dims, preferred_element_type=F32)

        def finish(r):
            if mul_ref is not None:
                r = r * (2.0 * mul_ref[...].astype(F32))
            if add_ref is not None:
                r = r + add_scale * add_ref[...]
            o_ref[...] = r.astype(o_ref.dtype)

        if nk == 1:
            finish(p)
        else:
            acc_ref = scr[0]
            k = pl.program_id(2)

            @pl.when(k == 0)
            def _():
                acc_ref[...] = p

            @pl.when(k > 0)
            def _():
                acc_ref[...] += p

            @pl.when(k == nk - 1)
            def _():
                finish(acc_ref[...])

        _plan_wait(plan, pctx, grid)

    p_in, p_ospec, p_oshape, p_scr, p_alias = _plan_io(plan, n_in, 1)
    aliases = dict(p_alias)
    if into is not None:
        aliases[n_in - 1] = 0
    return pl.pallas_call(
        body, name=name, grid=grid,
        in_specs=[a_spec, b_spec] + [e_spec] * (len(extra) - (into is not None)) + [_ANY] * (into is not None)
        + [_ANY] * len(p_in),
        out_specs=[o_spec] + p_ospec, out_shape=[out_shape] + p_oshape,
        scratch_shapes=([pltpu.VMEM((tm, tn), F32)] if nk > 1 else []) + p_scr,
        input_output_aliases=aliases, compiler_params=_params(48, 3),
    )(*_hbm(a, b, *extra), *p_in)


def _rows4_spec(rowblk, n_axes):
    return pl.BlockSpec((4, 256, D), lambda *_: (0, rowblk, 0))


def _proj_ln(a_b, w, h_prev, g, b, *, name, w_rowblk=None, plan=None):
    T = a_b.shape[0]
    tm = min(ROW_BLOCK, T)
    grid = (T // tm,)
    row = pl.BlockSpec((tm, D), lambda i: (i, 0))
    vec = pl.BlockSpec((1, D), lambda i: (0, 0))
    w_spec = pl.BlockSpec((D, D), lambda i: (0, 0)) if w_rowblk is None else _rows4_spec(w_rowblk, 1)

    def body(*refs):
        (a_ref, w_ref, h_ref, g_ref, b_ref), (r_ref, ho_ref, hb_ref), _, pctx = _split_refs(refs, 5, 3, 0, plan)
        _plan_start(plan, pctx, grid)
        mix = jnp.dot(a_ref[...], w_ref[...].reshape(D, D), preferred_element_type=F32)
        r = ALPHA * h_ref[...] + mix
        xhat, _ = _ln_stats(r)
        y = xhat * g_ref[...] + b_ref[...]
        r_ref[...] = r
        ho_ref[...] = y
        hb_ref[...] = y.astype(BF16)
        _plan_wait(plan, pctx, grid)

    p_in, p_ospec, p_oshape, p_scr, p_alias = _plan_io(plan, 5, 3)
    return pl.pallas_call(
        body, name=name, grid=grid,
        in_specs=[row, w_spec, row, vec, vec] + [_ANY] * len(p_in),
        out_specs=[row, row, row] + p_ospec,
        out_shape=[jax.ShapeDtypeStruct((T, D), F32), jax.ShapeDtypeStruct((T, D), F32),
                   jax.ShapeDtypeStruct((T, D), BF16)] + p_oshape,
        scratch_shapes=p_scr, input_output_aliases=p_alias, compiler_params=_params(40, 1),
    )(*_hbm(a_b, w, h_prev, g, b), *p_in)


def _ffn_ln(h_b, wbuf, h, g, b, *, name, plan=None):
    T = h_b.shape[0]
    tm, tf = min(ROW_BLOCK, T), 1024
    nf = 4
    F = nf * tf
    grid = (T // tm, nf)
    row = pl.BlockSpec((tm, D), lambda i, j: (i, 0))
    vec = pl.BlockSpec((1, D), lambda i, j: (0, 0))

    def body(*refs):
        ((hb_ref, w1_ref, w2_ref, h_ref, g_ref, b_ref), (ra_ref, r_ref, ho_ref, hbo_ref), (acc_ref,),
         pctx) = _split_refs(refs, 6, 4, 1, plan)
        _plan_start(plan, pctx, grid)
        j = pl.program_id(1)
        a = jnp.dot(hb_ref[...], w1_ref[...], preferred_element_type=F32)
        ra = jnp.maximum(a, 0.0)
        ra_ref[...] = ra.astype(BF16)
        p = jnp.dot((ra * ra).astype(BF16), w2_ref[...], preferred_element_type=F32)

        @pl.when(j == 0)
        def _():
            acc_ref[...] = p

        @pl.when(j > 0)
        def _():
            acc_ref[...] += p

        @pl.when(j == nf - 1)
        def _():
            r = ALPHA * h_ref[...] + acc_ref[...]
            xhat, _ = _ln_stats(r)
            y = xhat * g_ref[...] + b_ref[...]
            r_ref[...] = r
            ho_ref[...] = y
            hbo_ref[...] = y.astype(BF16)

        _plan_wait(plan, pctx, grid)

    p_in, p_ospec, p_oshape, p_scr, p_alias = _plan_io(plan, 6, 4)
    return pl.pallas_call(
        body, name=name, grid=grid,
        in_specs=[row, pl.BlockSpec((None, D, tf), lambda i, j: (j, 0, 0)),
                  pl.BlockSpec((None, tf, D), lambda i, j: (j, 1, 0)), row, vec, vec] + [_ANY] * len(p_in),
        out_specs=[pl.BlockSpec((tm, tf), lambda i, j: (i, j)), row, row, row] + p_ospec,
        out_shape=[jax.ShapeDtypeStruct((T, F), BF16), jax.ShapeDtypeStruct((T, D), F32),
                   jax.ShapeDtypeStruct((T, D), F32), jax.ShapeDtypeStruct((T, D), BF16)] + p_oshape,
        scratch_shapes=[pltpu.VMEM((tm, D), F32)] + p_scr,
        input_output_aliases=p_alias, compiler_params=_params(48, 2),
    )(*_hbm(h_b, wbuf, wbuf, h, g, b), *p_in)


def _ln_bwd(dy, r, g, *, name, tgt=None):
    T = dy.shape[0]
    tm = min(ROW_BLOCK, T)
    row = pl.BlockSpec((tm, D), lambda i: (i, 0))
    acc = pl.BlockSpec((8, D), lambda i: (0, 0))
    n_in = 3 + (tgt is not None)

    def body(*refs):
        dy_ref, r_ref, g_ref = refs[:3]
        dr_ref, drb_ref, dg_ref, db_ref = refs[n_in:n_in + 4]

        @pl.when(pl.program_id(0) == 0)
        def _():
            for ref in refs[n_in + 2:]:
                ref[...] = jnp.zeros_like(ref)

        dy_ = dy_ref[...]
        if tgt is not None:
            err = dy_ - refs[3][...]
            refs[n_in + 4][...] += _fold8(err * err)
            dy_ = err * (1.0 / D)
        xhat, rstd = _ln_stats(r_ref[...])
        dxh = dy_ * g_ref[...]
        m1 = jnp.mean(dxh, -1, keepdims=True)
        m2 = jnp.mean(dxh * xhat, -1, keepdims=True)
        dr = rstd * (dxh - m1 - xhat * m2)
        dr_ref[...] = dr
        drb_ref[...] = dr.astype(BF16)
        dg_ref[...] += _fold8(dy_ * xhat)
        db_ref[...] += _fold8(dy_)

    extra = [] if tgt is None else [tgt]
    return pl.pallas_call(
        body, name=name, grid=(T // tm,),
        in_specs=[row, row, pl.BlockSpec((1, D), lambda i: (0, 0))] + [row] * len(extra),
        out_specs=[row, row, acc, acc] + [acc] * len(extra),
        out_shape=[jax.ShapeDtypeStruct((T, D), F32), jax.ShapeDtypeStruct((T, D), BF16)]
        + [jax.ShapeDtypeStruct((8, D), F32)] * (2 + len(extra)),
        compiler_params=_params(40, 1),
    )(*_hbm(dy, r, g, *extra))


def _rope(x, c, s1, s2):
    return x * c + pltpu.roll(x, 112, 1) * s1 + pltpu.roll(x, 16, 1) * s2


def _rope_t(dy, c, s1, s2):
    return dy * c + pltpu.roll(dy * s1, 16, 1) + pltpu.roll(dy * s2, 112, 1)


def _rms(x, g):
    rstd = lax.rsqrt(jnp.mean(x * x, -1, keepdims=True) + EPS)
    xhat = x * rstd
    return xhat * g, xhat, rstd


def _mla_prep(z0, gq, gkv, wq, wk, wv, rc, rs1, rs2):
    T = z0.shape[0]
    tm = min(ROW_BLOCK, T)
    HW = HEADS * 128

    def body(cq_ref, ckv_ref, kr_ref, gq_ref, gkv_ref, wq_ref, wk_ref, wv_ref, c_ref, s1_ref, s2_ref,
             q_ref, k_ref, v_ref):
        nq = _rms(cq_ref[...], gq_ref[...])[0].astype(BF16)
        nkv = _rms(ckv_ref[...], gkv_ref[...])[0].astype(BF16)
        q = jnp.dot(nq, wq_ref[...], preferred_element_type=F32)
        k = jnp.dot(nkv, wk_ref[...], preferred_element_type=F32)
        v = jnp.dot(nkv, wv_ref[...], preferred_element_type=F32)
        c, s1, s2 = c_ref[...], s1_ref[...], s2_ref[...]
        kr = _rope(pltpu.roll(kr_ref[...], 64, 1), c, s1, s2)
        for h in range(HEADS):
            sl = slice(h * 128, (h + 1) * 128)
            q_ref[:, sl] = (_rope(q[:, sl], c, s1, s2) * QK_SCALE).astype(BF16)
            k_ref[:, sl] = (k[:, sl] + kr).astype(BF16)
        v_ref[...] = v.astype(BF16)

    full = lambda shape: pl.BlockSpec(shape, lambda i: (0, 0))
    tab = pl.BlockSpec((tm, 128), lambda i: (i, 0))
    return pl.pallas_call(
        body, name="mla_prep", grid=(T // tm,),
        in_specs=[pl.BlockSpec((tm, 256), lambda i: (i, 0)), pl.BlockSpec((tm, 256), lambda i: (i, 1)),
                  pl.BlockSpec((tm, 128), lambda i: (i, 12)), full((1, 256)), full((1, 256)),
                  full((256, HW)), full((256, HW)), full((256, 512)), tab, tab, tab],
        out_specs=[pl.BlockSpec((tm, HW), lambda i: (i, 0)), pl.BlockSpec((tm, HW), lambda i: (i, 0)),
                   pl.BlockSpec((tm, 512), lambda i: (i, 0))],
        out_shape=[jax.ShapeDtypeStruct((T, HW), BF16), jax.ShapeDtypeStruct((T, HW), BF16),
                   jax.ShapeDtypeStruct((T, 512), BF16)],
        compiler_params=_params(40, 1),
    )(z0, z0, z0, gq, gkv, wq, wk, wv, rc, rs1, rs2)


def _flash_fwd(q, k, v, plan=None):
    T = q.shape[0]
    bq = min(2 * ROW_BLOCK, T)
    nq = T // bq
    grid = (4, nq, nq)

    def body(*refs):
        (q_ref, k_ref, v_ref), (o_ref, lse_ref), (m_sc, acc_sc), pctx = _split_refs(refs, 3, 2, 2, plan)
        _plan_start(plan, pctx, grid)
        i, j = pl.program_id(1), pl.program_id(2)
        first = lax.broadcasted_iota(jnp.int32, (bq, 128), 1) < 64

        @pl.when(j == 0)
        def _():
            m_sc[...] = jnp.full_like(m_sc, -jnp.inf)
            acc_sc[...] = jnp.zeros_like(acc_sc)

        def step(masked):
            vp = v_ref[...]
            for h in range(2):
                sl = slice(h * 128, (h + 1) * 128)
                s = lax.dot_general(q_ref[:, sl], k_ref[:, sl], NT_DIMS, preferred_element_type=F32)
                if masked:
                    rows = lax.broadcasted_iota(jnp.int32, (bq, bq), 0)
                    cols = lax.broadcasted_iota(jnp.int32, (bq, bq), 1)
                    s = jnp.where(cols <= rows, s, -jnp.inf)
                m_prev = m_sc[h, :, 0:1]
                m_new = jnp.maximum(m_prev, jnp.max(s, axis=1, keepdims=True))
                alpha = jnp.exp(m_prev - m_new)
                p = jnp.exp(s - m_new).astype(BF16)
                mine = first if h == 0 else jnp.logical_not(first)
                vh = jnp.where(mine, vp, jnp.ones_like(vp))
                acc_sc[h] = acc_sc[h] * alpha + jnp.dot(p, vh, preferred_element_type=F32)
                m_sc[h] = jnp.broadcast_to(m_new, (bq, 128))

        @pl.when(j < i)
        def _():
            step(False)

        @pl.when(j == i)
        def _():
            step(True)
            a0, a1 = acc_sc[0], acc_sc[1]
            l0, l1 = pltpu.roll(a0, 64, 1), pltpu.roll(a1, 64, 1)
            o_ref[...] = jnp.where(first, a0 / l0, a1 / l1).astype(BF16)
            lse_ref[...] = jnp.where(first, m_sc[0] + jnp.log(l0), m_sc[1] + jnp.log(l1))

        _plan_wait(plan, pctx, grid)

    kv = lambda hp, i, j: (jnp.minimum(i, j), hp)
    p_in, p_ospec, p_oshape, p_scr, p_alias = _plan_io(plan, 3, 2)
    return pl.pallas_call(
        body, name="flash_fwd", grid=grid,
        in_specs=[pl.BlockSpec((bq, 256), lambda hp, i, j: (i, hp)), pl.BlockSpec((bq, 256), kv),
                  pl.BlockSpec((bq, 128), kv)] + [_ANY] * len(p_in),
        out_specs=[pl.BlockSpec((bq, 128), lambda hp, i, j: (i, hp)),
                   pl.BlockSpec((bq, 128), lambda hp, i, j: (i, hp))] + p_ospec,
        out_shape=[jax.ShapeDtypeStruct((T, 512), BF16), jax.ShapeDtypeStruct((T, 512), F32)] + p_oshape,
        scratch_shapes=[pltpu.VMEM((2, bq, 128), F32), pltpu.VMEM((2, bq, 128), F32)] + p_scr,
        input_output_aliases=p_alias, compiler_params=_params(56, 3),
    )(*_hbm(q, k, v), *p_in)


def _attn_delta(dmix, o):
    T = o.shape[0]
    tm = min(ROW_BLOCK, T)
    blk = pl.BlockSpec((tm, 512), lambda i: (i, 0))

    def body(do_ref, o_ref, delta_ref, dob_ref):
        first = lax.broadcasted_iota(jnp.int32, (tm, 128), 1) < 64
        for hp in range(4):
            sl = slice(hp * 128, (hp + 1) * 128)
            prod = do_ref[:, sl] * o_ref[:, sl].astype(F32)
            d0 = jnp.sum(jnp.where(first, prod, 0.0), axis=1, keepdims=True)
            d1 = jnp.sum(jnp.where(first, 0.0, prod), axis=1, keepdims=True)
            delta_ref[:, sl] = jnp.where(first, d0, d1)
        dob_ref[...] = do_ref[...].astype(BF16)

    return pl.pallas_call(
        body, name="attn_delta", grid=(T // tm,), in_specs=[blk, blk], out_specs=[blk, blk],
        out_shape=[jax.ShapeDtypeStruct((T, 512), F32), jax.ShapeDtypeStruct((T, 512), BF16)],
        compiler_params=_params(32, 1),
    )(dmix, o)


def _flash_bwd(q, k, v, do_b, lse, delta, plan=None):
    T = q.shape[0]
    bq = min(2 * ROW_BLOCK, T)
    nq = T // bq
    grid = (4, nq, nq)

    def body(*refs):
        ((q_ref, k_ref, v_ref, do_ref, lse_ref, dl_ref), (dq_hbm, dk_ref, dv_ref), (dq_sc, dk_sc, dv_sc, sem),
         pctx) = _split_refs(refs, 6, 3, 4, plan)
        _plan_start(plan, pctx, grid)
        hp, j, i = pl.program_id(0), pl.program_id(1), pl.program_id(2)
        first = lax.broadcasted_iota(jnp.int32, (bq, 128), 1) < 64

        @pl.when((j == 0) & (i == 0))
        def _():
            dq_sc[...] = jnp.zeros_like(dq_sc)

        @pl.when(i == j)
        def _():
            dk_sc[...] = jnp.zeros_like(dk_sc)
            dv_sc[...] = jnp.zeros_like(dv_sc)

        def step(masked):
            vp = v_ref[...]
            do = do_ref[...]
            for h in range(2):
                sl = slice(h * 128, (h + 1) * 128)
                qh, kh = q_ref[:, sl], k_ref[:, sl]
                s = lax.dot_general(qh, kh, NT_DIMS, preferred_element_type=F32)
                p = jnp.exp(s - lse_ref[:, h * 64:h * 64 + 1])
                if masked:
                    rows = lax.broadcasted_iota(jnp.int32, (bq, bq), 0)
                    cols = lax.broadcasted_iota(jnp.int32, (bq, bq), 1)
                    p = jnp.where(cols <= rows, p, 0.0)
                mine = first if h == 0 else jnp.logical_not(first)
                do_h = jnp.where(mine, do, jnp.zeros_like(do))
                dv_sc[...] += lax.dot_general(p.astype(BF16), do_h, TN_DIMS, preferred_element_type=F32)
                dp = lax.dot_general(do_h, vp, NT_DIMS, preferred_element_type=F32)
                ds = (p * (dp - dl_ref[:, h * 64:h * 64 + 1])).astype(BF16)
                dq_sc[i, :, sl] += jnp.dot(ds, kh, preferred_element_type=F32)
                dk_sc[:, sl] += lax.dot_general(ds, qh, TN_DIMS, preferred_element_type=F32)

        @pl.when(i > j)
        def _():
            step(False)

        @pl.when(i == j)
        def _():
            step(True)

        @pl.when(i == nq - 1)
        def _():
            dk_ref[...] = dk_sc[...]
            dv_ref[...] = dv_sc[...]

        @pl.when((j == nq - 1) & (i == nq - 1))
        def _():
            cp = pltpu.make_async_copy(dq_sc, dq_hbm.at[hp], sem)
            cp.start()
            cp.wait()

        _plan_wait(plan, pctx, grid)

    qi = lambda hp, j, i: (jnp.maximum(i, j), hp)
    kj = lambda hp, j, i: (j, hp)
    p_in, p_ospec, p_oshape, p_scr, p_alias = _plan_io(plan, 6, 3)
    return pl.pallas_call(
        body, name="flash_bwd", grid=grid,
        in_specs=[pl.BlockSpec((bq, 256), qi), pl.BlockSpec((bq, 256), kj), pl.BlockSpec((bq, 128), kj),
                  pl.BlockSpec((bq, 128), qi), pl.BlockSpec((bq, 128), qi), pl.BlockSpec((bq, 128), qi)]
        + [_ANY] * len(p_in),
        out_specs=[_ANY, pl.BlockSpec((bq, 256), kj), pl.BlockSpec((bq, 128), kj)] + p_ospec,
        out_shape=[jax.ShapeDtypeStruct((4, nq, bq, 256), F32), jax.ShapeDtypeStruct((T, 1024), F32),
                   jax.ShapeDtypeStruct((T, 512), F32)] + p_oshape,
        scratch_shapes=[pltpu.VMEM((nq, bq, 256), F32), pltpu.VMEM((bq, 256), F32), pltpu.VMEM((bq, 128), F32),
                        pltpu.SemaphoreType.DMA] + p_scr,
        input_output_aliases=p_alias, compiler_params=_params(56, 3),
    )(*_hbm(q, k, v, do_b, lse, delta), *p_in)


def _mla_bwd(z0, dq4, dk, dv, gq, gkv, wq, wk, wv, rc, rs1, rs2, plan=None):
    T = z0.shape[0]
    tm = min(ROW_BLOCK, T)
    HW = HEADS * 128
    grid = (T // tm,)
    dq4 = dq4.reshape(4, T, 256)

    def body(*refs):
        ((cq_ref, ckv_ref, dq_ref, dk_ref, dv_ref, gq_ref, gkv_ref, wq_ref, wk_ref, wv_ref, c_ref, s1_ref, s2_ref),
         (dc_ref, dkr_ref, dwq_ref, dwk_ref, dwv_ref, dgq_ref, dgkv_ref), _, pctx) = _split_refs(refs, 13, 7, 0, plan)
        _plan_start(plan, pctx, grid)

        @pl.when(pl.program_id(0) == 0)
        def _():
            for ref in (dwq_ref, dwk_ref, dwv_ref, dgq_ref, dgkv_ref):
                ref[...] = jnp.zeros_like(ref)

        c, s1, s2 = c_ref[...], s1_ref[...], s2_ref[...]
        lane = lax.broadcasted_iota(jnp.int32, (tm, 128), 1)
        nq, xq, rq = _rms(cq_ref[...], gq_ref[...])
        nkv, xkv, rkv = _rms(ckv_ref[...], gkv_ref[...])
        nq_b, nkv_b = nq.astype(BF16), nkv.astype(BF16)

        dq_parts, dk_parts = [], []
        dkr = jnp.zeros((tm, 128), F32)
        for h in range(HEADS):
            blk = dq_ref[h // 2, :, (h % 2) * 128:(h % 2 + 1) * 128] * QK_SCALE
            dq_parts.append(_rope_t(blk, c, s1, s2).astype(BF16))
            kb = dk_ref[:, h * 128:(h + 1) * 128]
            dk_parts.append(jnp.where(lane < NOPE, kb, 0.0).astype(BF16))
            dkr = dkr + kb
        dq_b = jnp.concatenate(dq_parts, axis=1)
        dk_b = jnp.concatenate(dk_parts, axis=1)
        dv_b = dv_ref[...].astype(BF16)

        dwq_ref[...] += lax.dot_general(nq_b, dq_b, TN_DIMS, preferred_element_type=F32)
        dwk_ref[...] += lax.dot_general(nkv_b, dk_b, TN_DIMS, preferred_element_type=F32)
        dwv_ref[...] += lax.dot_general(nkv_b, dv_b, TN_DIMS, preferred_element_type=F32)
        dnq = lax.dot_general(dq_b, wq_ref[...], NT_DIMS, preferred_element_type=F32)
        dnkv = (lax.dot_general(dk_b, wk_ref[...], NT_DIMS, preferred_element_type=F32)
                + lax.dot_general(dv_b, wv_ref[...], NT_DIMS, preferred_element_type=F32))

        def rms_bwd(dn, xhat, rstd, g):
            dxh = dn * g
            return rstd * (dxh - xhat * jnp.mean(dxh * xhat, -1, keepdims=True))

        dc_ref[:, :256] = rms_bwd(dnq, xq, rq, gq_ref[...]).astype(BF16)
        dc_ref[:, 256:] = rms_bwd(dnkv, xkv, rkv, gkv_ref[...]).astype(BF16)
        dgq_ref[...] += _fold8(dnq * xq)
        dgkv_ref[...] += _fold8(dnkv * xkv)
        dkr = pltpu.roll(_rope_t(dkr, c, s1, s2), 64, 1)
        dkr_ref[...] = jnp.where(lane < ROPE, dkr, 0.0).astype(BF16)
        _plan_wait(plan, pctx, grid)

    full = lambda shape: pl.BlockSpec(shape, lambda i: (0,) * len(shape))
    tab = pl.BlockSpec((tm, 128), lambda i: (i, 0))
    p_in, p_ospec, p_oshape, p_scr, p_alias = _plan_io(plan, 13, 7)
    return pl.pallas_call(
        body, name="mla_bwd", grid=grid,
        in_specs=[pl.BlockSpec((tm, 256), lambda i: (i, 0)), pl.BlockSpec((tm, 256), lambda i: (i, 1)),
                  pl.BlockSpec((4, tm, 256), lambda i: (0, i, 0)),
                  pl.BlockSpec((tm, HW), lambda i: (i, 0)), pl.BlockSpec((tm, 512), lambda i: (i, 0)),
                  full((1, 256)), full((1, 256)), full((256, HW)), full((256, HW)), full((256, 512)), tab, tab, tab]
        + [_ANY] * len(p_in),
        out_specs=[pl.BlockSpec((tm, 512), lambda i: (i, 0)), tab, full((256, HW)), full((256, HW)),
                   full((256, 512)), full((8, 256)), full((8, 256))] + p_ospec,
        out_shape=[jax.ShapeDtypeStruct((T, 512), BF16), jax.ShapeDtypeStruct((T, 128), BF16),
                   jax.ShapeDtypeStruct((256, HW), F32), jax.ShapeDtypeStruct((256, HW), F32),
                   jax.ShapeDtypeStruct((256, 512), F32), jax.ShapeDtypeStruct((8, 256), F32),
                   jax.ShapeDtypeStruct((8, 256), F32)] + p_oshape,
        scratch_shapes=p_scr, input_output_aliases=p_alias, compiler_params=_params(48, 1),
    )(*_hbm(z0, z0, dq4, dk, dv, gq, gkv, wq, wk, wv, rc, rs1, rs2), *p_in)


def _sgu_fwd(z0, a_out, ln_g, ln_b, w, b_t):
    T = z0.shape[0]
    tm = min(ROW_BLOCK, T)
    W = SGU_G * SGU_C

    def body(u_ref, v_ref, a_ref, g_ref, b_ref, w_ref, bt_ref, o_ref):
        o_ref[:, :W] = a_ref[...]
        ug = _gelu(u_ref[...])
        xhat, _ = _ln_stats(_gelu(v_ref[...]))
        vn = (xhat * g_ref[...] + b_ref[...]).astype(BF16)
        tril = lax.broadcasted_iota(jnp.int32, (SGU_C, SGU_C), 0) >= lax.broadcasted_iota(jnp.int32, (SGU_C, SGU_C), 1)
        for g in range(SGU_G):
            cs = slice(g * SGU_C, (g + 1) * SGU_C)
            wg = jnp.where(tril, w_ref[g], 0.0).astype(BF16)
            bcol = bt_ref[:, g:g + 1]
            for c in range(tm // SGU_C):
                rs = slice(c * SGU_C, (c + 1) * SGU_C)
                mixed = jnp.dot(wg, vn[rs, cs], preferred_element_type=F32) + bcol
                o_ref[rs, W + g * SGU_C:W + (g + 1) * SGU_C] = (ug[rs, cs] * mixed).astype(BF16)

    full = lambda shape: pl.BlockSpec(shape, lambda i: (0,) * len(shape))
    return pl.pallas_call(
        body, name="sgu_fwd", grid=(T // tm,),
        in_specs=[pl.BlockSpec((tm, W), lambda i: (i, 1)), pl.BlockSpec((tm, W), lambda i: (i, 2)),
                  pl.BlockSpec((tm, W), lambda i: (i, 0)),
                  full((1, W)), full((1, W)), full((SGU_G, SGU_C, SGU_C)), full((SGU_C, SGU_G))],
        out_specs=pl.BlockSpec((tm, 2 * W), lambda i: (i, 0)),
        out_shape=jax.ShapeDtypeStruct((T, 2 * W), BF16),
        compiler_params=_params(32, 1),
    )(z0, z0, a_out, ln_g, ln_b, w, b_t)


def _sgu_bwd(z0, dmix, dc, dkr, ln_g, ln_b, w, b_t):
    T = z0.shape[0]
    tm = min(ROW_BLOCK, T)
    W = SGU_G * SGU_C

    def body(u_ref, v_ref, do_ref, dc_ref, dkr_ref, g_ref, b_ref, w_ref, bt_ref, dz_ref, dw_ref, db_ref, dlg_ref,
             dlb_ref):
        @pl.when(pl.program_id(0) == 0)
        def _():
            for ref in (dw_ref, db_ref, dlg_ref, dlb_ref):
                ref[...] = jnp.zeros_like(ref)

        dz_ref[:, :W] = dc_ref[...]
        dz_ref[:, 3 * W:] = dkr_ref[...]

        u, v, dout = u_ref[...], v_ref[...], do_ref[...]
        ug = _gelu(u)
        xhat, rstd = _ln_stats(_gelu(v))
        vn = (xhat * g_ref[...] + b_ref[...]).astype(BF16)
        dmixed = dout * ug
        dmixed_b = dmixed.astype(BF16)
        tril = lax.broadcasted_iota(jnp.int32, (SGU_C, SGU_C), 0) >= lax.broadcasted_iota(jnp.int32, (SGU_C, SGU_C), 1)
        lane = lax.broadcasted_iota(jnp.int32, (SGU_C, SGU_C), 1)
        dvn_cols = []
        for g in range(SGU_G):
            cs = slice(g * SGU_C, (g + 1) * SGU_C)
            wg = jnp.where(tril, w_ref[g], 0.0).astype(BF16)
            bcol = bt_ref[:, g:g + 1]
            dw_g = jnp.zeros((SGU_C, SGU_C), F32)
            db_g = jnp.zeros((SGU_C, 1), F32)
            dvn_rows = []
            for c in range(tm // SGU_C):
                rs = slice(c * SGU_C, (c + 1) * SGU_C)
                mixed = jnp.dot(wg, vn[rs, cs], preferred_element_type=F32) + bcol
                dz_ref[rs, W + g * SGU_C:W + (g + 1) * SGU_C] = (dout[rs, cs] * mixed * _gelu_grad(u[rs, cs])).astype(BF16)
                dm = dmixed_b[rs, cs]
                dw_g = dw_g + lax.dot_general(dm, vn[rs, cs], NT_DIMS, preferred_element_type=F32)
                db_g = db_g + jnp.sum(dmixed[rs, cs], axis=1, keepdims=True)
                dvn_rows.append(lax.dot_general(wg, dm, TN_DIMS, preferred_element_type=F32))
            dw_ref[g] += jnp.where(tril, dw_g, 0.0)
            db_ref[...] += jnp.where(lane == g, db_g, 0.0)
            dvn_cols.append(jnp.concatenate(dvn_rows, axis=0))
        dvn = jnp.concatenate(dvn_cols, axis=1)
        dxh = dvn * g_ref[...]
        m1 = jnp.mean(dxh, -1, keepdims=True)
        m2 = jnp.mean(dxh * xhat, -1, keepdims=True)
        dvg = rstd * (dxh - m1 - xhat * m2)
        dz_ref[:, 2 * W:3 * W] = (dvg * _gelu_grad(v)).astype(BF16)
        dlg_ref[...] += _fold8(dvn * xhat)
        dlb_ref[...] += _fold8(dvn)

    full = lambda shape: pl.BlockSpec(shape, lambda i: (0,) * len(shape))
    return pl.pallas_call(
        body, name="sgu_bwd", grid=(T // tm,),
        in_specs=[pl.BlockSpec((tm, W), lambda i: (i, 1)), pl.BlockSpec((tm, W), lambda i: (i, 2)),
                  pl.BlockSpec((tm, W), lambda i: (i, 1)), pl.BlockSpec((tm, W), lambda i: (i, 0)),
                  pl.BlockSpec((tm, 128), lambda i: (i, 0)),
                  full((1, W)), full((1, W)), full((SGU_G, SGU_C, SGU_C)), full((SGU_C, SGU_G))],
        out_specs=[pl.BlockSpec((tm, 3 * W + 128), lambda i: (i, 0)), full((SGU_G, SGU_C, SGU_C)),
                   full((SGU_C, SGU_C)), full((8, W)), full((8, W))],
        out_shape=[jax.ShapeDtypeStruct((T, 3 * W + 128), BF16), jax.ShapeDtypeStruct((SGU_G, SGU_C, SGU_C), F32),
                   jax.ShapeDtypeStruct((SGU_C, SGU_C), F32), jax.ShapeDtypeStruct((8, W), F32),
                   jax.ShapeDtypeStruct((8, W), F32)],
        compiler_params=_params(40, 1),
    )(z0, z0, dmix, dc, dkr, ln_g, ln_b, w, b_t)


def _hg_lower_bound(lb_ref):
    a0, a1 = lb_ref[0:1, :], lb_ref[1:2, :]
    m = jnp.maximum(a0, a1)
    e0, e1 = jnp.exp(a0 - m), jnp.exp(a1 - m)
    return e1 / (e0 + e1)


def _running_sum(x, reverse=False):
    n = x.shape[0]
    row = lax.broadcasted_iota(jnp.int32, x.shape, 0)
    s = 1
    while s < n:
        if reverse:
            x = x + jnp.where(row < n - s, pltpu.roll(x, n - s, 0), 0.0)
        else:
            x = x + jnp.where(row >= s, pltpu.roll(x, s, 0), 0.0)
        s *= 2
    return x


def _hg_chunk(qc, fc, lb):
    C = HG_CHUNK
    rows = lax.broadcasted_iota(jnp.int32, (C, C), 0)
    cols = lax.broadcasted_iota(jnp.int32, (C, C), 1)
    rowid = lax.broadcasted_iota(jnp.int32, (C, 128), 0)
    sq, sg = _sigmoid(qc), _sigmoid(fc)
    qf = qc * sq
    gate = lb + (1.0 - lb) * sg
    kk = 1.0 - gate
    lg = jnp.log(gate)
    bcum = _running_sum(lg)
    b_mid = jnp.sum(jnp.where(rowid < C // 2, lg, 0.0), axis=0, keepdims=True)
    b_last = jnp.sum(lg, axis=0, keepdims=True)
    eq, ek, e, eh = jnp.exp(bcum - b_mid), jnp.exp(b_mid - bcum), jnp.exp(bcum), jnp.exp(b_last - bcum)
    qt, kt, qe, khat = qf * eq, kk * ek, qf * e, kk * eh
    a = lax.dot_general(qt.astype(BF16), kt.astype(BF16), NT_DIMS, preferred_element_type=F32)
    a = jnp.where(rows >= cols, a, 0.0)
    return dict(sq=sq, sg=sg, gate=gate, kk=kk, eq=eq, ek=ek, e=e, eh=eh, qt=qt, kt=kt, qe=qe, khat=khat, a=a,
                e_last=jnp.exp(b_last), tril=rows >= cols, rowid=rowid)


def _hgrn_fwd(z4, hg_lb, gnorm):
    T = z4.shape[1]
    tb = min(ROW_BLOCK, T)
    C = HG_CHUNK
    ncb = tb // C
    HPB = HG_HEADS_PER_STEP

    def body(q_ref, f_ref, i_ref, g_ref, lb_ref, gn_ref, y_ref, o_ref, st_ref, st_sc):
        @pl.when(pl.program_id(1) == 0)
        def _():
            st_sc[...] = jnp.zeros_like(st_sc)

        def chunk(c, carry):
            rs = pl.ds(pl.multiple_of(c * C, C), C)
            for hh in range(HPB):
                hs = slice(hh * 128, (hh + 1) * 128)
                lb = _hg_lower_bound(lb_ref.at[:, hs])
                v_b = i_ref[rs, hs].astype(BF16)
                gc = g_ref[rs, hs]
                x = _hg_chunk(q_ref[rs, hs], f_ref[rs, hs], lb)
                st = st_sc[hh]
                st_ref[hh, c] = st
                o = (jnp.dot(x["a"].astype(BF16), v_b, preferred_element_type=F32)
                     + lax.dot_general(x["qe"].astype(BF16), st.astype(BF16), NT_DIMS, preferred_element_type=F32))
                st_sc[hh] = st * x["e_last"] + lax.dot_general(v_b, x["khat"].astype(BF16), TN_DIMS,
                                                               preferred_element_type=F32)
                o_ref[rs, hs] = o
                n = o * lax.rsqrt(jnp.mean(o * o, -1, keepdims=True) + EPS)
                y_ref[rs, hs] = (n * gn_ref[:, hs] * (gc * _sigmoid(gc))).astype(BF16)
            return carry

        lax.fori_loop(0, ncb, chunk, 0)

    W = 128 * HPB
    zb = lambda k: pl.BlockSpec((None, tb, W), lambda h, t: (k, t, h))
    out = pl.BlockSpec((tb, W), lambda h, t: (t, h))
    return pl.pallas_call(
        body, name="hgrn_fwd", grid=(HEADS // HPB, T // tb),
        in_specs=[zb(0), zb(1), zb(2), zb(3), pl.BlockSpec((2, W), lambda h, t: (0, h)),
                  pl.BlockSpec((1, W), lambda h, t: (0, h))],
        out_specs=[out, out, pl.BlockSpec((HPB, ncb, 128, 128), lambda h, t: (h, t, 0, 0))],
        out_shape=[jax.ShapeDtypeStruct((T, D), BF16), jax.ShapeDtypeStruct((T, D), F32),
                   jax.ShapeDtypeStruct((HEADS, T // C, 128, 128), F32)],
        scratch_shapes=[pltpu.VMEM((HPB, 128, 128), F32)],
        compiler_params=_params(48, 2),
    )(*_hbm(z4, z4, z4, z4, hg_lb, gnorm))


def _hgrn_bwd(z4, o_raw, dy, states, hg_lb, gnorm):
    T = z4.shape[1]
    tb = min(ROW_BLOCK, T)
    C = HG_CHUNK
    ncb = tb // C
    nt = T // tb
    HPB = HG_HEADS_PER_STEP

    def body(q_ref, f_ref, i_ref, g_ref, o_ref, dy_ref, st_ref, lb_ref, gn_ref, dz_ref, dlb_ref, dgn_ref, dst_sc):
        @pl.when(pl.program_id(1) == 0)
        def _():
            dst_sc[...] = jnp.zeros_like(dst_sc)
            dlb_ref[...] = jnp.zeros_like(dlb_ref)
            dgn_ref[...] = jnp.zeros_like(dgn_ref)

        def chunk(cc, carry):
            for hh in range(HPB):
                one_head(ncb - 1 - cc, hh, slice(hh * 128, (hh + 1) * 128))
            return carry

        def one_head(c, hh, hs):
            rs = pl.ds(pl.multiple_of(c * C, C), C)
            lb = _hg_lower_bound(lb_ref.at[:, hs])
            gn = gn_ref[:, hs]
            qc, gc = q_ref[rs, hs], g_ref[rs, hs]
            v_b = i_ref[rs, hs].astype(BF16)
            x = _hg_chunk(qc, f_ref[rs, hs], lb)
            st, dst = st_ref[hh, c], dst_sc[hh]
            st_b, dst_b = st.astype(BF16), dst.astype(BF16)
            o, dyc = o_ref[rs, hs], dy_ref[rs, hs]
            sgg = _sigmoid(gc)
            sil = gc * sgg
            rstd = lax.rsqrt(jnp.mean(o * o, -1, keepdims=True) + EPS)
            n = o * rstd
            dgn_ref[:, hs] += _fold8(dyc * n * sil)
            dn = dyc * gn * sil
            do = rstd * (dn - n * jnp.mean(dn * n, -1, keepdims=True))
            dg = dyc * n * gn * (sgg * (1.0 + gc * (1.0 - sgg)))
            do_b = do.astype(BF16)
            da = jnp.where(x["tril"], lax.dot_general(do_b, v_b, NT_DIMS, preferred_element_type=F32), 0.0).astype(BF16)
            qt_b, kt_b, qe_b, khat_b = (x[n_].astype(BF16) for n_ in ("qt", "kt", "qe", "khat"))
            dv = (lax.dot_general(x["a"].astype(BF16), do_b, TN_DIMS, preferred_element_type=F32)
                  + lax.dot_general(khat_b, dst_b, NT_DIMS, preferred_element_type=F32))
            dqt = jnp.dot(da, kt_b, preferred_element_type=F32)
            dqe = jnp.dot(do_b, st_b, preferred_element_type=F32)
            dkt = lax.dot_general(da, qt_b, TN_DIMS, preferred_element_type=F32)
            dkhat = jnp.dot(v_b, dst_b, preferred_element_type=F32)
            dst_sc[hh] = lax.dot_general(do_b, qe_b, TN_DIMS, preferred_element_type=F32) + dst * x["e_last"]
            de_last = jnp.sum(st * dst, axis=0, keepdims=True)
            dqf = dqt * x["eq"] + dqe * x["e"]
            dkk = dkt * x["ek"] + dkhat * x["eh"]
            dkh_kh = dkhat * x["khat"]
            db = dqt * qt_b.astype(F32) - dkt * kt_b.astype(F32) + dqe * x["qe"] - dkh_kh
            db_last = jnp.sum(dkh_kh, axis=0, keepdims=True) + de_last * x["e_last"]
            db = db + jnp.where(x["rowid"] == C - 1, db_last, 0.0)
            dlg = _running_sum(db, reverse=True)
            dgate = dlg / x["gate"] - dkk
            sg, sq = x["sg"], x["sq"]
            dlb_ref[:, hs] += _fold8(dgate * (1.0 - sg)) * (lb * (1.0 - lb))
            dz_ref[0, rs, hs] = (dqf * (sq * (1.0 + qc * (1.0 - sq)))).astype(BF16)
            dz_ref[1, rs, hs] = (dgate * (1.0 - lb) * sg * (1.0 - sg)).astype(BF16)
            dz_ref[2, rs, hs] = dv.astype(BF16)
            dz_ref[3, rs, hs] = dg.astype(BF16)

        lax.fori_loop(0, ncb, chunk, 0)

    W = 128 * HPB
    zb = lambda k: pl.BlockSpec((None, tb, W), lambda h, t: (k, nt - 1 - t, h))
    blk = pl.BlockSpec((tb, W), lambda h, t: (nt - 1 - t, h))
    acc = pl.BlockSpec((8, W), lambda h, t: (0, h))
    return pl.pallas_call(
        body, name="hgrn_bwd", grid=(HEADS // HPB, nt),
        in_specs=[zb(0), zb(1), zb(2), zb(3), blk, blk,
                  pl.BlockSpec((HPB, ncb, 128, 128), lambda h, t: (h, nt - 1 - t, 0, 0)),
                  pl.BlockSpec((2, W), lambda h, t: (0, h)), pl.BlockSpec((1, W), lambda h, t: (0, h))],
        out_specs=[pl.BlockSpec((4, tb, W), lambda h, t: (0, nt - 1 - t, h)), acc, acc],
        out_shape=[jax.ShapeDtypeStruct((4, T, D), BF16), jax.ShapeDtypeStruct((8, D), F32),
                   jax.ShapeDtypeStruct((8, D), F32)],
        scratch_shapes=[pltpu.VMEM((HPB, 128, 128), F32)],
        compiler_params=_params(48, 2),
    )(*_hbm(z4, z4, z4, z4, o_raw, dy, states, hg_lb, gnorm))


def _adamw(w, g, m, v, *, name):
    R, L = w.shape
    tr = R if R <= 512 else 512
    assert R % tr == 0
    blk = pl.BlockSpec((tr, L), lambda i: (i, 0))
    c1, c2 = 1.0 - B1 ** STEP, 1.0 - B2 ** STEP

    def body(w_ref, g_ref, m_ref, v_ref, d_ref, mo_ref, vo_ref):
        g_ = g_ref[...]
        m_ = B1 * m_ref[...] + (1.0 - B1) * g_
        v_ = B2 * v_ref[...] + (1.0 - B2) * (g_ * g_)
        d_ref[...] = -LR * ((m_ / c1) / (jnp.sqrt(v_ / c2) + ADAM_EPS) + WD * w_ref[...])
        mo_ref[...] = m_
        vo_ref[...] = v_

    sds = jax.ShapeDtypeStruct((R, L), F32)
    return pl.pallas_call(
        body, name=name, grid=(R // tr,), in_specs=[blk] * 4, out_specs=[blk] * 3, out_shape=[sds] * 3,
        compiler_params=_params(32, 1),
    )(w, g, m, v)


def _adamw_rows(w, m, v, gbufs, row0, *, name, plan=None):
    L, R, C = w.shape
    tr = 256
    assert R % tr == 0 and row0 % tr == 0 and len(gbufs) == L
    grid = (L, R // tr)
    blk = pl.BlockSpec((None, tr, C), lambda l, i: (l, i, 0))
    gblk = pl.BlockSpec((tr, C), lambda l, i: (row0 // tr + i, 0))
    c1, c2 = 1.0 - B1 ** STEP, 1.0 - B2 ** STEP

    def body(*refs):
        ins, (go_ref, d_ref, mo_ref, vo_ref), _, pctx = _split_refs(refs, 3 + L, 4, 0, plan)
        w_ref, m_ref, v_ref = ins[:3]
        g_refs = ins[3:]
        _plan_start(plan, pctx, grid)
        g_ = g_refs[0][...]
        for l in range(1, L):
            g_ = jnp.where(pl.program_id(0) == l, g_refs[l][...], g_)
        m_ = B1 * m_ref[...] + (1.0 - B1) * g_
        v_ = B2 * v_ref[...] + (1.0 - B2) * (g_ * g_)
        go_ref[...] = g_
        d_ref[...] = -LR * ((m_ / c1) / (jnp.sqrt(v_ / c2) + ADAM_EPS) + WD * w_ref[...])
        mo_ref[...] = m_
        vo_ref[...] = v_
        _plan_wait(plan, pctx, grid)

    sds = jax.ShapeDtypeStruct((L, R, C), F32)
    p_in, p_ospec, p_oshape, p_scr, p_alias = _plan_io(plan, 3 + L, 4)
    return pl.pallas_call(
        body, name=name, grid=grid, in_specs=[blk] * 3 + [gblk] * L + [_ANY] * len(p_in),
        out_specs=[blk] * 4 + p_ospec, out_shape=[sds] * 4 + p_oshape, scratch_shapes=p_scr,
        input_output_aliases=p_alias, compiler_params=_params(32, 2),
    )(*_hbm(w, m, v, *gbufs), *p_in)


def _add_pairs(g, theirs, ids, *, name):
    n, R, L = theirs.shape
    tr = 128
    nb = R // tr

    def body(ids_ref, a_ref, b_ref, o_ref):
        o_ref[...] = (a_ref[...].astype(F32) + b_ref[...].astype(F32)).astype(BF16)

    blk = pl.BlockSpec((n, tr, L), lambda i, ids: (0, i, 0))
    return pl.pallas_call(
        body, name=name, out_shape=jax.ShapeDtypeStruct((n, R, L), BF16),
        grid_spec=pltpu.PrefetchScalarGridSpec(
            num_scalar_prefetch=1, grid=(nb,),
            in_specs=[pl.BlockSpec((n, tr, L), lambda i, ids: (0, ids[1] * nb + i, 0)), blk], out_specs=blk),
        compiler_params=_params(16, 1),
    )(ids, g, theirs)


def _sum_chips(pair, parts, ids, *, name):
    _, R, L = parts.shape
    tr = 128

    def body(ids_ref, o_ref, r_ref, out_ref):
        out_ref[...] = ((o_ref[...].astype(F32) + r_ref[0].astype(F32)) + r_ref[1].astype(F32)) + r_ref[2].astype(F32)

    return pl.pallas_call(
        body, name=name, out_shape=jax.ShapeDtypeStruct((2, R, L), F32),
        grid_spec=pltpu.PrefetchScalarGridSpec(
            num_scalar_prefetch=1, grid=(R // tr,),
            in_specs=[pl.BlockSpec((None, tr, L), lambda i, ids: (ids[0], i, 0)),
                      pl.BlockSpec((3, tr, L), lambda i, ids: (0, i, 0))],
            out_specs=pl.BlockSpec((None, tr, L), lambda i, ids: (ids[1], i, 0))),
        compiler_params=_params(32, 1),
    )(ids, pair, parts)


def _mesh_ids():
    x, y, c = _mesh_pos()
    return jnp.stack([2 * x + y, c]).astype(jnp.int32)


def _place_shard(rows, ids, *, name):
    R, L = rows.shape
    tr = 256

    def body(ids_ref, in_ref, out_ref):
        out_ref[...] = in_ref[...].astype(BF16)

    return pl.pallas_call(
        body, name=name, out_shape=jax.ShapeDtypeStruct((4, R, L), BF16),
        grid_spec=pltpu.PrefetchScalarGridSpec(
            num_scalar_prefetch=1, grid=(R // tr,), in_specs=[pl.BlockSpec((tr, L), lambda i, ids: (i, 0))],
            out_specs=pl.BlockSpec((None, tr, L), lambda i, ids: (ids[0], i, 0))),
        compiler_params=_params(16, 1),
    )(ids, rows)


def _remote(src, dst, send_sem, recv_sem, to):
    return pltpu.make_async_remote_copy(src_ref=src, dst_ref=dst, send_sem=send_sem, recv_sem=recv_sem,
                                        device_id=to, device_id_type=MESH_IDS)


def _rows(ref, lead, start, size):
    return ref.at[tuple(pl.ds(0, n) for n in ref.shape[:lead]) + (pl.ds(start, size),)]


def _other_chips():
    x, y, _ = _mesh_pos()
    return [(1 - x, y), (x, 1 - y), (1 - x, 1 - y)]


def _plan_gather_ici(bufs):
    n = len(bufs)

    def copies(outs, send, recv):
        x, y, c = _mesh_pos()
        res = []
        for b in range(n):
            half = bufs[b].shape[1] // 2
            mine = _rows(outs[b].at[2 * x + y], 0, c * half, half)
            for j, (cx, cy) in enumerate(_other_chips()):
                res.append((_remote(mine, mine, send(3 * b + j), recv(3 * b + j), (cx, cy, c)),
                            _remote(mine, _rows(outs[b].at[2 * cx + cy], 0, c * half, half),
                                    send(3 * b + j), recv(3 * b + j), (x, y, c))))
        return res

    def start(ins, outs, send, recv, loc):
        for out_cp, _ in copies(outs, send, recv):
            out_cp.start()

    def wait(ins, outs, send, recv, loc):
        for out_cp, in_cp in copies(outs, send, recv):
            in_cp.wait_recv()
            out_cp.wait_send()

    outs = [jax.ShapeDtypeStruct(b.shape, b.dtype) for b in bufs]
    return _Plan(bufs, outs, 3 * n, 0, start, wait, aliases={b: b for b in range(n)})


def _plan_gather_forward(bufs):
    n = len(bufs)

    def copies(outs, send, recv):
        x, y, c = _mesh_pos()
        res = []
        for b in range(n):
            half = bufs[b].shape[1] // 2
            for j, (cx, cy) in enumerate(_other_chips()):
                slot = outs[b].at[2 * cx + cy]
                res.append((_remote(_rows(slot, 0, c * half, half), _rows(slot, 0, c * half, half),
                                    send(3 * b + j), recv(3 * b + j), (x, y, 1 - c)),
                            _remote(_rows(slot, 0, c * half, half), _rows(slot, 0, (1 - c) * half, half),
                                    send(3 * b + j), recv(3 * b + j), (x, y, c))))
        return res

    def start(ins, outs, send, recv, loc):
        for out_cp, _ in copies(outs, send, recv):
            out_cp.start()

    def wait(ins, outs, send, recv, loc):
        for out_cp, in_cp in copies(outs, send, recv):
            in_cp.wait_recv()
            out_cp.wait_send()

    outs = [jax.ShapeDtypeStruct(b.shape, b.dtype) for b in bufs]
    return _Plan(bufs, outs, 3 * n, 0, start, wait, aliases={b: b for b in range(n)})


def _plan_pair_swap(g):
    half = g.shape[1] // 2

    def copy(ins, outs, send, recv, loc):
        x, y, c = _mesh_pos()
        return _remote(_rows(ins[0], 1, (1 - c) * half, half), outs[0], send(0), recv(0), (x, y, 1 - c))

    return _Plan([g], [jax.ShapeDtypeStruct((4, half, g.shape[2]), g.dtype)], 1, 0,
                 lambda *a: copy(*a).start(), lambda *a: copy(*a).wait())


def _plan_pair_gather(buf):
    def copies(ins, outs, send, recv, loc):
        x, y, c = _mesh_pos()
        return (_remote(outs[0].at[c], outs[0].at[c], send(0), recv(0), (x, y, 1 - c)),
                _remote(outs[0].at[c], outs[0].at[1 - c], send(0), recv(0), (x, y, c)))

    def wait(*a):
        out_cp, in_cp = copies(*a)
        in_cp.wait_recv()
        out_cp.wait_send()

    return _Plan([buf], [jax.ShapeDtypeStruct(buf.shape, buf.dtype)], 1, 0, lambda *a: copies(*a)[0].start(), wait,
                 aliases={0: 0})


def _plan_chip_scatter(p):
    def copies(ins, outs, send, recv, loc):
        _, _, c = _mesh_pos()
        return [_remote(ins[0].at[2 * cx + cy], outs[0].at[j], send(j), recv(j), (cx, cy, c))
                for j, (cx, cy) in enumerate(_other_chips())]

    def start(*a):
        for cp in copies(*a):
            cp.start()

    def wait(*a):
        for cp in copies(*a):
            cp.wait()

    return _Plan([p], [jax.ShapeDtypeStruct((3,) + p.shape[1:], p.dtype)], 3, 0, start, wait)


def _plan_exchange_all(vec):
    def copies(ins, outs, send, recv, loc):
        x, y, c = _mesh_pos()
        return [_remote(ins[0], outs[0].at[r - 1], send(r - 1), recv(r - 1), (x ^ (r >> 2), y ^ ((r >> 1) & 1), c ^ (r & 1)))
                for r in range(1, 8)]

    def start(*a):
        for cp in copies(*a):
            cp.start()

    def wait(*a):
        for cp in copies(*a):
            cp.wait()

    return _Plan([vec], [jax.ShapeDtypeStruct((7,) + vec.shape, vec.dtype)], 7, 0, start, wait)


SMALL_LAYOUT = {
    "mla_gq": (0, 1, 256, (1, 256)), "mla_gkv": (1, 1, 256, (1, 256)), "sgu_ln_g": (2, 1, 512, (1, 512)),
    "sgu_ln_b": (3, 1, 512, (1, 512)), "sgu_w": (4, 64, 1024, (64, 1024)), "sgu_b": (68, 1, 512, (1, 512)),
    "hg_lb": (69, 2, 1024, (2, 1024)), "hg_gnorm": (71, 1, 1024, (1, 256)), "ln1_g": (72, 2, 1024, (2, 1024)),
    "ln1_b": (74, 2, 1024, (2, 1024)), "ln2_g": (76, 2, 1024, (2, 1024)), "ln2_b": (78, 2, 1024, (2, 1024)),
}


def _small_pack(dgq, dgkv, dslg, dslb, dsw, dsb, dlb, dgn, ln_parts, sq_err):
    flat_ln = [p for pair in ln_parts for p in pair]

    def body(*refs):
        gq_ref, gkv_ref, slg_ref, slb_ref, sw_ref, sb_ref, lb_ref, gn_ref = refs[:8]
        ln_refs, err_ref, out_ref, t_sc = refs[8:16], refs[16], refs[17], refs[18]
        s8 = lambda ref: jnp.sum(ref[...], axis=0, keepdims=True)
        out_ref[...] = jnp.zeros_like(out_ref)
        out_ref[0:1, 0:256] = s8(gq_ref)
        out_ref[1:2, 0:256] = s8(gkv_ref)
        out_ref[2:3, 0:512] = s8(slg_ref)
        out_ref[3:4, 0:512] = s8(slb_ref)
        out_ref[4:68, :] = sw_ref[...]
        t_sc[...] = sb_ref[...].T
        for g in range(SGU_G):
            out_ref[68:69, g * SGU_C:(g + 1) * SGU_C] = t_sc[g:g + 1, :]
        d_lb1 = s8(lb_ref)
        out_ref[69:70, :] = -d_lb1
        out_ref[70:71, :] = d_lb1
        out_ref[71:72, :] = s8(gn_ref)
        for k, ref in enumerate(ln_refs):
            out_ref[72 + k:73 + k, :] = s8(ref)
        out_ref[0:1, 1023:1024] = jnp.sum(s8(err_ref), axis=1, keepdims=True) * (0.5 / D)

    vm = pl.BlockSpec(memory_space=pltpu.VMEM)
    return pl.pallas_call(
        body, name="small_grad_pack", in_specs=[vm] * 17, out_specs=vm,
        out_shape=jax.ShapeDtypeStruct((SMALL_ROWS, 1024), F32), scratch_shapes=[pltpu.VMEM((SGU_C, SGU_C), F32)],
        compiler_params=_params(16),
    )(dgq, dgkv, dslg, dslb, dsw.reshape(64, 1024), dsb, dlb, dgn, *flat_ln, sq_err)


def _small_update(vec, others, ids, w, m, v):
    names = list(SMALL_LAYOUT)
    n = len(names)
    c1, c2 = 1.0 - B1 ** STEP, 1.0 - B2 ** STEP
    have_others = others is not None

    def body(*refs):
        ids_ref, v_ref = refs[0], refs[1]
        k = 2 + have_others
        w_refs, m_refs, v_refs = refs[k:k + n], refs[k + n:k + 2 * n], refs[k + 2 * n:k + 3 * n]
        outs = refs[k + 3 * n:]
        row0_ref, tot_sc = outs[0], outs[-1]
        total = v_ref[...]
        if have_others:
            me = 2 * ids_ref[0] + ids_ref[1]
            total = None
            for d in range(8):
                rel = d ^ me
                term = jnp.where(rel == 0, v_ref[...], refs[2][jnp.maximum(rel - 1, 0)])
                total = term if total is None else total + term
        tot_sc[...] = total
        row0_ref[...] = tot_sc[0:1, :]
        for i, name in enumerate(names):
            r0, nr, width, _ = SMALL_LAYOUT[name]
            if name == "hg_gnorm":
                g_ = tot_sc[r0:r0 + 1, 0:256]
                for chip in range(1, 4):
                    g_ = jnp.where(ids_ref[0] == chip, tot_sc[r0:r0 + 1, chip * 256:(chip + 1) * 256], g_)
            else:
                g_ = tot_sc[r0:r0 + nr, 0:width]
            m_ = B1 * m_refs[i][...] + (1.0 - B1) * g_
            v_ = B2 * v_refs[i][...] + (1.0 - B2) * (g_ * g_)
            go, do, mo, vo = outs[1 + 4 * i:5 + 4 * i]
            go[...] = g_
            do[...] = -LR * ((m_ / c1) / (jnp.sqrt(v_ / c2) + ADAM_EPS) + WD * w_refs[i][...])
            mo[...] = m_
            vo[...] = v_

    full = lambda shape: pl.BlockSpec(shape, lambda i, ids, nd=len(shape): (0,) * nd)
    kshapes = [SMALL_LAYOUT[name][3] for name in names]
    operands = [vec] + ([others] if have_others else []) + [d[name] for d in (w, m, v) for name in names]
    out_shapes = [jax.ShapeDtypeStruct((1, 1024), F32)] + [jax.ShapeDtypeStruct(s, F32) for s in kshapes for _ in range(4)]
    res = pl.pallas_call(
        body, name="small_update", out_shape=out_shapes,
        grid_spec=pltpu.PrefetchScalarGridSpec(
            num_scalar_prefetch=1, grid=(1,), in_specs=[full(o.shape) for o in operands],
            out_specs=[full(s.shape) for s in out_shapes],
            scratch_shapes=[pltpu.VMEM((SMALL_ROWS, 1024), F32)]),
        compiler_params=_params(32, 1),
    )(ids, *operands)
    return res[0], {name: tuple(res[1 + 4 * i:5 + 4 * i]) for i, name in enumerate(names)}


ROWS_L1, ROWS_L0, ROWS_ODD = 3328, 2048, 768
ODD_PARTS = (("w_out_e", (256, 1024)), ("w_in_e", (1024, 392)), ("w_qb", (256, 192)), ("w_kvb", (256, 256)))


def _odd_rows(parts, dtype, gnorm=None):
    rows = [parts[n].reshape(-1, 1024).astype(dtype) for n, _ in ODD_PARTS]
    used = sum(r.shape[0] for r in rows)
    if gnorm is not None:
        bits = lax.bitcast_convert_type(gnorm.reshape(-1), BF16).reshape(1, 512)
        rows.append(jnp.pad(bits, ((0, 0), (0, 512))))
        used += 1
    rows.append(jnp.zeros((ROWS_ODD - used, 1024), dtype))
    return jnp.concatenate(rows, axis=0)


def _odd_unrows(buf, with_gnorm=False):
    out, off = {}, 0
    for n, shape in ODD_PARTS:
        nr = math.prod(shape) // 1024
        out[n] = buf[off:off + nr].reshape(shape)
        off += nr
    if with_gnorm:
        out["hg_gnorm"] = lax.bitcast_convert_type(buf[off, :512].reshape(256, 2), F32).reshape(1, 256)
    return out


def _rope_tables(positions):
    half = ROPE // 2
    inv_freq = ROPE_BASE ** (-jnp.arange(half, dtype=F32) / half)
    ang = positions.astype(F32).reshape(-1, 1) * inv_freq
    cos, sin = jnp.cos(ang), jnp.sin(ang)
    T = ang.shape[0]
    one, z16, z32 = jnp.ones((T, NOPE), F32), jnp.zeros((T, half), F32), jnp.zeros((T, 32), F32)
    z64 = jnp.zeros((T, NOPE), F32)
    c = jnp.concatenate([one, cos, cos, z32], axis=1)
    s1 = jnp.concatenate([z64, -sin, z16, z32], axis=1)
    s2 = jnp.concatenate([z64, z16, sin, z32], axis=1)
    return c, s1, s2


def _local_step(x, positions, tgt, odd, bufs, P, exchange):
    T = x.shape[0]
    row = lambda a: a.reshape(1, -1)
    rc, rs1, rs2 = _rope_tables(positions)
    blk = lambda f: pl.BlockSpec((None, D, D), f)

    w_in_e = odd["w_in_e"]
    w_in = jnp.concatenate([w_in_e[:, :512], w_in_e[:, 544:1568], w_in_e[:, 512:544], jnp.zeros((D, 96), BF16)], axis=1)
    wq = jnp.pad(odd["w_qb"].reshape(256, HEADS, NOPE + ROPE), ((0, 0), (0, 0), (0, 32))).reshape(256, HEADS * 128)
    kvb = odd["w_kvb"].reshape(256, HEADS, NOPE + VDIM)
    wk = jnp.pad(kvb[:, :, :NOPE], ((0, 0), (0, 0), (0, 64))).reshape(256, HEADS * 128)
    wv = kvb[:, :, NOPE:].reshape(256, HEADS * VDIM)
    w_out_e = odd["w_out_e"]
    sgu_w = P["sgu_w"][0]
    sgu_bt = P["sgu_b"][0].T
    gq, gkv = P["mla_gq"], P["mla_gkv"]
    gnorm = P["hg_gnorm"]

    z0 = _matmul(x, w_in, name="in_proj_e", M=T, N=1664, K=D, tn=1664)[0]
    q, k, v = _mla_prep(z0, gq, gkv, wq, wk, wv, rc, rs1, rs2)
    if exchange:
        ids = _mesh_ids()
        placed = [_place_shard(b, ids, name=f"place_shard_{l}") for l, b in enumerate(bufs)]
        a_out, lse, wga, wgb = _flash_fwd(q, k, v, plan=_plan_gather_ici(placed[:2]))
    else:
        a_out, lse = _flash_fwd(q, k, v)
        wga, wgb, wgc = bufs
    mix0 = _sgu_fwd(z0, a_out, P["sgu_ln_g"], P["sgu_ln_b"], sgu_w, sgu_bt)
    res = _proj_ln(mix0, w_out_e, x, row(P["ln1_g"][0]), row(P["ln1_b"][0]), name="out_proj_ln_e",
                   plan=_plan_gather_forward([wga, wgb]) if exchange else None)
    r1, h1, h1b = res[:3]
    if exchange:
        wga, wgb = res[3:]
    res = _ffn_ln(h1b, wga, h1, row(P["ln2_g"][0]), row(P["ln2_b"][0]), name="ffn_ln_0",
                  plan=_plan_gather_ici(placed[2:]) if exchange else None)
    ra0, r2, h2, h2b = res[:4]
    z4 = _matmul(h2b, wgb, name="in_proj_o", M=T, N=4 * D, K=D, b_spec=blk(lambda i, j, k: (j, 0, 0)),
                 out_shape=jax.ShapeDtypeStruct((4, T, D), F32),
                 o_spec=pl.BlockSpec((None, min(MM_ROWS, T), D), lambda i, j, k: (j, i, 0)))[0]
    y1, o_raw, states = _hgrn_fwd(z4, P["hg_lb"], gnorm)
    res2 = _proj_ln(y1, wgb, h2, row(P["ln1_g"][1]), row(P["ln1_b"][1]), name="out_proj_ln_o", w_rowblk=4,
                    plan=_plan_gather_forward([res[4]]) if exchange else None)
    r3, h3, h3b = res2[:3]
    if exchange:
        wgc = res2[3]
    ra1, r4, h4, _ = _ffn_ln(h3b, wgc, h3, row(P["ln2_g"][1]), row(P["ln2_b"][1]), name="ffn_ln_1")

    ln1_g, ln1_b, ln2_g, ln2_b = [None, None], [None, None], [None, None], [None, None]
    sq_err_parts = []

    def ffn_bwd(l, dh, r_out, ra, h_mid_b, g2, wg, rows, plan=None, tgt=None):
        dr, dr_b, dg, db, *sq_err = _ln_bwd(dh, r_out, row(g2), name=f"ln2_bwd_{l}", tgt=tgt)
        sq_err_parts.extend(sq_err)
        ln2_g[l], ln2_b[l] = dg, db
        da, *extra = _matmul(dr_b, wg, tb=True, mul=ra, out_dtype=BF16, name=f"ffn_da_{l}", M=T, N=4 * D, K=D,
                             b_spec=blk(lambda i, j, k: (j, 1, 0)), plan=plan)
        gbuf = _matmul(ra, dr_b, ta=True, a_sq=True, name=f"ffn_dw2_{l}", M=4 * D, N=D, K=T, tm=1024, tk=DW_TOKENS,
                       out_shape=jax.ShapeDtypeStruct((4, rows, D), BF16), o_spec=blk(lambda i, j, k: (i, 1, 0)))[0]
        gbuf = _matmul(h_mid_b, da, ta=True, name=f"ffn_dw1_{l}", M=D, N=4 * D, K=T, tm=1024, tk=DW_TOKENS, into=gbuf,
                       out_shape=jax.ShapeDtypeStruct((4, rows, D), BF16), o_spec=blk(lambda i, j, k: (j, 0, 0)))[0]
        dh_mid = _matmul(da, wg, tb=True, add=dr, add_scale=ALPHA, name=f"ffn_dh_{l}", M=T, N=D, K=4 * D,
                         b_spec=blk(lambda i, j, k: (k, 0, 0)))[0]
        return dh_mid, gbuf, extra

    dh3, g1, _ = ffn_bwd(1, h4, r4, ra1, h3b, P["ln2_g"][1], wgc, ROWS_L1, tgt=tgt)
    loss_parts = sq_err_parts[0]
    dr3, dr3_b, dg, db = _ln_bwd(dh3, r3, row(P["ln1_g"][1]), name="ln1_bwd_1")
    ln1_g[1], ln1_b[1] = dg, db
    g1_sds = jax.ShapeDtypeStruct((4, ROWS_L1, D), BF16)
    g1 = _matmul(y1, dr3_b, ta=True, name="dw_out_o", M=D, N=D, K=T, tm=256, tk=DW_TOKENS, into=g1, out_shape=g1_sds,
                 o_spec=pl.BlockSpec((None, 256, D), lambda i, j, k: (i, 12, 0)))[0]
    dmix1 = _matmul(dr3_b, wgb, tb=True, name="dmix_o", M=T, N=D, K=D, b_spec=_rows4_spec(4, 3), b_merge=(D, D))[0]
    dz4, dlb, dgn = _hgrn_bwd(z4, o_raw, dmix1, states, P["hg_lb"], gnorm)
    g1 = _matmul(h2b, dz4, ta=True, name="dw_in_o", M=D, N=4 * D, K=T, tm=1024, tk=DW_TOKENS, into=g1, out_shape=g1_sds,
                 b_spec=pl.BlockSpec((None, min(DW_TOKENS, T), D), lambda i, j, k: (j, k, 0)),
                 o_spec=blk(lambda i, j, k: (j, 2, 0)))[0]
    dh2 = _matmul(dz4, wgb, tb=True, add=dr3, add_scale=ALPHA, name="dh_in_o", M=T, N=D, K=4 * D,
                  a_spec=pl.BlockSpec((None, min(MM_ROWS, T), D), lambda i, j, k: (k, i, 0)),
                  b_spec=blk(lambda i, j, k: (k, 0, 0)))[0]

    dh1, g0, swapped1 = ffn_bwd(0, dh2, r2, ra0, h1b, P["ln2_g"][0], wga, ROWS_L0,
                                plan=_plan_pair_swap(g1) if exchange else None)
    dr1, dr1_b, dg, db = _ln_bwd(dh1, r1, row(P["ln1_g"][0]), name="ln1_bwd_0")
    ln1_g[0], ln1_b[0] = dg, db
    godd = {"w_out_e": _matmul(mix0, dr1_b, ta=True, name="dw_out_e", M=D, N=D, K=T, tm=1024, tk=DW_TOKENS)[0]}
    dmix0, *swapped0 = _matmul(dr1_b, w_out_e, tb=True, name="dmix_e", M=T, N=D, K=D,
                               plan=_plan_pair_swap(g0) if exchange else None)
    delta, do_b = _attn_delta(dmix0, a_out)
    if exchange:
        pair1 = _add_pairs(g1, swapped1[0], ids, name="grad_pair_add_1")
        pair0 = _add_pairs(g0, swapped0[0], ids, name="grad_pair_add_0")
        dq4, dk, dv, parts0, parts1 = _flash_bwd(
            q, k, v, do_b, lse, delta, plan=_join_plans([_plan_chip_scatter(pair0), _plan_chip_scatter(pair1)]))
        half0 = _sum_chips(pair0, parts0, ids, name="grad_chip_sum_0")
        half1 = _sum_chips(pair1, parts1, ids, name="grad_chip_sum_1")
        dc, dkr, dwq, dwk, dwv, dgq, dgkv, g0, g1 = _mla_bwd(
            z0, dq4, dk, dv, gq, gkv, wq, wk, wv, rc, rs1, rs2,
            plan=_join_plans([_plan_pair_gather(half0), _plan_pair_gather(half1)]))
        g0, g1 = g0.reshape(ROWS_L0, D), g1.reshape(ROWS_L1, D)
    else:
        dq4, dk, dv = _flash_bwd(q, k, v, do_b, lse, delta)
        dc, dkr, dwq, dwk, dwv, dgq, dgkv = _mla_bwd(z0, dq4, dk, dv, gq, gkv, wq, wk, wv, rc, rs1, rs2)
    dz0, dsw, dsb, dslg, dslb = _sgu_bwd(z0, dmix0, dc, dkr, P["sgu_ln_g"], P["sgu_ln_b"], sgu_w, sgu_bt)
    small_vec = _small_pack(dgq, dgkv, dslg, dslb, dsw, dsb, dlb, dgn, [ln1_g, ln1_b, ln2_g, ln2_b], loss_parts)
    dw_in, *small_others = _matmul(x, dz0, ta=True, name="dw_in_e", M=D, N=1664, K=T, tm=1024, tn=1664,
                                   tk=DW_TOKENS // 2, plan=_plan_exchange_all(small_vec) if exchange else None)
    godd["w_in_e"] = jnp.concatenate([dw_in[:, :512], dw_in[:, 1536:1568], dw_in[:, 512:1536]], axis=1)
    grad_x = _matmul(dz0, w_in, tb=True, add=dr1, add_scale=ALPHA, name="dx", M=T, N=D, K=1664, tk=1664)[0]

    godd["w_qb"] = dwq.reshape(256, HEADS, 128)[:, :, :NOPE + ROPE].reshape(256, HEADS * (NOPE + ROPE))
    godd["w_kvb"] = jnp.concatenate([dwk.reshape(256, HEADS, 128)[:, :, :NOPE], dwv.reshape(256, HEADS, VDIM)],
                                    axis=2).reshape(256, HEADS * (NOPE + VDIM))
    return grad_x, g0, g1, godd, small_vec, (small_others[0] if exchange else None)


WEIGHTS = ['w_in_e', 'mla_gq', 'mla_gkv', 'w_qb', 'w_kvb', 'sgu_ln_g', 'sgu_ln_b', 'sgu_w', 'sgu_b', 'w_out_e',
           'w_in_o', 'hg_lb', 'hg_gnorm', 'w_out_o', 'ln1_g', 'ln1_b', 'w_ff1', 'w_ff2', 'ln2_g', 'ln2_b']


def kernel(x, positions, w_in_e, mla_gq, mla_gkv, w_qb, w_kvb, sgu_ln_g, sgu_ln_b, sgu_w, sgu_b, w_out_e, w_in_o, hg_lb, hg_gnorm, w_out_o, ln1_g, ln1_b, w_ff1, w_ff2, ln2_g, ln2_b, loss_target, m_w_in_e, m_mla_gq, m_mla_gkv, m_w_qb, m_w_kvb, m_sgu_ln_g, m_sgu_ln_b, m_sgu_w, m_sgu_b, m_w_out_e, m_w_in_o, m_hg_lb, m_hg_gnorm, m_w_out_o, m_ln1_g, m_ln1_b, m_w_ff1, m_w_ff2, m_ln2_g, m_ln2_b, v_w_in_e, v_mla_gq, v_mla_gkv, v_w_qb, v_w_kvb, v_sgu_ln_g, v_sgu_ln_b, v_sgu_w, v_sgu_b, v_w_out_e, v_w_in_o, v_hg_lb, v_hg_gnorm, v_w_out_o, v_ln1_g, v_ln1_b, v_w_ff1, v_w_ff2, v_ln2_g, v_ln2_b):
    args = dict(locals())
    w = {n: args[n] for n in WEIGHTS}
    m = {n: args["m_" + n] for n in WEIGHTS}
    v = {n: args["v_" + n] for n in WEIGHTS}
    cx, cy, cc = _mesh_pos()
    chip = 2 * cx + cy

    odd_shard = _odd_rows({"w_out_e": w_out_e[0], "w_in_e": w_in_e[0], "w_qb": w_qb[0], "w_kvb": w_kvb[0]}, BF16,
                          gnorm=hg_gnorm)
    ids = _mesh_ids()
    gathered = _run_plan(_plan_gather_ici([_place_shard(odd_shard, ids, name="place_shard_odd")]), name="odd_gather")[0]
    gathered = _run_plan(_plan_gather_forward([gathered]), name="odd_gather_forward")[0]
    per_chip = [_odd_unrows(gathered[j], with_gnorm=True) for j in range(4)]
    odd = {"w_out_e": jnp.concatenate([p["w_out_e"] for p in per_chip], axis=0)}
    for n in ("w_in_e", "w_qb", "w_kvb"):
        odd[n] = jnp.concatenate([p[n] for p in per_chip], axis=1)
    small = {n: w[n] for n, _ in SMALL if n != "hg_gnorm"}
    small["hg_gnorm"] = jnp.concatenate([p["hg_gnorm"] for p in per_chip], axis=1)
    shard_rows = (jnp.concatenate([w_ff1[0], w_ff2[0]], axis=0).astype(BF16),
                  jnp.concatenate([w_in_o[0], w_out_o[0]], axis=0).astype(BF16),
                  jnp.concatenate([w_ff1[1], w_ff2[1]], axis=0).astype(BF16))

    grad_x, g_l0, g_l1, godd, small_vec, small_others = _local_step(
        x[0], positions[0], loss_target[0], odd, shard_rows, small, True)

    by_chip = [_odd_rows({"w_out_e": jnp.split(godd["w_out_e"], 4, axis=0)[j],
                          **{n: jnp.split(godd[n], 4, axis=1)[j] for n in ("w_in_e", "w_qb", "w_kvb")}}, BF16)
               for j in range(4)]
    godd_buf = jnp.stack(by_chip)
    theirs = _run_plan(_plan_pair_swap(godd_buf), name="odd_pair_swap")[0]
    pair = _add_pairs(godd_buf, theirs, ids, name="odd_pair_add")
    parts = _run_plan(_plan_chip_scatter(pair), name="odd_chip_scatter")[0]
    g_odd = _run_plan(_plan_pair_gather(_sum_chips(pair, parts, ids, name="odd_chip_sum")), name="odd_pair_gather")[0]
    g_odd = _odd_unrows(g_odd.reshape(ROWS_ODD, 1024))

    to_kernel = lambda d: {n: d[n].reshape(SMALL_LAYOUT[n][3]) for n in SMALL_LAYOUT}
    first_row, small_out = _small_update(small_vec, small_others, ids, to_kernel(w), to_kernel(m), to_kernel(v))
    loss = first_row[0, 1023]
    grads, delta, new_m, new_v = {}, {}, {}, {}
    for n, res in small_out.items():
        grads[n], delta[n], new_m[n], new_v[n] = (r.reshape(w[n].shape) for r in res)

    for n, bufs_, row0 in (("w_ff1", [g_l0, g_l1], 0), ("w_ff2", [g_l0, g_l1], 1024), ("w_in_o", [g_l1], 2048),
                           ("w_out_o", [g_l1], 3072)):
        grads[n], delta[n], new_m[n], new_v[n] = _adamw_rows(w[n], m[n], v[n], bufs_, row0, name=f"adamw_{n}")
    for n, _ in ODD_PARTS:
        grads[n] = g_odd[n][None]
        d_, m_, v_ = _adamw(w[n][0], g_odd[n], m[n][0], v[n][0], name=f"adamw_{n}")
        delta[n], new_m[n], new_v[n] = d_[None], m_[None], v_[None]

    return (loss, grad_x[None], *[grads[n] for n in WEIGHTS], *[delta[n] for n in WEIGHTS],
            *[new_m[n] for n in WEIGHTS], *[new_v[n] for n in WEIGHTS])
```

```python
import math

import jax
import jax.numpy as jnp
from jax import lax
from jax.experimental import pallas as pl
from jax.experimental.pallas import tpu as pltpu

F32 = jnp.float32
BF16 = jnp.bfloat16
MESH_IDS = pl.DeviceIdType.MESH

D = 1024
DEPTH = 2
HEADS = 8
NOPE, ROPE, VDIM = 64, 32, 64
QK_SCALE = (NOPE + ROPE) ** -0.5
ROPE_BASE = 10000.0
SGU_G, SGU_C = 4, 128
HG_CHUNK = 64
HG_HEADS_PER_STEP = 8
ALPHA = (2 * DEPTH) ** 0.25
EPS = 1e-5
LR, B1, B2, ADAM_EPS, WD, STEP = 0.001, 0.9, 0.999, 1e-08, 0.01, 10
GELU_C = math.sqrt(2.0 / math.pi)
GELU_A = 0.044715
MB = 1024 * 1024
ROW_BLOCK = 512
SMALL_ROWS = 80

NT_DIMS = (((1,), (1,)), ((), ()))
TN_DIMS = (((0,), (0,)), ((), ()))


def _params(vmem_mb, n_axes=0):
    kw = dict(vmem_limit_bytes=vmem_mb * MB)
    if n_axes:
        kw["dimension_semantics"] = ("arbitrary",) * n_axes
    return pltpu.CompilerParams(**kw)


_ANY = pl.BlockSpec(memory_space=pltpu.HBM)


def _mesh_pos():
    return lax.axis_index("x"), lax.axis_index("y"), lax.axis_index("c")


def _hbm(*arrays):
    return tuple(pltpu.with_memory_space_constraint(a, pltpu.HBM) if a.size >= 2 ** 18 else a for a in arrays)


class _Plan:
    def __init__(self, ins, outs, n_remote, n_local, start, wait, aliases=None):
        self.ins, self.outs, self.n_remote, self.n_local = list(ins), list(outs), n_remote, n_local
        self.start, self.wait, self.aliases = start, wait, dict(aliases or {})


def _join_plans(plans):
    ins, outs, aliases, parts = [], [], {}, []
    nr = nl = 0
    for p in plans:
        parts.append((p, len(ins), len(outs), nr, nl))
        aliases.update({len(ins) + i: len(outs) + o for i, o in p.aliases.items()})
        ins += p.ins
        outs += p.outs
        nr += p.n_remote
        nl += p.n_local

    def run(which):
        def go(in_refs, out_refs, send, recv, loc):
            for p, i0, o0, r0, l0 in parts:
                getattr(p, which)(in_refs[i0:i0 + len(p.ins)], out_refs[o0:o0 + len(p.outs)],
                                  lambda i, r0=r0: send(r0 + i), lambda i, r0=r0: recv(r0 + i),
                                  lambda i, l0=l0: loc(l0 + i))
        return go

    return _Plan(ins, outs, nr, nl, run("start"), run("wait"), aliases)


def _plan_io(plan, n_in, n_out):
    if plan is None:
        return [], [], [], [], {}
    sems = [pltpu.SemaphoreType.DMA((max(plan.n_remote, 1),)), pltpu.SemaphoreType.DMA((max(plan.n_remote, 1),)),
            pltpu.SemaphoreType.DMA((max(plan.n_local, 1),))]
    aliases = {n_in + i: n_out + o for i, o in plan.aliases.items()}
    return plan.ins, [_ANY] * len(plan.outs), plan.outs, sems, aliases


def _split_refs(refs, n_in, n_out, n_scr, plan):
    p_in, p_out = (len(plan.ins), len(plan.outs)) if plan is not None else (0, 0)
    refs = list(refs)
    ins, refs = refs[:n_in], refs[n_in:]
    pins, refs = refs[:p_in], refs[p_in:]
    outs, refs = refs[:n_out], refs[n_out:]
    pouts, refs = refs[:p_out], refs[p_out:]
    scr, psem = refs[:n_scr], refs[n_scr:]
    psem = tuple((lambda i, s=s: s.at[i]) for s in psem)
    return ins, outs, scr, (pins, pouts, psem)


def _grid_edge(grid, last):
    cond = None
    for ax, n in enumerate(grid):
        c = pl.program_id(ax) == (n - 1 if last else 0)
        cond = c if cond is None else cond & c
    return cond


def _plan_start(plan, pctx, grid):
    if plan is not None:
        pins, pouts, psem = pctx
        pl.when(_grid_edge(grid, False))(lambda: plan.start(pins, pouts, *psem))


def _plan_wait(plan, pctx, grid):
    if plan is not None:
        pins, pouts, psem = pctx
        pl.when(_grid_edge(grid, True))(lambda: plan.wait(pins, pouts, *psem))


def _run_plan(plan, *, name):
    def body(*refs):
        _, _, _, (pins, pouts, psem) = _split_refs(refs, 0, 0, 0, plan)
        plan.start(pins, pouts, *psem)
        plan.wait(pins, pouts, *psem)

    p_in, p_ospec, p_oshape, p_scr, p_alias = _plan_io(plan, 0, 0)
    return pl.pallas_call(body, name=name, in_specs=[_ANY] * len(p_in), out_specs=p_ospec, out_shape=p_oshape,
                          scratch_shapes=p_scr, input_output_aliases=p_alias)(*p_in)


def _fold8(x):
    return x.reshape(x.shape[0] // 8, 8, x.shape[1]).sum(axis=0)


def _ln_stats(r):
    mu = jnp.mean(r, -1, keepdims=True)
    xc = r - mu
    rstd = lax.rsqrt(jnp.mean(xc * xc, -1, keepdims=True) + EPS)
    return xc * rstd, rstd


def _sigmoid(x):
    return jax.nn.sigmoid(x)


def _gelu(x):
    return 0.5 * x * (1.0 + jnp.tanh(GELU_C * (x + GELU_A * x * x * x)))


def _gelu_grad(x):
    t = jnp.tanh(GELU_C * (x + GELU_A * x * x * x))
    return 0.5 * (1.0 + t) + 0.5 * x * (1.0 - t * t) * GELU_C * (1.0 + 3.0 * GELU_A * x * x)


MM_ROWS = 1024
DW_TOKENS = 2048


def _matmul(a, b, *, name, M, N, K, ta=False, tb=False, out_dtype=F32, tm=MM_ROWS, tn=1024, tk=1024,
            a_spec=None, b_spec=None, b_merge=None, out_shape=None, o_spec=None, into=None,
            a_sq=False, mul=None, add=None, add_scale=1.0, plan=None):
    tm, tn, tk = min(tm, M), min(tn, N), min(tk, K)
    assert M % tm == 0 and N % tn == 0 and K % tk == 0
    grid = (M // tm, N // tn, K // tk)
    nk = grid[2]
    if a_spec is None:
        a_spec = pl.BlockSpec((tk, tm), lambda i, j, k: (k, i)) if ta else pl.BlockSpec((tm, tk), lambda i, j, k: (i, k))
    if b_spec is None:
        b_spec = pl.BlockSpec((tn, tk), lambda i, j, k: (j, k)) if tb else pl.BlockSpec((tk, tn), lambda i, j, k: (k, j))
    if o_spec is None:
        o_spec = pl.BlockSpec((tm, tn), lambda i, j, k: (i, j))
        out_shape = jax.ShapeDtypeStruct((M, N), out_dtype)
    e_spec = pl.BlockSpec((tm, tn), lambda i, j, k: (i, j))
    dims = (((0 if ta else 1,), (1 if tb else 0,)), ((), ()))
    extra = [e for e in (mul, add, into) if e is not None]
    n_in = 2 + len(extra)

    def body(*refs):
        ins, outs, scr, pctx = _split_refs(refs, n_in, 1, 1 if nk > 1 else 0, plan)
        a_ref, b_ref = ins[0], ins[1]
        rest = list(ins[2:])
        mul_ref = rest.pop(0) if mul is not None else None
        add_ref = rest.pop(0) if add is not None else None
        o_ref = outs[0]
        _plan_start(plan, pctx, grid)
        av = a_ref[...].astype(BF16)
        if a_sq:
            av = av * av
        bv = b_ref[...]
        if b_merge is not None:
            bv = bv.reshape(b_merge)
        p = lax.dot_general(av, bv, dims, preferred_element_type=F32)

        def finish(r):
            if mul_ref is not None:
                r = r * (2.0 * mul_ref[...].astype(F32))
            if add_ref is not None:
                r = r + add_scale * add_ref[...]
            o_ref[...] = r.astype(o_ref.dtype)

        if nk == 1:
            finish(p)
        else:
            acc_ref = scr[0]
            k = pl.program_id(2)

            @pl.when(k == 0)
            def _():
                acc_ref[...] = p

            @pl.when(k > 0)
            def _():
                acc_ref[...] += p

            @pl.when(k == nk - 1)
            def _():
                finish(acc_ref[...])

        _plan_wait(plan, pctx, grid)

    p_in, p_ospec, p_oshape, p_scr, p_alias = _plan_io(plan, n_in, 1)
    aliases = dict(p_alias)
    if into is not None:
        aliases[n_in - 1] = 0
    return pl.pallas_call(
        body, name=name, grid=grid,
        in_specs=[a_spec, b_spec] + [e_spec] * (len(extra) - (into is not None)) + [_ANY] * (into is not None)
        + [_ANY] * len(p_in),
        out_specs=[o_spec] + p_ospec, out_shape=[out_shape] + p_oshape,
        scratch_shapes=([pltpu.VMEM((tm, tn), F32)] if nk > 1 else []) + p_scr,
        input_output_aliases=aliases, compiler_params=_params(48, 3),
    )(*_hbm(a, b, *extra), *p_in)


def _rows4_spec(rowblk, n_axes):
    return pl.BlockSpec((4, 256, D), lambda *_: (0, rowblk, 0))


def _proj_ln(a_b, w, h_prev, g, b, *, name, w_rowblk=None, plan=None):
    T = a_b.shape[0]
    tm = min(ROW_BLOCK, T)
    grid = (T // tm,)
    row = pl.BlockSpec((tm, D), lambda i: (i, 0))
    vec = pl.BlockSpec((1, D), lambda i: (0, 0))
    w_spec = pl.BlockSpec((D, D), lambda i: (0, 0)) if w_rowblk is None else _rows4_spec(w_rowblk, 1)

    def body(*refs):
        (a_ref, w_ref, h_ref, g_ref, b_ref), (r_ref, ho_ref, hb_ref), _, pctx = _split_refs(refs, 5, 3, 0, plan)
        _plan_start(plan, pctx, grid)
        mix = jnp.dot(a_ref[...], w_ref[...].reshape(D, D), preferred_element_type=F32)
        r = ALPHA * h_ref[...] + mix
        xhat, _ = _ln_stats(r)
        y = xhat * g_ref[...] + b_ref[...]
        r_ref[...] = r
        ho_ref[...] = y
        hb_ref[...] = y.astype(BF16)
        _plan_wait(plan, pctx, grid)

    p_in, p_ospec, p_oshape, p_scr, p_alias = _plan_io(plan, 5, 3)
    return pl.pallas_call(
        body, name=name, grid=grid,
        in_specs=[row, w_spec, row, vec, vec] + [_ANY] * len(p_in),
        out_specs=[row, row, row] + p_ospec,
        out_shape=[jax.ShapeDtypeStruct((T, D), F32), jax.ShapeDtypeStruct((T, D), F32),
                   jax.ShapeDtypeStruct((T, D), BF16)] + p_oshape,
        scratch_shapes=p_scr, input_output_aliases=p_alias, compiler_params=_params(40, 1),
    )(*_hbm(a_b, w, h_prev, g, b), *p_in)


def _ffn_ln(h_b, wbuf, h, g, b, *, name, plan=None):
    T = h_b.shape[0]
    tm, tf = min(ROW_BLOCK, T), 1024
    nf = 4
    F = nf * tf
    grid = (T // tm, nf)
    row = pl.BlockSpec((tm, D), lambda i, j: (i, 0))
    vec = pl.BlockSpec((1, D), lambda i, j: (0, 0))

    def body(*refs):
        ((hb_ref, w1_ref, w2_ref, h_ref, g_ref, b_ref), (ra_ref, r_ref, ho_ref, hbo_ref), (acc_ref,),
         pctx) = _split_refs(refs, 6, 4, 1, plan)
        _plan_start(plan, pctx, grid)
        j = pl.program_id(1)
        a = jnp.dot(hb_ref[...], w1_ref[...], preferred_element_type=F32)
        ra = jnp.maximum(a, 0.0)
        ra_ref[...] = ra.astype(BF16)
        p = jnp.dot((ra * ra).astype(BF16), w2_ref[...], preferred_element_type=F32)

        @pl.when(j == 0)
        def _():
            acc_ref[...] = p

        @pl.when(j > 0)
        def _():
            acc_ref[...] += p

        @pl.when(j == nf - 1)
        def _():
            r = ALPHA * h_ref[...] + acc_ref[...]
            xhat, _ = _ln_stats(r)
            y = xhat * g_ref[...] + b_ref[...]
            r_ref[...] = r
            ho_ref[...] = y
            hbo_ref[...] = y.astype(BF16)

        _plan_wait(plan, pctx, grid)

    p_in, p_ospec, p_oshape, p_scr, p_alias = _plan_io(plan, 6, 4)
    return pl.pallas_call(
        body, name=name, grid=grid,
        in_specs=[row, pl.BlockSpec((None, D, tf), lambda i, j: (j, 0, 0)),
                  pl.BlockSpec((None, tf, D), lambda i, j: (j, 1, 0)), row, vec, vec] + [_ANY] * len(p_in),
        out_specs=[pl.BlockSpec((tm, tf), lambda i, j: (i, j)), row, row, row] + p_ospec,
        out_shape=[jax.ShapeDtypeStruct((T, F), BF16), jax.ShapeDtypeStruct((T, D), F32),
                   jax.ShapeDtypeStruct((T, D), F32), jax.ShapeDtypeStruct((T, D), BF16)] + p_oshape,
        scratch_shapes=[pltpu.VMEM((tm, D), F32)] + p_scr,
        input_output_aliases=p_alias, compiler_params=_params(48, 2),
    )(*_hbm(h_b, wbuf, wbuf, h, g, b), *p_in)


def _ln_bwd(dy, r, g, *, name, tgt=None):
    T = dy.shape[0]
    tm = min(ROW_BLOCK, T)
    row = pl.BlockSpec((tm, D), lambda i: (i, 0))
    acc = pl.BlockSpec((8, D), lambda i: (0, 0))
    n_in = 3 + (tgt is not None)

    def body(*refs):
        dy_ref, r_ref, g_ref = refs[:3]
        dr_ref, drb_ref, dg_ref, db_ref = refs[n_in:n_in + 4]

        @pl.when(pl.program_id(0) == 0)
        def _():
            for ref in refs[n_in + 2:]:
                ref[...] = jnp.zeros_like(ref)

        dy_ = dy_ref[...]
        if tgt is not None:
            err = dy_ - refs[3][...]
            refs[n_in + 4][...] += _fold8(err * err)
            dy_ = err * (1.0 / D)
        xhat, rstd = _ln_stats(r_ref[...])
        dxh = dy_ * g_ref[...]
        m1 = jnp.mean(dxh, -1, keepdims=True)
        m2 = jnp.mean(dxh * xhat, -1, keepdims=True)
        dr = rstd * (dxh - m1 - xhat * m2)
        dr_ref[...] = dr
        drb_ref[...] = dr.astype(BF16)
        dg_ref[...] += _fold8(dy_ * xhat)
        db_ref[...] += _fold8(dy_)

    extra = [] if tgt is None else [tgt]
    return pl.pallas_call(
        body, name=name, grid=(T // tm,),
        in_specs=[row, row, pl.BlockSpec((1, D), lambda i: (0, 0))] + [row] * len(extra),
        out_specs=[row, row, acc, acc] + [acc] * len(extra),
        out_shape=[jax.ShapeDtypeStruct((T, D), F32), jax.ShapeDtypeStruct((T, D), BF16)]
        + [jax.ShapeDtypeStruct((8, D), F32)] * (2 + len(extra)),
        compiler_params=_params(40, 1),
    )(*_hbm(dy, r, g, *extra))


def _rope(x, c, s1, s2):
    return x * c + pltpu.roll(x, 112, 1) * s1 + pltpu.roll(x, 16, 1) * s2


def _rope_t(dy, c, s1, s2):
    return dy * c + pltpu.roll(dy * s1, 16, 1) + pltpu.roll(dy * s2, 112, 1)


def _rms(x, g):
    rstd = lax.rsqrt(jnp.mean(x * x, -1, keepdims=True) + EPS)
    xhat = x * rstd
    return xhat * g, xhat, rstd


def _mla_prep(z0, gq, gkv, wq, wk, wv, rc, rs1, rs2):
    T = z0.shape[0]
    tm = min(ROW_BLOCK, T)
    HW = HEADS * 128

    def body(cq_ref, ckv_ref, kr_ref, gq_ref, gkv_ref, wq_ref, wk_ref, wv_ref, c_ref, s1_ref, s2_ref,
             q_ref, k_ref, v_ref):
        nq = _rms(cq_ref[...], gq_ref[...])[0].astype(BF16)
        nkv = _rms(ckv_ref[...], gkv_ref[...])[0].astype(BF16)
        q = jnp.dot(nq, wq_ref[...], preferred_element_type=F32)
        k = jnp.dot(nkv, wk_ref[...], preferred_element_type=F32)
        v = jnp.dot(nkv, wv_ref[...], preferred_element_type=F32)
        c, s1, s2 = c_ref[...], s1_ref[...], s2_ref[...]
        kr = _rope(pltpu.roll(kr_ref[...], 64, 1), c, s1, s2)
        for h in range(HEADS):
            sl = slice(h * 128, (h + 1) * 128)
            q_ref[:, sl] = (_rope(q[:, sl], c, s1, s2) * QK_SCALE).astype(BF16)
            k_ref[:, sl] = (k[:, sl] + kr).astype(BF16)
        v_ref[...] = v.astype(BF16)

    full = lambda shape: pl.BlockSpec(shape, lambda i: (0, 0))
    tab = pl.BlockSpec((tm, 128), lambda i: (i, 0))
    return pl.pallas_call(
        body, name="mla_prep", grid=(T // tm,),
        in_specs=[pl.BlockSpec((tm, 256), lambda i: (i, 0)), pl.BlockSpec((tm, 256), lambda i: (i, 1)),
                  pl.BlockSpec((tm, 128), lambda i: (i, 12)), full((1, 256)), full((1, 256)),
                  full((256, HW)), full((256, HW)), full((256, 512)), tab, tab, tab],
        out_specs=[pl.BlockSpec((tm, HW), lambda i: (i, 0)), pl.BlockSpec((tm, HW), lambda i: (i, 0)),
                   pl.BlockSpec((tm, 512), lambda i: (i, 0))],
        out_shape=[jax.ShapeDtypeStruct((T, HW), BF16), jax.ShapeDtypeStruct((T, HW), BF16),
                   jax.ShapeDtypeStruct((T, 512), BF16)],
        compiler_params=_params(40, 1),
    )(z0, z0, z0, gq, gkv, wq, wk, wv, rc, rs1, rs2)


def _flash_fwd(q, k, v, plan=None):
    T = q.shape[0]
    bq = min(2 * ROW_BLOCK, T)
    nq = T // bq
    grid = (4, nq, nq)

    def body(*refs):
        (q_ref, k_ref, v_ref), (o_ref, lse_ref), (m_sc, acc_sc), pctx = _split_refs(refs, 3, 2, 2, plan)
        _plan_start(plan, pctx, grid)
        i, j = pl.program_id(1), pl.program_id(2)
        first = lax.broadcasted_iota(jnp.int32, (bq, 128), 1) < 64

        @pl.when(j == 0)
        def _():
            m_sc[...] = jnp.full_like(m_sc, -jnp.inf)
            acc_sc[...] = jnp.zeros_like(acc_sc)

        def tile(r0, nr, nc, masked):
            rs = slice(r0, r0 + nr)
            vp = v_ref[0:nc, :]
            lanes = first[0:nc, :]
            for h in range(2):
                sl = slice(h * 128, (h + 1) * 128)
                s = lax.dot_general(q_ref[rs, sl], k_ref[0:nc, sl], NT_DIMS, preferred_element_type=F32)
                if masked:
                    rows = r0 + lax.broadcasted_iota(jnp.int32, (nr, nc), 0)
                    cols = lax.broadcasted_iota(jnp.int32, (nr, nc), 1)
                    s = jnp.where(cols <= rows, s, -jnp.inf)
                m_prev = m_sc[h, rs, 0:1]
                m_new = jnp.maximum(m_prev, jnp.max(s, axis=1, keepdims=True))
                alpha = jnp.exp(m_prev - m_new)
                p = jnp.exp(s - m_new).astype(BF16)
                vh = jnp.where(lanes if h == 0 else jnp.logical_not(lanes), vp, jnp.ones_like(vp))
                acc_sc[h, rs, :] = acc_sc[h, rs, :] * alpha + jnp.dot(p, vh, preferred_element_type=F32)
                m_sc[h, rs, :] = jnp.broadcast_to(m_new, (nr, 128))

        @pl.when(j < i)
        def _():
            tile(0, bq, bq, False)

        @pl.when(j == i)
        def _():
            tile(0, bq, bq, True)
            a0, a1 = acc_sc[0], acc_sc[1]
            l0, l1 = pltpu.roll(a0, 64, 1), pltpu.roll(a1, 64, 1)
            o_ref[...] = jnp.where(first, a0 / l0, a1 / l1).astype(BF16)
            lse_ref[...] = jnp.where(first, m_sc[0] + jnp.log(l0), m_sc[1] + jnp.log(l1))

        _plan_wait(plan, pctx, grid)

    kv = lambda hp, i, j: (jnp.minimum(i, j), hp)
    p_in, p_ospec, p_oshape, p_scr, p_alias = _plan_io(plan, 3, 2)
    return pl.pallas_call(
        body, name="flash_fwd", grid=grid,
        in_specs=[pl.BlockSpec((bq, 256), lambda hp, i, j: (i, hp)), pl.BlockSpec((bq, 256), kv),
                  pl.BlockSpec((bq, 128), kv)] + [_ANY] * len(p_in),
        out_specs=[pl.BlockSpec((bq, 128), lambda hp, i, j: (i, hp)),
                   pl.BlockSpec((bq, 128), lambda hp, i, j: (i, hp))] + p_ospec,
        out_shape=[jax.ShapeDtypeStruct((T, 512), BF16), jax.ShapeDtypeStruct((T, 512), F32)] + p_oshape,
        scratch_shapes=[pltpu.VMEM((2, bq, 128), F32), pltpu.VMEM((2, bq, 128), F32)] + p_scr,
        input_output_aliases=p_alias, compiler_params=_params(56, 3),
    )(*_hbm(q, k, v), *p_in)


def _attn_delta(dmix, o):
    T = o.shape[0]
    tm = min(ROW_BLOCK, T)
    blk = pl.BlockSpec((tm, 512), lambda i: (i, 0))

    def body(do_ref, o_ref, delta_ref, dob_ref):
        first = lax.broadcasted_iota(jnp.int32, (tm, 128), 1) < 64
        for hp in range(4):
            sl = slice(hp * 128, (hp + 1) * 128)
            prod = do_ref[:, sl] * o_ref[:, sl].astype(F32)
            d0 = jnp.sum(jnp.where(first, prod, 0.0), axis=1, keepdims=True)
            d1 = jnp.sum(jnp.where(first, 0.0, prod), axis=1, keepdims=True)
            delta_ref[:, sl] = jnp.where(first, d0, d1)
        dob_ref[...] = do_ref[...].astype(BF16)

    return pl.pallas_call(
        body, name="attn_delta", grid=(T // tm,), in_specs=[blk, blk], out_specs=[blk, blk],
        out_shape=[jax.ShapeDtypeStruct((T, 512), F32), jax.ShapeDtypeStruct((T, 512), BF16)],
        compiler_params=_params(32, 1),
    )(dmix, o)


def _flash_bwd(q, k, v, do_b, lse, delta, plan=None):
    T = q.shape[0]
    bq = min(2 * ROW_BLOCK, T)
    nq = T // bq
    grid = (4, nq, nq)

    def body(*refs):
        ((q_ref, k_ref, v_ref, do_ref, lse_ref, dl_ref), (dq_hbm, dk_ref, dv_ref), (dq_sc, dk_sc, dv_sc, sem),
         pctx) = _split_refs(refs, 6, 3, 4, plan)
        _plan_start(plan, pctx, grid)
        hp, j, i = pl.program_id(0), pl.program_id(1), pl.program_id(2)
        first = lax.broadcasted_iota(jnp.int32, (bq, 128), 1) < 64

        @pl.when((j == 0) & (i == 0))
        def _():
            dq_sc[...] = jnp.zeros_like(dq_sc)

        @pl.when(i == j)
        def _():
            dk_sc[...] = jnp.zeros_like(dk_sc)
            dv_sc[...] = jnp.zeros_like(dv_sc)

        def tile(r0, nr, nc, masked):
            rs, cs = slice(r0, r0 + nr), slice(0, nc)
            vp = v_ref[cs, :]
            do = do_ref[rs, :]
            lanes = first[rs, :]
            for h in range(2):
                sl = slice(h * 128, (h + 1) * 128)
                qh, kh = q_ref[rs, sl], k_ref[cs, sl]
                s = lax.dot_general(qh, kh, NT_DIMS, preferred_element_type=F32)
                p = jnp.exp(s - lse_ref[rs, h * 64:h * 64 + 1])
                if masked:
                    rows = r0 + lax.broadcasted_iota(jnp.int32, (nr, nc), 0)
                    cols = lax.broadcasted_iota(jnp.int32, (nr, nc), 1)
                    p = jnp.where(cols <= rows, p, 0.0)
                do_h = jnp.where(lanes if h == 0 else jnp.logical_not(lanes), do, jnp.zeros_like(do))
                dv_sc[cs, :] += lax.dot_general(p.astype(BF16), do_h, TN_DIMS, preferred_element_type=F32)
                dp = lax.dot_general(do_h, vp, NT_DIMS, preferred_element_type=F32)
                ds = (p * (dp - dl_ref[rs, h * 64:h * 64 + 1])).astype(BF16)
                dq_sc[i, rs, sl] += jnp.dot(ds, kh, preferred_element_type=F32)
                dk_sc[cs, sl] += lax.dot_general(ds, qh, TN_DIMS, preferred_element_type=F32)

        @pl.when(i > j)
        def _():
            tile(0, bq, bq, False)

        @pl.when(i == j)
        def _():
            tile(0, bq // 2, bq // 2, True)
            tile(bq // 2, bq // 2, bq, True)

        @pl.when(i == nq - 1)
        def _():
            dk_ref[...] = dk_sc[...]
            dv_ref[...] = dv_sc[...]

        @pl.when((j == nq - 1) & (i == nq - 1))
        def _():
            cp = pltpu.make_async_copy(dq_sc, dq_hbm.at[hp], sem)
            cp.start()
            cp.wait()

        _plan_wait(plan, pctx, grid)

    qi = lambda hp, j, i: (jnp.maximum(i, j), hp)
    kj = lambda hp, j, i: (j, hp)
    p_in, p_ospec, p_oshape, p_scr, p_alias = _plan_io(plan, 6, 3)
    return pl.pallas_call(
        body, name="flash_bwd", grid=grid,
        in_specs=[pl.BlockSpec((bq, 256), qi), pl.BlockSpec((bq, 256), kj), pl.BlockSpec((bq, 128), kj),
                  pl.BlockSpec((bq, 128), qi), pl.BlockSpec((bq, 128), qi), pl.BlockSpec((bq, 128), qi)]
        + [_ANY] * len(p_in),
        out_specs=[_ANY, pl.BlockSpec((bq, 256), kj), pl.BlockSpec((bq, 128), kj)] + p_ospec,
        out_shape=[jax.ShapeDtypeStruct((4, nq, bq, 256), F32), jax.ShapeDtypeStruct((T, 1024), F32),
                   jax.ShapeDtypeStruct((T, 512), F32)] + p_oshape,
        scratch_shapes=[pltpu.VMEM((nq, bq, 256), F32), pltpu.VMEM((bq, 256), F32), pltpu.VMEM((bq, 128), F32),
                        pltpu.SemaphoreType.DMA] + p_scr,
        input_output_aliases=p_alias, compiler_params=_params(56, 3),
    )(*_hbm(q, k, v, do_b, lse, delta), *p_in)


def _mla_bwd(z0, dq4, dk, dv, gq, gkv, wq, wk, wv, rc, rs1, rs2, plan=None):
    T = z0.shape[0]
    tm = min(ROW_BLOCK, T)
    HW = HEADS * 128
    grid = (T // tm,)
    dq4 = dq4.reshape(4, T, 256)

    def body(*refs):
        ((cq_ref, ckv_ref, dq_ref, dk_ref, dv_ref, gq_ref, gkv_ref, wq_ref, wk_ref, wv_ref, c_ref, s1_ref, s2_ref),
         (dc_ref, dkr_ref, dwq_ref, dwk_ref, dwv_ref, dgq_ref, dgkv_ref), _, pctx) = _split_refs(refs, 13, 7, 0, plan)
        _plan_start(plan, pctx, grid)

        @pl.when(pl.program_id(0) == 0)
        def _():
            for ref in (dwq_ref, dwk_ref, dwv_ref, dgq_ref, dgkv_ref):
                ref[...] = jnp.zeros_like(ref)

        c, s1, s2 = c_ref[...], s1_ref[...], s2_ref[...]
        lane = lax.broadcasted_iota(jnp.int32, (tm, 128), 1)
        nq, xq, rq = _rms(cq_ref[...], gq_ref[...])
        nkv, xkv, rkv = _rms(ckv_ref[...], gkv_ref[...])
        nq_b, nkv_b = nq.astype(BF16), nkv.astype(BF16)

        dq_parts, dk_parts = [], []
        dkr = jnp.zeros((tm, 128), F32)
        for h in range(HEADS):
            blk = dq_ref[h // 2, :, (h % 2) * 128:(h % 2 + 1) * 128] * QK_SCALE
            dq_parts.append(_rope_t(blk, c, s1, s2).astype(BF16))
            kb = dk_ref[:, h * 128:(h + 1) * 128]
            dk_parts.append(jnp.where(lane < NOPE, kb, 0.0).astype(BF16))
            dkr = dkr + kb
        dq_b = jnp.concatenate(dq_parts, axis=1)
        dk_b = jnp.concatenate(dk_parts, axis=1)
        dv_b = dv_ref[...].astype(BF16)

        dwq_ref[...] += lax.dot_general(nq_b, dq_b, TN_DIMS, preferred_element_type=F32)
        dwk_ref[...] += lax.dot_general(nkv_b, dk_b, TN_DIMS, preferred_element_type=F32)
        dwv_ref[...] += lax.dot_general(nkv_b, dv_b, TN_DIMS, preferred_element_type=F32)
        dnq = lax.dot_general(dq_b, wq_ref[...], NT_DIMS, preferred_element_type=F32)
        dnkv = (lax.dot_general(dk_b, wk_ref[...], NT_DIMS, preferred_element_type=F32)
                + lax.dot_general(dv_b, wv_ref[...], NT_DIMS, preferred_element_type=F32))

        def rms_bwd(dn, xhat, rstd, g):
            dxh = dn * g
            return rstd * (dxh - xhat * jnp.mean(dxh * xhat, -1, keepdims=True))

        dc_ref[:, :256] = rms_bwd(dnq, xq, rq, gq_ref[...]).astype(BF16)
        dc_ref[:, 256:] = rms_bwd(dnkv, xkv, rkv, gkv_ref[...]).astype(BF16)
        dgq_ref[...] += _fold8(dnq * xq)
        dgkv_ref[...] += _fold8(dnkv * xkv)
        dkr = pltpu.roll(_rope_t(dkr, c, s1, s2), 64, 1)
        dkr_ref[...] = jnp.where(lane < ROPE, dkr, 0.0).astype(BF16)
        _plan_wait(plan, pctx, grid)

    full = lambda shape: pl.BlockSpec(shape, lambda i: (0,) * len(shape))
    tab = pl.BlockSpec((tm, 128), lambda i: (i, 0))
    p_in, p_ospec, p_oshape, p_scr, p_alias = _plan_io(plan, 13, 7)
    return pl.pallas_call(
        body, name="mla_bwd", grid=grid,
        in_specs=[pl.BlockSpec((tm, 256), lambda i: (i, 0)), pl.BlockSpec((tm, 256), lambda i: (i, 1)),
                  pl.BlockSpec((4, tm, 256), lambda i: (0, i, 0)),
                  pl.BlockSpec((tm, HW), lambda i: (i, 0)), pl.BlockSpec((tm, 512), lambda i: (i, 0)),
                  full((1, 256)), full((1, 256)), full((256, HW)), full((256, HW)), full((256, 512)), tab, tab, tab]
        + [_ANY] * len(p_in),
        out_specs=[pl.BlockSpec((tm, 512), lambda i: (i, 0)), tab, full((256, HW)), full((256, HW)),
                   full((256, 512)), full((8, 256)), full((8, 256))] + p_ospec,
        out_shape=[jax.ShapeDtypeStruct((T, 512), BF16), jax.ShapeDtypeStruct((T, 128), BF16),
                   jax.ShapeDtypeStruct((256, HW), F32), jax.ShapeDtypeStruct((256, HW), F32),
                   jax.ShapeDtypeStruct((256, 512), F32), jax.ShapeDtypeStruct((8, 256), F32),
                   jax.ShapeDtypeStruct((8, 256), F32)] + p_oshape,
        scratch_shapes=p_scr, input_output_aliases=p_alias, compiler_params=_params(48, 1),
    )(*_hbm(z0, z0, dq4, dk, dv, gq, gkv, wq, wk, wv, rc, rs1, rs2), *p_in)


def _sgu_fwd(z0, a_out, ln_g, ln_b, w, b_t):
    T = z0.shape[0]
    tm = min(ROW_BLOCK, T)
    W = SGU_G * SGU_C

    def body(u_ref, v_ref, a_ref, g_ref, b_ref, w_ref, bt_ref, o_ref):
        o_ref[:, :W] = a_ref[...]
        ug = _gelu(u_ref[...])
        xhat, _ = _ln_stats(_gelu(v_ref[...]))
        vn = (xhat * g_ref[...] + b_ref[...]).astype(BF16)
        tril = lax.broadcasted_iota(jnp.int32, (SGU_C, SGU_C), 0) >= lax.broadcasted_iota(jnp.int32, (SGU_C, SGU_C), 1)
        for g in range(SGU_G):
            cs = slice(g * SGU_C, (g + 1) * SGU_C)
            wg = jnp.where(tril, w_ref[g], 0.0).astype(BF16)
            bcol = bt_ref[:, g:g + 1]
            for c in range(tm // SGU_C):
                rs = slice(c * SGU_C, (c + 1) * SGU_C)
                mixed = jnp.dot(wg, vn[rs, cs], preferred_element_type=F32) + bcol
                o_ref[rs, W + g * SGU_C:W + (g + 1) * SGU_C] = (ug[rs, cs] * mixed).astype(BF16)

    full = lambda shape: pl.BlockSpec(shape, lambda i: (0,) * len(shape))
    return pl.pallas_call(
        body, name="sgu_fwd", grid=(T // tm,),
        in_specs=[pl.BlockSpec((tm, W), lambda i: (i, 1)), pl.BlockSpec((tm, W), lambda i: (i, 2)),
                  pl.BlockSpec((tm, W), lambda i: (i, 0)),
                  full((1, W)), full((1, W)), full((SGU_G, SGU_C, SGU_C)), full((SGU_C, SGU_G))],
        out_specs=pl.BlockSpec((tm, 2 * W), lambda i: (i, 0)),
        out_shape=jax.ShapeDtypeStruct((T, 2 * W), BF16),
        compiler_params=_params(32, 1),
    )(z0, z0, a_out, ln_g, ln_b, w, b_t)


def _sgu_bwd(z0, dmix, dc, dkr, ln_g, ln_b, w, b_t):
    T = z0.shape[0]
    tm = min(ROW_BLOCK, T)
    W = SGU_G * SGU_C

    def body(u_ref, v_ref, do_ref, dc_ref, dkr_ref, g_ref, b_ref, w_ref, bt_ref, dz_ref, dw_ref, db_ref, dlg_ref,
             dlb_ref):
        @pl.when(pl.program_id(0) == 0)
        def _():
            for ref in (dw_ref, db_ref, dlg_ref, dlb_ref):
                ref[...] = jnp.zeros_like(ref)

        dz_ref[:, :W] = dc_ref[...]
        dz_ref[:, 3 * W:] = dkr_ref[...]

        u, v, dout = u_ref[...], v_ref[...], do_ref[...]
        ug = _gelu(u)
        xhat, rstd = _ln_stats(_gelu(v))
        vn = (xhat * g_ref[...] + b_ref[...]).astype(BF16)
        dmixed = dout * ug
        dmixed_b = dmixed.astype(BF16)
        tril = lax.broadcasted_iota(jnp.int32, (SGU_C, SGU_C), 0) >= lax.broadcasted_iota(jnp.int32, (SGU_C, SGU_C), 1)
        lane = lax.broadcasted_iota(jnp.int32, (SGU_C, SGU_C), 1)
        dvn_cols = []
        for g in range(SGU_G):
            cs = slice(g * SGU_C, (g + 1) * SGU_C)
            wg = jnp.where(tril, w_ref[g], 0.0).astype(BF16)
            bcol = bt_ref[:, g:g + 1]
            dw_g = jnp.zeros((SGU_C, SGU_C), F32)
            db_g = jnp.zeros((SGU_C, 1), F32)
            dvn_rows = []
            for c in range(tm // SGU_C):
                rs = slice(c * SGU_C, (c + 1) * SGU_C)
                mixed = jnp.dot(wg, vn[rs, cs], preferred_element_type=F32) + bcol
                dz_ref[rs, W + g * SGU_C:W + (g + 1) * SGU_C] = (dout[rs, cs] * mixed * _gelu_grad(u[rs, cs])).astype(BF16)
                dm = dmixed_b[rs, cs]
                dw_g = dw_g + lax.dot_general(dm, vn[rs, cs], NT_DIMS, preferred_element_type=F32)
                db_g = db_g + jnp.sum(dmixed[rs, cs], axis=1, keepdims=True)
                dvn_rows.append(lax.dot_general(wg, dm, TN_DIMS, preferred_element_type=F32))
            dw_ref[g] += jnp.where(tril, dw_g, 0.0)
            db_ref[...] += jnp.where(lane == g, db_g, 0.0)
            dvn_cols.append(jnp.concatenate(dvn_rows, axis=0))
        dvn = jnp.concatenate(dvn_cols, axis=1)
        dxh = dvn * g_ref[...]
        m1 = jnp.mean(dxh, -1, keepdims=True)
        m2 = jnp.mean(dxh * xhat, -1, keepdims=True)
        dvg = rstd * (dxh - m1 - xhat * m2)
        dz_ref[:, 2 * W:3 * W] = (dvg * _gelu_grad(v)).astype(BF16)
        dlg_ref[...] += _fold8(dvn * xhat)
        dlb_ref[...] += _fold8(dvn)

    full = lambda shape: pl.BlockSpec(shape, lambda i: (0,) * len(shape))
    return pl.pallas_call(
        body, name="sgu_bwd", grid=(T // tm,),
        in_specs=[pl.BlockSpec((tm, W), lambda i: (i, 1)), pl.BlockSpec((tm, W), lambda i: (i, 2)),
                  pl.BlockSpec((tm, W), lambda i: (i, 1)), pl.BlockSpec((tm, W), lambda i: (i, 0)),
                  pl.BlockSpec((tm, 128), lambda i: (i, 0)),
                  full((1, W)), full((1, W)), full((SGU_G, SGU_C, SGU_C)), full((SGU_C, SGU_G))],
        out_specs=[pl.BlockSpec((tm, 3 * W + 128), lambda i: (i, 0)), full((SGU_G, SGU_C, SGU_C)),
                   full((SGU_C, SGU_C)), full((8, W)), full((8, W))],
        out_shape=[jax.ShapeDtypeStruct((T, 3 * W + 128), BF16), jax.ShapeDtypeStruct((SGU_G, SGU_C, SGU_C), F32),
                   jax.ShapeDtypeStruct((SGU_C, SGU_C), F32), jax.ShapeDtypeStruct((8, W), F32),
                   jax.ShapeDtypeStruct((8, W), F32)],
        compiler_params=_params(40, 1),
    )(z0, z0, dmix, dc, dkr, ln_g, ln_b, w, b_t)


def _hg_lower_bound(lb_ref):
    a0, a1 = lb_ref[0:1, :], lb_ref[1:2, :]
    m = jnp.maximum(a0, a1)
    e0, e1 = jnp.exp(a0 - m), jnp.exp(a1 - m)
    return e1 / (e0 + e1)


def _running_sum(x, reverse=False):
    n = x.shape[0]
    row = lax.broadcasted_iota(jnp.int32, x.shape, 0)
    s = 1
    while s < n:
        if reverse:
            x = x + jnp.where(row < n - s, pltpu.roll(x, n - s, 0), 0.0)
        else:
            x = x + jnp.where(row >= s, pltpu.roll(x, s, 0), 0.0)
        s *= 2
    return x


def _hg_chunk(qc, fc, lb):
    C = HG_CHUNK
    rows = lax.broadcasted_iota(jnp.int32, (C, C), 0)
    cols = lax.broadcasted_iota(jnp.int32, (C, C), 1)
    rowid = lax.broadcasted_iota(jnp.int32, (C, 128), 0)
    sq, sg = _sigmoid(qc), _sigmoid(fc)
    qf = qc * sq
    gate = lb + (1.0 - lb) * sg
    kk = 1.0 - gate
    lg = jnp.log(gate)
    bcum = _running_sum(lg)
    b_mid = jnp.sum(jnp.where(rowid < C // 2, lg, 0.0), axis=0, keepdims=True)
    b_last = jnp.sum(lg, axis=0, keepdims=True)
    eq, ek, e, eh = jnp.exp(bcum - b_mid), jnp.exp(b_mid - bcum), jnp.exp(bcum), jnp.exp(b_last - bcum)
    qt, kt, qe, khat = qf * eq, kk * ek, qf * e, kk * eh
    a = lax.dot_general(qt.astype(BF16), kt.astype(BF16), NT_DIMS, preferred_element_type=F32)
    a = jnp.where(rows >= cols, a, 0.0)
    return dict(sq=sq, sg=sg, gate=gate, kk=kk, eq=eq, ek=ek, e=e, eh=eh, qt=qt, kt=kt, qe=qe, khat=khat, a=a,
                e_last=jnp.exp(b_last), tril=rows >= cols, rowid=rowid)


def _hgrn_fwd(z4, hg_lb, gnorm):
    T = z4.shape[1]
    tb = min(ROW_BLOCK, T)
    C = HG_CHUNK
    ncb = tb // C
    HPB = HG_HEADS_PER_STEP

    def body(q_ref, f_ref, i_ref, g_ref, lb_ref, gn_ref, y_ref, o_ref, st_ref, st_sc):
        @pl.when(pl.program_id(1) == 0)
        def _():
            st_sc[...] = jnp.zeros_like(st_sc)

        def chunk(c, carry):
            rs = pl.ds(pl.multiple_of(c * C, C), C)
            for hh in range(HPB):
                hs = slice(hh * 128, (hh + 1) * 128)
                lb = _hg_lower_bound(lb_ref.at[:, hs])
                v_b = i_ref[rs, hs].astype(BF16)
                gc = g_ref[rs, hs]
                x = _hg_chunk(q_ref[rs, hs], f_ref[rs, hs], lb)
                st = st_sc[hh]
                st_ref[hh, c] = st
                o = (jnp.dot(x["a"].astype(BF16), v_b, preferred_element_type=F32)
                     + lax.dot_general(x["qe"].astype(BF16), st.astype(BF16), NT_DIMS, preferred_element_type=F32))
                st_sc[hh] = st * x["e_last"] + lax.dot_general(v_b, x["khat"].astype(BF16), TN_DIMS,
                                                               preferred_element_type=F32)
                o_ref[rs, hs] = o
                n = o * lax.rsqrt(jnp.mean(o * o, -1, keepdims=True) + EPS)
                y_ref[rs, hs] = (n * gn_ref[:, hs] * (gc * _sigmoid(gc))).astype(BF16)
            return carry

        lax.fori_loop(0, ncb, chunk, 0)

    W = 128 * HPB
    zb = lambda k: pl.BlockSpec((None, tb, W), lambda h, t: (k, t, h))
    out = pl.BlockSpec((tb, W), lambda h, t: (t, h))
    return pl.pallas_call(
        body, name="hgrn_fwd", grid=(HEADS // HPB, T // tb),
        in_specs=[zb(0), zb(1), zb(2), zb(3), pl.BlockSpec((2, W), lambda h, t: (0, h)),
                  pl.BlockSpec((1, W), lambda h, t: (0, h))],
        out_specs=[out, out, pl.BlockSpec((HPB, ncb, 128, 128), lambda h, t: (h, t, 0, 0))],
        out_shape=[jax.ShapeDtypeStruct((T, D), BF16), jax.ShapeDtypeStruct((T, D), F32),
                   jax.ShapeDtypeStruct((HEADS, T // C, 128, 128), F32)],
        scratch_shapes=[pltpu.VMEM((HPB, 128, 128), F32)],
        compiler_params=_params(48, 2),
    )(*_hbm(z4, z4, z4, z4, hg_lb, gnorm))


def _hgrn_bwd(z4, o_raw, dy, states, hg_lb, gnorm):
    T = z4.shape[1]
    tb = min(ROW_BLOCK, T)
    C = HG_CHUNK
    ncb = tb // C
    nt = T // tb
    HPB = HG_HEADS_PER_STEP

    def body(q_ref, f_ref, i_ref, g_ref, o_ref, dy_ref, st_ref, lb_ref, gn_ref, dz_ref, dlb_ref, dgn_ref, dst_sc):
        @pl.when(pl.program_id(1) == 0)
        def _():
            dst_sc[...] = jnp.zeros_like(dst_sc)
            dlb_ref[...] = jnp.zeros_like(dlb_ref)
            dgn_ref[...] = jnp.zeros_like(dgn_ref)

        def chunk(cc, carry):
            for hh in range(HPB):
                one_head(ncb - 1 - cc, hh, slice(hh * 128, (hh + 1) * 128))
            return carry

        def one_head(c, hh, hs):
            rs = pl.ds(pl.multiple_of(c * C, C), C)
            lb = _hg_lower_bound(lb_ref.at[:, hs])
            gn = gn_ref[:, hs]
            qc, gc = q_ref[rs, hs], g_ref[rs, hs]
            v_b = i_ref[rs, hs].astype(BF16)
            x = _hg_chunk(qc, f_ref[rs, hs], lb)
            st, dst = st_ref[hh, c], dst_sc[hh]
            st_b, dst_b = st.astype(BF16), dst.astype(BF16)
            o, dyc = o_ref[rs, hs], dy_ref[rs, hs]
            sgg = _sigmoid(gc)
            sil = gc * sgg
            rstd = lax.rsqrt(jnp.mean(o * o, -1, keepdims=True) + EPS)
            n = o * rstd
            dgn_ref[:, hs] += _fold8(dyc * n * sil)
            dn = dyc * gn * sil
            do = rstd * (dn - n * jnp.mean(dn * n, -1, keepdims=True))
            dg = dyc * n * gn * (sgg * (1.0 + gc * (1.0 - sgg)))
            do_b = do.astype(BF16)
            da = jnp.where(x["tril"], lax.dot_general(do_b, v_b, NT_DIMS, preferred_element_type=F32), 0.0).astype(BF16)
            qt_b, kt_b, qe_b, khat_b = (x[n_].astype(BF16) for n_ in ("qt", "kt", "qe", "khat"))
            dv = (lax.dot_general(x["a"].astype(BF16), do_b, TN_DIMS, preferred_element_type=F32)
                  + lax.dot_general(khat_b, dst_b, NT_DIMS, preferred_element_type=F32))
            dqt = jnp.dot(da, kt_b, preferred_element_type=F32)
            dqe = jnp.dot(do_b, st_b, preferred_element_type=F32)
            dkt = lax.dot_general(da, qt_b, TN_DIMS, preferred_element_type=F32)
            dkhat = jnp.dot(v_b, dst_b, preferred_element_type=F32)
            dst_sc[hh] = lax.dot_general(do_b, qe_b, TN_DIMS, preferred_element_type=F32) + dst * x["e_last"]
            de_last = jnp.sum(st * dst, axis=0, keepdims=True)
            dqf = dqt * x["eq"] + dqe * x["e"]
            dkk = dkt * x["ek"] + dkhat * x["eh"]
            dkh_kh = dkhat * x["khat"]
            db = dqt * qt_b.astype(F32) - dkt * kt_b.astype(F32) + dqe * x["qe"] - dkh_kh
            db_last = jnp.sum(dkh_kh, axis=0, keepdims=True) + de_last * x["e_last"]
            db = db + jnp.where(x["rowid"] == C - 1, db_last, 0.0)
            dlg = _running_sum(db, reverse=True)
            dgate = dlg / x["gate"] - dkk
            sg, sq = x["sg"], x["sq"]
            dlb_ref[:, hs] += _fold8(dgate * (1.0 - sg)) * (lb * (1.0 - lb))
            dz_ref[0, rs, hs] = (dqf * (sq * (1.0 + qc * (1.0 - sq)))).astype(BF16)
            dz_ref[1, rs, hs] = (dgate * (1.0 - lb) * sg * (1.0 - sg)).astype(BF16)
            dz_ref[2, rs, hs] = dv.astype(BF16)
            dz_ref[3, rs, hs] = dg.astype(BF16)

        lax.fori_loop(0, ncb, chunk, 0)

    W = 128 * HPB
    zb = lambda k: pl.BlockSpec((None, tb, W), lambda h, t: (k, nt - 1 - t, h))
    blk = pl.BlockSpec((tb, W), lambda h, t: (nt - 1 - t, h))
    acc = pl.BlockSpec((8, W), lambda h, t: (0, h))
    return pl.pallas_call(
        body, name="hgrn_bwd", grid=(HEADS // HPB, nt),
        in_specs=[zb(0), zb(1), zb(2), zb(3), blk, blk,
                  pl.BlockSpec((HPB, ncb, 128, 128), lambda h, t: (h, nt - 1 - t, 0, 0)),
                  pl.BlockSpec((2, W), lambda h, t: (0, h)), pl.BlockSpec((1, W), lambda h, t: (0, h))],
        out_specs=[pl.BlockSpec((4, tb, W), lambda h, t: (0, nt - 1 - t, h)), acc, acc],
        out_shape=[jax.ShapeDtypeStruct((4, T, D), BF16), jax.ShapeDtypeStruct((8, D), F32),
                   jax.ShapeDtypeStruct((8, D), F32)],
        scratch_shapes=[pltpu.VMEM((HPB, 128, 128), F32)],
        compiler_params=_params(48, 2),
    )(*_hbm(z4, z4, z4, z4, o_raw, dy, states, hg_lb, gnorm))


def _adamw(w, g, m, v, *, name):
    R, L = w.shape
    tr = R if R <= 512 else 512
    assert R % tr == 0
    blk = pl.BlockSpec((tr, L), lambda i: (i, 0))
    c1, c2 = 1.0 - B1 ** STEP, 1.0 - B2 ** STEP

    def body(w_ref, g_ref, m_ref, v_ref, d_ref, mo_ref, vo_ref):
        g_ = g_ref[...]
        m_ = B1 * m_ref[...] + (1.0 - B1) * g_
        v_ = B2 * v_ref[...] + (1.0 - B2) * (g_ * g_)
        d_ref[...] = -LR * ((m_ / c1) / (jnp.sqrt(v_ / c2) + ADAM_EPS) + WD * w_ref[...])
        mo_ref[...] = m_
        vo_ref[...] = v_

    sds = jax.ShapeDtypeStruct((R, L), F32)
    return pl.pallas_call(
        body, name=name, grid=(R // tr,), in_specs=[blk] * 4, out_specs=[blk] * 3, out_shape=[sds] * 3,
        compiler_params=_params(32, 1),
    )(w, g, m, v)


def _adamw_rows(w, m, v, gbufs, row0, *, name, plan=None):
    L, R, C = w.shape
    tr = 256
    assert R % tr == 0 and row0 % tr == 0 and len(gbufs) == L
    grid = (L, R // tr)
    blk = pl.BlockSpec((None, tr, C), lambda l, i: (l, i, 0))
    gblk = pl.BlockSpec((tr, C), lambda l, i: (row0 // tr + i, 0))
    c1, c2 = 1.0 - B1 ** STEP, 1.0 - B2 ** STEP

    def body(*refs):
        ins, (go_ref, d_ref, mo_ref, vo_ref), _, pctx = _split_refs(refs, 3 + L, 4, 0, plan)
        w_ref, m_ref, v_ref = ins[:3]
        g_refs = ins[3:]
        _plan_start(plan, pctx, grid)
        g_ = g_refs[0][...]
        for l in range(1, L):
            g_ = jnp.where(pl.program_id(0) == l, g_refs[l][...], g_)
        m_ = B1 * m_ref[...] + (1.0 - B1) * g_
        v_ = B2 * v_ref[...] + (1.0 - B2) * (g_ * g_)
        go_ref[...] = g_
        d_ref[...] = -LR * ((m_ / c1) / (jnp.sqrt(v_ / c2) + ADAM_EPS) + WD * w_ref[...])
        mo_ref[...] = m_
        vo_ref[...] = v_
        _plan_wait(plan, pctx, grid)

    sds = jax.ShapeDtypeStruct((L, R, C), F32)
    p_in, p_ospec, p_oshape, p_scr, p_alias = _plan_io(plan, 3 + L, 4)
    return pl.pallas_call(
        body, name=name, grid=grid, in_specs=[blk] * 3 + [gblk] * L + [_ANY] * len(p_in),
        out_specs=[blk] * 4 + p_ospec, out_shape=[sds] * 4 + p_oshape, scratch_shapes=p_scr,
        input_output_aliases=p_alias, compiler_params=_params(32, 2),
    )(*_hbm(w, m, v, *gbufs), *p_in)


def _add_pairs(g, theirs, ids, *, name):
    n, R, L = theirs.shape
    tr = 128
    nb = R // tr

    def body(ids_ref, a_ref, b_ref, o_ref):
        o_ref[...] = (a_ref[...].astype(F32) + b_ref[...].astype(F32)).astype(BF16)

    blk = pl.BlockSpec((n, tr, L), lambda i, ids: (0, i, 0))
    return pl.pallas_call(
        body, name=name, out_shape=jax.ShapeDtypeStruct((n, R, L), BF16),
        grid_spec=pltpu.PrefetchScalarGridSpec(
            num_scalar_prefetch=1, grid=(nb,),
            in_specs=[pl.BlockSpec((n, tr, L), lambda i, ids: (0, ids[1] * nb + i, 0)), blk], out_specs=blk),
        compiler_params=_params(16, 1),
    )(ids, g, theirs)


def _sum_chips(pair, parts, ids, *, name):
    _, R, L = parts.shape
    tr = 128

    def body(ids_ref, o_ref, r_ref, out_ref):
        out_ref[...] = ((o_ref[...].astype(F32) + r_ref[0].astype(F32)) + r_ref[1].astype(F32)) + r_ref[2].astype(F32)

    return pl.pallas_call(
        body, name=name, out_shape=jax.ShapeDtypeStruct((2, R, L), F32),
        grid_spec=pltpu.PrefetchScalarGridSpec(
            num_scalar_prefetch=1, grid=(R // tr,),
            in_specs=[pl.BlockSpec((None, tr, L), lambda i, ids: (ids[0], i, 0)),
                      pl.BlockSpec((3, tr, L), lambda i, ids: (0, i, 0))],
            out_specs=pl.BlockSpec((None, tr, L), lambda i, ids: (ids[1], i, 0))),
        compiler_params=_params(32, 1),
    )(ids, pair, parts)


def _mesh_ids():
    x, y, c = _mesh_pos()
    return jnp.stack([2 * x + y, c]).astype(jnp.int32)


def _place_shard(rows, ids, *, name):
    R, L = rows.shape
    tr = 256

    def body(ids_ref, in_ref, out_ref):
        out_ref[...] = in_ref[...].astype(BF16)

    return pl.pallas_call(
        body, name=name, out_shape=jax.ShapeDtypeStruct((4, R, L), BF16),
        grid_spec=pltpu.PrefetchScalarGridSpec(
            num_scalar_prefetch=1, grid=(R // tr,), in_specs=[pl.BlockSpec((tr, L), lambda i, ids: (i, 0))],
            out_specs=pl.BlockSpec((None, tr, L), lambda i, ids: (ids[0], i, 0))),
        compiler_params=_params(16, 1),
    )(ids, rows)


def _remote(src, dst, send_sem, recv_sem, to):
    return pltpu.make_async_remote_copy(src_ref=src, dst_ref=dst, send_sem=send_sem, recv_sem=recv_sem,
                                        device_id=to, device_id_type=MESH_IDS)


def _rows(ref, lead, start, size):
    return ref.at[tuple(pl.ds(0, n) for n in ref.shape[:lead]) + (pl.ds(start, size),)]


def _other_chips():
    x, y, _ = _mesh_pos()
    return [(1 - x, y), (x, 1 - y), (1 - x, 1 - y)]


def _plan_gather_ici(bufs):
    n = len(bufs)

    def copies(outs, send, recv):
        x, y, c = _mesh_pos()
        res = []
        for b in range(n):
            half = bufs[b].shape[1] // 2
            mine = _rows(outs[b].at[2 * x + y], 0, c * half, half)
            for j, (cx, cy) in enumerate(_other_chips()):
                res.append((_remote(mine, mine, send(3 * b + j), recv(3 * b + j), (cx, cy, c)),
                            _remote(mine, _rows(outs[b].at[2 * cx + cy], 0, c * half, half),
                                    send(3 * b + j), recv(3 * b + j), (x, y, c))))
        return res

    def start(ins, outs, send, recv, loc):
        for out_cp, _ in copies(outs, send, recv):
            out_cp.start()

    def wait(ins, outs, send, recv, loc):
        for out_cp, in_cp in copies(outs, send, recv):
            in_cp.wait_recv()
            out_cp.wait_send()

    outs = [jax.ShapeDtypeStruct(b.shape, b.dtype) for b in bufs]
    return _Plan(bufs, outs, 3 * n, 0, start, wait, aliases={b: b for b in range(n)})


def _plan_gather_forward(bufs):
    n = len(bufs)

    def copies(outs, send, recv):
        x, y, c = _mesh_pos()
        res = []
        for b in range(n):
            half = bufs[b].shape[1] // 2
            for j, (cx, cy) in enumerate(_other_chips()):
                slot = outs[b].at[2 * cx + cy]
                res.append((_remote(_rows(slot, 0, c * half, half), _rows(slot, 0, c * half, half),
                                    send(3 * b + j), recv(3 * b + j), (x, y, 1 - c)),
                            _remote(_rows(slot, 0, c * half, half), _rows(slot, 0, (1 - c) * half, half),
                                    send(3 * b + j), recv(3 * b + j), (x, y, c))))
        return res

    def start(ins, outs, send, recv, loc):
        for out_cp, _ in copies(outs, send, recv):
            out_cp.start()

    def wait(ins, outs, send, recv, loc):
        for out_cp, in_cp in copies(outs, send, recv):
            in_cp.wait_recv()
            out_cp.wait_send()

    outs = [jax.ShapeDtypeStruct(b.shape, b.dtype) for b in bufs]
    return _Plan(bufs, outs, 3 * n, 0, start, wait, aliases={b: b for b in range(n)})


def _plan_pair_swap(g):
    half = g.shape[1] // 2

    def copy(ins, outs, send, recv, loc):
        x, y, c = _mesh_pos()
        return _remote(_rows(ins[0], 1, (1 - c) * half, half), outs[0], send(0), recv(0), (x, y, 1 - c))

    return _Plan([g], [jax.ShapeDtypeStruct((4, half, g.shape[2]), g.dtype)], 1, 0,
                 lambda *a: copy(*a).start(), lambda *a: copy(*a).wait())


def _plan_pair_gather(buf):
    def copies(ins, outs, send, recv, loc):
        x, y, c = _mesh_pos()
        return (_remote(outs[0].at[c], outs[0].at[c], send(0), recv(0), (x, y, 1 - c)),
                _remote(outs[0].at[c], outs[0].at[1 - c], send(0), recv(0), (x, y, c)))

    def wait(*a):
        out_cp, in_cp = copies(*a)
        in_cp.wait_recv()
        out_cp.wait_send()

    return _Plan([buf], [jax.ShapeDtypeStruct(buf.shape, buf.dtype)], 1, 0, lambda *a: copies(*a)[0].start(), wait,
                 aliases={0: 0})


def _plan_chip_scatter(p):
    def copies(ins, outs, send, recv, loc):
        _, _, c = _mesh_pos()
        return [_remote(ins[0].at[2 * cx + cy], outs[0].at[j], send(j), recv(j), (cx, cy, c))
                for j, (cx, cy) in enumerate(_other_chips())]

    def start(*a):
        for cp in copies(*a):
            cp.start()

    def wait(*a):
        for cp in copies(*a):
            cp.wait()

    return _Plan([p], [jax.ShapeDtypeStruct((3,) + p.shape[1:], p.dtype)], 3, 0, start, wait)


def _plan_exchange_all(vec):
    def copies(ins, outs, send, recv, loc):
        x, y, c = _mesh_pos()
        return [_remote(ins[0], outs[0].at[r - 1], send(r - 1), recv(r - 1), (x ^ (r >> 2), y ^ ((r >> 1) & 1), c ^ (r & 1)))
                for r in range(1, 8)]

    def start(*a):
        for cp in copies(*a):
            cp.start()

    def wait(*a):
        for cp in copies(*a):
            cp.wait()

    return _Plan([vec], [jax.ShapeDtypeStruct((7,) + vec.shape, vec.dtype)], 7, 0, start, wait)


SMALL_LAYOUT = {
    "mla_gq": (0, 1, 256, (1, 256)), "mla_gkv": (1, 1, 256, (1, 256)), "sgu_ln_g": (2, 1, 512, (1, 512)),
    "sgu_ln_b": (3, 1, 512, (1, 512)), "sgu_w": (4, 64, 1024, (64, 1024)), "sgu_b": (68, 1, 512, (1, 512)),
    "hg_lb": (69, 2, 1024, (2, 1024)), "hg_gnorm": (71, 1, 1024, (1, 256)), "ln1_g": (72, 2, 1024, (2, 1024)),
    "ln1_b": (74, 2, 1024, (2, 1024)), "ln2_g": (76, 2, 1024, (2, 1024)), "ln2_b": (78, 2, 1024, (2, 1024)),
}


def _small_pack(dgq, dgkv, dslg, dslb, dsw, dsb, dlb, dgn, ln_parts, sq_err):
    flat_ln = [p for pair in ln_parts for p in pair]

    def body(*refs):
        gq_ref, gkv_ref, slg_ref, slb_ref, sw_ref, sb_ref, lb_ref, gn_ref = refs[:8]
        ln_refs, err_ref, out_ref, t_sc = refs[8:16], refs[16], refs[17], refs[18]
        s8 = lambda ref: jnp.sum(ref[...], axis=0, keepdims=True)
        out_ref[...] = jnp.zeros_like(out_ref)
        out_ref[0:1, 0:256] = s8(gq_ref)
        out_ref[1:2, 0:256] = s8(gkv_ref)
        out_ref[2:3, 0:512] = s8(slg_ref)
        out_ref[3:4, 0:512] = s8(slb_ref)
        out_ref[4:68, :] = sw_ref[...]
        t_sc[...] = sb_ref[...].T
        for g in range(SGU_G):
            out_ref[68:69, g * SGU_C:(g + 1) * SGU_C] = t_sc[g:g + 1, :]
        d_lb1 = s8(lb_ref)
        out_ref[69:70, :] = -d_lb1
        out_ref[70:71, :] = d_lb1
        out_ref[71:72, :] = s8(gn_ref)
        for k, ref in enumerate(ln_refs):
            out_ref[72 + k:73 + k, :] = s8(ref)
        out_ref[0:1, 1023:1024] = jnp.sum(s8(err_ref), axis=1, keepdims=True) * (0.5 / D)

    vm = pl.BlockSpec(memory_space=pltpu.VMEM)
    return pl.pallas_call(
        body, name="small_grad_pack", in_specs=[vm] * 17, out_specs=vm,
        out_shape=jax.ShapeDtypeStruct((SMALL_ROWS, 1024), F32), scratch_shapes=[pltpu.VMEM((SGU_C, SGU_C), F32)],
        compiler_params=_params(16),
    )(dgq, dgkv, dslg, dslb, dsw.reshape(64, 1024), dsb, dlb, dgn, *flat_ln, sq_err)


def _small_update(vec, others, ids, w, m, v):
    names = list(SMALL_LAYOUT)
    n = len(names)
    c1, c2 = 1.0 - B1 ** STEP, 1.0 - B2 ** STEP
    have_others = others is not None

    def body(*refs):
        ids_ref, v_ref = refs[0], refs[1]
        k = 2 + have_others
        w_refs, m_refs, v_refs = refs[k:k + n], refs[k + n:k + 2 * n], refs[k + 2 * n:k + 3 * n]
        outs = refs[k + 3 * n:]
        row0_ref, tot_sc = outs[0], outs[-1]
        total = v_ref[...]
        if have_others:
            me = 2 * ids_ref[0] + ids_ref[1]
            total = None
            for d in range(8):
                rel = d ^ me
                term = jnp.where(rel == 0, v_ref[...], refs[2][jnp.maximum(rel - 1, 0)])
                total = term if total is None else total + term
        tot_sc[...] = total
        row0_ref[...] = tot_sc[0:1, :]
        for i, name in enumerate(names):
            r0, nr, width, _ = SMALL_LAYOUT[name]
            if name == "hg_gnorm":
                g_ = tot_sc[r0:r0 + 1, 0:256]
                for chip in range(1, 4):
                    g_ = jnp.where(ids_ref[0] == chip, tot_sc[r0:r0 + 1, chip * 256:(chip + 1) * 256], g_)
            else:
                g_ = tot_sc[r0:r0 + nr, 0:width]
            m_ = B1 * m_refs[i][...] + (1.0 - B1) * g_
            v_ = B2 * v_refs[i][...] + (1.0 - B2) * (g_ * g_)
            go, do, mo, vo = outs[1 + 4 * i:5 + 4 * i]
            go[...] = g_
            do[...] = -LR * ((m_ / c1) / (jnp.sqrt(v_ / c2) + ADAM_EPS) + WD * w_refs[i][...])
            mo[...] = m_
            vo[...] = v_

    full = lambda shape: pl.BlockSpec(shape, lambda i, ids, nd=len(shape): (0,) * nd)
    kshapes = [SMALL_LAYOUT[name][3] for name in names]
    operands = [vec] + ([others] if have_others else []) + [d[name] for d in (w, m, v) for name in names]
    out_shapes = [jax.ShapeDtypeStruct((1, 1024), F32)] + [jax.ShapeDtypeStruct(s, F32) for s in kshapes for _ in range(4)]
    res = pl.pallas_call(
        body, name="small_update", out_shape=out_shapes,
        grid_spec=pltpu.PrefetchScalarGridSpec(
            num_scalar_prefetch=1, grid=(1,), in_specs=[full(o.shape) for o in operands],
            out_specs=[full(s.shape) for s in out_shapes],
            scratch_shapes=[pltpu.VMEM((SMALL_ROWS, 1024), F32)]),
        compiler_params=_params(32, 1),
    )(ids, *operands)
    return res[0], {name: tuple(res[1 + 4 * i:5 + 4 * i]) for i, name in enumerate(names)}


ROWS_L1, ROWS_L0, ROWS_ODD = 3328, 2048, 768
ODD_PARTS = (("w_out_e", (256, 1024)), ("w_in_e", (1024, 392)), ("w_qb", (256, 192)), ("w_kvb", (256, 256)))


def _odd_rows(parts, dtype, gnorm=None):
    rows = [parts[n].reshape(-1, 1024).astype(dtype) for n, _ in ODD_PARTS]
    used = sum(r.shape[0] for r in rows)
    if gnorm is not None:
        bits = lax.bitcast_convert_type(gnorm.reshape(-1), BF16).reshape(1, 512)
        rows.append(jnp.pad(bits, ((0, 0), (0, 512))))
        used += 1
    rows.append(jnp.zeros((ROWS_ODD - used, 1024), dtype))
    return jnp.concatenate(rows, axis=0)


def _odd_unrows(buf, with_gnorm=False):
    out, off = {}, 0
    for n, shape in ODD_PARTS:
        nr = math.prod(shape) // 1024
        out[n] = buf[off:off + nr].reshape(shape)
        off += nr
    if with_gnorm:
        out["hg_gnorm"] = lax.bitcast_convert_type(buf[off, :512].reshape(256, 2), F32).reshape(1, 256)
    return out


def _rope_tables(positions):
    half = ROPE // 2
    inv_freq = ROPE_BASE ** (-jnp.arange(half, dtype=F32) / half)
    ang = positions.astype(F32).reshape(-1, 1) * inv_freq
    cos, sin = jnp.cos(ang), jnp.sin(ang)
    T = ang.shape[0]
    one, z16, z32 = jnp.ones((T, NOPE), F32), jnp.zeros((T, half), F32), jnp.zeros((T, 32), F32)
    z64 = jnp.zeros((T, NOPE), F32)
    c = jnp.concatenate([one, cos, cos, z32], axis=1)
    s1 = jnp.concatenate([z64, -sin, z16, z32], axis=1)
    s2 = jnp.concatenate([z64, z16, sin, z32], axis=1)
    return c, s1, s2


def _local_step(x, positions, tgt, odd, bufs, P, exchange):
    T = x.shape[0]
    row = lambda a: a.reshape(1, -1)
    rc, rs1, rs2 = _rope_tables(positions)
    blk = lambda f: pl.BlockSpec((None, D, D), f)

    w_in_e = odd["w_in_e"]
    w_in = jnp.concatenate([w_in_e[:, :512], w_in_e[:, 544:1568], w_in_e[:, 512:544], jnp.zeros((D, 96), BF16)], axis=1)
    wq = jnp.pad(odd["w_qb"].reshape(256, HEADS, NOPE + ROPE), ((0, 0), (0, 0), (0, 32))).reshape(256, HEADS * 128)
    kvb = odd["w_kvb"].reshape(256, HEADS, NOPE + VDIM)
    wk = jnp.pad(kvb[:, :, :NOPE], ((0, 0), (0, 0), (0, 64))).reshape(256, HEADS * 128)
    wv = kvb[:, :, NOPE:].reshape(256, HEADS * VDIM)
    w_out_e = odd["w_out_e"]
    sgu_w = P["sgu_w"][0]
    sgu_bt = P["sgu_b"][0].T
    gq, gkv = P["mla_gq"], P["mla_gkv"]
    gnorm = P["hg_gnorm"]

    z0 = _matmul(x, w_in, name="in_proj_e", M=T, N=1664, K=D, tn=1664)[0]
    q, k, v = _mla_prep(z0, gq, gkv, wq, wk, wv, rc, rs1, rs2)
    if exchange:
        ids = _mesh_ids()
        placed = [_place_shard(b, ids, name=f"place_shard_{l}") for l, b in enumerate(bufs)]
        a_out, lse, wga, wgb = _flash_fwd(q, k, v, plan=_plan_gather_ici(placed[:2]))
    else:
        a_out, lse = _flash_fwd(q, k, v)
        wga, wgb, wgc = bufs
    mix0 = _sgu_fwd(z0, a_out, P["sgu_ln_g"], P["sgu_ln_b"], sgu_w, sgu_bt)
    res = _proj_ln(mix0, w_out_e, x, row(P["ln1_g"][0]), row(P["ln1_b"][0]), name="out_proj_ln_e",
                   plan=_plan_gather_forward([wga, wgb]) if exchange else None)
    r1, h1, h1b = res[:3]
    if exchange:
        wga, wgb = res[3:]
    res = _ffn_ln(h1b, wga, h1, row(P["ln2_g"][0]), row(P["ln2_b"][0]), name="ffn_ln_0",
                  plan=_plan_gather_ici(placed[2:]) if exchange else None)
    ra0, r2, h2, h2b = res[:4]
    z4 = _matmul(h2b, wgb, name="in_proj_o", M=T, N=4 * D, K=D, b_spec=blk(lambda i, j, k: (j, 0, 0)),
                 out_shape=jax.ShapeDtypeStruct((4, T, D), F32),
                 o_spec=pl.BlockSpec((None, min(MM_ROWS, T), D), lambda i, j, k: (j, i, 0)))[0]
    y1, o_raw, states = _hgrn_fwd(z4, P["hg_lb"], gnorm)
    res2 = _proj_ln(y1, wgb, h2, row(P["ln1_g"][1]), row(P["ln1_b"][1]), name="out_proj_ln_o", w_rowblk=4,
                    plan=_plan_gather_forward([res[4]]) if exchange else None)
    r3, h3, h3b = res2[:3]
    if exchange:
        wgc = res2[3]
    ra1, r4, h4, _ = _ffn_ln(h3b, wgc, h3, row(P["ln2_g"][1]), row(P["ln2_b"][1]), name="ffn_ln_1")

    ln1_g, ln1_b, ln2_g, ln2_b = [None, None], [None, None], [None, None], [None, None]
    sq_err_parts = []

    def ffn_bwd(l, dh, r_out, ra, h_mid_b, g2, wg, rows, plan=None, tgt=None):
        dr, dr_b, dg, db, *sq_err = _ln_bwd(dh, r_out, row(g2), name=f"ln2_bwd_{l}", tgt=tgt)
        sq_err_parts.extend(sq_err)
        ln2_g[l], ln2_b[l] = dg, db
        da, *extra = _matmul(dr_b, wg, tb=True, mul=ra, out_dtype=BF16, name=f"ffn_da_{l}", M=T, N=4 * D, K=D,
                             b_spec=blk(lambda i, j, k: (j, 1, 0)), plan=plan)
        gbuf = _matmul(ra, dr_b, ta=True, a_sq=True, name=f"ffn_dw2_{l}", M=4 * D, N=D, K=T, tm=1024, tk=DW_TOKENS,
                       out_shape=jax.ShapeDtypeStruct((4, rows, D), BF16), o_spec=blk(lambda i, j, k: (i, 1, 0)))[0]
        gbuf = _matmul(h_mid_b, da, ta=True, name=f"ffn_dw1_{l}", M=D, N=4 * D, K=T, tm=1024, tk=DW_TOKENS, into=gbuf,
                       out_shape=jax.ShapeDtypeStruct((4, rows, D), BF16), o_spec=blk(lambda i, j, k: (j, 0, 0)))[0]
        dh_mid = _matmul(da, wg, tb=True, add=dr, add_scale=ALPHA, name=f"ffn_dh_{l}", M=T, N=D, K=4 * D,
                         b_spec=blk(lambda i, j, k: (k, 0, 0)))[0]
        return dh_mid, gbuf, extra

    dh3, g1, _ = ffn_bwd(1, h4, r4, ra1, h3b, P["ln2_g"][1], wgc, ROWS_L1, tgt=tgt)
    loss_parts = sq_err_parts[0]
    dr3, dr3_b, dg, db = _ln_bwd(dh3, r3, row(P["ln1_g"][1]), name="ln1_bwd_1")
    ln1_g[1], ln1_b[1] = dg, db
    g1_sds = jax.ShapeDtypeStruct((4, ROWS_L1, D), BF16)
    g1 = _matmul(y1, dr3_b, ta=True, name="dw_out_o", M=D, N=D, K=T, tm=256, tk=DW_TOKENS, into=g1, out_shape=g1_sds,
                 o_spec=pl.BlockSpec((None, 256, D), lambda i, j, k: (i, 12, 0)))[0]
    dmix1 = _matmul(dr3_b, wgb, tb=True, name="dmix_o", M=T, N=D, K=D, b_spec=_rows4_spec(4, 3), b_merge=(D, D))[0]
    dz4, dlb, dgn = _hgrn_bwd(z4, o_raw, dmix1, states, P["hg_lb"], gnorm)
    g1 = _matmul(h2b, dz4, ta=True, name="dw_in_o", M=D, N=4 * D, K=T, tm=1024, tk=DW_TOKENS, into=g1, out_shape=g1_sds,
                 b_spec=pl.BlockSpec((None, min(DW_TOKENS, T), D), lambda i, j, k: (j, k, 0)),
                 o_spec=blk(lambda i, j, k: (j, 2, 0)))[0]
    dh2 = _matmul(dz4, wgb, tb=True, add=dr3, add_scale=ALPHA, name="dh_in_o", M=T, N=D, K=4 * D,
                  a_spec=pl.BlockSpec((None, min(MM_ROWS, T), D), lambda i, j, k: (k, i, 0)),
                  b_spec=blk(lambda i, j, k: (k, 0, 0)))[0]

    dh1, g0, swapped1 = ffn_bwd(0, dh2, r2, ra0, h1b, P["ln2_g"][0], wga, ROWS_L0,
                                plan=_plan_pair_swap(g1) if exchange else None)
    dr1, dr1_b, dg, db = _ln_bwd(dh1, r1, row(P["ln1_g"][0]), name="ln1_bwd_0")
    ln1_g[0], ln1_b[0] = dg, db
    godd = {"w_out_e": _matmul(mix0, dr1_b, ta=True, name="dw_out_e", M=D, N=D, K=T, tm=1024, tk=DW_TOKENS)[0]}
    dmix0, *swapped0 = _matmul(dr1_b, w_out_e, tb=True, name="dmix_e", M=T, N=D, K=D,
                               plan=_plan_pair_swap(g0) if exchange else None)
    delta, do_b = _attn_delta(dmix0, a_out)
    if exchange:
        pair1 = _add_pairs(g1, swapped1[0], ids, name="grad_pair_add_1")
        pair0 = _add_pairs(g0, swapped0[0], ids, name="grad_pair_add_0")
        dq4, dk, dv, parts0, parts1 = _flash_bwd(
            q, k, v, do_b, lse, delta, plan=_join_plans([_plan_chip_scatter(pair0), _plan_chip_scatter(pair1)]))
        half0 = _sum_chips(pair0, parts0, ids, name="grad_chip_sum_0")
        half1 = _sum_chips(pair1, parts1, ids, name="grad_chip_sum_1")
        dc, dkr, dwq, dwk, dwv, dgq, dgkv, g0, g1 = _mla_bwd(
            z0, dq4, dk, dv, gq, gkv, wq, wk, wv, rc, rs1, rs2,
            plan=_join_plans([_plan_pair_gather(half0), _plan_pair_gather(half1)]))
        g0, g1 = g0.reshape(ROWS_L0, D), g1.reshape(ROWS_L1, D)
    else:
        dq4, dk, dv = _flash_bwd(q, k, v, do_b, lse, delta)
        dc, dkr, dwq, dwk, dwv, dgq, dgkv = _mla_bwd(z0, dq4, dk, dv, gq, gkv, wq, wk, wv, rc, rs1, rs2)
    dz0, dsw, dsb, dslg, dslb = _sgu_bwd(z0, dmix0, dc, dkr, P["sgu_ln_g"], P["sgu_ln_b"], sgu_w, sgu_bt)
    small_vec = _small_pack(dgq, dgkv, dslg, dslb, dsw, dsb, dlb, dgn, [ln1_g, ln1_b, ln2_g, ln2_b], loss_parts)
    dw_in, *small_others = _matmul(x, dz0, ta=True, name="dw_in_e", M=D, N=1664, K=T, tm=1024, tn=1664,
                                   tk=DW_TOKENS // 2, plan=_plan_exchange_all(small_vec) if exchange else None)
    godd["w_in_e"] = jnp.concatenate([dw_in[:, :512], dw_in[:, 1536:1568], dw_in[:, 512:1536]], axis=1)
    godd["w_qb"] = dwq.reshape(256, HEADS, 128)[:, :, :NOPE + ROPE].reshape(256, HEADS * (NOPE + ROPE))
    godd["w_kvb"] = jnp.concatenate([dwk.reshape(256, HEADS, 128)[:, :, :NOPE], dwv.reshape(256, HEADS, VDIM)],
                                    axis=2).reshape(256, HEADS * (NOPE + VDIM))
    odd_plan = None
    if exchange:
        by_chip = [_odd_rows({"w_out_e": jnp.split(godd["w_out_e"], 4, axis=0)[j],
                              **{n: jnp.split(godd[n], 4, axis=1)[j] for n in ("w_in_e", "w_qb", "w_kvb")}}, BF16)
                   for j in range(4)]
        godd_buf = jnp.stack(by_chip)
        theirs = _run_plan(_plan_pair_swap(godd_buf), name="odd_pair_swap")[0]
        odd_pair = _add_pairs(godd_buf, theirs, ids, name="odd_pair_add")
        odd_plan = _plan_chip_scatter(odd_pair)
    grad_x, *odd_parts = _matmul(dz0, w_in, tb=True, add=dr1, add_scale=ALPHA, name="dx", M=T, N=D, K=1664, tk=1664,
                                 plan=odd_plan)
    if exchange:
        godd = (odd_pair, odd_parts[0])
    return grad_x, g0, g1, godd, small_vec, (small_others[0] if exchange else None)


WEIGHTS = ['w_in_e', 'mla_gq', 'mla_gkv', 'w_qb', 'w_kvb', 'sgu_ln_g', 'sgu_ln_b', 'sgu_w', 'sgu_b', 'w_out_e',
           'w_in_o', 'hg_lb', 'hg_gnorm', 'w_out_o', 'ln1_g', 'ln1_b', 'w_ff1', 'w_ff2', 'ln2_g', 'ln2_b']


def kernel(x, positions, w_in_e, mla_gq, mla_gkv, w_qb, w_kvb, sgu_ln_g, sgu_ln_b, sgu_w, sgu_b, w_out_e, w_in_o, hg_lb, hg_gnorm, w_out_o, ln1_g, ln1_b, w_ff1, w_ff2, ln2_g, ln2_b, loss_target, m_w_in_e, m_mla_gq, m_mla_gkv, m_w_qb, m_w_kvb, m_sgu_ln_g, m_sgu_ln_b, m_sgu_w, m_sgu_b, m_w_out_e, m_w_in_o, m_hg_lb, m_hg_gnorm, m_w_out_o, m_ln1_g, m_ln1_b, m_w_ff1, m_w_ff2, m_ln2_g, m_ln2_b, v_w_in_e, v_mla_gq, v_mla_gkv, v_w_qb, v_w_kvb, v_sgu_ln_g, v_sgu_ln_b, v_sgu_w, v_sgu_b, v_w_out_e, v_w_in_o, v_hg_lb, v_hg_gnorm, v_w_out_o, v_ln1_g, v_ln1_b, v_w_ff1, v_w_ff2, v_ln2_g, v_ln2_b):
    args = dict(locals())
    w = {n: args[n] for n in WEIGHTS}
    m = {n: args["m_" + n] for n in WEIGHTS}
    v = {n: args["v_" + n] for n in WEIGHTS}
    cx, cy, cc = _mesh_pos()
    chip = 2 * cx + cy

    odd_shard = _odd_rows({"w_out_e": w_out_e[0], "w_in_e": w_in_e[0], "w_qb": w_qb[0], "w_kvb": w_kvb[0]}, BF16,
                          gnorm=hg_gnorm)
    ids = _mesh_ids()
    gathered = _run_plan(_plan_gather_ici([_place_shard(odd_shard, ids, name="place_shard_odd")]), name="odd_gather")[0]
    gathered = _run_plan(_plan_gather_forward([gathered]), name="odd_gather_forward")[0]
    per_chip = [_odd_unrows(gathered[j], with_gnorm=True) for j in range(4)]
    odd = {"w_out_e": jnp.concatenate([p["w_out_e"] for p in per_chip], axis=0)}
    for n in ("w_in_e", "w_qb", "w_kvb"):
        odd[n] = jnp.concatenate([p[n] for p in per_chip], axis=1)
    small = {n: w[n] for n in SMALL_LAYOUT if n != "hg_gnorm"}
    small["hg_gnorm"] = jnp.concatenate([p["hg_gnorm"] for p in per_chip], axis=1)
    shard_rows = (jnp.concatenate([w_ff1[0], w_ff2[0]], axis=0).astype(BF16),
                  jnp.concatenate([w_in_o[0], w_out_o[0]], axis=0).astype(BF16),
                  jnp.concatenate([w_ff1[1], w_ff2[1]], axis=0).astype(BF16))

    grad_x, g_l0, g_l1, godd, small_vec, small_others = _local_step(
        x[0], positions[0], loss_target[0], odd, shard_rows, small, True)

    pair, parts = godd
    g_odd = _run_plan(_plan_pair_gather(_sum_chips(pair, parts, ids, name="odd_chip_sum")), name="odd_pair_gather")[0]
    g_odd = _odd_unrows(g_odd.reshape(ROWS_ODD, 1024))

    to_kernel = lambda d: {n: d[n].reshape(SMALL_LAYOUT[n][3]) for n in SMALL_LAYOUT}
    first_row, small_out = _small_update(small_vec, small_others, ids, to_kernel(w), to_kernel(m), to_kernel(v))
    loss = first_row[0, 1023]
    grads, delta, new_m, new_v = {}, {}, {}, {}
    for n, res in small_out.items():
        grads[n], delta[n], new_m[n], new_v[n] = (r.reshape(w[n].shape) for r in res)

    for n, bufs_, row0 in (("w_ff1", [g_l0, g_l1], 0), ("w_ff2", [g_l0, g_l1], 1024), ("w_in_o", [g_l1], 2048),
                           ("w_out_o", [g_l1], 3072)):
        grads[n], delta[n], new_m[n], new_v[n] = _adamw_rows(w[n], m[n], v[n], bufs_, row0, name=f"adamw_{n}")
    for n, _ in ODD_PARTS:
        grads[n] = g_odd[n][None]
        d_, m_, v_ = _adamw(w[n][0], g_odd[n], m[n][0], v[n][0], name=f"adamw_{n}")
        delta[n], new_m[n], new_v[n] = d_[None], m_[None], v_[None]

    return (loss, grad_x[None], *[grads[n] for n in WEIGHTS], *[delta[n] for n in WEIGHTS],
            *[new_m[n] for n in WEIGHTS], *[new_v[n] for n in WEIGHTS])
```

```python
import math

import jax
import jax.numpy as jnp
from jax import lax
from jax.experimental import pallas as pl
from jax.experimental.pallas import tpu as pltpu

F32 = jnp.float32
BF16 = jnp.bfloat16
MESH_IDS = pl.DeviceIdType.MESH

D = 1024
DEPTH = 2
HEADS = 8
NOPE, ROPE, VDIM = 64, 32, 64
QK_SCALE = (NOPE + ROPE) ** -0.5
ROPE_BASE = 10000.0
SGU_G, SGU_C = 4, 128
HG_CHUNK = 64
HG_HEADS_PER_STEP = 8
ALPHA = (2 * DEPTH) ** 0.25
EPS = 1e-5
LR, B1, B2, ADAM_EPS, WD, STEP = 0.001, 0.9, 0.999, 1e-08, 0.01, 10
GELU_C = math.sqrt(2.0 / math.pi)
GELU_A = 0.044715
MB = 1024 * 1024
ROW_BLOCK = 512
SMALL_ROWS = 80

NT_DIMS = (((1,), (1,)), ((), ()))
TN_DIMS = (((0,), (0,)), ((), ()))


def _params(vmem_mb, n_axes=0):
    kw = dict(vmem_limit_bytes=vmem_mb * MB)
    if n_axes:
        kw["dimension_semantics"] = ("arbitrary",) * n_axes
    return pltpu.CompilerParams(**kw)


_ANY = pl.BlockSpec(memory_space=pltpu.HBM)


def _mesh_pos():
    return lax.axis_index("x"), lax.axis_index("y"), lax.axis_index("c")


def _hbm(*arrays):
    return tuple(pltpu.with_memory_space_constraint(a, pltpu.HBM) if a.size >= 2 ** 18 else a for a in arrays)


class _Plan:
    def __init__(self, ins, outs, n_remote, n_local, start, wait, aliases=None):
        self.ins, self.outs, self.n_remote, self.n_local = list(ins), list(outs), n_remote, n_local
        self.start, self.wait, self.aliases = start, wait, dict(aliases or {})


def _join_plans(plans):
    ins, outs, aliases, parts = [], [], {}, []
    nr = nl = 0
    for p in plans:
        parts.append((p, len(ins), len(outs), nr, nl))
        aliases.update({len(ins) + i: len(outs) + o for i, o in p.aliases.items()})
        ins += p.ins
        outs += p.outs
        nr += p.n_remote
        nl += p.n_local

    def run(which):
        def go(in_refs, out_refs, send, recv, loc):
            for p, i0, o0, r0, l0 in parts:
                getattr(p, which)(in_refs[i0:i0 + len(p.ins)], out_refs[o0:o0 + len(p.outs)],
                                  lambda i, r0=r0: send(r0 + i), lambda i, r0=r0: recv(r0 + i),
                                  lambda i, l0=l0: loc(l0 + i))
        return go

    return _Plan(ins, outs, nr, nl, run("start"), run("wait"), aliases)


def _plan_io(plan, n_in, n_out):
    if plan is None:
        return [], [], [], [], {}
    sems = [pltpu.SemaphoreType.DMA((max(plan.n_remote, 1),)), pltpu.SemaphoreType.DMA((max(plan.n_remote, 1),)),
            pltpu.SemaphoreType.DMA((max(plan.n_local, 1),))]
    aliases = {n_in + i: n_out + o for i, o in plan.aliases.items()}
    return plan.ins, [_ANY] * len(plan.outs), plan.outs, sems, aliases


def _split_refs(refs, n_in, n_out, n_scr, plan):
    p_in, p_out = (len(plan.ins), len(plan.outs)) if plan is not None else (0, 0)
    refs = list(refs)
    ins, refs = refs[:n_in], refs[n_in:]
    pins, refs = refs[:p_in], refs[p_in:]
    outs, refs = refs[:n_out], refs[n_out:]
    pouts, refs = refs[:p_out], refs[p_out:]
    scr, psem = refs[:n_scr], refs[n_scr:]
    psem = tuple((lambda i, s=s: s.at[i]) for s in psem)
    return ins, outs, scr, (pins, pouts, psem)


def _grid_edge(grid, last):
    cond = None
    for ax, n in enumerate(grid):
        c = pl.program_id(ax) == (n - 1 if last else 0)
        cond = c if cond is None else cond & c
    return cond


def _plan_start(plan, pctx, grid):
    if plan is not None:
        pins, pouts, psem = pctx
        pl.when(_grid_edge(grid, False))(lambda: plan.start(pins, pouts, *psem))


def _plan_wait(plan, pctx, grid):
    if plan is not None:
        pins, pouts, psem = pctx
        pl.when(_grid_edge(grid, True))(lambda: plan.wait(pins, pouts, *psem))


def _run_plan(plan, *, name):
    def body(*refs):
        _, _, _, (pins, pouts, psem) = _split_refs(refs, 0, 0, 0, plan)
        plan.start(pins, pouts, *psem)
        plan.wait(pins, pouts, *psem)

    p_in, p_ospec, p_oshape, p_scr, p_alias = _plan_io(plan, 0, 0)
    return pl.pallas_call(body, name=name, in_specs=[_ANY] * len(p_in), out_specs=p_ospec, out_shape=p_oshape,
                          scratch_shapes=p_scr, input_output_aliases=p_alias)(*p_in)


def _fold8(x):
    return x.reshape(x.shape[0] // 8, 8, x.shape[1]).sum(axis=0)


def _ln_stats(r):
    mu = jnp.mean(r, -1, keepdims=True)
    xc = r - mu
    rstd = lax.rsqrt(jnp.mean(xc * xc, -1, keepdims=True) + EPS)
    return xc * rstd, rstd


def _sigmoid(x):
    return jax.nn.sigmoid(x)


def _gelu(x):
    return 0.5 * x * (1.0 + jnp.tanh(GELU_C * (x + GELU_A * x * x * x)))


def _gelu_grad(x):
    t = jnp.tanh(GELU_C * (x + GELU_A * x * x * x))
    return 0.5 * (1.0 + t) + 0.5 * x * (1.0 - t * t) * GELU_C * (1.0 + 3.0 * GELU_A * x * x)


MM_ROWS = 1024
DW_TOKENS = 2048


def _matmul(a, b, *, name, M, N, K, ta=False, tb=False, out_dtype=F32, tm=MM_ROWS, tn=1024, tk=1024,
            a_spec=None, b_spec=None, b_merge=None, out_shape=None, o_spec=None, into=None,
            a_sq=False, mul=None, add=None, add_scale=1.0, plan=None):
    tm, tn, tk = min(tm, M), min(tn, N), min(tk, K)
    assert M % tm == 0 and N % tn == 0 and K % tk == 0
    grid = (M // tm, N // tn, K // tk)
    nk = grid[2]
    if a_spec is None:
        a_spec = pl.BlockSpec((tk, tm), lambda i, j, k: (k, i)) if ta else pl.BlockSpec((tm, tk), lambda i, j, k: (i, k))
    if b_spec is None:
        b_spec = pl.BlockSpec((tn, tk), lambda i, j, k: (j, k)) if tb else pl.BlockSpec((tk, tn), lambda i, j, k: (k, j))
    if o_spec is None:
        o_spec = pl.BlockSpec((tm, tn), lambda i, j, k: (i, j))
        out_shape = jax.ShapeDtypeStruct((M, N), out_dtype)
    e_spec = pl.BlockSpec((tm, tn), lambda i, j, k: (i, j))
    dims = (((0 if ta else 1,), (1 if tb else 0,)), ((), ()))
    extra = [e for e in (mul, add, into) if e is not None]
    n_in = 2 + len(extra)

    def body(*refs):
        ins, outs, scr, pctx = _split_refs(refs, n_in, 1, 1 if nk > 1 else 0, plan)
        a_ref, b_ref = ins[0], ins[1]
        rest = list(ins[2:])
        mul_ref = rest.pop(0) if mul is not None else None
        add_ref = rest.pop(0) if add is not None else None
        o_ref = outs[0]
        _plan_start(plan, pctx, grid)
        av = a_ref[...].astype(BF16)
        if a_sq:
            av = av * av
        bv = b_ref[...]
        if b_merge is not None:
            bv = bv.reshape(b_merge)
        p = lax.dot_general(av, bv, dims, preferred_element_type=F32)

        def finish(r):
            if mul_ref is not None:
                r = r * (2.0 * mul_ref[...].astype(F32))
            if add_ref is not None:
                r = r + add_scale * add_ref[...]
            o_ref[...] = r.astype(o_ref.dtype)

        if nk == 1:
            finish(p)
        else:
            acc_ref = scr[0]
            k = pl.program_id(2)

            @pl.when(k == 0)
            def _():
                acc_ref[...] = p

            @pl.when(k > 0)
            def _():
                acc_ref[...] += p

            @pl.when(k == nk - 1)
            def _():
                finish(acc_ref[...])

        _plan_wait(plan, pctx, grid)

    p_in, p_ospec, p_oshape, p_scr, p_alias = _plan_io(plan, n_in, 1)
    aliases = dict(p_alias)
    if into is not None:
        aliases[n_in - 1] = 0
    return pl.pallas_call(
        body, name=name, grid=grid,
        in_specs=[a_spec, b_spec] + [e_spec] * (len(extra) - (into is not None)) + [_ANY] * (into is not None)
        + [_ANY] * len(p_in),
        out_specs=[o_spec] + p_ospec, out_shape=[out_shape] + p_oshape,
        scratch_shapes=([pltpu.VMEM((tm, tn), F32)] if nk > 1 else []) + p_scr,
        input_output_aliases=aliases, compiler_params=_params(48, 3),
    )(*_hbm(a, b, *extra), *p_in)


def _rows4_spec(rowblk, n_axes):
    return pl.BlockSpec((4, 256, D), lambda *_: (0, rowblk, 0))


def _proj_ln(a_b, w, h_prev, g, b, *, name, w_rowblk=None, plan=None):
    T = a_b.shape[0]
    tm = min(ROW_BLOCK, T)
    grid = (T // tm,)
    row = pl.BlockSpec((tm, D), lambda i: (i, 0))
    vec = pl.BlockSpec((1, D), lambda i: (0, 0))
    w_spec = pl.BlockSpec((D, D), lambda i: (0, 0)) if w_rowblk is None else _rows4_spec(w_rowblk, 1)

    def body(*refs):
        (a_ref, w_ref, h_ref, g_ref, b_ref), (r_ref, ho_ref, hb_ref), _, pctx = _split_refs(refs, 5, 3, 0, plan)
        _plan_start(plan, pctx, grid)
        mix = jnp.dot(a_ref[...], w_ref[...].reshape(D, D), preferred_element_type=F32)
        r = ALPHA * h_ref[...] + mix
        xhat, _ = _ln_stats(r)
        y = xhat * g_ref[...] + b_ref[...]
        r_ref[...] = r
        ho_ref[...] = y
        hb_ref[...] = y.astype(BF16)
        _plan_wait(plan, pctx, grid)

    p_in, p_ospec, p_oshape, p_scr, p_alias = _plan_io(plan, 5, 3)
    return pl.pallas_call(
        body, name=name, grid=grid,
        in_specs=[row, w_spec, row, vec, vec] + [_ANY] * len(p_in),
        out_specs=[row, row, row] + p_ospec,
        out_shape=[jax.ShapeDtypeStruct((T, D), F32), jax.ShapeDtypeStruct((T, D), F32),
                   jax.ShapeDtypeStruct((T, D), BF16)] + p_oshape,
        scratch_shapes=p_scr, input_output_aliases=p_alias, compiler_params=_params(40, 1),
    )(*_hbm(a_b, w, h_prev, g, b), *p_in)


def _ffn_ln(h_b, wbuf, h, g, b, *, name, plan=None):
    T = h_b.shape[0]
    tm, tf = min(ROW_BLOCK, T), 1024
    nf = 4
    F = nf * tf
    grid = (T // tm, nf)
    row = pl.BlockSpec((tm, D), lambda i, j: (i, 0))
    vec = pl.BlockSpec((1, D), lambda i, j: (0, 0))

    def body(*refs):
        ((hb_ref, w1_ref, w2_ref, h_ref, g_ref, b_ref), (ra_ref, r_ref, ho_ref, hbo_ref), (acc_ref,),
         pctx) = _split_refs(refs, 6, 4, 1, plan)
        _plan_start(plan, pctx, grid)
        j = pl.program_id(1)
        a = jnp.dot(hb_ref[...], w1_ref[...], preferred_element_type=F32)
        ra = jnp.maximum(a, 0.0)
        ra_ref[...] = ra.astype(BF16)
        p = jnp.dot((ra * ra).astype(BF16), w2_ref[...], preferred_element_type=F32)

        @pl.when(j == 0)
        def _():
            acc_ref[...] = p

        @pl.when(j > 0)
        def _():
            acc_ref[...] += p

        @pl.when(j == nf - 1)
        def _():
            r = ALPHA * h_ref[...] + acc_ref[...]
            xhat, _ = _ln_stats(r)
            y = xhat * g_ref[...] + b_ref[...]
            r_ref[...] = r
            ho_ref[...] = y
            hbo_ref[...] = y.astype(BF16)

        _plan_wait(plan, pctx, grid)

    p_in, p_ospec, p_oshape, p_scr, p_alias = _plan_io(plan, 6, 4)
    return pl.pallas_call(
        body, name=name, grid=grid,
        in_specs=[row, pl.BlockSpec((None, D, tf), lambda i, j: (j, 0, 0)),
                  pl.BlockSpec((None, tf, D), lambda i, j: (j, 1, 0)), row, vec, vec] + [_ANY] * len(p_in),
        out_specs=[pl.BlockSpec((tm, tf), lambda i, j: (i, j)), row, row, row] + p_ospec,
        out_shape=[jax.ShapeDtypeStruct((T, F), BF16), jax.ShapeDtypeStruct((T, D), F32),
                   jax.ShapeDtypeStruct((T, D), F32), jax.ShapeDtypeStruct((T, D), BF16)] + p_oshape,
        scratch_shapes=[pltpu.VMEM((tm, D), F32)] + p_scr,
        input_output_aliases=p_alias, compiler_params=_params(48, 2),
    )(*_hbm(h_b, wbuf, wbuf, h, g, b), *p_in)


def _ln_bwd(dy, r, g, *, name, tgt=None):
    T = dy.shape[0]
    tm = min(ROW_BLOCK, T)
    row = pl.BlockSpec((tm, D), lambda i: (i, 0))
    acc = pl.BlockSpec((8, D), lambda i: (0, 0))
    n_in = 3 + (tgt is not None)

    def body(*refs):
        dy_ref, r_ref, g_ref = refs[:3]
        dr_ref, drb_ref, dg_ref, db_ref = refs[n_in:n_in + 4]

        @pl.when(pl.program_id(0) == 0)
        def _():
            for ref in refs[n_in + 2:]:
                ref[...] = jnp.zeros_like(ref)

        dy_ = dy_ref[...]
        if tgt is not None:
            err = dy_ - refs[3][...]
            refs[n_in + 4][...] += _fold8(err * err)
            dy_ = err * (1.0 / D)
        xhat, rstd = _ln_stats(r_ref[...])
        dxh = dy_ * g_ref[...]
        m1 = jnp.mean(dxh, -1, keepdims=True)
        m2 = jnp.mean(dxh * xhat, -1, keepdims=True)
        dr = rstd * (dxh - m1 - xhat * m2)
        dr_ref[...] = dr
        drb_ref[...] = dr.astype(BF16)
        dg_ref[...] += _fold8(dy_ * xhat)
        db_ref[...] += _fold8(dy_)

    extra = [] if tgt is None else [tgt]
    return pl.pallas_call(
        body, name=name, grid=(T // tm,),
        in_specs=[row, row, pl.BlockSpec((1, D), lambda i: (0, 0))] + [row] * len(extra),
        out_specs=[row, row, acc, acc] + [acc] * len(extra),
        out_shape=[jax.ShapeDtypeStruct((T, D), F32), jax.ShapeDtypeStruct((T, D), BF16)]
        + [jax.ShapeDtypeStruct((8, D), F32)] * (2 + len(extra)),
        compiler_params=_params(40, 1),
    )(*_hbm(dy, r, g, *extra))


def _rope(x, c, s1, s2):
    return x * c + pltpu.roll(x, 112, 1) * s1 + pltpu.roll(x, 16, 1) * s2


def _rope_t(dy, c, s1, s2):
    return dy * c + pltpu.roll(dy * s1, 16, 1) + pltpu.roll(dy * s2, 112, 1)


def _rms(x, g):
    rstd = lax.rsqrt(jnp.mean(x * x, -1, keepdims=True) + EPS)
    xhat = x * rstd
    return xhat * g, xhat, rstd


def _mla_prep(z0, gq, gkv, wq, wk, wv, rc, rs1, rs2):
    T = z0.shape[0]
    tm = min(ROW_BLOCK, T)
    HW = HEADS * 128

    def body(cq_ref, ckv_ref, kr_ref, gq_ref, gkv_ref, wq_ref, wk_ref, wv_ref, c_ref, s1_ref, s2_ref,
             q_ref, k_ref, v_ref):
        nq = _rms(cq_ref[...], gq_ref[...])[0].astype(BF16)
        nkv = _rms(ckv_ref[...], gkv_ref[...])[0].astype(BF16)
        q = jnp.dot(nq, wq_ref[...], preferred_element_type=F32)
        k = jnp.dot(nkv, wk_ref[...], preferred_element_type=F32)
        v = jnp.dot(nkv, wv_ref[...], preferred_element_type=F32)
        c, s1, s2 = c_ref[...], s1_ref[...], s2_ref[...]
        kr = _rope(pltpu.roll(kr_ref[...], 64, 1), c, s1, s2)
        for h in range(HEADS):
            sl = slice(h * 128, (h + 1) * 128)
            q_ref[:, sl] = (_rope(q[:, sl], c, s1, s2) * QK_SCALE).astype(BF16)
            k_ref[:, sl] = (k[:, sl] + kr).astype(BF16)
        v_ref[...] = v.astype(BF16)

    full = lambda shape: pl.BlockSpec(shape, lambda i: (0, 0))
    tab = pl.BlockSpec((tm, 128), lambda i: (i, 0))
    return pl.pallas_call(
        body, name="mla_prep", grid=(T // tm,),
        in_specs=[pl.BlockSpec((tm, 256), lambda i: (i, 0)), pl.BlockSpec((tm, 256), lambda i: (i, 1)),
                  pl.BlockSpec((tm, 128), lambda i: (i, 12)), full((1, 256)), full((1, 256)),
                  full((256, HW)), full((256, HW)), full((256, 512)), tab, tab, tab],
        out_specs=[pl.BlockSpec((tm, HW), lambda i: (i, 0)), pl.BlockSpec((tm, HW), lambda i: (i, 0)),
                   pl.BlockSpec((tm, 512), lambda i: (i, 0))],
        out_shape=[jax.ShapeDtypeStruct((T, HW), BF16), jax.ShapeDtypeStruct((T, HW), BF16),
                   jax.ShapeDtypeStruct((T, 512), BF16)],
        compiler_params=_params(40, 1),
    )(z0, z0, z0, gq, gkv, wq, wk, wv, rc, rs1, rs2)


def _flash_fwd(q, k, v, plan=None):
    T = q.shape[0]
    bq = min(2 * ROW_BLOCK, T)
    nq = T // bq
    grid = (4, nq, nq)

    def body(*refs):
        (q_ref, k_ref, v_ref), (o_ref, lse_ref), (m_sc, acc_sc), pctx = _split_refs(refs, 3, 2, 2, plan)
        _plan_start(plan, pctx, grid)
        i, j = pl.program_id(1), pl.program_id(2)
        first = lax.broadcasted_iota(jnp.int32, (bq, 128), 1) < 64

        @pl.when(j == 0)
        def _():
            m_sc[...] = jnp.full_like(m_sc, -jnp.inf)
            acc_sc[...] = jnp.zeros_like(acc_sc)

        def tile(r0, nr, nc, masked):
            rs = slice(r0, r0 + nr)
            vp = v_ref[0:nc, :]
            lanes = first[0:nc, :]
            for h in range(2):
                sl = slice(h * 128, (h + 1) * 128)
                s = lax.dot_general(q_ref[rs, sl], k_ref[0:nc, sl], NT_DIMS, preferred_element_type=F32)
                if masked:
                    rows = r0 + lax.broadcasted_iota(jnp.int32, (nr, nc), 0)
                    cols = lax.broadcasted_iota(jnp.int32, (nr, nc), 1)
                    s = jnp.where(cols <= rows, s, -jnp.inf)
                m_prev = m_sc[h, rs, 0:1]
                m_new = jnp.maximum(m_prev, jnp.max(s, axis=1, keepdims=True))
                alpha = jnp.exp(m_prev - m_new)
                p = jnp.exp(s - m_new).astype(BF16)
                vh = jnp.where(lanes if h == 0 else jnp.logical_not(lanes), vp, jnp.ones_like(vp))
                acc_sc[h, rs, :] = acc_sc[h, rs, :] * alpha + jnp.dot(p, vh, preferred_element_type=F32)
                m_sc[h, rs, :] = jnp.broadcast_to(m_new, (nr, 128))

        @pl.when(j < i)
        def _():
            tile(0, bq, bq, False)

        @pl.when(j == i)
        def _():
            tile(0, bq, bq, True)
            a0, a1 = acc_sc[0], acc_sc[1]
            l0, l1 = pltpu.roll(a0, 64, 1), pltpu.roll(a1, 64, 1)
            o_ref[...] = jnp.where(first, a0 / l0, a1 / l1).astype(BF16)
            lse_ref[...] = jnp.where(first, m_sc[0] + jnp.log(l0), m_sc[1] + jnp.log(l1))

        _plan_wait(plan, pctx, grid)

    kv = lambda hp, i, j: (jnp.minimum(i, j), hp)
    p_in, p_ospec, p_oshape, p_scr, p_alias = _plan_io(plan, 3, 2)
    return pl.pallas_call(
        body, name="flash_fwd", grid=grid,
        in_specs=[pl.BlockSpec((bq, 256), lambda hp, i, j: (i, hp)), pl.BlockSpec((bq, 256), kv),
                  pl.BlockSpec((bq, 128), kv)] + [_ANY] * len(p_in),
        out_specs=[pl.BlockSpec((bq, 128), lambda hp, i, j: (i, hp)),
                   pl.BlockSpec((bq, 128), lambda hp, i, j: (i, hp))] + p_ospec,
        out_shape=[jax.ShapeDtypeStruct((T, 512), BF16), jax.ShapeDtypeStruct((T, 512), F32)] + p_oshape,
        scratch_shapes=[pltpu.VMEM((2, bq, 128), F32), pltpu.VMEM((2, bq, 128), F32)] + p_scr,
        input_output_aliases=p_alias, compiler_params=_params(56, 3),
    )(*_hbm(q, k, v), *p_in)


def _attn_delta(dmix, o):
    T = o.shape[0]
    tm = min(ROW_BLOCK, T)
    blk = pl.BlockSpec((tm, 512), lambda i: (i, 0))

    def body(do_ref, o_ref, delta_ref, dob_ref):
        first = lax.broadcasted_iota(jnp.int32, (tm, 128), 1) < 64
        for hp in range(4):
            sl = slice(hp * 128, (hp + 1) * 128)
            prod = do_ref[:, sl] * o_ref[:, sl].astype(F32)
            d0 = jnp.sum(jnp.where(first, prod, 0.0), axis=1, keepdims=True)
            d1 = jnp.sum(jnp.where(first, 0.0, prod), axis=1, keepdims=True)
            delta_ref[:, sl] = jnp.where(first, d0, d1)
        dob_ref[...] = do_ref[...].astype(BF16)

    return pl.pallas_call(
        body, name="attn_delta", grid=(T // tm,), in_specs=[blk, blk], out_specs=[blk, blk],
        out_shape=[jax.ShapeDtypeStruct((T, 512), F32), jax.ShapeDtypeStruct((T, 512), BF16)],
        compiler_params=_params(32, 1),
    )(dmix, o)


def _flash_bwd(q, k, v, do_b, lse, delta, plan=None):
    T = q.shape[0]
    bq = min(2 * ROW_BLOCK, T)
    nq = T // bq
    grid = (4, nq, nq)

    def body(*refs):
        ((q_ref, k_ref, v_ref, do_ref, lse_ref, dl_ref), (dq_hbm, dk_ref, dv_ref), (dq_sc, dk_sc, dv_sc, sem),
         pctx) = _split_refs(refs, 6, 3, 4, plan)
        _plan_start(plan, pctx, grid)
        hp, j, i = pl.program_id(0), pl.program_id(1), pl.program_id(2)
        first = lax.broadcasted_iota(jnp.int32, (bq, 128), 1) < 64

        @pl.when((j == 0) & (i == 0))
        def _():
            dq_sc[...] = jnp.zeros_like(dq_sc)

        @pl.when(i == j)
        def _():
            dk_sc[...] = jnp.zeros_like(dk_sc)
            dv_sc[...] = jnp.zeros_like(dv_sc)

        def tile(r0, nr, nc, masked):
            rs, cs = slice(r0, r0 + nr), slice(0, nc)
            vp = v_ref[cs, :]
            do = do_ref[rs, :]
            lanes = first[rs, :]
            for h in range(2):
                sl = slice(h * 128, (h + 1) * 128)
                qh, kh = q_ref[rs, sl], k_ref[cs, sl]
                s = lax.dot_general(qh, kh, NT_DIMS, preferred_element_type=F32)
                p = jnp.exp(s - lse_ref[rs, h * 64:h * 64 + 1])
                if masked:
                    rows = r0 + lax.broadcasted_iota(jnp.int32, (nr, nc), 0)
                    cols = lax.broadcasted_iota(jnp.int32, (nr, nc), 1)
                    p = jnp.where(cols <= rows, p, 0.0)
                do_h = jnp.where(lanes if h == 0 else jnp.logical_not(lanes), do, jnp.zeros_like(do))
                dv_sc[cs, :] += lax.dot_general(p.astype(BF16), do_h, TN_DIMS, preferred_element_type=F32)
                dp = lax.dot_general(do_h, vp, NT_DIMS, preferred_element_type=F32)
                ds = (p * (dp - dl_ref[rs, h * 64:h * 64 + 1])).astype(BF16)
                dq_sc[i, rs, sl] += jnp.dot(ds, kh, preferred_element_type=F32)
                dk_sc[cs, sl] += lax.dot_general(ds, qh, TN_DIMS, preferred_element_type=F32)

        @pl.when(i > j)
        def _():
            tile(0, bq, bq, False)

        @pl.when(i == j)
        def _():
            tile(0, bq // 2, bq // 2, True)
            tile(bq // 2, bq // 2, bq, True)

        @pl.when(i == nq - 1)
        def _():
            dk_ref[...] = dk_sc[...]
            dv_ref[...] = dv_sc[...]

        @pl.when((j == nq - 1) & (i == nq - 1))
        def _():
            cp = pltpu.make_async_copy(dq_sc, dq_hbm.at[hp], sem)
            cp.start()
            cp.wait()

        _plan_wait(plan, pctx, grid)

    qi = lambda hp, j, i: (jnp.maximum(i, j), hp)
    kj = lambda hp, j, i: (j, hp)
    p_in, p_ospec, p_oshape, p_scr, p_alias = _plan_io(plan, 6, 3)
    return pl.pallas_call(
        body, name="flash_bwd", grid=grid,
        in_specs=[pl.BlockSpec((bq, 256), qi), pl.BlockSpec((bq, 256), kj), pl.BlockSpec((bq, 128), kj),
                  pl.BlockSpec((bq, 128), qi), pl.BlockSpec((bq, 128), qi), pl.BlockSpec((bq, 128), qi)]
        + [_ANY] * len(p_in),
        out_specs=[_ANY, pl.BlockSpec((bq, 256), kj), pl.BlockSpec((bq, 128), kj)] + p_ospec,
        out_shape=[jax.ShapeDtypeStruct((4, nq, bq, 256), F32), jax.ShapeDtypeStruct((T, 1024), F32),
                   jax.ShapeDtypeStruct((T, 512), F32)] + p_oshape,
        scratch_shapes=[pltpu.VMEM((nq, bq, 256), F32), pltpu.VMEM((bq, 256), F32), pltpu.VMEM((bq, 128), F32),
                        pltpu.SemaphoreType.DMA] + p_scr,
        input_output_aliases=p_alias, compiler_params=_params(56, 3),
    )(*_hbm(q, k, v, do_b, lse, delta), *p_in)


def _mla_bwd(z0, dq4, dk, dv, gq, gkv, wq, wk, wv, rc, rs1, rs2, plan=None):
    T = z0.shape[0]
    tm = min(ROW_BLOCK, T)
    HW = HEADS * 128
    grid = (T // tm,)
    dq4 = dq4.reshape(4, T, 256)

    def body(*refs):
        ((cq_ref, ckv_ref, dq_ref, dk_ref, dv_ref, gq_ref, gkv_ref, wq_ref, wk_ref, wv_ref, c_ref, s1_ref, s2_ref),
         (dc_ref, dkr_ref, dwq_ref, dwk_ref, dwv_ref, dgq_ref, dgkv_ref), _, pctx) = _split_refs(refs, 13, 7, 0, plan)
        _plan_start(plan, pctx, grid)

        @pl.when(pl.program_id(0) == 0)
        def _():
            for ref in (dwq_ref, dwk_ref, dwv_ref, dgq_ref, dgkv_ref):
                ref[...] = jnp.zeros_like(ref)

        c, s1, s2 = c_ref[...], s1_ref[...], s2_ref[...]
        lane = lax.broadcasted_iota(jnp.int32, (tm, 128), 1)
        nq, xq, rq = _rms(cq_ref[...], gq_ref[...])
        nkv, xkv, rkv = _rms(ckv_ref[...], gkv_ref[...])
        nq_b, nkv_b = nq.astype(BF16), nkv.astype(BF16)

        dq_parts, dk_parts = [], []
        dkr = jnp.zeros((tm, 128), F32)
        for h in range(HEADS):
            blk = dq_ref[h // 2, :, (h % 2) * 128:(h % 2 + 1) * 128] * QK_SCALE
            dq_parts.append(_rope_t(blk, c, s1, s2).astype(BF16))
            kb = dk_ref[:, h * 128:(h + 1) * 128]
            dk_parts.append(jnp.where(lane < NOPE, kb, 0.0).astype(BF16))
            dkr = dkr + kb
        dq_b = jnp.concatenate(dq_parts, axis=1)
        dk_b = jnp.concatenate(dk_parts, axis=1)
        dv_b = dv_ref[...].astype(BF16)

        dwq_ref[...] += lax.dot_general(nq_b, dq_b, TN_DIMS, preferred_element_type=F32)
        dwk_ref[...] += lax.dot_general(nkv_b, dk_b, TN_DIMS, preferred_element_type=F32)
        dwv_ref[...] += lax.dot_general(nkv_b, dv_b, TN_DIMS, preferred_element_type=F32)
        dnq = lax.dot_general(dq_b, wq_ref[...], NT_DIMS, preferred_element_type=F32)
        dnkv = (lax.dot_general(dk_b, wk_ref[...], NT_DIMS, preferred_element_type=F32)
                + lax.dot_general(dv_b, wv_ref[...], NT_DIMS, preferred_element_type=F32))

        def rms_bwd(dn, xhat, rstd, g):
            dxh = dn * g
            return rstd * (dxh - xhat * jnp.mean(dxh * xhat, -1, keepdims=True))

        dc_ref[:, :256] = rms_bwd(dnq, xq, rq, gq_ref[...]).astype(BF16)
        dc_ref[:, 256:] = rms_bwd(dnkv, xkv, rkv, gkv_ref[...]).astype(BF16)
        dgq_ref[...] += _fold8(dnq * xq)
        dgkv_ref[...] += _fold8(dnkv * xkv)
        dkr = pltpu.roll(_rope_t(dkr, c, s1, s2), 64, 1)
        dkr_ref[...] = jnp.where(lane < ROPE, dkr, 0.0).astype(BF16)
        _plan_wait(plan, pctx, grid)

    full = lambda shape: pl.BlockSpec(shape, lambda i: (0,) * len(shape))
    tab = pl.BlockSpec((tm, 128), lambda i: (i, 0))
    p_in, p_ospec, p_oshape, p_scr, p_alias = _plan_io(plan, 13, 7)
    return pl.pallas_call(
        body, name="mla_bwd", grid=grid,
        in_specs=[pl.BlockSpec((tm, 256), lambda i: (i, 0)), pl.BlockSpec((tm, 256), lambda i: (i, 1)),
                  pl.BlockSpec((4, tm, 256), lambda i: (0, i, 0)),
                  pl.BlockSpec((tm, HW), lambda i: (i, 0)), pl.BlockSpec((tm, 512), lambda i: (i, 0)),
                  full((1, 256)), full((1, 256)), full((256, HW)), full((256, HW)), full((256, 512)), tab, tab, tab]
        + [_ANY] * len(p_in),
        out_specs=[pl.BlockSpec((tm, 512), lambda i: (i, 0)), tab, full((256, HW)), full((256, HW)),
                   full((256, 512)), full((8, 256)), full((8, 256))] + p_ospec,
        out_shape=[jax.ShapeDtypeStruct((T, 512), BF16), jax.ShapeDtypeStruct((T, 128), BF16),
                   jax.ShapeDtypeStruct((256, HW), F32), jax.ShapeDtypeStruct((256, HW), F32),
                   jax.ShapeDtypeStruct((256, 512), F32), jax.ShapeDtypeStruct((8, 256), F32),
                   jax.ShapeDtypeStruct((8, 256), F32)] + p_oshape,
        scratch_shapes=p_scr, input_output_aliases=p_alias, compiler_params=_params(48, 1),
    )(*_hbm(z0, z0, dq4, dk, dv, gq, gkv, wq, wk, wv, rc, rs1, rs2), *p_in)


def _sgu_fwd(z0, a_out, ln_g, ln_b, w, b_t):
    T = z0.shape[0]
    tm = min(ROW_BLOCK, T)
    W = SGU_G * SGU_C

    def body(u_ref, v_ref, a_ref, g_ref, b_ref, w_ref, bt_ref, o_ref):
        o_ref[:, :W] = a_ref[...]
        ug = _gelu(u_ref[...])
        xhat, _ = _ln_stats(_gelu(v_ref[...]))
        vn = (xhat * g_ref[...] + b_ref[...]).astype(BF16)
        tril = lax.broadcasted_iota(jnp.int32, (SGU_C, SGU_C), 0) >= lax.broadcasted_iota(jnp.int32, (SGU_C, SGU_C), 1)
        for g in range(SGU_G):
            cs = slice(g * SGU_C, (g + 1) * SGU_C)
            wg = jnp.where(tril, w_ref[g], 0.0).astype(BF16)
            bcol = bt_ref[:, g:g + 1]
            for c in range(tm // SGU_C):
                rs = slice(c * SGU_C, (c + 1) * SGU_C)
                mixed = jnp.dot(wg, vn[rs, cs], preferred_element_type=F32) + bcol
                o_ref[rs, W + g * SGU_C:W + (g + 1) * SGU_C] = (ug[rs, cs] * mixed).astype(BF16)

    full = lambda shape: pl.BlockSpec(shape, lambda i: (0,) * len(shape))
    return pl.pallas_call(
        body, name="sgu_fwd", grid=(T // tm,),
        in_specs=[pl.BlockSpec((tm, W), lambda i: (i, 1)), pl.BlockSpec((tm, W), lambda i: (i, 2)),
                  pl.BlockSpec((tm, W), lambda i: (i, 0)),
                  full((1, W)), full((1, W)), full((SGU_G, SGU_C, SGU_C)), full((SGU_C, SGU_G))],
        out_specs=pl.BlockSpec((tm, 2 * W), lambda i: (i, 0)),
        out_shape=jax.ShapeDtypeStruct((T, 2 * W), BF16),
        compiler_params=_params(32, 1),
    )(z0, z0, a_out, ln_g, ln_b, w, b_t)


def _sgu_bwd(z0, dmix, dc, dkr, ln_g, ln_b, w, b_t):
    T = z0.shape[0]
    tm = min(ROW_BLOCK, T)
    W = SGU_G * SGU_C

    def body(u_ref, v_ref, do_ref, dc_ref, dkr_ref, g_ref, b_ref, w_ref, bt_ref, dz_ref, dw_ref, db_ref, dlg_ref,
             dlb_ref):
        @pl.when(pl.program_id(0) == 0)
        def _():
            for ref in (dw_ref, db_ref, dlg_ref, dlb_ref):
                ref[...] = jnp.zeros_like(ref)

        dz_ref[:, :W] = dc_ref[...]
        dz_ref[:, 3 * W:] = dkr_ref[...]

        u, v, dout = u_ref[...], v_ref[...], do_ref[...]
        ug = _gelu(u)
        xhat, rstd = _ln_stats(_gelu(v))
        vn = (xhat * g_ref[...] + b_ref[...]).astype(BF16)
        dmixed = dout * ug
        dmixed_b = dmixed.astype(BF16)
        tril = lax.broadcasted_iota(jnp.int32, (SGU_C, SGU_C), 0) >= lax.broadcasted_iota(jnp.int32, (SGU_C, SGU_C), 1)
        lane = lax.broadcasted_iota(jnp.int32, (SGU_C, SGU_C), 1)
        dvn_cols = []
        for g in range(SGU_G):
            cs = slice(g * SGU_C, (g + 1) * SGU_C)
            wg = jnp.where(tril, w_ref[g], 0.0).astype(BF16)
            bcol = bt_ref[:, g:g + 1]
            dw_g = jnp.zeros((SGU_C, SGU_C), F32)
            db_g = jnp.zeros((SGU_C, 1), F32)
            dvn_rows = []
            for c in range(tm // SGU_C):
                rs = slice(c * SGU_C, (c + 1) * SGU_C)
                mixed = jnp.dot(wg, vn[rs, cs], preferred_element_type=F32) + bcol
                dz_ref[rs, W + g * SGU_C:W + (g + 1) * SGU_C] = (dout[rs, cs] * mixed * _gelu_grad(u[rs, cs])).astype(BF16)
                dm = dmixed_b[rs, cs]
                dw_g = dw_g + lax.dot_general(dm, vn[rs, cs], NT_DIMS, preferred_element_type=F32)
                db_g = db_g + jnp.sum(dmixed[rs, cs], axis=1, keepdims=True)
                dvn_rows.append(lax.dot_general(wg, dm, TN_DIMS, preferred_element_type=F32))
            dw_ref[g] += jnp.where(tril, dw_g, 0.0)
            db_ref[...] += jnp.where(lane == g, db_g, 0.0)
            dvn_cols.append(jnp.concatenate(dvn_rows, axis=0))
        dvn = jnp.concatenate(dvn_cols, axis=1)
        dxh = dvn * g_ref[...]
        m1 = jnp.mean(dxh, -1, keepdims=True)
        m2 = jnp.mean(dxh * xhat, -1, keepdims=True)
        dvg = rstd * (dxh - m1 - xhat * m2)
        dz_ref[:, 2 * W:3 * W] = (dvg * _gelu_grad(v)).astype(BF16)
        dlg_ref[...] += _fold8(dvn * xhat)
        dlb_ref[...] += _fold8(dvn)

    full = lambda shape: pl.BlockSpec(shape, lambda i: (0,) * len(shape))
    return pl.pallas_call(
        body, name="sgu_bwd", grid=(T // tm,),
        in_specs=[pl.BlockSpec((tm, W), lambda i: (i, 1)), pl.BlockSpec((tm, W), lambda i: (i, 2)),
                  pl.BlockSpec((tm, W), lambda i: (i, 1)), pl.BlockSpec((tm, W), lambda i: (i, 0)),
                  pl.BlockSpec((tm, 128), lambda i: (i, 0)),
                  full((1, W)), full((1, W)), full((SGU_G, SGU_C, SGU_C)), full((SGU_C, SGU_G))],
        out_specs=[pl.BlockSpec((tm, 3 * W + 128), lambda i: (i, 0)), full((SGU_G, SGU_C, SGU_C)),
                   full((SGU_C, SGU_C)), full((8, W)), full((8, W))],
        out_shape=[jax.ShapeDtypeStruct((T, 3 * W + 128), BF16), jax.ShapeDtypeStruct((SGU_G, SGU_C, SGU_C), F32),
                   jax.ShapeDtypeStruct((SGU_C, SGU_C), F32), jax.ShapeDtypeStruct((8, W), F32),
                   jax.ShapeDtypeStruct((8, W), F32)],
        compiler_params=_params(40, 1),
    )(z0, z0, dmix, dc, dkr, ln_g, ln_b, w, b_t)


def _hg_lower_bound(lb_ref):
    a0, a1 = lb_ref[0:1, :], lb_ref[1:2, :]
    m = jnp.maximum(a0, a1)
    e0, e1 = jnp.exp(a0 - m), jnp.exp(a1 - m)
    return e1 / (e0 + e1)


def _running_sum(x, reverse=False):
    n = x.shape[0]
    row = lax.broadcasted_iota(jnp.int32, x.shape, 0)
    s = 1
    while s < n:
        if reverse:
            x = x + jnp.where(row < n - s, pltpu.roll(x, n - s, 0), 0.0)
        else:
            x = x + jnp.where(row >= s, pltpu.roll(x, s, 0), 0.0)
        s *= 2
    return x


def _hg_chunk(qc, fc, lb):
    C = HG_CHUNK
    rows = lax.broadcasted_iota(jnp.int32, (C, C), 0)
    cols = lax.broadcasted_iota(jnp.int32, (C, C), 1)
    rowid = lax.broadcasted_iota(jnp.int32, (C, 128), 0)
    sq, sg = _sigmoid(qc), _sigmoid(fc)
    qf = qc * sq
    gate = lb + (1.0 - lb) * sg
    kk = 1.0 - gate
    lg = jnp.log(gate)
    bcum = _running_sum(lg)
    b_mid = jnp.sum(jnp.where(rowid < C // 2, lg, 0.0), axis=0, keepdims=True)
    b_last = jnp.sum(lg, axis=0, keepdims=True)
    eq, ek, e, eh = jnp.exp(bcum - b_mid), jnp.exp(b_mid - bcum), jnp.exp(bcum), jnp.exp(b_last - bcum)
    qt, kt, qe, khat = qf * eq, kk * ek, qf * e, kk * eh
    a = lax.dot_general(qt.astype(BF16), kt.astype(BF16), NT_DIMS, preferred_element_type=F32)
    a = jnp.where(rows >= cols, a, 0.0)
    return dict(sq=sq, sg=sg, gate=gate, kk=kk, eq=eq, ek=ek, e=e, eh=eh, qt=qt, kt=kt, qe=qe, khat=khat, a=a,
                e_last=jnp.exp(b_last), tril=rows >= cols, rowid=rowid)


def _hgrn_fwd(z4, hg_lb, gnorm):
    T = z4.shape[1]
    tb = min(ROW_BLOCK, T)
    C = HG_CHUNK
    ncb = tb // C
    HPB = HG_HEADS_PER_STEP

    def body(q_ref, f_ref, i_ref, g_ref, lb_ref, gn_ref, y_ref, o_ref, st_ref, st_sc):
        @pl.when(pl.program_id(1) == 0)
        def _():
            st_sc[...] = jnp.zeros_like(st_sc)

        def chunk(c, carry):
            rs = pl.ds(pl.multiple_of(c * C, C), C)
            for hh in range(HPB):
                hs = slice(hh * 128, (hh + 1) * 128)
                lb = _hg_lower_bound(lb_ref.at[:, hs])
                v_b = i_ref[rs, hs].astype(BF16)
                gc = g_ref[rs, hs]
                x = _hg_chunk(q_ref[rs, hs], f_ref[rs, hs], lb)
                st = st_sc[hh]
                st_ref[hh, c] = st
                o = (jnp.dot(x["a"].astype(BF16), v_b, preferred_element_type=F32)
                     + lax.dot_general(x["qe"].astype(BF16), st.astype(BF16), NT_DIMS, preferred_element_type=F32))
                st_sc[hh] = st * x["e_last"] + lax.dot_general(v_b, x["khat"].astype(BF16), TN_DIMS,
                                                               preferred_element_type=F32)
                o_ref[rs, hs] = o
                n = o * lax.rsqrt(jnp.mean(o * o, -1, keepdims=True) + EPS)
                y_ref[rs, hs] = (n * gn_ref[:, hs] * (gc * _sigmoid(gc))).astype(BF16)
            return carry

        lax.fori_loop(0, ncb, chunk, 0)

    W = 128 * HPB
    zb = lambda k: pl.BlockSpec((None, tb, W), lambda h, t: (k, t, h))
    out = pl.BlockSpec((tb, W), lambda h, t: (t, h))
    return pl.pallas_call(
        body, name="hgrn_fwd", grid=(HEADS // HPB, T // tb),
        in_specs=[zb(0), zb(1), zb(2), zb(3), pl.BlockSpec((2, W), lambda h, t: (0, h)),
                  pl.BlockSpec((1, W), lambda h, t: (0, h))],
        out_specs=[out, out, pl.BlockSpec((HPB, ncb, 128, 128), lambda h, t: (h, t, 0, 0))],
        out_shape=[jax.ShapeDtypeStruct((T, D), BF16), jax.ShapeDtypeStruct((T, D), F32),
                   jax.ShapeDtypeStruct((HEADS, T // C, 128, 128), F32)],
        scratch_shapes=[pltpu.VMEM((HPB, 128, 128), F32)],
        compiler_params=_params(48, 2),
    )(*_hbm(z4, z4, z4, z4, hg_lb, gnorm))


def _hgrn_bwd(z4, o_raw, dy, states, hg_lb, gnorm):
    T = z4.shape[1]
    tb = min(ROW_BLOCK, T)
    C = HG_CHUNK
    ncb = tb // C
    nt = T // tb
    HPB = HG_HEADS_PER_STEP

    def body(q_ref, f_ref, i_ref, g_ref, o_ref, dy_ref, st_ref, lb_ref, gn_ref, dz_ref, dlb_ref, dgn_ref, dst_sc):
        @pl.when(pl.program_id(1) == 0)
        def _():
            dst_sc[...] = jnp.zeros_like(dst_sc)
            dlb_ref[...] = jnp.zeros_like(dlb_ref)
            dgn_ref[...] = jnp.zeros_like(dgn_ref)

        def chunk(cc, carry):
            for hh in range(HPB):
                one_head(ncb - 1 - cc, hh, slice(hh * 128, (hh + 1) * 128))
            return carry

        def one_head(c, hh, hs):
            rs = pl.ds(pl.multiple_of(c * C, C), C)
            lb = _hg_lower_bound(lb_ref.at[:, hs])
            gn = gn_ref[:, hs]
            qc, gc = q_ref[rs, hs], g_ref[rs, hs]
            v_b = i_ref[rs, hs].astype(BF16)
            x = _hg_chunk(qc, f_ref[rs, hs], lb)
            st, dst = st_ref[hh, c], dst_sc[hh]
            st_b, dst_b = st.astype(BF16), dst.astype(BF16)
            o, dyc = o_ref[rs, hs], dy_ref[rs, hs]
            sgg = _sigmoid(gc)
            sil = gc * sgg
            rstd = lax.rsqrt(jnp.mean(o * o, -1, keepdims=True) + EPS)
            n = o * rstd
            dgn_ref[:, hs] += _fold8(dyc * n * sil)
            dn = dyc * gn * sil
            do = rstd * (dn - n * jnp.mean(dn * n, -1, keepdims=True))
            dg = dyc * n * gn * (sgg * (1.0 + gc * (1.0 - sgg)))
            do_b = do.astype(BF16)
            da = jnp.where(x["tril"], lax.dot_general(do_b, v_b, NT_DIMS, preferred_element_type=F32), 0.0).astype(BF16)
            qt_b, kt_b, qe_b, khat_b = (x[n_].astype(BF16) for n_ in ("qt", "kt", "qe", "khat"))
            dv = (lax.dot_general(x["a"].astype(BF16), do_b, TN_DIMS, preferred_element_type=F32)
                  + lax.dot_general(khat_b, dst_b, NT_DIMS, preferred_element_type=F32))
            dqt = jnp.dot(da, kt_b, preferred_element_type=F32)
            dqe = jnp.dot(do_b, st_b, preferred_element_type=F32)
            dkt = lax.dot_general(da, qt_b, TN_DIMS, preferred_element_type=F32)
            dkhat = jnp.dot(v_b, dst_b, preferred_element_type=F32)
            dst_sc[hh] = lax.dot_general(do_b, qe_b, TN_DIMS, preferred_element_type=F32) + dst * x["e_last"]
            de_last = jnp.sum(st * dst, axis=0, keepdims=True)
            dqf = dqt * x["eq"] + dqe * x["e"]
            dkk = dkt * x["ek"] + dkhat * x["eh"]
            dkh_kh = dkhat * x["khat"]
            db = dqt * qt_b.astype(F32) - dkt * kt_b.astype(F32) + dqe * x["qe"] - dkh_kh
            db_last = jnp.sum(dkh_kh, axis=0, keepdims=True) + de_last * x["e_last"]
            db = db + jnp.where(x["rowid"] == C - 1, db_last, 0.0)
            dlg = _running_sum(db, reverse=True)
            dgate = dlg / x["gate"] - dkk
            sg, sq = x["sg"], x["sq"]
            dlb_ref[:, hs] += _fold8(dgate * (1.0 - sg)) * (lb * (1.0 - lb))
            dz_ref[0, rs, hs] = (dqf * (sq * (1.0 + qc * (1.0 - sq)))).astype(BF16)
            dz_ref[1, rs, hs] = (dgate * (1.0 - lb) * sg * (1.0 - sg)).astype(BF16)
            dz_ref[2, rs, hs] = dv.astype(BF16)
            dz_ref[3, rs, hs] = dg.astype(BF16)

        lax.fori_loop(0, ncb, chunk, 0)

    W = 128 * HPB
    zb = lambda k: pl.BlockSpec((None, tb, W), lambda h, t: (k, nt - 1 - t, h))
    blk = pl.BlockSpec((tb, W), lambda h, t: (nt - 1 - t, h))
    acc = pl.BlockSpec((8, W), lambda h, t: (0, h))
    return pl.pallas_call(
        body, name="hgrn_bwd", grid=(HEADS // HPB, nt),
        in_specs=[zb(0), zb(1), zb(2), zb(3), blk, blk,
                  pl.BlockSpec((HPB, ncb, 128, 128), lambda h, t: (h, nt - 1 - t, 0, 0)),
                  pl.BlockSpec((2, W), lambda h, t: (0, h)), pl.BlockSpec((1, W), lambda h, t: (0, h))],
        out_specs=[pl.BlockSpec((4, tb, W), lambda h, t: (0, nt - 1 - t, h)), acc, acc],
        out_shape=[jax.ShapeDtypeStruct((4, T, D), BF16), jax.ShapeDtypeStruct((8, D), F32),
                   jax.ShapeDtypeStruct((8, D), F32)],
        scratch_shapes=[pltpu.VMEM((HPB, 128, 128), F32)],
        compiler_params=_params(48, 2),
    )(*_hbm(z4, z4, z4, z4, o_raw, dy, states, hg_lb, gnorm))


def _adamw(w, g, m, v, *, name):
    R, L = w.shape
    tr = R if R <= 512 else 512
    assert R % tr == 0
    blk = pl.BlockSpec((tr, L), lambda i: (i, 0))
    c1, c2 = 1.0 - B1 ** STEP, 1.0 - B2 ** STEP

    def body(w_ref, g_ref, m_ref, v_ref, d_ref, mo_ref, vo_ref):
        g_ = g_ref[...]
        m_ = B1 * m_ref[...] + (1.0 - B1) * g_
        v_ = B2 * v_ref[...] + (1.0 - B2) * (g_ * g_)
        d_ref[...] = -LR * ((m_ / c1) / (jnp.sqrt(v_ / c2) + ADAM_EPS) + WD * w_ref[...])
        mo_ref[...] = m_
        vo_ref[...] = v_

    sds = jax.ShapeDtypeStruct((R, L), F32)
    return pl.pallas_call(
        body, name=name, grid=(R // tr,), in_specs=[blk] * 4, out_specs=[blk] * 3, out_shape=[sds] * 3,
        compiler_params=_params(32, 1),
    )(w, g, m, v)


def _adamw_rows(w, m, v, gbufs, row0, *, name, plan=None):
    L, R, C = w.shape
    tr = 256
    assert R % tr == 0 and row0 % tr == 0 and len(gbufs) == L
    grid = (L, R // tr)
    blk = pl.BlockSpec((None, tr, C), lambda l, i: (l, i, 0))
    gblk = pl.BlockSpec((tr, C), lambda l, i: (row0 // tr + i, 0))
    c1, c2 = 1.0 - B1 ** STEP, 1.0 - B2 ** STEP

    def body(*refs):
        ins, (go_ref, d_ref, mo_ref, vo_ref), _, pctx = _split_refs(refs, 3 + L, 4, 0, plan)
        w_ref, m_ref, v_ref = ins[:3]
        g_refs = ins[3:]
        _plan_start(plan, pctx, grid)
        g_ = g_refs[0][...]
        for l in range(1, L):
            g_ = jnp.where(pl.program_id(0) == l, g_refs[l][...], g_)
        m_ = B1 * m_ref[...] + (1.0 - B1) * g_
        v_ = B2 * v_ref[...] + (1.0 - B2) * (g_ * g_)
        go_ref[...] = g_
        d_ref[...] = -LR * ((m_ / c1) / (jnp.sqrt(v_ / c2) + ADAM_EPS) + WD * w_ref[...])
        mo_ref[...] = m_
        vo_ref[...] = v_
        _plan_wait(plan, pctx, grid)

    sds = jax.ShapeDtypeStruct((L, R, C), F32)
    p_in, p_ospec, p_oshape, p_scr, p_alias = _plan_io(plan, 3 + L, 4)
    return pl.pallas_call(
        body, name=name, grid=grid, in_specs=[blk] * 3 + [gblk] * L + [_ANY] * len(p_in),
        out_specs=[blk] * 4 + p_ospec, out_shape=[sds] * 4 + p_oshape, scratch_shapes=p_scr,
        input_output_aliases=p_alias, compiler_params=_params(32, 2),
    )(*_hbm(w, m, v, *gbufs), *p_in)


def _add_pairs(g, theirs, ids, *, name):
    n, R, L = theirs.shape
    tr = 128
    nb = R // tr

    def body(ids_ref, a_ref, b_ref, o_ref):
        o_ref[...] = (a_ref[...].astype(F32) + b_ref[...].astype(F32)).astype(BF16)

    blk = pl.BlockSpec((n, tr, L), lambda i, ids: (0, i, 0))
    return pl.pallas_call(
        body, name=name, out_shape=jax.ShapeDtypeStruct((n, R, L), BF16),
        grid_spec=pltpu.PrefetchScalarGridSpec(
            num_scalar_prefetch=1, grid=(nb,),
            in_specs=[pl.BlockSpec((n, tr, L), lambda i, ids: (0, ids[1] * nb + i, 0)), blk], out_specs=blk),
        compiler_params=_params(16, 1),
    )(ids, g, theirs)


def _sum_chips(pair, parts, ids, *, name):
    _, R, L = parts.shape
    tr = 128

    def body(ids_ref, o_ref, r_ref, out_ref):
        out_ref[...] = ((o_ref[...].astype(F32) + r_ref[0].astype(F32)) + r_ref[1].astype(F32)) + r_ref[2].astype(F32)

    return pl.pallas_call(
        body, name=name, out_shape=jax.ShapeDtypeStruct((2, R, L), F32),
        grid_spec=pltpu.PrefetchScalarGridSpec(
            num_scalar_prefetch=1, grid=(R // tr,),
            in_specs=[pl.BlockSpec((None, tr, L), lambda i, ids: (ids[0], i, 0)),
                      pl.BlockSpec((3, tr, L), lambda i, ids: (0, i, 0))],
            out_specs=pl.BlockSpec((None, tr, L), lambda i, ids: (ids[1], i, 0))),
        compiler_params=_params(32, 1),
    )(ids, pair, parts)


def _mesh_ids():
    x, y, c = _mesh_pos()
    return jnp.stack([2 * x + y, c]).astype(jnp.int32)


def _place_shard(rows, ids, *, name):
    R, L = rows.shape
    tr = 128

    def body(ids_ref, in_ref, out_ref):
        out_ref[...] = in_ref[...].astype(BF16)

    return pl.pallas_call(
        body, name=name, out_shape=jax.ShapeDtypeStruct((4, R, L), BF16),
        grid_spec=pltpu.PrefetchScalarGridSpec(
            num_scalar_prefetch=1, grid=(R // tr,), in_specs=[pl.BlockSpec((tr, L), lambda i, ids: (i, 0))],
            out_specs=pl.BlockSpec((None, tr, L), lambda i, ids: (ids[0], i, 0))),
        compiler_params=_params(16, 1),
    )(ids, rows)


def _remote(src, dst, send_sem, recv_sem, to):
    return pltpu.make_async_remote_copy(src_ref=src, dst_ref=dst, send_sem=send_sem, recv_sem=recv_sem,
                                        device_id=to, device_id_type=MESH_IDS)


def _rows(ref, lead, start, size):
    return ref.at[tuple(pl.ds(0, n) for n in ref.shape[:lead]) + (pl.ds(start, size),)]


def _other_chips():
    x, y, _ = _mesh_pos()
    return [(1 - x, y), (x, 1 - y), (1 - x, 1 - y)]


def _plan_gather_ici(bufs):
    n = len(bufs)

    def copies(outs, send, recv):
        x, y, c = _mesh_pos()
        res = []
        for b in range(n):
            half = bufs[b].shape[1] // 2
            mine = _rows(outs[b].at[2 * x + y], 0, c * half, half)
            for j, (cx, cy) in enumerate(_other_chips()):
                res.append((_remote(mine, mine, send(3 * b + j), recv(3 * b + j), (cx, cy, c)),
                            _remote(mine, _rows(outs[b].at[2 * cx + cy], 0, c * half, half),
                                    send(3 * b + j), recv(3 * b + j), (x, y, c))))
        return res

    def start(ins, outs, send, recv, loc):
        for out_cp, _ in copies(outs, send, recv):
            out_cp.start()

    def wait(ins, outs, send, recv, loc):
        for out_cp, in_cp in copies(outs, send, recv):
            in_cp.wait_recv()
            out_cp.wait_send()

    outs = [jax.ShapeDtypeStruct(b.shape, b.dtype) for b in bufs]
    return _Plan(bufs, outs, 3 * n, 0, start, wait, aliases={b: b for b in range(n)})


def _plan_gather_forward(bufs):
    n = len(bufs)

    def copies(outs, send, recv):
        x, y, c = _mesh_pos()
        res = []
        for b in range(n):
            half = bufs[b].shape[1] // 2
            for j, (cx, cy) in enumerate(_other_chips()):
                slot = outs[b].at[2 * cx + cy]
                res.append((_remote(_rows(slot, 0, c * half, half), _rows(slot, 0, c * half, half),
                                    send(3 * b + j), recv(3 * b + j), (x, y, 1 - c)),
                            _remote(_rows(slot, 0, c * half, half), _rows(slot, 0, (1 - c) * half, half),
                                    send(3 * b + j), recv(3 * b + j), (x, y, c))))
        return res

    def start(ins, outs, send, recv, loc):
        for out_cp, _ in copies(outs, send, recv):
            out_cp.start()

    def wait(ins, outs, send, recv, loc):
        for out_cp, in_cp in copies(outs, send, recv):
            in_cp.wait_recv()
            out_cp.wait_send()

    outs = [jax.ShapeDtypeStruct(b.shape, b.dtype) for b in bufs]
    return _Plan(bufs, outs, 3 * n, 0, start, wait, aliases={b: b for b in range(n)})


def _plan_pair_swap(g):
    half = g.shape[1] // 2

    def copy(ins, outs, send, recv, loc):
        x, y, c = _mesh_pos()
        return _remote(_rows(ins[0], 1, (1 - c) * half, half), outs[0], send(0), recv(0), (x, y, 1 - c))

    return _Plan([g], [jax.ShapeDtypeStruct((4, half, g.shape[2]), g.dtype)], 1, 0,
                 lambda *a: copy(*a).start(), lambda *a: copy(*a).wait())


def _plan_pair_gather(buf):
    def copies(ins, outs, send, recv, loc):
        x, y, c = _mesh_pos()
        return (_remote(outs[0].at[c], outs[0].at[c], send(0), recv(0), (x, y, 1 - c)),
                _remote(outs[0].at[c], outs[0].at[1 - c], send(0), recv(0), (x, y, c)))

    def wait(*a):
        out_cp, in_cp = copies(*a)
        in_cp.wait_recv()
        out_cp.wait_send()

    return _Plan([buf], [jax.ShapeDtypeStruct(buf.shape, buf.dtype)], 1, 0, lambda *a: copies(*a)[0].start(), wait,
                 aliases={0: 0})


def _plan_chip_scatter(p):
    def copies(ins, outs, send, recv, loc):
        _, _, c = _mesh_pos()
        return [_remote(ins[0].at[2 * cx + cy], outs[0].at[j], send(j), recv(j), (cx, cy, c))
                for j, (cx, cy) in enumerate(_other_chips())]

    def start(*a):
        for cp in copies(*a):
            cp.start()

    def wait(*a):
        for cp in copies(*a):
            cp.wait()

    return _Plan([p], [jax.ShapeDtypeStruct((3,) + p.shape[1:], p.dtype)], 3, 0, start, wait)


def _plan_exchange_all(vec):
    def copies(ins, outs, send, recv, loc):
        x, y, c = _mesh_pos()
        return [_remote(ins[0], outs[0].at[r - 1], send(r - 1), recv(r - 1), (x ^ (r >> 2), y ^ ((r >> 1) & 1), c ^ (r & 1)))
                for r in range(1, 8)]

    def start(*a):
        for cp in copies(*a):
            cp.start()

    def wait(*a):
        for cp in copies(*a):
            cp.wait()

    return _Plan([vec], [jax.ShapeDtypeStruct((7,) + vec.shape, vec.dtype)], 7, 0, start, wait)


SMALL_LAYOUT = {
    "mla_gq": (0, 1, 256, (1, 256)), "mla_gkv": (1, 1, 256, (1, 256)), "sgu_ln_g": (2, 1, 512, (1, 512)),
    "sgu_ln_b": (3, 1, 512, (1, 512)), "sgu_w": (4, 64, 1024, (64, 1024)), "sgu_b": (68, 1, 512, (1, 512)),
    "hg_lb": (69, 2, 1024, (2, 1024)), "hg_gnorm": (71, 1, 1024, (1, 256)), "ln1_g": (72, 2, 1024, (2, 1024)),
    "ln1_b": (74, 2, 1024, (2, 1024)), "ln2_g": (76, 2, 1024, (2, 1024)), "ln2_b": (78, 2, 1024, (2, 1024)),
}


def _small_pack(dgq, dgkv, dslg, dslb, dsw, dsb, dlb, dgn, ln_parts, sq_err):
    flat_ln = [p for pair in ln_parts for p in pair]

    def body(*refs):
        gq_ref, gkv_ref, slg_ref, slb_ref, sw_ref, sb_ref, lb_ref, gn_ref = refs[:8]
        ln_refs, err_ref, out_ref, t_sc = refs[8:16], refs[16], refs[17], refs[18]
        s8 = lambda ref: jnp.sum(ref[...], axis=0, keepdims=True)
        out_ref[...] = jnp.zeros_like(out_ref)
        out_ref[0:1, 0:256] = s8(gq_ref)
        out_ref[1:2, 0:256] = s8(gkv_ref)
        out_ref[2:3, 0:512] = s8(slg_ref)
        out_ref[3:4, 0:512] = s8(slb_ref)
        out_ref[4:68, :] = sw_ref[...]
        t_sc[...] = sb_ref[...].T
        for g in range(SGU_G):
            out_ref[68:69, g * SGU_C:(g + 1) * SGU_C] = t_sc[g:g + 1, :]
        d_lb1 = s8(lb_ref)
        out_ref[69:70, :] = -d_lb1
        out_ref[70:71, :] = d_lb1
        out_ref[71:72, :] = s8(gn_ref)
        for k, ref in enumerate(ln_refs):
            out_ref[72 + k:73 + k, :] = s8(ref)
        out_ref[0:1, 1023:1024] = jnp.sum(s8(err_ref), axis=1, keepdims=True) * (0.5 / D)

    vm = pl.BlockSpec(memory_space=pltpu.VMEM)
    return pl.pallas_call(
        body, name="small_grad_pack", in_specs=[vm] * 17, out_specs=vm,
        out_shape=jax.ShapeDtypeStruct((SMALL_ROWS, 1024), F32), scratch_shapes=[pltpu.VMEM((SGU_C, SGU_C), F32)],
        compiler_params=_params(16),
    )(dgq, dgkv, dslg, dslb, dsw.reshape(64, 1024), dsb, dlb, dgn, *flat_ln, sq_err)


def _small_update(vec, others, ids, w, m, v):
    names = list(SMALL_LAYOUT)
    n = len(names)
    c1, c2 = 1.0 - B1 ** STEP, 1.0 - B2 ** STEP
    have_others = others is not None

    def body(*refs):
        ids_ref, v_ref = refs[0], refs[1]
        k = 2 + have_others
        w_refs, m_refs, v_refs = refs[k:k + n], refs[k + n:k + 2 * n], refs[k + 2 * n:k + 3 * n]
        outs = refs[k + 3 * n:]
        row0_ref, tot_sc = outs[0], outs[-1]
        total = v_ref[...]
        if have_others:
            me = 2 * ids_ref[0] + ids_ref[1]
            total = None
            for d in range(8):
                rel = d ^ me
                term = jnp.where(rel == 0, v_ref[...], refs[2][jnp.maximum(rel - 1, 0)])
                total = term if total is None else total + term
        tot_sc[...] = total
        row0_ref[...] = tot_sc[0:1, :]
        for i, name in enumerate(names):
            r0, nr, width, _ = SMALL_LAYOUT[name]
            if name == "hg_gnorm":
                g_ = tot_sc[r0:r0 + 1, 0:256]
                for chip in range(1, 4):
                    g_ = jnp.where(ids_ref[0] == chip, tot_sc[r0:r0 + 1, chip * 256:(chip + 1) * 256], g_)
            else:
                g_ = tot_sc[r0:r0 + nr, 0:width]
            m_ = B1 * m_refs[i][...] + (1.0 - B1) * g_
            v_ = B2 * v_refs[i][...] + (1.0 - B2) * (g_ * g_)
            go, do, mo, vo = outs[1 + 4 * i:5 + 4 * i]
            go[...] = g_
            do[...] = -LR * ((m_ / c1) / (jnp.sqrt(v_ / c2) + ADAM_EPS) + WD * w_refs[i][...])
            mo[...] = m_
            vo[...] = v_

    full = lambda shape: pl.BlockSpec(shape, lambda i, ids, nd=len(shape): (0,) * nd)
    kshapes = [SMALL_LAYOUT[name][3] for name in names]
    operands = [vec] + ([others] if have_others else []) + [d[name] for d in (w, m, v) for name in names]
    out_shapes = [jax.ShapeDtypeStruct((1, 1024), F32)] + [jax.ShapeDtypeStruct(s, F32) for s in kshapes for _ in range(4)]
    res = pl.pallas_call(
        body, name="small_update", out_shape=out_shapes,
        grid_spec=pltpu.PrefetchScalarGridSpec(
            num_scalar_prefetch=1, grid=(1,), in_specs=[full(o.shape) for o in operands],
            out_specs=[full(s.shape) for s in out_shapes],
            scratch_shapes=[pltpu.VMEM((SMALL_ROWS, 1024), F32)]),
        compiler_params=_params(32, 1),
    )(ids, *operands)
    return res[0], {name: tuple(res[1 + 4 * i:5 + 4 * i]) for i, name in enumerate(names)}


ROWS_L1, ROWS_L0, ROWS_ODD, ROWS_ODD_W = 3328, 2048, 768, 384
ODD_PARTS = (("w_out_e", (256, 1024)), ("w_in_e", (1024, 392)), ("w_qb", (256, 192)), ("w_kvb", (256, 256)))
ODD_W_PARTS = tuple(p for p in ODD_PARTS if p[0] != "w_in_e")


def _odd_rows(parts, dtype, layout, total, gnorm=None):
    rows = [parts[n].reshape(-1, 1024).astype(dtype) for n, _ in layout]
    used = sum(r.shape[0] for r in rows)
    if gnorm is not None:
        bits = lax.bitcast_convert_type(gnorm.reshape(-1), BF16).reshape(1, 512)
        rows.append(jnp.pad(bits, ((0, 0), (0, 512))))
        used += 1
    rows.append(jnp.zeros((total - used, 1024), dtype))
    return jnp.concatenate(rows, axis=0)


def _odd_unrows(buf, layout, with_gnorm=False):
    out, off = {}, 0
    for n, shape in layout:
        nr = math.prod(shape) // 1024
        out[n] = buf[off:off + nr].reshape(shape)
        off += nr
    if with_gnorm:
        out["hg_gnorm"] = lax.bitcast_convert_type(buf[off, :512].reshape(256, 2), F32).reshape(1, 256)
    return out


def _rope_tables(positions):
    half = ROPE // 2
    inv_freq = ROPE_BASE ** (-jnp.arange(half, dtype=F32) / half)
    ang = positions.astype(F32).reshape(-1, 1) * inv_freq
    cos, sin = jnp.cos(ang), jnp.sin(ang)
    T = ang.shape[0]
    one, z16, z32 = jnp.ones((T, NOPE), F32), jnp.zeros((T, half), F32), jnp.zeros((T, 32), F32)
    z64 = jnp.zeros((T, NOPE), F32)
    c = jnp.concatenate([one, cos, cos, z32], axis=1)
    s1 = jnp.concatenate([z64, -sin, z16, z32], axis=1)
    s2 = jnp.concatenate([z64, z16, sin, z32], axis=1)
    return c, s1, s2


def _local_step(x, positions, tgt, odd, bufs, P, exchange):
    T = x.shape[0]
    row = lambda a: a.reshape(1, -1)
    rc, rs1, rs2 = _rope_tables(positions)
    blk = lambda f: pl.BlockSpec((None, D, D), f)

    w_in_e = odd["w_in_e"]
    w_in = jnp.concatenate([w_in_e[:, :512], w_in_e[:, 544:1568], w_in_e[:, 512:544], jnp.zeros((D, 96), BF16)], axis=1)
    wq = jnp.pad(odd["w_qb"].reshape(256, HEADS, NOPE + ROPE), ((0, 0), (0, 0), (0, 32))).reshape(256, HEADS * 128)
    kvb = odd["w_kvb"].reshape(256, HEADS, NOPE + VDIM)
    wk = jnp.pad(kvb[:, :, :NOPE], ((0, 0), (0, 0), (0, 64))).reshape(256, HEADS * 128)
    wv = kvb[:, :, NOPE:].reshape(256, HEADS * VDIM)
    w_out_e = odd["w_out_e"]
    sgu_w = P["sgu_w"][0]
    sgu_bt = P["sgu_b"][0].T
    gq, gkv = P["mla_gq"], P["mla_gkv"]
    gnorm = P["hg_gnorm"]

    z0 = _matmul(x, w_in, name="in_proj_e", M=T, N=1664, K=D, tn=1664)[0]
    q, k, v = _mla_prep(z0, gq, gkv, wq, wk, wv, rc, rs1, rs2)
    if exchange:
        ids = _mesh_ids()
        placed = [_place_shard(b, ids, name=f"place_shard_{l}") for l, b in enumerate(bufs)]
        a_out, lse, wga, wgb = _flash_fwd(q, k, v, plan=_plan_gather_ici(placed[:2]))
    else:
        a_out, lse = _flash_fwd(q, k, v)
        wga, wgb, wgc = bufs
    mix0 = _sgu_fwd(z0, a_out, P["sgu_ln_g"], P["sgu_ln_b"], sgu_w, sgu_bt)
    res = _proj_ln(mix0, w_out_e, x, row(P["ln1_g"][0]), row(P["ln1_b"][0]), name="out_proj_ln_e",
                   plan=_plan_gather_forward([wga, wgb]) if exchange else None)
    r1, h1, h1b = res[:3]
    if exchange:
        wga, wgb = res[3:]
    res = _ffn_ln(h1b, wga, h1, row(P["ln2_g"][0]), row(P["ln2_b"][0]), name="ffn_ln_0",
                  plan=_plan_gather_ici(placed[2:]) if exchange else None)
    ra0, r2, h2, h2b = res[:4]
    z4 = _matmul(h2b, wgb, name="in_proj_o", M=T, N=4 * D, K=D, b_spec=blk(lambda i, j, k: (j, 0, 0)),
                 out_shape=jax.ShapeDtypeStruct((4, T, D), F32),
                 o_spec=pl.BlockSpec((None, min(MM_ROWS, T), D), lambda i, j, k: (j, i, 0)))[0]
    y1, o_raw, states = _hgrn_fwd(z4, P["hg_lb"], gnorm)
    res2 = _proj_ln(y1, wgb, h2, row(P["ln1_g"][1]), row(P["ln1_b"][1]), name="out_proj_ln_o", w_rowblk=4,
                    plan=_plan_gather_forward([res[4]]) if exchange else None)
    r3, h3, h3b = res2[:3]
    if exchange:
        wgc = res2[3]
    ra1, r4, h4, _ = _ffn_ln(h3b, wgc, h3, row(P["ln2_g"][1]), row(P["ln2_b"][1]), name="ffn_ln_1")

    ln1_g, ln1_b, ln2_g, ln2_b = [None, None], [None, None], [None, None], [None, None]
    sq_err_parts = []

    def ffn_bwd(l, dh, r_out, ra, h_mid_b, g2, wg, rows, plan=None, tgt=None):
        dr, dr_b, dg, db, *sq_err = _ln_bwd(dh, r_out, row(g2), name=f"ln2_bwd_{l}", tgt=tgt)
        sq_err_parts.extend(sq_err)
        ln2_g[l], ln2_b[l] = dg, db
        da, *extra = _matmul(dr_b, wg, tb=True, mul=ra, out_dtype=BF16, name=f"ffn_da_{l}", M=T, N=4 * D, K=D,
                             b_spec=blk(lambda i, j, k: (j, 1, 0)), plan=plan)
        gbuf = _matmul(ra, dr_b, ta=True, a_sq=True, name=f"ffn_dw2_{l}", M=4 * D, N=D, K=T, tm=1024, tk=DW_TOKENS,
                       out_shape=jax.ShapeDtypeStruct((4, rows, D), BF16), o_spec=blk(lambda i, j, k: (i, 1, 0)))[0]
        gbuf = _matmul(h_mid_b, da, ta=True, name=f"ffn_dw1_{l}", M=D, N=4 * D, K=T, tm=1024, tk=DW_TOKENS, into=gbuf,
                       out_shape=jax.ShapeDtypeStruct((4, rows, D), BF16), o_spec=blk(lambda i, j, k: (j, 0, 0)))[0]
        dh_mid = _matmul(da, wg, tb=True, add=dr, add_scale=ALPHA, name=f"ffn_dh_{l}", M=T, N=D, K=4 * D,
                         b_spec=blk(lambda i, j, k: (k, 0, 0)))[0]
        return dh_mid, gbuf, extra

    dh3, g1, _ = ffn_bwd(1, h4, r4, ra1, h3b, P["ln2_g"][1], wgc, ROWS_L1, tgt=tgt)
    loss_parts = sq_err_parts[0]
    dr3, dr3_b, dg, db = _ln_bwd(dh3, r3, row(P["ln1_g"][1]), name="ln1_bwd_1")
    ln1_g[1], ln1_b[1] = dg, db
    g1_sds = jax.ShapeDtypeStruct((4, ROWS_L1, D), BF16)
    g1 = _matmul(y1, dr3_b, ta=True, name="dw_out_o", M=D, N=D, K=T, tm=256, tk=DW_TOKENS, into=g1, out_shape=g1_sds,
                 o_spec=pl.BlockSpec((None, 256, D), lambda i, j, k: (i, 12, 0)))[0]
    dmix1 = _matmul(dr3_b, wgb, tb=True, name="dmix_o", M=T, N=D, K=D, b_spec=_rows4_spec(4, 3), b_merge=(D, D))[0]
    dz4, dlb, dgn = _hgrn_bwd(z4, o_raw, dmix1, states, P["hg_lb"], gnorm)
    g1 = _matmul(h2b, dz4, ta=True, name="dw_in_o", M=D, N=4 * D, K=T, tm=1024, tk=DW_TOKENS, into=g1, out_shape=g1_sds,
                 b_spec=pl.BlockSpec((None, min(DW_TOKENS, T), D), lambda i, j, k: (j, k, 0)),
                 o_spec=blk(lambda i, j, k: (j, 2, 0)))[0]
    dh2 = _matmul(dz4, wgb, tb=True, add=dr3, add_scale=ALPHA, name="dh_in_o", M=T, N=D, K=4 * D,
                  a_spec=pl.BlockSpec((None, min(MM_ROWS, T), D), lambda i, j, k: (k, i, 0)),
                  b_spec=blk(lambda i, j, k: (k, 0, 0)))[0]

    dh1, g0, swapped1 = ffn_bwd(0, dh2, r2, ra0, h1b, P["ln2_g"][0], wga, ROWS_L0,
                                plan=_plan_pair_swap(g1) if exchange else None)
    dr1, dr1_b, dg, db = _ln_bwd(dh1, r1, row(P["ln1_g"][0]), name="ln1_bwd_0")
    ln1_g[0], ln1_b[0] = dg, db
    godd = {"w_out_e": _matmul(mix0, dr1_b, ta=True, name="dw_out_e", M=D, N=D, K=T, tm=1024, tk=DW_TOKENS)[0]}
    dmix0, *swapped0 = _matmul(dr1_b, w_out_e, tb=True, name="dmix_e", M=T, N=D, K=D,
                               plan=_plan_pair_swap(g0) if exchange else None)
    delta, do_b = _attn_delta(dmix0, a_out)
    if exchange:
        pair1 = _add_pairs(g1, swapped1[0], ids, name="grad_pair_add_1")
        pair0 = _add_pairs(g0, swapped0[0], ids, name="grad_pair_add_0")
        dq4, dk, dv, parts0, parts1 = _flash_bwd(
            q, k, v, do_b, lse, delta, plan=_join_plans([_plan_chip_scatter(pair0), _plan_chip_scatter(pair1)]))
        half0 = _sum_chips(pair0, parts0, ids, name="grad_chip_sum_0")
        half1 = _sum_chips(pair1, parts1, ids, name="grad_chip_sum_1")
        dc, dkr, dwq, dwk, dwv, dgq, dgkv, g0, g1 = _mla_bwd(
            z0, dq4, dk, dv, gq, gkv, wq, wk, wv, rc, rs1, rs2,
            plan=_join_plans([_plan_pair_gather(half0), _plan_pair_gather(half1)]))
        g0, g1 = g0.reshape(ROWS_L0, D), g1.reshape(ROWS_L1, D)
    else:
        dq4, dk, dv = _flash_bwd(q, k, v, do_b, lse, delta)
        dc, dkr, dwq, dwk, dwv, dgq, dgkv = _mla_bwd(z0, dq4, dk, dv, gq, gkv, wq, wk, wv, rc, rs1, rs2)
    dz0, dsw, dsb, dslg, dslb = _sgu_bwd(z0, dmix0, dc, dkr, P["sgu_ln_g"], P["sgu_ln_b"], sgu_w, sgu_bt)
    small_vec = _small_pack(dgq, dgkv, dslg, dslb, dsw, dsb, dlb, dgn, [ln1_g, ln1_b, ln2_g, ln2_b], loss_parts)
    dw_in, *small_others = _matmul(x, dz0, ta=True, name="dw_in_e", M=D, N=1664, K=T, tm=1024, tn=1664,
                                   tk=DW_TOKENS // 2, plan=_plan_exchange_all(small_vec) if exchange else None)
    godd["w_in_e"] = jnp.concatenate([dw_in[:, :512], dw_in[:, 1536:1568], dw_in[:, 512:1536]], axis=1)
    godd["w_qb"] = dwq.reshape(256, HEADS, 128)[:, :, :NOPE + ROPE].reshape(256, HEADS * (NOPE + ROPE))
    godd["w_kvb"] = jnp.concatenate([dwk.reshape(256, HEADS, 128)[:, :, :NOPE], dwv.reshape(256, HEADS, VDIM)],
                                    axis=2).reshape(256, HEADS * (NOPE + VDIM))
    odd_plan = None
    if exchange:
        by_chip = [_odd_rows({"w_out_e": jnp.split(godd["w_out_e"], 4, axis=0)[j],
                              **{n: jnp.split(godd[n], 4, axis=1)[j] for n in ("w_in_e", "w_qb", "w_kvb")}}, BF16,
                             ODD_PARTS, ROWS_ODD)
                   for j in range(4)]
        godd_buf = jnp.stack(by_chip)
        theirs = _run_plan(_plan_pair_swap(godd_buf), name="odd_pair_swap")[0]
        odd_pair = _add_pairs(godd_buf, theirs, ids, name="odd_pair_add")
        odd_plan = _plan_chip_scatter(odd_pair)
    grad_x, *odd_parts = _matmul(dz0, w_in, tb=True, add=dr1, add_scale=ALPHA, name="dx", M=T, N=D, K=1664, tk=1664,
                                 plan=odd_plan)
    if exchange:
        godd = (odd_pair, odd_parts[0])
    return grad_x, g0, g1, godd, small_vec, (small_others[0] if exchange else None)


WEIGHTS = ['w_in_e', 'mla_gq', 'mla_gkv', 'w_qb', 'w_kvb', 'sgu_ln_g', 'sgu_ln_b', 'sgu_w', 'sgu_b', 'w_out_e',
           'w_in_o', 'hg_lb', 'hg_gnorm', 'w_out_o', 'ln1_g', 'ln1_b', 'w_ff1', 'w_ff2', 'ln2_g', 'ln2_b']


def kernel(x, positions, w_in_e, mla_gq, mla_gkv, w_qb, w_kvb, sgu_ln_g, sgu_ln_b, sgu_w, sgu_b, w_out_e, w_in_o, hg_lb, hg_gnorm, w_out_o, ln1_g, ln1_b, w_ff1, w_ff2, ln2_g, ln2_b, loss_target, m_w_in_e, m_mla_gq, m_mla_gkv, m_w_qb, m_w_kvb, m_sgu_ln_g, m_sgu_ln_b, m_sgu_w, m_sgu_b, m_w_out_e, m_w_in_o, m_hg_lb, m_hg_gnorm, m_w_out_o, m_ln1_g, m_ln1_b, m_w_ff1, m_w_ff2, m_ln2_g, m_ln2_b, v_w_in_e, v_mla_gq, v_mla_gkv, v_w_qb, v_w_kvb, v_sgu_ln_g, v_sgu_ln_b, v_sgu_w, v_sgu_b, v_w_out_e, v_w_in_o, v_hg_lb, v_hg_gnorm, v_w_out_o, v_ln1_g, v_ln1_b, v_w_ff1, v_w_ff2, v_ln2_g, v_ln2_b):
    args = dict(locals())
    w = {n: args[n] for n in WEIGHTS}
    m = {n: args["m_" + n] for n in WEIGHTS}
    v = {n: args["v_" + n] for n in WEIGHTS}
    cx, cy, cc = _mesh_pos()
    chip = 2 * cx + cy

    odd_shard = _odd_rows({"w_out_e": w_out_e[0], "w_qb": w_qb[0], "w_kvb": w_kvb[0]}, BF16, ODD_W_PARTS, ROWS_ODD_W,
                          gnorm=hg_gnorm)
    ids = _mesh_ids()
    placed = [_place_shard(w_in_e[0], ids, name="place_shard_in_e"), _place_shard(odd_shard, ids, name="place_shard_odd")]
    gathered = _run_plan(_plan_gather_forward(_run_plan(_plan_gather_ici(placed), name="odd_gather")),
                         name="odd_gather_forward")
    per_chip = [_odd_unrows(gathered[1][j], ODD_W_PARTS, with_gnorm=True) for j in range(4)]
    odd = {"w_out_e": jnp.concatenate([p["w_out_e"] for p in per_chip], axis=0),
           "w_in_e": jnp.concatenate([gathered[0][j] for j in range(4)], axis=1)}
    for n in ("w_qb", "w_kvb"):
        odd[n] = jnp.concatenate([p[n] for p in per_chip], axis=1)
    small = {n: w[n] for n in SMALL_LAYOUT if n != "hg_gnorm"}
    small["hg_gnorm"] = jnp.concatenate([p["hg_gnorm"] for p in per_chip], axis=1)
    shard_rows = (jnp.concatenate([w_ff1[0], w_ff2[0]], axis=0).astype(BF16),
                  jnp.concatenate([w_in_o[0], w_out_o[0]], axis=0).astype(BF16),
                  jnp.concatenate([w_ff1[1], w_ff2[1]], axis=0).astype(BF16))

    grad_x, g_l0, g_l1, godd, small_vec, small_others = _local_step(
        x[0], positions[0], loss_target[0], odd, shard_rows, small, True)

    pair, parts = godd
    g_odd = _run_plan(_plan_pair_gather(_sum_chips(pair, parts, ids, name="odd_chip_sum")), name="odd_pair_gather")[0]
    g_odd = _odd_unrows(g_odd.reshape(ROWS_ODD, 1024), ODD_PARTS)

    to_kernel = lambda d: {n: d[n].reshape(SMALL_LAYOUT[n][3]) for n in SMALL_LAYOUT}
    first_row, small_out = _small_update(small_vec, small_others, ids, to_kernel(w), to_kernel(m), to_kernel(v))
    loss = first_row[0, 1023]
    grads, delta, new_m, new_v = {}, {}, {}, {}
    for n, res in small_out.items():
        grads[n], delta[n], new_m[n], new_v[n] = (r.reshape(w[n].shape) for r in res)

    for n, bufs_, row0 in (("w_ff1", [g_l0, g_l1], 0), ("w_ff2", [g_l0, g_l1], 1024), ("w_in_o", [g_l1], 2048),
                           ("w_out_o", [g_l1], 3072)):
        grads[n], delta[n], new_m[n], new_v[n] = _adamw_rows(w[n], m[n], v[n], bufs_, row0, name=f"adamw_{n}")
    for n, _ in ODD_PARTS:
        grads[n] = g_odd[n][None]
        d_, m_, v_ = _adamw(w[n][0], g_odd[n], m[n][0], v[n][0], name=f"adamw_{n}")
        delta[n], new_m[n], new_v[n] = d_[None], m_[None], v_[None]

    return (loss, grad_x[None], *[grads[n] for n in WEIGHTS], *[delta[n] for n in WEIGHTS],
            *[new_m[n] for n in WEIGHTS], *[new_v[n] for n in WEIGHTS])
```

```python
import math

import jax
import jax.numpy as jnp
from jax import lax
from jax.experimental import pallas as pl
from jax.experimental.pallas import tpu as pltpu

F32 = jnp.float32
BF16 = jnp.bfloat16
MESH_IDS = pl.DeviceIdType.MESH

D = 1024
DEPTH = 2
HEADS = 8
NOPE, ROPE, VDIM = 64, 32, 64
QK_SCALE = (NOPE + ROPE) ** -0.5
ROPE_BASE = 10000.0
SGU_G, SGU_C = 4, 128
HG_CHUNK = 64
HG_HEADS_PER_STEP = 8
ALPHA = (2 * DEPTH) ** 0.25
EPS = 1e-5
LR, B1, B2, ADAM_EPS, WD, STEP = 0.001, 0.9, 0.999, 1e-08, 0.01, 10
GELU_C = math.sqrt(2.0 / math.pi)
GELU_A = 0.044715
MB = 1024 * 1024
ROW_BLOCK = 512
SMALL_ROWS = 80

NT_DIMS = (((1,), (1,)), ((), ()))
TN_DIMS = (((0,), (0,)), ((), ()))


def _params(vmem_mb, n_axes=0):
    kw = dict(vmem_limit_bytes=vmem_mb * MB)
    if n_axes:
        kw["dimension_semantics"] = ("arbitrary",) * n_axes
    return pltpu.CompilerParams(**kw)


_ANY = pl.BlockSpec(memory_space=pltpu.HBM)


def _mesh_pos():
    return lax.axis_index("x"), lax.axis_index("y"), lax.axis_index("c")


def _hbm(*arrays):
    return tuple(pltpu.with_memory_space_constraint(a, pltpu.HBM) if a.size >= 2 ** 18 else a for a in arrays)


class _Plan:
    def __init__(self, ins, outs, n_remote, n_local, start, wait, aliases=None):
        self.ins, self.outs, self.n_remote, self.n_local = list(ins), list(outs), n_remote, n_local
        self.start, self.wait, self.aliases = start, wait, dict(aliases or {})


def _join_plans(plans):
    ins, outs, aliases, parts = [], [], {}, []
    nr = nl = 0
    for p in plans:
        parts.append((p, len(ins), len(outs), nr, nl))
        aliases.update({len(ins) + i: len(outs) + o for i, o in p.aliases.items()})
        ins += p.ins
        outs += p.outs
        nr += p.n_remote
        nl += p.n_local

    def run(which):
        def go(in_refs, out_refs, send, recv, loc):
            for p, i0, o0, r0, l0 in parts:
                getattr(p, which)(in_refs[i0:i0 + len(p.ins)], out_refs[o0:o0 + len(p.outs)],
                                  lambda i, r0=r0: send(r0 + i), lambda i, r0=r0: recv(r0 + i),
                                  lambda i, l0=l0: loc(l0 + i))
        return go

    return _Plan(ins, outs, nr, nl, run("start"), run("wait"), aliases)


def _plan_io(plan, n_in, n_out):
    if plan is None:
        return [], [], [], [], {}
    sems = [pltpu.SemaphoreType.DMA((max(plan.n_remote, 1),)), pltpu.SemaphoreType.DMA((max(plan.n_remote, 1),)),
            pltpu.SemaphoreType.DMA((max(plan.n_local, 1),))]
    aliases = {n_in + i: n_out + o for i, o in plan.aliases.items()}
    return plan.ins, [_ANY] * len(plan.outs), plan.outs, sems, aliases


def _split_refs(refs, n_in, n_out, n_scr, plan):
    p_in, p_out = (len(plan.ins), len(plan.outs)) if plan is not None else (0, 0)
    refs = list(refs)
    ins, refs = refs[:n_in], refs[n_in:]
    pins, refs = refs[:p_in], refs[p_in:]
    outs, refs = refs[:n_out], refs[n_out:]
    pouts, refs = refs[:p_out], refs[p_out:]
    scr, psem = refs[:n_scr], refs[n_scr:]
    psem = tuple((lambda i, s=s: s.at[i]) for s in psem)
    return ins, outs, scr, (pins, pouts, psem)


def _grid_edge(grid, last):
    cond = None
    for ax, n in enumerate(grid):
        c = pl.program_id(ax) == (n - 1 if last else 0)
        cond = c if cond is None else cond & c
    return cond


def _plan_start(plan, pctx, grid):
    if plan is not None:
        pins, pouts, psem = pctx
        pl.when(_grid_edge(grid, False))(lambda: plan.start(pins, pouts, *psem))


def _plan_wait(plan, pctx, grid):
    if plan is not None:
        pins, pouts, psem = pctx
        pl.when(_grid_edge(grid, True))(lambda: plan.wait(pins, pouts, *psem))


def _run_plan(plan, *, name):
    def body(*refs):
        _, _, _, (pins, pouts, psem) = _split_refs(refs, 0, 0, 0, plan)
        plan.start(pins, pouts, *psem)
        plan.wait(pins, pouts, *psem)

    p_in, p_ospec, p_oshape, p_scr, p_alias = _plan_io(plan, 0, 0)
    return pl.pallas_call(body, name=name, in_specs=[_ANY] * len(p_in), out_specs=p_ospec, out_shape=p_oshape,
                          scratch_shapes=p_scr, input_output_aliases=p_alias)(*p_in)


def _fold8(x):
    return x.reshape(x.shape[0] // 8, 8, x.shape[1]).sum(axis=0)


def _ln_stats(r):
    mu = jnp.mean(r, -1, keepdims=True)
    xc = r - mu
    rstd = lax.rsqrt(jnp.mean(xc * xc, -1, keepdims=True) + EPS)
    return xc * rstd, rstd


def _sigmoid(x):
    return jax.nn.sigmoid(x)


def _gelu(x):
    return 0.5 * x * (1.0 + jnp.tanh(GELU_C * (x + GELU_A * x * x * x)))


def _gelu_grad(x):
    t = jnp.tanh(GELU_C * (x + GELU_A * x * x * x))
    return 0.5 * (1.0 + t) + 0.5 * x * (1.0 - t * t) * GELU_C * (1.0 + 3.0 * GELU_A * x * x)


MM_ROWS = 1024
DW_TOKENS = 2048


def _matmul(a, b, *, name, M, N, K, ta=False, tb=False, out_dtype=F32, tm=MM_ROWS, tn=1024, tk=1024,
            a_spec=None, b_spec=None, b_merge=None, out_shape=None, o_spec=None, into=None,
            a_sq=False, mul=None, add=None, add_scale=1.0, plan=None):
    tm, tn, tk = min(tm, M), min(tn, N), min(tk, K)
    assert M % tm == 0 and N % tn == 0 and K % tk == 0
    grid = (M // tm, N // tn, K // tk)
    nk = grid[2]
    if a_spec is None:
        a_spec = pl.BlockSpec((tk, tm), lambda i, j, k: (k, i)) if ta else pl.BlockSpec((tm, tk), lambda i, j, k: (i, k))
    if b_spec is None:
        b_spec = pl.BlockSpec((tn, tk), lambda i, j, k: (j, k)) if tb else pl.BlockSpec((tk, tn), lambda i, j, k: (k, j))
    if o_spec is None:
        o_spec = pl.BlockSpec((tm, tn), lambda i, j, k: (i, j))
        out_shape = jax.ShapeDtypeStruct((M, N), out_dtype)
    e_spec = pl.BlockSpec((tm, tn), lambda i, j, k: (i, j))
    dims = (((0 if ta else 1,), (1 if tb else 0,)), ((), ()))
    extra = [e for e in (mul, add, into) if e is not None]
    n_in = 2 + len(extra)

    def body(*refs):
        ins, outs, scr, pctx = _split_refs(refs, n_in, 1, 1 if nk > 1 else 0, plan)
        a_ref, b_ref = ins[0], ins[1]
        rest = list(ins[2:])
        mul_ref = rest.pop(0) if mul is not None else None
        add_ref = rest.pop(0) if add is not None else None
        o_ref = outs[0]
        _plan_start(plan, pctx, grid)
        av = a_ref[...].astype(BF16)
        if a_sq:
            av = av * av
        bv = b_ref[...]
        if b_merge is not None:
            bv = bv.reshape(b_merge)
        p = lax.dot_general(av, bv, dims, preferred_element_type=F32)

        def finish(r):
            if mul_ref is not None:
                r = r * (2.0 * mul_ref[...].astype(F32))
            if add_ref is not None:
                r = r + add_scale * add_ref[...]
            o_ref[...] = r.astype(o_ref.dtype)

        if nk == 1:
            finish(p)
        else:
            acc_ref = scr[0]
            k = pl.program_id(2)

            @pl.when(k == 0)
            def _():
                acc_ref[...] = p

            @pl.when(k > 0)
            def _():
                acc_ref[...] += p

            @pl.when(k == nk - 1)
            def _():
                finish(acc_ref[...])

        _plan_wait(plan, pctx, grid)

    p_in, p_ospec, p_oshape, p_scr, p_alias = _plan_io(plan, n_in, 1)
    aliases = dict(p_alias)
    if into is not None:
        aliases[n_in - 1] = 0
    return pl.pallas_call(
        body, name=name, grid=grid,
        in_specs=[a_spec, b_spec] + [e_spec] * (len(extra) - (into is not None)) + [_ANY] * (into is not None)
        + [_ANY] * len(p_in),
        out_specs=[o_spec] + p_ospec, out_shape=[out_shape] + p_oshape,
        scratch_shapes=([pltpu.VMEM((tm, tn), F32)] if nk > 1 else []) + p_scr,
        input_output_aliases=aliases, compiler_params=_params(48, 3),
    )(*_hbm(a, b, *extra), *p_in)


def _rows4_spec(rowblk, n_axes):
    return pl.BlockSpec((4, 256, D), lambda *_: (0, rowblk, 0))


def _proj_ln(a_b, w, h_prev, g, b, *, name, w_rowblk=None, plan=None):
    T = a_b.shape[0]
    tm = min(ROW_BLOCK, T)
    grid = (T // tm,)
    row = pl.BlockSpec((tm, D), lambda i: (i, 0))
    vec = pl.BlockSpec((1, D), lambda i: (0, 0))
    w_spec = pl.BlockSpec((D, D), lambda i: (0, 0)) if w_rowblk is None else _rows4_spec(w_rowblk, 1)

    def body(*refs):
        (a_ref, w_ref, h_ref, g_ref, b_ref), (r_ref, ho_ref, hb_ref), _, pctx = _split_refs(refs, 5, 3, 0, plan)
        _plan_start(plan, pctx, grid)
        mix = jnp.dot(a_ref[...], w_ref[...].reshape(D, D), preferred_element_type=F32)
        r = ALPHA * h_ref[...] + mix
        xhat, _ = _ln_stats(r)
        y = xhat * g_ref[...] + b_ref[...]
        r_ref[...] = r
        ho_ref[...] = y
        hb_ref[...] = y.astype(BF16)
        _plan_wait(plan, pctx, grid)

    p_in, p_ospec, p_oshape, p_scr, p_alias = _plan_io(plan, 5, 3)
    return pl.pallas_call(
        body, name=name, grid=grid,
        in_specs=[row, w_spec, row, vec, vec] + [_ANY] * len(p_in),
        out_specs=[row, row, row] + p_ospec,
        out_shape=[jax.ShapeDtypeStruct((T, D), F32), jax.ShapeDtypeStruct((T, D), F32),
                   jax.ShapeDtypeStruct((T, D), BF16)] + p_oshape,
        scratch_shapes=p_scr, input_output_aliases=p_alias, compiler_params=_params(40, 1),
    )(*_hbm(a_b, w, h_prev, g, b), *p_in)


def _ffn_ln(h_b, wbuf, h, g, b, *, name, plan=None):
    T = h_b.shape[0]
    tm, tf = min(ROW_BLOCK, T), 1024
    nf = 4
    F = nf * tf
    grid = (T // tm, nf)
    row = pl.BlockSpec((tm, D), lambda i, j: (i, 0))
    vec = pl.BlockSpec((1, D), lambda i, j: (0, 0))

    def body(*refs):
        ((hb_ref, w1_ref, w2_ref, h_ref, g_ref, b_ref), (ra_ref, r_ref, ho_ref, hbo_ref), (acc_ref,),
         pctx) = _split_refs(refs, 6, 4, 1, plan)
        _plan_start(plan, pctx, grid)
        j = pl.program_id(1)
        a = jnp.dot(hb_ref[...], w1_ref[...], preferred_element_type=F32)
        ra = jnp.maximum(a, 0.0)
        ra_ref[...] = ra.astype(BF16)
        p = jnp.dot((ra * ra).astype(BF16), w2_ref[...], preferred_element_type=F32)

        @pl.when(j == 0)
        def _():
            acc_ref[...] = p

        @pl.when(j > 0)
        def _():
            acc_ref[...] += p

        @pl.when(j == nf - 1)
        def _():
            r = ALPHA * h_ref[...] + acc_ref[...]
            xhat, _ = _ln_stats(r)
            y = xhat * g_ref[...] + b_ref[...]
            r_ref[...] = r
            ho_ref[...] = y
            hbo_ref[...] = y.astype(BF16)

        _plan_wait(plan, pctx, grid)

    p_in, p_ospec, p_oshape, p_scr, p_alias = _plan_io(plan, 6, 4)
    return pl.pallas_call(
        body, name=name, grid=grid,
        in_specs=[row, pl.BlockSpec((None, D, tf), lambda i, j: (j, 0, 0)),
                  pl.BlockSpec((None, tf, D), lambda i, j: (j, 1, 0)), row, vec, vec] + [_ANY] * len(p_in),
        out_specs=[pl.BlockSpec((tm, tf), lambda i, j: (i, j)), row, row, row] + p_ospec,
        out_shape=[jax.ShapeDtypeStruct((T, F), BF16), jax.ShapeDtypeStruct((T, D), F32),
                   jax.ShapeDtypeStruct((T, D), F32), jax.ShapeDtypeStruct((T, D), BF16)] + p_oshape,
        scratch_shapes=[pltpu.VMEM((tm, D), F32)] + p_scr,
        input_output_aliases=p_alias, compiler_params=_params(48, 2),
    )(*_hbm(h_b, wbuf, wbuf, h, g, b), *p_in)


def _ln_bwd(dy, r, g, *, name, tgt=None):
    T = dy.shape[0]
    tm = min(ROW_BLOCK, T)
    row = pl.BlockSpec((tm, D), lambda i: (i, 0))
    acc = pl.BlockSpec((8, D), lambda i: (0, 0))
    n_in = 3 + (tgt is not None)

    def body(*refs):
        dy_ref, r_ref, g_ref = refs[:3]
        dr_ref, drb_ref, dg_ref, db_ref = refs[n_in:n_in + 4]

        @pl.when(pl.program_id(0) == 0)
        def _():
            for ref in refs[n_in + 2:]:
                ref[...] = jnp.zeros_like(ref)

        dy_ = dy_ref[...]
        if tgt is not None:
            err = dy_ - refs[3][...]
            refs[n_in + 4][...] += _fold8(err * err)
            dy_ = err * (1.0 / D)
        xhat, rstd = _ln_stats(r_ref[...])
        dxh = dy_ * g_ref[...]
        m1 = jnp.mean(dxh, -1, keepdims=True)
        m2 = jnp.mean(dxh * xhat, -1, keepdims=True)
        dr = rstd * (dxh - m1 - xhat * m2)
        dr_ref[...] = dr
        drb_ref[...] = dr.astype(BF16)
        dg_ref[...] += _fold8(dy_ * xhat)
        db_ref[...] += _fold8(dy_)

    extra = [] if tgt is None else [tgt]
    return pl.pallas_call(
        body, name=name, grid=(T // tm,),
        in_specs=[row, row, pl.BlockSpec((1, D), lambda i: (0, 0))] + [row] * len(extra),
        out_specs=[row, row, acc, acc] + [acc] * len(extra),
        out_shape=[jax.ShapeDtypeStruct((T, D), F32), jax.ShapeDtypeStruct((T, D), BF16)]
        + [jax.ShapeDtypeStruct((8, D), F32)] * (2 + len(extra)),
        compiler_params=_params(40, 1),
    )(*_hbm(dy, r, g, *extra))


def _rope(x, c, s1, s2):
    return x * c + pltpu.roll(x, 112, 1) * s1 + pltpu.roll(x, 16, 1) * s2


def _rope_t(dy, c, s1, s2):
    return dy * c + pltpu.roll(dy * s1, 16, 1) + pltpu.roll(dy * s2, 112, 1)


def _rms(x, g):
    rstd = lax.rsqrt(jnp.mean(x * x, -1, keepdims=True) + EPS)
    xhat = x * rstd
    return xhat * g, xhat, rstd


def _mla_prep(z0, gq, gkv, wq, wk, wv, rc, rs1, rs2):
    T = z0.shape[0]
    tm = min(ROW_BLOCK, T)
    HW = HEADS * 128

    def body(cq_ref, ckv_ref, kr_ref, gq_ref, gkv_ref, wq_ref, wk_ref, wv_ref, c_ref, s1_ref, s2_ref,
             q_ref, k_ref, v_ref):
        nq = _rms(cq_ref[...], gq_ref[...])[0].astype(BF16)
        nkv = _rms(ckv_ref[...], gkv_ref[...])[0].astype(BF16)
        q = jnp.dot(nq, wq_ref[...], preferred_element_type=F32)
        k = jnp.dot(nkv, wk_ref[...], preferred_element_type=F32)
        v = jnp.dot(nkv, wv_ref[...], preferred_element_type=F32)
        c, s1, s2 = c_ref[...], s1_ref[...], s2_ref[...]
        kr = _rope(pltpu.roll(kr_ref[...], 64, 1), c, s1, s2)
        for h in range(HEADS):
            sl = slice(h * 128, (h + 1) * 128)
            q_ref[:, sl] = (_rope(q[:, sl], c, s1, s2) * QK_SCALE).astype(BF16)
            k_ref[:, sl] = (k[:, sl] + kr).astype(BF16)
        v_ref[...] = v.astype(BF16)

    full = lambda shape: pl.BlockSpec(shape, lambda i: (0, 0))
    tab = pl.BlockSpec((tm, 128), lambda i: (i, 0))
    return pl.pallas_call(
        body, name="mla_prep", grid=(T // tm,),
        in_specs=[pl.BlockSpec((tm, 256), lambda i: (i, 0)), pl.BlockSpec((tm, 256), lambda i: (i, 1)),
                  pl.BlockSpec((tm, 128), lambda i: (i, 12)), full((1, 256)), full((1, 256)),
                  full((256, HW)), full((256, HW)), full((256, 512)), tab, tab, tab],
        out_specs=[pl.BlockSpec((tm, HW), lambda i: (i, 0)), pl.BlockSpec((tm, HW), lambda i: (i, 0)),
                   pl.BlockSpec((tm, 512), lambda i: (i, 0))],
        out_shape=[jax.ShapeDtypeStruct((T, HW), BF16), jax.ShapeDtypeStruct((T, HW), BF16),
                   jax.ShapeDtypeStruct((T, 512), BF16)],
        compiler_params=_params(40, 1),
    )(z0, z0, z0, gq, gkv, wq, wk, wv, rc, rs1, rs2)


def _flash_fwd(q, k, v, plan=None):
    T = q.shape[0]
    bq = min(2 * ROW_BLOCK, T)
    nq = T // bq
    grid = (4, nq, nq)

    def body(*refs):
        (q_ref, k_ref, v_ref), (o_ref, lse_ref), (m_sc, acc_sc), pctx = _split_refs(refs, 3, 2, 2, plan)
        _plan_start(plan, pctx, grid)
        i, j = pl.program_id(1), pl.program_id(2)
        first = lax.broadcasted_iota(jnp.int32, (bq, 128), 1) < 64

        @pl.when(j == 0)
        def _():
            m_sc[...] = jnp.full_like(m_sc, -jnp.inf)
            acc_sc[...] = jnp.zeros_like(acc_sc)

        def tile(r0, nr, nc, masked):
            rs = slice(r0, r0 + nr)
            vp = v_ref[0:nc, :]
            lanes = first[0:nc, :]
            for h in range(2):
                sl = slice(h * 128, (h + 1) * 128)
                s = lax.dot_general(q_ref[rs, sl], k_ref[0:nc, sl], NT_DIMS, preferred_element_type=F32)
                if masked:
                    rows = r0 + lax.broadcasted_iota(jnp.int32, (nr, nc), 0)
                    cols = lax.broadcasted_iota(jnp.int32, (nr, nc), 1)
                    s = jnp.where(cols <= rows, s, -jnp.inf)
                m_prev = m_sc[h, rs, 0:1]
                m_new = jnp.maximum(m_prev, jnp.max(s, axis=1, keepdims=True))
                alpha = jnp.exp(m_prev - m_new)
                p = jnp.exp(s - m_new).astype(BF16)
                vh = jnp.where(lanes if h == 0 else jnp.logical_not(lanes), vp, jnp.ones_like(vp))
                acc_sc[h, rs, :] = acc_sc[h, rs, :] * alpha + jnp.dot(p, vh, preferred_element_type=F32)
                m_sc[h, rs, :] = jnp.broadcast_to(m_new, (nr, 128))

        @pl.when(j < i)
        def _():
            tile(0, bq, bq, False)

        @pl.when(j == i)
        def _():
            tile(0, bq, bq, True)
            a0, a1 = acc_sc[0], acc_sc[1]
            l0, l1 = pltpu.roll(a0, 64, 1), pltpu.roll(a1, 64, 1)
            o_ref[...] = jnp.where(first, a0 / l0, a1 / l1).astype(BF16)
            lse_ref[...] = jnp.where(first, m_sc[0] + jnp.log(l0), m_sc[1] + jnp.log(l1))

        _plan_wait(plan, pctx, grid)

    kv = lambda hp, i, j: (jnp.minimum(i, j), hp)
    p_in, p_ospec, p_oshape, p_scr, p_alias = _plan_io(plan, 3, 2)
    return pl.pallas_call(
        body, name="flash_fwd", grid=grid,
        in_specs=[pl.BlockSpec((bq, 256), lambda hp, i, j: (i, hp)), pl.BlockSpec((bq, 256), kv),
                  pl.BlockSpec((bq, 128), kv)] + [_ANY] * len(p_in),
        out_specs=[pl.BlockSpec((bq, 128), lambda hp, i, j: (i, hp)),
                   pl.BlockSpec((bq, 128), lambda hp, i, j: (i, hp))] + p_ospec,
        out_shape=[jax.ShapeDtypeStruct((T, 512), BF16), jax.ShapeDtypeStruct((T, 512), F32)] + p_oshape,
        scratch_shapes=[pltpu.VMEM((2, bq, 128), F32), pltpu.VMEM((2, bq, 128), F32)] + p_scr,
        input_output_aliases=p_alias, compiler_params=_params(56, 3),
    )(*_hbm(q, k, v), *p_in)


def _attn_delta(dmix, o):
    T = o.shape[0]
    tm = min(ROW_BLOCK, T)
    blk = pl.BlockSpec((tm, 512), lambda i: (i, 0))

    def body(do_ref, o_ref, delta_ref, dob_ref):
        first = lax.broadcasted_iota(jnp.int32, (tm, 128), 1) < 64
        for hp in range(4):
            sl = slice(hp * 128, (hp + 1) * 128)
            prod = do_ref[:, sl] * o_ref[:, sl].astype(F32)
            d0 = jnp.sum(jnp.where(first, prod, 0.0), axis=1, keepdims=True)
            d1 = jnp.sum(jnp.where(first, 0.0, prod), axis=1, keepdims=True)
            delta_ref[:, sl] = jnp.where(first, d0, d1)
        dob_ref[...] = do_ref[...].astype(BF16)

    return pl.pallas_call(
        body, name="attn_delta", grid=(T // tm,), in_specs=[blk, blk], out_specs=[blk, blk],
        out_shape=[jax.ShapeDtypeStruct((T, 512), F32), jax.ShapeDtypeStruct((T, 512), BF16)],
        compiler_params=_params(32, 1),
    )(dmix, o)


def _flash_bwd(q, k, v, do_b, lse, delta, plan=None):
    T = q.shape[0]
    bq = min(2 * ROW_BLOCK, T)
    nq = T // bq
    grid = (4, nq, nq)

    def body(*refs):
        ((q_ref, k_ref, v_ref, do_ref, lse_ref, dl_ref), (dq_hbm, dk_ref, dv_ref), (dq_sc, dk_sc, dv_sc, sem),
         pctx) = _split_refs(refs, 6, 3, 4, plan)
        _plan_start(plan, pctx, grid)
        hp, j, i = pl.program_id(0), pl.program_id(1), pl.program_id(2)
        first = lax.broadcasted_iota(jnp.int32, (bq, 128), 1) < 64

        @pl.when((j == 0) & (i == 0))
        def _():
            dq_sc[...] = jnp.zeros_like(dq_sc)

        @pl.when(i == j)
        def _():
            dk_sc[...] = jnp.zeros_like(dk_sc)
            dv_sc[...] = jnp.zeros_like(dv_sc)

        def tile(r0, nr, nc, masked):
            rs, cs = slice(r0, r0 + nr), slice(0, nc)
            vp = v_ref[cs, :]
            do = do_ref[rs, :]
            lanes = first[rs, :]
            for h in range(2):
                sl = slice(h * 128, (h + 1) * 128)
                qh, kh = q_ref[rs, sl], k_ref[cs, sl]
                s = lax.dot_general(qh, kh, NT_DIMS, preferred_element_type=F32)
                p = jnp.exp(s - lse_ref[rs, h * 64:h * 64 + 1])
                if masked:
                    rows = r0 + lax.broadcasted_iota(jnp.int32, (nr, nc), 0)
                    cols = lax.broadcasted_iota(jnp.int32, (nr, nc), 1)
                    p = jnp.where(cols <= rows, p, 0.0)
                do_h = jnp.where(lanes if h == 0 else jnp.logical_not(lanes), do, jnp.zeros_like(do))
                dv_sc[cs, :] += lax.dot_general(p.astype(BF16), do_h, TN_DIMS, preferred_element_type=F32)
                dp = lax.dot_general(do_h, vp, NT_DIMS, preferred_element_type=F32)
                ds = (p * (dp - dl_ref[rs, h * 64:h * 64 + 1])).astype(BF16)
                dq_sc[i, rs, sl] += jnp.dot(ds, kh, preferred_element_type=F32)
                dk_sc[cs, sl] += lax.dot_general(ds, qh, TN_DIMS, preferred_element_type=F32)

        @pl.when(i > j)
        def _():
            tile(0, bq, bq, False)

        @pl.when(i == j)
        def _():
            tile(0, bq // 2, bq // 2, True)
            tile(bq // 2, bq // 2, bq, True)

        @pl.when(i == nq - 1)
        def _():
            dk_ref[...] = dk_sc[...]
            dv_ref[...] = dv_sc[...]

        @pl.when((j == nq - 1) & (i == nq - 1))
        def _():
            cp = pltpu.make_async_copy(dq_sc, dq_hbm.at[hp], sem)
            cp.start()
            cp.wait()

        _plan_wait(plan, pctx, grid)

    qi = lambda hp, j, i: (jnp.maximum(i, j), hp)
    kj = lambda hp, j, i: (j, hp)
    p_in, p_ospec, p_oshape, p_scr, p_alias = _plan_io(plan, 6, 3)
    return pl.pallas_call(
        body, name="flash_bwd", grid=grid,
        in_specs=[pl.BlockSpec((bq, 256), qi), pl.BlockSpec((bq, 256), kj), pl.BlockSpec((bq, 128), kj),
                  pl.BlockSpec((bq, 128), qi), pl.BlockSpec((bq, 128), qi), pl.BlockSpec((bq, 128), qi)]
        + [_ANY] * len(p_in),
        out_specs=[_ANY, pl.BlockSpec((bq, 256), kj), pl.BlockSpec((bq, 128), kj)] + p_ospec,
        out_shape=[jax.ShapeDtypeStruct((4, nq, bq, 256), F32), jax.ShapeDtypeStruct((T, 1024), F32),
                   jax.ShapeDtypeStruct((T, 512), F32)] + p_oshape,
        scratch_shapes=[pltpu.VMEM((nq, bq, 256), F32), pltpu.VMEM((bq, 256), F32), pltpu.VMEM((bq, 128), F32),
                        pltpu.SemaphoreType.DMA] + p_scr,
        input_output_aliases=p_alias, compiler_params=_params(56, 3),
    )(*_hbm(q, k, v, do_b, lse, delta), *p_in)


def _mla_bwd(z0, dq4, dk, dv, gq, gkv, wq, wk, wv, rc, rs1, rs2, plan=None):
    T = z0.shape[0]
    tm = min(ROW_BLOCK, T)
    HW = HEADS * 128
    grid = (T // tm,)
    dq4 = dq4.reshape(4, T, 256)

    def body(*refs):
        ((cq_ref, ckv_ref, dq_ref, dk_ref, dv_ref, gq_ref, gkv_ref, wq_ref, wk_ref, wv_ref, c_ref, s1_ref, s2_ref),
         (dc_ref, dkr_ref, dwq_ref, dwk_ref, dwv_ref, dgq_ref, dgkv_ref), _, pctx) = _split_refs(refs, 13, 7, 0, plan)
        _plan_start(plan, pctx, grid)

        @pl.when(pl.program_id(0) == 0)
        def _():
            for ref in (dwq_ref, dwk_ref, dwv_ref, dgq_ref, dgkv_ref):
                ref[...] = jnp.zeros_like(ref)

        c, s1, s2 = c_ref[...], s1_ref[...], s2_ref[...]
        lane = lax.broadcasted_iota(jnp.int32, (tm, 128), 1)
        nq, xq, rq = _rms(cq_ref[...], gq_ref[...])
        nkv, xkv, rkv = _rms(ckv_ref[...], gkv_ref[...])
        nq_b, nkv_b = nq.astype(BF16), nkv.astype(BF16)

        dq_parts, dk_parts = [], []
        dkr = jnp.zeros((tm, 128), F32)
        for h in range(HEADS):
            blk = dq_ref[h // 2, :, (h % 2) * 128:(h % 2 + 1) * 128] * QK_SCALE
            dq_parts.append(_rope_t(blk, c, s1, s2).astype(BF16))
            kb = dk_ref[:, h * 128:(h + 1) * 128]
            dk_parts.append(jnp.where(lane < NOPE, kb, 0.0).astype(BF16))
            dkr = dkr + kb
        dq_b = jnp.concatenate(dq_parts, axis=1)
        dk_b = jnp.concatenate(dk_parts, axis=1)
        dv_b = dv_ref[...].astype(BF16)

        dwq_ref[...] += lax.dot_general(nq_b, dq_b, TN_DIMS, preferred_element_type=F32)
        dwk_ref[...] += lax.dot_general(nkv_b, dk_b, TN_DIMS, preferred_element_type=F32)
        dwv_ref[...] += lax.dot_general(nkv_b, dv_b, TN_DIMS, preferred_element_type=F32)
        dnq = lax.dot_general(dq_b, wq_ref[...], NT_DIMS, preferred_element_type=F32)
        dnkv = (lax.dot_general(dk_b, wk_ref[...], NT_DIMS, preferred_element_type=F32)
                + lax.dot_general(dv_b, wv_ref[...], NT_DIMS, preferred_element_type=F32))

        def rms_bwd(dn, xhat, rstd, g):
            dxh = dn * g
            return rstd * (dxh - xhat * jnp.mean(dxh * xhat, -1, keepdims=True))

        dc_ref[:, :256] = rms_bwd(dnq, xq, rq, gq_ref[...]).astype(BF16)
        dc_ref[:, 256:] = rms_bwd(dnkv, xkv, rkv, gkv_ref[...]).astype(BF16)
        dgq_ref[...] += _fold8(dnq * xq)
        dgkv_ref[...] += _fold8(dnkv * xkv)
        dkr = pltpu.roll(_rope_t(dkr, c, s1, s2), 64, 1)
        dkr_ref[...] = jnp.where(lane < ROPE, dkr, 0.0).astype(BF16)
        _plan_wait(plan, pctx, grid)

    full = lambda shape: pl.BlockSpec(shape, lambda i: (0,) * len(shape))
    tab = pl.BlockSpec((tm, 128), lambda i: (i, 0))
    p_in, p_ospec, p_oshape, p_scr, p_alias = _plan_io(plan, 13, 7)
    return pl.pallas_call(
        body, name="mla_bwd", grid=grid,
        in_specs=[pl.BlockSpec((tm, 256), lambda i: (i, 0)), pl.BlockSpec((tm, 256), lambda i: (i, 1)),
                  pl.BlockSpec((4, tm, 256), lambda i: (0, i, 0)),
                  pl.BlockSpec((tm, HW), lambda i: (i, 0)), pl.BlockSpec((tm, 512), lambda i: (i, 0)),
                  full((1, 256)), full((1, 256)), full((256, HW)), full((256, HW)), full((256, 512)), tab, tab, tab]
        + [_ANY] * len(p_in),
        out_specs=[pl.BlockSpec((tm, 512), lambda i: (i, 0)), tab, full((256, HW)), full((256, HW)),
                   full((256, 512)), full((8, 256)), full((8, 256))] + p_ospec,
        out_shape=[jax.ShapeDtypeStruct((T, 512), BF16), jax.ShapeDtypeStruct((T, 128), BF16),
                   jax.ShapeDtypeStruct((256, HW), F32), jax.ShapeDtypeStruct((256, HW), F32),
                   jax.ShapeDtypeStruct((256, 512), F32), jax.ShapeDtypeStruct((8, 256), F32),
                   jax.ShapeDtypeStruct((8, 256), F32)] + p_oshape,
        scratch_shapes=p_scr, input_output_aliases=p_alias, compiler_params=_params(48, 1),
    )(*_hbm(z0, z0, dq4, dk, dv, gq, gkv, wq, wk, wv, rc, rs1, rs2), *p_in)


def _sgu_fwd(z0, a_out, ln_g, ln_b, w, b_t):
    T = z0.shape[0]
    tm = min(ROW_BLOCK, T)
    W = SGU_G * SGU_C

    def body(u_ref, v_ref, a_ref, g_ref, b_ref, w_ref, bt_ref, o_ref):
        o_ref[:, :W] = a_ref[...]
        ug = _gelu(u_ref[...])
        xhat, _ = _ln_stats(_gelu(v_ref[...]))
        vn = (xhat * g_ref[...] + b_ref[...]).astype(BF16)
        tril = lax.broadcasted_iota(jnp.int32, (SGU_C, SGU_C), 0) >= lax.broadcasted_iota(jnp.int32, (SGU_C, SGU_C), 1)
        for g in range(SGU_G):
            cs = slice(g * SGU_C, (g + 1) * SGU_C)
            wg = jnp.where(tril, w_ref[g], 0.0).astype(BF16)
            bcol = bt_ref[:, g:g + 1]
            for c in range(tm // SGU_C):
                rs = slice(c * SGU_C, (c + 1) * SGU_C)
                mixed = jnp.dot(wg, vn[rs, cs], preferred_element_type=F32) + bcol
                o_ref[rs, W + g * SGU_C:W + (g + 1) * SGU_C] = (ug[rs, cs] * mixed).astype(BF16)

    full = lambda shape: pl.BlockSpec(shape, lambda i: (0,) * len(shape))
    return pl.pallas_call(
        body, name="sgu_fwd", grid=(T // tm,),
        in_specs=[pl.BlockSpec((tm, W), lambda i: (i, 1)), pl.BlockSpec((tm, W), lambda i: (i, 2)),
                  pl.BlockSpec((tm, W), lambda i: (i, 0)),
                  full((1, W)), full((1, W)), full((SGU_G, SGU_C, SGU_C)), full((SGU_C, SGU_G))],
        out_specs=pl.BlockSpec((tm, 2 * W), lambda i: (i, 0)),
        out_shape=jax.ShapeDtypeStruct((T, 2 * W), BF16),
        compiler_params=_params(32, 1),
    )(z0, z0, a_out, ln_g, ln_b, w, b_t)


def _sgu_bwd(z0, dmix, dc, dkr, ln_g, ln_b, w, b_t):
    T = z0.shape[0]
    tm = min(ROW_BLOCK, T)
    W = SGU_G * SGU_C

    def body(u_ref, v_ref, do_ref, dc_ref, dkr_ref, g_ref, b_ref, w_ref, bt_ref, dz_ref, dw_ref, db_ref, dlg_ref,
             dlb_ref):
        @pl.when(pl.program_id(0) == 0)
        def _():
            for ref in (dw_ref, db_ref, dlg_ref, dlb_ref):
                ref[...] = jnp.zeros_like(ref)

        dz_ref[:, :W] = dc_ref[...]
        dz_ref[:, 3 * W:] = dkr_ref[...]

        u, v, dout = u_ref[...], v_ref[...], do_ref[...]
        ug = _gelu(u)
        xhat, rstd = _ln_stats(_gelu(v))
        vn = (xhat * g_ref[...] + b_ref[...]).astype(BF16)
        dmixed = dout * ug
        dmixed_b = dmixed.astype(BF16)
        tril = lax.broadcasted_iota(jnp.int32, (SGU_C, SGU_C), 0) >= lax.broadcasted_iota(jnp.int32, (SGU_C, SGU_C), 1)
        lane = lax.broadcasted_iota(jnp.int32, (SGU_C, SGU_C), 1)
        dvn_cols = []
        for g in range(SGU_G):
            cs = slice(g * SGU_C, (g + 1) * SGU_C)
            wg = jnp.where(tril, w_ref[g], 0.0).astype(BF16)
            bcol = bt_ref[:, g:g + 1]
            dw_g = jnp.zeros((SGU_C, SGU_C), F32)
            db_g = jnp.zeros((SGU_C, 1), F32)
            dvn_rows = []
            for c in range(tm // SGU_C):
                rs = slice(c * SGU_C, (c + 1) * SGU_C)
                mixed = jnp.dot(wg, vn[rs, cs], preferred_element_type=F32) + bcol
                dz_ref[rs, W + g * SGU_C:W + (g + 1) * SGU_C] = (dout[rs, cs] * mixed * _gelu_grad(u[rs, cs])).astype(BF16)
                dm = dmixed_b[rs, cs]
                dw_g = dw_g + lax.dot_general(dm, vn[rs, cs], NT_DIMS, preferred_element_type=F32)
                db_g = db_g + jnp.sum(dmixed[rs, cs], axis=1, keepdims=True)
                dvn_rows.append(lax.dot_general(wg, dm, TN_DIMS, preferred_element_type=F32))
            dw_ref[g] += jnp.where(tril, dw_g, 0.0)
            db_ref[...] += jnp.where(lane == g, db_g, 0.0)
            dvn_cols.append(jnp.concatenate(dvn_rows, axis=0))
        dvn = jnp.concatenate(dvn_cols, axis=1)
        dxh = dvn * g_ref[...]
        m1 = jnp.mean(dxh, -1, keepdims=True)
        m2 = jnp.mean(dxh * xhat, -1, keepdims=True)
        dvg = rstd * (dxh - m1 - xhat * m2)
        dz_ref[:, 2 * W:3 * W] = (dvg * _gelu_grad(v)).astype(BF16)
        dlg_ref[...] += _fold8(dvn * xhat)
        dlb_ref[...] += _fold8(dvn)

    full = lambda shape: pl.BlockSpec(shape, lambda i: (0,) * len(shape))
    return pl.pallas_call(
        body, name="sgu_bwd", grid=(T // tm,),
        in_specs=[pl.BlockSpec((tm, W), lambda i: (i, 1)), pl.BlockSpec((tm, W), lambda i: (i, 2)),
                  pl.BlockSpec((tm, W), lambda i: (i, 1)), pl.BlockSpec((tm, W), lambda i: (i, 0)),
                  pl.BlockSpec((tm, 128), lambda i: (i, 0)),
                  full((1, W)), full((1, W)), full((SGU_G, SGU_C, SGU_C)), full((SGU_C, SGU_G))],
        out_specs=[pl.BlockSpec((tm, 3 * W + 128), lambda i: (i, 0)), full((SGU_G, SGU_C, SGU_C)),
                   full((SGU_C, SGU_C)), full((8, W)), full((8, W))],
        out_shape=[jax.ShapeDtypeStruct((T, 3 * W + 128), BF16), jax.ShapeDtypeStruct((SGU_G, SGU_C, SGU_C), F32),
                   jax.ShapeDtypeStruct((SGU_C, SGU_C), F32), jax.ShapeDtypeStruct((8, W), F32),
                   jax.ShapeDtypeStruct((8, W), F32)],
        compiler_params=_params(40, 1),
    )(z0, z0, dmix, dc, dkr, ln_g, ln_b, w, b_t)


def _hg_lower_bound(lb_ref):
    a0, a1 = lb_ref[0:1, :], lb_ref[1:2, :]
    m = jnp.maximum(a0, a1)
    e0, e1 = jnp.exp(a0 - m), jnp.exp(a1 - m)
    return e1 / (e0 + e1)


def _running_sum(x, reverse=False):
    n = x.shape[0]
    row = lax.broadcasted_iota(jnp.int32, x.shape, 0)
    s = 1
    while s < n:
        if reverse:
            x = x + jnp.where(row < n - s, pltpu.roll(x, n - s, 0), 0.0)
        else:
            x = x + jnp.where(row >= s, pltpu.roll(x, s, 0), 0.0)
        s *= 2
    return x


def _hg_chunk(qc, fc, lb):
    C = HG_CHUNK
    rows = lax.broadcasted_iota(jnp.int32, (C, C), 0)
    cols = lax.broadcasted_iota(jnp.int32, (C, C), 1)
    rowid = lax.broadcasted_iota(jnp.int32, (C, 128), 0)
    sq, sg = _sigmoid(qc), _sigmoid(fc)
    qf = qc * sq
    gate = lb + (1.0 - lb) * sg
    kk = 1.0 - gate
    lg = jnp.log(gate)
    bcum = _running_sum(lg)
    b_mid = jnp.sum(jnp.where(rowid < C // 2, lg, 0.0), axis=0, keepdims=True)
    b_last = jnp.sum(lg, axis=0, keepdims=True)
    eq, ek, e, eh = jnp.exp(bcum - b_mid), jnp.exp(b_mid - bcum), jnp.exp(bcum), jnp.exp(b_last - bcum)
    qt, kt, qe, khat = qf * eq, kk * ek, qf * e, kk * eh
    a = lax.dot_general(qt.astype(BF16), kt.astype(BF16), NT_DIMS, preferred_element_type=F32)
    a = jnp.where(rows >= cols, a, 0.0)
    return dict(sq=sq, sg=sg, gate=gate, kk=kk, eq=eq, ek=ek, e=e, eh=eh, qt=qt, kt=kt, qe=qe, khat=khat, a=a,
                e_last=jnp.exp(b_last), tril=rows >= cols, rowid=rowid)


def _hgrn_fwd(z4, hg_lb, gnorm):
    T = z4.shape[1]
    tb = min(ROW_BLOCK, T)
    C = HG_CHUNK
    ncb = tb // C
    HPB = HG_HEADS_PER_STEP

    def body(q_ref, f_ref, i_ref, g_ref, lb_ref, gn_ref, y_ref, o_ref, st_ref, st_sc):
        @pl.when(pl.program_id(1) == 0)
        def _():
            st_sc[...] = jnp.zeros_like(st_sc)

        def chunk(c, carry):
            rs = pl.ds(pl.multiple_of(c * C, C), C)
            for hh in range(HPB):
                hs = slice(hh * 128, (hh + 1) * 128)
                lb = _hg_lower_bound(lb_ref.at[:, hs])
                v_b = i_ref[rs, hs].astype(BF16)
                gc = g_ref[rs, hs]
                x = _hg_chunk(q_ref[rs, hs], f_ref[rs, hs], lb)
                st = st_sc[hh]
                st_ref[hh, c] = st
                o = (jnp.dot(x["a"].astype(BF16), v_b, preferred_element_type=F32)
                     + lax.dot_general(x["qe"].astype(BF16), st.astype(BF16), NT_DIMS, preferred_element_type=F32))
                st_sc[hh] = st * x["e_last"] + lax.dot_general(v_b, x["khat"].astype(BF16), TN_DIMS,
                                                               preferred_element_type=F32)
                o_ref[rs, hs] = o
                n = o * lax.rsqrt(jnp.mean(o * o, -1, keepdims=True) + EPS)
                y_ref[rs, hs] = (n * gn_ref[:, hs] * (gc * _sigmoid(gc))).astype(BF16)
            return carry

        lax.fori_loop(0, ncb, chunk, 0)

    W = 128 * HPB
    zb = lambda k: pl.BlockSpec((None, tb, W), lambda h, t: (k, t, h))
    out = pl.BlockSpec((tb, W), lambda h, t: (t, h))
    return pl.pallas_call(
        body, name="hgrn_fwd", grid=(HEADS // HPB, T // tb),
        in_specs=[zb(0), zb(1), zb(2), zb(3), pl.BlockSpec((2, W), lambda h, t: (0, h)),
                  pl.BlockSpec((1, W), lambda h, t: (0, h))],
        out_specs=[out, out, pl.BlockSpec((HPB, ncb, 128, 128), lambda h, t: (h, t, 0, 0))],
        out_shape=[jax.ShapeDtypeStruct((T, D), BF16), jax.ShapeDtypeStruct((T, D), F32),
                   jax.ShapeDtypeStruct((HEADS, T // C, 128, 128), F32)],
        scratch_shapes=[pltpu.VMEM((HPB, 128, 128), F32)],
        compiler_params=_params(48, 2),
    )(*_hbm(z4, z4, z4, z4, hg_lb, gnorm))


def _hgrn_bwd(z4, o_raw, dy, states, hg_lb, gnorm):
    T = z4.shape[1]
    tb = min(ROW_BLOCK, T)
    C = HG_CHUNK
    ncb = tb // C
    nt = T // tb
    HPB = HG_HEADS_PER_STEP

    def body(q_ref, f_ref, i_ref, g_ref, o_ref, dy_ref, st_ref, lb_ref, gn_ref, dz_ref, dlb_ref, dgn_ref, dst_sc):
        @pl.when(pl.program_id(1) == 0)
        def _():
            dst_sc[...] = jnp.zeros_like(dst_sc)
            dlb_ref[...] = jnp.zeros_like(dlb_ref)
            dgn_ref[...] = jnp.zeros_like(dgn_ref)

        def chunk(cc, carry):
            for hh in range(HPB):
                one_head(ncb - 1 - cc, hh, slice(hh * 128, (hh + 1) * 128))
            return carry

        def one_head(c, hh, hs):
            rs = pl.ds(pl.multiple_of(c * C, C), C)
            lb = _hg_lower_bound(lb_ref.at[:, hs])
            gn = gn_ref[:, hs]
            qc, gc = q_ref[rs, hs], g_ref[rs, hs]
            v_b = i_ref[rs, hs].astype(BF16)
            x = _hg_chunk(qc, f_ref[rs, hs], lb)
            st, dst = st_ref[hh, c], dst_sc[hh]
            st_b, dst_b = st.astype(BF16), dst.astype(BF16)
            o, dyc = o_ref[rs, hs], dy_ref[rs, hs]
            sgg = _sigmoid(gc)
            sil = gc * sgg
            rstd = lax.rsqrt(jnp.mean(o * o, -1, keepdims=True) + EPS)
            n = o * rstd
            dgn_ref[:, hs] += _fold8(dyc * n * sil)
            dn = dyc * gn * sil
            do = rstd * (dn - n * jnp.mean(dn * n, -1, keepdims=True))
            dg = dyc * n * gn * (sgg * (1.0 + gc * (1.0 - sgg)))
            do_b = do.astype(BF16)
            da = jnp.where(x["tril"], lax.dot_general(do_b, v_b, NT_DIMS, preferred_element_type=F32), 0.0).astype(BF16)
            qt_b, kt_b, qe_b, khat_b = (x[n_].astype(BF16) for n_ in ("qt", "kt", "qe", "khat"))
            dv = (lax.dot_general(x["a"].astype(BF16), do_b, TN_DIMS, preferred_element_type=F32)
                  + lax.dot_general(khat_b, dst_b, NT_DIMS, preferred_element_type=F32))
            dqt = jnp.dot(da, kt_b, preferred_element_type=F32)
            dqe = jnp.dot(do_b, st_b, preferred_element_type=F32)
            dkt = lax.dot_general(da, qt_b, TN_DIMS, preferred_element_type=F32)
            dkhat = jnp.dot(v_b, dst_b, preferred_element_type=F32)
            dst_sc[hh] = lax.dot_general(do_b, qe_b, TN_DIMS, preferred_element_type=F32) + dst * x["e_last"]
            de_last = jnp.sum(st * dst, axis=0, keepdims=True)
            dqf = dqt * x["eq"] + dqe * x["e"]
            dkk = dkt * x["ek"] + dkhat * x["eh"]
            dkh_kh = dkhat * x["khat"]
            db = dqt * qt_b.astype(F32) - dkt * kt_b.astype(F32) + dqe * x["qe"] - dkh_kh
            db_last = jnp.sum(dkh_kh, axis=0, keepdims=True) + de_last * x["e_last"]
            db = db + jnp.where(x["rowid"] == C - 1, db_last, 0.0)
            dlg = _running_sum(db, reverse=True)
            dgate = dlg / x["gate"] - dkk
            sg, sq = x["sg"], x["sq"]
            dlb_ref[:, hs] += _fold8(dgate * (1.0 - sg)) * (lb * (1.0 - lb))
            dz_ref[0, rs, hs] = (dqf * (sq * (1.0 + qc * (1.0 - sq)))).astype(BF16)
            dz_ref[1, rs, hs] = (dgate * (1.0 - lb) * sg * (1.0 - sg)).astype(BF16)
            dz_ref[2, rs, hs] = dv.astype(BF16)
            dz_ref[3, rs, hs] = dg.astype(BF16)

        lax.fori_loop(0, ncb, chunk, 0)

    W = 128 * HPB
    zb = lambda k: pl.BlockSpec((None, tb, W), lambda h, t: (k, nt - 1 - t, h))
    blk = pl.BlockSpec((tb, W), lambda h, t: (nt - 1 - t, h))
    acc = pl.BlockSpec((8, W), lambda h, t: (0, h))
    return pl.pallas_call(
        body, name="hgrn_bwd", grid=(HEADS // HPB, nt),
        in_specs=[zb(0), zb(1), zb(2), zb(3), blk, blk,
                  pl.BlockSpec((HPB, ncb, 128, 128), lambda h, t: (h, nt - 1 - t, 0, 0)),
                  pl.BlockSpec((2, W), lambda h, t: (0, h)), pl.BlockSpec((1, W), lambda h, t: (0, h))],
        out_specs=[pl.BlockSpec((4, tb, W), lambda h, t: (0, nt - 1 - t, h)), acc, acc],
        out_shape=[jax.ShapeDtypeStruct((4, T, D), BF16), jax.ShapeDtypeStruct((8, D), F32),
                   jax.ShapeDtypeStruct((8, D), F32)],
        scratch_shapes=[pltpu.VMEM((HPB, 128, 128), F32)],
        compiler_params=_params(48, 2),
    )(*_hbm(z4, z4, z4, z4, o_raw, dy, states, hg_lb, gnorm))


def _adamw(w, g, m, v, *, name):
    R, L = w.shape
    tr = R if R <= 512 else 512
    assert R % tr == 0
    blk = pl.BlockSpec((tr, L), lambda i: (i, 0))
    c1, c2 = 1.0 - B1 ** STEP, 1.0 - B2 ** STEP

    def body(w_ref, g_ref, m_ref, v_ref, d_ref, mo_ref, vo_ref):
        g_ = g_ref[...]
        m_ = B1 * m_ref[...] + (1.0 - B1) * g_
        v_ = B2 * v_ref[...] + (1.0 - B2) * (g_ * g_)
        d_ref[...] = -LR * ((m_ / c1) / (jnp.sqrt(v_ / c2) + ADAM_EPS) + WD * w_ref[...])
        mo_ref[...] = m_
        vo_ref[...] = v_

    sds = jax.ShapeDtypeStruct((R, L), F32)
    return pl.pallas_call(
        body, name=name, grid=(R // tr,), in_specs=[blk] * 4, out_specs=[blk] * 3, out_shape=[sds] * 3,
        compiler_params=_params(32, 1),
    )(w, g, m, v)


def _adamw_rows(w, m, v, gbufs, row0, *, name, plan=None):
    L, R, C = w.shape
    tr = 256
    assert R % tr == 0 and row0 % tr == 0 and len(gbufs) == L
    grid = (L, R // tr)
    blk = pl.BlockSpec((None, tr, C), lambda l, i: (l, i, 0))
    gblk = pl.BlockSpec((tr, C), lambda l, i: (row0 // tr + i, 0))
    c1, c2 = 1.0 - B1 ** STEP, 1.0 - B2 ** STEP

    def body(*refs):
        ins, (go_ref, d_ref, mo_ref, vo_ref), _, pctx = _split_refs(refs, 3 + L, 4, 0, plan)
        w_ref, m_ref, v_ref = ins[:3]
        g_refs = ins[3:]
        _plan_start(plan, pctx, grid)
        g_ = g_refs[0][...]
        for l in range(1, L):
            g_ = jnp.where(pl.program_id(0) == l, g_refs[l][...], g_)
        m_ = B1 * m_ref[...] + (1.0 - B1) * g_
        v_ = B2 * v_ref[...] + (1.0 - B2) * (g_ * g_)
        go_ref[...] = g_
        d_ref[...] = -LR * ((m_ / c1) / (jnp.sqrt(v_ / c2) + ADAM_EPS) + WD * w_ref[...])
        mo_ref[...] = m_
        vo_ref[...] = v_
        _plan_wait(plan, pctx, grid)

    sds = jax.ShapeDtypeStruct((L, R, C), F32)
    p_in, p_ospec, p_oshape, p_scr, p_alias = _plan_io(plan, 3 + L, 4)
    return pl.pallas_call(
        body, name=name, grid=grid, in_specs=[blk] * 3 + [gblk] * L + [_ANY] * len(p_in),
        out_specs=[blk] * 4 + p_ospec, out_shape=[sds] * 4 + p_oshape, scratch_shapes=p_scr,
        input_output_aliases=p_alias, compiler_params=_params(32, 2),
    )(*_hbm(w, m, v, *gbufs), *p_in)


def _add_pairs(g, theirs, ids, *, name):
    n, R, L = theirs.shape
    tr = math.gcd(R, 128)
    nb = R // tr

    def body(ids_ref, a_ref, b_ref, o_ref):
        o_ref[...] = (a_ref[...].astype(F32) + b_ref[...].astype(F32)).astype(BF16)

    blk = pl.BlockSpec((n, tr, L), lambda i, ids: (0, i, 0))
    return pl.pallas_call(
        body, name=name, out_shape=jax.ShapeDtypeStruct((n, R, L), BF16),
        grid_spec=pltpu.PrefetchScalarGridSpec(
            num_scalar_prefetch=1, grid=(nb,),
            in_specs=[pl.BlockSpec((n, tr, L), lambda i, ids: (0, ids[1] * nb + i, 0)), blk], out_specs=blk),
        compiler_params=_params(16, 1),
    )(ids, g, theirs)


def _sum_chips(pair, parts, ids, *, name):
    _, R, L = parts.shape
    tr = math.gcd(R, 128)

    def body(ids_ref, o_ref, r_ref, out_ref):
        out_ref[...] = ((o_ref[...].astype(F32) + r_ref[0].astype(F32)) + r_ref[1].astype(F32)) + r_ref[2].astype(F32)

    return pl.pallas_call(
        body, name=name, out_shape=jax.ShapeDtypeStruct((2, R, L), F32),
        grid_spec=pltpu.PrefetchScalarGridSpec(
            num_scalar_prefetch=1, grid=(R // tr,),
            in_specs=[pl.BlockSpec((None, tr, L), lambda i, ids: (ids[0], i, 0)),
                      pl.BlockSpec((3, tr, L), lambda i, ids: (0, i, 0))],
            out_specs=pl.BlockSpec((None, tr, L), lambda i, ids: (ids[1], i, 0))),
        compiler_params=_params(32, 1),
    )(ids, pair, parts)


def _mesh_ids():
    x, y, c = _mesh_pos()
    return jnp.stack([2 * x + y, c]).astype(jnp.int32)


def _place_shard(rows, ids, *, name):
    R, L = rows.shape
    tr = 128

    def body(ids_ref, in_ref, out_ref):
        out_ref[...] = in_ref[...].astype(BF16)

    return pl.pallas_call(
        body, name=name, out_shape=jax.ShapeDtypeStruct((4, R, L), BF16),
        grid_spec=pltpu.PrefetchScalarGridSpec(
            num_scalar_prefetch=1, grid=(R // tr,), in_specs=[pl.BlockSpec((tr, L), lambda i, ids: (i, 0))],
            out_specs=pl.BlockSpec((None, tr, L), lambda i, ids: (ids[0], i, 0))),
        compiler_params=_params(16, 1),
    )(ids, rows)


def _remote(src, dst, send_sem, recv_sem, to):
    return pltpu.make_async_remote_copy(src_ref=src, dst_ref=dst, send_sem=send_sem, recv_sem=recv_sem,
                                        device_id=to, device_id_type=MESH_IDS)


def _rows(ref, lead, start, size):
    return ref.at[tuple(pl.ds(0, n) for n in ref.shape[:lead]) + (pl.ds(start, size),)]


def _other_chips():
    x, y, _ = _mesh_pos()
    return [(1 - x, y), (x, 1 - y), (1 - x, 1 - y)]


def _plan_gather_ici(bufs):
    n = len(bufs)

    def copies(outs, send, recv):
        x, y, c = _mesh_pos()
        res = []
        for b in range(n):
            half = bufs[b].shape[1] // 2
            mine = _rows(outs[b].at[2 * x + y], 0, c * half, half)
            for j, (cx, cy) in enumerate(_other_chips()):
                res.append((_remote(mine, mine, send(3 * b + j), recv(3 * b + j), (cx, cy, c)),
                            _remote(mine, _rows(outs[b].at[2 * cx + cy], 0, c * half, half),
                                    send(3 * b + j), recv(3 * b + j), (x, y, c))))
        return res

    def start(ins, outs, send, recv, loc):
        for out_cp, _ in copies(outs, send, recv):
            out_cp.start()

    def wait(ins, outs, send, recv, loc):
        for out_cp, in_cp in copies(outs, send, recv):
            in_cp.wait_recv()
            out_cp.wait_send()

    outs = [jax.ShapeDtypeStruct(b.shape, b.dtype) for b in bufs]
    return _Plan(bufs, outs, 3 * n, 0, start, wait, aliases={b: b for b in range(n)})


def _plan_gather_forward(bufs):
    n = len(bufs)

    def copies(outs, send, recv):
        x, y, c = _mesh_pos()
        res = []
        for b in range(n):
            half = bufs[b].shape[1] // 2
            for j, (cx, cy) in enumerate(_other_chips()):
                slot = outs[b].at[2 * cx + cy]
                res.append((_remote(_rows(slot, 0, c * half, half), _rows(slot, 0, c * half, half),
                                    send(3 * b + j), recv(3 * b + j), (x, y, 1 - c)),
                            _remote(_rows(slot, 0, c * half, half), _rows(slot, 0, (1 - c) * half, half),
                                    send(3 * b + j), recv(3 * b + j), (x, y, c))))
        return res

    def start(ins, outs, send, recv, loc):
        for out_cp, _ in copies(outs, send, recv):
            out_cp.start()

    def wait(ins, outs, send, recv, loc):
        for out_cp, in_cp in copies(outs, send, recv):
            in_cp.wait_recv()
            out_cp.wait_send()

    outs = [jax.ShapeDtypeStruct(b.shape, b.dtype) for b in bufs]
    return _Plan(bufs, outs, 3 * n, 0, start, wait, aliases={b: b for b in range(n)})


def _plan_pair_swap(g):
    half = g.shape[1] // 2

    def copy(ins, outs, send, recv, loc):
        x, y, c = _mesh_pos()
        return _remote(_rows(ins[0], 1, (1 - c) * half, half), outs[0], send(0), recv(0), (x, y, 1 - c))

    return _Plan([g], [jax.ShapeDtypeStruct((4, half, g.shape[2]), g.dtype)], 1, 0,
                 lambda *a: copy(*a).start(), lambda *a: copy(*a).wait())


def _plan_pair_gather(buf):
    def copies(ins, outs, send, recv, loc):
        x, y, c = _mesh_pos()
        return (_remote(outs[0].at[c], outs[0].at[c], send(0), recv(0), (x, y, 1 - c)),
                _remote(outs[0].at[c], outs[0].at[1 - c], send(0), recv(0), (x, y, c)))

    def wait(*a):
        out_cp, in_cp = copies(*a)
        in_cp.wait_recv()
        out_cp.wait_send()

    return _Plan([buf], [jax.ShapeDtypeStruct(buf.shape, buf.dtype)], 1, 0, lambda *a: copies(*a)[0].start(), wait,
                 aliases={0: 0})


def _plan_chip_scatter(p):
    def copies(ins, outs, send, recv, loc):
        _, _, c = _mesh_pos()
        return [_remote(ins[0].at[2 * cx + cy], outs[0].at[j], send(j), recv(j), (cx, cy, c))
                for j, (cx, cy) in enumerate(_other_chips())]

    def start(*a):
        for cp in copies(*a):
            cp.start()

    def wait(*a):
        for cp in copies(*a):
            cp.wait()

    return _Plan([p], [jax.ShapeDtypeStruct((3,) + p.shape[1:], p.dtype)], 3, 0, start, wait)


def _plan_exchange_all(vec):
    def copies(ins, outs, send, recv, loc):
        x, y, c = _mesh_pos()
        return [_remote(ins[0], outs[0].at[r - 1], send(r - 1), recv(r - 1), (x ^ (r >> 2), y ^ ((r >> 1) & 1), c ^ (r & 1)))
                for r in range(1, 8)]

    def start(*a):
        for cp in copies(*a):
            cp.start()

    def wait(*a):
        for cp in copies(*a):
            cp.wait()

    return _Plan([vec], [jax.ShapeDtypeStruct((7,) + vec.shape, vec.dtype)], 7, 0, start, wait)


SMALL_LAYOUT = {
    "mla_gq": (0, 1, 256, (1, 256)), "mla_gkv": (1, 1, 256, (1, 256)), "sgu_ln_g": (2, 1, 512, (1, 512)),
    "sgu_ln_b": (3, 1, 512, (1, 512)), "sgu_w": (4, 64, 1024, (64, 1024)), "sgu_b": (68, 1, 512, (1, 512)),
    "hg_lb": (69, 2, 1024, (2, 1024)), "hg_gnorm": (71, 1, 1024, (1, 256)), "ln1_g": (72, 2, 1024, (2, 1024)),
    "ln1_b": (74, 2, 1024, (2, 1024)), "ln2_g": (76, 2, 1024, (2, 1024)), "ln2_b": (78, 2, 1024, (2, 1024)),
}


def _small_pack(dgq, dgkv, dslg, dslb, dsw, dsb, dlb, dgn, ln_parts, sq_err):
    flat_ln = [p for pair in ln_parts for p in pair]

    def body(*refs):
        gq_ref, gkv_ref, slg_ref, slb_ref, sw_ref, sb_ref, lb_ref, gn_ref = refs[:8]
        ln_refs, err_ref, out_ref, t_sc = refs[8:16], refs[16], refs[17], refs[18]
        s8 = lambda ref: jnp.sum(ref[...], axis=0, keepdims=True)
        out_ref[...] = jnp.zeros_like(out_ref)
        out_ref[0:1, 0:256] = s8(gq_ref)
        out_ref[1:2, 0:256] = s8(gkv_ref)
        out_ref[2:3, 0:512] = s8(slg_ref)
        out_ref[3:4, 0:512] = s8(slb_ref)
        out_ref[4:68, :] = sw_ref[...]
        t_sc[...] = sb_ref[...].T
        for g in range(SGU_G):
            out_ref[68:69, g * SGU_C:(g + 1) * SGU_C] = t_sc[g:g + 1, :]
        d_lb1 = s8(lb_ref)
        out_ref[69:70, :] = -d_lb1
        out_ref[70:71, :] = d_lb1
        out_ref[71:72, :] = s8(gn_ref)
        for k, ref in enumerate(ln_refs):
            out_ref[72 + k:73 + k, :] = s8(ref)
        out_ref[0:1, 1023:1024] = jnp.sum(s8(err_ref), axis=1, keepdims=True) * (0.5 / D)

    vm = pl.BlockSpec(memory_space=pltpu.VMEM)
    return pl.pallas_call(
        body, name="small_grad_pack", in_specs=[vm] * 17, out_specs=vm,
        out_shape=jax.ShapeDtypeStruct((SMALL_ROWS, 1024), F32), scratch_shapes=[pltpu.VMEM((SGU_C, SGU_C), F32)],
        compiler_params=_params(16),
    )(dgq, dgkv, dslg, dslb, dsw.reshape(64, 1024), dsb, dlb, dgn, *flat_ln, sq_err)


def _small_update(vec, others, ids, w, m, v):
    names = list(SMALL_LAYOUT)
    n = len(names)
    c1, c2 = 1.0 - B1 ** STEP, 1.0 - B2 ** STEP
    have_others = others is not None

    def body(*refs):
        ids_ref, v_ref = refs[0], refs[1]
        k = 2 + have_others
        w_refs, m_refs, v_refs = refs[k:k + n], refs[k + n:k + 2 * n], refs[k + 2 * n:k + 3 * n]
        outs = refs[k + 3 * n:]
        row0_ref, tot_sc = outs[0], outs[-1]
        total = v_ref[...]
        if have_others:
            me = 2 * ids_ref[0] + ids_ref[1]
            total = None
            for d in range(8):
                rel = d ^ me
                term = jnp.where(rel == 0, v_ref[...], refs[2][jnp.maximum(rel - 1, 0)])
                total = term if total is None else total + term
        tot_sc[...] = total
        row0_ref[...] = tot_sc[0:1, :]
        for i, name in enumerate(names):
            r0, nr, width, _ = SMALL_LAYOUT[name]
            if name == "hg_gnorm":
                g_ = tot_sc[r0:r0 + 1, 0:256]
                for chip in range(1, 4):
                    g_ = jnp.where(ids_ref[0] == chip, tot_sc[r0:r0 + 1, chip * 256:(chip + 1) * 256], g_)
            else:
                g_ = tot_sc[r0:r0 + nr, 0:width]
            m_ = B1 * m_refs[i][...] + (1.0 - B1) * g_
            v_ = B2 * v_refs[i][...] + (1.0 - B2) * (g_ * g_)
            go, do, mo, vo = outs[1 + 4 * i:5 + 4 * i]
            go[...] = g_
            do[...] = -LR * ((m_ / c1) / (jnp.sqrt(v_ / c2) + ADAM_EPS) + WD * w_refs[i][...])
            mo[...] = m_
            vo[...] = v_

    full = lambda shape: pl.BlockSpec(shape, lambda i, ids, nd=len(shape): (0,) * nd)
    kshapes = [SMALL_LAYOUT[name][3] for name in names]
    operands = [vec] + ([others] if have_others else []) + [d[name] for d in (w, m, v) for name in names]
    out_shapes = [jax.ShapeDtypeStruct((1, 1024), F32)] + [jax.ShapeDtypeStruct(s, F32) for s in kshapes for _ in range(4)]
    res = pl.pallas_call(
        body, name="small_update", out_shape=out_shapes,
        grid_spec=pltpu.PrefetchScalarGridSpec(
            num_scalar_prefetch=1, grid=(1,), in_specs=[full(o.shape) for o in operands],
            out_specs=[full(s.shape) for s in out_shapes],
            scratch_shapes=[pltpu.VMEM((SMALL_ROWS, 1024), F32)]),
        compiler_params=_params(32, 1),
    )(ids, *operands)
    return res[0], {name: tuple(res[1 + 4 * i:5 + 4 * i]) for i, name in enumerate(names)}


ROWS_L1, ROWS_L0, ROWS_ODD, ROWS_ODD_W = 3328, 2048, 768, 384
ODD_PARTS = (("w_out_e", (256, 1024)), ("w_in_e", (1024, 392)), ("w_qb", (256, 192)), ("w_kvb", (256, 256)))
ODD_W_PARTS = tuple(p for p in ODD_PARTS if p[0] != "w_in_e")


def _odd_rows(parts, dtype, layout, total, gnorm=None):
    rows = [parts[n].reshape(-1, 1024).astype(dtype) for n, _ in layout]
    used = sum(r.shape[0] for r in rows)
    if gnorm is not None:
        bits = lax.bitcast_convert_type(gnorm.reshape(-1), BF16).reshape(1, 512)
        rows.append(jnp.pad(bits, ((0, 0), (0, 512))))
        used += 1
    rows.append(jnp.zeros((total - used, 1024), dtype))
    return jnp.concatenate(rows, axis=0)


def _odd_unrows(buf, layout, with_gnorm=False):
    out, off = {}, 0
    for n, shape in layout:
        nr = math.prod(shape) // 1024
        out[n] = buf[off:off + nr].reshape(shape)
        off += nr
    if with_gnorm:
        out["hg_gnorm"] = lax.bitcast_convert_type(buf[off, :512].reshape(256, 2), F32).reshape(1, 256)
    return out


def _rope_tables(positions):
    half = ROPE // 2
    inv_freq = ROPE_BASE ** (-jnp.arange(half, dtype=F32) / half)
    ang = positions.astype(F32).reshape(-1, 1) * inv_freq
    cos, sin = jnp.cos(ang), jnp.sin(ang)
    T = ang.shape[0]
    one, z16, z32 = jnp.ones((T, NOPE), F32), jnp.zeros((T, half), F32), jnp.zeros((T, 32), F32)
    z64 = jnp.zeros((T, NOPE), F32)
    c = jnp.concatenate([one, cos, cos, z32], axis=1)
    s1 = jnp.concatenate([z64, -sin, z16, z32], axis=1)
    s2 = jnp.concatenate([z64, z16, sin, z32], axis=1)
    return c, s1, s2


def _local_step(x, positions, tgt, odd, bufs, P, exchange):
    T = x.shape[0]
    row = lambda a: a.reshape(1, -1)
    rc, rs1, rs2 = _rope_tables(positions)
    blk = lambda f: pl.BlockSpec((None, D, D), f)

    w_in_e = odd["w_in_e"]
    w_in = jnp.concatenate([w_in_e[:, :512], w_in_e[:, 544:1568], w_in_e[:, 512:544], jnp.zeros((D, 96), BF16)], axis=1)
    wq = jnp.pad(odd["w_qb"].reshape(256, HEADS, NOPE + ROPE), ((0, 0), (0, 0), (0, 32))).reshape(256, HEADS * 128)
    kvb = odd["w_kvb"].reshape(256, HEADS, NOPE + VDIM)
    wk = jnp.pad(kvb[:, :, :NOPE], ((0, 0), (0, 0), (0, 64))).reshape(256, HEADS * 128)
    wv = kvb[:, :, NOPE:].reshape(256, HEADS * VDIM)
    w_out_e = odd["w_out_e"]
    sgu_w = P["sgu_w"][0]
    sgu_bt = P["sgu_b"][0].T
    gq, gkv = P["mla_gq"], P["mla_gkv"]
    gnorm = P["hg_gnorm"]

    z0 = _matmul(x, w_in, name="in_proj_e", M=T, N=1664, K=D, tn=1664)[0]
    q, k, v = _mla_prep(z0, gq, gkv, wq, wk, wv, rc, rs1, rs2)
    if exchange:
        ids = _mesh_ids()
        placed = [_place_shard(b, ids, name=f"place_shard_{l}") for l, b in enumerate(bufs)]
        a_out, lse, wga, wgb = _flash_fwd(q, k, v, plan=_plan_gather_ici(placed[:2]))
    else:
        a_out, lse = _flash_fwd(q, k, v)
        wga, wgb, wgc = bufs
    mix0 = _sgu_fwd(z0, a_out, P["sgu_ln_g"], P["sgu_ln_b"], sgu_w, sgu_bt)
    res = _proj_ln(mix0, w_out_e, x, row(P["ln1_g"][0]), row(P["ln1_b"][0]), name="out_proj_ln_e",
                   plan=_plan_gather_forward([wga, wgb]) if exchange else None)
    r1, h1, h1b = res[:3]
    if exchange:
        wga, wgb = res[3:]
    res = _ffn_ln(h1b, wga, h1, row(P["ln2_g"][0]), row(P["ln2_b"][0]), name="ffn_ln_0",
                  plan=_plan_gather_ici(placed[2:]) if exchange else None)
    ra0, r2, h2, h2b = res[:4]
    z4 = _matmul(h2b, wgb, name="in_proj_o", M=T, N=4 * D, K=D, b_spec=blk(lambda i, j, k: (j, 0, 0)),
                 out_shape=jax.ShapeDtypeStruct((4, T, D), F32),
                 o_spec=pl.BlockSpec((None, min(MM_ROWS, T), D), lambda i, j, k: (j, i, 0)))[0]
    y1, o_raw, states = _hgrn_fwd(z4, P["hg_lb"], gnorm)
    res2 = _proj_ln(y1, wgb, h2, row(P["ln1_g"][1]), row(P["ln1_b"][1]), name="out_proj_ln_o", w_rowblk=4,
                    plan=_plan_gather_forward([res[4]]) if exchange else None)
    r3, h3, h3b = res2[:3]
    if exchange:
        wgc = res2[3]
    ra1, r4, h4, _ = _ffn_ln(h3b, wgc, h3, row(P["ln2_g"][1]), row(P["ln2_b"][1]), name="ffn_ln_1")

    ln1_g, ln1_b, ln2_g, ln2_b = [None, None], [None, None], [None, None], [None, None]
    sq_err_parts = []

    def ffn_bwd(l, dh, r_out, ra, h_mid_b, g2, wg, rows, plan=None, tgt=None):
        dr, dr_b, dg, db, *sq_err = _ln_bwd(dh, r_out, row(g2), name=f"ln2_bwd_{l}", tgt=tgt)
        sq_err_parts.extend(sq_err)
        ln2_g[l], ln2_b[l] = dg, db
        da, *extra = _matmul(dr_b, wg, tb=True, mul=ra, out_dtype=BF16, name=f"ffn_da_{l}", M=T, N=4 * D, K=D,
                             b_spec=blk(lambda i, j, k: (j, 1, 0)), plan=plan)
        gbuf = _matmul(ra, dr_b, ta=True, a_sq=True, name=f"ffn_dw2_{l}", M=4 * D, N=D, K=T, tm=1024, tk=DW_TOKENS,
                       out_shape=jax.ShapeDtypeStruct((4, rows, D), BF16), o_spec=blk(lambda i, j, k: (i, 1, 0)))[0]
        gbuf = _matmul(h_mid_b, da, ta=True, name=f"ffn_dw1_{l}", M=D, N=4 * D, K=T, tm=1024, tk=DW_TOKENS, into=gbuf,
                       out_shape=jax.ShapeDtypeStruct((4, rows, D), BF16), o_spec=blk(lambda i, j, k: (j, 0, 0)))[0]
        dh_mid = _matmul(da, wg, tb=True, add=dr, add_scale=ALPHA, name=f"ffn_dh_{l}", M=T, N=D, K=4 * D,
                         b_spec=blk(lambda i, j, k: (k, 0, 0)))[0]
        return dh_mid, gbuf, extra

    dh3, g1, _ = ffn_bwd(1, h4, r4, ra1, h3b, P["ln2_g"][1], wgc, ROWS_L1, tgt=tgt)
    loss_parts = sq_err_parts[0]
    dr3, dr3_b, dg, db = _ln_bwd(dh3, r3, row(P["ln1_g"][1]), name="ln1_bwd_1")
    ln1_g[1], ln1_b[1] = dg, db
    g1_sds = jax.ShapeDtypeStruct((4, ROWS_L1, D), BF16)
    g1 = _matmul(y1, dr3_b, ta=True, name="dw_out_o", M=D, N=D, K=T, tm=256, tk=DW_TOKENS, into=g1, out_shape=g1_sds,
                 o_spec=pl.BlockSpec((None, 256, D), lambda i, j, k: (i, 12, 0)))[0]
    dmix1 = _matmul(dr3_b, wgb, tb=True, name="dmix_o", M=T, N=D, K=D, b_spec=_rows4_spec(4, 3), b_merge=(D, D))[0]
    dz4, dlb, dgn = _hgrn_bwd(z4, o_raw, dmix1, states, P["hg_lb"], gnorm)
    g1 = _matmul(h2b, dz4, ta=True, name="dw_in_o", M=D, N=4 * D, K=T, tm=1024, tk=DW_TOKENS, into=g1, out_shape=g1_sds,
                 b_spec=pl.BlockSpec((None, min(DW_TOKENS, T), D), lambda i, j, k: (j, k, 0)),
                 o_spec=blk(lambda i, j, k: (j, 2, 0)))[0]
    dh2 = _matmul(dz4, wgb, tb=True, add=dr3, add_scale=ALPHA, name="dh_in_o", M=T, N=D, K=4 * D,
                  a_spec=pl.BlockSpec((None, min(MM_ROWS, T), D), lambda i, j, k: (k, i, 0)),
                  b_spec=blk(lambda i, j, k: (k, 0, 0)))[0]

    dh1, g0, swapped1 = ffn_bwd(0, dh2, r2, ra0, h1b, P["ln2_g"][0], wga, ROWS_L0,
                                plan=_plan_pair_swap(g1) if exchange else None)
    dr1, dr1_b, dg, db = _ln_bwd(dh1, r1, row(P["ln1_g"][0]), name="ln1_bwd_0")
    ln1_g[0], ln1_b[0] = dg, db
    godd = {"w_out_e": _matmul(mix0, dr1_b, ta=True, name="dw_out_e", M=D, N=D, K=T, tm=1024, tk=DW_TOKENS)[0]}
    dmix0, *swapped0 = _matmul(dr1_b, w_out_e, tb=True, name="dmix_e", M=T, N=D, K=D,
                               plan=_plan_pair_swap(g0) if exchange else None)
    delta, do_b = _attn_delta(dmix0, a_out)
    if exchange:
        pair1 = _add_pairs(g1, swapped1[0], ids, name="grad_pair_add_1")
        pair0 = _add_pairs(g0, swapped0[0], ids, name="grad_pair_add_0")
        dq4, dk, dv, parts0, parts1 = _flash_bwd(
            q, k, v, do_b, lse, delta, plan=_join_plans([_plan_chip_scatter(pair0), _plan_chip_scatter(pair1)]))
        half0 = _sum_chips(pair0, parts0, ids, name="grad_chip_sum_0")
        half1 = _sum_chips(pair1, parts1, ids, name="grad_chip_sum_1")
        dc, dkr, dwq, dwk, dwv, dgq, dgkv, g0, g1 = _mla_bwd(
            z0, dq4, dk, dv, gq, gkv, wq, wk, wv, rc, rs1, rs2,
            plan=_join_plans([_plan_pair_gather(half0), _plan_pair_gather(half1)]))
        g0, g1 = g0.reshape(ROWS_L0, D), g1.reshape(ROWS_L1, D)
    else:
        dq4, dk, dv = _flash_bwd(q, k, v, do_b, lse, delta)
        dc, dkr, dwq, dwk, dwv, dgq, dgkv = _mla_bwd(z0, dq4, dk, dv, gq, gkv, wq, wk, wv, rc, rs1, rs2)
    dz0, dsw, dsb, dslg, dslb = _sgu_bwd(z0, dmix0, dc, dkr, P["sgu_ln_g"], P["sgu_ln_b"], sgu_w, sgu_bt)
    small_vec = _small_pack(dgq, dgkv, dslg, dslb, dsw, dsb, dlb, dgn, [ln1_g, ln1_b, ln2_g, ln2_b], loss_parts)
    dw_in, *small_others = _matmul(x, dz0, ta=True, name="dw_in_e", M=D, N=1664, K=T, tm=1024, tn=1664,
                                   tk=DW_TOKENS // 2, plan=_plan_exchange_all(small_vec) if exchange else None)
    godd["w_in_e"] = jnp.concatenate([dw_in[:, :512], dw_in[:, 1536:1568], dw_in[:, 512:1536]], axis=1)
    godd["w_qb"] = dwq.reshape(256, HEADS, 128)[:, :, :NOPE + ROPE].reshape(256, HEADS * (NOPE + ROPE))
    godd["w_kvb"] = jnp.concatenate([dwk.reshape(256, HEADS, 128)[:, :, :NOPE], dwv.reshape(256, HEADS, VDIM)],
                                    axis=2).reshape(256, HEADS * (NOPE + VDIM))
    odd_plan = None
    if exchange:
        by_chip = [_odd_rows({"w_out_e": jnp.split(godd["w_out_e"], 4, axis=0)[j],
                              **{n: jnp.split(godd[n], 4, axis=1)[j] for n in ("w_qb", "w_kvb")}}, BF16,
                             ODD_W_PARTS, ROWS_ODD_W)
                   for j in range(4)]
        bufs_odd = [godd["w_in_e"].reshape(D, 4, 392).transpose(1, 0, 2).astype(BF16), jnp.stack(by_chip)]
        theirs = _run_plan(_join_plans([_plan_pair_swap(b) for b in bufs_odd]), name="odd_pair_swap")
        odd_pairs = [_add_pairs(b, t, ids, name=f"odd_pair_add_{k}") for k, (b, t) in enumerate(zip(bufs_odd, theirs))]
        odd_plan = _join_plans([_plan_chip_scatter(p) for p in odd_pairs])
    grad_x, *odd_parts = _matmul(dz0, w_in, tb=True, add=dr1, add_scale=ALPHA, name="dx", M=T, N=D, K=1664, tk=1664,
                                 plan=odd_plan)
    if exchange:
        godd = (odd_pairs, odd_parts)
    return grad_x, g0, g1, godd, small_vec, (small_others[0] if exchange else None)


WEIGHTS = ['w_in_e', 'mla_gq', 'mla_gkv', 'w_qb', 'w_kvb', 'sgu_ln_g', 'sgu_ln_b', 'sgu_w', 'sgu_b', 'w_out_e',
           'w_in_o', 'hg_lb', 'hg_gnorm', 'w_out_o', 'ln1_g', 'ln1_b', 'w_ff1', 'w_ff2', 'ln2_g', 'ln2_b']


def kernel(x, positions, w_in_e, mla_gq, mla_gkv, w_qb, w_kvb, sgu_ln_g, sgu_ln_b, sgu_w, sgu_b, w_out_e, w_in_o, hg_lb, hg_gnorm, w_out_o, ln1_g, ln1_b, w_ff1, w_ff2, ln2_g, ln2_b, loss_target, m_w_in_e, m_mla_gq, m_mla_gkv, m_w_qb, m_w_kvb, m_sgu_ln_g, m_sgu_ln_b, m_sgu_w, m_sgu_b, m_w_out_e, m_w_in_o, m_hg_lb, m_hg_gnorm, m_w_out_o, m_ln1_g, m_ln1_b, m_w_ff1, m_w_ff2, m_ln2_g, m_ln2_b, v_w_in_e, v_mla_gq, v_mla_gkv, v_w_qb, v_w_kvb, v_sgu_ln_g, v_sgu_ln_b, v_sgu_w, v_sgu_b, v_w_out_e, v_w_in_o, v_hg_lb, v_hg_gnorm, v_w_out_o, v_ln1_g, v_ln1_b, v_w_ff1, v_w_ff2, v_ln2_g, v_ln2_b):
    args = dict(locals())
    w = {n: args[n] for n in WEIGHTS}
    m = {n: args["m_" + n] for n in WEIGHTS}
    v = {n: args["v_" + n] for n in WEIGHTS}
    cx, cy, cc = _mesh_pos()
    chip = 2 * cx + cy

    odd_shard = _odd_rows({"w_out_e": w_out_e[0], "w_qb": w_qb[0], "w_kvb": w_kvb[0]}, BF16, ODD_W_PARTS, ROWS_ODD_W,
                          gnorm=hg_gnorm)
    ids = _mesh_ids()
    placed = [_place_shard(w_in_e[0], ids, name="place_shard_in_e"), _place_shard(odd_shard, ids, name="place_shard_odd")]
    gathered = _run_plan(_plan_gather_forward(_run_plan(_plan_gather_ici(placed), name="odd_gather")),
                         name="odd_gather_forward")
    per_chip = [_odd_unrows(gathered[1][j], ODD_W_PARTS, with_gnorm=True) for j in range(4)]
    odd = {"w_out_e": jnp.concatenate([p["w_out_e"] for p in per_chip], axis=0),
           "w_in_e": jnp.concatenate([gathered[0][j] for j in range(4)], axis=1)}
    for n in ("w_qb", "w_kvb"):
        odd[n] = jnp.concatenate([p[n] for p in per_chip], axis=1)
    small = {n: w[n] for n in SMALL_LAYOUT if n != "hg_gnorm"}
    small["hg_gnorm"] = jnp.concatenate([p["hg_gnorm"] for p in per_chip], axis=1)
    shard_rows = (jnp.concatenate([w_ff1[0], w_ff2[0]], axis=0).astype(BF16),
                  jnp.concatenate([w_in_o[0], w_out_o[0]], axis=0).astype(BF16),
                  jnp.concatenate([w_ff1[1], w_ff2[1]], axis=0).astype(BF16))

    grad_x, g_l0, g_l1, godd, small_vec, small_others = _local_step(
        x[0], positions[0], loss_target[0], odd, shard_rows, small, True)

    sums = [_sum_chips(pair, parts, ids, name=f"odd_chip_sum_{k}") for k, (pair, parts) in enumerate(zip(*godd))]
    g_in_e, g_rest = _run_plan(_join_plans([_plan_pair_gather(s) for s in sums]), name="odd_pair_gather")
    g_odd = _odd_unrows(g_rest.reshape(ROWS_ODD_W, 1024), ODD_W_PARTS)
    g_odd["w_in_e"] = g_in_e.reshape(D, 392)

    to_kernel = lambda d: {n: d[n].reshape(SMALL_LAYOUT[n][3]) for n in SMALL_LAYOUT}
    first_row, small_out = _small_update(small_vec, small_others, ids, to_kernel(w), to_kernel(m), to_kernel(v))
    loss = first_row[0, 1023]
    grads, delta, new_m, new_v = {}, {}, {}, {}
    for n, res in small_out.items():
        grads[n], delta[n], new_m[n], new_v[n] = (r.reshape(w[n].shape) for r in res)

    for n, bufs_, row0 in (("w_ff1", [g_l0, g_l1], 0), ("w_ff2", [g_l0, g_l1], 1024), ("w_in_o", [g_l1], 2048),
                           ("w_out_o", [g_l1], 3072)):
        grads[n], delta[n], new_m[n], new_v[n] = _adamw_rows(w[n], m[n], v[n], bufs_, row0, name=f"adamw_{n}")
    for n, _ in ODD_PARTS:
        grads[n] = g_odd[n][None]
        d_, m_, v_ = _adamw(w[n][0], g_odd[n], m[n][0], v[n][0], name=f"adamw_{n}")
        delta[n], new_m[n], new_v[n] = d_[None], m_[None], v_[None]

    return (loss, grad_x[None], *[grads[n] for n in WEIGHTS], *[delta[n] for n in WEIGHTS],
            *[new_m[n] for n in WEIGHTS], *[new_v[n] for n in WEIGHTS])
```

```python
import math

import jax
import jax.numpy as jnp
from jax import lax
from jax.experimental import pallas as pl
from jax.experimental.pallas import tpu as pltpu

F32 = jnp.float32
BF16 = jnp.bfloat16
MESH_IDS = pl.DeviceIdType.MESH

D = 1024
DEPTH = 2
HEADS = 8
NOPE, ROPE, VDIM = 64, 32, 64
QK_SCALE = (NOPE + ROPE) ** -0.5
ROPE_BASE = 10000.0
SGU_G, SGU_C = 4, 128
HG_CHUNK = 64
HG_HEADS_PER_STEP = 8
ALPHA = (2 * DEPTH) ** 0.25
EPS = 1e-5
LR, B1, B2, ADAM_EPS, WD, STEP = 0.001, 0.9, 0.999, 1e-08, 0.01, 10
GELU_C = math.sqrt(2.0 / math.pi)
GELU_A = 0.044715
MB = 1024 * 1024
ROW_BLOCK = 512
SMALL_ROWS = 80

NT_DIMS = (((1,), (1,)), ((), ()))
TN_DIMS = (((0,), (0,)), ((), ()))


def _params(vmem_mb, n_axes=0):
    kw = dict(vmem_limit_bytes=vmem_mb * MB)
    if n_axes:
        kw["dimension_semantics"] = ("arbitrary",) * n_axes
    return pltpu.CompilerParams(**kw)


_ANY = pl.BlockSpec(memory_space=pltpu.HBM)


def _mesh_pos():
    return lax.axis_index("x"), lax.axis_index("y"), lax.axis_index("c")


def _hbm(*arrays):
    return tuple(pltpu.with_memory_space_constraint(a, pltpu.HBM) if a.size >= 2 ** 18 else a for a in arrays)


class _Plan:
    def __init__(self, ins, outs, n_remote, n_local, start, wait, aliases=None):
        self.ins, self.outs, self.n_remote, self.n_local = list(ins), list(outs), n_remote, n_local
        self.start, self.wait, self.aliases = start, wait, dict(aliases or {})


def _join_plans(plans):
    ins, outs, aliases, parts = [], [], {}, []
    nr = nl = 0
    for p in plans:
        parts.append((p, len(ins), len(outs), nr, nl))
        aliases.update({len(ins) + i: len(outs) + o for i, o in p.aliases.items()})
        ins += p.ins
        outs += p.outs
        nr += p.n_remote
        nl += p.n_local

    def run(which):
        def go(in_refs, out_refs, send, recv, loc):
            for p, i0, o0, r0, l0 in parts:
                getattr(p, which)(in_refs[i0:i0 + len(p.ins)], out_refs[o0:o0 + len(p.outs)],
                                  lambda i, r0=r0: send(r0 + i), lambda i, r0=r0: recv(r0 + i),
                                  lambda i, l0=l0: loc(l0 + i))
        return go

    return _Plan(ins, outs, nr, nl, run("start"), run("wait"), aliases)


def _plan_io(plan, n_in, n_out):
    if plan is None:
        return [], [], [], [], {}
    sems = [pltpu.SemaphoreType.DMA((max(plan.n_remote, 1),)), pltpu.SemaphoreType.DMA((max(plan.n_remote, 1),)),
            pltpu.SemaphoreType.DMA((max(plan.n_local, 1),))]
    aliases = {n_in + i: n_out + o for i, o in plan.aliases.items()}
    return plan.ins, [_ANY] * len(plan.outs), plan.outs, sems, aliases


def _split_refs(refs, n_in, n_out, n_scr, plan):
    p_in, p_out = (len(plan.ins), len(plan.outs)) if plan is not None else (0, 0)
    refs = list(refs)
    ins, refs = refs[:n_in], refs[n_in:]
    pins, refs = refs[:p_in], refs[p_in:]
    outs, refs = refs[:n_out], refs[n_out:]
    pouts, refs = refs[:p_out], refs[p_out:]
    scr, psem = refs[:n_scr], refs[n_scr:]
    psem = tuple((lambda i, s=s: s.at[i]) for s in psem)
    return ins, outs, scr, (pins, pouts, psem)


def _grid_edge(grid, last):
    cond = None
    for ax, n in enumerate(grid):
        c = pl.program_id(ax) == (n - 1 if last else 0)
        cond = c if cond is None else cond & c
    return cond


def _plan_start(plan, pctx, grid):
    if plan is not None:
        pins, pouts, psem = pctx
        pl.when(_grid_edge(grid, False))(lambda: plan.start(pins, pouts, *psem))


def _plan_wait(plan, pctx, grid):
    if plan is not None:
        pins, pouts, psem = pctx
        pl.when(_grid_edge(grid, True))(lambda: plan.wait(pins, pouts, *psem))


def _run_plan(plan, *, name):
    def body(*refs):
        _, _, _, (pins, pouts, psem) = _split_refs(refs, 0, 0, 0, plan)
        plan.start(pins, pouts, *psem)
        plan.wait(pins, pouts, *psem)

    p_in, p_ospec, p_oshape, p_scr, p_alias = _plan_io(plan, 0, 0)
    return pl.pallas_call(body, name=name, in_specs=[_ANY] * len(p_in), out_specs=p_ospec, out_shape=p_oshape,
                          scratch_shapes=p_scr, input_output_aliases=p_alias)(*p_in)


def _fold8(x):
    return x.reshape(x.shape[0] // 8, 8, x.shape[1]).sum(axis=0)


def _ln_stats(r):
    mu = jnp.mean(r, -1, keepdims=True)
    xc = r - mu
    rstd = lax.rsqrt(jnp.mean(xc * xc, -1, keepdims=True) + EPS)
    return xc * rstd, rstd


def _sigmoid(x):
    return jax.nn.sigmoid(x)


def _gelu(x):
    return 0.5 * x * (1.0 + jnp.tanh(GELU_C * (x + GELU_A * x * x * x)))


def _gelu_grad(x):
    t = jnp.tanh(GELU_C * (x + GELU_A * x * x * x))
    return 0.5 * (1.0 + t) + 0.5 * x * (1.0 - t * t) * GELU_C * (1.0 + 3.0 * GELU_A * x * x)


MM_ROWS = 1024
DW_TOKENS = 2048


def _matmul(a, b, *, name, M, N, K, ta=False, tb=False, out_dtype=F32, tm=MM_ROWS, tn=1024, tk=1024,
            a_spec=None, b_spec=None, b_merge=None, out_shape=None, o_spec=None, into=None,
            a_sq=False, mul=None, add=None, add_scale=1.0, plan=None):
    tm, tn, tk = min(tm, M), min(tn, N), min(tk, K)
    assert M % tm == 0 and N % tn == 0 and K % tk == 0
    grid = (M // tm, N // tn, K // tk)
    nk = grid[2]
    if a_spec is None:
        a_spec = pl.BlockSpec((tk, tm), lambda i, j, k: (k, i)) if ta else pl.BlockSpec((tm, tk), lambda i, j, k: (i, k))
    if b_spec is None:
        b_spec = pl.BlockSpec((tn, tk), lambda i, j, k: (j, k)) if tb else pl.BlockSpec((tk, tn), lambda i, j, k: (k, j))
    if o_spec is None:
        o_spec = pl.BlockSpec((tm, tn), lambda i, j, k: (i, j))
        out_shape = jax.ShapeDtypeStruct((M, N), out_dtype)
    e_spec = pl.BlockSpec((tm, tn), lambda i, j, k: (i, j))
    dims = (((0 if ta else 1,), (1 if tb else 0,)), ((), ()))
    extra = [e for e in (mul, add, into) if e is not None]
    n_in = 2 + len(extra)

    def body(*refs):
        ins, outs, scr, pctx = _split_refs(refs, n_in, 1, 1 if nk > 1 else 0, plan)
        a_ref, b_ref = ins[0], ins[1]
        rest = list(ins[2:])
        mul_ref = rest.pop(0) if mul is not None else None
        add_ref = rest.pop(0) if add is not None else None
        o_ref = outs[0]
        _plan_start(plan, pctx, grid)
        av = a_ref[...].astype(BF16)
        if a_sq:
            av = av * av
        bv = b_ref[...]
        if b_merge is not None:
            bv = bv.reshape(b_merge)
        if bv.ndim == 3:
            w = av.shape[1] // bv.shape[0]
            p = sum(lax.dot_general(av[:, s * w:(s + 1) * w], bv[s], dims, preferred_element_type=F32)
                    for s in range(bv.shape[0]))
        else:
            p = lax.dot_general(av, bv, dims, preferred_element_type=F32)

        def finish(r):
            if mul_ref is not None:
                r = r * (2.0 * mul_ref[...].astype(F32))
            if add_ref is not None:
                r = r + add_scale * add_ref[...]
            o_ref[...] = r.astype(o_ref.dtype)

        if nk == 1:
            finish(p)
        else:
            acc_ref = scr[0]
            k = pl.program_id(2)

            @pl.when(k == 0)
            def _():
                acc_ref[...] = p

            @pl.when(k > 0)
            def _():
                acc_ref[...] += p

            @pl.when(k == nk - 1)
            def _():
                finish(acc_ref[...])

        _plan_wait(plan, pctx, grid)

    p_in, p_ospec, p_oshape, p_scr, p_alias = _plan_io(plan, n_in, 1)
    aliases = dict(p_alias)
    if into is not None:
        aliases[n_in - 1] = 0
    return pl.pallas_call(
        body, name=name, grid=grid,
        in_specs=[a_spec, b_spec] + [e_spec] * (len(extra) - (into is not None)) + [_ANY] * (into is not None)
        + [_ANY] * len(p_in),
        out_specs=[o_spec] + p_ospec, out_shape=[out_shape] + p_oshape,
        scratch_shapes=([pltpu.VMEM((tm, tn), F32)] if nk > 1 else []) + p_scr,
        input_output_aliases=aliases, compiler_params=_params(48, 3),
    )(*_hbm(a, b, *extra), *p_in)


def _rows4_spec(rowblk, n_axes):
    return pl.BlockSpec((4, 256, D), lambda *_: (0, rowblk, 0))


def _proj_ln(a_b, w, h_prev, g, b, *, name, w_rowblk=None, plan=None):
    T = a_b.shape[0]
    tm = min(ROW_BLOCK, T)
    grid = (T // tm,)
    row = pl.BlockSpec((tm, D), lambda i: (i, 0))
    vec = pl.BlockSpec((1, D), lambda i: (0, 0))
    w_spec = pl.BlockSpec((D, D), lambda i: (0, 0)) if w_rowblk is None else _rows4_spec(w_rowblk, 1)

    def body(*refs):
        (a_ref, w_ref, h_ref, g_ref, b_ref), (r_ref, ho_ref, hb_ref), _, pctx = _split_refs(refs, 5, 3, 0, plan)
        _plan_start(plan, pctx, grid)
        mix = jnp.dot(a_ref[...], w_ref[...].reshape(D, D), preferred_element_type=F32)
        r = ALPHA * h_ref[...] + mix
        xhat, _ = _ln_stats(r)
        y = xhat * g_ref[...] + b_ref[...]
        r_ref[...] = r
        ho_ref[...] = y
        hb_ref[...] = y.astype(BF16)
        _plan_wait(plan, pctx, grid)

    p_in, p_ospec, p_oshape, p_scr, p_alias = _plan_io(plan, 5, 3)
    return pl.pallas_call(
        body, name=name, grid=grid,
        in_specs=[row, w_spec, row, vec, vec] + [_ANY] * len(p_in),
        out_specs=[row, row, row] + p_ospec,
        out_shape=[jax.ShapeDtypeStruct((T, D), F32), jax.ShapeDtypeStruct((T, D), F32),
                   jax.ShapeDtypeStruct((T, D), BF16)] + p_oshape,
        scratch_shapes=p_scr, input_output_aliases=p_alias, compiler_params=_params(40, 1),
    )(*_hbm(a_b, w, h_prev, g, b), *p_in)


def _ffn_ln(h_b, wbuf, h, g, b, *, name, plan=None):
    T = h_b.shape[0]
    tm, tf = min(ROW_BLOCK, T), 1024
    nf = 4
    F = nf * tf
    grid = (T // tm, nf)
    row = pl.BlockSpec((tm, D), lambda i, j: (i, 0))
    vec = pl.BlockSpec((1, D), lambda i, j: (0, 0))

    def body(*refs):
        ((hb_ref, w1_ref, w2_ref, h_ref, g_ref, b_ref), (ra_ref, r_ref, ho_ref, hbo_ref), (acc_ref,),
         pctx) = _split_refs(refs, 6, 4, 1, plan)
        _plan_start(plan, pctx, grid)
        j = pl.program_id(1)
        a = jnp.dot(hb_ref[...], w1_ref[...], preferred_element_type=F32)
        ra = jnp.maximum(a, 0.0)
        ra_ref[...] = ra.astype(BF16)
        p = jnp.dot((ra * ra).astype(BF16), w2_ref[...], preferred_element_type=F32)

        @pl.when(j == 0)
        def _():
            acc_ref[...] = p

        @pl.when(j > 0)
        def _():
            acc_ref[...] += p

        @pl.when(j == nf - 1)
        def _():
            r = ALPHA * h_ref[...] + acc_ref[...]
            xhat, _ = _ln_stats(r)
            y = xhat * g_ref[...] + b_ref[...]
            r_ref[...] = r
            ho_ref[...] = y
            hbo_ref[...] = y.astype(BF16)

        _plan_wait(plan, pctx, grid)

    p_in, p_ospec, p_oshape, p_scr, p_alias = _plan_io(plan, 6, 4)
    return pl.pallas_call(
        body, name=name, grid=grid,
        in_specs=[row, pl.BlockSpec((None, D, tf), lambda i, j: (j, 0, 0)),
                  pl.BlockSpec((None, tf, D), lambda i, j: (j, 1, 0)), row, vec, vec] + [_ANY] * len(p_in),
        out_specs=[pl.BlockSpec((tm, tf), lambda i, j: (i, j)), row, row, row] + p_ospec,
        out_shape=[jax.ShapeDtypeStruct((T, F), BF16), jax.ShapeDtypeStruct((T, D), F32),
                   jax.ShapeDtypeStruct((T, D), F32), jax.ShapeDtypeStruct((T, D), BF16)] + p_oshape,
        scratch_shapes=[pltpu.VMEM((tm, D), F32)] + p_scr,
        input_output_aliases=p_alias, compiler_params=_params(48, 2),
    )(*_hbm(h_b, wbuf, wbuf, h, g, b), *p_in)


def _ln_bwd(dy, r, g, *, name, tgt=None):
    T = dy.shape[0]
    tm = min(ROW_BLOCK, T)
    row = pl.BlockSpec((tm, D), lambda i: (i, 0))
    acc = pl.BlockSpec((8, D), lambda i: (0, 0))
    n_in = 3 + (tgt is not None)

    def body(*refs):
        dy_ref, r_ref, g_ref = refs[:3]
        dr_ref, drb_ref, dg_ref, db_ref = refs[n_in:n_in + 4]

        @pl.when(pl.program_id(0) == 0)
        def _():
            for ref in refs[n_in + 2:]:
                ref[...] = jnp.zeros_like(ref)

        dy_ = dy_ref[...]
        if tgt is not None:
            err = dy_ - refs[3][...]
            refs[n_in + 4][...] += _fold8(err * err)
            dy_ = err * (1.0 / D)
        xhat, rstd = _ln_stats(r_ref[...])
        dxh = dy_ * g_ref[...]
        m1 = jnp.mean(dxh, -1, keepdims=True)
        m2 = jnp.mean(dxh * xhat, -1, keepdims=True)
        dr = rstd * (dxh - m1 - xhat * m2)
        dr_ref[...] = dr
        drb_ref[...] = dr.astype(BF16)
        dg_ref[...] += _fold8(dy_ * xhat)
        db_ref[...] += _fold8(dy_)

    extra = [] if tgt is None else [tgt]
    return pl.pallas_call(
        body, name=name, grid=(T // tm,),
        in_specs=[row, row, pl.BlockSpec((1, D), lambda i: (0, 0))] + [row] * len(extra),
        out_specs=[row, row, acc, acc] + [acc] * len(extra),
        out_shape=[jax.ShapeDtypeStruct((T, D), F32), jax.ShapeDtypeStruct((T, D), BF16)]
        + [jax.ShapeDtypeStruct((8, D), F32)] * (2 + len(extra)),
        compiler_params=_params(40, 1),
    )(*_hbm(dy, r, g, *extra))


def _rope(x, c, s1, s2):
    return x * c + pltpu.roll(x, 112, 1) * s1 + pltpu.roll(x, 16, 1) * s2


def _rope_t(dy, c, s1, s2):
    return dy * c + pltpu.roll(dy * s1, 16, 1) + pltpu.roll(dy * s2, 112, 1)


def _rms(x, g):
    rstd = lax.rsqrt(jnp.mean(x * x, -1, keepdims=True) + EPS)
    xhat = x * rstd
    return xhat * g, xhat, rstd


def _mla_prep(z0, gq, gkv, wq, wk, wv, rc, rs1, rs2):
    T = z0.shape[0]
    tm = min(ROW_BLOCK, T)
    HW = HEADS * 128

    def body(cq_ref, ckv_ref, kr_ref, gq_ref, gkv_ref, wq_ref, wk_ref, wv_ref, c_ref, s1_ref, s2_ref,
             q_ref, k_ref, v_ref):
        nq = _rms(cq_ref[...], gq_ref[...])[0].astype(BF16)
        nkv = _rms(ckv_ref[...], gkv_ref[...])[0].astype(BF16)
        q = jnp.dot(nq, wq_ref[...], preferred_element_type=F32)
        k = jnp.dot(nkv, wk_ref[...], preferred_element_type=F32)
        v = jnp.dot(nkv, wv_ref[...], preferred_element_type=F32)
        c, s1, s2 = c_ref[...], s1_ref[...], s2_ref[...]
        kr = _rope(pltpu.roll(kr_ref[...], 64, 1), c, s1, s2)
        for h in range(HEADS):
            sl = slice(h * 128, (h + 1) * 128)
            q_ref[:, sl] = (_rope(q[:, sl], c, s1, s2) * QK_SCALE).astype(BF16)
            k_ref[:, sl] = (k[:, sl] + kr).astype(BF16)
        v_ref[...] = v.astype(BF16)

    full = lambda shape: pl.BlockSpec(shape, lambda i: (0, 0))
    tab = pl.BlockSpec((tm, 128), lambda i: (i, 0))
    return pl.pallas_call(
        body, name="mla_prep", grid=(T // tm,),
        in_specs=[pl.BlockSpec((tm, 256), lambda i: (i, 0)), pl.BlockSpec((tm, 256), lambda i: (i, 1)),
                  pl.BlockSpec((tm, 128), lambda i: (i, 12)), full((1, 256)), full((1, 256)),
                  full((256, HW)), full((256, HW)), full((256, 512)), tab, tab, tab],
        out_specs=[pl.BlockSpec((tm, HW), lambda i: (i, 0)), pl.BlockSpec((tm, HW), lambda i: (i, 0)),
                   pl.BlockSpec((tm, 512), lambda i: (i, 0))],
        out_shape=[jax.ShapeDtypeStruct((T, HW), BF16), jax.ShapeDtypeStruct((T, HW), BF16),
                   jax.ShapeDtypeStruct((T, 512), BF16)],
        compiler_params=_params(40, 1),
    )(z0, z0, z0, gq, gkv, wq, wk, wv, rc, rs1, rs2)


def _flash_fwd(q, k, v, plan=None):
    T = q.shape[0]
    bq = min(2 * ROW_BLOCK, T)
    nq = T // bq
    grid = (4, nq, nq)

    def body(*refs):
        (q_ref, k_ref, v_ref), (o_ref, lse_ref), (m_sc, acc_sc), pctx = _split_refs(refs, 3, 2, 2, plan)
        _plan_start(plan, pctx, grid)
        i, j = pl.program_id(1), pl.program_id(2)
        first = lax.broadcasted_iota(jnp.int32, (bq, 128), 1) < 64

        @pl.when(j == 0)
        def _():
            m_sc[...] = jnp.full_like(m_sc, -jnp.inf)
            acc_sc[...] = jnp.zeros_like(acc_sc)

        def tile(r0, nr, nc, masked):
            rs = slice(r0, r0 + nr)
            vp = v_ref[0:nc, :]
            lanes = first[0:nc, :]
            for h in range(2):
                sl = slice(h * 128, (h + 1) * 128)
                s = lax.dot_general(q_ref[rs, sl], k_ref[0:nc, sl], NT_DIMS, preferred_element_type=F32)
                if masked:
                    rows = r0 + lax.broadcasted_iota(jnp.int32, (nr, nc), 0)
                    cols = lax.broadcasted_iota(jnp.int32, (nr, nc), 1)
                    s = jnp.where(cols <= rows, s, -jnp.inf)
                m_prev = m_sc[h, rs, 0:1]
                m_new = jnp.maximum(m_prev, jnp.max(s, axis=1, keepdims=True))
                alpha = jnp.exp(m_prev - m_new)
                p = jnp.exp(s - m_new).astype(BF16)
                vh = jnp.where(lanes if h == 0 else jnp.logical_not(lanes), vp, jnp.ones_like(vp))
                acc_sc[h, rs, :] = acc_sc[h, rs, :] * alpha + jnp.dot(p, vh, preferred_element_type=F32)
                m_sc[h, rs, :] = jnp.broadcast_to(m_new, (nr, 128))

        @pl.when(j < i)
        def _():
            tile(0, bq, bq, False)

        @pl.when(j == i)
        def _():
            tile(0, bq, bq, True)
            a0, a1 = acc_sc[0], acc_sc[1]
            l0, l1 = pltpu.roll(a0, 64, 1), pltpu.roll(a1, 64, 1)
            o_ref[...] = jnp.where(first, a0 / l0, a1 / l1).astype(BF16)
            lse_ref[...] = jnp.where(first, m_sc[0] + jnp.log(l0), m_sc[1] + jnp.log(l1))

        _plan_wait(plan, pctx, grid)

    kv = lambda hp, i, j: (jnp.minimum(i, j), hp)
    p_in, p_ospec, p_oshape, p_scr, p_alias = _plan_io(plan, 3, 2)
    return pl.pallas_call(
        body, name="flash_fwd", grid=grid,
        in_specs=[pl.BlockSpec((bq, 256), lambda hp, i, j: (i, hp)), pl.BlockSpec((bq, 256), kv),
                  pl.BlockSpec((bq, 128), kv)] + [_ANY] * len(p_in),
        out_specs=[pl.BlockSpec((bq, 128), lambda hp, i, j: (i, hp)),
                   pl.BlockSpec((bq, 128), lambda hp, i, j: (i, hp))] + p_ospec,
        out_shape=[jax.ShapeDtypeStruct((T, 512), BF16), jax.ShapeDtypeStruct((T, 512), F32)] + p_oshape,
        scratch_shapes=[pltpu.VMEM((2, bq, 128), F32), pltpu.VMEM((2, bq, 128), F32)] + p_scr,
        input_output_aliases=p_alias, compiler_params=_params(56, 3),
    )(*_hbm(q, k, v), *p_in)


def _attn_delta(dmix, o):
    T = o.shape[0]
    tm = min(ROW_BLOCK, T)
    blk = pl.BlockSpec((tm, 512), lambda i: (i, 0))

    def body(do_ref, o_ref, delta_ref, dob_ref):
        first = lax.broadcasted_iota(jnp.int32, (tm, 128), 1) < 64
        for hp in range(4):
            sl = slice(hp * 128, (hp + 1) * 128)
            prod = do_ref[:, sl] * o_ref[:, sl].astype(F32)
            d0 = jnp.sum(jnp.where(first, prod, 0.0), axis=1, keepdims=True)
            d1 = jnp.sum(jnp.where(first, 0.0, prod), axis=1, keepdims=True)
            delta_ref[:, sl] = jnp.where(first, d0, d1)
        dob_ref[...] = do_ref[...].astype(BF16)

    return pl.pallas_call(
        body, name="attn_delta", grid=(T // tm,), in_specs=[blk, blk], out_specs=[blk, blk],
        out_shape=[jax.ShapeDtypeStruct((T, 512), F32), jax.ShapeDtypeStruct((T, 512), BF16)],
        compiler_params=_params(32, 1),
    )(dmix, o)


def _flash_bwd(q, k, v, do_b, lse, delta, plan=None):
    T = q.shape[0]
    bq = min(2 * ROW_BLOCK, T)
    nq = T // bq
    grid = (4, nq, nq)

    def body(*refs):
        ((q_ref, k_ref, v_ref, do_ref, lse_ref, dl_ref), (dq_hbm, dk_ref, dv_ref), (dq_sc, dk_sc, dv_sc, sem),
         pctx) = _split_refs(refs, 6, 3, 4, plan)
        _plan_start(plan, pctx, grid)
        hp, j, i = pl.program_id(0), pl.program_id(1), pl.program_id(2)
        first = lax.broadcasted_iota(jnp.int32, (bq, 128), 1) < 64

        @pl.when((j == 0) & (i == 0))
        def _():
            dq_sc[...] = jnp.zeros_like(dq_sc)

        @pl.when(i == j)
        def _():
            dk_sc[...] = jnp.zeros_like(dk_sc)
            dv_sc[...] = jnp.zeros_like(dv_sc)

        def tile(r0, nr, nc, masked):
            rs, cs = slice(r0, r0 + nr), slice(0, nc)
            vp = v_ref[cs, :]
            do = do_ref[rs, :]
            lanes = first[rs, :]
            for h in range(2):
                sl = slice(h * 128, (h + 1) * 128)
                qh, kh = q_ref[rs, sl], k_ref[cs, sl]
                s = lax.dot_general(qh, kh, NT_DIMS, preferred_element_type=F32)
                p = jnp.exp(s - lse_ref[rs, h * 64:h * 64 + 1])
                if masked:
                    rows = r0 + lax.broadcasted_iota(jnp.int32, (nr, nc), 0)
                    cols = lax.broadcasted_iota(jnp.int32, (nr, nc), 1)
                    p = jnp.where(cols <= rows, p, 0.0)
                do_h = jnp.where(lanes if h == 0 else jnp.logical_not(lanes), do, jnp.zeros_like(do))
                dv_sc[cs, :] += lax.dot_general(p.astype(BF16), do_h, TN_DIMS, preferred_element_type=F32)
                dp = lax.dot_general(do_h, vp, NT_DIMS, preferred_element_type=F32)
                ds = (p * (dp - dl_ref[rs, h * 64:h * 64 + 1])).astype(BF16)
                dq_sc[i, rs, sl] += jnp.dot(ds, kh, preferred_element_type=F32)
                dk_sc[cs, sl] += lax.dot_general(ds, qh, TN_DIMS, preferred_element_type=F32)

        @pl.when(i > j)
        def _():
            tile(0, bq, bq, False)

        @pl.when(i == j)
        def _():
            tile(0, bq // 2, bq // 2, True)
            tile(bq // 2, bq // 2, bq, True)

        @pl.when(i == nq - 1)
        def _():
            dk_ref[...] = dk_sc[...]
            dv_ref[...] = dv_sc[...]

        @pl.when((j == nq - 1) & (i == nq - 1))
        def _():
            cp = pltpu.make_async_copy(dq_sc, dq_hbm.at[hp], sem)
            cp.start()
            cp.wait()

        _plan_wait(plan, pctx, grid)

    qi = lambda hp, j, i: (jnp.maximum(i, j), hp)
    kj = lambda hp, j, i: (j, hp)
    p_in, p_ospec, p_oshape, p_scr, p_alias = _plan_io(plan, 6, 3)
    return pl.pallas_call(
        body, name="flash_bwd", grid=grid,
        in_specs=[pl.BlockSpec((bq, 256), qi), pl.BlockSpec((bq, 256), kj), pl.BlockSpec((bq, 128), kj),
                  pl.BlockSpec((bq, 128), qi), pl.BlockSpec((bq, 128), qi), pl.BlockSpec((bq, 128), qi)]
        + [_ANY] * len(p_in),
        out_specs=[_ANY, pl.BlockSpec((bq, 256), kj), pl.BlockSpec((bq, 128), kj)] + p_ospec,
        out_shape=[jax.ShapeDtypeStruct((4, nq, bq, 256), F32), jax.ShapeDtypeStruct((T, 1024), F32),
                   jax.ShapeDtypeStruct((T, 512), F32)] + p_oshape,
        scratch_shapes=[pltpu.VMEM((nq, bq, 256), F32), pltpu.VMEM((bq, 256), F32), pltpu.VMEM((bq, 128), F32),
                        pltpu.SemaphoreType.DMA] + p_scr,
        input_output_aliases=p_alias, compiler_params=_params(56, 3),
    )(*_hbm(q, k, v, do_b, lse, delta), *p_in)


def _mla_bwd(z0, dq4, dk, dv, gq, gkv, wq, wk, wv, rc, rs1, rs2, plan=None):
    T = z0.shape[0]
    tm = min(ROW_BLOCK, T)
    HW = HEADS * 128
    grid = (T // tm,)
    dq4 = dq4.reshape(4, T, 256)

    def body(*refs):
        ((cq_ref, ckv_ref, dq_ref, dk_ref, dv_ref, gq_ref, gkv_ref, wq_ref, wk_ref, wv_ref, c_ref, s1_ref, s2_ref),
         (dc_ref, dkr_ref, dwq_ref, dwk_ref, dwv_ref, dgq_ref, dgkv_ref), _, pctx) = _split_refs(refs, 13, 7, 0, plan)
        _plan_start(plan, pctx, grid)

        @pl.when(pl.program_id(0) == 0)
        def _():
            for ref in (dwq_ref, dwk_ref, dwv_ref, dgq_ref, dgkv_ref):
                ref[...] = jnp.zeros_like(ref)

        c, s1, s2 = c_ref[...], s1_ref[...], s2_ref[...]
        lane = lax.broadcasted_iota(jnp.int32, (tm, 128), 1)
        nq, xq, rq = _rms(cq_ref[...], gq_ref[...])
        nkv, xkv, rkv = _rms(ckv_ref[...], gkv_ref[...])
        nq_b, nkv_b = nq.astype(BF16), nkv.astype(BF16)

        dq_parts, dk_parts = [], []
        dkr = jnp.zeros((tm, 128), F32)
        for h in range(HEADS):
            blk = dq_ref[h // 2, :, (h % 2) * 128:(h % 2 + 1) * 128] * QK_SCALE
            dq_parts.append(_rope_t(blk, c, s1, s2).astype(BF16))
            kb = dk_ref[:, h * 128:(h + 1) * 128]
            dk_parts.append(jnp.where(lane < NOPE, kb, 0.0).astype(BF16))
            dkr = dkr + kb
        dq_b = jnp.concatenate(dq_parts, axis=1)
        dk_b = jnp.concatenate(dk_parts, axis=1)
        dv_b = dv_ref[...].astype(BF16)

        dwq_ref[...] += lax.dot_general(nq_b, dq_b, TN_DIMS, preferred_element_type=F32)
        dwk_ref[...] += lax.dot_general(nkv_b, dk_b, TN_DIMS, preferred_element_type=F32)
        dwv_ref[...] += lax.dot_general(nkv_b, dv_b, TN_DIMS, preferred_element_type=F32)
        dnq = lax.dot_general(dq_b, wq_ref[...], NT_DIMS, preferred_element_type=F32)
        dnkv = (lax.dot_general(dk_b, wk_ref[...], NT_DIMS, preferred_element_type=F32)
                + lax.dot_general(dv_b, wv_ref[...], NT_DIMS, preferred_element_type=F32))

        def rms_bwd(dn, xhat, rstd, g):
            dxh = dn * g
            return rstd * (dxh - xhat * jnp.mean(dxh * xhat, -1, keepdims=True))

        dc_ref[:, :256] = rms_bwd(dnq, xq, rq, gq_ref[...]).astype(BF16)
        dc_ref[:, 256:] = rms_bwd(dnkv, xkv, rkv, gkv_ref[...]).astype(BF16)
        dgq_ref[...] += _fold8(dnq * xq)
        dgkv_ref[...] += _fold8(dnkv * xkv)
        dkr = pltpu.roll(_rope_t(dkr, c, s1, s2), 64, 1)
        dkr_ref[...] = jnp.where(lane < ROPE, dkr, 0.0).astype(BF16)
        _plan_wait(plan, pctx, grid)

    full = lambda shape: pl.BlockSpec(shape, lambda i: (0,) * len(shape))
    tab = pl.BlockSpec((tm, 128), lambda i: (i, 0))
    p_in, p_ospec, p_oshape, p_scr, p_alias = _plan_io(plan, 13, 7)
    return pl.pallas_call(
        body, name="mla_bwd", grid=grid,
        in_specs=[pl.BlockSpec((tm, 256), lambda i: (i, 0)), pl.BlockSpec((tm, 256), lambda i: (i, 1)),
                  pl.BlockSpec((4, tm, 256), lambda i: (0, i, 0)),
                  pl.BlockSpec((tm, HW), lambda i: (i, 0)), pl.BlockSpec((tm, 512), lambda i: (i, 0)),
                  full((1, 256)), full((1, 256)), full((256, HW)), full((256, HW)), full((256, 512)), tab, tab, tab]
        + [_ANY] * len(p_in),
        out_specs=[pl.BlockSpec((tm, 512), lambda i: (i, 0)), tab, full((256, HW)), full((256, HW)),
                   full((256, 512)), full((8, 256)), full((8, 256))] + p_ospec,
        out_shape=[jax.ShapeDtypeStruct((T, 512), BF16), jax.ShapeDtypeStruct((T, 128), BF16),
                   jax.ShapeDtypeStruct((256, HW), F32), jax.ShapeDtypeStruct((256, HW), F32),
                   jax.ShapeDtypeStruct((256, 512), F32), jax.ShapeDtypeStruct((8, 256), F32),
                   jax.ShapeDtypeStruct((8, 256), F32)] + p_oshape,
        scratch_shapes=p_scr, input_output_aliases=p_alias, compiler_params=_params(48, 1),
    )(*_hbm(z0, z0, dq4, dk, dv, gq, gkv, wq, wk, wv, rc, rs1, rs2), *p_in)


def _sgu_fwd(z0, a_out, ln_g, ln_b, w, b_t):
    T = z0.shape[0]
    tm = min(ROW_BLOCK, T)
    W = SGU_G * SGU_C

    def body(u_ref, v_ref, a_ref, g_ref, b_ref, w_ref, bt_ref, o_ref):
        o_ref[:, :W] = a_ref[...]
        ug = _gelu(u_ref[...])
        xhat, _ = _ln_stats(_gelu(v_ref[...]))
        vn = (xhat * g_ref[...] + b_ref[...]).astype(BF16)
        tril = lax.broadcasted_iota(jnp.int32, (SGU_C, SGU_C), 0) >= lax.broadcasted_iota(jnp.int32, (SGU_C, SGU_C), 1)
        for g in range(SGU_G):
            cs = slice(g * SGU_C, (g + 1) * SGU_C)
            wg = jnp.where(tril, w_ref[g], 0.0).astype(BF16)
            bcol = bt_ref[:, g:g + 1]
            for c in range(tm // SGU_C):
                rs = slice(c * SGU_C, (c + 1) * SGU_C)
                mixed = jnp.dot(wg, vn[rs, cs], preferred_element_type=F32) + bcol
                o_ref[rs, W + g * SGU_C:W + (g + 1) * SGU_C] = (ug[rs, cs] * mixed).astype(BF16)

    full = lambda shape: pl.BlockSpec(shape, lambda i: (0,) * len(shape))
    return pl.pallas_call(
        body, name="sgu_fwd", grid=(T // tm,),
        in_specs=[pl.BlockSpec((tm, W), lambda i: (i, 1)), pl.BlockSpec((tm, W), lambda i: (i, 2)),
                  pl.BlockSpec((tm, W), lambda i: (i, 0)),
                  full((1, W)), full((1, W)), full((SGU_G, SGU_C, SGU_C)), full((SGU_C, SGU_G))],
        out_specs=pl.BlockSpec((tm, 2 * W), lambda i: (i, 0)),
        out_shape=jax.ShapeDtypeStruct((T, 2 * W), BF16),
        compiler_params=_params(32, 1),
    )(z0, z0, a_out, ln_g, ln_b, w, b_t)


def _sgu_bwd(z0, dmix, dc, dkr, ln_g, ln_b, w, b_t):
    T = z0.shape[0]
    tm = min(ROW_BLOCK, T)
    W = SGU_G * SGU_C

    def body(u_ref, v_ref, do_ref, dc_ref, dkr_ref, g_ref, b_ref, w_ref, bt_ref, dz_ref, dw_ref, db_ref, dlg_ref,
             dlb_ref):
        @pl.when(pl.program_id(0) == 0)
        def _():
            for ref in (dw_ref, db_ref, dlg_ref, dlb_ref):
                ref[...] = jnp.zeros_like(ref)

        dz_ref[:, :W] = dc_ref[...]
        dz_ref[:, 3 * W:] = dkr_ref[...]

        u, v, dout = u_ref[...], v_ref[...], do_ref[...]
        ug = _gelu(u)
        xhat, rstd = _ln_stats(_gelu(v))
        vn = (xhat * g_ref[...] + b_ref[...]).astype(BF16)
        dmixed = dout * ug
        dmixed_b = dmixed.astype(BF16)
        tril = lax.broadcasted_iota(jnp.int32, (SGU_C, SGU_C), 0) >= lax.broadcasted_iota(jnp.int32, (SGU_C, SGU_C), 1)
        lane = lax.broadcasted_iota(jnp.int32, (SGU_C, SGU_C), 1)
        dvn_cols = []
        for g in range(SGU_G):
            cs = slice(g * SGU_C, (g + 1) * SGU_C)
            wg = jnp.where(tril, w_ref[g], 0.0).astype(BF16)
            bcol = bt_ref[:, g:g + 1]
            dw_g = jnp.zeros((SGU_C, SGU_C), F32)
            db_g = jnp.zeros((SGU_C, 1), F32)
            dvn_rows = []
            for c in range(tm // SGU_C):
                rs = slice(c * SGU_C, (c + 1) * SGU_C)
                mixed = jnp.dot(wg, vn[rs, cs], preferred_element_type=F32) + bcol
                dz_ref[rs, W + g * SGU_C:W + (g + 1) * SGU_C] = (dout[rs, cs] * mixed * _gelu_grad(u[rs, cs])).astype(BF16)
                dm = dmixed_b[rs, cs]
                dw_g = dw_g + lax.dot_general(dm, vn[rs, cs], NT_DIMS, preferred_element_type=F32)
                db_g = db_g + jnp.sum(dmixed[rs, cs], axis=1, keepdims=True)
                dvn_rows.append(lax.dot_general(wg, dm, TN_DIMS, preferred_element_type=F32))
            dw_ref[g] += jnp.where(tril, dw_g, 0.0)
            db_ref[...] += jnp.where(lane == g, db_g, 0.0)
            dvn_cols.append(jnp.concatenate(dvn_rows, axis=0))
        dvn = jnp.concatenate(dvn_cols, axis=1)
        dxh = dvn * g_ref[...]
        m1 = jnp.mean(dxh, -1, keepdims=True)
        m2 = jnp.mean(dxh * xhat, -1, keepdims=True)
        dvg = rstd * (dxh - m1 - xhat * m2)
        dz_ref[:, 2 * W:3 * W] = (dvg * _gelu_grad(v)).astype(BF16)
        dlg_ref[...] += _fold8(dvn * xhat)
        dlb_ref[...] += _fold8(dvn)

    full = lambda shape: pl.BlockSpec(shape, lambda i: (0,) * len(shape))
    return pl.pallas_call(
        body, name="sgu_bwd", grid=(T // tm,),
        in_specs=[pl.BlockSpec((tm, W), lambda i: (i, 1)), pl.BlockSpec((tm, W), lambda i: (i, 2)),
                  pl.BlockSpec((tm, W), lambda i: (i, 1)), pl.BlockSpec((tm, W), lambda i: (i, 0)),
                  pl.BlockSpec((tm, 128), lambda i: (i, 0)),
                  full((1, W)), full((1, W)), full((SGU_G, SGU_C, SGU_C)), full((SGU_C, SGU_G))],
        out_specs=[pl.BlockSpec((tm, 3 * W + 128), lambda i: (i, 0)), full((SGU_G, SGU_C, SGU_C)),
                   full((SGU_C, SGU_C)), full((8, W)), full((8, W))],
        out_shape=[jax.ShapeDtypeStruct((T, 3 * W + 128), BF16), jax.ShapeDtypeStruct((SGU_G, SGU_C, SGU_C), F32),
                   jax.ShapeDtypeStruct((SGU_C, SGU_C), F32), jax.ShapeDtypeStruct((8, W), F32),
                   jax.ShapeDtypeStruct((8, W), F32)],
        compiler_params=_params(40, 1),
    )(z0, z0, dmix, dc, dkr, ln_g, ln_b, w, b_t)


def _hg_lower_bound(lb_ref):
    a0, a1 = lb_ref[0:1, :], lb_ref[1:2, :]
    m = jnp.maximum(a0, a1)
    e0, e1 = jnp.exp(a0 - m), jnp.exp(a1 - m)
    return e1 / (e0 + e1)


def _running_sum(x, reverse=False):
    n = x.shape[0]
    row = lax.broadcasted_iota(jnp.int32, x.shape, 0)
    s = 1
    while s < n:
        if reverse:
            x = x + jnp.where(row < n - s, pltpu.roll(x, n - s, 0), 0.0)
        else:
            x = x + jnp.where(row >= s, pltpu.roll(x, s, 0), 0.0)
        s *= 2
    return x


def _hg_chunk(qc, fc, lb):
    C = HG_CHUNK
    rows = lax.broadcasted_iota(jnp.int32, (C, C), 0)
    cols = lax.broadcasted_iota(jnp.int32, (C, C), 1)
    rowid = lax.broadcasted_iota(jnp.int32, (C, 128), 0)
    sq, sg = _sigmoid(qc), _sigmoid(fc)
    qf = qc * sq
    gate = lb + (1.0 - lb) * sg
    kk = 1.0 - gate
    lg = jnp.log(gate)
    bcum = _running_sum(lg)
    b_mid = jnp.sum(jnp.where(rowid < C // 2, lg, 0.0), axis=0, keepdims=True)
    b_last = jnp.sum(lg, axis=0, keepdims=True)
    eq, ek, e, eh = jnp.exp(bcum - b_mid), jnp.exp(b_mid - bcum), jnp.exp(bcum), jnp.exp(b_last - bcum)
    qt, kt, qe, khat = qf * eq, kk * ek, qf * e, kk * eh
    a = lax.dot_general(qt.astype(BF16), kt.astype(BF16), NT_DIMS, preferred_element_type=F32)
    a = jnp.where(rows >= cols, a, 0.0)
    return dict(sq=sq, sg=sg, gate=gate, kk=kk, eq=eq, ek=ek, e=e, eh=eh, qt=qt, kt=kt, qe=qe, khat=khat, a=a,
                e_last=jnp.exp(b_last), tril=rows >= cols, rowid=rowid)


def _hgrn_fwd(z4, hg_lb, gnorm):
    T = z4.shape[1]
    tb = min(ROW_BLOCK, T)
    C = HG_CHUNK
    ncb = tb // C
    HPB = HG_HEADS_PER_STEP

    def body(q_ref, f_ref, i_ref, g_ref, lb_ref, gn_ref, y_ref, o_ref, st_ref, st_sc):
        @pl.when(pl.program_id(1) == 0)
        def _():
            st_sc[...] = jnp.zeros_like(st_sc)

        def chunk(c, carry):
            rs = pl.ds(pl.multiple_of(c * C, C), C)
            for hh in range(HPB):
                hs = slice(hh * 128, (hh + 1) * 128)
                lb = _hg_lower_bound(lb_ref.at[:, hs])
                v_b = i_ref[rs, hs].astype(BF16)
                gc = g_ref[rs, hs]
                x = _hg_chunk(q_ref[rs, hs], f_ref[rs, hs], lb)
                st = st_sc[hh]
                st_ref[hh, c] = st
                o = (jnp.dot(x["a"].astype(BF16), v_b, preferred_element_type=F32)
                     + lax.dot_general(x["qe"].astype(BF16), st.astype(BF16), NT_DIMS, preferred_element_type=F32))
                st_sc[hh] = st * x["e_last"] + lax.dot_general(v_b, x["khat"].astype(BF16), TN_DIMS,
                                                               preferred_element_type=F32)
                o_ref[rs, hs] = o
                n = o * lax.rsqrt(jnp.mean(o * o, -1, keepdims=True) + EPS)
                y_ref[rs, hs] = (n * gn_ref[:, hs] * (gc * _sigmoid(gc))).astype(BF16)
            return carry

        lax.fori_loop(0, ncb, chunk, 0)

    W = 128 * HPB
    zb = lambda k: pl.BlockSpec((None, tb, W), lambda h, t: (k, t, h))
    out = pl.BlockSpec((tb, W), lambda h, t: (t, h))
    return pl.pallas_call(
        body, name="hgrn_fwd", grid=(HEADS // HPB, T // tb),
        in_specs=[zb(0), zb(1), zb(2), zb(3), pl.BlockSpec((2, W), lambda h, t: (0, h)),
                  pl.BlockSpec((1, W), lambda h, t: (0, h))],
        out_specs=[out, out, pl.BlockSpec((HPB, ncb, 128, 128), lambda h, t: (h, t, 0, 0))],
        out_shape=[jax.ShapeDtypeStruct((T, D), BF16), jax.ShapeDtypeStruct((T, D), F32),
                   jax.ShapeDtypeStruct((HEADS, T // C, 128, 128), F32)],
        scratch_shapes=[pltpu.VMEM((HPB, 128, 128), F32)],
        compiler_params=_params(48, 2),
    )(*_hbm(z4, z4, z4, z4, hg_lb, gnorm))


def _hgrn_bwd(z4, o_raw, dy, states, hg_lb, gnorm):
    T = z4.shape[1]
    tb = min(ROW_BLOCK, T)
    C = HG_CHUNK
    ncb = tb // C
    nt = T // tb
    HPB = HG_HEADS_PER_STEP

    def body(q_ref, f_ref, i_ref, g_ref, o_ref, dy_ref, st_ref, lb_ref, gn_ref, dz_ref, dlb_ref, dgn_ref, dst_sc):
        @pl.when(pl.program_id(1) == 0)
        def _():
            dst_sc[...] = jnp.zeros_like(dst_sc)
            dlb_ref[...] = jnp.zeros_like(dlb_ref)
            dgn_ref[...] = jnp.zeros_like(dgn_ref)

        def chunk(cc, carry):
            for hh in range(HPB):
                one_head(ncb - 1 - cc, hh, slice(hh * 128, (hh + 1) * 128))
            return carry

        def one_head(c, hh, hs):
            rs = pl.ds(pl.multiple_of(c * C, C), C)
            lb = _hg_lower_bound(lb_ref.at[:, hs])
            gn = gn_ref[:, hs]
            qc, gc = q_ref[rs, hs], g_ref[rs, hs]
            v_b = i_ref[rs, hs].astype(BF16)
            x = _hg_chunk(qc, f_ref[rs, hs], lb)
            st, dst = st_ref[hh, c], dst_sc[hh]
            st_b, dst_b = st.astype(BF16), dst.astype(BF16)
            o, dyc = o_ref[rs, hs], dy_ref[rs, hs]
            sgg = _sigmoid(gc)
            sil = gc * sgg
            rstd = lax.rsqrt(jnp.mean(o * o, -1, keepdims=True) + EPS)
            n = o * rstd
            dgn_ref[:, hs] += _fold8(dyc * n * sil)
            dn = dyc * gn * sil
            do = rstd * (dn - n * jnp.mean(dn * n, -1, keepdims=True))
            dg = dyc * n * gn * (sgg * (1.0 + gc * (1.0 - sgg)))
            do_b = do.astype(BF16)
            da = jnp.where(x["tril"], lax.dot_general(do_b, v_b, NT_DIMS, preferred_element_type=F32), 0.0).astype(BF16)
            qt_b, kt_b, qe_b, khat_b = (x[n_].astype(BF16) for n_ in ("qt", "kt", "qe", "khat"))
            dv = (lax.dot_general(x["a"].astype(BF16), do_b, TN_DIMS, preferred_element_type=F32)
                  + lax.dot_general(khat_b, dst_b, NT_DIMS, preferred_element_type=F32))
            dqt = jnp.dot(da, kt_b, preferred_element_type=F32)
            dqe = jnp.dot(do_b, st_b, preferred_element_type=F32)
            dkt = lax.dot_general(da, qt_b, TN_DIMS, preferred_element_type=F32)
            dkhat = jnp.dot(v_b, dst_b, preferred_element_type=F32)
            dst_sc[hh] = lax.dot_general(do_b, qe_b, TN_DIMS, preferred_element_type=F32) + dst * x["e_last"]
            de_last = jnp.sum(st * dst, axis=0, keepdims=True)
            dqf = dqt * x["eq"] + dqe * x["e"]
            dkk = dkt * x["ek"] + dkhat * x["eh"]
            dkh_kh = dkhat * x["khat"]
            db = dqt * qt_b.astype(F32) - dkt * kt_b.astype(F32) + dqe * x["qe"] - dkh_kh
            db_last = jnp.sum(dkh_kh, axis=0, keepdims=True) + de_last * x["e_last"]
            db = db + jnp.where(x["rowid"] == C - 1, db_last, 0.0)
            dlg = _running_sum(db, reverse=True)
            dgate = dlg / x["gate"] - dkk
            sg, sq = x["sg"], x["sq"]
            dlb_ref[:, hs] += _fold8(dgate * (1.0 - sg)) * (lb * (1.0 - lb))
            dz_ref[0, rs, hs] = (dqf * (sq * (1.0 + qc * (1.0 - sq)))).astype(BF16)
            dz_ref[1, rs, hs] = (dgate * (1.0 - lb) * sg * (1.0 - sg)).astype(BF16)
            dz_ref[2, rs, hs] = dv.astype(BF16)
            dz_ref[3, rs, hs] = dg.astype(BF16)

        lax.fori_loop(0, ncb, chunk, 0)

    W = 128 * HPB
    zb = lambda k: pl.BlockSpec((None, tb, W), lambda h, t: (k, nt - 1 - t, h))
    blk = pl.BlockSpec((tb, W), lambda h, t: (nt - 1 - t, h))
    acc = pl.BlockSpec((8, W), lambda h, t: (0, h))
    return pl.pallas_call(
        body, name="hgrn_bwd", grid=(HEADS // HPB, nt),
        in_specs=[zb(0), zb(1), zb(2), zb(3), blk, blk,
                  pl.BlockSpec((HPB, ncb, 128, 128), lambda h, t: (h, nt - 1 - t, 0, 0)),
                  pl.BlockSpec((2, W), lambda h, t: (0, h)), pl.BlockSpec((1, W), lambda h, t: (0, h))],
        out_specs=[pl.BlockSpec((4, tb, W), lambda h, t: (0, nt - 1 - t, h)), acc, acc],
        out_shape=[jax.ShapeDtypeStruct((4, T, D), BF16), jax.ShapeDtypeStruct((8, D), F32),
                   jax.ShapeDtypeStruct((8, D), F32)],
        scratch_shapes=[pltpu.VMEM((HPB, 128, 128), F32)],
        compiler_params=_params(48, 2),
    )(*_hbm(z4, z4, z4, z4, o_raw, dy, states, hg_lb, gnorm))


def _adamw(w, g, m, v, *, name):
    R, L = w.shape
    tr = R if R <= 512 else 512
    assert R % tr == 0
    blk = pl.BlockSpec((tr, L), lambda i: (i, 0))
    c1, c2 = 1.0 - B1 ** STEP, 1.0 - B2 ** STEP

    def body(w_ref, g_ref, m_ref, v_ref, d_ref, mo_ref, vo_ref):
        g_ = g_ref[...]
        m_ = B1 * m_ref[...] + (1.0 - B1) * g_
        v_ = B2 * v_ref[...] + (1.0 - B2) * (g_ * g_)
        d_ref[...] = -LR * ((m_ / c1) / (jnp.sqrt(v_ / c2) + ADAM_EPS) + WD * w_ref[...])
        mo_ref[...] = m_
        vo_ref[...] = v_

    sds = jax.ShapeDtypeStruct((R, L), F32)
    return pl.pallas_call(
        body, name=name, grid=(R // tr,), in_specs=[blk] * 4, out_specs=[blk] * 3, out_shape=[sds] * 3,
        compiler_params=_params(32, 1),
    )(w, g, m, v)


def _adamw_rows(w, m, v, gbufs, row0, *, name, plan=None):
    L, R, C = w.shape
    tr = 256
    assert R % tr == 0 and row0 % tr == 0 and len(gbufs) == L
    grid = (L, R // tr)
    blk = pl.BlockSpec((None, tr, C), lambda l, i: (l, i, 0))
    gblk = pl.BlockSpec((tr, C), lambda l, i: (row0 // tr + i, 0))
    c1, c2 = 1.0 - B1 ** STEP, 1.0 - B2 ** STEP

    def body(*refs):
        ins, (go_ref, d_ref, mo_ref, vo_ref), _, pctx = _split_refs(refs, 3 + L, 4, 0, plan)
        w_ref, m_ref, v_ref = ins[:3]
        g_refs = ins[3:]
        _plan_start(plan, pctx, grid)
        g_ = g_refs[0][...]
        for l in range(1, L):
            g_ = jnp.where(pl.program_id(0) == l, g_refs[l][...], g_)
        m_ = B1 * m_ref[...] + (1.0 - B1) * g_
        v_ = B2 * v_ref[...] + (1.0 - B2) * (g_ * g_)
        go_ref[...] = g_
        d_ref[...] = -LR * ((m_ / c1) / (jnp.sqrt(v_ / c2) + ADAM_EPS) + WD * w_ref[...])
        mo_ref[...] = m_
        vo_ref[...] = v_
        _plan_wait(plan, pctx, grid)

    sds = jax.ShapeDtypeStruct((L, R, C), F32)
    p_in, p_ospec, p_oshape, p_scr, p_alias = _plan_io(plan, 3 + L, 4)
    return pl.pallas_call(
        body, name=name, grid=grid, in_specs=[blk] * 3 + [gblk] * L + [_ANY] * len(p_in),
        out_specs=[blk] * 4 + p_ospec, out_shape=[sds] * 4 + p_oshape, scratch_shapes=p_scr,
        input_output_aliases=p_alias, compiler_params=_params(32, 2),
    )(*_hbm(w, m, v, *gbufs), *p_in)


def _add_pairs(g, theirs, ids, *, name):
    n, R, L = theirs.shape
    tr = math.gcd(R, 128)
    nb = R // tr

    def body(ids_ref, a_ref, b_ref, o_ref):
        o_ref[...] = (a_ref[...].astype(F32) + b_ref[...].astype(F32)).astype(BF16)

    blk = pl.BlockSpec((n, tr, L), lambda i, ids: (0, i, 0))
    return pl.pallas_call(
        body, name=name, out_shape=jax.ShapeDtypeStruct((n, R, L), BF16),
        grid_spec=pltpu.PrefetchScalarGridSpec(
            num_scalar_prefetch=1, grid=(nb,),
            in_specs=[pl.BlockSpec((n, tr, L), lambda i, ids: (0, ids[1] * nb + i, 0)), blk], out_specs=blk),
        compiler_params=_params(16, 1),
    )(ids, g, theirs)


def _sum_chips(pair, parts, ids, *, name):
    _, R, L = parts.shape
    tr = math.gcd(R, 128)

    def body(ids_ref, o_ref, r_ref, out_ref):
        out_ref[...] = ((o_ref[...].astype(F32) + r_ref[0].astype(F32)) + r_ref[1].astype(F32)) + r_ref[2].astype(F32)

    return pl.pallas_call(
        body, name=name, out_shape=jax.ShapeDtypeStruct((2, R, L), F32),
        grid_spec=pltpu.PrefetchScalarGridSpec(
            num_scalar_prefetch=1, grid=(R // tr,),
            in_specs=[pl.BlockSpec((None, tr, L), lambda i, ids: (ids[0], i, 0)),
                      pl.BlockSpec((3, tr, L), lambda i, ids: (0, i, 0))],
            out_specs=pl.BlockSpec((None, tr, L), lambda i, ids: (ids[1], i, 0))),
        compiler_params=_params(32, 1),
    )(ids, pair, parts)


def _mesh_ids():
    x, y, c = _mesh_pos()
    return jnp.stack([2 * x + y, c]).astype(jnp.int32)


def _place_shard(rows, ids, *, name):
    R, L = rows.shape
    tr = 128

    def body(ids_ref, in_ref, out_ref):
        out_ref[...] = in_ref[...].astype(BF16)

    return pl.pallas_call(
        body, name=name, out_shape=jax.ShapeDtypeStruct((4, R, L), BF16),
        grid_spec=pltpu.PrefetchScalarGridSpec(
            num_scalar_prefetch=1, grid=(R // tr,), in_specs=[pl.BlockSpec((tr, L), lambda i, ids: (i, 0))],
            out_specs=pl.BlockSpec((None, tr, L), lambda i, ids: (ids[0], i, 0))),
        compiler_params=_params(16, 1),
    )(ids, rows)


def _remote(src, dst, send_sem, recv_sem, to):
    return pltpu.make_async_remote_copy(src_ref=src, dst_ref=dst, send_sem=send_sem, recv_sem=recv_sem,
                                        device_id=to, device_id_type=MESH_IDS)


def _rows(ref, lead, start, size):
    return ref.at[tuple(pl.ds(0, n) for n in ref.shape[:lead]) + (pl.ds(start, size),)]


def _other_chips():
    x, y, _ = _mesh_pos()
    return [(1 - x, y), (x, 1 - y), (1 - x, 1 - y)]


def _plan_gather_ici(bufs):
    n = len(bufs)

    def copies(outs, send, recv):
        x, y, c = _mesh_pos()
        res = []
        for b in range(n):
            half = bufs[b].shape[1] // 2
            mine = _rows(outs[b].at[2 * x + y], 0, c * half, half)
            for j, (cx, cy) in enumerate(_other_chips()):
                res.append((_remote(mine, mine, send(3 * b + j), recv(3 * b + j), (cx, cy, c)),
                            _remote(mine, _rows(outs[b].at[2 * cx + cy], 0, c * half, half),
                                    send(3 * b + j), recv(3 * b + j), (x, y, c))))
        return res

    def start(ins, outs, send, recv, loc):
        for out_cp, _ in copies(outs, send, recv):
            out_cp.start()

    def wait(ins, outs, send, recv, loc):
        for out_cp, in_cp in copies(outs, send, recv):
            in_cp.wait_recv()
            out_cp.wait_send()

    outs = [jax.ShapeDtypeStruct(b.shape, b.dtype) for b in bufs]
    return _Plan(bufs, outs, 3 * n, 0, start, wait, aliases={b: b for b in range(n)})


def _plan_gather_forward(bufs):
    n = len(bufs)

    def copies(outs, send, recv):
        x, y, c = _mesh_pos()
        res = []
        for b in range(n):
            half = bufs[b].shape[1] // 2
            for j, (cx, cy) in enumerate(_other_chips()):
                slot = outs[b].at[2 * cx + cy]
                res.append((_remote(_rows(slot, 0, c * half, half), _rows(slot, 0, c * half, half),
                                    send(3 * b + j), recv(3 * b + j), (x, y, 1 - c)),
                            _remote(_rows(slot, 0, c * half, half), _rows(slot, 0, (1 - c) * half, half),
                                    send(3 * b + j), recv(3 * b + j), (x, y, c))))
        return res

    def start(ins, outs, send, recv, loc):
        for out_cp, _ in copies(outs, send, recv):
            out_cp.start()

    def wait(ins, outs, send, recv, loc):
        for out_cp, in_cp in copies(outs, send, recv):
            in_cp.wait_recv()
            out_cp.wait_send()

    outs = [jax.ShapeDtypeStruct(b.shape, b.dtype) for b in bufs]
    return _Plan(bufs, outs, 3 * n, 0, start, wait, aliases={b: b for b in range(n)})


def _plan_pair_swap(g):
    half = g.shape[1] // 2

    def copy(ins, outs, send, recv, loc):
        x, y, c = _mesh_pos()
        return _remote(_rows(ins[0], 1, (1 - c) * half, half), outs[0], send(0), recv(0), (x, y, 1 - c))

    return _Plan([g], [jax.ShapeDtypeStruct((4, half, g.shape[2]), g.dtype)], 1, 0,
                 lambda *a: copy(*a).start(), lambda *a: copy(*a).wait())


def _plan_pair_gather(buf):
    def copies(ins, outs, send, recv, loc):
        x, y, c = _mesh_pos()
        return (_remote(outs[0].at[c], outs[0].at[c], send(0), recv(0), (x, y, 1 - c)),
                _remote(outs[0].at[c], outs[0].at[1 - c], send(0), recv(0), (x, y, c)))

    def wait(*a):
        out_cp, in_cp = copies(*a)
        in_cp.wait_recv()
        out_cp.wait_send()

    return _Plan([buf], [jax.ShapeDtypeStruct(buf.shape, buf.dtype)], 1, 0, lambda *a: copies(*a)[0].start(), wait,
                 aliases={0: 0})


def _plan_chip_scatter(p):
    def copies(ins, outs, send, recv, loc):
        _, _, c = _mesh_pos()
        return [_remote(ins[0].at[2 * cx + cy], outs[0].at[j], send(j), recv(j), (cx, cy, c))
                for j, (cx, cy) in enumerate(_other_chips())]

    def start(*a):
        for cp in copies(*a):
            cp.start()

    def wait(*a):
        for cp in copies(*a):
            cp.wait()

    return _Plan([p], [jax.ShapeDtypeStruct((3,) + p.shape[1:], p.dtype)], 3, 0, start, wait)


def _plan_exchange_all(vec):
    def copies(ins, outs, send, recv, loc):
        x, y, c = _mesh_pos()
        return [_remote(ins[0], outs[0].at[r - 1], send(r - 1), recv(r - 1), (x ^ (r >> 2), y ^ ((r >> 1) & 1), c ^ (r & 1)))
                for r in range(1, 8)]

    def start(*a):
        for cp in copies(*a):
            cp.start()

    def wait(*a):
        for cp in copies(*a):
            cp.wait()

    return _Plan([vec], [jax.ShapeDtypeStruct((7,) + vec.shape, vec.dtype)], 7, 0, start, wait)


SMALL_LAYOUT = {
    "mla_gq": (0, 1, 256, (1, 256)), "mla_gkv": (1, 1, 256, (1, 256)), "sgu_ln_g": (2, 1, 512, (1, 512)),
    "sgu_ln_b": (3, 1, 512, (1, 512)), "sgu_w": (4, 64, 1024, (64, 1024)), "sgu_b": (68, 1, 512, (1, 512)),
    "hg_lb": (69, 2, 1024, (2, 1024)), "hg_gnorm": (71, 1, 1024, (1, 256)), "ln1_g": (72, 2, 1024, (2, 1024)),
    "ln1_b": (74, 2, 1024, (2, 1024)), "ln2_g": (76, 2, 1024, (2, 1024)), "ln2_b": (78, 2, 1024, (2, 1024)),
}


def _small_pack(dgq, dgkv, dslg, dslb, dsw, dsb, dlb, dgn, ln_parts, sq_err):
    flat_ln = [p for pair in ln_parts for p in pair]

    def body(*refs):
        gq_ref, gkv_ref, slg_ref, slb_ref, sw_ref, sb_ref, lb_ref, gn_ref = refs[:8]
        ln_refs, err_ref, out_ref, t_sc = refs[8:16], refs[16], refs[17], refs[18]
        s8 = lambda ref: jnp.sum(ref[...], axis=0, keepdims=True)
        out_ref[...] = jnp.zeros_like(out_ref)
        out_ref[0:1, 0:256] = s8(gq_ref)
        out_ref[1:2, 0:256] = s8(gkv_ref)
        out_ref[2:3, 0:512] = s8(slg_ref)
        out_ref[3:4, 0:512] = s8(slb_ref)
        out_ref[4:68, :] = sw_ref[...]
        t_sc[...] = sb_ref[...].T
        for g in range(SGU_G):
            out_ref[68:69, g * SGU_C:(g + 1) * SGU_C] = t_sc[g:g + 1, :]
        d_lb1 = s8(lb_ref)
        out_ref[69:70, :] = -d_lb1
        out_ref[70:71, :] = d_lb1
        out_ref[71:72, :] = s8(gn_ref)
        for k, ref in enumerate(ln_refs):
            out_ref[72 + k:73 + k, :] = s8(ref)
        out_ref[0:1, 1023:1024] = jnp.sum(s8(err_ref), axis=1, keepdims=True) * (0.5 / D)

    vm = pl.BlockSpec(memory_space=pltpu.VMEM)
    return pl.pallas_call(
        body, name="small_grad_pack", in_specs=[vm] * 17, out_specs=vm,
        out_shape=jax.ShapeDtypeStruct((SMALL_ROWS, 1024), F32), scratch_shapes=[pltpu.VMEM((SGU_C, SGU_C), F32)],
        compiler_params=_params(16),
    )(dgq, dgkv, dslg, dslb, dsw.reshape(64, 1024), dsb, dlb, dgn, *flat_ln, sq_err)


def _small_update(vec, others, ids, w, m, v):
    names = list(SMALL_LAYOUT)
    n = len(names)
    c1, c2 = 1.0 - B1 ** STEP, 1.0 - B2 ** STEP
    have_others = others is not None

    def body(*refs):
        ids_ref, v_ref = refs[0], refs[1]
        k = 2 + have_others
        w_refs, m_refs, v_refs = refs[k:k + n], refs[k + n:k + 2 * n], refs[k + 2 * n:k + 3 * n]
        outs = refs[k + 3 * n:]
        row0_ref, tot_sc = outs[0], outs[-1]
        total = v_ref[...]
        if have_others:
            me = 2 * ids_ref[0] + ids_ref[1]
            total = None
            for d in range(8):
                rel = d ^ me
                term = jnp.where(rel == 0, v_ref[...], refs[2][jnp.maximum(rel - 1, 0)])
                total = term if total is None else total + term
        tot_sc[...] = total
        row0_ref[...] = tot_sc[0:1, :]
        for i, name in enumerate(names):
            r0, nr, width, _ = SMALL_LAYOUT[name]
            if name == "hg_gnorm":
                g_ = tot_sc[r0:r0 + 1, 0:256]
                for chip in range(1, 4):
                    g_ = jnp.where(ids_ref[0] == chip, tot_sc[r0:r0 + 1, chip * 256:(chip + 1) * 256], g_)
            else:
                g_ = tot_sc[r0:r0 + nr, 0:width]
            m_ = B1 * m_refs[i][...] + (1.0 - B1) * g_
            v_ = B2 * v_refs[i][...] + (1.0 - B2) * (g_ * g_)
            go, do, mo, vo = outs[1 + 4 * i:5 + 4 * i]
            go[...] = g_
            do[...] = -LR * ((m_ / c1) / (jnp.sqrt(v_ / c2) + ADAM_EPS) + WD * w_refs[i][...])
            mo[...] = m_
            vo[...] = v_

    full = lambda shape: pl.BlockSpec(shape, lambda i, ids, nd=len(shape): (0,) * nd)
    kshapes = [SMALL_LAYOUT[name][3] for name in names]
    operands = [vec] + ([others] if have_others else []) + [d[name] for d in (w, m, v) for name in names]
    out_shapes = [jax.ShapeDtypeStruct((1, 1024), F32)] + [jax.ShapeDtypeStruct(s, F32) for s in kshapes for _ in range(4)]
    res = pl.pallas_call(
        body, name="small_update", out_shape=out_shapes,
        grid_spec=pltpu.PrefetchScalarGridSpec(
            num_scalar_prefetch=1, grid=(1,), in_specs=[full(o.shape) for o in operands],
            out_specs=[full(s.shape) for s in out_shapes],
            scratch_shapes=[pltpu.VMEM((SMALL_ROWS, 1024), F32)]),
        compiler_params=_params(32, 1),
    )(ids, *operands)
    return res[0], {name: tuple(res[1 + 4 * i:5 + 4 * i]) for i, name in enumerate(names)}


ROWS_L1, ROWS_L0, ROWS_ODD, ROWS_ODD_W = 3328, 2048, 768, 384
ODD_PARTS = (("w_out_e", (256, 1024)), ("w_in_e", (1024, 392)), ("w_qb", (256, 192)), ("w_kvb", (256, 256)))
ODD_W_PARTS = tuple(p for p in ODD_PARTS if p[0] != "w_in_e")


def _odd_rows(parts, dtype, layout, total, gnorm=None):
    rows = [parts[n].reshape(-1, 1024).astype(dtype) for n, _ in layout]
    used = sum(r.shape[0] for r in rows)
    if gnorm is not None:
        bits = lax.bitcast_convert_type(gnorm.reshape(-1), BF16).reshape(1, 512)
        rows.append(jnp.pad(bits, ((0, 0), (0, 512))))
        used += 1
    rows.append(jnp.zeros((total - used, 1024), dtype))
    return jnp.concatenate(rows, axis=0)


def _odd_unrows(buf, layout, with_gnorm=False):
    out, off = {}, 0
    for n, shape in layout:
        nr = math.prod(shape) // 1024
        out[n] = buf[off:off + nr].reshape(shape)
        off += nr
    if with_gnorm:
        out["hg_gnorm"] = lax.bitcast_convert_type(buf[off, :512].reshape(256, 2), F32).reshape(1, 256)
    return out


def _rope_tables(positions):
    half = ROPE // 2
    inv_freq = ROPE_BASE ** (-jnp.arange(half, dtype=F32) / half)
    ang = positions.astype(F32).reshape(-1, 1) * inv_freq
    cos, sin = jnp.cos(ang), jnp.sin(ang)
    T = ang.shape[0]
    one, z16, z32 = jnp.ones((T, NOPE), F32), jnp.zeros((T, half), F32), jnp.zeros((T, 32), F32)
    z64 = jnp.zeros((T, NOPE), F32)
    c = jnp.concatenate([one, cos, cos, z32], axis=1)
    s1 = jnp.concatenate([z64, -sin, z16, z32], axis=1)
    s2 = jnp.concatenate([z64, z16, sin, z32], axis=1)
    return c, s1, s2


def _local_step(x, positions, tgt, odd, bufs, P, exchange):
    T = x.shape[0]
    row = lambda a: a.reshape(1, -1)
    rc, rs1, rs2 = _rope_tables(positions)
    blk = lambda f: pl.BlockSpec((None, D, D), f)

    w_in_e = odd["w_in_e"]
    w_in = jnp.concatenate([w_in_e[:, :512], w_in_e[:, 544:1568], w_in_e[:, 512:544], jnp.zeros((D, 96), BF16)], axis=1)
    wq = jnp.pad(odd["w_qb"].reshape(256, HEADS, NOPE + ROPE), ((0, 0), (0, 0), (0, 32))).reshape(256, HEADS * 128)
    kvb = odd["w_kvb"].reshape(256, HEADS, NOPE + VDIM)
    wk = jnp.pad(kvb[:, :, :NOPE], ((0, 0), (0, 0), (0, 64))).reshape(256, HEADS * 128)
    wv = kvb[:, :, NOPE:].reshape(256, HEADS * VDIM)
    w_out_e = odd["w_out_e"]
    sgu_w = P["sgu_w"][0]
    sgu_bt = P["sgu_b"][0].T
    gq, gkv = P["mla_gq"], P["mla_gkv"]
    gnorm = P["hg_gnorm"]

    z0 = _matmul(x, w_in, name="in_proj_e", M=T, N=1664, K=D, tn=1664)[0]
    q, k, v = _mla_prep(z0, gq, gkv, wq, wk, wv, rc, rs1, rs2)
    if exchange:
        ids = _mesh_ids()
        placed = [_place_shard(b, ids, name=f"place_shard_{l}") for l, b in enumerate(bufs)]
        a_out, lse, wga, wgb = _flash_fwd(q, k, v, plan=_plan_gather_ici(placed[:2]))
    else:
        a_out, lse = _flash_fwd(q, k, v)
        wga, wgb, wgc = bufs
    mix0 = _sgu_fwd(z0, a_out, P["sgu_ln_g"], P["sgu_ln_b"], sgu_w, sgu_bt)
    res = _proj_ln(mix0, w_out_e, x, row(P["ln1_g"][0]), row(P["ln1_b"][0]), name="out_proj_ln_e",
                   plan=_plan_gather_forward([wga, wgb]) if exchange else None)
    r1, h1, h1b = res[:3]
    if exchange:
        wga, wgb = res[3:]
    res = _ffn_ln(h1b, wga, h1, row(P["ln2_g"][0]), row(P["ln2_b"][0]), name="ffn_ln_0",
                  plan=_plan_gather_ici(placed[2:]) if exchange else None)
    ra0, r2, h2, h2b = res[:4]
    z4 = _matmul(h2b, wgb, name="in_proj_o", M=T, N=4 * D, K=D, b_spec=blk(lambda i, j, k: (j, 0, 0)),
                 out_shape=jax.ShapeDtypeStruct((4, T, D), F32),
                 o_spec=pl.BlockSpec((None, min(MM_ROWS, T), D), lambda i, j, k: (j, i, 0)))[0]
    y1, o_raw, states = _hgrn_fwd(z4, P["hg_lb"], gnorm)
    res2 = _proj_ln(y1, wgb, h2, row(P["ln1_g"][1]), row(P["ln1_b"][1]), name="out_proj_ln_o", w_rowblk=4,
                    plan=_plan_gather_forward([res[4]]) if exchange else None)
    r3, h3, h3b = res2[:3]
    if exchange:
        wgc = res2[3]
    ra1, r4, h4, _ = _ffn_ln(h3b, wgc, h3, row(P["ln2_g"][1]), row(P["ln2_b"][1]), name="ffn_ln_1")

    ln1_g, ln1_b, ln2_g, ln2_b = [None, None], [None, None], [None, None], [None, None]
    sq_err_parts = []

    def ffn_bwd(l, dh, r_out, ra, h_mid_b, g2, wg, rows, plan=None, tgt=None):
        dr, dr_b, dg, db, *sq_err = _ln_bwd(dh, r_out, row(g2), name=f"ln2_bwd_{l}", tgt=tgt)
        sq_err_parts.extend(sq_err)
        ln2_g[l], ln2_b[l] = dg, db
        da, *extra = _matmul(dr_b, wg, tb=True, mul=ra, out_dtype=BF16, name=f"ffn_da_{l}", M=T, N=4 * D, K=D,
                             b_spec=blk(lambda i, j, k: (j, 1, 0)), plan=plan)
        gbuf = _matmul(ra, dr_b, ta=True, a_sq=True, name=f"ffn_dw2_{l}", M=4 * D, N=D, K=T, tm=1024, tk=DW_TOKENS,
                       out_shape=jax.ShapeDtypeStruct((4, rows, D), BF16), o_spec=blk(lambda i, j, k: (i, 1, 0)))[0]
        gbuf = _matmul(h_mid_b, da, ta=True, name=f"ffn_dw1_{l}", M=D, N=4 * D, K=T, tm=1024, tk=DW_TOKENS, into=gbuf,
                       out_shape=jax.ShapeDtypeStruct((4, rows, D), BF16), o_spec=blk(lambda i, j, k: (j, 0, 0)))[0]
        dh_mid = _matmul(da, wg, tb=True, add=dr, add_scale=ALPHA, name=f"ffn_dh_{l}", M=T, N=D, K=4 * D, tk=2 * D,
                         b_spec=pl.BlockSpec((2, D, D), lambda i, j, k: (k, 0, 0)))[0]
        return dh_mid, gbuf, extra

    dh3, g1, _ = ffn_bwd(1, h4, r4, ra1, h3b, P["ln2_g"][1], wgc, ROWS_L1, tgt=tgt)
    loss_parts = sq_err_parts[0]
    dr3, dr3_b, dg, db = _ln_bwd(dh3, r3, row(P["ln1_g"][1]), name="ln1_bwd_1")
    ln1_g[1], ln1_b[1] = dg, db
    g1_sds = jax.ShapeDtypeStruct((4, ROWS_L1, D), BF16)
    g1 = _matmul(y1, dr3_b, ta=True, name="dw_out_o", M=D, N=D, K=T, tm=256, tk=DW_TOKENS, into=g1, out_shape=g1_sds,
                 o_spec=pl.BlockSpec((None, 256, D), lambda i, j, k: (i, 12, 0)))[0]
    dmix1 = _matmul(dr3_b, wgb, tb=True, name="dmix_o", M=T, N=D, K=D, b_spec=_rows4_spec(4, 3), b_merge=(D, D))[0]
    dz4, dlb, dgn = _hgrn_bwd(z4, o_raw, dmix1, states, P["hg_lb"], gnorm)
    g1 = _matmul(h2b, dz4, ta=True, name="dw_in_o", M=D, N=4 * D, K=T, tm=1024, tk=DW_TOKENS, into=g1, out_shape=g1_sds,
                 b_spec=pl.BlockSpec((None, min(DW_TOKENS, T), D), lambda i, j, k: (j, k, 0)),
                 o_spec=blk(lambda i, j, k: (j, 2, 0)))[0]
    dh2 = _matmul(dz4, wgb, tb=True, add=dr3, add_scale=ALPHA, name="dh_in_o", M=T, N=D, K=4 * D,
                  a_spec=pl.BlockSpec((None, min(MM_ROWS, T), D), lambda i, j, k: (k, i, 0)),
                  b_spec=blk(lambda i, j, k: (k, 0, 0)))[0]

    dh1, g0, swapped1 = ffn_bwd(0, dh2, r2, ra0, h1b, P["ln2_g"][0], wga, ROWS_L0,
                                plan=_plan_pair_swap(g1) if exchange else None)
    dr1, dr1_b, dg, db = _ln_bwd(dh1, r1, row(P["ln1_g"][0]), name="ln1_bwd_0")
    ln1_g[0], ln1_b[0] = dg, db
    godd = {"w_out_e": _matmul(mix0, dr1_b, ta=True, name="dw_out_e", M=D, N=D, K=T, tm=1024, tk=DW_TOKENS)[0]}
    dmix0, *swapped0 = _matmul(dr1_b, w_out_e, tb=True, name="dmix_e", M=T, N=D, K=D,
                               plan=_plan_pair_swap(g0) if exchange else None)
    delta, do_b = _attn_delta(dmix0, a_out)
    if exchange:
        pair1 = _add_pairs(g1, swapped1[0], ids, name="grad_pair_add_1")
        pair0 = _add_pairs(g0, swapped0[0], ids, name="grad_pair_add_0")
        dq4, dk, dv, parts0, parts1 = _flash_bwd(
            q, k, v, do_b, lse, delta, plan=_join_plans([_plan_chip_scatter(pair0), _plan_chip_scatter(pair1)]))
        half0 = _sum_chips(pair0, parts0, ids, name="grad_chip_sum_0")
        half1 = _sum_chips(pair1, parts1, ids, name="grad_chip_sum_1")
        dc, dkr, dwq, dwk, dwv, dgq, dgkv, g0, g1 = _mla_bwd(
            z0, dq4, dk, dv, gq, gkv, wq, wk, wv, rc, rs1, rs2,
            plan=_join_plans([_plan_pair_gather(half0), _plan_pair_gather(half1)]))
        g0, g1 = g0.reshape(ROWS_L0, D), g1.reshape(ROWS_L1, D)
    else:
        dq4, dk, dv = _flash_bwd(q, k, v, do_b, lse, delta)
        dc, dkr, dwq, dwk, dwv, dgq, dgkv = _mla_bwd(z0, dq4, dk, dv, gq, gkv, wq, wk, wv, rc, rs1, rs2)
    dz0, dsw, dsb, dslg, dslb = _sgu_bwd(z0, dmix0, dc, dkr, P["sgu_ln_g"], P["sgu_ln_b"], sgu_w, sgu_bt)
    small_vec = _small_pack(dgq, dgkv, dslg, dslb, dsw, dsb, dlb, dgn, [ln1_g, ln1_b, ln2_g, ln2_b], loss_parts)
    dw_in, *small_others = _matmul(x, dz0, ta=True, name="dw_in_e", M=D, N=1664, K=T, tm=1024, tn=1664,
                                   tk=DW_TOKENS // 2, plan=_plan_exchange_all(small_vec) if exchange else None)
    godd["w_in_e"] = jnp.concatenate([dw_in[:, :512], dw_in[:, 1536:1568], dw_in[:, 512:1536]], axis=1)
    godd["w_qb"] = dwq.reshape(256, HEADS, 128)[:, :, :NOPE + ROPE].reshape(256, HEADS * (NOPE + ROPE))
    godd["w_kvb"] = jnp.concatenate([dwk.reshape(256, HEADS, 128)[:, :, :NOPE], dwv.reshape(256, HEADS, VDIM)],
                                    axis=2).reshape(256, HEADS * (NOPE + VDIM))
    odd_plan = None
    if exchange:
        by_chip = [_odd_rows({"w_out_e": jnp.split(godd["w_out_e"], 4, axis=0)[j],
                              **{n: jnp.split(godd[n], 4, axis=1)[j] for n in ("w_qb", "w_kvb")}}, BF16,
                             ODD_W_PARTS, ROWS_ODD_W)
                   for j in range(4)]
        bufs_odd = [godd["w_in_e"].reshape(D, 4, 392).transpose(1, 0, 2).astype(BF16), jnp.stack(by_chip)]
        theirs = _run_plan(_join_plans([_plan_pair_swap(b) for b in bufs_odd]), name="odd_pair_swap")
        odd_pairs = [_add_pairs(b, t, ids, name=f"odd_pair_add_{k}") for k, (b, t) in enumerate(zip(bufs_odd, theirs))]
        odd_plan = _join_plans([_plan_chip_scatter(p) for p in odd_pairs])
    grad_x, *odd_parts = _matmul(dz0, w_in, tb=True, add=dr1, add_scale=ALPHA, name="dx", M=T, N=D, K=1664, tk=1664,
                                 plan=odd_plan)
    if exchange:
        godd = (odd_pairs, odd_parts)
    return grad_x, g0, g1, godd, small_vec, (small_others[0] if exchange else None)


WEIGHTS = ['w_in_e', 'mla_gq', 'mla_gkv', 'w_qb', 'w_kvb', 'sgu_ln_g', 'sgu_ln_b', 'sgu_w', 'sgu_b', 'w_out_e',
           'w_in_o', 'hg_lb', 'hg_gnorm', 'w_out_o', 'ln1_g', 'ln1_b', 'w_ff1', 'w_ff2', 'ln2_g', 'ln2_b']


def kernel(x, positions, w_in_e, mla_gq, mla_gkv, w_qb, w_kvb, sgu_ln_g, sgu_ln_b, sgu_w, sgu_b, w_out_e, w_in_o, hg_lb, hg_gnorm, w_out_o, ln1_g, ln1_b, w_ff1, w_ff2, ln2_g, ln2_b, loss_target, m_w_in_e, m_mla_gq, m_mla_gkv, m_w_qb, m_w_kvb, m_sgu_ln_g, m_sgu_ln_b, m_sgu_w, m_sgu_b, m_w_out_e, m_w_in_o, m_hg_lb, m_hg_gnorm, m_w_out_o, m_ln1_g, m_ln1_b, m_w_ff1, m_w_ff2, m_ln2_g, m_ln2_b, v_w_in_e, v_mla_gq, v_mla_gkv, v_w_qb, v_w_kvb, v_sgu_ln_g, v_sgu_ln_b, v_sgu_w, v_sgu_b, v_w_out_e, v_w_in_o, v_hg_lb, v_hg_gnorm, v_w_out_o, v_ln1_g, v_ln1_b, v_w_ff1, v_w_ff2, v_ln2_g, v_ln2_b):
    args = dict(locals())
    w = {n: args[n] for n in WEIGHTS}
    m = {n: args["m_" + n] for n in WEIGHTS}
    v = {n: args["v_" + n] for n in WEIGHTS}
    cx, cy, cc = _mesh_pos()
    chip = 2 * cx + cy

    odd_shard = _odd_rows({"w_out_e": w_out_e[0], "w_qb": w_qb[0], "w_kvb": w_kvb[0]}, BF16, ODD_W_PARTS, ROWS_ODD_W,
                          gnorm=hg_gnorm)
    ids = _mesh_ids()
    placed = [_place_shard(w_in_e[0], ids, name="place_shard_in_e"), _place_shard(odd_shard, ids, name="place_shard_odd")]
    gathered = _run_plan(_plan_gather_forward(_run_plan(_plan_gather_ici(placed), name="odd_gather")),
                         name="odd_gather_forward")
    per_chip = [_odd_unrows(gathered[1][j], ODD_W_PARTS, with_gnorm=True) for j in range(4)]
    odd = {"w_out_e": jnp.concatenate([p["w_out_e"] for p in per_chip], axis=0),
           "w_in_e": jnp.concatenate([gathered[0][j] for j in range(4)], axis=1)}
    for n in ("w_qb", "w_kvb"):
        odd[n] = jnp.concatenate([p[n] for p in per_chip], axis=1)
    small = {n: w[n] for n in SMALL_LAYOUT if n != "hg_gnorm"}
    small["hg_gnorm"] = jnp.concatenate([p["hg_gnorm"] for p in per_chip], axis=1)
    shard_rows = (jnp.concatenate([w_ff1[0], w_ff2[0]], axis=0).astype(BF16),
                  jnp.concatenate([w_in_o[0], w_out_o[0]], axis=0).astype(BF16),
                  jnp.concatenate([w_ff1[1], w_ff2[1]], axis=0).astype(BF16))

    grad_x, g_l0, g_l1, godd, small_vec, small_others = _local_step(
        x[0], positions[0], loss_target[0], odd, shard_rows, small, True)

    sums = [_sum_chips(pair, parts, ids, name=f"odd_chip_sum_{k}") for k, (pair, parts) in enumerate(zip(*godd))]
    g_in_e, g_rest = _run_plan(_join_plans([_plan_pair_gather(s) for s in sums]), name="odd_pair_gather")
    g_odd = _odd_unrows(g_rest.reshape(ROWS_ODD_W, 1024), ODD_W_PARTS)
    g_odd["w_in_e"] = g_in_e.reshape(D, 392)

    to_kernel = lambda d: {n: d[n].reshape(SMALL_LAYOUT[n][3]) for n in SMALL_LAYOUT}
    first_row, small_out = _small_update(small_vec, small_others, ids, to_kernel(w), to_kernel(m), to_kernel(v))
    loss = first_row[0, 1023]
    grads, delta, new_m, new_v = {}, {}, {}, {}
    for n, res in small_out.items():
        grads[n], delta[n], new_m[n], new_v[n] = (r.reshape(w[n].shape) for r in res)

    for n, bufs_, row0 in (("w_ff1", [g_l0, g_l1], 0), ("w_ff2", [g_l0, g_l1], 1024), ("w_in_o", [g_l1], 2048),
                           ("w_out_o", [g_l1], 3072)):
        grads[n], delta[n], new_m[n], new_v[n] = _adamw_rows(w[n], m[n], v[n], bufs_, row0, name=f"adamw_{n}")
    for n, _ in ODD_PARTS:
        grads[n] = g_odd[n][None]
        d_, m_, v_ = _adamw(w[n][0], g_odd[n], m[n][0], v[n][0], name=f"adamw_{n}")
        delta[n], new_m[n], new_v[n] = d_[None], m_[None], v_[None]

    return (loss, grad_x[None], *[grads[n] for n in WEIGHTS], *[delta[n] for n in WEIGHTS],
            *[new_m[n] for n in WEIGHTS], *[new_v[n] for n in WEIGHTS])
```

```python
import math

import jax
import jax.numpy as jnp
from jax import lax
from jax.experimental import pallas as pl
from jax.experimental.pallas import tpu as pltpu

F32 = jnp.float32
BF16 = jnp.bfloat16
MESH_IDS = pl.DeviceIdType.MESH

D = 1024
DEPTH = 2
HEADS = 8
NOPE, ROPE, VDIM = 64, 32, 64
QK_SCALE = (NOPE + ROPE) ** -0.5
ROPE_BASE = 10000.0
SGU_G, SGU_C = 4, 128
HG_CHUNK = 64
HG_HEADS_PER_STEP = 8
ALPHA = (2 * DEPTH) ** 0.25
EPS = 1e-5
LR, B1, B2, ADAM_EPS, WD, STEP = 0.001, 0.9, 0.999, 1e-08, 0.01, 10
GELU_C = math.sqrt(2.0 / math.pi)
GELU_A = 0.044715
MB = 1024 * 1024
ROW_BLOCK = 512
SMALL_ROWS = 80

NT_DIMS = (((1,), (1,)), ((), ()))
TN_DIMS = (((0,), (0,)), ((), ()))


def _params(vmem_mb, n_axes=0):
    kw = dict(vmem_limit_bytes=vmem_mb * MB)
    if n_axes:
        kw["dimension_semantics"] = ("arbitrary",) * n_axes
    return pltpu.CompilerParams(**kw)


_ANY = pl.BlockSpec(memory_space=pltpu.HBM)


def _mesh_pos():
    return lax.axis_index("x"), lax.axis_index("y"), lax.axis_index("c")


def _hbm(*arrays):
    return tuple(pltpu.with_memory_space_constraint(a, pltpu.HBM) if a.size >= 2 ** 18 else a for a in arrays)


class _Plan:
    def __init__(self, ins, outs, n_remote, n_local, start, wait, aliases=None):
        self.ins, self.outs, self.n_remote, self.n_local = list(ins), list(outs), n_remote, n_local
        self.start, self.wait, self.aliases = start, wait, dict(aliases or {})


def _join_plans(plans):
    ins, outs, aliases, parts = [], [], {}, []
    nr = nl = 0
    for p in plans:
        parts.append((p, len(ins), len(outs), nr, nl))
        aliases.update({len(ins) + i: len(outs) + o for i, o in p.aliases.items()})
        ins += p.ins
        outs += p.outs
        nr += p.n_remote
        nl += p.n_local

    def run(which):
        def go(in_refs, out_refs, send, recv, loc):
            for p, i0, o0, r0, l0 in parts:
                getattr(p, which)(in_refs[i0:i0 + len(p.ins)], out_refs[o0:o0 + len(p.outs)],
                                  lambda i, r0=r0: send(r0 + i), lambda i, r0=r0: recv(r0 + i),
                                  lambda i, l0=l0: loc(l0 + i))
        return go

    return _Plan(ins, outs, nr, nl, run("start"), run("wait"), aliases)


def _plan_io(plan, n_in, n_out):
    if plan is None:
        return [], [], [], [], {}
    sems = [pltpu.SemaphoreType.DMA((max(plan.n_remote, 1),)), pltpu.SemaphoreType.DMA((max(plan.n_remote, 1),)),
            pltpu.SemaphoreType.DMA((max(plan.n_local, 1),))]
    aliases = {n_in + i: n_out + o for i, o in plan.aliases.items()}
    return plan.ins, [_ANY] * len(plan.outs), plan.outs, sems, aliases


def _split_refs(refs, n_in, n_out, n_scr, plan):
    p_in, p_out = (len(plan.ins), len(plan.outs)) if plan is not None else (0, 0)
    refs = list(refs)
    ins, refs = refs[:n_in], refs[n_in:]
    pins, refs = refs[:p_in], refs[p_in:]
    outs, refs = refs[:n_out], refs[n_out:]
    pouts, refs = refs[:p_out], refs[p_out:]
    scr, psem = refs[:n_scr], refs[n_scr:]
    psem = tuple((lambda i, s=s: s.at[i]) for s in psem)
    return ins, outs, scr, (pins, pouts, psem)


def _grid_edge(grid, last):
    cond = None
    for ax, n in enumerate(grid):
        c = pl.program_id(ax) == (n - 1 if last else 0)
        cond = c if cond is None else cond & c
    return cond


def _plan_start(plan, pctx, grid):
    if plan is not None:
        pins, pouts, psem = pctx
        pl.when(_grid_edge(grid, False))(lambda: plan.start(pins, pouts, *psem))


def _plan_wait(plan, pctx, grid):
    if plan is not None:
        pins, pouts, psem = pctx
        pl.when(_grid_edge(grid, True))(lambda: plan.wait(pins, pouts, *psem))


def _run_plan(plan, *, name):
    def body(*refs):
        _, _, _, (pins, pouts, psem) = _split_refs(refs, 0, 0, 0, plan)
        plan.start(pins, pouts, *psem)
        plan.wait(pins, pouts, *psem)

    p_in, p_ospec, p_oshape, p_scr, p_alias = _plan_io(plan, 0, 0)
    return pl.pallas_call(body, name=name, in_specs=[_ANY] * len(p_in), out_specs=p_ospec, out_shape=p_oshape,
                          scratch_shapes=p_scr, input_output_aliases=p_alias)(*p_in)


def _fold8(x):
    return x.reshape(x.shape[0] // 8, 8, x.shape[1]).sum(axis=0)


def _ln_stats(r):
    mu = jnp.mean(r, -1, keepdims=True)
    xc = r - mu
    rstd = lax.rsqrt(jnp.mean(xc * xc, -1, keepdims=True) + EPS)
    return xc * rstd, rstd


def _sigmoid(x):
    return jax.nn.sigmoid(x)


def _gelu(x):
    return 0.5 * x * (1.0 + jnp.tanh(GELU_C * (x + GELU_A * x * x * x)))


def _gelu_grad(x):
    t = jnp.tanh(GELU_C * (x + GELU_A * x * x * x))
    return 0.5 * (1.0 + t) + 0.5 * x * (1.0 - t * t) * GELU_C * (1.0 + 3.0 * GELU_A * x * x)


MM_ROWS = 1024
DW_TOKENS = 2048


def _matmul(a, b, *, name, M, N, K, ta=False, tb=False, out_dtype=F32, tm=MM_ROWS, tn=1024, tk=1024,
            a_spec=None, b_spec=None, b_merge=None, out_shape=None, o_spec=None, into=None,
            a_sq=False, mul=None, add=None, add_scale=1.0, plan=None):
    tm, tn, tk = min(tm, M), min(tn, N), min(tk, K)
    assert M % tm == 0 and N % tn == 0 and K % tk == 0
    grid = (M // tm, N // tn, K // tk)
    nk = grid[2]
    if a_spec is None:
        a_spec = pl.BlockSpec((tk, tm), lambda i, j, k: (k, i)) if ta else pl.BlockSpec((tm, tk), lambda i, j, k: (i, k))
    if b_spec is None:
        b_spec = pl.BlockSpec((tn, tk), lambda i, j, k: (j, k)) if tb else pl.BlockSpec((tk, tn), lambda i, j, k: (k, j))
    if o_spec is None:
        o_spec = pl.BlockSpec((tm, tn), lambda i, j, k: (i, j))
        out_shape = jax.ShapeDtypeStruct((M, N), out_dtype)
    e_spec = pl.BlockSpec((tm, tn), lambda i, j, k: (i, j))
    dims = (((0 if ta else 1,), (1 if tb else 0,)), ((), ()))
    extra = [e for e in (mul, add, into) if e is not None]
    n_in = 2 + len(extra)

    def body(*refs):
        ins, outs, scr, pctx = _split_refs(refs, n_in, 1, 1 if nk > 1 else 0, plan)
        a_ref, b_ref = ins[0], ins[1]
        rest = list(ins[2:])
        mul_ref = rest.pop(0) if mul is not None else None
        add_ref = rest.pop(0) if add is not None else None
        o_ref = outs[0]
        _plan_start(plan, pctx, grid)
        av = a_ref[...].astype(BF16)
        if a_sq:
            av = av * av
        bv = b_ref[...]
        if b_merge is not None:
            bv = bv.reshape(b_merge)
        if bv.ndim == 3:
            w = av.shape[-1] // (1 if av.ndim == 3 else bv.shape[0])
            a_parts = [av[s] if av.ndim == 3 else av[:, s * w:(s + 1) * w] for s in range(bv.shape[0])]
            p = sum(lax.dot_general(a_parts[s], bv[s], dims, preferred_element_type=F32) for s in range(bv.shape[0]))
        else:
            p = lax.dot_general(av, bv, dims, preferred_element_type=F32)

        def finish(r):
            if mul_ref is not None:
                r = r * (2.0 * mul_ref[...].astype(F32))
            if add_ref is not None:
                r = r + add_scale * add_ref[...]
            o_ref[...] = r.astype(o_ref.dtype)

        if nk == 1:
            finish(p)
        else:
            acc_ref = scr[0]
            k = pl.program_id(2)

            @pl.when(k == 0)
            def _():
                acc_ref[...] = p

            @pl.when(k > 0)
            def _():
                acc_ref[...] += p

            @pl.when(k == nk - 1)
            def _():
                finish(acc_ref[...])

        _plan_wait(plan, pctx, grid)

    p_in, p_ospec, p_oshape, p_scr, p_alias = _plan_io(plan, n_in, 1)
    aliases = dict(p_alias)
    if into is not None:
        aliases[n_in - 1] = 0
    return pl.pallas_call(
        body, name=name, grid=grid,
        in_specs=[a_spec, b_spec] + [e_spec] * (len(extra) - (into is not None)) + [_ANY] * (into is not None)
        + [_ANY] * len(p_in),
        out_specs=[o_spec] + p_ospec, out_shape=[out_shape] + p_oshape,
        scratch_shapes=([pltpu.VMEM((tm, tn), F32)] if nk > 1 else []) + p_scr,
        input_output_aliases=aliases, compiler_params=_params(48, 3),
    )(*_hbm(a, b, *extra), *p_in)


def _rows4_spec(rowblk, n_axes):
    return pl.BlockSpec((4, 256, D), lambda *_: (0, rowblk, 0))


def _proj_ln(a_b, w, h_prev, g, b, *, name, w_rowblk=None, plan=None):
    T = a_b.shape[0]
    tm = min(ROW_BLOCK, T)
    grid = (T // tm,)
    row = pl.BlockSpec((tm, D), lambda i: (i, 0))
    vec = pl.BlockSpec((1, D), lambda i: (0, 0))
    w_spec = pl.BlockSpec((D, D), lambda i: (0, 0)) if w_rowblk is None else _rows4_spec(w_rowblk, 1)

    def body(*refs):
        (a_ref, w_ref, h_ref, g_ref, b_ref), (r_ref, ho_ref, hb_ref), _, pctx = _split_refs(refs, 5, 3, 0, plan)
        _plan_start(plan, pctx, grid)
        mix = jnp.dot(a_ref[...], w_ref[...].reshape(D, D), preferred_element_type=F32)
        r = ALPHA * h_ref[...] + mix
        xhat, _ = _ln_stats(r)
        y = xhat * g_ref[...] + b_ref[...]
        r_ref[...] = r
        ho_ref[...] = y
        hb_ref[...] = y.astype(BF16)
        _plan_wait(plan, pctx, grid)

    p_in, p_ospec, p_oshape, p_scr, p_alias = _plan_io(plan, 5, 3)
    return pl.pallas_call(
        body, name=name, grid=grid,
        in_specs=[row, w_spec, row, vec, vec] + [_ANY] * len(p_in),
        out_specs=[row, row, row] + p_ospec,
        out_shape=[jax.ShapeDtypeStruct((T, D), F32), jax.ShapeDtypeStruct((T, D), F32),
                   jax.ShapeDtypeStruct((T, D), BF16)] + p_oshape,
        scratch_shapes=p_scr, input_output_aliases=p_alias, compiler_params=_params(40, 1),
    )(*_hbm(a_b, w, h_prev, g, b), *p_in)


def _ffn_ln(h_b, wbuf, h, g, b, *, name, plan=None):
    T = h_b.shape[0]
    tm, tf = min(ROW_BLOCK, T), 1024
    nf = 4
    F = nf * tf
    grid = (T // tm, nf)
    row = pl.BlockSpec((tm, D), lambda i, j: (i, 0))
    vec = pl.BlockSpec((1, D), lambda i, j: (0, 0))

    def body(*refs):
        ((hb_ref, w1_ref, w2_ref, h_ref, g_ref, b_ref), (ra_ref, r_ref, ho_ref, hbo_ref), (acc_ref,),
         pctx) = _split_refs(refs, 6, 4, 1, plan)
        _plan_start(plan, pctx, grid)
        j = pl.program_id(1)
        a = jnp.dot(hb_ref[...], w1_ref[...], preferred_element_type=F32)
        ra = jnp.maximum(a, 0.0)
        ra_ref[...] = ra.astype(BF16)
        p = jnp.dot((ra * ra).astype(BF16), w2_ref[...], preferred_element_type=F32)

        @pl.when(j == 0)
        def _():
            acc_ref[...] = p

        @pl.when(j > 0)
        def _():
            acc_ref[...] += p

        @pl.when(j == nf - 1)
        def _():
            r = ALPHA * h_ref[...] + acc_ref[...]
            xhat, _ = _ln_stats(r)
            y = xhat * g_ref[...] + b_ref[...]
            r_ref[...] = r
            ho_ref[...] = y
            hbo_ref[...] = y.astype(BF16)

        _plan_wait(plan, pctx, grid)

    p_in, p_ospec, p_oshape, p_scr, p_alias = _plan_io(plan, 6, 4)
    return pl.pallas_call(
        body, name=name, grid=grid,
        in_specs=[row, pl.BlockSpec((None, D, tf), lambda i, j: (j, 0, 0)),
                  pl.BlockSpec((None, tf, D), lambda i, j: (j, 1, 0)), row, vec, vec] + [_ANY] * len(p_in),
        out_specs=[pl.BlockSpec((tm, tf), lambda i, j: (i, j)), row, row, row] + p_ospec,
        out_shape=[jax.ShapeDtypeStruct((T, F), BF16), jax.ShapeDtypeStruct((T, D), F32),
                   jax.ShapeDtypeStruct((T, D), F32), jax.ShapeDtypeStruct((T, D), BF16)] + p_oshape,
        scratch_shapes=[pltpu.VMEM((tm, D), F32)] + p_scr,
        input_output_aliases=p_alias, compiler_params=_params(48, 2),
    )(*_hbm(h_b, wbuf, wbuf, h, g, b), *p_in)


def _ln_bwd(dy, r, g, *, name, tgt=None):
    T = dy.shape[0]
    tm = min(ROW_BLOCK, T)
    row = pl.BlockSpec((tm, D), lambda i: (i, 0))
    acc = pl.BlockSpec((8, D), lambda i: (0, 0))
    n_in = 3 + (tgt is not None)

    def body(*refs):
        dy_ref, r_ref, g_ref = refs[:3]
        dr_ref, drb_ref, dg_ref, db_ref = refs[n_in:n_in + 4]

        @pl.when(pl.program_id(0) == 0)
        def _():
            for ref in refs[n_in + 2:]:
                ref[...] = jnp.zeros_like(ref)

        dy_ = dy_ref[...]
        if tgt is not None:
            err = dy_ - refs[3][...]
            refs[n_in + 4][...] += _fold8(err * err)
            dy_ = err * (1.0 / D)
        xhat, rstd = _ln_stats(r_ref[...])
        dxh = dy_ * g_ref[...]
        m1 = jnp.mean(dxh, -1, keepdims=True)
        m2 = jnp.mean(dxh * xhat, -1, keepdims=True)
        dr = rstd * (dxh - m1 - xhat * m2)
        dr_ref[...] = dr
        drb_ref[...] = dr.astype(BF16)
        dg_ref[...] += _fold8(dy_ * xhat)
        db_ref[...] += _fold8(dy_)

    extra = [] if tgt is None else [tgt]
    return pl.pallas_call(
        body, name=name, grid=(T // tm,),
        in_specs=[row, row, pl.BlockSpec((1, D), lambda i: (0, 0))] + [row] * len(extra),
        out_specs=[row, row, acc, acc] + [acc] * len(extra),
        out_shape=[jax.ShapeDtypeStruct((T, D), F32), jax.ShapeDtypeStruct((T, D), BF16)]
        + [jax.ShapeDtypeStruct((8, D), F32)] * (2 + len(extra)),
        compiler_params=_params(40, 1),
    )(*_hbm(dy, r, g, *extra))


def _rope(x, c, s1, s2):
    return x * c + pltpu.roll(x, 112, 1) * s1 + pltpu.roll(x, 16, 1) * s2


def _rope_t(dy, c, s1, s2):
    return dy * c + pltpu.roll(dy * s1, 16, 1) + pltpu.roll(dy * s2, 112, 1)


def _rms(x, g):
    rstd = lax.rsqrt(jnp.mean(x * x, -1, keepdims=True) + EPS)
    xhat = x * rstd
    return xhat * g, xhat, rstd


def _mla_prep(z0, gq, gkv, wq, wk, wv, rc, rs1, rs2):
    T = z0.shape[0]
    tm = min(ROW_BLOCK, T)
    HW = HEADS * 128

    def body(cq_ref, ckv_ref, kr_ref, gq_ref, gkv_ref, wq_ref, wk_ref, wv_ref, c_ref, s1_ref, s2_ref,
             q_ref, k_ref, v_ref):
        nq = _rms(cq_ref[...], gq_ref[...])[0].astype(BF16)
        nkv = _rms(ckv_ref[...], gkv_ref[...])[0].astype(BF16)
        q = jnp.dot(nq, wq_ref[...], preferred_element_type=F32)
        k = jnp.dot(nkv, wk_ref[...], preferred_element_type=F32)
        v = jnp.dot(nkv, wv_ref[...], preferred_element_type=F32)
        c, s1, s2 = c_ref[...], s1_ref[...], s2_ref[...]
        kr = _rope(pltpu.roll(kr_ref[...], 64, 1), c, s1, s2)
        for h in range(HEADS):
            sl = slice(h * 128, (h + 1) * 128)
            q_ref[:, sl] = (_rope(q[:, sl], c, s1, s2) * QK_SCALE).astype(BF16)
            k_ref[:, sl] = (k[:, sl] + kr).astype(BF16)
        v_ref[...] = v.astype(BF16)

    full = lambda shape: pl.BlockSpec(shape, lambda i: (0, 0))
    tab = pl.BlockSpec((tm, 128), lambda i: (i, 0))
    return pl.pallas_call(
        body, name="mla_prep", grid=(T // tm,),
        in_specs=[pl.BlockSpec((tm, 256), lambda i: (i, 0)), pl.BlockSpec((tm, 256), lambda i: (i, 1)),
                  pl.BlockSpec((tm, 128), lambda i: (i, 12)), full((1, 256)), full((1, 256)),
                  full((256, HW)), full((256, HW)), full((256, 512)), tab, tab, tab],
        out_specs=[pl.BlockSpec((tm, HW), lambda i: (i, 0)), pl.BlockSpec((tm, HW), lambda i: (i, 0)),
                   pl.BlockSpec((tm, 512), lambda i: (i, 0))],
        out_shape=[jax.ShapeDtypeStruct((T, HW), BF16), jax.ShapeDtypeStruct((T, HW), BF16),
                   jax.ShapeDtypeStruct((T, 512), BF16)],
        compiler_params=_params(40, 1),
    )(z0, z0, z0, gq, gkv, wq, wk, wv, rc, rs1, rs2)


def _flash_fwd(q, k, v, plan=None):
    T = q.shape[0]
    bq = min(2 * ROW_BLOCK, T)
    nq = T // bq
    grid = (4, nq, nq)

    def body(*refs):
        (q_ref, k_ref, v_ref), (o_ref, lse_ref), (m_sc, acc_sc), pctx = _split_refs(refs, 3, 2, 2, plan)
        _plan_start(plan, pctx, grid)
        i, j = pl.program_id(1), pl.program_id(2)
        first = lax.broadcasted_iota(jnp.int32, (bq, 128), 1) < 64

        @pl.when(j == 0)
        def _():
            m_sc[...] = jnp.full_like(m_sc, -jnp.inf)
            acc_sc[...] = jnp.zeros_like(acc_sc)

        def tile(r0, nr, nc, masked):
            rs = slice(r0, r0 + nr)
            vp = v_ref[0:nc, :]
            lanes = first[0:nc, :]
            for h in range(2):
                sl = slice(h * 128, (h + 1) * 128)
                s = lax.dot_general(q_ref[rs, sl], k_ref[0:nc, sl], NT_DIMS, preferred_element_type=F32)
                if masked:
                    rows = r0 + lax.broadcasted_iota(jnp.int32, (nr, nc), 0)
                    cols = lax.broadcasted_iota(jnp.int32, (nr, nc), 1)
                    s = jnp.where(cols <= rows, s, -jnp.inf)
                m_prev = m_sc[h, rs, 0:1]
                m_new = jnp.maximum(m_prev, jnp.max(s, axis=1, keepdims=True))
                alpha = jnp.exp(m_prev - m_new)
                p = jnp.exp(s - m_new).astype(BF16)
                vh = jnp.where(lanes if h == 0 else jnp.logical_not(lanes), vp, jnp.ones_like(vp))
                acc_sc[h, rs, :] = acc_sc[h, rs, :] * alpha + jnp.dot(p, vh, preferred_element_type=F32)
                m_sc[h, rs, :] = jnp.broadcast_to(m_new, (nr, 128))

        @pl.when(j < i)
        def _():
            tile(0, bq, bq, False)

        @pl.when(j == i)
        def _():
            tile(0, bq, bq, True)
            a0, a1 = acc_sc[0], acc_sc[1]
            l0, l1 = pltpu.roll(a0, 64, 1), pltpu.roll(a1, 64, 1)
            o_ref[...] = jnp.where(first, a0 / l0, a1 / l1).astype(BF16)
            lse_ref[...] = jnp.where(first, m_sc[0] + jnp.log(l0), m_sc[1] + jnp.log(l1))

        _plan_wait(plan, pctx, grid)

    kv = lambda hp, i, j: (jnp.minimum(i, j), hp)
    p_in, p_ospec, p_oshape, p_scr, p_alias = _plan_io(plan, 3, 2)
    return pl.pallas_call(
        body, name="flash_fwd", grid=grid,
        in_specs=[pl.BlockSpec((bq, 256), lambda hp, i, j: (i, hp)), pl.BlockSpec((bq, 256), kv),
                  pl.BlockSpec((bq, 128), kv)] + [_ANY] * len(p_in),
        out_specs=[pl.BlockSpec((bq, 128), lambda hp, i, j: (i, hp)),
                   pl.BlockSpec((bq, 128), lambda hp, i, j: (i, hp))] + p_ospec,
        out_shape=[jax.ShapeDtypeStruct((T, 512), BF16), jax.ShapeDtypeStruct((T, 512), F32)] + p_oshape,
        scratch_shapes=[pltpu.VMEM((2, bq, 128), F32), pltpu.VMEM((2, bq, 128), F32)] + p_scr,
        input_output_aliases=p_alias, compiler_params=_params(56, 3),
    )(*_hbm(q, k, v), *p_in)


def _attn_delta(dmix, o):
    T = o.shape[0]
    tm = min(ROW_BLOCK, T)
    blk = pl.BlockSpec((tm, 512), lambda i: (i, 0))

    def body(do_ref, o_ref, delta_ref, dob_ref):
        first = lax.broadcasted_iota(jnp.int32, (tm, 128), 1) < 64
        for hp in range(4):
            sl = slice(hp * 128, (hp + 1) * 128)
            prod = do_ref[:, sl] * o_ref[:, sl].astype(F32)
            d0 = jnp.sum(jnp.where(first, prod, 0.0), axis=1, keepdims=True)
            d1 = jnp.sum(jnp.where(first, 0.0, prod), axis=1, keepdims=True)
            delta_ref[:, sl] = jnp.where(first, d0, d1)
        dob_ref[...] = do_ref[...].astype(BF16)

    return pl.pallas_call(
        body, name="attn_delta", grid=(T // tm,), in_specs=[blk, blk], out_specs=[blk, blk],
        out_shape=[jax.ShapeDtypeStruct((T, 512), F32), jax.ShapeDtypeStruct((T, 512), BF16)],
        compiler_params=_params(32, 1),
    )(dmix, o)


def _flash_bwd(q, k, v, do_b, lse, delta, plan=None):
    T = q.shape[0]
    bq = min(2 * ROW_BLOCK, T)
    nq = T // bq
    grid = (4, nq, nq)

    def body(*refs):
        ((q_ref, k_ref, v_ref, do_ref, lse_ref, dl_ref), (dq_hbm, dk_ref, dv_ref), (dq_sc, dk_sc, dv_sc, sem),
         pctx) = _split_refs(refs, 6, 3, 4, plan)
        _plan_start(plan, pctx, grid)
        hp, j, i = pl.program_id(0), pl.program_id(1), pl.program_id(2)
        first = lax.broadcasted_iota(jnp.int32, (bq, 128), 1) < 64

        @pl.when((j == 0) & (i == 0))
        def _():
            dq_sc[...] = jnp.zeros_like(dq_sc)

        @pl.when(i == j)
        def _():
            dk_sc[...] = jnp.zeros_like(dk_sc)
            dv_sc[...] = jnp.zeros_like(dv_sc)

        def tile(r0, nr, nc, masked):
            rs, cs = slice(r0, r0 + nr), slice(0, nc)
            vp = v_ref[cs, :]
            do = do_ref[rs, :]
            lanes = first[rs, :]
            for h in range(2):
                sl = slice(h * 128, (h + 1) * 128)
                qh, kh = q_ref[rs, sl], k_ref[cs, sl]
                s = lax.dot_general(qh, kh, NT_DIMS, preferred_element_type=F32)
                p = jnp.exp(s - lse_ref[rs, h * 64:h * 64 + 1])
                if masked:
                    rows = r0 + lax.broadcasted_iota(jnp.int32, (nr, nc), 0)
                    cols = lax.broadcasted_iota(jnp.int32, (nr, nc), 1)
                    p = jnp.where(cols <= rows, p, 0.0)
                do_h = jnp.where(lanes if h == 0 else jnp.logical_not(lanes), do, jnp.zeros_like(do))
                dv_sc[cs, :] += lax.dot_general(p.astype(BF16), do_h, TN_DIMS, preferred_element_type=F32)
                dp = lax.dot_general(do_h, vp, NT_DIMS, preferred_element_type=F32)
                ds = (p * (dp - dl_ref[rs, h * 64:h * 64 + 1])).astype(BF16)
                dq_sc[i, rs, sl] += jnp.dot(ds, kh, preferred_element_type=F32)
                dk_sc[cs, sl] += lax.dot_general(ds, qh, TN_DIMS, preferred_element_type=F32)

        @pl.when(i > j)
        def _():
            tile(0, bq, bq, False)

        @pl.when(i == j)
        def _():
            tile(0, bq // 2, bq // 2, True)
            tile(bq // 2, bq // 2, bq, True)

        @pl.when(i == nq - 1)
        def _():
            dk_ref[...] = dk_sc[...]
            dv_ref[...] = dv_sc[...]

        @pl.when((j == nq - 1) & (i == nq - 1))
        def _():
            cp = pltpu.make_async_copy(dq_sc, dq_hbm.at[hp], sem)
            cp.start()
            cp.wait()

        _plan_wait(plan, pctx, grid)

    qi = lambda hp, j, i: (jnp.maximum(i, j), hp)
    kj = lambda hp, j, i: (j, hp)
    p_in, p_ospec, p_oshape, p_scr, p_alias = _plan_io(plan, 6, 3)
    return pl.pallas_call(
        body, name="flash_bwd", grid=grid,
        in_specs=[pl.BlockSpec((bq, 256), qi), pl.BlockSpec((bq, 256), kj), pl.BlockSpec((bq, 128), kj),
                  pl.BlockSpec((bq, 128), qi), pl.BlockSpec((bq, 128), qi), pl.BlockSpec((bq, 128), qi)]
        + [_ANY] * len(p_in),
        out_specs=[_ANY, pl.BlockSpec((bq, 256), kj), pl.BlockSpec((bq, 128), kj)] + p_ospec,
        out_shape=[jax.ShapeDtypeStruct((4, nq, bq, 256), F32), jax.ShapeDtypeStruct((T, 1024), F32),
                   jax.ShapeDtypeStruct((T, 512), F32)] + p_oshape,
        scratch_shapes=[pltpu.VMEM((nq, bq, 256), F32), pltpu.VMEM((bq, 256), F32), pltpu.VMEM((bq, 128), F32),
                        pltpu.SemaphoreType.DMA] + p_scr,
        input_output_aliases=p_alias, compiler_params=_params(56, 3),
    )(*_hbm(q, k, v, do_b, lse, delta), *p_in)


def _mla_bwd(z0, dq4, dk, dv, gq, gkv, wq, wk, wv, rc, rs1, rs2, plan=None):
    T = z0.shape[0]
    tm = min(ROW_BLOCK, T)
    HW = HEADS * 128
    grid = (T // tm,)
    dq4 = dq4.reshape(4, T, 256)

    def body(*refs):
        ((cq_ref, ckv_ref, dq_ref, dk_ref, dv_ref, gq_ref, gkv_ref, wq_ref, wk_ref, wv_ref, c_ref, s1_ref, s2_ref),
         (dc_ref, dkr_ref, dwq_ref, dwk_ref, dwv_ref, dgq_ref, dgkv_ref), _, pctx) = _split_refs(refs, 13, 7, 0, plan)
        _plan_start(plan, pctx, grid)

        @pl.when(pl.program_id(0) == 0)
        def _():
            for ref in (dwq_ref, dwk_ref, dwv_ref, dgq_ref, dgkv_ref):
                ref[...] = jnp.zeros_like(ref)

        c, s1, s2 = c_ref[...], s1_ref[...], s2_ref[...]
        lane = lax.broadcasted_iota(jnp.int32, (tm, 128), 1)
        nq, xq, rq = _rms(cq_ref[...], gq_ref[...])
        nkv, xkv, rkv = _rms(ckv_ref[...], gkv_ref[...])
        nq_b, nkv_b = nq.astype(BF16), nkv.astype(BF16)

        dq_parts, dk_parts = [], []
        dkr = jnp.zeros((tm, 128), F32)
        for h in range(HEADS):
            blk = dq_ref[h // 2, :, (h % 2) * 128:(h % 2 + 1) * 128] * QK_SCALE
            dq_parts.append(_rope_t(blk, c, s1, s2).astype(BF16))
            kb = dk_ref[:, h * 128:(h + 1) * 128]
            dk_parts.append(jnp.where(lane < NOPE, kb, 0.0).astype(BF16))
            dkr = dkr + kb
        dq_b = jnp.concatenate(dq_parts, axis=1)
        dk_b = jnp.concatenate(dk_parts, axis=1)
        dv_b = dv_ref[...].astype(BF16)

        dwq_ref[...] += lax.dot_general(nq_b, dq_b, TN_DIMS, preferred_element_type=F32)
        dwk_ref[...] += lax.dot_general(nkv_b, dk_b, TN_DIMS, preferred_element_type=F32)
        dwv_ref[...] += lax.dot_general(nkv_b, dv_b, TN_DIMS, preferred_element_type=F32)
        dnq = lax.dot_general(dq_b, wq_ref[...], NT_DIMS, preferred_element_type=F32)
        dnkv = (lax.dot_general(dk_b, wk_ref[...], NT_DIMS, preferred_element_type=F32)
                + lax.dot_general(dv_b, wv_ref[...], NT_DIMS, preferred_element_type=F32))

        def rms_bwd(dn, xhat, rstd, g):
            dxh = dn * g
            return rstd * (dxh - xhat * jnp.mean(dxh * xhat, -1, keepdims=True))

        dc_ref[:, :256] = rms_bwd(dnq, xq, rq, gq_ref[...]).astype(BF16)
        dc_ref[:, 256:] = rms_bwd(dnkv, xkv, rkv, gkv_ref[...]).astype(BF16)
        dgq_ref[...] += _fold8(dnq * xq)
        dgkv_ref[...] += _fold8(dnkv * xkv)
        dkr = pltpu.roll(_rope_t(dkr, c, s1, s2), 64, 1)
        dkr_ref[...] = jnp.where(lane < ROPE, dkr, 0.0).astype(BF16)
        _plan_wait(plan, pctx, grid)

    full = lambda shape: pl.BlockSpec(shape, lambda i: (0,) * len(shape))
    tab = pl.BlockSpec((tm, 128), lambda i: (i, 0))
    p_in, p_ospec, p_oshape, p_scr, p_alias = _plan_io(plan, 13, 7)
    return pl.pallas_call(
        body, name="mla_bwd", grid=grid,
        in_specs=[pl.BlockSpec((tm, 256), lambda i: (i, 0)), pl.BlockSpec((tm, 256), lambda i: (i, 1)),
                  pl.BlockSpec((4, tm, 256), lambda i: (0, i, 0)),
                  pl.BlockSpec((tm, HW), lambda i: (i, 0)), pl.BlockSpec((tm, 512), lambda i: (i, 0)),
                  full((1, 256)), full((1, 256)), full((256, HW)), full((256, HW)), full((256, 512)), tab, tab, tab]
        + [_ANY] * len(p_in),
        out_specs=[pl.BlockSpec((tm, 512), lambda i: (i, 0)), tab, full((256, HW)), full((256, HW)),
                   full((256, 512)), full((8, 256)), full((8, 256))] + p_ospec,
        out_shape=[jax.ShapeDtypeStruct((T, 512), BF16), jax.ShapeDtypeStruct((T, 128), BF16),
                   jax.ShapeDtypeStruct((256, HW), F32), jax.ShapeDtypeStruct((256, HW), F32),
                   jax.ShapeDtypeStruct((256, 512), F32), jax.ShapeDtypeStruct((8, 256), F32),
                   jax.ShapeDtypeStruct((8, 256), F32)] + p_oshape,
        scratch_shapes=p_scr, input_output_aliases=p_alias, compiler_params=_params(48, 1),
    )(*_hbm(z0, z0, dq4, dk, dv, gq, gkv, wq, wk, wv, rc, rs1, rs2), *p_in)


def _sgu_fwd(z0, a_out, ln_g, ln_b, w, b_t):
    T = z0.shape[0]
    tm = min(ROW_BLOCK, T)
    W = SGU_G * SGU_C

    def body(u_ref, v_ref, a_ref, g_ref, b_ref, w_ref, bt_ref, o_ref):
        o_ref[:, :W] = a_ref[...]
        ug = _gelu(u_ref[...])
        xhat, _ = _ln_stats(_gelu(v_ref[...]))
        vn = (xhat * g_ref[...] + b_ref[...]).astype(BF16)
        tril = lax.broadcasted_iota(jnp.int32, (SGU_C, SGU_C), 0) >= lax.broadcasted_iota(jnp.int32, (SGU_C, SGU_C), 1)
        for g in range(SGU_G):
            cs = slice(g * SGU_C, (g + 1) * SGU_C)
            wg = jnp.where(tril, w_ref[g], 0.0).astype(BF16)
            bcol = bt_ref[:, g:g + 1]
            for c in range(tm // SGU_C):
                rs = slice(c * SGU_C, (c + 1) * SGU_C)
                mixed = jnp.dot(wg, vn[rs, cs], preferred_element_type=F32) + bcol
                o_ref[rs, W + g * SGU_C:W + (g + 1) * SGU_C] = (ug[rs, cs] * mixed).astype(BF16)

    full = lambda shape: pl.BlockSpec(shape, lambda i: (0,) * len(shape))
    return pl.pallas_call(
        body, name="sgu_fwd", grid=(T // tm,),
        in_specs=[pl.BlockSpec((tm, W), lambda i: (i, 1)), pl.BlockSpec((tm, W), lambda i: (i, 2)),
                  pl.BlockSpec((tm, W), lambda i: (i, 0)),
                  full((1, W)), full((1, W)), full((SGU_G, SGU_C, SGU_C)), full((SGU_C, SGU_G))],
        out_specs=pl.BlockSpec((tm, 2 * W), lambda i: (i, 0)),
        out_shape=jax.ShapeDtypeStruct((T, 2 * W), BF16),
        compiler_params=_params(32, 1),
    )(z0, z0, a_out, ln_g, ln_b, w, b_t)


def _sgu_bwd(z0, dmix, dc, dkr, ln_g, ln_b, w, b_t):
    T = z0.shape[0]
    tm = min(ROW_BLOCK, T)
    W = SGU_G * SGU_C

    def body(u_ref, v_ref, do_ref, dc_ref, dkr_ref, g_ref, b_ref, w_ref, bt_ref, dz_ref, dw_ref, db_ref, dlg_ref,
             dlb_ref):
        @pl.when(pl.program_id(0) == 0)
        def _():
            for ref in (dw_ref, db_ref, dlg_ref, dlb_ref):
                ref[...] = jnp.zeros_like(ref)

        dz_ref[:, :W] = dc_ref[...]
        dz_ref[:, 3 * W:] = dkr_ref[...]

        u, v, dout = u_ref[...], v_ref[...], do_ref[...]
        ug = _gelu(u)
        xhat, rstd = _ln_stats(_gelu(v))
        vn = (xhat * g_ref[...] + b_ref[...]).astype(BF16)
        dmixed = dout * ug
        dmixed_b = dmixed.astype(BF16)
        tril = lax.broadcasted_iota(jnp.int32, (SGU_C, SGU_C), 0) >= lax.broadcasted_iota(jnp.int32, (SGU_C, SGU_C), 1)
        lane = lax.broadcasted_iota(jnp.int32, (SGU_C, SGU_C), 1)
        dvn_cols = []
        for g in range(SGU_G):
            cs = slice(g * SGU_C, (g + 1) * SGU_C)
            wg = jnp.where(tril, w_ref[g], 0.0).astype(BF16)
            bcol = bt_ref[:, g:g + 1]
            dw_g = jnp.zeros((SGU_C, SGU_C), F32)
            db_g = jnp.zeros((SGU_C, 1), F32)
            dvn_rows = []
            for c in range(tm // SGU_C):
                rs = slice(c * SGU_C, (c + 1) * SGU_C)
                mixed = jnp.dot(wg, vn[rs, cs], preferred_element_type=F32) + bcol
                dz_ref[rs, W + g * SGU_C:W + (g + 1) * SGU_C] = (dout[rs, cs] * mixed * _gelu_grad(u[rs, cs])).astype(BF16)
                dm = dmixed_b[rs, cs]
                dw_g = dw_g + lax.dot_general(dm, vn[rs, cs], NT_DIMS, preferred_element_type=F32)
                db_g = db_g + jnp.sum(dmixed[rs, cs], axis=1, keepdims=True)
                dvn_rows.append(lax.dot_general(wg, dm, TN_DIMS, preferred_element_type=F32))
            dw_ref[g] += jnp.where(tril, dw_g, 0.0)
            db_ref[...] += jnp.where(lane == g, db_g, 0.0)
            dvn_cols.append(jnp.concatenate(dvn_rows, axis=0))
        dvn = jnp.concatenate(dvn_cols, axis=1)
        dxh = dvn * g_ref[...]
        m1 = jnp.mean(dxh, -1, keepdims=True)
        m2 = jnp.mean(dxh * xhat, -1, keepdims=True)
        dvg = rstd * (dxh - m1 - xhat * m2)
        dz_ref[:, 2 * W:3 * W] = (dvg * _gelu_grad(v)).astype(BF16)
        dlg_ref[...] += _fold8(dvn * xhat)
        dlb_ref[...] += _fold8(dvn)

    full = lambda shape: pl.BlockSpec(shape, lambda i: (0,) * len(shape))
    return pl.pallas_call(
        body, name="sgu_bwd", grid=(T // tm,),
        in_specs=[pl.BlockSpec((tm, W), lambda i: (i, 1)), pl.BlockSpec((tm, W), lambda i: (i, 2)),
                  pl.BlockSpec((tm, W), lambda i: (i, 1)), pl.BlockSpec((tm, W), lambda i: (i, 0)),
                  pl.BlockSpec((tm, 128), lambda i: (i, 0)),
                  full((1, W)), full((1, W)), full((SGU_G, SGU_C, SGU_C)), full((SGU_C, SGU_G))],
        out_specs=[pl.BlockSpec((tm, 3 * W + 128), lambda i: (i, 0)), full((SGU_G, SGU_C, SGU_C)),
                   full((SGU_C, SGU_C)), full((8, W)), full((8, W))],
        out_shape=[jax.ShapeDtypeStruct((T, 3 * W + 128), BF16), jax.ShapeDtypeStruct((SGU_G, SGU_C, SGU_C), F32),
                   jax.ShapeDtypeStruct((SGU_C, SGU_C), F32), jax.ShapeDtypeStruct((8, W), F32),
                   jax.ShapeDtypeStruct((8, W), F32)],
        compiler_params=_params(40, 1),
    )(z0, z0, dmix, dc, dkr, ln_g, ln_b, w, b_t)


def _hg_lower_bound(lb_ref):
    a0, a1 = lb_ref[0:1, :], lb_ref[1:2, :]
    m = jnp.maximum(a0, a1)
    e0, e1 = jnp.exp(a0 - m), jnp.exp(a1 - m)
    return e1 / (e0 + e1)


def _running_sum(x, reverse=False):
    n = x.shape[0]
    row = lax.broadcasted_iota(jnp.int32, x.shape, 0)
    s = 1
    while s < n:
        if reverse:
            x = x + jnp.where(row < n - s, pltpu.roll(x, n - s, 0), 0.0)
        else:
            x = x + jnp.where(row >= s, pltpu.roll(x, s, 0), 0.0)
        s *= 2
    return x


def _hg_chunk(qc, fc, lb):
    C = HG_CHUNK
    rows = lax.broadcasted_iota(jnp.int32, (C, C), 0)
    cols = lax.broadcasted_iota(jnp.int32, (C, C), 1)
    rowid = lax.broadcasted_iota(jnp.int32, (C, 128), 0)
    sq, sg = _sigmoid(qc), _sigmoid(fc)
    qf = qc * sq
    gate = lb + (1.0 - lb) * sg
    kk = 1.0 - gate
    lg = jnp.log(gate)
    bcum = _running_sum(lg)
    b_mid = jnp.sum(jnp.where(rowid < C // 2, lg, 0.0), axis=0, keepdims=True)
    b_last = jnp.sum(lg, axis=0, keepdims=True)
    eq, ek, e, eh = jnp.exp(bcum - b_mid), jnp.exp(b_mid - bcum), jnp.exp(bcum), jnp.exp(b_last - bcum)
    qt, kt, qe, khat = qf * eq, kk * ek, qf * e, kk * eh
    a = lax.dot_general(qt.astype(BF16), kt.astype(BF16), NT_DIMS, preferred_element_type=F32)
    a = jnp.where(rows >= cols, a, 0.0)
    return dict(sq=sq, sg=sg, gate=gate, kk=kk, eq=eq, ek=ek, e=e, eh=eh, qt=qt, kt=kt, qe=qe, khat=khat, a=a,
                e_last=jnp.exp(b_last), tril=rows >= cols, rowid=rowid)


def _hgrn_fwd(z4, hg_lb, gnorm):
    T = z4.shape[1]
    tb = min(ROW_BLOCK, T)
    C = HG_CHUNK
    ncb = tb // C
    HPB = HG_HEADS_PER_STEP

    def body(q_ref, f_ref, i_ref, g_ref, lb_ref, gn_ref, y_ref, o_ref, st_ref, st_sc):
        @pl.when(pl.program_id(1) == 0)
        def _():
            st_sc[...] = jnp.zeros_like(st_sc)

        def chunk(c, carry):
            rs = pl.ds(pl.multiple_of(c * C, C), C)
            for hh in range(HPB):
                hs = slice(hh * 128, (hh + 1) * 128)
                lb = _hg_lower_bound(lb_ref.at[:, hs])
                v_b = i_ref[rs, hs].astype(BF16)
                gc = g_ref[rs, hs]
                x = _hg_chunk(q_ref[rs, hs], f_ref[rs, hs], lb)
                st = st_sc[hh]
                st_ref[hh, c] = st
                o = (jnp.dot(x["a"].astype(BF16), v_b, preferred_element_type=F32)
                     + lax.dot_general(x["qe"].astype(BF16), st.astype(BF16), NT_DIMS, preferred_element_type=F32))
                st_sc[hh] = st * x["e_last"] + lax.dot_general(v_b, x["khat"].astype(BF16), TN_DIMS,
                                                               preferred_element_type=F32)
                o_ref[rs, hs] = o
                n = o * lax.rsqrt(jnp.mean(o * o, -1, keepdims=True) + EPS)
                y_ref[rs, hs] = (n * gn_ref[:, hs] * (gc * _sigmoid(gc))).astype(BF16)
            return carry

        lax.fori_loop(0, ncb, chunk, 0)

    W = 128 * HPB
    zb = lambda k: pl.BlockSpec((None, tb, W), lambda h, t: (k, t, h))
    out = pl.BlockSpec((tb, W), lambda h, t: (t, h))
    return pl.pallas_call(
        body, name="hgrn_fwd", grid=(HEADS // HPB, T // tb),
        in_specs=[zb(0), zb(1), zb(2), zb(3), pl.BlockSpec((2, W), lambda h, t: (0, h)),
                  pl.BlockSpec((1, W), lambda h, t: (0, h))],
        out_specs=[out, out, pl.BlockSpec((HPB, ncb, 128, 128), lambda h, t: (h, t, 0, 0))],
        out_shape=[jax.ShapeDtypeStruct((T, D), BF16), jax.ShapeDtypeStruct((T, D), F32),
                   jax.ShapeDtypeStruct((HEADS, T // C, 128, 128), F32)],
        scratch_shapes=[pltpu.VMEM((HPB, 128, 128), F32)],
        compiler_params=_params(48, 2),
    )(*_hbm(z4, z4, z4, z4, hg_lb, gnorm))


def _hgrn_bwd(z4, o_raw, dy, states, hg_lb, gnorm):
    T = z4.shape[1]
    tb = min(ROW_BLOCK, T)
    C = HG_CHUNK
    ncb = tb // C
    nt = T // tb
    HPB = HG_HEADS_PER_STEP

    def body(q_ref, f_ref, i_ref, g_ref, o_ref, dy_ref, st_ref, lb_ref, gn_ref, dz_ref, dlb_ref, dgn_ref, dst_sc):
        @pl.when(pl.program_id(1) == 0)
        def _():
            dst_sc[...] = jnp.zeros_like(dst_sc)
            dlb_ref[...] = jnp.zeros_like(dlb_ref)
            dgn_ref[...] = jnp.zeros_like(dgn_ref)

        def chunk(cc, carry):
            for hh in range(HPB):
                one_head(ncb - 1 - cc, hh, slice(hh * 128, (hh + 1) * 128))
            return carry

        def one_head(c, hh, hs):
            rs = pl.ds(pl.multiple_of(c * C, C), C)
            lb = _hg_lower_bound(lb_ref.at[:, hs])
            gn = gn_ref[:, hs]
            qc, gc = q_ref[rs, hs], g_ref[rs, hs]
            v_b = i_ref[rs, hs].astype(BF16)
            x = _hg_chunk(qc, f_ref[rs, hs], lb)
            st, dst = st_ref[hh, c], dst_sc[hh]
            st_b, dst_b = st.astype(BF16), dst.astype(BF16)
            o, dyc = o_ref[rs, hs], dy_ref[rs, hs]
            sgg = _sigmoid(gc)
            sil = gc * sgg
            rstd = lax.rsqrt(jnp.mean(o * o, -1, keepdims=True) + EPS)
            n = o * rstd
            dgn_ref[:, hs] += _fold8(dyc * n * sil)
            dn = dyc * gn * sil
            do = rstd * (dn - n * jnp.mean(dn * n, -1, keepdims=True))
            dg = dyc * n * gn * (sgg * (1.0 + gc * (1.0 - sgg)))
            do_b = do.astype(BF16)
            da = jnp.where(x["tril"], lax.dot_general(do_b, v_b, NT_DIMS, preferred_element_type=F32), 0.0).astype(BF16)
            qt_b, kt_b, qe_b, khat_b = (x[n_].astype(BF16) for n_ in ("qt", "kt", "qe", "khat"))
            dv = (lax.dot_general(x["a"].astype(BF16), do_b, TN_DIMS, preferred_element_type=F32)
                  + lax.dot_general(khat_b, dst_b, NT_DIMS, preferred_element_type=F32))
            dqt = jnp.dot(da, kt_b, preferred_element_type=F32)
            dqe = jnp.dot(do_b, st_b, preferred_element_type=F32)
            dkt = lax.dot_general(da, qt_b, TN_DIMS, preferred_element_type=F32)
            dkhat = jnp.dot(v_b, dst_b, preferred_element_type=F32)
            dst_sc[hh] = lax.dot_general(do_b, qe_b, TN_DIMS, preferred_element_type=F32) + dst * x["e_last"]
            de_last = jnp.sum(st * dst, axis=0, keepdims=True)
            dqf = dqt * x["eq"] + dqe * x["e"]
            dkk = dkt * x["ek"] + dkhat * x["eh"]
            dkh_kh = dkhat * x["khat"]
            db = dqt * qt_b.astype(F32) - dkt * kt_b.astype(F32) + dqe * x["qe"] - dkh_kh
            db_last = jnp.sum(dkh_kh, axis=0, keepdims=True) + de_last * x["e_last"]
            db = db + jnp.where(x["rowid"] == C - 1, db_last, 0.0)
            dlg = _running_sum(db, reverse=True)
            dgate = dlg / x["gate"] - dkk
            sg, sq = x["sg"], x["sq"]
            dlb_ref[:, hs] += _fold8(dgate * (1.0 - sg)) * (lb * (1.0 - lb))
            dz_ref[0, rs, hs] = (dqf * (sq * (1.0 + qc * (1.0 - sq)))).astype(BF16)
            dz_ref[1, rs, hs] = (dgate * (1.0 - lb) * sg * (1.0 - sg)).astype(BF16)
            dz_ref[2, rs, hs] = dv.astype(BF16)
            dz_ref[3, rs, hs] = dg.astype(BF16)

        lax.fori_loop(0, ncb, chunk, 0)

    W = 128 * HPB
    zb = lambda k: pl.BlockSpec((None, tb, W), lambda h, t: (k, nt - 1 - t, h))
    blk = pl.BlockSpec((tb, W), lambda h, t: (nt - 1 - t, h))
    acc = pl.BlockSpec((8, W), lambda h, t: (0, h))
    return pl.pallas_call(
        body, name="hgrn_bwd", grid=(HEADS // HPB, nt),
        in_specs=[zb(0), zb(1), zb(2), zb(3), blk, blk,
                  pl.BlockSpec((HPB, ncb, 128, 128), lambda h, t: (h, nt - 1 - t, 0, 0)),
                  pl.BlockSpec((2, W), lambda h, t: (0, h)), pl.BlockSpec((1, W), lambda h, t: (0, h))],
        out_specs=[pl.BlockSpec((4, tb, W), lambda h, t: (0, nt - 1 - t, h)), acc, acc],
        out_shape=[jax.ShapeDtypeStruct((4, T, D), BF16), jax.ShapeDtypeStruct((8, D), F32),
                   jax.ShapeDtypeStruct((8, D), F32)],
        scratch_shapes=[pltpu.VMEM((HPB, 128, 128), F32)],
        compiler_params=_params(48, 2),
    )(*_hbm(z4, z4, z4, z4, o_raw, dy, states, hg_lb, gnorm))


def _adamw(w, g, m, v, *, name):
    R, L = w.shape
    tr = R if R <= 512 else 512
    assert R % tr == 0
    blk = pl.BlockSpec((tr, L), lambda i: (i, 0))
    c1, c2 = 1.0 - B1 ** STEP, 1.0 - B2 ** STEP

    def body(w_ref, g_ref, m_ref, v_ref, d_ref, mo_ref, vo_ref):
        g_ = g_ref[...]
        m_ = B1 * m_ref[...] + (1.0 - B1) * g_
        v_ = B2 * v_ref[...] + (1.0 - B2) * (g_ * g_)
        d_ref[...] = -LR * ((m_ / c1) / (jnp.sqrt(v_ / c2) + ADAM_EPS) + WD * w_ref[...])
        mo_ref[...] = m_
        vo_ref[...] = v_

    sds = jax.ShapeDtypeStruct((R, L), F32)
    return pl.pallas_call(
        body, name=name, grid=(R // tr,), in_specs=[blk] * 4, out_specs=[blk] * 3, out_shape=[sds] * 3,
        compiler_params=_params(32, 1),
    )(w, g, m, v)


def _adamw_rows(w, m, v, gbufs, row0, *, name, plan=None):
    L, R, C = w.shape
    tr = 256
    assert R % tr == 0 and row0 % tr == 0 and len(gbufs) == L
    grid = (L, R // tr)
    blk = pl.BlockSpec((None, tr, C), lambda l, i: (l, i, 0))
    gblks = [pl.BlockSpec((tr, C), lambda l, i, k=k: (row0 // tr + jnp.where(l == k, i, 0), 0)) for k in range(L)]
    c1, c2 = 1.0 - B1 ** STEP, 1.0 - B2 ** STEP

    def body(*refs):
        ins, (go_ref, d_ref, mo_ref, vo_ref), _, pctx = _split_refs(refs, 3 + L, 4, 0, plan)
        w_ref, m_ref, v_ref = ins[:3]
        g_refs = ins[3:]
        _plan_start(plan, pctx, grid)
        g_ = g_refs[0][...]
        for l in range(1, L):
            g_ = jnp.where(pl.program_id(0) == l, g_refs[l][...], g_)
        m_ = B1 * m_ref[...] + (1.0 - B1) * g_
        v_ = B2 * v_ref[...] + (1.0 - B2) * (g_ * g_)
        go_ref[...] = g_
        d_ref[...] = -LR * ((m_ / c1) / (jnp.sqrt(v_ / c2) + ADAM_EPS) + WD * w_ref[...])
        mo_ref[...] = m_
        vo_ref[...] = v_
        _plan_wait(plan, pctx, grid)

    sds = jax.ShapeDtypeStruct((L, R, C), F32)
    p_in, p_ospec, p_oshape, p_scr, p_alias = _plan_io(plan, 3 + L, 4)
    return pl.pallas_call(
        body, name=name, grid=grid, in_specs=[blk] * 3 + gblks + [_ANY] * len(p_in),
        out_specs=[blk] * 4 + p_ospec, out_shape=[sds] * 4 + p_oshape, scratch_shapes=p_scr,
        input_output_aliases=p_alias, compiler_params=_params(32, 2),
    )(*_hbm(w, m, v, *gbufs), *p_in)


def _add_pairs(g, theirs, ids, *, name):
    n, R, L = theirs.shape
    tr = math.gcd(R, 128)
    nb = R // tr

    def body(ids_ref, a_ref, b_ref, o_ref):
        o_ref[...] = (a_ref[...].astype(F32) + b_ref[...].astype(F32)).astype(BF16)

    blk = pl.BlockSpec((n, tr, L), lambda i, ids: (0, i, 0))
    return pl.pallas_call(
        body, name=name, out_shape=jax.ShapeDtypeStruct((n, R, L), BF16),
        grid_spec=pltpu.PrefetchScalarGridSpec(
            num_scalar_prefetch=1, grid=(nb,),
            in_specs=[pl.BlockSpec((n, tr, L), lambda i, ids: (0, ids[1] * nb + i, 0)), blk], out_specs=blk),
        compiler_params=_params(16, 1),
    )(ids, g, theirs)


def _sum_chips(pair, parts, ids, *, name):
    _, R, L = parts.shape
    tr = math.gcd(R, 128)

    def body(ids_ref, o_ref, r_ref, out_ref):
        out_ref[...] = ((o_ref[...].astype(F32) + r_ref[0].astype(F32)) + r_ref[1].astype(F32)) + r_ref[2].astype(F32)

    return pl.pallas_call(
        body, name=name, out_shape=jax.ShapeDtypeStruct((2, R, L), F32),
        grid_spec=pltpu.PrefetchScalarGridSpec(
            num_scalar_prefetch=1, grid=(R // tr,),
            in_specs=[pl.BlockSpec((None, tr, L), lambda i, ids: (ids[0], i, 0)),
                      pl.BlockSpec((3, tr, L), lambda i, ids: (0, i, 0))],
            out_specs=pl.BlockSpec((None, tr, L), lambda i, ids: (ids[1], i, 0))),
        compiler_params=_params(32, 1),
    )(ids, pair, parts)


def _mesh_ids():
    x, y, c = _mesh_pos()
    return jnp.stack([2 * x + y, c]).astype(jnp.int32)


def _place_shard(rows, ids, *, name):
    R, L = rows.shape
    tr = 128

    def body(ids_ref, in_ref, out_ref):
        out_ref[...] = in_ref[...].astype(BF16)

    return pl.pallas_call(
        body, name=name, out_shape=jax.ShapeDtypeStruct((4, R, L), BF16),
        grid_spec=pltpu.PrefetchScalarGridSpec(
            num_scalar_prefetch=1, grid=(R // tr,), in_specs=[pl.BlockSpec((tr, L), lambda i, ids: (i, 0))],
            out_specs=pl.BlockSpec((None, tr, L), lambda i, ids: (ids[0], i, 0))),
        compiler_params=_params(16, 1),
    )(ids, rows)


def _remote(src, dst, send_sem, recv_sem, to):
    return pltpu.make_async_remote_copy(src_ref=src, dst_ref=dst, send_sem=send_sem, recv_sem=recv_sem,
                                        device_id=to, device_id_type=MESH_IDS)


def _rows(ref, lead, start, size):
    return ref.at[tuple(pl.ds(0, n) for n in ref.shape[:lead]) + (pl.ds(start, size),)]


def _other_chips():
    x, y, _ = _mesh_pos()
    return [(1 - x, y), (x, 1 - y), (1 - x, 1 - y)]


def _plan_gather_ici(bufs):
    n = len(bufs)

    def copies(outs, send, recv):
        x, y, c = _mesh_pos()
        res = []
        for b in range(n):
            half = bufs[b].shape[1] // 2
            mine = _rows(outs[b].at[2 * x + y], 0, c * half, half)
            for j, (cx, cy) in enumerate(_other_chips()):
                res.append((_remote(mine, mine, send(3 * b + j), recv(3 * b + j), (cx, cy, c)),
                            _remote(mine, _rows(outs[b].at[2 * cx + cy], 0, c * half, half),
                                    send(3 * b + j), recv(3 * b + j), (x, y, c))))
        return res

    def start(ins, outs, send, recv, loc):
        for out_cp, _ in copies(outs, send, recv):
            out_cp.start()

    def wait(ins, outs, send, recv, loc):
        for out_cp, in_cp in copies(outs, send, recv):
            in_cp.wait_recv()
            out_cp.wait_send()

    outs = [jax.ShapeDtypeStruct(b.shape, b.dtype) for b in bufs]
    return _Plan(bufs, outs, 3 * n, 0, start, wait, aliases={b: b for b in range(n)})


def _plan_gather_forward(bufs):
    n = len(bufs)

    def copies(outs, send, recv):
        x, y, c = _mesh_pos()
        res = []
        for b in range(n):
            half = bufs[b].shape[1] // 2
            for j, (cx, cy) in enumerate(_other_chips()):
                slot = outs[b].at[2 * cx + cy]
                res.append((_remote(_rows(slot, 0, c * half, half), _rows(slot, 0, c * half, half),
                                    send(3 * b + j), recv(3 * b + j), (x, y, 1 - c)),
                            _remote(_rows(slot, 0, c * half, half), _rows(slot, 0, (1 - c) * half, half),
                                    send(3 * b + j), recv(3 * b + j), (x, y, c))))
        return res

    def start(ins, outs, send, recv, loc):
        for out_cp, _ in copies(outs, send, recv):
            out_cp.start()

    def wait(ins, outs, send, recv, loc):
        for out_cp, in_cp in copies(outs, send, recv):
            in_cp.wait_recv()
            out_cp.wait_send()

    outs = [jax.ShapeDtypeStruct(b.shape, b.dtype) for b in bufs]
    return _Plan(bufs, outs, 3 * n, 0, start, wait, aliases={b: b for b in range(n)})


def _plan_pair_swap(g):
    half = g.shape[1] // 2

    def copy(ins, outs, send, recv, loc):
        x, y, c = _mesh_pos()
        return _remote(_rows(ins[0], 1, (1 - c) * half, half), outs[0], send(0), recv(0), (x, y, 1 - c))

    return _Plan([g], [jax.ShapeDtypeStruct((4, half, g.shape[2]), g.dtype)], 1, 0,
                 lambda *a: copy(*a).start(), lambda *a: copy(*a).wait())


def _plan_pair_gather(buf):
    def copies(ins, outs, send, recv, loc):
        x, y, c = _mesh_pos()
        return (_remote(outs[0].at[c], outs[0].at[c], send(0), recv(0), (x, y, 1 - c)),
                _remote(outs[0].at[c], outs[0].at[1 - c], send(0), recv(0), (x, y, c)))

    def wait(*a):
        out_cp, in_cp = copies(*a)
        in_cp.wait_recv()
        out_cp.wait_send()

    return _Plan([buf], [jax.ShapeDtypeStruct(buf.shape, buf.dtype)], 1, 0, lambda *a: copies(*a)[0].start(), wait,
                 aliases={0: 0})


def _plan_chip_scatter(p):
    def copies(ins, outs, send, recv, loc):
        _, _, c = _mesh_pos()
        return [_remote(ins[0].at[2 * cx + cy], outs[0].at[j], send(j), recv(j), (cx, cy, c))
                for j, (cx, cy) in enumerate(_other_chips())]

    def start(*a):
        for cp in copies(*a):
            cp.start()

    def wait(*a):
        for cp in copies(*a):
            cp.wait()

    return _Plan([p], [jax.ShapeDtypeStruct((3,) + p.shape[1:], p.dtype)], 3, 0, start, wait)


def _plan_exchange_all(vec):
    def copies(ins, outs, send, recv, loc):
        x, y, c = _mesh_pos()
        return [_remote(ins[0], outs[0].at[r - 1], send(r - 1), recv(r - 1), (x ^ (r >> 2), y ^ ((r >> 1) & 1), c ^ (r & 1)))
                for r in range(1, 8)]

    def start(*a):
        for cp in copies(*a):
            cp.start()

    def wait(*a):
        for cp in copies(*a):
            cp.wait()

    return _Plan([vec], [jax.ShapeDtypeStruct((7,) + vec.shape, vec.dtype)], 7, 0, start, wait)


SMALL_LAYOUT = {
    "mla_gq": (0, 1, 256, (1, 256)), "mla_gkv": (1, 1, 256, (1, 256)), "sgu_ln_g": (2, 1, 512, (1, 512)),
    "sgu_ln_b": (3, 1, 512, (1, 512)), "sgu_w": (4, 64, 1024, (64, 1024)), "sgu_b": (68, 1, 512, (1, 512)),
    "hg_lb": (69, 2, 1024, (2, 1024)), "hg_gnorm": (71, 1, 1024, (1, 256)), "ln1_g": (72, 2, 1024, (2, 1024)),
    "ln1_b": (74, 2, 1024, (2, 1024)), "ln2_g": (76, 2, 1024, (2, 1024)), "ln2_b": (78, 2, 1024, (2, 1024)),
}


def _small_pack(dgq, dgkv, dslg, dslb, dsw, dsb, dlb, dgn, ln_parts, sq_err):
    flat_ln = [p for pair in ln_parts for p in pair]

    def body(*refs):
        gq_ref, gkv_ref, slg_ref, slb_ref, sw_ref, sb_ref, lb_ref, gn_ref = refs[:8]
        ln_refs, err_ref, out_ref, t_sc = refs[8:16], refs[16], refs[17], refs[18]
        s8 = lambda ref: jnp.sum(ref[...], axis=0, keepdims=True)
        out_ref[...] = jnp.zeros_like(out_ref)
        out_ref[0:1, 0:256] = s8(gq_ref)
        out_ref[1:2, 0:256] = s8(gkv_ref)
        out_ref[2:3, 0:512] = s8(slg_ref)
        out_ref[3:4, 0:512] = s8(slb_ref)
        out_ref[4:68, :] = sw_ref[...]
        t_sc[...] = sb_ref[...].T
        for g in range(SGU_G):
            out_ref[68:69, g * SGU_C:(g + 1) * SGU_C] = t_sc[g:g + 1, :]
        d_lb1 = s8(lb_ref)
        out_ref[69:70, :] = -d_lb1
        out_ref[70:71, :] = d_lb1
        out_ref[71:72, :] = s8(gn_ref)
        for k, ref in enumerate(ln_refs):
            out_ref[72 + k:73 + k, :] = s8(ref)
        out_ref[0:1, 1023:1024] = jnp.sum(s8(err_ref), axis=1, keepdims=True) * (0.5 / D)

    vm = pl.BlockSpec(memory_space=pltpu.VMEM)
    return pl.pallas_call(
        body, name="small_grad_pack", in_specs=[vm] * 17, out_specs=vm,
        out_shape=jax.ShapeDtypeStruct((SMALL_ROWS, 1024), F32), scratch_shapes=[pltpu.VMEM((SGU_C, SGU_C), F32)],
        compiler_params=_params(16),
    )(dgq, dgkv, dslg, dslb, dsw.reshape(64, 1024), dsb, dlb, dgn, *flat_ln, sq_err)


def _small_update(vec, others, ids, w, m, v):
    names = list(SMALL_LAYOUT)
    n = len(names)
    c1, c2 = 1.0 - B1 ** STEP, 1.0 - B2 ** STEP
    have_others = others is not None

    def body(*refs):
        ids_ref, v_ref = refs[0], refs[1]
        k = 2 + have_others
        w_refs, m_refs, v_refs = refs[k:k + n], refs[k + n:k + 2 * n], refs[k + 2 * n:k + 3 * n]
        outs = refs[k + 3 * n:]
        row0_ref, tot_sc = outs[0], outs[-1]
        total = v_ref[...]
        if have_others:
            me = 2 * ids_ref[0] + ids_ref[1]
            total = None
            for d in range(8):
                rel = d ^ me
                term = jnp.where(rel == 0, v_ref[...], refs[2][jnp.maximum(rel - 1, 0)])
                total = term if total is None else total + term
        tot_sc[...] = total
        row0_ref[...] = tot_sc[0:1, :]
        for i, name in enumerate(names):
            r0, nr, width, _ = SMALL_LAYOUT[name]
            if name == "hg_gnorm":
                g_ = tot_sc[r0:r0 + 1, 0:256]
                for chip in range(1, 4):
                    g_ = jnp.where(ids_ref[0] == chip, tot_sc[r0:r0 + 1, chip * 256:(chip + 1) * 256], g_)
            else:
                g_ = tot_sc[r0:r0 + nr, 0:width]
            m_ = B1 * m_refs[i][...] + (1.0 - B1) * g_
            v_ = B2 * v_refs[i][...] + (1.0 - B2) * (g_ * g_)
            go, do, mo, vo = outs[1 + 4 * i:5 + 4 * i]
            go[...] = g_
            do[...] = -LR * ((m_ / c1) / (jnp.sqrt(v_ / c2) + ADAM_EPS) + WD * w_refs[i][...])
            mo[...] = m_
            vo[...] = v_

    full = lambda shape: pl.BlockSpec(shape, lambda i, ids, nd=len(shape): (0,) * nd)
    kshapes = [SMALL_LAYOUT[name][3] for name in names]
    operands = [vec] + ([others] if have_others else []) + [d[name] for d in (w, m, v) for name in names]
    out_shapes = [jax.ShapeDtypeStruct((1, 1024), F32)] + [jax.ShapeDtypeStruct(s, F32) for s in kshapes for _ in range(4)]
    res = pl.pallas_call(
        body, name="small_update", out_shape=out_shapes,
        grid_spec=pltpu.PrefetchScalarGridSpec(
            num_scalar_prefetch=1, grid=(1,), in_specs=[full(o.shape) for o in operands],
            out_specs=[full(s.shape) for s in out_shapes],
            scratch_shapes=[pltpu.VMEM((SMALL_ROWS, 1024), F32)]),
        compiler_params=_params(32, 1),
    )(ids, *operands)
    return res[0], {name: tuple(res[1 + 4 * i:5 + 4 * i]) for i, name in enumerate(names)}


ROWS_L1, ROWS_L0, ROWS_ODD, ROWS_ODD_W = 3328, 2048, 768, 384
ODD_PARTS = (("w_out_e", (256, 1024)), ("w_in_e", (1024, 392)), ("w_qb", (256, 192)), ("w_kvb", (256, 256)))
ODD_W_PARTS = tuple(p for p in ODD_PARTS if p[0] != "w_in_e")


def _odd_rows(parts, dtype, layout, total, gnorm=None):
    rows = [parts[n].reshape(-1, 1024).astype(dtype) for n, _ in layout]
    used = sum(r.shape[0] for r in rows)
    if gnorm is not None:
        bits = lax.bitcast_convert_type(gnorm.reshape(-1), BF16).reshape(1, 512)
        rows.append(jnp.pad(bits, ((0, 0), (0, 512))))
        used += 1
    rows.append(jnp.zeros((total - used, 1024), dtype))
    return jnp.concatenate(rows, axis=0)


def _odd_unrows(buf, layout, with_gnorm=False):
    out, off = {}, 0
    for n, shape in layout:
        nr = math.prod(shape) // 1024
        out[n] = buf[off:off + nr].reshape(shape)
        off += nr
    if with_gnorm:
        out["hg_gnorm"] = lax.bitcast_convert_type(buf[off, :512].reshape(256, 2), F32).reshape(1, 256)
    return out


def _rope_tables(positions):
    half = ROPE // 2
    inv_freq = ROPE_BASE ** (-jnp.arange(half, dtype=F32) / half)
    ang = positions.astype(F32).reshape(-1, 1) * inv_freq
    cos, sin = jnp.cos(ang), jnp.sin(ang)
    T = ang.shape[0]
    one, z16, z32 = jnp.ones((T, NOPE), F32), jnp.zeros((T, half), F32), jnp.zeros((T, 32), F32)
    z64 = jnp.zeros((T, NOPE), F32)
    c = jnp.concatenate([one, cos, cos, z32], axis=1)
    s1 = jnp.concatenate([z64, -sin, z16, z32], axis=1)
    s2 = jnp.concatenate([z64, z16, sin, z32], axis=1)
    return c, s1, s2


def _local_step(x, positions, tgt, odd, bufs, P, exchange):
    T = x.shape[0]
    row = lambda a: a.reshape(1, -1)
    rc, rs1, rs2 = _rope_tables(positions)
    blk = lambda f: pl.BlockSpec((None, D, D), f)

    w_in_e = odd["w_in_e"]
    w_in = jnp.concatenate([w_in_e[:, :512], w_in_e[:, 544:1568], w_in_e[:, 512:544], jnp.zeros((D, 96), BF16)], axis=1)
    wq = jnp.pad(odd["w_qb"].reshape(256, HEADS, NOPE + ROPE), ((0, 0), (0, 0), (0, 32))).reshape(256, HEADS * 128)
    kvb = odd["w_kvb"].reshape(256, HEADS, NOPE + VDIM)
    wk = jnp.pad(kvb[:, :, :NOPE], ((0, 0), (0, 0), (0, 64))).reshape(256, HEADS * 128)
    wv = kvb[:, :, NOPE:].reshape(256, HEADS * VDIM)
    w_out_e = odd["w_out_e"]
    sgu_w = P["sgu_w"][0]
    sgu_bt = P["sgu_b"][0].T
    gq, gkv = P["mla_gq"], P["mla_gkv"]
    gnorm = P["hg_gnorm"]

    z0 = _matmul(x, w_in, name="in_proj_e", M=T, N=1664, K=D, tn=1664)[0]
    q, k, v = _mla_prep(z0, gq, gkv, wq, wk, wv, rc, rs1, rs2)
    if exchange:
        ids = _mesh_ids()
        placed = [_place_shard(b, ids, name=f"place_shard_{l}") for l, b in enumerate(bufs)]
        a_out, lse, wga, wgb = _flash_fwd(q, k, v, plan=_plan_gather_ici(placed[:2]))
    else:
        a_out, lse = _flash_fwd(q, k, v)
        wga, wgb, wgc = bufs
    mix0 = _sgu_fwd(z0, a_out, P["sgu_ln_g"], P["sgu_ln_b"], sgu_w, sgu_bt)
    res = _proj_ln(mix0, w_out_e, x, row(P["ln1_g"][0]), row(P["ln1_b"][0]), name="out_proj_ln_e",
                   plan=_plan_gather_forward([wga, wgb]) if exchange else None)
    r1, h1, h1b = res[:3]
    if exchange:
        wga, wgb = res[3:]
    res = _ffn_ln(h1b, wga, h1, row(P["ln2_g"][0]), row(P["ln2_b"][0]), name="ffn_ln_0",
                  plan=_plan_gather_ici(placed[2:]) if exchange else None)
    ra0, r2, h2, h2b = res[:4]
    z4 = _matmul(h2b, wgb, name="in_proj_o", M=T, N=4 * D, K=D, b_spec=blk(lambda i, j, k: (j, 0, 0)),
                 out_shape=jax.ShapeDtypeStruct((4, T, D), F32),
                 o_spec=pl.BlockSpec((None, min(MM_ROWS, T), D), lambda i, j, k: (j, i, 0)))[0]
    y1, o_raw, states = _hgrn_fwd(z4, P["hg_lb"], gnorm)
    res2 = _proj_ln(y1, wgb, h2, row(P["ln1_g"][1]), row(P["ln1_b"][1]), name="out_proj_ln_o", w_rowblk=4,
                    plan=_plan_gather_forward([res[4]]) if exchange else None)
    r3, h3, h3b = res2[:3]
    if exchange:
        wgc = res2[3]
    ra1, r4, h4, _ = _ffn_ln(h3b, wgc, h3, row(P["ln2_g"][1]), row(P["ln2_b"][1]), name="ffn_ln_1")

    ln1_g, ln1_b, ln2_g, ln2_b = [None, None], [None, None], [None, None], [None, None]
    sq_err_parts = []

    def ffn_bwd(l, dh, r_out, ra, h_mid_b, g2, wg, rows, plan=None, tgt=None):
        dr, dr_b, dg, db, *sq_err = _ln_bwd(dh, r_out, row(g2), name=f"ln2_bwd_{l}", tgt=tgt)
        sq_err_parts.extend(sq_err)
        ln2_g[l], ln2_b[l] = dg, db
        da, *extra = _matmul(dr_b, wg, tb=True, mul=ra, out_dtype=BF16, name=f"ffn_da_{l}", M=T, N=4 * D, K=D,
                             b_spec=blk(lambda i, j, k: (j, 1, 0)), plan=plan)
        gbuf = _matmul(ra, dr_b, ta=True, a_sq=True, name=f"ffn_dw2_{l}", M=4 * D, N=D, K=T, tm=1024, tk=DW_TOKENS,
                       out_shape=jax.ShapeDtypeStruct((4, rows, D), BF16), o_spec=blk(lambda i, j, k: (i, 1, 0)))[0]
        gbuf = _matmul(h_mid_b, da, ta=True, name=f"ffn_dw1_{l}", M=D, N=4 * D, K=T, tm=1024, tk=DW_TOKENS, into=gbuf,
                       out_shape=jax.ShapeDtypeStruct((4, rows, D), BF16), o_spec=blk(lambda i, j, k: (j, 0, 0)))[0]
        dh_mid = _matmul(da, wg, tb=True, add=dr, add_scale=ALPHA, name=f"ffn_dh_{l}", M=T, N=D, K=4 * D, tk=2 * D,
                         b_spec=pl.BlockSpec((2, D, D), lambda i, j, k: (k, 0, 0)))[0]
        return dh_mid, gbuf, extra

    dh3, g1, _ = ffn_bwd(1, h4, r4, ra1, h3b, P["ln2_g"][1], wgc, ROWS_L1, tgt=tgt)
    loss_parts = sq_err_parts[0]
    dr3, dr3_b, dg, db = _ln_bwd(dh3, r3, row(P["ln1_g"][1]), name="ln1_bwd_1")
    ln1_g[1], ln1_b[1] = dg, db
    g1_sds = jax.ShapeDtypeStruct((4, ROWS_L1, D), BF16)
    g1 = _matmul(y1, dr3_b, ta=True, name="dw_out_o", M=D, N=D, K=T, tm=256, tk=DW_TOKENS, into=g1, out_shape=g1_sds,
                 o_spec=pl.BlockSpec((None, 256, D), lambda i, j, k: (i, 12, 0)))[0]
    dmix1 = _matmul(dr3_b, wgb, tb=True, name="dmix_o", M=T, N=D, K=D, b_spec=_rows4_spec(4, 3), b_merge=(D, D))[0]
    dz4, dlb, dgn = _hgrn_bwd(z4, o_raw, dmix1, states, P["hg_lb"], gnorm)
    g1 = _matmul(h2b, dz4, ta=True, name="dw_in_o", M=D, N=4 * D, K=T, tm=1024, tk=DW_TOKENS, into=g1, out_shape=g1_sds,
                 b_spec=pl.BlockSpec((None, min(DW_TOKENS, T), D), lambda i, j, k: (j, k, 0)),
                 o_spec=blk(lambda i, j, k: (j, 2, 0)))[0]
    dh2 = _matmul(dz4, wgb, tb=True, add=dr3, add_scale=ALPHA, name="dh_in_o", M=T, N=D, K=4 * D, tk=2 * D,
                  a_spec=pl.BlockSpec((2, min(MM_ROWS, T), D), lambda i, j, k: (k, i, 0)),
                  b_spec=pl.BlockSpec((2, D, D), lambda i, j, k: (k, 0, 0)))[0]

    dh1, g0, swapped1 = ffn_bwd(0, dh2, r2, ra0, h1b, P["ln2_g"][0], wga, ROWS_L0,
                                plan=_plan_pair_swap(g1) if exchange else None)
    dr1, dr1_b, dg, db = _ln_bwd(dh1, r1, row(P["ln1_g"][0]), name="ln1_bwd_0")
    ln1_g[0], ln1_b[0] = dg, db
    godd = {"w_out_e": _matmul(mix0, dr1_b, ta=True, name="dw_out_e", M=D, N=D, K=T, tm=1024, tk=DW_TOKENS)[0]}
    dmix0, *swapped0 = _matmul(dr1_b, w_out_e, tb=True, name="dmix_e", M=T, N=D, K=D,
                               plan=_plan_pair_swap(g0) if exchange else None)
    delta, do_b = _attn_delta(dmix0, a_out)
    if exchange:
        pair1 = _add_pairs(g1, swapped1[0], ids, name="grad_pair_add_1")
        pair0 = _add_pairs(g0, swapped0[0], ids, name="grad_pair_add_0")
        dq4, dk, dv, parts0, parts1 = _flash_bwd(
            q, k, v, do_b, lse, delta, plan=_join_plans([_plan_chip_scatter(pair0), _plan_chip_scatter(pair1)]))
        half0 = _sum_chips(pair0, parts0, ids, name="grad_chip_sum_0")
        half1 = _sum_chips(pair1, parts1, ids, name="grad_chip_sum_1")
        dc, dkr, dwq, dwk, dwv, dgq, dgkv, g0, g1 = _mla_bwd(
            z0, dq4, dk, dv, gq, gkv, wq, wk, wv, rc, rs1, rs2,
            plan=_join_plans([_plan_pair_gather(half0), _plan_pair_gather(half1)]))
        g0, g1 = g0.reshape(ROWS_L0, D), g1.reshape(ROWS_L1, D)
    else:
        dq4, dk, dv = _flash_bwd(q, k, v, do_b, lse, delta)
        dc, dkr, dwq, dwk, dwv, dgq, dgkv = _mla_bwd(z0, dq4, dk, dv, gq, gkv, wq, wk, wv, rc, rs1, rs2)
    dz0, dsw, dsb, dslg, dslb = _sgu_bwd(z0, dmix0, dc, dkr, P["sgu_ln_g"], P["sgu_ln_b"], sgu_w, sgu_bt)
    small_vec = _small_pack(dgq, dgkv, dslg, dslb, dsw, dsb, dlb, dgn, [ln1_g, ln1_b, ln2_g, ln2_b], loss_parts)
    dw_in, *small_others = _matmul(x, dz0, ta=True, name="dw_in_e", M=D, N=1664, K=T, tm=1024, tn=1664,
                                   tk=DW_TOKENS // 2, plan=_plan_exchange_all(small_vec) if exchange else None)
    godd["w_in_e"] = jnp.concatenate([dw_in[:, :512], dw_in[:, 1536:1568], dw_in[:, 512:1536]], axis=1)
    godd["w_qb"] = dwq.reshape(256, HEADS, 128)[:, :, :NOPE + ROPE].reshape(256, HEADS * (NOPE + ROPE))
    godd["w_kvb"] = jnp.concatenate([dwk.reshape(256, HEADS, 128)[:, :, :NOPE], dwv.reshape(256, HEADS, VDIM)],
                                    axis=2).reshape(256, HEADS * (NOPE + VDIM))
    odd_plan = None
    if exchange:
        by_chip = [_odd_rows({"w_out_e": jnp.split(godd["w_out_e"], 4, axis=0)[j],
                              **{n: jnp.split(godd[n], 4, axis=1)[j] for n in ("w_qb", "w_kvb")}}, BF16,
                             ODD_W_PARTS, ROWS_ODD_W)
                   for j in range(4)]
        bufs_odd = [godd["w_in_e"].reshape(D, 4, 392).transpose(1, 0, 2).astype(BF16), jnp.stack(by_chip)]
        theirs = _run_plan(_join_plans([_plan_pair_swap(b) for b in bufs_odd]), name="odd_pair_swap")
        odd_pairs = [_add_pairs(b, t, ids, name=f"odd_pair_add_{k}") for k, (b, t) in enumerate(zip(bufs_odd, theirs))]
        odd_plan = _join_plans([_plan_chip_scatter(p) for p in odd_pairs])
    grad_x, *odd_parts = _matmul(dz0, w_in, tb=True, add=dr1, add_scale=ALPHA, name="dx", M=T, N=D, K=1664, tk=1664,
                                 plan=odd_plan)
    if exchange:
        godd = (odd_pairs, odd_parts)
    return grad_x, g0, g1, godd, small_vec, (small_others[0] if exchange else None)


WEIGHTS = ['w_in_e', 'mla_gq', 'mla_gkv', 'w_qb', 'w_kvb', 'sgu_ln_g', 'sgu_ln_b', 'sgu_w', 'sgu_b', 'w_out_e',
           'w_in_o', 'hg_lb', 'hg_gnorm', 'w_out_o', 'ln1_g', 'ln1_b', 'w_ff1', 'w_ff2', 'ln2_g', 'ln2_b']


def kernel(x, positions, w_in_e, mla_gq, mla_gkv, w_qb, w_kvb, sgu_ln_g, sgu_ln_b, sgu_w, sgu_b, w_out_e, w_in_o, hg_lb, hg_gnorm, w_out_o, ln1_g, ln1_b, w_ff1, w_ff2, ln2_g, ln2_b, loss_target, m_w_in_e, m_mla_gq, m_mla_gkv, m_w_qb, m_w_kvb, m_sgu_ln_g, m_sgu_ln_b, m_sgu_w, m_sgu_b, m_w_out_e, m_w_in_o, m_hg_lb, m_hg_gnorm, m_w_out_o, m_ln1_g, m_ln1_b, m_w_ff1, m_w_ff2, m_ln2_g, m_ln2_b, v_w_in_e, v_mla_gq, v_mla_gkv, v_w_qb, v_w_kvb, v_sgu_ln_g, v_sgu_ln_b, v_sgu_w, v_sgu_b, v_w_out_e, v_w_in_o, v_hg_lb, v_hg_gnorm, v_w_out_o, v_ln1_g, v_ln1_b, v_w_ff1, v_w_ff2, v_ln2_g, v_ln2_b):
    args = dict(locals())
    w = {n: args[n] for n in WEIGHTS}
    m = {n: args["m_" + n] for n in WEIGHTS}
    v = {n: args["v_" + n] for n in WEIGHTS}
    cx, cy, cc = _mesh_pos()
    chip = 2 * cx + cy

    odd_shard = _odd_rows({"w_out_e": w_out_e[0], "w_qb": w_qb[0], "w_kvb": w_kvb[0]}, BF16, ODD_W_PARTS, ROWS_ODD_W,
                          gnorm=hg_gnorm)
    ids = _mesh_ids()
    placed = [_place_shard(w_in_e[0], ids, name="place_shard_in_e"), _place_shard(odd_shard, ids, name="place_shard_odd")]
    gathered = _run_plan(_plan_gather_forward(_run_plan(_plan_gather_ici(placed), name="odd_gather")),
                         name="odd_gather_forward")
    per_chip = [_odd_unrows(gathered[1][j], ODD_W_PARTS, with_gnorm=True) for j in range(4)]
    odd = {"w_out_e": jnp.concatenate([p["w_out_e"] for p in per_chip], axis=0),
           "w_in_e": jnp.concatenate([gathered[0][j] for j in range(4)], axis=1)}
    for n in ("w_qb", "w_kvb"):
        odd[n] = jnp.concatenate([p[n] for p in per_chip], axis=1)
    small = {n: w[n] for n in SMALL_LAYOUT if n != "hg_gnorm"}
    small["hg_gnorm"] = jnp.concatenate([p["hg_gnorm"] for p in per_chip], axis=1)
    shard_rows = (jnp.concatenate([w_ff1[0], w_ff2[0]], axis=0).astype(BF16),
                  jnp.concatenate([w_in_o[0], w_out_o[0]], axis=0).astype(BF16),
                  jnp.concatenate([w_ff1[1], w_ff2[1]], axis=0).astype(BF16))

    grad_x, g_l0, g_l1, godd, small_vec, small_others = _local_step(
        x[0], positions[0], loss_target[0], odd, shard_rows, small, True)

    sums = [_sum_chips(pair, parts, ids, name=f"odd_chip_sum_{k}") for k, (pair, parts) in enumerate(zip(*godd))]
    g_in_e, g_rest = _run_plan(_join_plans([_plan_pair_gather(s) for s in sums]), name="odd_pair_gather")
    g_odd = _odd_unrows(g_rest.reshape(ROWS_ODD_W, 1024), ODD_W_PARTS)
    g_odd["w_in_e"] = g_in_e.reshape(D, 392)

    to_kernel = lambda d: {n: d[n].reshape(SMALL_LAYOUT[n][3]) for n in SMALL_LAYOUT}
    first_row, small_out = _small_update(small_vec, small_others, ids, to_kernel(w), to_kernel(m), to_kernel(v))
    loss = first_row[0, 1023]
    grads, delta, new_m, new_v = {}, {}, {}, {}
    for n, res in small_out.items():
        grads[n], delta[n], new_m[n], new_v[n] = (r.reshape(w[n].shape) for r in res)

    for n, bufs_, row0 in (("w_ff1", [g_l0, g_l1], 0), ("w_ff2", [g_l0, g_l1], 1024), ("w_in_o", [g_l1], 2048),
                           ("w_out_o", [g_l1], 3072)):
        grads[n], delta[n], new_m[n], new_v[n] = _adamw_rows(w[n], m[n], v[n], bufs_, row0, name=f"adamw_{n}")
    for n, _ in ODD_PARTS:
        grads[n] = g_odd[n][None]
        d_, m_, v_ = _adamw(w[n][0], g_odd[n], m[n][0], v[n][0], name=f"adamw_{n}")
        delta[n], new_m[n], new_v[n] = d_[None], m_[None], v_[None]

    return (loss, grad_x[None], *[grads[n] for n in WEIGHTS], *[delta[n] for n in WEIGHTS],
            *[new_m[n] for n in WEIGHTS], *[new_v[n] for n in WEIGHTS])
```

```python
import math

import jax
import jax.numpy as jnp
from jax import lax
from jax.experimental import pallas as pl
from jax.experimental.pallas import tpu as pltpu

F32 = jnp.float32
BF16 = jnp.bfloat16
MESH_IDS = pl.DeviceIdType.MESH

D = 1024
DEPTH = 2
HEADS = 8
NOPE, ROPE, VDIM = 64, 32, 64
QK_SCALE = (NOPE + ROPE) ** -0.5
ROPE_BASE = 10000.0
SGU_G, SGU_C = 4, 128
HG_CHUNK = 64
HG_HEADS_PER_STEP = 8
ALPHA = (2 * DEPTH) ** 0.25
EPS = 1e-5
LR, B1, B2, ADAM_EPS, WD, STEP = 0.001, 0.9, 0.999, 1e-08, 0.01, 10
GELU_C = math.sqrt(2.0 / math.pi)
GELU_A = 0.044715
MB = 1024 * 1024
ROW_BLOCK = 512
SMALL_ROWS = 80

NT_DIMS = (((1,), (1,)), ((), ()))
TN_DIMS = (((0,), (0,)), ((), ()))


def _params(vmem_mb, n_axes=0):
    kw = dict(vmem_limit_bytes=vmem_mb * MB)
    if n_axes:
        kw["dimension_semantics"] = ("arbitrary",) * n_axes
    return pltpu.CompilerParams(**kw)


_ANY = pl.BlockSpec(memory_space=pltpu.HBM)


def _mesh_pos():
    return lax.axis_index("x"), lax.axis_index("y"), lax.axis_index("c")


def _hbm(*arrays):
    return tuple(pltpu.with_memory_space_constraint(a, pltpu.HBM) if a.size >= 2 ** 18 else a for a in arrays)


class _Plan:
    def __init__(self, ins, outs, n_remote, n_local, start, wait, aliases=None):
        self.ins, self.outs, self.n_remote, self.n_local = list(ins), list(outs), n_remote, n_local
        self.start, self.wait, self.aliases = start, wait, dict(aliases or {})


def _join_plans(plans):
    ins, outs, aliases, parts = [], [], {}, []
    nr = nl = 0
    for p in plans:
        parts.append((p, len(ins), len(outs), nr, nl))
        aliases.update({len(ins) + i: len(outs) + o for i, o in p.aliases.items()})
        ins += p.ins
        outs += p.outs
        nr += p.n_remote
        nl += p.n_local

    def run(which):
        def go(in_refs, out_refs, send, recv, loc):
            for p, i0, o0, r0, l0 in parts:
                getattr(p, which)(in_refs[i0:i0 + len(p.ins)], out_refs[o0:o0 + len(p.outs)],
                                  lambda i, r0=r0: send(r0 + i), lambda i, r0=r0: recv(r0 + i),
                                  lambda i, l0=l0: loc(l0 + i))
        return go

    return _Plan(ins, outs, nr, nl, run("start"), run("wait"), aliases)


def _plan_io(plan, n_in, n_out):
    if plan is None:
        return [], [], [], [], {}
    sems = [pltpu.SemaphoreType.DMA((max(plan.n_remote, 1),)), pltpu.SemaphoreType.DMA((max(plan.n_remote, 1),)),
            pltpu.SemaphoreType.DMA((max(plan.n_local, 1),))]
    aliases = {n_in + i: n_out + o for i, o in plan.aliases.items()}
    return plan.ins, [_ANY] * len(plan.outs), plan.outs, sems, aliases


def _split_refs(refs, n_in, n_out, n_scr, plan):
    p_in, p_out = (len(plan.ins), len(plan.outs)) if plan is not None else (0, 0)
    refs = list(refs)
    ins, refs = refs[:n_in], refs[n_in:]
    pins, refs = refs[:p_in], refs[p_in:]
    outs, refs = refs[:n_out], refs[n_out:]
    pouts, refs = refs[:p_out], refs[p_out:]
    scr, psem = refs[:n_scr], refs[n_scr:]
    psem = tuple((lambda i, s=s: s.at[i]) for s in psem)
    return ins, outs, scr, (pins, pouts, psem)


def _grid_edge(grid, last):
    cond = None
    for ax, n in enumerate(grid):
        c = pl.program_id(ax) == (n - 1 if last else 0)
        cond = c if cond is None else cond & c
    return cond


def _plan_start(plan, pctx, grid):
    if plan is not None:
        pins, pouts, psem = pctx
        pl.when(_grid_edge(grid, False))(lambda: plan.start(pins, pouts, *psem))


def _plan_wait(plan, pctx, grid):
    if plan is not None:
        pins, pouts, psem = pctx
        pl.when(_grid_edge(grid, True))(lambda: plan.wait(pins, pouts, *psem))


def _run_plan(plan, *, name):
    def body(*refs):
        _, _, _, (pins, pouts, psem) = _split_refs(refs, 0, 0, 0, plan)
        plan.start(pins, pouts, *psem)
        plan.wait(pins, pouts, *psem)

    p_in, p_ospec, p_oshape, p_scr, p_alias = _plan_io(plan, 0, 0)
    return pl.pallas_call(body, name=name, in_specs=[_ANY] * len(p_in), out_specs=p_ospec, out_shape=p_oshape,
                          scratch_shapes=p_scr, input_output_aliases=p_alias)(*p_in)


def _fold8(x):
    return x.reshape(x.shape[0] // 8, 8, x.shape[1]).sum(axis=0)


def _ln_stats(r):
    mu = jnp.mean(r, -1, keepdims=True)
    xc = r - mu
    rstd = lax.rsqrt(jnp.mean(xc * xc, -1, keepdims=True) + EPS)
    return xc * rstd, rstd


def _sigmoid(x):
    return jax.nn.sigmoid(x)


def _gelu(x):
    return 0.5 * x * (1.0 + jnp.tanh(GELU_C * (x + GELU_A * x * x * x)))


def _gelu_grad(x):
    t = jnp.tanh(GELU_C * (x + GELU_A * x * x * x))
    return 0.5 * (1.0 + t) + 0.5 * x * (1.0 - t * t) * GELU_C * (1.0 + 3.0 * GELU_A * x * x)


MM_ROWS = 1024
DW_TOKENS = 4096


def _matmul(a, b, *, name, M, N, K, ta=False, tb=False, out_dtype=F32, tm=MM_ROWS, tn=1024, tk=1024,
            a_spec=None, b_spec=None, b_merge=None, out_shape=None, o_spec=None, into=None,
            a_sq=False, mul=None, add=None, add_scale=1.0, plan=None):
    tm, tn, tk = min(tm, M), min(tn, N), min(tk, K)
    assert M % tm == 0 and N % tn == 0 and K % tk == 0
    grid = (M // tm, N // tn, K // tk)
    nk = grid[2]
    if a_spec is None:
        a_spec = pl.BlockSpec((tk, tm), lambda i, j, k: (k, i)) if ta else pl.BlockSpec((tm, tk), lambda i, j, k: (i, k))
    if b_spec is None:
        b_spec = pl.BlockSpec((tn, tk), lambda i, j, k: (j, k)) if tb else pl.BlockSpec((tk, tn), lambda i, j, k: (k, j))
    if o_spec is None:
        o_spec = pl.BlockSpec((tm, tn), lambda i, j, k: (i, j))
        out_shape = jax.ShapeDtypeStruct((M, N), out_dtype)
    e_spec = pl.BlockSpec((tm, tn), lambda i, j, k: (i, j))
    dims = (((0 if ta else 1,), (1 if tb else 0,)), ((), ()))
    extra = [e for e in (mul, add, into) if e is not None]
    n_in = 2 + len(extra)

    def body(*refs):
        ins, outs, scr, pctx = _split_refs(refs, n_in, 1, 1 if nk > 1 else 0, plan)
        a_ref, b_ref = ins[0], ins[1]
        rest = list(ins[2:])
        mul_ref = rest.pop(0) if mul is not None else None
        add_ref = rest.pop(0) if add is not None else None
        o_ref = outs[0]
        _plan_start(plan, pctx, grid)
        av = a_ref[...].astype(BF16)
        if a_sq:
            av = av * av
        bv = b_ref[...]
        if b_merge is not None:
            bv = bv.reshape(b_merge)
        if bv.ndim == 3:
            w = av.shape[-1] // (1 if av.ndim == 3 else bv.shape[0])
            a_parts = [av[s] if av.ndim == 3 else av[:, s * w:(s + 1) * w] for s in range(bv.shape[0])]
            p = sum(lax.dot_general(a_parts[s], bv[s], dims, preferred_element_type=F32) for s in range(bv.shape[0]))
        else:
            p = lax.dot_general(av, bv, dims, preferred_element_type=F32)

        def finish(r):
            if mul_ref is not None:
                r = r * (2.0 * mul_ref[...].astype(F32))
            if add_ref is not None:
                r = r + add_scale * add_ref[...]
            o_ref[...] = r.astype(o_ref.dtype)

        if nk == 1:
            finish(p)
        else:
            acc_ref = scr[0]
            k = pl.program_id(2)

            @pl.when(k == 0)
            def _():
                acc_ref[...] = p

            @pl.when(k > 0)
            def _():
                acc_ref[...] += p

            @pl.when(k == nk - 1)
            def _():
                finish(acc_ref[...])

        _plan_wait(plan, pctx, grid)

    p_in, p_ospec, p_oshape, p_scr, p_alias = _plan_io(plan, n_in, 1)
    aliases = dict(p_alias)
    if into is not None:
        aliases[n_in - 1] = 0
    return pl.pallas_call(
        body, name=name, grid=grid,
        in_specs=[a_spec, b_spec] + [e_spec] * (len(extra) - (into is not None)) + [_ANY] * (into is not None)
        + [_ANY] * len(p_in),
        out_specs=[o_spec] + p_ospec, out_shape=[out_shape] + p_oshape,
        scratch_shapes=([pltpu.VMEM((tm, tn), F32)] if nk > 1 else []) + p_scr,
        input_output_aliases=aliases, compiler_params=_params(48, 3),
    )(*_hbm(a, b, *extra), *p_in)


def _rows4_spec(rowblk, n_axes):
    return pl.BlockSpec((4, 256, D), lambda *_: (0, rowblk, 0))


def _proj_ln(a_b, w, h_prev, g, b, *, name, w_rowblk=None, plan=None):
    T = a_b.shape[0]
    tm = min(ROW_BLOCK, T)
    grid = (T // tm,)
    row = pl.BlockSpec((tm, D), lambda i: (i, 0))
    vec = pl.BlockSpec((1, D), lambda i: (0, 0))
    w_spec = pl.BlockSpec((D, D), lambda i: (0, 0)) if w_rowblk is None else _rows4_spec(w_rowblk, 1)

    def body(*refs):
        (a_ref, w_ref, h_ref, g_ref, b_ref), (r_ref, ho_ref, hb_ref), _, pctx = _split_refs(refs, 5, 3, 0, plan)
        _plan_start(plan, pctx, grid)
        mix = jnp.dot(a_ref[...], w_ref[...].reshape(D, D), preferred_element_type=F32)
        r = ALPHA * h_ref[...] + mix
        xhat, _ = _ln_stats(r)
        y = xhat * g_ref[...] + b_ref[...]
        r_ref[...] = r
        ho_ref[...] = y
        hb_ref[...] = y.astype(BF16)
        _plan_wait(plan, pctx, grid)

    p_in, p_ospec, p_oshape, p_scr, p_alias = _plan_io(plan, 5, 3)
    return pl.pallas_call(
        body, name=name, grid=grid,
        in_specs=[row, w_spec, row, vec, vec] + [_ANY] * len(p_in),
        out_specs=[row, row, row] + p_ospec,
        out_shape=[jax.ShapeDtypeStruct((T, D), F32), jax.ShapeDtypeStruct((T, D), F32),
                   jax.ShapeDtypeStruct((T, D), BF16)] + p_oshape,
        scratch_shapes=p_scr, input_output_aliases=p_alias, compiler_params=_params(40, 1),
    )(*_hbm(a_b, w, h_prev, g, b), *p_in)


def _ffn_ln(h_b, wbuf, h, g, b, *, name, plan=None):
    T = h_b.shape[0]
    slots = 2
    tm, tf = min(ROW_BLOCK, T), slots * 1024
    nf = 4 // slots
    F = nf * tf
    grid = (T // tm, nf)
    row = pl.BlockSpec((tm, D), lambda i, j: (i, 0))
    vec = pl.BlockSpec((1, D), lambda i, j: (0, 0))

    def body(*refs):
        ((hb_ref, w1_ref, w2_ref, h_ref, g_ref, b_ref), (ra_ref, r_ref, ho_ref, hbo_ref), (acc_ref,),
         pctx) = _split_refs(refs, 6, 4, 1, plan)
        _plan_start(plan, pctx, grid)
        j = pl.program_id(1)
        hb = hb_ref[...]
        p = None
        for s in range(slots):
            ra = jnp.maximum(jnp.dot(hb, w1_ref[s], preferred_element_type=F32), 0.0)
            ra_ref[:, s * 1024:(s + 1) * 1024] = ra.astype(BF16)
            ps = jnp.dot((ra * ra).astype(BF16), w2_ref[s], preferred_element_type=F32)
            p = ps if p is None else p + ps

        @pl.when(j == 0)
        def _():
            acc_ref[...] = p

        @pl.when(j > 0)
        def _():
            acc_ref[...] += p

        @pl.when(j == nf - 1)
        def _():
            r = ALPHA * h_ref[...] + acc_ref[...]
            xhat, _ = _ln_stats(r)
            y = xhat * g_ref[...] + b_ref[...]
            r_ref[...] = r
            ho_ref[...] = y
            hbo_ref[...] = y.astype(BF16)

        _plan_wait(plan, pctx, grid)

    p_in, p_ospec, p_oshape, p_scr, p_alias = _plan_io(plan, 6, 4)
    return pl.pallas_call(
        body, name=name, grid=grid,
        in_specs=[row, pl.BlockSpec((slots, D, D), lambda i, j: (j, 0, 0)),
                  pl.BlockSpec((slots, D, D), lambda i, j: (j, 1, 0)), row, vec, vec] + [_ANY] * len(p_in),
        out_specs=[pl.BlockSpec((tm, tf), lambda i, j: (i, j)), row, row, row] + p_ospec,
        out_shape=[jax.ShapeDtypeStruct((T, F), BF16), jax.ShapeDtypeStruct((T, D), F32),
                   jax.ShapeDtypeStruct((T, D), F32), jax.ShapeDtypeStruct((T, D), BF16)] + p_oshape,
        scratch_shapes=[pltpu.VMEM((tm, D), F32)] + p_scr,
        input_output_aliases=p_alias, compiler_params=_params(56, 2),
    )(*_hbm(h_b, wbuf, wbuf, h, g, b), *p_in)


def _ln_bwd(dy, r, g, *, name, tgt=None):
    T = dy.shape[0]
    tm = min(ROW_BLOCK, T)
    row = pl.BlockSpec((tm, D), lambda i: (i, 0))
    acc = pl.BlockSpec((8, D), lambda i: (0, 0))
    n_in = 3 + (tgt is not None)

    def body(*refs):
        dy_ref, r_ref, g_ref = refs[:3]
        dr_ref, drb_ref, dg_ref, db_ref = refs[n_in:n_in + 4]

        @pl.when(pl.program_id(0) == 0)
        def _():
            for ref in refs[n_in + 2:]:
                ref[...] = jnp.zeros_like(ref)

        dy_ = dy_ref[...]
        if tgt is not None:
            err = dy_ - refs[3][...]
            refs[n_in + 4][...] += _fold8(err * err)
            dy_ = err * (1.0 / D)
        xhat, rstd = _ln_stats(r_ref[...])
        dxh = dy_ * g_ref[...]
        m1 = jnp.mean(dxh, -1, keepdims=True)
        m2 = jnp.mean(dxh * xhat, -1, keepdims=True)
        dr = rstd * (dxh - m1 - xhat * m2)
        dr_ref[...] = dr
        drb_ref[...] = dr.astype(BF16)
        dg_ref[...] += _fold8(dy_ * xhat)
        db_ref[...] += _fold8(dy_)

    extra = [] if tgt is None else [tgt]
    return pl.pallas_call(
        body, name=name, grid=(T // tm,),
        in_specs=[row, row, pl.BlockSpec((1, D), lambda i: (0, 0))] + [row] * len(extra),
        out_specs=[row, row, acc, acc] + [acc] * len(extra),
        out_shape=[jax.ShapeDtypeStruct((T, D), F32), jax.ShapeDtypeStruct((T, D), BF16)]
        + [jax.ShapeDtypeStruct((8, D), F32)] * (2 + len(extra)),
        compiler_params=_params(40, 1),
    )(*_hbm(dy, r, g, *extra))


def _rope(x, c, s1, s2):
    return x * c + pltpu.roll(x, 112, 1) * s1 + pltpu.roll(x, 16, 1) * s2


def _rope_t(dy, c, s1, s2):
    return dy * c + pltpu.roll(dy * s1, 16, 1) + pltpu.roll(dy * s2, 112, 1)


def _rms(x, g):
    rstd = lax.rsqrt(jnp.mean(x * x, -1, keepdims=True) + EPS)
    xhat = x * rstd
    return xhat * g, xhat, rstd


def _mla_prep(z0, gq, gkv, wq, wk, wv, rc, rs1, rs2):
    T = z0.shape[0]
    tm = min(ROW_BLOCK, T)
    HW = HEADS * 128

    def body(cq_ref, ckv_ref, kr_ref, gq_ref, gkv_ref, wq_ref, wk_ref, wv_ref, c_ref, s1_ref, s2_ref,
             q_ref, k_ref, v_ref):
        nq = _rms(cq_ref[...], gq_ref[...])[0].astype(BF16)
        nkv = _rms(ckv_ref[...], gkv_ref[...])[0].astype(BF16)
        q = jnp.dot(nq, wq_ref[...], preferred_element_type=F32)
        k = jnp.dot(nkv, wk_ref[...], preferred_element_type=F32)
        v = jnp.dot(nkv, wv_ref[...], preferred_element_type=F32)
        c, s1, s2 = c_ref[...], s1_ref[...], s2_ref[...]
        kr = _rope(pltpu.roll(kr_ref[...], 64, 1), c, s1, s2)
        for h in range(HEADS):
            sl = slice(h * 128, (h + 1) * 128)
            q_ref[:, sl] = (_rope(q[:, sl], c, s1, s2) * QK_SCALE).astype(BF16)
            k_ref[:, sl] = (k[:, sl] + kr).astype(BF16)
        v_ref[...] = v.astype(BF16)

    full = lambda shape: pl.BlockSpec(shape, lambda i: (0, 0))
    tab = pl.BlockSpec((tm, 128), lambda i: (i, 0))
    return pl.pallas_call(
        body, name="mla_prep", grid=(T // tm,),
        in_specs=[pl.BlockSpec((tm, 256), lambda i: (i, 0)), pl.BlockSpec((tm, 256), lambda i: (i, 1)),
                  pl.BlockSpec((tm, 128), lambda i: (i, 12)), full((1, 256)), full((1, 256)),
                  full((256, HW)), full((256, HW)), full((256, 512)), tab, tab, tab],
        out_specs=[pl.BlockSpec((tm, HW), lambda i: (i, 0)), pl.BlockSpec((tm, HW), lambda i: (i, 0)),
                   pl.BlockSpec((tm, 512), lambda i: (i, 0))],
        out_shape=[jax.ShapeDtypeStruct((T, HW), BF16), jax.ShapeDtypeStruct((T, HW), BF16),
                   jax.ShapeDtypeStruct((T, 512), BF16)],
        compiler_params=_params(40, 1),
    )(z0, z0, z0, gq, gkv, wq, wk, wv, rc, rs1, rs2)


def _flash_fwd(q, k, v, plan=None):
    T = q.shape[0]
    bq = min(2 * ROW_BLOCK, T)
    nq = T // bq
    grid = (4, nq, nq)

    def body(*refs):
        (q_ref, k_ref, v_ref), (o_ref, lse_ref), (m_sc, acc_sc), pctx = _split_refs(refs, 3, 2, 2, plan)
        _plan_start(plan, pctx, grid)
        i, j = pl.program_id(1), pl.program_id(2)
        first = lax.broadcasted_iota(jnp.int32, (bq, 128), 1) < 64

        @pl.when(j == 0)
        def _():
            m_sc[...] = jnp.full_like(m_sc, -jnp.inf)
            acc_sc[...] = jnp.zeros_like(acc_sc)

        def tile(r0, nr, nc, masked):
            rs = slice(r0, r0 + nr)
            vp = v_ref[0:nc, :]
            lanes = first[0:nc, :]
            for h in range(2):
                sl = slice(h * 128, (h + 1) * 128)
                s = lax.dot_general(q_ref[rs, sl], k_ref[0:nc, sl], NT_DIMS, preferred_element_type=F32)
                if masked:
                    rows = r0 + lax.broadcasted_iota(jnp.int32, (nr, nc), 0)
                    cols = lax.broadcasted_iota(jnp.int32, (nr, nc), 1)
                    s = jnp.where(cols <= rows, s, -jnp.inf)
                m_prev = m_sc[h, rs, 0:1]
                m_new = jnp.maximum(m_prev, jnp.max(s, axis=1, keepdims=True))
                alpha = jnp.exp(m_prev - m_new)
                p = jnp.exp(s - m_new).astype(BF16)
                vh = jnp.where(lanes if h == 0 else jnp.logical_not(lanes), vp, jnp.ones_like(vp))
                acc_sc[h, rs, :] = acc_sc[h, rs, :] * alpha + jnp.dot(p, vh, preferred_element_type=F32)
                m_sc[h, rs, :] = jnp.broadcast_to(m_new, (nr, 128))

        @pl.when(j < i)
        def _():
            tile(0, bq, bq, False)

        @pl.when(j == i)
        def _():
            tile(0, bq, bq, True)
            a0, a1 = acc_sc[0], acc_sc[1]
            l0, l1 = pltpu.roll(a0, 64, 1), pltpu.roll(a1, 64, 1)
            o_ref[...] = jnp.where(first, a0 / l0, a1 / l1).astype(BF16)
            lse_ref[...] = jnp.where(first, m_sc[0] + jnp.log(l0), m_sc[1] + jnp.log(l1))

        _plan_wait(plan, pctx, grid)

    kv = lambda hp, i, j: (jnp.minimum(i, j), hp)
    p_in, p_ospec, p_oshape, p_scr, p_alias = _plan_io(plan, 3, 2)
    return pl.pallas_call(
        body, name="flash_fwd", grid=grid,
        in_specs=[pl.BlockSpec((bq, 256), lambda hp, i, j: (i, hp)), pl.BlockSpec((bq, 256), kv),
                  pl.BlockSpec((bq, 128), kv)] + [_ANY] * len(p_in),
        out_specs=[pl.BlockSpec((bq, 128), lambda hp, i, j: (i, hp)),
                   pl.BlockSpec((bq, 128), lambda hp, i, j: (i, hp))] + p_ospec,
        out_shape=[jax.ShapeDtypeStruct((T, 512), BF16), jax.ShapeDtypeStruct((T, 512), F32)] + p_oshape,
        scratch_shapes=[pltpu.VMEM((2, bq, 128), F32), pltpu.VMEM((2, bq, 128), F32)] + p_scr,
        input_output_aliases=p_alias, compiler_params=_params(56, 3),
    )(*_hbm(q, k, v), *p_in)


def _attn_delta(dmix, o):
    T = o.shape[0]
    tm = min(ROW_BLOCK, T)
    blk = pl.BlockSpec((tm, 512), lambda i: (i, 0))

    def body(do_ref, o_ref, delta_ref, dob_ref):
        first = lax.broadcasted_iota(jnp.int32, (tm, 128), 1) < 64
        for hp in range(4):
            sl = slice(hp * 128, (hp + 1) * 128)
            prod = do_ref[:, sl] * o_ref[:, sl].astype(F32)
            d0 = jnp.sum(jnp.where(first, prod, 0.0), axis=1, keepdims=True)
            d1 = jnp.sum(jnp.where(first, 0.0, prod), axis=1, keepdims=True)
            delta_ref[:, sl] = jnp.where(first, d0, d1)
        dob_ref[...] = do_ref[...].astype(BF16)

    return pl.pallas_call(
        body, name="attn_delta", grid=(T // tm,), in_specs=[blk, blk], out_specs=[blk, blk],
        out_shape=[jax.ShapeDtypeStruct((T, 512), F32), jax.ShapeDtypeStruct((T, 512), BF16)],
        compiler_params=_params(32, 1),
    )(dmix, o)


def _flash_bwd(q, k, v, do_b, lse, delta, plan=None):
    T = q.shape[0]
    bq = min(2 * ROW_BLOCK, T)
    nq = T // bq
    grid = (4, nq, nq)

    def body(*refs):
        ((q_ref, k_ref, v_ref, do_ref, lse_ref, dl_ref), (dq_hbm, dk_ref, dv_ref), (dq_sc, dk_sc, dv_sc, sem),
         pctx) = _split_refs(refs, 6, 3, 4, plan)
        _plan_start(plan, pctx, grid)
        hp, j, i = pl.program_id(0), pl.program_id(1), pl.program_id(2)
        first = lax.broadcasted_iota(jnp.int32, (bq, 128), 1) < 64

        @pl.when((j == 0) & (i == 0))
        def _():
            dq_sc[...] = jnp.zeros_like(dq_sc)

        @pl.when(i == j)
        def _():
            dk_sc[...] = jnp.zeros_like(dk_sc)
            dv_sc[...] = jnp.zeros_like(dv_sc)

        def tile(r0, nr, nc, masked):
            rs, cs = slice(r0, r0 + nr), slice(0, nc)
            vp = v_ref[cs, :]
            do = do_ref[rs, :]
            lanes = first[rs, :]
            for h in range(2):
                sl = slice(h * 128, (h + 1) * 128)
                qh, kh = q_ref[rs, sl], k_ref[cs, sl]
                s = lax.dot_general(qh, kh, NT_DIMS, preferred_element_type=F32)
                p = jnp.exp(s - lse_ref[rs, h * 64:h * 64 + 1])
                if masked:
                    rows = r0 + lax.broadcasted_iota(jnp.int32, (nr, nc), 0)
                    cols = lax.broadcasted_iota(jnp.int32, (nr, nc), 1)
                    p = jnp.where(cols <= rows, p, 0.0)
                do_h = jnp.where(lanes if h == 0 else jnp.logical_not(lanes), do, jnp.zeros_like(do))
                dv_sc[cs, :] += lax.dot_general(p.astype(BF16), do_h, TN_DIMS, preferred_element_type=F32)
                dp = lax.dot_general(do_h, vp, NT_DIMS, preferred_element_type=F32)
                ds = (p * (dp - dl_ref[rs, h * 64:h * 64 + 1])).astype(BF16)
                dq_sc[i, rs, sl] += jnp.dot(ds, kh, preferred_element_type=F32)
                dk_sc[cs, sl] += lax.dot_general(ds, qh, TN_DIMS, preferred_element_type=F32)

        @pl.when(i > j)
        def _():
            tile(0, bq, bq, False)

        @pl.when(i == j)
        def _():
            tile(0, bq // 2, bq // 2, True)
            tile(bq // 2, bq // 2, bq, True)

        @pl.when(i == nq - 1)
        def _():
            dk_ref[...] = dk_sc[...]
            dv_ref[...] = dv_sc[...]

        @pl.when((j == nq - 1) & (i == nq - 1))
        def _():
            cp = pltpu.make_async_copy(dq_sc, dq_hbm.at[hp], sem)
            cp.start()
            cp.wait()

        _plan_wait(plan, pctx, grid)

    qi = lambda hp, j, i: (jnp.maximum(i, j), hp)
    kj = lambda hp, j, i: (j, hp)
    p_in, p_ospec, p_oshape, p_scr, p_alias = _plan_io(plan, 6, 3)
    return pl.pallas_call(
        body, name="flash_bwd", grid=grid,
        in_specs=[pl.BlockSpec((bq, 256), qi), pl.BlockSpec((bq, 256), kj), pl.BlockSpec((bq, 128), kj),
                  pl.BlockSpec((bq, 128), qi), pl.BlockSpec((bq, 128), qi), pl.BlockSpec((bq, 128), qi)]
        + [_ANY] * len(p_in),
        out_specs=[_ANY, pl.BlockSpec((bq, 256), kj), pl.BlockSpec((bq, 128), kj)] + p_ospec,
        out_shape=[jax.ShapeDtypeStruct((4, nq, bq, 256), F32), jax.ShapeDtypeStruct((T, 1024), F32),
                   jax.ShapeDtypeStruct((T, 512), F32)] + p_oshape,
        scratch_shapes=[pltpu.VMEM((nq, bq, 256), F32), pltpu.VMEM((bq, 256), F32), pltpu.VMEM((bq, 128), F32),
                        pltpu.SemaphoreType.DMA] + p_scr,
        input_output_aliases=p_alias, compiler_params=_params(56, 3),
    )(*_hbm(q, k, v, do_b, lse, delta), *p_in)


def _mla_bwd(z0, dq4, dk, dv, gq, gkv, wq, wk, wv, rc, rs1, rs2, plan=None):
    T = z0.shape[0]
    tm = min(ROW_BLOCK, T)
    HW = HEADS * 128
    grid = (T // tm,)
    dq4 = dq4.reshape(4, T, 256)

    def body(*refs):
        ((cq_ref, ckv_ref, dq_ref, dk_ref, dv_ref, gq_ref, gkv_ref, wq_ref, wk_ref, wv_ref, c_ref, s1_ref, s2_ref),
         (dc_ref, dkr_ref, dwq_ref, dwk_ref, dwv_ref, dgq_ref, dgkv_ref), _, pctx) = _split_refs(refs, 13, 7, 0, plan)
        _plan_start(plan, pctx, grid)

        @pl.when(pl.program_id(0) == 0)
        def _():
            for ref in (dwq_ref, dwk_ref, dwv_ref, dgq_ref, dgkv_ref):
                ref[...] = jnp.zeros_like(ref)

        c, s1, s2 = c_ref[...], s1_ref[...], s2_ref[...]
        lane = lax.broadcasted_iota(jnp.int32, (tm, 128), 1)
        nq, xq, rq = _rms(cq_ref[...], gq_ref[...])
        nkv, xkv, rkv = _rms(ckv_ref[...], gkv_ref[...])
        nq_b, nkv_b = nq.astype(BF16), nkv.astype(BF16)

        dq_parts, dk_parts = [], []
        dkr = jnp.zeros((tm, 128), F32)
        for h in range(HEADS):
            blk = dq_ref[h // 2, :, (h % 2) * 128:(h % 2 + 1) * 128] * QK_SCALE
            dq_parts.append(_rope_t(blk, c, s1, s2).astype(BF16))
            kb = dk_ref[:, h * 128:(h + 1) * 128]
            dk_parts.append(jnp.where(lane < NOPE, kb, 0.0).astype(BF16))
            dkr = dkr + kb
        dq_b = jnp.concatenate(dq_parts, axis=1)
        dk_b = jnp.concatenate(dk_parts, axis=1)
        dv_b = dv_ref[...].astype(BF16)

        dwq_ref[...] += lax.dot_general(nq_b, dq_b, TN_DIMS, preferred_element_type=F32)
        dwk_ref[...] += lax.dot_general(nkv_b, dk_b, TN_DIMS, preferred_element_type=F32)
        dwv_ref[...] += lax.dot_general(nkv_b, dv_b, TN_DIMS, preferred_element_type=F32)
        dnq = lax.dot_general(dq_b, wq_ref[...], NT_DIMS, preferred_element_type=F32)
        dnkv = (lax.dot_general(dk_b, wk_ref[...], NT_DIMS, preferred_element_type=F32)
                + lax.dot_general(dv_b, wv_ref[...], NT_DIMS, preferred_element_type=F32))

        def rms_bwd(dn, xhat, rstd, g):
            dxh = dn * g
            return rstd * (dxh - xhat * jnp.mean(dxh * xhat, -1, keepdims=True))

        dc_ref[:, :256] = rms_bwd(dnq, xq, rq, gq_ref[...]).astype(BF16)
        dc_ref[:, 256:] = rms_bwd(dnkv, xkv, rkv, gkv_ref[...]).astype(BF16)
        dgq_ref[...] += _fold8(dnq * xq)
        dgkv_ref[...] += _fold8(dnkv * xkv)
        dkr = pltpu.roll(_rope_t(dkr, c, s1, s2), 64, 1)
        dkr_ref[...] = jnp.where(lane < ROPE, dkr, 0.0).astype(BF16)
        _plan_wait(plan, pctx, grid)

    full = lambda shape: pl.BlockSpec(shape, lambda i: (0,) * len(shape))
    tab = pl.BlockSpec((tm, 128), lambda i: (i, 0))
    p_in, p_ospec, p_oshape, p_scr, p_alias = _plan_io(plan, 13, 7)
    return pl.pallas_call(
        body, name="mla_bwd", grid=grid,
        in_specs=[pl.BlockSpec((tm, 256), lambda i: (i, 0)), pl.BlockSpec((tm, 256), lambda i: (i, 1)),
                  pl.BlockSpec((4, tm, 256), lambda i: (0, i, 0)),
                  pl.BlockSpec((tm, HW), lambda i: (i, 0)), pl.BlockSpec((tm, 512), lambda i: (i, 0)),
                  full((1, 256)), full((1, 256)), full((256, HW)), full((256, HW)), full((256, 512)), tab, tab, tab]
        + [_ANY] * len(p_in),
        out_specs=[pl.BlockSpec((tm, 512), lambda i: (i, 0)), tab, full((256, HW)), full((256, HW)),
                   full((256, 512)), full((8, 256)), full((8, 256))] + p_ospec,
        out_shape=[jax.ShapeDtypeStruct((T, 512), BF16), jax.ShapeDtypeStruct((T, 128), BF16),
                   jax.ShapeDtypeStruct((256, HW), F32), jax.ShapeDtypeStruct((256, HW), F32),
                   jax.ShapeDtypeStruct((256, 512), F32), jax.ShapeDtypeStruct((8, 256), F32),
                   jax.ShapeDtypeStruct((8, 256), F32)] + p_oshape,
        scratch_shapes=p_scr, input_output_aliases=p_alias, compiler_params=_params(48, 1),
    )(*_hbm(z0, z0, dq4, dk, dv, gq, gkv, wq, wk, wv, rc, rs1, rs2), *p_in)


def _sgu_fwd(z0, a_out, ln_g, ln_b, w, b_t):
    T = z0.shape[0]
    tm = min(ROW_BLOCK, T)
    W = SGU_G * SGU_C

    def body(u_ref, v_ref, a_ref, g_ref, b_ref, w_ref, bt_ref, o_ref):
        o_ref[:, :W] = a_ref[...]
        ug = _gelu(u_ref[...])
        xhat, _ = _ln_stats(_gelu(v_ref[...]))
        vn = (xhat * g_ref[...] + b_ref[...]).astype(BF16)
        tril = lax.broadcasted_iota(jnp.int32, (SGU_C, SGU_C), 0) >= lax.broadcasted_iota(jnp.int32, (SGU_C, SGU_C), 1)
        for g in range(SGU_G):
            cs = slice(g * SGU_C, (g + 1) * SGU_C)
            wg = jnp.where(tril, w_ref[g], 0.0).astype(BF16)
            bcol = bt_ref[:, g:g + 1]
            for c in range(tm // SGU_C):
                rs = slice(c * SGU_C, (c + 1) * SGU_C)
                mixed = jnp.dot(wg, vn[rs, cs], preferred_element_type=F32) + bcol
                o_ref[rs, W + g * SGU_C:W + (g + 1) * SGU_C] = (ug[rs, cs] * mixed).astype(BF16)

    full = lambda shape: pl.BlockSpec(shape, lambda i: (0,) * len(shape))
    return pl.pallas_call(
        body, name="sgu_fwd", grid=(T // tm,),
        in_specs=[pl.BlockSpec((tm, W), lambda i: (i, 1)), pl.BlockSpec((tm, W), lambda i: (i, 2)),
                  pl.BlockSpec((tm, W), lambda i: (i, 0)),
                  full((1, W)), full((1, W)), full((SGU_G, SGU_C, SGU_C)), full((SGU_C, SGU_G))],
        out_specs=pl.BlockSpec((tm, 2 * W), lambda i: (i, 0)),
        out_shape=jax.ShapeDtypeStruct((T, 2 * W), BF16),
        compiler_params=_params(32, 1),
    )(z0, z0, a_out, ln_g, ln_b, w, b_t)


def _sgu_bwd(z0, dmix, dc, dkr, ln_g, ln_b, w, b_t):
    T = z0.shape[0]
    tm = min(ROW_BLOCK, T)
    W = SGU_G * SGU_C

    def body(u_ref, v_ref, do_ref, dc_ref, dkr_ref, g_ref, b_ref, w_ref, bt_ref, dz_ref, dw_ref, db_ref, dlg_ref,
             dlb_ref):
        @pl.when(pl.program_id(0) == 0)
        def _():
            for ref in (dw_ref, db_ref, dlg_ref, dlb_ref):
                ref[...] = jnp.zeros_like(ref)

        dz_ref[:, :W] = dc_ref[...]
        dz_ref[:, 3 * W:] = dkr_ref[...]

        u, v, dout = u_ref[...], v_ref[...], do_ref[...]
        ug = _gelu(u)
        xhat, rstd = _ln_stats(_gelu(v))
        vn = (xhat * g_ref[...] + b_ref[...]).astype(BF16)
        dmixed = dout * ug
        dmixed_b = dmixed.astype(BF16)
        tril = lax.broadcasted_iota(jnp.int32, (SGU_C, SGU_C), 0) >= lax.broadcasted_iota(jnp.int32, (SGU_C, SGU_C), 1)
        lane = lax.broadcasted_iota(jnp.int32, (SGU_C, SGU_C), 1)
        dvn_cols = []
        for g in range(SGU_G):
            cs = slice(g * SGU_C, (g + 1) * SGU_C)
            wg = jnp.where(tril, w_ref[g], 0.0).astype(BF16)
            bcol = bt_ref[:, g:g + 1]
            dw_g = jnp.zeros((SGU_C, SGU_C), F32)
            db_g = jnp.zeros((SGU_C, 1), F32)
            dvn_rows = []
            for c in range(tm // SGU_C):
                rs = slice(c * SGU_C, (c + 1) * SGU_C)
                mixed = jnp.dot(wg, vn[rs, cs], preferred_element_type=F32) + bcol
                dz_ref[rs, W + g * SGU_C:W + (g + 1) * SGU_C] = (dout[rs, cs] * mixed * _gelu_grad(u[rs, cs])).astype(BF16)
                dm = dmixed_b[rs, cs]
                dw_g = dw_g + lax.dot_general(dm, vn[rs, cs], NT_DIMS, preferred_element_type=F32)
                db_g = db_g + jnp.sum(dmixed[rs, cs], axis=1, keepdims=True)
                dvn_rows.append(lax.dot_general(wg, dm, TN_DIMS, preferred_element_type=F32))
            dw_ref[g] += jnp.where(tril, dw_g, 0.0)
            db_ref[...] += jnp.where(lane == g, db_g, 0.0)
            dvn_cols.append(jnp.concatenate(dvn_rows, axis=0))
        dvn = jnp.concatenate(dvn_cols, axis=1)
        dxh = dvn * g_ref[...]
        m1 = jnp.mean(dxh, -1, keepdims=True)
        m2 = jnp.mean(dxh * xhat, -1, keepdims=True)
        dvg = rstd * (dxh - m1 - xhat * m2)
        dz_ref[:, 2 * W:3 * W] = (dvg * _gelu_grad(v)).astype(BF16)
        dlg_ref[...] += _fold8(dvn * xhat)
        dlb_ref[...] += _fold8(dvn)

    full = lambda shape: pl.BlockSpec(shape, lambda i: (0,) * len(shape))
    return pl.pallas_call(
        body, name="sgu_bwd", grid=(T // tm,),
        in_specs=[pl.BlockSpec((tm, W), lambda i: (i, 1)), pl.BlockSpec((tm, W), lambda i: (i, 2)),
                  pl.BlockSpec((tm, W), lambda i: (i, 1)), pl.BlockSpec((tm, W), lambda i: (i, 0)),
                  pl.BlockSpec((tm, 128), lambda i: (i, 0)),
                  full((1, W)), full((1, W)), full((SGU_G, SGU_C, SGU_C)), full((SGU_C, SGU_G))],
        out_specs=[pl.BlockSpec((tm, 3 * W + 128), lambda i: (i, 0)), full((SGU_G, SGU_C, SGU_C)),
                   full((SGU_C, SGU_C)), full((8, W)), full((8, W))],
        out_shape=[jax.ShapeDtypeStruct((T, 3 * W + 128), BF16), jax.ShapeDtypeStruct((SGU_G, SGU_C, SGU_C), F32),
                   jax.ShapeDtypeStruct((SGU_C, SGU_C), F32), jax.ShapeDtypeStruct((8, W), F32),
                   jax.ShapeDtypeStruct((8, W), F32)],
        compiler_params=_params(40, 1),
    )(z0, z0, dmix, dc, dkr, ln_g, ln_b, w, b_t)


def _hg_lower_bound(lb_ref):
    a0, a1 = lb_ref[0:1, :], lb_ref[1:2, :]
    m = jnp.maximum(a0, a1)
    e0, e1 = jnp.exp(a0 - m), jnp.exp(a1 - m)
    return e1 / (e0 + e1)


def _running_sum(x, reverse=False):
    n = x.shape[0]
    row = lax.broadcasted_iota(jnp.int32, x.shape, 0)
    s = 1
    while s < n:
        if reverse:
            x = x + jnp.where(row < n - s, pltpu.roll(x, n - s, 0), 0.0)
        else:
            x = x + jnp.where(row >= s, pltpu.roll(x, s, 0), 0.0)
        s *= 2
    return x


def _hg_chunk(qc, fc, lb):
    C = HG_CHUNK
    rows = lax.broadcasted_iota(jnp.int32, (C, C), 0)
    cols = lax.broadcasted_iota(jnp.int32, (C, C), 1)
    rowid = lax.broadcasted_iota(jnp.int32, (C, 128), 0)
    sq, sg = _sigmoid(qc), _sigmoid(fc)
    qf = qc * sq
    gate = lb + (1.0 - lb) * sg
    kk = 1.0 - gate
    lg = jnp.log(gate)
    bcum = _running_sum(lg)
    b_mid = jnp.sum(jnp.where(rowid < C // 2, lg, 0.0), axis=0, keepdims=True)
    b_last = jnp.sum(lg, axis=0, keepdims=True)
    eq, ek, e, eh = jnp.exp(bcum - b_mid), jnp.exp(b_mid - bcum), jnp.exp(bcum), jnp.exp(b_last - bcum)
    qt, kt, qe, khat = qf * eq, kk * ek, qf * e, kk * eh
    a = lax.dot_general(qt.astype(BF16), kt.astype(BF16), NT_DIMS, preferred_element_type=F32)
    a = jnp.where(rows >= cols, a, 0.0)
    return dict(sq=sq, sg=sg, gate=gate, kk=kk, eq=eq, ek=ek, e=e, eh=eh, qt=qt, kt=kt, qe=qe, khat=khat, a=a,
                e_last=jnp.exp(b_last), tril=rows >= cols, rowid=rowid)


def _hgrn_fwd(z4, hg_lb, gnorm):
    T = z4.shape[1]
    tb = min(ROW_BLOCK, T)
    C = HG_CHUNK
    ncb = tb // C
    HPB = HG_HEADS_PER_STEP

    def body(q_ref, f_ref, i_ref, g_ref, lb_ref, gn_ref, y_ref, o_ref, st_ref, st_sc):
        @pl.when(pl.program_id(1) == 0)
        def _():
            st_sc[...] = jnp.zeros_like(st_sc)

        def chunk(c, carry):
            rs = pl.ds(pl.multiple_of(c * C, C), C)
            for hh in range(HPB):
                hs = slice(hh * 128, (hh + 1) * 128)
                lb = _hg_lower_bound(lb_ref.at[:, hs])
                v_b = i_ref[rs, hs].astype(BF16)
                gc = g_ref[rs, hs]
                x = _hg_chunk(q_ref[rs, hs], f_ref[rs, hs], lb)
                st = st_sc[hh]
                st_ref[hh, c] = st
                o = (jnp.dot(x["a"].astype(BF16), v_b, preferred_element_type=F32)
                     + lax.dot_general(x["qe"].astype(BF16), st.astype(BF16), NT_DIMS, preferred_element_type=F32))
                st_sc[hh] = st * x["e_last"] + lax.dot_general(v_b, x["khat"].astype(BF16), TN_DIMS,
                                                               preferred_element_type=F32)
                o_ref[rs, hs] = o
                n = o * lax.rsqrt(jnp.mean(o * o, -1, keepdims=True) + EPS)
                y_ref[rs, hs] = (n * gn_ref[:, hs] * (gc * _sigmoid(gc))).astype(BF16)
            return carry

        lax.fori_loop(0, ncb, chunk, 0)

    W = 128 * HPB
    zb = lambda k: pl.BlockSpec((None, tb, W), lambda h, t: (k, t, h))
    out = pl.BlockSpec((tb, W), lambda h, t: (t, h))
    return pl.pallas_call(
        body, name="hgrn_fwd", grid=(HEADS // HPB, T // tb),
        in_specs=[zb(0), zb(1), zb(2), zb(3), pl.BlockSpec((2, W), lambda h, t: (0, h)),
                  pl.BlockSpec((1, W), lambda h, t: (0, h))],
        out_specs=[out, out, pl.BlockSpec((HPB, ncb, 128, 128), lambda h, t: (h, t, 0, 0))],
        out_shape=[jax.ShapeDtypeStruct((T, D), BF16), jax.ShapeDtypeStruct((T, D), F32),
                   jax.ShapeDtypeStruct((HEADS, T // C, 128, 128), F32)],
        scratch_shapes=[pltpu.VMEM((HPB, 128, 128), F32)],
        compiler_params=_params(48, 2),
    )(*_hbm(z4, z4, z4, z4, hg_lb, gnorm))


def _hgrn_bwd(z4, o_raw, dy, states, hg_lb, gnorm):
    T = z4.shape[1]
    tb = min(ROW_BLOCK, T)
    C = HG_CHUNK
    ncb = tb // C
    nt = T // tb
    HPB = HG_HEADS_PER_STEP

    def body(q_ref, f_ref, i_ref, g_ref, o_ref, dy_ref, st_ref, lb_ref, gn_ref, dz_ref, dlb_ref, dgn_ref, dst_sc):
        @pl.when(pl.program_id(1) == 0)
        def _():
            dst_sc[...] = jnp.zeros_like(dst_sc)
            dlb_ref[...] = jnp.zeros_like(dlb_ref)
            dgn_ref[...] = jnp.zeros_like(dgn_ref)

        def chunk(cc, carry):
            for hh in range(HPB):
                one_head(ncb - 1 - cc, hh, slice(hh * 128, (hh + 1) * 128))
            return carry

        def one_head(c, hh, hs):
            rs = pl.ds(pl.multiple_of(c * C, C), C)
            lb = _hg_lower_bound(lb_ref.at[:, hs])
            gn = gn_ref[:, hs]
            qc, gc = q_ref[rs, hs], g_ref[rs, hs]
            v_b = i_ref[rs, hs].astype(BF16)
            x = _hg_chunk(qc, f_ref[rs, hs], lb)
            st, dst = st_ref[hh, c], dst_sc[hh]
            st_b, dst_b = st.astype(BF16), dst.astype(BF16)
            o, dyc = o_ref[rs, hs], dy_ref[rs, hs]
            sgg = _sigmoid(gc)
            sil = gc * sgg
            rstd = lax.rsqrt(jnp.mean(o * o, -1, keepdims=True) + EPS)
            n = o * rstd
            dgn_ref[:, hs] += _fold8(dyc * n * sil)
            dn = dyc * gn * sil
            do = rstd * (dn - n * jnp.mean(dn * n, -1, keepdims=True))
            dg = dyc * n * gn * (sgg * (1.0 + gc * (1.0 - sgg)))
            do_b = do.astype(BF16)
            da = jnp.where(x["tril"], lax.dot_general(do_b, v_b, NT_DIMS, preferred_element_type=F32), 0.0).astype(BF16)
            qt_b, kt_b, qe_b, khat_b = (x[n_].astype(BF16) for n_ in ("qt", "kt", "qe", "khat"))
            dv = (lax.dot_general(x["a"].astype(BF16), do_b, TN_DIMS, preferred_element_type=F32)
                  + lax.dot_general(khat_b, dst_b, NT_DIMS, preferred_element_type=F32))
            dqt = jnp.dot(da, kt_b, preferred_element_type=F32)
            dqe = jnp.dot(do_b, st_b, preferred_element_type=F32)
            dkt = lax.dot_general(da, qt_b, TN_DIMS, preferred_element_type=F32)
            dkhat = jnp.dot(v_b, dst_b, preferred_element_type=F32)
            dst_sc[hh] = lax.dot_general(do_b, qe_b, TN_DIMS, preferred_element_type=F32) + dst * x["e_last"]
            de_last = jnp.sum(st * dst, axis=0, keepdims=True)
            dqf = dqt * x["eq"] + dqe * x["e"]
            dkk = dkt * x["ek"] + dkhat * x["eh"]
            dkh_kh = dkhat * x["khat"]
            db = dqt * qt_b.astype(F32) - dkt * kt_b.astype(F32) + dqe * x["qe"] - dkh_kh
            db_last = jnp.sum(dkh_kh, axis=0, keepdims=True) + de_last * x["e_last"]
            db = db + jnp.where(x["rowid"] == C - 1, db_last, 0.0)
            dlg = _running_sum(db, reverse=True)
            dgate = dlg / x["gate"] - dkk
            sg, sq = x["sg"], x["sq"]
            dlb_ref[:, hs] += _fold8(dgate * (1.0 - sg)) * (lb * (1.0 - lb))
            dz_ref[0, rs, hs] = (dqf * (sq * (1.0 + qc * (1.0 - sq)))).astype(BF16)
            dz_ref[1, rs, hs] = (dgate * (1.0 - lb) * sg * (1.0 - sg)).astype(BF16)
            dz_ref[2, rs, hs] = dv.astype(BF16)
            dz_ref[3, rs, hs] = dg.astype(BF16)

        lax.fori_loop(0, ncb, chunk, 0)

    W = 128 * HPB
    zb = lambda k: pl.BlockSpec((None, tb, W), lambda h, t: (k, nt - 1 - t, h))
    blk = pl.BlockSpec((tb, W), lambda h, t: (nt - 1 - t, h))
    acc = pl.BlockSpec((8, W), lambda h, t: (0, h))
    return pl.pallas_call(
        body, name="hgrn_bwd", grid=(HEADS // HPB, nt),
        in_specs=[zb(0), zb(1), zb(2), zb(3), blk, blk,
                  pl.BlockSpec((HPB, ncb, 128, 128), lambda h, t: (h, nt - 1 - t, 0, 0)),
                  pl.BlockSpec((2, W), lambda h, t: (0, h)), pl.BlockSpec((1, W), lambda h, t: (0, h))],
        out_specs=[pl.BlockSpec((4, tb, W), lambda h, t: (0, nt - 1 - t, h)), acc, acc],
        out_shape=[jax.ShapeDtypeStruct((4, T, D), BF16), jax.ShapeDtypeStruct((8, D), F32),
                   jax.ShapeDtypeStruct((8, D), F32)],
        scratch_shapes=[pltpu.VMEM((HPB, 128, 128), F32)],
        compiler_params=_params(48, 2),
    )(*_hbm(z4, z4, z4, z4, o_raw, dy, states, hg_lb, gnorm))


def _adamw(w, g, m, v, *, name):
    R, L = w.shape
    tr = R if R <= 512 else 512
    assert R % tr == 0
    blk = pl.BlockSpec((tr, L), lambda i: (i, 0))
    c1, c2 = 1.0 - B1 ** STEP, 1.0 - B2 ** STEP

    def body(w_ref, g_ref, m_ref, v_ref, d_ref, mo_ref, vo_ref):
        g_ = g_ref[...]
        m_ = B1 * m_ref[...] + (1.0 - B1) * g_
        v_ = B2 * v_ref[...] + (1.0 - B2) * (g_ * g_)
        d_ref[...] = -LR * ((m_ / c1) / (jnp.sqrt(v_ / c2) + ADAM_EPS) + WD * w_ref[...])
        mo_ref[...] = m_
        vo_ref[...] = v_

    sds = jax.ShapeDtypeStruct((R, L), F32)
    return pl.pallas_call(
        body, name=name, grid=(R // tr,), in_specs=[blk] * 4, out_specs=[blk] * 3, out_shape=[sds] * 3,
        compiler_params=_params(32, 1),
    )(w, g, m, v)


def _adamw_rows(w, m, v, gbufs, row0, *, name, plan=None):
    L, R, C = w.shape
    tr = 256
    assert R % tr == 0 and row0 % tr == 0 and len(gbufs) == L
    grid = (L, R // tr)
    blk = pl.BlockSpec((None, tr, C), lambda l, i: (l, i, 0))
    gblks = [pl.BlockSpec((tr, C), lambda l, i, k=k: (row0 // tr + jnp.where(l == k, i, 0), 0)) for k in range(L)]
    c1, c2 = 1.0 - B1 ** STEP, 1.0 - B2 ** STEP

    def body(*refs):
        ins, (go_ref, d_ref, mo_ref, vo_ref), _, pctx = _split_refs(refs, 3 + L, 4, 0, plan)
        w_ref, m_ref, v_ref = ins[:3]
        g_refs = ins[3:]
        _plan_start(plan, pctx, grid)
        g_ = g_refs[0][...]
        for l in range(1, L):
            g_ = jnp.where(pl.program_id(0) == l, g_refs[l][...], g_)
        m_ = B1 * m_ref[...] + (1.0 - B1) * g_
        v_ = B2 * v_ref[...] + (1.0 - B2) * (g_ * g_)
        go_ref[...] = g_
        d_ref[...] = -LR * ((m_ / c1) / (jnp.sqrt(v_ / c2) + ADAM_EPS) + WD * w_ref[...])
        mo_ref[...] = m_
        vo_ref[...] = v_
        _plan_wait(plan, pctx, grid)

    sds = jax.ShapeDtypeStruct((L, R, C), F32)
    p_in, p_ospec, p_oshape, p_scr, p_alias = _plan_io(plan, 3 + L, 4)
    return pl.pallas_call(
        body, name=name, grid=grid, in_specs=[blk] * 3 + gblks + [_ANY] * len(p_in),
        out_specs=[blk] * 4 + p_ospec, out_shape=[sds] * 4 + p_oshape, scratch_shapes=p_scr,
        input_output_aliases=p_alias, compiler_params=_params(32, 2),
    )(*_hbm(w, m, v, *gbufs), *p_in)


def _add_pairs(g, theirs, ids, *, name):
    n, R, L = theirs.shape
    tr = math.gcd(R, 128)
    nb = R // tr

    def body(ids_ref, a_ref, b_ref, o_ref):
        o_ref[...] = (a_ref[...].astype(F32) + b_ref[...].astype(F32)).astype(BF16)

    blk = pl.BlockSpec((n, tr, L), lambda i, ids: (0, i, 0))
    return pl.pallas_call(
        body, name=name, out_shape=jax.ShapeDtypeStruct((n, R, L), BF16),
        grid_spec=pltpu.PrefetchScalarGridSpec(
            num_scalar_prefetch=1, grid=(nb,),
            in_specs=[pl.BlockSpec((n, tr, L), lambda i, ids: (0, ids[1] * nb + i, 0)), blk], out_specs=blk),
        compiler_params=_params(16, 1),
    )(ids, g, theirs)


def _sum_chips(pair, parts, ids, *, name):
    _, R, L = parts.shape
    tr = math.gcd(R, 128)

    def body(ids_ref, o_ref, r_ref, out_ref):
        out_ref[...] = ((o_ref[...].astype(F32) + r_ref[0].astype(F32)) + r_ref[1].astype(F32)) + r_ref[2].astype(F32)

    return pl.pallas_call(
        body, name=name, out_shape=jax.ShapeDtypeStruct((2, R, L), F32),
        grid_spec=pltpu.PrefetchScalarGridSpec(
            num_scalar_prefetch=1, grid=(R // tr,),
            in_specs=[pl.BlockSpec((None, tr, L), lambda i, ids: (ids[0], i, 0)),
                      pl.BlockSpec((3, tr, L), lambda i, ids: (0, i, 0))],
            out_specs=pl.BlockSpec((None, tr, L), lambda i, ids: (ids[1], i, 0))),
        compiler_params=_params(32, 1),
    )(ids, pair, parts)


def _mesh_ids():
    x, y, c = _mesh_pos()
    return jnp.stack([2 * x + y, c]).astype(jnp.int32)


def _place_shard(rows, ids, *, name):
    R, L = rows.shape
    tr = 128

    def body(ids_ref, in_ref, out_ref):
        out_ref[...] = in_ref[...].astype(BF16)

    return pl.pallas_call(
        body, name=name, out_shape=jax.ShapeDtypeStruct((4, R, L), BF16),
        grid_spec=pltpu.PrefetchScalarGridSpec(
            num_scalar_prefetch=1, grid=(R // tr,), in_specs=[pl.BlockSpec((tr, L), lambda i, ids: (i, 0))],
            out_specs=pl.BlockSpec((None, tr, L), lambda i, ids: (ids[0], i, 0))),
        compiler_params=_params(16, 1),
    )(ids, rows)


def _remote(src, dst, send_sem, recv_sem, to):
    return pltpu.make_async_remote_copy(src_ref=src, dst_ref=dst, send_sem=send_sem, recv_sem=recv_sem,
                                        device_id=to, device_id_type=MESH_IDS)


def _rows(ref, lead, start, size):
    return ref.at[tuple(pl.ds(0, n) for n in ref.shape[:lead]) + (pl.ds(start, size),)]


def _other_chips():
    x, y, _ = _mesh_pos()
    return [(1 - x, y), (x, 1 - y), (1 - x, 1 - y)]


def _plan_gather_ici(bufs):
    n = len(bufs)

    def copies(outs, send, recv):
        x, y, c = _mesh_pos()
        res = []
        for b in range(n):
            half = bufs[b].shape[1] // 2
            mine = _rows(outs[b].at[2 * x + y], 0, c * half, half)
            for j, (cx, cy) in enumerate(_other_chips()):
                res.append((_remote(mine, mine, send(3 * b + j), recv(3 * b + j), (cx, cy, c)),
                            _remote(mine, _rows(outs[b].at[2 * cx + cy], 0, c * half, half),
                                    send(3 * b + j), recv(3 * b + j), (x, y, c))))
        return res

    def start(ins, outs, send, recv, loc):
        for out_cp, _ in copies(outs, send, recv):
            out_cp.start()

    def wait(ins, outs, send, recv, loc):
        for out_cp, in_cp in copies(outs, send, recv):
            in_cp.wait_recv()
            out_cp.wait_send()

    outs = [jax.ShapeDtypeStruct(b.shape, b.dtype) for b in bufs]
    return _Plan(bufs, outs, 3 * n, 0, start, wait, aliases={b: b for b in range(n)})


def _plan_gather_forward(bufs):
    n = len(bufs)

    def copies(outs, send, recv):
        x, y, c = _mesh_pos()
        res = []
        for b in range(n):
            half = bufs[b].shape[1] // 2
            for j, (cx, cy) in enumerate(_other_chips()):
                slot = outs[b].at[2 * cx + cy]
                res.append((_remote(_rows(slot, 0, c * half, half), _rows(slot, 0, c * half, half),
                                    send(3 * b + j), recv(3 * b + j), (x, y, 1 - c)),
                            _remote(_rows(slot, 0, c * half, half), _rows(slot, 0, (1 - c) * half, half),
                                    send(3 * b + j), recv(3 * b + j), (x, y, c))))
        return res

    def start(ins, outs, send, recv, loc):
        for out_cp, _ in copies(outs, send, recv):
            out_cp.start()

    def wait(ins, outs, send, recv, loc):
        for out_cp, in_cp in copies(outs, send, recv):
            in_cp.wait_recv()
            out_cp.wait_send()

    outs = [jax.ShapeDtypeStruct(b.shape, b.dtype) for b in bufs]
    return _Plan(bufs, outs, 3 * n, 0, start, wait, aliases={b: b for b in range(n)})


def _plan_pair_swap(g):
    half = g.shape[1] // 2

    def copy(ins, outs, send, recv, loc):
        x, y, c = _mesh_pos()
        return _remote(_rows(ins[0], 1, (1 - c) * half, half), outs[0], send(0), recv(0), (x, y, 1 - c))

    return _Plan([g], [jax.ShapeDtypeStruct((4, half, g.shape[2]), g.dtype)], 1, 0,
                 lambda *a: copy(*a).start(), lambda *a: copy(*a).wait())


def _plan_pair_gather(buf):
    def copies(ins, outs, send, recv, loc):
        x, y, c = _mesh_pos()
        return (_remote(outs[0].at[c], outs[0].at[c], send(0), recv(0), (x, y, 1 - c)),
                _remote(outs[0].at[c], outs[0].at[1 - c], send(0), recv(0), (x, y, c)))

    def wait(*a):
        out_cp, in_cp = copies(*a)
        in_cp.wait_recv()
        out_cp.wait_send()

    return _Plan([buf], [jax.ShapeDtypeStruct(buf.shape, buf.dtype)], 1, 0, lambda *a: copies(*a)[0].start(), wait,
                 aliases={0: 0})


def _plan_chip_scatter(p):
    def copies(ins, outs, send, recv, loc):
        _, _, c = _mesh_pos()
        return [_remote(ins[0].at[2 * cx + cy], outs[0].at[j], send(j), recv(j), (cx, cy, c))
                for j, (cx, cy) in enumerate(_other_chips())]

    def start(*a):
        for cp in copies(*a):
            cp.start()

    def wait(*a):
        for cp in copies(*a):
            cp.wait()

    return _Plan([p], [jax.ShapeDtypeStruct((3,) + p.shape[1:], p.dtype)], 3, 0, start, wait)


def _plan_exchange_all(vec):
    def copies(ins, outs, send, recv, loc):
        x, y, c = _mesh_pos()
        return [_remote(ins[0], outs[0].at[r - 1], send(r - 1), recv(r - 1), (x ^ (r >> 2), y ^ ((r >> 1) & 1), c ^ (r & 1)))
                for r in range(1, 8)]

    def start(*a):
        for cp in copies(*a):
            cp.start()

    def wait(*a):
        for cp in copies(*a):
            cp.wait()

    return _Plan([vec], [jax.ShapeDtypeStruct((7,) + vec.shape, vec.dtype)], 7, 0, start, wait)


SMALL_LAYOUT = {
    "mla_gq": (0, 1, 256, (1, 256)), "mla_gkv": (1, 1, 256, (1, 256)), "sgu_ln_g": (2, 1, 512, (1, 512)),
    "sgu_ln_b": (3, 1, 512, (1, 512)), "sgu_w": (4, 64, 1024, (64, 1024)), "sgu_b": (68, 1, 512, (1, 512)),
    "hg_lb": (69, 2, 1024, (2, 1024)), "hg_gnorm": (71, 1, 1024, (1, 256)), "ln1_g": (72, 2, 1024, (2, 1024)),
    "ln1_b": (74, 2, 1024, (2, 1024)), "ln2_g": (76, 2, 1024, (2, 1024)), "ln2_b": (78, 2, 1024, (2, 1024)),
}


def _small_pack(dgq, dgkv, dslg, dslb, dsw, dsb, dlb, dgn, ln_parts, sq_err):
    flat_ln = [p for pair in ln_parts for p in pair]

    def body(*refs):
        gq_ref, gkv_ref, slg_ref, slb_ref, sw_ref, sb_ref, lb_ref, gn_ref = refs[:8]
        ln_refs, err_ref, out_ref, t_sc = refs[8:16], refs[16], refs[17], refs[18]
        s8 = lambda ref: jnp.sum(ref[...], axis=0, keepdims=True)
        out_ref[...] = jnp.zeros_like(out_ref)
        out_ref[0:1, 0:256] = s8(gq_ref)
        out_ref[1:2, 0:256] = s8(gkv_ref)
        out_ref[2:3, 0:512] = s8(slg_ref)
        out_ref[3:4, 0:512] = s8(slb_ref)
        out_ref[4:68, :] = sw_ref[...]
        t_sc[...] = sb_ref[...].T
        for g in range(SGU_G):
            out_ref[68:69, g * SGU_C:(g + 1) * SGU_C] = t_sc[g:g + 1, :]
        d_lb1 = s8(lb_ref)
        out_ref[69:70, :] = -d_lb1
        out_ref[70:71, :] = d_lb1
        out_ref[71:72, :] = s8(gn_ref)
        for k, ref in enumerate(ln_refs):
            out_ref[72 + k:73 + k, :] = s8(ref)
        out_ref[0:1, 1023:1024] = jnp.sum(s8(err_ref), axis=1, keepdims=True) * (0.5 / D)

    vm = pl.BlockSpec(memory_space=pltpu.VMEM)
    return pl.pallas_call(
        body, name="small_grad_pack", in_specs=[vm] * 17, out_specs=vm,
        out_shape=jax.ShapeDtypeStruct((SMALL_ROWS, 1024), F32), scratch_shapes=[pltpu.VMEM((SGU_C, SGU_C), F32)],
        compiler_params=_params(16),
    )(dgq, dgkv, dslg, dslb, dsw.reshape(64, 1024), dsb, dlb, dgn, *flat_ln, sq_err)


def _small_update(vec, others, ids, w, m, v):
    names = list(SMALL_LAYOUT)
    n = len(names)
    c1, c2 = 1.0 - B1 ** STEP, 1.0 - B2 ** STEP
    have_others = others is not None

    def body(*refs):
        ids_ref, v_ref = refs[0], refs[1]
        k = 2 + have_others
        w_refs, m_refs, v_refs = refs[k:k + n], refs[k + n:k + 2 * n], refs[k + 2 * n:k + 3 * n]
        outs = refs[k + 3 * n:]
        row0_ref, tot_sc = outs[0], outs[-1]
        total = v_ref[...]
        if have_others:
            me = 2 * ids_ref[0] + ids_ref[1]
            total = None
            for d in range(8):
                rel = d ^ me
                term = jnp.where(rel == 0, v_ref[...], refs[2][jnp.maximum(rel - 1, 0)])
                total = term if total is None else total + term
        tot_sc[...] = total
        row0_ref[...] = tot_sc[0:1, :]
        for i, name in enumerate(names):
            r0, nr, width, _ = SMALL_LAYOUT[name]
            if name == "hg_gnorm":
                g_ = tot_sc[r0:r0 + 1, 0:256]
                for chip in range(1, 4):
                    g_ = jnp.where(ids_ref[0] == chip, tot_sc[r0:r0 + 1, chip * 256:(chip + 1) * 256], g_)
            else:
                g_ = tot_sc[r0:r0 + nr, 0:width]
            m_ = B1 * m_refs[i][...] + (1.0 - B1) * g_
            v_ = B2 * v_refs[i][...] + (1.0 - B2) * (g_ * g_)
            go, do, mo, vo = outs[1 + 4 * i:5 + 4 * i]
            go[...] = g_
            do[...] = -LR * ((m_ / c1) / (jnp.sqrt(v_ / c2) + ADAM_EPS) + WD * w_refs[i][...])
            mo[...] = m_
            vo[...] = v_

    full = lambda shape: pl.BlockSpec(shape, lambda i, ids, nd=len(shape): (0,) * nd)
    kshapes = [SMALL_LAYOUT[name][3] for name in names]
    operands = [vec] + ([others] if have_others else []) + [d[name] for d in (w, m, v) for name in names]
    out_shapes = [jax.ShapeDtypeStruct((1, 1024), F32)] + [jax.ShapeDtypeStruct(s, F32) for s in kshapes for _ in range(4)]
    res = pl.pallas_call(
        body, name="small_update", out_shape=out_shapes,
        grid_spec=pltpu.PrefetchScalarGridSpec(
            num_scalar_prefetch=1, grid=(1,), in_specs=[full(o.shape) for o in operands],
            out_specs=[full(s.shape) for s in out_shapes],
            scratch_shapes=[pltpu.VMEM((SMALL_ROWS, 1024), F32)]),
        compiler_params=_params(32, 1),
    )(ids, *operands)
    return res[0], {name: tuple(res[1 + 4 * i:5 + 4 * i]) for i, name in enumerate(names)}


ROWS_L1, ROWS_L0, ROWS_ODD, ROWS_ODD_W = 3328, 2048, 768, 384
ODD_PARTS = (("w_out_e", (256, 1024)), ("w_in_e", (1024, 392)), ("w_qb", (256, 192)), ("w_kvb", (256, 256)))
ODD_W_PARTS = tuple(p for p in ODD_PARTS if p[0] != "w_in_e")


def _odd_rows(parts, dtype, layout, total, gnorm=None):
    rows = [parts[n].reshape(-1, 1024).astype(dtype) for n, _ in layout]
    used = sum(r.shape[0] for r in rows)
    if gnorm is not None:
        bits = lax.bitcast_convert_type(gnorm.reshape(-1), BF16).reshape(1, 512)
        rows.append(jnp.pad(bits, ((0, 0), (0, 512))))
        used += 1
    rows.append(jnp.zeros((total - used, 1024), dtype))
    return jnp.concatenate(rows, axis=0)


def _odd_unrows(buf, layout, with_gnorm=False):
    out, off = {}, 0
    for n, shape in layout:
        nr = math.prod(shape) // 1024
        out[n] = buf[off:off + nr].reshape(shape)
        off += nr
    if with_gnorm:
        out["hg_gnorm"] = lax.bitcast_convert_type(buf[off, :512].reshape(256, 2), F32).reshape(1, 256)
    return out


def _rope_tables(positions):
    half = ROPE // 2
    inv_freq = ROPE_BASE ** (-jnp.arange(half, dtype=F32) / half)
    ang = positions.astype(F32).reshape(-1, 1) * inv_freq
    cos, sin = jnp.cos(ang), jnp.sin(ang)
    T = ang.shape[0]
    one, z16, z32 = jnp.ones((T, NOPE), F32), jnp.zeros((T, half), F32), jnp.zeros((T, 32), F32)
    z64 = jnp.zeros((T, NOPE), F32)
    c = jnp.concatenate([one, cos, cos, z32], axis=1)
    s1 = jnp.concatenate([z64, -sin, z16, z32], axis=1)
    s2 = jnp.concatenate([z64, z16, sin, z32], axis=1)
    return c, s1, s2


def _local_step(x, positions, tgt, odd, bufs, P, exchange):
    T = x.shape[0]
    row = lambda a: a.reshape(1, -1)
    rc, rs1, rs2 = _rope_tables(positions)
    blk = lambda f: pl.BlockSpec((None, D, D), f)

    w_in_e = odd["w_in_e"]
    w_in = jnp.concatenate([w_in_e[:, :512], w_in_e[:, 544:1568], w_in_e[:, 512:544], jnp.zeros((D, 96), BF16)], axis=1)
    wq = jnp.pad(odd["w_qb"].reshape(256, HEADS, NOPE + ROPE), ((0, 0), (0, 0), (0, 32))).reshape(256, HEADS * 128)
    kvb = odd["w_kvb"].reshape(256, HEADS, NOPE + VDIM)
    wk = jnp.pad(kvb[:, :, :NOPE], ((0, 0), (0, 0), (0, 64))).reshape(256, HEADS * 128)
    wv = kvb[:, :, NOPE:].reshape(256, HEADS * VDIM)
    w_out_e = odd["w_out_e"]
    sgu_w = P["sgu_w"][0]
    sgu_bt = P["sgu_b"][0].T
    gq, gkv = P["mla_gq"], P["mla_gkv"]
    gnorm = P["hg_gnorm"]

    z0 = _matmul(x, w_in, name="in_proj_e", M=T, N=1664, K=D, tn=1664)[0]
    q, k, v = _mla_prep(z0, gq, gkv, wq, wk, wv, rc, rs1, rs2)
    if exchange:
        ids = _mesh_ids()
        placed = [_place_shard(b, ids, name=f"place_shard_{l}") for l, b in enumerate(bufs)]
        a_out, lse, wga, wgb = _flash_fwd(q, k, v, plan=_plan_gather_ici(placed[:2]))
    else:
        a_out, lse = _flash_fwd(q, k, v)
        wga, wgb, wgc = bufs
    mix0 = _sgu_fwd(z0, a_out, P["sgu_ln_g"], P["sgu_ln_b"], sgu_w, sgu_bt)
    res = _proj_ln(mix0, w_out_e, x, row(P["ln1_g"][0]), row(P["ln1_b"][0]), name="out_proj_ln_e",
                   plan=_plan_gather_forward([wga, wgb]) if exchange else None)
    r1, h1, h1b = res[:3]
    if exchange:
        wga, wgb = res[3:]
    res = _ffn_ln(h1b, wga, h1, row(P["ln2_g"][0]), row(P["ln2_b"][0]), name="ffn_ln_0",
                  plan=_plan_gather_ici(placed[2:]) if exchange else None)
    ra0, r2, h2, h2b = res[:4]
    z4 = _matmul(h2b, wgb, name="in_proj_o", M=T, N=4 * D, K=D, b_spec=blk(lambda i, j, k: (j, 0, 0)),
                 out_shape=jax.ShapeDtypeStruct((4, T, D), F32),
                 o_spec=pl.BlockSpec((None, min(MM_ROWS, T), D), lambda i, j, k: (j, i, 0)))[0]
    y1, o_raw, states = _hgrn_fwd(z4, P["hg_lb"], gnorm)
    res2 = _proj_ln(y1, wgb, h2, row(P["ln1_g"][1]), row(P["ln1_b"][1]), name="out_proj_ln_o", w_rowblk=4,
                    plan=_plan_gather_forward([res[4]]) if exchange else None)
    r3, h3, h3b = res2[:3]
    if exchange:
        wgc = res2[3]
    ra1, r4, h4, _ = _ffn_ln(h3b, wgc, h3, row(P["ln2_g"][1]), row(P["ln2_b"][1]), name="ffn_ln_1")

    ln1_g, ln1_b, ln2_g, ln2_b = [None, None], [None, None], [None, None], [None, None]
    sq_err_parts = []

    def ffn_bwd(l, dh, r_out, ra, h_mid_b, g2, wg, rows, plan=None, tgt=None):
        dr, dr_b, dg, db, *sq_err = _ln_bwd(dh, r_out, row(g2), name=f"ln2_bwd_{l}", tgt=tgt)
        sq_err_parts.extend(sq_err)
        ln2_g[l], ln2_b[l] = dg, db
        da, *extra = _matmul(dr_b, wg, tb=True, mul=ra, out_dtype=BF16, name=f"ffn_da_{l}", M=T, N=4 * D, K=D,
                             b_spec=blk(lambda i, j, k: (j, 1, 0)), plan=plan)
        gbuf = _matmul(ra, dr_b, ta=True, a_sq=True, name=f"ffn_dw2_{l}", M=4 * D, N=D, K=T, tm=1024, tk=DW_TOKENS // 2,
                       out_shape=jax.ShapeDtypeStruct((4, rows, D), BF16), o_spec=blk(lambda i, j, k: (i, 1, 0)))[0]
        gbuf = _matmul(h_mid_b, da, ta=True, name=f"ffn_dw1_{l}", M=D, N=4 * D, K=T, tm=1024, tk=DW_TOKENS, into=gbuf,
                       out_shape=jax.ShapeDtypeStruct((4, rows, D), BF16), o_spec=blk(lambda i, j, k: (j, 0, 0)))[0]
        dh_mid = _matmul(da, wg, tb=True, add=dr, add_scale=ALPHA, name=f"ffn_dh_{l}", M=T, N=D, K=4 * D, tk=2 * D,
                         b_spec=pl.BlockSpec((2, D, D), lambda i, j, k: (k, 0, 0)))[0]
        return dh_mid, gbuf, extra

    dh3, g1, _ = ffn_bwd(1, h4, r4, ra1, h3b, P["ln2_g"][1], wgc, ROWS_L1, tgt=tgt)
    loss_parts = sq_err_parts[0]
    dr3, dr3_b, dg, db = _ln_bwd(dh3, r3, row(P["ln1_g"][1]), name="ln1_bwd_1")
    ln1_g[1], ln1_b[1] = dg, db
    g1_sds = jax.ShapeDtypeStruct((4, ROWS_L1, D), BF16)
    g1 = _matmul(y1, dr3_b, ta=True, name="dw_out_o", M=D, N=D, K=T, tm=256, tk=DW_TOKENS, into=g1, out_shape=g1_sds,
                 o_spec=pl.BlockSpec((None, 256, D), lambda i, j, k: (i, 12, 0)))[0]
    dmix1 = _matmul(dr3_b, wgb, tb=True, name="dmix_o", M=T, N=D, K=D, b_spec=_rows4_spec(4, 3), b_merge=(D, D))[0]
    dz4, dlb, dgn = _hgrn_bwd(z4, o_raw, dmix1, states, P["hg_lb"], gnorm)
    g1 = _matmul(h2b, dz4, ta=True, name="dw_in_o", M=D, N=4 * D, K=T, tm=1024, tk=DW_TOKENS, into=g1, out_shape=g1_sds,
                 b_spec=pl.BlockSpec((None, min(DW_TOKENS, T), D), lambda i, j, k: (j, k, 0)),
                 o_spec=blk(lambda i, j, k: (j, 2, 0)))[0]
    dh2 = _matmul(dz4, wgb, tb=True, add=dr3, add_scale=ALPHA, name="dh_in_o", M=T, N=D, K=4 * D, tk=2 * D,
                  a_spec=pl.BlockSpec((2, min(MM_ROWS, T), D), lambda i, j, k: (k, i, 0)),
                  b_spec=pl.BlockSpec((2, D, D), lambda i, j, k: (k, 0, 0)))[0]

    dh1, g0, swapped1 = ffn_bwd(0, dh2, r2, ra0, h1b, P["ln2_g"][0], wga, ROWS_L0,
                                plan=_plan_pair_swap(g1) if exchange else None)
    dr1, dr1_b, dg, db = _ln_bwd(dh1, r1, row(P["ln1_g"][0]), name="ln1_bwd_0")
    ln1_g[0], ln1_b[0] = dg, db
    godd = {"w_out_e": _matmul(mix0, dr1_b, ta=True, name="dw_out_e", M=D, N=D, K=T, tm=1024, tk=DW_TOKENS)[0]}
    dmix0, *swapped0 = _matmul(dr1_b, w_out_e, tb=True, name="dmix_e", M=T, N=D, K=D,
                               plan=_plan_pair_swap(g0) if exchange else None)
    delta, do_b = _attn_delta(dmix0, a_out)
    if exchange:
        pair1 = _add_pairs(g1, swapped1[0], ids, name="grad_pair_add_1")
        pair0 = _add_pairs(g0, swapped0[0], ids, name="grad_pair_add_0")
        dq4, dk, dv, parts0, parts1 = _flash_bwd(
            q, k, v, do_b, lse, delta, plan=_join_plans([_plan_chip_scatter(pair0), _plan_chip_scatter(pair1)]))
        half0 = _sum_chips(pair0, parts0, ids, name="grad_chip_sum_0")
        half1 = _sum_chips(pair1, parts1, ids, name="grad_chip_sum_1")
        dc, dkr, dwq, dwk, dwv, dgq, dgkv, g0, g1 = _mla_bwd(
            z0, dq4, dk, dv, gq, gkv, wq, wk, wv, rc, rs1, rs2,
            plan=_join_plans([_plan_pair_gather(half0), _plan_pair_gather(half1)]))
        g0, g1 = g0.reshape(ROWS_L0, D), g1.reshape(ROWS_L1, D)
    else:
        dq4, dk, dv = _flash_bwd(q, k, v, do_b, lse, delta)
        dc, dkr, dwq, dwk, dwv, dgq, dgkv = _mla_bwd(z0, dq4, dk, dv, gq, gkv, wq, wk, wv, rc, rs1, rs2)
    dz0, dsw, dsb, dslg, dslb = _sgu_bwd(z0, dmix0, dc, dkr, P["sgu_ln_g"], P["sgu_ln_b"], sgu_w, sgu_bt)
    small_vec = _small_pack(dgq, dgkv, dslg, dslb, dsw, dsb, dlb, dgn, [ln1_g, ln1_b, ln2_g, ln2_b], loss_parts)
    dw_in, *small_others = _matmul(x, dz0, ta=True, name="dw_in_e", M=D, N=1664, K=T, tm=1024, tn=1664,
                                   tk=DW_TOKENS // 4, plan=_plan_exchange_all(small_vec) if exchange else None)
    godd["w_in_e"] = jnp.concatenate([dw_in[:, :512], dw_in[:, 1536:1568], dw_in[:, 512:1536]], axis=1)
    godd["w_qb"] = dwq.reshape(256, HEADS, 128)[:, :, :NOPE + ROPE].reshape(256, HEADS * (NOPE + ROPE))
    godd["w_kvb"] = jnp.concatenate([dwk.reshape(256, HEADS, 128)[:, :, :NOPE], dwv.reshape(256, HEADS, VDIM)],
                                    axis=2).reshape(256, HEADS * (NOPE + VDIM))
    odd_plan = None
    if exchange:
        by_chip = [_odd_rows({"w_out_e": jnp.split(godd["w_out_e"], 4, axis=0)[j],
                              **{n: jnp.split(godd[n], 4, axis=1)[j] for n in ("w_qb", "w_kvb")}}, BF16,
                             ODD_W_PARTS, ROWS_ODD_W)
                   for j in range(4)]
        bufs_odd = [godd["w_in_e"].reshape(D, 4, 392).transpose(1, 0, 2).astype(BF16), jnp.stack(by_chip)]
        theirs = _run_plan(_join_plans([_plan_pair_swap(b) for b in bufs_odd]), name="odd_pair_swap")
        odd_pairs = [_add_pairs(b, t, ids, name=f"odd_pair_add_{k}") for k, (b, t) in enumerate(zip(bufs_odd, theirs))]
        odd_plan = _join_plans([_plan_chip_scatter(p) for p in odd_pairs])
    grad_x, *odd_parts = _matmul(dz0, w_in, tb=True, add=dr1, add_scale=ALPHA, name="dx", M=T, N=D, K=1664, tk=1664,
                                 plan=odd_plan)
    if exchange:
        godd = (odd_pairs, odd_parts)
    return grad_x, g0, g1, godd, small_vec, (small_others[0] if exchange else None)


WEIGHTS = ['w_in_e', 'mla_gq', 'mla_gkv', 'w_qb', 'w_kvb', 'sgu_ln_g', 'sgu_ln_b', 'sgu_w', 'sgu_b', 'w_out_e',
           'w_in_o', 'hg_lb', 'hg_gnorm', 'w_out_o', 'ln1_g', 'ln1_b', 'w_ff1', 'w_ff2', 'ln2_g', 'ln2_b']


def kernel(x, positions, w_in_e, mla_gq, mla_gkv, w_qb, w_kvb, sgu_ln_g, sgu_ln_b, sgu_w, sgu_b, w_out_e, w_in_o, hg_lb, hg_gnorm, w_out_o, ln1_g, ln1_b, w_ff1, w_ff2, ln2_g, ln2_b, loss_target, m_w_in_e, m_mla_gq, m_mla_gkv, m_w_qb, m_w_kvb, m_sgu_ln_g, m_sgu_ln_b, m_sgu_w, m_sgu_b, m_w_out_e, m_w_in_o, m_hg_lb, m_hg_gnorm, m_w_out_o, m_ln1_g, m_ln1_b, m_w_ff1, m_w_ff2, m_ln2_g, m_ln2_b, v_w_in_e, v_mla_gq, v_mla_gkv, v_w_qb, v_w_kvb, v_sgu_ln_g, v_sgu_ln_b, v_sgu_w, v_sgu_b, v_w_out_e, v_w_in_o, v_hg_lb, v_hg_gnorm, v_w_out_o, v_ln1_g, v_ln1_b, v_w_ff1, v_w_ff2, v_ln2_g, v_ln2_b):
    args = dict(locals())
    w = {n: args[n] for n in WEIGHTS}
    m = {n: args["m_" + n] for n in WEIGHTS}
    v = {n: args["v_" + n] for n in WEIGHTS}
    cx, cy, cc = _mesh_pos()
    chip = 2 * cx + cy

    odd_shard = _odd_rows({"w_out_e": w_out_e[0], "w_qb": w_qb[0], "w_kvb": w_kvb[0]}, BF16, ODD_W_PARTS, ROWS_ODD_W,
                          gnorm=hg_gnorm)
    ids = _mesh_ids()
    placed = [_place_shard(w_in_e[0], ids, name="place_shard_in_e"), _place_shard(odd_shard, ids, name="place_shard_odd")]
    gathered = _run_plan(_plan_gather_forward(_run_plan(_plan_gather_ici(placed), name="odd_gather")),
                         name="odd_gather_forward")
    per_chip = [_odd_unrows(gathered[1][j], ODD_W_PARTS, with_gnorm=True) for j in range(4)]
    odd = {"w_out_e": jnp.concatenate([p["w_out_e"] for p in per_chip], axis=0),
           "w_in_e": jnp.concatenate([gathered[0][j] for j in range(4)], axis=1)}
    for n in ("w_qb", "w_kvb"):
        odd[n] = jnp.concatenate([p[n] for p in per_chip], axis=1)
    small = {n: w[n] for n in SMALL_LAYOUT if n != "hg_gnorm"}
    small["hg_gnorm"] = jnp.concatenate([p["hg_gnorm"] for p in per_chip], axis=1)
    shard_rows = (jnp.concatenate([w_ff1[0], w_ff2[0]], axis=0).astype(BF16),
                  jnp.concatenate([w_in_o[0], w_out_o[0]], axis=0).astype(BF16),
                  jnp.concatenate([w_ff1[1], w_ff2[1]], axis=0).astype(BF16))

    grad_x, g_l0, g_l1, godd, small_vec, small_others = _local_step(
        x[0], positions[0], loss_target[0], odd, shard_rows, small, True)

    sums = [_sum_chips(pair, parts, ids, name=f"odd_chip_sum_{k}") for k, (pair, parts) in enumerate(zip(*godd))]
    g_in_e, g_rest = _run_plan(_join_plans([_plan_pair_gather(s) for s in sums]), name="odd_pair_gather")
    g_odd = _odd_unrows(g_rest.reshape(ROWS_ODD_W, 1024), ODD_W_PARTS)
    g_odd["w_in_e"] = g_in_e.reshape(D, 392)

    to_kernel = lambda d: {n: d[n].reshape(SMALL_LAYOUT[n][3]) for n in SMALL_LAYOUT}
    first_row, small_out = _small_update(small_vec, small_others, ids, to_kernel(w), to_kernel(m), to_kernel(v))
    loss = first_row[0, 1023]
    grads, delta, new_m, new_v = {}, {}, {}, {}
    for n, res in small_out.items():
        grads[n], delta[n], new_m[n], new_v[n] = (r.reshape(w[n].shape) for r in res)

    for n, bufs_, row0 in (("w_ff1", [g_l0, g_l1], 0), ("w_ff2", [g_l0, g_l1], 1024), ("w_in_o", [g_l1], 2048),
                           ("w_out_o", [g_l1], 3072)):
        grads[n], delta[n], new_m[n], new_v[n] = _adamw_rows(w[n], m[n], v[n], bufs_, row0, name=f"adamw_{n}")
    for n, _ in ODD_PARTS:
        grads[n] = g_odd[n][None]
        d_, m_, v_ = _adamw(w[n][0], g_odd[n], m[n][0], v[n][0], name=f"adamw_{n}")
        delta[n], new_m[n], new_v[n] = d_[None], m_[None], v_[None]

    return (loss, grad_x[None], *[grads[n] for n in WEIGHTS], *[delta[n] for n in WEIGHTS],
            *[new_m[n] for n in WEIGHTS], *[new_v[n] for n in WEIGHTS])
```

```python
import math

import jax
import jax.numpy as jnp
from jax import lax
from jax.experimental import pallas as pl
from jax.experimental.pallas import tpu as pltpu

F32 = jnp.float32
BF16 = jnp.bfloat16
MESH_IDS = pl.DeviceIdType.MESH

D = 1024
DEPTH = 2
HEADS = 8
NOPE, ROPE, VDIM = 64, 32, 64
QK_SCALE = (NOPE + ROPE) ** -0.5
ROPE_BASE = 10000.0
SGU_G, SGU_C = 4, 128
HG_CHUNK = 64
HG_HEADS_PER_STEP = 8
ALPHA = (2 * DEPTH) ** 0.25
EPS = 1e-5
LR, B1, B2, ADAM_EPS, WD, STEP = 0.001, 0.9, 0.999, 1e-08, 0.01, 10
GELU_C = math.sqrt(2.0 / math.pi)
GELU_A = 0.044715
MB = 1024 * 1024
ROW_BLOCK = 512
SMALL_ROWS = 80

NT_DIMS = (((1,), (1,)), ((), ()))
TN_DIMS = (((0,), (0,)), ((), ()))


def _params(vmem_mb, n_axes=0):
    kw = dict(vmem_limit_bytes=vmem_mb * MB)
    if n_axes:
        kw["dimension_semantics"] = ("arbitrary",) * n_axes
    return pltpu.CompilerParams(**kw)


_ANY = pl.BlockSpec(memory_space=pltpu.HBM)


def _mesh_pos():
    return lax.axis_index("x"), lax.axis_index("y"), lax.axis_index("c")


def _hbm(*arrays):
    return tuple(pltpu.with_memory_space_constraint(a, pltpu.HBM) if a.size >= 2 ** 18 else a for a in arrays)


class _Plan:
    def __init__(self, ins, outs, n_remote, n_local, start, wait, aliases=None):
        self.ins, self.outs, self.n_remote, self.n_local = list(ins), list(outs), n_remote, n_local
        self.start, self.wait, self.aliases = start, wait, dict(aliases or {})


def _join_plans(plans):
    ins, outs, aliases, parts = [], [], {}, []
    nr = nl = 0
    for p in plans:
        parts.append((p, len(ins), len(outs), nr, nl))
        aliases.update({len(ins) + i: len(outs) + o for i, o in p.aliases.items()})
        ins += p.ins
        outs += p.outs
        nr += p.n_remote
        nl += p.n_local

    def run(which):
        def go(in_refs, out_refs, send, recv, loc):
            for p, i0, o0, r0, l0 in parts:
                getattr(p, which)(in_refs[i0:i0 + len(p.ins)], out_refs[o0:o0 + len(p.outs)],
                                  lambda i, r0=r0: send(r0 + i), lambda i, r0=r0: recv(r0 + i),
                                  lambda i, l0=l0: loc(l0 + i))
        return go

    return _Plan(ins, outs, nr, nl, run("start"), run("wait"), aliases)


def _plan_io(plan, n_in, n_out):
    if plan is None:
        return [], [], [], [], {}
    sems = [pltpu.SemaphoreType.DMA((max(plan.n_remote, 1),)), pltpu.SemaphoreType.DMA((max(plan.n_remote, 1),)),
            pltpu.SemaphoreType.DMA((max(plan.n_local, 1),))]
    aliases = {n_in + i: n_out + o for i, o in plan.aliases.items()}
    return plan.ins, [_ANY] * len(plan.outs), plan.outs, sems, aliases


def _split_refs(refs, n_in, n_out, n_scr, plan):
    p_in, p_out = (len(plan.ins), len(plan.outs)) if plan is not None else (0, 0)
    refs = list(refs)
    ins, refs = refs[:n_in], refs[n_in:]
    pins, refs = refs[:p_in], refs[p_in:]
    outs, refs = refs[:n_out], refs[n_out:]
    pouts, refs = refs[:p_out], refs[p_out:]
    scr, psem = refs[:n_scr], refs[n_scr:]
    psem = tuple((lambda i, s=s: s.at[i]) for s in psem)
    return ins, outs, scr, (pins, pouts, psem)


def _grid_edge(grid, last):
    cond = None
    for ax, n in enumerate(grid):
        c = pl.program_id(ax) == (n - 1 if last else 0)
        cond = c if cond is None else cond & c
    return cond


def _plan_start(plan, pctx, grid):
    if plan is not None:
        pins, pouts, psem = pctx
        pl.when(_grid_edge(grid, False))(lambda: plan.start(pins, pouts, *psem))


def _plan_wait(plan, pctx, grid):
    if plan is not None:
        pins, pouts, psem = pctx
        pl.when(_grid_edge(grid, True))(lambda: plan.wait(pins, pouts, *psem))


def _run_plan(plan, *, name):
    def body(*refs):
        _, _, _, (pins, pouts, psem) = _split_refs(refs, 0, 0, 0, plan)
        plan.start(pins, pouts, *psem)
        plan.wait(pins, pouts, *psem)

    p_in, p_ospec, p_oshape, p_scr, p_alias = _plan_io(plan, 0, 0)
    return pl.pallas_call(body, name=name, in_specs=[_ANY] * len(p_in), out_specs=p_ospec, out_shape=p_oshape,
                          scratch_shapes=p_scr, input_output_aliases=p_alias)(*p_in)


def _fold8(x):
    return x.reshape(x.shape[0] // 8, 8, x.shape[1]).sum(axis=0)


def _ln_stats(r):
    mu = jnp.mean(r, -1, keepdims=True)
    xc = r - mu
    rstd = lax.rsqrt(jnp.mean(xc * xc, -1, keepdims=True) + EPS)
    return xc * rstd, rstd


def _sigmoid(x):
    return jax.nn.sigmoid(x)


def _gelu(x):
    return 0.5 * x * (1.0 + jnp.tanh(GELU_C * (x + GELU_A * x * x * x)))


def _gelu_grad(x):
    t = jnp.tanh(GELU_C * (x + GELU_A * x * x * x))
    return 0.5 * (1.0 + t) + 0.5 * x * (1.0 - t * t) * GELU_C * (1.0 + 3.0 * GELU_A * x * x)


MM_ROWS = 1024
DW_TOKENS = 4096


def _matmul(a, b, *, name, M, N, K, ta=False, tb=False, out_dtype=F32, tm=MM_ROWS, tn=1024, tk=1024,
            a_spec=None, b_spec=None, b_merge=None, out_shape=None, o_spec=None, into=None,
            a_sq=False, mul=None, add=None, add_scale=1.0, plan=None):
    tm, tn, tk = min(tm, M), min(tn, N), min(tk, K)
    assert M % tm == 0 and N % tn == 0 and K % tk == 0
    grid = (M // tm, N // tn, K // tk)
    nk = grid[2]
    if a_spec is None:
        a_spec = pl.BlockSpec((tk, tm), lambda i, j, k: (k, i)) if ta else pl.BlockSpec((tm, tk), lambda i, j, k: (i, k))
    if b_spec is None:
        b_spec = pl.BlockSpec((tn, tk), lambda i, j, k: (j, k)) if tb else pl.BlockSpec((tk, tn), lambda i, j, k: (k, j))
    if o_spec is None:
        o_spec = pl.BlockSpec((tm, tn), lambda i, j, k: (i, j))
        out_shape = jax.ShapeDtypeStruct((M, N), out_dtype)
    e_spec = pl.BlockSpec((tm, tn), lambda i, j, k: (i, j))
    dims = (((0 if ta else 1,), (1 if tb else 0,)), ((), ()))
    extra = [e for e in (mul, add, into) if e is not None]
    n_in = 2 + len(extra)

    def body(*refs):
        ins, outs, scr, pctx = _split_refs(refs, n_in, 1, 1 if nk > 1 else 0, plan)
        a_ref, b_ref = ins[0], ins[1]
        rest = list(ins[2:])
        mul_ref = rest.pop(0) if mul is not None else None
        add_ref = rest.pop(0) if add is not None else None
        o_ref = outs[0]
        _plan_start(plan, pctx, grid)
        av = a_ref[...].astype(BF16)
        if a_sq:
            av = av * av
        bv = b_ref[...]
        if b_merge is not None:
            bv = bv.reshape(b_merge)
        if bv.ndim == 3:
            w = av.shape[-1] // (1 if av.ndim == 3 else bv.shape[0])
            a_parts = [av[s] if av.ndim == 3 else av[:, s * w:(s + 1) * w] for s in range(bv.shape[0])]
            p = sum(lax.dot_general(a_parts[s], bv[s], dims, preferred_element_type=F32) for s in range(bv.shape[0]))
        else:
            p = lax.dot_general(av, bv, dims, preferred_element_type=F32)

        def finish(r):
            if mul_ref is not None:
                r = r * (2.0 * mul_ref[...].astype(F32))
            if add_ref is not None:
                r = r + add_scale * add_ref[...]
            o_ref[...] = r.astype(o_ref.dtype)

        if nk == 1:
            finish(p)
        else:
            acc_ref = scr[0]
            k = pl.program_id(2)

            @pl.when(k == 0)
            def _():
                acc_ref[...] = p

            @pl.when(k > 0)
            def _():
                acc_ref[...] += p

            @pl.when(k == nk - 1)
            def _():
                finish(acc_ref[...])

        _plan_wait(plan, pctx, grid)

    p_in, p_ospec, p_oshape, p_scr, p_alias = _plan_io(plan, n_in, 1)
    aliases = dict(p_alias)
    if into is not None:
        aliases[n_in - 1] = 0
    return pl.pallas_call(
        body, name=name, grid=grid,
        in_specs=[a_spec, b_spec] + [e_spec] * (len(extra) - (into is not None)) + [_ANY] * (into is not None)
        + [_ANY] * len(p_in),
        out_specs=[o_spec] + p_ospec, out_shape=[out_shape] + p_oshape,
        scratch_shapes=([pltpu.VMEM((tm, tn), F32)] if nk > 1 else []) + p_scr,
        input_output_aliases=aliases, compiler_params=_params(48, 3),
    )(*_hbm(a, b, *extra), *p_in)


def _rows4_spec(rowblk, n_axes):
    return pl.BlockSpec((4, 256, D), lambda *_: (0, rowblk, 0))


def _proj_ln(a_b, w, h_prev, g, b, *, name, w_rowblk=None, plan=None):
    T = a_b.shape[0]
    tm = min(MM_ROWS, T)
    grid = (T // tm,)
    row = pl.BlockSpec((tm, D), lambda i: (i, 0))
    vec = pl.BlockSpec((1, D), lambda i: (0, 0))
    w_spec = pl.BlockSpec((D, D), lambda i: (0, 0)) if w_rowblk is None else _rows4_spec(w_rowblk, 1)

    def body(*refs):
        (a_ref, w_ref, h_ref, g_ref, b_ref), (r_ref, ho_ref, hb_ref), _, pctx = _split_refs(refs, 5, 3, 0, plan)
        _plan_start(plan, pctx, grid)
        mix = jnp.dot(a_ref[...], w_ref[...].reshape(D, D), preferred_element_type=F32)
        r = ALPHA * h_ref[...] + mix
        xhat, _ = _ln_stats(r)
        y = xhat * g_ref[...] + b_ref[...]
        r_ref[...] = r
        ho_ref[...] = y
        hb_ref[...] = y.astype(BF16)
        _plan_wait(plan, pctx, grid)

    p_in, p_ospec, p_oshape, p_scr, p_alias = _plan_io(plan, 5, 3)
    return pl.pallas_call(
        body, name=name, grid=grid,
        in_specs=[row, w_spec, row, vec, vec] + [_ANY] * len(p_in),
        out_specs=[row, row, row] + p_ospec,
        out_shape=[jax.ShapeDtypeStruct((T, D), F32), jax.ShapeDtypeStruct((T, D), F32),
                   jax.ShapeDtypeStruct((T, D), BF16)] + p_oshape,
        scratch_shapes=p_scr, input_output_aliases=p_alias, compiler_params=_params(40, 1),
    )(*_hbm(a_b, w, h_prev, g, b), *p_in)


def _ffn_ln(h_b, wbuf, h, g, b, *, name, plan=None):
    T = h_b.shape[0]
    slots = 2
    tm, tf = min(ROW_BLOCK, T), slots * 1024
    nf = 4 // slots
    F = nf * tf
    grid = (T // tm, nf)
    row = pl.BlockSpec((tm, D), lambda i, j: (i, 0))
    vec = pl.BlockSpec((1, D), lambda i, j: (0, 0))

    def body(*refs):
        ((hb_ref, w1_ref, w2_ref, h_ref, g_ref, b_ref), (ra_ref, r_ref, ho_ref, hbo_ref), (acc_ref,),
         pctx) = _split_refs(refs, 6, 4, 1, plan)
        _plan_start(plan, pctx, grid)
        j = pl.program_id(1)
        hb = hb_ref[...]
        p = None
        for s in range(slots):
            ra = jnp.maximum(jnp.dot(hb, w1_ref[s], preferred_element_type=F32), 0.0)
            ra_ref[:, s * 1024:(s + 1) * 1024] = ra.astype(BF16)
            ps = jnp.dot((ra * ra).astype(BF16), w2_ref[s], preferred_element_type=F32)
            p = ps if p is None else p + ps

        @pl.when(j == 0)
        def _():
            acc_ref[...] = p

        @pl.when(j > 0)
        def _():
            acc_ref[...] += p

        @pl.when(j == nf - 1)
        def _():
            r = ALPHA * h_ref[...] + acc_ref[...]
            xhat, _ = _ln_stats(r)
            y = xhat * g_ref[...] + b_ref[...]
            r_ref[...] = r
            ho_ref[...] = y
            hbo_ref[...] = y.astype(BF16)

        _plan_wait(plan, pctx, grid)

    p_in, p_ospec, p_oshape, p_scr, p_alias = _plan_io(plan, 6, 4)
    return pl.pallas_call(
        body, name=name, grid=grid,
        in_specs=[row, pl.BlockSpec((slots, D, D), lambda i, j: (j, 0, 0)),
                  pl.BlockSpec((slots, D, D), lambda i, j: (j, 1, 0)), row, vec, vec] + [_ANY] * len(p_in),
        out_specs=[pl.BlockSpec((tm, tf), lambda i, j: (i, j)), row, row, row] + p_ospec,
        out_shape=[jax.ShapeDtypeStruct((T, F), BF16), jax.ShapeDtypeStruct((T, D), F32),
                   jax.ShapeDtypeStruct((T, D), F32), jax.ShapeDtypeStruct((T, D), BF16)] + p_oshape,
        scratch_shapes=[pltpu.VMEM((tm, D), F32)] + p_scr,
        input_output_aliases=p_alias, compiler_params=_params(56, 2),
    )(*_hbm(h_b, wbuf, wbuf, h, g, b), *p_in)


def _ln_bwd(dy, r, g, *, name, tgt=None):
    T = dy.shape[0]
    tm = min(ROW_BLOCK, T)
    row = pl.BlockSpec((tm, D), lambda i: (i, 0))
    acc = pl.BlockSpec((8, D), lambda i: (0, 0))
    n_in = 3 + (tgt is not None)

    def body(*refs):
        dy_ref, r_ref, g_ref = refs[:3]
        dr_ref, drb_ref, dg_ref, db_ref = refs[n_in:n_in + 4]

        @pl.when(pl.program_id(0) == 0)
        def _():
            for ref in refs[n_in + 2:]:
                ref[...] = jnp.zeros_like(ref)

        dy_ = dy_ref[...]
        if tgt is not None:
            err = dy_ - refs[3][...]
            refs[n_in + 4][...] += _fold8(err * err)
            dy_ = err * (1.0 / D)
        xhat, rstd = _ln_stats(r_ref[...])
        dxh = dy_ * g_ref[...]
        m1 = jnp.mean(dxh, -1, keepdims=True)
        m2 = jnp.mean(dxh * xhat, -1, keepdims=True)
        dr = rstd * (dxh - m1 - xhat * m2)
        dr_ref[...] = dr
        drb_ref[...] = dr.astype(BF16)
        dg_ref[...] += _fold8(dy_ * xhat)
        db_ref[...] += _fold8(dy_)

    extra = [] if tgt is None else [tgt]
    return pl.pallas_call(
        body, name=name, grid=(T // tm,),
        in_specs=[row, row, pl.BlockSpec((1, D), lambda i: (0, 0))] + [row] * len(extra),
        out_specs=[row, row, acc, acc] + [acc] * len(extra),
        out_shape=[jax.ShapeDtypeStruct((T, D), F32), jax.ShapeDtypeStruct((T, D), BF16)]
        + [jax.ShapeDtypeStruct((8, D), F32)] * (2 + len(extra)),
        compiler_params=_params(40, 1),
    )(*_hbm(dy, r, g, *extra))


def _rope(x, c, s1, s2):
    return x * c + pltpu.roll(x, 112, 1) * s1 + pltpu.roll(x, 16, 1) * s2


def _rope_t(dy, c, s1, s2):
    return dy * c + pltpu.roll(dy * s1, 16, 1) + pltpu.roll(dy * s2, 112, 1)


def _rms(x, g):
    rstd = lax.rsqrt(jnp.mean(x * x, -1, keepdims=True) + EPS)
    xhat = x * rstd
    return xhat * g, xhat, rstd


def _mla_prep(z0, gq, gkv, wq, wk, wv, rc, rs1, rs2):
    T = z0.shape[0]
    tm = min(ROW_BLOCK, T)
    HW = HEADS * 128

    def body(cq_ref, ckv_ref, kr_ref, gq_ref, gkv_ref, wq_ref, wk_ref, wv_ref, c_ref, s1_ref, s2_ref,
             q_ref, k_ref, v_ref):
        nq = _rms(cq_ref[...], gq_ref[...])[0].astype(BF16)
        nkv = _rms(ckv_ref[...], gkv_ref[...])[0].astype(BF16)
        q = jnp.dot(nq, wq_ref[...], preferred_element_type=F32)
        k = jnp.dot(nkv, wk_ref[...], preferred_element_type=F32)
        v = jnp.dot(nkv, wv_ref[...], preferred_element_type=F32)
        c, s1, s2 = c_ref[...], s1_ref[...], s2_ref[...]
        kr = _rope(pltpu.roll(kr_ref[...], 64, 1), c, s1, s2)
        for h in range(HEADS):
            sl = slice(h * 128, (h + 1) * 128)
            q_ref[:, sl] = (_rope(q[:, sl], c, s1, s2) * QK_SCALE).astype(BF16)
            k_ref[:, sl] = (k[:, sl] + kr).astype(BF16)
        v_ref[...] = v.astype(BF16)

    full = lambda shape: pl.BlockSpec(shape, lambda i: (0, 0))
    tab = pl.BlockSpec((tm, 128), lambda i: (i, 0))
    return pl.pallas_call(
        body, name="mla_prep", grid=(T // tm,),
        in_specs=[pl.BlockSpec((tm, 256), lambda i: (i, 0)), pl.BlockSpec((tm, 256), lambda i: (i, 1)),
                  pl.BlockSpec((tm, 128), lambda i: (i, 12)), full((1, 256)), full((1, 256)),
                  full((256, HW)), full((256, HW)), full((256, 512)), tab, tab, tab],
        out_specs=[pl.BlockSpec((tm, HW), lambda i: (i, 0)), pl.BlockSpec((tm, HW), lambda i: (i, 0)),
                   pl.BlockSpec((tm, 512), lambda i: (i, 0))],
        out_shape=[jax.ShapeDtypeStruct((T, HW), BF16), jax.ShapeDtypeStruct((T, HW), BF16),
                   jax.ShapeDtypeStruct((T, 512), BF16)],
        compiler_params=_params(40, 1),
    )(z0, z0, z0, gq, gkv, wq, wk, wv, rc, rs1, rs2)


def _flash_fwd(q, k, v, plan=None):
    T = q.shape[0]
    bq = min(2 * ROW_BLOCK, T)
    nq = T // bq
    grid = (4, nq, nq)

    def body(*refs):
        (q_ref, k_ref, v_ref), (o_ref, lse_ref), (m_sc, acc_sc), pctx = _split_refs(refs, 3, 2, 2, plan)
        _plan_start(plan, pctx, grid)
        i, j = pl.program_id(1), pl.program_id(2)
        first = lax.broadcasted_iota(jnp.int32, (bq, 128), 1) < 64

        @pl.when(j == 0)
        def _():
            m_sc[...] = jnp.full_like(m_sc, -jnp.inf)
            acc_sc[...] = jnp.zeros_like(acc_sc)

        def tile(r0, nr, nc, masked):
            rs = slice(r0, r0 + nr)
            vp = v_ref[0:nc, :]
            lanes = first[0:nc, :]
            for h in range(2):
                sl = slice(h * 128, (h + 1) * 128)
                s = lax.dot_general(q_ref[rs, sl], k_ref[0:nc, sl], NT_DIMS, preferred_element_type=F32)
                if masked:
                    rows = r0 + lax.broadcasted_iota(jnp.int32, (nr, nc), 0)
                    cols = lax.broadcasted_iota(jnp.int32, (nr, nc), 1)
                    s = jnp.where(cols <= rows, s, -jnp.inf)
                m_prev = m_sc[h, rs, 0:1]
                m_new = jnp.maximum(m_prev, jnp.max(s, axis=1, keepdims=True))
                alpha = jnp.exp(m_prev - m_new)
                p = jnp.exp(s - m_new).astype(BF16)
                vh = jnp.where(lanes if h == 0 else jnp.logical_not(lanes), vp, jnp.ones_like(vp))
                acc_sc[h, rs, :] = acc_sc[h, rs, :] * alpha + jnp.dot(p, vh, preferred_element_type=F32)
                m_sc[h, rs, :] = jnp.broadcast_to(m_new, (nr, 128))

        @pl.when(j < i)
        def _():
            tile(0, bq, bq, False)

        @pl.when(j == i)
        def _():
            tile(0, bq, bq, True)
            a0, a1 = acc_sc[0], acc_sc[1]
            l0, l1 = pltpu.roll(a0, 64, 1), pltpu.roll(a1, 64, 1)
            o_ref[...] = jnp.where(first, a0 / l0, a1 / l1).astype(BF16)
            lse_ref[...] = jnp.where(first, m_sc[0] + jnp.log(l0), m_sc[1] + jnp.log(l1))

        _plan_wait(plan, pctx, grid)

    kv = lambda hp, i, j: (jnp.minimum(i, j), hp)
    p_in, p_ospec, p_oshape, p_scr, p_alias = _plan_io(plan, 3, 2)
    return pl.pallas_call(
        body, name="flash_fwd", grid=grid,
        in_specs=[pl.BlockSpec((bq, 256), lambda hp, i, j: (i, hp)), pl.BlockSpec((bq, 256), kv),
                  pl.BlockSpec((bq, 128), kv)] + [_ANY] * len(p_in),
        out_specs=[pl.BlockSpec((bq, 128), lambda hp, i, j: (i, hp)),
                   pl.BlockSpec((bq, 128), lambda hp, i, j: (i, hp))] + p_ospec,
        out_shape=[jax.ShapeDtypeStruct((T, 512), BF16), jax.ShapeDtypeStruct((T, 512), F32)] + p_oshape,
        scratch_shapes=[pltpu.VMEM((2, bq, 128), F32), pltpu.VMEM((2, bq, 128), F32)] + p_scr,
        input_output_aliases=p_alias, compiler_params=_params(56, 3),
    )(*_hbm(q, k, v), *p_in)


def _attn_delta(dmix, o):
    T = o.shape[0]
    tm = min(ROW_BLOCK, T)
    blk = pl.BlockSpec((tm, 512), lambda i: (i, 0))

    def body(do_ref, o_ref, delta_ref, dob_ref):
        first = lax.broadcasted_iota(jnp.int32, (tm, 128), 1) < 64
        for hp in range(4):
            sl = slice(hp * 128, (hp + 1) * 128)
            prod = do_ref[:, sl] * o_ref[:, sl].astype(F32)
            d0 = jnp.sum(jnp.where(first, prod, 0.0), axis=1, keepdims=True)
            d1 = jnp.sum(jnp.where(first, 0.0, prod), axis=1, keepdims=True)
            delta_ref[:, sl] = jnp.where(first, d0, d1)
        dob_ref[...] = do_ref[...].astype(BF16)

    return pl.pallas_call(
        body, name="attn_delta", grid=(T // tm,), in_specs=[blk, blk], out_specs=[blk, blk],
        out_shape=[jax.ShapeDtypeStruct((T, 512), F32), jax.ShapeDtypeStruct((T, 512), BF16)],
        compiler_params=_params(32, 1),
    )(dmix, o)


def _flash_bwd(q, k, v, do_b, lse, delta, plan=None):
    T = q.shape[0]
    bq = min(2 * ROW_BLOCK, T)
    nq = T // bq
    grid = (4, nq, nq)

    def body(*refs):
        ((q_ref, k_ref, v_ref, do_ref, lse_ref, dl_ref), (dq_hbm, dk_ref, dv_ref), (dq_sc, dk_sc, dv_sc, sem),
         pctx) = _split_refs(refs, 6, 3, 4, plan)
        _plan_start(plan, pctx, grid)
        hp, j, i = pl.program_id(0), pl.program_id(1), pl.program_id(2)
        first = lax.broadcasted_iota(jnp.int32, (bq, 128), 1) < 64

        @pl.when((j == 0) & (i == 0))
        def _():
            dq_sc[...] = jnp.zeros_like(dq_sc)

        @pl.when(i == j)
        def _():
            dk_sc[...] = jnp.zeros_like(dk_sc)
            dv_sc[...] = jnp.zeros_like(dv_sc)

        def tile(r0, nr, nc, masked):
            rs, cs = slice(r0, r0 + nr), slice(0, nc)
            vp = v_ref[cs, :]
            do = do_ref[rs, :]
            lanes = first[rs, :]
            for h in range(2):
                sl = slice(h * 128, (h + 1) * 128)
                qh, kh = q_ref[rs, sl], k_ref[cs, sl]
                s = lax.dot_general(qh, kh, NT_DIMS, preferred_element_type=F32)
                p = jnp.exp(s - lse_ref[rs, h * 64:h * 64 + 1])
                if masked:
                    rows = r0 + lax.broadcasted_iota(jnp.int32, (nr, nc), 0)
                    cols = lax.broadcasted_iota(jnp.int32, (nr, nc), 1)
                    p = jnp.where(cols <= rows, p, 0.0)
                do_h = jnp.where(lanes if h == 0 else jnp.logical_not(lanes), do, jnp.zeros_like(do))
                dv_sc[cs, :] += lax.dot_general(p.astype(BF16), do_h, TN_DIMS, preferred_element_type=F32)
                dp = lax.dot_general(do_h, vp, NT_DIMS, preferred_element_type=F32)
                ds = (p * (dp - dl_ref[rs, h * 64:h * 64 + 1])).astype(BF16)
                dq_sc[i, rs, sl] += jnp.dot(ds, kh, preferred_element_type=F32)
                dk_sc[cs, sl] += lax.dot_general(ds, qh, TN_DIMS, preferred_element_type=F32)

        @pl.when(i > j)
        def _():
            tile(0, bq, bq, False)

        @pl.when(i == j)
        def _():
            tile(0, bq // 2, bq // 2, True)
            tile(bq // 2, bq // 2, bq, True)

        @pl.when(i == nq - 1)
        def _():
            dk_ref[...] = dk_sc[...]
            dv_ref[...] = dv_sc[...]

        @pl.when((j == nq - 1) & (i == nq - 1))
        def _():
            cp = pltpu.make_async_copy(dq_sc, dq_hbm.at[hp], sem)
            cp.start()
            cp.wait()

        _plan_wait(plan, pctx, grid)

    qi = lambda hp, j, i: (jnp.maximum(i, j), hp)
    kj = lambda hp, j, i: (j, hp)
    p_in, p_ospec, p_oshape, p_scr, p_alias = _plan_io(plan, 6, 3)
    return pl.pallas_call(
        body, name="flash_bwd", grid=grid,
        in_specs=[pl.BlockSpec((bq, 256), qi), pl.BlockSpec((bq, 256), kj), pl.BlockSpec((bq, 128), kj),
                  pl.BlockSpec((bq, 128), qi), pl.BlockSpec((bq, 128), qi), pl.BlockSpec((bq, 128), qi)]
        + [_ANY] * len(p_in),
        out_specs=[_ANY, pl.BlockSpec((bq, 256), kj), pl.BlockSpec((bq, 128), kj)] + p_ospec,
        out_shape=[jax.ShapeDtypeStruct((4, nq, bq, 256), F32), jax.ShapeDtypeStruct((T, 1024), F32),
                   jax.ShapeDtypeStruct((T, 512), F32)] + p_oshape,
        scratch_shapes=[pltpu.VMEM((nq, bq, 256), F32), pltpu.VMEM((bq, 256), F32), pltpu.VMEM((bq, 128), F32),
                        pltpu.SemaphoreType.DMA] + p_scr,
        input_output_aliases=p_alias, compiler_params=_params(56, 3),
    )(*_hbm(q, k, v, do_b, lse, delta), *p_in)


def _mla_bwd(z0, dq4, dk, dv, gq, gkv, wq, wk, wv, rc, rs1, rs2, plan=None):
    T = z0.shape[0]
    tm = min(ROW_BLOCK, T)
    HW = HEADS * 128
    grid = (T // tm,)
    dq4 = dq4.reshape(4, T, 256)

    def body(*refs):
        ((cq_ref, ckv_ref, dq_ref, dk_ref, dv_ref, gq_ref, gkv_ref, wq_ref, wk_ref, wv_ref, c_ref, s1_ref, s2_ref),
         (dc_ref, dkr_ref, dwq_ref, dwk_ref, dwv_ref, dgq_ref, dgkv_ref), _, pctx) = _split_refs(refs, 13, 7, 0, plan)
        _plan_start(plan, pctx, grid)

        @pl.when(pl.program_id(0) == 0)
        def _():
            for ref in (dwq_ref, dwk_ref, dwv_ref, dgq_ref, dgkv_ref):
                ref[...] = jnp.zeros_like(ref)

        c, s1, s2 = c_ref[...], s1_ref[...], s2_ref[...]
        lane = lax.broadcasted_iota(jnp.int32, (tm, 128), 1)
        nq, xq, rq = _rms(cq_ref[...], gq_ref[...])
        nkv, xkv, rkv = _rms(ckv_ref[...], gkv_ref[...])
        nq_b, nkv_b = nq.astype(BF16), nkv.astype(BF16)

        dq_parts, dk_parts = [], []
        dkr = jnp.zeros((tm, 128), F32)
        for h in range(HEADS):
            blk = dq_ref[h // 2, :, (h % 2) * 128:(h % 2 + 1) * 128] * QK_SCALE
            dq_parts.append(_rope_t(blk, c, s1, s2).astype(BF16))
            kb = dk_ref[:, h * 128:(h + 1) * 128]
            dk_parts.append(jnp.where(lane < NOPE, kb, 0.0).astype(BF16))
            dkr = dkr + kb
        dq_b = jnp.concatenate(dq_parts, axis=1)
        dk_b = jnp.concatenate(dk_parts, axis=1)
        dv_b = dv_ref[...].astype(BF16)

        dwq_ref[...] += lax.dot_general(nq_b, dq_b, TN_DIMS, preferred_element_type=F32)
        dwk_ref[...] += lax.dot_general(nkv_b, dk_b, TN_DIMS, preferred_element_type=F32)
        dwv_ref[...] += lax.dot_general(nkv_b, dv_b, TN_DIMS, preferred_element_type=F32)
        dnq = lax.dot_general(dq_b, wq_ref[...], NT_DIMS, preferred_element_type=F32)
        dnkv = (lax.dot_general(dk_b, wk_ref[...], NT_DIMS, preferred_element_type=F32)
                + lax.dot_general(dv_b, wv_ref[...], NT_DIMS, preferred_element_type=F32))

        def rms_bwd(dn, xhat, rstd, g):
            dxh = dn * g
            return rstd * (dxh - xhat * jnp.mean(dxh * xhat, -1, keepdims=True))

        dc_ref[:, :256] = rms_bwd(dnq, xq, rq, gq_ref[...]).astype(BF16)
        dc_ref[:, 256:] = rms_bwd(dnkv, xkv, rkv, gkv_ref[...]).astype(BF16)
        dgq_ref[...] += _fold8(dnq * xq)
        dgkv_ref[...] += _fold8(dnkv * xkv)
        dkr = pltpu.roll(_rope_t(dkr, c, s1, s2), 64, 1)
        dkr_ref[...] = jnp.where(lane < ROPE, dkr, 0.0).astype(BF16)
        _plan_wait(plan, pctx, grid)

    full = lambda shape: pl.BlockSpec(shape, lambda i: (0,) * len(shape))
    tab = pl.BlockSpec((tm, 128), lambda i: (i, 0))
    p_in, p_ospec, p_oshape, p_scr, p_alias = _plan_io(plan, 13, 7)
    return pl.pallas_call(
        body, name="mla_bwd", grid=grid,
        in_specs=[pl.BlockSpec((tm, 256), lambda i: (i, 0)), pl.BlockSpec((tm, 256), lambda i: (i, 1)),
                  pl.BlockSpec((4, tm, 256), lambda i: (0, i, 0)),
                  pl.BlockSpec((tm, HW), lambda i: (i, 0)), pl.BlockSpec((tm, 512), lambda i: (i, 0)),
                  full((1, 256)), full((1, 256)), full((256, HW)), full((256, HW)), full((256, 512)), tab, tab, tab]
        + [_ANY] * len(p_in),
        out_specs=[pl.BlockSpec((tm, 512), lambda i: (i, 0)), tab, full((256, HW)), full((256, HW)),
                   full((256, 512)), full((8, 256)), full((8, 256))] + p_ospec,
        out_shape=[jax.ShapeDtypeStruct((T, 512), BF16), jax.ShapeDtypeStruct((T, 128), BF16),
                   jax.ShapeDtypeStruct((256, HW), F32), jax.ShapeDtypeStruct((256, HW), F32),
                   jax.ShapeDtypeStruct((256, 512), F32), jax.ShapeDtypeStruct((8, 256), F32),
                   jax.ShapeDtypeStruct((8, 256), F32)] + p_oshape,
        scratch_shapes=p_scr, input_output_aliases=p_alias, compiler_params=_params(48, 1),
    )(*_hbm(z0, z0, dq4, dk, dv, gq, gkv, wq, wk, wv, rc, rs1, rs2), *p_in)


def _sgu_fwd(z0, a_out, ln_g, ln_b, w, b_t):
    T = z0.shape[0]
    tm = min(ROW_BLOCK, T)
    W = SGU_G * SGU_C

    def body(u_ref, v_ref, a_ref, g_ref, b_ref, w_ref, bt_ref, o_ref):
        o_ref[:, :W] = a_ref[...]
        ug = _gelu(u_ref[...])
        xhat, _ = _ln_stats(_gelu(v_ref[...]))
        vn = (xhat * g_ref[...] + b_ref[...]).astype(BF16)
        tril = lax.broadcasted_iota(jnp.int32, (SGU_C, SGU_C), 0) >= lax.broadcasted_iota(jnp.int32, (SGU_C, SGU_C), 1)
        for g in range(SGU_G):
            cs = slice(g * SGU_C, (g + 1) * SGU_C)
            wg = jnp.where(tril, w_ref[g], 0.0).astype(BF16)
            bcol = bt_ref[:, g:g + 1]
            for c in range(tm // SGU_C):
                rs = slice(c * SGU_C, (c + 1) * SGU_C)
                mixed = jnp.dot(wg, vn[rs, cs], preferred_element_type=F32) + bcol
                o_ref[rs, W + g * SGU_C:W + (g + 1) * SGU_C] = (ug[rs, cs] * mixed).astype(BF16)

    full = lambda shape: pl.BlockSpec(shape, lambda i: (0,) * len(shape))
    return pl.pallas_call(
        body, name="sgu_fwd", grid=(T // tm,),
        in_specs=[pl.BlockSpec((tm, W), lambda i: (i, 1)), pl.BlockSpec((tm, W), lambda i: (i, 2)),
                  pl.BlockSpec((tm, W), lambda i: (i, 0)),
                  full((1, W)), full((1, W)), full((SGU_G, SGU_C, SGU_C)), full((SGU_C, SGU_G))],
        out_specs=pl.BlockSpec((tm, 2 * W), lambda i: (i, 0)),
        out_shape=jax.ShapeDtypeStruct((T, 2 * W), BF16),
        compiler_params=_params(32, 1),
    )(z0, z0, a_out, ln_g, ln_b, w, b_t)


def _sgu_bwd(z0, dmix, dc, dkr, ln_g, ln_b, w, b_t):
    T = z0.shape[0]
    tm = min(ROW_BLOCK, T)
    W = SGU_G * SGU_C

    def body(u_ref, v_ref, do_ref, dc_ref, dkr_ref, g_ref, b_ref, w_ref, bt_ref, dz_ref, dw_ref, db_ref, dlg_ref,
             dlb_ref):
        @pl.when(pl.program_id(0) == 0)
        def _():
            for ref in (dw_ref, db_ref, dlg_ref, dlb_ref):
                ref[...] = jnp.zeros_like(ref)

        dz_ref[:, :W] = dc_ref[...]
        dz_ref[:, 3 * W:] = dkr_ref[...]

        u, v, dout = u_ref[...], v_ref[...], do_ref[...]
        ug = _gelu(u)
        xhat, rstd = _ln_stats(_gelu(v))
        vn = (xhat * g_ref[...] + b_ref[...]).astype(BF16)
        dmixed = dout * ug
        dmixed_b = dmixed.astype(BF16)
        tril = lax.broadcasted_iota(jnp.int32, (SGU_C, SGU_C), 0) >= lax.broadcasted_iota(jnp.int32, (SGU_C, SGU_C), 1)
        lane = lax.broadcasted_iota(jnp.int32, (SGU_C, SGU_C), 1)
        dvn_cols = []
        for g in range(SGU_G):
            cs = slice(g * SGU_C, (g + 1) * SGU_C)
            wg = jnp.where(tril, w_ref[g], 0.0).astype(BF16)
            bcol = bt_ref[:, g:g + 1]
            dw_g = jnp.zeros((SGU_C, SGU_C), F32)
            db_g = jnp.zeros((SGU_C, 1), F32)
            dvn_rows = []
            for c in range(tm // SGU_C):
                rs = slice(c * SGU_C, (c + 1) * SGU_C)
                mixed = jnp.dot(wg, vn[rs, cs], preferred_element_type=F32) + bcol
                dz_ref[rs, W + g * SGU_C:W + (g + 1) * SGU_C] = (dout[rs, cs] * mixed * _gelu_grad(u[rs, cs])).astype(BF16)
                dm = dmixed_b[rs, cs]
                dw_g = dw_g + lax.dot_general(dm, vn[rs, cs], NT_DIMS, preferred_element_type=F32)
                db_g = db_g + jnp.sum(dmixed[rs, cs], axis=1, keepdims=True)
                dvn_rows.append(lax.dot_general(wg, dm, TN_DIMS, preferred_element_type=F32))
            dw_ref[g] += jnp.where(tril, dw_g, 0.0)
            db_ref[...] += jnp.where(lane == g, db_g, 0.0)
            dvn_cols.append(jnp.concatenate(dvn_rows, axis=0))
        dvn = jnp.concatenate(dvn_cols, axis=1)
        dxh = dvn * g_ref[...]
        m1 = jnp.mean(dxh, -1, keepdims=True)
        m2 = jnp.mean(dxh * xhat, -1, keepdims=True)
        dvg = rstd * (dxh - m1 - xhat * m2)
        dz_ref[:, 2 * W:3 * W] = (dvg * _gelu_grad(v)).astype(BF16)
        dlg_ref[...] += _fold8(dvn * xhat)
        dlb_ref[...] += _fold8(dvn)

    full = lambda shape: pl.BlockSpec(shape, lambda i: (0,) * len(shape))
    return pl.pallas_call(
        body, name="sgu_bwd", grid=(T // tm,),
        in_specs=[pl.BlockSpec((tm, W), lambda i: (i, 1)), pl.BlockSpec((tm, W), lambda i: (i, 2)),
                  pl.BlockSpec((tm, W), lambda i: (i, 1)), pl.BlockSpec((tm, W), lambda i: (i, 0)),
                  pl.BlockSpec((tm, 128), lambda i: (i, 0)),
                  full((1, W)), full((1, W)), full((SGU_G, SGU_C, SGU_C)), full((SGU_C, SGU_G))],
        out_specs=[pl.BlockSpec((tm, 3 * W + 128), lambda i: (i, 0)), full((SGU_G, SGU_C, SGU_C)),
                   full((SGU_C, SGU_C)), full((8, W)), full((8, W))],
        out_shape=[jax.ShapeDtypeStruct((T, 3 * W + 128), BF16), jax.ShapeDtypeStruct((SGU_G, SGU_C, SGU_C), F32),
                   jax.ShapeDtypeStruct((SGU_C, SGU_C), F32), jax.ShapeDtypeStruct((8, W), F32),
                   jax.ShapeDtypeStruct((8, W), F32)],
        compiler_params=_params(40, 1),
    )(z0, z0, dmix, dc, dkr, ln_g, ln_b, w, b_t)


def _hg_lower_bound(lb_ref):
    a0, a1 = lb_ref[0:1, :], lb_ref[1:2, :]
    m = jnp.maximum(a0, a1)
    e0, e1 = jnp.exp(a0 - m), jnp.exp(a1 - m)
    return e1 / (e0 + e1)


def _running_sum(x, reverse=False):
    n = x.shape[0]
    row = lax.broadcasted_iota(jnp.int32, x.shape, 0)
    s = 1
    while s < n:
        if reverse:
            x = x + jnp.where(row < n - s, pltpu.roll(x, n - s, 0), 0.0)
        else:
            x = x + jnp.where(row >= s, pltpu.roll(x, s, 0), 0.0)
        s *= 2
    return x


def _hg_chunk(qc, fc, lb):
    C = HG_CHUNK
    rows = lax.broadcasted_iota(jnp.int32, (C, C), 0)
    cols = lax.broadcasted_iota(jnp.int32, (C, C), 1)
    rowid = lax.broadcasted_iota(jnp.int32, (C, 128), 0)
    sq, sg = _sigmoid(qc), _sigmoid(fc)
    qf = qc * sq
    gate = lb + (1.0 - lb) * sg
    kk = 1.0 - gate
    lg = jnp.log(gate)
    bcum = _running_sum(lg)
    b_mid = jnp.sum(jnp.where(rowid < C // 2, lg, 0.0), axis=0, keepdims=True)
    b_last = jnp.sum(lg, axis=0, keepdims=True)
    eq, ek, e, eh = jnp.exp(bcum - b_mid), jnp.exp(b_mid - bcum), jnp.exp(bcum), jnp.exp(b_last - bcum)
    qt, kt, qe, khat = qf * eq, kk * ek, qf * e, kk * eh
    a = lax.dot_general(qt.astype(BF16), kt.astype(BF16), NT_DIMS, preferred_element_type=F32)
    a = jnp.where(rows >= cols, a, 0.0)
    return dict(sq=sq, sg=sg, gate=gate, kk=kk, eq=eq, ek=ek, e=e, eh=eh, qt=qt, kt=kt, qe=qe, khat=khat, a=a,
                e_last=jnp.exp(b_last), tril=rows >= cols, rowid=rowid)


def _hgrn_fwd(z4, hg_lb, gnorm):
    T = z4.shape[1]
    tb = min(ROW_BLOCK, T)
    C = HG_CHUNK
    ncb = tb // C
    HPB = HG_HEADS_PER_STEP

    def body(q_ref, f_ref, i_ref, g_ref, lb_ref, gn_ref, y_ref, o_ref, st_ref, st_sc):
        @pl.when(pl.program_id(1) == 0)
        def _():
            st_sc[...] = jnp.zeros_like(st_sc)

        def chunk(c, carry):
            rs = pl.ds(pl.multiple_of(c * C, C), C)
            for hh in range(HPB):
                hs = slice(hh * 128, (hh + 1) * 128)
                lb = _hg_lower_bound(lb_ref.at[:, hs])
                v_b = i_ref[rs, hs].astype(BF16)
                gc = g_ref[rs, hs]
                x = _hg_chunk(q_ref[rs, hs], f_ref[rs, hs], lb)
                st = st_sc[hh]
                st_ref[hh, c] = st
                o = (jnp.dot(x["a"].astype(BF16), v_b, preferred_element_type=F32)
                     + lax.dot_general(x["qe"].astype(BF16), st.astype(BF16), NT_DIMS, preferred_element_type=F32))
                st_sc[hh] = st * x["e_last"] + lax.dot_general(v_b, x["khat"].astype(BF16), TN_DIMS,
                                                               preferred_element_type=F32)
                o_ref[rs, hs] = o
                n = o * lax.rsqrt(jnp.mean(o * o, -1, keepdims=True) + EPS)
                y_ref[rs, hs] = (n * gn_ref[:, hs] * (gc * _sigmoid(gc))).astype(BF16)
            return carry

        lax.fori_loop(0, ncb, chunk, 0)

    W = 128 * HPB
    zb = lambda k: pl.BlockSpec((None, tb, W), lambda h, t: (k, t, h))
    out = pl.BlockSpec((tb, W), lambda h, t: (t, h))
    return pl.pallas_call(
        body, name="hgrn_fwd", grid=(HEADS // HPB, T // tb),
        in_specs=[zb(0), zb(1), zb(2), zb(3), pl.BlockSpec((2, W), lambda h, t: (0, h)),
                  pl.BlockSpec((1, W), lambda h, t: (0, h))],
        out_specs=[out, out, pl.BlockSpec((HPB, ncb, 128, 128), lambda h, t: (h, t, 0, 0))],
        out_shape=[jax.ShapeDtypeStruct((T, D), BF16), jax.ShapeDtypeStruct((T, D), F32),
                   jax.ShapeDtypeStruct((HEADS, T // C, 128, 128), F32)],
        scratch_shapes=[pltpu.VMEM((HPB, 128, 128), F32)],
        compiler_params=_params(48, 2),
    )(*_hbm(z4, z4, z4, z4, hg_lb, gnorm))


def _hgrn_bwd(z4, o_raw, dy, states, hg_lb, gnorm):
    T = z4.shape[1]
    tb = min(ROW_BLOCK, T)
    C = HG_CHUNK
    ncb = tb // C
    nt = T // tb
    HPB = HG_HEADS_PER_STEP

    def body(q_ref, f_ref, i_ref, g_ref, o_ref, dy_ref, st_ref, lb_ref, gn_ref, dz_ref, dlb_ref, dgn_ref, dst_sc):
        @pl.when(pl.program_id(1) == 0)
        def _():
            dst_sc[...] = jnp.zeros_like(dst_sc)
            dlb_ref[...] = jnp.zeros_like(dlb_ref)
            dgn_ref[...] = jnp.zeros_like(dgn_ref)

        def chunk(cc, carry):
            for hh in range(HPB):
                one_head(ncb - 1 - cc, hh, slice(hh * 128, (hh + 1) * 128))
            return carry

        def one_head(c, hh, hs):
            rs = pl.ds(pl.multiple_of(c * C, C), C)
            lb = _hg_lower_bound(lb_ref.at[:, hs])
            gn = gn_ref[:, hs]
            qc, gc = q_ref[rs, hs], g_ref[rs, hs]
            v_b = i_ref[rs, hs].astype(BF16)
            x = _hg_chunk(qc, f_ref[rs, hs], lb)
            st, dst = st_ref[hh, c], dst_sc[hh]
            st_b, dst_b = st.astype(BF16), dst.astype(BF16)
            o, dyc = o_ref[rs, hs], dy_ref[rs, hs]
            sgg = _sigmoid(gc)
            sil = gc * sgg
            rstd = lax.rsqrt(jnp.mean(o * o, -1, keepdims=True) + EPS)
            n = o * rstd
            dgn_ref[:, hs] += _fold8(dyc * n * sil)
            dn = dyc * gn * sil
            do = rstd * (dn - n * jnp.mean(dn * n, -1, keepdims=True))
            dg = dyc * n * gn * (sgg * (1.0 + gc * (1.0 - sgg)))
            do_b = do.astype(BF16)
            da = jnp.where(x["tril"], lax.dot_general(do_b, v_b, NT_DIMS, preferred_element_type=F32), 0.0).astype(BF16)
            qt_b, kt_b, qe_b, khat_b = (x[n_].astype(BF16) for n_ in ("qt", "kt", "qe", "khat"))
            dv = (lax.dot_general(x["a"].astype(BF16), do_b, TN_DIMS, preferred_element_type=F32)
                  + lax.dot_general(khat_b, dst_b, NT_DIMS, preferred_element_type=F32))
            dqt = jnp.dot(da, kt_b, preferred_element_type=F32)
            dqe = jnp.dot(do_b, st_b, preferred_element_type=F32)
            dkt = lax.dot_general(da, qt_b, TN_DIMS, preferred_element_type=F32)
            dkhat = jnp.dot(v_b, dst_b, preferred_element_type=F32)
            dst_sc[hh] = lax.dot_general(do_b, qe_b, TN_DIMS, preferred_element_type=F32) + dst * x["e_last"]
            de_last = jnp.sum(st * dst, axis=0, keepdims=True)
            dqf = dqt * x["eq"] + dqe * x["e"]
            dkk = dkt * x["ek"] + dkhat * x["eh"]
            dkh_kh = dkhat * x["khat"]
            db = dqt * qt_b.astype(F32) - dkt * kt_b.astype(F32) + dqe * x["qe"] - dkh_kh
            db_last = jnp.sum(dkh_kh, axis=0, keepdims=True) + de_last * x["e_last"]
            db = db + jnp.where(x["rowid"] == C - 1, db_last, 0.0)
            dlg = _running_sum(db, reverse=True)
            dgate = dlg / x["gate"] - dkk
            sg, sq = x["sg"], x["sq"]
            dlb_ref[:, hs] += _fold8(dgate * (1.0 - sg)) * (lb * (1.0 - lb))
            dz_ref[0, rs, hs] = (dqf * (sq * (1.0 + qc * (1.0 - sq)))).astype(BF16)
            dz_ref[1, rs, hs] = (dgate * (1.0 - lb) * sg * (1.0 - sg)).astype(BF16)
            dz_ref[2, rs, hs] = dv.astype(BF16)
            dz_ref[3, rs, hs] = dg.astype(BF16)

        lax.fori_loop(0, ncb, chunk, 0)

    W = 128 * HPB
    zb = lambda k: pl.BlockSpec((None, tb, W), lambda h, t: (k, nt - 1 - t, h))
    blk = pl.BlockSpec((tb, W), lambda h, t: (nt - 1 - t, h))
    acc = pl.BlockSpec((8, W), lambda h, t: (0, h))
    return pl.pallas_call(
        body, name="hgrn_bwd", grid=(HEADS // HPB, nt),
        in_specs=[zb(0), zb(1), zb(2), zb(3), blk, blk,
                  pl.BlockSpec((HPB, ncb, 128, 128), lambda h, t: (h, nt - 1 - t, 0, 0)),
                  pl.BlockSpec((2, W), lambda h, t: (0, h)), pl.BlockSpec((1, W), lambda h, t: (0, h))],
        out_specs=[pl.BlockSpec((4, tb, W), lambda h, t: (0, nt - 1 - t, h)), acc, acc],
        out_shape=[jax.ShapeDtypeStruct((4, T, D), BF16), jax.ShapeDtypeStruct((8, D), F32),
                   jax.ShapeDtypeStruct((8, D), F32)],
        scratch_shapes=[pltpu.VMEM((HPB, 128, 128), F32)],
        compiler_params=_params(48, 2),
    )(*_hbm(z4, z4, z4, z4, o_raw, dy, states, hg_lb, gnorm))


def _adamw(w, g, m, v, *, name):
    R, L = w.shape
    tr = R if R <= 512 else 512
    assert R % tr == 0
    blk = pl.BlockSpec((tr, L), lambda i: (i, 0))
    c1, c2 = 1.0 - B1 ** STEP, 1.0 - B2 ** STEP

    def body(w_ref, g_ref, m_ref, v_ref, d_ref, mo_ref, vo_ref):
        g_ = g_ref[...]
        m_ = B1 * m_ref[...] + (1.0 - B1) * g_
        v_ = B2 * v_ref[...] + (1.0 - B2) * (g_ * g_)
        d_ref[...] = -LR * ((m_ / c1) / (jnp.sqrt(v_ / c2) + ADAM_EPS) + WD * w_ref[...])
        mo_ref[...] = m_
        vo_ref[...] = v_

    sds = jax.ShapeDtypeStruct((R, L), F32)
    return pl.pallas_call(
        body, name=name, grid=(R // tr,), in_specs=[blk] * 4, out_specs=[blk] * 3, out_shape=[sds] * 3,
        compiler_params=_params(32, 1),
    )(w, g, m, v)


def _adamw_rows(w, m, v, gbufs, row0, *, name, plan=None):
    L, R, C = w.shape
    tr = 256
    assert R % tr == 0 and row0 % tr == 0 and len(gbufs) == L
    grid = (L, R // tr)
    blk = pl.BlockSpec((None, tr, C), lambda l, i: (l, i, 0))
    gblks = [pl.BlockSpec((tr, C), lambda l, i, k=k: (row0 // tr + jnp.where(l == k, i, 0), 0)) for k in range(L)]
    c1, c2 = 1.0 - B1 ** STEP, 1.0 - B2 ** STEP

    def body(*refs):
        ins, (go_ref, d_ref, mo_ref, vo_ref), _, pctx = _split_refs(refs, 3 + L, 4, 0, plan)
        w_ref, m_ref, v_ref = ins[:3]
        g_refs = ins[3:]
        _plan_start(plan, pctx, grid)
        g_ = g_refs[0][...]
        for l in range(1, L):
            g_ = jnp.where(pl.program_id(0) == l, g_refs[l][...], g_)
        m_ = B1 * m_ref[...] + (1.0 - B1) * g_
        v_ = B2 * v_ref[...] + (1.0 - B2) * (g_ * g_)
        go_ref[...] = g_
        d_ref[...] = -LR * ((m_ / c1) / (jnp.sqrt(v_ / c2) + ADAM_EPS) + WD * w_ref[...])
        mo_ref[...] = m_
        vo_ref[...] = v_
        _plan_wait(plan, pctx, grid)

    sds = jax.ShapeDtypeStruct((L, R, C), F32)
    p_in, p_ospec, p_oshape, p_scr, p_alias = _plan_io(plan, 3 + L, 4)
    return pl.pallas_call(
        body, name=name, grid=grid, in_specs=[blk] * 3 + gblks + [_ANY] * len(p_in),
        out_specs=[blk] * 4 + p_ospec, out_shape=[sds] * 4 + p_oshape, scratch_shapes=p_scr,
        input_output_aliases=p_alias, compiler_params=_params(32, 2),
    )(*_hbm(w, m, v, *gbufs), *p_in)


def _add_pairs(g, theirs, ids, *, name):
    n, R, L = theirs.shape
    tr = math.gcd(R, 128)
    nb = R // tr

    def body(ids_ref, a_ref, b_ref, o_ref):
        o_ref[...] = (a_ref[...].astype(F32) + b_ref[...].astype(F32)).astype(BF16)

    blk = pl.BlockSpec((n, tr, L), lambda i, ids: (0, i, 0))
    return pl.pallas_call(
        body, name=name, out_shape=jax.ShapeDtypeStruct((n, R, L), BF16),
        grid_spec=pltpu.PrefetchScalarGridSpec(
            num_scalar_prefetch=1, grid=(nb,),
            in_specs=[pl.BlockSpec((n, tr, L), lambda i, ids: (0, ids[1] * nb + i, 0)), blk], out_specs=blk),
        compiler_params=_params(16, 1),
    )(ids, g, theirs)


def _sum_chips(pair, parts, ids, *, name):
    _, R, L = parts.shape
    tr = math.gcd(R, 128)

    def body(ids_ref, o_ref, r_ref, out_ref):
        out_ref[...] = ((o_ref[...].astype(F32) + r_ref[0].astype(F32)) + r_ref[1].astype(F32)) + r_ref[2].astype(F32)

    return pl.pallas_call(
        body, name=name, out_shape=jax.ShapeDtypeStruct((2, R, L), F32),
        grid_spec=pltpu.PrefetchScalarGridSpec(
            num_scalar_prefetch=1, grid=(R // tr,),
            in_specs=[pl.BlockSpec((None, tr, L), lambda i, ids: (ids[0], i, 0)),
                      pl.BlockSpec((3, tr, L), lambda i, ids: (0, i, 0))],
            out_specs=pl.BlockSpec((None, tr, L), lambda i, ids: (ids[1], i, 0))),
        compiler_params=_params(32, 1),
    )(ids, pair, parts)


def _mesh_ids():
    x, y, c = _mesh_pos()
    return jnp.stack([2 * x + y, c]).astype(jnp.int32)


def _place_shard(rows, ids, *, name):
    R, L = rows.shape
    tr = 128

    def body(ids_ref, in_ref, out_ref):
        out_ref[...] = in_ref[...].astype(BF16)

    return pl.pallas_call(
        body, name=name, out_shape=jax.ShapeDtypeStruct((4, R, L), BF16),
        grid_spec=pltpu.PrefetchScalarGridSpec(
            num_scalar_prefetch=1, grid=(R // tr,), in_specs=[pl.BlockSpec((tr, L), lambda i, ids: (i, 0))],
            out_specs=pl.BlockSpec((None, tr, L), lambda i, ids: (ids[0], i, 0))),
        compiler_params=_params(16, 1),
    )(ids, rows)


def _remote(src, dst, send_sem, recv_sem, to):
    return pltpu.make_async_remote_copy(src_ref=src, dst_ref=dst, send_sem=send_sem, recv_sem=recv_sem,
                                        device_id=to, device_id_type=MESH_IDS)


def _rows(ref, lead, start, size):
    return ref.at[tuple(pl.ds(0, n) for n in ref.shape[:lead]) + (pl.ds(start, size),)]


def _other_chips():
    x, y, _ = _mesh_pos()
    return [(1 - x, y), (x, 1 - y), (1 - x, 1 - y)]


def _plan_gather_ici(bufs):
    n = len(bufs)

    def copies(outs, send, recv):
        x, y, c = _mesh_pos()
        res = []
        for b in range(n):
            half = bufs[b].shape[1] // 2
            mine = _rows(outs[b].at[2 * x + y], 0, c * half, half)
            for j, (cx, cy) in enumerate(_other_chips()):
                res.append((_remote(mine, mine, send(3 * b + j), recv(3 * b + j), (cx, cy, c)),
                            _remote(mine, _rows(outs[b].at[2 * cx + cy], 0, c * half, half),
                                    send(3 * b + j), recv(3 * b + j), (x, y, c))))
        return res

    def start(ins, outs, send, recv, loc):
        for out_cp, _ in copies(outs, send, recv):
            out_cp.start()

    def wait(ins, outs, send, recv, loc):
        for out_cp, in_cp in copies(outs, send, recv):
            in_cp.wait_recv()
            out_cp.wait_send()

    outs = [jax.ShapeDtypeStruct(b.shape, b.dtype) for b in bufs]
    return _Plan(bufs, outs, 3 * n, 0, start, wait, aliases={b: b for b in range(n)})


def _plan_gather_forward(bufs):
    n = len(bufs)

    def copies(outs, send, recv):
        x, y, c = _mesh_pos()
        res = []
        for b in range(n):
            half = bufs[b].shape[1] // 2
            for j, (cx, cy) in enumerate(_other_chips()):
                slot = outs[b].at[2 * cx + cy]
                res.append((_remote(_rows(slot, 0, c * half, half), _rows(slot, 0, c * half, half),
                                    send(3 * b + j), recv(3 * b + j), (x, y, 1 - c)),
                            _remote(_rows(slot, 0, c * half, half), _rows(slot, 0, (1 - c) * half, half),
                                    send(3 * b + j), recv(3 * b + j), (x, y, c))))
        return res

    def start(ins, outs, send, recv, loc):
        for out_cp, _ in copies(outs, send, recv):
            out_cp.start()

    def wait(ins, outs, send, recv, loc):
        for out_cp, in_cp in copies(outs, send, recv):
            in_cp.wait_recv()
            out_cp.wait_send()

    outs = [jax.ShapeDtypeStruct(b.shape, b.dtype) for b in bufs]
    return _Plan(bufs, outs, 3 * n, 0, start, wait, aliases={b: b for b in range(n)})


def _plan_pair_swap(g):
    half = g.shape[1] // 2

    def copy(ins, outs, send, recv, loc):
        x, y, c = _mesh_pos()
        return _remote(_rows(ins[0], 1, (1 - c) * half, half), outs[0], send(0), recv(0), (x, y, 1 - c))

    return _Plan([g], [jax.ShapeDtypeStruct((4, half, g.shape[2]), g.dtype)], 1, 0,
                 lambda *a: copy(*a).start(), lambda *a: copy(*a).wait())


def _plan_pair_gather(buf):
    def copies(ins, outs, send, recv, loc):
        x, y, c = _mesh_pos()
        return (_remote(outs[0].at[c], outs[0].at[c], send(0), recv(0), (x, y, 1 - c)),
                _remote(outs[0].at[c], outs[0].at[1 - c], send(0), recv(0), (x, y, c)))

    def wait(*a):
        out_cp, in_cp = copies(*a)
        in_cp.wait_recv()
        out_cp.wait_send()

    return _Plan([buf], [jax.ShapeDtypeStruct(buf.shape, buf.dtype)], 1, 0, lambda *a: copies(*a)[0].start(), wait,
                 aliases={0: 0})


def _plan_chip_scatter(p):
    def copies(ins, outs, send, recv, loc):
        _, _, c = _mesh_pos()
        return [_remote(ins[0].at[2 * cx + cy], outs[0].at[j], send(j), recv(j), (cx, cy, c))
                for j, (cx, cy) in enumerate(_other_chips())]

    def start(*a):
        for cp in copies(*a):
            cp.start()

    def wait(*a):
        for cp in copies(*a):
            cp.wait()

    return _Plan([p], [jax.ShapeDtypeStruct((3,) + p.shape[1:], p.dtype)], 3, 0, start, wait)


def _plan_exchange_all(vec):
    def copies(ins, outs, send, recv, loc):
        x, y, c = _mesh_pos()
        return [_remote(ins[0], outs[0].at[r - 1], send(r - 1), recv(r - 1), (x ^ (r >> 2), y ^ ((r >> 1) & 1), c ^ (r & 1)))
                for r in range(1, 8)]

    def start(*a):
        for cp in copies(*a):
            cp.start()

    def wait(*a):
        for cp in copies(*a):
            cp.wait()

    return _Plan([vec], [jax.ShapeDtypeStruct((7,) + vec.shape, vec.dtype)], 7, 0, start, wait)


SMALL_LAYOUT = {
    "mla_gq": (0, 1, 256, (1, 256)), "mla_gkv": (1, 1, 256, (1, 256)), "sgu_ln_g": (2, 1, 512, (1, 512)),
    "sgu_ln_b": (3, 1, 512, (1, 512)), "sgu_w": (4, 64, 1024, (64, 1024)), "sgu_b": (68, 1, 512, (1, 512)),
    "hg_lb": (69, 2, 1024, (2, 1024)), "hg_gnorm": (71, 1, 1024, (1, 256)), "ln1_g": (72, 2, 1024, (2, 1024)),
    "ln1_b": (74, 2, 1024, (2, 1024)), "ln2_g": (76, 2, 1024, (2, 1024)), "ln2_b": (78, 2, 1024, (2, 1024)),
}


def _small_pack(dgq, dgkv, dslg, dslb, dsw, dsb, dlb, dgn, ln_parts, sq_err):
    flat_ln = [p for pair in ln_parts for p in pair]

    def body(*refs):
        gq_ref, gkv_ref, slg_ref, slb_ref, sw_ref, sb_ref, lb_ref, gn_ref = refs[:8]
        ln_refs, err_ref, out_ref, t_sc = refs[8:16], refs[16], refs[17], refs[18]
        s8 = lambda ref: jnp.sum(ref[...], axis=0, keepdims=True)
        out_ref[...] = jnp.zeros_like(out_ref)
        out_ref[0:1, 0:256] = s8(gq_ref)
        out_ref[1:2, 0:256] = s8(gkv_ref)
        out_ref[2:3, 0:512] = s8(slg_ref)
        out_ref[3:4, 0:512] = s8(slb_ref)
        out_ref[4:68, :] = sw_ref[...]
        t_sc[...] = sb_ref[...].T
        for g in range(SGU_G):
            out_ref[68:69, g * SGU_C:(g + 1) * SGU_C] = t_sc[g:g + 1, :]
        d_lb1 = s8(lb_ref)
        out_ref[69:70, :] = -d_lb1
        out_ref[70:71, :] = d_lb1
        out_ref[71:72, :] = s8(gn_ref)
        for k, ref in enumerate(ln_refs):
            out_ref[72 + k:73 + k, :] = s8(ref)
        out_ref[0:1, 1023:1024] = jnp.sum(s8(err_ref), axis=1, keepdims=True) * (0.5 / D)

    vm = pl.BlockSpec(memory_space=pltpu.VMEM)
    return pl.pallas_call(
        body, name="small_grad_pack", in_specs=[vm] * 17, out_specs=vm,
        out_shape=jax.ShapeDtypeStruct((SMALL_ROWS, 1024), F32), scratch_shapes=[pltpu.VMEM((SGU_C, SGU_C), F32)],
        compiler_params=_params(16),
    )(dgq, dgkv, dslg, dslb, dsw.reshape(64, 1024), dsb, dlb, dgn, *flat_ln, sq_err)


def _small_update(vec, others, ids, w, m, v):
    names = list(SMALL_LAYOUT)
    n = len(names)
    c1, c2 = 1.0 - B1 ** STEP, 1.0 - B2 ** STEP
    have_others = others is not None

    def body(*refs):
        ids_ref, v_ref = refs[0], refs[1]
        k = 2 + have_others
        w_refs, m_refs, v_refs = refs[k:k + n], refs[k + n:k + 2 * n], refs[k + 2 * n:k + 3 * n]
        outs = refs[k + 3 * n:]
        row0_ref, tot_sc = outs[0], outs[-1]
        total = v_ref[...]
        if have_others:
            me = 2 * ids_ref[0] + ids_ref[1]
            total = None
            for d in range(8):
                rel = d ^ me
                term = jnp.where(rel == 0, v_ref[...], refs[2][jnp.maximum(rel - 1, 0)])
                total = term if total is None else total + term
        tot_sc[...] = total
        row0_ref[...] = tot_sc[0:1, :]
        for i, name in enumerate(names):
            r0, nr, width, _ = SMALL_LAYOUT[name]
            if name == "hg_gnorm":
                g_ = tot_sc[r0:r0 + 1, 0:256]
                for chip in range(1, 4):
                    g_ = jnp.where(ids_ref[0] == chip, tot_sc[r0:r0 + 1, chip * 256:(chip + 1) * 256], g_)
            else:
                g_ = tot_sc[r0:r0 + nr, 0:width]
            m_ = B1 * m_refs[i][...] + (1.0 - B1) * g_
            v_ = B2 * v_refs[i][...] + (1.0 - B2) * (g_ * g_)
            go, do, mo, vo = outs[1 + 4 * i:5 + 4 * i]
            go[...] = g_
            do[...] = -LR * ((m_ / c1) / (jnp.sqrt(v_ / c2) + ADAM_EPS) + WD * w_refs[i][...])
            mo[...] = m_
            vo[...] = v_

    full = lambda shape: pl.BlockSpec(shape, lambda i, ids, nd=len(shape): (0,) * nd)
    kshapes = [SMALL_LAYOUT[name][3] for name in names]
    operands = [vec] + ([others] if have_others else []) + [d[name] for d in (w, m, v) for name in names]
    out_shapes = [jax.ShapeDtypeStruct((1, 1024), F32)] + [jax.ShapeDtypeStruct(s, F32) for s in kshapes for _ in range(4)]
    res = pl.pallas_call(
        body, name="small_update", out_shape=out_shapes,
        grid_spec=pltpu.PrefetchScalarGridSpec(
            num_scalar_prefetch=1, grid=(1,), in_specs=[full(o.shape) for o in operands],
            out_specs=[full(s.shape) for s in out_shapes],
            scratch_shapes=[pltpu.VMEM((SMALL_ROWS, 1024), F32)]),
        compiler_params=_params(32, 1),
    )(ids, *operands)
    return res[0], {name: tuple(res[1 + 4 * i:5 + 4 * i]) for i, name in enumerate(names)}


ROWS_L1, ROWS_L0, ROWS_ODD_W = 3328, 2048, 384
ODD_PARTS = (("w_out_e", (256, 1024)), ("w_in_e", (1024, 392)), ("w_qb", (256, 192)), ("w_kvb", (256, 256)))
ODD_W_PARTS = tuple(p for p in ODD_PARTS if p[0] != "w_in_e")


def _odd_rows(parts, dtype, layout, total, gnorm=None):
    rows = [parts[n].reshape(-1, 1024).astype(dtype) for n, _ in layout]
    used = sum(r.shape[0] for r in rows)
    if gnorm is not None:
        bits = lax.bitcast_convert_type(gnorm.reshape(-1), BF16).reshape(1, 512)
        rows.append(jnp.pad(bits, ((0, 0), (0, 512))))
        used += 1
    rows.append(jnp.zeros((total - used, 1024), dtype))
    return jnp.concatenate(rows, axis=0)


def _odd_unrows(buf, layout, with_gnorm=False):
    out, off = {}, 0
    for n, shape in layout:
        nr = math.prod(shape) // 1024
        out[n] = buf[off:off + nr].reshape(shape)
        off += nr
    if with_gnorm:
        out["hg_gnorm"] = lax.bitcast_convert_type(buf[off, :512].reshape(256, 2), F32).reshape(1, 256)
    return out


def _rope_tables(positions):
    half = ROPE // 2
    inv_freq = ROPE_BASE ** (-jnp.arange(half, dtype=F32) / half)
    ang = positions.astype(F32).reshape(-1, 1) * inv_freq
    cos, sin = jnp.cos(ang), jnp.sin(ang)
    T = ang.shape[0]
    one, z16, z32 = jnp.ones((T, NOPE), F32), jnp.zeros((T, half), F32), jnp.zeros((T, 32), F32)
    z64 = jnp.zeros((T, NOPE), F32)
    c = jnp.concatenate([one, cos, cos, z32], axis=1)
    s1 = jnp.concatenate([z64, -sin, z16, z32], axis=1)
    s2 = jnp.concatenate([z64, z16, sin, z32], axis=1)
    return c, s1, s2


def _local_step(x, positions, tgt, odd, bufs, P, exchange):
    T = x.shape[0]
    row = lambda a: a.reshape(1, -1)
    rc, rs1, rs2 = _rope_tables(positions)
    blk = lambda f: pl.BlockSpec((None, D, D), f)

    w_in_e = odd["w_in_e"]
    w_in = jnp.concatenate([w_in_e[:, :512], w_in_e[:, 544:1568], w_in_e[:, 512:544], jnp.zeros((D, 96), BF16)], axis=1)
    wq = jnp.pad(odd["w_qb"].reshape(256, HEADS, NOPE + ROPE), ((0, 0), (0, 0), (0, 32))).reshape(256, HEADS * 128)
    kvb = odd["w_kvb"].reshape(256, HEADS, NOPE + VDIM)
    wk = jnp.pad(kvb[:, :, :NOPE], ((0, 0), (0, 0), (0, 64))).reshape(256, HEADS * 128)
    wv = kvb[:, :, NOPE:].reshape(256, HEADS * VDIM)
    w_out_e = odd["w_out_e"]
    sgu_w = P["sgu_w"][0]
    sgu_bt = P["sgu_b"][0].T
    gq, gkv = P["mla_gq"], P["mla_gkv"]
    gnorm = P["hg_gnorm"]

    z0 = _matmul(x, w_in, name="in_proj_e", M=T, N=1664, K=D, tn=1664)[0]
    q, k, v = _mla_prep(z0, gq, gkv, wq, wk, wv, rc, rs1, rs2)
    if exchange:
        ids = _mesh_ids()
        placed = [_place_shard(b, ids, name=f"place_shard_{l}") for l, b in enumerate(bufs)]
        a_out, lse, wga, wgb = _flash_fwd(q, k, v, plan=_plan_gather_ici(placed[:2]))
    else:
        a_out, lse = _flash_fwd(q, k, v)
        wga, wgb, wgc = bufs
    mix0 = _sgu_fwd(z0, a_out, P["sgu_ln_g"], P["sgu_ln_b"], sgu_w, sgu_bt)
    res = _proj_ln(mix0, w_out_e, x, row(P["ln1_g"][0]), row(P["ln1_b"][0]), name="out_proj_ln_e",
                   plan=_plan_gather_forward([wga, wgb]) if exchange else None)
    r1, h1, h1b = res[:3]
    if exchange:
        wga, wgb = res[3:]
    res = _ffn_ln(h1b, wga, h1, row(P["ln2_g"][0]), row(P["ln2_b"][0]), name="ffn_ln_0",
                  plan=_plan_gather_ici(placed[2:]) if exchange else None)
    ra0, r2, h2, h2b = res[:4]
    z4 = _matmul(h2b, wgb, name="in_proj_o", M=T, N=4 * D, K=D, b_spec=blk(lambda i, j, k: (j, 0, 0)),
                 out_shape=jax.ShapeDtypeStruct((4, T, D), F32),
                 o_spec=pl.BlockSpec((None, min(MM_ROWS, T), D), lambda i, j, k: (j, i, 0)))[0]
    y1, o_raw, states = _hgrn_fwd(z4, P["hg_lb"], gnorm)
    res2 = _proj_ln(y1, wgb, h2, row(P["ln1_g"][1]), row(P["ln1_b"][1]), name="out_proj_ln_o", w_rowblk=4,
                    plan=_plan_gather_forward([res[4]]) if exchange else None)
    r3, h3, h3b = res2[:3]
    if exchange:
        wgc = res2[3]
    ra1, r4, h4, _ = _ffn_ln(h3b, wgc, h3, row(P["ln2_g"][1]), row(P["ln2_b"][1]), name="ffn_ln_1")

    ln1_g, ln1_b, ln2_g, ln2_b = [None, None], [None, None], [None, None], [None, None]
    sq_err_parts = []

    def ffn_bwd(l, dh, r_out, ra, h_mid_b, g2, wg, rows, plan=None, tgt=None):
        dr, dr_b, dg, db, *sq_err = _ln_bwd(dh, r_out, row(g2), name=f"ln2_bwd_{l}", tgt=tgt)
        sq_err_parts.extend(sq_err)
        ln2_g[l], ln2_b[l] = dg, db
        da, *extra = _matmul(dr_b, wg, tb=True, mul=ra, out_dtype=BF16, name=f"ffn_da_{l}", M=T, N=4 * D, K=D,
                             b_spec=blk(lambda i, j, k: (j, 1, 0)), plan=plan)
        gbuf = _matmul(ra, dr_b, ta=True, a_sq=True, name=f"ffn_dw2_{l}", M=4 * D, N=D, K=T, tm=1024, tk=DW_TOKENS // 2,
                       out_shape=jax.ShapeDtypeStruct((4, rows, D), BF16), o_spec=blk(lambda i, j, k: (i, 1, 0)))[0]
        gbuf = _matmul(h_mid_b, da, ta=True, name=f"ffn_dw1_{l}", M=D, N=4 * D, K=T, tm=1024, tk=DW_TOKENS, into=gbuf,
                       out_shape=jax.ShapeDtypeStruct((4, rows, D), BF16), o_spec=blk(lambda i, j, k: (j, 0, 0)))[0]
        dh_mid = _matmul(da, wg, tb=True, add=dr, add_scale=ALPHA, name=f"ffn_dh_{l}", M=T, N=D, K=4 * D, tk=2 * D,
                         b_spec=pl.BlockSpec((2, D, D), lambda i, j, k: (k, 0, 0)))[0]
        return dh_mid, gbuf, extra

    dh3, g1, _ = ffn_bwd(1, h4, r4, ra1, h3b, P["ln2_g"][1], wgc, ROWS_L1, tgt=tgt)
    loss_parts = sq_err_parts[0]
    dr3, dr3_b, dg, db = _ln_bwd(dh3, r3, row(P["ln1_g"][1]), name="ln1_bwd_1")
    ln1_g[1], ln1_b[1] = dg, db
    g1_sds = jax.ShapeDtypeStruct((4, ROWS_L1, D), BF16)
    g1 = _matmul(y1, dr3_b, ta=True, name="dw_out_o", M=D, N=D, K=T, tm=256, tk=DW_TOKENS, into=g1, out_shape=g1_sds,
                 o_spec=pl.BlockSpec((None, 256, D), lambda i, j, k: (i, 12, 0)))[0]
    dmix1 = _matmul(dr3_b, wgb, tb=True, name="dmix_o", M=T, N=D, K=D, b_spec=_rows4_spec(4, 3), b_merge=(D, D))[0]
    dz4, dlb, dgn = _hgrn_bwd(z4, o_raw, dmix1, states, P["hg_lb"], gnorm)
    g1 = _matmul(h2b, dz4, ta=True, name="dw_in_o", M=D, N=4 * D, K=T, tm=1024, tk=DW_TOKENS, into=g1, out_shape=g1_sds,
                 b_spec=pl.BlockSpec((None, min(DW_TOKENS, T), D), lambda i, j, k: (j, k, 0)),
                 o_spec=blk(lambda i, j, k: (j, 2, 0)))[0]
    dh2 = _matmul(dz4, wgb, tb=True, add=dr3, add_scale=ALPHA, name="dh_in_o", M=T, N=D, K=4 * D, tk=2 * D,
                  a_spec=pl.BlockSpec((2, min(MM_ROWS, T), D), lambda i, j, k: (k, i, 0)),
                  b_spec=pl.BlockSpec((2, D, D), lambda i, j, k: (k, 0, 0)))[0]

    dh1, g0, swapped1 = ffn_bwd(0, dh2, r2, ra0, h1b, P["ln2_g"][0], wga, ROWS_L0,
                                plan=_plan_pair_swap(g1) if exchange else None)
    dr1, dr1_b, dg, db = _ln_bwd(dh1, r1, row(P["ln1_g"][0]), name="ln1_bwd_0")
    ln1_g[0], ln1_b[0] = dg, db
    godd = {"w_out_e": _matmul(mix0, dr1_b, ta=True, name="dw_out_e", M=D, N=D, K=T, tm=1024, tk=DW_TOKENS)[0]}
    dmix0, *swapped0 = _matmul(dr1_b, w_out_e, tb=True, name="dmix_e", M=T, N=D, K=D,
                               plan=_plan_pair_swap(g0) if exchange else None)
    delta, do_b = _attn_delta(dmix0, a_out)
    if exchange:
        pair1 = _add_pairs(g1, swapped1[0], ids, name="grad_pair_add_1")
        pair0 = _add_pairs(g0, swapped0[0], ids, name="grad_pair_add_0")
        dq4, dk, dv, parts0, parts1 = _flash_bwd(
            q, k, v, do_b, lse, delta, plan=_join_plans([_plan_chip_scatter(pair0), _plan_chip_scatter(pair1)]))
        half0 = _sum_chips(pair0, parts0, ids, name="grad_chip_sum_0")
        half1 = _sum_chips(pair1, parts1, ids, name="grad_chip_sum_1")
        dc, dkr, dwq, dwk, dwv, dgq, dgkv, g0, g1 = _mla_bwd(
            z0, dq4, dk, dv, gq, gkv, wq, wk, wv, rc, rs1, rs2,
            plan=_join_plans([_plan_pair_gather(half0), _plan_pair_gather(half1)]))
        g0, g1 = g0.reshape(ROWS_L0, D), g1.reshape(ROWS_L1, D)
    else:
        dq4, dk, dv = _flash_bwd(q, k, v, do_b, lse, delta)
        dc, dkr, dwq, dwk, dwv, dgq, dgkv = _mla_bwd(z0, dq4, dk, dv, gq, gkv, wq, wk, wv, rc, rs1, rs2)
    dz0, dsw, dsb, dslg, dslb = _sgu_bwd(z0, dmix0, dc, dkr, P["sgu_ln_g"], P["sgu_ln_b"], sgu_w, sgu_bt)
    small_vec = _small_pack(dgq, dgkv, dslg, dslb, dsw, dsb, dlb, dgn, [ln1_g, ln1_b, ln2_g, ln2_b], loss_parts)
    dw_in, *small_others = _matmul(x, dz0, ta=True, name="dw_in_e", M=D, N=1664, K=T, tm=1024, tn=1664,
                                   tk=DW_TOKENS // 4, plan=_plan_exchange_all(small_vec) if exchange else None)
    godd["w_in_e"] = jnp.concatenate([dw_in[:, :512], dw_in[:, 1536:1568], dw_in[:, 512:1536]], axis=1)
    godd["w_qb"] = dwq.reshape(256, HEADS, 128)[:, :, :NOPE + ROPE].reshape(256, HEADS * (NOPE + ROPE))
    godd["w_kvb"] = jnp.concatenate([dwk.reshape(256, HEADS, 128)[:, :, :NOPE], dwv.reshape(256, HEADS, VDIM)],
                                    axis=2).reshape(256, HEADS * (NOPE + VDIM))
    odd_plan = None
    if exchange:
        by_chip = [_odd_rows({"w_out_e": jnp.split(godd["w_out_e"], 4, axis=0)[j],
                              **{n: jnp.split(godd[n], 4, axis=1)[j] for n in ("w_qb", "w_kvb")}}, BF16,
                             ODD_W_PARTS, ROWS_ODD_W)
                   for j in range(4)]
        bufs_odd = [godd["w_in_e"].reshape(D, 4, 392).transpose(1, 0, 2).astype(BF16), jnp.stack(by_chip)]
        theirs = _run_plan(_join_plans([_plan_pair_swap(b) for b in bufs_odd]), name="odd_pair_swap")
        odd_pairs = [_add_pairs(b, t, ids, name=f"odd_pair_add_{k}") for k, (b, t) in enumerate(zip(bufs_odd, theirs))]
        odd_plan = _join_plans([_plan_chip_scatter(p) for p in odd_pairs])
    grad_x, *odd_parts = _matmul(dz0, w_in, tb=True, add=dr1, add_scale=ALPHA, name="dx", M=T, N=D, K=1664, tk=1664,
                                 plan=odd_plan)
    if exchange:
        godd = (odd_pairs, odd_parts)
    return grad_x, g0, g1, godd, small_vec, (small_others[0] if exchange else None)


WEIGHTS = ['w_in_e', 'mla_gq', 'mla_gkv', 'w_qb', 'w_kvb', 'sgu_ln_g', 'sgu_ln_b', 'sgu_w', 'sgu_b', 'w_out_e',
           'w_in_o', 'hg_lb', 'hg_gnorm', 'w_out_o', 'ln1_g', 'ln1_b', 'w_ff1', 'w_ff2', 'ln2_g', 'ln2_b']


def kernel(x, positions, w_in_e, mla_gq, mla_gkv, w_qb, w_kvb, sgu_ln_g, sgu_ln_b, sgu_w, sgu_b, w_out_e, w_in_o, hg_lb, hg_gnorm, w_out_o, ln1_g, ln1_b, w_ff1, w_ff2, ln2_g, ln2_b, loss_target, m_w_in_e, m_mla_gq, m_mla_gkv, m_w_qb, m_w_kvb, m_sgu_ln_g, m_sgu_ln_b, m_sgu_w, m_sgu_b, m_w_out_e, m_w_in_o, m_hg_lb, m_hg_gnorm, m_w_out_o, m_ln1_g, m_ln1_b, m_w_ff1, m_w_ff2, m_ln2_g, m_ln2_b, v_w_in_e, v_mla_gq, v_mla_gkv, v_w_qb, v_w_kvb, v_sgu_ln_g, v_sgu_ln_b, v_sgu_w, v_sgu_b, v_w_out_e, v_w_in_o, v_hg_lb, v_hg_gnorm, v_w_out_o, v_ln1_g, v_ln1_b, v_w_ff1, v_w_ff2, v_ln2_g, v_ln2_b):
    args = dict(locals())
    w = {n: args[n] for n in WEIGHTS}
    m = {n: args["m_" + n] for n in WEIGHTS}
    v = {n: args["v_" + n] for n in WEIGHTS}
    cx, cy, cc = _mesh_pos()
    chip = 2 * cx + cy

    odd_shard = _odd_rows({"w_out_e": w_out_e[0], "w_qb": w_qb[0], "w_kvb": w_kvb[0]}, BF16, ODD_W_PARTS, ROWS_ODD_W,
                          gnorm=hg_gnorm)
    ids = _mesh_ids()
    placed = [_place_shard(w_in_e[0], ids, name="place_shard_in_e"), _place_shard(odd_shard, ids, name="place_shard_odd")]
    gathered = _run_plan(_plan_gather_forward(_run_plan(_plan_gather_ici(placed), name="odd_gather")),
                         name="odd_gather_forward")
    per_chip = [_odd_unrows(gathered[1][j], ODD_W_PARTS, with_gnorm=True) for j in range(4)]
    odd = {"w_out_e": jnp.concatenate([p["w_out_e"] for p in per_chip], axis=0),
           "w_in_e": jnp.concatenate([gathered[0][j] for j in range(4)], axis=1)}
    for n in ("w_qb", "w_kvb"):
        odd[n] = jnp.concatenate([p[n] for p in per_chip], axis=1)
    small = {n: w[n] for n in SMALL_LAYOUT if n != "hg_gnorm"}
    small["hg_gnorm"] = jnp.concatenate([p["hg_gnorm"] for p in per_chip], axis=1)
    shard_rows = (jnp.concatenate([w_ff1[0], w_ff2[0]], axis=0).astype(BF16),
                  jnp.concatenate([w_in_o[0], w_out_o[0]], axis=0).astype(BF16),
                  jnp.concatenate([w_ff1[1], w_ff2[1]], axis=0).astype(BF16))

    grad_x, g_l0, g_l1, godd, small_vec, small_others = _local_step(
        x[0], positions[0], loss_target[0], odd, shard_rows, small, True)

    sums = [_sum_chips(pair, parts, ids, name=f"odd_chip_sum_{k}") for k, (pair, parts) in enumerate(zip(*godd))]
    g_in_e, g_rest = _run_plan(_join_plans([_plan_pair_gather(s) for s in sums]), name="odd_pair_gather")
    g_odd = _odd_unrows(g_rest.reshape(ROWS_ODD_W, 1024), ODD_W_PARTS)
    g_odd["w_in_e"] = g_in_e.reshape(D, 392)

    to_kernel = lambda d: {n: d[n].reshape(SMALL_LAYOUT[n][3]) for n in SMALL_LAYOUT}
    first_row, small_out = _small_update(small_vec, small_others, ids, to_kernel(w), to_kernel(m), to_kernel(v))
    loss = first_row[0, 1023]
    grads, delta, new_m, new_v = {}, {}, {}, {}
    for n, res in small_out.items():
        grads[n], delta[n], new_m[n], new_v[n] = (r.reshape(w[n].shape) for r in res)

    for n, bufs_, row0 in (("w_ff1", [g_l0, g_l1], 0), ("w_ff2", [g_l0, g_l1], 1024), ("w_in_o", [g_l1], 2048),
                           ("w_out_o", [g_l1], 3072)):
        grads[n], delta[n], new_m[n], new_v[n] = _adamw_rows(w[n], m[n], v[n], bufs_, row0, name=f"adamw_{n}")
    for n, _ in ODD_PARTS:
        grads[n] = g_odd[n][None]
        d_, m_, v_ = _adamw(w[n][0], g_odd[n], m[n][0], v[n][0], name=f"adamw_{n}")
        delta[n], new_m[n], new_v[n] = d_[None], m_[None], v_[None]

    return (loss, grad_x[None], *[grads[n] for n in WEIGHTS], *[delta[n] for n in WEIGHTS],
            *[new_m[n] for n in WEIGHTS], *[new_v[n] for n in WEIGHTS])
```

```python
import math

import jax
import jax.numpy as jnp
from jax import lax
from jax.experimental import pallas as pl
from jax.experimental.pallas import tpu as pltpu

F32 = jnp.float32
BF16 = jnp.bfloat16
MESH_IDS = pl.DeviceIdType.MESH

D = 1024
DEPTH = 2
HEADS = 8
NOPE, ROPE, VDIM = 64, 32, 64
QK_SCALE = (NOPE + ROPE) ** -0.5
ROPE_BASE = 10000.0
SGU_G, SGU_C = 4, 128
HG_CHUNK = 64
HG_HEADS_PER_STEP = 8
ALPHA = (2 * DEPTH) ** 0.25
EPS = 1e-5
LR, B1, B2, ADAM_EPS, WD, STEP = 0.001, 0.9, 0.999, 1e-08, 0.01, 10
GELU_C = math.sqrt(2.0 / math.pi)
GELU_A = 0.044715
MB = 1024 * 1024
ROW_BLOCK = 512
SMALL_ROWS = 80

NT_DIMS = (((1,), (1,)), ((), ()))
TN_DIMS = (((0,), (0,)), ((), ()))


def _params(vmem_mb, n_axes=0):
    kw = dict(vmem_limit_bytes=vmem_mb * MB)
    if n_axes:
        kw["dimension_semantics"] = ("arbitrary",) * n_axes
    return pltpu.CompilerParams(**kw)


_ANY = pl.BlockSpec(memory_space=pltpu.HBM)


def _mesh_pos():
    return lax.axis_index("x"), lax.axis_index("y"), lax.axis_index("c")


def _hbm(*arrays):
    return tuple(pltpu.with_memory_space_constraint(a, pltpu.HBM) if a.size >= 2 ** 18 else a for a in arrays)


class _Plan:
    def __init__(self, ins, outs, n_remote, n_local, start, wait, aliases=None):
        self.ins, self.outs, self.n_remote, self.n_local = list(ins), list(outs), n_remote, n_local
        self.start, self.wait, self.aliases = start, wait, dict(aliases or {})


def _join_plans(plans):
    ins, outs, aliases, parts = [], [], {}, []
    nr = nl = 0
    for p in plans:
        parts.append((p, len(ins), len(outs), nr, nl))
        aliases.update({len(ins) + i: len(outs) + o for i, o in p.aliases.items()})
        ins += p.ins
        outs += p.outs
        nr += p.n_remote
        nl += p.n_local

    def run(which):
        def go(in_refs, out_refs, send, recv, loc):
            for p, i0, o0, r0, l0 in parts:
                getattr(p, which)(in_refs[i0:i0 + len(p.ins)], out_refs[o0:o0 + len(p.outs)],
                                  lambda i, r0=r0: send(r0 + i), lambda i, r0=r0: recv(r0 + i),
                                  lambda i, l0=l0: loc(l0 + i))
        return go

    return _Plan(ins, outs, nr, nl, run("start"), run("wait"), aliases)


def _plan_io(plan, n_in, n_out):
    if plan is None:
        return [], [], [], [], {}
    sems = [pltpu.SemaphoreType.DMA((max(plan.n_remote, 1),)), pltpu.SemaphoreType.DMA((max(plan.n_remote, 1),)),
            pltpu.SemaphoreType.DMA((max(plan.n_local, 1),))]
    aliases = {n_in + i: n_out + o for i, o in plan.aliases.items()}
    return plan.ins, [_ANY] * len(plan.outs), plan.outs, sems, aliases


def _split_refs(refs, n_in, n_out, n_scr, plan):
    p_in, p_out = (len(plan.ins), len(plan.outs)) if plan is not None else (0, 0)
    refs = list(refs)
    ins, refs = refs[:n_in], refs[n_in:]
    pins, refs = refs[:p_in], refs[p_in:]
    outs, refs = refs[:n_out], refs[n_out:]
    pouts, refs = refs[:p_out], refs[p_out:]
    scr, psem = refs[:n_scr], refs[n_scr:]
    psem = tuple((lambda i, s=s: s.at[i]) for s in psem)
    return ins, outs, scr, (pins, pouts, psem)


def _grid_edge(grid, last):
    cond = None
    for ax, n in enumerate(grid):
        c = pl.program_id(ax) == (n - 1 if last else 0)
        cond = c if cond is None else cond & c
    return cond


def _plan_start(plan, pctx, grid):
    if plan is not None:
        pins, pouts, psem = pctx
        pl.when(_grid_edge(grid, False))(lambda: plan.start(pins, pouts, *psem))


def _plan_wait(plan, pctx, grid):
    if plan is not None:
        pins, pouts, psem = pctx
        pl.when(_grid_edge(grid, True))(lambda: plan.wait(pins, pouts, *psem))


def _run_plan(plan, *, name):
    def body(*refs):
        _, _, _, (pins, pouts, psem) = _split_refs(refs, 0, 0, 0, plan)
        plan.start(pins, pouts, *psem)
        plan.wait(pins, pouts, *psem)

    p_in, p_ospec, p_oshape, p_scr, p_alias = _plan_io(plan, 0, 0)
    return pl.pallas_call(body, name=name, in_specs=[_ANY] * len(p_in), out_specs=p_ospec, out_shape=p_oshape,
                          scratch_shapes=p_scr, input_output_aliases=p_alias)(*p_in)


def _fold8(x):
    return x.reshape(x.shape[0] // 8, 8, x.shape[1]).sum(axis=0)


def _ln_stats(r):
    mu = jnp.mean(r, -1, keepdims=True)
    xc = r - mu
    rstd = lax.rsqrt(jnp.mean(xc * xc, -1, keepdims=True) + EPS)
    return xc * rstd, rstd


def _sigmoid(x):
    return jax.nn.sigmoid(x)


def _gelu(x):
    return 0.5 * x * (1.0 + jnp.tanh(GELU_C * (x + GELU_A * x * x * x)))


def _gelu_grad(x):
    t = jnp.tanh(GELU_C * (x + GELU_A * x * x * x))
    return 0.5 * (1.0 + t) + 0.5 * x * (1.0 - t * t) * GELU_C * (1.0 + 3.0 * GELU_A * x * x)


MM_ROWS = 1024
DW_TOKENS = 4096


def _matmul(a, b, *, name, M, N, K, ta=False, tb=False, out_dtype=F32, tm=MM_ROWS, tn=1024, tk=1024,
            a_spec=None, b_spec=None, b_merge=None, out_shape=None, o_spec=None, into=None,
            a_sq=False, mul=None, add=None, add_scale=1.0, plan=None):
    tm, tn, tk = min(tm, M), min(tn, N), min(tk, K)
    assert M % tm == 0 and N % tn == 0 and K % tk == 0
    grid = (M // tm, N // tn, K // tk)
    nk = grid[2]
    if a_spec is None:
        a_spec = pl.BlockSpec((tk, tm), lambda i, j, k: (k, i)) if ta else pl.BlockSpec((tm, tk), lambda i, j, k: (i, k))
    if b_spec is None:
        b_spec = pl.BlockSpec((tn, tk), lambda i, j, k: (j, k)) if tb else pl.BlockSpec((tk, tn), lambda i, j, k: (k, j))
    if o_spec is None:
        o_spec = pl.BlockSpec((tm, tn), lambda i, j, k: (i, j))
        out_shape = jax.ShapeDtypeStruct((M, N), out_dtype)
    e_spec = pl.BlockSpec((tm, tn), lambda i, j, k: (i, j))
    dims = (((0 if ta else 1,), (1 if tb else 0,)), ((), ()))
    extra = [e for e in (mul, add, into) if e is not None]
    n_in = 2 + len(extra)

    def body(*refs):
        ins, outs, scr, pctx = _split_refs(refs, n_in, 1, 1 if nk > 1 else 0, plan)
        a_ref, b_ref = ins[0], ins[1]
        rest = list(ins[2:])
        mul_ref = rest.pop(0) if mul is not None else None
        add_ref = rest.pop(0) if add is not None else None
        o_ref = outs[0]
        _plan_start(plan, pctx, grid)
        av = a_ref[...].astype(BF16)
        if a_sq:
            av = av * av
        bv = b_ref[...]
        if b_merge is not None:
            bv = bv.reshape(b_merge)
        if bv.ndim == 3:
            w = av.shape[-1] // (1 if av.ndim == 3 else bv.shape[0])
            a_parts = [av[s] if av.ndim == 3 else av[:, s * w:(s + 1) * w] for s in range(bv.shape[0])]
            p = sum(lax.dot_general(a_parts[s], bv[s], dims, preferred_element_type=F32) for s in range(bv.shape[0]))
        else:
            p = lax.dot_general(av, bv, dims, preferred_element_type=F32)

        def finish(r):
            if mul_ref is not None:
                r = r * (2.0 * mul_ref[...].astype(F32))
            if add_ref is not None:
                r = r + add_scale * add_ref[...]
            o_ref[...] = r.astype(o_ref.dtype)

        if nk == 1:
            finish(p)
        else:
            acc_ref = scr[0]
            k = pl.program_id(2)

            @pl.when(k == 0)
            def _():
                acc_ref[...] = p

            @pl.when(k > 0)
            def _():
                acc_ref[...] += p

            @pl.when(k == nk - 1)
            def _():
                finish(acc_ref[...])

        _plan_wait(plan, pctx, grid)

    p_in, p_ospec, p_oshape, p_scr, p_alias = _plan_io(plan, n_in, 1)
    aliases = dict(p_alias)
    if into is not None:
        aliases[n_in - 1] = 0
    return pl.pallas_call(
        body, name=name, grid=grid,
        in_specs=[a_spec, b_spec] + [e_spec] * (len(extra) - (into is not None)) + [_ANY] * (into is not None)
        + [_ANY] * len(p_in),
        out_specs=[o_spec] + p_ospec, out_shape=[out_shape] + p_oshape,
        scratch_shapes=([pltpu.VMEM((tm, tn), F32)] if nk > 1 else []) + p_scr,
        input_output_aliases=aliases, compiler_params=_params(48, 3),
    )(*_hbm(a, b, *extra), *p_in)


def _rows4_spec(rowblk, n_axes):
    return pl.BlockSpec((4, 256, D), lambda *_: (0, rowblk, 0))


def _proj_ln(a_b, w, h_prev, g, b, *, name, w_rowblk=None, plan=None):
    T = a_b.shape[0]
    tm = min(MM_ROWS, T)
    grid = (T // tm,)
    row = pl.BlockSpec((tm, D), lambda i: (i, 0))
    vec = pl.BlockSpec((1, D), lambda i: (0, 0))
    w_spec = pl.BlockSpec((D, D), lambda i: (0, 0)) if w_rowblk is None else _rows4_spec(w_rowblk, 1)

    def body(*refs):
        (a_ref, w_ref, h_ref, g_ref, b_ref), (r_ref, ho_ref, hb_ref), _, pctx = _split_refs(refs, 5, 3, 0, plan)
        _plan_start(plan, pctx, grid)
        mix = jnp.dot(a_ref[...], w_ref[...].reshape(D, D), preferred_element_type=F32)
        r = ALPHA * h_ref[...] + mix
        xhat, _ = _ln_stats(r)
        y = xhat * g_ref[...] + b_ref[...]
        r_ref[...] = r
        ho_ref[...] = y
        hb_ref[...] = y.astype(BF16)
        _plan_wait(plan, pctx, grid)

    p_in, p_ospec, p_oshape, p_scr, p_alias = _plan_io(plan, 5, 3)
    return pl.pallas_call(
        body, name=name, grid=grid,
        in_specs=[row, w_spec, row, vec, vec] + [_ANY] * len(p_in),
        out_specs=[row, row, row] + p_ospec,
        out_shape=[jax.ShapeDtypeStruct((T, D), F32), jax.ShapeDtypeStruct((T, D), F32),
                   jax.ShapeDtypeStruct((T, D), BF16)] + p_oshape,
        scratch_shapes=p_scr, input_output_aliases=p_alias, compiler_params=_params(40, 1),
    )(*_hbm(a_b, w, h_prev, g, b), *p_in)


def _ffn_ln(h_b, wbuf, h, g, b, *, name, plan=None):
    T = h_b.shape[0]
    slots = 2
    tm, tf = min(ROW_BLOCK, T), slots * 1024
    nf = 4 // slots
    F = nf * tf
    grid = (T // tm, nf)
    row = pl.BlockSpec((tm, D), lambda i, j: (i, 0))
    vec = pl.BlockSpec((1, D), lambda i, j: (0, 0))

    def body(*refs):
        ((hb_ref, w1_ref, w2_ref, h_ref, g_ref, b_ref), (ra_ref, r_ref, ho_ref, hbo_ref), (acc_ref,),
         pctx) = _split_refs(refs, 6, 4, 1, plan)
        _plan_start(plan, pctx, grid)
        j = pl.program_id(1)
        hb = hb_ref[...]
        p = None
        for s in range(slots):
            ra = jnp.maximum(jnp.dot(hb, w1_ref[s], preferred_element_type=F32), 0.0)
            ra_ref[:, s * 1024:(s + 1) * 1024] = ra.astype(BF16)
            ps = jnp.dot((ra * ra).astype(BF16), w2_ref[s], preferred_element_type=F32)
            p = ps if p is None else p + ps

        @pl.when(j == 0)
        def _():
            acc_ref[...] = p

        @pl.when(j > 0)
        def _():
            acc_ref[...] += p

        @pl.when(j == nf - 1)
        def _():
            r = ALPHA * h_ref[...] + acc_ref[...]
            xhat, _ = _ln_stats(r)
            y = xhat * g_ref[...] + b_ref[...]
            r_ref[...] = r
            ho_ref[...] = y
            hbo_ref[...] = y.astype(BF16)

        _plan_wait(plan, pctx, grid)

    p_in, p_ospec, p_oshape, p_scr, p_alias = _plan_io(plan, 6, 4)
    return pl.pallas_call(
        body, name=name, grid=grid,
        in_specs=[row, pl.BlockSpec((slots, D, D), lambda i, j: (j, 0, 0)),
                  pl.BlockSpec((slots, D, D), lambda i, j: (j, 1, 0)), row, vec, vec] + [_ANY] * len(p_in),
        out_specs=[pl.BlockSpec((tm, tf), lambda i, j: (i, j)), row, row, row] + p_ospec,
        out_shape=[jax.ShapeDtypeStruct((T, F), BF16), jax.ShapeDtypeStruct((T, D), F32),
                   jax.ShapeDtypeStruct((T, D), F32), jax.ShapeDtypeStruct((T, D), BF16)] + p_oshape,
        scratch_shapes=[pltpu.VMEM((tm, D), F32)] + p_scr,
        input_output_aliases=p_alias, compiler_params=_params(56, 2),
    )(*_hbm(h_b, wbuf, wbuf, h, g, b), *p_in)


def _ln_bwd(dy, r, g, *, name, tgt=None):
    T = dy.shape[0]
    tm = min(ROW_BLOCK, T)
    row = pl.BlockSpec((tm, D), lambda i: (i, 0))
    acc = pl.BlockSpec((8, D), lambda i: (0, 0))
    n_in = 3 + (tgt is not None)

    def body(*refs):
        dy_ref, r_ref, g_ref = refs[:3]
        dr_ref, drb_ref, dg_ref, db_ref = refs[n_in:n_in + 4]

        @pl.when(pl.program_id(0) == 0)
        def _():
            for ref in refs[n_in + 2:]:
                ref[...] = jnp.zeros_like(ref)

        dy_ = dy_ref[...]
        if tgt is not None:
            err = dy_ - refs[3][...]
            refs[n_in + 4][...] += _fold8(err * err)
            dy_ = err * (1.0 / D)
        xhat, rstd = _ln_stats(r_ref[...])
        dxh = dy_ * g_ref[...]
        m1 = jnp.mean(dxh, -1, keepdims=True)
        m2 = jnp.mean(dxh * xhat, -1, keepdims=True)
        dr = rstd * (dxh - m1 - xhat * m2)
        dr_ref[...] = dr
        drb_ref[...] = dr.astype(BF16)
        dg_ref[...] += _fold8(dy_ * xhat)
        db_ref[...] += _fold8(dy_)

    extra = [] if tgt is None else [tgt]
    return pl.pallas_call(
        body, name=name, grid=(T // tm,),
        in_specs=[row, row, pl.BlockSpec((1, D), lambda i: (0, 0))] + [row] * len(extra),
        out_specs=[row, row, acc, acc] + [acc] * len(extra),
        out_shape=[jax.ShapeDtypeStruct((T, D), F32), jax.ShapeDtypeStruct((T, D), BF16)]
        + [jax.ShapeDtypeStruct((8, D), F32)] * (2 + len(extra)),
        compiler_params=_params(40, 1),
    )(*_hbm(dy, r, g, *extra))


def _rope(x, c, s1, s2):
    return x * c + pltpu.roll(x, 112, 1) * s1 + pltpu.roll(x, 16, 1) * s2


def _rope_t(dy, c, s1, s2):
    return dy * c + pltpu.roll(dy * s1, 16, 1) + pltpu.roll(dy * s2, 112, 1)


def _rms(x, g):
    rstd = lax.rsqrt(jnp.mean(x * x, -1, keepdims=True) + EPS)
    xhat = x * rstd
    return xhat * g, xhat, rstd


def _mla_prep(z0, gq, gkv, wq, wk, wv, rc, rs1, rs2):
    T = z0.shape[0]
    tm = min(ROW_BLOCK, T)
    HW = HEADS * 128

    def body(cq_ref, ckv_ref, kr_ref, gq_ref, gkv_ref, wq_ref, wk_ref, wv_ref, c_ref, s1_ref, s2_ref,
             q_ref, k_ref, v_ref):
        nq = _rms(cq_ref[...], gq_ref[...])[0].astype(BF16)
        nkv = _rms(ckv_ref[...], gkv_ref[...])[0].astype(BF16)
        q = jnp.dot(nq, wq_ref[...], preferred_element_type=F32)
        k = jnp.dot(nkv, wk_ref[...], preferred_element_type=F32)
        v = jnp.dot(nkv, wv_ref[...], preferred_element_type=F32)
        c, s1, s2 = c_ref[...], s1_ref[...], s2_ref[...]
        kr = _rope(pltpu.roll(kr_ref[...], 64, 1), c, s1, s2)
        for h in range(HEADS):
            sl = slice(h * 128, (h + 1) * 128)
            q_ref[:, sl] = (_rope(q[:, sl], c, s1, s2) * QK_SCALE).astype(BF16)
            k_ref[:, sl] = (k[:, sl] + kr).astype(BF16)
        v_ref[...] = v.astype(BF16)

    full = lambda shape: pl.BlockSpec(shape, lambda i: (0, 0))
    tab = pl.BlockSpec((tm, 128), lambda i: (i, 0))
    return pl.pallas_call(
        body, name="mla_prep", grid=(T // tm,),
        in_specs=[pl.BlockSpec((tm, 256), lambda i: (i, 0)), pl.BlockSpec((tm, 256), lambda i: (i, 1)),
                  pl.BlockSpec((tm, 128), lambda i: (i, 12)), full((1, 256)), full((1, 256)),
                  full((256, HW)), full((256, HW)), full((256, 512)), tab, tab, tab],
        out_specs=[pl.BlockSpec((tm, HW), lambda i: (i, 0)), pl.BlockSpec((tm, HW), lambda i: (i, 0)),
                   pl.BlockSpec((tm, 512), lambda i: (i, 0))],
        out_shape=[jax.ShapeDtypeStruct((T, HW), BF16), jax.ShapeDtypeStruct((T, HW), BF16),
                   jax.ShapeDtypeStruct((T, 512), BF16)],
        compiler_params=_params(40, 1),
    )(z0, z0, z0, gq, gkv, wq, wk, wv, rc, rs1, rs2)


def _flash_fwd(q, k, v, plan=None):
    T = q.shape[0]
    bq = min(2 * ROW_BLOCK, T)
    nq = T // bq
    grid = (4, nq, nq)

    def body(*refs):
        (q_ref, k_ref, v_ref), (o_ref, lse_ref), (m_sc, acc_sc), pctx = _split_refs(refs, 3, 2, 2, plan)
        _plan_start(plan, pctx, grid)
        i, j = pl.program_id(1), pl.program_id(2)
        first = lax.broadcasted_iota(jnp.int32, (bq, 128), 1) < 64

        @pl.when(j == 0)
        def _():
            m_sc[...] = jnp.full_like(m_sc, -jnp.inf)
            acc_sc[...] = jnp.zeros_like(acc_sc)

        def tile(r0, nr, nc, masked):
            rs = slice(r0, r0 + nr)
            vp = v_ref[0:nc, :]
            lanes = first[0:nc, :]
            for h in range(2):
                sl = slice(h * 128, (h + 1) * 128)
                s = lax.dot_general(q_ref[rs, sl], k_ref[0:nc, sl], NT_DIMS, preferred_element_type=F32)
                if masked:
                    rows = r0 + lax.broadcasted_iota(jnp.int32, (nr, nc), 0)
                    cols = lax.broadcasted_iota(jnp.int32, (nr, nc), 1)
                    s = jnp.where(cols <= rows, s, -jnp.inf)
                m_prev = m_sc[h, rs, 0:1]
                m_new = jnp.maximum(m_prev, jnp.max(s, axis=1, keepdims=True))
                alpha = jnp.exp(m_prev - m_new)
                p = jnp.exp(s - m_new).astype(BF16)
                vh = jnp.where(lanes if h == 0 else jnp.logical_not(lanes), vp, jnp.ones_like(vp))
                acc_sc[h, rs, :] = acc_sc[h, rs, :] * alpha + jnp.dot(p, vh, preferred_element_type=F32)
                m_sc[h, rs, :] = jnp.broadcast_to(m_new, (nr, 128))

        @pl.when(j < i)
        def _():
            tile(0, bq, bq, False)

        @pl.when(j == i)
        def _():
            tile(0, bq, bq, True)
            a0, a1 = acc_sc[0], acc_sc[1]
            l0, l1 = pltpu.roll(a0, 64, 1), pltpu.roll(a1, 64, 1)
            o_ref[...] = jnp.where(first, a0 / l0, a1 / l1).astype(BF16)
            lse_ref[...] = jnp.where(first, m_sc[0] + jnp.log(l0), m_sc[1] + jnp.log(l1))

        _plan_wait(plan, pctx, grid)

    kv = lambda hp, i, j: (jnp.minimum(i, j), hp)
    p_in, p_ospec, p_oshape, p_scr, p_alias = _plan_io(plan, 3, 2)
    return pl.pallas_call(
        body, name="flash_fwd", grid=grid,
        in_specs=[pl.BlockSpec((bq, 256), lambda hp, i, j: (i, hp)), pl.BlockSpec((bq, 256), kv),
                  pl.BlockSpec((bq, 128), kv)] + [_ANY] * len(p_in),
        out_specs=[pl.BlockSpec((bq, 128), lambda hp, i, j: (i, hp)),
                   pl.BlockSpec((bq, 128), lambda hp, i, j: (i, hp))] + p_ospec,
        out_shape=[jax.ShapeDtypeStruct((T, 512), BF16), jax.ShapeDtypeStruct((T, 512), F32)] + p_oshape,
        scratch_shapes=[pltpu.VMEM((2, bq, 128), F32), pltpu.VMEM((2, bq, 128), F32)] + p_scr,
        input_output_aliases=p_alias, compiler_params=_params(56, 3),
    )(*_hbm(q, k, v), *p_in)


def _attn_delta(dmix, o):
    T = o.shape[0]
    tm = min(ROW_BLOCK, T)
    blk = pl.BlockSpec((tm, 512), lambda i: (i, 0))

    def body(do_ref, o_ref, delta_ref, dob_ref):
        first = lax.broadcasted_iota(jnp.int32, (tm, 128), 1) < 64
        for hp in range(4):
            sl = slice(hp * 128, (hp + 1) * 128)
            prod = do_ref[:, sl] * o_ref[:, sl].astype(F32)
            d0 = jnp.sum(jnp.where(first, prod, 0.0), axis=1, keepdims=True)
            d1 = jnp.sum(jnp.where(first, 0.0, prod), axis=1, keepdims=True)
            delta_ref[:, sl] = jnp.where(first, d0, d1)
        dob_ref[...] = do_ref[...].astype(BF16)

    return pl.pallas_call(
        body, name="attn_delta", grid=(T // tm,), in_specs=[blk, blk], out_specs=[blk, blk],
        out_shape=[jax.ShapeDtypeStruct((T, 512), F32), jax.ShapeDtypeStruct((T, 512), BF16)],
        compiler_params=_params(32, 1),
    )(dmix, o)


def _flash_bwd(q, k, v, do_b, lse, delta, plan=None):
    T = q.shape[0]
    bq = min(2 * ROW_BLOCK, T)
    nq = T // bq
    grid = (4, nq, nq)

    def body(*refs):
        ((q_ref, k_ref, v_ref, do_ref, lse_ref, dl_ref), (dq_hbm, dk_ref, dv_ref), (dq_sc, dk_sc, dv_sc, sem),
         pctx) = _split_refs(refs, 6, 3, 4, plan)
        _plan_start(plan, pctx, grid)
        hp, j, i = pl.program_id(0), pl.program_id(1), pl.program_id(2)
        first = lax.broadcasted_iota(jnp.int32, (bq, 128), 1) < 64

        @pl.when((j == 0) & (i == 0))
        def _():
            dq_sc[...] = jnp.zeros_like(dq_sc)

        @pl.when(i == j)
        def _():
            dk_sc[...] = jnp.zeros_like(dk_sc)
            dv_sc[...] = jnp.zeros_like(dv_sc)

        def tile(r0, nr, nc, masked):
            rs, cs = slice(r0, r0 + nr), slice(0, nc)
            vp = v_ref[cs, :]
            do = do_ref[rs, :]
            lanes = first[rs, :]
            for h in range(2):
                sl = slice(h * 128, (h + 1) * 128)
                qh, kh = q_ref[rs, sl], k_ref[cs, sl]
                s = lax.dot_general(qh, kh, NT_DIMS, preferred_element_type=F32)
                p = jnp.exp(s - lse_ref[rs, h * 64:h * 64 + 1])
                if masked:
                    rows = r0 + lax.broadcasted_iota(jnp.int32, (nr, nc), 0)
                    cols = lax.broadcasted_iota(jnp.int32, (nr, nc), 1)
                    p = jnp.where(cols <= rows, p, 0.0)
                do_h = jnp.where(lanes if h == 0 else jnp.logical_not(lanes), do, jnp.zeros_like(do))
                dv_sc[cs, :] += lax.dot_general(p.astype(BF16), do_h, TN_DIMS, preferred_element_type=F32)
                dp = lax.dot_general(do_h, vp, NT_DIMS, preferred_element_type=F32)
                ds = (p * (dp - dl_ref[rs, h * 64:h * 64 + 1])).astype(BF16)
                dq_sc[i, rs, sl] += jnp.dot(ds, kh, preferred_element_type=F32)
                dk_sc[cs, sl] += lax.dot_general(ds, qh, TN_DIMS, preferred_element_type=F32)

        @pl.when(i > j)
        def _():
            tile(0, bq, bq, False)

        @pl.when(i == j)
        def _():
            tile(0, bq // 2, bq // 2, True)
            tile(bq // 2, bq // 2, bq, True)

        @pl.when(i == nq - 1)
        def _():
            dk_ref[...] = dk_sc[...]
            dv_ref[...] = dv_sc[...]

        @pl.when((j == nq - 1) & (i == nq - 1))
        def _():
            cp = pltpu.make_async_copy(dq_sc, dq_hbm.at[hp], sem)
            cp.start()
            cp.wait()

        _plan_wait(plan, pctx, grid)

    qi = lambda hp, j, i: (jnp.maximum(i, j), hp)
    kj = lambda hp, j, i: (j, hp)
    p_in, p_ospec, p_oshape, p_scr, p_alias = _plan_io(plan, 6, 3)
    return pl.pallas_call(
        body, name="flash_bwd", grid=grid,
        in_specs=[pl.BlockSpec((bq, 256), qi), pl.BlockSpec((bq, 256), kj), pl.BlockSpec((bq, 128), kj),
                  pl.BlockSpec((bq, 128), qi), pl.BlockSpec((bq, 128), qi), pl.BlockSpec((bq, 128), qi)]
        + [_ANY] * len(p_in),
        out_specs=[_ANY, pl.BlockSpec((bq, 256), kj), pl.BlockSpec((bq, 128), kj)] + p_ospec,
        out_shape=[jax.ShapeDtypeStruct((4, nq, bq, 256), F32), jax.ShapeDtypeStruct((T, 1024), F32),
                   jax.ShapeDtypeStruct((T, 512), F32)] + p_oshape,
        scratch_shapes=[pltpu.VMEM((nq, bq, 256), F32), pltpu.VMEM((bq, 256), F32), pltpu.VMEM((bq, 128), F32),
                        pltpu.SemaphoreType.DMA] + p_scr,
        input_output_aliases=p_alias, compiler_params=_params(56, 3),
    )(*_hbm(q, k, v, do_b, lse, delta), *p_in)


def _mla_bwd(z0, dq4, dk, dv, gq, gkv, wq, wk, wv, rc, rs1, rs2, plan=None):
    T = z0.shape[0]
    tm = min(ROW_BLOCK, T)
    HW = HEADS * 128
    grid = (T // tm,)
    dq4 = dq4.reshape(4, T, 256)

    def body(*refs):
        ((cq_ref, ckv_ref, dq_ref, dk_ref, dv_ref, gq_ref, gkv_ref, wq_ref, wk_ref, wv_ref, c_ref, s1_ref, s2_ref),
         (dc_ref, dkr_ref, dwq_ref, dwk_ref, dwv_ref, dgq_ref, dgkv_ref), _, pctx) = _split_refs(refs, 13, 7, 0, plan)
        _plan_start(plan, pctx, grid)

        @pl.when(pl.program_id(0) == 0)
        def _():
            for ref in (dwq_ref, dwk_ref, dwv_ref, dgq_ref, dgkv_ref):
                ref[...] = jnp.zeros_like(ref)

        c, s1, s2 = c_ref[...], s1_ref[...], s2_ref[...]
        lane = lax.broadcasted_iota(jnp.int32, (tm, 128), 1)
        nq, xq, rq = _rms(cq_ref[...], gq_ref[...])
        nkv, xkv, rkv = _rms(ckv_ref[...], gkv_ref[...])
        nq_b, nkv_b = nq.astype(BF16), nkv.astype(BF16)

        dq_parts, dk_parts = [], []
        dkr = jnp.zeros((tm, 128), F32)
        for h in range(HEADS):
            blk = dq_ref[h // 2, :, (h % 2) * 128:(h % 2 + 1) * 128] * QK_SCALE
            dq_parts.append(_rope_t(blk, c, s1, s2).astype(BF16))
            kb = dk_ref[:, h * 128:(h + 1) * 128]
            dk_parts.append(jnp.where(lane < NOPE, kb, 0.0).astype(BF16))
            dkr = dkr + kb
        dq_b = jnp.concatenate(dq_parts, axis=1)
        dk_b = jnp.concatenate(dk_parts, axis=1)
        dv_b = dv_ref[...].astype(BF16)

        dwq_ref[...] += lax.dot_general(nq_b, dq_b, TN_DIMS, preferred_element_type=F32)
        dwk_ref[...] += lax.dot_general(nkv_b, dk_b, TN_DIMS, preferred_element_type=F32)
        dwv_ref[...] += lax.dot_general(nkv_b, dv_b, TN_DIMS, preferred_element_type=F32)
        dnq = lax.dot_general(dq_b, wq_ref[...], NT_DIMS, preferred_element_type=F32)
        dnkv = (lax.dot_general(dk_b, wk_ref[...], NT_DIMS, preferred_element_type=F32)
                + lax.dot_general(dv_b, wv_ref[...], NT_DIMS, preferred_element_type=F32))

        def rms_bwd(dn, xhat, rstd, g):
            dxh = dn * g
            return rstd * (dxh - xhat * jnp.mean(dxh * xhat, -1, keepdims=True))

        dc_ref[:, :256] = rms_bwd(dnq, xq, rq, gq_ref[...]).astype(BF16)
        dc_ref[:, 256:] = rms_bwd(dnkv, xkv, rkv, gkv_ref[...]).astype(BF16)
        dgq_ref[...] += _fold8(dnq * xq)
        dgkv_ref[...] += _fold8(dnkv * xkv)
        dkr = pltpu.roll(_rope_t(dkr, c, s1, s2), 64, 1)
        dkr_ref[...] = jnp.where(lane < ROPE, dkr, 0.0).astype(BF16)
        _plan_wait(plan, pctx, grid)

    full = lambda shape: pl.BlockSpec(shape, lambda i: (0,) * len(shape))
    tab = pl.BlockSpec((tm, 128), lambda i: (i, 0))
    p_in, p_ospec, p_oshape, p_scr, p_alias = _plan_io(plan, 13, 7)
    return pl.pallas_call(
        body, name="mla_bwd", grid=grid,
        in_specs=[pl.BlockSpec((tm, 256), lambda i: (i, 0)), pl.BlockSpec((tm, 256), lambda i: (i, 1)),
                  pl.BlockSpec((4, tm, 256), lambda i: (0, i, 0)),
                  pl.BlockSpec((tm, HW), lambda i: (i, 0)), pl.BlockSpec((tm, 512), lambda i: (i, 0)),
                  full((1, 256)), full((1, 256)), full((256, HW)), full((256, HW)), full((256, 512)), tab, tab, tab]
        + [_ANY] * len(p_in),
        out_specs=[pl.BlockSpec((tm, 512), lambda i: (i, 0)), tab, full((256, HW)), full((256, HW)),
                   full((256, 512)), full((8, 256)), full((8, 256))] + p_ospec,
        out_shape=[jax.ShapeDtypeStruct((T, 512), BF16), jax.ShapeDtypeStruct((T, 128), BF16),
                   jax.ShapeDtypeStruct((256, HW), F32), jax.ShapeDtypeStruct((256, HW), F32),
                   jax.ShapeDtypeStruct((256, 512), F32), jax.ShapeDtypeStruct((8, 256), F32),
                   jax.ShapeDtypeStruct((8, 256), F32)] + p_oshape,
        scratch_shapes=p_scr, input_output_aliases=p_alias, compiler_params=_params(48, 1),
    )(*_hbm(z0, z0, dq4, dk, dv, gq, gkv, wq, wk, wv, rc, rs1, rs2), *p_in)


def _sgu_fwd(z0, a_out, ln_g, ln_b, w, b_t):
    T = z0.shape[0]
    tm = min(ROW_BLOCK, T)
    W = SGU_G * SGU_C

    def body(u_ref, v_ref, a_ref, g_ref, b_ref, w_ref, bt_ref, o_ref):
        o_ref[:, :W] = a_ref[...]
        ug = _gelu(u_ref[...])
        xhat, _ = _ln_stats(_gelu(v_ref[...]))
        vn = (xhat * g_ref[...] + b_ref[...]).astype(BF16)
        tril = lax.broadcasted_iota(jnp.int32, (SGU_C, SGU_C), 0) >= lax.broadcasted_iota(jnp.int32, (SGU_C, SGU_C), 1)
        for g in range(SGU_G):
            cs = slice(g * SGU_C, (g + 1) * SGU_C)
            wg = jnp.where(tril, w_ref[g], 0.0).astype(BF16)
            bcol = bt_ref[:, g:g + 1]
            for c in range(tm // SGU_C):
                rs = slice(c * SGU_C, (c + 1) * SGU_C)
                mixed = jnp.dot(wg, vn[rs, cs], preferred_element_type=F32) + bcol
                o_ref[rs, W + g * SGU_C:W + (g + 1) * SGU_C] = (ug[rs, cs] * mixed).astype(BF16)

    full = lambda shape: pl.BlockSpec(shape, lambda i: (0,) * len(shape))
    return pl.pallas_call(
        body, name="sgu_fwd", grid=(T // tm,),
        in_specs=[pl.BlockSpec((tm, W), lambda i: (i, 1)), pl.BlockSpec((tm, W), lambda i: (i, 2)),
                  pl.BlockSpec((tm, W), lambda i: (i, 0)),
                  full((1, W)), full((1, W)), full((SGU_G, SGU_C, SGU_C)), full((SGU_C, SGU_G))],
        out_specs=pl.BlockSpec((tm, 2 * W), lambda i: (i, 0)),
        out_shape=jax.ShapeDtypeStruct((T, 2 * W), BF16),
        compiler_params=_params(32, 1),
    )(z0, z0, a_out, ln_g, ln_b, w, b_t)


def _sgu_bwd(z0, dmix, dc, dkr, ln_g, ln_b, w, b_t):
    T = z0.shape[0]
    tm = min(ROW_BLOCK, T)
    W = SGU_G * SGU_C

    def body(u_ref, v_ref, do_ref, dc_ref, dkr_ref, g_ref, b_ref, w_ref, bt_ref, dz_ref, dw_ref, db_ref, dlg_ref,
             dlb_ref):
        @pl.when(pl.program_id(0) == 0)
        def _():
            for ref in (dw_ref, db_ref, dlg_ref, dlb_ref):
                ref[...] = jnp.zeros_like(ref)

        dz_ref[:, :W] = dc_ref[...]
        dz_ref[:, 3 * W:] = dkr_ref[...]

        u, v, dout = u_ref[...], v_ref[...], do_ref[...]
        ug = _gelu(u)
        xhat, rstd = _ln_stats(_gelu(v))
        vn = (xhat * g_ref[...] + b_ref[...]).astype(BF16)
        dmixed = dout * ug
        dmixed_b = dmixed.astype(BF16)
        tril = lax.broadcasted_iota(jnp.int32, (SGU_C, SGU_C), 0) >= lax.broadcasted_iota(jnp.int32, (SGU_C, SGU_C), 1)
        lane = lax.broadcasted_iota(jnp.int32, (SGU_C, SGU_C), 1)
        dvn_cols = []
        for g in range(SGU_G):
            cs = slice(g * SGU_C, (g + 1) * SGU_C)
            wg = jnp.where(tril, w_ref[g], 0.0).astype(BF16)
            bcol = bt_ref[:, g:g + 1]
            dw_g = jnp.zeros((SGU_C, SGU_C), F32)
            db_g = jnp.zeros((SGU_C, 1), F32)
            dvn_rows = []
            for c in range(tm // SGU_C):
                rs = slice(c * SGU_C, (c + 1) * SGU_C)
                mixed = jnp.dot(wg, vn[rs, cs], preferred_element_type=F32) + bcol
                dz_ref[rs, W + g * SGU_C:W + (g + 1) * SGU_C] = (dout[rs, cs] * mixed * _gelu_grad(u[rs, cs])).astype(BF16)
                dm = dmixed_b[rs, cs]
                dw_g = dw_g + lax.dot_general(dm, vn[rs, cs], NT_DIMS, preferred_element_type=F32)
                db_g = db_g + jnp.sum(dmixed[rs, cs], axis=1, keepdims=True)
                dvn_rows.append(lax.dot_general(wg, dm, TN_DIMS, preferred_element_type=F32))
            dw_ref[g] += jnp.where(tril, dw_g, 0.0)
            db_ref[...] += jnp.where(lane == g, db_g, 0.0)
            dvn_cols.append(jnp.concatenate(dvn_rows, axis=0))
        dvn = jnp.concatenate(dvn_cols, axis=1)
        dxh = dvn * g_ref[...]
        m1 = jnp.mean(dxh, -1, keepdims=True)
        m2 = jnp.mean(dxh * xhat, -1, keepdims=True)
        dvg = rstd * (dxh - m1 - xhat * m2)
        dz_ref[:, 2 * W:3 * W] = (dvg * _gelu_grad(v)).astype(BF16)
        dlg_ref[...] += _fold8(dvn * xhat)
        dlb_ref[...] += _fold8(dvn)

    full = lambda shape: pl.BlockSpec(shape, lambda i: (0,) * len(shape))
    return pl.pallas_call(
        body, name="sgu_bwd", grid=(T // tm,),
        in_specs=[pl.BlockSpec((tm, W), lambda i: (i, 1)), pl.BlockSpec((tm, W), lambda i: (i, 2)),
                  pl.BlockSpec((tm, W), lambda i: (i, 1)), pl.BlockSpec((tm, W), lambda i: (i, 0)),
                  pl.BlockSpec((tm, 128), lambda i: (i, 0)),
                  full((1, W)), full((1, W)), full((SGU_G, SGU_C, SGU_C)), full((SGU_C, SGU_G))],
        out_specs=[pl.BlockSpec((tm, 3 * W + 128), lambda i: (i, 0)), full((SGU_G, SGU_C, SGU_C)),
                   full((SGU_C, SGU_C)), full((8, W)), full((8, W))],
        out_shape=[jax.ShapeDtypeStruct((T, 3 * W + 128), BF16), jax.ShapeDtypeStruct((SGU_G, SGU_C, SGU_C), F32),
                   jax.ShapeDtypeStruct((SGU_C, SGU_C), F32), jax.ShapeDtypeStruct((8, W), F32),
                   jax.ShapeDtypeStruct((8, W), F32)],
        compiler_params=_params(40, 1),
    )(z0, z0, dmix, dc, dkr, ln_g, ln_b, w, b_t)


def _hg_lower_bound(lb_ref):
    a0, a1 = lb_ref[0:1, :], lb_ref[1:2, :]
    m = jnp.maximum(a0, a1)
    e0, e1 = jnp.exp(a0 - m), jnp.exp(a1 - m)
    return e1 / (e0 + e1)


def _running_sum(x, reverse=False):
    n = x.shape[0]
    row = lax.broadcasted_iota(jnp.int32, x.shape, 0)
    s = 1
    while s < n:
        if reverse:
            x = x + jnp.where(row < n - s, pltpu.roll(x, n - s, 0), 0.0)
        else:
            x = x + jnp.where(row >= s, pltpu.roll(x, s, 0), 0.0)
        s *= 2
    return x


def _hg_chunk(qc, fc, lb):
    C = HG_CHUNK
    rows = lax.broadcasted_iota(jnp.int32, (C, C), 0)
    cols = lax.broadcasted_iota(jnp.int32, (C, C), 1)
    rowid = lax.broadcasted_iota(jnp.int32, (C, 128), 0)
    sq, sg = _sigmoid(qc), _sigmoid(fc)
    qf = qc * sq
    gate = lb + (1.0 - lb) * sg
    kk = 1.0 - gate
    lg = jnp.log(gate)
    bcum = _running_sum(lg)
    b_mid = jnp.sum(jnp.where(rowid < C // 2, lg, 0.0), axis=0, keepdims=True)
    b_last = jnp.sum(lg, axis=0, keepdims=True)
    eq, ek, e, eh = jnp.exp(bcum - b_mid), jnp.exp(b_mid - bcum), jnp.exp(bcum), jnp.exp(b_last - bcum)
    qt, kt, qe, khat = qf * eq, kk * ek, qf * e, kk * eh
    a = lax.dot_general(qt.astype(BF16), kt.astype(BF16), NT_DIMS, preferred_element_type=F32)
    a = jnp.where(rows >= cols, a, 0.0)
    return dict(sq=sq, sg=sg, gate=gate, kk=kk, eq=eq, ek=ek, e=e, eh=eh, qt=qt, kt=kt, qe=qe, khat=khat, a=a,
                e_last=jnp.exp(b_last), tril=rows >= cols, rowid=rowid)


def _hgrn_fwd(z4, hg_lb, gnorm):
    T = z4.shape[1]
    tb = min(ROW_BLOCK, T)
    C = HG_CHUNK
    ncb = tb // C
    HPB = HG_HEADS_PER_STEP

    def body(q_ref, f_ref, i_ref, g_ref, lb_ref, gn_ref, y_ref, o_ref, st_ref, st_sc):
        @pl.when(pl.program_id(1) == 0)
        def _():
            st_sc[...] = jnp.zeros_like(st_sc)

        def chunk(c, carry):
            rs = pl.ds(pl.multiple_of(c * C, C), C)
            for hh in range(HPB):
                hs = slice(hh * 128, (hh + 1) * 128)
                lb = _hg_lower_bound(lb_ref.at[:, hs])
                v_b = i_ref[rs, hs].astype(BF16)
                gc = g_ref[rs, hs]
                x = _hg_chunk(q_ref[rs, hs], f_ref[rs, hs], lb)
                st = st_sc[hh]
                st_ref[hh, c] = st
                o = (jnp.dot(x["a"].astype(BF16), v_b, preferred_element_type=F32)
                     + lax.dot_general(x["qe"].astype(BF16), st.astype(BF16), NT_DIMS, preferred_element_type=F32))
                st_sc[hh] = st * x["e_last"] + lax.dot_general(v_b, x["khat"].astype(BF16), TN_DIMS,
                                                               preferred_element_type=F32)
                o_ref[rs, hs] = o
                n = o * lax.rsqrt(jnp.mean(o * o, -1, keepdims=True) + EPS)
                y_ref[rs, hs] = (n * gn_ref[:, hs] * (gc * _sigmoid(gc))).astype(BF16)
            return carry

        lax.fori_loop(0, ncb, chunk, 0, unroll=4)

    W = 128 * HPB
    zb = lambda k: pl.BlockSpec((None, tb, W), lambda h, t: (k, t, h))
    out = pl.BlockSpec((tb, W), lambda h, t: (t, h))
    return pl.pallas_call(
        body, name="hgrn_fwd", grid=(HEADS // HPB, T // tb),
        in_specs=[zb(0), zb(1), zb(2), zb(3), pl.BlockSpec((2, W), lambda h, t: (0, h)),
                  pl.BlockSpec((1, W), lambda h, t: (0, h))],
        out_specs=[out, out, pl.BlockSpec((HPB, ncb, 128, 128), lambda h, t: (h, t, 0, 0))],
        out_shape=[jax.ShapeDtypeStruct((T, D), BF16), jax.ShapeDtypeStruct((T, D), F32),
                   jax.ShapeDtypeStruct((HEADS, T // C, 128, 128), F32)],
        scratch_shapes=[pltpu.VMEM((HPB, 128, 128), F32)],
        compiler_params=_params(48, 2),
    )(*_hbm(z4, z4, z4, z4, hg_lb, gnorm))


def _hgrn_bwd(z4, o_raw, dy, states, hg_lb, gnorm):
    T = z4.shape[1]
    tb = min(ROW_BLOCK, T)
    C = HG_CHUNK
    ncb = tb // C
    nt = T // tb
    HPB = HG_HEADS_PER_STEP

    def body(q_ref, f_ref, i_ref, g_ref, o_ref, dy_ref, st_ref, lb_ref, gn_ref, dz_ref, dlb_ref, dgn_ref, dst_sc):
        @pl.when(pl.program_id(1) == 0)
        def _():
            dst_sc[...] = jnp.zeros_like(dst_sc)
            dlb_ref[...] = jnp.zeros_like(dlb_ref)
            dgn_ref[...] = jnp.zeros_like(dgn_ref)

        def chunk(cc, carry):
            for hh in range(HPB):
                one_head(ncb - 1 - cc, hh, slice(hh * 128, (hh + 1) * 128))
            return carry

        def one_head(c, hh, hs):
            rs = pl.ds(pl.multiple_of(c * C, C), C)
            lb = _hg_lower_bound(lb_ref.at[:, hs])
            gn = gn_ref[:, hs]
            qc, gc = q_ref[rs, hs], g_ref[rs, hs]
            v_b = i_ref[rs, hs].astype(BF16)
            x = _hg_chunk(qc, f_ref[rs, hs], lb)
            st, dst = st_ref[hh, c], dst_sc[hh]
            st_b, dst_b = st.astype(BF16), dst.astype(BF16)
            o, dyc = o_ref[rs, hs], dy_ref[rs, hs]
            sgg = _sigmoid(gc)
            sil = gc * sgg
            rstd = lax.rsqrt(jnp.mean(o * o, -1, keepdims=True) + EPS)
            n = o * rstd
            dgn_ref[:, hs] += _fold8(dyc * n * sil)
            dn = dyc * gn * sil
            do = rstd * (dn - n * jnp.mean(dn * n, -1, keepdims=True))
            dg = dyc * n * gn * (sgg * (1.0 + gc * (1.0 - sgg)))
            do_b = do.astype(BF16)
            da = jnp.where(x["tril"], lax.dot_general(do_b, v_b, NT_DIMS, preferred_element_type=F32), 0.0).astype(BF16)
            qt_b, kt_b, qe_b, khat_b = (x[n_].astype(BF16) for n_ in ("qt", "kt", "qe", "khat"))
            dv = (lax.dot_general(x["a"].astype(BF16), do_b, TN_DIMS, preferred_element_type=F32)
                  + lax.dot_general(khat_b, dst_b, NT_DIMS, preferred_element_type=F32))
            dqt = jnp.dot(da, kt_b, preferred_element_type=F32)
            dqe = jnp.dot(do_b, st_b, preferred_element_type=F32)
            dkt = lax.dot_general(da, qt_b, TN_DIMS, preferred_element_type=F32)
            dkhat = jnp.dot(v_b, dst_b, preferred_element_type=F32)
            dst_sc[hh] = lax.dot_general(do_b, qe_b, TN_DIMS, preferred_element_type=F32) + dst * x["e_last"]
            de_last = jnp.sum(st * dst, axis=0, keepdims=True)
            dqf = dqt * x["eq"] + dqe * x["e"]
            dkk = dkt * x["ek"] + dkhat * x["eh"]
            dkh_kh = dkhat * x["khat"]
            db = dqt * qt_b.astype(F32) - dkt * kt_b.astype(F32) + dqe * x["qe"] - dkh_kh
            db_last = jnp.sum(dkh_kh, axis=0, keepdims=True) + de_last * x["e_last"]
            db = db + jnp.where(x["rowid"] == C - 1, db_last, 0.0)
            dlg = _running_sum(db, reverse=True)
            dgate = dlg / x["gate"] - dkk
            sg, sq = x["sg"], x["sq"]
            dlb_ref[:, hs] += _fold8(dgate * (1.0 - sg)) * (lb * (1.0 - lb))
            dz_ref[0, rs, hs] = (dqf * (sq * (1.0 + qc * (1.0 - sq)))).astype(BF16)
            dz_ref[1, rs, hs] = (dgate * (1.0 - lb) * sg * (1.0 - sg)).astype(BF16)
            dz_ref[2, rs, hs] = dv.astype(BF16)
            dz_ref[3, rs, hs] = dg.astype(BF16)

        lax.fori_loop(0, ncb, chunk, 0, unroll=4)

    W = 128 * HPB
    zb = lambda k: pl.BlockSpec((None, tb, W), lambda h, t: (k, nt - 1 - t, h))
    blk = pl.BlockSpec((tb, W), lambda h, t: (nt - 1 - t, h))
    acc = pl.BlockSpec((8, W), lambda h, t: (0, h))
    return pl.pallas_call(
        body, name="hgrn_bwd", grid=(HEADS // HPB, nt),
        in_specs=[zb(0), zb(1), zb(2), zb(3), blk, blk,
                  pl.BlockSpec((HPB, ncb, 128, 128), lambda h, t: (h, nt - 1 - t, 0, 0)),
                  pl.BlockSpec((2, W), lambda h, t: (0, h)), pl.BlockSpec((1, W), lambda h, t: (0, h))],
        out_specs=[pl.BlockSpec((4, tb, W), lambda h, t: (0, nt - 1 - t, h)), acc, acc],
        out_shape=[jax.ShapeDtypeStruct((4, T, D), BF16), jax.ShapeDtypeStruct((8, D), F32),
                   jax.ShapeDtypeStruct((8, D), F32)],
        scratch_shapes=[pltpu.VMEM((HPB, 128, 128), F32)],
        compiler_params=_params(48, 2),
    )(*_hbm(z4, z4, z4, z4, o_raw, dy, states, hg_lb, gnorm))


def _adamw(w, g, m, v, *, name):
    R, L = w.shape
    tr = R if R <= 512 else 512
    assert R % tr == 0
    blk = pl.BlockSpec((tr, L), lambda i: (i, 0))
    c1, c2 = 1.0 - B1 ** STEP, 1.0 - B2 ** STEP

    def body(w_ref, g_ref, m_ref, v_ref, d_ref, mo_ref, vo_ref):
        g_ = g_ref[...]
        m_ = B1 * m_ref[...] + (1.0 - B1) * g_
        v_ = B2 * v_ref[...] + (1.0 - B2) * (g_ * g_)
        d_ref[...] = -LR * ((m_ / c1) / (jnp.sqrt(v_ / c2) + ADAM_EPS) + WD * w_ref[...])
        mo_ref[...] = m_
        vo_ref[...] = v_

    sds = jax.ShapeDtypeStruct((R, L), F32)
    return pl.pallas_call(
        body, name=name, grid=(R // tr,), in_specs=[blk] * 4, out_specs=[blk] * 3, out_shape=[sds] * 3,
        compiler_params=_params(32, 1),
    )(w, g, m, v)


def _adamw_rows(w, m, v, gbufs, row0, *, name, plan=None):
    L, R, C = w.shape
    tr = 256
    assert R % tr == 0 and row0 % tr == 0 and len(gbufs) == L
    grid = (L, R // tr)
    blk = pl.BlockSpec((None, tr, C), lambda l, i: (l, i, 0))
    gblks = [pl.BlockSpec((tr, C), lambda l, i, k=k: (row0 // tr + jnp.where(l == k, i, 0), 0)) for k in range(L)]
    c1, c2 = 1.0 - B1 ** STEP, 1.0 - B2 ** STEP

    def body(*refs):
        ins, (go_ref, d_ref, mo_ref, vo_ref), _, pctx = _split_refs(refs, 3 + L, 4, 0, plan)
        w_ref, m_ref, v_ref = ins[:3]
        g_refs = ins[3:]
        _plan_start(plan, pctx, grid)
        g_ = g_refs[0][...]
        for l in range(1, L):
            g_ = jnp.where(pl.program_id(0) == l, g_refs[l][...], g_)
        m_ = B1 * m_ref[...] + (1.0 - B1) * g_
        v_ = B2 * v_ref[...] + (1.0 - B2) * (g_ * g_)
        go_ref[...] = g_
        d_ref[...] = -LR * ((m_ / c1) / (jnp.sqrt(v_ / c2) + ADAM_EPS) + WD * w_ref[...])
        mo_ref[...] = m_
        vo_ref[...] = v_
        _plan_wait(plan, pctx, grid)

    sds = jax.ShapeDtypeStruct((L, R, C), F32)
    p_in, p_ospec, p_oshape, p_scr, p_alias = _plan_io(plan, 3 + L, 4)
    return pl.pallas_call(
        body, name=name, grid=grid, in_specs=[blk] * 3 + gblks + [_ANY] * len(p_in),
        out_specs=[blk] * 4 + p_ospec, out_shape=[sds] * 4 + p_oshape, scratch_shapes=p_scr,
        input_output_aliases=p_alias, compiler_params=_params(32, 2),
    )(*_hbm(w, m, v, *gbufs), *p_in)


def _add_pairs(g, theirs, ids, *, name):
    n, R, L = theirs.shape
    tr = math.gcd(R, 128)
    nb = R // tr

    def body(ids_ref, a_ref, b_ref, o_ref):
        o_ref[...] = (a_ref[...].astype(F32) + b_ref[...].astype(F32)).astype(BF16)

    blk = pl.BlockSpec((n, tr, L), lambda i, ids: (0, i, 0))
    return pl.pallas_call(
        body, name=name, out_shape=jax.ShapeDtypeStruct((n, R, L), BF16),
        grid_spec=pltpu.PrefetchScalarGridSpec(
            num_scalar_prefetch=1, grid=(nb,),
            in_specs=[pl.BlockSpec((n, tr, L), lambda i, ids: (0, ids[1] * nb + i, 0)), blk], out_specs=blk),
        compiler_params=_params(16, 1),
    )(ids, g, theirs)


def _sum_chips(pair, parts, ids, *, name):
    _, R, L = parts.shape
    tr = math.gcd(R, 128)

    def body(ids_ref, o_ref, r_ref, out_ref):
        out_ref[...] = ((o_ref[...].astype(F32) + r_ref[0].astype(F32)) + r_ref[1].astype(F32)) + r_ref[2].astype(F32)

    return pl.pallas_call(
        body, name=name, out_shape=jax.ShapeDtypeStruct((2, R, L), F32),
        grid_spec=pltpu.PrefetchScalarGridSpec(
            num_scalar_prefetch=1, grid=(R // tr,),
            in_specs=[pl.BlockSpec((None, tr, L), lambda i, ids: (ids[0], i, 0)),
                      pl.BlockSpec((3, tr, L), lambda i, ids: (0, i, 0))],
            out_specs=pl.BlockSpec((None, tr, L), lambda i, ids: (ids[1], i, 0))),
        compiler_params=_params(32, 1),
    )(ids, pair, parts)


def _mesh_ids():
    x, y, c = _mesh_pos()
    return jnp.stack([2 * x + y, c]).astype(jnp.int32)


def _place_shard(rows, ids, *, name):
    R, L = rows.shape
    tr = 128

    def body(ids_ref, in_ref, out_ref):
        out_ref[...] = in_ref[...].astype(BF16)

    return pl.pallas_call(
        body, name=name, out_shape=jax.ShapeDtypeStruct((4, R, L), BF16),
        grid_spec=pltpu.PrefetchScalarGridSpec(
            num_scalar_prefetch=1, grid=(R // tr,), in_specs=[pl.BlockSpec((tr, L), lambda i, ids: (i, 0))],
            out_specs=pl.BlockSpec((None, tr, L), lambda i, ids: (ids[0], i, 0))),
        compiler_params=_params(16, 1),
    )(ids, rows)


def _remote(src, dst, send_sem, recv_sem, to):
    return pltpu.make_async_remote_copy(src_ref=src, dst_ref=dst, send_sem=send_sem, recv_sem=recv_sem,
                                        device_id=to, device_id_type=MESH_IDS)


def _rows(ref, lead, start, size):
    return ref.at[tuple(pl.ds(0, n) for n in ref.shape[:lead]) + (pl.ds(start, size),)]


def _other_chips():
    x, y, _ = _mesh_pos()
    return [(1 - x, y), (x, 1 - y), (1 - x, 1 - y)]


def _plan_gather_ici(bufs):
    n = len(bufs)

    def copies(outs, send, recv):
        x, y, c = _mesh_pos()
        res = []
        for b in range(n):
            half = bufs[b].shape[1] // 2
            mine = _rows(outs[b].at[2 * x + y], 0, c * half, half)
            for j, (cx, cy) in enumerate(_other_chips()):
                res.append((_remote(mine, mine, send(3 * b + j), recv(3 * b + j), (cx, cy, c)),
                            _remote(mine, _rows(outs[b].at[2 * cx + cy], 0, c * half, half),
                                    send(3 * b + j), recv(3 * b + j), (x, y, c))))
        return res

    def start(ins, outs, send, recv, loc):
        for out_cp, _ in copies(outs, send, recv):
            out_cp.start()

    def wait(ins, outs, send, recv, loc):
        for out_cp, in_cp in copies(outs, send, recv):
            in_cp.wait_recv()
            out_cp.wait_send()

    outs = [jax.ShapeDtypeStruct(b.shape, b.dtype) for b in bufs]
    return _Plan(bufs, outs, 3 * n, 0, start, wait, aliases={b: b for b in range(n)})


def _plan_gather_forward(bufs):
    n = len(bufs)

    def copies(outs, send, recv):
        x, y, c = _mesh_pos()
        res = []
        for b in range(n):
            half = bufs[b].shape[1] // 2
            for j, (cx, cy) in enumerate(_other_chips()):
                slot = outs[b].at[2 * cx + cy]
                res.append((_remote(_rows(slot, 0, c * half, half), _rows(slot, 0, c * half, half),
                                    send(3 * b + j), recv(3 * b + j), (x, y, 1 - c)),
                            _remote(_rows(slot, 0, c * half, half), _rows(slot, 0, (1 - c) * half, half),
                                    send(3 * b + j), recv(3 * b + j), (x, y, c))))
        return res

    def start(ins, outs, send, recv, loc):
        for out_cp, _ in copies(outs, send, recv):
            out_cp.start()

    def wait(ins, outs, send, recv, loc):
        for out_cp, in_cp in copies(outs, send, recv):
            in_cp.wait_recv()
            out_cp.wait_send()

    outs = [jax.ShapeDtypeStruct(b.shape, b.dtype) for b in bufs]
    return _Plan(bufs, outs, 3 * n, 0, start, wait, aliases={b: b for b in range(n)})


def _plan_pair_swap(g):
    half = g.shape[1] // 2

    def copy(ins, outs, send, recv, loc):
        x, y, c = _mesh_pos()
        return _remote(_rows(ins[0], 1, (1 - c) * half, half), outs[0], send(0), recv(0), (x, y, 1 - c))

    return _Plan([g], [jax.ShapeDtypeStruct((4, half, g.shape[2]), g.dtype)], 1, 0,
                 lambda *a: copy(*a).start(), lambda *a: copy(*a).wait())


def _plan_pair_gather(buf):
    def copies(ins, outs, send, recv, loc):
        x, y, c = _mesh_pos()
        return (_remote(outs[0].at[c], outs[0].at[c], send(0), recv(0), (x, y, 1 - c)),
                _remote(outs[0].at[c], outs[0].at[1 - c], send(0), recv(0), (x, y, c)))

    def wait(*a):
        out_cp, in_cp = copies(*a)
        in_cp.wait_recv()
        out_cp.wait_send()

    return _Plan([buf], [jax.ShapeDtypeStruct(buf.shape, buf.dtype)], 1, 0, lambda *a: copies(*a)[0].start(), wait,
                 aliases={0: 0})


def _plan_chip_scatter(p):
    def copies(ins, outs, send, recv, loc):
        _, _, c = _mesh_pos()
        return [_remote(ins[0].at[2 * cx + cy], outs[0].at[j], send(j), recv(j), (cx, cy, c))
                for j, (cx, cy) in enumerate(_other_chips())]

    def start(*a):
        for cp in copies(*a):
            cp.start()

    def wait(*a):
        for cp in copies(*a):
            cp.wait()

    return _Plan([p], [jax.ShapeDtypeStruct((3,) + p.shape[1:], p.dtype)], 3, 0, start, wait)


def _plan_exchange_all(vec):
    def copies(ins, outs, send, recv, loc):
        x, y, c = _mesh_pos()
        return [_remote(ins[0], outs[0].at[r - 1], send(r - 1), recv(r - 1), (x ^ (r >> 2), y ^ ((r >> 1) & 1), c ^ (r & 1)))
                for r in range(1, 8)]

    def start(*a):
        for cp in copies(*a):
            cp.start()

    def wait(*a):
        for cp in copies(*a):
            cp.wait()

    return _Plan([vec], [jax.ShapeDtypeStruct((7,) + vec.shape, vec.dtype)], 7, 0, start, wait)


SMALL_LAYOUT = {
    "mla_gq": (0, 1, 256, (1, 256)), "mla_gkv": (1, 1, 256, (1, 256)), "sgu_ln_g": (2, 1, 512, (1, 512)),
    "sgu_ln_b": (3, 1, 512, (1, 512)), "sgu_w": (4, 64, 1024, (64, 1024)), "sgu_b": (68, 1, 512, (1, 512)),
    "hg_lb": (69, 2, 1024, (2, 1024)), "hg_gnorm": (71, 1, 1024, (1, 256)), "ln1_g": (72, 2, 1024, (2, 1024)),
    "ln1_b": (74, 2, 1024, (2, 1024)), "ln2_g": (76, 2, 1024, (2, 1024)), "ln2_b": (78, 2, 1024, (2, 1024)),
}


def _small_pack(dgq, dgkv, dslg, dslb, dsw, dsb, dlb, dgn, ln_parts, sq_err):
    flat_ln = [p for pair in ln_parts for p in pair]

    def body(*refs):
        gq_ref, gkv_ref, slg_ref, slb_ref, sw_ref, sb_ref, lb_ref, gn_ref = refs[:8]
        ln_refs, err_ref, out_ref, t_sc = refs[8:16], refs[16], refs[17], refs[18]
        s8 = lambda ref: jnp.sum(ref[...], axis=0, keepdims=True)
        out_ref[...] = jnp.zeros_like(out_ref)
        out_ref[0:1, 0:256] = s8(gq_ref)
        out_ref[1:2, 0:256] = s8(gkv_ref)
        out_ref[2:3, 0:512] = s8(slg_ref)
        out_ref[3:4, 0:512] = s8(slb_ref)
        out_ref[4:68, :] = sw_ref[...]
        t_sc[...] = sb_ref[...].T
        for g in range(SGU_G):
            out_ref[68:69, g * SGU_C:(g + 1) * SGU_C] = t_sc[g:g + 1, :]
        d_lb1 = s8(lb_ref)
        out_ref[69:70, :] = -d_lb1
        out_ref[70:71, :] = d_lb1
        out_ref[71:72, :] = s8(gn_ref)
        for k, ref in enumerate(ln_refs):
            out_ref[72 + k:73 + k, :] = s8(ref)
        out_ref[0:1, 1023:1024] = jnp.sum(s8(err_ref), axis=1, keepdims=True) * (0.5 / D)

    vm = pl.BlockSpec(memory_space=pltpu.VMEM)
    return pl.pallas_call(
        body, name="small_grad_pack", in_specs=[vm] * 17, out_specs=vm,
        out_shape=jax.ShapeDtypeStruct((SMALL_ROWS, 1024), F32), scratch_shapes=[pltpu.VMEM((SGU_C, SGU_C), F32)],
        compiler_params=_params(16),
    )(dgq, dgkv, dslg, dslb, dsw.reshape(64, 1024), dsb, dlb, dgn, *flat_ln, sq_err)


def _small_update(vec, others, ids, w, m, v):
    names = list(SMALL_LAYOUT)
    n = len(names)
    c1, c2 = 1.0 - B1 ** STEP, 1.0 - B2 ** STEP
    have_others = others is not None

    def body(*refs):
        ids_ref, v_ref = refs[0], refs[1]
        k = 2 + have_others
        w_refs, m_refs, v_refs = refs[k:k + n], refs[k + n:k + 2 * n], refs[k + 2 * n:k + 3 * n]
        outs = refs[k + 3 * n:]
        row0_ref, tot_sc = outs[0], outs[-1]
        total = v_ref[...]
        if have_others:
            me = 2 * ids_ref[0] + ids_ref[1]
            total = None
            for d in range(8):
                rel = d ^ me
                term = jnp.where(rel == 0, v_ref[...], refs[2][jnp.maximum(rel - 1, 0)])
                total = term if total is None else total + term
        tot_sc[...] = total
        row0_ref[...] = tot_sc[0:1, :]
        for i, name in enumerate(names):
            r0, nr, width, _ = SMALL_LAYOUT[name]
            if name == "hg_gnorm":
                g_ = tot_sc[r0:r0 + 1, 0:256]
                for chip in range(1, 4):
                    g_ = jnp.where(ids_ref[0] == chip, tot_sc[r0:r0 + 1, chip * 256:(chip + 1) * 256], g_)
            else:
                g_ = tot_sc[r0:r0 + nr, 0:width]
            m_ = B1 * m_refs[i][...] + (1.0 - B1) * g_
            v_ = B2 * v_refs[i][...] + (1.0 - B2) * (g_ * g_)
            go, do, mo, vo = outs[1 + 4 * i:5 + 4 * i]
            go[...] = g_
            do[...] = -LR * ((m_ / c1) / (jnp.sqrt(v_ / c2) + ADAM_EPS) + WD * w_refs[i][...])
            mo[...] = m_
            vo[...] = v_

    full = lambda shape: pl.BlockSpec(shape, lambda i, ids, nd=len(shape): (0,) * nd)
    kshapes = [SMALL_LAYOUT[name][3] for name in names]
    operands = [vec] + ([others] if have_others else []) + [d[name] for d in (w, m, v) for name in names]
    out_shapes = [jax.ShapeDtypeStruct((1, 1024), F32)] + [jax.ShapeDtypeStruct(s, F32) for s in kshapes for _ in range(4)]
    res = pl.pallas_call(
        body, name="small_update", out_shape=out_shapes,
        grid_spec=pltpu.PrefetchScalarGridSpec(
            num_scalar_prefetch=1, grid=(1,), in_specs=[full(o.shape) for o in operands],
            out_specs=[full(s.shape) for s in out_shapes],
            scratch_shapes=[pltpu.VMEM((SMALL_ROWS, 1024), F32)]),
        compiler_params=_params(32, 1),
    )(ids, *operands)
    return res[0], {name: tuple(res[1 + 4 * i:5 + 4 * i]) for i, name in enumerate(names)}


ROWS_L1, ROWS_L0, ROWS_ODD_W = 3328, 2048, 384
ODD_PARTS = (("w_out_e", (256, 1024)), ("w_in_e", (1024, 392)), ("w_qb", (256, 192)), ("w_kvb", (256, 256)))
ODD_W_PARTS = tuple(p for p in ODD_PARTS if p[0] != "w_in_e")


def _odd_rows(parts, dtype, layout, total, gnorm=None):
    rows = [parts[n].reshape(-1, 1024).astype(dtype) for n, _ in layout]
    used = sum(r.shape[0] for r in rows)
    if gnorm is not None:
        bits = lax.bitcast_convert_type(gnorm.reshape(-1), BF16).reshape(1, 512)
        rows.append(jnp.pad(bits, ((0, 0), (0, 512))))
        used += 1
    rows.append(jnp.zeros((total - used, 1024), dtype))
    return jnp.concatenate(rows, axis=0)


def _odd_unrows(buf, layout, with_gnorm=False):
    out, off = {}, 0
    for n, shape in layout:
        nr = math.prod(shape) // 1024
        out[n] = buf[off:off + nr].reshape(shape)
        off += nr
    if with_gnorm:
        out["hg_gnorm"] = lax.bitcast_convert_type(buf[off, :512].reshape(256, 2), F32).reshape(1, 256)
    return out


def _rope_tables(positions):
    half = ROPE // 2
    inv_freq = ROPE_BASE ** (-jnp.arange(half, dtype=F32) / half)
    ang = positions.astype(F32).reshape(-1, 1) * inv_freq
    cos, sin = jnp.cos(ang), jnp.sin(ang)
    T = ang.shape[0]
    one, z16, z32 = jnp.ones((T, NOPE), F32), jnp.zeros((T, half), F32), jnp.zeros((T, 32), F32)
    z64 = jnp.zeros((T, NOPE), F32)
    c = jnp.concatenate([one, cos, cos, z32], axis=1)
    s1 = jnp.concatenate([z64, -sin, z16, z32], axis=1)
    s2 = jnp.concatenate([z64, z16, sin, z32], axis=1)
    return c, s1, s2


def _local_step(x, positions, tgt, odd, bufs, P, exchange):
    T = x.shape[0]
    row = lambda a: a.reshape(1, -1)
    rc, rs1, rs2 = _rope_tables(positions)
    blk = lambda f: pl.BlockSpec((None, D, D), f)

    w_in_e = odd["w_in_e"]
    w_in = jnp.concatenate([w_in_e[:, :512], w_in_e[:, 544:1568], w_in_e[:, 512:544], jnp.zeros((D, 96), BF16)], axis=1)
    wq = jnp.pad(odd["w_qb"].reshape(256, HEADS, NOPE + ROPE), ((0, 0), (0, 0), (0, 32))).reshape(256, HEADS * 128)
    kvb = odd["w_kvb"].reshape(256, HEADS, NOPE + VDIM)
    wk = jnp.pad(kvb[:, :, :NOPE], ((0, 0), (0, 0), (0, 64))).reshape(256, HEADS * 128)
    wv = kvb[:, :, NOPE:].reshape(256, HEADS * VDIM)
    w_out_e = odd["w_out_e"]
    sgu_w = P["sgu_w"][0]
    sgu_bt = P["sgu_b"][0].T
    gq, gkv = P["mla_gq"], P["mla_gkv"]
    gnorm = P["hg_gnorm"]

    z0 = _matmul(x, w_in, name="in_proj_e", M=T, N=1664, K=D, tn=1664)[0]
    q, k, v = _mla_prep(z0, gq, gkv, wq, wk, wv, rc, rs1, rs2)
    if exchange:
        ids = _mesh_ids()
        placed = [_place_shard(b, ids, name=f"place_shard_{l}") for l, b in enumerate(bufs)]
        a_out, lse, wga, wgb = _flash_fwd(q, k, v, plan=_plan_gather_ici(placed[:2]))
    else:
        a_out, lse = _flash_fwd(q, k, v)
        wga, wgb, wgc = bufs
    mix0 = _sgu_fwd(z0, a_out, P["sgu_ln_g"], P["sgu_ln_b"], sgu_w, sgu_bt)
    res = _proj_ln(mix0, w_out_e, x, row(P["ln1_g"][0]), row(P["ln1_b"][0]), name="out_proj_ln_e",
                   plan=_plan_gather_forward([wga, wgb]) if exchange else None)
    r1, h1, h1b = res[:3]
    if exchange:
        wga, wgb = res[3:]
    res = _ffn_ln(h1b, wga, h1, row(P["ln2_g"][0]), row(P["ln2_b"][0]), name="ffn_ln_0",
                  plan=_plan_gather_ici(placed[2:]) if exchange else None)
    ra0, r2, h2, h2b = res[:4]
    z4 = _matmul(h2b, wgb, name="in_proj_o", M=T, N=4 * D, K=D, b_spec=blk(lambda i, j, k: (j, 0, 0)),
                 out_shape=jax.ShapeDtypeStruct((4, T, D), F32),
                 o_spec=pl.BlockSpec((None, min(MM_ROWS, T), D), lambda i, j, k: (j, i, 0)))[0]
    y1, o_raw, states = _hgrn_fwd(z4, P["hg_lb"], gnorm)
    res2 = _proj_ln(y1, wgb, h2, row(P["ln1_g"][1]), row(P["ln1_b"][1]), name="out_proj_ln_o", w_rowblk=4,
                    plan=_plan_gather_forward([res[4]]) if exchange else None)
    r3, h3, h3b = res2[:3]
    if exchange:
        wgc = res2[3]
    ra1, r4, h4, _ = _ffn_ln(h3b, wgc, h3, row(P["ln2_g"][1]), row(P["ln2_b"][1]), name="ffn_ln_1")

    ln1_g, ln1_b, ln2_g, ln2_b = [None, None], [None, None], [None, None], [None, None]
    sq_err_parts = []

    def ffn_bwd(l, dh, r_out, ra, h_mid_b, g2, wg, rows, plan=None, tgt=None):
        dr, dr_b, dg, db, *sq_err = _ln_bwd(dh, r_out, row(g2), name=f"ln2_bwd_{l}", tgt=tgt)
        sq_err_parts.extend(sq_err)
        ln2_g[l], ln2_b[l] = dg, db
        da, *extra = _matmul(dr_b, wg, tb=True, mul=ra, out_dtype=BF16, name=f"ffn_da_{l}", M=T, N=4 * D, K=D,
                             b_spec=blk(lambda i, j, k: (j, 1, 0)), plan=plan)
        gbuf = _matmul(ra, dr_b, ta=True, a_sq=True, name=f"ffn_dw2_{l}", M=4 * D, N=D, K=T, tm=1024, tk=DW_TOKENS // 2,
                       out_shape=jax.ShapeDtypeStruct((4, rows, D), BF16), o_spec=blk(lambda i, j, k: (i, 1, 0)))[0]
        gbuf = _matmul(h_mid_b, da, ta=True, name=f"ffn_dw1_{l}", M=D, N=4 * D, K=T, tm=1024, tk=DW_TOKENS, into=gbuf,
                       out_shape=jax.ShapeDtypeStruct((4, rows, D), BF16), o_spec=blk(lambda i, j, k: (j, 0, 0)))[0]
        dh_mid = _matmul(da, wg, tb=True, add=dr, add_scale=ALPHA, name=f"ffn_dh_{l}", M=T, N=D, K=4 * D, tk=2 * D,
                         b_spec=pl.BlockSpec((2, D, D), lambda i, j, k: (k, 0, 0)))[0]
        return dh_mid, gbuf, extra

    dh3, g1, _ = ffn_bwd(1, h4, r4, ra1, h3b, P["ln2_g"][1], wgc, ROWS_L1, tgt=tgt)
    loss_parts = sq_err_parts[0]
    dr3, dr3_b, dg, db = _ln_bwd(dh3, r3, row(P["ln1_g"][1]), name="ln1_bwd_1")
    ln1_g[1], ln1_b[1] = dg, db
    g1_sds = jax.ShapeDtypeStruct((4, ROWS_L1, D), BF16)
    g1 = _matmul(y1, dr3_b, ta=True, name="dw_out_o", M=D, N=D, K=T, tm=256, tk=DW_TOKENS, into=g1, out_shape=g1_sds,
                 o_spec=pl.BlockSpec((None, 256, D), lambda i, j, k: (i, 12, 0)))[0]
    dmix1 = _matmul(dr3_b, wgb, tb=True, name="dmix_o", M=T, N=D, K=D, b_spec=_rows4_spec(4, 3), b_merge=(D, D))[0]
    dz4, dlb, dgn = _hgrn_bwd(z4, o_raw, dmix1, states, P["hg_lb"], gnorm)
    g1 = _matmul(h2b, dz4, ta=True, name="dw_in_o", M=D, N=4 * D, K=T, tm=1024, tk=DW_TOKENS, into=g1, out_shape=g1_sds,
                 b_spec=pl.BlockSpec((None, min(DW_TOKENS, T), D), lambda i, j, k: (j, k, 0)),
                 o_spec=blk(lambda i, j, k: (j, 2, 0)))[0]
    dh2 = _matmul(dz4, wgb, tb=True, add=dr3, add_scale=ALPHA, name="dh_in_o", M=T, N=D, K=4 * D, tk=2 * D,
                  a_spec=pl.BlockSpec((2, min(MM_ROWS, T), D), lambda i, j, k: (k, i, 0)),
                  b_spec=pl.BlockSpec((2, D, D), lambda i, j, k: (k, 0, 0)))[0]

    dh1, g0, swapped1 = ffn_bwd(0, dh2, r2, ra0, h1b, P["ln2_g"][0], wga, ROWS_L0,
                                plan=_plan_pair_swap(g1) if exchange else None)
    dr1, dr1_b, dg, db = _ln_bwd(dh1, r1, row(P["ln1_g"][0]), name="ln1_bwd_0")
    ln1_g[0], ln1_b[0] = dg, db
    godd = {"w_out_e": _matmul(mix0, dr1_b, ta=True, name="dw_out_e", M=D, N=D, K=T, tm=1024, tk=DW_TOKENS)[0]}
    dmix0, *swapped0 = _matmul(dr1_b, w_out_e, tb=True, name="dmix_e", M=T, N=D, K=D,
                               plan=_plan_pair_swap(g0) if exchange else None)
    delta, do_b = _attn_delta(dmix0, a_out)
    if exchange:
        pair1 = _add_pairs(g1, swapped1[0], ids, name="grad_pair_add_1")
        pair0 = _add_pairs(g0, swapped0[0], ids, name="grad_pair_add_0")
        dq4, dk, dv, parts0, parts1 = _flash_bwd(
            q, k, v, do_b, lse, delta, plan=_join_plans([_plan_chip_scatter(pair0), _plan_chip_scatter(pair1)]))
        half0 = _sum_chips(pair0, parts0, ids, name="grad_chip_sum_0")
        half1 = _sum_chips(pair1, parts1, ids, name="grad_chip_sum_1")
        dc, dkr, dwq, dwk, dwv, dgq, dgkv, g0, g1 = _mla_bwd(
            z0, dq4, dk, dv, gq, gkv, wq, wk, wv, rc, rs1, rs2,
            plan=_join_plans([_plan_pair_gather(half0), _plan_pair_gather(half1)]))
        g0, g1 = g0.reshape(ROWS_L0, D), g1.reshape(ROWS_L1, D)
    else:
        dq4, dk, dv = _flash_bwd(q, k, v, do_b, lse, delta)
        dc, dkr, dwq, dwk, dwv, dgq, dgkv = _mla_bwd(z0, dq4, dk, dv, gq, gkv, wq, wk, wv, rc, rs1, rs2)
    dz0, dsw, dsb, dslg, dslb = _sgu_bwd(z0, dmix0, dc, dkr, P["sgu_ln_g"], P["sgu_ln_b"], sgu_w, sgu_bt)
    small_vec = _small_pack(dgq, dgkv, dslg, dslb, dsw, dsb, dlb, dgn, [ln1_g, ln1_b, ln2_g, ln2_b], loss_parts)
    dw_in, *small_others = _matmul(x, dz0, ta=True, name="dw_in_e", M=D, N=1664, K=T, tm=1024, tn=1664,
                                   tk=DW_TOKENS // 4, plan=_plan_exchange_all(small_vec) if exchange else None)
    godd["w_in_e"] = jnp.concatenate([dw_in[:, :512], dw_in[:, 1536:1568], dw_in[:, 512:1536]], axis=1)
    godd["w_qb"] = dwq.reshape(256, HEADS, 128)[:, :, :NOPE + ROPE].reshape(256, HEADS * (NOPE + ROPE))
    godd["w_kvb"] = jnp.concatenate([dwk.reshape(256, HEADS, 128)[:, :, :NOPE], dwv.reshape(256, HEADS, VDIM)],
                                    axis=2).reshape(256, HEADS * (NOPE + VDIM))
    odd_plan = None
    if exchange:
        by_chip = [_odd_rows({"w_out_e": jnp.split(godd["w_out_e"], 4, axis=0)[j],
                              **{n: jnp.split(godd[n], 4, axis=1)[j] for n in ("w_qb", "w_kvb")}}, BF16,
                             ODD_W_PARTS, ROWS_ODD_W)
                   for j in range(4)]
        bufs_odd = [godd["w_in_e"].reshape(D, 4, 392).transpose(1, 0, 2).astype(BF16), jnp.stack(by_chip)]
        theirs = _run_plan(_join_plans([_plan_pair_swap(b) for b in bufs_odd]), name="odd_pair_swap")
        odd_pairs = [_add_pairs(b, t, ids, name=f"odd_pair_add_{k}") for k, (b, t) in enumerate(zip(bufs_odd, theirs))]
        odd_plan = _join_plans([_plan_chip_scatter(p) for p in odd_pairs])
    grad_x, *odd_parts = _matmul(dz0, w_in, tb=True, add=dr1, add_scale=ALPHA, name="dx", M=T, N=D, K=1664, tk=1664,
                                 plan=odd_plan)
    if exchange:
        godd = (odd_pairs, odd_parts)
    return grad_x, g0, g1, godd, small_vec, (small_others[0] if exchange else None)


WEIGHTS = ['w_in_e', 'mla_gq', 'mla_gkv', 'w_qb', 'w_kvb', 'sgu_ln_g', 'sgu_ln_b', 'sgu_w', 'sgu_b', 'w_out_e',
           'w_in_o', 'hg_lb', 'hg_gnorm', 'w_out_o', 'ln1_g', 'ln1_b', 'w_ff1', 'w_ff2', 'ln2_g', 'ln2_b']


def kernel(x, positions, w_in_e, mla_gq, mla_gkv, w_qb, w_kvb, sgu_ln_g, sgu_ln_b, sgu_w, sgu_b, w_out_e, w_in_o, hg_lb, hg_gnorm, w_out_o, ln1_g, ln1_b, w_ff1, w_ff2, ln2_g, ln2_b, loss_target, m_w_in_e, m_mla_gq, m_mla_gkv, m_w_qb, m_w_kvb, m_sgu_ln_g, m_sgu_ln_b, m_sgu_w, m_sgu_b, m_w_out_e, m_w_in_o, m_hg_lb, m_hg_gnorm, m_w_out_o, m_ln1_g, m_ln1_b, m_w_ff1, m_w_ff2, m_ln2_g, m_ln2_b, v_w_in_e, v_mla_gq, v_mla_gkv, v_w_qb, v_w_kvb, v_sgu_ln_g, v_sgu_ln_b, v_sgu_w, v_sgu_b, v_w_out_e, v_w_in_o, v_hg_lb, v_hg_gnorm, v_w_out_o, v_ln1_g, v_ln1_b, v_w_ff1, v_w_ff2, v_ln2_g, v_ln2_b):
    args = dict(locals())
    w = {n: args[n] for n in WEIGHTS}
    m = {n: args["m_" + n] for n in WEIGHTS}
    v = {n: args["v_" + n] for n in WEIGHTS}
    cx, cy, cc = _mesh_pos()
    chip = 2 * cx + cy

    odd_shard = _odd_rows({"w_out_e": w_out_e[0], "w_qb": w_qb[0], "w_kvb": w_kvb[0]}, BF16, ODD_W_PARTS, ROWS_ODD_W,
                          gnorm=hg_gnorm)
    ids = _mesh_ids()
    placed = [_place_shard(w_in_e[0], ids, name="place_shard_in_e"), _place_shard(odd_shard, ids, name="place_shard_odd")]
    gathered = _run_plan(_plan_gather_forward(_run_plan(_plan_gather_ici(placed), name="odd_gather")),
                         name="odd_gather_forward")
    per_chip = [_odd_unrows(gathered[1][j], ODD_W_PARTS, with_gnorm=True) for j in range(4)]
    odd = {"w_out_e": jnp.concatenate([p["w_out_e"] for p in per_chip], axis=0),
           "w_in_e": jnp.concatenate([gathered[0][j] for j in range(4)], axis=1)}
    for n in ("w_qb", "w_kvb"):
        odd[n] = jnp.concatenate([p[n] for p in per_chip], axis=1)
    small = {n: w[n] for n in SMALL_LAYOUT if n != "hg_gnorm"}
    small["hg_gnorm"] = jnp.concatenate([p["hg_gnorm"] for p in per_chip], axis=1)
    shard_rows = (jnp.concatenate([w_ff1[0], w_ff2[0]], axis=0).astype(BF16),
                  jnp.concatenate([w_in_o[0], w_out_o[0]], axis=0).astype(BF16),
                  jnp.concatenate([w_ff1[1], w_ff2[1]], axis=0).astype(BF16))

    grad_x, g_l0, g_l1, godd, small_vec, small_others = _local_step(
        x[0], positions[0], loss_target[0], odd, shard_rows, small, True)

    sums = [_sum_chips(pair, parts, ids, name=f"odd_chip_sum_{k}") for k, (pair, parts) in enumerate(zip(*godd))]
    g_in_e, g_rest = _run_plan(_join_plans([_plan_pair_gather(s) for s in sums]), name="odd_pair_gather")
    g_odd = _odd_unrows(g_rest.reshape(ROWS_ODD_W, 1024), ODD_W_PARTS)
    g_odd["w_in_e"] = g_in_e.reshape(D, 392)

    to_kernel = lambda d: {n: d[n].reshape(SMALL_LAYOUT[n][3]) for n in SMALL_LAYOUT}
    first_row, small_out = _small_update(small_vec, small_others, ids, to_kernel(w), to_kernel(m), to_kernel(v))
    loss = first_row[0, 1023]
    grads, delta, new_m, new_v = {}, {}, {}, {}
    for n, res in small_out.items():
        grads[n], delta[n], new_m[n], new_v[n] = (r.reshape(w[n].shape) for r in res)

    for n, bufs_, row0 in (("w_ff1", [g_l0, g_l1], 0), ("w_ff2", [g_l0, g_l1], 1024), ("w_in_o", [g_l1], 2048),
                           ("w_out_o", [g_l1], 3072)):
        grads[n], delta[n], new_m[n], new_v[n] = _adamw_rows(w[n], m[n], v[n], bufs_, row0, name=f"adamw_{n}")
    for n, _ in ODD_PARTS:
        grads[n] = g_odd[n][None]
        d_, m_, v_ = _adamw(w[n][0], g_odd[n], m[n][0], v[n][0], name=f"adamw_{n}")
        delta[n], new_m[n], new_v[n] = d_[None], m_[None], v_[None]

    return (loss, grad_x[None], *[grads[n] for n in WEIGHTS], *[delta[n] for n in WEIGHTS],
            *[new_m[n] for n in WEIGHTS], *[new_v[n] for n in WEIGHTS])
```

```python
import math

import jax
import jax.numpy as jnp
from jax import lax
from jax.experimental import pallas as pl
from jax.experimental.pallas import tpu as pltpu

F32 = jnp.float32
BF16 = jnp.bfloat16
MESH_IDS = pl.DeviceIdType.MESH

D = 1024
DEPTH = 2
HEADS = 8
NOPE, ROPE, VDIM = 64, 32, 64
QK_SCALE = (NOPE + ROPE) ** -0.5
ROPE_BASE = 10000.0
SGU_G, SGU_C = 4, 128
HG_CHUNK = 64
HG_HEADS_PER_STEP = 8
ALPHA = (2 * DEPTH) ** 0.25
EPS = 1e-5
LR, B1, B2, ADAM_EPS, WD, STEP = 0.001, 0.9, 0.999, 1e-08, 0.01, 10
GELU_C = math.sqrt(2.0 / math.pi)
GELU_A = 0.044715
MB = 1024 * 1024
ROW_BLOCK = 512
SMALL_ROWS = 80

NT_DIMS = (((1,), (1,)), ((), ()))
TN_DIMS = (((0,), (0,)), ((), ()))


def _params(vmem_mb, n_axes=0):
    kw = dict(vmem_limit_bytes=vmem_mb * MB)
    if n_axes:
        kw["dimension_semantics"] = ("arbitrary",) * n_axes
    return pltpu.CompilerParams(**kw)


_ANY = pl.BlockSpec(memory_space=pltpu.HBM)


def _mesh_pos():
    return lax.axis_index("x"), lax.axis_index("y"), lax.axis_index("c")


def _hbm(*arrays):
    return tuple(pltpu.with_memory_space_constraint(a, pltpu.HBM) if a.size >= 2 ** 18 else a for a in arrays)


class _Plan:
    def __init__(self, ins, outs, n_remote, n_local, start, wait, aliases=None):
        self.ins, self.outs, self.n_remote, self.n_local = list(ins), list(outs), n_remote, n_local
        self.start, self.wait, self.aliases = start, wait, dict(aliases or {})


def _join_plans(plans):
    ins, outs, aliases, parts = [], [], {}, []
    nr = nl = 0
    for p in plans:
        parts.append((p, len(ins), len(outs), nr, nl))
        aliases.update({len(ins) + i: len(outs) + o for i, o in p.aliases.items()})
        ins += p.ins
        outs += p.outs
        nr += p.n_remote
        nl += p.n_local

    def run(which):
        def go(in_refs, out_refs, send, recv, loc):
            for p, i0, o0, r0, l0 in parts:
                getattr(p, which)(in_refs[i0:i0 + len(p.ins)], out_refs[o0:o0 + len(p.outs)],
                                  lambda i, r0=r0: send(r0 + i), lambda i, r0=r0: recv(r0 + i),
                                  lambda i, l0=l0: loc(l0 + i))
        return go

    return _Plan(ins, outs, nr, nl, run("start"), run("wait"), aliases)


def _plan_io(plan, n_in, n_out):
    if plan is None:
        return [], [], [], [], {}
    sems = [pltpu.SemaphoreType.DMA((max(plan.n_remote, 1),)), pltpu.SemaphoreType.DMA((max(plan.n_remote, 1),)),
            pltpu.SemaphoreType.DMA((max(plan.n_local, 1),))]
    aliases = {n_in + i: n_out + o for i, o in plan.aliases.items()}
    return plan.ins, [_ANY] * len(plan.outs), plan.outs, sems, aliases


def _split_refs(refs, n_in, n_out, n_scr, plan):
    p_in, p_out = (len(plan.ins), len(plan.outs)) if plan is not None else (0, 0)
    refs = list(refs)
    ins, refs = refs[:n_in], refs[n_in:]
    pins, refs = refs[:p_in], refs[p_in:]
    outs, refs = refs[:n_out], refs[n_out:]
    pouts, refs = refs[:p_out], refs[p_out:]
    scr, psem = refs[:n_scr], refs[n_scr:]
    psem = tuple((lambda i, s=s: s.at[i]) for s in psem)
    return ins, outs, scr, (pins, pouts, psem)


def _grid_edge(grid, last):
    cond = None
    for ax, n in enumerate(grid):
        c = pl.program_id(ax) == (n - 1 if last else 0)
        cond = c if cond is None else cond & c
    return cond


def _plan_start(plan, pctx, grid):
    if plan is not None:
        pins, pouts, psem = pctx
        pl.when(_grid_edge(grid, False))(lambda: plan.start(pins, pouts, *psem))


def _plan_wait(plan, pctx, grid):
    if plan is not None:
        pins, pouts, psem = pctx
        pl.when(_grid_edge(grid, True))(lambda: plan.wait(pins, pouts, *psem))


def _run_plan(plan, *, name):
    def body(*refs):
        _, _, _, (pins, pouts, psem) = _split_refs(refs, 0, 0, 0, plan)
        plan.start(pins, pouts, *psem)
        plan.wait(pins, pouts, *psem)

    p_in, p_ospec, p_oshape, p_scr, p_alias = _plan_io(plan, 0, 0)
    return pl.pallas_call(body, name=name, in_specs=[_ANY] * len(p_in), out_specs=p_ospec, out_shape=p_oshape,
                          scratch_shapes=p_scr, input_output_aliases=p_alias)(*p_in)


def _fold8(x):
    return x.reshape(x.shape[0] // 8, 8, x.shape[1]).sum(axis=0)


def _ln_stats(r):
    mu = jnp.mean(r, -1, keepdims=True)
    xc = r - mu
    rstd = lax.rsqrt(jnp.mean(xc * xc, -1, keepdims=True) + EPS)
    return xc * rstd, rstd


def _sigmoid(x):
    return jax.nn.sigmoid(x)


def _gelu(x):
    return 0.5 * x * (1.0 + jnp.tanh(GELU_C * (x + GELU_A * x * x * x)))


def _gelu_grad(x):
    t = jnp.tanh(GELU_C * (x + GELU_A * x * x * x))
    return 0.5 * (1.0 + t) + 0.5 * x * (1.0 - t * t) * GELU_C * (1.0 + 3.0 * GELU_A * x * x)


MM_ROWS = 1024
DW_TOKENS = 4096


def _matmul(a, b, *, name, M, N, K, ta=False, tb=False, out_dtype=F32, tm=MM_ROWS, tn=1024, tk=1024,
            a_spec=None, b_spec=None, b_merge=None, out_shape=None, o_spec=None, into=None,
            a_sq=False, mul=None, add=None, add_scale=1.0, plan=None):
    tm, tn, tk = min(tm, M), min(tn, N), min(tk, K)
    assert M % tm == 0 and N % tn == 0 and K % tk == 0
    grid = (M // tm, N // tn, K // tk)
    nk = grid[2]
    if a_spec is None:
        a_spec = pl.BlockSpec((tk, tm), lambda i, j, k: (k, i)) if ta else pl.BlockSpec((tm, tk), lambda i, j, k: (i, k))
    if b_spec is None:
        b_spec = pl.BlockSpec((tn, tk), lambda i, j, k: (j, k)) if tb else pl.BlockSpec((tk, tn), lambda i, j, k: (k, j))
    if o_spec is None:
        o_spec = pl.BlockSpec((tm, tn), lambda i, j, k: (i, j))
        out_shape = jax.ShapeDtypeStruct((M, N), out_dtype)
    e_spec = pl.BlockSpec((tm, tn), lambda i, j, k: (i, j))
    dims = (((0 if ta else 1,), (1 if tb else 0,)), ((), ()))
    extra = [e for e in (mul, add, into) if e is not None]
    n_in = 2 + len(extra)

    def body(*refs):
        ins, outs, scr, pctx = _split_refs(refs, n_in, 1, 1 if nk > 1 else 0, plan)
        a_ref, b_ref = ins[0], ins[1]
        rest = list(ins[2:])
        mul_ref = rest.pop(0) if mul is not None else None
        add_ref = rest.pop(0) if add is not None else None
        o_ref = outs[0]
        _plan_start(plan, pctx, grid)
        av = a_ref[...].astype(BF16)
        if a_sq:
            av = av * av
        bv = b_ref[...]
        if b_merge is not None:
            bv = bv.reshape(b_merge)
        if bv.ndim == 3:
            w = av.shape[-1] // (1 if av.ndim == 3 else bv.shape[0])
            a_parts = [av[s] if av.ndim == 3 else av[:, s * w:(s + 1) * w] for s in range(bv.shape[0])]
            p = sum(lax.dot_general(a_parts[s], bv[s], dims, preferred_element_type=F32) for s in range(bv.shape[0]))
        else:
            p = lax.dot_general(av, bv, dims, preferred_element_type=F32)

        def finish(r):
            if mul_ref is not None:
                r = r * (2.0 * mul_ref[...].astype(F32))
            if add_ref is not None:
                r = r + add_scale * add_ref[...]
            o_ref[...] = r.astype(o_ref.dtype)

        if nk == 1:
            finish(p)
        else:
            acc_ref = scr[0]
            k = pl.program_id(2)

            @pl.when(k == 0)
            def _():
                acc_ref[...] = p

            @pl.when(k > 0)
            def _():
                acc_ref[...] += p

            @pl.when(k == nk - 1)
            def _():
                finish(acc_ref[...])

        _plan_wait(plan, pctx, grid)

    p_in, p_ospec, p_oshape, p_scr, p_alias = _plan_io(plan, n_in, 1)
    aliases = dict(p_alias)
    if into is not None:
        aliases[n_in - 1] = 0
    return pl.pallas_call(
        body, name=name, grid=grid,
        in_specs=[a_spec, b_spec] + [e_spec] * (len(extra) - (into is not None)) + [_ANY] * (into is not None)
        + [_ANY] * len(p_in),
        out_specs=[o_spec] + p_ospec, out_shape=[out_shape] + p_oshape,
        scratch_shapes=([pltpu.VMEM((tm, tn), F32)] if nk > 1 else []) + p_scr,
        input_output_aliases=aliases, compiler_params=_params(48, 3),
    )(*_hbm(a, b, *extra), *p_in)


def _rows4_spec(rowblk, n_axes):
    return pl.BlockSpec((4, 256, D), lambda *_: (0, rowblk, 0))


def _residual(h_ref, prev_refs):
    if not prev_refs:
        return h_ref[...]
    xhat, _ = _ln_stats(h_ref[...])
    return xhat * prev_refs[0][...] + prev_refs[1][...]


def _proj_ln(a_b, w, h_prev, g, b, *, name, prev_ln=(), w_rowblk=None, plan=None):
    T = a_b.shape[0]
    tm = min(MM_ROWS, T)
    grid = (T // tm,)
    row = pl.BlockSpec((tm, D), lambda i: (i, 0))
    vec = pl.BlockSpec((1, D), lambda i: (0, 0))
    w_spec = pl.BlockSpec((D, D), lambda i: (0, 0)) if w_rowblk is None else _rows4_spec(w_rowblk, 1)
    n_in = 5 + len(prev_ln)

    def body(*refs):
        ins, (r_ref, hb_ref), _, pctx = _split_refs(refs, n_in, 2, 0, plan)
        a_ref, w_ref, h_ref, g_ref, b_ref = ins[:5]
        _plan_start(plan, pctx, grid)
        mix = jnp.dot(a_ref[...], w_ref[...].reshape(D, D), preferred_element_type=F32)
        r = ALPHA * _residual(h_ref, ins[5:]) + mix
        xhat, _ = _ln_stats(r)
        r_ref[...] = r
        hb_ref[...] = (xhat * g_ref[...] + b_ref[...]).astype(BF16)
        _plan_wait(plan, pctx, grid)

    p_in, p_ospec, p_oshape, p_scr, p_alias = _plan_io(plan, n_in, 2)
    return pl.pallas_call(
        body, name=name, grid=grid,
        in_specs=[row, w_spec, row, vec, vec] + [vec] * len(prev_ln) + [_ANY] * len(p_in),
        out_specs=[row, row] + p_ospec,
        out_shape=[jax.ShapeDtypeStruct((T, D), F32), jax.ShapeDtypeStruct((T, D), BF16)] + p_oshape,
        scratch_shapes=p_scr, input_output_aliases=p_alias, compiler_params=_params(40, 1),
    )(*_hbm(a_b, w, h_prev, g, b, *prev_ln), *p_in)


def _ffn_ln(h_b, wbuf, h, g, b, *, name, prev_ln=(), plan=None):
    T = h_b.shape[0]
    slots = 2
    tm, tf = min(ROW_BLOCK, T), slots * 1024
    nf = 4 // slots
    F = nf * tf
    grid = (T // tm, nf)
    row = pl.BlockSpec((tm, D), lambda i, j: (i, 0))
    vec = pl.BlockSpec((1, D), lambda i, j: (0, 0))
    n_in = 6 + len(prev_ln)

    def body(*refs):
        ins, (ra_ref, r_ref, hbo_ref), (acc_ref,), pctx = _split_refs(refs, n_in, 3, 1, plan)
        hb_ref, w1_ref, w2_ref, h_ref, g_ref, b_ref = ins[:6]
        _plan_start(plan, pctx, grid)
        j = pl.program_id(1)
        hb = hb_ref[...]
        p = None
        for s in range(slots):
            ra = jnp.maximum(jnp.dot(hb, w1_ref[s], preferred_element_type=F32), 0.0)
            ra_ref[:, s * 1024:(s + 1) * 1024] = ra.astype(BF16)
            ps = jnp.dot((ra * ra).astype(BF16), w2_ref[s], preferred_element_type=F32)
            p = ps if p is None else p + ps

        @pl.when(j == 0)
        def _():
            acc_ref[...] = p

        @pl.when(j > 0)
        def _():
            acc_ref[...] += p

        @pl.when(j == nf - 1)
        def _():
            r = ALPHA * _residual(h_ref, ins[6:]) + acc_ref[...]
            xhat, _ = _ln_stats(r)
            r_ref[...] = r
            hbo_ref[...] = (xhat * g_ref[...] + b_ref[...]).astype(BF16)

        _plan_wait(plan, pctx, grid)

    p_in, p_ospec, p_oshape, p_scr, p_alias = _plan_io(plan, n_in, 3)
    return pl.pallas_call(
        body, name=name, grid=grid,
        in_specs=[row, pl.BlockSpec((slots, D, D), lambda i, j: (j, 0, 0)),
                  pl.BlockSpec((slots, D, D), lambda i, j: (j, 1, 0)), row, vec, vec] + [vec] * len(prev_ln)
        + [_ANY] * len(p_in),
        out_specs=[pl.BlockSpec((tm, tf), lambda i, j: (i, j)), row, row] + p_ospec,
        out_shape=[jax.ShapeDtypeStruct((T, F), BF16), jax.ShapeDtypeStruct((T, D), F32),
                   jax.ShapeDtypeStruct((T, D), BF16)] + p_oshape,
        scratch_shapes=[pltpu.VMEM((tm, D), F32)] + p_scr,
        input_output_aliases=p_alias, compiler_params=_params(56, 2),
    )(*_hbm(h_b, wbuf, wbuf, h, g, b, *prev_ln), *p_in)


def _ln_bwd(dy, r, g, *, name, loss_head=()):
    T = r.shape[0]
    tm = min(ROW_BLOCK, T)
    row = pl.BlockSpec((tm, D), lambda i: (i, 0))
    vec = pl.BlockSpec((1, D), lambda i: (0, 0))
    acc = pl.BlockSpec((8, D), lambda i: (0, 0))
    operands, in_specs = ([r, g, *loss_head], [row, vec, vec, row]) if loss_head else ([r, g, dy], [row, vec, row])
    n_in = len(operands)

    def body(*refs):
        r_ref, g_ref = refs[:2]
        dr_ref, drb_ref, dg_ref, db_ref = refs[n_in:n_in + 4]

        @pl.when(pl.program_id(0) == 0)
        def _():
            for ref in refs[n_in + 2:]:
                ref[...] = jnp.zeros_like(ref)

        xhat, rstd = _ln_stats(r_ref[...])
        if loss_head:
            err = xhat * g_ref[...] + refs[2][...] - refs[3][...]
            refs[n_in + 4][...] += _fold8(err * err)
            dy_ = err * (1.0 / D)
        else:
            dy_ = refs[2][...]
        dxh = dy_ * g_ref[...]
        m1 = jnp.mean(dxh, -1, keepdims=True)
        m2 = jnp.mean(dxh * xhat, -1, keepdims=True)
        dr = rstd * (dxh - m1 - xhat * m2)
        dr_ref[...] = dr
        drb_ref[...] = dr.astype(BF16)
        dg_ref[...] += _fold8(dy_ * xhat)
        db_ref[...] += _fold8(dy_)

    n_acc = 3 if loss_head else 2
    return pl.pallas_call(
        body, name=name, grid=(T // tm,), in_specs=in_specs, out_specs=[row, row] + [acc] * n_acc,
        out_shape=[jax.ShapeDtypeStruct((T, D), F32), jax.ShapeDtypeStruct((T, D), BF16)]
        + [jax.ShapeDtypeStruct((8, D), F32)] * n_acc,
        compiler_params=_params(40, 1),
    )(*_hbm(*operands))


def _rope(x, c, s1, s2):
    return x * c + pltpu.roll(x, 112, 1) * s1 + pltpu.roll(x, 16, 1) * s2


def _rope_t(dy, c, s1, s2):
    return dy * c + pltpu.roll(dy * s1, 16, 1) + pltpu.roll(dy * s2, 112, 1)


def _rms(x, g):
    rstd = lax.rsqrt(jnp.mean(x * x, -1, keepdims=True) + EPS)
    xhat = x * rstd
    return xhat * g, xhat, rstd


def _mla_prep(z0, gq, gkv, wq, wk, wv, rc, rs1, rs2):
    T = z0.shape[0]
    tm = min(ROW_BLOCK, T)
    HW = HEADS * 128

    def body(cq_ref, ckv_ref, kr_ref, gq_ref, gkv_ref, wq_ref, wk_ref, wv_ref, c_ref, s1_ref, s2_ref,
             q_ref, k_ref, v_ref):
        nq = _rms(cq_ref[...], gq_ref[...])[0].astype(BF16)
        nkv = _rms(ckv_ref[...], gkv_ref[...])[0].astype(BF16)
        q = jnp.dot(nq, wq_ref[...], preferred_element_type=F32)
        k = jnp.dot(nkv, wk_ref[...], preferred_element_type=F32)
        v = jnp.dot(nkv, wv_ref[...], preferred_element_type=F32)
        c, s1, s2 = c_ref[...], s1_ref[...], s2_ref[...]
        kr = _rope(pltpu.roll(kr_ref[...], 64, 1), c, s1, s2)
        for h in range(HEADS):
            sl = slice(h * 128, (h + 1) * 128)
            q_ref[:, sl] = (_rope(q[:, sl], c, s1, s2) * QK_SCALE).astype(BF16)
            k_ref[:, sl] = (k[:, sl] + kr).astype(BF16)
        v_ref[...] = v.astype(BF16)

    full = lambda shape: pl.BlockSpec(shape, lambda i: (0, 0))
    tab = pl.BlockSpec((tm, 128), lambda i: (i, 0))
    return pl.pallas_call(
        body, name="mla_prep", grid=(T // tm,),
        in_specs=[pl.BlockSpec((tm, 256), lambda i: (i, 0)), pl.BlockSpec((tm, 256), lambda i: (i, 1)),
                  pl.BlockSpec((tm, 128), lambda i: (i, 12)), full((1, 256)), full((1, 256)),
                  full((256, HW)), full((256, HW)), full((256, 512)), tab, tab, tab],
        out_specs=[pl.BlockSpec((tm, HW), lambda i: (i, 0)), pl.BlockSpec((tm, HW), lambda i: (i, 0)),
                   pl.BlockSpec((tm, 512), lambda i: (i, 0))],
        out_shape=[jax.ShapeDtypeStruct((T, HW), BF16), jax.ShapeDtypeStruct((T, HW), BF16),
                   jax.ShapeDtypeStruct((T, 512), BF16)],
        compiler_params=_params(40, 1),
    )(z0, z0, z0, gq, gkv, wq, wk, wv, rc, rs1, rs2)


def _flash_fwd(q, k, v, plan=None):
    T = q.shape[0]
    bq = min(2 * ROW_BLOCK, T)
    nq = T // bq
    grid = (4, nq, nq)

    def body(*refs):
        (q_ref, k_ref, v_ref), (o_ref, lse_ref), (m_sc, acc_sc), pctx = _split_refs(refs, 3, 2, 2, plan)
        _plan_start(plan, pctx, grid)
        i, j = pl.program_id(1), pl.program_id(2)
        first = lax.broadcasted_iota(jnp.int32, (bq, 128), 1) < 64

        @pl.when(j == 0)
        def _():
            m_sc[...] = jnp.full_like(m_sc, -jnp.inf)
            acc_sc[...] = jnp.zeros_like(acc_sc)

        def tile(r0, nr, nc, masked):
            rs = slice(r0, r0 + nr)
            vp = v_ref[0:nc, :]
            lanes = first[0:nc, :]
            for h in range(2):
                sl = slice(h * 128, (h + 1) * 128)
                s = lax.dot_general(q_ref[rs, sl], k_ref[0:nc, sl], NT_DIMS, preferred_element_type=F32)
                if masked:
                    rows = r0 + lax.broadcasted_iota(jnp.int32, (nr, nc), 0)
                    cols = lax.broadcasted_iota(jnp.int32, (nr, nc), 1)
                    s = jnp.where(cols <= rows, s, -jnp.inf)
                m_prev = m_sc[h, rs, 0:1]
                m_new = jnp.maximum(m_prev, jnp.max(s, axis=1, keepdims=True))
                alpha = jnp.exp(m_prev - m_new)
                p = jnp.exp(s - m_new).astype(BF16)
                vh = jnp.where(lanes if h == 0 else jnp.logical_not(lanes), vp, jnp.ones_like(vp))
                acc_sc[h, rs, :] = acc_sc[h, rs, :] * alpha + jnp.dot(p, vh, preferred_element_type=F32)
                m_sc[h, rs, :] = jnp.broadcast_to(m_new, (nr, 128))

        @pl.when(j < i)
        def _():
            tile(0, bq, bq, False)

        @pl.when(j == i)
        def _():
            tile(0, bq, bq, True)
            a0, a1 = acc_sc[0], acc_sc[1]
            l0, l1 = pltpu.roll(a0, 64, 1), pltpu.roll(a1, 64, 1)
            o_ref[...] = jnp.where(first, a0 / l0, a1 / l1).astype(BF16)
            lse_ref[...] = jnp.where(first, m_sc[0] + jnp.log(l0), m_sc[1] + jnp.log(l1))

        _plan_wait(plan, pctx, grid)

    kv = lambda hp, i, j: (jnp.minimum(i, j), hp)
    p_in, p_ospec, p_oshape, p_scr, p_alias = _plan_io(plan, 3, 2)
    return pl.pallas_call(
        body, name="flash_fwd", grid=grid,
        in_specs=[pl.BlockSpec((bq, 256), lambda hp, i, j: (i, hp)), pl.BlockSpec((bq, 256), kv),
                  pl.BlockSpec((bq, 128), kv)] + [_ANY] * len(p_in),
        out_specs=[pl.BlockSpec((bq, 128), lambda hp, i, j: (i, hp)),
                   pl.BlockSpec((bq, 128), lambda hp, i, j: (i, hp))] + p_ospec,
        out_shape=[jax.ShapeDtypeStruct((T, 512), BF16), jax.ShapeDtypeStruct((T, 512), F32)] + p_oshape,
        scratch_shapes=[pltpu.VMEM((2, bq, 128), F32), pltpu.VMEM((2, bq, 128), F32)] + p_scr,
        input_output_aliases=p_alias, compiler_params=_params(56, 3),
    )(*_hbm(q, k, v), *p_in)


def _attn_delta(dmix, o):
    T = o.shape[0]
    tm = min(ROW_BLOCK, T)
    blk = pl.BlockSpec((tm, 512), lambda i: (i, 0))

    def body(do_ref, o_ref, delta_ref, dob_ref):
        first = lax.broadcasted_iota(jnp.int32, (tm, 128), 1) < 64
        for hp in range(4):
            sl = slice(hp * 128, (hp + 1) * 128)
            prod = do_ref[:, sl] * o_ref[:, sl].astype(F32)
            d0 = jnp.sum(jnp.where(first, prod, 0.0), axis=1, keepdims=True)
            d1 = jnp.sum(jnp.where(first, 0.0, prod), axis=1, keepdims=True)
            delta_ref[:, sl] = jnp.where(first, d0, d1)
        dob_ref[...] = do_ref[...].astype(BF16)

    return pl.pallas_call(
        body, name="attn_delta", grid=(T // tm,), in_specs=[blk, blk], out_specs=[blk, blk],
        out_shape=[jax.ShapeDtypeStruct((T, 512), F32), jax.ShapeDtypeStruct((T, 512), BF16)],
        compiler_params=_params(32, 1),
    )(dmix, o)


def _flash_bwd(q, k, v, do_b, lse, delta, plan=None):
    T = q.shape[0]
    bq = min(2 * ROW_BLOCK, T)
    nq = T // bq
    grid = (4, nq, nq)

    def body(*refs):
        ((q_ref, k_ref, v_ref, do_ref, lse_ref, dl_ref), (dq_hbm, dk_ref, dv_ref), (dq_sc, dk_sc, dv_sc, sem),
         pctx) = _split_refs(refs, 6, 3, 4, plan)
        _plan_start(plan, pctx, grid)
        hp, j, i = pl.program_id(0), pl.program_id(1), pl.program_id(2)
        first = lax.broadcasted_iota(jnp.int32, (bq, 128), 1) < 64

        @pl.when((j == 0) & (i == 0))
        def _():
            dq_sc[...] = jnp.zeros_like(dq_sc)

        @pl.when(i == j)
        def _():
            dk_sc[...] = jnp.zeros_like(dk_sc)
            dv_sc[...] = jnp.zeros_like(dv_sc)

        def tile(r0, nr, nc, masked):
            rs, cs = slice(r0, r0 + nr), slice(0, nc)
            vp = v_ref[cs, :]
            do = do_ref[rs, :]
            lanes = first[rs, :]
            for h in range(2):
                sl = slice(h * 128, (h + 1) * 128)
                qh, kh = q_ref[rs, sl], k_ref[cs, sl]
                s = lax.dot_general(qh, kh, NT_DIMS, preferred_element_type=F32)
                p = jnp.exp(s - lse_ref[rs, h * 64:h * 64 + 1])
                if masked:
                    rows = r0 + lax.broadcasted_iota(jnp.int32, (nr, nc), 0)
                    cols = lax.broadcasted_iota(jnp.int32, (nr, nc), 1)
                    p = jnp.where(cols <= rows, p, 0.0)
                do_h = jnp.where(lanes if h == 0 else jnp.logical_not(lanes), do, jnp.zeros_like(do))
                dv_sc[cs, :] += lax.dot_general(p.astype(BF16), do_h, TN_DIMS, preferred_element_type=F32)
                dp = lax.dot_general(do_h, vp, NT_DIMS, preferred_element_type=F32)
                ds = (p * (dp - dl_ref[rs, h * 64:h * 64 + 1])).astype(BF16)
                dq_sc[i, rs, sl] += jnp.dot(ds, kh, preferred_element_type=F32)
                dk_sc[cs, sl] += lax.dot_general(ds, qh, TN_DIMS, preferred_element_type=F32)

        @pl.when(i > j)
        def _():
            tile(0, bq, bq, False)

        @pl.when(i == j)
        def _():
            tile(0, bq // 2, bq // 2, True)
            tile(bq // 2, bq // 2, bq, True)

        @pl.when(i == nq - 1)
        def _():
            dk_ref[...] = dk_sc[...]
            dv_ref[...] = dv_sc[...]

        @pl.when((j == nq - 1) & (i == nq - 1))
        def _():
            cp = pltpu.make_async_copy(dq_sc, dq_hbm.at[hp], sem)
            cp.start()
            cp.wait()

        _plan_wait(plan, pctx, grid)

    qi = lambda hp, j, i: (jnp.maximum(i, j), hp)
    kj = lambda hp, j, i: (j, hp)
    p_in, p_ospec, p_oshape, p_scr, p_alias = _plan_io(plan, 6, 3)
    return pl.pallas_call(
        body, name="flash_bwd", grid=grid,
        in_specs=[pl.BlockSpec((bq, 256), qi), pl.BlockSpec((bq, 256), kj), pl.BlockSpec((bq, 128), kj),
                  pl.BlockSpec((bq, 128), qi), pl.BlockSpec((bq, 128), qi), pl.BlockSpec((bq, 128), qi)]
        + [_ANY] * len(p_in),
        out_specs=[_ANY, pl.BlockSpec((bq, 256), kj), pl.BlockSpec((bq, 128), kj)] + p_ospec,
        out_shape=[jax.ShapeDtypeStruct((4, nq, bq, 256), F32), jax.ShapeDtypeStruct((T, 1024), F32),
                   jax.ShapeDtypeStruct((T, 512), F32)] + p_oshape,
        scratch_shapes=[pltpu.VMEM((nq, bq, 256), F32), pltpu.VMEM((bq, 256), F32), pltpu.VMEM((bq, 128), F32),
                        pltpu.SemaphoreType.DMA] + p_scr,
        input_output_aliases=p_alias, compiler_params=_params(56, 3),
    )(*_hbm(q, k, v, do_b, lse, delta), *p_in)


def _mla_bwd(z0, dq4, dk, dv, gq, gkv, wq, wk, wv, rc, rs1, rs2, plan=None):
    T = z0.shape[0]
    tm = min(ROW_BLOCK, T)
    HW = HEADS * 128
    grid = (T // tm,)
    dq4 = dq4.reshape(4, T, 256)

    def body(*refs):
        ((cq_ref, ckv_ref, dq_ref, dk_ref, dv_ref, gq_ref, gkv_ref, wq_ref, wk_ref, wv_ref, c_ref, s1_ref, s2_ref),
         (dc_ref, dkr_ref, dwq_ref, dwk_ref, dwv_ref, dgq_ref, dgkv_ref), _, pctx) = _split_refs(refs, 13, 7, 0, plan)
        _plan_start(plan, pctx, grid)

        @pl.when(pl.program_id(0) == 0)
        def _():
            for ref in (dwq_ref, dwk_ref, dwv_ref, dgq_ref, dgkv_ref):
                ref[...] = jnp.zeros_like(ref)

        c, s1, s2 = c_ref[...], s1_ref[...], s2_ref[...]
        lane = lax.broadcasted_iota(jnp.int32, (tm, 128), 1)
        nq, xq, rq = _rms(cq_ref[...], gq_ref[...])
        nkv, xkv, rkv = _rms(ckv_ref[...], gkv_ref[...])
        nq_b, nkv_b = nq.astype(BF16), nkv.astype(BF16)

        dq_parts, dk_parts = [], []
        dkr = jnp.zeros((tm, 128), F32)
        for h in range(HEADS):
            blk = dq_ref[h // 2, :, (h % 2) * 128:(h % 2 + 1) * 128] * QK_SCALE
            dq_parts.append(_rope_t(blk, c, s1, s2).astype(BF16))
            kb = dk_ref[:, h * 128:(h + 1) * 128]
            dk_parts.append(jnp.where(lane < NOPE, kb, 0.0).astype(BF16))
            dkr = dkr + kb
        dq_b = jnp.concatenate(dq_parts, axis=1)
        dk_b = jnp.concatenate(dk_parts, axis=1)
        dv_b = dv_ref[...].astype(BF16)

        dwq_ref[...] += lax.dot_general(nq_b, dq_b, TN_DIMS, preferred_element_type=F32)
        dwk_ref[...] += lax.dot_general(nkv_b, dk_b, TN_DIMS, preferred_element_type=F32)
        dwv_ref[...] += lax.dot_general(nkv_b, dv_b, TN_DIMS, preferred_element_type=F32)
        dnq = lax.dot_general(dq_b, wq_ref[...], NT_DIMS, preferred_element_type=F32)
        dnkv = (lax.dot_general(dk_b, wk_ref[...], NT_DIMS, preferred_element_type=F32)
                + lax.dot_general(dv_b, wv_ref[...], NT_DIMS, preferred_element_type=F32))

        def rms_bwd(dn, xhat, rstd, g):
            dxh = dn * g
            return rstd * (dxh - xhat * jnp.mean(dxh * xhat, -1, keepdims=True))

        dc_ref[:, :256] = rms_bwd(dnq, xq, rq, gq_ref[...]).astype(BF16)
        dc_ref[:, 256:] = rms_bwd(dnkv, xkv, rkv, gkv_ref[...]).astype(BF16)
        dgq_ref[...] += _fold8(dnq * xq)
        dgkv_ref[...] += _fold8(dnkv * xkv)
        dkr = pltpu.roll(_rope_t(dkr, c, s1, s2), 64, 1)
        dkr_ref[...] = jnp.where(lane < ROPE, dkr, 0.0).astype(BF16)
        _plan_wait(plan, pctx, grid)

    full = lambda shape: pl.BlockSpec(shape, lambda i: (0,) * len(shape))
    tab = pl.BlockSpec((tm, 128), lambda i: (i, 0))
    p_in, p_ospec, p_oshape, p_scr, p_alias = _plan_io(plan, 13, 7)
    return pl.pallas_call(
        body, name="mla_bwd", grid=grid,
        in_specs=[pl.BlockSpec((tm, 256), lambda i: (i, 0)), pl.BlockSpec((tm, 256), lambda i: (i, 1)),
                  pl.BlockSpec((4, tm, 256), lambda i: (0, i, 0)),
                  pl.BlockSpec((tm, HW), lambda i: (i, 0)), pl.BlockSpec((tm, 512), lambda i: (i, 0)),
                  full((1, 256)), full((1, 256)), full((256, HW)), full((256, HW)), full((256, 512)), tab, tab, tab]
        + [_ANY] * len(p_in),
        out_specs=[pl.BlockSpec((tm, 512), lambda i: (i, 0)), tab, full((256, HW)), full((256, HW)),
                   full((256, 512)), full((8, 256)), full((8, 256))] + p_ospec,
        out_shape=[jax.ShapeDtypeStruct((T, 512), BF16), jax.ShapeDtypeStruct((T, 128), BF16),
                   jax.ShapeDtypeStruct((256, HW), F32), jax.ShapeDtypeStruct((256, HW), F32),
                   jax.ShapeDtypeStruct((256, 512), F32), jax.ShapeDtypeStruct((8, 256), F32),
                   jax.ShapeDtypeStruct((8, 256), F32)] + p_oshape,
        scratch_shapes=p_scr, input_output_aliases=p_alias, compiler_params=_params(48, 1),
    )(*_hbm(z0, z0, dq4, dk, dv, gq, gkv, wq, wk, wv, rc, rs1, rs2), *p_in)


def _sgu_fwd(z0, a_out, ln_g, ln_b, w, b_t):
    T = z0.shape[0]
    tm = min(ROW_BLOCK, T)
    W = SGU_G * SGU_C

    def body(u_ref, v_ref, a_ref, g_ref, b_ref, w_ref, bt_ref, o_ref):
        o_ref[:, :W] = a_ref[...]
        ug = _gelu(u_ref[...])
        xhat, _ = _ln_stats(_gelu(v_ref[...]))
        vn = (xhat * g_ref[...] + b_ref[...]).astype(BF16)
        tril = lax.broadcasted_iota(jnp.int32, (SGU_C, SGU_C), 0) >= lax.broadcasted_iota(jnp.int32, (SGU_C, SGU_C), 1)
        for g in range(SGU_G):
            cs = slice(g * SGU_C, (g + 1) * SGU_C)
            wg = jnp.where(tril, w_ref[g], 0.0).astype(BF16)
            bcol = bt_ref[:, g:g + 1]
            for c in range(tm // SGU_C):
                rs = slice(c * SGU_C, (c + 1) * SGU_C)
                mixed = jnp.dot(wg, vn[rs, cs], preferred_element_type=F32) + bcol
                o_ref[rs, W + g * SGU_C:W + (g + 1) * SGU_C] = (ug[rs, cs] * mixed).astype(BF16)

    full = lambda shape: pl.BlockSpec(shape, lambda i: (0,) * len(shape))
    return pl.pallas_call(
        body, name="sgu_fwd", grid=(T // tm,),
        in_specs=[pl.BlockSpec((tm, W), lambda i: (i, 1)), pl.BlockSpec((tm, W), lambda i: (i, 2)),
                  pl.BlockSpec((tm, W), lambda i: (i, 0)),
                  full((1, W)), full((1, W)), full((SGU_G, SGU_C, SGU_C)), full((SGU_C, SGU_G))],
        out_specs=pl.BlockSpec((tm, 2 * W), lambda i: (i, 0)),
        out_shape=jax.ShapeDtypeStruct((T, 2 * W), BF16),
        compiler_params=_params(32, 1),
    )(z0, z0, a_out, ln_g, ln_b, w, b_t)


def _sgu_bwd(z0, dmix, dc, dkr, ln_g, ln_b, w, b_t):
    T = z0.shape[0]
    tm = min(ROW_BLOCK, T)
    W = SGU_G * SGU_C

    def body(u_ref, v_ref, do_ref, dc_ref, dkr_ref, g_ref, b_ref, w_ref, bt_ref, dz_ref, dw_ref, db_ref, dlg_ref,
             dlb_ref):
        @pl.when(pl.program_id(0) == 0)
        def _():
            for ref in (dw_ref, db_ref, dlg_ref, dlb_ref):
                ref[...] = jnp.zeros_like(ref)

        dz_ref[:, :W] = dc_ref[...]
        dz_ref[:, 3 * W:] = dkr_ref[...]

        u, v, dout = u_ref[...], v_ref[...], do_ref[...]
        ug = _gelu(u)
        xhat, rstd = _ln_stats(_gelu(v))
        vn = (xhat * g_ref[...] + b_ref[...]).astype(BF16)
        dmixed = dout * ug
        dmixed_b = dmixed.astype(BF16)
        tril = lax.broadcasted_iota(jnp.int32, (SGU_C, SGU_C), 0) >= lax.broadcasted_iota(jnp.int32, (SGU_C, SGU_C), 1)
        lane = lax.broadcasted_iota(jnp.int32, (SGU_C, SGU_C), 1)
        dvn_cols = []
        for g in range(SGU_G):
            cs = slice(g * SGU_C, (g + 1) * SGU_C)
            wg = jnp.where(tril, w_ref[g], 0.0).astype(BF16)
            bcol = bt_ref[:, g:g + 1]
            dw_g = jnp.zeros((SGU_C, SGU_C), F32)
            db_g = jnp.zeros((SGU_C, 1), F32)
            dvn_rows = []
            for c in range(tm // SGU_C):
                rs = slice(c * SGU_C, (c + 1) * SGU_C)
                mixed = jnp.dot(wg, vn[rs, cs], preferred_element_type=F32) + bcol
                dz_ref[rs, W + g * SGU_C:W + (g + 1) * SGU_C] = (dout[rs, cs] * mixed * _gelu_grad(u[rs, cs])).astype(BF16)
                dm = dmixed_b[rs, cs]
                dw_g = dw_g + lax.dot_general(dm, vn[rs, cs], NT_DIMS, preferred_element_type=F32)
                db_g = db_g + jnp.sum(dmixed[rs, cs], axis=1, keepdims=True)
                dvn_rows.append(lax.dot_general(wg, dm, TN_DIMS, preferred_element_type=F32))
            dw_ref[g] += jnp.where(tril, dw_g, 0.0)
            db_ref[...] += jnp.where(lane == g, db_g, 0.0)
            dvn_cols.append(jnp.concatenate(dvn_rows, axis=0))
        dvn = jnp.concatenate(dvn_cols, axis=1)
        dxh = dvn * g_ref[...]
        m1 = jnp.mean(dxh, -1, keepdims=True)
        m2 = jnp.mean(dxh * xhat, -1, keepdims=True)
        dvg = rstd * (dxh - m1 - xhat * m2)
        dz_ref[:, 2 * W:3 * W] = (dvg * _gelu_grad(v)).astype(BF16)
        dlg_ref[...] += _fold8(dvn * xhat)
        dlb_ref[...] += _fold8(dvn)

    full = lambda shape: pl.BlockSpec(shape, lambda i: (0,) * len(shape))
    return pl.pallas_call(
        body, name="sgu_bwd", grid=(T // tm,),
        in_specs=[pl.BlockSpec((tm, W), lambda i: (i, 1)), pl.BlockSpec((tm, W), lambda i: (i, 2)),
                  pl.BlockSpec((tm, W), lambda i: (i, 1)), pl.BlockSpec((tm, W), lambda i: (i, 0)),
                  pl.BlockSpec((tm, 128), lambda i: (i, 0)),
                  full((1, W)), full((1, W)), full((SGU_G, SGU_C, SGU_C)), full((SGU_C, SGU_G))],
        out_specs=[pl.BlockSpec((tm, 3 * W + 128), lambda i: (i, 0)), full((SGU_G, SGU_C, SGU_C)),
                   full((SGU_C, SGU_C)), full((8, W)), full((8, W))],
        out_shape=[jax.ShapeDtypeStruct((T, 3 * W + 128), BF16), jax.ShapeDtypeStruct((SGU_G, SGU_C, SGU_C), F32),
                   jax.ShapeDtypeStruct((SGU_C, SGU_C), F32), jax.ShapeDtypeStruct((8, W), F32),
                   jax.ShapeDtypeStruct((8, W), F32)],
        compiler_params=_params(40, 1),
    )(z0, z0, dmix, dc, dkr, ln_g, ln_b, w, b_t)


def _hg_lower_bound(lb_ref):
    a0, a1 = lb_ref[0:1, :], lb_ref[1:2, :]
    m = jnp.maximum(a0, a1)
    e0, e1 = jnp.exp(a0 - m), jnp.exp(a1 - m)
    return e1 / (e0 + e1)


def _running_sum(x, reverse=False):
    n = x.shape[0]
    row = lax.broadcasted_iota(jnp.int32, x.shape, 0)
    s = 1
    while s < n:
        if reverse:
            x = x + jnp.where(row < n - s, pltpu.roll(x, n - s, 0), 0.0)
        else:
            x = x + jnp.where(row >= s, pltpu.roll(x, s, 0), 0.0)
        s *= 2
    return x


def _hg_chunk(qc, fc, lb):
    C = HG_CHUNK
    rows = lax.broadcasted_iota(jnp.int32, (C, C), 0)
    cols = lax.broadcasted_iota(jnp.int32, (C, C), 1)
    rowid = lax.broadcasted_iota(jnp.int32, (C, 128), 0)
    sq, sg = _sigmoid(qc), _sigmoid(fc)
    qf = qc * sq
    gate = lb + (1.0 - lb) * sg
    kk = 1.0 - gate
    lg = jnp.log(gate)
    bcum = _running_sum(lg)
    b_mid = jnp.sum(jnp.where(rowid < C // 2, lg, 0.0), axis=0, keepdims=True)
    b_last = jnp.sum(lg, axis=0, keepdims=True)
    eq, ek, e, eh = jnp.exp(bcum - b_mid), jnp.exp(b_mid - bcum), jnp.exp(bcum), jnp.exp(b_last - bcum)
    qt, kt, qe, khat = qf * eq, kk * ek, qf * e, kk * eh
    a = lax.dot_general(qt.astype(BF16), kt.astype(BF16), NT_DIMS, preferred_element_type=F32)
    a = jnp.where(rows >= cols, a, 0.0)
    return dict(sq=sq, sg=sg, gate=gate, kk=kk, eq=eq, ek=ek, e=e, eh=eh, qt=qt, kt=kt, qe=qe, khat=khat, a=a,
                e_last=jnp.exp(b_last), tril=rows >= cols, rowid=rowid)


def _hgrn_fwd(z4, hg_lb, gnorm):
    T = z4.shape[1]
    tb = min(ROW_BLOCK, T)
    C = HG_CHUNK
    ncb = tb // C
    HPB = HG_HEADS_PER_STEP

    def body(q_ref, f_ref, i_ref, g_ref, lb_ref, gn_ref, y_ref, o_ref, st_ref, st_sc):
        @pl.when(pl.program_id(1) == 0)
        def _():
            st_sc[...] = jnp.zeros_like(st_sc)

        def chunk(c, carry):
            rs = pl.ds(pl.multiple_of(c * C, C), C)
            for hh in range(HPB):
                hs = slice(hh * 128, (hh + 1) * 128)
                lb = _hg_lower_bound(lb_ref.at[:, hs])
                v_b = i_ref[rs, hs].astype(BF16)
                gc = g_ref[rs, hs]
                x = _hg_chunk(q_ref[rs, hs], f_ref[rs, hs], lb)
                st = st_sc[hh]
                st_ref[hh, c] = st
                o = (jnp.dot(x["a"].astype(BF16), v_b, preferred_element_type=F32)
                     + lax.dot_general(x["qe"].astype(BF16), st.astype(BF16), NT_DIMS, preferred_element_type=F32))
                st_sc[hh] = st * x["e_last"] + lax.dot_general(v_b, x["khat"].astype(BF16), TN_DIMS,
                                                               preferred_element_type=F32)
                o_ref[rs, hs] = o
                n = o * lax.rsqrt(jnp.mean(o * o, -1, keepdims=True) + EPS)
                y_ref[rs, hs] = (n * gn_ref[:, hs] * (gc * _sigmoid(gc))).astype(BF16)
            return carry

        lax.fori_loop(0, ncb, chunk, 0, unroll=4)

    W = 128 * HPB
    zb = lambda k: pl.BlockSpec((None, tb, W), lambda h, t: (k, t, h))
    out = pl.BlockSpec((tb, W), lambda h, t: (t, h))
    return pl.pallas_call(
        body, name="hgrn_fwd", grid=(HEADS // HPB, T // tb),
        in_specs=[zb(0), zb(1), zb(2), zb(3), pl.BlockSpec((2, W), lambda h, t: (0, h)),
                  pl.BlockSpec((1, W), lambda h, t: (0, h))],
        out_specs=[out, out, pl.BlockSpec((HPB, ncb, 128, 128), lambda h, t: (h, t, 0, 0))],
        out_shape=[jax.ShapeDtypeStruct((T, D), BF16), jax.ShapeDtypeStruct((T, D), F32),
                   jax.ShapeDtypeStruct((HEADS, T // C, 128, 128), F32)],
        scratch_shapes=[pltpu.VMEM((HPB, 128, 128), F32)],
        compiler_params=_params(48, 2),
    )(*_hbm(z4, z4, z4, z4, hg_lb, gnorm))


def _hgrn_bwd(z4, o_raw, dy, states, hg_lb, gnorm):
    T = z4.shape[1]
    tb = min(ROW_BLOCK, T)
    C = HG_CHUNK
    ncb = tb // C
    nt = T // tb
    HPB = HG_HEADS_PER_STEP

    def body(q_ref, f_ref, i_ref, g_ref, o_ref, dy_ref, st_ref, lb_ref, gn_ref, dz_ref, dlb_ref, dgn_ref, dst_sc):
        @pl.when(pl.program_id(1) == 0)
        def _():
            dst_sc[...] = jnp.zeros_like(dst_sc)
            dlb_ref[...] = jnp.zeros_like(dlb_ref)
            dgn_ref[...] = jnp.zeros_like(dgn_ref)

        def chunk(cc, carry):
            for hh in range(HPB):
                one_head(ncb - 1 - cc, hh, slice(hh * 128, (hh + 1) * 128))
            return carry

        def one_head(c, hh, hs):
            rs = pl.ds(pl.multiple_of(c * C, C), C)
            lb = _hg_lower_bound(lb_ref.at[:, hs])
            gn = gn_ref[:, hs]
            qc, gc = q_ref[rs, hs], g_ref[rs, hs]
            v_b = i_ref[rs, hs].astype(BF16)
            x = _hg_chunk(qc, f_ref[rs, hs], lb)
            st, dst = st_ref[hh, c], dst_sc[hh]
            st_b, dst_b = st.astype(BF16), dst.astype(BF16)
            o, dyc = o_ref[rs, hs], dy_ref[rs, hs]
            sgg = _sigmoid(gc)
            sil = gc * sgg
            rstd = lax.rsqrt(jnp.mean(o * o, -1, keepdims=True) + EPS)
            n = o * rstd
            dgn_ref[:, hs] += _fold8(dyc * n * sil)
            dn = dyc * gn * sil
            do = rstd * (dn - n * jnp.mean(dn * n, -1, keepdims=True))
            dg = dyc * n * gn * (sgg * (1.0 + gc * (1.0 - sgg)))
            do_b = do.astype(BF16)
            da = jnp.where(x["tril"], lax.dot_general(do_b, v_b, NT_DIMS, preferred_element_type=F32), 0.0).astype(BF16)
            qt_b, kt_b, qe_b, khat_b = (x[n_].astype(BF16) for n_ in ("qt", "kt", "qe", "khat"))
            dv = (lax.dot_general(x["a"].astype(BF16), do_b, TN_DIMS, preferred_element_type=F32)
                  + lax.dot_general(khat_b, dst_b, NT_DIMS, preferred_element_type=F32))
            dqt = jnp.dot(da, kt_b, preferred_element_type=F32)
            dqe = jnp.dot(do_b, st_b, preferred_element_type=F32)
            dkt = lax.dot_general(da, qt_b, TN_DIMS, preferred_element_type=F32)
            dkhat = jnp.dot(v_b, dst_b, preferred_element_type=F32)
            dst_sc[hh] = lax.dot_general(do_b, qe_b, TN_DIMS, preferred_element_type=F32) + dst * x["e_last"]
            de_last = jnp.sum(st * dst, axis=0, keepdims=True)
            dqf = dqt * x["eq"] + dqe * x["e"]
            dkk = dkt * x["ek"] + dkhat * x["eh"]
            dkh_kh = dkhat * x["khat"]
            db = dqt * qt_b.astype(F32) - dkt * kt_b.astype(F32) + dqe * x["qe"] - dkh_kh
            db_last = jnp.sum(dkh_kh, axis=0, keepdims=True) + de_last * x["e_last"]
            db = db + jnp.where(x["rowid"] == C - 1, db_last, 0.0)
            dlg = _running_sum(db, reverse=True)
            dgate = dlg / x["gate"] - dkk
            sg, sq = x["sg"], x["sq"]
            dlb_ref[:, hs] += _fold8(dgate * (1.0 - sg)) * (lb * (1.0 - lb))
            dz_ref[0, rs, hs] = (dqf * (sq * (1.0 + qc * (1.0 - sq)))).astype(BF16)
            dz_ref[1, rs, hs] = (dgate * (1.0 - lb) * sg * (1.0 - sg)).astype(BF16)
            dz_ref[2, rs, hs] = dv.astype(BF16)
            dz_ref[3, rs, hs] = dg.astype(BF16)

        lax.fori_loop(0, ncb, chunk, 0, unroll=4)

    W = 128 * HPB
    zb = lambda k: pl.BlockSpec((None, tb, W), lambda h, t: (k, nt - 1 - t, h))
    blk = pl.BlockSpec((tb, W), lambda h, t: (nt - 1 - t, h))
    acc = pl.BlockSpec((8, W), lambda h, t: (0, h))
    return pl.pallas_call(
        body, name="hgrn_bwd", grid=(HEADS // HPB, nt),
        in_specs=[zb(0), zb(1), zb(2), zb(3), blk, blk,
                  pl.BlockSpec((HPB, ncb, 128, 128), lambda h, t: (h, nt - 1 - t, 0, 0)),
                  pl.BlockSpec((2, W), lambda h, t: (0, h)), pl.BlockSpec((1, W), lambda h, t: (0, h))],
        out_specs=[pl.BlockSpec((4, tb, W), lambda h, t: (0, nt - 1 - t, h)), acc, acc],
        out_shape=[jax.ShapeDtypeStruct((4, T, D), BF16), jax.ShapeDtypeStruct((8, D), F32),
                   jax.ShapeDtypeStruct((8, D), F32)],
        scratch_shapes=[pltpu.VMEM((HPB, 128, 128), F32)],
        compiler_params=_params(48, 2),
    )(*_hbm(z4, z4, z4, z4, o_raw, dy, states, hg_lb, gnorm))


def _adamw(w, g, m, v, *, name):
    R, L = w.shape
    tr = R if R <= 512 else 512
    assert R % tr == 0
    blk = pl.BlockSpec((tr, L), lambda i: (i, 0))
    c1, c2 = 1.0 - B1 ** STEP, 1.0 - B2 ** STEP

    def body(w_ref, g_ref, m_ref, v_ref, d_ref, mo_ref, vo_ref):
        g_ = g_ref[...]
        m_ = B1 * m_ref[...] + (1.0 - B1) * g_
        v_ = B2 * v_ref[...] + (1.0 - B2) * (g_ * g_)
        d_ref[...] = -LR * ((m_ / c1) / (jnp.sqrt(v_ / c2) + ADAM_EPS) + WD * w_ref[...])
        mo_ref[...] = m_
        vo_ref[...] = v_

    sds = jax.ShapeDtypeStruct((R, L), F32)
    return pl.pallas_call(
        body, name=name, grid=(R // tr,), in_specs=[blk] * 4, out_specs=[blk] * 3, out_shape=[sds] * 3,
        compiler_params=_params(32, 1),
    )(w, g, m, v)


def _adamw_rows(w, m, v, gbufs, row0, *, name, plan=None):
    L, R, C = w.shape
    tr = 256
    assert R % tr == 0 and row0 % tr == 0 and len(gbufs) == L
    grid = (L, R // tr)
    blk = pl.BlockSpec((None, tr, C), lambda l, i: (l, i, 0))
    gblks = [pl.BlockSpec((tr, C), lambda l, i, k=k: (row0 // tr + jnp.where(l == k, i, 0), 0)) for k in range(L)]
    c1, c2 = 1.0 - B1 ** STEP, 1.0 - B2 ** STEP

    def body(*refs):
        ins, (go_ref, d_ref, mo_ref, vo_ref), _, pctx = _split_refs(refs, 3 + L, 4, 0, plan)
        w_ref, m_ref, v_ref = ins[:3]
        g_refs = ins[3:]
        _plan_start(plan, pctx, grid)
        g_ = g_refs[0][...]
        for l in range(1, L):
            g_ = jnp.where(pl.program_id(0) == l, g_refs[l][...], g_)
        m_ = B1 * m_ref[...] + (1.0 - B1) * g_
        v_ = B2 * v_ref[...] + (1.0 - B2) * (g_ * g_)
        go_ref[...] = g_
        d_ref[...] = -LR * ((m_ / c1) / (jnp.sqrt(v_ / c2) + ADAM_EPS) + WD * w_ref[...])
        mo_ref[...] = m_
        vo_ref[...] = v_
        _plan_wait(plan, pctx, grid)

    sds = jax.ShapeDtypeStruct((L, R, C), F32)
    p_in, p_ospec, p_oshape, p_scr, p_alias = _plan_io(plan, 3 + L, 4)
    return pl.pallas_call(
        body, name=name, grid=grid, in_specs=[blk] * 3 + gblks + [_ANY] * len(p_in),
        out_specs=[blk] * 4 + p_ospec, out_shape=[sds] * 4 + p_oshape, scratch_shapes=p_scr,
        input_output_aliases=p_alias, compiler_params=_params(32, 2),
    )(*_hbm(w, m, v, *gbufs), *p_in)


def _add_pairs(g, theirs, ids, *, name):
    n, R, L = theirs.shape
    tr = math.gcd(R, 128)
    nb = R // tr

    def body(ids_ref, a_ref, b_ref, o_ref):
        o_ref[...] = (a_ref[...].astype(F32) + b_ref[...].astype(F32)).astype(BF16)

    blk = pl.BlockSpec((n, tr, L), lambda i, ids: (0, i, 0))
    return pl.pallas_call(
        body, name=name, out_shape=jax.ShapeDtypeStruct((n, R, L), BF16),
        grid_spec=pltpu.PrefetchScalarGridSpec(
            num_scalar_prefetch=1, grid=(nb,),
            in_specs=[pl.BlockSpec((n, tr, L), lambda i, ids: (0, ids[1] * nb + i, 0)), blk], out_specs=blk),
        compiler_params=_params(16, 1),
    )(ids, g, theirs)


def _sum_chips(pair, parts, ids, *, name):
    _, R, L = parts.shape
    tr = math.gcd(R, 128)

    def body(ids_ref, o_ref, r_ref, out_ref):
        out_ref[...] = ((o_ref[...].astype(F32) + r_ref[0].astype(F32)) + r_ref[1].astype(F32)) + r_ref[2].astype(F32)

    return pl.pallas_call(
        body, name=name, out_shape=jax.ShapeDtypeStruct((2, R, L), F32),
        grid_spec=pltpu.PrefetchScalarGridSpec(
            num_scalar_prefetch=1, grid=(R // tr,),
            in_specs=[pl.BlockSpec((None, tr, L), lambda i, ids: (ids[0], i, 0)),
                      pl.BlockSpec((3, tr, L), lambda i, ids: (0, i, 0))],
            out_specs=pl.BlockSpec((None, tr, L), lambda i, ids: (ids[1], i, 0))),
        compiler_params=_params(32, 1),
    )(ids, pair, parts)


def _mesh_ids():
    x, y, c = _mesh_pos()
    return jnp.stack([2 * x + y, c]).astype(jnp.int32)


def _place_shard(rows, ids, *, name):
    R, L = rows.shape
    tr = 128

    def body(ids_ref, in_ref, out_ref):
        out_ref[...] = in_ref[...].astype(BF16)

    return pl.pallas_call(
        body, name=name, out_shape=jax.ShapeDtypeStruct((4, R, L), BF16),
        grid_spec=pltpu.PrefetchScalarGridSpec(
            num_scalar_prefetch=1, grid=(R // tr,), in_specs=[pl.BlockSpec((tr, L), lambda i, ids: (i, 0))],
            out_specs=pl.BlockSpec((None, tr, L), lambda i, ids: (ids[0], i, 0))),
        compiler_params=_params(16, 1),
    )(ids, rows)


def _remote(src, dst, send_sem, recv_sem, to):
    return pltpu.make_async_remote_copy(src_ref=src, dst_ref=dst, send_sem=send_sem, recv_sem=recv_sem,
                                        device_id=to, device_id_type=MESH_IDS)


def _rows(ref, lead, start, size):
    return ref.at[tuple(pl.ds(0, n) for n in ref.shape[:lead]) + (pl.ds(start, size),)]


def _other_chips():
    x, y, _ = _mesh_pos()
    return [(1 - x, y), (x, 1 - y), (1 - x, 1 - y)]


def _plan_gather_ici(bufs):
    n = len(bufs)

    def copies(outs, send, recv):
        x, y, c = _mesh_pos()
        res = []
        for b in range(n):
            half = bufs[b].shape[1] // 2
            mine = _rows(outs[b].at[2 * x + y], 0, c * half, half)
            for j, (cx, cy) in enumerate(_other_chips()):
                res.append((_remote(mine, mine, send(3 * b + j), recv(3 * b + j), (cx, cy, c)),
                            _remote(mine, _rows(outs[b].at[2 * cx + cy], 0, c * half, half),
                                    send(3 * b + j), recv(3 * b + j), (x, y, c))))
        return res

    def start(ins, outs, send, recv, loc):
        for out_cp, _ in copies(outs, send, recv):
            out_cp.start()

    def wait(ins, outs, send, recv, loc):
        for out_cp, in_cp in copies(outs, send, recv):
            in_cp.wait_recv()
            out_cp.wait_send()

    outs = [jax.ShapeDtypeStruct(b.shape, b.dtype) for b in bufs]
    return _Plan(bufs, outs, 3 * n, 0, start, wait, aliases={b: b for b in range(n)})


def _plan_gather_forward(bufs):
    n = len(bufs)

    def copies(outs, send, recv):
        x, y, c = _mesh_pos()
        res = []
        for b in range(n):
            half = bufs[b].shape[1] // 2
            for j, (cx, cy) in enumerate(_other_chips()):
                slot = outs[b].at[2 * cx + cy]
                res.append((_remote(_rows(slot, 0, c * half, half), _rows(slot, 0, c * half, half),
                                    send(3 * b + j), recv(3 * b + j), (x, y, 1 - c)),
                            _remote(_rows(slot, 0, c * half, half), _rows(slot, 0, (1 - c) * half, half),
                                    send(3 * b + j), recv(3 * b + j), (x, y, c))))
        return res

    def start(ins, outs, send, recv, loc):
        for out_cp, _ in copies(outs, send, recv):
            out_cp.start()

    def wait(ins, outs, send, recv, loc):
        for out_cp, in_cp in copies(outs, send, recv):
            in_cp.wait_recv()
            out_cp.wait_send()

    outs = [jax.ShapeDtypeStruct(b.shape, b.dtype) for b in bufs]
    return _Plan(bufs, outs, 3 * n, 0, start, wait, aliases={b: b for b in range(n)})


def _plan_pair_swap(g):
    half = g.shape[1] // 2

    def copy(ins, outs, send, recv, loc):
        x, y, c = _mesh_pos()
        return _remote(_rows(ins[0], 1, (1 - c) * half, half), outs[0], send(0), recv(0), (x, y, 1 - c))

    return _Plan([g], [jax.ShapeDtypeStruct((4, half, g.shape[2]), g.dtype)], 1, 0,
                 lambda *a: copy(*a).start(), lambda *a: copy(*a).wait())


def _plan_pair_gather(buf):
    def copies(ins, outs, send, recv, loc):
        x, y, c = _mesh_pos()
        return (_remote(outs[0].at[c], outs[0].at[c], send(0), recv(0), (x, y, 1 - c)),
                _remote(outs[0].at[c], outs[0].at[1 - c], send(0), recv(0), (x, y, c)))

    def wait(*a):
        out_cp, in_cp = copies(*a)
        in_cp.wait_recv()
        out_cp.wait_send()

    return _Plan([buf], [jax.ShapeDtypeStruct(buf.shape, buf.dtype)], 1, 0, lambda *a: copies(*a)[0].start(), wait,
                 aliases={0: 0})


def _plan_chip_scatter(p):
    def copies(ins, outs, send, recv, loc):
        _, _, c = _mesh_pos()
        return [_remote(ins[0].at[2 * cx + cy], outs[0].at[j], send(j), recv(j), (cx, cy, c))
                for j, (cx, cy) in enumerate(_other_chips())]

    def start(*a):
        for cp in copies(*a):
            cp.start()

    def wait(*a):
        for cp in copies(*a):
            cp.wait()

    return _Plan([p], [jax.ShapeDtypeStruct((3,) + p.shape[1:], p.dtype)], 3, 0, start, wait)


def _plan_exchange_all(vec):
    def copies(ins, outs, send, recv, loc):
        x, y, c = _mesh_pos()
        return [_remote(ins[0], outs[0].at[r - 1], send(r - 1), recv(r - 1), (x ^ (r >> 2), y ^ ((r >> 1) & 1), c ^ (r & 1)))
                for r in range(1, 8)]

    def start(*a):
        for cp in copies(*a):
            cp.start()

    def wait(*a):
        for cp in copies(*a):
            cp.wait()

    return _Plan([vec], [jax.ShapeDtypeStruct((7,) + vec.shape, vec.dtype)], 7, 0, start, wait)


SMALL_LAYOUT = {
    "mla_gq": (0, 1, 256, (1, 256)), "mla_gkv": (1, 1, 256, (1, 256)), "sgu_ln_g": (2, 1, 512, (1, 512)),
    "sgu_ln_b": (3, 1, 512, (1, 512)), "sgu_w": (4, 64, 1024, (64, 1024)), "sgu_b": (68, 1, 512, (1, 512)),
    "hg_lb": (69, 2, 1024, (2, 1024)), "hg_gnorm": (71, 1, 1024, (1, 256)), "ln1_g": (72, 2, 1024, (2, 1024)),
    "ln1_b": (74, 2, 1024, (2, 1024)), "ln2_g": (76, 2, 1024, (2, 1024)), "ln2_b": (78, 2, 1024, (2, 1024)),
}


def _small_pack(dgq, dgkv, dslg, dslb, dsw, dsb, dlb, dgn, ln_parts, sq_err):
    flat_ln = [p for pair in ln_parts for p in pair]

    def body(*refs):
        gq_ref, gkv_ref, slg_ref, slb_ref, sw_ref, sb_ref, lb_ref, gn_ref = refs[:8]
        ln_refs, err_ref, out_ref, t_sc = refs[8:16], refs[16], refs[17], refs[18]
        s8 = lambda ref: jnp.sum(ref[...], axis=0, keepdims=True)
        out_ref[...] = jnp.zeros_like(out_ref)
        out_ref[0:1, 0:256] = s8(gq_ref)
        out_ref[1:2, 0:256] = s8(gkv_ref)
        out_ref[2:3, 0:512] = s8(slg_ref)
        out_ref[3:4, 0:512] = s8(slb_ref)
        out_ref[4:68, :] = sw_ref[...]
        t_sc[...] = sb_ref[...].T
        for g in range(SGU_G):
            out_ref[68:69, g * SGU_C:(g + 1) * SGU_C] = t_sc[g:g + 1, :]
        d_lb1 = s8(lb_ref)
        out_ref[69:70, :] = -d_lb1
        out_ref[70:71, :] = d_lb1
        out_ref[71:72, :] = s8(gn_ref)
        for k, ref in enumerate(ln_refs):
            out_ref[72 + k:73 + k, :] = s8(ref)
        out_ref[0:1, 1023:1024] = jnp.sum(s8(err_ref), axis=1, keepdims=True) * (0.5 / D)

    vm = pl.BlockSpec(memory_space=pltpu.VMEM)
    return pl.pallas_call(
        body, name="small_grad_pack", in_specs=[vm] * 17, out_specs=vm,
        out_shape=jax.ShapeDtypeStruct((SMALL_ROWS, 1024), F32), scratch_shapes=[pltpu.VMEM((SGU_C, SGU_C), F32)],
        compiler_params=_params(16),
    )(dgq, dgkv, dslg, dslb, dsw.reshape(64, 1024), dsb, dlb, dgn, *flat_ln, sq_err)


def _small_update(vec, others, ids, w, m, v):
    names = list(SMALL_LAYOUT)
    n = len(names)
    c1, c2 = 1.0 - B1 ** STEP, 1.0 - B2 ** STEP
    have_others = others is not None

    def body(*refs):
        ids_ref, v_ref = refs[0], refs[1]
        k = 2 + have_others
        w_refs, m_refs, v_refs = refs[k:k + n], refs[k + n:k + 2 * n], refs[k + 2 * n:k + 3 * n]
        outs = refs[k + 3 * n:]
        row0_ref, tot_sc = outs[0], outs[-1]
        total = v_ref[...]
        if have_others:
            me = 2 * ids_ref[0] + ids_ref[1]
            total = None
            for d in range(8):
                rel = d ^ me
                term = jnp.where(rel == 0, v_ref[...], refs[2][jnp.maximum(rel - 1, 0)])
                total = term if total is None else total + term
        tot_sc[...] = total
        row0_ref[...] = tot_sc[0:1, :]
        for i, name in enumerate(names):
            r0, nr, width, _ = SMALL_LAYOUT[name]
            if name == "hg_gnorm":
                g_ = tot_sc[r0:r0 + 1, 0:256]
                for chip in range(1, 4):
                    g_ = jnp.where(ids_ref[0] == chip, tot_sc[r0:r0 + 1, chip * 256:(chip + 1) * 256], g_)
            else:
                g_ = tot_sc[r0:r0 + nr, 0:width]
            m_ = B1 * m_refs[i][...] + (1.0 - B1) * g_
            v_ = B2 * v_refs[i][...] + (1.0 - B2) * (g_ * g_)
            go, do, mo, vo = outs[1 + 4 * i:5 + 4 * i]
            go[...] = g_
            do[...] = -LR * ((m_ / c1) / (jnp.sqrt(v_ / c2) + ADAM_EPS) + WD * w_refs[i][...])
            mo[...] = m_
            vo[...] = v_

    full = lambda shape: pl.BlockSpec(shape, lambda i, ids, nd=len(shape): (0,) * nd)
    kshapes = [SMALL_LAYOUT[name][3] for name in names]
    operands = [vec] + ([others] if have_others else []) + [d[name] for d in (w, m, v) for name in names]
    out_shapes = [jax.ShapeDtypeStruct((1, 1024), F32)] + [jax.ShapeDtypeStruct(s, F32) for s in kshapes for _ in range(4)]
    res = pl.pallas_call(
        body, name="small_update", out_shape=out_shapes,
        grid_spec=pltpu.PrefetchScalarGridSpec(
            num_scalar_prefetch=1, grid=(1,), in_specs=[full(o.shape) for o in operands],
            out_specs=[full(s.shape) for s in out_shapes],
            scratch_shapes=[pltpu.VMEM((SMALL_ROWS, 1024), F32)]),
        compiler_params=_params(32, 1),
    )(ids, *operands)
    return res[0], {name: tuple(res[1 + 4 * i:5 + 4 * i]) for i, name in enumerate(names)}


ROWS_L1, ROWS_L0, ROWS_ODD_W = 3328, 2048, 384
ODD_PARTS = (("w_out_e", (256, 1024)), ("w_in_e", (1024, 392)), ("w_qb", (256, 192)), ("w_kvb", (256, 256)))
ODD_W_PARTS = tuple(p for p in ODD_PARTS if p[0] != "w_in_e")


def _odd_rows(parts, dtype, layout, total, gnorm=None):
    rows = [parts[n].reshape(-1, 1024).astype(dtype) for n, _ in layout]
    used = sum(r.shape[0] for r in rows)
    if gnorm is not None:
        bits = lax.bitcast_convert_type(gnorm.reshape(-1), BF16).reshape(1, 512)
        rows.append(jnp.pad(bits, ((0, 0), (0, 512))))
        used += 1
    rows.append(jnp.zeros((total - used, 1024), dtype))
    return jnp.concatenate(rows, axis=0)


def _odd_unrows(buf, layout, with_gnorm=False):
    out, off = {}, 0
    for n, shape in layout:
        nr = math.prod(shape) // 1024
        out[n] = buf[off:off + nr].reshape(shape)
        off += nr
    if with_gnorm:
        out["hg_gnorm"] = lax.bitcast_convert_type(buf[off, :512].reshape(256, 2), F32).reshape(1, 256)
    return out


def _rope_tables(positions):
    half = ROPE // 2
    inv_freq = ROPE_BASE ** (-jnp.arange(half, dtype=F32) / half)
    ang = positions.astype(F32).reshape(-1, 1) * inv_freq
    cos, sin = jnp.cos(ang), jnp.sin(ang)
    T = ang.shape[0]
    one, z16, z32 = jnp.ones((T, NOPE), F32), jnp.zeros((T, half), F32), jnp.zeros((T, 32), F32)
    z64 = jnp.zeros((T, NOPE), F32)
    c = jnp.concatenate([one, cos, cos, z32], axis=1)
    s1 = jnp.concatenate([z64, -sin, z16, z32], axis=1)
    s2 = jnp.concatenate([z64, z16, sin, z32], axis=1)
    return c, s1, s2


def _local_step(x, positions, tgt, odd, bufs, P, exchange):
    T = x.shape[0]
    row = lambda a: a.reshape(1, -1)
    rc, rs1, rs2 = _rope_tables(positions)
    blk = lambda f: pl.BlockSpec((None, D, D), f)

    w_in_e = odd["w_in_e"]
    w_in = jnp.concatenate([w_in_e[:, :512], w_in_e[:, 544:1568], w_in_e[:, 512:544], jnp.zeros((D, 96), BF16)], axis=1)
    wq = jnp.pad(odd["w_qb"].reshape(256, HEADS, NOPE + ROPE), ((0, 0), (0, 0), (0, 32))).reshape(256, HEADS * 128)
    kvb = odd["w_kvb"].reshape(256, HEADS, NOPE + VDIM)
    wk = jnp.pad(kvb[:, :, :NOPE], ((0, 0), (0, 0), (0, 64))).reshape(256, HEADS * 128)
    wv = kvb[:, :, NOPE:].reshape(256, HEADS * VDIM)
    w_out_e = odd["w_out_e"]
    sgu_w = P["sgu_w"][0]
    sgu_bt = P["sgu_b"][0].T
    gq, gkv = P["mla_gq"], P["mla_gkv"]
    gnorm = P["hg_gnorm"]

    z0 = _matmul(x, w_in, name="in_proj_e", M=T, N=1664, K=D, tn=1664)[0]
    q, k, v = _mla_prep(z0, gq, gkv, wq, wk, wv, rc, rs1, rs2)
    if exchange:
        ids = _mesh_ids()
        placed = [_place_shard(b, ids, name=f"place_shard_{l}") for l, b in enumerate(bufs)]
        a_out, lse, wga, wgb = _flash_fwd(q, k, v, plan=_plan_gather_ici(placed[:2]))
    else:
        a_out, lse = _flash_fwd(q, k, v)
        wga, wgb, wgc = bufs
    mix0 = _sgu_fwd(z0, a_out, P["sgu_ln_g"], P["sgu_ln_b"], sgu_w, sgu_bt)
    res = _proj_ln(mix0, w_out_e, x, row(P["ln1_g"][0]), row(P["ln1_b"][0]), name="out_proj_ln_e",
                   plan=_plan_gather_forward([wga, wgb]) if exchange else None)
    r1, h1b = res[:2]
    if exchange:
        wga, wgb = res[2:]
    ln = lambda name, l: (row(P[name + "_g"][l]), row(P[name + "_b"][l]))
    res = _ffn_ln(h1b, wga, r1, *ln("ln2", 0), name="ffn_ln_0", prev_ln=ln("ln1", 0),
                  plan=_plan_gather_ici(placed[2:]) if exchange else None)
    ra0, r2, h2b = res[:3]
    z4 = _matmul(h2b, wgb, name="in_proj_o", M=T, N=4 * D, K=D, b_spec=blk(lambda i, j, k: (j, 0, 0)),
                 out_shape=jax.ShapeDtypeStruct((4, T, D), F32),
                 o_spec=pl.BlockSpec((None, min(MM_ROWS, T), D), lambda i, j, k: (j, i, 0)))[0]
    y1, o_raw, states = _hgrn_fwd(z4, P["hg_lb"], gnorm)
    res2 = _proj_ln(y1, wgb, r2, *ln("ln1", 1), name="out_proj_ln_o", prev_ln=ln("ln2", 0), w_rowblk=4,
                    plan=_plan_gather_forward([res[3]]) if exchange else None)
    r3, h3b = res2[:2]
    if exchange:
        wgc = res2[2]
    ra1, r4, _ = _ffn_ln(h3b, wgc, r3, *ln("ln2", 1), name="ffn_ln_1", prev_ln=ln("ln1", 1))

    ln1_g, ln1_b, ln2_g, ln2_b = [None, None], [None, None], [None, None], [None, None]
    sq_err_parts = []

    def ffn_bwd(l, dh, r_out, ra, h_mid_b, g2, wg, rows, plan=None, loss_head=()):
        dr, dr_b, dg, db, *sq_err = _ln_bwd(dh, r_out, row(g2), name=f"ln2_bwd_{l}", loss_head=loss_head)
        sq_err_parts.extend(sq_err)
        ln2_g[l], ln2_b[l] = dg, db
        da, *extra = _matmul(dr_b, wg, tb=True, mul=ra, out_dtype=BF16, name=f"ffn_da_{l}", M=T, N=4 * D, K=D,
                             b_spec=blk(lambda i, j, k: (j, 1, 0)), plan=plan)
        gbuf = _matmul(ra, dr_b, ta=True, a_sq=True, name=f"ffn_dw2_{l}", M=4 * D, N=D, K=T, tm=1024, tk=DW_TOKENS // 2,
                       out_shape=jax.ShapeDtypeStruct((4, rows, D), BF16), o_spec=blk(lambda i, j, k: (i, 1, 0)))[0]
        gbuf = _matmul(h_mid_b, da, ta=True, name=f"ffn_dw1_{l}", M=D, N=4 * D, K=T, tm=1024, tk=DW_TOKENS, into=gbuf,
                       out_shape=jax.ShapeDtypeStruct((4, rows, D), BF16), o_spec=blk(lambda i, j, k: (j, 0, 0)))[0]
        dh_mid = _matmul(da, wg, tb=True, add=dr, add_scale=ALPHA, name=f"ffn_dh_{l}", M=T, N=D, K=4 * D, tk=2 * D,
                         b_spec=pl.BlockSpec((2, D, D), lambda i, j, k: (k, 0, 0)))[0]
        return dh_mid, gbuf, extra

    dh3, g1, _ = ffn_bwd(1, None, r4, ra1, h3b, P["ln2_g"][1], wgc, ROWS_L1, loss_head=(row(P["ln2_b"][1]), tgt))
    loss_parts = sq_err_parts[0]
    dr3, dr3_b, dg, db = _ln_bwd(dh3, r3, row(P["ln1_g"][1]), name="ln1_bwd_1")
    ln1_g[1], ln1_b[1] = dg, db
    g1_sds = jax.ShapeDtypeStruct((4, ROWS_L1, D), BF16)
    g1 = _matmul(y1, dr3_b, ta=True, name="dw_out_o", M=D, N=D, K=T, tm=256, tk=DW_TOKENS, into=g1, out_shape=g1_sds,
                 o_spec=pl.BlockSpec((None, 256, D), lambda i, j, k: (i, 12, 0)))[0]
    dmix1 = _matmul(dr3_b, wgb, tb=True, name="dmix_o", M=T, N=D, K=D, b_spec=_rows4_spec(4, 3), b_merge=(D, D))[0]
    dz4, dlb, dgn = _hgrn_bwd(z4, o_raw, dmix1, states, P["hg_lb"], gnorm)
    g1 = _matmul(h2b, dz4, ta=True, name="dw_in_o", M=D, N=4 * D, K=T, tm=1024, tk=DW_TOKENS, into=g1, out_shape=g1_sds,
                 b_spec=pl.BlockSpec((None, min(DW_TOKENS, T), D), lambda i, j, k: (j, k, 0)),
                 o_spec=blk(lambda i, j, k: (j, 2, 0)))[0]
    dh2 = _matmul(dz4, wgb, tb=True, add=dr3, add_scale=ALPHA, name="dh_in_o", M=T, N=D, K=4 * D, tk=2 * D,
                  a_spec=pl.BlockSpec((2, min(MM_ROWS, T), D), lambda i, j, k: (k, i, 0)),
                  b_spec=pl.BlockSpec((2, D, D), lambda i, j, k: (k, 0, 0)))[0]

    dh1, g0, swapped1 = ffn_bwd(0, dh2, r2, ra0, h1b, P["ln2_g"][0], wga, ROWS_L0,
                                plan=_plan_pair_swap(g1) if exchange else None)
    dr1, dr1_b, dg, db = _ln_bwd(dh1, r1, row(P["ln1_g"][0]), name="ln1_bwd_0")
    ln1_g[0], ln1_b[0] = dg, db
    godd = {"w_out_e": _matmul(mix0, dr1_b, ta=True, name="dw_out_e", M=D, N=D, K=T, tm=1024, tk=DW_TOKENS)[0]}
    dmix0, *swapped0 = _matmul(dr1_b, w_out_e, tb=True, name="dmix_e", M=T, N=D, K=D,
                               plan=_plan_pair_swap(g0) if exchange else None)
    delta, do_b = _attn_delta(dmix0, a_out)
    if exchange:
        pair1 = _add_pairs(g1, swapped1[0], ids, name="grad_pair_add_1")
        pair0 = _add_pairs(g0, swapped0[0], ids, name="grad_pair_add_0")
        dq4, dk, dv, parts0, parts1 = _flash_bwd(
            q, k, v, do_b, lse, delta, plan=_join_plans([_plan_chip_scatter(pair0), _plan_chip_scatter(pair1)]))
        half0 = _sum_chips(pair0, parts0, ids, name="grad_chip_sum_0")
        half1 = _sum_chips(pair1, parts1, ids, name="grad_chip_sum_1")
        dc, dkr, dwq, dwk, dwv, dgq, dgkv, g0, g1 = _mla_bwd(
            z0, dq4, dk, dv, gq, gkv, wq, wk, wv, rc, rs1, rs2,
            plan=_join_plans([_plan_pair_gather(half0), _plan_pair_gather(half1)]))
        g0, g1 = g0.reshape(ROWS_L0, D), g1.reshape(ROWS_L1, D)
    else:
        dq4, dk, dv = _flash_bwd(q, k, v, do_b, lse, delta)
        dc, dkr, dwq, dwk, dwv, dgq, dgkv = _mla_bwd(z0, dq4, dk, dv, gq, gkv, wq, wk, wv, rc, rs1, rs2)
    dz0, dsw, dsb, dslg, dslb = _sgu_bwd(z0, dmix0, dc, dkr, P["sgu_ln_g"], P["sgu_ln_b"], sgu_w, sgu_bt)
    small_vec = _small_pack(dgq, dgkv, dslg, dslb, dsw, dsb, dlb, dgn, [ln1_g, ln1_b, ln2_g, ln2_b], loss_parts)
    dw_in, *small_others = _matmul(x, dz0, ta=True, name="dw_in_e", M=D, N=1664, K=T, tm=1024, tn=1664,
                                   tk=DW_TOKENS // 4, plan=_plan_exchange_all(small_vec) if exchange else None)
    godd["w_in_e"] = jnp.concatenate([dw_in[:, :512], dw_in[:, 1536:1568], dw_in[:, 512:1536]], axis=1)
    godd["w_qb"] = dwq.reshape(256, HEADS, 128)[:, :, :NOPE + ROPE].reshape(256, HEADS * (NOPE + ROPE))
    godd["w_kvb"] = jnp.concatenate([dwk.reshape(256, HEADS, 128)[:, :, :NOPE], dwv.reshape(256, HEADS, VDIM)],
                                    axis=2).reshape(256, HEADS * (NOPE + VDIM))
    odd_plan = None
    if exchange:
        by_chip = [_odd_rows({"w_out_e": jnp.split(godd["w_out_e"], 4, axis=0)[j],
                              **{n: jnp.split(godd[n], 4, axis=1)[j] for n in ("w_qb", "w_kvb")}}, BF16,
                             ODD_W_PARTS, ROWS_ODD_W)
                   for j in range(4)]
        bufs_odd = [godd["w_in_e"].reshape(D, 4, 392).transpose(1, 0, 2).astype(BF16), jnp.stack(by_chip)]
        theirs = _run_plan(_join_plans([_plan_pair_swap(b) for b in bufs_odd]), name="odd_pair_swap")
        odd_pairs = [_add_pairs(b, t, ids, name=f"odd_pair_add_{k}") for k, (b, t) in enumerate(zip(bufs_odd, theirs))]
        odd_plan = _join_plans([_plan_chip_scatter(p) for p in odd_pairs])
    grad_x, *odd_parts = _matmul(dz0, w_in, tb=True, add=dr1, add_scale=ALPHA, name="dx", M=T, N=D, K=1664, tk=1664,
                                 plan=odd_plan)
    if exchange:
        godd = (odd_pairs, odd_parts)
    return grad_x, g0, g1, godd, small_vec, (small_others[0] if exchange else None)


WEIGHTS = ['w_in_e', 'mla_gq', 'mla_gkv', 'w_qb', 'w_kvb', 'sgu_ln_g', 'sgu_ln_b', 'sgu_w', 'sgu_b', 'w_out_e',
           'w_in_o', 'hg_lb', 'hg_gnorm', 'w_out_o', 'ln1_g', 'ln1_b', 'w_ff1', 'w_ff2', 'ln2_g', 'ln2_b']


def kernel(x, positions, w_in_e, mla_gq, mla_gkv, w_qb, w_kvb, sgu_ln_g, sgu_ln_b, sgu_w, sgu_b, w_out_e, w_in_o, hg_lb, hg_gnorm, w_out_o, ln1_g, ln1_b, w_ff1, w_ff2, ln2_g, ln2_b, loss_target, m_w_in_e, m_mla_gq, m_mla_gkv, m_w_qb, m_w_kvb, m_sgu_ln_g, m_sgu_ln_b, m_sgu_w, m_sgu_b, m_w_out_e, m_w_in_o, m_hg_lb, m_hg_gnorm, m_w_out_o, m_ln1_g, m_ln1_b, m_w_ff1, m_w_ff2, m_ln2_g, m_ln2_b, v_w_in_e, v_mla_gq, v_mla_gkv, v_w_qb, v_w_kvb, v_sgu_ln_g, v_sgu_ln_b, v_sgu_w, v_sgu_b, v_w_out_e, v_w_in_o, v_hg_lb, v_hg_gnorm, v_w_out_o, v_ln1_g, v_ln1_b, v_w_ff1, v_w_ff2, v_ln2_g, v_ln2_b):
    args = dict(locals())
    w = {n: args[n] for n in WEIGHTS}
    m = {n: args["m_" + n] for n in WEIGHTS}
    v = {n: args["v_" + n] for n in WEIGHTS}
    cx, cy, cc = _mesh_pos()
    chip = 2 * cx + cy

    odd_shard = _odd_rows({"w_out_e": w_out_e[0], "w_qb": w_qb[0], "w_kvb": w_kvb[0]}, BF16, ODD_W_PARTS, ROWS_ODD_W,
                          gnorm=hg_gnorm)
    ids = _mesh_ids()
    placed = [_place_shard(w_in_e[0], ids, name="place_shard_in_e"), _place_shard(odd_shard, ids, name="place_shard_odd")]
    gathered = _run_plan(_plan_gather_forward(_run_plan(_plan_gather_ici(placed), name="odd_gather")),
                         name="odd_gather_forward")
    per_chip = [_odd_unrows(gathered[1][j], ODD_W_PARTS, with_gnorm=True) for j in range(4)]
    odd = {"w_out_e": jnp.concatenate([p["w_out_e"] for p in per_chip], axis=0),
           "w_in_e": jnp.concatenate([gathered[0][j] for j in range(4)], axis=1)}
    for n in ("w_qb", "w_kvb"):
        odd[n] = jnp.concatenate([p[n] for p in per_chip], axis=1)
    small = {n: w[n] for n in SMALL_LAYOUT if n != "hg_gnorm"}
    small["hg_gnorm"] = jnp.concatenate([p["hg_gnorm"] for p in per_chip], axis=1)
    shard_rows = (jnp.concatenate([w_ff1[0], w_ff2[0]], axis=0).astype(BF16),
                  jnp.concatenate([w_in_o[0], w_out_o[0]], axis=0).astype(BF16),
                  jnp.concatenate([w_ff1[1], w_ff2[1]], axis=0).astype(BF16))

    grad_x, g_l0, g_l1, godd, small_vec, small_others = _local_step(
        x[0], positions[0], loss_target[0], odd, shard_rows, small, True)

    sums = [_sum_chips(pair, parts, ids, name=f"odd_chip_sum_{k}") for k, (pair, parts) in enumerate(zip(*godd))]
    g_in_e, g_rest = _run_plan(_join_plans([_plan_pair_gather(s) for s in sums]), name="odd_pair_gather")
    g_odd = _odd_unrows(g_rest.reshape(ROWS_ODD_W, 1024), ODD_W_PARTS)
    g_odd["w_in_e"] = g_in_e.reshape(D, 392)

    to_kernel = lambda d: {n: d[n].reshape(SMALL_LAYOUT[n][3]) for n in SMALL_LAYOUT}
    first_row, small_out = _small_update(small_vec, small_others, ids, to_kernel(w), to_kernel(m), to_kernel(v))
    loss = first_row[0, 1023]
    grads, delta, new_m, new_v = {}, {}, {}, {}
    for n, res in small_out.items():
        grads[n], delta[n], new_m[n], new_v[n] = (r.reshape(w[n].shape) for r in res)

    for n, bufs_, row0 in (("w_ff1", [g_l0, g_l1], 0), ("w_ff2", [g_l0, g_l1], 1024), ("w_in_o", [g_l1], 2048),
                           ("w_out_o", [g_l1], 3072)):
        grads[n], delta[n], new_m[n], new_v[n] = _adamw_rows(w[n], m[n], v[n], bufs_, row0, name=f"adamw_{n}")
    for n, _ in ODD_PARTS:
        grads[n] = g_odd[n][None]
        d_, m_, v_ = _adamw(w[n][0], g_odd[n], m[n][0], v[n][0], name=f"adamw_{n}")
        delta[n], new_m[n], new_v[n] = d_[None], m_[None], v_[None]

    return (loss, grad_x[None], *[grads[n] for n in WEIGHTS], *[delta[n] for n in WEIGHTS],
            *[new_m[n] for n in WEIGHTS], *[new_v[n] for n in WEIGHTS])
```

```python
import math

import jax
import jax.numpy as jnp
from jax import lax
from jax.experimental import pallas as pl
from jax.experimental.pallas import tpu as pltpu

F32 = jnp.float32
BF16 = jnp.bfloat16
MESH_IDS = pl.DeviceIdType.MESH

D = 1024
DEPTH = 2
HEADS = 8
NOPE, ROPE, VDIM = 64, 32, 64
QK_SCALE = (NOPE + ROPE) ** -0.5
ROPE_BASE = 10000.0
SGU_G, SGU_C = 4, 128
HG_CHUNK = 64
HG_HEADS_PER_STEP = 8
ALPHA = (2 * DEPTH) ** 0.25
EPS = 1e-5
LR, B1, B2, ADAM_EPS, WD, STEP = 0.001, 0.9, 0.999, 1e-08, 0.01, 10
GELU_C = math.sqrt(2.0 / math.pi)
GELU_A = 0.044715
MB = 1024 * 1024
ROW_BLOCK = 512
SMALL_ROWS = 80

NT_DIMS = (((1,), (1,)), ((), ()))
TN_DIMS = (((0,), (0,)), ((), ()))


def _params(vmem_mb, n_axes=0):
    kw = dict(vmem_limit_bytes=vmem_mb * MB)
    if n_axes:
        kw["dimension_semantics"] = ("arbitrary",) * n_axes
    return pltpu.CompilerParams(**kw)


_ANY = pl.BlockSpec(memory_space=pltpu.HBM)


def _mesh_pos():
    return lax.axis_index("x"), lax.axis_index("y"), lax.axis_index("c")


def _hbm(*arrays):
    return tuple(pltpu.with_memory_space_constraint(a, pltpu.HBM) if a.size >= 2 ** 18 else a for a in arrays)


class _Plan:
    def __init__(self, ins, outs, n_remote, n_local, start, wait, aliases=None):
        self.ins, self.outs, self.n_remote, self.n_local = list(ins), list(outs), n_remote, n_local
        self.start, self.wait, self.aliases = start, wait, dict(aliases or {})


def _join_plans(plans):
    ins, outs, aliases, parts = [], [], {}, []
    nr = nl = 0
    for p in plans:
        parts.append((p, len(ins), len(outs), nr, nl))
        aliases.update({len(ins) + i: len(outs) + o for i, o in p.aliases.items()})
        ins += p.ins
        outs += p.outs
        nr += p.n_remote
        nl += p.n_local

    def run(which):
        def go(in_refs, out_refs, send, recv, loc):
            for p, i0, o0, r0, l0 in parts:
                getattr(p, which)(in_refs[i0:i0 + len(p.ins)], out_refs[o0:o0 + len(p.outs)],
                                  lambda i, r0=r0: send(r0 + i), lambda i, r0=r0: recv(r0 + i),
                                  lambda i, l0=l0: loc(l0 + i))
        return go

    return _Plan(ins, outs, nr, nl, run("start"), run("wait"), aliases)


def _plan_io(plan, n_in, n_out):
    if plan is None:
        return [], [], [], [], {}
    sems = [pltpu.SemaphoreType.DMA((max(plan.n_remote, 1),)), pltpu.SemaphoreType.DMA((max(plan.n_remote, 1),)),
            pltpu.SemaphoreType.DMA((max(plan.n_local, 1),))]
    aliases = {n_in + i: n_out + o for i, o in plan.aliases.items()}
    return plan.ins, [_ANY] * len(plan.outs), plan.outs, sems, aliases


def _split_refs(refs, n_in, n_out, n_scr, plan):
    p_in, p_out = (len(plan.ins), len(plan.outs)) if plan is not None else (0, 0)
    refs = list(refs)
    ins, refs = refs[:n_in], refs[n_in:]
    pins, refs = refs[:p_in], refs[p_in:]
    outs, refs = refs[:n_out], refs[n_out:]
    pouts, refs = refs[:p_out], refs[p_out:]
    scr, psem = refs[:n_scr], refs[n_scr:]
    psem = tuple((lambda i, s=s: s.at[i]) for s in psem)
    return ins, outs, scr, (pins, pouts, psem)


def _grid_edge(grid, last):
    cond = None
    for ax, n in enumerate(grid):
        c = pl.program_id(ax) == (n - 1 if last else 0)
        cond = c if cond is None else cond & c
    return cond


def _plan_start(plan, pctx, grid):
    if plan is not None:
        pins, pouts, psem = pctx
        pl.when(_grid_edge(grid, False))(lambda: plan.start(pins, pouts, *psem))


def _plan_wait(plan, pctx, grid):
    if plan is not None:
        pins, pouts, psem = pctx
        pl.when(_grid_edge(grid, True))(lambda: plan.wait(pins, pouts, *psem))


def _run_plan(plan, *, name):
    def body(*refs):
        _, _, _, (pins, pouts, psem) = _split_refs(refs, 0, 0, 0, plan)
        plan.start(pins, pouts, *psem)
        plan.wait(pins, pouts, *psem)

    p_in, p_ospec, p_oshape, p_scr, p_alias = _plan_io(plan, 0, 0)
    return pl.pallas_call(body, name=name, in_specs=[_ANY] * len(p_in), out_specs=p_ospec, out_shape=p_oshape,
                          scratch_shapes=p_scr, input_output_aliases=p_alias)(*p_in)


def _fold8(x):
    return x.reshape(x.shape[0] // 8, 8, x.shape[1]).sum(axis=0)


def _ln_stats(r):
    mu = jnp.mean(r, -1, keepdims=True)
    xc = r - mu
    rstd = lax.rsqrt(jnp.mean(xc * xc, -1, keepdims=True) + EPS)
    return xc * rstd, rstd


def _sigmoid(x):
    return jax.nn.sigmoid(x)


def _gelu(x):
    return 0.5 * x * (1.0 + jnp.tanh(GELU_C * (x + GELU_A * x * x * x)))


def _gelu_grad(x):
    t = jnp.tanh(GELU_C * (x + GELU_A * x * x * x))
    return 0.5 * (1.0 + t) + 0.5 * x * (1.0 - t * t) * GELU_C * (1.0 + 3.0 * GELU_A * x * x)


MM_ROWS = 1024
DW_TOKENS = 4096


def _matmul(a, b, *, name, M, N, K, ta=False, tb=False, out_dtype=F32, tm=MM_ROWS, tn=1024, tk=1024,
            a_spec=None, b_spec=None, b_merge=None, out_shape=None, o_spec=None, into=None,
            a_sq=False, mul=None, add=None, add_scale=1.0, plan=None):
    tm, tn, tk = min(tm, M), min(tn, N), min(tk, K)
    assert M % tm == 0 and N % tn == 0 and K % tk == 0
    grid = (M // tm, N // tn, K // tk)
    nk = grid[2]
    if a_spec is None:
        a_spec = pl.BlockSpec((tk, tm), lambda i, j, k: (k, i)) if ta else pl.BlockSpec((tm, tk), lambda i, j, k: (i, k))
    if b_spec is None:
        b_spec = pl.BlockSpec((tn, tk), lambda i, j, k: (j, k)) if tb else pl.BlockSpec((tk, tn), lambda i, j, k: (k, j))
    if o_spec is None:
        o_spec = pl.BlockSpec((tm, tn), lambda i, j, k: (i, j))
        out_shape = jax.ShapeDtypeStruct((M, N), out_dtype)
    e_spec = pl.BlockSpec((tm, tn), lambda i, j, k: (i, j))
    dims = (((0 if ta else 1,), (1 if tb else 0,)), ((), ()))
    extra = [e for e in (mul, add, into) if e is not None]
    n_in = 2 + len(extra)

    def body(*refs):
        ins, outs, scr, pctx = _split_refs(refs, n_in, 1, 1 if nk > 1 else 0, plan)
        a_ref, b_ref = ins[0], ins[1]
        rest = list(ins[2:])
        mul_ref = rest.pop(0) if mul is not None else None
        add_ref = rest.pop(0) if add is not None else None
        o_ref = outs[0]
        _plan_start(plan, pctx, grid)
        av = a_ref[...].astype(BF16)
        if a_sq:
            av = av * av
        bv = b_ref[...]
        if b_merge is not None:
            bv = bv.reshape(b_merge)
        if bv.ndim == 3:
            w = av.shape[-1] // (1 if av.ndim == 3 else bv.shape[0])
            a_parts = [av[s] if av.ndim == 3 else av[:, s * w:(s + 1) * w] for s in range(bv.shape[0])]
            p = sum(lax.dot_general(a_parts[s], bv[s], dims, preferred_element_type=F32) for s in range(bv.shape[0]))
        else:
            p = lax.dot_general(av, bv, dims, preferred_element_type=F32)

        def finish(r):
            if mul_ref is not None:
                r = r * (2.0 * mul_ref[...].astype(F32))
            if add_ref is not None:
                r = r + add_scale * add_ref[...]
            o_ref[...] = r.astype(o_ref.dtype)

        if nk == 1:
            finish(p)
        else:
            acc_ref = scr[0]
            k = pl.program_id(2)

            @pl.when(k == 0)
            def _():
                acc_ref[...] = p

            @pl.when(k > 0)
            def _():
                acc_ref[...] += p

            @pl.when(k == nk - 1)
            def _():
                finish(acc_ref[...])

        _plan_wait(plan, pctx, grid)

    p_in, p_ospec, p_oshape, p_scr, p_alias = _plan_io(plan, n_in, 1)
    aliases = dict(p_alias)
    if into is not None:
        aliases[n_in - 1] = 0
    return pl.pallas_call(
        body, name=name, grid=grid,
        in_specs=[a_spec, b_spec] + [e_spec] * (len(extra) - (into is not None)) + [_ANY] * (into is not None)
        + [_ANY] * len(p_in),
        out_specs=[o_spec] + p_ospec, out_shape=[out_shape] + p_oshape,
        scratch_shapes=([pltpu.VMEM((tm, tn), F32)] if nk > 1 else []) + p_scr,
        input_output_aliases=aliases, compiler_params=_params(48, 3),
    )(*_hbm(a, b, *extra), *p_in)


def _rows4_spec(rowblk, n_axes):
    return pl.BlockSpec((4, 256, D), lambda *_: (0, rowblk, 0))


def _residual(h_ref, prev_refs):
    if not prev_refs:
        return h_ref[...]
    xhat, _ = _ln_stats(h_ref[...])
    return xhat * prev_refs[0][...] + prev_refs[1][...]


def _proj_ln(a_b, w, h_prev, g, b, *, name, prev_ln=(), w_rowblk=None, plan=None):
    T = a_b.shape[0]
    tm = min(MM_ROWS, T)
    grid = (T // tm,)
    row = pl.BlockSpec((tm, D), lambda i: (i, 0))
    vec = pl.BlockSpec((1, D), lambda i: (0, 0))
    w_spec = pl.BlockSpec((D, D), lambda i: (0, 0)) if w_rowblk is None else _rows4_spec(w_rowblk, 1)
    n_in = 5 + len(prev_ln)

    def body(*refs):
        ins, (r_ref, hb_ref), _, pctx = _split_refs(refs, n_in, 2, 0, plan)
        a_ref, w_ref, h_ref, g_ref, b_ref = ins[:5]
        _plan_start(plan, pctx, grid)
        mix = jnp.dot(a_ref[...], w_ref[...].reshape(D, D), preferred_element_type=F32)
        r = ALPHA * _residual(h_ref, ins[5:]) + mix
        xhat, _ = _ln_stats(r)
        r_ref[...] = r
        hb_ref[...] = (xhat * g_ref[...] + b_ref[...]).astype(BF16)
        _plan_wait(plan, pctx, grid)

    p_in, p_ospec, p_oshape, p_scr, p_alias = _plan_io(plan, n_in, 2)
    return pl.pallas_call(
        body, name=name, grid=grid,
        in_specs=[row, w_spec, row, vec, vec] + [vec] * len(prev_ln) + [_ANY] * len(p_in),
        out_specs=[row, row] + p_ospec,
        out_shape=[jax.ShapeDtypeStruct((T, D), F32), jax.ShapeDtypeStruct((T, D), BF16)] + p_oshape,
        scratch_shapes=p_scr, input_output_aliases=p_alias, compiler_params=_params(40, 1),
    )(*_hbm(a_b, w, h_prev, g, b, *prev_ln), *p_in)


def _ffn_ln(h_b, wbuf, h, g, b, *, name, prev_ln=(), plan=None):
    T = h_b.shape[0]
    slots = 2
    tm, tf = min(ROW_BLOCK, T), slots * 1024
    nf = 4 // slots
    F = nf * tf
    grid = (T // tm, nf)
    row = pl.BlockSpec((tm, D), lambda i, j: (i, 0))
    vec = pl.BlockSpec((1, D), lambda i, j: (0, 0))
    n_in = 6 + len(prev_ln)

    def body(*refs):
        ins, (ra_ref, r_ref, hbo_ref), (acc_ref,), pctx = _split_refs(refs, n_in, 3, 1, plan)
        hb_ref, w1_ref, w2_ref, h_ref, g_ref, b_ref = ins[:6]
        _plan_start(plan, pctx, grid)
        j = pl.program_id(1)
        hb = hb_ref[...]
        p = None
        for s in range(slots):
            ra = jnp.maximum(jnp.dot(hb, w1_ref[s], preferred_element_type=F32), 0.0)
            ra_ref[:, s * 1024:(s + 1) * 1024] = ra.astype(BF16)
            ps = jnp.dot((ra * ra).astype(BF16), w2_ref[s], preferred_element_type=F32)
            p = ps if p is None else p + ps

        @pl.when(j == 0)
        def _():
            acc_ref[...] = p

        @pl.when(j > 0)
        def _():
            acc_ref[...] += p

        @pl.when(j == nf - 1)
        def _():
            r = ALPHA * _residual(h_ref, ins[6:]) + acc_ref[...]
            xhat, _ = _ln_stats(r)
            r_ref[...] = r
            hbo_ref[...] = (xhat * g_ref[...] + b_ref[...]).astype(BF16)

        _plan_wait(plan, pctx, grid)

    p_in, p_ospec, p_oshape, p_scr, p_alias = _plan_io(plan, n_in, 3)
    return pl.pallas_call(
        body, name=name, grid=grid,
        in_specs=[row, pl.BlockSpec((slots, D, D), lambda i, j: (j, 0, 0)),
                  pl.BlockSpec((slots, D, D), lambda i, j: (j, 1, 0)), row, vec, vec] + [vec] * len(prev_ln)
        + [_ANY] * len(p_in),
        out_specs=[pl.BlockSpec((tm, tf), lambda i, j: (i, j)), row, row] + p_ospec,
        out_shape=[jax.ShapeDtypeStruct((T, F), BF16), jax.ShapeDtypeStruct((T, D), F32),
                   jax.ShapeDtypeStruct((T, D), BF16)] + p_oshape,
        scratch_shapes=[pltpu.VMEM((tm, D), F32)] + p_scr,
        input_output_aliases=p_alias, compiler_params=_params(56, 2),
    )(*_hbm(h_b, wbuf, wbuf, h, g, b, *prev_ln), *p_in)


def _ln_bwd(dy, r, g, *, name, loss_head=()):
    T = r.shape[0]
    tm = min(ROW_BLOCK, T)
    row = pl.BlockSpec((tm, D), lambda i: (i, 0))
    vec = pl.BlockSpec((1, D), lambda i: (0, 0))
    acc = pl.BlockSpec((8, D), lambda i: (0, 0))
    operands, in_specs = ([r, g, *loss_head], [row, vec, vec, row]) if loss_head else ([r, g, dy], [row, vec, row])
    n_in = len(operands)

    def body(*refs):
        r_ref, g_ref = refs[:2]
        dr_ref, drb_ref, dg_ref, db_ref = refs[n_in:n_in + 4]

        @pl.when(pl.program_id(0) == 0)
        def _():
            for ref in refs[n_in + 2:]:
                ref[...] = jnp.zeros_like(ref)

        xhat, rstd = _ln_stats(r_ref[...])
        if loss_head:
            err = xhat * g_ref[...] + refs[2][...] - refs[3][...]
            refs[n_in + 4][...] += _fold8(err * err)
            dy_ = err * (1.0 / D)
        else:
            dy_ = refs[2][...]
        dxh = dy_ * g_ref[...]
        m1 = jnp.mean(dxh, -1, keepdims=True)
        m2 = jnp.mean(dxh * xhat, -1, keepdims=True)
        dr = rstd * (dxh - m1 - xhat * m2)
        dr_ref[...] = dr
        drb_ref[...] = dr.astype(BF16)
        dg_ref[...] += _fold8(dy_ * xhat)
        db_ref[...] += _fold8(dy_)

    n_acc = 3 if loss_head else 2
    return pl.pallas_call(
        body, name=name, grid=(T // tm,), in_specs=in_specs, out_specs=[row, row] + [acc] * n_acc,
        out_shape=[jax.ShapeDtypeStruct((T, D), F32), jax.ShapeDtypeStruct((T, D), BF16)]
        + [jax.ShapeDtypeStruct((8, D), F32)] * n_acc,
        compiler_params=_params(40, 1),
    )(*_hbm(*operands))


def _rope(x, c, s1, s2):
    return x * c + pltpu.roll(x, 112, 1) * s1 + pltpu.roll(x, 16, 1) * s2


def _rope_t(dy, c, s1, s2):
    return dy * c + pltpu.roll(dy * s1, 16, 1) + pltpu.roll(dy * s2, 112, 1)


def _rms(x, g):
    rstd = lax.rsqrt(jnp.mean(x * x, -1, keepdims=True) + EPS)
    xhat = x * rstd
    return xhat * g, xhat, rstd


def _mla_prep(z0, gq, gkv, wq, wk, wv, rc, rs1, rs2):
    T = z0.shape[0]
    tm = min(ROW_BLOCK, T)
    HW = HEADS * 128

    def body(cq_ref, ckv_ref, kr_ref, gq_ref, gkv_ref, wq_ref, wk_ref, wv_ref, c_ref, s1_ref, s2_ref,
             q_ref, k_ref, v_ref):
        nq = _rms(cq_ref[...], gq_ref[...])[0].astype(BF16)
        nkv = _rms(ckv_ref[...], gkv_ref[...])[0].astype(BF16)
        q = jnp.dot(nq, wq_ref[...], preferred_element_type=F32)
        k = jnp.dot(nkv, wk_ref[...], preferred_element_type=F32)
        v = jnp.dot(nkv, wv_ref[...], preferred_element_type=F32)
        c, s1, s2 = c_ref[...], s1_ref[...], s2_ref[...]
        kr = _rope(pltpu.roll(kr_ref[...], 64, 1), c, s1, s2)
        for h in range(HEADS):
            sl = slice(h * 128, (h + 1) * 128)
            q_ref[:, sl] = (_rope(q[:, sl], c, s1, s2) * QK_SCALE).astype(BF16)
            k_ref[:, sl] = (k[:, sl] + kr).astype(BF16)
        v_ref[...] = v.astype(BF16)

    full = lambda shape: pl.BlockSpec(shape, lambda i: (0, 0))
    tab = pl.BlockSpec((tm, 128), lambda i: (i, 0))
    return pl.pallas_call(
        body, name="mla_prep", grid=(T // tm,),
        in_specs=[pl.BlockSpec((tm, 256), lambda i: (i, 0)), pl.BlockSpec((tm, 256), lambda i: (i, 1)),
                  pl.BlockSpec((tm, 128), lambda i: (i, 12)), full((1, 256)), full((1, 256)),
                  full((256, HW)), full((256, HW)), full((256, 512)), tab, tab, tab],
        out_specs=[pl.BlockSpec((tm, HW), lambda i: (i, 0)), pl.BlockSpec((tm, HW), lambda i: (i, 0)),
                   pl.BlockSpec((tm, 512), lambda i: (i, 0))],
        out_shape=[jax.ShapeDtypeStruct((T, HW), BF16), jax.ShapeDtypeStruct((T, HW), BF16),
                   jax.ShapeDtypeStruct((T, 512), BF16)],
        compiler_params=_params(40, 1),
    )(z0, z0, z0, gq, gkv, wq, wk, wv, rc, rs1, rs2)


def _flash_fwd(q, k, v, plan=None):
    T = q.shape[0]
    bq = min(2 * ROW_BLOCK, T)
    nq = T // bq
    grid = (4, nq, nq)

    def body(*refs):
        (q_ref, k_ref, v_ref), (o_ref, lse_ref), (m_sc, acc_sc), pctx = _split_refs(refs, 3, 2, 2, plan)
        _plan_start(plan, pctx, grid)
        i, j = pl.program_id(1), pl.program_id(2)
        first = lax.broadcasted_iota(jnp.int32, (bq, 128), 1) < 64

        @pl.when(j == 0)
        def _():
            m_sc[...] = jnp.full_like(m_sc, -jnp.inf)
            acc_sc[...] = jnp.zeros_like(acc_sc)

        def tile(r0, nr, nc, masked):
            rs = slice(r0, r0 + nr)
            vp = v_ref[0:nc, :]
            lanes = first[0:nc, :]
            for h in range(2):
                sl = slice(h * 128, (h + 1) * 128)
                s = lax.dot_general(q_ref[rs, sl], k_ref[0:nc, sl], NT_DIMS, preferred_element_type=F32)
                if masked:
                    rows = r0 + lax.broadcasted_iota(jnp.int32, (nr, nc), 0)
                    cols = lax.broadcasted_iota(jnp.int32, (nr, nc), 1)
                    s = jnp.where(cols <= rows, s, -jnp.inf)
                m_prev = m_sc[h, rs, 0:1]
                m_new = jnp.maximum(m_prev, jnp.max(s, axis=1, keepdims=True))
                alpha = jnp.exp(m_prev - m_new)
                p = jnp.exp(s - m_new).astype(BF16)
                vh = jnp.where(lanes if h == 0 else jnp.logical_not(lanes), vp, jnp.ones_like(vp))
                acc_sc[h, rs, :] = acc_sc[h, rs, :] * alpha + jnp.dot(p, vh, preferred_element_type=F32)
                m_sc[h, rs, :] = jnp.broadcast_to(m_new, (nr, 128))

        @pl.when(j < i)
        def _():
            tile(0, bq, bq, False)

        @pl.when(j == i)
        def _():
            tile(0, bq, bq, True)
            a0, a1 = acc_sc[0], acc_sc[1]
            l0, l1 = pltpu.roll(a0, 64, 1), pltpu.roll(a1, 64, 1)
            o_ref[...] = jnp.where(first, a0 / l0, a1 / l1).astype(BF16)
            lse_ref[...] = jnp.where(first, m_sc[0] + jnp.log(l0), m_sc[1] + jnp.log(l1))

        _plan_wait(plan, pctx, grid)

    kv = lambda hp, i, j: (jnp.minimum(i, j), hp)
    p_in, p_ospec, p_oshape, p_scr, p_alias = _plan_io(plan, 3, 2)
    return pl.pallas_call(
        body, name="flash_fwd", grid=grid,
        in_specs=[pl.BlockSpec((bq, 256), lambda hp, i, j: (i, hp)), pl.BlockSpec((bq, 256), kv),
                  pl.BlockSpec((bq, 128), kv)] + [_ANY] * len(p_in),
        out_specs=[pl.BlockSpec((bq, 128), lambda hp, i, j: (i, hp)),
                   pl.BlockSpec((bq, 128), lambda hp, i, j: (i, hp))] + p_ospec,
        out_shape=[jax.ShapeDtypeStruct((T, 512), BF16), jax.ShapeDtypeStruct((T, 512), F32)] + p_oshape,
        scratch_shapes=[pltpu.VMEM((2, bq, 128), F32), pltpu.VMEM((2, bq, 128), F32)] + p_scr,
        input_output_aliases=p_alias, compiler_params=_params(56, 3),
    )(*_hbm(q, k, v), *p_in)


def _attn_delta(dmix, o):
    T = o.shape[0]
    tm = min(ROW_BLOCK, T)
    blk = pl.BlockSpec((tm, 512), lambda i: (i, 0))

    def body(do_ref, o_ref, delta_ref, dob_ref):
        first = lax.broadcasted_iota(jnp.int32, (tm, 128), 1) < 64
        for hp in range(4):
            sl = slice(hp * 128, (hp + 1) * 128)
            prod = do_ref[:, sl] * o_ref[:, sl].astype(F32)
            d0 = jnp.sum(jnp.where(first, prod, 0.0), axis=1, keepdims=True)
            d1 = jnp.sum(jnp.where(first, 0.0, prod), axis=1, keepdims=True)
            delta_ref[:, sl] = jnp.where(first, d0, d1)
        dob_ref[...] = do_ref[...].astype(BF16)

    return pl.pallas_call(
        body, name="attn_delta", grid=(T // tm,), in_specs=[blk, blk], out_specs=[blk, blk],
        out_shape=[jax.ShapeDtypeStruct((T, 512), F32), jax.ShapeDtypeStruct((T, 512), BF16)],
        compiler_params=_params(32, 1),
    )(dmix, o)


def _flash_bwd(q, k, v, do_b, lse, delta, plan=None):
    T = q.shape[0]
    bq = min(2 * ROW_BLOCK, T)
    nq = T // bq
    grid = (4, nq, nq)

    def body(*refs):
        ((q_ref, k_ref, v_ref, do_ref, lse_ref, dl_ref), (dq_hbm, dk_ref, dv_ref), (dq_sc, dk_sc, dv_sc, sem),
         pctx) = _split_refs(refs, 6, 3, 4, plan)
        _plan_start(plan, pctx, grid)
        hp, j, i = pl.program_id(0), pl.program_id(1), pl.program_id(2)
        first = lax.broadcasted_iota(jnp.int32, (bq, 128), 1) < 64

        @pl.when((j == 0) & (i == 0))
        def _():
            dq_sc[...] = jnp.zeros_like(dq_sc)

        @pl.when(i == j)
        def _():
            dk_sc[...] = jnp.zeros_like(dk_sc)
            dv_sc[...] = jnp.zeros_like(dv_sc)

        def tile(r0, nr, nc, masked):
            rs, cs = slice(r0, r0 + nr), slice(0, nc)
            vp = v_ref[cs, :]
            do = do_ref[rs, :]
            lanes = first[rs, :]
            for h in range(2):
                sl = slice(h * 128, (h + 1) * 128)
                qh, kh = q_ref[rs, sl], k_ref[cs, sl]
                s = lax.dot_general(qh, kh, NT_DIMS, preferred_element_type=F32)
                p = jnp.exp(s - lse_ref[rs, h * 64:h * 64 + 1])
                if masked:
                    rows = r0 + lax.broadcasted_iota(jnp.int32, (nr, nc), 0)
                    cols = lax.broadcasted_iota(jnp.int32, (nr, nc), 1)
                    p = jnp.where(cols <= rows, p, 0.0)
                do_h = jnp.where(lanes if h == 0 else jnp.logical_not(lanes), do, jnp.zeros_like(do))
                dv_sc[cs, :] += lax.dot_general(p.astype(BF16), do_h, TN_DIMS, preferred_element_type=F32)
                dp = lax.dot_general(do_h, vp, NT_DIMS, preferred_element_type=F32)
                ds = (p * (dp - dl_ref[rs, h * 64:h * 64 + 1])).astype(BF16)
                dq_sc[i, rs, sl] += jnp.dot(ds, kh, preferred_element_type=F32)
                dk_sc[cs, sl] += lax.dot_general(ds, qh, TN_DIMS, preferred_element_type=F32)

        @pl.when(i > j)
        def _():
            tile(0, bq, bq, False)

        @pl.when(i == j)
        def _():
            tile(0, bq // 2, bq // 2, True)
            tile(bq // 2, bq // 2, bq, True)

        @pl.when(i == nq - 1)
        def _():
            dk_ref[...] = dk_sc[...]
            dv_ref[...] = dv_sc[...]

        @pl.when((j == nq - 1) & (i == nq - 1))
        def _():
            cp = pltpu.make_async_copy(dq_sc, dq_hbm.at[hp], sem)
            cp.start()
            cp.wait()

        _plan_wait(plan, pctx, grid)

    qi = lambda hp, j, i: (jnp.maximum(i, j), hp)
    kj = lambda hp, j, i: (j, hp)
    p_in, p_ospec, p_oshape, p_scr, p_alias = _plan_io(plan, 6, 3)
    return pl.pallas_call(
        body, name="flash_bwd", grid=grid,
        in_specs=[pl.BlockSpec((bq, 256), qi), pl.BlockSpec((bq, 256), kj), pl.BlockSpec((bq, 128), kj),
                  pl.BlockSpec((bq, 128), qi), pl.BlockSpec((bq, 128), qi), pl.BlockSpec((bq, 128), qi)]
        + [_ANY] * len(p_in),
        out_specs=[_ANY, pl.BlockSpec((bq, 256), kj), pl.BlockSpec((bq, 128), kj)] + p_ospec,
        out_shape=[jax.ShapeDtypeStruct((4, nq, bq, 256), F32), jax.ShapeDtypeStruct((T, 1024), F32),
                   jax.ShapeDtypeStruct((T, 512), F32)] + p_oshape,
        scratch_shapes=[pltpu.VMEM((nq, bq, 256), F32), pltpu.VMEM((bq, 256), F32), pltpu.VMEM((bq, 128), F32),
                        pltpu.SemaphoreType.DMA] + p_scr,
        input_output_aliases=p_alias, compiler_params=_params(56, 3),
    )(*_hbm(q, k, v, do_b, lse, delta), *p_in)


def _mla_bwd(z0, dq4, dk, dv, gq, gkv, wq, wk, wv, rc, rs1, rs2, plan=None):
    T = z0.shape[0]
    tm = min(ROW_BLOCK, T)
    HW = HEADS * 128
    grid = (T // tm,)
    dq4 = dq4.reshape(4, T, 256)

    def body(*refs):
        ((cq_ref, ckv_ref, dq_ref, dk_ref, dv_ref, gq_ref, gkv_ref, wq_ref, wk_ref, wv_ref, c_ref, s1_ref, s2_ref),
         (dc_ref, dkr_ref, dwq_ref, dwk_ref, dwv_ref, dgq_ref, dgkv_ref), _, pctx) = _split_refs(refs, 13, 7, 0, plan)
        _plan_start(plan, pctx, grid)

        @pl.when(pl.program_id(0) == 0)
        def _():
            for ref in (dwq_ref, dwk_ref, dwv_ref, dgq_ref, dgkv_ref):
                ref[...] = jnp.zeros_like(ref)

        c, s1, s2 = c_ref[...], s1_ref[...], s2_ref[...]
        lane = lax.broadcasted_iota(jnp.int32, (tm, 128), 1)
        nq, xq, rq = _rms(cq_ref[...], gq_ref[...])
        nkv, xkv, rkv = _rms(ckv_ref[...], gkv_ref[...])
        nq_b, nkv_b = nq.astype(BF16), nkv.astype(BF16)

        dq_parts, dk_parts = [], []
        dkr = jnp.zeros((tm, 128), F32)
        for h in range(HEADS):
            blk = dq_ref[h // 2, :, (h % 2) * 128:(h % 2 + 1) * 128] * QK_SCALE
            dq_parts.append(_rope_t(blk, c, s1, s2).astype(BF16))
            kb = dk_ref[:, h * 128:(h + 1) * 128]
            dk_parts.append(jnp.where(lane < NOPE, kb, 0.0).astype(BF16))
            dkr = dkr + kb
        dq_b = jnp.concatenate(dq_parts, axis=1)
        dk_b = jnp.concatenate(dk_parts, axis=1)
        dv_b = dv_ref[...].astype(BF16)

        dwq_ref[...] += lax.dot_general(nq_b, dq_b, TN_DIMS, preferred_element_type=F32)
        dwk_ref[...] += lax.dot_general(nkv_b, dk_b, TN_DIMS, preferred_element_type=F32)
        dwv_ref[...] += lax.dot_general(nkv_b, dv_b, TN_DIMS, preferred_element_type=F32)
        dnq = lax.dot_general(dq_b, wq_ref[...], NT_DIMS, preferred_element_type=F32)
        dnkv = (lax.dot_general(dk_b, wk_ref[...], NT_DIMS, preferred_element_type=F32)
                + lax.dot_general(dv_b, wv_ref[...], NT_DIMS, preferred_element_type=F32))

        def rms_bwd(dn, xhat, rstd, g):
            dxh = dn * g
            return rstd * (dxh - xhat * jnp.mean(dxh * xhat, -1, keepdims=True))

        dc_ref[:, :256] = rms_bwd(dnq, xq, rq, gq_ref[...]).astype(BF16)
        dc_ref[:, 256:] = rms_bwd(dnkv, xkv, rkv, gkv_ref[...]).astype(BF16)
        dgq_ref[...] += _fold8(dnq * xq)
        dgkv_ref[...] += _fold8(dnkv * xkv)
        dkr = pltpu.roll(_rope_t(dkr, c, s1, s2), 64, 1)
        dkr_ref[...] = jnp.where(lane < ROPE, dkr, 0.0).astype(BF16)
        _plan_wait(plan, pctx, grid)

    full = lambda shape: pl.BlockSpec(shape, lambda i: (0,) * len(shape))
    tab = pl.BlockSpec((tm, 128), lambda i: (i, 0))
    p_in, p_ospec, p_oshape, p_scr, p_alias = _plan_io(plan, 13, 7)
    return pl.pallas_call(
        body, name="mla_bwd", grid=grid,
        in_specs=[pl.BlockSpec((tm, 256), lambda i: (i, 0)), pl.BlockSpec((tm, 256), lambda i: (i, 1)),
                  pl.BlockSpec((4, tm, 256), lambda i: (0, i, 0)),
                  pl.BlockSpec((tm, HW), lambda i: (i, 0)), pl.BlockSpec((tm, 512), lambda i: (i, 0)),
                  full((1, 256)), full((1, 256)), full((256, HW)), full((256, HW)), full((256, 512)), tab, tab, tab]
        + [_ANY] * len(p_in),
        out_specs=[pl.BlockSpec((tm, 512), lambda i: (i, 0)), tab, full((256, HW)), full((256, HW)),
                   full((256, 512)), full((8, 256)), full((8, 256))] + p_ospec,
        out_shape=[jax.ShapeDtypeStruct((T, 512), BF16), jax.ShapeDtypeStruct((T, 128), BF16),
                   jax.ShapeDtypeStruct((256, HW), F32), jax.ShapeDtypeStruct((256, HW), F32),
                   jax.ShapeDtypeStruct((256, 512), F32), jax.ShapeDtypeStruct((8, 256), F32),
                   jax.ShapeDtypeStruct((8, 256), F32)] + p_oshape,
        scratch_shapes=p_scr, input_output_aliases=p_alias, compiler_params=_params(48, 1),
    )(*_hbm(z0, z0, dq4, dk, dv, gq, gkv, wq, wk, wv, rc, rs1, rs2), *p_in)


def _sgu_fwd(z0, a_out, ln_g, ln_b, w, b_t):
    T = z0.shape[0]
    tm = min(ROW_BLOCK, T)
    W = SGU_G * SGU_C

    def body(u_ref, v_ref, a_ref, g_ref, b_ref, w_ref, bt_ref, o_ref):
        o_ref[:, :W] = a_ref[...]
        ug = _gelu(u_ref[...])
        xhat, _ = _ln_stats(_gelu(v_ref[...]))
        vn = (xhat * g_ref[...] + b_ref[...]).astype(BF16)
        tril = lax.broadcasted_iota(jnp.int32, (SGU_C, SGU_C), 0) >= lax.broadcasted_iota(jnp.int32, (SGU_C, SGU_C), 1)
        for g in range(SGU_G):
            cs = slice(g * SGU_C, (g + 1) * SGU_C)
            wg = jnp.where(tril, w_ref[g], 0.0).astype(BF16)
            bcol = bt_ref[:, g:g + 1]
            for c in range(tm // SGU_C):
                rs = slice(c * SGU_C, (c + 1) * SGU_C)
                mixed = jnp.dot(wg, vn[rs, cs], preferred_element_type=F32) + bcol
                o_ref[rs, W + g * SGU_C:W + (g + 1) * SGU_C] = (ug[rs, cs] * mixed).astype(BF16)

    full = lambda shape: pl.BlockSpec(shape, lambda i: (0,) * len(shape))
    return pl.pallas_call(
        body, name="sgu_fwd", grid=(T // tm,),
        in_specs=[pl.BlockSpec((tm, W), lambda i: (i, 1)), pl.BlockSpec((tm, W), lambda i: (i, 2)),
                  pl.BlockSpec((tm, W), lambda i: (i, 0)),
                  full((1, W)), full((1, W)), full((SGU_G, SGU_C, SGU_C)), full((SGU_C, SGU_G))],
        out_specs=pl.BlockSpec((tm, 2 * W), lambda i: (i, 0)),
        out_shape=jax.ShapeDtypeStruct((T, 2 * W), BF16),
        compiler_params=_params(32, 1),
    )(z0, z0, a_out, ln_g, ln_b, w, b_t)


def _sgu_bwd(z0, dmix, dc, dkr, ln_g, ln_b, w, b_t):
    T = z0.shape[0]
    tm = min(ROW_BLOCK, T)
    W = SGU_G * SGU_C

    def body(u_ref, v_ref, do_ref, dc_ref, dkr_ref, g_ref, b_ref, w_ref, bt_ref, dz_ref, dw_ref, db_ref, dlg_ref,
             dlb_ref):
        @pl.when(pl.program_id(0) == 0)
        def _():
            for ref in (dw_ref, db_ref, dlg_ref, dlb_ref):
                ref[...] = jnp.zeros_like(ref)

        dz_ref[:, :W] = dc_ref[...]
        dz_ref[:, 3 * W:] = dkr_ref[...]

        u, v, dout = u_ref[...], v_ref[...], do_ref[...]
        ug = _gelu(u)
        xhat, rstd = _ln_stats(_gelu(v))
        vn = (xhat * g_ref[...] + b_ref[...]).astype(BF16)
        dmixed = dout * ug
        dmixed_b = dmixed.astype(BF16)
        tril = lax.broadcasted_iota(jnp.int32, (SGU_C, SGU_C), 0) >= lax.broadcasted_iota(jnp.int32, (SGU_C, SGU_C), 1)
        lane = lax.broadcasted_iota(jnp.int32, (SGU_C, SGU_C), 1)
        dvn_cols = []
        for g in range(SGU_G):
            cs = slice(g * SGU_C, (g + 1) * SGU_C)
            wg = jnp.where(tril, w_ref[g], 0.0).astype(BF16)
            bcol = bt_ref[:, g:g + 1]
            dw_g = jnp.zeros((SGU_C, SGU_C), F32)
            db_g = jnp.zeros((SGU_C, 1), F32)
            dvn_rows = []
            for c in range(tm // SGU_C):
                rs = slice(c * SGU_C, (c + 1) * SGU_C)
                mixed = jnp.dot(wg, vn[rs, cs], preferred_element_type=F32) + bcol
                dz_ref[rs, W + g * SGU_C:W + (g + 1) * SGU_C] = (dout[rs, cs] * mixed * _gelu_grad(u[rs, cs])).astype(BF16)
                dm = dmixed_b[rs, cs]
                dw_g = dw_g + lax.dot_general(dm, vn[rs, cs], NT_DIMS, preferred_element_type=F32)
                db_g = db_g + jnp.sum(dmixed[rs, cs], axis=1, keepdims=True)
                dvn_rows.append(lax.dot_general(wg, dm, TN_DIMS, preferred_element_type=F32))
            dw_ref[g] += jnp.where(tril, dw_g, 0.0)
            db_ref[...] += jnp.where(lane == g, db_g, 0.0)
            dvn_cols.append(jnp.concatenate(dvn_rows, axis=0))
        dvn = jnp.concatenate(dvn_cols, axis=1)
        dxh = dvn * g_ref[...]
        m1 = jnp.mean(dxh, -1, keepdims=True)
        m2 = jnp.mean(dxh * xhat, -1, keepdims=True)
        dvg = rstd * (dxh - m1 - xhat * m2)
        dz_ref[:, 2 * W:3 * W] = (dvg * _gelu_grad(v)).astype(BF16)
        dlg_ref[...] += _fold8(dvn * xhat)
        dlb_ref[...] += _fold8(dvn)

    full = lambda shape: pl.BlockSpec(shape, lambda i: (0,) * len(shape))
    return pl.pallas_call(
        body, name="sgu_bwd", grid=(T // tm,),
        in_specs=[pl.BlockSpec((tm, W), lambda i: (i, 1)), pl.BlockSpec((tm, W), lambda i: (i, 2)),
                  pl.BlockSpec((tm, W), lambda i: (i, 1)), pl.BlockSpec((tm, W), lambda i: (i, 0)),
                  pl.BlockSpec((tm, 128), lambda i: (i, 0)),
                  full((1, W)), full((1, W)), full((SGU_G, SGU_C, SGU_C)), full((SGU_C, SGU_G))],
        out_specs=[pl.BlockSpec((tm, 3 * W + 128), lambda i: (i, 0)), full((SGU_G, SGU_C, SGU_C)),
                   full((SGU_C, SGU_C)), full((8, W)), full((8, W))],
        out_shape=[jax.ShapeDtypeStruct((T, 3 * W + 128), BF16), jax.ShapeDtypeStruct((SGU_G, SGU_C, SGU_C), F32),
                   jax.ShapeDtypeStruct((SGU_C, SGU_C), F32), jax.ShapeDtypeStruct((8, W), F32),
                   jax.ShapeDtypeStruct((8, W), F32)],
        compiler_params=_params(40, 1),
    )(z0, z0, dmix, dc, dkr, ln_g, ln_b, w, b_t)


def _hg_lower_bound(lb_ref):
    a0, a1 = lb_ref[0:1, :], lb_ref[1:2, :]
    m = jnp.maximum(a0, a1)
    e0, e1 = jnp.exp(a0 - m), jnp.exp(a1 - m)
    return e1 / (e0 + e1)


def _running_sum(x, reverse=False):
    n = x.shape[0]
    row = lax.broadcasted_iota(jnp.int32, x.shape, 0)
    s = 1
    while s < n:
        if reverse:
            x = x + jnp.where(row < n - s, pltpu.roll(x, n - s, 0), 0.0)
        else:
            x = x + jnp.where(row >= s, pltpu.roll(x, s, 0), 0.0)
        s *= 2
    return x


def _hg_chunk(qc, fc, lb):
    C = HG_CHUNK
    rows = lax.broadcasted_iota(jnp.int32, (C, C), 0)
    cols = lax.broadcasted_iota(jnp.int32, (C, C), 1)
    rowid = lax.broadcasted_iota(jnp.int32, (C, 128), 0)
    sq, sg = _sigmoid(qc), _sigmoid(fc)
    qf = qc * sq
    gate = lb + (1.0 - lb) * sg
    kk = 1.0 - gate
    lg = jnp.log(gate)
    bcum = _running_sum(lg)
    b_mid = jnp.sum(jnp.where(rowid < C // 2, lg, 0.0), axis=0, keepdims=True)
    b_last = jnp.sum(lg, axis=0, keepdims=True)
    eq, ek, e, eh = jnp.exp(bcum - b_mid), jnp.exp(b_mid - bcum), jnp.exp(bcum), jnp.exp(b_last - bcum)
    qt, kt, qe, khat = qf * eq, kk * ek, qf * e, kk * eh
    a = lax.dot_general(qt.astype(BF16), kt.astype(BF16), NT_DIMS, preferred_element_type=F32)
    a = jnp.where(rows >= cols, a, 0.0)
    return dict(sq=sq, sg=sg, gate=gate, kk=kk, eq=eq, ek=ek, e=e, eh=eh, qt=qt, kt=kt, qe=qe, khat=khat, a=a,
                e_last=jnp.exp(b_last), tril=rows >= cols, rowid=rowid)


def _hgrn_fwd(z4, hg_lb, gnorm):
    T = z4.shape[1]
    tb = min(ROW_BLOCK, T)
    C = HG_CHUNK
    ncb = tb // C
    HPB = HG_HEADS_PER_STEP

    def body(q_ref, f_ref, i_ref, g_ref, lb_ref, gn_ref, y_ref, o_ref, st_ref, st_sc):
        @pl.when(pl.program_id(1) == 0)
        def _():
            st_sc[...] = jnp.zeros_like(st_sc)

        def chunk(c, carry):
            rs = pl.ds(pl.multiple_of(c * C, C), C)
            for hh in range(HPB):
                hs = slice(hh * 128, (hh + 1) * 128)
                lb = _hg_lower_bound(lb_ref.at[:, hs])
                v_b = i_ref[rs, hs].astype(BF16)
                gc = g_ref[rs, hs]
                x = _hg_chunk(q_ref[rs, hs], f_ref[rs, hs], lb)
                st = st_sc[hh]
                st_ref[hh, c] = st
                o = (jnp.dot(x["a"].astype(BF16), v_b, preferred_element_type=F32)
                     + lax.dot_general(x["qe"].astype(BF16), st.astype(BF16), NT_DIMS, preferred_element_type=F32))
                st_sc[hh] = st * x["e_last"] + lax.dot_general(v_b, x["khat"].astype(BF16), TN_DIMS,
                                                               preferred_element_type=F32)
                o_ref[rs, hs] = o
                n = o * lax.rsqrt(jnp.mean(o * o, -1, keepdims=True) + EPS)
                y_ref[rs, hs] = (n * gn_ref[:, hs] * (gc * _sigmoid(gc))).astype(BF16)
            return carry

        lax.fori_loop(0, ncb, chunk, 0, unroll=4)

    W = 128 * HPB
    zb = lambda k: pl.BlockSpec((None, tb, W), lambda h, t: (k, t, h))
    out = pl.BlockSpec((tb, W), lambda h, t: (t, h))
    return pl.pallas_call(
        body, name="hgrn_fwd", grid=(HEADS // HPB, T // tb),
        in_specs=[zb(0), zb(1), zb(2), zb(3), pl.BlockSpec((2, W), lambda h, t: (0, h)),
                  pl.BlockSpec((1, W), lambda h, t: (0, h))],
        out_specs=[out, out, pl.BlockSpec((HPB, ncb, 128, 128), lambda h, t: (h, t, 0, 0))],
        out_shape=[jax.ShapeDtypeStruct((T, D), BF16), jax.ShapeDtypeStruct((T, D), F32),
                   jax.ShapeDtypeStruct((HEADS, T // C, 128, 128), F32)],
        scratch_shapes=[pltpu.VMEM((HPB, 128, 128), F32)],
        compiler_params=_params(48, 2),
    )(*_hbm(z4, z4, z4, z4, hg_lb, gnorm))


def _hgrn_bwd(z4, o_raw, dy, states, hg_lb, gnorm):
    T = z4.shape[1]
    tb = min(ROW_BLOCK, T)
    C = HG_CHUNK
    ncb = tb // C
    nt = T // tb
    HPB = HG_HEADS_PER_STEP

    def body(q_ref, f_ref, i_ref, g_ref, o_ref, dy_ref, st_ref, lb_ref, gn_ref, dz_ref, dlb_ref, dgn_ref, dst_sc):
        @pl.when(pl.program_id(1) == 0)
        def _():
            dst_sc[...] = jnp.zeros_like(dst_sc)
            dlb_ref[...] = jnp.zeros_like(dlb_ref)
            dgn_ref[...] = jnp.zeros_like(dgn_ref)

        def chunk(cc, carry):
            for hh in range(HPB):
                one_head(ncb - 1 - cc, hh, slice(hh * 128, (hh + 1) * 128))
            return carry

        def one_head(c, hh, hs):
            rs = pl.ds(pl.multiple_of(c * C, C), C)
            lb = _hg_lower_bound(lb_ref.at[:, hs])
            gn = gn_ref[:, hs]
            qc, gc = q_ref[rs, hs], g_ref[rs, hs]
            v_b = i_ref[rs, hs].astype(BF16)
            x = _hg_chunk(qc, f_ref[rs, hs], lb)
            st, dst = st_ref[hh, c], dst_sc[hh]
            st_b, dst_b = st.astype(BF16), dst.astype(BF16)
            o, dyc = o_ref[rs, hs], dy_ref[rs, hs]
            sgg = _sigmoid(gc)
            sil = gc * sgg
            rstd = lax.rsqrt(jnp.mean(o * o, -1, keepdims=True) + EPS)
            n = o * rstd
            dgn_ref[:, hs] += _fold8(dyc * n * sil)
            dn = dyc * gn * sil
            do = rstd * (dn - n * jnp.mean(dn * n, -1, keepdims=True))
            dg = dyc * n * gn * (sgg * (1.0 + gc * (1.0 - sgg)))
            do_b = do.astype(BF16)
            da = jnp.where(x["tril"], lax.dot_general(do_b, v_b, NT_DIMS, preferred_element_type=F32), 0.0).astype(BF16)
            qt_b, kt_b, qe_b, khat_b = (x[n_].astype(BF16) for n_ in ("qt", "kt", "qe", "khat"))
            dv = (lax.dot_general(x["a"].astype(BF16), do_b, TN_DIMS, preferred_element_type=F32)
                  + lax.dot_general(khat_b, dst_b, NT_DIMS, preferred_element_type=F32))
            dqt = jnp.dot(da, kt_b, preferred_element_type=F32)
            dqe = jnp.dot(do_b, st_b, preferred_element_type=F32)
            dkt = lax.dot_general(da, qt_b, TN_DIMS, preferred_element_type=F32)
            dkhat = jnp.dot(v_b, dst_b, preferred_element_type=F32)
            dst_sc[hh] = lax.dot_general(do_b, qe_b, TN_DIMS, preferred_element_type=F32) + dst * x["e_last"]
            de_last = jnp.sum(st * dst, axis=0, keepdims=True)
            dqf = dqt * x["eq"] + dqe * x["e"]
            dkk = dkt * x["ek"] + dkhat * x["eh"]
            dkh_kh = dkhat * x["khat"]
            db = dqt * qt_b.astype(F32) - dkt * kt_b.astype(F32) + dqe * x["qe"] - dkh_kh
            db_last = jnp.sum(dkh_kh, axis=0, keepdims=True) + de_last * x["e_last"]
            db = db + jnp.where(x["rowid"] == C - 1, db_last, 0.0)
            dlg = _running_sum(db, reverse=True)
            dgate = dlg / x["gate"] - dkk
            sg, sq = x["sg"], x["sq"]
            dlb_ref[:, hs] += _fold8(dgate * (1.0 - sg)) * (lb * (1.0 - lb))
            dz_ref[0, rs, hs] = (dqf * (sq * (1.0 + qc * (1.0 - sq)))).astype(BF16)
            dz_ref[1, rs, hs] = (dgate * (1.0 - lb) * sg * (1.0 - sg)).astype(BF16)
            dz_ref[2, rs, hs] = dv.astype(BF16)
            dz_ref[3, rs, hs] = dg.astype(BF16)

        lax.fori_loop(0, ncb, chunk, 0, unroll=4)

    W = 128 * HPB
    zb = lambda k: pl.BlockSpec((None, tb, W), lambda h, t: (k, nt - 1 - t, h))
    blk = pl.BlockSpec((tb, W), lambda h, t: (nt - 1 - t, h))
    acc = pl.BlockSpec((8, W), lambda h, t: (0, h))
    return pl.pallas_call(
        body, name="hgrn_bwd", grid=(HEADS // HPB, nt),
        in_specs=[zb(0), zb(1), zb(2), zb(3), blk, blk,
                  pl.BlockSpec((HPB, ncb, 128, 128), lambda h, t: (h, nt - 1 - t, 0, 0)),
                  pl.BlockSpec((2, W), lambda h, t: (0, h)), pl.BlockSpec((1, W), lambda h, t: (0, h))],
        out_specs=[pl.BlockSpec((4, tb, W), lambda h, t: (0, nt - 1 - t, h)), acc, acc],
        out_shape=[jax.ShapeDtypeStruct((4, T, D), BF16), jax.ShapeDtypeStruct((8, D), F32),
                   jax.ShapeDtypeStruct((8, D), F32)],
        scratch_shapes=[pltpu.VMEM((HPB, 128, 128), F32)],
        compiler_params=_params(48, 2),
    )(*_hbm(z4, z4, z4, z4, o_raw, dy, states, hg_lb, gnorm))


def _adamw(w, g, m, v, *, name):
    R, L = w.shape
    tr = R if R <= 512 else 512
    assert R % tr == 0
    blk = pl.BlockSpec((tr, L), lambda i: (i, 0))
    c1, c2 = 1.0 - B1 ** STEP, 1.0 - B2 ** STEP

    def body(w_ref, g_ref, m_ref, v_ref, d_ref, mo_ref, vo_ref):
        g_ = g_ref[...]
        m_ = B1 * m_ref[...] + (1.0 - B1) * g_
        v_ = B2 * v_ref[...] + (1.0 - B2) * (g_ * g_)
        d_ref[...] = -LR * ((m_ / c1) / (jnp.sqrt(v_ / c2) + ADAM_EPS) + WD * w_ref[...])
        mo_ref[...] = m_
        vo_ref[...] = v_

    sds = jax.ShapeDtypeStruct((R, L), F32)
    return pl.pallas_call(
        body, name=name, grid=(R // tr,), in_specs=[blk] * 4, out_specs=[blk] * 3, out_shape=[sds] * 3,
        compiler_params=_params(32, 1),
    )(w, g, m, v)


def _adamw_rows(w, m, v, gbufs, row0, *, name, plan=None):
    L, R, C = w.shape
    tr = 256
    assert R % tr == 0 and row0 % tr == 0 and len(gbufs) == L
    grid = (L, R // tr)
    blk = pl.BlockSpec((None, tr, C), lambda l, i: (l, i, 0))
    gblks = [pl.BlockSpec((tr, C), lambda l, i, k=k: (row0 // tr + jnp.where(l == k, i, 0), 0)) for k in range(L)]
    c1, c2 = 1.0 - B1 ** STEP, 1.0 - B2 ** STEP

    def body(*refs):
        ins, (go_ref, d_ref, mo_ref, vo_ref), _, pctx = _split_refs(refs, 3 + L, 4, 0, plan)
        w_ref, m_ref, v_ref = ins[:3]
        g_refs = ins[3:]
        _plan_start(plan, pctx, grid)
        g_ = g_refs[0][...]
        for l in range(1, L):
            g_ = jnp.where(pl.program_id(0) == l, g_refs[l][...], g_)
        m_ = B1 * m_ref[...] + (1.0 - B1) * g_
        v_ = B2 * v_ref[...] + (1.0 - B2) * (g_ * g_)
        go_ref[...] = g_
        d_ref[...] = -LR * ((m_ / c1) / (jnp.sqrt(v_ / c2) + ADAM_EPS) + WD * w_ref[...])
        mo_ref[...] = m_
        vo_ref[...] = v_
        _plan_wait(plan, pctx, grid)

    sds = jax.ShapeDtypeStruct((L, R, C), F32)
    p_in, p_ospec, p_oshape, p_scr, p_alias = _plan_io(plan, 3 + L, 4)
    return pl.pallas_call(
        body, name=name, grid=grid, in_specs=[blk] * 3 + gblks + [_ANY] * len(p_in),
        out_specs=[blk] * 4 + p_ospec, out_shape=[sds] * 4 + p_oshape, scratch_shapes=p_scr,
        input_output_aliases=p_alias, compiler_params=_params(32, 2),
    )(*_hbm(w, m, v, *gbufs), *p_in)


def _add_pairs(g, theirs, ids, *, name):
    n, R, L = theirs.shape
    tr = math.gcd(R, 128)
    nb = R // tr

    def body(ids_ref, a_ref, b_ref, o_ref):
        o_ref[...] = (a_ref[...].astype(F32) + b_ref[...].astype(F32)).astype(BF16)

    blk = pl.BlockSpec((n, tr, L), lambda i, ids: (0, i, 0))
    return pl.pallas_call(
        body, name=name, out_shape=jax.ShapeDtypeStruct((n, R, L), BF16),
        grid_spec=pltpu.PrefetchScalarGridSpec(
            num_scalar_prefetch=1, grid=(nb,),
            in_specs=[pl.BlockSpec((n, tr, L), lambda i, ids: (0, ids[1] * nb + i, 0)), blk], out_specs=blk),
        compiler_params=_params(16, 1),
    )(ids, g, theirs)


def _sum_chips(pair, parts, ids, *, name):
    _, R, L = parts.shape
    tr = math.gcd(R, 128)

    def body(ids_ref, o_ref, r_ref, out_ref):
        out_ref[...] = ((o_ref[...].astype(F32) + r_ref[0].astype(F32)) + r_ref[1].astype(F32)) + r_ref[2].astype(F32)

    return pl.pallas_call(
        body, name=name, out_shape=jax.ShapeDtypeStruct((2, R, L), F32),
        grid_spec=pltpu.PrefetchScalarGridSpec(
            num_scalar_prefetch=1, grid=(R // tr,),
            in_specs=[pl.BlockSpec((None, tr, L), lambda i, ids: (ids[0], i, 0)),
                      pl.BlockSpec((3, tr, L), lambda i, ids: (0, i, 0))],
            out_specs=pl.BlockSpec((None, tr, L), lambda i, ids: (ids[1], i, 0))),
        compiler_params=_params(32, 1),
    )(ids, pair, parts)


def _mesh_ids():
    x, y, c = _mesh_pos()
    return jnp.stack([2 * x + y, c]).astype(jnp.int32)


def _place_shard(rows, ids, *, name):
    R, L = rows.shape
    tr = 128

    def body(ids_ref, in_ref, out_ref):
        out_ref[...] = in_ref[...].astype(BF16)

    return pl.pallas_call(
        body, name=name, out_shape=jax.ShapeDtypeStruct((4, R, L), BF16),
        grid_spec=pltpu.PrefetchScalarGridSpec(
            num_scalar_prefetch=1, grid=(R // tr,), in_specs=[pl.BlockSpec((tr, L), lambda i, ids: (i, 0))],
            out_specs=pl.BlockSpec((None, tr, L), lambda i, ids: (ids[0], i, 0))),
        compiler_params=_params(16, 1),
    )(ids, rows)


def _place_weights(pieces, buffer_rows, ids, *, plan=None):
    tr = 256
    steps, s = [], 0
    for arr, layer, buf, row0 in pieces:
        nblk = arr.shape[1] // tr
        steps.append((s, nblk))
        s += nblk
    total = s
    buf_start = [min(st for (st, _), p in zip(steps, pieces) if p[2] == k) for k in range(len(buffer_rows))]
    grid = (total,)
    n_in = len(pieces)

    def body(*refs):
        ins, outs, _, pctx = _split_refs(refs[1:], n_in, len(buffer_rows), 0, plan)
        _plan_start(plan, pctx, grid)
        i = pl.program_id(0)
        for (st, nblk), (_, _, buf, _), ref in zip(steps, pieces, ins):
            @pl.when((i >= st) & (i < st + nblk))
            def _(ref=ref, buf=buf):
                outs[buf][...] = ref[...].astype(BF16)
        _plan_wait(plan, pctx, grid)

    in_specs = [pl.BlockSpec((None, tr, D), lambda i, ids, layer=layer, st=st, nblk=nblk:
                             (layer, jnp.clip(i - st, 0, nblk - 1), 0))
                for (st, nblk), (_, layer, _, _) in zip(steps, pieces)]
    out_specs = [pl.BlockSpec((None, tr, D), lambda i, ids, st=st, nb=rows // tr: (ids[0], jnp.clip(i - st, 0, nb - 1), 0))
                 for st, rows in zip(buf_start, buffer_rows)]
    p_in, p_ospec, p_oshape, p_scr, p_alias = _plan_io(plan, 1 + n_in, len(buffer_rows))
    return pl.pallas_call(
        body, name="place_weights",
        out_shape=[jax.ShapeDtypeStruct((4, rows, D), BF16) for rows in buffer_rows] + p_oshape,
        grid_spec=pltpu.PrefetchScalarGridSpec(
            num_scalar_prefetch=1, grid=grid, in_specs=in_specs + [_ANY] * len(p_in), out_specs=out_specs + p_ospec,
            scratch_shapes=p_scr),
        input_output_aliases=p_alias, compiler_params=_params(16, 1),
    )(ids, *[p[0] for p in pieces], *p_in)


def _remote(src, dst, send_sem, recv_sem, to):
    return pltpu.make_async_remote_copy(src_ref=src, dst_ref=dst, send_sem=send_sem, recv_sem=recv_sem,
                                        device_id=to, device_id_type=MESH_IDS)


def _rows(ref, lead, start, size):
    return ref.at[tuple(pl.ds(0, n) for n in ref.shape[:lead]) + (pl.ds(start, size),)]


def _other_chips():
    x, y, _ = _mesh_pos()
    return [(1 - x, y), (x, 1 - y), (1 - x, 1 - y)]


def _plan_gather_ici(bufs):
    n = len(bufs)

    def copies(outs, send, recv):
        x, y, c = _mesh_pos()
        res = []
        for b in range(n):
            half = bufs[b].shape[1] // 2
            mine = _rows(outs[b].at[2 * x + y], 0, c * half, half)
            for j, (cx, cy) in enumerate(_other_chips()):
                res.append((_remote(mine, mine, send(3 * b + j), recv(3 * b + j), (cx, cy, c)),
                            _remote(mine, _rows(outs[b].at[2 * cx + cy], 0, c * half, half),
                                    send(3 * b + j), recv(3 * b + j), (x, y, c))))
        return res

    def start(ins, outs, send, recv, loc):
        for out_cp, _ in copies(outs, send, recv):
            out_cp.start()

    def wait(ins, outs, send, recv, loc):
        for out_cp, in_cp in copies(outs, send, recv):
            in_cp.wait_recv()
            out_cp.wait_send()

    outs = [jax.ShapeDtypeStruct(b.shape, b.dtype) for b in bufs]
    return _Plan(bufs, outs, 3 * n, 0, start, wait, aliases={b: b for b in range(n)})


def _plan_gather_forward(bufs):
    n = len(bufs)

    def copies(outs, send, recv):
        x, y, c = _mesh_pos()
        res = []
        for b in range(n):
            half = bufs[b].shape[1] // 2
            for j, (cx, cy) in enumerate(_other_chips()):
                slot = outs[b].at[2 * cx + cy]
                res.append((_remote(_rows(slot, 0, c * half, half), _rows(slot, 0, c * half, half),
                                    send(3 * b + j), recv(3 * b + j), (x, y, 1 - c)),
                            _remote(_rows(slot, 0, c * half, half), _rows(slot, 0, (1 - c) * half, half),
                                    send(3 * b + j), recv(3 * b + j), (x, y, c))))
        return res

    def start(ins, outs, send, recv, loc):
        for out_cp, _ in copies(outs, send, recv):
            out_cp.start()

    def wait(ins, outs, send, recv, loc):
        for out_cp, in_cp in copies(outs, send, recv):
            in_cp.wait_recv()
            out_cp.wait_send()

    outs = [jax.ShapeDtypeStruct(b.shape, b.dtype) for b in bufs]
    return _Plan(bufs, outs, 3 * n, 0, start, wait, aliases={b: b for b in range(n)})


def _plan_pair_swap(g):
    half = g.shape[1] // 2

    def copy(ins, outs, send, recv, loc):
        x, y, c = _mesh_pos()
        return _remote(_rows(ins[0], 1, (1 - c) * half, half), outs[0], send(0), recv(0), (x, y, 1 - c))

    return _Plan([g], [jax.ShapeDtypeStruct((4, half, g.shape[2]), g.dtype)], 1, 0,
                 lambda *a: copy(*a).start(), lambda *a: copy(*a).wait())


def _plan_pair_gather(buf):
    def copies(ins, outs, send, recv, loc):
        x, y, c = _mesh_pos()
        return (_remote(outs[0].at[c], outs[0].at[c], send(0), recv(0), (x, y, 1 - c)),
                _remote(outs[0].at[c], outs[0].at[1 - c], send(0), recv(0), (x, y, c)))

    def wait(*a):
        out_cp, in_cp = copies(*a)
        in_cp.wait_recv()
        out_cp.wait_send()

    return _Plan([buf], [jax.ShapeDtypeStruct(buf.shape, buf.dtype)], 1, 0, lambda *a: copies(*a)[0].start(), wait,
                 aliases={0: 0})


def _plan_chip_scatter(p):
    def copies(ins, outs, send, recv, loc):
        _, _, c = _mesh_pos()
        return [_remote(ins[0].at[2 * cx + cy], outs[0].at[j], send(j), recv(j), (cx, cy, c))
                for j, (cx, cy) in enumerate(_other_chips())]

    def start(*a):
        for cp in copies(*a):
            cp.start()

    def wait(*a):
        for cp in copies(*a):
            cp.wait()

    return _Plan([p], [jax.ShapeDtypeStruct((3,) + p.shape[1:], p.dtype)], 3, 0, start, wait)


def _plan_exchange_all(vec):
    def copies(ins, outs, send, recv, loc):
        x, y, c = _mesh_pos()
        return [_remote(ins[0], outs[0].at[r - 1], send(r - 1), recv(r - 1), (x ^ (r >> 2), y ^ ((r >> 1) & 1), c ^ (r & 1)))
                for r in range(1, 8)]

    def start(*a):
        for cp in copies(*a):
            cp.start()

    def wait(*a):
        for cp in copies(*a):
            cp.wait()

    return _Plan([vec], [jax.ShapeDtypeStruct((7,) + vec.shape, vec.dtype)], 7, 0, start, wait)


SMALL_LAYOUT = {
    "mla_gq": (0, 1, 256, (1, 256)), "mla_gkv": (1, 1, 256, (1, 256)), "sgu_ln_g": (2, 1, 512, (1, 512)),
    "sgu_ln_b": (3, 1, 512, (1, 512)), "sgu_w": (4, 64, 1024, (64, 1024)), "sgu_b": (68, 1, 512, (1, 512)),
    "hg_lb": (69, 2, 1024, (2, 1024)), "hg_gnorm": (71, 1, 1024, (1, 256)), "ln1_g": (72, 2, 1024, (2, 1024)),
    "ln1_b": (74, 2, 1024, (2, 1024)), "ln2_g": (76, 2, 1024, (2, 1024)), "ln2_b": (78, 2, 1024, (2, 1024)),
}


def _small_pack(dgq, dgkv, dslg, dslb, dsw, dsb, dlb, dgn, ln_parts, sq_err):
    flat_ln = [p for pair in ln_parts for p in pair]

    def body(*refs):
        gq_ref, gkv_ref, slg_ref, slb_ref, sw_ref, sb_ref, lb_ref, gn_ref = refs[:8]
        ln_refs, err_ref, out_ref, t_sc = refs[8:16], refs[16], refs[17], refs[18]
        s8 = lambda ref: jnp.sum(ref[...], axis=0, keepdims=True)
        out_ref[...] = jnp.zeros_like(out_ref)
        out_ref[0:1, 0:256] = s8(gq_ref)
        out_ref[1:2, 0:256] = s8(gkv_ref)
        out_ref[2:3, 0:512] = s8(slg_ref)
        out_ref[3:4, 0:512] = s8(slb_ref)
        out_ref[4:68, :] = sw_ref[...]
        t_sc[...] = sb_ref[...].T
        for g in range(SGU_G):
            out_ref[68:69, g * SGU_C:(g + 1) * SGU_C] = t_sc[g:g + 1, :]
        d_lb1 = s8(lb_ref)
        out_ref[69:70, :] = -d_lb1
        out_ref[70:71, :] = d_lb1
        out_ref[71:72, :] = s8(gn_ref)
        for k, ref in enumerate(ln_refs):
            out_ref[72 + k:73 + k, :] = s8(ref)
        out_ref[0:1, 1023:1024] = jnp.sum(s8(err_ref), axis=1, keepdims=True) * (0.5 / D)

    vm = pl.BlockSpec(memory_space=pltpu.VMEM)
    return pl.pallas_call(
        body, name="small_grad_pack", in_specs=[vm] * 17, out_specs=vm,
        out_shape=jax.ShapeDtypeStruct((SMALL_ROWS, 1024), F32), scratch_shapes=[pltpu.VMEM((SGU_C, SGU_C), F32)],
        compiler_params=_params(16),
    )(dgq, dgkv, dslg, dslb, dsw.reshape(64, 1024), dsb, dlb, dgn, *flat_ln, sq_err)


def _small_update(vec, others, ids, w, m, v):
    names = list(SMALL_LAYOUT)
    n = len(names)
    c1, c2 = 1.0 - B1 ** STEP, 1.0 - B2 ** STEP
    have_others = others is not None

    def body(*refs):
        ids_ref, v_ref = refs[0], refs[1]
        k = 2 + have_others
        w_refs, m_refs, v_refs = refs[k:k + n], refs[k + n:k + 2 * n], refs[k + 2 * n:k + 3 * n]
        outs = refs[k + 3 * n:]
        row0_ref, tot_sc = outs[0], outs[-1]
        total = v_ref[...]
        if have_others:
            me = 2 * ids_ref[0] + ids_ref[1]
            total = None
            for d in range(8):
                rel = d ^ me
                term = jnp.where(rel == 0, v_ref[...], refs[2][jnp.maximum(rel - 1, 0)])
                total = term if total is None else total + term
        tot_sc[...] = total
        row0_ref[...] = tot_sc[0:1, :]
        for i, name in enumerate(names):
            r0, nr, width, _ = SMALL_LAYOUT[name]
            if name == "hg_gnorm":
                g_ = tot_sc[r0:r0 + 1, 0:256]
                for chip in range(1, 4):
                    g_ = jnp.where(ids_ref[0] == chip, tot_sc[r0:r0 + 1, chip * 256:(chip + 1) * 256], g_)
            else:
                g_ = tot_sc[r0:r0 + nr, 0:width]
            m_ = B1 * m_refs[i][...] + (1.0 - B1) * g_
            v_ = B2 * v_refs[i][...] + (1.0 - B2) * (g_ * g_)
            go, do, mo, vo = outs[1 + 4 * i:5 + 4 * i]
            go[...] = g_
            do[...] = -LR * ((m_ / c1) / (jnp.sqrt(v_ / c2) + ADAM_EPS) + WD * w_refs[i][...])
            mo[...] = m_
            vo[...] = v_

    full = lambda shape: pl.BlockSpec(shape, lambda i, ids, nd=len(shape): (0,) * nd)
    kshapes = [SMALL_LAYOUT[name][3] for name in names]
    operands = [vec] + ([others] if have_others else []) + [d[name] for d in (w, m, v) for name in names]
    out_shapes = [jax.ShapeDtypeStruct((1, 1024), F32)] + [jax.ShapeDtypeStruct(s, F32) for s in kshapes for _ in range(4)]
    res = pl.pallas_call(
        body, name="small_update", out_shape=out_shapes,
        grid_spec=pltpu.PrefetchScalarGridSpec(
            num_scalar_prefetch=1, grid=(1,), in_specs=[full(o.shape) for o in operands],
            out_specs=[full(s.shape) for s in out_shapes],
            scratch_shapes=[pltpu.VMEM((SMALL_ROWS, 1024), F32)]),
        compiler_params=_params(32, 1),
    )(ids, *operands)
    return res[0], {name: tuple(res[1 + 4 * i:5 + 4 * i]) for i, name in enumerate(names)}


ROWS_L1, ROWS_L0, ROWS_ODD_W = 3328, 2048, 384
ODD_PARTS = (("w_out_e", (256, 1024)), ("w_in_e", (1024, 392)), ("w_qb", (256, 192)), ("w_kvb", (256, 256)))
ODD_W_PARTS = tuple(p for p in ODD_PARTS if p[0] != "w_in_e")


def _odd_rows(parts, dtype, layout, total, gnorm=None):
    rows = [parts[n].reshape(-1, 1024).astype(dtype) for n, _ in layout]
    used = sum(r.shape[0] for r in rows)
    if gnorm is not None:
        bits = lax.bitcast_convert_type(gnorm.reshape(-1), BF16).reshape(1, 512)
        rows.append(jnp.pad(bits, ((0, 15), (0, 512))))
        used += 16
    if total > used:
        rows.append(jnp.zeros((total - used, 1024), dtype))
    return jnp.concatenate(rows, axis=0)


def _odd_unrows(buf, layout, with_gnorm=False):
    out, off = {}, 0
    for n, shape in layout:
        nr = math.prod(shape) // 1024
        out[n] = buf[off:off + nr].reshape(shape)
        off += nr
    if with_gnorm:
        out["hg_gnorm"] = lax.bitcast_convert_type(buf[off, :512].reshape(256, 2), F32).reshape(1, 256)
    return out


def _rope_tables(positions):
    half = ROPE // 2
    inv_freq = ROPE_BASE ** (-jnp.arange(half, dtype=F32) / half)
    ang = positions.astype(F32).reshape(-1, 1) * inv_freq
    cos, sin = jnp.cos(ang), jnp.sin(ang)
    T = ang.shape[0]
    one, z16, z32 = jnp.ones((T, NOPE), F32), jnp.zeros((T, half), F32), jnp.zeros((T, 32), F32)
    z64 = jnp.zeros((T, NOPE), F32)
    c = jnp.concatenate([one, cos, cos, z32], axis=1)
    s1 = jnp.concatenate([z64, -sin, z16, z32], axis=1)
    s2 = jnp.concatenate([z64, z16, sin, z32], axis=1)
    return c, s1, s2


def _local_step(x, positions, tgt, odd, bufs, P, exchange):
    T = x.shape[0]
    row = lambda a: a.reshape(1, -1)
    rc, rs1, rs2 = _rope_tables(positions)
    blk = lambda f: pl.BlockSpec((None, D, D), f)

    w_in_e = odd["w_in_e"]
    w_in = jnp.concatenate([w_in_e[:, :512], w_in_e[:, 544:1568], w_in_e[:, 512:544], jnp.zeros((D, 96), BF16)], axis=1)
    wq = jnp.pad(odd["w_qb"].reshape(256, HEADS, NOPE + ROPE), ((0, 0), (0, 0), (0, 32))).reshape(256, HEADS * 128)
    kvb = odd["w_kvb"].reshape(256, HEADS, NOPE + VDIM)
    wk = jnp.pad(kvb[:, :, :NOPE], ((0, 0), (0, 0), (0, 64))).reshape(256, HEADS * 128)
    wv = kvb[:, :, NOPE:].reshape(256, HEADS * VDIM)
    w_out_e = odd["w_out_e"]
    sgu_w = P["sgu_w"][0]
    sgu_bt = P["sgu_b"][0].T
    gq, gkv = P["mla_gq"], P["mla_gkv"]
    gnorm = P["hg_gnorm"]

    z0 = _matmul(x, w_in, name="in_proj_e", M=T, N=1664, K=D, tn=1664)[0]
    q, k, v = _mla_prep(z0, gq, gkv, wq, wk, wv, rc, rs1, rs2)
    if exchange:
        ids = _mesh_ids()
        placed = list(bufs)
        a_out, lse, wga, wgb = _flash_fwd(q, k, v, plan=_plan_gather_ici(placed[:2]))
    else:
        a_out, lse = _flash_fwd(q, k, v)
        wga, wgb, wgc = bufs
    mix0 = _sgu_fwd(z0, a_out, P["sgu_ln_g"], P["sgu_ln_b"], sgu_w, sgu_bt)
    res = _proj_ln(mix0, w_out_e, x, row(P["ln1_g"][0]), row(P["ln1_b"][0]), name="out_proj_ln_e",
                   plan=_plan_gather_forward([wga, wgb]) if exchange else None)
    r1, h1b = res[:2]
    if exchange:
        wga, wgb = res[2:]
    ln = lambda name, l: (row(P[name + "_g"][l]), row(P[name + "_b"][l]))
    res = _ffn_ln(h1b, wga, r1, *ln("ln2", 0), name="ffn_ln_0", prev_ln=ln("ln1", 0),
                  plan=_plan_gather_ici(placed[2:]) if exchange else None)
    ra0, r2, h2b = res[:3]
    z4 = _matmul(h2b, wgb, name="in_proj_o", M=T, N=4 * D, K=D, b_spec=blk(lambda i, j, k: (j, 0, 0)),
                 out_shape=jax.ShapeDtypeStruct((4, T, D), F32),
                 o_spec=pl.BlockSpec((None, min(MM_ROWS, T), D), lambda i, j, k: (j, i, 0)))[0]
    y1, o_raw, states = _hgrn_fwd(z4, P["hg_lb"], gnorm)
    res2 = _proj_ln(y1, wgb, r2, *ln("ln1", 1), name="out_proj_ln_o", prev_ln=ln("ln2", 0), w_rowblk=4,
                    plan=_plan_gather_forward([res[3]]) if exchange else None)
    r3, h3b = res2[:2]
    if exchange:
        wgc = res2[2]
    ra1, r4, _ = _ffn_ln(h3b, wgc, r3, *ln("ln2", 1), name="ffn_ln_1", prev_ln=ln("ln1", 1))

    ln1_g, ln1_b, ln2_g, ln2_b = [None, None], [None, None], [None, None], [None, None]
    sq_err_parts = []

    def ffn_bwd(l, dh, r_out, ra, h_mid_b, g2, wg, rows, plan=None, loss_head=()):
        dr, dr_b, dg, db, *sq_err = _ln_bwd(dh, r_out, row(g2), name=f"ln2_bwd_{l}", loss_head=loss_head)
        sq_err_parts.extend(sq_err)
        ln2_g[l], ln2_b[l] = dg, db
        da, *extra = _matmul(dr_b, wg, tb=True, mul=ra, out_dtype=BF16, name=f"ffn_da_{l}", M=T, N=4 * D, K=D,
                             b_spec=blk(lambda i, j, k: (j, 1, 0)), plan=plan)
        gbuf = _matmul(ra, dr_b, ta=True, a_sq=True, name=f"ffn_dw2_{l}", M=4 * D, N=D, K=T, tm=1024, tk=DW_TOKENS // 2,
                       out_shape=jax.ShapeDtypeStruct((4, rows, D), BF16), o_spec=blk(lambda i, j, k: (i, 1, 0)))[0]
        gbuf = _matmul(h_mid_b, da, ta=True, name=f"ffn_dw1_{l}", M=D, N=4 * D, K=T, tm=1024, tk=DW_TOKENS, into=gbuf,
                       out_shape=jax.ShapeDtypeStruct((4, rows, D), BF16), o_spec=blk(lambda i, j, k: (j, 0, 0)))[0]
        dh_mid = _matmul(da, wg, tb=True, add=dr, add_scale=ALPHA, name=f"ffn_dh_{l}", M=T, N=D, K=4 * D, tk=2 * D,
                         b_spec=pl.BlockSpec((2, D, D), lambda i, j, k: (k, 0, 0)))[0]
        return dh_mid, gbuf, extra

    dh3, g1, _ = ffn_bwd(1, None, r4, ra1, h3b, P["ln2_g"][1], wgc, ROWS_L1, loss_head=(row(P["ln2_b"][1]), tgt))
    loss_parts = sq_err_parts[0]
    dr3, dr3_b, dg, db = _ln_bwd(dh3, r3, row(P["ln1_g"][1]), name="ln1_bwd_1")
    ln1_g[1], ln1_b[1] = dg, db
    g1_sds = jax.ShapeDtypeStruct((4, ROWS_L1, D), BF16)
    g1 = _matmul(y1, dr3_b, ta=True, name="dw_out_o", M=D, N=D, K=T, tm=256, tk=DW_TOKENS, into=g1, out_shape=g1_sds,
                 o_spec=pl.BlockSpec((None, 256, D), lambda i, j, k: (i, 12, 0)))[0]
    dmix1 = _matmul(dr3_b, wgb, tb=True, name="dmix_o", M=T, N=D, K=D, b_spec=_rows4_spec(4, 3), b_merge=(D, D))[0]
    dz4, dlb, dgn = _hgrn_bwd(z4, o_raw, dmix1, states, P["hg_lb"], gnorm)
    g1 = _matmul(h2b, dz4, ta=True, name="dw_in_o", M=D, N=4 * D, K=T, tm=1024, tk=DW_TOKENS, into=g1, out_shape=g1_sds,
                 b_spec=pl.BlockSpec((None, min(DW_TOKENS, T), D), lambda i, j, k: (j, k, 0)),
                 o_spec=blk(lambda i, j, k: (j, 2, 0)))[0]
    dh2 = _matmul(dz4, wgb, tb=True, add=dr3, add_scale=ALPHA, name="dh_in_o", M=T, N=D, K=4 * D, tk=2 * D,
                  a_spec=pl.BlockSpec((2, min(MM_ROWS, T), D), lambda i, j, k: (k, i, 0)),
                  b_spec=pl.BlockSpec((2, D, D), lambda i, j, k: (k, 0, 0)))[0]

    dh1, g0, swapped1 = ffn_bwd(0, dh2, r2, ra0, h1b, P["ln2_g"][0], wga, ROWS_L0,
                                plan=_plan_pair_swap(g1) if exchange else None)
    dr1, dr1_b, dg, db = _ln_bwd(dh1, r1, row(P["ln1_g"][0]), name="ln1_bwd_0")
    ln1_g[0], ln1_b[0] = dg, db
    godd = {"w_out_e": _matmul(mix0, dr1_b, ta=True, name="dw_out_e", M=D, N=D, K=T, tm=1024, tk=DW_TOKENS)[0]}
    dmix0, *swapped0 = _matmul(dr1_b, w_out_e, tb=True, name="dmix_e", M=T, N=D, K=D,
                               plan=_plan_pair_swap(g0) if exchange else None)
    delta, do_b = _attn_delta(dmix0, a_out)
    if exchange:
        pair1 = _add_pairs(g1, swapped1[0], ids, name="grad_pair_add_1")
        pair0 = _add_pairs(g0, swapped0[0], ids, name="grad_pair_add_0")
        dq4, dk, dv, parts0, parts1 = _flash_bwd(
            q, k, v, do_b, lse, delta, plan=_join_plans([_plan_chip_scatter(pair0), _plan_chip_scatter(pair1)]))
        half0 = _sum_chips(pair0, parts0, ids, name="grad_chip_sum_0")
        half1 = _sum_chips(pair1, parts1, ids, name="grad_chip_sum_1")
        dc, dkr, dwq, dwk, dwv, dgq, dgkv, g0, g1 = _mla_bwd(
            z0, dq4, dk, dv, gq, gkv, wq, wk, wv, rc, rs1, rs2,
            plan=_join_plans([_plan_pair_gather(half0), _plan_pair_gather(half1)]))
        g0, g1 = g0.reshape(ROWS_L0, D), g1.reshape(ROWS_L1, D)
    else:
        dq4, dk, dv = _flash_bwd(q, k, v, do_b, lse, delta)
        dc, dkr, dwq, dwk, dwv, dgq, dgkv = _mla_bwd(z0, dq4, dk, dv, gq, gkv, wq, wk, wv, rc, rs1, rs2)
    dz0, dsw, dsb, dslg, dslb = _sgu_bwd(z0, dmix0, dc, dkr, P["sgu_ln_g"], P["sgu_ln_b"], sgu_w, sgu_bt)
    small_vec = _small_pack(dgq, dgkv, dslg, dslb, dsw, dsb, dlb, dgn, [ln1_g, ln1_b, ln2_g, ln2_b], loss_parts)
    dw_in, *small_others = _matmul(x, dz0, ta=True, name="dw_in_e", M=D, N=1664, K=T, tm=1024, tn=1664,
                                   tk=DW_TOKENS // 4, plan=_plan_exchange_all(small_vec) if exchange else None)
    godd["w_in_e"] = jnp.concatenate([dw_in[:, :512], dw_in[:, 1536:1568], dw_in[:, 512:1536]], axis=1)
    godd["w_qb"] = dwq.reshape(256, HEADS, 128)[:, :, :NOPE + ROPE].reshape(256, HEADS * (NOPE + ROPE))
    godd["w_kvb"] = jnp.concatenate([dwk.reshape(256, HEADS, 128)[:, :, :NOPE], dwv.reshape(256, HEADS, VDIM)],
                                    axis=2).reshape(256, HEADS * (NOPE + VDIM))
    odd_plan = None
    if exchange:
        by_chip = [_odd_rows({"w_out_e": jnp.split(godd["w_out_e"], 4, axis=0)[j],
                              **{n: jnp.split(godd[n], 4, axis=1)[j] for n in ("w_qb", "w_kvb")}}, BF16,
                             ODD_W_PARTS, ROWS_ODD_W)
                   for j in range(4)]
        bufs_odd = [godd["w_in_e"].reshape(D, 4, 392).transpose(1, 0, 2).astype(BF16), jnp.stack(by_chip)]
        theirs = _run_plan(_join_plans([_plan_pair_swap(b) for b in bufs_odd]), name="odd_pair_swap")
        odd_pairs = [_add_pairs(b, t, ids, name=f"odd_pair_add_{k}") for k, (b, t) in enumerate(zip(bufs_odd, theirs))]
        odd_plan = _join_plans([_plan_chip_scatter(p) for p in odd_pairs])
    grad_x, *odd_parts = _matmul(dz0, w_in, tb=True, add=dr1, add_scale=ALPHA, name="dx", M=T, N=D, K=1664, tk=1664,
                                 plan=odd_plan)
    if exchange:
        godd = (odd_pairs, odd_parts)
    return grad_x, g0, g1, godd, small_vec, (small_others[0] if exchange else None)


WEIGHTS = ['w_in_e', 'mla_gq', 'mla_gkv', 'w_qb', 'w_kvb', 'sgu_ln_g', 'sgu_ln_b', 'sgu_w', 'sgu_b', 'w_out_e',
           'w_in_o', 'hg_lb', 'hg_gnorm', 'w_out_o', 'ln1_g', 'ln1_b', 'w_ff1', 'w_ff2', 'ln2_g', 'ln2_b']


def kernel(x, positions, w_in_e, mla_gq, mla_gkv, w_qb, w_kvb, sgu_ln_g, sgu_ln_b, sgu_w, sgu_b, w_out_e, w_in_o, hg_lb, hg_gnorm, w_out_o, ln1_g, ln1_b, w_ff1, w_ff2, ln2_g, ln2_b, loss_target, m_w_in_e, m_mla_gq, m_mla_gkv, m_w_qb, m_w_kvb, m_sgu_ln_g, m_sgu_ln_b, m_sgu_w, m_sgu_b, m_w_out_e, m_w_in_o, m_hg_lb, m_hg_gnorm, m_w_out_o, m_ln1_g, m_ln1_b, m_w_ff1, m_w_ff2, m_ln2_g, m_ln2_b, v_w_in_e, v_mla_gq, v_mla_gkv, v_w_qb, v_w_kvb, v_sgu_ln_g, v_sgu_ln_b, v_sgu_w, v_sgu_b, v_w_out_e, v_w_in_o, v_hg_lb, v_hg_gnorm, v_w_out_o, v_ln1_g, v_ln1_b, v_w_ff1, v_w_ff2, v_ln2_g, v_ln2_b):
    args = dict(locals())
    w = {n: args[n] for n in WEIGHTS}
    m = {n: args["m_" + n] for n in WEIGHTS}
    v = {n: args["v_" + n] for n in WEIGHTS}
    cx, cy, cc = _mesh_pos()
    chip = 2 * cx + cy

    odd_shard = _odd_rows({"w_out_e": w_out_e[0], "w_qb": w_qb[0], "w_kvb": w_kvb[0]}, BF16, ODD_W_PARTS, ROWS_ODD_W,
                          gnorm=hg_gnorm)
    ids = _mesh_ids()
    placed = [_place_shard(w_in_e[0], ids, name="place_shard_in_e"), _place_shard(odd_shard, ids, name="place_shard_odd")]
    pieces = [(w_ff1, 0, 0, 0), (w_ff2, 0, 0, 1024), (w_in_o, 0, 1, 0), (w_out_o, 0, 1, 1024),
              (w_ff1, 1, 2, 0), (w_ff2, 1, 2, 1024)]
    *big_bufs, odd_a, odd_b = _place_weights(pieces, (2048, 1280, 2048), ids, plan=_plan_gather_ici(placed))
    gathered = _run_plan(_plan_gather_forward([odd_a, odd_b]), name="odd_gather_forward")
    per_chip = [_odd_unrows(gathered[1][j], ODD_W_PARTS, with_gnorm=True) for j in range(4)]
    odd = {"w_out_e": jnp.concatenate([p["w_out_e"] for p in per_chip], axis=0),
           "w_in_e": jnp.concatenate([gathered[0][j] for j in range(4)], axis=1)}
    for n in ("w_qb", "w_kvb"):
        odd[n] = jnp.concatenate([p[n] for p in per_chip], axis=1)
    small = {n: w[n] for n in SMALL_LAYOUT if n != "hg_gnorm"}
    small["hg_gnorm"] = jnp.concatenate([p["hg_gnorm"] for p in per_chip], axis=1)
    grad_x, g_l0, g_l1, godd, small_vec, small_others = _local_step(
        x[0], positions[0], loss_target[0], odd, big_bufs, small, True)

    sums = [_sum_chips(pair, parts, ids, name=f"odd_chip_sum_{k}") for k, (pair, parts) in enumerate(zip(*godd))]
    g_in_e, g_rest = _run_plan(_join_plans([_plan_pair_gather(s) for s in sums]), name="odd_pair_gather")
    g_odd = _odd_unrows(g_rest.reshape(ROWS_ODD_W, 1024), ODD_W_PARTS)
    g_odd["w_in_e"] = g_in_e.reshape(D, 392)

    to_kernel = lambda d: {n: d[n].reshape(SMALL_LAYOUT[n][3]) for n in SMALL_LAYOUT}
    first_row, small_out = _small_update(small_vec, small_others, ids, to_kernel(w), to_kernel(m), to_kernel(v))
    loss = first_row[0, 1023]
    grads, delta, new_m, new_v = {}, {}, {}, {}
    for n, res in small_out.items():
        grads[n], delta[n], new_m[n], new_v[n] = (r.reshape(w[n].shape) for r in res)

    for n, bufs_, row0 in (("w_ff1", [g_l0, g_l1], 0), ("w_ff2", [g_l0, g_l1], 1024), ("w_in_o", [g_l1], 2048),
                           ("w_out_o", [g_l1], 3072)):
        grads[n], delta[n], new_m[n], new_v[n] = _adamw_rows(w[n], m[n], v[n], bufs_, row0, name=f"adamw_{n}")
    for n, _ in ODD_PARTS:
        grads[n] = g_odd[n][None]
        d_, m_, v_ = _adamw(w[n][0], g_odd[n], m[n][0], v[n][0], name=f"adamw_{n}")
        delta[n], new_m[n], new_v[n] = d_[None], m_[None], v_[None]

    return (loss, grad_x[None], *[grads[n] for n in WEIGHTS], *[delta[n] for n in WEIGHTS],
            *[new_m[n] for n in WEIGHTS], *[new_v[n] for n in WEIGHTS])
```

```python
import math

import jax
import jax.numpy as jnp
from jax import lax
from jax.experimental import pallas as pl
from jax.experimental.pallas import tpu as pltpu

F32 = jnp.float32
BF16 = jnp.bfloat16
MESH_IDS = pl.DeviceIdType.MESH

D = 1024
DEPTH = 2
HEADS = 8
NOPE, ROPE, VDIM = 64, 32, 64
QK_SCALE = (NOPE + ROPE) ** -0.5
ROPE_BASE = 10000.0
SGU_G, SGU_C = 4, 128
HG_CHUNK = 64
HG_HEADS_PER_STEP = 8
ALPHA = (2 * DEPTH) ** 0.25
EPS = 1e-5
LR, B1, B2, ADAM_EPS, WD, STEP = 0.001, 0.9, 0.999, 1e-08, 0.01, 10
GELU_C = math.sqrt(2.0 / math.pi)
GELU_A = 0.044715
MB = 1024 * 1024
ROW_BLOCK = 512
SMALL_ROWS = 80

NT_DIMS = (((1,), (1,)), ((), ()))
TN_DIMS = (((0,), (0,)), ((), ()))


def _params(vmem_mb, n_axes=0):
    kw = dict(vmem_limit_bytes=vmem_mb * MB)
    if n_axes:
        kw["dimension_semantics"] = ("arbitrary",) * n_axes
    return pltpu.CompilerParams(**kw)


_ANY = pl.BlockSpec(memory_space=pltpu.HBM)


def _mesh_pos():
    return lax.axis_index("x"), lax.axis_index("y"), lax.axis_index("c")


def _hbm(*arrays):
    return tuple(pltpu.with_memory_space_constraint(a, pltpu.HBM) if a.size >= 2 ** 18 else a for a in arrays)


class _Plan:
    def __init__(self, ins, outs, n_remote, n_local, start, wait, aliases=None):
        self.ins, self.outs, self.n_remote, self.n_local = list(ins), list(outs), n_remote, n_local
        self.start, self.wait, self.aliases = start, wait, dict(aliases or {})


def _join_plans(plans):
    ins, outs, aliases, parts = [], [], {}, []
    nr = nl = 0
    for p in plans:
        parts.append((p, len(ins), len(outs), nr, nl))
        aliases.update({len(ins) + i: len(outs) + o for i, o in p.aliases.items()})
        ins += p.ins
        outs += p.outs
        nr += p.n_remote
        nl += p.n_local

    def run(which):
        def go(in_refs, out_refs, send, recv, loc):
            for p, i0, o0, r0, l0 in parts:
                getattr(p, which)(in_refs[i0:i0 + len(p.ins)], out_refs[o0:o0 + len(p.outs)],
                                  lambda i, r0=r0: send(r0 + i), lambda i, r0=r0: recv(r0 + i),
                                  lambda i, l0=l0: loc(l0 + i))
        return go

    return _Plan(ins, outs, nr, nl, run("start"), run("wait"), aliases)


def _plan_io(plan, n_in, n_out):
    if plan is None:
        return [], [], [], [], {}
    sems = [pltpu.SemaphoreType.DMA((max(plan.n_remote, 1),)), pltpu.SemaphoreType.DMA((max(plan.n_remote, 1),)),
            pltpu.SemaphoreType.DMA((max(plan.n_local, 1),))]
    aliases = {n_in + i: n_out + o for i, o in plan.aliases.items()}
    return plan.ins, [_ANY] * len(plan.outs), plan.outs, sems, aliases


def _split_refs(refs, n_in, n_out, n_scr, plan):
    p_in, p_out = (len(plan.ins), len(plan.outs)) if plan is not None else (0, 0)
    refs = list(refs)
    ins, refs = refs[:n_in], refs[n_in:]
    pins, refs = refs[:p_in], refs[p_in:]
    outs, refs = refs[:n_out], refs[n_out:]
    pouts, refs = refs[:p_out], refs[p_out:]
    scr, psem = refs[:n_scr], refs[n_scr:]
    psem = tuple((lambda i, s=s: s.at[i]) for s in psem)
    return ins, outs, scr, (pins, pouts, psem)


def _grid_edge(grid, last):
    cond = None
    for ax, n in enumerate(grid):
        c = pl.program_id(ax) == (n - 1 if last else 0)
        cond = c if cond is None else cond & c
    return cond


def _plan_start(plan, pctx, grid):
    if plan is not None:
        pins, pouts, psem = pctx
        pl.when(_grid_edge(grid, False))(lambda: plan.start(pins, pouts, *psem))


def _plan_wait(plan, pctx, grid):
    if plan is not None:
        pins, pouts, psem = pctx
        pl.when(_grid_edge(grid, True))(lambda: plan.wait(pins, pouts, *psem))


def _run_plan(plan, *, name):
    def body(*refs):
        _, _, _, (pins, pouts, psem) = _split_refs(refs, 0, 0, 0, plan)
        plan.start(pins, pouts, *psem)
        plan.wait(pins, pouts, *psem)

    p_in, p_ospec, p_oshape, p_scr, p_alias = _plan_io(plan, 0, 0)
    return pl.pallas_call(body, name=name, in_specs=[_ANY] * len(p_in), out_specs=p_ospec, out_shape=p_oshape,
                          scratch_shapes=p_scr, input_output_aliases=p_alias)(*p_in)


def _fold8(x):
    return x.reshape(x.shape[0] // 8, 8, x.shape[1]).sum(axis=0)


def _ln_stats(r):
    mu = jnp.mean(r, -1, keepdims=True)
    xc = r - mu
    rstd = lax.rsqrt(jnp.mean(xc * xc, -1, keepdims=True) + EPS)
    return xc * rstd, rstd


def _sigmoid(x):
    return jax.nn.sigmoid(x)


def _gelu(x):
    return 0.5 * x * (1.0 + jnp.tanh(GELU_C * (x + GELU_A * x * x * x)))


def _gelu_grad(x):
    t = jnp.tanh(GELU_C * (x + GELU_A * x * x * x))
    return 0.5 * (1.0 + t) + 0.5 * x * (1.0 - t * t) * GELU_C * (1.0 + 3.0 * GELU_A * x * x)


MM_ROWS = 1024
DW_TOKENS = 4096


def _matmul(a, b, *, name, M, N, K, ta=False, tb=False, out_dtype=F32, tm=MM_ROWS, tn=1024, tk=1024,
            a_spec=None, b_spec=None, b_merge=None, out_shape=None, o_spec=None, into=None,
            a_sq=False, mul=None, add=None, add_scale=1.0, plan=None):
    tm, tn, tk = min(tm, M), min(tn, N), min(tk, K)
    assert M % tm == 0 and N % tn == 0 and K % tk == 0
    grid = (M // tm, N // tn, K // tk)
    nk = grid[2]
    if a_spec is None:
        a_spec = pl.BlockSpec((tk, tm), lambda i, j, k: (k, i)) if ta else pl.BlockSpec((tm, tk), lambda i, j, k: (i, k))
    if b_spec is None:
        b_spec = pl.BlockSpec((tn, tk), lambda i, j, k: (j, k)) if tb else pl.BlockSpec((tk, tn), lambda i, j, k: (k, j))
    if o_spec is None:
        o_spec = pl.BlockSpec((tm, tn), lambda i, j, k: (i, j))
        out_shape = jax.ShapeDtypeStruct((M, N), out_dtype)
    e_spec = pl.BlockSpec((tm, tn), lambda i, j, k: (i, j))
    dims = (((0 if ta else 1,), (1 if tb else 0,)), ((), ()))
    extra = [e for e in (mul, add, into) if e is not None]
    n_in = 2 + len(extra)

    def body(*refs):
        ins, outs, scr, pctx = _split_refs(refs, n_in, 1, 1 if nk > 1 else 0, plan)
        a_ref, b_ref = ins[0], ins[1]
        rest = list(ins[2:])
        mul_ref = rest.pop(0) if mul is not None else None
        add_ref = rest.pop(0) if add is not None else None
        o_ref = outs[0]
        _plan_start(plan, pctx, grid)
        av = a_ref[...].astype(BF16)
        if a_sq:
            av = av * av
        bv = b_ref[...]
        if b_merge is not None:
            bv = bv.reshape(b_merge)
        if bv.ndim == 3:
            w = av.shape[-1] // (1 if av.ndim == 3 else bv.shape[0])
            a_parts = [av[s] if av.ndim == 3 else av[:, s * w:(s + 1) * w] for s in range(bv.shape[0])]
            p = sum(lax.dot_general(a_parts[s], bv[s], dims, preferred_element_type=F32) for s in range(bv.shape[0]))
        else:
            p = lax.dot_general(av, bv, dims, preferred_element_type=F32)

        def finish(r):
            if mul_ref is not None:
                r = r * (2.0 * mul_ref[...].astype(F32))
            if add_ref is not None:
                r = r + add_scale * add_ref[...]
            o_ref[...] = r.astype(o_ref.dtype)

        if nk == 1:
            finish(p)
        else:
            acc_ref = scr[0]
            k = pl.program_id(2)

            @pl.when(k == 0)
            def _():
                acc_ref[...] = p

            @pl.when(k > 0)
            def _():
                acc_ref[...] += p

            @pl.when(k == nk - 1)
            def _():
                finish(acc_ref[...])

        _plan_wait(plan, pctx, grid)

    p_in, p_ospec, p_oshape, p_scr, p_alias = _plan_io(plan, n_in, 1)
    aliases = dict(p_alias)
    if into is not None:
        aliases[n_in - 1] = 0
    return pl.pallas_call(
        body, name=name, grid=grid,
        in_specs=[a_spec, b_spec] + [e_spec] * (len(extra) - (into is not None)) + [_ANY] * (into is not None)
        + [_ANY] * len(p_in),
        out_specs=[o_spec] + p_ospec, out_shape=[out_shape] + p_oshape,
        scratch_shapes=([pltpu.VMEM((tm, tn), F32)] if nk > 1 else []) + p_scr,
        input_output_aliases=aliases, compiler_params=_params(48, 3),
    )(*_hbm(a, b, *extra), *p_in)


def _rows4_spec(rowblk, n_axes):
    return pl.BlockSpec((4, 256, D), lambda *_: (0, rowblk, 0))


def _residual(h_ref, prev_refs):
    if not prev_refs:
        return h_ref[...]
    xhat, _ = _ln_stats(h_ref[...])
    return xhat * prev_refs[0][...] + prev_refs[1][...]


def _proj_ln(a_b, w, h_prev, g, b, *, name, prev_ln=(), w_rowblk=None, plan=None):
    T = a_b.shape[0]
    tm = min(MM_ROWS, T)
    grid = (T // tm,)
    row = pl.BlockSpec((tm, D), lambda i: (i, 0))
    vec = pl.BlockSpec((1, D), lambda i: (0, 0))
    w_spec = pl.BlockSpec((D, D), lambda i: (0, 0)) if w_rowblk is None else _rows4_spec(w_rowblk, 1)
    n_in = 5 + len(prev_ln)

    def body(*refs):
        ins, (r_ref, hb_ref), _, pctx = _split_refs(refs, n_in, 2, 0, plan)
        a_ref, w_ref, h_ref, g_ref, b_ref = ins[:5]
        _plan_start(plan, pctx, grid)
        mix = jnp.dot(a_ref[...], w_ref[...].reshape(D, D), preferred_element_type=F32)
        r = ALPHA * _residual(h_ref, ins[5:]) + mix
        xhat, _ = _ln_stats(r)
        r_ref[...] = r
        hb_ref[...] = (xhat * g_ref[...] + b_ref[...]).astype(BF16)
        _plan_wait(plan, pctx, grid)

    p_in, p_ospec, p_oshape, p_scr, p_alias = _plan_io(plan, n_in, 2)
    return pl.pallas_call(
        body, name=name, grid=grid,
        in_specs=[row, w_spec, row, vec, vec] + [vec] * len(prev_ln) + [_ANY] * len(p_in),
        out_specs=[row, row] + p_ospec,
        out_shape=[jax.ShapeDtypeStruct((T, D), F32), jax.ShapeDtypeStruct((T, D), BF16)] + p_oshape,
        scratch_shapes=p_scr, input_output_aliases=p_alias, compiler_params=_params(40, 1),
    )(*_hbm(a_b, w, h_prev, g, b, *prev_ln), *p_in)


def _ffn_ln(h_b, wbuf, h, g, b, *, name, prev_ln=(), plan=None):
    T = h_b.shape[0]
    slots = 2
    tm, tf = min(ROW_BLOCK, T), slots * 1024
    nf = 4 // slots
    F = nf * tf
    grid = (T // tm, nf)
    row = pl.BlockSpec((tm, D), lambda i, j: (i, 0))
    vec = pl.BlockSpec((1, D), lambda i, j: (0, 0))
    n_in = 6 + len(prev_ln)

    def body(*refs):
        ins, (ra_ref, r_ref, hbo_ref), (acc_ref,), pctx = _split_refs(refs, n_in, 3, 1, plan)
        hb_ref, w1_ref, w2_ref, h_ref, g_ref, b_ref = ins[:6]
        _plan_start(plan, pctx, grid)
        j = pl.program_id(1)
        hb = hb_ref[...]
        p = None
        for s in range(slots):
            ra = jnp.maximum(jnp.dot(hb, w1_ref[s], preferred_element_type=F32), 0.0)
            ra_ref[:, s * 1024:(s + 1) * 1024] = ra.astype(BF16)
            ps = jnp.dot((ra * ra).astype(BF16), w2_ref[s], preferred_element_type=F32)
            p = ps if p is None else p + ps

        @pl.when(j == 0)
        def _():
            acc_ref[...] = p

        @pl.when(j > 0)
        def _():
            acc_ref[...] += p

        @pl.when(j == nf - 1)
        def _():
            r = ALPHA * _residual(h_ref, ins[6:]) + acc_ref[...]
            xhat, _ = _ln_stats(r)
            r_ref[...] = r
            hbo_ref[...] = (xhat * g_ref[...] + b_ref[...]).astype(BF16)

        _plan_wait(plan, pctx, grid)

    p_in, p_ospec, p_oshape, p_scr, p_alias = _plan_io(plan, n_in, 3)
    return pl.pallas_call(
        body, name=name, grid=grid,
        in_specs=[row, pl.BlockSpec((slots, D, D), lambda i, j: (j, 0, 0)),
                  pl.BlockSpec((slots, D, D), lambda i, j: (j, 1, 0)), row, vec, vec] + [vec] * len(prev_ln)
        + [_ANY] * len(p_in),
        out_specs=[pl.BlockSpec((tm, tf), lambda i, j: (i, j)), row, row] + p_ospec,
        out_shape=[jax.ShapeDtypeStruct((T, F), BF16), jax.ShapeDtypeStruct((T, D), F32),
                   jax.ShapeDtypeStruct((T, D), BF16)] + p_oshape,
        scratch_shapes=[pltpu.VMEM((tm, D), F32)] + p_scr,
        input_output_aliases=p_alias, compiler_params=_params(56, 2),
    )(*_hbm(h_b, wbuf, wbuf, h, g, b, *prev_ln), *p_in)


def _ln_bwd(dy, r, g, *, name, loss_head=()):
    T = r.shape[0]
    tm = min(ROW_BLOCK, T)
    row = pl.BlockSpec((tm, D), lambda i: (i, 0))
    vec = pl.BlockSpec((1, D), lambda i: (0, 0))
    acc = pl.BlockSpec((8, D), lambda i: (0, 0))
    operands, in_specs = ([r, g, *loss_head], [row, vec, vec, row]) if loss_head else ([r, g, dy], [row, vec, row])
    n_in = len(operands)

    def body(*refs):
        r_ref, g_ref = refs[:2]
        dr_ref, drb_ref, dg_ref, db_ref = refs[n_in:n_in + 4]

        @pl.when(pl.program_id(0) == 0)
        def _():
            for ref in refs[n_in + 2:]:
                ref[...] = jnp.zeros_like(ref)

        xhat, rstd = _ln_stats(r_ref[...])
        if loss_head:
            err = xhat * g_ref[...] + refs[2][...] - refs[3][...]
            refs[n_in + 4][...] += _fold8(err * err)
            dy_ = err * (1.0 / D)
        else:
            dy_ = refs[2][...]
        dxh = dy_ * g_ref[...]
        m1 = jnp.mean(dxh, -1, keepdims=True)
        m2 = jnp.mean(dxh * xhat, -1, keepdims=True)
        dr = rstd * (dxh - m1 - xhat * m2)
        dr_ref[...] = dr
        drb_ref[...] = dr.astype(BF16)
        dg_ref[...] += _fold8(dy_ * xhat)
        db_ref[...] += _fold8(dy_)

    n_acc = 3 if loss_head else 2
    return pl.pallas_call(
        body, name=name, grid=(T // tm,), in_specs=in_specs, out_specs=[row, row] + [acc] * n_acc,
        out_shape=[jax.ShapeDtypeStruct((T, D), F32), jax.ShapeDtypeStruct((T, D), BF16)]
        + [jax.ShapeDtypeStruct((8, D), F32)] * n_acc,
        compiler_params=_params(40, 1),
    )(*_hbm(*operands))


def _rope(x, c, s1, s2):
    return x * c + pltpu.roll(x, 112, 1) * s1 + pltpu.roll(x, 16, 1) * s2


def _rope_t(dy, c, s1, s2):
    return dy * c + pltpu.roll(dy * s1, 16, 1) + pltpu.roll(dy * s2, 112, 1)


def _rms(x, g):
    rstd = lax.rsqrt(jnp.mean(x * x, -1, keepdims=True) + EPS)
    xhat = x * rstd
    return xhat * g, xhat, rstd


def _mla_prep(z0, gq, gkv, wq, wk, wv, rc, rs1, rs2):
    T = z0.shape[0]
    tm = min(ROW_BLOCK, T)
    HW = HEADS * 128

    def body(cq_ref, ckv_ref, kr_ref, gq_ref, gkv_ref, wq_ref, wk_ref, wv_ref, c_ref, s1_ref, s2_ref,
             q_ref, k_ref, v_ref):
        nq = _rms(cq_ref[...], gq_ref[...])[0].astype(BF16)
        nkv = _rms(ckv_ref[...], gkv_ref[...])[0].astype(BF16)
        q = jnp.dot(nq, wq_ref[...], preferred_element_type=F32)
        k = jnp.dot(nkv, wk_ref[...], preferred_element_type=F32)
        v = jnp.dot(nkv, wv_ref[...], preferred_element_type=F32)
        c, s1, s2 = c_ref[...], s1_ref[...], s2_ref[...]
        kr = _rope(pltpu.roll(kr_ref[...], 64, 1), c, s1, s2)
        for h in range(HEADS):
            sl = slice(h * 128, (h + 1) * 128)
            q_ref[:, sl] = (_rope(q[:, sl], c, s1, s2) * QK_SCALE).astype(BF16)
            k_ref[:, sl] = (k[:, sl] + kr).astype(BF16)
        v_ref[...] = v.astype(BF16)

    full = lambda shape: pl.BlockSpec(shape, lambda i: (0, 0))
    tab = pl.BlockSpec((tm, 128), lambda i: (i, 0))
    return pl.pallas_call(
        body, name="mla_prep", grid=(T // tm,),
        in_specs=[pl.BlockSpec((tm, 256), lambda i: (i, 0)), pl.BlockSpec((tm, 256), lambda i: (i, 1)),
                  pl.BlockSpec((tm, 128), lambda i: (i, 12)), full((1, 256)), full((1, 256)),
                  full((256, HW)), full((256, HW)), full((256, 512)), tab, tab, tab],
        out_specs=[pl.BlockSpec((tm, HW), lambda i: (i, 0)), pl.BlockSpec((tm, HW), lambda i: (i, 0)),
                   pl.BlockSpec((tm, 512), lambda i: (i, 0))],
        out_shape=[jax.ShapeDtypeStruct((T, HW), BF16), jax.ShapeDtypeStruct((T, HW), BF16),
                   jax.ShapeDtypeStruct((T, 512), BF16)],
        compiler_params=_params(40, 1),
    )(z0, z0, z0, gq, gkv, wq, wk, wv, rc, rs1, rs2)


def _flash_fwd(q, k, v, plan=None):
    T = q.shape[0]
    bq = min(2 * ROW_BLOCK, T)
    nq = T // bq
    grid = (4, nq, nq)

    def body(*refs):
        (q_ref, k_ref, v_ref), (o_ref, lse_ref), (m_sc, acc_sc), pctx = _split_refs(refs, 3, 2, 2, plan)
        _plan_start(plan, pctx, grid)
        i, j = pl.program_id(1), pl.program_id(2)
        first = lax.broadcasted_iota(jnp.int32, (bq, 128), 1) < 64

        @pl.when(j == 0)
        def _():
            m_sc[...] = jnp.full_like(m_sc, -jnp.inf)
            acc_sc[...] = jnp.zeros_like(acc_sc)

        def tile(r0, nr, nc, masked):
            rs = slice(r0, r0 + nr)
            vp = v_ref[0:nc, :]
            lanes = first[0:nc, :]
            for h in range(2):
                sl = slice(h * 128, (h + 1) * 128)
                s = lax.dot_general(q_ref[rs, sl], k_ref[0:nc, sl], NT_DIMS, preferred_element_type=F32)
                if masked:
                    rows = r0 + lax.broadcasted_iota(jnp.int32, (nr, nc), 0)
                    cols = lax.broadcasted_iota(jnp.int32, (nr, nc), 1)
                    s = jnp.where(cols <= rows, s, -jnp.inf)
                m_prev = m_sc[h, rs, 0:1]
                m_new = jnp.maximum(m_prev, jnp.max(s, axis=1, keepdims=True))
                alpha = jnp.exp(m_prev - m_new)
                p = jnp.exp(s - m_new).astype(BF16)
                vh = jnp.where(lanes if h == 0 else jnp.logical_not(lanes), vp, jnp.ones_like(vp))
                acc_sc[h, rs, :] = acc_sc[h, rs, :] * alpha + jnp.dot(p, vh, preferred_element_type=F32)
                m_sc[h, rs, :] = jnp.broadcast_to(m_new, (nr, 128))

        @pl.when(j < i)
        def _():
            tile(0, bq, bq, False)

        @pl.when(j == i)
        def _():
            tile(0, bq, bq, True)
            a0, a1 = acc_sc[0], acc_sc[1]
            l0, l1 = pltpu.roll(a0, 64, 1), pltpu.roll(a1, 64, 1)
            o_ref[...] = jnp.where(first, a0 / l0, a1 / l1).astype(BF16)
            lse_ref[...] = jnp.where(first, m_sc[0] + jnp.log(l0), m_sc[1] + jnp.log(l1))

        _plan_wait(plan, pctx, grid)

    kv = lambda hp, i, j: (jnp.minimum(i, j), hp)
    p_in, p_ospec, p_oshape, p_scr, p_alias = _plan_io(plan, 3, 2)
    return pl.pallas_call(
        body, name="flash_fwd", grid=grid,
        in_specs=[pl.BlockSpec((bq, 256), lambda hp, i, j: (i, hp)), pl.BlockSpec((bq, 256), kv),
                  pl.BlockSpec((bq, 128), kv)] + [_ANY] * len(p_in),
        out_specs=[pl.BlockSpec((bq, 128), lambda hp, i, j: (i, hp)),
                   pl.BlockSpec((bq, 128), lambda hp, i, j: (i, hp))] + p_ospec,
        out_shape=[jax.ShapeDtypeStruct((T, 512), BF16), jax.ShapeDtypeStruct((T, 512), F32)] + p_oshape,
        scratch_shapes=[pltpu.VMEM((2, bq, 128), F32), pltpu.VMEM((2, bq, 128), F32)] + p_scr,
        input_output_aliases=p_alias, compiler_params=_params(56, 3),
    )(*_hbm(q, k, v), *p_in)


def _attn_delta(dmix, o):
    T = o.shape[0]
    tm = min(ROW_BLOCK, T)
    blk = pl.BlockSpec((tm, 512), lambda i: (i, 0))

    def body(do_ref, o_ref, delta_ref, dob_ref):
        first = lax.broadcasted_iota(jnp.int32, (tm, 128), 1) < 64
        for hp in range(4):
            sl = slice(hp * 128, (hp + 1) * 128)
            prod = do_ref[:, sl] * o_ref[:, sl].astype(F32)
            d0 = jnp.sum(jnp.where(first, prod, 0.0), axis=1, keepdims=True)
            d1 = jnp.sum(jnp.where(first, 0.0, prod), axis=1, keepdims=True)
            delta_ref[:, sl] = jnp.where(first, d0, d1)
        dob_ref[...] = do_ref[...].astype(BF16)

    return pl.pallas_call(
        body, name="attn_delta", grid=(T // tm,), in_specs=[blk, blk], out_specs=[blk, blk],
        out_shape=[jax.ShapeDtypeStruct((T, 512), F32), jax.ShapeDtypeStruct((T, 512), BF16)],
        compiler_params=_params(32, 1),
    )(dmix, o)


def _flash_bwd(q, k, v, do_b, lse, delta, plan=None):
    T = q.shape[0]
    bq = min(2 * ROW_BLOCK, T)
    nq = T // bq
    grid = (4, nq, nq)

    def body(*refs):
        ((q_ref, k_ref, v_ref, do_ref, lse_ref, dl_ref), (dq_hbm, dk_ref, dv_ref), (dq_sc, dk_sc, dv_sc, sem),
         pctx) = _split_refs(refs, 6, 3, 4, plan)
        _plan_start(plan, pctx, grid)
        hp, j, i = pl.program_id(0), pl.program_id(1), pl.program_id(2)
        first = lax.broadcasted_iota(jnp.int32, (bq, 128), 1) < 64

        @pl.when((j == 0) & (i == 0))
        def _():
            dq_sc[...] = jnp.zeros_like(dq_sc)

        @pl.when(i == j)
        def _():
            dk_sc[...] = jnp.zeros_like(dk_sc)
            dv_sc[...] = jnp.zeros_like(dv_sc)

        def tile(r0, nr, nc, masked):
            rs, cs = slice(r0, r0 + nr), slice(0, nc)
            vp = v_ref[cs, :]
            do = do_ref[rs, :]
            lanes = first[rs, :]
            for h in range(2):
                sl = slice(h * 128, (h + 1) * 128)
                qh, kh = q_ref[rs, sl], k_ref[cs, sl]
                s = lax.dot_general(qh, kh, NT_DIMS, preferred_element_type=F32)
                p = jnp.exp(s - lse_ref[rs, h * 64:h * 64 + 1])
                if masked:
                    rows = r0 + lax.broadcasted_iota(jnp.int32, (nr, nc), 0)
                    cols = lax.broadcasted_iota(jnp.int32, (nr, nc), 1)
                    p = jnp.where(cols <= rows, p, 0.0)
                do_h = jnp.where(lanes if h == 0 else jnp.logical_not(lanes), do, jnp.zeros_like(do))
                dv_sc[cs, :] += lax.dot_general(p.astype(BF16), do_h, TN_DIMS, preferred_element_type=F32)
                dp = lax.dot_general(do_h, vp, NT_DIMS, preferred_element_type=F32)
                ds = (p * (dp - dl_ref[rs, h * 64:h * 64 + 1])).astype(BF16)
                dq_sc[i, rs, sl] += jnp.dot(ds, kh, preferred_element_type=F32)
                dk_sc[cs, sl] += lax.dot_general(ds, qh, TN_DIMS, preferred_element_type=F32)

        @pl.when(i > j)
        def _():
            tile(0, bq, bq, False)

        @pl.when(i == j)
        def _():
            tile(0, bq // 2, bq // 2, True)
            tile(bq // 2, bq // 2, bq, True)

        @pl.when(i == nq - 1)
        def _():
            dk_ref[...] = dk_sc[...]
            dv_ref[...] = dv_sc[...]

        @pl.when((j == nq - 1) & (i == nq - 1))
        def _():
            cp = pltpu.make_async_copy(dq_sc, dq_hbm.at[hp], sem)
            cp.start()
            cp.wait()

        _plan_wait(plan, pctx, grid)

    qi = lambda hp, j, i: (jnp.maximum(i, j), hp)
    kj = lambda hp, j, i: (j, hp)
    p_in, p_ospec, p_oshape, p_scr, p_alias = _plan_io(plan, 6, 3)
    return pl.pallas_call(
        body, name="flash_bwd", grid=grid,
        in_specs=[pl.BlockSpec((bq, 256), qi), pl.BlockSpec((bq, 256), kj), pl.BlockSpec((bq, 128), kj),
                  pl.BlockSpec((bq, 128), qi), pl.BlockSpec((bq, 128), qi), pl.BlockSpec((bq, 128), qi)]
        + [_ANY] * len(p_in),
        out_specs=[_ANY, pl.BlockSpec((bq, 256), kj), pl.BlockSpec((bq, 128), kj)] + p_ospec,
        out_shape=[jax.ShapeDtypeStruct((4, nq, bq, 256), F32), jax.ShapeDtypeStruct((T, 1024), F32),
                   jax.ShapeDtypeStruct((T, 512), F32)] + p_oshape,
        scratch_shapes=[pltpu.VMEM((nq, bq, 256), F32), pltpu.VMEM((bq, 256), F32), pltpu.VMEM((bq, 128), F32),
                        pltpu.SemaphoreType.DMA] + p_scr,
        input_output_aliases=p_alias, compiler_params=_params(56, 3),
    )(*_hbm(q, k, v, do_b, lse, delta), *p_in)


def _mla_bwd(z0, dq4, dk, dv, gq, gkv, wq, wk, wv, rc, rs1, rs2, plan=None):
    T = z0.shape[0]
    tm = min(ROW_BLOCK, T)
    HW = HEADS * 128
    grid = (T // tm,)
    dq4 = dq4.reshape(4, T, 256)

    def body(*refs):
        ((cq_ref, ckv_ref, dq_ref, dk_ref, dv_ref, gq_ref, gkv_ref, wq_ref, wk_ref, wv_ref, c_ref, s1_ref, s2_ref),
         (dc_ref, dkr_ref, dwq_ref, dwk_ref, dwv_ref, dgq_ref, dgkv_ref), _, pctx) = _split_refs(refs, 13, 7, 0, plan)
        _plan_start(plan, pctx, grid)

        @pl.when(pl.program_id(0) == 0)
        def _():
            for ref in (dwq_ref, dwk_ref, dwv_ref, dgq_ref, dgkv_ref):
                ref[...] = jnp.zeros_like(ref)

        c, s1, s2 = c_ref[...], s1_ref[...], s2_ref[...]
        lane = lax.broadcasted_iota(jnp.int32, (tm, 128), 1)
        nq, xq, rq = _rms(cq_ref[...], gq_ref[...])
        nkv, xkv, rkv = _rms(ckv_ref[...], gkv_ref[...])
        nq_b, nkv_b = nq.astype(BF16), nkv.astype(BF16)

        dq_parts, dk_parts = [], []
        dkr = jnp.zeros((tm, 128), F32)
        for h in range(HEADS):
            blk = dq_ref[h // 2, :, (h % 2) * 128:(h % 2 + 1) * 128] * QK_SCALE
            dq_parts.append(_rope_t(blk, c, s1, s2).astype(BF16))
            kb = dk_ref[:, h * 128:(h + 1) * 128]
            dk_parts.append(jnp.where(lane < NOPE, kb, 0.0).astype(BF16))
            dkr = dkr + kb
        dq_b = jnp.concatenate(dq_parts, axis=1)
        dk_b = jnp.concatenate(dk_parts, axis=1)
        dv_b = dv_ref[...].astype(BF16)

        dwq_ref[...] += lax.dot_general(nq_b, dq_b, TN_DIMS, preferred_element_type=F32)
        dwk_ref[...] += lax.dot_general(nkv_b, dk_b, TN_DIMS, preferred_element_type=F32)
        dwv_ref[...] += lax.dot_general(nkv_b, dv_b, TN_DIMS, preferred_element_type=F32)
        dnq = lax.dot_general(dq_b, wq_ref[...], NT_DIMS, preferred_element_type=F32)
        dnkv = (lax.dot_general(dk_b, wk_ref[...], NT_DIMS, preferred_element_type=F32)
                + lax.dot_general(dv_b, wv_ref[...], NT_DIMS, preferred_element_type=F32))

        def rms_bwd(dn, xhat, rstd, g):
            dxh = dn * g
            return rstd * (dxh - xhat * jnp.mean(dxh * xhat, -1, keepdims=True))

        dc_ref[:, :256] = rms_bwd(dnq, xq, rq, gq_ref[...]).astype(BF16)
        dc_ref[:, 256:] = rms_bwd(dnkv, xkv, rkv, gkv_ref[...]).astype(BF16)
        dgq_ref[...] += _fold8(dnq * xq)
        dgkv_ref[...] += _fold8(dnkv * xkv)
        dkr = pltpu.roll(_rope_t(dkr, c, s1, s2), 64, 1)
        dkr_ref[...] = jnp.where(lane < ROPE, dkr, 0.0).astype(BF16)
        _plan_wait(plan, pctx, grid)

    full = lambda shape: pl.BlockSpec(shape, lambda i: (0,) * len(shape))
    tab = pl.BlockSpec((tm, 128), lambda i: (i, 0))
    p_in, p_ospec, p_oshape, p_scr, p_alias = _plan_io(plan, 13, 7)
    return pl.pallas_call(
        body, name="mla_bwd", grid=grid,
        in_specs=[pl.BlockSpec((tm, 256), lambda i: (i, 0)), pl.BlockSpec((tm, 256), lambda i: (i, 1)),
                  pl.BlockSpec((4, tm, 256), lambda i: (0, i, 0)),
                  pl.BlockSpec((tm, HW), lambda i: (i, 0)), pl.BlockSpec((tm, 512), lambda i: (i, 0)),
                  full((1, 256)), full((1, 256)), full((256, HW)), full((256, HW)), full((256, 512)), tab, tab, tab]
        + [_ANY] * len(p_in),
        out_specs=[pl.BlockSpec((tm, 512), lambda i: (i, 0)), tab, full((256, HW)), full((256, HW)),
                   full((256, 512)), full((8, 256)), full((8, 256))] + p_ospec,
        out_shape=[jax.ShapeDtypeStruct((T, 512), BF16), jax.ShapeDtypeStruct((T, 128), BF16),
                   jax.ShapeDtypeStruct((256, HW), F32), jax.ShapeDtypeStruct((256, HW), F32),
                   jax.ShapeDtypeStruct((256, 512), F32), jax.ShapeDtypeStruct((8, 256), F32),
                   jax.ShapeDtypeStruct((8, 256), F32)] + p_oshape,
        scratch_shapes=p_scr, input_output_aliases=p_alias, compiler_params=_params(48, 1),
    )(*_hbm(z0, z0, dq4, dk, dv, gq, gkv, wq, wk, wv, rc, rs1, rs2), *p_in)


def _sgu_fwd(z0, a_out, ln_g, ln_b, w, b_t):
    T = z0.shape[0]
    tm = min(ROW_BLOCK, T)
    W = SGU_G * SGU_C

    def body(u_ref, v_ref, a_ref, g_ref, b_ref, w_ref, bt_ref, o_ref):
        o_ref[:, :W] = a_ref[...]
        ug = _gelu(u_ref[...])
        xhat, _ = _ln_stats(_gelu(v_ref[...]))
        vn = (xhat * g_ref[...] + b_ref[...]).astype(BF16)
        tril = lax.broadcasted_iota(jnp.int32, (SGU_C, SGU_C), 0) >= lax.broadcasted_iota(jnp.int32, (SGU_C, SGU_C), 1)
        for g in range(SGU_G):
            cs = slice(g * SGU_C, (g + 1) * SGU_C)
            wg = jnp.where(tril, w_ref[g], 0.0).astype(BF16)
            bcol = bt_ref[:, g:g + 1]
            for c in range(tm // SGU_C):
                rs = slice(c * SGU_C, (c + 1) * SGU_C)
                mixed = jnp.dot(wg, vn[rs, cs], preferred_element_type=F32) + bcol
                o_ref[rs, W + g * SGU_C:W + (g + 1) * SGU_C] = (ug[rs, cs] * mixed).astype(BF16)

    full = lambda shape: pl.BlockSpec(shape, lambda i: (0,) * len(shape))
    return pl.pallas_call(
        body, name="sgu_fwd", grid=(T // tm,),
        in_specs=[pl.BlockSpec((tm, W), lambda i: (i, 1)), pl.BlockSpec((tm, W), lambda i: (i, 2)),
                  pl.BlockSpec((tm, W), lambda i: (i, 0)),
                  full((1, W)), full((1, W)), full((SGU_G, SGU_C, SGU_C)), full((SGU_C, SGU_G))],
        out_specs=pl.BlockSpec((tm, 2 * W), lambda i: (i, 0)),
        out_shape=jax.ShapeDtypeStruct((T, 2 * W), BF16),
        compiler_params=_params(32, 1),
    )(z0, z0, a_out, ln_g, ln_b, w, b_t)


def _sgu_bwd(z0, dmix, dc, dkr, ln_g, ln_b, w, b_t, plan=None):
    T = z0.shape[0]
    tm = min(ROW_BLOCK, T)
    W = SGU_G * SGU_C
    grid = (T // tm,)

    def body(*refs):
        ((u_ref, v_ref, do_ref, dc_ref, dkr_ref, g_ref, b_ref, w_ref, bt_ref),
         (dz_ref, dw_ref, db_ref, dlg_ref, dlb_ref), _, pctx) = _split_refs(refs, 9, 5, 0, plan)
        _plan_start(plan, pctx, grid)

        @pl.when(pl.program_id(0) == 0)
        def _():
            for ref in (dw_ref, db_ref, dlg_ref, dlb_ref):
                ref[...] = jnp.zeros_like(ref)

        dz_ref[:, :W] = dc_ref[...]
        dz_ref[:, 3 * W:] = dkr_ref[...]

        u, v, dout = u_ref[...], v_ref[...], do_ref[...]
        ug = _gelu(u)
        xhat, rstd = _ln_stats(_gelu(v))
        vn = (xhat * g_ref[...] + b_ref[...]).astype(BF16)
        dmixed = dout * ug
        dmixed_b = dmixed.astype(BF16)
        tril = lax.broadcasted_iota(jnp.int32, (SGU_C, SGU_C), 0) >= lax.broadcasted_iota(jnp.int32, (SGU_C, SGU_C), 1)
        lane = lax.broadcasted_iota(jnp.int32, (SGU_C, SGU_C), 1)
        dvn_cols = []
        for g in range(SGU_G):
            cs = slice(g * SGU_C, (g + 1) * SGU_C)
            wg = jnp.where(tril, w_ref[g], 0.0).astype(BF16)
            bcol = bt_ref[:, g:g + 1]
            dw_g = jnp.zeros((SGU_C, SGU_C), F32)
            db_g = jnp.zeros((SGU_C, 1), F32)
            dvn_rows = []
            for c in range(tm // SGU_C):
                rs = slice(c * SGU_C, (c + 1) * SGU_C)
                mixed = jnp.dot(wg, vn[rs, cs], preferred_element_type=F32) + bcol
                dz_ref[rs, W + g * SGU_C:W + (g + 1) * SGU_C] = (dout[rs, cs] * mixed * _gelu_grad(u[rs, cs])).astype(BF16)
                dm = dmixed_b[rs, cs]
                dw_g = dw_g + lax.dot_general(dm, vn[rs, cs], NT_DIMS, preferred_element_type=F32)
                db_g = db_g + jnp.sum(dmixed[rs, cs], axis=1, keepdims=True)
                dvn_rows.append(lax.dot_general(wg, dm, TN_DIMS, preferred_element_type=F32))
            dw_ref[g] += jnp.where(tril, dw_g, 0.0)
            db_ref[...] += jnp.where(lane == g, db_g, 0.0)
            dvn_cols.append(jnp.concatenate(dvn_rows, axis=0))
        dvn = jnp.concatenate(dvn_cols, axis=1)
        dxh = dvn * g_ref[...]
        m1 = jnp.mean(dxh, -1, keepdims=True)
        m2 = jnp.mean(dxh * xhat, -1, keepdims=True)
        dvg = rstd * (dxh - m1 - xhat * m2)
        dz_ref[:, 2 * W:3 * W] = (dvg * _gelu_grad(v)).astype(BF16)
        dlg_ref[...] += _fold8(dvn * xhat)
        dlb_ref[...] += _fold8(dvn)
        _plan_wait(plan, pctx, grid)

    full = lambda shape: pl.BlockSpec(shape, lambda i: (0,) * len(shape))
    p_in, p_ospec, p_oshape, p_scr, p_alias = _plan_io(plan, 9, 5)
    return pl.pallas_call(
        body, name="sgu_bwd", grid=grid,
        in_specs=[pl.BlockSpec((tm, W), lambda i: (i, 1)), pl.BlockSpec((tm, W), lambda i: (i, 2)),
                  pl.BlockSpec((tm, W), lambda i: (i, 1)), pl.BlockSpec((tm, W), lambda i: (i, 0)),
                  pl.BlockSpec((tm, 128), lambda i: (i, 0)),
                  full((1, W)), full((1, W)), full((SGU_G, SGU_C, SGU_C)), full((SGU_C, SGU_G))] + [_ANY] * len(p_in),
        out_specs=[pl.BlockSpec((tm, 3 * W + 128), lambda i: (i, 0)), full((SGU_G, SGU_C, SGU_C)),
                   full((SGU_C, SGU_C)), full((8, W)), full((8, W))] + p_ospec,
        out_shape=[jax.ShapeDtypeStruct((T, 3 * W + 128), BF16), jax.ShapeDtypeStruct((SGU_G, SGU_C, SGU_C), F32),
                   jax.ShapeDtypeStruct((SGU_C, SGU_C), F32), jax.ShapeDtypeStruct((8, W), F32),
                   jax.ShapeDtypeStruct((8, W), F32)] + p_oshape,
        scratch_shapes=p_scr, input_output_aliases=p_alias, compiler_params=_params(40, 1),
    )(z0, z0, dmix, dc, dkr, ln_g, ln_b, w, b_t, *p_in)


def _hg_lower_bound(lb_ref):
    a0, a1 = lb_ref[0:1, :], lb_ref[1:2, :]
    m = jnp.maximum(a0, a1)
    e0, e1 = jnp.exp(a0 - m), jnp.exp(a1 - m)
    return e1 / (e0 + e1)


def _running_sum(x, reverse=False):
    n = x.shape[0]
    row = lax.broadcasted_iota(jnp.int32, x.shape, 0)
    s = 1
    while s < n:
        if reverse:
            x = x + jnp.where(row < n - s, pltpu.roll(x, n - s, 0), 0.0)
        else:
            x = x + jnp.where(row >= s, pltpu.roll(x, s, 0), 0.0)
        s *= 2
    return x


def _hg_chunk(qc, fc, lb):
    C = HG_CHUNK
    rows = lax.broadcasted_iota(jnp.int32, (C, C), 0)
    cols = lax.broadcasted_iota(jnp.int32, (C, C), 1)
    rowid = lax.broadcasted_iota(jnp.int32, (C, 128), 0)
    sq, sg = _sigmoid(qc), _sigmoid(fc)
    qf = qc * sq
    gate = lb + (1.0 - lb) * sg
    kk = 1.0 - gate
    lg = jnp.log(gate)
    bcum = _running_sum(lg)
    b_mid = jnp.sum(jnp.where(rowid < C // 2, lg, 0.0), axis=0, keepdims=True)
    b_last = jnp.sum(lg, axis=0, keepdims=True)
    eq, ek, e, eh = jnp.exp(bcum - b_mid), jnp.exp(b_mid - bcum), jnp.exp(bcum), jnp.exp(b_last - bcum)
    qt, kt, qe, khat = qf * eq, kk * ek, qf * e, kk * eh
    a = lax.dot_general(qt.astype(BF16), kt.astype(BF16), NT_DIMS, preferred_element_type=F32)
    a = jnp.where(rows >= cols, a, 0.0)
    return dict(sq=sq, sg=sg, gate=gate, kk=kk, eq=eq, ek=ek, e=e, eh=eh, qt=qt, kt=kt, qe=qe, khat=khat, a=a,
                e_last=jnp.exp(b_last), tril=rows >= cols, rowid=rowid)


def _hgrn_fwd(z4, hg_lb, gnorm):
    T = z4.shape[1]
    tb = min(ROW_BLOCK, T)
    C = HG_CHUNK
    ncb = tb // C
    HPB = HG_HEADS_PER_STEP

    def body(q_ref, f_ref, i_ref, g_ref, lb_ref, gn_ref, y_ref, o_ref, st_ref, st_sc):
        @pl.when(pl.program_id(1) == 0)
        def _():
            st_sc[...] = jnp.zeros_like(st_sc)

        def chunk(c, carry):
            rs = pl.ds(pl.multiple_of(c * C, C), C)
            for hh in range(HPB):
                hs = slice(hh * 128, (hh + 1) * 128)
                lb = _hg_lower_bound(lb_ref.at[:, hs])
                v_b = i_ref[rs, hs].astype(BF16)
                gc = g_ref[rs, hs]
                x = _hg_chunk(q_ref[rs, hs], f_ref[rs, hs], lb)
                st = st_sc[hh]
                st_ref[hh, c] = st
                o = (jnp.dot(x["a"].astype(BF16), v_b, preferred_element_type=F32)
                     + lax.dot_general(x["qe"].astype(BF16), st.astype(BF16), NT_DIMS, preferred_element_type=F32))
                st_sc[hh] = st * x["e_last"] + lax.dot_general(v_b, x["khat"].astype(BF16), TN_DIMS,
                                                               preferred_element_type=F32)
                o_ref[rs, hs] = o
                n = o * lax.rsqrt(jnp.mean(o * o, -1, keepdims=True) + EPS)
                y_ref[rs, hs] = (n * gn_ref[:, hs] * (gc * _sigmoid(gc))).astype(BF16)
            return carry

        lax.fori_loop(0, ncb, chunk, 0, unroll=4)

    W = 128 * HPB
    zb = lambda k: pl.BlockSpec((None, tb, W), lambda h, t: (k, t, h))
    out = pl.BlockSpec((tb, W), lambda h, t: (t, h))
    return pl.pallas_call(
        body, name="hgrn_fwd", grid=(HEADS // HPB, T // tb),
        in_specs=[zb(0), zb(1), zb(2), zb(3), pl.BlockSpec((2, W), lambda h, t: (0, h)),
                  pl.BlockSpec((1, W), lambda h, t: (0, h))],
        out_specs=[out, out, pl.BlockSpec((HPB, ncb, 128, 128), lambda h, t: (h, t, 0, 0))],
        out_shape=[jax.ShapeDtypeStruct((T, D), BF16), jax.ShapeDtypeStruct((T, D), F32),
                   jax.ShapeDtypeStruct((HEADS, T // C, 128, 128), F32)],
        scratch_shapes=[pltpu.VMEM((HPB, 128, 128), F32)],
        compiler_params=_params(48, 2),
    )(*_hbm(z4, z4, z4, z4, hg_lb, gnorm))


def _hgrn_bwd(z4, o_raw, dy, states, hg_lb, gnorm):
    T = z4.shape[1]
    tb = min(ROW_BLOCK, T)
    C = HG_CHUNK
    ncb = tb // C
    nt = T // tb
    HPB = HG_HEADS_PER_STEP

    def body(q_ref, f_ref, i_ref, g_ref, o_ref, dy_ref, st_ref, lb_ref, gn_ref, dz_ref, dlb_ref, dgn_ref, dst_sc):
        @pl.when(pl.program_id(1) == 0)
        def _():
            dst_sc[...] = jnp.zeros_like(dst_sc)
            dlb_ref[...] = jnp.zeros_like(dlb_ref)
            dgn_ref[...] = jnp.zeros_like(dgn_ref)

        def chunk(cc, carry):
            for hh in range(HPB):
                one_head(ncb - 1 - cc, hh, slice(hh * 128, (hh + 1) * 128))
            return carry

        def one_head(c, hh, hs):
            rs = pl.ds(pl.multiple_of(c * C, C), C)
            lb = _hg_lower_bound(lb_ref.at[:, hs])
            gn = gn_ref[:, hs]
            qc, gc = q_ref[rs, hs], g_ref[rs, hs]
            v_b = i_ref[rs, hs].astype(BF16)
            x = _hg_chunk(qc, f_ref[rs, hs], lb)
            st, dst = st_ref[hh, c], dst_sc[hh]
            st_b, dst_b = st.astype(BF16), dst.astype(BF16)
            o, dyc = o_ref[rs, hs], dy_ref[rs, hs]
            sgg = _sigmoid(gc)
            sil = gc * sgg
            rstd = lax.rsqrt(jnp.mean(o * o, -1, keepdims=True) + EPS)
            n = o * rstd
            dgn_ref[:, hs] += _fold8(dyc * n * sil)
            dn = dyc * gn * sil
            do = rstd * (dn - n * jnp.mean(dn * n, -1, keepdims=True))
            dg = dyc * n * gn * (sgg * (1.0 + gc * (1.0 - sgg)))
            do_b = do.astype(BF16)
            da = jnp.where(x["tril"], lax.dot_general(do_b, v_b, NT_DIMS, preferred_element_type=F32), 0.0).astype(BF16)
            qt_b, kt_b, qe_b, khat_b = (x[n_].astype(BF16) for n_ in ("qt", "kt", "qe", "khat"))
            dv = (lax.dot_general(x["a"].astype(BF16), do_b, TN_DIMS, preferred_element_type=F32)
                  + lax.dot_general(khat_b, dst_b, NT_DIMS, preferred_element_type=F32))
            dqt = jnp.dot(da, kt_b, preferred_element_type=F32)
            dqe = jnp.dot(do_b, st_b, preferred_element_type=F32)
            dkt = lax.dot_general(da, qt_b, TN_DIMS, preferred_element_type=F32)
            dkhat = jnp.dot(v_b, dst_b, preferred_element_type=F32)
            dst_sc[hh] = lax.dot_general(do_b, qe_b, TN_DIMS, preferred_element_type=F32) + dst * x["e_last"]
            de_last = jnp.sum(st * dst, axis=0, keepdims=True)
            dqf = dqt * x["eq"] + dqe * x["e"]
            dkk = dkt * x["ek"] + dkhat * x["eh"]
            dkh_kh = dkhat * x["khat"]
            db = dqt * qt_b.astype(F32) - dkt * kt_b.astype(F32) + dqe * x["qe"] - dkh_kh
            db_last = jnp.sum(dkh_kh, axis=0, keepdims=True) + de_last * x["e_last"]
            db = db + jnp.where(x["rowid"] == C - 1, db_last, 0.0)
            dlg = _running_sum(db, reverse=True)
            dgate = dlg / x["gate"] - dkk
            sg, sq = x["sg"], x["sq"]
            dlb_ref[:, hs] += _fold8(dgate * (1.0 - sg)) * (lb * (1.0 - lb))
            dz_ref[0, rs, hs] = (dqf * (sq * (1.0 + qc * (1.0 - sq)))).astype(BF16)
            dz_ref[1, rs, hs] = (dgate * (1.0 - lb) * sg * (1.0 - sg)).astype(BF16)
            dz_ref[2, rs, hs] = dv.astype(BF16)
            dz_ref[3, rs, hs] = dg.astype(BF16)

        lax.fori_loop(0, ncb, chunk, 0, unroll=4)

    W = 128 * HPB
    zb = lambda k: pl.BlockSpec((None, tb, W), lambda h, t: (k, nt - 1 - t, h))
    blk = pl.BlockSpec((tb, W), lambda h, t: (nt - 1 - t, h))
    acc = pl.BlockSpec((8, W), lambda h, t: (0, h))
    return pl.pallas_call(
        body, name="hgrn_bwd", grid=(HEADS // HPB, nt),
        in_specs=[zb(0), zb(1), zb(2), zb(3), blk, blk,
                  pl.BlockSpec((HPB, ncb, 128, 128), lambda h, t: (h, nt - 1 - t, 0, 0)),
                  pl.BlockSpec((2, W), lambda h, t: (0, h)), pl.BlockSpec((1, W), lambda h, t: (0, h))],
        out_specs=[pl.BlockSpec((4, tb, W), lambda h, t: (0, nt - 1 - t, h)), acc, acc],
        out_shape=[jax.ShapeDtypeStruct((4, T, D), BF16), jax.ShapeDtypeStruct((8, D), F32),
                   jax.ShapeDtypeStruct((8, D), F32)],
        scratch_shapes=[pltpu.VMEM((HPB, 128, 128), F32)],
        compiler_params=_params(48, 2),
    )(*_hbm(z4, z4, z4, z4, o_raw, dy, states, hg_lb, gnorm))


def _adamw(w, g, m, v, *, name):
    R, L = w.shape
    tr = R if R <= 512 else 512
    assert R % tr == 0
    blk = pl.BlockSpec((tr, L), lambda i: (i, 0))
    c1, c2 = 1.0 - B1 ** STEP, 1.0 - B2 ** STEP

    def body(w_ref, g_ref, m_ref, v_ref, d_ref, mo_ref, vo_ref):
        g_ = g_ref[...]
        m_ = B1 * m_ref[...] + (1.0 - B1) * g_
        v_ = B2 * v_ref[...] + (1.0 - B2) * (g_ * g_)
        d_ref[...] = -LR * ((m_ / c1) / (jnp.sqrt(v_ / c2) + ADAM_EPS) + WD * w_ref[...])
        mo_ref[...] = m_
        vo_ref[...] = v_

    sds = jax.ShapeDtypeStruct((R, L), F32)
    return pl.pallas_call(
        body, name=name, grid=(R // tr,), in_specs=[blk] * 4, out_specs=[blk] * 3, out_shape=[sds] * 3,
        compiler_params=_params(32, 1),
    )(w, g, m, v)


def _adamw_rows(w, m, v, gbufs, row0, *, name, plan=None):
    L, R, C = w.shape
    tr = 256
    assert R % tr == 0 and row0 % tr == 0 and len(gbufs) == L
    grid = (L, R // tr)
    blk = pl.BlockSpec((None, tr, C), lambda l, i: (l, i, 0))
    gblks = [pl.BlockSpec((tr, C), lambda l, i, k=k: (row0 // tr + jnp.where(l == k, i, 0), 0)) for k in range(L)]
    c1, c2 = 1.0 - B1 ** STEP, 1.0 - B2 ** STEP

    def body(*refs):
        ins, (go_ref, d_ref, mo_ref, vo_ref), _, pctx = _split_refs(refs, 3 + L, 4, 0, plan)
        w_ref, m_ref, v_ref = ins[:3]
        g_refs = ins[3:]
        _plan_start(plan, pctx, grid)
        g_ = g_refs[0][...]
        for l in range(1, L):
            g_ = jnp.where(pl.program_id(0) == l, g_refs[l][...], g_)
        m_ = B1 * m_ref[...] + (1.0 - B1) * g_
        v_ = B2 * v_ref[...] + (1.0 - B2) * (g_ * g_)
        go_ref[...] = g_
        d_ref[...] = -LR * ((m_ / c1) / (jnp.sqrt(v_ / c2) + ADAM_EPS) + WD * w_ref[...])
        mo_ref[...] = m_
        vo_ref[...] = v_
        _plan_wait(plan, pctx, grid)

    sds = jax.ShapeDtypeStruct((L, R, C), F32)
    p_in, p_ospec, p_oshape, p_scr, p_alias = _plan_io(plan, 3 + L, 4)
    return pl.pallas_call(
        body, name=name, grid=grid, in_specs=[blk] * 3 + gblks + [_ANY] * len(p_in),
        out_specs=[blk] * 4 + p_ospec, out_shape=[sds] * 4 + p_oshape, scratch_shapes=p_scr,
        input_output_aliases=p_alias, compiler_params=_params(32, 2),
    )(*_hbm(w, m, v, *gbufs), *p_in)


def _add_pairs(g, theirs, ids, *, name):
    n, R, L = theirs.shape
    tr = math.gcd(R, 128)
    nb = R // tr

    def body(ids_ref, a_ref, b_ref, o_ref):
        o_ref[...] = (a_ref[...].astype(F32) + b_ref[...].astype(F32)).astype(BF16)

    blk = pl.BlockSpec((n, tr, L), lambda i, ids: (0, i, 0))
    return pl.pallas_call(
        body, name=name, out_shape=jax.ShapeDtypeStruct((n, R, L), BF16),
        grid_spec=pltpu.PrefetchScalarGridSpec(
            num_scalar_prefetch=1, grid=(nb,),
            in_specs=[pl.BlockSpec((n, tr, L), lambda i, ids: (0, ids[1] * nb + i, 0)), blk], out_specs=blk),
        compiler_params=_params(16, 1),
    )(ids, g, theirs)


def _sum_chips(pair, parts, ids, *, name):
    _, R, L = parts.shape
    tr = math.gcd(R, 128)

    def body(ids_ref, o_ref, r_ref, out_ref):
        out_ref[...] = ((o_ref[...].astype(F32) + r_ref[0].astype(F32)) + r_ref[1].astype(F32)) + r_ref[2].astype(F32)

    return pl.pallas_call(
        body, name=name, out_shape=jax.ShapeDtypeStruct((2, R, L), F32),
        grid_spec=pltpu.PrefetchScalarGridSpec(
            num_scalar_prefetch=1, grid=(R // tr,),
            in_specs=[pl.BlockSpec((None, tr, L), lambda i, ids: (ids[0], i, 0)),
                      pl.BlockSpec((3, tr, L), lambda i, ids: (0, i, 0))],
            out_specs=pl.BlockSpec((None, tr, L), lambda i, ids: (ids[1], i, 0))),
        compiler_params=_params(32, 1),
    )(ids, pair, parts)


def _mesh_ids():
    x, y, c = _mesh_pos()
    return jnp.stack([2 * x + y, c]).astype(jnp.int32)


def _place_shard(rows, ids, *, name):
    R, L = rows.shape
    tr = 128

    def body(ids_ref, in_ref, out_ref):
        out_ref[...] = in_ref[...].astype(BF16)

    return pl.pallas_call(
        body, name=name, out_shape=jax.ShapeDtypeStruct((4, R, L), BF16),
        grid_spec=pltpu.PrefetchScalarGridSpec(
            num_scalar_prefetch=1, grid=(R // tr,), in_specs=[pl.BlockSpec((tr, L), lambda i, ids: (i, 0))],
            out_specs=pl.BlockSpec((None, tr, L), lambda i, ids: (ids[0], i, 0))),
        compiler_params=_params(16, 1),
    )(ids, rows)


def _place_weights(pieces, buffer_rows, ids, *, plan=None):
    tr = 256
    steps, s = [], 0
    for arr, layer, buf, row0 in pieces:
        nblk = arr.shape[1] // tr
        steps.append((s, nblk))
        s += nblk
    total = s
    buf_start = [min(st for (st, _), p in zip(steps, pieces) if p[2] == k) for k in range(len(buffer_rows))]
    grid = (total,)
    n_in = len(pieces)

    def body(*refs):
        ins, outs, _, pctx = _split_refs(refs[1:], n_in, len(buffer_rows), 0, plan)
        _plan_start(plan, pctx, grid)
        i = pl.program_id(0)
        for (st, nblk), (_, _, buf, _), ref in zip(steps, pieces, ins):
            @pl.when((i >= st) & (i < st + nblk))
            def _(ref=ref, buf=buf):
                outs[buf][...] = ref[...].astype(BF16)
        _plan_wait(plan, pctx, grid)

    in_specs = [pl.BlockSpec((None, tr, D), lambda i, ids, layer=layer, st=st, nblk=nblk:
                             (layer, jnp.clip(i - st, 0, nblk - 1), 0))
                for (st, nblk), (_, layer, _, _) in zip(steps, pieces)]
    out_specs = [pl.BlockSpec((None, tr, D), lambda i, ids, st=st, nb=rows // tr: (ids[0], jnp.clip(i - st, 0, nb - 1), 0))
                 for st, rows in zip(buf_start, buffer_rows)]
    p_in, p_ospec, p_oshape, p_scr, p_alias = _plan_io(plan, 1 + n_in, len(buffer_rows))
    return pl.pallas_call(
        body, name="place_weights",
        out_shape=[jax.ShapeDtypeStruct((4, rows, D), BF16) for rows in buffer_rows] + p_oshape,
        grid_spec=pltpu.PrefetchScalarGridSpec(
            num_scalar_prefetch=1, grid=grid, in_specs=in_specs + [_ANY] * len(p_in), out_specs=out_specs + p_ospec,
            scratch_shapes=p_scr),
        input_output_aliases=p_alias, compiler_params=_params(16, 1),
    )(ids, *[p[0] for p in pieces], *p_in)


def _remote(src, dst, send_sem, recv_sem, to):
    return pltpu.make_async_remote_copy(src_ref=src, dst_ref=dst, send_sem=send_sem, recv_sem=recv_sem,
                                        device_id=to, device_id_type=MESH_IDS)


def _rows(ref, lead, start, size):
    return ref.at[tuple(pl.ds(0, n) for n in ref.shape[:lead]) + (pl.ds(start, size),)]


def _other_chips():
    x, y, _ = _mesh_pos()
    return [(1 - x, y), (x, 1 - y), (1 - x, 1 - y)]


def _plan_gather_ici(bufs):
    n = len(bufs)

    def copies(outs, send, recv):
        x, y, c = _mesh_pos()
        res = []
        for b in range(n):
            half = bufs[b].shape[1] // 2
            mine = _rows(outs[b].at[2 * x + y], 0, c * half, half)
            for j, (cx, cy) in enumerate(_other_chips()):
                res.append((_remote(mine, mine, send(3 * b + j), recv(3 * b + j), (cx, cy, c)),
                            _remote(mine, _rows(outs[b].at[2 * cx + cy], 0, c * half, half),
                                    send(3 * b + j), recv(3 * b + j), (x, y, c))))
        return res

    def start(ins, outs, send, recv, loc):
        for out_cp, _ in copies(outs, send, recv):
            out_cp.start()

    def wait(ins, outs, send, recv, loc):
        for out_cp, in_cp in copies(outs, send, recv):
            in_cp.wait_recv()
            out_cp.wait_send()

    outs = [jax.ShapeDtypeStruct(b.shape, b.dtype) for b in bufs]
    return _Plan(bufs, outs, 3 * n, 0, start, wait, aliases={b: b for b in range(n)})


def _plan_gather_forward(bufs):
    n = len(bufs)

    def copies(outs, send, recv):
        x, y, c = _mesh_pos()
        res = []
        for b in range(n):
            half = bufs[b].shape[1] // 2
            for j, (cx, cy) in enumerate(_other_chips()):
                slot = outs[b].at[2 * cx + cy]
                res.append((_remote(_rows(slot, 0, c * half, half), _rows(slot, 0, c * half, half),
                                    send(3 * b + j), recv(3 * b + j), (x, y, 1 - c)),
                            _remote(_rows(slot, 0, c * half, half), _rows(slot, 0, (1 - c) * half, half),
                                    send(3 * b + j), recv(3 * b + j), (x, y, c))))
        return res

    def start(ins, outs, send, recv, loc):
        for out_cp, _ in copies(outs, send, recv):
            out_cp.start()

    def wait(ins, outs, send, recv, loc):
        for out_cp, in_cp in copies(outs, send, recv):
            in_cp.wait_recv()
            out_cp.wait_send()

    outs = [jax.ShapeDtypeStruct(b.shape, b.dtype) for b in bufs]
    return _Plan(bufs, outs, 3 * n, 0, start, wait, aliases={b: b for b in range(n)})


def _plan_pair_swap(g):
    half = g.shape[1] // 2

    def copy(ins, outs, send, recv, loc):
        x, y, c = _mesh_pos()
        return _remote(_rows(ins[0], 1, (1 - c) * half, half), outs[0], send(0), recv(0), (x, y, 1 - c))

    return _Plan([g], [jax.ShapeDtypeStruct((4, half, g.shape[2]), g.dtype)], 1, 0,
                 lambda *a: copy(*a).start(), lambda *a: copy(*a).wait())


def _plan_pair_gather(buf):
    def copies(ins, outs, send, recv, loc):
        x, y, c = _mesh_pos()
        return (_remote(outs[0].at[c], outs[0].at[c], send(0), recv(0), (x, y, 1 - c)),
                _remote(outs[0].at[c], outs[0].at[1 - c], send(0), recv(0), (x, y, c)))

    def wait(*a):
        out_cp, in_cp = copies(*a)
        in_cp.wait_recv()
        out_cp.wait_send()

    return _Plan([buf], [jax.ShapeDtypeStruct(buf.shape, buf.dtype)], 1, 0, lambda *a: copies(*a)[0].start(), wait,
                 aliases={0: 0})


def _plan_chip_scatter(p):
    def copies(ins, outs, send, recv, loc):
        _, _, c = _mesh_pos()
        return [_remote(ins[0].at[2 * cx + cy], outs[0].at[j], send(j), recv(j), (cx, cy, c))
                for j, (cx, cy) in enumerate(_other_chips())]

    def start(*a):
        for cp in copies(*a):
            cp.start()

    def wait(*a):
        for cp in copies(*a):
            cp.wait()

    return _Plan([p], [jax.ShapeDtypeStruct((3,) + p.shape[1:], p.dtype)], 3, 0, start, wait)


def _plan_exchange_all(vec):
    def copies(ins, outs, send, recv, loc):
        x, y, c = _mesh_pos()
        return [_remote(ins[0], outs[0].at[r - 1], send(r - 1), recv(r - 1), (x ^ (r >> 2), y ^ ((r >> 1) & 1), c ^ (r & 1)))
                for r in range(1, 8)]

    def start(*a):
        for cp in copies(*a):
            cp.start()

    def wait(*a):
        for cp in copies(*a):
            cp.wait()

    return _Plan([vec], [jax.ShapeDtypeStruct((7,) + vec.shape, vec.dtype)], 7, 0, start, wait)


SMALL_LAYOUT = {
    "mla_gq": (0, 1, 256, (1, 256)), "mla_gkv": (1, 1, 256, (1, 256)), "sgu_ln_g": (2, 1, 512, (1, 512)),
    "sgu_ln_b": (3, 1, 512, (1, 512)), "sgu_w": (4, 64, 1024, (64, 1024)), "sgu_b": (68, 1, 512, (1, 512)),
    "hg_lb": (69, 2, 1024, (2, 1024)), "hg_gnorm": (71, 1, 1024, (1, 256)), "ln1_g": (72, 2, 1024, (2, 1024)),
    "ln1_b": (74, 2, 1024, (2, 1024)), "ln2_g": (76, 2, 1024, (2, 1024)), "ln2_b": (78, 2, 1024, (2, 1024)),
}


def _small_pack(dgq, dgkv, dslg, dslb, dsw, dsb, dlb, dgn, ln_parts, sq_err):
    flat_ln = [p for pair in ln_parts for p in pair]

    def body(*refs):
        gq_ref, gkv_ref, slg_ref, slb_ref, sw_ref, sb_ref, lb_ref, gn_ref = refs[:8]
        ln_refs, err_ref, out_ref, t_sc = refs[8:16], refs[16], refs[17], refs[18]
        s8 = lambda ref: jnp.sum(ref[...], axis=0, keepdims=True)
        out_ref[...] = jnp.zeros_like(out_ref)
        out_ref[0:1, 0:256] = s8(gq_ref)
        out_ref[1:2, 0:256] = s8(gkv_ref)
        out_ref[2:3, 0:512] = s8(slg_ref)
        out_ref[3:4, 0:512] = s8(slb_ref)
        out_ref[4:68, :] = sw_ref[...]
        t_sc[...] = sb_ref[...].T
        for g in range(SGU_G):
            out_ref[68:69, g * SGU_C:(g + 1) * SGU_C] = t_sc[g:g + 1, :]
        d_lb1 = s8(lb_ref)
        out_ref[69:70, :] = -d_lb1
        out_ref[70:71, :] = d_lb1
        out_ref[71:72, :] = s8(gn_ref)
        for k, ref in enumerate(ln_refs):
            out_ref[72 + k:73 + k, :] = s8(ref)
        out_ref[0:1, 1023:1024] = jnp.sum(s8(err_ref), axis=1, keepdims=True) * (0.5 / D)

    vm = pl.BlockSpec(memory_space=pltpu.VMEM)
    return pl.pallas_call(
        body, name="small_grad_pack", in_specs=[vm] * 17, out_specs=vm,
        out_shape=jax.ShapeDtypeStruct((SMALL_ROWS, 1024), F32), scratch_shapes=[pltpu.VMEM((SGU_C, SGU_C), F32)],
        compiler_params=_params(16),
    )(dgq, dgkv, dslg, dslb, dsw.reshape(64, 1024), dsb, dlb, dgn, *flat_ln, sq_err)


def _small_update(vec, others, ids, w, m, v):
    names = list(SMALL_LAYOUT)
    n = len(names)
    c1, c2 = 1.0 - B1 ** STEP, 1.0 - B2 ** STEP
    have_others = others is not None

    def body(*refs):
        ids_ref, v_ref = refs[0], refs[1]
        k = 2 + have_others
        w_refs, m_refs, v_refs = refs[k:k + n], refs[k + n:k + 2 * n], refs[k + 2 * n:k + 3 * n]
        outs = refs[k + 3 * n:]
        row0_ref, tot_sc = outs[0], outs[-1]
        total = v_ref[...]
        if have_others:
            me = 2 * ids_ref[0] + ids_ref[1]
            total = None
            for d in range(8):
                rel = d ^ me
                term = jnp.where(rel == 0, v_ref[...], refs[2][jnp.maximum(rel - 1, 0)])
                total = term if total is None else total + term
        tot_sc[...] = total
        row0_ref[...] = tot_sc[0:1, :]
        for i, name in enumerate(names):
            r0, nr, width, _ = SMALL_LAYOUT[name]
            if name == "hg_gnorm":
                g_ = tot_sc[r0:r0 + 1, 0:256]
                for chip in range(1, 4):
                    g_ = jnp.where(ids_ref[0] == chip, tot_sc[r0:r0 + 1, chip * 256:(chip + 1) * 256], g_)
            else:
                g_ = tot_sc[r0:r0 + nr, 0:width]
            m_ = B1 * m_refs[i][...] + (1.0 - B1) * g_
            v_ = B2 * v_refs[i][...] + (1.0 - B2) * (g_ * g_)
            go, do, mo, vo = outs[1 + 4 * i:5 + 4 * i]
            go[...] = g_
            do[...] = -LR * ((m_ / c1) / (jnp.sqrt(v_ / c2) + ADAM_EPS) + WD * w_refs[i][...])
            mo[...] = m_
            vo[...] = v_

    full = lambda shape: pl.BlockSpec(shape, lambda i, ids, nd=len(shape): (0,) * nd)
    kshapes = [SMALL_LAYOUT[name][3] for name in names]
    operands = [vec] + ([others] if have_others else []) + [d[name] for d in (w, m, v) for name in names]
    out_shapes = [jax.ShapeDtypeStruct((1, 1024), F32)] + [jax.ShapeDtypeStruct(s, F32) for s in kshapes for _ in range(4)]
    res = pl.pallas_call(
        body, name="small_update", out_shape=out_shapes,
        grid_spec=pltpu.PrefetchScalarGridSpec(
            num_scalar_prefetch=1, grid=(1,), in_specs=[full(o.shape) for o in operands],
            out_specs=[full(s.shape) for s in out_shapes],
            scratch_shapes=[pltpu.VMEM((SMALL_ROWS, 1024), F32)]),
        compiler_params=_params(32, 1),
    )(ids, *operands)
    return res[0], {name: tuple(res[1 + 4 * i:5 + 4 * i]) for i, name in enumerate(names)}


ROWS_L1, ROWS_L0, ROWS_ODD_W = 3328, 2048, 384
ODD_PARTS = (("w_out_e", (256, 1024)), ("w_in_e", (1024, 392)), ("w_qb", (256, 192)), ("w_kvb", (256, 256)))
ODD_W_PARTS = tuple(p for p in ODD_PARTS if p[0] != "w_in_e")


def _odd_rows(parts, dtype, layout, total, gnorm=None):
    rows = [parts[n].reshape(-1, 1024).astype(dtype) for n, _ in layout]
    used = sum(r.shape[0] for r in rows)
    if gnorm is not None:
        bits = lax.bitcast_convert_type(gnorm.reshape(-1), BF16).reshape(1, 512)
        rows.append(jnp.pad(bits, ((0, 15), (0, 512))))
        used += 16
    if total > used:
        rows.append(jnp.zeros((total - used, 1024), dtype))
    return jnp.concatenate(rows, axis=0)


def _odd_unrows(buf, layout, with_gnorm=False):
    out, off = {}, 0
    for n, shape in layout:
        nr = math.prod(shape) // 1024
        out[n] = buf[off:off + nr].reshape(shape)
        off += nr
    if with_gnorm:
        out["hg_gnorm"] = lax.bitcast_convert_type(buf[off, :512].reshape(256, 2), F32).reshape(1, 256)
    return out


def _rope_tables(positions):
    half = ROPE // 2
    inv_freq = ROPE_BASE ** (-jnp.arange(half, dtype=F32) / half)
    ang = positions.astype(F32).reshape(-1, 1) * inv_freq
    cos, sin = jnp.cos(ang), jnp.sin(ang)
    T = ang.shape[0]
    one, z16, z32 = jnp.ones((T, NOPE), F32), jnp.zeros((T, half), F32), jnp.zeros((T, 32), F32)
    z64 = jnp.zeros((T, NOPE), F32)
    c = jnp.concatenate([one, cos, cos, z32], axis=1)
    s1 = jnp.concatenate([z64, -sin, z16, z32], axis=1)
    s2 = jnp.concatenate([z64, z16, sin, z32], axis=1)
    return c, s1, s2


def _local_step(x, positions, tgt, odd, bufs, P, exchange):
    T = x.shape[0]
    row = lambda a: a.reshape(1, -1)
    rc, rs1, rs2 = _rope_tables(positions)
    blk = lambda f: pl.BlockSpec((None, D, D), f)

    w_in_e = odd["w_in_e"]
    w_in = jnp.concatenate([w_in_e[:, :512], w_in_e[:, 544:1568], w_in_e[:, 512:544], jnp.zeros((D, 96), BF16)], axis=1)
    wq = jnp.pad(odd["w_qb"].reshape(256, HEADS, NOPE + ROPE), ((0, 0), (0, 0), (0, 32))).reshape(256, HEADS * 128)
    kvb = odd["w_kvb"].reshape(256, HEADS, NOPE + VDIM)
    wk = jnp.pad(kvb[:, :, :NOPE], ((0, 0), (0, 0), (0, 64))).reshape(256, HEADS * 128)
    wv = kvb[:, :, NOPE:].reshape(256, HEADS * VDIM)
    w_out_e = odd["w_out_e"]
    sgu_w = P["sgu_w"][0]
    sgu_bt = P["sgu_b"][0].T
    gq, gkv = P["mla_gq"], P["mla_gkv"]
    gnorm = P["hg_gnorm"]

    z0 = _matmul(x, w_in, name="in_proj_e", M=T, N=1664, K=D, tn=1664)[0]
    q, k, v = _mla_prep(z0, gq, gkv, wq, wk, wv, rc, rs1, rs2)
    if exchange:
        ids = _mesh_ids()
        placed = list(bufs)
        a_out, lse, wga, wgb = _flash_fwd(q, k, v, plan=_plan_gather_ici(placed[:2]))
    else:
        a_out, lse = _flash_fwd(q, k, v)
        wga, wgb, wgc = bufs
    mix0 = _sgu_fwd(z0, a_out, P["sgu_ln_g"], P["sgu_ln_b"], sgu_w, sgu_bt)
    res = _proj_ln(mix0, w_out_e, x, row(P["ln1_g"][0]), row(P["ln1_b"][0]), name="out_proj_ln_e",
                   plan=_plan_gather_forward([wga, wgb]) if exchange else None)
    r1, h1b = res[:2]
    if exchange:
        wga, wgb = res[2:]
    ln = lambda name, l: (row(P[name + "_g"][l]), row(P[name + "_b"][l]))
    res = _ffn_ln(h1b, wga, r1, *ln("ln2", 0), name="ffn_ln_0", prev_ln=ln("ln1", 0),
                  plan=_plan_gather_ici(placed[2:]) if exchange else None)
    ra0, r2, h2b = res[:3]
    z4 = _matmul(h2b, wgb, name="in_proj_o", M=T, N=4 * D, K=D, b_spec=blk(lambda i, j, k: (j, 0, 0)),
                 out_shape=jax.ShapeDtypeStruct((4, T, D), F32),
                 o_spec=pl.BlockSpec((None, min(MM_ROWS, T), D), lambda i, j, k: (j, i, 0)))[0]
    y1, o_raw, states = _hgrn_fwd(z4, P["hg_lb"], gnorm)
    res2 = _proj_ln(y1, wgb, r2, *ln("ln1", 1), name="out_proj_ln_o", prev_ln=ln("ln2", 0), w_rowblk=4,
                    plan=_plan_gather_forward([res[3]]) if exchange else None)
    r3, h3b = res2[:2]
    if exchange:
        wgc = res2[2]
    ra1, r4, _ = _ffn_ln(h3b, wgc, r3, *ln("ln2", 1), name="ffn_ln_1", prev_ln=ln("ln1", 1))

    ln1_g, ln1_b, ln2_g, ln2_b = [None, None], [None, None], [None, None], [None, None]
    sq_err_parts = []

    def ffn_bwd(l, dh, r_out, ra, h_mid_b, g2, wg, rows, plan=None, loss_head=()):
        dr, dr_b, dg, db, *sq_err = _ln_bwd(dh, r_out, row(g2), name=f"ln2_bwd_{l}", loss_head=loss_head)
        sq_err_parts.extend(sq_err)
        ln2_g[l], ln2_b[l] = dg, db
        da, *extra = _matmul(dr_b, wg, tb=True, mul=ra, out_dtype=BF16, name=f"ffn_da_{l}", M=T, N=4 * D, K=D,
                             b_spec=blk(lambda i, j, k: (j, 1, 0)), plan=plan)
        gbuf = _matmul(ra, dr_b, ta=True, a_sq=True, name=f"ffn_dw2_{l}", M=4 * D, N=D, K=T, tm=1024, tk=DW_TOKENS // 2,
                       out_shape=jax.ShapeDtypeStruct((4, rows, D), BF16), o_spec=blk(lambda i, j, k: (i, 1, 0)))[0]
        gbuf = _matmul(h_mid_b, da, ta=True, name=f"ffn_dw1_{l}", M=D, N=4 * D, K=T, tm=1024, tk=DW_TOKENS, into=gbuf,
                       out_shape=jax.ShapeDtypeStruct((4, rows, D), BF16), o_spec=blk(lambda i, j, k: (j, 0, 0)))[0]
        dh_mid = _matmul(da, wg, tb=True, add=dr, add_scale=ALPHA, name=f"ffn_dh_{l}", M=T, N=D, K=4 * D, tk=2 * D,
                         b_spec=pl.BlockSpec((2, D, D), lambda i, j, k: (k, 0, 0)))[0]
        return dh_mid, gbuf, extra

    dh3, g1, _ = ffn_bwd(1, None, r4, ra1, h3b, P["ln2_g"][1], wgc, ROWS_L1, loss_head=(row(P["ln2_b"][1]), tgt))
    loss_parts = sq_err_parts[0]
    dr3, dr3_b, dg, db = _ln_bwd(dh3, r3, row(P["ln1_g"][1]), name="ln1_bwd_1")
    ln1_g[1], ln1_b[1] = dg, db
    g1_sds = jax.ShapeDtypeStruct((4, ROWS_L1, D), BF16)
    g1 = _matmul(y1, dr3_b, ta=True, name="dw_out_o", M=D, N=D, K=T, tm=256, tk=DW_TOKENS, into=g1, out_shape=g1_sds,
                 o_spec=pl.BlockSpec((None, 256, D), lambda i, j, k: (i, 12, 0)))[0]
    dmix1 = _matmul(dr3_b, wgb, tb=True, name="dmix_o", M=T, N=D, K=D, b_spec=_rows4_spec(4, 3), b_merge=(D, D))[0]
    dz4, dlb, dgn = _hgrn_bwd(z4, o_raw, dmix1, states, P["hg_lb"], gnorm)
    g1 = _matmul(h2b, dz4, ta=True, name="dw_in_o", M=D, N=4 * D, K=T, tm=1024, tk=DW_TOKENS, into=g1, out_shape=g1_sds,
                 b_spec=pl.BlockSpec((None, min(DW_TOKENS, T), D), lambda i, j, k: (j, k, 0)),
                 o_spec=blk(lambda i, j, k: (j, 2, 0)))[0]
    dh2 = _matmul(dz4, wgb, tb=True, add=dr3, add_scale=ALPHA, name="dh_in_o", M=T, N=D, K=4 * D, tk=2 * D,
                  a_spec=pl.BlockSpec((2, min(MM_ROWS, T), D), lambda i, j, k: (k, i, 0)),
                  b_spec=pl.BlockSpec((2, D, D), lambda i, j, k: (k, 0, 0)))[0]

    dh1, g0, swapped1 = ffn_bwd(0, dh2, r2, ra0, h1b, P["ln2_g"][0], wga, ROWS_L0,
                                plan=_plan_pair_swap(g1) if exchange else None)
    dr1, dr1_b, dg, db = _ln_bwd(dh1, r1, row(P["ln1_g"][0]), name="ln1_bwd_0")
    ln1_g[0], ln1_b[0] = dg, db
    godd = {"w_out_e": _matmul(mix0, dr1_b, ta=True, name="dw_out_e", M=D, N=D, K=T, tm=1024, tk=DW_TOKENS)[0]}
    dmix0, *swapped0 = _matmul(dr1_b, w_out_e, tb=True, name="dmix_e", M=T, N=D, K=D,
                               plan=_plan_pair_swap(g0) if exchange else None)
    delta, do_b = _attn_delta(dmix0, a_out)
    if exchange:
        pair1 = _add_pairs(g1, swapped1[0], ids, name="grad_pair_add_1")
        pair0 = _add_pairs(g0, swapped0[0], ids, name="grad_pair_add_0")
        dq4, dk, dv, parts0, parts1 = _flash_bwd(
            q, k, v, do_b, lse, delta, plan=_join_plans([_plan_chip_scatter(pair0), _plan_chip_scatter(pair1)]))
        half0 = _sum_chips(pair0, parts0, ids, name="grad_chip_sum_0")
        half1 = _sum_chips(pair1, parts1, ids, name="grad_chip_sum_1")
        dc, dkr, dwq, dwk, dwv, dgq, dgkv, g0, g1 = _mla_bwd(
            z0, dq4, dk, dv, gq, gkv, wq, wk, wv, rc, rs1, rs2,
            plan=_join_plans([_plan_pair_gather(half0), _plan_pair_gather(half1)]))
        g0, g1 = g0.reshape(ROWS_L0, D), g1.reshape(ROWS_L1, D)
    else:
        dq4, dk, dv = _flash_bwd(q, k, v, do_b, lse, delta)
        dc, dkr, dwq, dwk, dwv, dgq, dgkv = _mla_bwd(z0, dq4, dk, dv, gq, gkv, wq, wk, wv, rc, rs1, rs2)
    godd["w_qb"] = dwq.reshape(256, HEADS, 128)[:, :, :NOPE + ROPE].reshape(256, HEADS * (NOPE + ROPE))
    godd["w_kvb"] = jnp.concatenate([dwk.reshape(256, HEADS, 128)[:, :, :NOPE], dwv.reshape(256, HEADS, VDIM)],
                                    axis=2).reshape(256, HEADS * (NOPE + VDIM))
    swap_b = None
    if exchange:
        by_chip = [_odd_rows({"w_out_e": jnp.split(godd["w_out_e"], 4, axis=0)[j],
                              **{n: jnp.split(godd[n], 4, axis=1)[j] for n in ("w_qb", "w_kvb")}}, BF16,
                             ODD_W_PARTS, ROWS_ODD_W)
                   for j in range(4)]
        odd_b = jnp.stack(by_chip)
        swap_b = _plan_pair_swap(odd_b)
    dz0, dsw, dsb, dslg, dslb, *theirs_b = _sgu_bwd(z0, dmix0, dc, dkr, P["sgu_ln_g"], P["sgu_ln_b"], sgu_w, sgu_bt,
                                                    plan=swap_b)
    small_vec = _small_pack(dgq, dgkv, dslg, dslb, dsw, dsb, dlb, dgn, [ln1_g, ln1_b, ln2_g, ln2_b], loss_parts)
    plan_in = None
    if exchange:
        pair_b = _add_pairs(odd_b, theirs_b[0], ids, name="odd_pair_add_1")
        plan_in = _join_plans([_plan_exchange_all(small_vec), _plan_chip_scatter(pair_b)])
    dw_in, *carried = _matmul(x, dz0, ta=True, name="dw_in_e", M=D, N=1664, K=T, tm=1024, tn=1664, tk=DW_TOKENS // 4,
                              plan=plan_in)
    godd["w_in_e"] = jnp.concatenate([dw_in[:, :512], dw_in[:, 1536:1568], dw_in[:, 512:1536]], axis=1)
    plan_x = None
    if exchange:
        small_others, parts_b = carried
        odd_a = godd["w_in_e"].reshape(D, 4, 392).transpose(1, 0, 2).astype(BF16)
        theirs_a = _run_plan(_plan_pair_swap(odd_a), name="odd_pair_swap")[0]
        pair_a = _add_pairs(odd_a, theirs_a, ids, name="odd_pair_add_0")
        plan_x = _plan_chip_scatter(pair_a)
    grad_x, *parts_a = _matmul(dz0, w_in, tb=True, add=dr1, add_scale=ALPHA, name="dx", M=T, N=D, K=1664, tk=1664,
                               plan=plan_x)
    if exchange:
        godd = ([pair_a, pair_b], [parts_a[0], parts_b])
        return grad_x, g0, g1, godd, small_vec, small_others
    return grad_x, g0, g1, godd, small_vec, None


WEIGHTS = ['w_in_e', 'mla_gq', 'mla_gkv', 'w_qb', 'w_kvb', 'sgu_ln_g', 'sgu_ln_b', 'sgu_w', 'sgu_b', 'w_out_e',
           'w_in_o', 'hg_lb', 'hg_gnorm', 'w_out_o', 'ln1_g', 'ln1_b', 'w_ff1', 'w_ff2', 'ln2_g', 'ln2_b']


def kernel(x, positions, w_in_e, mla_gq, mla_gkv, w_qb, w_kvb, sgu_ln_g, sgu_ln_b, sgu_w, sgu_b, w_out_e, w_in_o, hg_lb, hg_gnorm, w_out_o, ln1_g, ln1_b, w_ff1, w_ff2, ln2_g, ln2_b, loss_target, m_w_in_e, m_mla_gq, m_mla_gkv, m_w_qb, m_w_kvb, m_sgu_ln_g, m_sgu_ln_b, m_sgu_w, m_sgu_b, m_w_out_e, m_w_in_o, m_hg_lb, m_hg_gnorm, m_w_out_o, m_ln1_g, m_ln1_b, m_w_ff1, m_w_ff2, m_ln2_g, m_ln2_b, v_w_in_e, v_mla_gq, v_mla_gkv, v_w_qb, v_w_kvb, v_sgu_ln_g, v_sgu_ln_b, v_sgu_w, v_sgu_b, v_w_out_e, v_w_in_o, v_hg_lb, v_hg_gnorm, v_w_out_o, v_ln1_g, v_ln1_b, v_w_ff1, v_w_ff2, v_ln2_g, v_ln2_b):
    args = dict(locals())
    w = {n: args[n] for n in WEIGHTS}
    m = {n: args["m_" + n] for n in WEIGHTS}
    v = {n: args["v_" + n] for n in WEIGHTS}
    cx, cy, cc = _mesh_pos()
    chip = 2 * cx + cy

    odd_shard = _odd_rows({"w_out_e": w_out_e[0], "w_qb": w_qb[0], "w_kvb": w_kvb[0]}, BF16, ODD_W_PARTS, ROWS_ODD_W,
                          gnorm=hg_gnorm)
    ids = _mesh_ids()
    placed = [_place_shard(w_in_e[0], ids, name="place_shard_in_e"), _place_shard(odd_shard, ids, name="place_shard_odd")]
    pieces = [(w_ff1, 0, 0, 0), (w_ff2, 0, 0, 1024), (w_in_o, 0, 1, 0), (w_out_o, 0, 1, 1024),
              (w_ff1, 1, 2, 0), (w_ff2, 1, 2, 1024)]
    *big_bufs, odd_a, odd_b = _place_weights(pieces, (2048, 1280, 2048), ids, plan=_plan_gather_ici(placed))
    gathered = _run_plan(_plan_gather_forward([odd_a, odd_b]), name="odd_gather_forward")
    per_chip = [_odd_unrows(gathered[1][j], ODD_W_PARTS, with_gnorm=True) for j in range(4)]
    odd = {"w_out_e": jnp.concatenate([p["w_out_e"] for p in per_chip], axis=0),
           "w_in_e": jnp.concatenate([gathered[0][j] for j in range(4)], axis=1)}
    for n in ("w_qb", "w_kvb"):
        odd[n] = jnp.concatenate([p[n] for p in per_chip], axis=1)
    small = {n: w[n] for n in SMALL_LAYOUT if n != "hg_gnorm"}
    small["hg_gnorm"] = jnp.concatenate([p["hg_gnorm"] for p in per_chip], axis=1)
    grad_x, g_l0, g_l1, godd, small_vec, small_others = _local_step(
        x[0], positions[0], loss_target[0], odd, big_bufs, small, True)

    sums = [_sum_chips(pair, parts, ids, name=f"odd_chip_sum_{k}") for k, (pair, parts) in enumerate(zip(*godd))]
    g_in_e, g_rest = _run_plan(_join_plans([_plan_pair_gather(s) for s in sums]), name="odd_pair_gather")
    g_odd = _odd_unrows(g_rest.reshape(ROWS_ODD_W, 1024), ODD_W_PARTS)
    g_odd["w_in_e"] = g_in_e.reshape(D, 392)

    to_kernel = lambda d: {n: d[n].reshape(SMALL_LAYOUT[n][3]) for n in SMALL_LAYOUT}
    first_row, small_out = _small_update(small_vec, small_others, ids, to_kernel(w), to_kernel(m), to_kernel(v))
    loss = first_row[0, 1023]
    grads, delta, new_m, new_v = {}, {}, {}, {}
    for n, res in small_out.items():
        grads[n], delta[n], new_m[n], new_v[n] = (r.reshape(w[n].shape) for r in res)

    for n, bufs_, row0 in (("w_ff1", [g_l0, g_l1], 0), ("w_ff2", [g_l0, g_l1], 1024), ("w_in_o", [g_l1], 2048),
                           ("w_out_o", [g_l1], 3072)):
        grads[n], delta[n], new_m[n], new_v[n] = _adamw_rows(w[n], m[n], v[n], bufs_, row0, name=f"adamw_{n}")
    for n, _ in ODD_PARTS:
        grads[n] = g_odd[n][None]
        d_, m_, v_ = _adamw(w[n][0], g_odd[n], m[n][0], v[n][0], name=f"adamw_{n}")
        delta[n], new_m[n], new_v[n] = d_[None], m_[None], v_[None]

    return (loss, grad_x[None], *[grads[n] for n in WEIGHTS], *[delta[n] for n in WEIGHTS],
            *[new_m[n] for n in WEIGHTS], *[new_v[n] for n in WEIGHTS])
```

```python
import math

import jax
import jax.numpy as jnp
from jax import lax
from jax.experimental import pallas as pl
from jax.experimental.pallas import tpu as pltpu

F32 = jnp.float32
BF16 = jnp.bfloat16
MESH_IDS = pl.DeviceIdType.MESH

D = 1024
DEPTH = 2
HEADS = 8
NOPE, ROPE, VDIM = 64, 32, 64
QK_SCALE = (NOPE + ROPE) ** -0.5
ROPE_BASE = 10000.0
SGU_G, SGU_C = 4, 128
HG_CHUNK = 64
HG_HEADS_PER_STEP = 8
ALPHA = (2 * DEPTH) ** 0.25
EPS = 1e-5
LR, B1, B2, ADAM_EPS, WD, STEP = 0.001, 0.9, 0.999, 1e-08, 0.01, 10
GELU_C = math.sqrt(2.0 / math.pi)
GELU_A = 0.044715
MB = 1024 * 1024
ROW_BLOCK = 512
SMALL_ROWS = 80

NT_DIMS = (((1,), (1,)), ((), ()))
TN_DIMS = (((0,), (0,)), ((), ()))


def _params(vmem_mb, n_axes=0):
    kw = dict(vmem_limit_bytes=vmem_mb * MB)
    if n_axes:
        kw["dimension_semantics"] = ("arbitrary",) * n_axes
    return pltpu.CompilerParams(**kw)


_ANY = pl.BlockSpec(memory_space=pltpu.HBM)


def _mesh_pos():
    return lax.axis_index("x"), lax.axis_index("y"), lax.axis_index("c")


def _hbm(*arrays):
    return tuple(pltpu.with_memory_space_constraint(a, pltpu.HBM) if a.size >= 2 ** 18 else a for a in arrays)


class _Plan:
    def __init__(self, ins, outs, n_remote, n_local, start, wait, aliases=None):
        self.ins, self.outs, self.n_remote, self.n_local = list(ins), list(outs), n_remote, n_local
        self.start, self.wait, self.aliases = start, wait, dict(aliases or {})


def _join_plans(plans):
    ins, outs, aliases, parts = [], [], {}, []
    nr = nl = 0
    for p in plans:
        parts.append((p, len(ins), len(outs), nr, nl))
        aliases.update({len(ins) + i: len(outs) + o for i, o in p.aliases.items()})
        ins += p.ins
        outs += p.outs
        nr += p.n_remote
        nl += p.n_local

    def run(which):
        def go(in_refs, out_refs, send, recv, loc):
            for p, i0, o0, r0, l0 in parts:
                getattr(p, which)(in_refs[i0:i0 + len(p.ins)], out_refs[o0:o0 + len(p.outs)],
                                  lambda i, r0=r0: send(r0 + i), lambda i, r0=r0: recv(r0 + i),
                                  lambda i, l0=l0: loc(l0 + i))
        return go

    return _Plan(ins, outs, nr, nl, run("start"), run("wait"), aliases)


def _plan_io(plan, n_in, n_out):
    if plan is None:
        return [], [], [], [], {}
    sems = [pltpu.SemaphoreType.DMA((max(plan.n_remote, 1),)), pltpu.SemaphoreType.DMA((max(plan.n_remote, 1),)),
            pltpu.SemaphoreType.DMA((max(plan.n_local, 1),))]
    aliases = {n_in + i: n_out + o for i, o in plan.aliases.items()}
    return plan.ins, [_ANY] * len(plan.outs), plan.outs, sems, aliases


def _split_refs(refs, n_in, n_out, n_scr, plan):
    p_in, p_out = (len(plan.ins), len(plan.outs)) if plan is not None else (0, 0)
    refs = list(refs)
    ins, refs = refs[:n_in], refs[n_in:]
    pins, refs = refs[:p_in], refs[p_in:]
    outs, refs = refs[:n_out], refs[n_out:]
    pouts, refs = refs[:p_out], refs[p_out:]
    scr, psem = refs[:n_scr], refs[n_scr:]
    psem = tuple((lambda i, s=s: s.at[i]) for s in psem)
    return ins, outs, scr, (pins, pouts, psem)


def _grid_edge(grid, last):
    cond = None
    for ax, n in enumerate(grid):
        c = pl.program_id(ax) == (n - 1 if last else 0)
        cond = c if cond is None else cond & c
    return cond


def _plan_start(plan, pctx, grid):
    if plan is not None:
        pins, pouts, psem = pctx
        pl.when(_grid_edge(grid, False))(lambda: plan.start(pins, pouts, *psem))


def _plan_wait(plan, pctx, grid):
    if plan is not None:
        pins, pouts, psem = pctx
        pl.when(_grid_edge(grid, True))(lambda: plan.wait(pins, pouts, *psem))


def _run_plan(plan, *, name):
    def body(*refs):
        _, _, _, (pins, pouts, psem) = _split_refs(refs, 0, 0, 0, plan)
        plan.start(pins, pouts, *psem)
        plan.wait(pins, pouts, *psem)

    p_in, p_ospec, p_oshape, p_scr, p_alias = _plan_io(plan, 0, 0)
    return pl.pallas_call(body, name=name, in_specs=[_ANY] * len(p_in), out_specs=p_ospec, out_shape=p_oshape,
                          scratch_shapes=p_scr, input_output_aliases=p_alias)(*p_in)


def _fold8(x):
    return x.reshape(x.shape[0] // 8, 8, x.shape[1]).sum(axis=0)


def _ln_stats(r):
    mu = jnp.mean(r, -1, keepdims=True)
    xc = r - mu
    rstd = lax.rsqrt(jnp.mean(xc * xc, -1, keepdims=True) + EPS)
    return xc * rstd, rstd


def _sigmoid(x):
    return jax.nn.sigmoid(x)


def _gelu(x):
    return 0.5 * x * (1.0 + jnp.tanh(GELU_C * (x + GELU_A * x * x * x)))


def _gelu_grad(x):
    t = jnp.tanh(GELU_C * (x + GELU_A * x * x * x))
    return 0.5 * (1.0 + t) + 0.5 * x * (1.0 - t * t) * GELU_C * (1.0 + 3.0 * GELU_A * x * x)


MM_ROWS = 1024
DW_TOKENS = 4096


def _matmul(a, b, *, name, M, N, K, ta=False, tb=False, out_dtype=F32, tm=MM_ROWS, tn=1024, tk=1024,
            a_spec=None, b_spec=None, b_merge=None, out_shape=None, o_spec=None, into=None,
            a_sq=False, mul=None, add=None, add_scale=1.0, plan=None):
    tm, tn, tk = min(tm, M), min(tn, N), min(tk, K)
    assert M % tm == 0 and N % tn == 0 and K % tk == 0
    grid = (M // tm, N // tn, K // tk)
    nk = grid[2]
    if a_spec is None:
        a_spec = pl.BlockSpec((tk, tm), lambda i, j, k: (k, i)) if ta else pl.BlockSpec((tm, tk), lambda i, j, k: (i, k))
    if b_spec is None:
        b_spec = pl.BlockSpec((tn, tk), lambda i, j, k: (j, k)) if tb else pl.BlockSpec((tk, tn), lambda i, j, k: (k, j))
    if o_spec is None:
        o_spec = pl.BlockSpec((tm, tn), lambda i, j, k: (i, j))
        out_shape = jax.ShapeDtypeStruct((M, N), out_dtype)
    e_spec = pl.BlockSpec((tm, tn), lambda i, j, k: (i, j))
    dims = (((0 if ta else 1,), (1 if tb else 0,)), ((), ()))
    extra = [e for e in (mul, add, into) if e is not None]
    n_in = 2 + len(extra)

    def body(*refs):
        ins, outs, scr, pctx = _split_refs(refs, n_in, 1, 1 if nk > 1 else 0, plan)
        a_ref, b_ref = ins[0], ins[1]
        rest = list(ins[2:])
        mul_ref = rest.pop(0) if mul is not None else None
        add_ref = rest.pop(0) if add is not None else None
        o_ref = outs[0]
        _plan_start(plan, pctx, grid)
        av = a_ref[...].astype(BF16)
        if a_sq:
            av = av * av
        bv = b_ref[...]
        if b_merge is not None:
            bv = bv.reshape(b_merge)
        if bv.ndim == 3:
            w = av.shape[-1] // (1 if av.ndim == 3 else bv.shape[0])
            a_parts = [av[s] if av.ndim == 3 else av[:, s * w:(s + 1) * w] for s in range(bv.shape[0])]
            p = sum(lax.dot_general(a_parts[s], bv[s], dims, preferred_element_type=F32) for s in range(bv.shape[0]))
        else:
            p = lax.dot_general(av, bv, dims, preferred_element_type=F32)

        def finish(r):
            if mul_ref is not None:
                r = r * (2.0 * mul_ref[...].astype(F32))
            if add_ref is not None:
                r = r + add_scale * add_ref[...]
            o_ref[...] = r.astype(o_ref.dtype)

        if nk == 1:
            finish(p)
        else:
            acc_ref = scr[0]
            k = pl.program_id(2)

            @pl.when(k == 0)
            def _():
                acc_ref[...] = p

            @pl.when(k > 0)
            def _():
                acc_ref[...] += p

            @pl.when(k == nk - 1)
            def _():
                finish(acc_ref[...])

        _plan_wait(plan, pctx, grid)

    p_in, p_ospec, p_oshape, p_scr, p_alias = _plan_io(plan, n_in, 1)
    aliases = dict(p_alias)
    if into is not None:
        aliases[n_in - 1] = 0
    return pl.pallas_call(
        body, name=name, grid=grid,
        in_specs=[a_spec, b_spec] + [e_spec] * (len(extra) - (into is not None)) + [_ANY] * (into is not None)
        + [_ANY] * len(p_in),
        out_specs=[o_spec] + p_ospec, out_shape=[out_shape] + p_oshape,
        scratch_shapes=([pltpu.VMEM((tm, tn), F32)] if nk > 1 else []) + p_scr,
        input_output_aliases=aliases, compiler_params=_params(48, 3),
    )(*_hbm(a, b, *extra), *p_in)


def _rows4_spec(rowblk, n_axes):
    return pl.BlockSpec((4, 256, D), lambda *_: (0, rowblk, 0))


def _residual(h_ref, prev_refs):
    if not prev_refs:
        return h_ref[...]
    xhat, _ = _ln_stats(h_ref[...])
    return xhat * prev_refs[0][...] + prev_refs[1][...]


def _proj_ln(a_b, w, h_prev, g, b, *, name, prev_ln=(), w_rowblk=None, plan=None):
    T = a_b.shape[0]
    tm = min(MM_ROWS, T)
    grid = (T // tm,)
    row = pl.BlockSpec((tm, D), lambda i: (i, 0))
    vec = pl.BlockSpec((1, D), lambda i: (0, 0))
    w_spec = pl.BlockSpec((D, D), lambda i: (0, 0)) if w_rowblk is None else _rows4_spec(w_rowblk, 1)
    n_in = 5 + len(prev_ln)

    def body(*refs):
        ins, (r_ref, hb_ref), _, pctx = _split_refs(refs, n_in, 2, 0, plan)
        a_ref, w_ref, h_ref, g_ref, b_ref = ins[:5]
        _plan_start(plan, pctx, grid)
        mix = jnp.dot(a_ref[...], w_ref[...].reshape(D, D), preferred_element_type=F32)
        r = ALPHA * _residual(h_ref, ins[5:]) + mix
        xhat, _ = _ln_stats(r)
        r_ref[...] = r
        hb_ref[...] = (xhat * g_ref[...] + b_ref[...]).astype(BF16)
        _plan_wait(plan, pctx, grid)

    p_in, p_ospec, p_oshape, p_scr, p_alias = _plan_io(plan, n_in, 2)
    return pl.pallas_call(
        body, name=name, grid=grid,
        in_specs=[row, w_spec, row, vec, vec] + [vec] * len(prev_ln) + [_ANY] * len(p_in),
        out_specs=[row, row] + p_ospec,
        out_shape=[jax.ShapeDtypeStruct((T, D), F32), jax.ShapeDtypeStruct((T, D), BF16)] + p_oshape,
        scratch_shapes=p_scr, input_output_aliases=p_alias, compiler_params=_params(40, 1),
    )(*_hbm(a_b, w, h_prev, g, b, *prev_ln), *p_in)


def _ffn_ln(h_b, wbuf, h, g, b, *, name, prev_ln=(), plan=None):
    T = h_b.shape[0]
    slots = 2
    tm, tf = min(ROW_BLOCK, T), slots * 1024
    nf = 4 // slots
    F = nf * tf
    grid = (T // tm, nf)
    row = pl.BlockSpec((tm, D), lambda i, j: (i, 0))
    vec = pl.BlockSpec((1, D), lambda i, j: (0, 0))
    n_in = 6 + len(prev_ln)

    def body(*refs):
        ins, (ra_ref, r_ref, hbo_ref), (acc_ref,), pctx = _split_refs(refs, n_in, 3, 1, plan)
        hb_ref, w1_ref, w2_ref, h_ref, g_ref, b_ref = ins[:6]
        _plan_start(plan, pctx, grid)
        j = pl.program_id(1)
        hb = hb_ref[...]
        p = None
        for s in range(slots):
            ra = jnp.maximum(jnp.dot(hb, w1_ref[s], preferred_element_type=F32), 0.0)
            ra_ref[:, s * 1024:(s + 1) * 1024] = ra.astype(BF16)
            ps = jnp.dot((ra * ra).astype(BF16), w2_ref[s], preferred_element_type=F32)
            p = ps if p is None else p + ps

        @pl.when(j == 0)
        def _():
            acc_ref[...] = p

        @pl.when(j > 0)
        def _():
            acc_ref[...] += p

        @pl.when(j == nf - 1)
        def _():
            r = ALPHA * _residual(h_ref, ins[6:]) + acc_ref[...]
            xhat, _ = _ln_stats(r)
            r_ref[...] = r
            hbo_ref[...] = (xhat * g_ref[...] + b_ref[...]).astype(BF16)

        _plan_wait(plan, pctx, grid)

    p_in, p_ospec, p_oshape, p_scr, p_alias = _plan_io(plan, n_in, 3)
    return pl.pallas_call(
        body, name=name, grid=grid,
        in_specs=[row, pl.BlockSpec((slots, D, D), lambda i, j: (j, 0, 0)),
                  pl.BlockSpec((slots, D, D), lambda i, j: (j, 1, 0)), row, vec, vec] + [vec] * len(prev_ln)
        + [_ANY] * len(p_in),
        out_specs=[pl.BlockSpec((tm, tf), lambda i, j: (i, j)), row, row] + p_ospec,
        out_shape=[jax.ShapeDtypeStruct((T, F), BF16), jax.ShapeDtypeStruct((T, D), F32),
                   jax.ShapeDtypeStruct((T, D), BF16)] + p_oshape,
        scratch_shapes=[pltpu.VMEM((tm, D), F32)] + p_scr,
        input_output_aliases=p_alias, compiler_params=_params(56, 2),
    )(*_hbm(h_b, wbuf, wbuf, h, g, b, *prev_ln), *p_in)


def _ln_bwd(dy, r, g, *, name, loss_head=()):
    T = r.shape[0]
    tm = min(ROW_BLOCK, T)
    row = pl.BlockSpec((tm, D), lambda i: (i, 0))
    vec = pl.BlockSpec((1, D), lambda i: (0, 0))
    acc = pl.BlockSpec((8, D), lambda i: (0, 0))
    operands, in_specs = ([r, g, *loss_head], [row, vec, vec, row]) if loss_head else ([r, g, dy], [row, vec, row])
    n_in = len(operands)

    def body(*refs):
        r_ref, g_ref = refs[:2]
        dr_ref, drb_ref, dg_ref, db_ref = refs[n_in:n_in + 4]

        @pl.when(pl.program_id(0) == 0)
        def _():
            for ref in refs[n_in + 2:]:
                ref[...] = jnp.zeros_like(ref)

        xhat, rstd = _ln_stats(r_ref[...])
        if loss_head:
            err = xhat * g_ref[...] + refs[2][...] - refs[3][...]
            refs[n_in + 4][...] += _fold8(err * err)
            dy_ = err * (1.0 / D)
        else:
            dy_ = refs[2][...]
        dxh = dy_ * g_ref[...]
        m1 = jnp.mean(dxh, -1, keepdims=True)
        m2 = jnp.mean(dxh * xhat, -1, keepdims=True)
        dr = rstd * (dxh - m1 - xhat * m2)
        dr_ref[...] = dr
        drb_ref[...] = dr.astype(BF16)
        dg_ref[...] += _fold8(dy_ * xhat)
        db_ref[...] += _fold8(dy_)

    n_acc = 3 if loss_head else 2
    return pl.pallas_call(
        body, name=name, grid=(T // tm,), in_specs=in_specs, out_specs=[row, row] + [acc] * n_acc,
        out_shape=[jax.ShapeDtypeStruct((T, D), F32), jax.ShapeDtypeStruct((T, D), BF16)]
        + [jax.ShapeDtypeStruct((8, D), F32)] * n_acc,
        compiler_params=_params(40, 1),
    )(*_hbm(*operands))


def _rope(x, c, s1, s2):
    return x * c + pltpu.roll(x, 112, 1) * s1 + pltpu.roll(x, 16, 1) * s2


def _rope_t(dy, c, s1, s2):
    return dy * c + pltpu.roll(dy * s1, 16, 1) + pltpu.roll(dy * s2, 112, 1)


def _rms(x, g):
    rstd = lax.rsqrt(jnp.mean(x * x, -1, keepdims=True) + EPS)
    xhat = x * rstd
    return xhat * g, xhat, rstd


def _mla_prep(z0, gq, gkv, wq, wk, wv, rc, rs1, rs2):
    T = z0.shape[0]
    tm = min(ROW_BLOCK, T)
    HW = HEADS * 128

    def body(cq_ref, ckv_ref, kr_ref, gq_ref, gkv_ref, wq_ref, wk_ref, wv_ref, c_ref, s1_ref, s2_ref,
             q_ref, k_ref, v_ref):
        nq = _rms(cq_ref[...], gq_ref[...])[0].astype(BF16)
        nkv = _rms(ckv_ref[...], gkv_ref[...])[0].astype(BF16)
        q = jnp.dot(nq, wq_ref[...], preferred_element_type=F32)
        k = jnp.dot(nkv, wk_ref[...], preferred_element_type=F32)
        v = jnp.dot(nkv, wv_ref[...], preferred_element_type=F32)
        c, s1, s2 = c_ref[...], s1_ref[...], s2_ref[...]
        kr = _rope(pltpu.roll(kr_ref[...], 64, 1), c, s1, s2)
        for h in range(HEADS):
            sl = slice(h * 128, (h + 1) * 128)
            q_ref[:, sl] = (_rope(q[:, sl], c, s1, s2) * QK_SCALE).astype(BF16)
            k_ref[:, sl] = (k[:, sl] + kr).astype(BF16)
        v_ref[...] = v.astype(BF16)

    full = lambda shape: pl.BlockSpec(shape, lambda i: (0, 0))
    tab = pl.BlockSpec((tm, 128), lambda i: (i, 0))
    return pl.pallas_call(
        body, name="mla_prep", grid=(T // tm,),
        in_specs=[pl.BlockSpec((tm, 256), lambda i: (i, 0)), pl.BlockSpec((tm, 256), lambda i: (i, 1)),
                  pl.BlockSpec((tm, 128), lambda i: (i, 12)), full((1, 256)), full((1, 256)),
                  full((256, HW)), full((256, HW)), full((256, 512)), tab, tab, tab],
        out_specs=[pl.BlockSpec((tm, HW), lambda i: (i, 0)), pl.BlockSpec((tm, HW), lambda i: (i, 0)),
                   pl.BlockSpec((tm, 512), lambda i: (i, 0))],
        out_shape=[jax.ShapeDtypeStruct((T, HW), BF16), jax.ShapeDtypeStruct((T, HW), BF16),
                   jax.ShapeDtypeStruct((T, 512), BF16)],
        compiler_params=_params(40, 1),
    )(z0, z0, z0, gq, gkv, wq, wk, wv, rc, rs1, rs2)


def _flash_fwd(q, k, v, plan=None):
    T = q.shape[0]
    bq = min(2 * ROW_BLOCK, T)
    nq = T // bq
    grid = (4, nq, nq)

    def body(*refs):
        (q_ref, k_ref, v_ref), (o_ref, lse_ref), (m_sc, acc_sc), pctx = _split_refs(refs, 3, 2, 2, plan)
        _plan_start(plan, pctx, grid)
        i, j = pl.program_id(1), pl.program_id(2)
        first = lax.broadcasted_iota(jnp.int32, (bq, 128), 1) < 64

        @pl.when(j == 0)
        def _():
            m_sc[...] = jnp.full_like(m_sc, -jnp.inf)
            acc_sc[...] = jnp.zeros_like(acc_sc)

        def step(masked):
            vp = v_ref[...]
            for h in range(2):
                sl = slice(h * 128, (h + 1) * 128)
                s = lax.dot_general(q_ref[:, sl], k_ref[:, sl], NT_DIMS, preferred_element_type=F32)
                if masked:
                    rows = lax.broadcasted_iota(jnp.int32, (bq, bq), 0)
                    cols = lax.broadcasted_iota(jnp.int32, (bq, bq), 1)
                    s = jnp.where(cols <= rows, s, -jnp.inf)
                m_prev = m_sc[h, :, 0:1]
                m_new = jnp.maximum(m_prev, jnp.max(s, axis=1, keepdims=True))
                alpha = jnp.exp(m_prev - m_new)
                p = jnp.exp(s - m_new).astype(BF16)
                vh = jnp.where(first if h == 0 else jnp.logical_not(first), vp, jnp.ones_like(vp))
                acc_sc[h] = acc_sc[h] * alpha + jnp.dot(p, vh, preferred_element_type=F32)
                m_sc[h] = jnp.broadcast_to(m_new, (bq, 128))

        @pl.when(j < i)
        def _():
            step(False)

        @pl.when(j == i)
        def _():
            step(True)
            a0, a1 = acc_sc[0], acc_sc[1]
            l0, l1 = pltpu.roll(a0, 64, 1), pltpu.roll(a1, 64, 1)
            o_ref[...] = jnp.where(first, a0 / l0, a1 / l1).astype(BF16)
            lse_ref[...] = jnp.where(first, m_sc[0] + jnp.log(l0), m_sc[1] + jnp.log(l1))

        _plan_wait(plan, pctx, grid)

    kv = lambda hp, i, j: (jnp.minimum(i, j), hp)
    p_in, p_ospec, p_oshape, p_scr, p_alias = _plan_io(plan, 3, 2)
    return pl.pallas_call(
        body, name="flash_fwd", grid=grid,
        in_specs=[pl.BlockSpec((bq, 256), lambda hp, i, j: (i, hp)), pl.BlockSpec((bq, 256), kv),
                  pl.BlockSpec((bq, 128), kv)] + [_ANY] * len(p_in),
        out_specs=[pl.BlockSpec((bq, 128), lambda hp, i, j: (i, hp)),
                   pl.BlockSpec((bq, 128), lambda hp, i, j: (i, hp))] + p_ospec,
        out_shape=[jax.ShapeDtypeStruct((T, 512), BF16), jax.ShapeDtypeStruct((T, 512), F32)] + p_oshape,
        scratch_shapes=[pltpu.VMEM((2, bq, 128), F32), pltpu.VMEM((2, bq, 128), F32)] + p_scr,
        input_output_aliases=p_alias, compiler_params=_params(56, 3),
    )(*_hbm(q, k, v), *p_in)


def _attn_delta(dmix, o):
    T = o.shape[0]
    tm = min(ROW_BLOCK, T)
    blk = pl.BlockSpec((tm, 512), lambda i: (i, 0))

    def body(do_ref, o_ref, delta_ref, dob_ref):
        first = lax.broadcasted_iota(jnp.int32, (tm, 128), 1) < 64
        for hp in range(4):
            sl = slice(hp * 128, (hp + 1) * 128)
            prod = do_ref[:, sl] * o_ref[:, sl].astype(F32)
            d0 = jnp.sum(jnp.where(first, prod, 0.0), axis=1, keepdims=True)
            d1 = jnp.sum(jnp.where(first, 0.0, prod), axis=1, keepdims=True)
            delta_ref[:, sl] = jnp.where(first, d0, d1)
        dob_ref[...] = do_ref[...].astype(BF16)

    return pl.pallas_call(
        body, name="attn_delta", grid=(T // tm,), in_specs=[blk, blk], out_specs=[blk, blk],
        out_shape=[jax.ShapeDtypeStruct((T, 512), F32), jax.ShapeDtypeStruct((T, 512), BF16)],
        compiler_params=_params(32, 1),
    )(dmix, o)


def _flash_bwd(q, k, v, do_b, lse, delta, plan=None):
    T = q.shape[0]
    bq = min(2 * ROW_BLOCK, T)
    nq = T // bq
    grid = (4, nq, nq)

    def body(*refs):
        ((q_ref, k_ref, v_ref, do_ref, lse_ref, dl_ref), (dq_hbm, dk_ref, dv_ref), (dq_sc, dk_sc, dv_sc, sem),
         pctx) = _split_refs(refs, 6, 3, 4, plan)
        _plan_start(plan, pctx, grid)
        hp, j, i = pl.program_id(0), pl.program_id(1), pl.program_id(2)
        first = lax.broadcasted_iota(jnp.int32, (bq, 128), 1) < 64

        @pl.when((j == 0) & (i == 0))
        def _():
            dq_sc[...] = jnp.zeros_like(dq_sc)

        @pl.when(i == j)
        def _():
            dk_sc[...] = jnp.zeros_like(dk_sc)
            dv_sc[...] = jnp.zeros_like(dv_sc)

        def tile(r0, nr, nc, masked):
            rs, cs = slice(r0, r0 + nr), slice(0, nc)
            vp = v_ref[cs, :]
            do = do_ref[rs, :]
            lanes = first[rs, :]
            for h in range(2):
                sl = slice(h * 128, (h + 1) * 128)
                qh, kh = q_ref[rs, sl], k_ref[cs, sl]
                s = lax.dot_general(qh, kh, NT_DIMS, preferred_element_type=F32)
                p = jnp.exp(s - lse_ref[rs, h * 64:h * 64 + 1])
                if masked:
                    rows = r0 + lax.broadcasted_iota(jnp.int32, (nr, nc), 0)
                    cols = lax.broadcasted_iota(jnp.int32, (nr, nc), 1)
                    p = jnp.where(cols <= rows, p, 0.0)
                do_h = jnp.where(lanes if h == 0 else jnp.logical_not(lanes), do, jnp.zeros_like(do))
                dv_sc[cs, :] += lax.dot_general(p.astype(BF16), do_h, TN_DIMS, preferred_element_type=F32)
                dp = lax.dot_general(do_h, vp, NT_DIMS, preferred_element_type=F32)
                ds = (p * (dp - dl_ref[rs, h * 64:h * 64 + 1])).astype(BF16)
                dq_sc[i, rs, sl] += jnp.dot(ds, kh, preferred_element_type=F32)
                dk_sc[cs, sl] += lax.dot_general(ds, qh, TN_DIMS, preferred_element_type=F32)

        @pl.when(i > j)
        def _():
            tile(0, bq, bq, False)

        @pl.when(i == j)
        def _():
            tile(0, bq // 2, bq // 2, True)
            tile(bq // 2, bq // 2, bq, True)

        @pl.when(i == nq - 1)
        def _():
            dk_ref[...] = dk_sc[...]
            dv_ref[...] = dv_sc[...]

        @pl.when((j == nq - 1) & (i == nq - 1))
        def _():
            cp = pltpu.make_async_copy(dq_sc, dq_hbm.at[hp], sem)
            cp.start()
            cp.wait()

        _plan_wait(plan, pctx, grid)

    qi = lambda hp, j, i: (jnp.maximum(i, j), hp)
    kj = lambda hp, j, i: (j, hp)
    p_in, p_ospec, p_oshape, p_scr, p_alias = _plan_io(plan, 6, 3)
    return pl.pallas_call(
        body, name="flash_bwd", grid=grid,
        in_specs=[pl.BlockSpec((bq, 256), qi), pl.BlockSpec((bq, 256), kj), pl.BlockSpec((bq, 128), kj),
                  pl.BlockSpec((bq, 128), qi), pl.BlockSpec((bq, 128), qi), pl.BlockSpec((bq, 128), qi)]
        + [_ANY] * len(p_in),
        out_specs=[_ANY, pl.BlockSpec((bq, 256), kj), pl.BlockSpec((bq, 128), kj)] + p_ospec,
        out_shape=[jax.ShapeDtypeStruct((4, nq, bq, 256), F32), jax.ShapeDtypeStruct((T, 1024), F32),
                   jax.ShapeDtypeStruct((T, 512), F32)] + p_oshape,
        scratch_shapes=[pltpu.VMEM((nq, bq, 256), F32), pltpu.VMEM((bq, 256), F32), pltpu.VMEM((bq, 128), F32),
                        pltpu.SemaphoreType.DMA] + p_scr,
        input_output_aliases=p_alias, compiler_params=_params(56, 3),
    )(*_hbm(q, k, v, do_b, lse, delta), *p_in)


def _mla_bwd(z0, dq4, dk, dv, gq, gkv, wq, wk, wv, rc, rs1, rs2, plan=None):
    T = z0.shape[0]
    tm = min(ROW_BLOCK, T)
    HW = HEADS * 128
    grid = (T // tm,)
    dq4 = dq4.reshape(4, T, 256)

    def body(*refs):
        ((cq_ref, ckv_ref, dq_ref, dk_ref, dv_ref, gq_ref, gkv_ref, wq_ref, wk_ref, wv_ref, c_ref, s1_ref, s2_ref),
         (dc_ref, dkr_ref, dwq_ref, dwk_ref, dwv_ref, dgq_ref, dgkv_ref), _, pctx) = _split_refs(refs, 13, 7, 0, plan)
        _plan_start(plan, pctx, grid)

        @pl.when(pl.program_id(0) == 0)
        def _():
            for ref in (dwq_ref, dwk_ref, dwv_ref, dgq_ref, dgkv_ref):
                ref[...] = jnp.zeros_like(ref)

        c, s1, s2 = c_ref[...], s1_ref[...], s2_ref[...]
        lane = lax.broadcasted_iota(jnp.int32, (tm, 128), 1)
        nq, xq, rq = _rms(cq_ref[...], gq_ref[...])
        nkv, xkv, rkv = _rms(ckv_ref[...], gkv_ref[...])
        nq_b, nkv_b = nq.astype(BF16), nkv.astype(BF16)

        dq_parts, dk_parts = [], []
        dkr = jnp.zeros((tm, 128), F32)
        for h in range(HEADS):
            blk = dq_ref[h // 2, :, (h % 2) * 128:(h % 2 + 1) * 128] * QK_SCALE
            dq_parts.append(_rope_t(blk, c, s1, s2).astype(BF16))
            kb = dk_ref[:, h * 128:(h + 1) * 128]
            dk_parts.append(jnp.where(lane < NOPE, kb, 0.0).astype(BF16))
            dkr = dkr + kb
        dq_b = jnp.concatenate(dq_parts, axis=1)
        dk_b = jnp.concatenate(dk_parts, axis=1)
        dv_b = dv_ref[...].astype(BF16)

        dwq_ref[...] += lax.dot_general(nq_b, dq_b, TN_DIMS, preferred_element_type=F32)
        dwk_ref[...] += lax.dot_general(nkv_b, dk_b, TN_DIMS, preferred_element_type=F32)
        dwv_ref[...] += lax.dot_general(nkv_b, dv_b, TN_DIMS, preferred_element_type=F32)
        dnq = lax.dot_general(dq_b, wq_ref[...], NT_DIMS, preferred_element_type=F32)
        dnkv = (lax.dot_general(dk_b, wk_ref[...], NT_DIMS, preferred_element_type=F32)
                + lax.dot_general(dv_b, wv_ref[...], NT_DIMS, preferred_element_type=F32))

        def rms_bwd(dn, xhat, rstd, g):
            dxh = dn * g
            return rstd * (dxh - xhat * jnp.mean(dxh * xhat, -1, keepdims=True))

        dc_ref[:, :256] = rms_bwd(dnq, xq, rq, gq_ref[...]).astype(BF16)
        dc_ref[:, 256:] = rms_bwd(dnkv, xkv, rkv, gkv_ref[...]).astype(BF16)
        dgq_ref[...] += _fold8(dnq * xq)
        dgkv_ref[...] += _fold8(dnkv * xkv)
        dkr = pltpu.roll(_rope_t(dkr, c, s1, s2), 64, 1)
        dkr_ref[...] = jnp.where(lane < ROPE, dkr, 0.0).astype(BF16)
        _plan_wait(plan, pctx, grid)

    full = lambda shape: pl.BlockSpec(shape, lambda i: (0,) * len(shape))
    tab = pl.BlockSpec((tm, 128), lambda i: (i, 0))
    p_in, p_ospec, p_oshape, p_scr, p_alias = _plan_io(plan, 13, 7)
    return pl.pallas_call(
        body, name="mla_bwd", grid=grid,
        in_specs=[pl.BlockSpec((tm, 256), lambda i: (i, 0)), pl.BlockSpec((tm, 256), lambda i: (i, 1)),
                  pl.BlockSpec((4, tm, 256), lambda i: (0, i, 0)),
                  pl.BlockSpec((tm, HW), lambda i: (i, 0)), pl.BlockSpec((tm, 512), lambda i: (i, 0)),
                  full((1, 256)), full((1, 256)), full((256, HW)), full((256, HW)), full((256, 512)), tab, tab, tab]
        + [_ANY] * len(p_in),
        out_specs=[pl.BlockSpec((tm, 512), lambda i: (i, 0)), tab, full((256, HW)), full((256, HW)),
                   full((256, 512)), full((8, 256)), full((8, 256))] + p_ospec,
        out_shape=[jax.ShapeDtypeStruct((T, 512), BF16), jax.ShapeDtypeStruct((T, 128), BF16),
                   jax.ShapeDtypeStruct((256, HW), F32), jax.ShapeDtypeStruct((256, HW), F32),
                   jax.ShapeDtypeStruct((256, 512), F32), jax.ShapeDtypeStruct((8, 256), F32),
                   jax.ShapeDtypeStruct((8, 256), F32)] + p_oshape,
        scratch_shapes=p_scr, input_output_aliases=p_alias, compiler_params=_params(48, 1),
    )(*_hbm(z0, z0, dq4, dk, dv, gq, gkv, wq, wk, wv, rc, rs1, rs2), *p_in)


def _sgu_fwd(z0, a_out, ln_g, ln_b, w, b_t):
    T = z0.shape[0]
    tm = min(ROW_BLOCK, T)
    W = SGU_G * SGU_C

    def body(u_ref, v_ref, a_ref, g_ref, b_ref, w_ref, bt_ref, o_ref):
        o_ref[:, :W] = a_ref[...]
        ug = _gelu(u_ref[...])
        xhat, _ = _ln_stats(_gelu(v_ref[...]))
        vn = (xhat * g_ref[...] + b_ref[...]).astype(BF16)
        tril = lax.broadcasted_iota(jnp.int32, (SGU_C, SGU_C), 0) >= lax.broadcasted_iota(jnp.int32, (SGU_C, SGU_C), 1)
        for g in range(SGU_G):
            cs = slice(g * SGU_C, (g + 1) * SGU_C)
            wg = jnp.where(tril, w_ref[g], 0.0).astype(BF16)
            bcol = bt_ref[:, g:g + 1]
            for c in range(tm // SGU_C):
                rs = slice(c * SGU_C, (c + 1) * SGU_C)
                mixed = jnp.dot(wg, vn[rs, cs], preferred_element_type=F32) + bcol
                o_ref[rs, W + g * SGU_C:W + (g + 1) * SGU_C] = (ug[rs, cs] * mixed).astype(BF16)

    full = lambda shape: pl.BlockSpec(shape, lambda i: (0,) * len(shape))
    return pl.pallas_call(
        body, name="sgu_fwd", grid=(T // tm,),
        in_specs=[pl.BlockSpec((tm, W), lambda i: (i, 1)), pl.BlockSpec((tm, W), lambda i: (i, 2)),
                  pl.BlockSpec((tm, W), lambda i: (i, 0)),
                  full((1, W)), full((1, W)), full((SGU_G, SGU_C, SGU_C)), full((SGU_C, SGU_G))],
        out_specs=pl.BlockSpec((tm, 2 * W), lambda i: (i, 0)),
        out_shape=jax.ShapeDtypeStruct((T, 2 * W), BF16),
        compiler_params=_params(32, 1),
    )(z0, z0, a_out, ln_g, ln_b, w, b_t)


def _sgu_bwd(z0, dmix, dc, dkr, ln_g, ln_b, w, b_t, plan=None):
    T = z0.shape[0]
    tm = min(ROW_BLOCK, T)
    W = SGU_G * SGU_C
    grid = (T // tm,)

    def body(*refs):
        ((u_ref, v_ref, do_ref, dc_ref, dkr_ref, g_ref, b_ref, w_ref, bt_ref),
         (dz_ref, dw_ref, db_ref, dlg_ref, dlb_ref), _, pctx) = _split_refs(refs, 9, 5, 0, plan)
        _plan_start(plan, pctx, grid)

        @pl.when(pl.program_id(0) == 0)
        def _():
            for ref in (dw_ref, db_ref, dlg_ref, dlb_ref):
                ref[...] = jnp.zeros_like(ref)

        dz_ref[:, :W] = dc_ref[...]
        dz_ref[:, 3 * W:] = dkr_ref[...]

        u, v, dout = u_ref[...], v_ref[...], do_ref[...]
        ug = _gelu(u)
        xhat, rstd = _ln_stats(_gelu(v))
        vn = (xhat * g_ref[...] + b_ref[...]).astype(BF16)
        dmixed = dout * ug
        dmixed_b = dmixed.astype(BF16)
        tril = lax.broadcasted_iota(jnp.int32, (SGU_C, SGU_C), 0) >= lax.broadcasted_iota(jnp.int32, (SGU_C, SGU_C), 1)
        lane = lax.broadcasted_iota(jnp.int32, (SGU_C, SGU_C), 1)
        dvn_cols = []
        for g in range(SGU_G):
            cs = slice(g * SGU_C, (g + 1) * SGU_C)
            wg = jnp.where(tril, w_ref[g], 0.0).astype(BF16)
            bcol = bt_ref[:, g:g + 1]
            dw_g = jnp.zeros((SGU_C, SGU_C), F32)
            db_g = jnp.zeros((SGU_C, 1), F32)
            dvn_rows = []
            for c in range(tm // SGU_C):
                rs = slice(c * SGU_C, (c + 1) * SGU_C)
                mixed = jnp.dot(wg, vn[rs, cs], preferred_element_type=F32) + bcol
                dz_ref[rs, W + g * SGU_C:W + (g + 1) * SGU_C] = (dout[rs, cs] * mixed * _gelu_grad(u[rs, cs])).astype(BF16)
                dm = dmixed_b[rs, cs]
                dw_g = dw_g + lax.dot_general(dm, vn[rs, cs], NT_DIMS, preferred_element_type=F32)
                db_g = db_g + jnp.sum(dmixed[rs, cs], axis=1, keepdims=True)
                dvn_rows.append(lax.dot_general(wg, dm, TN_DIMS, preferred_element_type=F32))
            dw_ref[g] += jnp.where(tril, dw_g, 0.0)
            db_ref[...] += jnp.where(lane == g, db_g, 0.0)
            dvn_cols.append(jnp.concatenate(dvn_rows, axis=0))
        dvn = jnp.concatenate(dvn_cols, axis=1)
        dxh = dvn * g_ref[...]
        m1 = jnp.mean(dxh, -1, keepdims=True)
        m2 = jnp.mean(dxh * xhat, -1, keepdims=True)
        dvg = rstd * (dxh - m1 - xhat * m2)
        dz_ref[:, 2 * W:3 * W] = (dvg * _gelu_grad(v)).astype(BF16)
        dlg_ref[...] += _fold8(dvn * xhat)
        dlb_ref[...] += _fold8(dvn)
        _plan_wait(plan, pctx, grid)

    full = lambda shape: pl.BlockSpec(shape, lambda i: (0,) * len(shape))
    p_in, p_ospec, p_oshape, p_scr, p_alias = _plan_io(plan, 9, 5)
    return pl.pallas_call(
        body, name="sgu_bwd", grid=grid,
        in_specs=[pl.BlockSpec((tm, W), lambda i: (i, 1)), pl.BlockSpec((tm, W), lambda i: (i, 2)),
                  pl.BlockSpec((tm, W), lambda i: (i, 1)), pl.BlockSpec((tm, W), lambda i: (i, 0)),
                  pl.BlockSpec((tm, 128), lambda i: (i, 0)),
                  full((1, W)), full((1, W)), full((SGU_G, SGU_C, SGU_C)), full((SGU_C, SGU_G))] + [_ANY] * len(p_in),
        out_specs=[pl.BlockSpec((tm, 3 * W + 128), lambda i: (i, 0)), full((SGU_G, SGU_C, SGU_C)),
                   full((SGU_C, SGU_C)), full((8, W)), full((8, W))] + p_ospec,
        out_shape=[jax.ShapeDtypeStruct((T, 3 * W + 128), BF16), jax.ShapeDtypeStruct((SGU_G, SGU_C, SGU_C), F32),
                   jax.ShapeDtypeStruct((SGU_C, SGU_C), F32), jax.ShapeDtypeStruct((8, W), F32),
                   jax.ShapeDtypeStruct((8, W), F32)] + p_oshape,
        scratch_shapes=p_scr, input_output_aliases=p_alias, compiler_params=_params(40, 1),
    )(z0, z0, dmix, dc, dkr, ln_g, ln_b, w, b_t, *p_in)


def _hg_lower_bound(lb_ref):
    a0, a1 = lb_ref[0:1, :], lb_ref[1:2, :]
    m = jnp.maximum(a0, a1)
    e0, e1 = jnp.exp(a0 - m), jnp.exp(a1 - m)
    return e1 / (e0 + e1)


def _running_sum(x, reverse=False):
    n = x.shape[0]
    row = lax.broadcasted_iota(jnp.int32, x.shape, 0)
    s = 1
    while s < n:
        if reverse:
            x = x + jnp.where(row < n - s, pltpu.roll(x, n - s, 0), 0.0)
        else:
            x = x + jnp.where(row >= s, pltpu.roll(x, s, 0), 0.0)
        s *= 2
    return x


def _hg_chunk(qc, fc, lb):
    C = HG_CHUNK
    rows = lax.broadcasted_iota(jnp.int32, (C, C), 0)
    cols = lax.broadcasted_iota(jnp.int32, (C, C), 1)
    rowid = lax.broadcasted_iota(jnp.int32, (C, 128), 0)
    sq, sg = _sigmoid(qc), _sigmoid(fc)
    qf = qc * sq
    gate = lb + (1.0 - lb) * sg
    kk = 1.0 - gate
    lg = jnp.log(gate)
    bcum = _running_sum(lg)
    b_mid = jnp.sum(jnp.where(rowid < C // 2, lg, 0.0), axis=0, keepdims=True)
    b_last = jnp.sum(lg, axis=0, keepdims=True)
    eq, ek, e, eh = jnp.exp(bcum - b_mid), jnp.exp(b_mid - bcum), jnp.exp(bcum), jnp.exp(b_last - bcum)
    qt, kt, qe, khat = qf * eq, kk * ek, qf * e, kk * eh
    a = lax.dot_general(qt.astype(BF16), kt.astype(BF16), NT_DIMS, preferred_element_type=F32)
    a = jnp.where(rows >= cols, a, 0.0)
    return dict(sq=sq, sg=sg, gate=gate, kk=kk, eq=eq, ek=ek, e=e, eh=eh, qt=qt, kt=kt, qe=qe, khat=khat, a=a,
                e_last=jnp.exp(b_last), tril=rows >= cols, rowid=rowid)


def _hgrn_fwd(z4, hg_lb, gnorm):
    T = z4.shape[1]
    tb = min(ROW_BLOCK, T)
    C = HG_CHUNK
    ncb = tb // C
    HPB = HG_HEADS_PER_STEP

    def body(q_ref, f_ref, i_ref, g_ref, lb_ref, gn_ref, y_ref, o_ref, st_ref, st_sc):
        @pl.when(pl.program_id(1) == 0)
        def _():
            st_sc[...] = jnp.zeros_like(st_sc)

        def chunk(c, carry):
            rs = pl.ds(pl.multiple_of(c * C, C), C)
            for hh in range(HPB):
                hs = slice(hh * 128, (hh + 1) * 128)
                lb = _hg_lower_bound(lb_ref.at[:, hs])
                v_b = i_ref[rs, hs].astype(BF16)
                gc = g_ref[rs, hs]
                x = _hg_chunk(q_ref[rs, hs], f_ref[rs, hs], lb)
                st = st_sc[hh]
                st_ref[hh, c] = st
                o = (jnp.dot(x["a"].astype(BF16), v_b, preferred_element_type=F32)
                     + lax.dot_general(x["qe"].astype(BF16), st.astype(BF16), NT_DIMS, preferred_element_type=F32))
                st_sc[hh] = st * x["e_last"] + lax.dot_general(v_b, x["khat"].astype(BF16), TN_DIMS,
                                                               preferred_element_type=F32)
                o_ref[rs, hs] = o
                n = o * lax.rsqrt(jnp.mean(o * o, -1, keepdims=True) + EPS)
                y_ref[rs, hs] = (n * gn_ref[:, hs] * (gc * _sigmoid(gc))).astype(BF16)
            return carry

        lax.fori_loop(0, ncb, chunk, 0, unroll=4)

    W = 128 * HPB
    zb = lambda k: pl.BlockSpec((None, tb, W), lambda h, t: (k, t, h))
    out = pl.BlockSpec((tb, W), lambda h, t: (t, h))
    return pl.pallas_call(
        body, name="hgrn_fwd", grid=(HEADS // HPB, T // tb),
        in_specs=[zb(0), zb(1), zb(2), zb(3), pl.BlockSpec((2, W), lambda h, t: (0, h)),
                  pl.BlockSpec((1, W), lambda h, t: (0, h))],
        out_specs=[out, out, pl.BlockSpec((HPB, ncb, 128, 128), lambda h, t: (h, t, 0, 0))],
        out_shape=[jax.ShapeDtypeStruct((T, D), BF16), jax.ShapeDtypeStruct((T, D), F32),
                   jax.ShapeDtypeStruct((HEADS, T // C, 128, 128), F32)],
        scratch_shapes=[pltpu.VMEM((HPB, 128, 128), F32)],
        compiler_params=_params(48, 2),
    )(*_hbm(z4, z4, z4, z4, hg_lb, gnorm))


def _hgrn_bwd(z4, o_raw, dy, states, hg_lb, gnorm):
    T = z4.shape[1]
    tb = min(ROW_BLOCK, T)
    C = HG_CHUNK
    ncb = tb // C
    nt = T // tb
    HPB = HG_HEADS_PER_STEP

    def body(q_ref, f_ref, i_ref, g_ref, o_ref, dy_ref, st_ref, lb_ref, gn_ref, dz_ref, dlb_ref, dgn_ref, dst_sc):
        @pl.when(pl.program_id(1) == 0)
        def _():
            dst_sc[...] = jnp.zeros_like(dst_sc)
            dlb_ref[...] = jnp.zeros_like(dlb_ref)
            dgn_ref[...] = jnp.zeros_like(dgn_ref)

        def chunk(cc, carry):
            for hh in range(HPB):
                one_head(ncb - 1 - cc, hh, slice(hh * 128, (hh + 1) * 128))
            return carry

        def one_head(c, hh, hs):
            rs = pl.ds(pl.multiple_of(c * C, C), C)
            lb = _hg_lower_bound(lb_ref.at[:, hs])
            gn = gn_ref[:, hs]
            qc, gc = q_ref[rs, hs], g_ref[rs, hs]
            v_b = i_ref[rs, hs].astype(BF16)
            x = _hg_chunk(qc, f_ref[rs, hs], lb)
            st, dst = st_ref[hh, c], dst_sc[hh]
            st_b, dst_b = st.astype(BF16), dst.astype(BF16)
            o, dyc = o_ref[rs, hs], dy_ref[rs, hs]
            sgg = _sigmoid(gc)
            sil = gc * sgg
            rstd = lax.rsqrt(jnp.mean(o * o, -1, keepdims=True) + EPS)
            n = o * rstd
            dgn_ref[:, hs] += _fold8(dyc * n * sil)
            dn = dyc * gn * sil
            do = rstd * (dn - n * jnp.mean(dn * n, -1, keepdims=True))
            dg = dyc * n * gn * (sgg * (1.0 + gc * (1.0 - sgg)))
            do_b = do.astype(BF16)
            da = jnp.where(x["tril"], lax.dot_general(do_b, v_b, NT_DIMS, preferred_element_type=F32), 0.0).astype(BF16)
            qt_b, kt_b, qe_b, khat_b = (x[n_].astype(BF16) for n_ in ("qt", "kt", "qe", "khat"))
            dv = (lax.dot_general(x["a"].astype(BF16), do_b, TN_DIMS, preferred_element_type=F32)
                  + lax.dot_general(khat_b, dst_b, NT_DIMS, preferred_element_type=F32))
            dqt = jnp.dot(da, kt_b, preferred_element_type=F32)
            dqe = jnp.dot(do_b, st_b, preferred_element_type=F32)
            dkt = lax.dot_general(da, qt_b, TN_DIMS, preferred_element_type=F32)
            dkhat = jnp.dot(v_b, dst_b, preferred_element_type=F32)
            dst_sc[hh] = lax.dot_general(do_b, qe_b, TN_DIMS, preferred_element_type=F32) + dst * x["e_last"]
            de_last = jnp.sum(st * dst, axis=0, keepdims=True)
            dqf = dqt * x["eq"] + dqe * x["e"]
            dkk = dkt * x["ek"] + dkhat * x["eh"]
            dkh_kh = dkhat * x["khat"]
            db = dqt * qt_b.astype(F32) - dkt * kt_b.astype(F32) + dqe * x["qe"] - dkh_kh
            db_last = jnp.sum(dkh_kh, axis=0, keepdims=True) + de_last * x["e_last"]
            db = db + jnp.where(x["rowid"] == C - 1, db_last, 0.0)
            dlg = _running_sum(db, reverse=True)
            dgate = dlg / x["gate"] - dkk
            sg, sq = x["sg"], x["sq"]
            dlb_ref[:, hs] += _fold8(dgate * (1.0 - sg)) * (lb * (1.0 - lb))
            dz_ref[0, rs, hs] = (dqf * (sq * (1.0 + qc * (1.0 - sq)))).astype(BF16)
            dz_ref[1, rs, hs] = (dgate * (1.0 - lb) * sg * (1.0 - sg)).astype(BF16)
            dz_ref[2, rs, hs] = dv.astype(BF16)
            dz_ref[3, rs, hs] = dg.astype(BF16)

        lax.fori_loop(0, ncb, chunk, 0, unroll=4)

    W = 128 * HPB
    zb = lambda k: pl.BlockSpec((None, tb, W), lambda h, t: (k, nt - 1 - t, h))
    blk = pl.BlockSpec((tb, W), lambda h, t: (nt - 1 - t, h))
    acc = pl.BlockSpec((8, W), lambda h, t: (0, h))
    return pl.pallas_call(
        body, name="hgrn_bwd", grid=(HEADS // HPB, nt),
        in_specs=[zb(0), zb(1), zb(2), zb(3), blk, blk,
                  pl.BlockSpec((HPB, ncb, 128, 128), lambda h, t: (h, nt - 1 - t, 0, 0)),
                  pl.BlockSpec((2, W), lambda h, t: (0, h)), pl.BlockSpec((1, W), lambda h, t: (0, h))],
        out_specs=[pl.BlockSpec((4, tb, W), lambda h, t: (0, nt - 1 - t, h)), acc, acc],
        out_shape=[jax.ShapeDtypeStruct((4, T, D), BF16), jax.ShapeDtypeStruct((8, D), F32),
                   jax.ShapeDtypeStruct((8, D), F32)],
        scratch_shapes=[pltpu.VMEM((HPB, 128, 128), F32)],
        compiler_params=_params(48, 2),
    )(*_hbm(z4, z4, z4, z4, o_raw, dy, states, hg_lb, gnorm))


def _adamw(w, g, m, v, *, name):
    R, L = w.shape
    tr = R if R <= 512 else 512
    assert R % tr == 0
    blk = pl.BlockSpec((tr, L), lambda i: (i, 0))
    c1, c2 = 1.0 - B1 ** STEP, 1.0 - B2 ** STEP

    def body(w_ref, g_ref, m_ref, v_ref, d_ref, mo_ref, vo_ref):
        g_ = g_ref[...]
        m_ = B1 * m_ref[...] + (1.0 - B1) * g_
        v_ = B2 * v_ref[...] + (1.0 - B2) * (g_ * g_)
        d_ref[...] = -LR * ((m_ / c1) / (jnp.sqrt(v_ / c2) + ADAM_EPS) + WD * w_ref[...])
        mo_ref[...] = m_
        vo_ref[...] = v_

    sds = jax.ShapeDtypeStruct((R, L), F32)
    return pl.pallas_call(
        body, name=name, grid=(R // tr,), in_specs=[blk] * 4, out_specs=[blk] * 3, out_shape=[sds] * 3,
        compiler_params=_params(32, 1),
    )(w, g, m, v)


def _adamw_rows(w, m, v, gbufs, row0, *, name, plan=None):
    L, R, C = w.shape
    tr = 256
    assert R % tr == 0 and row0 % tr == 0 and len(gbufs) == L
    grid = (L, R // tr)
    blk = pl.BlockSpec((None, tr, C), lambda l, i: (l, i, 0))
    gblks = [pl.BlockSpec((tr, C), lambda l, i, k=k: (row0 // tr + jnp.where(l == k, i, 0), 0)) for k in range(L)]
    c1, c2 = 1.0 - B1 ** STEP, 1.0 - B2 ** STEP

    def body(*refs):
        ins, (go_ref, d_ref, mo_ref, vo_ref), _, pctx = _split_refs(refs, 3 + L, 4, 0, plan)
        w_ref, m_ref, v_ref = ins[:3]
        g_refs = ins[3:]
        _plan_start(plan, pctx, grid)
        g_ = g_refs[0][...]
        for l in range(1, L):
            g_ = jnp.where(pl.program_id(0) == l, g_refs[l][...], g_)
        m_ = B1 * m_ref[...] + (1.0 - B1) * g_
        v_ = B2 * v_ref[...] + (1.0 - B2) * (g_ * g_)
        go_ref[...] = g_
        d_ref[...] = -LR * ((m_ / c1) / (jnp.sqrt(v_ / c2) + ADAM_EPS) + WD * w_ref[...])
        mo_ref[...] = m_
        vo_ref[...] = v_
        _plan_wait(plan, pctx, grid)

    sds = jax.ShapeDtypeStruct((L, R, C), F32)
    p_in, p_ospec, p_oshape, p_scr, p_alias = _plan_io(plan, 3 + L, 4)
    return pl.pallas_call(
        body, name=name, grid=grid, in_specs=[blk] * 3 + gblks + [_ANY] * len(p_in),
        out_specs=[blk] * 4 + p_ospec, out_shape=[sds] * 4 + p_oshape, scratch_shapes=p_scr,
        input_output_aliases=p_alias, compiler_params=_params(32, 2),
    )(*_hbm(w, m, v, *gbufs), *p_in)


def _add_pairs(g, theirs, ids, *, name):
    n, R, L = theirs.shape
    tr = math.gcd(R, 128)
    nb = R // tr

    def body(ids_ref, a_ref, b_ref, o_ref):
        o_ref[...] = (a_ref[...].astype(F32) + b_ref[...].astype(F32)).astype(BF16)

    blk = pl.BlockSpec((n, tr, L), lambda i, ids: (0, i, 0))
    return pl.pallas_call(
        body, name=name, out_shape=jax.ShapeDtypeStruct((n, R, L), BF16),
        grid_spec=pltpu.PrefetchScalarGridSpec(
            num_scalar_prefetch=1, grid=(nb,),
            in_specs=[pl.BlockSpec((n, tr, L), lambda i, ids: (0, ids[1] * nb + i, 0)), blk], out_specs=blk),
        compiler_params=_params(16, 1),
    )(ids, g, theirs)


def _sum_chips(pair, parts, ids, *, name):
    _, R, L = parts.shape
    tr = math.gcd(R, 128)

    def body(ids_ref, o_ref, r_ref, out_ref):
        out_ref[...] = ((o_ref[...].astype(F32) + r_ref[0].astype(F32)) + r_ref[1].astype(F32)) + r_ref[2].astype(F32)

    return pl.pallas_call(
        body, name=name, out_shape=jax.ShapeDtypeStruct((2, R, L), F32),
        grid_spec=pltpu.PrefetchScalarGridSpec(
            num_scalar_prefetch=1, grid=(R // tr,),
            in_specs=[pl.BlockSpec((None, tr, L), lambda i, ids: (ids[0], i, 0)),
                      pl.BlockSpec((3, tr, L), lambda i, ids: (0, i, 0))],
            out_specs=pl.BlockSpec((None, tr, L), lambda i, ids: (ids[1], i, 0))),
        compiler_params=_params(32, 1),
    )(ids, pair, parts)


def _mesh_ids():
    x, y, c = _mesh_pos()
    return jnp.stack([2 * x + y, c]).astype(jnp.int32)


def _place_shard(rows, ids, *, name):
    R, L = rows.shape
    tr = 128

    def body(ids_ref, in_ref, out_ref):
        out_ref[...] = in_ref[...].astype(BF16)

    return pl.pallas_call(
        body, name=name, out_shape=jax.ShapeDtypeStruct((4, R, L), BF16),
        grid_spec=pltpu.PrefetchScalarGridSpec(
            num_scalar_prefetch=1, grid=(R // tr,), in_specs=[pl.BlockSpec((tr, L), lambda i, ids: (i, 0))],
            out_specs=pl.BlockSpec((None, tr, L), lambda i, ids: (ids[0], i, 0))),
        compiler_params=_params(16, 1),
    )(ids, rows)


def _place_weights(pieces, buffer_rows, ids, *, plan=None):
    tr = 256
    steps, s = [], 0
    for arr, layer, buf, row0 in pieces:
        nblk = arr.shape[1] // tr
        steps.append((s, nblk))
        s += nblk
    total = s
    buf_start = [min(st for (st, _), p in zip(steps, pieces) if p[2] == k) for k in range(len(buffer_rows))]
    grid = (total,)
    n_in = len(pieces)

    def body(*refs):
        ins, outs, _, pctx = _split_refs(refs[1:], n_in, len(buffer_rows), 0, plan)
        _plan_start(plan, pctx, grid)
        i = pl.program_id(0)
        for (st, nblk), (_, _, buf, _), ref in zip(steps, pieces, ins):
            @pl.when((i >= st) & (i < st + nblk))
            def _(ref=ref, buf=buf):
                outs[buf][...] = ref[...].astype(BF16)
        _plan_wait(plan, pctx, grid)

    in_specs = [pl.BlockSpec((None, tr, D), lambda i, ids, layer=layer, st=st, nblk=nblk:
                             (layer, jnp.clip(i - st, 0, nblk - 1), 0))
                for (st, nblk), (_, layer, _, _) in zip(steps, pieces)]
    out_specs = [pl.BlockSpec((None, tr, D), lambda i, ids, st=st, nb=rows // tr: (ids[0], jnp.clip(i - st, 0, nb - 1), 0))
                 for st, rows in zip(buf_start, buffer_rows)]
    p_in, p_ospec, p_oshape, p_scr, p_alias = _plan_io(plan, 1 + n_in, len(buffer_rows))
    return pl.pallas_call(
        body, name="place_weights",
        out_shape=[jax.ShapeDtypeStruct((4, rows, D), BF16) for rows in buffer_rows] + p_oshape,
        grid_spec=pltpu.PrefetchScalarGridSpec(
            num_scalar_prefetch=1, grid=grid, in_specs=in_specs + [_ANY] * len(p_in), out_specs=out_specs + p_ospec,
            scratch_shapes=p_scr),
        input_output_aliases=p_alias, compiler_params=_params(16, 1),
    )(ids, *[p[0] for p in pieces], *p_in)


def _remote(src, dst, send_sem, recv_sem, to):
    return pltpu.make_async_remote_copy(src_ref=src, dst_ref=dst, send_sem=send_sem, recv_sem=recv_sem,
                                        device_id=to, device_id_type=MESH_IDS)


def _rows(ref, lead, start, size):
    return ref.at[tuple(pl.ds(0, n) for n in ref.shape[:lead]) + (pl.ds(start, size),)]


def _other_chips():
    x, y, _ = _mesh_pos()
    return [(1 - x, y), (x, 1 - y), (1 - x, 1 - y)]


def _plan_gather_ici(bufs):
    n = len(bufs)

    def copies(outs, send, recv):
        x, y, c = _mesh_pos()
        res = []
        for b in range(n):
            half = bufs[b].shape[1] // 2
            mine = _rows(outs[b].at[2 * x + y], 0, c * half, half)
            for j, (cx, cy) in enumerate(_other_chips()):
                res.append((_remote(mine, mine, send(3 * b + j), recv(3 * b + j), (cx, cy, c)),
                            _remote(mine, _rows(outs[b].at[2 * cx + cy], 0, c * half, half),
                                    send(3 * b + j), recv(3 * b + j), (x, y, c))))
        return res

    def start(ins, outs, send, recv, loc):
        for out_cp, _ in copies(outs, send, recv):
            out_cp.start()

    def wait(ins, outs, send, recv, loc):
        for out_cp, in_cp in copies(outs, send, recv):
            in_cp.wait_recv()
            out_cp.wait_send()

    outs = [jax.ShapeDtypeStruct(b.shape, b.dtype) for b in bufs]
    return _Plan(bufs, outs, 3 * n, 0, start, wait, aliases={b: b for b in range(n)})


def _plan_gather_forward(bufs):
    n = len(bufs)

    def copies(outs, send, recv):
        x, y, c = _mesh_pos()
        res = []
        for b in range(n):
            half = bufs[b].shape[1] // 2
            for j, (cx, cy) in enumerate(_other_chips()):
                slot = outs[b].at[2 * cx + cy]
                res.append((_remote(_rows(slot, 0, c * half, half), _rows(slot, 0, c * half, half),
                                    send(3 * b + j), recv(3 * b + j), (x, y, 1 - c)),
                            _remote(_rows(slot, 0, c * half, half), _rows(slot, 0, (1 - c) * half, half),
                                    send(3 * b + j), recv(3 * b + j), (x, y, c))))
        return res

    def start(ins, outs, send, recv, loc):
        for out_cp, _ in copies(outs, send, recv):
            out_cp.start()

    def wait(ins, outs, send, recv, loc):
        for out_cp, in_cp in copies(outs, send, recv):
            in_cp.wait_recv()
            out_cp.wait_send()

    outs = [jax.ShapeDtypeStruct(b.shape, b.dtype) for b in bufs]
    return _Plan(bufs, outs, 3 * n, 0, start, wait, aliases={b: b for b in range(n)})


def _plan_pair_swap(g):
    half = g.shape[1] // 2

    def copy(ins, outs, send, recv, loc):
        x, y, c = _mesh_pos()
        return _remote(_rows(ins[0], 1, (1 - c) * half, half), outs[0], send(0), recv(0), (x, y, 1 - c))

    return _Plan([g], [jax.ShapeDtypeStruct((4, half, g.shape[2]), g.dtype)], 1, 0,
                 lambda *a: copy(*a).start(), lambda *a: copy(*a).wait())


def _plan_pair_gather(buf):
    def copies(ins, outs, send, recv, loc):
        x, y, c = _mesh_pos()
        return (_remote(outs[0].at[c], outs[0].at[c], send(0), recv(0), (x, y, 1 - c)),
                _remote(outs[0].at[c], outs[0].at[1 - c], send(0), recv(0), (x, y, c)))

    def wait(*a):
        out_cp, in_cp = copies(*a)
        in_cp.wait_recv()
        out_cp.wait_send()

    return _Plan([buf], [jax.ShapeDtypeStruct(buf.shape, buf.dtype)], 1, 0, lambda *a: copies(*a)[0].start(), wait,
                 aliases={0: 0})


def _plan_chip_scatter(p):
    def copies(ins, outs, send, recv, loc):
        _, _, c = _mesh_pos()
        return [_remote(ins[0].at[2 * cx + cy], outs[0].at[j], send(j), recv(j), (cx, cy, c))
                for j, (cx, cy) in enumerate(_other_chips())]

    def start(*a):
        for cp in copies(*a):
            cp.start()

    def wait(*a):
        for cp in copies(*a):
            cp.wait()

    return _Plan([p], [jax.ShapeDtypeStruct((3,) + p.shape[1:], p.dtype)], 3, 0, start, wait)


def _plan_exchange_all(vec):
    def copies(ins, outs, send, recv, loc):
        x, y, c = _mesh_pos()
        return [_remote(ins[0], outs[0].at[r - 1], send(r - 1), recv(r - 1), (x ^ (r >> 2), y ^ ((r >> 1) & 1), c ^ (r & 1)))
                for r in range(1, 8)]

    def start(*a):
        for cp in copies(*a):
            cp.start()

    def wait(*a):
        for cp in copies(*a):
            cp.wait()

    return _Plan([vec], [jax.ShapeDtypeStruct((7,) + vec.shape, vec.dtype)], 7, 0, start, wait)


SMALL_LAYOUT = {
    "mla_gq": (0, 1, 256, (1, 256)), "mla_gkv": (1, 1, 256, (1, 256)), "sgu_ln_g": (2, 1, 512, (1, 512)),
    "sgu_ln_b": (3, 1, 512, (1, 512)), "sgu_w": (4, 64, 1024, (64, 1024)), "sgu_b": (68, 1, 512, (1, 512)),
    "hg_lb": (69, 2, 1024, (2, 1024)), "hg_gnorm": (71, 1, 1024, (1, 256)), "ln1_g": (72, 2, 1024, (2, 1024)),
    "ln1_b": (74, 2, 1024, (2, 1024)), "ln2_g": (76, 2, 1024, (2, 1024)), "ln2_b": (78, 2, 1024, (2, 1024)),
}


def _small_pack(dgq, dgkv, dslg, dslb, dsw, dsb, dlb, dgn, ln_parts, sq_err):
    flat_ln = [p for pair in ln_parts for p in pair]

    def body(*refs):
        gq_ref, gkv_ref, slg_ref, slb_ref, sw_ref, sb_ref, lb_ref, gn_ref = refs[:8]
        ln_refs, err_ref, out_ref, t_sc = refs[8:16], refs[16], refs[17], refs[18]
        s8 = lambda ref: jnp.sum(ref[...], axis=0, keepdims=True)
        out_ref[...] = jnp.zeros_like(out_ref)
        out_ref[0:1, 0:256] = s8(gq_ref)
        out_ref[1:2, 0:256] = s8(gkv_ref)
        out_ref[2:3, 0:512] = s8(slg_ref)
        out_ref[3:4, 0:512] = s8(slb_ref)
        out_ref[4:68, :] = sw_ref[...]
        t_sc[...] = sb_ref[...].T
        for g in range(SGU_G):
            out_ref[68:69, g * SGU_C:(g + 1) * SGU_C] = t_sc[g:g + 1, :]
        d_lb1 = s8(lb_ref)
        out_ref[69:70, :] = -d_lb1
        out_ref[70:71, :] = d_lb1
        out_ref[71:72, :] = s8(gn_ref)
        for k, ref in enumerate(ln_refs):
            out_ref[72 + k:73 + k, :] = s8(ref)
        out_ref[0:1, 1023:1024] = jnp.sum(s8(err_ref), axis=1, keepdims=True) * (0.5 / D)

    vm = pl.BlockSpec(memory_space=pltpu.VMEM)
    return pl.pallas_call(
        body, name="small_grad_pack", in_specs=[vm] * 17, out_specs=vm,
        out_shape=jax.ShapeDtypeStruct((SMALL_ROWS, 1024), F32), scratch_shapes=[pltpu.VMEM((SGU_C, SGU_C), F32)],
        compiler_params=_params(16),
    )(dgq, dgkv, dslg, dslb, dsw.reshape(64, 1024), dsb, dlb, dgn, *flat_ln, sq_err)


def _small_update(vec, others, ids, w, m, v):
    names = list(SMALL_LAYOUT)
    n = len(names)
    c1, c2 = 1.0 - B1 ** STEP, 1.0 - B2 ** STEP
    have_others = others is not None

    def body(*refs):
        ids_ref, v_ref = refs[0], refs[1]
        k = 2 + have_others
        w_refs, m_refs, v_refs = refs[k:k + n], refs[k + n:k + 2 * n], refs[k + 2 * n:k + 3 * n]
        outs = refs[k + 3 * n:]
        row0_ref, tot_sc = outs[0], outs[-1]
        total = v_ref[...]
        if have_others:
            me = 2 * ids_ref[0] + ids_ref[1]
            total = None
            for d in range(8):
                rel = d ^ me
                term = jnp.where(rel == 0, v_ref[...], refs[2][jnp.maximum(rel - 1, 0)])
                total = term if total is None else total + term
        tot_sc[...] = total
        row0_ref[...] = tot_sc[0:1, :]
        for i, name in enumerate(names):
            r0, nr, width, _ = SMALL_LAYOUT[name]
            if name == "hg_gnorm":
                g_ = tot_sc[r0:r0 + 1, 0:256]
                for chip in range(1, 4):
                    g_ = jnp.where(ids_ref[0] == chip, tot_sc[r0:r0 + 1, chip * 256:(chip + 1) * 256], g_)
            else:
                g_ = tot_sc[r0:r0 + nr, 0:width]
            m_ = B1 * m_refs[i][...] + (1.0 - B1) * g_
            v_ = B2 * v_refs[i][...] + (1.0 - B2) * (g_ * g_)
            go, do, mo, vo = outs[1 + 4 * i:5 + 4 * i]
            go[...] = g_
            do[...] = -LR * ((m_ / c1) / (jnp.sqrt(v_ / c2) + ADAM_EPS) + WD * w_refs[i][...])
            mo[...] = m_
            vo[...] = v_

    full = lambda shape: pl.BlockSpec(shape, lambda i, ids, nd=len(shape): (0,) * nd)
    kshapes = [SMALL_LAYOUT[name][3] for name in names]
    operands = [vec] + ([others] if have_others else []) + [d[name] for d in (w, m, v) for name in names]
    out_shapes = [jax.ShapeDtypeStruct((1, 1024), F32)] + [jax.ShapeDtypeStruct(s, F32) for s in kshapes for _ in range(4)]
    res = pl.pallas_call(
        body, name="small_update", out_shape=out_shapes,
        grid_spec=pltpu.PrefetchScalarGridSpec(
            num_scalar_prefetch=1, grid=(1,), in_specs=[full(o.shape) for o in operands],
            out_specs=[full(s.shape) for s in out_shapes],
            scratch_shapes=[pltpu.VMEM((SMALL_ROWS, 1024), F32)]),
        compiler_params=_params(32, 1),
    )(ids, *operands)
    return res[0], {name: tuple(res[1 + 4 * i:5 + 4 * i]) for i, name in enumerate(names)}


ROWS_L1, ROWS_L0, ROWS_ODD_W = 3328, 2048, 384
ODD_PARTS = (("w_out_e", (256, 1024)), ("w_in_e", (1024, 392)), ("w_qb", (256, 192)), ("w_kvb", (256, 256)))
ODD_W_PARTS = tuple(p for p in ODD_PARTS if p[0] != "w_in_e")


def _odd_rows(parts, dtype, layout, total, gnorm=None):
    rows = [parts[n].reshape(-1, 1024).astype(dtype) for n, _ in layout]
    used = sum(r.shape[0] for r in rows)
    if gnorm is not None:
        bits = lax.bitcast_convert_type(gnorm.reshape(-1), BF16).reshape(1, 512)
        rows.append(jnp.pad(bits, ((0, 15), (0, 512))))
        used += 16
    if total > used:
        rows.append(jnp.zeros((total - used, 1024), dtype))
    return jnp.concatenate(rows, axis=0)


def _odd_unrows(buf, layout, with_gnorm=False):
    out, off = {}, 0
    for n, shape in layout:
        nr = math.prod(shape) // 1024
        out[n] = buf[off:off + nr].reshape(shape)
        off += nr
    if with_gnorm:
        out["hg_gnorm"] = lax.bitcast_convert_type(buf[off, :512].reshape(256, 2), F32).reshape(1, 256)
    return out


def _rope_tables(positions):
    half = ROPE // 2
    inv_freq = ROPE_BASE ** (-jnp.arange(half, dtype=F32) / half)
    ang = positions.astype(F32).reshape(-1, 1) * inv_freq
    cos, sin = jnp.cos(ang), jnp.sin(ang)
    T = ang.shape[0]
    one, z16, z32 = jnp.ones((T, NOPE), F32), jnp.zeros((T, half), F32), jnp.zeros((T, 32), F32)
    z64 = jnp.zeros((T, NOPE), F32)
    c = jnp.concatenate([one, cos, cos, z32], axis=1)
    s1 = jnp.concatenate([z64, -sin, z16, z32], axis=1)
    s2 = jnp.concatenate([z64, z16, sin, z32], axis=1)
    return c, s1, s2


def _local_step(x, positions, tgt, odd, bufs, P, exchange):
    T = x.shape[0]
    row = lambda a: a.reshape(1, -1)
    rc, rs1, rs2 = _rope_tables(positions)
    blk = lambda f: pl.BlockSpec((None, D, D), f)

    w_in_e = odd["w_in_e"]
    w_in = jnp.concatenate([w_in_e[:, :512], w_in_e[:, 544:1568], w_in_e[:, 512:544], jnp.zeros((D, 96), BF16)], axis=1)
    wq = jnp.pad(odd["w_qb"].reshape(256, HEADS, NOPE + ROPE), ((0, 0), (0, 0), (0, 32))).reshape(256, HEADS * 128)
    kvb = odd["w_kvb"].reshape(256, HEADS, NOPE + VDIM)
    wk = jnp.pad(kvb[:, :, :NOPE], ((0, 0), (0, 0), (0, 64))).reshape(256, HEADS * 128)
    wv = kvb[:, :, NOPE:].reshape(256, HEADS * VDIM)
    w_out_e = odd["w_out_e"]
    sgu_w = P["sgu_w"][0]
    sgu_bt = P["sgu_b"][0].T
    gq, gkv = P["mla_gq"], P["mla_gkv"]
    gnorm = P["hg_gnorm"]

    z0 = _matmul(x, w_in, name="in_proj_e", M=T, N=1664, K=D, tn=1664)[0]
    q, k, v = _mla_prep(z0, gq, gkv, wq, wk, wv, rc, rs1, rs2)
    if exchange:
        ids = _mesh_ids()
        placed = list(bufs)
        a_out, lse, wga, wgb = _flash_fwd(q, k, v, plan=_plan_gather_ici(placed[:2]))
    else:
        a_out, lse = _flash_fwd(q, k, v)
        wga, wgb, wgc = bufs
    mix0 = _sgu_fwd(z0, a_out, P["sgu_ln_g"], P["sgu_ln_b"], sgu_w, sgu_bt)
    res = _proj_ln(mix0, w_out_e, x, row(P["ln1_g"][0]), row(P["ln1_b"][0]), name="out_proj_ln_e",
                   plan=_plan_gather_forward([wga, wgb]) if exchange else None)
    r1, h1b = res[:2]
    if exchange:
        wga, wgb = res[2:]
    ln = lambda name, l: (row(P[name + "_g"][l]), row(P[name + "_b"][l]))
    res = _ffn_ln(h1b, wga, r1, *ln("ln2", 0), name="ffn_ln_0", prev_ln=ln("ln1", 0),
                  plan=_plan_gather_ici(placed[2:]) if exchange else None)
    ra0, r2, h2b = res[:3]
    z4 = _matmul(h2b, wgb, name="in_proj_o", M=T, N=4 * D, K=D, b_spec=blk(lambda i, j, k: (j, 0, 0)),
                 out_shape=jax.ShapeDtypeStruct((4, T, D), F32),
                 o_spec=pl.BlockSpec((None, min(MM_ROWS, T), D), lambda i, j, k: (j, i, 0)))[0]
    y1, o_raw, states = _hgrn_fwd(z4, P["hg_lb"], gnorm)
    res2 = _proj_ln(y1, wgb, r2, *ln("ln1", 1), name="out_proj_ln_o", prev_ln=ln("ln2", 0), w_rowblk=4,
                    plan=_plan_gather_forward([res[3]]) if exchange else None)
    r3, h3b = res2[:2]
    if exchange:
        wgc = res2[2]
    ra1, r4, _ = _ffn_ln(h3b, wgc, r3, *ln("ln2", 1), name="ffn_ln_1", prev_ln=ln("ln1", 1))

    ln1_g, ln1_b, ln2_g, ln2_b = [None, None], [None, None], [None, None], [None, None]
    sq_err_parts = []

    def ffn_bwd(l, dh, r_out, ra, h_mid_b, g2, wg, rows, plan=None, loss_head=()):
        dr, dr_b, dg, db, *sq_err = _ln_bwd(dh, r_out, row(g2), name=f"ln2_bwd_{l}", loss_head=loss_head)
        sq_err_parts.extend(sq_err)
        ln2_g[l], ln2_b[l] = dg, db
        da, *extra = _matmul(dr_b, wg, tb=True, mul=ra, out_dtype=BF16, name=f"ffn_da_{l}", M=T, N=4 * D, K=D,
                             b_spec=blk(lambda i, j, k: (j, 1, 0)), plan=plan)
        gbuf = _matmul(ra, dr_b, ta=True, a_sq=True, name=f"ffn_dw2_{l}", M=4 * D, N=D, K=T, tm=1024, tk=DW_TOKENS // 2,
                       out_shape=jax.ShapeDtypeStruct((4, rows, D), BF16), o_spec=blk(lambda i, j, k: (i, 1, 0)))[0]
        gbuf = _matmul(h_mid_b, da, ta=True, name=f"ffn_dw1_{l}", M=D, N=4 * D, K=T, tm=1024, tk=DW_TOKENS, into=gbuf,
                       out_shape=jax.ShapeDtypeStruct((4, rows, D), BF16), o_spec=blk(lambda i, j, k: (j, 0, 0)))[0]
        dh_mid = _matmul(da, wg, tb=True, add=dr, add_scale=ALPHA, name=f"ffn_dh_{l}", M=T, N=D, K=4 * D, tk=2 * D,
                         b_spec=pl.BlockSpec((2, D, D), lambda i, j, k: (k, 0, 0)))[0]
        return dh_mid, gbuf, extra

    dh3, g1, _ = ffn_bwd(1, None, r4, ra1, h3b, P["ln2_g"][1], wgc, ROWS_L1, loss_head=(row(P["ln2_b"][1]), tgt))
    loss_parts = sq_err_parts[0]
    dr3, dr3_b, dg, db = _ln_bwd(dh3, r3, row(P["ln1_g"][1]), name="ln1_bwd_1")
    ln1_g[1], ln1_b[1] = dg, db
    g1_sds = jax.ShapeDtypeStruct((4, ROWS_L1, D), BF16)
    g1 = _matmul(y1, dr3_b, ta=True, name="dw_out_o", M=D, N=D, K=T, tm=256, tk=DW_TOKENS, into=g1, out_shape=g1_sds,
                 o_spec=pl.BlockSpec((None, 256, D), lambda i, j, k: (i, 12, 0)))[0]
    dmix1 = _matmul(dr3_b, wgb, tb=True, name="dmix_o", M=T, N=D, K=D, b_spec=_rows4_spec(4, 3), b_merge=(D, D))[0]
    dz4, dlb, dgn = _hgrn_bwd(z4, o_raw, dmix1, states, P["hg_lb"], gnorm)
    g1 = _matmul(h2b, dz4, ta=True, name="dw_in_o", M=D, N=4 * D, K=T, tm=1024, tk=DW_TOKENS, into=g1, out_shape=g1_sds,
                 b_spec=pl.BlockSpec((None, min(DW_TOKENS, T), D), lambda i, j, k: (j, k, 0)),
                 o_spec=blk(lambda i, j, k: (j, 2, 0)))[0]
    dh2 = _matmul(dz4, wgb, tb=True, add=dr3, add_scale=ALPHA, name="dh_in_o", M=T, N=D, K=4 * D, tk=2 * D,
                  a_spec=pl.BlockSpec((2, min(MM_ROWS, T), D), lambda i, j, k: (k, i, 0)),
                  b_spec=pl.BlockSpec((2, D, D), lambda i, j, k: (k, 0, 0)))[0]

    dh1, g0, swapped1 = ffn_bwd(0, dh2, r2, ra0, h1b, P["ln2_g"][0], wga, ROWS_L0,
                                plan=_plan_pair_swap(g1) if exchange else None)
    dr1, dr1_b, dg, db = _ln_bwd(dh1, r1, row(P["ln1_g"][0]), name="ln1_bwd_0")
    ln1_g[0], ln1_b[0] = dg, db
    godd = {"w_out_e": _matmul(mix0, dr1_b, ta=True, name="dw_out_e", M=D, N=D, K=T, tm=1024, tk=DW_TOKENS)[0]}
    dmix0, *swapped0 = _matmul(dr1_b, w_out_e, tb=True, name="dmix_e", M=T, N=D, K=D,
                               plan=_plan_pair_swap(g0) if exchange else None)
    delta, do_b = _attn_delta(dmix0, a_out)
    if exchange:
        pair1 = _add_pairs(g1, swapped1[0], ids, name="grad_pair_add_1")
        pair0 = _add_pairs(g0, swapped0[0], ids, name="grad_pair_add_0")
        dq4, dk, dv, parts0, parts1 = _flash_bwd(
            q, k, v, do_b, lse, delta, plan=_join_plans([_plan_chip_scatter(pair0), _plan_chip_scatter(pair1)]))
        half0 = _sum_chips(pair0, parts0, ids, name="grad_chip_sum_0")
        half1 = _sum_chips(pair1, parts1, ids, name="grad_chip_sum_1")
        dc, dkr, dwq, dwk, dwv, dgq, dgkv, g0, g1 = _mla_bwd(
            z0, dq4, dk, dv, gq, gkv, wq, wk, wv, rc, rs1, rs2,
            plan=_join_plans([_plan_pair_gather(half0), _plan_pair_gather(half1)]))
        g0, g1 = g0.reshape(ROWS_L0, D), g1.reshape(ROWS_L1, D)
    else:
        dq4, dk, dv = _flash_bwd(q, k, v, do_b, lse, delta)
        dc, dkr, dwq, dwk, dwv, dgq, dgkv = _mla_bwd(z0, dq4, dk, dv, gq, gkv, wq, wk, wv, rc, rs1, rs2)
    godd["w_qb"] = dwq.reshape(256, HEADS, 128)[:, :, :NOPE + ROPE].reshape(256, HEADS * (NOPE + ROPE))
    godd["w_kvb"] = jnp.concatenate([dwk.reshape(256, HEADS, 128)[:, :, :NOPE], dwv.reshape(256, HEADS, VDIM)],
                                    axis=2).reshape(256, HEADS * (NOPE + VDIM))
    swap_b = None
    if exchange:
        by_chip = [_odd_rows({"w_out_e": jnp.split(godd["w_out_e"], 4, axis=0)[j],
                              **{n: jnp.split(godd[n], 4, axis=1)[j] for n in ("w_qb", "w_kvb")}}, BF16,
                             ODD_W_PARTS, ROWS_ODD_W)
                   for j in range(4)]
        odd_b = jnp.stack(by_chip)
        swap_b = _plan_pair_swap(odd_b)
    dz0, dsw, dsb, dslg, dslb, *theirs_b = _sgu_bwd(z0, dmix0, dc, dkr, P["sgu_ln_g"], P["sgu_ln_b"], sgu_w, sgu_bt,
                                                    plan=swap_b)
    small_vec = _small_pack(dgq, dgkv, dslg, dslb, dsw, dsb, dlb, dgn, [ln1_g, ln1_b, ln2_g, ln2_b], loss_parts)
    plan_in = None
    if exchange:
        pair_b = _add_pairs(odd_b, theirs_b[0], ids, name="odd_pair_add_1")
        plan_in = _join_plans([_plan_exchange_all(small_vec), _plan_chip_scatter(pair_b)])
    dw_in, *carried = _matmul(x, dz0, ta=True, name="dw_in_e", M=D, N=1664, K=T, tm=1024, tn=1664, tk=DW_TOKENS // 4,
                              plan=plan_in)
    godd["w_in_e"] = jnp.concatenate([dw_in[:, :512], dw_in[:, 1536:1568], dw_in[:, 512:1536]], axis=1)
    plan_x = None
    if exchange:
        small_others, parts_b = carried
        odd_a = godd["w_in_e"].reshape(D, 4, 392).transpose(1, 0, 2).astype(BF16)
        theirs_a = _run_plan(_plan_pair_swap(odd_a), name="odd_pair_swap")[0]
        pair_a = _add_pairs(odd_a, theirs_a, ids, name="odd_pair_add_0")
        plan_x = _plan_chip_scatter(pair_a)
    grad_x, *parts_a = _matmul(dz0, w_in, tb=True, add=dr1, add_scale=ALPHA, name="dx", M=T, N=D, K=1664, tk=1664,
                               plan=plan_x)
    if exchange:
        godd = ([pair_a, pair_b], [parts_a[0], parts_b])
        return grad_x, g0, g1, godd, small_vec, small_others
    return grad_x, g0, g1, godd, small_vec, None


WEIGHTS = ['w_in_e', 'mla_gq', 'mla_gkv', 'w_qb', 'w_kvb', 'sgu_ln_g', 'sgu_ln_b', 'sgu_w', 'sgu_b', 'w_out_e',
           'w_in_o', 'hg_lb', 'hg_gnorm', 'w_out_o', 'ln1_g', 'ln1_b', 'w_ff1', 'w_ff2', 'ln2_g', 'ln2_b']


def kernel(x, positions, w_in_e, mla_gq, mla_gkv, w_qb, w_kvb, sgu_ln_g, sgu_ln_b, sgu_w, sgu_b, w_out_e, w_in_o, hg_lb, hg_gnorm, w_out_o, ln1_g, ln1_b, w_ff1, w_ff2, ln2_g, ln2_b, loss_target, m_w_in_e, m_mla_gq, m_mla_gkv, m_w_qb, m_w_kvb, m_sgu_ln_g, m_sgu_ln_b, m_sgu_w, m_sgu_b, m_w_out_e, m_w_in_o, m_hg_lb, m_hg_gnorm, m_w_out_o, m_ln1_g, m_ln1_b, m_w_ff1, m_w_ff2, m_ln2_g, m_ln2_b, v_w_in_e, v_mla_gq, v_mla_gkv, v_w_qb, v_w_kvb, v_sgu_ln_g, v_sgu_ln_b, v_sgu_w, v_sgu_b, v_w_out_e, v_w_in_o, v_hg_lb, v_hg_gnorm, v_w_out_o, v_ln1_g, v_ln1_b, v_w_ff1, v_w_ff2, v_ln2_g, v_ln2_b):
    args = dict(locals())
    w = {n: args[n] for n in WEIGHTS}
    m = {n: args["m_" + n] for n in WEIGHTS}
    v = {n: args["v_" + n] for n in WEIGHTS}
    cx, cy, cc = _mesh_pos()
    chip = 2 * cx + cy

    odd_shard = _odd_rows({"w_out_e": w_out_e[0], "w_qb": w_qb[0], "w_kvb": w_kvb[0]}, BF16, ODD_W_PARTS, ROWS_ODD_W,
                          gnorm=hg_gnorm)
    ids = _mesh_ids()
    placed = [_place_shard(w_in_e[0], ids, name="place_shard_in_e"), _place_shard(odd_shard, ids, name="place_shard_odd")]
    pieces = [(w_ff1, 0, 0, 0), (w_ff2, 0, 0, 1024), (w_in_o, 0, 1, 0), (w_out_o, 0, 1, 1024),
              (w_ff1, 1, 2, 0), (w_ff2, 1, 2, 1024)]
    *big_bufs, odd_a, odd_b = _place_weights(pieces, (2048, 1280, 2048), ids, plan=_plan_gather_ici(placed))
    gathered = _run_plan(_plan_gather_forward([odd_a, odd_b]), name="odd_gather_forward")
    per_chip = [_odd_unrows(gathered[1][j], ODD_W_PARTS, with_gnorm=True) for j in range(4)]
    odd = {"w_out_e": jnp.concatenate([p["w_out_e"] for p in per_chip], axis=0),
           "w_in_e": jnp.concatenate([gathered[0][j] for j in range(4)], axis=1)}
    for n in ("w_qb", "w_kvb"):
        odd[n] = jnp.concatenate([p[n] for p in per_chip], axis=1)
    small = {n: w[n] for n in SMALL_LAYOUT if n != "hg_gnorm"}
    small["hg_gnorm"] = jnp.concatenate([p["hg_gnorm"] for p in per_chip], axis=1)
    grad_x, g_l0, g_l1, godd, small_vec, small_others = _local_step(
        x[0], positions[0], loss_target[0], odd, big_bufs, small, True)

    sums = [_sum_chips(pair, parts, ids, name=f"odd_chip_sum_{k}") for k, (pair, parts) in enumerate(zip(*godd))]
    g_in_e, g_rest = _run_plan(_join_plans([_plan_pair_gather(s) for s in sums]), name="odd_pair_gather")
    g_odd = _odd_unrows(g_rest.reshape(ROWS_ODD_W, 1024), ODD_W_PARTS)
    g_odd["w_in_e"] = g_in_e.reshape(D, 392)

    to_kernel = lambda d: {n: d[n].reshape(SMALL_LAYOUT[n][3]) for n in SMALL_LAYOUT}
    first_row, small_out = _small_update(small_vec, small_others, ids, to_kernel(w), to_kernel(m), to_kernel(v))
    loss = first_row[0, 1023]
    grads, delta, new_m, new_v = {}, {}, {}, {}
    for n, res in small_out.items():
        grads[n], delta[n], new_m[n], new_v[n] = (r.reshape(w[n].shape) for r in res)

    for n, bufs_, row0 in (("w_ff1", [g_l0, g_l1], 0), ("w_ff2", [g_l0, g_l1], 1024), ("w_in_o", [g_l1], 2048),
                           ("w_out_o", [g_l1], 3072)):
        grads[n], delta[n], new_m[n], new_v[n] = _adamw_rows(w[n], m[n], v[n], bufs_, row0, name=f"adamw_{n}")
    for n, _ in ODD_PARTS:
        grads[n] = g_odd[n][None]
        d_, m_, v_ = _adamw(w[n][0], g_odd[n], m[n][0], v[n][0], name=f"adamw_{n}")
        delta[n], new_m[n], new_v[n] = d_[None], m_[None], v_[None]

    return (loss, grad_x[None], *[grads[n] for n in WEIGHTS], *[delta[n] for n in WEIGHTS],
            *[new_m[n] for n in WEIGHTS], *[new_v[n] for n in WEIGHTS])
```

```python
import math

import jax
import jax.numpy as jnp
from jax import lax
from jax.experimental import pallas as pl
from jax.experimental.pallas import tpu as pltpu

F32 = jnp.float32
BF16 = jnp.bfloat16
MESH_IDS = pl.DeviceIdType.MESH

D = 1024
DEPTH = 2
HEADS = 8
NOPE, ROPE, VDIM = 64, 32, 64
QK_SCALE = (NOPE + ROPE) ** -0.5
ROPE_BASE = 10000.0
SGU_G, SGU_C = 4, 128
HG_CHUNK = 64
HG_HEADS_PER_STEP = 8
ALPHA = (2 * DEPTH) ** 0.25
EPS = 1e-5
LR, B1, B2, ADAM_EPS, WD, STEP = 0.001, 0.9, 0.999, 1e-08, 0.01, 10
GELU_C = math.sqrt(2.0 / math.pi)
GELU_A = 0.044715
MB = 1024 * 1024
ROW_BLOCK = 512
SMALL_ROWS = 80

NT_DIMS = (((1,), (1,)), ((), ()))
TN_DIMS = (((0,), (0,)), ((), ()))


def _params(vmem_mb, n_axes=0):
    kw = dict(vmem_limit_bytes=vmem_mb * MB)
    if n_axes:
        kw["dimension_semantics"] = ("arbitrary",) * n_axes
    return pltpu.CompilerParams(**kw)


_ANY = pl.BlockSpec(memory_space=pltpu.HBM)


def _mesh_pos():
    return lax.axis_index("x"), lax.axis_index("y"), lax.axis_index("c")


def _hbm(*arrays):
    return tuple(pltpu.with_memory_space_constraint(a, pltpu.HBM) if a.size >= 2 ** 18 else a for a in arrays)


class _Plan:
    def __init__(self, ins, outs, n_remote, n_local, start, wait, aliases=None):
        self.ins, self.outs, self.n_remote, self.n_local = list(ins), list(outs), n_remote, n_local
        self.start, self.wait, self.aliases = start, wait, dict(aliases or {})


def _join_plans(plans):
    ins, outs, aliases, parts = [], [], {}, []
    nr = nl = 0
    for p in plans:
        parts.append((p, len(ins), len(outs), nr, nl))
        aliases.update({len(ins) + i: len(outs) + o for i, o in p.aliases.items()})
        ins += p.ins
        outs += p.outs
        nr += p.n_remote
        nl += p.n_local

    def run(which):
        def go(in_refs, out_refs, send, recv, loc):
            for p, i0, o0, r0, l0 in parts:
                getattr(p, which)(in_refs[i0:i0 + len(p.ins)], out_refs[o0:o0 + len(p.outs)],
                                  lambda i, r0=r0: send(r0 + i), lambda i, r0=r0: recv(r0 + i),
                                  lambda i, l0=l0: loc(l0 + i))
        return go

    return _Plan(ins, outs, nr, nl, run("start"), run("wait"), aliases)


def _plan_io(plan, n_in, n_out):
    if plan is None:
        return [], [], [], [], {}
    sems = [pltpu.SemaphoreType.DMA((max(plan.n_remote, 1),)), pltpu.SemaphoreType.DMA((max(plan.n_remote, 1),)),
            pltpu.SemaphoreType.DMA((max(plan.n_local, 1),))]
    aliases = {n_in + i: n_out + o for i, o in plan.aliases.items()}
    return plan.ins, [_ANY] * len(plan.outs), plan.outs, sems, aliases


def _split_refs(refs, n_in, n_out, n_scr, plan):
    p_in, p_out = (len(plan.ins), len(plan.outs)) if plan is not None else (0, 0)
    refs = list(refs)
    ins, refs = refs[:n_in], refs[n_in:]
    pins, refs = refs[:p_in], refs[p_in:]
    outs, refs = refs[:n_out], refs[n_out:]
    pouts, refs = refs[:p_out], refs[p_out:]
    scr, psem = refs[:n_scr], refs[n_scr:]
    psem = tuple((lambda i, s=s: s.at[i]) for s in psem)
    return ins, outs, scr, (pins, pouts, psem)


def _grid_edge(grid, last):
    cond = None
    for ax, n in enumerate(grid):
        c = pl.program_id(ax) == (n - 1 if last else 0)
        cond = c if cond is None else cond & c
    return cond


def _plan_start(plan, pctx, grid):
    if plan is not None:
        pins, pouts, psem = pctx
        pl.when(_grid_edge(grid, False))(lambda: plan.start(pins, pouts, *psem))


def _plan_wait(plan, pctx, grid):
    if plan is not None:
        pins, pouts, psem = pctx
        pl.when(_grid_edge(grid, True))(lambda: plan.wait(pins, pouts, *psem))


def _run_plan(plan, *, name):
    def body(*refs):
        _, _, _, (pins, pouts, psem) = _split_refs(refs, 0, 0, 0, plan)
        plan.start(pins, pouts, *psem)
        plan.wait(pins, pouts, *psem)

    p_in, p_ospec, p_oshape, p_scr, p_alias = _plan_io(plan, 0, 0)
    return pl.pallas_call(body, name=name, in_specs=[_ANY] * len(p_in), out_specs=p_ospec, out_shape=p_oshape,
                          scratch_shapes=p_scr, input_output_aliases=p_alias)(*p_in)


def _fold8(x):
    return x.reshape(x.shape[0] // 8, 8, x.shape[1]).sum(axis=0)


def _ln_stats(r):
    mu = jnp.mean(r, -1, keepdims=True)
    xc = r - mu
    rstd = lax.rsqrt(jnp.mean(xc * xc, -1, keepdims=True) + EPS)
    return xc * rstd, rstd


def _sigmoid(x):
    return jax.nn.sigmoid(x)


def _gelu(x):
    return 0.5 * x * (1.0 + jnp.tanh(GELU_C * (x + GELU_A * x * x * x)))


def _gelu_grad(x):
    t = jnp.tanh(GELU_C * (x + GELU_A * x * x * x))
    return 0.5 * (1.0 + t) + 0.5 * x * (1.0 - t * t) * GELU_C * (1.0 + 3.0 * GELU_A * x * x)


MM_ROWS = 1024
DW_TOKENS = 4096


def _matmul(a, b, *, name, M, N, K, ta=False, tb=False, out_dtype=F32, tm=MM_ROWS, tn=1024, tk=1024,
            a_spec=None, b_spec=None, b_merge=None, out_shape=None, o_spec=None, into=None,
            a_sq=False, mul=None, add=None, add_scale=1.0, n_slots=False, plan=None):
    assert not n_slots or (K // min(tk, K) == 1 and add is None)
    tm, tn, tk = min(tm, M), min(tn, N), min(tk, K)
    assert M % tm == 0 and N % tn == 0 and K % tk == 0
    grid = (M // tm, N // tn, K // tk)
    nk = grid[2]
    if a_spec is None:
        a_spec = pl.BlockSpec((tk, tm), lambda i, j, k: (k, i)) if ta else pl.BlockSpec((tm, tk), lambda i, j, k: (i, k))
    if b_spec is None:
        b_spec = pl.BlockSpec((tn, tk), lambda i, j, k: (j, k)) if tb else pl.BlockSpec((tk, tn), lambda i, j, k: (k, j))
    if o_spec is None:
        o_spec = pl.BlockSpec((tm, tn), lambda i, j, k: (i, j))
        out_shape = jax.ShapeDtypeStruct((M, N), out_dtype)
    e_spec = pl.BlockSpec((tm, tn), lambda i, j, k: (i, j))
    dims = (((0 if ta else 1,), (1 if tb else 0,)), ((), ()))
    extra = [e for e in (mul, add, into) if e is not None]
    n_in = 2 + len(extra)

    def body(*refs):
        ins, outs, scr, pctx = _split_refs(refs, n_in, 1, 1 if nk > 1 else 0, plan)
        a_ref, b_ref = ins[0], ins[1]
        rest = list(ins[2:])
        mul_ref = rest.pop(0) if mul is not None else None
        add_ref = rest.pop(0) if add is not None else None
        o_ref = outs[0]
        _plan_start(plan, pctx, grid)
        av = a_ref[...].astype(BF16)
        if a_sq:
            av = av * av
        bv = b_ref[...]
        if b_merge is not None:
            bv = bv.reshape(b_merge)
        if n_slots:
            for s in range(bv.shape[0]):
                r = lax.dot_general(av, bv[s], dims, preferred_element_type=F32)
                w = r.shape[1]
                if mul_ref is not None:
                    r = r * (2.0 * mul_ref[:, s * w:(s + 1) * w].astype(F32))
                if o_ref.ndim == 3:
                    o_ref[s] = r.astype(o_ref.dtype)
                else:
                    o_ref[:, s * w:(s + 1) * w] = r.astype(o_ref.dtype)
            _plan_wait(plan, pctx, grid)
            return
        if bv.ndim == 3:
            w = av.shape[-1] // (1 if av.ndim == 3 else bv.shape[0])
            a_parts = [av[s] if av.ndim == 3 else av[:, s * w:(s + 1) * w] for s in range(bv.shape[0])]
            p = sum(lax.dot_general(a_parts[s], bv[s], dims, preferred_element_type=F32) for s in range(bv.shape[0]))
        else:
            p = lax.dot_general(av, bv, dims, preferred_element_type=F32)

        def finish(r):
            if mul_ref is not None:
                r = r * (2.0 * mul_ref[...].astype(F32))
            if add_ref is not None:
                r = r + add_scale * add_ref[...]
            o_ref[...] = r.astype(o_ref.dtype)

        if nk == 1:
            finish(p)
        else:
            acc_ref = scr[0]
            k = pl.program_id(2)

            @pl.when(k == 0)
            def _():
                acc_ref[...] = p

            @pl.when(k > 0)
            def _():
                acc_ref[...] += p

            @pl.when(k == nk - 1)
            def _():
                finish(acc_ref[...])

        _plan_wait(plan, pctx, grid)

    p_in, p_ospec, p_oshape, p_scr, p_alias = _plan_io(plan, n_in, 1)
    aliases = dict(p_alias)
    if into is not None:
        aliases[n_in - 1] = 0
    return pl.pallas_call(
        body, name=name, grid=grid,
        in_specs=[a_spec, b_spec] + [e_spec] * (len(extra) - (into is not None)) + [_ANY] * (into is not None)
        + [_ANY] * len(p_in),
        out_specs=[o_spec] + p_ospec, out_shape=[out_shape] + p_oshape,
        scratch_shapes=([pltpu.VMEM((tm, tn), F32)] if nk > 1 else []) + p_scr,
        input_output_aliases=aliases, compiler_params=_params(48, 3),
    )(*_hbm(a, b, *extra), *p_in)


def _rows4_spec(rowblk, n_axes):
    return pl.BlockSpec((4, 256, D), lambda *_: (0, rowblk, 0))


def _residual(h_ref, prev_refs):
    if not prev_refs:
        return h_ref[...]
    xhat, _ = _ln_stats(h_ref[...])
    return xhat * prev_refs[0][...] + prev_refs[1][...]


def _proj_ln(a_b, w, h_prev, g, b, *, name, prev_ln=(), w_rowblk=None, plan=None):
    T = a_b.shape[0]
    tm = min(MM_ROWS, T)
    grid = (T // tm,)
    row = pl.BlockSpec((tm, D), lambda i: (i, 0))
    vec = pl.BlockSpec((1, D), lambda i: (0, 0))
    w_spec = pl.BlockSpec((D, D), lambda i: (0, 0)) if w_rowblk is None else _rows4_spec(w_rowblk, 1)
    n_in = 5 + len(prev_ln)

    def body(*refs):
        ins, (r_ref, hb_ref), _, pctx = _split_refs(refs, n_in, 2, 0, plan)
        a_ref, w_ref, h_ref, g_ref, b_ref = ins[:5]
        _plan_start(plan, pctx, grid)
        mix = jnp.dot(a_ref[...], w_ref[...].reshape(D, D), preferred_element_type=F32)
        r = ALPHA * _residual(h_ref, ins[5:]) + mix
        xhat, _ = _ln_stats(r)
        r_ref[...] = r
        hb_ref[...] = (xhat * g_ref[...] + b_ref[...]).astype(BF16)
        _plan_wait(plan, pctx, grid)

    p_in, p_ospec, p_oshape, p_scr, p_alias = _plan_io(plan, n_in, 2)
    return pl.pallas_call(
        body, name=name, grid=grid,
        in_specs=[row, w_spec, row, vec, vec] + [vec] * len(prev_ln) + [_ANY] * len(p_in),
        out_specs=[row, row] + p_ospec,
        out_shape=[jax.ShapeDtypeStruct((T, D), F32), jax.ShapeDtypeStruct((T, D), BF16)] + p_oshape,
        scratch_shapes=p_scr, input_output_aliases=p_alias, compiler_params=_params(40, 1),
    )(*_hbm(a_b, w, h_prev, g, b, *prev_ln), *p_in)


def _ffn_ln(h_b, wbuf, h, g, b, *, name, prev_ln=(), plan=None):
    T = h_b.shape[0]
    slots = 2
    tm, tf = min(ROW_BLOCK, T), slots * 1024
    nf = 4 // slots
    F = nf * tf
    grid = (T // tm, nf)
    row = pl.BlockSpec((tm, D), lambda i, j: (i, 0))
    vec = pl.BlockSpec((1, D), lambda i, j: (0, 0))
    n_in = 6 + len(prev_ln)

    def body(*refs):
        ins, (ra_ref, r_ref, hbo_ref), (acc_ref,), pctx = _split_refs(refs, n_in, 3, 1, plan)
        hb_ref, w1_ref, w2_ref, h_ref, g_ref, b_ref = ins[:6]
        _plan_start(plan, pctx, grid)
        j = pl.program_id(1)
        hb = hb_ref[...]
        p = None
        for s in range(slots):
            ra = jnp.maximum(jnp.dot(hb, w1_ref[s], preferred_element_type=F32), 0.0)
            ra_ref[:, s * 1024:(s + 1) * 1024] = ra.astype(BF16)
            ps = jnp.dot((ra * ra).astype(BF16), w2_ref[s], preferred_element_type=F32)
            p = ps if p is None else p + ps

        @pl.when(j == 0)
        def _():
            acc_ref[...] = p

        @pl.when(j > 0)
        def _():
            acc_ref[...] += p

        @pl.when(j == nf - 1)
        def _():
            r = ALPHA * _residual(h_ref, ins[6:]) + acc_ref[...]
            xhat, _ = _ln_stats(r)
            r_ref[...] = r
            hbo_ref[...] = (xhat * g_ref[...] + b_ref[...]).astype(BF16)

        _plan_wait(plan, pctx, grid)

    p_in, p_ospec, p_oshape, p_scr, p_alias = _plan_io(plan, n_in, 3)
    return pl.pallas_call(
        body, name=name, grid=grid,
        in_specs=[row, pl.BlockSpec((slots, D, D), lambda i, j: (j, 0, 0)),
                  pl.BlockSpec((slots, D, D), lambda i, j: (j, 1, 0)), row, vec, vec] + [vec] * len(prev_ln)
        + [_ANY] * len(p_in),
        out_specs=[pl.BlockSpec((tm, tf), lambda i, j: (i, j)), row, row] + p_ospec,
        out_shape=[jax.ShapeDtypeStruct((T, F), BF16), jax.ShapeDtypeStruct((T, D), F32),
                   jax.ShapeDtypeStruct((T, D), BF16)] + p_oshape,
        scratch_shapes=[pltpu.VMEM((tm, D), F32)] + p_scr,
        input_output_aliases=p_alias, compiler_params=_params(56, 2),
    )(*_hbm(h_b, wbuf, wbuf, h, g, b, *prev_ln), *p_in)


def _ln_bwd(dy, r, g, *, name, loss_head=()):
    T = r.shape[0]
    tm = min(ROW_BLOCK, T)
    row = pl.BlockSpec((tm, D), lambda i: (i, 0))
    vec = pl.BlockSpec((1, D), lambda i: (0, 0))
    acc = pl.BlockSpec((8, D), lambda i: (0, 0))
    operands, in_specs = ([r, g, *loss_head], [row, vec, vec, row]) if loss_head else ([r, g, dy], [row, vec, row])
    n_in = len(operands)

    def body(*refs):
        r_ref, g_ref = refs[:2]
        dr_ref, drb_ref, dg_ref, db_ref = refs[n_in:n_in + 4]

        @pl.when(pl.program_id(0) == 0)
        def _():
            for ref in refs[n_in + 2:]:
                ref[...] = jnp.zeros_like(ref)

        xhat, rstd = _ln_stats(r_ref[...])
        if loss_head:
            err = xhat * g_ref[...] + refs[2][...] - refs[3][...]
            refs[n_in + 4][...] += _fold8(err * err)
            dy_ = err * (1.0 / D)
        else:
            dy_ = refs[2][...]
        dxh = dy_ * g_ref[...]
        m1 = jnp.mean(dxh, -1, keepdims=True)
        m2 = jnp.mean(dxh * xhat, -1, keepdims=True)
        dr = rstd * (dxh - m1 - xhat * m2)
        dr_ref[...] = dr
        drb_ref[...] = dr.astype(BF16)
        dg_ref[...] += _fold8(dy_ * xhat)
        db_ref[...] += _fold8(dy_)

    n_acc = 3 if loss_head else 2
    return pl.pallas_call(
        body, name=name, grid=(T // tm,), in_specs=in_specs, out_specs=[row, row] + [acc] * n_acc,
        out_shape=[jax.ShapeDtypeStruct((T, D), F32), jax.ShapeDtypeStruct((T, D), BF16)]
        + [jax.ShapeDtypeStruct((8, D), F32)] * n_acc,
        compiler_params=_params(40, 1),
    )(*_hbm(*operands))


def _rope(x, c, s1, s2):
    return x * c + pltpu.roll(x, 112, 1) * s1 + pltpu.roll(x, 16, 1) * s2


def _rope_t(dy, c, s1, s2):
    return dy * c + pltpu.roll(dy * s1, 16, 1) + pltpu.roll(dy * s2, 112, 1)


def _rms(x, g):
    rstd = lax.rsqrt(jnp.mean(x * x, -1, keepdims=True) + EPS)
    xhat = x * rstd
    return xhat * g, xhat, rstd


def _mla_prep(z0, gq, gkv, wq, wk, wv, rc, rs1, rs2):
    T = z0.shape[0]
    tm = min(ROW_BLOCK, T)
    HW = HEADS * 128

    def body(cq_ref, ckv_ref, kr_ref, gq_ref, gkv_ref, wq_ref, wk_ref, wv_ref, c_ref, s1_ref, s2_ref,
             q_ref, k_ref, v_ref):
        nq = _rms(cq_ref[...], gq_ref[...])[0].astype(BF16)
        nkv = _rms(ckv_ref[...], gkv_ref[...])[0].astype(BF16)
        q = jnp.dot(nq, wq_ref[...], preferred_element_type=F32)
        k = jnp.dot(nkv, wk_ref[...], preferred_element_type=F32)
        v = jnp.dot(nkv, wv_ref[...], preferred_element_type=F32)
        c, s1, s2 = c_ref[...], s1_ref[...], s2_ref[...]
        kr = _rope(pltpu.roll(kr_ref[...], 64, 1), c, s1, s2)
        for h in range(HEADS):
            sl = slice(h * 128, (h + 1) * 128)
            q_ref[:, sl] = (_rope(q[:, sl], c, s1, s2) * QK_SCALE).astype(BF16)
            k_ref[:, sl] = (k[:, sl] + kr).astype(BF16)
        v_ref[...] = v.astype(BF16)

    full = lambda shape: pl.BlockSpec(shape, lambda i: (0, 0))
    tab = pl.BlockSpec((tm, 128), lambda i: (i, 0))
    return pl.pallas_call(
        body, name="mla_prep", grid=(T // tm,),
        in_specs=[pl.BlockSpec((tm, 256), lambda i: (i, 0)), pl.BlockSpec((tm, 256), lambda i: (i, 1)),
                  pl.BlockSpec((tm, 128), lambda i: (i, 12)), full((1, 256)), full((1, 256)),
                  full((256, HW)), full((256, HW)), full((256, 512)), tab, tab, tab],
        out_specs=[pl.BlockSpec((tm, HW), lambda i: (i, 0)), pl.BlockSpec((tm, HW), lambda i: (i, 0)),
                   pl.BlockSpec((tm, 512), lambda i: (i, 0))],
        out_shape=[jax.ShapeDtypeStruct((T, HW), BF16), jax.ShapeDtypeStruct((T, HW), BF16),
                   jax.ShapeDtypeStruct((T, 512), BF16)],
        compiler_params=_params(40, 1),
    )(z0, z0, z0, gq, gkv, wq, wk, wv, rc, rs1, rs2)


def _flash_fwd(q, k, v, plan=None):
    T = q.shape[0]
    bq = min(2 * ROW_BLOCK, T)
    nq = T // bq
    grid = (4, nq, nq)

    def body(*refs):
        (q_ref, k_ref, v_ref), (o_ref, lse_ref), (m_sc, acc_sc), pctx = _split_refs(refs, 3, 2, 2, plan)
        _plan_start(plan, pctx, grid)
        i, j = pl.program_id(1), pl.program_id(2)
        first = lax.broadcasted_iota(jnp.int32, (bq, 128), 1) < 64

        @pl.when(j == 0)
        def _():
            m_sc[...] = jnp.full_like(m_sc, -jnp.inf)
            acc_sc[...] = jnp.zeros_like(acc_sc)

        def step(masked):
            vp = v_ref[...]
            for h in range(2):
                sl = slice(h * 128, (h + 1) * 128)
                s = lax.dot_general(q_ref[:, sl], k_ref[:, sl], NT_DIMS, preferred_element_type=F32)
                if masked:
                    rows = lax.broadcasted_iota(jnp.int32, (bq, bq), 0)
                    cols = lax.broadcasted_iota(jnp.int32, (bq, bq), 1)
                    s = jnp.where(cols <= rows, s, -jnp.inf)
                m_prev = m_sc[h, :, 0:1]
                m_new = jnp.maximum(m_prev, jnp.max(s, axis=1, keepdims=True))
                alpha = jnp.exp(m_prev - m_new)
                p = jnp.exp(s - m_new).astype(BF16)
                vh = jnp.where(first if h == 0 else jnp.logical_not(first), vp, jnp.ones_like(vp))
                acc_sc[h] = acc_sc[h] * alpha + jnp.dot(p, vh, preferred_element_type=F32)
                m_sc[h] = jnp.broadcast_to(m_new, (bq, 128))

        @pl.when(j < i)
        def _():
            step(False)

        @pl.when(j == i)
        def _():
            step(True)
            a0, a1 = acc_sc[0], acc_sc[1]
            l0, l1 = pltpu.roll(a0, 64, 1), pltpu.roll(a1, 64, 1)
            o_ref[...] = jnp.where(first, a0 / l0, a1 / l1).astype(BF16)
            lse_ref[...] = jnp.where(first, m_sc[0] + jnp.log(l0), m_sc[1] + jnp.log(l1))

        _plan_wait(plan, pctx, grid)

    kv = lambda hp, i, j: (jnp.minimum(i, j), hp)
    p_in, p_ospec, p_oshape, p_scr, p_alias = _plan_io(plan, 3, 2)
    return pl.pallas_call(
        body, name="flash_fwd", grid=grid,
        in_specs=[pl.BlockSpec((bq, 256), lambda hp, i, j: (i, hp)), pl.BlockSpec((bq, 256), kv),
                  pl.BlockSpec((bq, 128), kv)] + [_ANY] * len(p_in),
        out_specs=[pl.BlockSpec((bq, 128), lambda hp, i, j: (i, hp)),
                   pl.BlockSpec((bq, 128), lambda hp, i, j: (i, hp))] + p_ospec,
        out_shape=[jax.ShapeDtypeStruct((T, 512), BF16), jax.ShapeDtypeStruct((T, 512), F32)] + p_oshape,
        scratch_shapes=[pltpu.VMEM((2, bq, 128), F32), pltpu.VMEM((2, bq, 128), F32)] + p_scr,
        input_output_aliases=p_alias, compiler_params=_params(56, 3),
    )(*_hbm(q, k, v), *p_in)


def _attn_delta(dmix, o):
    T = o.shape[0]
    tm = min(ROW_BLOCK, T)
    blk = pl.BlockSpec((tm, 512), lambda i: (i, 0))

    def body(do_ref, o_ref, delta_ref, dob_ref):
        first = lax.broadcasted_iota(jnp.int32, (tm, 128), 1) < 64
        for hp in range(4):
            sl = slice(hp * 128, (hp + 1) * 128)
            prod = do_ref[:, sl] * o_ref[:, sl].astype(F32)
            d0 = jnp.sum(jnp.where(first, prod, 0.0), axis=1, keepdims=True)
            d1 = jnp.sum(jnp.where(first, 0.0, prod), axis=1, keepdims=True)
            delta_ref[:, sl] = jnp.where(first, d0, d1)
        dob_ref[...] = do_ref[...].astype(BF16)

    return pl.pallas_call(
        body, name="attn_delta", grid=(T // tm,), in_specs=[blk, blk], out_specs=[blk, blk],
        out_shape=[jax.ShapeDtypeStruct((T, 512), F32), jax.ShapeDtypeStruct((T, 512), BF16)],
        compiler_params=_params(32, 1),
    )(dmix, o)


def _flash_bwd(q, k, v, do_b, lse, delta, plan=None):
    T = q.shape[0]
    bq = min(2 * ROW_BLOCK, T)
    nq = T // bq
    grid = (4, nq, nq)

    def body(*refs):
        ((q_ref, k_ref, v_ref, do_ref, lse_ref, dl_ref), (dq_hbm, dk_ref, dv_ref), (dq_sc, dk_sc, dv_sc, sem),
         pctx) = _split_refs(refs, 6, 3, 4, plan)
        _plan_start(plan, pctx, grid)
        hp, j, i = pl.program_id(0), pl.program_id(1), pl.program_id(2)
        first = lax.broadcasted_iota(jnp.int32, (bq, 128), 1) < 64

        @pl.when((j == 0) & (i == 0))
        def _():
            dq_sc[...] = jnp.zeros_like(dq_sc)

        @pl.when(i == j)
        def _():
            dk_sc[...] = jnp.zeros_like(dk_sc)
            dv_sc[...] = jnp.zeros_like(dv_sc)

        def tile(r0, nr, nc, masked):
            rs, cs = slice(r0, r0 + nr), slice(0, nc)
            vp = v_ref[cs, :]
            do = do_ref[rs, :]
            lanes = first[rs, :]
            for h in range(2):
                sl = slice(h * 128, (h + 1) * 128)
                qh, kh = q_ref[rs, sl], k_ref[cs, sl]
                s = lax.dot_general(qh, kh, NT_DIMS, preferred_element_type=F32)
                p = jnp.exp(s - lse_ref[rs, h * 64:h * 64 + 1])
                if masked:
                    rows = r0 + lax.broadcasted_iota(jnp.int32, (nr, nc), 0)
                    cols = lax.broadcasted_iota(jnp.int32, (nr, nc), 1)
                    p = jnp.where(cols <= rows, p, 0.0)
                do_h = jnp.where(lanes if h == 0 else jnp.logical_not(lanes), do, jnp.zeros_like(do))
                dv_sc[cs, :] += lax.dot_general(p.astype(BF16), do_h, TN_DIMS, preferred_element_type=F32)
                dp = lax.dot_general(do_h, vp, NT_DIMS, preferred_element_type=F32)
                ds = (p * (dp - dl_ref[rs, h * 64:h * 64 + 1])).astype(BF16)
                dq_sc[i, rs, sl] += jnp.dot(ds, kh, preferred_element_type=F32)
                dk_sc[cs, sl] += lax.dot_general(ds, qh, TN_DIMS, preferred_element_type=F32)

        @pl.when(i > j)
        def _():
            tile(0, bq, bq, False)

        @pl.when(i == j)
        def _():
            tile(0, bq // 2, bq // 2, True)
            tile(bq // 2, bq // 2, bq, True)

        @pl.when(i == nq - 1)
        def _():
            dk_ref[...] = dk_sc[...]
            dv_ref[...] = dv_sc[...]

        @pl.when((j == nq - 1) & (i == nq - 1))
        def _():
            cp = pltpu.make_async_copy(dq_sc, dq_hbm.at[hp], sem)
            cp.start()
            cp.wait()

        _plan_wait(plan, pctx, grid)

    qi = lambda hp, j, i: (jnp.maximum(i, j), hp)
    kj = lambda hp, j, i: (j, hp)
    p_in, p_ospec, p_oshape, p_scr, p_alias = _plan_io(plan, 6, 3)
    return pl.pallas_call(
        body, name="flash_bwd", grid=grid,
        in_specs=[pl.BlockSpec((bq, 256), qi), pl.BlockSpec((bq, 256), kj), pl.BlockSpec((bq, 128), kj),
                  pl.BlockSpec((bq, 128), qi), pl.BlockSpec((bq, 128), qi), pl.BlockSpec((bq, 128), qi)]
        + [_ANY] * len(p_in),
        out_specs=[_ANY, pl.BlockSpec((bq, 256), kj), pl.BlockSpec((bq, 128), kj)] + p_ospec,
        out_shape=[jax.ShapeDtypeStruct((4, nq, bq, 256), F32), jax.ShapeDtypeStruct((T, 1024), F32),
                   jax.ShapeDtypeStruct((T, 512), F32)] + p_oshape,
        scratch_shapes=[pltpu.VMEM((nq, bq, 256), F32), pltpu.VMEM((bq, 256), F32), pltpu.VMEM((bq, 128), F32),
                        pltpu.SemaphoreType.DMA] + p_scr,
        input_output_aliases=p_alias, compiler_params=_params(56, 3),
    )(*_hbm(q, k, v, do_b, lse, delta), *p_in)


def _mla_bwd(z0, dq4, dk, dv, gq, gkv, wq, wk, wv, rc, rs1, rs2, plan=None):
    T = z0.shape[0]
    tm = min(ROW_BLOCK, T)
    HW = HEADS * 128
    grid = (T // tm,)
    dq4 = dq4.reshape(4, T, 256)

    def body(*refs):
        ((cq_ref, ckv_ref, dq_ref, dk_ref, dv_ref, gq_ref, gkv_ref, wq_ref, wk_ref, wv_ref, c_ref, s1_ref, s2_ref),
         (dc_ref, dkr_ref, dwq_ref, dwk_ref, dwv_ref, dgq_ref, dgkv_ref), _, pctx) = _split_refs(refs, 13, 7, 0, plan)
        _plan_start(plan, pctx, grid)

        @pl.when(pl.program_id(0) == 0)
        def _():
            for ref in (dwq_ref, dwk_ref, dwv_ref, dgq_ref, dgkv_ref):
                ref[...] = jnp.zeros_like(ref)

        c, s1, s2 = c_ref[...], s1_ref[...], s2_ref[...]
        lane = lax.broadcasted_iota(jnp.int32, (tm, 128), 1)
        nq, xq, rq = _rms(cq_ref[...], gq_ref[...])
        nkv, xkv, rkv = _rms(ckv_ref[...], gkv_ref[...])
        nq_b, nkv_b = nq.astype(BF16), nkv.astype(BF16)

        dq_parts, dk_parts = [], []
        dkr = jnp.zeros((tm, 128), F32)
        for h in range(HEADS):
            blk = dq_ref[h // 2, :, (h % 2) * 128:(h % 2 + 1) * 128] * QK_SCALE
            dq_parts.append(_rope_t(blk, c, s1, s2).astype(BF16))
            kb = dk_ref[:, h * 128:(h + 1) * 128]
            dk_parts.append(jnp.where(lane < NOPE, kb, 0.0).astype(BF16))
            dkr = dkr + kb
        dq_b = jnp.concatenate(dq_parts, axis=1)
        dk_b = jnp.concatenate(dk_parts, axis=1)
        dv_b = dv_ref[...].astype(BF16)

        dwq_ref[...] += lax.dot_general(nq_b, dq_b, TN_DIMS, preferred_element_type=F32)
        dwk_ref[...] += lax.dot_general(nkv_b, dk_b, TN_DIMS, preferred_element_type=F32)
        dwv_ref[...] += lax.dot_general(nkv_b, dv_b, TN_DIMS, preferred_element_type=F32)
        dnq = lax.dot_general(dq_b, wq_ref[...], NT_DIMS, preferred_element_type=F32)
        dnkv = (lax.dot_general(dk_b, wk_ref[...], NT_DIMS, preferred_element_type=F32)
                + lax.dot_general(dv_b, wv_ref[...], NT_DIMS, preferred_element_type=F32))

        def rms_bwd(dn, xhat, rstd, g):
            dxh = dn * g
            return rstd * (dxh - xhat * jnp.mean(dxh * xhat, -1, keepdims=True))

        dc_ref[:, :256] = rms_bwd(dnq, xq, rq, gq_ref[...]).astype(BF16)
        dc_ref[:, 256:] = rms_bwd(dnkv, xkv, rkv, gkv_ref[...]).astype(BF16)
        dgq_ref[...] += _fold8(dnq * xq)
        dgkv_ref[...] += _fold8(dnkv * xkv)
        dkr = pltpu.roll(_rope_t(dkr, c, s1, s2), 64, 1)
        dkr_ref[...] = jnp.where(lane < ROPE, dkr, 0.0).astype(BF16)
        _plan_wait(plan, pctx, grid)

    full = lambda shape: pl.BlockSpec(shape, lambda i: (0,) * len(shape))
    tab = pl.BlockSpec((tm, 128), lambda i: (i, 0))
    p_in, p_ospec, p_oshape, p_scr, p_alias = _plan_io(plan, 13, 7)
    return pl.pallas_call(
        body, name="mla_bwd", grid=grid,
        in_specs=[pl.BlockSpec((tm, 256), lambda i: (i, 0)), pl.BlockSpec((tm, 256), lambda i: (i, 1)),
                  pl.BlockSpec((4, tm, 256), lambda i: (0, i, 0)),
                  pl.BlockSpec((tm, HW), lambda i: (i, 0)), pl.BlockSpec((tm, 512), lambda i: (i, 0)),
                  full((1, 256)), full((1, 256)), full((256, HW)), full((256, HW)), full((256, 512)), tab, tab, tab]
        + [_ANY] * len(p_in),
        out_specs=[pl.BlockSpec((tm, 512), lambda i: (i, 0)), tab, full((256, HW)), full((256, HW)),
                   full((256, 512)), full((8, 256)), full((8, 256))] + p_ospec,
        out_shape=[jax.ShapeDtypeStruct((T, 512), BF16), jax.ShapeDtypeStruct((T, 128), BF16),
                   jax.ShapeDtypeStruct((256, HW), F32), jax.ShapeDtypeStruct((256, HW), F32),
                   jax.ShapeDtypeStruct((256, 512), F32), jax.ShapeDtypeStruct((8, 256), F32),
                   jax.ShapeDtypeStruct((8, 256), F32)] + p_oshape,
        scratch_shapes=p_scr, input_output_aliases=p_alias, compiler_params=_params(48, 1),
    )(*_hbm(z0, z0, dq4, dk, dv, gq, gkv, wq, wk, wv, rc, rs1, rs2), *p_in)


def _sgu_fwd(z0, a_out, ln_g, ln_b, w, b_t):
    T = z0.shape[0]
    tm = min(ROW_BLOCK, T)
    W = SGU_G * SGU_C

    def body(u_ref, v_ref, a_ref, g_ref, b_ref, w_ref, bt_ref, o_ref):
        o_ref[:, :W] = a_ref[...]
        ug = _gelu(u_ref[...])
        xhat, _ = _ln_stats(_gelu(v_ref[...]))
        vn = (xhat * g_ref[...] + b_ref[...]).astype(BF16)
        tril = lax.broadcasted_iota(jnp.int32, (SGU_C, SGU_C), 0) >= lax.broadcasted_iota(jnp.int32, (SGU_C, SGU_C), 1)
        for g in range(SGU_G):
            cs = slice(g * SGU_C, (g + 1) * SGU_C)
            wg = jnp.where(tril, w_ref[g], 0.0).astype(BF16)
            bcol = bt_ref[:, g:g + 1]
            for c in range(tm // SGU_C):
                rs = slice(c * SGU_C, (c + 1) * SGU_C)
                mixed = jnp.dot(wg, vn[rs, cs], preferred_element_type=F32) + bcol
                o_ref[rs, W + g * SGU_C:W + (g + 1) * SGU_C] = (ug[rs, cs] * mixed).astype(BF16)

    full = lambda shape: pl.BlockSpec(shape, lambda i: (0,) * len(shape))
    return pl.pallas_call(
        body, name="sgu_fwd", grid=(T // tm,),
        in_specs=[pl.BlockSpec((tm, W), lambda i: (i, 1)), pl.BlockSpec((tm, W), lambda i: (i, 2)),
                  pl.BlockSpec((tm, W), lambda i: (i, 0)),
                  full((1, W)), full((1, W)), full((SGU_G, SGU_C, SGU_C)), full((SGU_C, SGU_G))],
        out_specs=pl.BlockSpec((tm, 2 * W), lambda i: (i, 0)),
        out_shape=jax.ShapeDtypeStruct((T, 2 * W), BF16),
        compiler_params=_params(32, 1),
    )(z0, z0, a_out, ln_g, ln_b, w, b_t)


def _sgu_bwd(z0, dmix, dc, dkr, ln_g, ln_b, w, b_t, plan=None):
    T = z0.shape[0]
    tm = min(ROW_BLOCK, T)
    W = SGU_G * SGU_C
    grid = (T // tm,)

    def body(*refs):
        ((u_ref, v_ref, do_ref, dc_ref, dkr_ref, g_ref, b_ref, w_ref, bt_ref),
         (dz_ref, dw_ref, db_ref, dlg_ref, dlb_ref), _, pctx) = _split_refs(refs, 9, 5, 0, plan)
        _plan_start(plan, pctx, grid)

        @pl.when(pl.program_id(0) == 0)
        def _():
            for ref in (dw_ref, db_ref, dlg_ref, dlb_ref):
                ref[...] = jnp.zeros_like(ref)

        dz_ref[:, :W] = dc_ref[...]
        dz_ref[:, 3 * W:] = dkr_ref[...]

        u, v, dout = u_ref[...], v_ref[...], do_ref[...]
        ug = _gelu(u)
        xhat, rstd = _ln_stats(_gelu(v))
        vn = (xhat * g_ref[...] + b_ref[...]).astype(BF16)
        dmixed = dout * ug
        dmixed_b = dmixed.astype(BF16)
        tril = lax.broadcasted_iota(jnp.int32, (SGU_C, SGU_C), 0) >= lax.broadcasted_iota(jnp.int32, (SGU_C, SGU_C), 1)
        lane = lax.broadcasted_iota(jnp.int32, (SGU_C, SGU_C), 1)
        dvn_cols = []
        for g in range(SGU_G):
            cs = slice(g * SGU_C, (g + 1) * SGU_C)
            wg = jnp.where(tril, w_ref[g], 0.0).astype(BF16)
            bcol = bt_ref[:, g:g + 1]
            dw_g = jnp.zeros((SGU_C, SGU_C), F32)
            db_g = jnp.zeros((SGU_C, 1), F32)
            dvn_rows = []
            for c in range(tm // SGU_C):
                rs = slice(c * SGU_C, (c + 1) * SGU_C)
                mixed = jnp.dot(wg, vn[rs, cs], preferred_element_type=F32) + bcol
                dz_ref[rs, W + g * SGU_C:W + (g + 1) * SGU_C] = (dout[rs, cs] * mixed * _gelu_grad(u[rs, cs])).astype(BF16)
                dm = dmixed_b[rs, cs]
                dw_g = dw_g + lax.dot_general(dm, vn[rs, cs], NT_DIMS, preferred_element_type=F32)
                db_g = db_g + jnp.sum(dmixed[rs, cs], axis=1, keepdims=True)
                dvn_rows.append(lax.dot_general(wg, dm, TN_DIMS, preferred_element_type=F32))
            dw_ref[g] += jnp.where(tril, dw_g, 0.0)
            db_ref[...] += jnp.where(lane == g, db_g, 0.0)
            dvn_cols.append(jnp.concatenate(dvn_rows, axis=0))
        dvn = jnp.concatenate(dvn_cols, axis=1)
        dxh = dvn * g_ref[...]
        m1 = jnp.mean(dxh, -1, keepdims=True)
        m2 = jnp.mean(dxh * xhat, -1, keepdims=True)
        dvg = rstd * (dxh - m1 - xhat * m2)
        dz_ref[:, 2 * W:3 * W] = (dvg * _gelu_grad(v)).astype(BF16)
        dlg_ref[...] += _fold8(dvn * xhat)
        dlb_ref[...] += _fold8(dvn)
        _plan_wait(plan, pctx, grid)

    full = lambda shape: pl.BlockSpec(shape, lambda i: (0,) * len(shape))
    p_in, p_ospec, p_oshape, p_scr, p_alias = _plan_io(plan, 9, 5)
    return pl.pallas_call(
        body, name="sgu_bwd", grid=grid,
        in_specs=[pl.BlockSpec((tm, W), lambda i: (i, 1)), pl.BlockSpec((tm, W), lambda i: (i, 2)),
                  pl.BlockSpec((tm, W), lambda i: (i, 1)), pl.BlockSpec((tm, W), lambda i: (i, 0)),
                  pl.BlockSpec((tm, 128), lambda i: (i, 0)),
                  full((1, W)), full((1, W)), full((SGU_G, SGU_C, SGU_C)), full((SGU_C, SGU_G))] + [_ANY] * len(p_in),
        out_specs=[pl.BlockSpec((tm, 3 * W + 128), lambda i: (i, 0)), full((SGU_G, SGU_C, SGU_C)),
                   full((SGU_C, SGU_C)), full((8, W)), full((8, W))] + p_ospec,
        out_shape=[jax.ShapeDtypeStruct((T, 3 * W + 128), BF16), jax.ShapeDtypeStruct((SGU_G, SGU_C, SGU_C), F32),
                   jax.ShapeDtypeStruct((SGU_C, SGU_C), F32), jax.ShapeDtypeStruct((8, W), F32),
                   jax.ShapeDtypeStruct((8, W), F32)] + p_oshape,
        scratch_shapes=p_scr, input_output_aliases=p_alias, compiler_params=_params(40, 1),
    )(z0, z0, dmix, dc, dkr, ln_g, ln_b, w, b_t, *p_in)


def _hg_lower_bound(lb_ref):
    a0, a1 = lb_ref[0:1, :], lb_ref[1:2, :]
    m = jnp.maximum(a0, a1)
    e0, e1 = jnp.exp(a0 - m), jnp.exp(a1 - m)
    return e1 / (e0 + e1)


def _running_sum(x, reverse=False):
    n = x.shape[0]
    row = lax.broadcasted_iota(jnp.int32, x.shape, 0)
    s = 1
    while s < n:
        if reverse:
            x = x + jnp.where(row < n - s, pltpu.roll(x, n - s, 0), 0.0)
        else:
            x = x + jnp.where(row >= s, pltpu.roll(x, s, 0), 0.0)
        s *= 2
    return x


def _hg_chunk(qc, fc, lb):
    C = HG_CHUNK
    rows = lax.broadcasted_iota(jnp.int32, (C, C), 0)
    cols = lax.broadcasted_iota(jnp.int32, (C, C), 1)
    rowid = lax.broadcasted_iota(jnp.int32, (C, 128), 0)
    sq, sg = _sigmoid(qc), _sigmoid(fc)
    qf = qc * sq
    gate = lb + (1.0 - lb) * sg
    kk = 1.0 - gate
    lg = jnp.log(gate)
    bcum = _running_sum(lg)
    b_mid = jnp.sum(jnp.where(rowid < C // 2, lg, 0.0), axis=0, keepdims=True)
    b_last = jnp.sum(lg, axis=0, keepdims=True)
    eq, ek, e, eh = jnp.exp(bcum - b_mid), jnp.exp(b_mid - bcum), jnp.exp(bcum), jnp.exp(b_last - bcum)
    qt, kt, qe, khat = qf * eq, kk * ek, qf * e, kk * eh
    a = lax.dot_general(qt.astype(BF16), kt.astype(BF16), NT_DIMS, preferred_element_type=F32)
    a = jnp.where(rows >= cols, a, 0.0)
    return dict(sq=sq, sg=sg, gate=gate, kk=kk, eq=eq, ek=ek, e=e, eh=eh, qt=qt, kt=kt, qe=qe, khat=khat, a=a,
                e_last=jnp.exp(b_last), tril=rows >= cols, rowid=rowid)


def _hgrn_fwd(z4, hg_lb, gnorm):
    T = z4.shape[1]
    tb = min(ROW_BLOCK, T)
    C = HG_CHUNK
    ncb = tb // C
    HPB = HG_HEADS_PER_STEP

    def body(q_ref, f_ref, i_ref, g_ref, lb_ref, gn_ref, y_ref, o_ref, st_ref, st_sc):
        @pl.when(pl.program_id(1) == 0)
        def _():
            st_sc[...] = jnp.zeros_like(st_sc)

        def chunk(c, carry):
            rs = pl.ds(pl.multiple_of(c * C, C), C)
            for hh in range(HPB):
                hs = slice(hh * 128, (hh + 1) * 128)
                lb = _hg_lower_bound(lb_ref.at[:, hs])
                v_b = i_ref[rs, hs].astype(BF16)
                gc = g_ref[rs, hs]
                x = _hg_chunk(q_ref[rs, hs], f_ref[rs, hs], lb)
                st = st_sc[hh]
                st_ref[hh, c] = st
                o = (jnp.dot(x["a"].astype(BF16), v_b, preferred_element_type=F32)
                     + lax.dot_general(x["qe"].astype(BF16), st.astype(BF16), NT_DIMS, preferred_element_type=F32))
                st_sc[hh] = st * x["e_last"] + lax.dot_general(v_b, x["khat"].astype(BF16), TN_DIMS,
                                                               preferred_element_type=F32)
                o_ref[rs, hs] = o
                n = o * lax.rsqrt(jnp.mean(o * o, -1, keepdims=True) + EPS)
                y_ref[rs, hs] = (n * gn_ref[:, hs] * (gc * _sigmoid(gc))).astype(BF16)
            return carry

        lax.fori_loop(0, ncb, chunk, 0, unroll=4)

    W = 128 * HPB
    zb = lambda k: pl.BlockSpec((None, tb, W), lambda h, t: (k, t, h))
    out = pl.BlockSpec((tb, W), lambda h, t: (t, h))
    return pl.pallas_call(
        body, name="hgrn_fwd", grid=(HEADS // HPB, T // tb),
        in_specs=[zb(0), zb(1), zb(2), zb(3), pl.BlockSpec((2, W), lambda h, t: (0, h)),
                  pl.BlockSpec((1, W), lambda h, t: (0, h))],
        out_specs=[out, out, pl.BlockSpec((HPB, ncb, 128, 128), lambda h, t: (h, t, 0, 0))],
        out_shape=[jax.ShapeDtypeStruct((T, D), BF16), jax.ShapeDtypeStruct((T, D), F32),
                   jax.ShapeDtypeStruct((HEADS, T // C, 128, 128), F32)],
        scratch_shapes=[pltpu.VMEM((HPB, 128, 128), F32)],
        compiler_params=_params(48, 2),
    )(*_hbm(z4, z4, z4, z4, hg_lb, gnorm))


def _hgrn_bwd(z4, o_raw, dy, states, hg_lb, gnorm):
    T = z4.shape[1]
    tb = min(ROW_BLOCK, T)
    C = HG_CHUNK
    ncb = tb // C
    nt = T // tb
    HPB = HG_HEADS_PER_STEP

    def body(q_ref, f_ref, i_ref, g_ref, o_ref, dy_ref, st_ref, lb_ref, gn_ref, dz_ref, dlb_ref, dgn_ref, dst_sc):
        @pl.when(pl.program_id(1) == 0)
        def _():
            dst_sc[...] = jnp.zeros_like(dst_sc)
            dlb_ref[...] = jnp.zeros_like(dlb_ref)
            dgn_ref[...] = jnp.zeros_like(dgn_ref)

        def chunk(cc, carry):
            for hh in range(HPB):
                one_head(ncb - 1 - cc, hh, slice(hh * 128, (hh + 1) * 128))
            return carry

        def one_head(c, hh, hs):
            rs = pl.ds(pl.multiple_of(c * C, C), C)
            lb = _hg_lower_bound(lb_ref.at[:, hs])
            gn = gn_ref[:, hs]
            qc, gc = q_ref[rs, hs], g_ref[rs, hs]
            v_b = i_ref[rs, hs].astype(BF16)
            x = _hg_chunk(qc, f_ref[rs, hs], lb)
            st, dst = st_ref[hh, c], dst_sc[hh]
            st_b, dst_b = st.astype(BF16), dst.astype(BF16)
            o, dyc = o_ref[rs, hs], dy_ref[rs, hs]
            sgg = _sigmoid(gc)
            sil = gc * sgg
            rstd = lax.rsqrt(jnp.mean(o * o, -1, keepdims=True) + EPS)
            n = o * rstd
            dgn_ref[:, hs] += _fold8(dyc * n * sil)
            dn = dyc * gn * sil
            do = rstd * (dn - n * jnp.mean(dn * n, -1, keepdims=True))
            dg = dyc * n * gn * (sgg * (1.0 + gc * (1.0 - sgg)))
            do_b = do.astype(BF16)
            da = jnp.where(x["tril"], lax.dot_general(do_b, v_b, NT_DIMS, preferred_element_type=F32), 0.0).astype(BF16)
            qt_b, kt_b, qe_b, khat_b = (x[n_].astype(BF16) for n_ in ("qt", "kt", "qe", "khat"))
            dv = (lax.dot_general(x["a"].astype(BF16), do_b, TN_DIMS, preferred_element_type=F32)
                  + lax.dot_general(khat_b, dst_b, NT_DIMS, preferred_element_type=F32))
            dqt = jnp.dot(da, kt_b, preferred_element_type=F32)
            dqe = jnp.dot(do_b, st_b, preferred_element_type=F32)
            dkt = lax.dot_general(da, qt_b, TN_DIMS, preferred_element_type=F32)
            dkhat = jnp.dot(v_b, dst_b, preferred_element_type=F32)
            dst_sc[hh] = lax.dot_general(do_b, qe_b, TN_DIMS, preferred_element_type=F32) + dst * x["e_last"]
            de_last = jnp.sum(st * dst, axis=0, keepdims=True)
            dqf = dqt * x["eq"] + dqe * x["e"]
            dkk = dkt * x["ek"] + dkhat * x["eh"]
            dkh_kh = dkhat * x["khat"]
            db = dqt * qt_b.astype(F32) - dkt * kt_b.astype(F32) + dqe * x["qe"] - dkh_kh
            db_last = jnp.sum(dkh_kh, axis=0, keepdims=True) + de_last * x["e_last"]
            db = db + jnp.where(x["rowid"] == C - 1, db_last, 0.0)
            dlg = _running_sum(db, reverse=True)
            dgate = dlg / x["gate"] - dkk
            sg, sq = x["sg"], x["sq"]
            dlb_ref[:, hs] += _fold8(dgate * (1.0 - sg)) * (lb * (1.0 - lb))
            dz_ref[0, rs, hs] = (dqf * (sq * (1.0 + qc * (1.0 - sq)))).astype(BF16)
            dz_ref[1, rs, hs] = (dgate * (1.0 - lb) * sg * (1.0 - sg)).astype(BF16)
            dz_ref[2, rs, hs] = dv.astype(BF16)
            dz_ref[3, rs, hs] = dg.astype(BF16)

        lax.fori_loop(0, ncb, chunk, 0, unroll=4)

    W = 128 * HPB
    zb = lambda k: pl.BlockSpec((None, tb, W), lambda h, t: (k, nt - 1 - t, h))
    blk = pl.BlockSpec((tb, W), lambda h, t: (nt - 1 - t, h))
    acc = pl.BlockSpec((8, W), lambda h, t: (0, h))
    return pl.pallas_call(
        body, name="hgrn_bwd", grid=(HEADS // HPB, nt),
        in_specs=[zb(0), zb(1), zb(2), zb(3), blk, blk,
                  pl.BlockSpec((HPB, ncb, 128, 128), lambda h, t: (h, nt - 1 - t, 0, 0)),
                  pl.BlockSpec((2, W), lambda h, t: (0, h)), pl.BlockSpec((1, W), lambda h, t: (0, h))],
        out_specs=[pl.BlockSpec((4, tb, W), lambda h, t: (0, nt - 1 - t, h)), acc, acc],
        out_shape=[jax.ShapeDtypeStruct((4, T, D), BF16), jax.ShapeDtypeStruct((8, D), F32),
                   jax.ShapeDtypeStruct((8, D), F32)],
        scratch_shapes=[pltpu.VMEM((HPB, 128, 128), F32)],
        compiler_params=_params(48, 2),
    )(*_hbm(z4, z4, z4, z4, o_raw, dy, states, hg_lb, gnorm))


def _adamw(w, g, m, v, *, name):
    R, L = w.shape
    tr = R if R <= 512 else 512
    assert R % tr == 0
    blk = pl.BlockSpec((tr, L), lambda i: (i, 0))
    c1, c2 = 1.0 - B1 ** STEP, 1.0 - B2 ** STEP

    def body(w_ref, g_ref, m_ref, v_ref, d_ref, mo_ref, vo_ref):
        g_ = g_ref[...]
        m_ = B1 * m_ref[...] + (1.0 - B1) * g_
        v_ = B2 * v_ref[...] + (1.0 - B2) * (g_ * g_)
        d_ref[...] = -LR * ((m_ / c1) / (jnp.sqrt(v_ / c2) + ADAM_EPS) + WD * w_ref[...])
        mo_ref[...] = m_
        vo_ref[...] = v_

    sds = jax.ShapeDtypeStruct((R, L), F32)
    return pl.pallas_call(
        body, name=name, grid=(R // tr,), in_specs=[blk] * 4, out_specs=[blk] * 3, out_shape=[sds] * 3,
        compiler_params=_params(32, 1),
    )(w, g, m, v)


def _adamw_rows(w, m, v, gbufs, row0, *, name, plan=None):
    L, R, C = w.shape
    tr = 256
    assert R % tr == 0 and row0 % tr == 0 and len(gbufs) == L
    grid = (L, R // tr)
    blk = pl.BlockSpec((None, tr, C), lambda l, i: (l, i, 0))
    gblks = [pl.BlockSpec((tr, C), lambda l, i, k=k: (row0 // tr + jnp.where(l == k, i, 0), 0)) for k in range(L)]
    c1, c2 = 1.0 - B1 ** STEP, 1.0 - B2 ** STEP

    def body(*refs):
        ins, (go_ref, d_ref, mo_ref, vo_ref), _, pctx = _split_refs(refs, 3 + L, 4, 0, plan)
        w_ref, m_ref, v_ref = ins[:3]
        g_refs = ins[3:]
        _plan_start(plan, pctx, grid)
        g_ = g_refs[0][...]
        for l in range(1, L):
            g_ = jnp.where(pl.program_id(0) == l, g_refs[l][...], g_)
        m_ = B1 * m_ref[...] + (1.0 - B1) * g_
        v_ = B2 * v_ref[...] + (1.0 - B2) * (g_ * g_)
        go_ref[...] = g_
        d_ref[...] = -LR * ((m_ / c1) / (jnp.sqrt(v_ / c2) + ADAM_EPS) + WD * w_ref[...])
        mo_ref[...] = m_
        vo_ref[...] = v_
        _plan_wait(plan, pctx, grid)

    sds = jax.ShapeDtypeStruct((L, R, C), F32)
    p_in, p_ospec, p_oshape, p_scr, p_alias = _plan_io(plan, 3 + L, 4)
    return pl.pallas_call(
        body, name=name, grid=grid, in_specs=[blk] * 3 + gblks + [_ANY] * len(p_in),
        out_specs=[blk] * 4 + p_ospec, out_shape=[sds] * 4 + p_oshape, scratch_shapes=p_scr,
        input_output_aliases=p_alias, compiler_params=_params(32, 2),
    )(*_hbm(w, m, v, *gbufs), *p_in)


def _add_pairs(g, theirs, ids, *, name):
    n, R, L = theirs.shape
    tr = math.gcd(R, 128)
    nb = R // tr

    def body(ids_ref, a_ref, b_ref, o_ref):
        o_ref[...] = (a_ref[...].astype(F32) + b_ref[...].astype(F32)).astype(BF16)

    blk = pl.BlockSpec((n, tr, L), lambda i, ids: (0, i, 0))
    return pl.pallas_call(
        body, name=name, out_shape=jax.ShapeDtypeStruct((n, R, L), BF16),
        grid_spec=pltpu.PrefetchScalarGridSpec(
            num_scalar_prefetch=1, grid=(nb,),
            in_specs=[pl.BlockSpec((n, tr, L), lambda i, ids: (0, ids[1] * nb + i, 0)), blk], out_specs=blk),
        compiler_params=_params(16, 1),
    )(ids, g, theirs)


def _sum_chips(pair, parts, ids, *, name):
    _, R, L = parts.shape
    tr = math.gcd(R, 128)

    def body(ids_ref, o_ref, r_ref, out_ref):
        out_ref[...] = ((o_ref[...].astype(F32) + r_ref[0].astype(F32)) + r_ref[1].astype(F32)) + r_ref[2].astype(F32)

    return pl.pallas_call(
        body, name=name, out_shape=jax.ShapeDtypeStruct((2, R, L), F32),
        grid_spec=pltpu.PrefetchScalarGridSpec(
            num_scalar_prefetch=1, grid=(R // tr,),
            in_specs=[pl.BlockSpec((None, tr, L), lambda i, ids: (ids[0], i, 0)),
                      pl.BlockSpec((3, tr, L), lambda i, ids: (0, i, 0))],
            out_specs=pl.BlockSpec((None, tr, L), lambda i, ids: (ids[1], i, 0))),
        compiler_params=_params(32, 1),
    )(ids, pair, parts)


def _mesh_ids():
    x, y, c = _mesh_pos()
    return jnp.stack([2 * x + y, c]).astype(jnp.int32)


def _place_shard(rows, ids, *, name):
    R, L = rows.shape
    tr = 128

    def body(ids_ref, in_ref, out_ref):
        out_ref[...] = in_ref[...].astype(BF16)

    return pl.pallas_call(
        body, name=name, out_shape=jax.ShapeDtypeStruct((4, R, L), BF16),
        grid_spec=pltpu.PrefetchScalarGridSpec(
            num_scalar_prefetch=1, grid=(R // tr,), in_specs=[pl.BlockSpec((tr, L), lambda i, ids: (i, 0))],
            out_specs=pl.BlockSpec((None, tr, L), lambda i, ids: (ids[0], i, 0))),
        compiler_params=_params(16, 1),
    )(ids, rows)


def _place_weights(pieces, buffer_rows, ids, *, plan=None):
    tr = 256
    steps, s = [], 0
    for arr, layer, buf, row0 in pieces:
        nblk = arr.shape[1] // tr
        steps.append((s, nblk))
        s += nblk
    total = s
    buf_start = [min(st for (st, _), p in zip(steps, pieces) if p[2] == k) for k in range(len(buffer_rows))]
    grid = (total,)
    n_in = len(pieces)

    def body(*refs):
        ins, outs, _, pctx = _split_refs(refs[1:], n_in, len(buffer_rows), 0, plan)
        _plan_start(plan, pctx, grid)
        i = pl.program_id(0)
        for (st, nblk), (_, _, buf, _), ref in zip(steps, pieces, ins):
            @pl.when((i >= st) & (i < st + nblk))
            def _(ref=ref, buf=buf):
                outs[buf][...] = ref[...].astype(BF16)
        _plan_wait(plan, pctx, grid)

    in_specs = [pl.BlockSpec((None, tr, D), lambda i, ids, layer=layer, st=st, nblk=nblk:
                             (layer, jnp.clip(i - st, 0, nblk - 1), 0))
                for (st, nblk), (_, layer, _, _) in zip(steps, pieces)]
    out_specs = [pl.BlockSpec((None, tr, D), lambda i, ids, st=st, nb=rows // tr: (ids[0], jnp.clip(i - st, 0, nb - 1), 0))
                 for st, rows in zip(buf_start, buffer_rows)]
    p_in, p_ospec, p_oshape, p_scr, p_alias = _plan_io(plan, 1 + n_in, len(buffer_rows))
    return pl.pallas_call(
        body, name="place_weights",
        out_shape=[jax.ShapeDtypeStruct((4, rows, D), BF16) for rows in buffer_rows] + p_oshape,
        grid_spec=pltpu.PrefetchScalarGridSpec(
            num_scalar_prefetch=1, grid=grid, in_specs=in_specs + [_ANY] * len(p_in), out_specs=out_specs + p_ospec,
            scratch_shapes=p_scr),
        input_output_aliases=p_alias, compiler_params=_params(16, 1),
    )(ids, *[p[0] for p in pieces], *p_in)


def _remote(src, dst, send_sem, recv_sem, to):
    return pltpu.make_async_remote_copy(src_ref=src, dst_ref=dst, send_sem=send_sem, recv_sem=recv_sem,
                                        device_id=to, device_id_type=MESH_IDS)


def _rows(ref, lead, start, size):
    return ref.at[tuple(pl.ds(0, n) for n in ref.shape[:lead]) + (pl.ds(start, size),)]


def _other_chips():
    x, y, _ = _mesh_pos()
    return [(1 - x, y), (x, 1 - y), (1 - x, 1 - y)]


def _plan_gather_ici(bufs):
    n = len(bufs)

    def copies(outs, send, recv):
        x, y, c = _mesh_pos()
        res = []
        for b in range(n):
            half = bufs[b].shape[1] // 2
            mine = _rows(outs[b].at[2 * x + y], 0, c * half, half)
            for j, (cx, cy) in enumerate(_other_chips()):
                res.append((_remote(mine, mine, send(3 * b + j), recv(3 * b + j), (cx, cy, c)),
                            _remote(mine, _rows(outs[b].at[2 * cx + cy], 0, c * half, half),
                                    send(3 * b + j), recv(3 * b + j), (x, y, c))))
        return res

    def start(ins, outs, send, recv, loc):
        for out_cp, _ in copies(outs, send, recv):
            out_cp.start()

    def wait(ins, outs, send, recv, loc):
        for out_cp, in_cp in copies(outs, send, recv):
            in_cp.wait_recv()
            out_cp.wait_send()

    outs = [jax.ShapeDtypeStruct(b.shape, b.dtype) for b in bufs]
    return _Plan(bufs, outs, 3 * n, 0, start, wait, aliases={b: b for b in range(n)})


def _plan_gather_forward(bufs):
    n = len(bufs)

    def copies(outs, send, recv):
        x, y, c = _mesh_pos()
        res = []
        for b in range(n):
            half = bufs[b].shape[1] // 2
            for j, (cx, cy) in enumerate(_other_chips()):
                slot = outs[b].at[2 * cx + cy]
                res.append((_remote(_rows(slot, 0, c * half, half), _rows(slot, 0, c * half, half),
                                    send(3 * b + j), recv(3 * b + j), (x, y, 1 - c)),
                            _remote(_rows(slot, 0, c * half, half), _rows(slot, 0, (1 - c) * half, half),
                                    send(3 * b + j), recv(3 * b + j), (x, y, c))))
        return res

    def start(ins, outs, send, recv, loc):
        for out_cp, _ in copies(outs, send, recv):
            out_cp.start()

    def wait(ins, outs, send, recv, loc):
        for out_cp, in_cp in copies(outs, send, recv):
            in_cp.wait_recv()
            out_cp.wait_send()

    outs = [jax.ShapeDtypeStruct(b.shape, b.dtype) for b in bufs]
    return _Plan(bufs, outs, 3 * n, 0, start, wait, aliases={b: b for b in range(n)})


def _plan_pair_swap(g):
    half = g.shape[1] // 2

    def copy(ins, outs, send, recv, loc):
        x, y, c = _mesh_pos()
        return _remote(_rows(ins[0], 1, (1 - c) * half, half), outs[0], send(0), recv(0), (x, y, 1 - c))

    return _Plan([g], [jax.ShapeDtypeStruct((4, half, g.shape[2]), g.dtype)], 1, 0,
                 lambda *a: copy(*a).start(), lambda *a: copy(*a).wait())


def _plan_pair_gather(buf):
    def copies(ins, outs, send, recv, loc):
        x, y, c = _mesh_pos()
        return (_remote(outs[0].at[c], outs[0].at[c], send(0), recv(0), (x, y, 1 - c)),
                _remote(outs[0].at[c], outs[0].at[1 - c], send(0), recv(0), (x, y, c)))

    def wait(*a):
        out_cp, in_cp = copies(*a)
        in_cp.wait_recv()
        out_cp.wait_send()

    return _Plan([buf], [jax.ShapeDtypeStruct(buf.shape, buf.dtype)], 1, 0, lambda *a: copies(*a)[0].start(), wait,
                 aliases={0: 0})


def _plan_chip_scatter(p):
    def copies(ins, outs, send, recv, loc):
        _, _, c = _mesh_pos()
        return [_remote(ins[0].at[2 * cx + cy], outs[0].at[j], send(j), recv(j), (cx, cy, c))
                for j, (cx, cy) in enumerate(_other_chips())]

    def start(*a):
        for cp in copies(*a):
            cp.start()

    def wait(*a):
        for cp in copies(*a):
            cp.wait()

    return _Plan([p], [jax.ShapeDtypeStruct((3,) + p.shape[1:], p.dtype)], 3, 0, start, wait)


def _plan_exchange_all(vec):
    def copies(ins, outs, send, recv, loc):
        x, y, c = _mesh_pos()
        return [_remote(ins[0], outs[0].at[r - 1], send(r - 1), recv(r - 1), (x ^ (r >> 2), y ^ ((r >> 1) & 1), c ^ (r & 1)))
                for r in range(1, 8)]

    def start(*a):
        for cp in copies(*a):
            cp.start()

    def wait(*a):
        for cp in copies(*a):
            cp.wait()

    return _Plan([vec], [jax.ShapeDtypeStruct((7,) + vec.shape, vec.dtype)], 7, 0, start, wait)


SMALL_LAYOUT = {
    "mla_gq": (0, 1, 256, (1, 256)), "mla_gkv": (1, 1, 256, (1, 256)), "sgu_ln_g": (2, 1, 512, (1, 512)),
    "sgu_ln_b": (3, 1, 512, (1, 512)), "sgu_w": (4, 64, 1024, (64, 1024)), "sgu_b": (68, 1, 512, (1, 512)),
    "hg_lb": (69, 2, 1024, (2, 1024)), "hg_gnorm": (71, 1, 1024, (1, 256)), "ln1_g": (72, 2, 1024, (2, 1024)),
    "ln1_b": (74, 2, 1024, (2, 1024)), "ln2_g": (76, 2, 1024, (2, 1024)), "ln2_b": (78, 2, 1024, (2, 1024)),
}


def _small_pack(dgq, dgkv, dslg, dslb, dsw, dsb, dlb, dgn, ln_parts, sq_err):
    flat_ln = [p for pair in ln_parts for p in pair]

    def body(*refs):
        gq_ref, gkv_ref, slg_ref, slb_ref, sw_ref, sb_ref, lb_ref, gn_ref = refs[:8]
        ln_refs, err_ref, out_ref, t_sc = refs[8:16], refs[16], refs[17], refs[18]
        s8 = lambda ref: jnp.sum(ref[...], axis=0, keepdims=True)
        out_ref[...] = jnp.zeros_like(out_ref)
        out_ref[0:1, 0:256] = s8(gq_ref)
        out_ref[1:2, 0:256] = s8(gkv_ref)
        out_ref[2:3, 0:512] = s8(slg_ref)
        out_ref[3:4, 0:512] = s8(slb_ref)
        out_ref[4:68, :] = sw_ref[...]
        t_sc[...] = sb_ref[...].T
        for g in range(SGU_G):
            out_ref[68:69, g * SGU_C:(g + 1) * SGU_C] = t_sc[g:g + 1, :]
        d_lb1 = s8(lb_ref)
        out_ref[69:70, :] = -d_lb1
        out_ref[70:71, :] = d_lb1
        out_ref[71:72, :] = s8(gn_ref)
        for k, ref in enumerate(ln_refs):
            out_ref[72 + k:73 + k, :] = s8(ref)
        out_ref[0:1, 1023:1024] = jnp.sum(s8(err_ref), axis=1, keepdims=True) * (0.5 / D)

    vm = pl.BlockSpec(memory_space=pltpu.VMEM)
    return pl.pallas_call(
        body, name="small_grad_pack", in_specs=[vm] * 17, out_specs=vm,
        out_shape=jax.ShapeDtypeStruct((SMALL_ROWS, 1024), F32), scratch_shapes=[pltpu.VMEM((SGU_C, SGU_C), F32)],
        compiler_params=_params(16),
    )(dgq, dgkv, dslg, dslb, dsw.reshape(64, 1024), dsb, dlb, dgn, *flat_ln, sq_err)


def _small_update(vec, others, ids, w, m, v):
    names = list(SMALL_LAYOUT)
    n = len(names)
    c1, c2 = 1.0 - B1 ** STEP, 1.0 - B2 ** STEP
    have_others = others is not None

    def body(*refs):
        ids_ref, v_ref = refs[0], refs[1]
        k = 2 + have_others
        w_refs, m_refs, v_refs = refs[k:k + n], refs[k + n:k + 2 * n], refs[k + 2 * n:k + 3 * n]
        outs = refs[k + 3 * n:]
        row0_ref, tot_sc = outs[0], outs[-1]
        total = v_ref[...]
        if have_others:
            me = 2 * ids_ref[0] + ids_ref[1]
            total = None
            for d in range(8):
                rel = d ^ me
                term = jnp.where(rel == 0, v_ref[...], refs[2][jnp.maximum(rel - 1, 0)])
                total = term if total is None else total + term
        tot_sc[...] = total
        row0_ref[...] = tot_sc[0:1, :]
        for i, name in enumerate(names):
            r0, nr, width, _ = SMALL_LAYOUT[name]
            if name == "hg_gnorm":
                g_ = tot_sc[r0:r0 + 1, 0:256]
                for chip in range(1, 4):
                    g_ = jnp.where(ids_ref[0] == chip, tot_sc[r0:r0 + 1, chip * 256:(chip + 1) * 256], g_)
            else:
                g_ = tot_sc[r0:r0 + nr, 0:width]
            m_ = B1 * m_refs[i][...] + (1.0 - B1) * g_
            v_ = B2 * v_refs[i][...] + (1.0 - B2) * (g_ * g_)
            go, do, mo, vo = outs[1 + 4 * i:5 + 4 * i]
            go[...] = g_
            do[...] = -LR * ((m_ / c1) / (jnp.sqrt(v_ / c2) + ADAM_EPS) + WD * w_refs[i][...])
            mo[...] = m_
            vo[...] = v_

    full = lambda shape: pl.BlockSpec(shape, lambda i, ids, nd=len(shape): (0,) * nd)
    kshapes = [SMALL_LAYOUT[name][3] for name in names]
    operands = [vec] + ([others] if have_others else []) + [d[name] for d in (w, m, v) for name in names]
    out_shapes = [jax.ShapeDtypeStruct((1, 1024), F32)] + [jax.ShapeDtypeStruct(s, F32) for s in kshapes for _ in range(4)]
    res = pl.pallas_call(
        body, name="small_update", out_shape=out_shapes,
        grid_spec=pltpu.PrefetchScalarGridSpec(
            num_scalar_prefetch=1, grid=(1,), in_specs=[full(o.shape) for o in operands],
            out_specs=[full(s.shape) for s in out_shapes],
            scratch_shapes=[pltpu.VMEM((SMALL_ROWS, 1024), F32)]),
        compiler_params=_params(32, 1),
    )(ids, *operands)
    return res[0], {name: tuple(res[1 + 4 * i:5 + 4 * i]) for i, name in enumerate(names)}


ROWS_L1, ROWS_L0, ROWS_ODD_W = 3328, 2048, 384
ODD_PARTS = (("w_out_e", (256, 1024)), ("w_in_e", (1024, 392)), ("w_qb", (256, 192)), ("w_kvb", (256, 256)))
ODD_W_PARTS = tuple(p for p in ODD_PARTS if p[0] != "w_in_e")


def _odd_rows(parts, dtype, layout, total, gnorm=None):
    rows = [parts[n].reshape(-1, 1024).astype(dtype) for n, _ in layout]
    used = sum(r.shape[0] for r in rows)
    if gnorm is not None:
        bits = lax.bitcast_convert_type(gnorm.reshape(-1), BF16).reshape(1, 512)
        rows.append(jnp.pad(bits, ((0, 15), (0, 512))))
        used += 16
    if total > used:
        rows.append(jnp.zeros((total - used, 1024), dtype))
    return jnp.concatenate(rows, axis=0)


def _odd_unrows(buf, layout, with_gnorm=False):
    out, off = {}, 0
    for n, shape in layout:
        nr = math.prod(shape) // 1024
        out[n] = buf[off:off + nr].reshape(shape)
        off += nr
    if with_gnorm:
        out["hg_gnorm"] = lax.bitcast_convert_type(buf[off, :512].reshape(256, 2), F32).reshape(1, 256)
    return out


def _rope_tables(positions):
    half = ROPE // 2
    inv_freq = ROPE_BASE ** (-jnp.arange(half, dtype=F32) / half)
    ang = positions.astype(F32).reshape(-1, 1) * inv_freq
    cos, sin = jnp.cos(ang), jnp.sin(ang)
    T = ang.shape[0]
    one, z16, z32 = jnp.ones((T, NOPE), F32), jnp.zeros((T, half), F32), jnp.zeros((T, 32), F32)
    z64 = jnp.zeros((T, NOPE), F32)
    c = jnp.concatenate([one, cos, cos, z32], axis=1)
    s1 = jnp.concatenate([z64, -sin, z16, z32], axis=1)
    s2 = jnp.concatenate([z64, z16, sin, z32], axis=1)
    return c, s1, s2


def _local_step(x, positions, tgt, odd, bufs, P, exchange):
    T = x.shape[0]
    row = lambda a: a.reshape(1, -1)
    rc, rs1, rs2 = _rope_tables(positions)
    blk = lambda f: pl.BlockSpec((None, D, D), f)

    w_in_e = odd["w_in_e"]
    w_in = jnp.concatenate([w_in_e[:, :512], w_in_e[:, 544:1568], w_in_e[:, 512:544], jnp.zeros((D, 96), BF16)], axis=1)
    wq = jnp.pad(odd["w_qb"].reshape(256, HEADS, NOPE + ROPE), ((0, 0), (0, 0), (0, 32))).reshape(256, HEADS * 128)
    kvb = odd["w_kvb"].reshape(256, HEADS, NOPE + VDIM)
    wk = jnp.pad(kvb[:, :, :NOPE], ((0, 0), (0, 0), (0, 64))).reshape(256, HEADS * 128)
    wv = kvb[:, :, NOPE:].reshape(256, HEADS * VDIM)
    w_out_e = odd["w_out_e"]
    sgu_w = P["sgu_w"][0]
    sgu_bt = P["sgu_b"][0].T
    gq, gkv = P["mla_gq"], P["mla_gkv"]
    gnorm = P["hg_gnorm"]

    z0 = _matmul(x, w_in, name="in_proj_e", M=T, N=1664, K=D, tn=1664)[0]
    q, k, v = _mla_prep(z0, gq, gkv, wq, wk, wv, rc, rs1, rs2)
    if exchange:
        ids = _mesh_ids()
        placed = list(bufs)
        a_out, lse, wga, wgb = _flash_fwd(q, k, v, plan=_plan_gather_ici(placed[:2]))
    else:
        a_out, lse = _flash_fwd(q, k, v)
        wga, wgb, wgc = bufs
    mix0 = _sgu_fwd(z0, a_out, P["sgu_ln_g"], P["sgu_ln_b"], sgu_w, sgu_bt)
    res = _proj_ln(mix0, w_out_e, x, row(P["ln1_g"][0]), row(P["ln1_b"][0]), name="out_proj_ln_e",
                   plan=_plan_gather_forward([wga, wgb]) if exchange else None)
    r1, h1b = res[:2]
    if exchange:
        wga, wgb = res[2:]
    ln = lambda name, l: (row(P[name + "_g"][l]), row(P[name + "_b"][l]))
    res = _ffn_ln(h1b, wga, r1, *ln("ln2", 0), name="ffn_ln_0", prev_ln=ln("ln1", 0),
                  plan=_plan_gather_ici(placed[2:]) if exchange else None)
    ra0, r2, h2b = res[:3]
    z4 = _matmul(h2b, wgb, name="in_proj_o", M=T, N=4 * D, K=D, tn=2 * D, n_slots=True,
                 b_spec=pl.BlockSpec((2, D, D), lambda i, j, k: (j, 0, 0)),
                 out_shape=jax.ShapeDtypeStruct((4, T, D), F32),
                 o_spec=pl.BlockSpec((2, min(MM_ROWS, T), D), lambda i, j, k: (j, i, 0)))[0]
    y1, o_raw, states = _hgrn_fwd(z4, P["hg_lb"], gnorm)
    res2 = _proj_ln(y1, wgb, r2, *ln("ln1", 1), name="out_proj_ln_o", prev_ln=ln("ln2", 0), w_rowblk=4,
                    plan=_plan_gather_forward([res[3]]) if exchange else None)
    r3, h3b = res2[:2]
    if exchange:
        wgc = res2[2]
    ra1, r4, _ = _ffn_ln(h3b, wgc, r3, *ln("ln2", 1), name="ffn_ln_1", prev_ln=ln("ln1", 1))

    ln1_g, ln1_b, ln2_g, ln2_b = [None, None], [None, None], [None, None], [None, None]
    sq_err_parts = []

    def ffn_bwd(l, dh, r_out, ra, h_mid_b, g2, wg, rows, plan=None, loss_head=()):
        dr, dr_b, dg, db, *sq_err = _ln_bwd(dh, r_out, row(g2), name=f"ln2_bwd_{l}", loss_head=loss_head)
        sq_err_parts.extend(sq_err)
        ln2_g[l], ln2_b[l] = dg, db
        da, *extra = _matmul(dr_b, wg, tb=True, mul=ra, out_dtype=BF16, name=f"ffn_da_{l}", M=T, N=4 * D, K=D, tn=2 * D,
                             b_spec=pl.BlockSpec((2, D, D), lambda i, j, k: (j, 1, 0)), n_slots=True, plan=plan)
        gbuf = _matmul(ra, dr_b, ta=True, a_sq=True, name=f"ffn_dw2_{l}", M=4 * D, N=D, K=T, tm=1024, tk=DW_TOKENS // 2,
                       out_shape=jax.ShapeDtypeStruct((4, rows, D), BF16), o_spec=blk(lambda i, j, k: (i, 1, 0)))[0]
        gbuf = _matmul(h_mid_b, da, ta=True, name=f"ffn_dw1_{l}", M=D, N=4 * D, K=T, tm=1024, tk=DW_TOKENS, into=gbuf,
                       out_shape=jax.ShapeDtypeStruct((4, rows, D), BF16), o_spec=blk(lambda i, j, k: (j, 0, 0)))[0]
        dh_mid = _matmul(da, wg, tb=True, add=dr, add_scale=ALPHA, name=f"ffn_dh_{l}", M=T, N=D, K=4 * D, tk=2 * D,
                         b_spec=pl.BlockSpec((2, D, D), lambda i, j, k: (k, 0, 0)))[0]
        return dh_mid, gbuf, extra

    dh3, g1, _ = ffn_bwd(1, None, r4, ra1, h3b, P["ln2_g"][1], wgc, ROWS_L1, loss_head=(row(P["ln2_b"][1]), tgt))
    loss_parts = sq_err_parts[0]
    dr3, dr3_b, dg, db = _ln_bwd(dh3, r3, row(P["ln1_g"][1]), name="ln1_bwd_1")
    ln1_g[1], ln1_b[1] = dg, db
    g1_sds = jax.ShapeDtypeStruct((4, ROWS_L1, D), BF16)
    g1 = _matmul(y1, dr3_b, ta=True, name="dw_out_o", M=D, N=D, K=T, tm=256, tk=DW_TOKENS, into=g1, out_shape=g1_sds,
                 o_spec=pl.BlockSpec((None, 256, D), lambda i, j, k: (i, 12, 0)))[0]
    dmix1 = _matmul(dr3_b, wgb, tb=True, name="dmix_o", M=T, N=D, K=D, b_spec=_rows4_spec(4, 3), b_merge=(D, D))[0]
    dz4, dlb, dgn = _hgrn_bwd(z4, o_raw, dmix1, states, P["hg_lb"], gnorm)
    g1 = _matmul(h2b, dz4, ta=True, name="dw_in_o", M=D, N=4 * D, K=T, tm=1024, tk=DW_TOKENS, into=g1, out_shape=g1_sds,
                 b_spec=pl.BlockSpec((None, min(DW_TOKENS, T), D), lambda i, j, k: (j, k, 0)),
                 o_spec=blk(lambda i, j, k: (j, 2, 0)))[0]
    dh2 = _matmul(dz4, wgb, tb=True, add=dr3, add_scale=ALPHA, name="dh_in_o", M=T, N=D, K=4 * D, tk=2 * D,
                  a_spec=pl.BlockSpec((2, min(MM_ROWS, T), D), lambda i, j, k: (k, i, 0)),
                  b_spec=pl.BlockSpec((2, D, D), lambda i, j, k: (k, 0, 0)))[0]

    dh1, g0, swapped1 = ffn_bwd(0, dh2, r2, ra0, h1b, P["ln2_g"][0], wga, ROWS_L0,
                                plan=_plan_pair_swap(g1) if exchange else None)
    dr1, dr1_b, dg, db = _ln_bwd(dh1, r1, row(P["ln1_g"][0]), name="ln1_bwd_0")
    ln1_g[0], ln1_b[0] = dg, db
    godd = {"w_out_e": _matmul(mix0, dr1_b, ta=True, name="dw_out_e", M=D, N=D, K=T, tm=1024, tk=DW_TOKENS)[0]}
    dmix0, *swapped0 = _matmul(dr1_b, w_out_e, tb=True, name="dmix_e", M=T, N=D, K=D,
                               plan=_plan_pair_swap(g0) if exchange else None)
    delta, do_b = _attn_delta(dmix0, a_out)
    if exchange:
        pair1 = _add_pairs(g1, swapped1[0], ids, name="grad_pair_add_1")
        pair0 = _add_pairs(g0, swapped0[0], ids, name="grad_pair_add_0")
        dq4, dk, dv, parts0, parts1 = _flash_bwd(
            q, k, v, do_b, lse, delta, plan=_join_plans([_plan_chip_scatter(pair0), _plan_chip_scatter(pair1)]))
        half0 = _sum_chips(pair0, parts0, ids, name="grad_chip_sum_0")
        half1 = _sum_chips(pair1, parts1, ids, name="grad_chip_sum_1")
        dc, dkr, dwq, dwk, dwv, dgq, dgkv, g0, g1 = _mla_bwd(
            z0, dq4, dk, dv, gq, gkv, wq, wk, wv, rc, rs1, rs2,
            plan=_join_plans([_plan_pair_gather(half0), _plan_pair_gather(half1)]))
        g0, g1 = g0.reshape(ROWS_L0, D), g1.reshape(ROWS_L1, D)
    else:
        dq4, dk, dv = _flash_bwd(q, k, v, do_b, lse, delta)
        dc, dkr, dwq, dwk, dwv, dgq, dgkv = _mla_bwd(z0, dq4, dk, dv, gq, gkv, wq, wk, wv, rc, rs1, rs2)
    godd["w_qb"] = dwq.reshape(256, HEADS, 128)[:, :, :NOPE + ROPE].reshape(256, HEADS * (NOPE + ROPE))
    godd["w_kvb"] = jnp.concatenate([dwk.reshape(256, HEADS, 128)[:, :, :NOPE], dwv.reshape(256, HEADS, VDIM)],
                                    axis=2).reshape(256, HEADS * (NOPE + VDIM))
    swap_b = None
    if exchange:
        by_chip = [_odd_rows({"w_out_e": jnp.split(godd["w_out_e"], 4, axis=0)[j],
                              **{n: jnp.split(godd[n], 4, axis=1)[j] for n in ("w_qb", "w_kvb")}}, BF16,
                             ODD_W_PARTS, ROWS_ODD_W)
                   for j in range(4)]
        odd_b = jnp.stack(by_chip)
        swap_b = _plan_pair_swap(odd_b)
    dz0, dsw, dsb, dslg, dslb, *theirs_b = _sgu_bwd(z0, dmix0, dc, dkr, P["sgu_ln_g"], P["sgu_ln_b"], sgu_w, sgu_bt,
                                                    plan=swap_b)
    small_vec = _small_pack(dgq, dgkv, dslg, dslb, dsw, dsb, dlb, dgn, [ln1_g, ln1_b, ln2_g, ln2_b], loss_parts)
    plan_in = None
    if exchange:
        pair_b = _add_pairs(odd_b, theirs_b[0], ids, name="odd_pair_add_1")
        plan_in = _join_plans([_plan_exchange_all(small_vec), _plan_chip_scatter(pair_b)])
    dw_in, *carried = _matmul(x, dz0, ta=True, name="dw_in_e", M=D, N=1664, K=T, tm=1024, tn=1664, tk=DW_TOKENS // 4,
                              plan=plan_in)
    godd["w_in_e"] = jnp.concatenate([dw_in[:, :512], dw_in[:, 1536:1568], dw_in[:, 512:1536]], axis=1)
    plan_x = None
    if exchange:
        small_others, parts_b = carried
        odd_a = godd["w_in_e"].reshape(D, 4, 392).transpose(1, 0, 2).astype(BF16)
        theirs_a = _run_plan(_plan_pair_swap(odd_a), name="odd_pair_swap")[0]
        pair_a = _add_pairs(odd_a, theirs_a, ids, name="odd_pair_add_0")
        plan_x = _plan_chip_scatter(pair_a)
    grad_x, *parts_a = _matmul(dz0, w_in, tb=True, add=dr1, add_scale=ALPHA, name="dx", M=T, N=D, K=1664, tk=1664,
                               plan=plan_x)
    if exchange:
        godd = ([pair_a, pair_b], [parts_a[0], parts_b])
        return grad_x, g0, g1, godd, small_vec, small_others
    return grad_x, g0, g1, godd, small_vec, None


WEIGHTS = ['w_in_e', 'mla_gq', 'mla_gkv', 'w_qb', 'w_kvb', 'sgu_ln_g', 'sgu_ln_b', 'sgu_w', 'sgu_b', 'w_out_e',
           'w_in_o', 'hg_lb', 'hg_gnorm', 'w_out_o', 'ln1_g', 'ln1_b', 'w_ff1', 'w_ff2', 'ln2_g', 'ln2_b']


def kernel(x, positions, w_in_e, mla_gq, mla_gkv, w_qb, w_kvb, sgu_ln_g, sgu_ln_b, sgu_w, sgu_b, w_out_e, w_in_o, hg_lb, hg_gnorm, w_out_o, ln1_g, ln1_b, w_ff1, w_ff2, ln2_g, ln2_b, loss_target, m_w_in_e, m_mla_gq, m_mla_gkv, m_w_qb, m_w_kvb, m_sgu_ln_g, m_sgu_ln_b, m_sgu_w, m_sgu_b, m_w_out_e, m_w_in_o, m_hg_lb, m_hg_gnorm, m_w_out_o, m_ln1_g, m_ln1_b, m_w_ff1, m_w_ff2, m_ln2_g, m_ln2_b, v_w_in_e, v_mla_gq, v_mla_gkv, v_w_qb, v_w_kvb, v_sgu_ln_g, v_sgu_ln_b, v_sgu_w, v_sgu_b, v_w_out_e, v_w_in_o, v_hg_lb, v_hg_gnorm, v_w_out_o, v_ln1_g, v_ln1_b, v_w_ff1, v_w_ff2, v_ln2_g, v_ln2_b):
    args = dict(locals())
    w = {n: args[n] for n in WEIGHTS}
    m = {n: args["m_" + n] for n in WEIGHTS}
    v = {n: args["v_" + n] for n in WEIGHTS}
    cx, cy, cc = _mesh_pos()
    chip = 2 * cx + cy

    odd_shard = _odd_rows({"w_out_e": w_out_e[0], "w_qb": w_qb[0], "w_kvb": w_kvb[0]}, BF16, ODD_W_PARTS, ROWS_ODD_W,
                          gnorm=hg_gnorm)
    ids = _mesh_ids()
    placed = [_place_shard(w_in_e[0], ids, name="place_shard_in_e"), _place_shard(odd_shard, ids, name="place_shard_odd")]
    pieces = [(w_ff1, 0, 0, 0), (w_ff2, 0, 0, 1024), (w_in_o, 0, 1, 0), (w_out_o, 0, 1, 1024),
              (w_ff1, 1, 2, 0), (w_ff2, 1, 2, 1024)]
    *big_bufs, odd_a, odd_b = _place_weights(pieces, (2048, 1280, 2048), ids, plan=_plan_gather_ici(placed))
    gathered = _run_plan(_plan_gather_forward([odd_a, odd_b]), name="odd_gather_forward")
    per_chip = [_odd_unrows(gathered[1][j], ODD_W_PARTS, with_gnorm=True) for j in range(4)]
    odd = {"w_out_e": jnp.concatenate([p["w_out_e"] for p in per_chip], axis=0),
           "w_in_e": jnp.concatenate([gathered[0][j] for j in range(4)], axis=1)}
    for n in ("w_qb", "w_kvb"):
        odd[n] = jnp.concatenate([p[n] for p in per_chip], axis=1)
    small = {n: w[n] for n in SMALL_LAYOUT if n != "hg_gnorm"}
    small["hg_gnorm"] = jnp.concatenate([p["hg_gnorm"] for p in per_chip], axis=1)
    grad_x, g_l0, g_l1, godd, small_vec, small_others = _local_step(
        x[0], positions[0], loss_target[0], odd, big_bufs, small, True)

    sums = [_sum_chips(pair, parts, ids, name=f"odd_chip_sum_{k}") for k, (pair, parts) in enumerate(zip(*godd))]
    g_in_e, g_rest = _run_plan(_join_plans([_plan_pair_gather(s) for s in sums]), name="odd_pair_gather")
    g_odd = _odd_unrows(g_rest.reshape(ROWS_ODD_W, 1024), ODD_W_PARTS)
    g_odd["w_in_e"] = g_in_e.reshape(D, 392)

    to_kernel = lambda d: {n: d[n].reshape(SMALL_LAYOUT[n][3]) for n in SMALL_LAYOUT}
    first_row, small_out = _small_update(small_vec, small_others, ids, to_kernel(w), to_kernel(m), to_kernel(v))
    loss = first_row[0, 1023]
    grads, delta, new_m, new_v = {}, {}, {}, {}
    for n, res in small_out.items():
        grads[n], delta[n], new_m[n], new_v[n] = (r.reshape(w[n].shape) for r in res)

    for n, bufs_, row0 in (("w_ff1", [g_l0, g_l1], 0), ("w_ff2", [g_l0, g_l1], 1024), ("w_in_o", [g_l1], 2048),
                           ("w_out_o", [g_l1], 3072)):
        grads[n], delta[n], new_m[n], new_v[n] = _adamw_rows(w[n], m[n], v[n], bufs_, row0, name=f"adamw_{n}")
    for n, _ in ODD_PARTS:
        grads[n] = g_odd[n][None]
        d_, m_, v_ = _adamw(w[n][0], g_odd[n], m[n][0], v[n][0], name=f"adamw_{n}")
        delta[n], new_m[n], new_v[n] = d_[None], m_[None], v_[None]

    return (loss, grad_x[None], *[grads[n] for n in WEIGHTS], *[delta[n] for n in WEIGHTS],
            *[new_m[n] for n in WEIGHTS], *[new_v[n] for n in WEIGHTS])
```

```python
import math

import jax
import jax.numpy as jnp
from jax import lax
from jax.experimental import pallas as pl
from jax.experimental.pallas import tpu as pltpu

F32 = jnp.float32
BF16 = jnp.bfloat16
MESH_IDS = pl.DeviceIdType.MESH

D = 1024
DEPTH = 2
HEADS = 8
NOPE, ROPE, VDIM = 64, 32, 64
QK_SCALE = (NOPE + ROPE) ** -0.5
ROPE_BASE = 10000.0
SGU_G, SGU_C = 4, 128
HG_CHUNK = 64
HG_HEADS_PER_STEP = 8
ALPHA = (2 * DEPTH) ** 0.25
EPS = 1e-5
LR, B1, B2, ADAM_EPS, WD, STEP = 0.001, 0.9, 0.999, 1e-08, 0.01, 10
GELU_C = math.sqrt(2.0 / math.pi)
GELU_A = 0.044715
MB = 1024 * 1024
ROW_BLOCK = 512
SMALL_ROWS = 80

NT_DIMS = (((1,), (1,)), ((), ()))
TN_DIMS = (((0,), (0,)), ((), ()))


def _params(vmem_mb, n_axes=0):
    kw = dict(vmem_limit_bytes=vmem_mb * MB)
    if n_axes:
        kw["dimension_semantics"] = ("arbitrary",) * n_axes
    return pltpu.CompilerParams(**kw)


_ANY = pl.BlockSpec(memory_space=pltpu.HBM)


def _mesh_pos():
    return lax.axis_index("x"), lax.axis_index("y"), lax.axis_index("c")


def _hbm(*arrays):
    return tuple(pltpu.with_memory_space_constraint(a, pltpu.HBM) if a.size >= 2 ** 18 else a for a in arrays)


class _Plan:
    def __init__(self, ins, outs, n_remote, n_local, start, wait, aliases=None):
        self.ins, self.outs, self.n_remote, self.n_local = list(ins), list(outs), n_remote, n_local
        self.start, self.wait, self.aliases = start, wait, dict(aliases or {})


def _join_plans(plans):
    ins, outs, aliases, parts = [], [], {}, []
    nr = nl = 0
    for p in plans:
        parts.append((p, len(ins), len(outs), nr, nl))
        aliases.update({len(ins) + i: len(outs) + o for i, o in p.aliases.items()})
        ins += p.ins
        outs += p.outs
        nr += p.n_remote
        nl += p.n_local

    def run(which):
        def go(in_refs, out_refs, send, recv, loc):
            for p, i0, o0, r0, l0 in parts:
                getattr(p, which)(in_refs[i0:i0 + len(p.ins)], out_refs[o0:o0 + len(p.outs)],
                                  lambda i, r0=r0: send(r0 + i), lambda i, r0=r0: recv(r0 + i),
                                  lambda i, l0=l0: loc(l0 + i))
        return go

    return _Plan(ins, outs, nr, nl, run("start"), run("wait"), aliases)


def _plan_io(plan, n_in, n_out):
    if plan is None:
        return [], [], [], [], {}
    sems = [pltpu.SemaphoreType.DMA((max(plan.n_remote, 1),)), pltpu.SemaphoreType.DMA((max(plan.n_remote, 1),)),
            pltpu.SemaphoreType.DMA((max(plan.n_local, 1),))]
    aliases = {n_in + i: n_out + o for i, o in plan.aliases.items()}
    return plan.ins, [_ANY] * len(plan.outs), plan.outs, sems, aliases


def _split_refs(refs, n_in, n_out, n_scr, plan):
    p_in, p_out = (len(plan.ins), len(plan.outs)) if plan is not None else (0, 0)
    refs = list(refs)
    ins, refs = refs[:n_in], refs[n_in:]
    pins, refs = refs[:p_in], refs[p_in:]
    outs, refs = refs[:n_out], refs[n_out:]
    pouts, refs = refs[:p_out], refs[p_out:]
    scr, psem = refs[:n_scr], refs[n_scr:]
    psem = tuple((lambda i, s=s: s.at[i]) for s in psem)
    return ins, outs, scr, (pins, pouts, psem)


def _grid_edge(grid, last):
    cond = None
    for ax, n in enumerate(grid):
        c = pl.program_id(ax) == (n - 1 if last else 0)
        cond = c if cond is None else cond & c
    return cond


def _plan_start(plan, pctx, grid):
    if plan is not None:
        pins, pouts, psem = pctx
        pl.when(_grid_edge(grid, False))(lambda: plan.start(pins, pouts, *psem))


def _plan_wait(plan, pctx, grid):
    if plan is not None:
        pins, pouts, psem = pctx
        pl.when(_grid_edge(grid, True))(lambda: plan.wait(pins, pouts, *psem))


def _run_plan(plan, *, name):
    def body(*refs):
        _, _, _, (pins, pouts, psem) = _split_refs(refs, 0, 0, 0, plan)
        plan.start(pins, pouts, *psem)
        plan.wait(pins, pouts, *psem)

    p_in, p_ospec, p_oshape, p_scr, p_alias = _plan_io(plan, 0, 0)
    return pl.pallas_call(body, name=name, in_specs=[_ANY] * len(p_in), out_specs=p_ospec, out_shape=p_oshape,
                          scratch_shapes=p_scr, input_output_aliases=p_alias)(*p_in)


def _fold8(x):
    return x.reshape(x.shape[0] // 8, 8, x.shape[1]).sum(axis=0)


def _ln_stats(r):
    mu = jnp.mean(r, -1, keepdims=True)
    xc = r - mu
    rstd = lax.rsqrt(jnp.mean(xc * xc, -1, keepdims=True) + EPS)
    return xc * rstd, rstd


def _sigmoid(x):
    return jax.nn.sigmoid(x)


def _gelu(x):
    return 0.5 * x * (1.0 + jnp.tanh(GELU_C * (x + GELU_A * x * x * x)))


def _gelu_grad(x):
    t = jnp.tanh(GELU_C * (x + GELU_A * x * x * x))
    return 0.5 * (1.0 + t) + 0.5 * x * (1.0 - t * t) * GELU_C * (1.0 + 3.0 * GELU_A * x * x)


MM_ROWS = 1024
DW_TOKENS = 4096


def _matmul(a, b, *, name, M, N, K, ta=False, tb=False, out_dtype=F32, tm=MM_ROWS, tn=1024, tk=1024,
            a_spec=None, b_spec=None, b_merge=None, out_shape=None, o_spec=None, into=None,
            a_sq=False, mul=None, add=None, add_scale=1.0, n_slots=False, plan=None):
    assert not n_slots or (K // min(tk, K) == 1 and add is None)
    tm, tn, tk = min(tm, M), min(tn, N), min(tk, K)
    assert M % tm == 0 and N % tn == 0 and K % tk == 0
    grid = (M // tm, N // tn, K // tk)
    nk = grid[2]
    if a_spec is None:
        a_spec = pl.BlockSpec((tk, tm), lambda i, j, k: (k, i)) if ta else pl.BlockSpec((tm, tk), lambda i, j, k: (i, k))
    if b_spec is None:
        b_spec = pl.BlockSpec((tn, tk), lambda i, j, k: (j, k)) if tb else pl.BlockSpec((tk, tn), lambda i, j, k: (k, j))
    if o_spec is None:
        o_spec = pl.BlockSpec((tm, tn), lambda i, j, k: (i, j))
        out_shape = jax.ShapeDtypeStruct((M, N), out_dtype)
    e_spec = pl.BlockSpec((tm, tn), lambda i, j, k: (i, j))
    dims = (((0 if ta else 1,), (1 if tb else 0,)), ((), ()))
    extra = [e for e in (mul, add, into) if e is not None]
    n_in = 2 + len(extra)

    def body(*refs):
        ins, outs, scr, pctx = _split_refs(refs, n_in, 1, 1 if nk > 1 else 0, plan)
        a_ref, b_ref = ins[0], ins[1]
        rest = list(ins[2:])
        mul_ref = rest.pop(0) if mul is not None else None
        add_ref = rest.pop(0) if add is not None else None
        o_ref = outs[0]
        _plan_start(plan, pctx, grid)
        av = a_ref[...].astype(BF16)
        if a_sq:
            av = av * av
        bv = b_ref[...]
        if b_merge is not None:
            bv = bv.reshape(b_merge)
        if n_slots:
            for s in range(bv.shape[0]):
                r = lax.dot_general(av, bv[s], dims, preferred_element_type=F32)
                w = r.shape[1]
                if mul_ref is not None:
                    r = r * (2.0 * mul_ref[:, s * w:(s + 1) * w].astype(F32))
                if o_ref.ndim == 3:
                    o_ref[s] = r.astype(o_ref.dtype)
                else:
                    o_ref[:, s * w:(s + 1) * w] = r.astype(o_ref.dtype)
            _plan_wait(plan, pctx, grid)
            return
        if bv.ndim == 3:
            w = av.shape[-1] // (1 if av.ndim == 3 else bv.shape[0])
            a_parts = [av[s] if av.ndim == 3 else av[:, s * w:(s + 1) * w] for s in range(bv.shape[0])]
            p = sum(lax.dot_general(a_parts[s], bv[s], dims, preferred_element_type=F32) for s in range(bv.shape[0]))
        else:
            p = lax.dot_general(av, bv, dims, preferred_element_type=F32)

        def finish(r):
            if mul_ref is not None:
                r = r * (2.0 * mul_ref[...].astype(F32))
            if add_ref is not None:
                r = r + add_scale * add_ref[...]
            o_ref[...] = r.astype(o_ref.dtype)

        if nk == 1:
            finish(p)
        else:
            acc_ref = scr[0]
            k = pl.program_id(2)

            @pl.when(k == 0)
            def _():
                acc_ref[...] = p

            @pl.when(k > 0)
            def _():
                acc_ref[...] += p

            @pl.when(k == nk - 1)
            def _():
                finish(acc_ref[...])

        _plan_wait(plan, pctx, grid)

    p_in, p_ospec, p_oshape, p_scr, p_alias = _plan_io(plan, n_in, 1)
    aliases = dict(p_alias)
    if into is not None:
        aliases[n_in - 1] = 0
    return pl.pallas_call(
        body, name=name, grid=grid,
        in_specs=[a_spec, b_spec] + [e_spec] * (len(extra) - (into is not None)) + [_ANY] * (into is not None)
        + [_ANY] * len(p_in),
        out_specs=[o_spec] + p_ospec, out_shape=[out_shape] + p_oshape,
        scratch_shapes=([pltpu.VMEM((tm, tn), F32)] if nk > 1 else []) + p_scr,
        input_output_aliases=aliases, compiler_params=_params(48, 3),
    )(*_hbm(a, b, *extra), *p_in)


def _rows4_spec(rowblk, n_axes):
    return pl.BlockSpec((4, 256, D), lambda *_: (0, rowblk, 0))


def _residual(h_ref, prev_refs):
    if not prev_refs:
        return h_ref[...]
    xhat, _ = _ln_stats(h_ref[...])
    return xhat * prev_refs[0][...] + prev_refs[1][...]


def _proj_ln(a_b, w, h_prev, g, b, *, name, prev_ln=(), w_rowblk=None, plan=None):
    T = a_b.shape[0]
    tm = min(MM_ROWS, T)
    grid = (T // tm,)
    row = pl.BlockSpec((tm, D), lambda i: (i, 0))
    vec = pl.BlockSpec((1, D), lambda i: (0, 0))
    w_spec = pl.BlockSpec((D, D), lambda i: (0, 0)) if w_rowblk is None else _rows4_spec(w_rowblk, 1)
    n_in = 5 + len(prev_ln)

    def body(*refs):
        ins, (r_ref, hb_ref), _, pctx = _split_refs(refs, n_in, 2, 0, plan)
        a_ref, w_ref, h_ref, g_ref, b_ref = ins[:5]
        _plan_start(plan, pctx, grid)
        mix = jnp.dot(a_ref[...], w_ref[...].reshape(D, D), preferred_element_type=F32)
        r = ALPHA * _residual(h_ref, ins[5:]) + mix
        xhat, _ = _ln_stats(r)
        r_ref[...] = r
        hb_ref[...] = (xhat * g_ref[...] + b_ref[...]).astype(BF16)
        _plan_wait(plan, pctx, grid)

    p_in, p_ospec, p_oshape, p_scr, p_alias = _plan_io(plan, n_in, 2)
    return pl.pallas_call(
        body, name=name, grid=grid,
        in_specs=[row, w_spec, row, vec, vec] + [vec] * len(prev_ln) + [_ANY] * len(p_in),
        out_specs=[row, row] + p_ospec,
        out_shape=[jax.ShapeDtypeStruct((T, D), F32), jax.ShapeDtypeStruct((T, D), BF16)] + p_oshape,
        scratch_shapes=p_scr, input_output_aliases=p_alias, compiler_params=_params(40, 1),
    )(*_hbm(a_b, w, h_prev, g, b, *prev_ln), *p_in)


def _ffn_ln(h_b, wbuf, h, g, b, *, name, prev_ln=(), plan=None):
    T = h_b.shape[0]
    slots = 2
    tm, tf = min(ROW_BLOCK, T), slots * 1024
    nf = 4 // slots
    F = nf * tf
    grid = (T // tm, nf)
    row = pl.BlockSpec((tm, D), lambda i, j: (i, 0))
    vec = pl.BlockSpec((1, D), lambda i, j: (0, 0))
    n_in = 6 + len(prev_ln)

    def body(*refs):
        ins, (ra_ref, r_ref, hbo_ref), (acc_ref,), pctx = _split_refs(refs, n_in, 3, 1, plan)
        hb_ref, w1_ref, w2_ref, h_ref, g_ref, b_ref = ins[:6]
        _plan_start(plan, pctx, grid)
        j = pl.program_id(1)
        hb = hb_ref[...]
        p = None
        for s in range(slots):
            ra = jnp.maximum(jnp.dot(hb, w1_ref[s], preferred_element_type=F32), 0.0)
            ra_ref[:, s * 1024:(s + 1) * 1024] = ra.astype(BF16)
            ps = jnp.dot((ra * ra).astype(BF16), w2_ref[s], preferred_element_type=F32)
            p = ps if p is None else p + ps

        @pl.when(j == 0)
        def _():
            acc_ref[...] = p

        @pl.when(j > 0)
        def _():
            acc_ref[...] += p

        @pl.when(j == nf - 1)
        def _():
            r = ALPHA * _residual(h_ref, ins[6:]) + acc_ref[...]
            xhat, _ = _ln_stats(r)
            r_ref[...] = r
            hbo_ref[...] = (xhat * g_ref[...] + b_ref[...]).astype(BF16)

        _plan_wait(plan, pctx, grid)

    p_in, p_ospec, p_oshape, p_scr, p_alias = _plan_io(plan, n_in, 3)
    return pl.pallas_call(
        body, name=name, grid=grid,
        in_specs=[row, pl.BlockSpec((slots, D, D), lambda i, j: (j, 0, 0)),
                  pl.BlockSpec((slots, D, D), lambda i, j: (j, 1, 0)), row, vec, vec] + [vec] * len(prev_ln)
        + [_ANY] * len(p_in),
        out_specs=[pl.BlockSpec((tm, tf), lambda i, j: (i, j)), row, row] + p_ospec,
        out_shape=[jax.ShapeDtypeStruct((T, F), BF16), jax.ShapeDtypeStruct((T, D), F32),
                   jax.ShapeDtypeStruct((T, D), BF16)] + p_oshape,
        scratch_shapes=[pltpu.VMEM((tm, D), F32)] + p_scr,
        input_output_aliases=p_alias, compiler_params=_params(56, 2),
    )(*_hbm(h_b, wbuf, wbuf, h, g, b, *prev_ln), *p_in)


def _ln_bwd(dy, r, g, *, name, loss_head=()):
    T = r.shape[0]
    tm = min(ROW_BLOCK, T)
    row = pl.BlockSpec((tm, D), lambda i: (i, 0))
    vec = pl.BlockSpec((1, D), lambda i: (0, 0))
    acc = pl.BlockSpec((8, D), lambda i: (0, 0))
    operands, in_specs = ([r, g, *loss_head], [row, vec, vec, row]) if loss_head else ([r, g, dy], [row, vec, row])
    n_in = len(operands)

    def body(*refs):
        r_ref, g_ref = refs[:2]
        dr_ref, drb_ref, dg_ref, db_ref = refs[n_in:n_in + 4]

        @pl.when(pl.program_id(0) == 0)
        def _():
            for ref in refs[n_in + 2:]:
                ref[...] = jnp.zeros_like(ref)

        xhat, rstd = _ln_stats(r_ref[...])
        if loss_head:
            err = xhat * g_ref[...] + refs[2][...] - refs[3][...]
            refs[n_in + 4][...] += _fold8(err * err)
            dy_ = err * (1.0 / D)
        else:
            dy_ = refs[2][...]
        dxh = dy_ * g_ref[...]
        m1 = jnp.mean(dxh, -1, keepdims=True)
        m2 = jnp.mean(dxh * xhat, -1, keepdims=True)
        dr = rstd * (dxh - m1 - xhat * m2)
        dr_ref[...] = dr
        drb_ref[...] = dr.astype(BF16)
        dg_ref[...] += _fold8(dy_ * xhat)
        db_ref[...] += _fold8(dy_)

    n_acc = 3 if loss_head else 2
    return pl.pallas_call(
        body, name=name, grid=(T // tm,), in_specs=in_specs, out_specs=[row, row] + [acc] * n_acc,
        out_shape=[jax.ShapeDtypeStruct((T, D), F32), jax.ShapeDtypeStruct((T, D), BF16)]
        + [jax.ShapeDtypeStruct((8, D), F32)] * n_acc,
        compiler_params=_params(40, 1),
    )(*_hbm(*operands))


def _rope(x, c, s1, s2):
    return x * c + pltpu.roll(x, 112, 1) * s1 + pltpu.roll(x, 16, 1) * s2


def _rope_t(dy, c, s1, s2):
    return dy * c + pltpu.roll(dy * s1, 16, 1) + pltpu.roll(dy * s2, 112, 1)


def _rms(x, g):
    rstd = lax.rsqrt(jnp.mean(x * x, -1, keepdims=True) + EPS)
    xhat = x * rstd
    return xhat * g, xhat, rstd


def _mla_prep(z0, gq, gkv, wq, wk, wv, rc, rs1, rs2):
    T = z0.shape[0]
    tm = min(ROW_BLOCK, T)
    HW = HEADS * 128

    def body(cq_ref, ckv_ref, kr_ref, gq_ref, gkv_ref, wq_ref, wk_ref, wv_ref, c_ref, s1_ref, s2_ref,
             q_ref, k_ref, v_ref):
        nq = _rms(cq_ref[...], gq_ref[...])[0].astype(BF16)
        nkv = _rms(ckv_ref[...], gkv_ref[...])[0].astype(BF16)
        q = jnp.dot(nq, wq_ref[...], preferred_element_type=F32)
        k = jnp.dot(nkv, wk_ref[...], preferred_element_type=F32)
        v = jnp.dot(nkv, wv_ref[...], preferred_element_type=F32)
        c, s1, s2 = c_ref[...], s1_ref[...], s2_ref[...]
        kr = _rope(pltpu.roll(kr_ref[...], 64, 1), c, s1, s2)
        for h in range(HEADS):
            sl = slice(h * 128, (h + 1) * 128)
            q_ref[:, sl] = (_rope(q[:, sl], c, s1, s2) * QK_SCALE).astype(BF16)
            k_ref[:, sl] = (k[:, sl] + kr).astype(BF16)
        v_ref[...] = v.astype(BF16)

    full = lambda shape: pl.BlockSpec(shape, lambda i: (0, 0))
    tab = pl.BlockSpec((tm, 128), lambda i: (i, 0))
    return pl.pallas_call(
        body, name="mla_prep", grid=(T // tm,),
        in_specs=[pl.BlockSpec((tm, 256), lambda i: (i, 0)), pl.BlockSpec((tm, 256), lambda i: (i, 1)),
                  pl.BlockSpec((tm, 128), lambda i: (i, 12)), full((1, 256)), full((1, 256)),
                  full((256, HW)), full((256, HW)), full((256, 512)), tab, tab, tab],
        out_specs=[pl.BlockSpec((tm, HW), lambda i: (i, 0)), pl.BlockSpec((tm, HW), lambda i: (i, 0)),
                   pl.BlockSpec((tm, 512), lambda i: (i, 0))],
        out_shape=[jax.ShapeDtypeStruct((T, HW), BF16), jax.ShapeDtypeStruct((T, HW), BF16),
                   jax.ShapeDtypeStruct((T, 512), BF16)],
        compiler_params=_params(40, 1),
    )(z0, z0, z0, gq, gkv, wq, wk, wv, rc, rs1, rs2)


def _flash_fwd(q, k, v, plan=None):
    T = q.shape[0]
    bq = min(2 * ROW_BLOCK, T)
    nq = T // bq
    pairs = [(i, j) for i in range(nq) for j in range(i + 1)]
    imap, jmap = (jnp.array(m, jnp.int32) for m in zip(*pairs))
    grid = (4, len(pairs))

    def body(imap_ref, jmap_ref, *refs):
        (q_ref, k_ref, v_ref), (o_ref, lse_ref), (m_sc, acc_sc), pctx = _split_refs(refs, 3, 2, 2, plan)
        _plan_start(plan, pctx, grid)
        i, j = imap_ref[pl.program_id(1)], jmap_ref[pl.program_id(1)]
        first = lax.broadcasted_iota(jnp.int32, (bq, 128), 1) < 64

        @pl.when(j == 0)
        def _():
            m_sc[...] = jnp.full_like(m_sc, -jnp.inf)
            acc_sc[...] = jnp.zeros_like(acc_sc)

        def step(masked):
            vp = v_ref[...]
            for h in range(2):
                sl = slice(h * 128, (h + 1) * 128)
                s = lax.dot_general(q_ref[:, sl], k_ref[:, sl], NT_DIMS, preferred_element_type=F32)
                if masked:
                    rows = lax.broadcasted_iota(jnp.int32, (bq, bq), 0)
                    cols = lax.broadcasted_iota(jnp.int32, (bq, bq), 1)
                    s = jnp.where(cols <= rows, s, -jnp.inf)
                m_prev = m_sc[h, :, 0:1]
                m_new = jnp.maximum(m_prev, jnp.max(s, axis=1, keepdims=True))
                alpha = jnp.exp(m_prev - m_new)
                p = jnp.exp(s - m_new).astype(BF16)
                vh = jnp.where(first if h == 0 else jnp.logical_not(first), vp, jnp.ones_like(vp))
                acc_sc[h] = acc_sc[h] * alpha + jnp.dot(p, vh, preferred_element_type=F32)
                m_sc[h] = jnp.broadcast_to(m_new, (bq, 128))

        @pl.when(j < i)
        def _():
            step(False)

        @pl.when(j == i)
        def _():
            step(True)
            a0, a1 = acc_sc[0], acc_sc[1]
            l0, l1 = pltpu.roll(a0, 64, 1), pltpu.roll(a1, 64, 1)
            o_ref[...] = jnp.where(first, a0 / l0, a1 / l1).astype(BF16)
            lse_ref[...] = jnp.where(first, m_sc[0] + jnp.log(l0), m_sc[1] + jnp.log(l1))

        _plan_wait(plan, pctx, grid)

    qi = lambda hp, t, im, jm: (im[t], hp)
    kj = lambda hp, t, im, jm: (jm[t], hp)
    p_in, p_ospec, p_oshape, p_scr, p_alias = _plan_io(plan, 2 + 3, 2)
    return pl.pallas_call(
        body, name="flash_fwd",
        out_shape=[jax.ShapeDtypeStruct((T, 512), BF16), jax.ShapeDtypeStruct((T, 512), F32)] + p_oshape,
        grid_spec=pltpu.PrefetchScalarGridSpec(
            num_scalar_prefetch=2, grid=grid,
            in_specs=[pl.BlockSpec((bq, 256), qi), pl.BlockSpec((bq, 256), kj), pl.BlockSpec((bq, 128), kj)]
            + [_ANY] * len(p_in),
            out_specs=[pl.BlockSpec((bq, 128), qi), pl.BlockSpec((bq, 128), qi)] + p_ospec,
            scratch_shapes=[pltpu.VMEM((2, bq, 128), F32), pltpu.VMEM((2, bq, 128), F32)] + p_scr),
        input_output_aliases=p_alias, compiler_params=_params(56, 2),
    )(imap, jmap, *_hbm(q, k, v), *p_in)


def _attn_delta(dmix, o):
    T = o.shape[0]
    tm = min(ROW_BLOCK, T)
    blk = pl.BlockSpec((tm, 512), lambda i: (i, 0))

    def body(do_ref, o_ref, delta_ref, dob_ref):
        first = lax.broadcasted_iota(jnp.int32, (tm, 128), 1) < 64
        for hp in range(4):
            sl = slice(hp * 128, (hp + 1) * 128)
            prod = do_ref[:, sl] * o_ref[:, sl].astype(F32)
            d0 = jnp.sum(jnp.where(first, prod, 0.0), axis=1, keepdims=True)
            d1 = jnp.sum(jnp.where(first, 0.0, prod), axis=1, keepdims=True)
            delta_ref[:, sl] = jnp.where(first, d0, d1)
        dob_ref[...] = do_ref[...].astype(BF16)

    return pl.pallas_call(
        body, name="attn_delta", grid=(T // tm,), in_specs=[blk, blk], out_specs=[blk, blk],
        out_shape=[jax.ShapeDtypeStruct((T, 512), F32), jax.ShapeDtypeStruct((T, 512), BF16)],
        compiler_params=_params(32, 1),
    )(dmix, o)


def _flash_bwd(q, k, v, do_b, lse, delta, plan=None):
    T = q.shape[0]
    bq = min(2 * ROW_BLOCK, T)
    nq = T // bq
    pairs = [(i, j) for j in range(nq) for i in range(j, nq)]
    imap, jmap = (jnp.array(m, jnp.int32) for m in zip(*pairs))
    grid = (4, len(pairs))

    def body(imap_ref, jmap_ref, *refs):
        ((q_ref, k_ref, v_ref, do_ref, lse_ref, dl_ref), (dq_hbm, dk_ref, dv_ref), (dq_sc, dk_sc, dv_sc, sem),
         pctx) = _split_refs(refs, 6, 3, 4, plan)
        _plan_start(plan, pctx, grid)
        hp = pl.program_id(0)
        i, j = imap_ref[pl.program_id(1)], jmap_ref[pl.program_id(1)]
        first = lax.broadcasted_iota(jnp.int32, (bq, 128), 1) < 64

        @pl.when((j == 0) & (i == 0))
        def _():
            dq_sc[...] = jnp.zeros_like(dq_sc)

        @pl.when(i == j)
        def _():
            dk_sc[...] = jnp.zeros_like(dk_sc)
            dv_sc[...] = jnp.zeros_like(dv_sc)

        def tile(r0, nr, nc, masked):
            rs, cs = slice(r0, r0 + nr), slice(0, nc)
            vp = v_ref[cs, :]
            do = do_ref[rs, :]
            lanes = first[rs, :]
            for h in range(2):
                sl = slice(h * 128, (h + 1) * 128)
                qh, kh = q_ref[rs, sl], k_ref[cs, sl]
                s = lax.dot_general(qh, kh, NT_DIMS, preferred_element_type=F32)
                p = jnp.exp(s - lse_ref[rs, h * 64:h * 64 + 1])
                if masked:
                    rows = r0 + lax.broadcasted_iota(jnp.int32, (nr, nc), 0)
                    cols = lax.broadcasted_iota(jnp.int32, (nr, nc), 1)
                    p = jnp.where(cols <= rows, p, 0.0)
                do_h = jnp.where(lanes if h == 0 else jnp.logical_not(lanes), do, jnp.zeros_like(do))
                dv_sc[cs, :] += lax.dot_general(p.astype(BF16), do_h, TN_DIMS, preferred_element_type=F32)
                dp = lax.dot_general(do_h, vp, NT_DIMS, preferred_element_type=F32)
                ds = (p * (dp - dl_ref[rs, h * 64:h * 64 + 1])).astype(BF16)
                dq_sc[i, rs, sl] += jnp.dot(ds, kh, preferred_element_type=F32)
                dk_sc[cs, sl] += lax.dot_general(ds, qh, TN_DIMS, preferred_element_type=F32)

        @pl.when(i > j)
        def _():
            tile(0, bq, bq, False)

        @pl.when(i == j)
        def _():
            tile(0, bq // 2, bq // 2, True)
            tile(bq // 2, bq // 2, bq, True)

        @pl.when(i == nq - 1)
        def _():
            dk_ref[...] = dk_sc[...]
            dv_ref[...] = dv_sc[...]

        @pl.when((j == nq - 1) & (i == nq - 1))
        def _():
            cp = pltpu.make_async_copy(dq_sc, dq_hbm.at[hp], sem)
            cp.start()
            cp.wait()

        _plan_wait(plan, pctx, grid)

    qi = lambda hp, t, im, jm: (im[t], hp)
    kj = lambda hp, t, im, jm: (jm[t], hp)
    p_in, p_ospec, p_oshape, p_scr, p_alias = _plan_io(plan, 2 + 6, 3)
    return pl.pallas_call(
        body, name="flash_bwd",
        out_shape=[jax.ShapeDtypeStruct((4, nq, bq, 256), F32), jax.ShapeDtypeStruct((T, 1024), F32),
                   jax.ShapeDtypeStruct((T, 512), F32)] + p_oshape,
        grid_spec=pltpu.PrefetchScalarGridSpec(
            num_scalar_prefetch=2, grid=grid,
            in_specs=[pl.BlockSpec((bq, 256), qi), pl.BlockSpec((bq, 256), kj), pl.BlockSpec((bq, 128), kj),
                      pl.BlockSpec((bq, 128), qi), pl.BlockSpec((bq, 128), qi), pl.BlockSpec((bq, 128), qi)]
            + [_ANY] * len(p_in),
            out_specs=[_ANY, pl.BlockSpec((bq, 256), kj), pl.BlockSpec((bq, 128), kj)] + p_ospec,
            scratch_shapes=[pltpu.VMEM((nq, bq, 256), F32), pltpu.VMEM((bq, 256), F32), pltpu.VMEM((bq, 128), F32),
                            pltpu.SemaphoreType.DMA] + p_scr),
        input_output_aliases=p_alias, compiler_params=_params(56, 2),
    )(imap, jmap, *_hbm(q, k, v, do_b, lse, delta), *p_in)


def _mla_bwd(z0, dq4, dk, dv, gq, gkv, wq, wk, wv, rc, rs1, rs2, plan=None):
    T = z0.shape[0]
    tm = min(ROW_BLOCK, T)
    HW = HEADS * 128
    grid = (T // tm,)
    dq4 = dq4.reshape(4, T, 256)

    def body(*refs):
        ((cq_ref, ckv_ref, dq_ref, dk_ref, dv_ref, gq_ref, gkv_ref, wq_ref, wk_ref, wv_ref, c_ref, s1_ref, s2_ref),
         (dc_ref, dkr_ref, dwq_ref, dwk_ref, dwv_ref, dgq_ref, dgkv_ref), _, pctx) = _split_refs(refs, 13, 7, 0, plan)
        _plan_start(plan, pctx, grid)

        @pl.when(pl.program_id(0) == 0)
        def _():
            for ref in (dwq_ref, dwk_ref, dwv_ref, dgq_ref, dgkv_ref):
                ref[...] = jnp.zeros_like(ref)

        c, s1, s2 = c_ref[...], s1_ref[...], s2_ref[...]
        lane = lax.broadcasted_iota(jnp.int32, (tm, 128), 1)
        nq, xq, rq = _rms(cq_ref[...], gq_ref[...])
        nkv, xkv, rkv = _rms(ckv_ref[...], gkv_ref[...])
        nq_b, nkv_b = nq.astype(BF16), nkv.astype(BF16)

        dq_parts, dk_parts = [], []
        dkr = jnp.zeros((tm, 128), F32)
        for h in range(HEADS):
            blk = dq_ref[h // 2, :, (h % 2) * 128:(h % 2 + 1) * 128] * QK_SCALE
            dq_parts.append(_rope_t(blk, c, s1, s2).astype(BF16))
            kb = dk_ref[:, h * 128:(h + 1) * 128]
            dk_parts.append(jnp.where(lane < NOPE, kb, 0.0).astype(BF16))
            dkr = dkr + kb
        dq_b = jnp.concatenate(dq_parts, axis=1)
        dk_b = jnp.concatenate(dk_parts, axis=1)
        dv_b = dv_ref[...].astype(BF16)

        dwq_ref[...] += lax.dot_general(nq_b, dq_b, TN_DIMS, preferred_element_type=F32)
        dwk_ref[...] += lax.dot_general(nkv_b, dk_b, TN_DIMS, preferred_element_type=F32)
        dwv_ref[...] += lax.dot_general(nkv_b, dv_b, TN_DIMS, preferred_element_type=F32)
        dnq = lax.dot_general(dq_b, wq_ref[...], NT_DIMS, preferred_element_type=F32)
        dnkv = (lax.dot_general(dk_b, wk_ref[...], NT_DIMS, preferred_element_type=F32)
                + lax.dot_general(dv_b, wv_ref[...], NT_DIMS, preferred_element_type=F32))

        def rms_bwd(dn, xhat, rstd, g):
            dxh = dn * g
            return rstd * (dxh - xhat * jnp.mean(dxh * xhat, -1, keepdims=True))

        dc_ref[:, :256] = rms_bwd(dnq, xq, rq, gq_ref[...]).astype(BF16)
        dc_ref[:, 256:] = rms_bwd(dnkv, xkv, rkv, gkv_ref[...]).astype(BF16)
        dgq_ref[...] += _fold8(dnq * xq)
        dgkv_ref[...] += _fold8(dnkv * xkv)
        dkr = pltpu.roll(_rope_t(dkr, c, s1, s2), 64, 1)
        dkr_ref[...] = jnp.where(lane < ROPE, dkr, 0.0).astype(BF16)
        _plan_wait(plan, pctx, grid)

    full = lambda shape: pl.BlockSpec(shape, lambda i: (0,) * len(shape))
    tab = pl.BlockSpec((tm, 128), lambda i: (i, 0))
    p_in, p_ospec, p_oshape, p_scr, p_alias = _plan_io(plan, 13, 7)
    return pl.pallas_call(
        body, name="mla_bwd", grid=grid,
        in_specs=[pl.BlockSpec((tm, 256), lambda i: (i, 0)), pl.BlockSpec((tm, 256), lambda i: (i, 1)),
                  pl.BlockSpec((4, tm, 256), lambda i: (0, i, 0)),
                  pl.BlockSpec((tm, HW), lambda i: (i, 0)), pl.BlockSpec((tm, 512), lambda i: (i, 0)),
                  full((1, 256)), full((1, 256)), full((256, HW)), full((256, HW)), full((256, 512)), tab, tab, tab]
        + [_ANY] * len(p_in),
        out_specs=[pl.BlockSpec((tm, 512), lambda i: (i, 0)), tab, full((256, HW)), full((256, HW)),
                   full((256, 512)), full((8, 256)), full((8, 256))] + p_ospec,
        out_shape=[jax.ShapeDtypeStruct((T, 512), BF16), jax.ShapeDtypeStruct((T, 128), BF16),
                   jax.ShapeDtypeStruct((256, HW), F32), jax.ShapeDtypeStruct((256, HW), F32),
                   jax.ShapeDtypeStruct((256, 512), F32), jax.ShapeDtypeStruct((8, 256), F32),
                   jax.ShapeDtypeStruct((8, 256), F32)] + p_oshape,
        scratch_shapes=p_scr, input_output_aliases=p_alias, compiler_params=_params(48, 1),
    )(*_hbm(z0, z0, dq4, dk, dv, gq, gkv, wq, wk, wv, rc, rs1, rs2), *p_in)


def _sgu_fwd(z0, a_out, ln_g, ln_b, w, b_t):
    T = z0.shape[0]
    tm = min(ROW_BLOCK, T)
    W = SGU_G * SGU_C

    def body(u_ref, v_ref, a_ref, g_ref, b_ref, w_ref, bt_ref, o_ref):
        o_ref[:, :W] = a_ref[...]
        ug = _gelu(u_ref[...])
        xhat, _ = _ln_stats(_gelu(v_ref[...]))
        vn = (xhat * g_ref[...] + b_ref[...]).astype(BF16)
        tril = lax.broadcasted_iota(jnp.int32, (SGU_C, SGU_C), 0) >= lax.broadcasted_iota(jnp.int32, (SGU_C, SGU_C), 1)
        for g in range(SGU_G):
            cs = slice(g * SGU_C, (g + 1) * SGU_C)
            wg = jnp.where(tril, w_ref[g], 0.0).astype(BF16)
            bcol = bt_ref[:, g:g + 1]
            for c in range(tm // SGU_C):
                rs = slice(c * SGU_C, (c + 1) * SGU_C)
                mixed = jnp.dot(wg, vn[rs, cs], preferred_element_type=F32) + bcol
                o_ref[rs, W + g * SGU_C:W + (g + 1) * SGU_C] = (ug[rs, cs] * mixed).astype(BF16)

    full = lambda shape: pl.BlockSpec(shape, lambda i: (0,) * len(shape))
    return pl.pallas_call(
        body, name="sgu_fwd", grid=(T // tm,),
        in_specs=[pl.BlockSpec((tm, W), lambda i: (i, 1)), pl.BlockSpec((tm, W), lambda i: (i, 2)),
                  pl.BlockSpec((tm, W), lambda i: (i, 0)),
                  full((1, W)), full((1, W)), full((SGU_G, SGU_C, SGU_C)), full((SGU_C, SGU_G))],
        out_specs=pl.BlockSpec((tm, 2 * W), lambda i: (i, 0)),
        out_shape=jax.ShapeDtypeStruct((T, 2 * W), BF16),
        compiler_params=_params(32, 1),
    )(z0, z0, a_out, ln_g, ln_b, w, b_t)


def _sgu_bwd(z0, dmix, dc, dkr, ln_g, ln_b, w, b_t, plan=None):
    T = z0.shape[0]
    tm = min(ROW_BLOCK, T)
    W = SGU_G * SGU_C
    grid = (T // tm,)

    def body(*refs):
        ((u_ref, v_ref, do_ref, dc_ref, dkr_ref, g_ref, b_ref, w_ref, bt_ref),
         (dz_ref, dw_ref, db_ref, dlg_ref, dlb_ref), _, pctx) = _split_refs(refs, 9, 5, 0, plan)
        _plan_start(plan, pctx, grid)

        @pl.when(pl.program_id(0) == 0)
        def _():
            for ref in (dw_ref, db_ref, dlg_ref, dlb_ref):
                ref[...] = jnp.zeros_like(ref)

        dz_ref[:, :W] = dc_ref[...]
        dz_ref[:, 3 * W:] = dkr_ref[...]

        u, v, dout = u_ref[...], v_ref[...], do_ref[...]
        ug = _gelu(u)
        xhat, rstd = _ln_stats(_gelu(v))
        vn = (xhat * g_ref[...] + b_ref[...]).astype(BF16)
        dmixed = dout * ug
        dmixed_b = dmixed.astype(BF16)
        tril = lax.broadcasted_iota(jnp.int32, (SGU_C, SGU_C), 0) >= lax.broadcasted_iota(jnp.int32, (SGU_C, SGU_C), 1)
        lane = lax.broadcasted_iota(jnp.int32, (SGU_C, SGU_C), 1)
        dvn_cols = []
        for g in range(SGU_G):
            cs = slice(g * SGU_C, (g + 1) * SGU_C)
            wg = jnp.where(tril, w_ref[g], 0.0).astype(BF16)
            bcol = bt_ref[:, g:g + 1]
            dw_g = jnp.zeros((SGU_C, SGU_C), F32)
            db_g = jnp.zeros((SGU_C, 1), F32)
            dvn_rows = []
            for c in range(tm // SGU_C):
                rs = slice(c * SGU_C, (c + 1) * SGU_C)
                mixed = jnp.dot(wg, vn[rs, cs], preferred_element_type=F32) + bcol
                dz_ref[rs, W + g * SGU_C:W + (g + 1) * SGU_C] = (dout[rs, cs] * mixed * _gelu_grad(u[rs, cs])).astype(BF16)
                dm = dmixed_b[rs, cs]
                dw_g = dw_g + lax.dot_general(dm, vn[rs, cs], NT_DIMS, preferred_element_type=F32)
                db_g = db_g + jnp.sum(dmixed[rs, cs], axis=1, keepdims=True)
                dvn_rows.append(lax.dot_general(wg, dm, TN_DIMS, preferred_element_type=F32))
            dw_ref[g] += jnp.where(tril, dw_g, 0.0)
            db_ref[...] += jnp.where(lane == g, db_g, 0.0)
            dvn_cols.append(jnp.concatenate(dvn_rows, axis=0))
        dvn = jnp.concatenate(dvn_cols, axis=1)
        dxh = dvn * g_ref[...]
        m1 = jnp.mean(dxh, -1, keepdims=True)
        m2 = jnp.mean(dxh * xhat, -1, keepdims=True)
        dvg = rstd * (dxh - m1 - xhat * m2)
        dz_ref[:, 2 * W:3 * W] = (dvg * _gelu_grad(v)).astype(BF16)
        dlg_ref[...] += _fold8(dvn * xhat)
        dlb_ref[...] += _fold8(dvn)
        _plan_wait(plan, pctx, grid)

    full = lambda shape: pl.BlockSpec(shape, lambda i: (0,) * len(shape))
    p_in, p_ospec, p_oshape, p_scr, p_alias = _plan_io(plan, 9, 5)
    return pl.pallas_call(
        body, name="sgu_bwd", grid=grid,
        in_specs=[pl.BlockSpec((tm, W), lambda i: (i, 1)), pl.BlockSpec((tm, W), lambda i: (i, 2)),
                  pl.BlockSpec((tm, W), lambda i: (i, 1)), pl.BlockSpec((tm, W), lambda i: (i, 0)),
                  pl.BlockSpec((tm, 128), lambda i: (i, 0)),
                  full((1, W)), full((1, W)), full((SGU_G, SGU_C, SGU_C)), full((SGU_C, SGU_G))] + [_ANY] * len(p_in),
        out_specs=[pl.BlockSpec((tm, 3 * W + 128), lambda i: (i, 0)), full((SGU_G, SGU_C, SGU_C)),
                   full((SGU_C, SGU_C)), full((8, W)), full((8, W))] + p_ospec,
        out_shape=[jax.ShapeDtypeStruct((T, 3 * W + 128), BF16), jax.ShapeDtypeStruct((SGU_G, SGU_C, SGU_C), F32),
                   jax.ShapeDtypeStruct((SGU_C, SGU_C), F32), jax.ShapeDtypeStruct((8, W), F32),
                   jax.ShapeDtypeStruct((8, W), F32)] + p_oshape,
        scratch_shapes=p_scr, input_output_aliases=p_alias, compiler_params=_params(40, 1),
    )(z0, z0, dmix, dc, dkr, ln_g, ln_b, w, b_t, *p_in)


def _hg_lower_bound(lb_ref):
    a0, a1 = lb_ref[0:1, :], lb_ref[1:2, :]
    m = jnp.maximum(a0, a1)
    e0, e1 = jnp.exp(a0 - m), jnp.exp(a1 - m)
    return e1 / (e0 + e1)


def _running_sum(x, reverse=False):
    n = x.shape[0]
    row = lax.broadcasted_iota(jnp.int32, x.shape, 0)
    s = 1
    while s < n:
        if reverse:
            x = x + jnp.where(row < n - s, pltpu.roll(x, n - s, 0), 0.0)
        else:
            x = x + jnp.where(row >= s, pltpu.roll(x, s, 0), 0.0)
        s *= 2
    return x


def _hg_chunk(qc, fc, lb):
    C = HG_CHUNK
    rows = lax.broadcasted_iota(jnp.int32, (C, C), 0)
    cols = lax.broadcasted_iota(jnp.int32, (C, C), 1)
    rowid = lax.broadcasted_iota(jnp.int32, (C, 128), 0)
    sq, sg = _sigmoid(qc), _sigmoid(fc)
    qf = qc * sq
    gate = lb + (1.0 - lb) * sg
    kk = 1.0 - gate
    lg = jnp.log(gate)
    bcum = _running_sum(lg)
    b_mid = jnp.sum(jnp.where(rowid < C // 2, lg, 0.0), axis=0, keepdims=True)
    b_last = jnp.sum(lg, axis=0, keepdims=True)
    eq, ek, e, eh = jnp.exp(bcum - b_mid), jnp.exp(b_mid - bcum), jnp.exp(bcum), jnp.exp(b_last - bcum)
    qt, kt, qe, khat = qf * eq, kk * ek, qf * e, kk * eh
    a = lax.dot_general(qt.astype(BF16), kt.astype(BF16), NT_DIMS, preferred_element_type=F32)
    a = jnp.where(rows >= cols, a, 0.0)
    return dict(sq=sq, sg=sg, gate=gate, kk=kk, eq=eq, ek=ek, e=e, eh=eh, qt=qt, kt=kt, qe=qe, khat=khat, a=a,
                e_last=jnp.exp(b_last), tril=rows >= cols, rowid=rowid)


def _hgrn_fwd(z4, hg_lb, gnorm):
    T = z4.shape[1]
    tb = min(ROW_BLOCK, T)
    C = HG_CHUNK
    ncb = tb // C
    HPB = HG_HEADS_PER_STEP

    def body(q_ref, f_ref, i_ref, g_ref, lb_ref, gn_ref, y_ref, o_ref, st_ref, st_sc):
        @pl.when(pl.program_id(1) == 0)
        def _():
            st_sc[...] = jnp.zeros_like(st_sc)

        def chunk(c, carry):
            rs = pl.ds(pl.multiple_of(c * C, C), C)
            for hh in range(HPB):
                hs = slice(hh * 128, (hh + 1) * 128)
                lb = _hg_lower_bound(lb_ref.at[:, hs])
                v_b = i_ref[rs, hs].astype(BF16)
                gc = g_ref[rs, hs]
                x = _hg_chunk(q_ref[rs, hs], f_ref[rs, hs], lb)
                st = st_sc[hh]
                st_ref[hh, c] = st
                o = (jnp.dot(x["a"].astype(BF16), v_b, preferred_element_type=F32)
                     + lax.dot_general(x["qe"].astype(BF16), st.astype(BF16), NT_DIMS, preferred_element_type=F32))
                st_sc[hh] = st * x["e_last"] + lax.dot_general(v_b, x["khat"].astype(BF16), TN_DIMS,
                                                               preferred_element_type=F32)
                o_ref[rs, hs] = o
                n = o * lax.rsqrt(jnp.mean(o * o, -1, keepdims=True) + EPS)
                y_ref[rs, hs] = (n * gn_ref[:, hs] * (gc * _sigmoid(gc))).astype(BF16)
            return carry

        lax.fori_loop(0, ncb, chunk, 0, unroll=4)

    W = 128 * HPB
    zb = lambda k: pl.BlockSpec((None, tb, W), lambda h, t: (k, t, h))
    out = pl.BlockSpec((tb, W), lambda h, t: (t, h))
    return pl.pallas_call(
        body, name="hgrn_fwd", grid=(HEADS // HPB, T // tb),
        in_specs=[zb(0), zb(1), zb(2), zb(3), pl.BlockSpec((2, W), lambda h, t: (0, h)),
                  pl.BlockSpec((1, W), lambda h, t: (0, h))],
        out_specs=[out, out, pl.BlockSpec((HPB, ncb, 128, 128), lambda h, t: (h, t, 0, 0))],
        out_shape=[jax.ShapeDtypeStruct((T, D), BF16), jax.ShapeDtypeStruct((T, D), F32),
                   jax.ShapeDtypeStruct((HEADS, T // C, 128, 128), F32)],
        scratch_shapes=[pltpu.VMEM((HPB, 128, 128), F32)],
        compiler_params=_params(48, 2),
    )(*_hbm(z4, z4, z4, z4, hg_lb, gnorm))


def _hgrn_bwd(z4, o_raw, dy, states, hg_lb, gnorm):
    T = z4.shape[1]
    tb = min(ROW_BLOCK, T)
    C = HG_CHUNK
    ncb = tb // C
    nt = T // tb
    HPB = HG_HEADS_PER_STEP

    def body(q_ref, f_ref, i_ref, g_ref, o_ref, dy_ref, st_ref, lb_ref, gn_ref, dz_ref, dlb_ref, dgn_ref, dst_sc):
        @pl.when(pl.program_id(1) == 0)
        def _():
            dst_sc[...] = jnp.zeros_like(dst_sc)
            dlb_ref[...] = jnp.zeros_like(dlb_ref)
            dgn_ref[...] = jnp.zeros_like(dgn_ref)

        def chunk(cc, carry):
            for hh in range(HPB):
                one_head(ncb - 1 - cc, hh, slice(hh * 128, (hh + 1) * 128))
            return carry

        def one_head(c, hh, hs):
            rs = pl.ds(pl.multiple_of(c * C, C), C)
            lb = _hg_lower_bound(lb_ref.at[:, hs])
            gn = gn_ref[:, hs]
            qc, gc = q_ref[rs, hs], g_ref[rs, hs]
            v_b = i_ref[rs, hs].astype(BF16)
            x = _hg_chunk(qc, f_ref[rs, hs], lb)
            st, dst = st_ref[hh, c], dst_sc[hh]
            st_b, dst_b = st.astype(BF16), dst.astype(BF16)
            o, dyc = o_ref[rs, hs], dy_ref[rs, hs]
            sgg = _sigmoid(gc)
            sil = gc * sgg
            rstd = lax.rsqrt(jnp.mean(o * o, -1, keepdims=True) + EPS)
            n = o * rstd
            dgn_ref[:, hs] += _fold8(dyc * n * sil)
            dn = dyc * gn * sil
            do = rstd * (dn - n * jnp.mean(dn * n, -1, keepdims=True))
            dg = dyc * n * gn * (sgg * (1.0 + gc * (1.0 - sgg)))
            do_b = do.astype(BF16)
            da = jnp.where(x["tril"], lax.dot_general(do_b, v_b, NT_DIMS, preferred_element_type=F32), 0.0).astype(BF16)
            qt_b, kt_b, qe_b, khat_b = (x[n_].astype(BF16) for n_ in ("qt", "kt", "qe", "khat"))
            dv = (lax.dot_general(x["a"].astype(BF16), do_b, TN_DIMS, preferred_element_type=F32)
                  + lax.dot_general(khat_b, dst_b, NT_DIMS, preferred_element_type=F32))
            dqt = jnp.dot(da, kt_b, preferred_element_type=F32)
            dqe = jnp.dot(do_b, st_b, preferred_element_type=F32)
            dkt = lax.dot_general(da, qt_b, TN_DIMS, preferred_element_type=F32)
            dkhat = jnp.dot(v_b, dst_b, preferred_element_type=F32)
            dst_sc[hh] = lax.dot_general(do_b, qe_b, TN_DIMS, preferred_element_type=F32) + dst * x["e_last"]
            de_last = jnp.sum(st * dst, axis=0, keepdims=True)
            dqf = dqt * x["eq"] + dqe * x["e"]
            dkk = dkt * x["ek"] + dkhat * x["eh"]
            dkh_kh = dkhat * x["khat"]
            db = dqt * qt_b.astype(F32) - dkt * kt_b.astype(F32) + dqe * x["qe"] - dkh_kh
            db_last = jnp.sum(dkh_kh, axis=0, keepdims=True) + de_last * x["e_last"]
            db = db + jnp.where(x["rowid"] == C - 1, db_last, 0.0)
            dlg = _running_sum(db, reverse=True)
            dgate = dlg / x["gate"] - dkk
            sg, sq = x["sg"], x["sq"]
            dlb_ref[:, hs] += _fold8(dgate * (1.0 - sg)) * (lb * (1.0 - lb))
            dz_ref[0, rs, hs] = (dqf * (sq * (1.0 + qc * (1.0 - sq)))).astype(BF16)
            dz_ref[1, rs, hs] = (dgate * (1.0 - lb) * sg * (1.0 - sg)).astype(BF16)
            dz_ref[2, rs, hs] = dv.astype(BF16)
            dz_ref[3, rs, hs] = dg.astype(BF16)

        lax.fori_loop(0, ncb, chunk, 0, unroll=4)

    W = 128 * HPB
    zb = lambda k: pl.BlockSpec((None, tb, W), lambda h, t: (k, nt - 1 - t, h))
    blk = pl.BlockSpec((tb, W), lambda h, t: (nt - 1 - t, h))
    acc = pl.BlockSpec((8, W), lambda h, t: (0, h))
    return pl.pallas_call(
        body, name="hgrn_bwd", grid=(HEADS // HPB, nt),
        in_specs=[zb(0), zb(1), zb(2), zb(3), blk, blk,
                  pl.BlockSpec((HPB, ncb, 128, 128), lambda h, t: (h, nt - 1 - t, 0, 0)),
                  pl.BlockSpec((2, W), lambda h, t: (0, h)), pl.BlockSpec((1, W), lambda h, t: (0, h))],
        out_specs=[pl.BlockSpec((4, tb, W), lambda h, t: (0, nt - 1 - t, h)), acc, acc],
        out_shape=[jax.ShapeDtypeStruct((4, T, D), BF16), jax.ShapeDtypeStruct((8, D), F32),
                   jax.ShapeDtypeStruct((8, D), F32)],
        scratch_shapes=[pltpu.VMEM((HPB, 128, 128), F32)],
        compiler_params=_params(48, 2),
    )(*_hbm(z4, z4, z4, z4, o_raw, dy, states, hg_lb, gnorm))


def _adamw(w, g, m, v, *, name):
    R, L = w.shape
    tr = R if R <= 512 else 512
    assert R % tr == 0
    blk = pl.BlockSpec((tr, L), lambda i: (i, 0))
    c1, c2 = 1.0 - B1 ** STEP, 1.0 - B2 ** STEP

    def body(w_ref, g_ref, m_ref, v_ref, d_ref, mo_ref, vo_ref):
        g_ = g_ref[...]
        m_ = B1 * m_ref[...] + (1.0 - B1) * g_
        v_ = B2 * v_ref[...] + (1.0 - B2) * (g_ * g_)
        d_ref[...] = -LR * ((m_ / c1) / (jnp.sqrt(v_ / c2) + ADAM_EPS) + WD * w_ref[...])
        mo_ref[...] = m_
        vo_ref[...] = v_

    sds = jax.ShapeDtypeStruct((R, L), F32)
    return pl.pallas_call(
        body, name=name, grid=(R // tr,), in_specs=[blk] * 4, out_specs=[blk] * 3, out_shape=[sds] * 3,
        compiler_params=_params(32, 1),
    )(w, g, m, v)


def _adamw_rows(w, m, v, gbufs, row0, *, name, plan=None):
    L, R, C = w.shape
    tr = 256
    assert R % tr == 0 and row0 % tr == 0 and len(gbufs) == L
    grid = (L, R // tr)
    blk = pl.BlockSpec((None, tr, C), lambda l, i: (l, i, 0))
    gblks = [pl.BlockSpec((tr, C), lambda l, i, k=k: (row0 // tr + jnp.where(l == k, i, 0), 0)) for k in range(L)]
    c1, c2 = 1.0 - B1 ** STEP, 1.0 - B2 ** STEP

    def body(*refs):
        ins, (go_ref, d_ref, mo_ref, vo_ref), _, pctx = _split_refs(refs, 3 + L, 4, 0, plan)
        w_ref, m_ref, v_ref = ins[:3]
        g_refs = ins[3:]
        _plan_start(plan, pctx, grid)
        g_ = g_refs[0][...]
        for l in range(1, L):
            g_ = jnp.where(pl.program_id(0) == l, g_refs[l][...], g_)
        m_ = B1 * m_ref[...] + (1.0 - B1) * g_
        v_ = B2 * v_ref[...] + (1.0 - B2) * (g_ * g_)
        go_ref[...] = g_
        d_ref[...] = -LR * ((m_ / c1) / (jnp.sqrt(v_ / c2) + ADAM_EPS) + WD * w_ref[...])
        mo_ref[...] = m_
        vo_ref[...] = v_
        _plan_wait(plan, pctx, grid)

    sds = jax.ShapeDtypeStruct((L, R, C), F32)
    p_in, p_ospec, p_oshape, p_scr, p_alias = _plan_io(plan, 3 + L, 4)
    return pl.pallas_call(
        body, name=name, grid=grid, in_specs=[blk] * 3 + gblks + [_ANY] * len(p_in),
        out_specs=[blk] * 4 + p_ospec, out_shape=[sds] * 4 + p_oshape, scratch_shapes=p_scr,
        input_output_aliases=p_alias, compiler_params=_params(32, 2),
    )(*_hbm(w, m, v, *gbufs), *p_in)


def _add_pairs(g, theirs, ids, *, name):
    n, R, L = theirs.shape
    tr = math.gcd(R, 128)
    nb = R // tr

    def body(ids_ref, a_ref, b_ref, o_ref):
        o_ref[...] = (a_ref[...].astype(F32) + b_ref[...].astype(F32)).astype(BF16)

    blk = pl.BlockSpec((n, tr, L), lambda i, ids: (0, i, 0))
    return pl.pallas_call(
        body, name=name, out_shape=jax.ShapeDtypeStruct((n, R, L), BF16),
        grid_spec=pltpu.PrefetchScalarGridSpec(
            num_scalar_prefetch=1, grid=(nb,),
            in_specs=[pl.BlockSpec((n, tr, L), lambda i, ids: (0, ids[1] * nb + i, 0)), blk], out_specs=blk),
        compiler_params=_params(16, 1),
    )(ids, g, theirs)


def _sum_chips(pair, parts, ids, *, name):
    _, R, L = parts.shape
    tr = math.gcd(R, 128)

    def body(ids_ref, o_ref, r_ref, out_ref):
        out_ref[...] = ((o_ref[...].astype(F32) + r_ref[0].astype(F32)) + r_ref[1].astype(F32)) + r_ref[2].astype(F32)

    return pl.pallas_call(
        body, name=name, out_shape=jax.ShapeDtypeStruct((2, R, L), F32),
        grid_spec=pltpu.PrefetchScalarGridSpec(
            num_scalar_prefetch=1, grid=(R // tr,),
            in_specs=[pl.BlockSpec((None, tr, L), lambda i, ids: (ids[0], i, 0)),
                      pl.BlockSpec((3, tr, L), lambda i, ids: (0, i, 0))],
            out_specs=pl.BlockSpec((None, tr, L), lambda i, ids: (ids[1], i, 0))),
        compiler_params=_params(32, 1),
    )(ids, pair, parts)


def _mesh_ids():
    x, y, c = _mesh_pos()
    return jnp.stack([2 * x + y, c]).astype(jnp.int32)


def _place_shard(rows, ids, *, name):
    R, L = rows.shape
    tr = 128

    def body(ids_ref, in_ref, out_ref):
        out_ref[...] = in_ref[...].astype(BF16)

    return pl.pallas_call(
        body, name=name, out_shape=jax.ShapeDtypeStruct((4, R, L), BF16),
        grid_spec=pltpu.PrefetchScalarGridSpec(
            num_scalar_prefetch=1, grid=(R // tr,), in_specs=[pl.BlockSpec((tr, L), lambda i, ids: (i, 0))],
            out_specs=pl.BlockSpec((None, tr, L), lambda i, ids: (ids[0], i, 0))),
        compiler_params=_params(16, 1),
    )(ids, rows)


def _place_weights(pieces, buffer_rows, ids, *, plan=None):
    tr = 256
    steps, s = [], 0
    for arr, layer, buf, row0 in pieces:
        nblk = arr.shape[1] // tr
        steps.append((s, nblk))
        s += nblk
    total = s
    buf_start = [min(st for (st, _), p in zip(steps, pieces) if p[2] == k) for k in range(len(buffer_rows))]
    grid = (total,)
    n_in = len(pieces)

    def body(*refs):
        ins, outs, _, pctx = _split_refs(refs[1:], n_in, len(buffer_rows), 0, plan)
        _plan_start(plan, pctx, grid)
        i = pl.program_id(0)
        for (st, nblk), (_, _, buf, _), ref in zip(steps, pieces, ins):
            @pl.when((i >= st) & (i < st + nblk))
            def _(ref=ref, buf=buf):
                outs[buf][...] = ref[...].astype(BF16)
        _plan_wait(plan, pctx, grid)

    in_specs = [pl.BlockSpec((None, tr, D), lambda i, ids, layer=layer, st=st, nblk=nblk:
                             (layer, jnp.clip(i - st, 0, nblk - 1), 0))
                for (st, nblk), (_, layer, _, _) in zip(steps, pieces)]
    out_specs = [pl.BlockSpec((None, tr, D), lambda i, ids, st=st, nb=rows // tr: (ids[0], jnp.clip(i - st, 0, nb - 1), 0))
                 for st, rows in zip(buf_start, buffer_rows)]
    p_in, p_ospec, p_oshape, p_scr, p_alias = _plan_io(plan, 1 + n_in, len(buffer_rows))
    return pl.pallas_call(
        body, name="place_weights",
        out_shape=[jax.ShapeDtypeStruct((4, rows, D), BF16) for rows in buffer_rows] + p_oshape,
        grid_spec=pltpu.PrefetchScalarGridSpec(
            num_scalar_prefetch=1, grid=grid, in_specs=in_specs + [_ANY] * len(p_in), out_specs=out_specs + p_ospec,
            scratch_shapes=p_scr),
        input_output_aliases=p_alias, compiler_params=_params(16, 1),
    )(ids, *[p[0] for p in pieces], *p_in)


def _remote(src, dst, send_sem, recv_sem, to):
    return pltpu.make_async_remote_copy(src_ref=src, dst_ref=dst, send_sem=send_sem, recv_sem=recv_sem,
                                        device_id=to, device_id_type=MESH_IDS)


def _rows(ref, lead, start, size):
    return ref.at[tuple(pl.ds(0, n) for n in ref.shape[:lead]) + (pl.ds(start, size),)]


def _other_chips():
    x, y, _ = _mesh_pos()
    return [(1 - x, y), (x, 1 - y), (1 - x, 1 - y)]


def _plan_gather_ici(bufs):
    n = len(bufs)

    def copies(outs, send, recv):
        x, y, c = _mesh_pos()
        res = []
        for b in range(n):
            half = bufs[b].shape[1] // 2
            mine = _rows(outs[b].at[2 * x + y], 0, c * half, half)
            for j, (cx, cy) in enumerate(_other_chips()):
                res.append((_remote(mine, mine, send(3 * b + j), recv(3 * b + j), (cx, cy, c)),
                            _remote(mine, _rows(outs[b].at[2 * cx + cy], 0, c * half, half),
                                    send(3 * b + j), recv(3 * b + j), (x, y, c))))
        return res

    def start(ins, outs, send, recv, loc):
        for out_cp, _ in copies(outs, send, recv):
            out_cp.start()

    def wait(ins, outs, send, recv, loc):
        for out_cp, in_cp in copies(outs, send, recv):
            in_cp.wait_recv()
            out_cp.wait_send()

    outs = [jax.ShapeDtypeStruct(b.shape, b.dtype) for b in bufs]
    return _Plan(bufs, outs, 3 * n, 0, start, wait, aliases={b: b for b in range(n)})


def _plan_gather_forward(bufs):
    n = len(bufs)

    def copies(outs, send, recv):
        x, y, c = _mesh_pos()
        res = []
        for b in range(n):
            half = bufs[b].shape[1] // 2
            for j, (cx, cy) in enumerate(_other_chips()):
                slot = outs[b].at[2 * cx + cy]
                res.append((_remote(_rows(slot, 0, c * half, half), _rows(slot, 0, c * half, half),
                                    send(3 * b + j), recv(3 * b + j), (x, y, 1 - c)),
                            _remote(_rows(slot, 0, c * half, half), _rows(slot, 0, (1 - c) * half, half),
                                    send(3 * b + j), recv(3 * b + j), (x, y, c))))
        return res

    def start(ins, outs, send, recv, loc):
        for out_cp, _ in copies(outs, send, recv):
            out_cp.start()

    def wait(ins, outs, send, recv, loc):
        for out_cp, in_cp in copies(outs, send, recv):
            in_cp.wait_recv()
            out_cp.wait_send()

    outs = [jax.ShapeDtypeStruct(b.shape, b.dtype) for b in bufs]
    return _Plan(bufs, outs, 3 * n, 0, start, wait, aliases={b: b for b in range(n)})


def _plan_pair_swap(g):
    half = g.shape[1] // 2

    def copy(ins, outs, send, recv, loc):
        x, y, c = _mesh_pos()
        return _remote(_rows(ins[0], 1, (1 - c) * half, half), outs[0], send(0), recv(0), (x, y, 1 - c))

    return _Plan([g], [jax.ShapeDtypeStruct((4, half, g.shape[2]), g.dtype)], 1, 0,
                 lambda *a: copy(*a).start(), lambda *a: copy(*a).wait())


def _plan_pair_gather(buf):
    def copies(ins, outs, send, recv, loc):
        x, y, c = _mesh_pos()
        return (_remote(outs[0].at[c], outs[0].at[c], send(0), recv(0), (x, y, 1 - c)),
                _remote(outs[0].at[c], outs[0].at[1 - c], send(0), recv(0), (x, y, c)))

    def wait(*a):
        out_cp, in_cp = copies(*a)
        in_cp.wait_recv()
        out_cp.wait_send()

    return _Plan([buf], [jax.ShapeDtypeStruct(buf.shape, buf.dtype)], 1, 0, lambda *a: copies(*a)[0].start(), wait,
                 aliases={0: 0})


def _plan_chip_scatter(p):
    def copies(ins, outs, send, recv, loc):
        _, _, c = _mesh_pos()
        return [_remote(ins[0].at[2 * cx + cy], outs[0].at[j], send(j), recv(j), (cx, cy, c))
                for j, (cx, cy) in enumerate(_other_chips())]

    def start(*a):
        for cp in copies(*a):
            cp.start()

    def wait(*a):
        for cp in copies(*a):
            cp.wait()

    return _Plan([p], [jax.ShapeDtypeStruct((3,) + p.shape[1:], p.dtype)], 3, 0, start, wait)


def _plan_exchange_all(vec):
    def copies(ins, outs, send, recv, loc):
        x, y, c = _mesh_pos()
        return [_remote(ins[0], outs[0].at[r - 1], send(r - 1), recv(r - 1), (x ^ (r >> 2), y ^ ((r >> 1) & 1), c ^ (r & 1)))
                for r in range(1, 8)]

    def start(*a):
        for cp in copies(*a):
            cp.start()

    def wait(*a):
        for cp in copies(*a):
            cp.wait()

    return _Plan([vec], [jax.ShapeDtypeStruct((7,) + vec.shape, vec.dtype)], 7, 0, start, wait)


SMALL_LAYOUT = {
    "mla_gq": (0, 1, 256, (1, 256)), "mla_gkv": (1, 1, 256, (1, 256)), "sgu_ln_g": (2, 1, 512, (1, 512)),
    "sgu_ln_b": (3, 1, 512, (1, 512)), "sgu_w": (4, 64, 1024, (64, 1024)), "sgu_b": (68, 1, 512, (1, 512)),
    "hg_lb": (69, 2, 1024, (2, 1024)), "hg_gnorm": (71, 1, 1024, (1, 256)), "ln1_g": (72, 2, 1024, (2, 1024)),
    "ln1_b": (74, 2, 1024, (2, 1024)), "ln2_g": (76, 2, 1024, (2, 1024)), "ln2_b": (78, 2, 1024, (2, 1024)),
}


def _small_pack(dgq, dgkv, dslg, dslb, dsw, dsb, dlb, dgn, ln_parts, sq_err):
    flat_ln = [p for pair in ln_parts for p in pair]

    def body(*refs):
        gq_ref, gkv_ref, slg_ref, slb_ref, sw_ref, sb_ref, lb_ref, gn_ref = refs[:8]
        ln_refs, err_ref, out_ref, t_sc = refs[8:16], refs[16], refs[17], refs[18]
        s8 = lambda ref: jnp.sum(ref[...], axis=0, keepdims=True)
        out_ref[...] = jnp.zeros_like(out_ref)
        out_ref[0:1, 0:256] = s8(gq_ref)
        out_ref[1:2, 0:256] = s8(gkv_ref)
        out_ref[2:3, 0:512] = s8(slg_ref)
        out_ref[3:4, 0:512] = s8(slb_ref)
        out_ref[4:68, :] = sw_ref[...]
        t_sc[...] = sb_ref[...].T
        for g in range(SGU_G):
            out_ref[68:69, g * SGU_C:(g + 1) * SGU_C] = t_sc[g:g + 1, :]
        d_lb1 = s8(lb_ref)
        out_ref[69:70, :] = -d_lb1
        out_ref[70:71, :] = d_lb1
        out_ref[71:72, :] = s8(gn_ref)
        for k, ref in enumerate(ln_refs):
            out_ref[72 + k:73 + k, :] = s8(ref)
        out_ref[0:1, 1023:1024] = jnp.sum(s8(err_ref), axis=1, keepdims=True) * (0.5 / D)

    vm = pl.BlockSpec(memory_space=pltpu.VMEM)
    return pl.pallas_call(
        body, name="small_grad_pack", in_specs=[vm] * 17, out_specs=vm,
        out_shape=jax.ShapeDtypeStruct((SMALL_ROWS, 1024), F32), scratch_shapes=[pltpu.VMEM((SGU_C, SGU_C), F32)],
        compiler_params=_params(16),
    )(dgq, dgkv, dslg, dslb, dsw.reshape(64, 1024), dsb, dlb, dgn, *flat_ln, sq_err)


def _small_update(vec, others, ids, w, m, v):
    names = list(SMALL_LAYOUT)
    n = len(names)
    c1, c2 = 1.0 - B1 ** STEP, 1.0 - B2 ** STEP
    have_others = others is not None

    def body(*refs):
        ids_ref, v_ref = refs[0], refs[1]
        k = 2 + have_others
        w_refs, m_refs, v_refs = refs[k:k + n], refs[k + n:k + 2 * n], refs[k + 2 * n:k + 3 * n]
        outs = refs[k + 3 * n:]
        row0_ref, tot_sc = outs[0], outs[-1]
        total = v_ref[...]
        if have_others:
            me = 2 * ids_ref[0] + ids_ref[1]
            total = None
            for d in range(8):
                rel = d ^ me
                term = jnp.where(rel == 0, v_ref[...], refs[2][jnp.maximum(rel - 1, 0)])
                total = term if total is None else total + term
        tot_sc[...] = total
        row0_ref[...] = tot_sc[0:1, :]
        for i, name in enumerate(names):
            r0, nr, width, _ = SMALL_LAYOUT[name]
            if name == "hg_gnorm":
                g_ = tot_sc[r0:r0 + 1, 0:256]
                for chip in range(1, 4):
                    g_ = jnp.where(ids_ref[0] == chip, tot_sc[r0:r0 + 1, chip * 256:(chip + 1) * 256], g_)
            else:
                g_ = tot_sc[r0:r0 + nr, 0:width]
            m_ = B1 * m_refs[i][...] + (1.0 - B1) * g_
            v_ = B2 * v_refs[i][...] + (1.0 - B2) * (g_ * g_)
            go, do, mo, vo = outs[1 + 4 * i:5 + 4 * i]
            go[...] = g_
            do[...] = -LR * ((m_ / c1) / (jnp.sqrt(v_ / c2) + ADAM_EPS) + WD * w_refs[i][...])
            mo[...] = m_
            vo[...] = v_

    full = lambda shape: pl.BlockSpec(shape, lambda i, ids, nd=len(shape): (0,) * nd)
    kshapes = [SMALL_LAYOUT[name][3] for name in names]
    operands = [vec] + ([others] if have_others else []) + [d[name] for d in (w, m, v) for name in names]
    out_shapes = [jax.ShapeDtypeStruct((1, 1024), F32)] + [jax.ShapeDtypeStruct(s, F32) for s in kshapes for _ in range(4)]
    res = pl.pallas_call(
        body, name="small_update", out_shape=out_shapes,
        grid_spec=pltpu.PrefetchScalarGridSpec(
            num_scalar_prefetch=1, grid=(1,), in_specs=[full(o.shape) for o in operands],
            out_specs=[full(s.shape) for s in out_shapes],
            scratch_shapes=[pltpu.VMEM((SMALL_ROWS, 1024), F32)]),
        compiler_params=_params(32, 1),
    )(ids, *operands)
    return res[0], {name: tuple(res[1 + 4 * i:5 + 4 * i]) for i, name in enumerate(names)}


ROWS_L1, ROWS_L0, ROWS_ODD_W = 3328, 2048, 384
ODD_PARTS = (("w_out_e", (256, 1024)), ("w_in_e", (1024, 392)), ("w_qb", (256, 192)), ("w_kvb", (256, 256)))
ODD_W_PARTS = tuple(p for p in ODD_PARTS if p[0] != "w_in_e")


def _odd_rows(parts, dtype, layout, total, gnorm=None):
    rows = [parts[n].reshape(-1, 1024).astype(dtype) for n, _ in layout]
    used = sum(r.shape[0] for r in rows)
    if gnorm is not None:
        bits = lax.bitcast_convert_type(gnorm.reshape(-1), BF16).reshape(1, 512)
        rows.append(jnp.pad(bits, ((0, 15), (0, 512))))
        used += 16
    if total > used:
        rows.append(jnp.zeros((total - used, 1024), dtype))
    return jnp.concatenate(rows, axis=0)


def _odd_unrows(buf, layout, with_gnorm=False):
    out, off = {}, 0
    for n, shape in layout:
        nr = math.prod(shape) // 1024
        out[n] = buf[off:off + nr].reshape(shape)
        off += nr
    if with_gnorm:
        out["hg_gnorm"] = lax.bitcast_convert_type(buf[off, :512].reshape(256, 2), F32).reshape(1, 256)
    return out


def _rope_tables(positions):
    half = ROPE // 2
    inv_freq = ROPE_BASE ** (-jnp.arange(half, dtype=F32) / half)
    ang = positions.astype(F32).reshape(-1, 1) * inv_freq
    cos, sin = jnp.cos(ang), jnp.sin(ang)
    T = ang.shape[0]
    one, z16, z32 = jnp.ones((T, NOPE), F32), jnp.zeros((T, half), F32), jnp.zeros((T, 32), F32)
    z64 = jnp.zeros((T, NOPE), F32)
    c = jnp.concatenate([one, cos, cos, z32], axis=1)
    s1 = jnp.concatenate([z64, -sin, z16, z32], axis=1)
    s2 = jnp.concatenate([z64, z16, sin, z32], axis=1)
    return c, s1, s2


def _local_step(x, positions, tgt, odd, bufs, P, exchange):
    T = x.shape[0]
    row = lambda a: a.reshape(1, -1)
    rc, rs1, rs2 = _rope_tables(positions)
    blk = lambda f: pl.BlockSpec((None, D, D), f)

    w_in_e = odd["w_in_e"]
    w_in = jnp.concatenate([w_in_e[:, :512], w_in_e[:, 544:1568], w_in_e[:, 512:544], jnp.zeros((D, 96), BF16)], axis=1)
    wq = jnp.pad(odd["w_qb"].reshape(256, HEADS, NOPE + ROPE), ((0, 0), (0, 0), (0, 32))).reshape(256, HEADS * 128)
    kvb = odd["w_kvb"].reshape(256, HEADS, NOPE + VDIM)
    wk = jnp.pad(kvb[:, :, :NOPE], ((0, 0), (0, 0), (0, 64))).reshape(256, HEADS * 128)
    wv = kvb[:, :, NOPE:].reshape(256, HEADS * VDIM)
    w_out_e = odd["w_out_e"]
    sgu_w = P["sgu_w"][0]
    sgu_bt = P["sgu_b"][0].T
    gq, gkv = P["mla_gq"], P["mla_gkv"]
    gnorm = P["hg_gnorm"]

    z0 = _matmul(x, w_in, name="in_proj_e", M=T, N=1664, K=D, tn=1664)[0]
    q, k, v = _mla_prep(z0, gq, gkv, wq, wk, wv, rc, rs1, rs2)
    if exchange:
        ids = _mesh_ids()
        placed = list(bufs)
        a_out, lse, wga, wgb = _flash_fwd(q, k, v, plan=_plan_gather_ici(placed[:2]))
    else:
        a_out, lse = _flash_fwd(q, k, v)
        wga, wgb, wgc = bufs
    mix0 = _sgu_fwd(z0, a_out, P["sgu_ln_g"], P["sgu_ln_b"], sgu_w, sgu_bt)
    res = _proj_ln(mix0, w_out_e, x, row(P["ln1_g"][0]), row(P["ln1_b"][0]), name="out_proj_ln_e",
                   plan=_plan_gather_forward([wga, wgb]) if exchange else None)
    r1, h1b = res[:2]
    if exchange:
        wga, wgb = res[2:]
    ln = lambda name, l: (row(P[name + "_g"][l]), row(P[name + "_b"][l]))
    res = _ffn_ln(h1b, wga, r1, *ln("ln2", 0), name="ffn_ln_0", prev_ln=ln("ln1", 0),
                  plan=_plan_gather_ici(placed[2:]) if exchange else None)
    ra0, r2, h2b = res[:3]
    z4 = _matmul(h2b, wgb, name="in_proj_o", M=T, N=4 * D, K=D, tn=2 * D, n_slots=True,
                 b_spec=pl.BlockSpec((2, D, D), lambda i, j, k: (j, 0, 0)),
                 out_shape=jax.ShapeDtypeStruct((4, T, D), F32),
                 o_spec=pl.BlockSpec((2, min(MM_ROWS, T), D), lambda i, j, k: (j, i, 0)))[0]
    y1, o_raw, states = _hgrn_fwd(z4, P["hg_lb"], gnorm)
    res2 = _proj_ln(y1, wgb, r2, *ln("ln1", 1), name="out_proj_ln_o", prev_ln=ln("ln2", 0), w_rowblk=4,
                    plan=_plan_gather_forward([res[3]]) if exchange else None)
    r3, h3b = res2[:2]
    if exchange:
        wgc = res2[2]
    ra1, r4, _ = _ffn_ln(h3b, wgc, r3, *ln("ln2", 1), name="ffn_ln_1", prev_ln=ln("ln1", 1))

    ln1_g, ln1_b, ln2_g, ln2_b = [None, None], [None, None], [None, None], [None, None]
    sq_err_parts = []

    def ffn_bwd(l, dh, r_out, ra, h_mid_b, g2, wg, rows, plan=None, loss_head=()):
        dr, dr_b, dg, db, *sq_err = _ln_bwd(dh, r_out, row(g2), name=f"ln2_bwd_{l}", loss_head=loss_head)
        sq_err_parts.extend(sq_err)
        ln2_g[l], ln2_b[l] = dg, db
        da, *extra = _matmul(dr_b, wg, tb=True, mul=ra, out_dtype=BF16, name=f"ffn_da_{l}", M=T, N=4 * D, K=D, tn=2 * D,
                             b_spec=pl.BlockSpec((2, D, D), lambda i, j, k: (j, 1, 0)), n_slots=True, plan=plan)
        gbuf = _matmul(ra, dr_b, ta=True, a_sq=True, name=f"ffn_dw2_{l}", M=4 * D, N=D, K=T, tm=1024, tk=DW_TOKENS // 2,
                       out_shape=jax.ShapeDtypeStruct((4, rows, D), BF16), o_spec=blk(lambda i, j, k: (i, 1, 0)))[0]
        gbuf = _matmul(h_mid_b, da, ta=True, name=f"ffn_dw1_{l}", M=D, N=4 * D, K=T, tm=1024, tk=DW_TOKENS, into=gbuf,
                       out_shape=jax.ShapeDtypeStruct((4, rows, D), BF16), o_spec=blk(lambda i, j, k: (j, 0, 0)))[0]
        dh_mid = _matmul(da, wg, tb=True, add=dr, add_scale=ALPHA, name=f"ffn_dh_{l}", M=T, N=D, K=4 * D, tk=2 * D,
                         b_spec=pl.BlockSpec((2, D, D), lambda i, j, k: (k, 0, 0)))[0]
        return dh_mid, gbuf, extra

    dh3, g1, _ = ffn_bwd(1, None, r4, ra1, h3b, P["ln2_g"][1], wgc, ROWS_L1, loss_head=(row(P["ln2_b"][1]), tgt))
    loss_parts = sq_err_parts[0]
    dr3, dr3_b, dg, db = _ln_bwd(dh3, r3, row(P["ln1_g"][1]), name="ln1_bwd_1")
    ln1_g[1], ln1_b[1] = dg, db
    g1_sds = jax.ShapeDtypeStruct((4, ROWS_L1, D), BF16)
    g1 = _matmul(y1, dr3_b, ta=True, name="dw_out_o", M=D, N=D, K=T, tm=256, tk=DW_TOKENS, into=g1, out_shape=g1_sds,
                 o_spec=pl.BlockSpec((None, 256, D), lambda i, j, k: (i, 12, 0)))[0]
    dmix1 = _matmul(dr3_b, wgb, tb=True, name="dmix_o", M=T, N=D, K=D, b_spec=_rows4_spec(4, 3), b_merge=(D, D))[0]
    dz4, dlb, dgn = _hgrn_bwd(z4, o_raw, dmix1, states, P["hg_lb"], gnorm)
    g1 = _matmul(h2b, dz4, ta=True, name="dw_in_o", M=D, N=4 * D, K=T, tm=1024, tk=DW_TOKENS, into=g1, out_shape=g1_sds,
                 b_spec=pl.BlockSpec((None, min(DW_TOKENS, T), D), lambda i, j, k: (j, k, 0)),
                 o_spec=blk(lambda i, j, k: (j, 2, 0)))[0]
    dh2 = _matmul(dz4, wgb, tb=True, add=dr3, add_scale=ALPHA, name="dh_in_o", M=T, N=D, K=4 * D, tk=2 * D,
                  a_spec=pl.BlockSpec((2, min(MM_ROWS, T), D), lambda i, j, k: (k, i, 0)),
                  b_spec=pl.BlockSpec((2, D, D), lambda i, j, k: (k, 0, 0)))[0]

    dh1, g0, swapped1 = ffn_bwd(0, dh2, r2, ra0, h1b, P["ln2_g"][0], wga, ROWS_L0,
                                plan=_plan_pair_swap(g1) if exchange else None)
    dr1, dr1_b, dg, db = _ln_bwd(dh1, r1, row(P["ln1_g"][0]), name="ln1_bwd_0")
    ln1_g[0], ln1_b[0] = dg, db
    godd = {"w_out_e": _matmul(mix0, dr1_b, ta=True, name="dw_out_e", M=D, N=D, K=T, tm=1024, tk=DW_TOKENS)[0]}
    dmix0, *swapped0 = _matmul(dr1_b, w_out_e, tb=True, name="dmix_e", M=T, N=D, K=D,
                               plan=_plan_pair_swap(g0) if exchange else None)
    delta, do_b = _attn_delta(dmix0, a_out)
    if exchange:
        pair1 = _add_pairs(g1, swapped1[0], ids, name="grad_pair_add_1")
        pair0 = _add_pairs(g0, swapped0[0], ids, name="grad_pair_add_0")
        dq4, dk, dv, parts0, parts1 = _flash_bwd(
            q, k, v, do_b, lse, delta, plan=_join_plans([_plan_chip_scatter(pair0), _plan_chip_scatter(pair1)]))
        half0 = _sum_chips(pair0, parts0, ids, name="grad_chip_sum_0")
        half1 = _sum_chips(pair1, parts1, ids, name="grad_chip_sum_1")
        dc, dkr, dwq, dwk, dwv, dgq, dgkv, g0, g1 = _mla_bwd(
            z0, dq4, dk, dv, gq, gkv, wq, wk, wv, rc, rs1, rs2,
            plan=_join_plans([_plan_pair_gather(half0), _plan_pair_gather(half1)]))
        g0, g1 = g0.reshape(ROWS_L0, D), g1.reshape(ROWS_L1, D)
    else:
        dq4, dk, dv = _flash_bwd(q, k, v, do_b, lse, delta)
        dc, dkr, dwq, dwk, dwv, dgq, dgkv = _mla_bwd(z0, dq4, dk, dv, gq, gkv, wq, wk, wv, rc, rs1, rs2)
    godd["w_qb"] = dwq.reshape(256, HEADS, 128)[:, :, :NOPE + ROPE].reshape(256, HEADS * (NOPE + ROPE))
    godd["w_kvb"] = jnp.concatenate([dwk.reshape(256, HEADS, 128)[:, :, :NOPE], dwv.reshape(256, HEADS, VDIM)],
                                    axis=2).reshape(256, HEADS * (NOPE + VDIM))
    swap_b = None
    if exchange:
        by_chip = [_odd_rows({"w_out_e": jnp.split(godd["w_out_e"], 4, axis=0)[j],
                              **{n: jnp.split(godd[n], 4, axis=1)[j] for n in ("w_qb", "w_kvb")}}, BF16,
                             ODD_W_PARTS, ROWS_ODD_W)
                   for j in range(4)]
        odd_b = jnp.stack(by_chip)
        swap_b = _plan_pair_swap(odd_b)
    dz0, dsw, dsb, dslg, dslb, *theirs_b = _sgu_bwd(z0, dmix0, dc, dkr, P["sgu_ln_g"], P["sgu_ln_b"], sgu_w, sgu_bt,
                                                    plan=swap_b)
    small_vec = _small_pack(dgq, dgkv, dslg, dslb, dsw, dsb, dlb, dgn, [ln1_g, ln1_b, ln2_g, ln2_b], loss_parts)
    plan_in = None
    if exchange:
        pair_b = _add_pairs(odd_b, theirs_b[0], ids, name="odd_pair_add_1")
        plan_in = _join_plans([_plan_exchange_all(small_vec), _plan_chip_scatter(pair_b)])
    dw_in, *carried = _matmul(x, dz0, ta=True, name="dw_in_e", M=D, N=1664, K=T, tm=1024, tn=1664, tk=DW_TOKENS // 4,
                              plan=plan_in)
    godd["w_in_e"] = jnp.concatenate([dw_in[:, :512], dw_in[:, 1536:1568], dw_in[:, 512:1536]], axis=1)
    plan_x = None
    if exchange:
        small_others, parts_b = carried
        odd_a = godd["w_in_e"].reshape(D, 4, 392).transpose(1, 0, 2).astype(BF16)
        theirs_a = _run_plan(_plan_pair_swap(odd_a), name="odd_pair_swap")[0]
        pair_a = _add_pairs(odd_a, theirs_a, ids, name="odd_pair_add_0")
        plan_x = _plan_chip_scatter(pair_a)
    grad_x, *parts_a = _matmul(dz0, w_in, tb=True, add=dr1, add_scale=ALPHA, name="dx", M=T, N=D, K=1664, tk=1664,
                               plan=plan_x)
    if exchange:
        godd = ([pair_a, pair_b], [parts_a[0], parts_b])
        return grad_x, g0, g1, godd, small_vec, small_others
    return grad_x, g0, g1, godd, small_vec, None


WEIGHTS = ['w_in_e', 'mla_gq', 'mla_gkv', 'w_qb', 'w_kvb', 'sgu_ln_g', 'sgu_ln_b', 'sgu_w', 'sgu_b', 'w_out_e',
           'w_in_o', 'hg_lb', 'hg_gnorm', 'w_out_o', 'ln1_g', 'ln1_b', 'w_ff1', 'w_ff2', 'ln2_g', 'ln2_b']


def kernel(x, positions, w_in_e, mla_gq, mla_gkv, w_qb, w_kvb, sgu_ln_g, sgu_ln_b, sgu_w, sgu_b, w_out_e, w_in_o, hg_lb, hg_gnorm, w_out_o, ln1_g, ln1_b, w_ff1, w_ff2, ln2_g, ln2_b, loss_target, m_w_in_e, m_mla_gq, m_mla_gkv, m_w_qb, m_w_kvb, m_sgu_ln_g, m_sgu_ln_b, m_sgu_w, m_sgu_b, m_w_out_e, m_w_in_o, m_hg_lb, m_hg_gnorm, m_w_out_o, m_ln1_g, m_ln1_b, m_w_ff1, m_w_ff2, m_ln2_g, m_ln2_b, v_w_in_e, v_mla_gq, v_mla_gkv, v_w_qb, v_w_kvb, v_sgu_ln_g, v_sgu_ln_b, v_sgu_w, v_sgu_b, v_w_out_e, v_w_in_o, v_hg_lb, v_hg_gnorm, v_w_out_o, v_ln1_g, v_ln1_b, v_w_ff1, v_w_ff2, v_ln2_g, v_ln2_b):
    args = dict(locals())
    w = {n: args[n] for n in WEIGHTS}
    m = {n: args["m_" + n] for n in WEIGHTS}
    v = {n: args["v_" + n] for n in WEIGHTS}
    cx, cy, cc = _mesh_pos()
    chip = 2 * cx + cy

    odd_shard = _odd_rows({"w_out_e": w_out_e[0], "w_qb": w_qb[0], "w_kvb": w_kvb[0]}, BF16, ODD_W_PARTS, ROWS_ODD_W,
                          gnorm=hg_gnorm)
    ids = _mesh_ids()
    placed = [_place_shard(w_in_e[0], ids, name="place_shard_in_e"), _place_shard(odd_shard, ids, name="place_shard_odd")]
    pieces = [(w_ff1, 0, 0, 0), (w_ff2, 0, 0, 1024), (w_in_o, 0, 1, 0), (w_out_o, 0, 1, 1024),
              (w_ff1, 1, 2, 0), (w_ff2, 1, 2, 1024)]
    *big_bufs, odd_a, odd_b = _place_weights(pieces, (2048, 1280, 2048), ids, plan=_plan_gather_ici(placed))
    gathered = _run_plan(_plan_gather_forward([odd_a, odd_b]), name="odd_gather_forward")
    per_chip = [_odd_unrows(gathered[1][j], ODD_W_PARTS, with_gnorm=True) for j in range(4)]
    odd = {"w_out_e": jnp.concatenate([p["w_out_e"] for p in per_chip], axis=0),
           "w_in_e": jnp.concatenate([gathered[0][j] for j in range(4)], axis=1)}
    for n in ("w_qb", "w_kvb"):
        odd[n] = jnp.concatenate([p[n] for p in per_chip], axis=1)
    small = {n: w[n] for n in SMALL_LAYOUT if n != "hg_gnorm"}
    small["hg_gnorm"] = jnp.concatenate([p["hg_gnorm"] for p in per_chip], axis=1)
    grad_x, g_l0, g_l1, godd, small_vec, small_others = _local_step(
        x[0], positions[0], loss_target[0], odd, big_bufs, small, True)

    sums = [_sum_chips(pair, parts, ids, name=f"odd_chip_sum_{k}") for k, (pair, parts) in enumerate(zip(*godd))]
    g_in_e, g_rest = _run_plan(_join_plans([_plan_pair_gather(s) for s in sums]), name="odd_pair_gather")
    g_odd = _odd_unrows(g_rest.reshape(ROWS_ODD_W, 1024), ODD_W_PARTS)
    g_odd["w_in_e"] = g_in_e.reshape(D, 392)

    to_kernel = lambda d: {n: d[n].reshape(SMALL_LAYOUT[n][3]) for n in SMALL_LAYOUT}
    first_row, small_out = _small_update(small_vec, small_others, ids, to_kernel(w), to_kernel(m), to_kernel(v))
    loss = first_row[0, 1023]
    grads, delta, new_m, new_v = {}, {}, {}, {}
    for n, res in small_out.items():
        grads[n], delta[n], new_m[n], new_v[n] = (r.reshape(w[n].shape) for r in res)

    for n, bufs_, row0 in (("w_ff1", [g_l0, g_l1], 0), ("w_ff2", [g_l0, g_l1], 1024), ("w_in_o", [g_l1], 2048),
                           ("w_out_o", [g_l1], 3072)):
        grads[n], delta[n], new_m[n], new_v[n] = _adamw_rows(w[n], m[n], v[n], bufs_, row0, name=f"adamw_{n}")
    for n, _ in ODD_PARTS:
        grads[n] = g_odd[n][None]
        d_, m_, v_ = _adamw(w[n][0], g_odd[n], m[n][0], v[n][0], name=f"adamw_{n}")
        delta[n], new_m[n], new_v[n] = d_[None], m_[None], v_[None]

    return (loss, grad_x[None], *[grads[n] for n in WEIGHTS], *[delta[n] for n in WEIGHTS],
            *[new_m[n] for n in WEIGHTS], *[new_v[n] for n in WEIGHTS])
```

```python
import math

import jax
import jax.numpy as jnp
from jax import lax
from jax.experimental import pallas as pl
from jax.experimental.pallas import tpu as pltpu

F32 = jnp.float32
BF16 = jnp.bfloat16
MESH_IDS = pl.DeviceIdType.MESH

D = 1024
DEPTH = 2
HEADS = 8
NOPE, ROPE, VDIM = 64, 32, 64
QK_SCALE = (NOPE + ROPE) ** -0.5
ROPE_BASE = 10000.0
SGU_G, SGU_C = 4, 128
HG_CHUNK = 64
HG_HEADS_PER_STEP = 8
ALPHA = (2 * DEPTH) ** 0.25
EPS = 1e-5
LR, B1, B2, ADAM_EPS, WD, STEP = 0.001, 0.9, 0.999, 1e-08, 0.01, 10
GELU_C = math.sqrt(2.0 / math.pi)
GELU_A = 0.044715
MB = 1024 * 1024
ROW_BLOCK = 512
SMALL_ROWS = 80

NT_DIMS = (((1,), (1,)), ((), ()))
TN_DIMS = (((0,), (0,)), ((), ()))


def _params(vmem_mb, n_axes=0):
    kw = dict(vmem_limit_bytes=vmem_mb * MB)
    if n_axes:
        kw["dimension_semantics"] = ("arbitrary",) * n_axes
    return pltpu.CompilerParams(**kw)


_ANY = pl.BlockSpec(memory_space=pltpu.HBM)


def _mesh_pos():
    return lax.axis_index("x"), lax.axis_index("y"), lax.axis_index("c")


def _hbm(*arrays):
    return tuple(pltpu.with_memory_space_constraint(a, pltpu.HBM) if a.size >= 2 ** 18 else a for a in arrays)


class _Plan:
    def __init__(self, ins, outs, n_remote, n_local, start, wait, aliases=None):
        self.ins, self.outs, self.n_remote, self.n_local = list(ins), list(outs), n_remote, n_local
        self.start, self.wait, self.aliases = start, wait, dict(aliases or {})


def _join_plans(plans):
    ins, outs, aliases, parts = [], [], {}, []
    nr = nl = 0
    for p in plans:
        parts.append((p, len(ins), len(outs), nr, nl))
        aliases.update({len(ins) + i: len(outs) + o for i, o in p.aliases.items()})
        ins += p.ins
        outs += p.outs
        nr += p.n_remote
        nl += p.n_local

    def run(which):
        def go(in_refs, out_refs, send, recv, loc):
            for p, i0, o0, r0, l0 in parts:
                getattr(p, which)(in_refs[i0:i0 + len(p.ins)], out_refs[o0:o0 + len(p.outs)],
                                  lambda i, r0=r0: send(r0 + i), lambda i, r0=r0: recv(r0 + i),
                                  lambda i, l0=l0: loc(l0 + i))
        return go

    return _Plan(ins, outs, nr, nl, run("start"), run("wait"), aliases)


def _plan_io(plan, n_in, n_out):
    if plan is None:
        return [], [], [], [], {}
    sems = [pltpu.SemaphoreType.DMA((max(plan.n_remote, 1),)), pltpu.SemaphoreType.DMA((max(plan.n_remote, 1),)),
            pltpu.SemaphoreType.DMA((max(plan.n_local, 1),))]
    aliases = {n_in + i: n_out + o for i, o in plan.aliases.items()}
    return plan.ins, [_ANY] * len(plan.outs), plan.outs, sems, aliases


def _split_refs(refs, n_in, n_out, n_scr, plan):
    p_in, p_out = (len(plan.ins), len(plan.outs)) if plan is not None else (0, 0)
    refs = list(refs)
    ins, refs = refs[:n_in], refs[n_in:]
    pins, refs = refs[:p_in], refs[p_in:]
    outs, refs = refs[:n_out], refs[n_out:]
    pouts, refs = refs[:p_out], refs[p_out:]
    scr, psem = refs[:n_scr], refs[n_scr:]
    psem = tuple((lambda i, s=s: s.at[i]) for s in psem)
    return ins, outs, scr, (pins, pouts, psem)


def _grid_edge(grid, last):
    cond = None
    for ax, n in enumerate(grid):
        c = pl.program_id(ax) == (n - 1 if last else 0)
        cond = c if cond is None else cond & c
    return cond


def _plan_start(plan, pctx, grid):
    if plan is not None:
        pins, pouts, psem = pctx
        pl.when(_grid_edge(grid, False))(lambda: plan.start(pins, pouts, *psem))


def _plan_wait(plan, pctx, grid):
    if plan is not None:
        pins, pouts, psem = pctx
        pl.when(_grid_edge(grid, True))(lambda: plan.wait(pins, pouts, *psem))


def _run_plan(plan, *, name):
    def body(*refs):
        _, _, _, (pins, pouts, psem) = _split_refs(refs, 0, 0, 0, plan)
        plan.start(pins, pouts, *psem)
        plan.wait(pins, pouts, *psem)

    p_in, p_ospec, p_oshape, p_scr, p_alias = _plan_io(plan, 0, 0)
    return pl.pallas_call(body, name=name, in_specs=[_ANY] * len(p_in), out_specs=p_ospec, out_shape=p_oshape,
                          scratch_shapes=p_scr, input_output_aliases=p_alias)(*p_in)


def _fold8(x):
    return x.reshape(x.shape[0] // 8, 8, x.shape[1]).sum(axis=0)


def _ln_stats(r):
    mu = jnp.mean(r, -1, keepdims=True)
    xc = r - mu
    rstd = lax.rsqrt(jnp.mean(xc * xc, -1, keepdims=True) + EPS)
    return xc * rstd, rstd


def _sigmoid(x):
    return jax.nn.sigmoid(x)


def _gelu(x):
    return 0.5 * x * (1.0 + jnp.tanh(GELU_C * (x + GELU_A * x * x * x)))


def _gelu_grad(x):
    t = jnp.tanh(GELU_C * (x + GELU_A * x * x * x))
    return 0.5 * (1.0 + t) + 0.5 * x * (1.0 - t * t) * GELU_C * (1.0 + 3.0 * GELU_A * x * x)


MM_ROWS = 1024
DW_TOKENS = 4096


def _matmul(a, b, *, name, M, N, K, ta=False, tb=False, out_dtype=F32, tm=MM_ROWS, tn=1024, tk=1024,
            a_spec=None, b_spec=None, b_merge=None, out_shape=None, o_spec=None, into=None,
            a_sq=False, mul=None, add=None, add_scale=1.0, n_slots=False, plan=None):
    assert not n_slots or (K // min(tk, K) == 1 and add is None)
    tm, tn, tk = min(tm, M), min(tn, N), min(tk, K)
    assert M % tm == 0 and N % tn == 0 and K % tk == 0
    grid = (M // tm, N // tn, K // tk)
    nk = grid[2]
    if a_spec is None:
        a_spec = pl.BlockSpec((tk, tm), lambda i, j, k: (k, i)) if ta else pl.BlockSpec((tm, tk), lambda i, j, k: (i, k))
    if b_spec is None:
        b_spec = pl.BlockSpec((tn, tk), lambda i, j, k: (j, k)) if tb else pl.BlockSpec((tk, tn), lambda i, j, k: (k, j))
    if o_spec is None:
        o_spec = pl.BlockSpec((tm, tn), lambda i, j, k: (i, j))
        out_shape = jax.ShapeDtypeStruct((M, N), out_dtype)
    e_spec = pl.BlockSpec((tm, tn), lambda i, j, k: (i, j))
    dims = (((0 if ta else 1,), (1 if tb else 0,)), ((), ()))
    extra = [e for e in (mul, add, into) if e is not None]
    n_in = 2 + len(extra)

    def body(*refs):
        ins, outs, scr, pctx = _split_refs(refs, n_in, 1, 1 if nk > 1 else 0, plan)
        a_ref, b_ref = ins[0], ins[1]
        rest = list(ins[2:])
        mul_ref = rest.pop(0) if mul is not None else None
        add_ref = rest.pop(0) if add is not None else None
        o_ref = outs[0]
        _plan_start(plan, pctx, grid)
        av = a_ref[...].astype(BF16)
        if a_sq:
            av = av * av
        bv = b_ref[...]
        if b_merge is not None:
            bv = bv.reshape(b_merge)
        if n_slots:
            for s in range(bv.shape[0]):
                r = lax.dot_general(av, bv[s], dims, preferred_element_type=F32)
                w = r.shape[1]
                if mul_ref is not None:
                    r = r * (2.0 * mul_ref[:, s * w:(s + 1) * w].astype(F32))
                if o_ref.ndim == 3:
                    o_ref[s] = r.astype(o_ref.dtype)
                else:
                    o_ref[:, s * w:(s + 1) * w] = r.astype(o_ref.dtype)
            _plan_wait(plan, pctx, grid)
            return
        if bv.ndim == 3:
            w = av.shape[-1] // (1 if av.ndim == 3 else bv.shape[0])
            a_parts = [av[s] if av.ndim == 3 else av[:, s * w:(s + 1) * w] for s in range(bv.shape[0])]
            p = sum(lax.dot_general(a_parts[s], bv[s], dims, preferred_element_type=F32) for s in range(bv.shape[0]))
        else:
            p = lax.dot_general(av, bv, dims, preferred_element_type=F32)

        def finish(r):
            if mul_ref is not None:
                r = r * (2.0 * mul_ref[...].astype(F32))
            if add_ref is not None:
                r = r + add_scale * add_ref[...]
            o_ref[...] = r.astype(o_ref.dtype)

        if nk == 1:
            finish(p)
        else:
            acc_ref = scr[0]
            k = pl.program_id(2)

            @pl.when(k == 0)
            def _():
                acc_ref[...] = p

            @pl.when(k > 0)
            def _():
                acc_ref[...] += p

            @pl.when(k == nk - 1)
            def _():
                finish(acc_ref[...])

        _plan_wait(plan, pctx, grid)

    p_in, p_ospec, p_oshape, p_scr, p_alias = _plan_io(plan, n_in, 1)
    aliases = dict(p_alias)
    if into is not None:
        aliases[n_in - 1] = 0
    return pl.pallas_call(
        body, name=name, grid=grid,
        in_specs=[a_spec, b_spec] + [e_spec] * (len(extra) - (into is not None)) + [_ANY] * (into is not None)
        + [_ANY] * len(p_in),
        out_specs=[o_spec] + p_ospec, out_shape=[out_shape] + p_oshape,
        scratch_shapes=([pltpu.VMEM((tm, tn), F32)] if nk > 1 else []) + p_scr,
        input_output_aliases=aliases, compiler_params=_params(48, 3),
    )(*_hbm(a, b, *extra), *p_in)


def _rows4_spec(rowblk, n_axes):
    return pl.BlockSpec((4, 256, D), lambda *_: (0, rowblk, 0))


def _residual(h_ref, prev_refs):
    if not prev_refs:
        return h_ref[...]
    xhat, _ = _ln_stats(h_ref[...])
    return xhat * prev_refs[0][...] + prev_refs[1][...]


def _proj_ln(a_b, w, h_prev, g, b, *, name, prev_ln=(), w_rowblk=None, plan=None):
    T = a_b.shape[0]
    tm = min(MM_ROWS, T)
    grid = (T // tm,)
    row = pl.BlockSpec((tm, D), lambda i: (i, 0))
    vec = pl.BlockSpec((1, D), lambda i: (0, 0))
    w_spec = pl.BlockSpec((D, D), lambda i: (0, 0)) if w_rowblk is None else _rows4_spec(w_rowblk, 1)
    n_in = 5 + len(prev_ln)

    def body(*refs):
        ins, (r_ref, hb_ref), _, pctx = _split_refs(refs, n_in, 2, 0, plan)
        a_ref, w_ref, h_ref, g_ref, b_ref = ins[:5]
        _plan_start(plan, pctx, grid)
        mix = jnp.dot(a_ref[...], w_ref[...].reshape(D, D), preferred_element_type=F32)
        r = ALPHA * _residual(h_ref, ins[5:]) + mix
        xhat, _ = _ln_stats(r)
        r_ref[...] = r
        hb_ref[...] = (xhat * g_ref[...] + b_ref[...]).astype(BF16)
        _plan_wait(plan, pctx, grid)

    p_in, p_ospec, p_oshape, p_scr, p_alias = _plan_io(plan, n_in, 2)
    return pl.pallas_call(
        body, name=name, grid=grid,
        in_specs=[row, w_spec, row, vec, vec] + [vec] * len(prev_ln) + [_ANY] * len(p_in),
        out_specs=[row, row] + p_ospec,
        out_shape=[jax.ShapeDtypeStruct((T, D), F32), jax.ShapeDtypeStruct((T, D), BF16)] + p_oshape,
        scratch_shapes=p_scr, input_output_aliases=p_alias, compiler_params=_params(40, 1),
    )(*_hbm(a_b, w, h_prev, g, b, *prev_ln), *p_in)


def _ffn_ln(h_b, wbuf, h, g, b, *, name, prev_ln=(), plan=None):
    T = h_b.shape[0]
    slots = 2
    tm, tf = min(ROW_BLOCK, T), slots * 1024
    nf = 4 // slots
    F = nf * tf
    grid = (T // tm, nf)
    row = pl.BlockSpec((tm, D), lambda i, j: (i, 0))
    vec = pl.BlockSpec((1, D), lambda i, j: (0, 0))
    n_in = 6 + len(prev_ln)

    def body(*refs):
        ins, (ra_ref, r_ref, hbo_ref), (acc_ref,), pctx = _split_refs(refs, n_in, 3, 1, plan)
        hb_ref, w1_ref, w2_ref, h_ref, g_ref, b_ref = ins[:6]
        _plan_start(plan, pctx, grid)
        j = pl.program_id(1)
        hb = hb_ref[...]
        p = None
        for s in range(slots):
            ra = jnp.maximum(jnp.dot(hb, w1_ref[s], preferred_element_type=F32), 0.0)
            ra_ref[:, s * 1024:(s + 1) * 1024] = ra.astype(BF16)
            ps = jnp.dot((ra * ra).astype(BF16), w2_ref[s], preferred_element_type=F32)
            p = ps if p is None else p + ps

        @pl.when(j == 0)
        def _():
            acc_ref[...] = p

        @pl.when(j > 0)
        def _():
            acc_ref[...] += p

        @pl.when(j == nf - 1)
        def _():
            r = ALPHA * _residual(h_ref, ins[6:]) + acc_ref[...]
            xhat, _ = _ln_stats(r)
            r_ref[...] = r
            hbo_ref[...] = (xhat * g_ref[...] + b_ref[...]).astype(BF16)

        _plan_wait(plan, pctx, grid)

    p_in, p_ospec, p_oshape, p_scr, p_alias = _plan_io(plan, n_in, 3)
    return pl.pallas_call(
        body, name=name, grid=grid,
        in_specs=[row, pl.BlockSpec((slots, D, D), lambda i, j: (j, 0, 0)),
                  pl.BlockSpec((slots, D, D), lambda i, j: (j, 1, 0)), row, vec, vec] + [vec] * len(prev_ln)
        + [_ANY] * len(p_in),
        out_specs=[pl.BlockSpec((tm, tf), lambda i, j: (i, j)), row, row] + p_ospec,
        out_shape=[jax.ShapeDtypeStruct((T, F), BF16), jax.ShapeDtypeStruct((T, D), F32),
                   jax.ShapeDtypeStruct((T, D), BF16)] + p_oshape,
        scratch_shapes=[pltpu.VMEM((tm, D), F32)] + p_scr,
        input_output_aliases=p_alias, compiler_params=_params(56, 2),
    )(*_hbm(h_b, wbuf, wbuf, h, g, b, *prev_ln), *p_in)


def _ln_bwd(dy, r, g, *, name, loss_head=()):
    T = r.shape[0]
    tm = min(ROW_BLOCK, T)
    row = pl.BlockSpec((tm, D), lambda i: (i, 0))
    vec = pl.BlockSpec((1, D), lambda i: (0, 0))
    acc = pl.BlockSpec((8, D), lambda i: (0, 0))
    operands, in_specs = ([r, g, *loss_head], [row, vec, vec, row]) if loss_head else ([r, g, dy], [row, vec, row])
    n_in = len(operands)

    def body(*refs):
        r_ref, g_ref = refs[:2]
        dr_ref, drb_ref, dg_ref, db_ref = refs[n_in:n_in + 4]

        @pl.when(pl.program_id(0) == 0)
        def _():
            for ref in refs[n_in + 2:]:
                ref[...] = jnp.zeros_like(ref)

        xhat, rstd = _ln_stats(r_ref[...])
        if loss_head:
            err = xhat * g_ref[...] + refs[2][...] - refs[3][...]
            refs[n_in + 4][...] += _fold8(err * err)
            dy_ = err * (1.0 / D)
        else:
            dy_ = refs[2][...]
        dxh = dy_ * g_ref[...]
        m1 = jnp.mean(dxh, -1, keepdims=True)
        m2 = jnp.mean(dxh * xhat, -1, keepdims=True)
        dr = rstd * (dxh - m1 - xhat * m2)
        dr_ref[...] = dr
        drb_ref[...] = dr.astype(BF16)
        dg_ref[...] += _fold8(dy_ * xhat)
        db_ref[...] += _fold8(dy_)

    n_acc = 3 if loss_head else 2
    return pl.pallas_call(
        body, name=name, grid=(T // tm,), in_specs=in_specs, out_specs=[row, row] + [acc] * n_acc,
        out_shape=[jax.ShapeDtypeStruct((T, D), F32), jax.ShapeDtypeStruct((T, D), BF16)]
        + [jax.ShapeDtypeStruct((8, D), F32)] * n_acc,
        compiler_params=_params(40, 1),
    )(*_hbm(*operands))


def _rope(x, c, s1, s2):
    return x * c + pltpu.roll(x, 112, 1) * s1 + pltpu.roll(x, 16, 1) * s2


def _rope_t(dy, c, s1, s2):
    return dy * c + pltpu.roll(dy * s1, 16, 1) + pltpu.roll(dy * s2, 112, 1)


def _rms(x, g):
    rstd = lax.rsqrt(jnp.mean(x * x, -1, keepdims=True) + EPS)
    xhat = x * rstd
    return xhat * g, xhat, rstd


def _mla_prep(z0, gq, gkv, wq, wk, wv, rc, rs1, rs2):
    T = z0.shape[0]
    tm = min(ROW_BLOCK, T)
    HW = HEADS * 128

    def body(cq_ref, ckv_ref, kr_ref, gq_ref, gkv_ref, wq_ref, wk_ref, wv_ref, c_ref, s1_ref, s2_ref,
             q_ref, k_ref, v_ref):
        nq = _rms(cq_ref[...], gq_ref[...])[0].astype(BF16)
        nkv = _rms(ckv_ref[...], gkv_ref[...])[0].astype(BF16)
        q = jnp.dot(nq, wq_ref[...], preferred_element_type=F32)
        k = jnp.dot(nkv, wk_ref[...], preferred_element_type=F32)
        v = jnp.dot(nkv, wv_ref[...], preferred_element_type=F32)
        c, s1, s2 = c_ref[...], s1_ref[...], s2_ref[...]
        kr = _rope(pltpu.roll(kr_ref[...], 64, 1), c, s1, s2)
        for h in range(HEADS):
            sl = slice(h * 128, (h + 1) * 128)
            q_ref[:, sl] = (_rope(q[:, sl], c, s1, s2) * QK_SCALE).astype(BF16)
            k_ref[:, sl] = (k[:, sl] + kr).astype(BF16)
        v_ref[...] = v.astype(BF16)

    full = lambda shape: pl.BlockSpec(shape, lambda i: (0, 0))
    tab = pl.BlockSpec((tm, 128), lambda i: (i, 0))
    return pl.pallas_call(
        body, name="mla_prep", grid=(T // tm,),
        in_specs=[pl.BlockSpec((tm, 256), lambda i: (i, 0)), pl.BlockSpec((tm, 256), lambda i: (i, 1)),
                  pl.BlockSpec((tm, 128), lambda i: (i, 12)), full((1, 256)), full((1, 256)),
                  full((256, HW)), full((256, HW)), full((256, 512)), tab, tab, tab],
        out_specs=[pl.BlockSpec((tm, HW), lambda i: (i, 0)), pl.BlockSpec((tm, HW), lambda i: (i, 0)),
                   pl.BlockSpec((tm, 512), lambda i: (i, 0))],
        out_shape=[jax.ShapeDtypeStruct((T, HW), BF16), jax.ShapeDtypeStruct((T, HW), BF16),
                   jax.ShapeDtypeStruct((T, 512), BF16)],
        compiler_params=_params(40, 1),
    )(z0, z0, z0, gq, gkv, wq, wk, wv, rc, rs1, rs2)


def _flash_fwd(q, k, v, plan=None):
    T = q.shape[0]
    bq = min(2 * ROW_BLOCK, T)
    nq = T // bq
    pairs = [(i, j) for i in range(nq) for j in range(i + 1)]
    imap, jmap = (jnp.array(m, jnp.int32) for m in zip(*pairs))
    grid = (4, len(pairs))

    def body(imap_ref, jmap_ref, *refs):
        (q_ref, k_ref, v_ref), (o_ref, lse_ref), (m_sc, acc_sc), pctx = _split_refs(refs, 3, 2, 2, plan)
        _plan_start(plan, pctx, grid)
        i, j = imap_ref[pl.program_id(1)], jmap_ref[pl.program_id(1)]
        first = lax.broadcasted_iota(jnp.int32, (bq, 128), 1) < 64

        @pl.when(j == 0)
        def _():
            m_sc[...] = jnp.full_like(m_sc, -jnp.inf)
            acc_sc[...] = jnp.zeros_like(acc_sc)

        def step(masked):
            vp = v_ref[...]
            for h in range(2):
                sl = slice(h * 128, (h + 1) * 128)
                s = lax.dot_general(q_ref[:, sl], k_ref[:, sl], NT_DIMS, preferred_element_type=F32)
                if masked:
                    rows = lax.broadcasted_iota(jnp.int32, (bq, bq), 0)
                    cols = lax.broadcasted_iota(jnp.int32, (bq, bq), 1)
                    s = jnp.where(cols <= rows, s, -jnp.inf)
                m_prev = m_sc[h, :, 0:1]
                m_new = jnp.maximum(m_prev, jnp.max(s, axis=1, keepdims=True))
                alpha = jnp.exp(m_prev - m_new)
                p = jnp.exp(s - m_new).astype(BF16)
                vh = jnp.where(first if h == 0 else jnp.logical_not(first), vp, jnp.ones_like(vp))
                acc_sc[h] = acc_sc[h] * alpha + jnp.dot(p, vh, preferred_element_type=F32)
                m_sc[h] = jnp.broadcast_to(m_new, (bq, 128))

        @pl.when(j < i)
        def _():
            step(False)

        @pl.when(j == i)
        def _():
            step(True)
            a0, a1 = acc_sc[0], acc_sc[1]
            l0, l1 = pltpu.roll(a0, 64, 1), pltpu.roll(a1, 64, 1)
            o_ref[...] = jnp.where(first, a0 / l0, a1 / l1).astype(BF16)
            lse_ref[...] = jnp.where(first, m_sc[0] + jnp.log(l0), m_sc[1] + jnp.log(l1))

        _plan_wait(plan, pctx, grid)

    qi = lambda hp, t, im, jm: (im[t], hp)
    kj = lambda hp, t, im, jm: (jm[t], hp)
    p_in, p_ospec, p_oshape, p_scr, p_alias = _plan_io(plan, 2 + 3, 2)
    return pl.pallas_call(
        body, name="flash_fwd",
        out_shape=[jax.ShapeDtypeStruct((T, 512), BF16), jax.ShapeDtypeStruct((T, 512), F32)] + p_oshape,
        grid_spec=pltpu.PrefetchScalarGridSpec(
            num_scalar_prefetch=2, grid=grid,
            in_specs=[pl.BlockSpec((bq, 256), qi), pl.BlockSpec((bq, 256), kj), pl.BlockSpec((bq, 128), kj)]
            + [_ANY] * len(p_in),
            out_specs=[pl.BlockSpec((bq, 128), qi), pl.BlockSpec((bq, 128), qi)] + p_ospec,
            scratch_shapes=[pltpu.VMEM((2, bq, 128), F32), pltpu.VMEM((2, bq, 128), F32)] + p_scr),
        input_output_aliases=p_alias, compiler_params=_params(56, 2),
    )(imap, jmap, *_hbm(q, k, v), *p_in)


def _attn_delta(dmix, o):
    T = o.shape[0]
    tm = min(ROW_BLOCK, T)
    blk = pl.BlockSpec((tm, 512), lambda i: (i, 0))

    def body(do_ref, o_ref, delta_ref, dob_ref):
        first = lax.broadcasted_iota(jnp.int32, (tm, 128), 1) < 64
        for hp in range(4):
            sl = slice(hp * 128, (hp + 1) * 128)
            prod = do_ref[:, sl] * o_ref[:, sl].astype(F32)
            d0 = jnp.sum(jnp.where(first, prod, 0.0), axis=1, keepdims=True)
            d1 = jnp.sum(jnp.where(first, 0.0, prod), axis=1, keepdims=True)
            delta_ref[:, sl] = jnp.where(first, d0, d1)
        dob_ref[...] = do_ref[...].astype(BF16)

    return pl.pallas_call(
        body, name="attn_delta", grid=(T // tm,), in_specs=[blk, blk], out_specs=[blk, blk],
        out_shape=[jax.ShapeDtypeStruct((T, 512), F32), jax.ShapeDtypeStruct((T, 512), BF16)],
        compiler_params=_params(32, 1),
    )(dmix, o)


def _flash_bwd(q, k, v, do_b, lse, delta, plan=None):
    T = q.shape[0]
    bq = min(2 * ROW_BLOCK, T)
    nq = T // bq
    pairs = [(i, j) for j in range(nq) for i in range(j, nq)]
    imap, jmap = (jnp.array(m, jnp.int32) for m in zip(*pairs))
    grid = (4, len(pairs))

    def body(imap_ref, jmap_ref, *refs):
        ((q_ref, k_ref, v_ref, do_ref, lse_ref, dl_ref), (dq_hbm, dk_ref, dv_ref), (dq_sc, dk_sc, dv_sc, sem),
         pctx) = _split_refs(refs, 6, 3, 4, plan)
        _plan_start(plan, pctx, grid)
        hp = pl.program_id(0)
        i, j = imap_ref[pl.program_id(1)], jmap_ref[pl.program_id(1)]
        first = lax.broadcasted_iota(jnp.int32, (bq, 128), 1) < 64

        @pl.when((j == 0) & (i == 0))
        def _():
            dq_sc[...] = jnp.zeros_like(dq_sc)

        @pl.when(i == j)
        def _():
            dk_sc[...] = jnp.zeros_like(dk_sc)
            dv_sc[...] = jnp.zeros_like(dv_sc)

        def tile(r0, nr, nc, masked):
            rs, cs = slice(r0, r0 + nr), slice(0, nc)
            vp = v_ref[cs, :]
            do = do_ref[rs, :]
            lanes = first[rs, :]
            for h in range(2):
                sl = slice(h * 128, (h + 1) * 128)
                qh, kh = q_ref[rs, sl], k_ref[cs, sl]
                s = lax.dot_general(qh, kh, NT_DIMS, preferred_element_type=F32)
                p = jnp.exp(s - lse_ref[rs, h * 64:h * 64 + 1])
                if masked:
                    rows = r0 + lax.broadcasted_iota(jnp.int32, (nr, nc), 0)
                    cols = lax.broadcasted_iota(jnp.int32, (nr, nc), 1)
                    p = jnp.where(cols <= rows, p, 0.0)
                do_h = jnp.where(lanes if h == 0 else jnp.logical_not(lanes), do, jnp.zeros_like(do))
                dv_sc[cs, :] += lax.dot_general(p.astype(BF16), do_h, TN_DIMS, preferred_element_type=F32)
                dp = lax.dot_general(do_h, vp, NT_DIMS, preferred_element_type=F32)
                ds = (p * (dp - dl_ref[rs, h * 64:h * 64 + 1])).astype(BF16)
                dq_sc[i, rs, sl] += jnp.dot(ds, kh, preferred_element_type=F32)
                dk_sc[cs, sl] += lax.dot_general(ds, qh, TN_DIMS, preferred_element_type=F32)

        @pl.when(i > j)
        def _():
            tile(0, bq, bq, False)

        @pl.when(i == j)
        def _():
            tile(0, bq // 2, bq // 2, True)
            tile(bq // 2, bq // 2, bq, True)

        @pl.when(i == nq - 1)
        def _():
            dk_ref[...] = dk_sc[...]
            dv_ref[...] = dv_sc[...]

        @pl.when((j == nq - 1) & (i == nq - 1))
        def _():
            cp = pltpu.make_async_copy(dq_sc, dq_hbm.at[hp], sem)
            cp.start()
            cp.wait()

        _plan_wait(plan, pctx, grid)

    qi = lambda hp, t, im, jm: (im[t], hp)
    kj = lambda hp, t, im, jm: (jm[t], hp)
    p_in, p_ospec, p_oshape, p_scr, p_alias = _plan_io(plan, 2 + 6, 3)
    return pl.pallas_call(
        body, name="flash_bwd",
        out_shape=[jax.ShapeDtypeStruct((4, nq, bq, 256), F32), jax.ShapeDtypeStruct((T, 1024), F32),
                   jax.ShapeDtypeStruct((T, 512), F32)] + p_oshape,
        grid_spec=pltpu.PrefetchScalarGridSpec(
            num_scalar_prefetch=2, grid=grid,
            in_specs=[pl.BlockSpec((bq, 256), qi), pl.BlockSpec((bq, 256), kj), pl.BlockSpec((bq, 128), kj),
                      pl.BlockSpec((bq, 128), qi), pl.BlockSpec((bq, 128), qi), pl.BlockSpec((bq, 128), qi)]
            + [_ANY] * len(p_in),
            out_specs=[_ANY, pl.BlockSpec((bq, 256), kj), pl.BlockSpec((bq, 128), kj)] + p_ospec,
            scratch_shapes=[pltpu.VMEM((nq, bq, 256), F32), pltpu.VMEM((bq, 256), F32), pltpu.VMEM((bq, 128), F32),
                            pltpu.SemaphoreType.DMA] + p_scr),
        input_output_aliases=p_alias, compiler_params=_params(56, 2),
    )(imap, jmap, *_hbm(q, k, v, do_b, lse, delta), *p_in)


def _mla_bwd(z0, dq4, dk, dv, gq, gkv, wq, wk, wv, rc, rs1, rs2, plan=None):
    T = z0.shape[0]
    tm = min(ROW_BLOCK, T)
    HW = HEADS * 128
    grid = (T // tm,)
    dq4 = dq4.reshape(4, T, 256)

    def body(*refs):
        ((cq_ref, ckv_ref, dq_ref, dk_ref, dv_ref, gq_ref, gkv_ref, wq_ref, wk_ref, wv_ref, c_ref, s1_ref, s2_ref),
         (dc_ref, dkr_ref, dwq_ref, dwk_ref, dwv_ref, dgq_ref, dgkv_ref), _, pctx) = _split_refs(refs, 13, 7, 0, plan)
        _plan_start(plan, pctx, grid)

        @pl.when(pl.program_id(0) == 0)
        def _():
            for ref in (dwq_ref, dwk_ref, dwv_ref, dgq_ref, dgkv_ref):
                ref[...] = jnp.zeros_like(ref)

        c, s1, s2 = c_ref[...], s1_ref[...], s2_ref[...]
        lane = lax.broadcasted_iota(jnp.int32, (tm, 128), 1)
        nq, xq, rq = _rms(cq_ref[...], gq_ref[...])
        nkv, xkv, rkv = _rms(ckv_ref[...], gkv_ref[...])
        nq_b, nkv_b = nq.astype(BF16), nkv.astype(BF16)

        dq_parts, dk_parts = [], []
        dkr = jnp.zeros((tm, 128), F32)
        for h in range(HEADS):
            blk = dq_ref[h // 2, :, (h % 2) * 128:(h % 2 + 1) * 128] * QK_SCALE
            dq_parts.append(_rope_t(blk, c, s1, s2).astype(BF16))
            kb = dk_ref[:, h * 128:(h + 1) * 128]
            dk_parts.append(jnp.where(lane < NOPE, kb, 0.0).astype(BF16))
            dkr = dkr + kb
        dq_b = jnp.concatenate(dq_parts, axis=1)
        dk_b = jnp.concatenate(dk_parts, axis=1)
        dv_b = dv_ref[...].astype(BF16)

        dwq_ref[...] += lax.dot_general(nq_b, dq_b, TN_DIMS, preferred_element_type=F32)
        dwk_ref[...] += lax.dot_general(nkv_b, dk_b, TN_DIMS, preferred_element_type=F32)
        dwv_ref[...] += lax.dot_general(nkv_b, dv_b, TN_DIMS, preferred_element_type=F32)
        dnq = lax.dot_general(dq_b, wq_ref[...], NT_DIMS, preferred_element_type=F32)
        dnkv = (lax.dot_general(dk_b, wk_ref[...], NT_DIMS, preferred_element_type=F32)
                + lax.dot_general(dv_b, wv_ref[...], NT_DIMS, preferred_element_type=F32))

        def rms_bwd(dn, xhat, rstd, g):
            dxh = dn * g
            return rstd * (dxh - xhat * jnp.mean(dxh * xhat, -1, keepdims=True))

        dc_ref[:, :256] = rms_bwd(dnq, xq, rq, gq_ref[...]).astype(BF16)
        dc_ref[:, 256:] = rms_bwd(dnkv, xkv, rkv, gkv_ref[...]).astype(BF16)
        dgq_ref[...] += _fold8(dnq * xq)
        dgkv_ref[...] += _fold8(dnkv * xkv)
        dkr = pltpu.roll(_rope_t(dkr, c, s1, s2), 64, 1)
        dkr_ref[...] = jnp.where(lane < ROPE, dkr, 0.0).astype(BF16)
        _plan_wait(plan, pctx, grid)

    full = lambda shape: pl.BlockSpec(shape, lambda i: (0,) * len(shape))
    tab = pl.BlockSpec((tm, 128), lambda i: (i, 0))
    p_in, p_ospec, p_oshape, p_scr, p_alias = _plan_io(plan, 13, 7)
    return pl.pallas_call(
        body, name="mla_bwd", grid=grid,
        in_specs=[pl.BlockSpec((tm, 256), lambda i: (i, 0)), pl.BlockSpec((tm, 256), lambda i: (i, 1)),
                  pl.BlockSpec((4, tm, 256), lambda i: (0, i, 0)),
                  pl.BlockSpec((tm, HW), lambda i: (i, 0)), pl.BlockSpec((tm, 512), lambda i: (i, 0)),
                  full((1, 256)), full((1, 256)), full((256, HW)), full((256, HW)), full((256, 512)), tab, tab, tab]
        + [_ANY] * len(p_in),
        out_specs=[pl.BlockSpec((tm, 512), lambda i: (i, 0)), tab, full((256, HW)), full((256, HW)),
                   full((256, 512)), full((8, 256)), full((8, 256))] + p_ospec,
        out_shape=[jax.ShapeDtypeStruct((T, 512), BF16), jax.ShapeDtypeStruct((T, 128), BF16),
                   jax.ShapeDtypeStruct((256, HW), F32), jax.ShapeDtypeStruct((256, HW), F32),
                   jax.ShapeDtypeStruct((256, 512), F32), jax.ShapeDtypeStruct((8, 256), F32),
                   jax.ShapeDtypeStruct((8, 256), F32)] + p_oshape,
        scratch_shapes=p_scr, input_output_aliases=p_alias, compiler_params=_params(48, 1),
    )(*_hbm(z0, z0, dq4, dk, dv, gq, gkv, wq, wk, wv, rc, rs1, rs2), *p_in)


def _sgu_fwd(z0, a_out, ln_g, ln_b, w, b_t):
    T = z0.shape[0]
    tm = min(ROW_BLOCK, T)
    W = SGU_G * SGU_C

    def body(u_ref, v_ref, a_ref, g_ref, b_ref, w_ref, bt_ref, o_ref):
        o_ref[:, :W] = a_ref[...]
        ug = _gelu(u_ref[...])
        xhat, _ = _ln_stats(_gelu(v_ref[...]))
        vn = (xhat * g_ref[...] + b_ref[...]).astype(BF16)
        tril = lax.broadcasted_iota(jnp.int32, (SGU_C, SGU_C), 0) >= lax.broadcasted_iota(jnp.int32, (SGU_C, SGU_C), 1)
        for g in range(SGU_G):
            cs = slice(g * SGU_C, (g + 1) * SGU_C)
            wg = jnp.where(tril, w_ref[g], 0.0).astype(BF16)
            bcol = bt_ref[:, g:g + 1]
            for c in range(tm // SGU_C):
                rs = slice(c * SGU_C, (c + 1) * SGU_C)
                mixed = jnp.dot(wg, vn[rs, cs], preferred_element_type=F32) + bcol
                o_ref[rs, W + g * SGU_C:W + (g + 1) * SGU_C] = (ug[rs, cs] * mixed).astype(BF16)

    full = lambda shape: pl.BlockSpec(shape, lambda i: (0,) * len(shape))
    return pl.pallas_call(
        body, name="sgu_fwd", grid=(T // tm,),
        in_specs=[pl.BlockSpec((tm, W), lambda i: (i, 1)), pl.BlockSpec((tm, W), lambda i: (i, 2)),
                  pl.BlockSpec((tm, W), lambda i: (i, 0)),
                  full((1, W)), full((1, W)), full((SGU_G, SGU_C, SGU_C)), full((SGU_C, SGU_G))],
        out_specs=pl.BlockSpec((tm, 2 * W), lambda i: (i, 0)),
        out_shape=jax.ShapeDtypeStruct((T, 2 * W), BF16),
        compiler_params=_params(32, 1),
    )(z0, z0, a_out, ln_g, ln_b, w, b_t)


def _sgu_bwd(z0, dmix, dc, dkr, ln_g, ln_b, w, b_t, plan=None):
    T = z0.shape[0]
    tm = min(ROW_BLOCK, T)
    W = SGU_G * SGU_C
    grid = (T // tm,)

    def body(*refs):
        ((u_ref, v_ref, do_ref, dc_ref, dkr_ref, g_ref, b_ref, w_ref, bt_ref),
         (dz_ref, dw_ref, db_ref, dlg_ref, dlb_ref), _, pctx) = _split_refs(refs, 9, 5, 0, plan)
        _plan_start(plan, pctx, grid)

        @pl.when(pl.program_id(0) == 0)
        def _():
            for ref in (dw_ref, db_ref, dlg_ref, dlb_ref):
                ref[...] = jnp.zeros_like(ref)

        dz_ref[:, :W] = dc_ref[...]
        dz_ref[:, 3 * W:] = dkr_ref[...]

        u, v, dout = u_ref[...], v_ref[...], do_ref[...]
        ug = _gelu(u)
        xhat, rstd = _ln_stats(_gelu(v))
        vn = (xhat * g_ref[...] + b_ref[...]).astype(BF16)
        dmixed = dout * ug
        dmixed_b = dmixed.astype(BF16)
        tril = lax.broadcasted_iota(jnp.int32, (SGU_C, SGU_C), 0) >= lax.broadcasted_iota(jnp.int32, (SGU_C, SGU_C), 1)
        lane = lax.broadcasted_iota(jnp.int32, (SGU_C, SGU_C), 1)
        dvn_cols = []
        for g in range(SGU_G):
            cs = slice(g * SGU_C, (g + 1) * SGU_C)
            wg = jnp.where(tril, w_ref[g], 0.0).astype(BF16)
            bcol = bt_ref[:, g:g + 1]
            dw_g = jnp.zeros((SGU_C, SGU_C), F32)
            db_g = jnp.zeros((SGU_C, 1), F32)
            dvn_rows = []
            for c in range(tm // SGU_C):
                rs = slice(c * SGU_C, (c + 1) * SGU_C)
                mixed = jnp.dot(wg, vn[rs, cs], preferred_element_type=F32) + bcol
                dz_ref[rs, W + g * SGU_C:W + (g + 1) * SGU_C] = (dout[rs, cs] * mixed * _gelu_grad(u[rs, cs])).astype(BF16)
                dm = dmixed_b[rs, cs]
                dw_g = dw_g + lax.dot_general(dm, vn[rs, cs], NT_DIMS, preferred_element_type=F32)
                db_g = db_g + jnp.sum(dmixed[rs, cs], axis=1, keepdims=True)
                dvn_rows.append(lax.dot_general(wg, dm, TN_DIMS, preferred_element_type=F32))
            dw_ref[g] += jnp.where(tril, dw_g, 0.0)
            db_ref[...] += jnp.where(lane == g, db_g, 0.0)
            dvn_cols.append(jnp.concatenate(dvn_rows, axis=0))
        dvn = jnp.concatenate(dvn_cols, axis=1)
        dxh = dvn * g_ref[...]
        m1 = jnp.mean(dxh, -1, keepdims=True)
        m2 = jnp.mean(dxh * xhat, -1, keepdims=True)
        dvg = rstd * (dxh - m1 - xhat * m2)
        dz_ref[:, 2 * W:3 * W] = (dvg * _gelu_grad(v)).astype(BF16)
        dlg_ref[...] += _fold8(dvn * xhat)
        dlb_ref[...] += _fold8(dvn)
        _plan_wait(plan, pctx, grid)

    full = lambda shape: pl.BlockSpec(shape, lambda i: (0,) * len(shape))
    p_in, p_ospec, p_oshape, p_scr, p_alias = _plan_io(plan, 9, 5)
    return pl.pallas_call(
        body, name="sgu_bwd", grid=grid,
        in_specs=[pl.BlockSpec((tm, W), lambda i: (i, 1)), pl.BlockSpec((tm, W), lambda i: (i, 2)),
                  pl.BlockSpec((tm, W), lambda i: (i, 1)), pl.BlockSpec((tm, W), lambda i: (i, 0)),
                  pl.BlockSpec((tm, 128), lambda i: (i, 0)),
                  full((1, W)), full((1, W)), full((SGU_G, SGU_C, SGU_C)), full((SGU_C, SGU_G))] + [_ANY] * len(p_in),
        out_specs=[pl.BlockSpec((tm, 3 * W + 128), lambda i: (i, 0)), full((SGU_G, SGU_C, SGU_C)),
                   full((SGU_C, SGU_C)), full((8, W)), full((8, W))] + p_ospec,
        out_shape=[jax.ShapeDtypeStruct((T, 3 * W + 128), BF16), jax.ShapeDtypeStruct((SGU_G, SGU_C, SGU_C), F32),
                   jax.ShapeDtypeStruct((SGU_C, SGU_C), F32), jax.ShapeDtypeStruct((8, W), F32),
                   jax.ShapeDtypeStruct((8, W), F32)] + p_oshape,
        scratch_shapes=p_scr, input_output_aliases=p_alias, compiler_params=_params(40, 1),
    )(z0, z0, dmix, dc, dkr, ln_g, ln_b, w, b_t, *p_in)


def _hg_lower_bound(lb_ref):
    a0, a1 = lb_ref[0:1, :], lb_ref[1:2, :]
    m = jnp.maximum(a0, a1)
    e0, e1 = jnp.exp(a0 - m), jnp.exp(a1 - m)
    return e1 / (e0 + e1)


def _running_sum(x, reverse=False):
    n = x.shape[0]
    row = lax.broadcasted_iota(jnp.int32, x.shape, 0)
    s = 1
    while s < n:
        if reverse:
            x = x + jnp.where(row < n - s, pltpu.roll(x, n - s, 0), 0.0)
        else:
            x = x + jnp.where(row >= s, pltpu.roll(x, s, 0), 0.0)
        s *= 2
    return x


def _hg_chunk(qc, fc, lb):
    C = HG_CHUNK
    rows = lax.broadcasted_iota(jnp.int32, (C, C), 0)
    cols = lax.broadcasted_iota(jnp.int32, (C, C), 1)
    rowid = lax.broadcasted_iota(jnp.int32, (C, 128), 0)
    sq, sg = _sigmoid(qc), _sigmoid(fc)
    qf = qc * sq
    gate = lb + (1.0 - lb) * sg
    kk = 1.0 - gate
    lg = jnp.log(gate)
    bcum = _running_sum(lg)
    b_mid = jnp.sum(jnp.where(rowid < C // 2, lg, 0.0), axis=0, keepdims=True)
    b_last = jnp.sum(lg, axis=0, keepdims=True)
    eq, ek, e, eh = jnp.exp(bcum - b_mid), jnp.exp(b_mid - bcum), jnp.exp(bcum), jnp.exp(b_last - bcum)
    qt, kt, qe, khat = qf * eq, kk * ek, qf * e, kk * eh
    a = lax.dot_general(qt.astype(BF16), kt.astype(BF16), NT_DIMS, preferred_element_type=F32)
    a = jnp.where(rows >= cols, a, 0.0)
    return dict(sq=sq, sg=sg, gate=gate, kk=kk, eq=eq, ek=ek, e=e, eh=eh, qt=qt, kt=kt, qe=qe, khat=khat, a=a,
                e_last=jnp.exp(b_last), tril=rows >= cols, rowid=rowid)


def _hgrn_fwd(z4, hg_lb, gnorm):
    T = z4.shape[1]
    tb = min(ROW_BLOCK, T)
    C = HG_CHUNK
    ncb = tb // C
    HPB = HG_HEADS_PER_STEP

    def body(q_ref, f_ref, i_ref, g_ref, lb_ref, gn_ref, y_ref, o_ref, st_ref, st_sc):
        @pl.when(pl.program_id(1) == 0)
        def _():
            st_sc[...] = jnp.zeros_like(st_sc)

        def chunk(c, carry):
            rs = pl.ds(pl.multiple_of(c * C, C), C)
            for hh in range(HPB):
                hs = slice(hh * 128, (hh + 1) * 128)
                lb = _hg_lower_bound(lb_ref.at[:, hs])
                v_b = i_ref[rs, hs].astype(BF16)
                gc = g_ref[rs, hs]
                x = _hg_chunk(q_ref[rs, hs], f_ref[rs, hs], lb)
                st = st_sc[hh]
                st_ref[hh, c] = st
                o = (jnp.dot(x["a"].astype(BF16), v_b, preferred_element_type=F32)
                     + lax.dot_general(x["qe"].astype(BF16), st.astype(BF16), NT_DIMS, preferred_element_type=F32))
                st_sc[hh] = st * x["e_last"] + lax.dot_general(v_b, x["khat"].astype(BF16), TN_DIMS,
                                                               preferred_element_type=F32)
                o_ref[rs, hs] = o
                n = o * lax.rsqrt(jnp.mean(o * o, -1, keepdims=True) + EPS)
                y_ref[rs, hs] = (n * gn_ref[:, hs] * (gc * _sigmoid(gc))).astype(BF16)
            return carry

        lax.fori_loop(0, ncb, chunk, 0, unroll=4)

    W = 128 * HPB
    zb = lambda k: pl.BlockSpec((None, tb, W), lambda h, t: (k, t, h))
    out = pl.BlockSpec((tb, W), lambda h, t: (t, h))
    return pl.pallas_call(
        body, name="hgrn_fwd", grid=(HEADS // HPB, T // tb),
        in_specs=[zb(0), zb(1), zb(2), zb(3), pl.BlockSpec((2, W), lambda h, t: (0, h)),
                  pl.BlockSpec((1, W), lambda h, t: (0, h))],
        out_specs=[out, out, pl.BlockSpec((HPB, ncb, 128, 128), lambda h, t: (h, t, 0, 0))],
        out_shape=[jax.ShapeDtypeStruct((T, D), BF16), jax.ShapeDtypeStruct((T, D), F32),
                   jax.ShapeDtypeStruct((HEADS, T // C, 128, 128), F32)],
        scratch_shapes=[pltpu.VMEM((HPB, 128, 128), F32)],
        compiler_params=_params(48, 2),
    )(*_hbm(z4, z4, z4, z4, hg_lb, gnorm))


def _hgrn_bwd(z4, o_raw, dy, states, hg_lb, gnorm):
    T = z4.shape[1]
    tb = min(ROW_BLOCK, T)
    C = HG_CHUNK
    ncb = tb // C
    nt = T // tb
    HPB = HG_HEADS_PER_STEP

    def body(q_ref, f_ref, i_ref, g_ref, o_ref, dy_ref, st_ref, lb_ref, gn_ref, dz_ref, dlb_ref, dgn_ref, dst_sc):
        @pl.when(pl.program_id(1) == 0)
        def _():
            dst_sc[...] = jnp.zeros_like(dst_sc)
            dlb_ref[...] = jnp.zeros_like(dlb_ref)
            dgn_ref[...] = jnp.zeros_like(dgn_ref)

        def chunk(cc, carry):
            for hh in range(HPB):
                one_head(ncb - 1 - cc, hh, slice(hh * 128, (hh + 1) * 128))
            return carry

        def one_head(c, hh, hs):
            rs = pl.ds(pl.multiple_of(c * C, C), C)
            lb = _hg_lower_bound(lb_ref.at[:, hs])
            gn = gn_ref[:, hs]
            qc, gc = q_ref[rs, hs], g_ref[rs, hs]
            v_b = i_ref[rs, hs].astype(BF16)
            x = _hg_chunk(qc, f_ref[rs, hs], lb)
            st, dst = st_ref[hh, c], dst_sc[hh]
            st_b, dst_b = st.astype(BF16), dst.astype(BF16)
            o, dyc = o_ref[rs, hs], dy_ref[rs, hs]
            sgg = _sigmoid(gc)
            sil = gc * sgg
            rstd = lax.rsqrt(jnp.mean(o * o, -1, keepdims=True) + EPS)
            n = o * rstd
            dgn_ref[:, hs] += _fold8(dyc * n * sil)
            dn = dyc * gn * sil
            do = rstd * (dn - n * jnp.mean(dn * n, -1, keepdims=True))
            dg = dyc * n * gn * (sgg * (1.0 + gc * (1.0 - sgg)))
            do_b = do.astype(BF16)
            da = jnp.where(x["tril"], lax.dot_general(do_b, v_b, NT_DIMS, preferred_element_type=F32), 0.0).astype(BF16)
            qt_b, kt_b, qe_b, khat_b = (x[n_].astype(BF16) for n_ in ("qt", "kt", "qe", "khat"))
            dv = (lax.dot_general(x["a"].astype(BF16), do_b, TN_DIMS, preferred_element_type=F32)
                  + lax.dot_general(khat_b, dst_b, NT_DIMS, preferred_element_type=F32))
            dqt = jnp.dot(da, kt_b, preferred_element_type=F32)
            dqe = jnp.dot(do_b, st_b, preferred_element_type=F32)
            dkt = lax.dot_general(da, qt_b, TN_DIMS, preferred_element_type=F32)
            dkhat = jnp.dot(v_b, dst_b, preferred_element_type=F32)
            dst_sc[hh] = lax.dot_general(do_b, qe_b, TN_DIMS, preferred_element_type=F32) + dst * x["e_last"]
            de_last = jnp.sum(st * dst, axis=0, keepdims=True)
            dqf = dqt * x["eq"] + dqe * x["e"]
            dkk = dkt * x["ek"] + dkhat * x["eh"]
            dkh_kh = dkhat * x["khat"]
            db = dqt * qt_b.astype(F32) - dkt * kt_b.astype(F32) + dqe * x["qe"] - dkh_kh
            db_last = jnp.sum(dkh_kh, axis=0, keepdims=True) + de_last * x["e_last"]
            db = db + jnp.where(x["rowid"] == C - 1, db_last, 0.0)
            dlg = _running_sum(db, reverse=True)
            dgate = dlg / x["gate"] - dkk
            sg, sq = x["sg"], x["sq"]
            dlb_ref[:, hs] += _fold8(dgate * (1.0 - sg)) * (lb * (1.0 - lb))
            dz_ref[0, rs, hs] = (dqf * (sq * (1.0 + qc * (1.0 - sq)))).astype(BF16)
            dz_ref[1, rs, hs] = (dgate * (1.0 - lb) * sg * (1.0 - sg)).astype(BF16)
            dz_ref[2, rs, hs] = dv.astype(BF16)
            dz_ref[3, rs, hs] = dg.astype(BF16)

        lax.fori_loop(0, ncb, chunk, 0, unroll=4)

    W = 128 * HPB
    zb = lambda k: pl.BlockSpec((None, tb, W), lambda h, t: (k, nt - 1 - t, h))
    blk = pl.BlockSpec((tb, W), lambda h, t: (nt - 1 - t, h))
    acc = pl.BlockSpec((8, W), lambda h, t: (0, h))
    return pl.pallas_call(
        body, name="hgrn_bwd", grid=(HEADS // HPB, nt),
        in_specs=[zb(0), zb(1), zb(2), zb(3), blk, blk,
                  pl.BlockSpec((HPB, ncb, 128, 128), lambda h, t: (h, nt - 1 - t, 0, 0)),
                  pl.BlockSpec((2, W), lambda h, t: (0, h)), pl.BlockSpec((1, W), lambda h, t: (0, h))],
        out_specs=[pl.BlockSpec((4, tb, W), lambda h, t: (0, nt - 1 - t, h)), acc, acc],
        out_shape=[jax.ShapeDtypeStruct((4, T, D), BF16), jax.ShapeDtypeStruct((8, D), F32),
                   jax.ShapeDtypeStruct((8, D), F32)],
        scratch_shapes=[pltpu.VMEM((HPB, 128, 128), F32)],
        compiler_params=_params(48, 2),
    )(*_hbm(z4, z4, z4, z4, o_raw, dy, states, hg_lb, gnorm))


def _adamw(w, g, m, v, *, name):
    R, L = w.shape
    tr = R if R <= 512 else 512
    assert R % tr == 0
    blk = pl.BlockSpec((tr, L), lambda i: (i, 0))
    c1, c2 = 1.0 - B1 ** STEP, 1.0 - B2 ** STEP

    def body(w_ref, g_ref, m_ref, v_ref, d_ref, mo_ref, vo_ref):
        g_ = g_ref[...]
        m_ = B1 * m_ref[...] + (1.0 - B1) * g_
        v_ = B2 * v_ref[...] + (1.0 - B2) * (g_ * g_)
        d_ref[...] = -LR * ((m_ / c1) / (jnp.sqrt(v_ / c2) + ADAM_EPS) + WD * w_ref[...])
        mo_ref[...] = m_
        vo_ref[...] = v_

    sds = jax.ShapeDtypeStruct((R, L), F32)
    return pl.pallas_call(
        body, name=name, grid=(R // tr,), in_specs=[blk] * 4, out_specs=[blk] * 3, out_shape=[sds] * 3,
        compiler_params=_params(32, 1),
    )(w, g, m, v)


def _adamw_rows(w, m, v, gbufs, row0, *, name, plan=None):
    L, R, C = w.shape
    tr = 256
    assert R % tr == 0 and row0 % tr == 0 and len(gbufs) == L
    grid = (L, R // tr)
    blk = pl.BlockSpec((None, tr, C), lambda l, i: (l, i, 0))
    gblks = [pl.BlockSpec((tr, C), lambda l, i, k=k: (row0 // tr + jnp.where(l == k, i, 0), 0)) for k in range(L)]
    c1, c2 = 1.0 - B1 ** STEP, 1.0 - B2 ** STEP

    def body(*refs):
        ins, (go_ref, d_ref, mo_ref, vo_ref), _, pctx = _split_refs(refs, 3 + L, 4, 0, plan)
        w_ref, m_ref, v_ref = ins[:3]
        g_refs = ins[3:]
        _plan_start(plan, pctx, grid)
        g_ = g_refs[0][...]
        for l in range(1, L):
            g_ = jnp.where(pl.program_id(0) == l, g_refs[l][...], g_)
        m_ = B1 * m_ref[...] + (1.0 - B1) * g_
        v_ = B2 * v_ref[...] + (1.0 - B2) * (g_ * g_)
        go_ref[...] = g_
        d_ref[...] = -LR * ((m_ / c1) / (jnp.sqrt(v_ / c2) + ADAM_EPS) + WD * w_ref[...])
        mo_ref[...] = m_
        vo_ref[...] = v_
        _plan_wait(plan, pctx, grid)

    sds = jax.ShapeDtypeStruct((L, R, C), F32)
    p_in, p_ospec, p_oshape, p_scr, p_alias = _plan_io(plan, 3 + L, 4)
    return pl.pallas_call(
        body, name=name, grid=grid, in_specs=[blk] * 3 + gblks + [_ANY] * len(p_in),
        out_specs=[blk] * 4 + p_ospec, out_shape=[sds] * 4 + p_oshape, scratch_shapes=p_scr,
        input_output_aliases=p_alias, compiler_params=_params(32, 2),
    )(*_hbm(w, m, v, *gbufs), *p_in)


def _add_pairs(g, theirs, ids, *, name):
    n, R, L = theirs.shape
    tr = math.gcd(R, 128)
    nb = R // tr

    def body(ids_ref, a_ref, b_ref, o_ref):
        o_ref[...] = (a_ref[...].astype(F32) + b_ref[...].astype(F32)).astype(BF16)

    blk = pl.BlockSpec((n, tr, L), lambda i, ids: (0, i, 0))
    return pl.pallas_call(
        body, name=name, out_shape=jax.ShapeDtypeStruct((n, R, L), BF16),
        grid_spec=pltpu.PrefetchScalarGridSpec(
            num_scalar_prefetch=1, grid=(nb,),
            in_specs=[pl.BlockSpec((n, tr, L), lambda i, ids: (0, ids[1] * nb + i, 0)), blk], out_specs=blk),
        compiler_params=_params(16, 1),
    )(ids, g, theirs)


def _sum_chips(pair, parts, ids, *, name):
    _, R, L = parts.shape
    tr = math.gcd(R, 128)

    def body(ids_ref, o_ref, r_ref, out_ref):
        out_ref[...] = ((o_ref[...].astype(F32) + r_ref[0].astype(F32)) + r_ref[1].astype(F32)) + r_ref[2].astype(F32)

    return pl.pallas_call(
        body, name=name, out_shape=jax.ShapeDtypeStruct((2, R, L), F32),
        grid_spec=pltpu.PrefetchScalarGridSpec(
            num_scalar_prefetch=1, grid=(R // tr,),
            in_specs=[pl.BlockSpec((None, tr, L), lambda i, ids: (ids[0], i, 0)),
                      pl.BlockSpec((3, tr, L), lambda i, ids: (0, i, 0))],
            out_specs=pl.BlockSpec((None, tr, L), lambda i, ids: (ids[1], i, 0))),
        compiler_params=_params(32, 1),
    )(ids, pair, parts)


def _mesh_ids():
    x, y, c = _mesh_pos()
    return jnp.stack([2 * x + y, c]).astype(jnp.int32)


def _place_shard(rows, ids, *, name):
    R, L = rows.shape
    tr = 128

    def body(ids_ref, in_ref, out_ref):
        out_ref[...] = in_ref[...].astype(BF16)

    return pl.pallas_call(
        body, name=name, out_shape=jax.ShapeDtypeStruct((4, R, L), BF16),
        grid_spec=pltpu.PrefetchScalarGridSpec(
            num_scalar_prefetch=1, grid=(R // tr,), in_specs=[pl.BlockSpec((tr, L), lambda i, ids: (i, 0))],
            out_specs=pl.BlockSpec((None, tr, L), lambda i, ids: (ids[0], i, 0))),
        compiler_params=_params(16, 1),
    )(ids, rows)


IN_E_SHARD, IN_E_LAYOUT = 392, 1664


def _in_e_runs():
    runs = []
    for lo, hi, dst in ((0, 512, 0), (512, 544, 1536), (544, 1568, 512)):
        while lo < hi:
            j = lo // IN_E_SHARD
            wd = min(hi, IN_E_SHARD * (j + 1)) - lo
            runs.append((j, lo - IN_E_SHARD * j, dst, wd))
            lo, dst = lo + wd, dst + wd
    return runs


def _in_e_to_layout(shards):
    tr = 256

    def body(s_ref, o_ref, t_ref):
        t_ref[...] = jnp.zeros_like(t_ref)
        for j in range(4):
            s = s_ref[j].astype(F32)
            for _, c, dst, wd in (r for r in _in_e_runs() if r[0] == j):
                t_ref[:, dst:dst + wd] = s[:, c:c + wd]
        o_ref[...] = t_ref[...].astype(BF16)

    return pl.pallas_call(
        body, name="in_e_to_layout", out_shape=jax.ShapeDtypeStruct((D, IN_E_LAYOUT), BF16), grid=(D // tr,),
        in_specs=[pl.BlockSpec((4, tr, IN_E_SHARD), lambda i: (0, i, 0))],
        out_specs=pl.BlockSpec((tr, IN_E_LAYOUT), lambda i: (i, 0)),
        scratch_shapes=[pltpu.VMEM((tr, IN_E_LAYOUT), F32)], compiler_params=_params(16, 1),
    )(shards)


def _in_e_from_layout(g):
    tr = 256

    def body(g_ref, o_ref, t_ref):
        g_ = g_ref[...]
        for j, c, src, wd in _in_e_runs():
            t_ref[j, :, c:c + wd] = g_[:, src:src + wd]
        o_ref[...] = t_ref[...].astype(BF16)

    return pl.pallas_call(
        body, name="in_e_from_layout", out_shape=jax.ShapeDtypeStruct((4, D, IN_E_SHARD), BF16), grid=(D // tr,),
        in_specs=[pl.BlockSpec((tr, IN_E_LAYOUT), lambda i: (i, 0))],
        out_specs=pl.BlockSpec((4, tr, IN_E_SHARD), lambda i: (0, i, 0)),
        scratch_shapes=[pltpu.VMEM((4, tr, IN_E_SHARD), F32)], compiler_params=_params(16, 1),
    )(g)


def _place_weights(pieces, buffer_rows, ids, *, plan=None):
    tr = 256
    steps, s = [], 0
    for arr, layer, buf, row0 in pieces:
        nblk = arr.shape[1] // tr
        steps.append((s, nblk))
        s += nblk
    total = s
    buf_start = [min(st for (st, _), p in zip(steps, pieces) if p[2] == k) for k in range(len(buffer_rows))]
    grid = (total,)
    n_in = len(pieces)

    def body(*refs):
        ins, outs, _, pctx = _split_refs(refs[1:], n_in, len(buffer_rows), 0, plan)
        _plan_start(plan, pctx, grid)
        i = pl.program_id(0)
        for (st, nblk), (_, _, buf, _), ref in zip(steps, pieces, ins):
            @pl.when((i >= st) & (i < st + nblk))
            def _(ref=ref, buf=buf):
                outs[buf][...] = ref[...].astype(BF16)
        _plan_wait(plan, pctx, grid)

    in_specs = [pl.BlockSpec((None, tr, D), lambda i, ids, layer=layer, st=st, nblk=nblk:
                             (layer, jnp.clip(i - st, 0, nblk - 1), 0))
                for (st, nblk), (_, layer, _, _) in zip(steps, pieces)]
    out_specs = [pl.BlockSpec((None, tr, D), lambda i, ids, st=st, nb=rows // tr: (ids[0], jnp.clip(i - st, 0, nb - 1), 0))
                 for st, rows in zip(buf_start, buffer_rows)]
    p_in, p_ospec, p_oshape, p_scr, p_alias = _plan_io(plan, 1 + n_in, len(buffer_rows))
    return pl.pallas_call(
        body, name="place_weights",
        out_shape=[jax.ShapeDtypeStruct((4, rows, D), BF16) for rows in buffer_rows] + p_oshape,
        grid_spec=pltpu.PrefetchScalarGridSpec(
            num_scalar_prefetch=1, grid=grid, in_specs=in_specs + [_ANY] * len(p_in), out_specs=out_specs + p_ospec,
            scratch_shapes=p_scr),
        input_output_aliases=p_alias, compiler_params=_params(16, 1),
    )(ids, *[p[0] for p in pieces], *p_in)


def _remote(src, dst, send_sem, recv_sem, to):
    return pltpu.make_async_remote_copy(src_ref=src, dst_ref=dst, send_sem=send_sem, recv_sem=recv_sem,
                                        device_id=to, device_id_type=MESH_IDS)


def _rows(ref, lead, start, size):
    return ref.at[tuple(pl.ds(0, n) for n in ref.shape[:lead]) + (pl.ds(start, size),)]


def _other_chips():
    x, y, _ = _mesh_pos()
    return [(1 - x, y), (x, 1 - y), (1 - x, 1 - y)]


def _plan_gather_ici(bufs):
    n = len(bufs)

    def copies(outs, send, recv):
        x, y, c = _mesh_pos()
        res = []
        for b in range(n):
            half = bufs[b].shape[1] // 2
            mine = _rows(outs[b].at[2 * x + y], 0, c * half, half)
            for j, (cx, cy) in enumerate(_other_chips()):
                res.append((_remote(mine, mine, send(3 * b + j), recv(3 * b + j), (cx, cy, c)),
                            _remote(mine, _rows(outs[b].at[2 * cx + cy], 0, c * half, half),
                                    send(3 * b + j), recv(3 * b + j), (x, y, c))))
        return res

    def start(ins, outs, send, recv, loc):
        for out_cp, _ in copies(outs, send, recv):
            out_cp.start()

    def wait(ins, outs, send, recv, loc):
        for out_cp, in_cp in copies(outs, send, recv):
            in_cp.wait_recv()
            out_cp.wait_send()

    outs = [jax.ShapeDtypeStruct(b.shape, b.dtype) for b in bufs]
    return _Plan(bufs, outs, 3 * n, 0, start, wait, aliases={b: b for b in range(n)})


def _plan_gather_forward(bufs):
    n = len(bufs)

    def copies(outs, send, recv):
        x, y, c = _mesh_pos()
        res = []
        for b in range(n):
            half = bufs[b].shape[1] // 2
            for j, (cx, cy) in enumerate(_other_chips()):
                slot = outs[b].at[2 * cx + cy]
                res.append((_remote(_rows(slot, 0, c * half, half), _rows(slot, 0, c * half, half),
                                    send(3 * b + j), recv(3 * b + j), (x, y, 1 - c)),
                            _remote(_rows(slot, 0, c * half, half), _rows(slot, 0, (1 - c) * half, half),
                                    send(3 * b + j), recv(3 * b + j), (x, y, c))))
        return res

    def start(ins, outs, send, recv, loc):
        for out_cp, _ in copies(outs, send, recv):
            out_cp.start()

    def wait(ins, outs, send, recv, loc):
        for out_cp, in_cp in copies(outs, send, recv):
            in_cp.wait_recv()
            out_cp.wait_send()

    outs = [jax.ShapeDtypeStruct(b.shape, b.dtype) for b in bufs]
    return _Plan(bufs, outs, 3 * n, 0, start, wait, aliases={b: b for b in range(n)})


def _plan_pair_swap(g):
    half = g.shape[1] // 2

    def copy(ins, outs, send, recv, loc):
        x, y, c = _mesh_pos()
        return _remote(_rows(ins[0], 1, (1 - c) * half, half), outs[0], send(0), recv(0), (x, y, 1 - c))

    return _Plan([g], [jax.ShapeDtypeStruct((4, half, g.shape[2]), g.dtype)], 1, 0,
                 lambda *a: copy(*a).start(), lambda *a: copy(*a).wait())


def _plan_pair_gather(buf):
    def copies(ins, outs, send, recv, loc):
        x, y, c = _mesh_pos()
        return (_remote(outs[0].at[c], outs[0].at[c], send(0), recv(0), (x, y, 1 - c)),
                _remote(outs[0].at[c], outs[0].at[1 - c], send(0), recv(0), (x, y, c)))

    def wait(*a):
        out_cp, in_cp = copies(*a)
        in_cp.wait_recv()
        out_cp.wait_send()

    return _Plan([buf], [jax.ShapeDtypeStruct(buf.shape, buf.dtype)], 1, 0, lambda *a: copies(*a)[0].start(), wait,
                 aliases={0: 0})


def _plan_chip_scatter(p):
    def copies(ins, outs, send, recv, loc):
        _, _, c = _mesh_pos()
        return [_remote(ins[0].at[2 * cx + cy], outs[0].at[j], send(j), recv(j), (cx, cy, c))
                for j, (cx, cy) in enumerate(_other_chips())]

    def start(*a):
        for cp in copies(*a):
            cp.start()

    def wait(*a):
        for cp in copies(*a):
            cp.wait()

    return _Plan([p], [jax.ShapeDtypeStruct((3,) + p.shape[1:], p.dtype)], 3, 0, start, wait)


def _plan_exchange_all(vec):
    def copies(ins, outs, send, recv, loc):
        x, y, c = _mesh_pos()
        return [_remote(ins[0], outs[0].at[r - 1], send(r - 1), recv(r - 1), (x ^ (r >> 2), y ^ ((r >> 1) & 1), c ^ (r & 1)))
                for r in range(1, 8)]

    def start(*a):
        for cp in copies(*a):
            cp.start()

    def wait(*a):
        for cp in copies(*a):
            cp.wait()

    return _Plan([vec], [jax.ShapeDtypeStruct((7,) + vec.shape, vec.dtype)], 7, 0, start, wait)


SMALL_LAYOUT = {
    "mla_gq": (0, 1, 256, (1, 256)), "mla_gkv": (1, 1, 256, (1, 256)), "sgu_ln_g": (2, 1, 512, (1, 512)),
    "sgu_ln_b": (3, 1, 512, (1, 512)), "sgu_w": (4, 64, 1024, (64, 1024)), "sgu_b": (68, 1, 512, (1, 512)),
    "hg_lb": (69, 2, 1024, (2, 1024)), "hg_gnorm": (71, 1, 1024, (1, 256)), "ln1_g": (72, 2, 1024, (2, 1024)),
    "ln1_b": (74, 2, 1024, (2, 1024)), "ln2_g": (76, 2, 1024, (2, 1024)), "ln2_b": (78, 2, 1024, (2, 1024)),
}


def _small_pack(dgq, dgkv, dslg, dslb, dsw, dsb, dlb, dgn, ln_parts, sq_err):
    flat_ln = [p for pair in ln_parts for p in pair]

    def body(*refs):
        gq_ref, gkv_ref, slg_ref, slb_ref, sw_ref, sb_ref, lb_ref, gn_ref = refs[:8]
        ln_refs, err_ref, out_ref, t_sc = refs[8:16], refs[16], refs[17], refs[18]
        s8 = lambda ref: jnp.sum(ref[...], axis=0, keepdims=True)
        out_ref[...] = jnp.zeros_like(out_ref)
        out_ref[0:1, 0:256] = s8(gq_ref)
        out_ref[1:2, 0:256] = s8(gkv_ref)
        out_ref[2:3, 0:512] = s8(slg_ref)
        out_ref[3:4, 0:512] = s8(slb_ref)
        out_ref[4:68, :] = sw_ref[...]
        t_sc[...] = sb_ref[...].T
        for g in range(SGU_G):
            out_ref[68:69, g * SGU_C:(g + 1) * SGU_C] = t_sc[g:g + 1, :]
        d_lb1 = s8(lb_ref)
        out_ref[69:70, :] = -d_lb1
        out_ref[70:71, :] = d_lb1
        out_ref[71:72, :] = s8(gn_ref)
        for k, ref in enumerate(ln_refs):
            out_ref[72 + k:73 + k, :] = s8(ref)
        out_ref[0:1, 1023:1024] = jnp.sum(s8(err_ref), axis=1, keepdims=True) * (0.5 / D)

    vm = pl.BlockSpec(memory_space=pltpu.VMEM)
    return pl.pallas_call(
        body, name="small_grad_pack", in_specs=[vm] * 17, out_specs=vm,
        out_shape=jax.ShapeDtypeStruct((SMALL_ROWS, 1024), F32), scratch_shapes=[pltpu.VMEM((SGU_C, SGU_C), F32)],
        compiler_params=_params(16),
    )(dgq, dgkv, dslg, dslb, dsw.reshape(64, 1024), dsb, dlb, dgn, *flat_ln, sq_err)


def _small_update(vec, others, ids, w, m, v):
    names = list(SMALL_LAYOUT)
    n = len(names)
    c1, c2 = 1.0 - B1 ** STEP, 1.0 - B2 ** STEP
    have_others = others is not None

    def body(*refs):
        ids_ref, v_ref = refs[0], refs[1]
        k = 2 + have_others
        w_refs, m_refs, v_refs = refs[k:k + n], refs[k + n:k + 2 * n], refs[k + 2 * n:k + 3 * n]
        outs = refs[k + 3 * n:]
        row0_ref, tot_sc = outs[0], outs[-1]
        total = v_ref[...]
        if have_others:
            me = 2 * ids_ref[0] + ids_ref[1]
            total = None
            for d in range(8):
                rel = d ^ me
                term = jnp.where(rel == 0, v_ref[...], refs[2][jnp.maximum(rel - 1, 0)])
                total = term if total is None else total + term
        tot_sc[...] = total
        row0_ref[...] = tot_sc[0:1, :]
        for i, name in enumerate(names):
            r0, nr, width, _ = SMALL_LAYOUT[name]
            if name == "hg_gnorm":
                g_ = tot_sc[r0:r0 + 1, 0:256]
                for chip in range(1, 4):
                    g_ = jnp.where(ids_ref[0] == chip, tot_sc[r0:r0 + 1, chip * 256:(chip + 1) * 256], g_)
            else:
                g_ = tot_sc[r0:r0 + nr, 0:width]
            m_ = B1 * m_refs[i][...] + (1.0 - B1) * g_
            v_ = B2 * v_refs[i][...] + (1.0 - B2) * (g_ * g_)
            go, do, mo, vo = outs[1 + 4 * i:5 + 4 * i]
            go[...] = g_
            do[...] = -LR * ((m_ / c1) / (jnp.sqrt(v_ / c2) + ADAM_EPS) + WD * w_refs[i][...])
            mo[...] = m_
            vo[...] = v_

    full = lambda shape: pl.BlockSpec(shape, lambda i, ids, nd=len(shape): (0,) * nd)
    kshapes = [SMALL_LAYOUT[name][3] for name in names]
    operands = [vec] + ([others] if have_others else []) + [d[name] for d in (w, m, v) for name in names]
    out_shapes = [jax.ShapeDtypeStruct((1, 1024), F32)] + [jax.ShapeDtypeStruct(s, F32) for s in kshapes for _ in range(4)]
    res = pl.pallas_call(
        body, name="small_update", out_shape=out_shapes,
        grid_spec=pltpu.PrefetchScalarGridSpec(
            num_scalar_prefetch=1, grid=(1,), in_specs=[full(o.shape) for o in operands],
            out_specs=[full(s.shape) for s in out_shapes],
            scratch_shapes=[pltpu.VMEM((SMALL_ROWS, 1024), F32)]),
        compiler_params=_params(32, 1),
    )(ids, *operands)
    return res[0], {name: tuple(res[1 + 4 * i:5 + 4 * i]) for i, name in enumerate(names)}


ROWS_L1, ROWS_L0, ROWS_ODD_W = 3328, 2048, 384
ODD_PARTS = (("w_out_e", (256, 1024)), ("w_in_e", (1024, 392)), ("w_qb", (256, 192)), ("w_kvb", (256, 256)))
ODD_W_PARTS = tuple(p for p in ODD_PARTS if p[0] != "w_in_e")


def _odd_rows(parts, dtype, layout, total, gnorm=None):
    rows = [parts[n].reshape(-1, 1024).astype(dtype) for n, _ in layout]
    used = sum(r.shape[0] for r in rows)
    if gnorm is not None:
        bits = lax.bitcast_convert_type(gnorm.reshape(-1), BF16).reshape(1, 512)
        rows.append(jnp.pad(bits, ((0, 15), (0, 512))))
        used += 16
    if total > used:
        rows.append(jnp.zeros((total - used, 1024), dtype))
    return jnp.concatenate(rows, axis=0)


def _odd_unrows(buf, layout, with_gnorm=False):
    out, off = {}, 0
    for n, shape in layout:
        nr = math.prod(shape) // 1024
        out[n] = buf[off:off + nr].reshape(shape)
        off += nr
    if with_gnorm:
        out["hg_gnorm"] = lax.bitcast_convert_type(buf[off, :512].reshape(256, 2), F32).reshape(1, 256)
    return out


def _rope_tables(positions):
    half = ROPE // 2
    inv_freq = ROPE_BASE ** (-jnp.arange(half, dtype=F32) / half)
    ang = positions.astype(F32).reshape(-1, 1) * inv_freq
    cos, sin = jnp.cos(ang), jnp.sin(ang)
    T = ang.shape[0]
    one, z16, z32 = jnp.ones((T, NOPE), F32), jnp.zeros((T, half), F32), jnp.zeros((T, 32), F32)
    z64 = jnp.zeros((T, NOPE), F32)
    c = jnp.concatenate([one, cos, cos, z32], axis=1)
    s1 = jnp.concatenate([z64, -sin, z16, z32], axis=1)
    s2 = jnp.concatenate([z64, z16, sin, z32], axis=1)
    return c, s1, s2


def _local_step(x, positions, tgt, odd, bufs, P, exchange):
    T = x.shape[0]
    row = lambda a: a.reshape(1, -1)
    rc, rs1, rs2 = _rope_tables(positions)
    blk = lambda f: pl.BlockSpec((None, D, D), f)

    w_in = _in_e_to_layout(odd["w_in_e"])
    wq = jnp.pad(odd["w_qb"].reshape(256, HEADS, NOPE + ROPE), ((0, 0), (0, 0), (0, 32))).reshape(256, HEADS * 128)
    kvb = odd["w_kvb"].reshape(256, HEADS, NOPE + VDIM)
    wk = jnp.pad(kvb[:, :, :NOPE], ((0, 0), (0, 0), (0, 64))).reshape(256, HEADS * 128)
    wv = kvb[:, :, NOPE:].reshape(256, HEADS * VDIM)
    w_out_e = odd["w_out_e"]
    sgu_w = P["sgu_w"][0]
    sgu_bt = P["sgu_b"][0].T
    gq, gkv = P["mla_gq"], P["mla_gkv"]
    gnorm = P["hg_gnorm"]

    z0 = _matmul(x, w_in, name="in_proj_e", M=T, N=1664, K=D, tn=1664)[0]
    q, k, v = _mla_prep(z0, gq, gkv, wq, wk, wv, rc, rs1, rs2)
    if exchange:
        ids = _mesh_ids()
        placed = list(bufs)
        a_out, lse, wga, wgb = _flash_fwd(q, k, v, plan=_plan_gather_ici(placed[:2]))
    else:
        a_out, lse = _flash_fwd(q, k, v)
        wga, wgb, wgc = bufs
    mix0 = _sgu_fwd(z0, a_out, P["sgu_ln_g"], P["sgu_ln_b"], sgu_w, sgu_bt)
    res = _proj_ln(mix0, w_out_e, x, row(P["ln1_g"][0]), row(P["ln1_b"][0]), name="out_proj_ln_e",
                   plan=_plan_gather_forward([wga, wgb]) if exchange else None)
    r1, h1b = res[:2]
    if exchange:
        wga, wgb = res[2:]
    ln = lambda name, l: (row(P[name + "_g"][l]), row(P[name + "_b"][l]))
    res = _ffn_ln(h1b, wga, r1, *ln("ln2", 0), name="ffn_ln_0", prev_ln=ln("ln1", 0),
                  plan=_plan_gather_ici(placed[2:]) if exchange else None)
    ra0, r2, h2b = res[:3]
    z4 = _matmul(h2b, wgb, name="in_proj_o", M=T, N=4 * D, K=D, tn=2 * D, n_slots=True,
                 b_spec=pl.BlockSpec((2, D, D), lambda i, j, k: (j, 0, 0)),
                 out_shape=jax.ShapeDtypeStruct((4, T, D), F32),
                 o_spec=pl.BlockSpec((2, min(MM_ROWS, T), D), lambda i, j, k: (j, i, 0)))[0]
    y1, o_raw, states = _hgrn_fwd(z4, P["hg_lb"], gnorm)
    res2 = _proj_ln(y1, wgb, r2, *ln("ln1", 1), name="out_proj_ln_o", prev_ln=ln("ln2", 0), w_rowblk=4,
                    plan=_plan_gather_forward([res[3]]) if exchange else None)
    r3, h3b = res2[:2]
    if exchange:
        wgc = res2[2]
    ra1, r4, _ = _ffn_ln(h3b, wgc, r3, *ln("ln2", 1), name="ffn_ln_1", prev_ln=ln("ln1", 1))

    ln1_g, ln1_b, ln2_g, ln2_b = [None, None], [None, None], [None, None], [None, None]
    sq_err_parts = []

    def ffn_bwd(l, dh, r_out, ra, h_mid_b, g2, wg, rows, plan=None, loss_head=()):
        dr, dr_b, dg, db, *sq_err = _ln_bwd(dh, r_out, row(g2), name=f"ln2_bwd_{l}", loss_head=loss_head)
        sq_err_parts.extend(sq_err)
        ln2_g[l], ln2_b[l] = dg, db
        da, *extra = _matmul(dr_b, wg, tb=True, mul=ra, out_dtype=BF16, name=f"ffn_da_{l}", M=T, N=4 * D, K=D, tn=2 * D,
                             b_spec=pl.BlockSpec((2, D, D), lambda i, j, k: (j, 1, 0)), n_slots=True, plan=plan)
        gbuf = _matmul(ra, dr_b, ta=True, a_sq=True, name=f"ffn_dw2_{l}", M=4 * D, N=D, K=T, tm=1024, tk=DW_TOKENS // 2,
                       out_shape=jax.ShapeDtypeStruct((4, rows, D), BF16), o_spec=blk(lambda i, j, k: (i, 1, 0)))[0]
        gbuf = _matmul(h_mid_b, da, ta=True, name=f"ffn_dw1_{l}", M=D, N=4 * D, K=T, tm=1024, tk=DW_TOKENS, into=gbuf,
                       out_shape=jax.ShapeDtypeStruct((4, rows, D), BF16), o_spec=blk(lambda i, j, k: (j, 0, 0)))[0]
        dh_mid = _matmul(da, wg, tb=True, add=dr, add_scale=ALPHA, name=f"ffn_dh_{l}", M=T, N=D, K=4 * D, tk=2 * D,
                         b_spec=pl.BlockSpec((2, D, D), lambda i, j, k: (k, 0, 0)))[0]
        return dh_mid, gbuf, extra

    dh3, g1, _ = ffn_bwd(1, None, r4, ra1, h3b, P["ln2_g"][1], wgc, ROWS_L1, loss_head=(row(P["ln2_b"][1]), tgt))
    loss_parts = sq_err_parts[0]
    dr3, dr3_b, dg, db = _ln_bwd(dh3, r3, row(P["ln1_g"][1]), name="ln1_bwd_1")
    ln1_g[1], ln1_b[1] = dg, db
    g1_sds = jax.ShapeDtypeStruct((4, ROWS_L1, D), BF16)
    g1 = _matmul(y1, dr3_b, ta=True, name="dw_out_o", M=D, N=D, K=T, tm=256, tk=DW_TOKENS, into=g1, out_shape=g1_sds,
                 o_spec=pl.BlockSpec((None, 256, D), lambda i, j, k: (i, 12, 0)))[0]
    dmix1 = _matmul(dr3_b, wgb, tb=True, name="dmix_o", M=T, N=D, K=D, b_spec=_rows4_spec(4, 3), b_merge=(D, D))[0]
    dz4, dlb, dgn = _hgrn_bwd(z4, o_raw, dmix1, states, P["hg_lb"], gnorm)
    g1 = _matmul(h2b, dz4, ta=True, name="dw_in_o", M=D, N=4 * D, K=T, tm=1024, tk=DW_TOKENS, into=g1, out_shape=g1_sds,
                 b_spec=pl.BlockSpec((None, min(DW_TOKENS, T), D), lambda i, j, k: (j, k, 0)),
                 o_spec=blk(lambda i, j, k: (j, 2, 0)))[0]
    dh2 = _matmul(dz4, wgb, tb=True, add=dr3, add_scale=ALPHA, name="dh_in_o", M=T, N=D, K=4 * D, tk=2 * D,
                  a_spec=pl.BlockSpec((2, min(MM_ROWS, T), D), lambda i, j, k: (k, i, 0)),
                  b_spec=pl.BlockSpec((2, D, D), lambda i, j, k: (k, 0, 0)))[0]

    dh1, g0, swapped1 = ffn_bwd(0, dh2, r2, ra0, h1b, P["ln2_g"][0], wga, ROWS_L0,
                                plan=_plan_pair_swap(g1) if exchange else None)
    dr1, dr1_b, dg, db = _ln_bwd(dh1, r1, row(P["ln1_g"][0]), name="ln1_bwd_0")
    ln1_g[0], ln1_b[0] = dg, db
    godd = {"w_out_e": _matmul(mix0, dr1_b, ta=True, name="dw_out_e", M=D, N=D, K=T, tm=1024, tk=DW_TOKENS)[0]}
    dmix0, *swapped0 = _matmul(dr1_b, w_out_e, tb=True, name="dmix_e", M=T, N=D, K=D,
                               plan=_plan_pair_swap(g0) if exchange else None)
    delta, do_b = _attn_delta(dmix0, a_out)
    if exchange:
        pair1 = _add_pairs(g1, swapped1[0], ids, name="grad_pair_add_1")
        pair0 = _add_pairs(g0, swapped0[0], ids, name="grad_pair_add_0")
        dq4, dk, dv, parts0, parts1 = _flash_bwd(
            q, k, v, do_b, lse, delta, plan=_join_plans([_plan_chip_scatter(pair0), _plan_chip_scatter(pair1)]))
        half0 = _sum_chips(pair0, parts0, ids, name="grad_chip_sum_0")
        half1 = _sum_chips(pair1, parts1, ids, name="grad_chip_sum_1")
        dc, dkr, dwq, dwk, dwv, dgq, dgkv, g0, g1 = _mla_bwd(
            z0, dq4, dk, dv, gq, gkv, wq, wk, wv, rc, rs1, rs2,
            plan=_join_plans([_plan_pair_gather(half0), _plan_pair_gather(half1)]))
        g0, g1 = g0.reshape(ROWS_L0, D), g1.reshape(ROWS_L1, D)
    else:
        dq4, dk, dv = _flash_bwd(q, k, v, do_b, lse, delta)
        dc, dkr, dwq, dwk, dwv, dgq, dgkv = _mla_bwd(z0, dq4, dk, dv, gq, gkv, wq, wk, wv, rc, rs1, rs2)
    godd["w_qb"] = dwq.reshape(256, HEADS, 128)[:, :, :NOPE + ROPE].reshape(256, HEADS * (NOPE + ROPE))
    godd["w_kvb"] = jnp.concatenate([dwk.reshape(256, HEADS, 128)[:, :, :NOPE], dwv.reshape(256, HEADS, VDIM)],
                                    axis=2).reshape(256, HEADS * (NOPE + VDIM))
    swap_b = None
    if exchange:
        by_chip = [_odd_rows({"w_out_e": jnp.split(godd["w_out_e"], 4, axis=0)[j],
                              **{n: jnp.split(godd[n], 4, axis=1)[j] for n in ("w_qb", "w_kvb")}}, BF16,
                             ODD_W_PARTS, ROWS_ODD_W)
                   for j in range(4)]
        odd_b = jnp.stack(by_chip)
        swap_b = _plan_pair_swap(odd_b)
    dz0, dsw, dsb, dslg, dslb, *theirs_b = _sgu_bwd(z0, dmix0, dc, dkr, P["sgu_ln_g"], P["sgu_ln_b"], sgu_w, sgu_bt,
                                                    plan=swap_b)
    small_vec = _small_pack(dgq, dgkv, dslg, dslb, dsw, dsb, dlb, dgn, [ln1_g, ln1_b, ln2_g, ln2_b], loss_parts)
    plan_in = None
    if exchange:
        pair_b = _add_pairs(odd_b, theirs_b[0], ids, name="odd_pair_add_1")
        plan_in = _join_plans([_plan_exchange_all(small_vec), _plan_chip_scatter(pair_b)])
    dw_in, *carried = _matmul(x, dz0, ta=True, name="dw_in_e", M=D, N=1664, K=T, tm=1024, tn=1664, tk=DW_TOKENS // 4,
                              plan=plan_in)
    odd_a = godd["w_in_e"] = _in_e_from_layout(dw_in)
    plan_x = None
    if exchange:
        small_others, parts_b = carried
        theirs_a = _run_plan(_plan_pair_swap(odd_a), name="odd_pair_swap")[0]
        pair_a = _add_pairs(odd_a, theirs_a, ids, name="odd_pair_add_0")
        plan_x = _plan_chip_scatter(pair_a)
    grad_x, *parts_a = _matmul(dz0, w_in, tb=True, add=dr1, add_scale=ALPHA, name="dx", M=T, N=D, K=1664, tk=1664,
                               plan=plan_x)
    if exchange:
        godd = ([pair_a, pair_b], [parts_a[0], parts_b])
        return grad_x, g0, g1, godd, small_vec, small_others
    return grad_x, g0, g1, godd, small_vec, None


WEIGHTS = ['w_in_e', 'mla_gq', 'mla_gkv', 'w_qb', 'w_kvb', 'sgu_ln_g', 'sgu_ln_b', 'sgu_w', 'sgu_b', 'w_out_e',
           'w_in_o', 'hg_lb', 'hg_gnorm', 'w_out_o', 'ln1_g', 'ln1_b', 'w_ff1', 'w_ff2', 'ln2_g', 'ln2_b']


def kernel(x, positions, w_in_e, mla_gq, mla_gkv, w_qb, w_kvb, sgu_ln_g, sgu_ln_b, sgu_w, sgu_b, w_out_e, w_in_o, hg_lb, hg_gnorm, w_out_o, ln1_g, ln1_b, w_ff1, w_ff2, ln2_g, ln2_b, loss_target, m_w_in_e, m_mla_gq, m_mla_gkv, m_w_qb, m_w_kvb, m_sgu_ln_g, m_sgu_ln_b, m_sgu_w, m_sgu_b, m_w_out_e, m_w_in_o, m_hg_lb, m_hg_gnorm, m_w_out_o, m_ln1_g, m_ln1_b, m_w_ff1, m_w_ff2, m_ln2_g, m_ln2_b, v_w_in_e, v_mla_gq, v_mla_gkv, v_w_qb, v_w_kvb, v_sgu_ln_g, v_sgu_ln_b, v_sgu_w, v_sgu_b, v_w_out_e, v_w_in_o, v_hg_lb, v_hg_gnorm, v_w_out_o, v_ln1_g, v_ln1_b, v_w_ff1, v_w_ff2, v_ln2_g, v_ln2_b):
    args = dict(locals())
    w = {n: args[n] for n in WEIGHTS}
    m = {n: args["m_" + n] for n in WEIGHTS}
    v = {n: args["v_" + n] for n in WEIGHTS}
    cx, cy, cc = _mesh_pos()
    chip = 2 * cx + cy

    odd_shard = _odd_rows({"w_out_e": w_out_e[0], "w_qb": w_qb[0], "w_kvb": w_kvb[0]}, BF16, ODD_W_PARTS, ROWS_ODD_W,
                          gnorm=hg_gnorm)
    ids = _mesh_ids()
    placed = [_place_shard(w_in_e[0], ids, name="place_shard_in_e"), _place_shard(odd_shard, ids, name="place_shard_odd")]
    pieces = [(w_ff1, 0, 0, 0), (w_ff2, 0, 0, 1024), (w_in_o, 0, 1, 0), (w_out_o, 0, 1, 1024),
              (w_ff1, 1, 2, 0), (w_ff2, 1, 2, 1024)]
    *big_bufs, odd_a, odd_b = _place_weights(pieces, (2048, 1280, 2048), ids, plan=_plan_gather_ici(placed))
    gathered = _run_plan(_plan_gather_forward([odd_a, odd_b]), name="odd_gather_forward")
    per_chip = [_odd_unrows(gathered[1][j], ODD_W_PARTS, with_gnorm=True) for j in range(4)]
    odd = {"w_out_e": jnp.concatenate([p["w_out_e"] for p in per_chip], axis=0),
           "w_in_e": gathered[0]}
    for n in ("w_qb", "w_kvb"):
        odd[n] = jnp.concatenate([p[n] for p in per_chip], axis=1)
    small = {n: w[n] for n in SMALL_LAYOUT if n != "hg_gnorm"}
    small["hg_gnorm"] = jnp.concatenate([p["hg_gnorm"] for p in per_chip], axis=1)
    grad_x, g_l0, g_l1, godd, small_vec, small_others = _local_step(
        x[0], positions[0], loss_target[0], odd, big_bufs, small, True)

    sums = [_sum_chips(pair, parts, ids, name=f"odd_chip_sum_{k}") for k, (pair, parts) in enumerate(zip(*godd))]
    g_in_e, g_rest = _run_plan(_join_plans([_plan_pair_gather(s) for s in sums]), name="odd_pair_gather")
    g_odd = _odd_unrows(g_rest.reshape(ROWS_ODD_W, 1024), ODD_W_PARTS)
    g_odd["w_in_e"] = g_in_e.reshape(D, 392)

    to_kernel = lambda d: {n: d[n].reshape(SMALL_LAYOUT[n][3]) for n in SMALL_LAYOUT}
    first_row, small_out = _small_update(small_vec, small_others, ids, to_kernel(w), to_kernel(m), to_kernel(v))
    loss = first_row[0, 1023]
    grads, delta, new_m, new_v = {}, {}, {}, {}
    for n, res in small_out.items():
        grads[n], delta[n], new_m[n], new_v[n] = (r.reshape(w[n].shape) for r in res)

    for n, bufs_, row0 in (("w_ff1", [g_l0, g_l1], 0), ("w_ff2", [g_l0, g_l1], 1024), ("w_in_o", [g_l1], 2048),
                           ("w_out_o", [g_l1], 3072)):
        grads[n], delta[n], new_m[n], new_v[n] = _adamw_rows(w[n], m[n], v[n], bufs_, row0, name=f"adamw_{n}")
    for n, _ in ODD_PARTS:
        grads[n] = g_odd[n][None]
        d_, m_, v_ = _adamw(w[n][0], g_odd[n], m[n][0], v[n][0], name=f"adamw_{n}")
        delta[n], new_m[n], new_v[n] = d_[None], m_[None], v_[None]

    return (loss, grad_x[None], *[grads[n] for n in WEIGHTS], *[delta[n] for n in WEIGHTS],
            *[new_m[n] for n in WEIGHTS], *[new_v[n] for n in WEIGHTS])
```

```python
import math

import jax
import jax.numpy as jnp
from jax import lax
from jax.experimental import pallas as pl
from jax.experimental.pallas import tpu as pltpu

F32 = jnp.float32
BF16 = jnp.bfloat16
MESH_IDS = pl.DeviceIdType.MESH

D = 1024
DEPTH = 2
HEADS = 8
NOPE, ROPE, VDIM = 64, 32, 64
QK_SCALE = (NOPE + ROPE) ** -0.5
ROPE_BASE = 10000.0
SGU_G, SGU_C = 4, 128
HG_CHUNK = 64
HG_HEADS_PER_STEP = 8
ALPHA = (2 * DEPTH) ** 0.25
EPS = 1e-5
LR, B1, B2, ADAM_EPS, WD, STEP = 0.001, 0.9, 0.999, 1e-08, 0.01, 10
GELU_C = math.sqrt(2.0 / math.pi)
GELU_A = 0.044715
MB = 1024 * 1024
ROW_BLOCK = 512
SMALL_ROWS = 80

NT_DIMS = (((1,), (1,)), ((), ()))
TN_DIMS = (((0,), (0,)), ((), ()))


def _params(vmem_mb, n_axes=0):
    kw = dict(vmem_limit_bytes=vmem_mb * MB)
    if n_axes:
        kw["dimension_semantics"] = ("arbitrary",) * n_axes
    return pltpu.CompilerParams(**kw)


_ANY = pl.BlockSpec(memory_space=pltpu.HBM)


def _mesh_pos():
    return lax.axis_index("x"), lax.axis_index("y"), lax.axis_index("c")


def _hbm(*arrays):
    return tuple(pltpu.with_memory_space_constraint(a, pltpu.HBM) if a.size >= 2 ** 18 else a for a in arrays)


class _Plan:
    def __init__(self, ins, outs, n_remote, n_local, start, wait, aliases=None):
        self.ins, self.outs, self.n_remote, self.n_local = list(ins), list(outs), n_remote, n_local
        self.start, self.wait, self.aliases = start, wait, dict(aliases or {})


def _join_plans(plans):
    ins, outs, aliases, parts = [], [], {}, []
    nr = nl = 0
    for p in plans:
        parts.append((p, len(ins), len(outs), nr, nl))
        aliases.update({len(ins) + i: len(outs) + o for i, o in p.aliases.items()})
        ins += p.ins
        outs += p.outs
        nr += p.n_remote
        nl += p.n_local

    def run(which):
        def go(in_refs, out_refs, send, recv, loc):
            for p, i0, o0, r0, l0 in parts:
                getattr(p, which)(in_refs[i0:i0 + len(p.ins)], out_refs[o0:o0 + len(p.outs)],
                                  lambda i, r0=r0: send(r0 + i), lambda i, r0=r0: recv(r0 + i),
                                  lambda i, l0=l0: loc(l0 + i))
        return go

    return _Plan(ins, outs, nr, nl, run("start"), run("wait"), aliases)


def _plan_io(plan, n_in, n_out):
    if plan is None:
        return [], [], [], [], {}
    sems = [pltpu.SemaphoreType.DMA((max(plan.n_remote, 1),)), pltpu.SemaphoreType.DMA((max(plan.n_remote, 1),)),
            pltpu.SemaphoreType.DMA((max(plan.n_local, 1),))]
    aliases = {n_in + i: n_out + o for i, o in plan.aliases.items()}
    return plan.ins, [_ANY] * len(plan.outs), plan.outs, sems, aliases


def _split_refs(refs, n_in, n_out, n_scr, plan):
    p_in, p_out = (len(plan.ins), len(plan.outs)) if plan is not None else (0, 0)
    refs = list(refs)
    ins, refs = refs[:n_in], refs[n_in:]
    pins, refs = refs[:p_in], refs[p_in:]
    outs, refs = refs[:n_out], refs[n_out:]
    pouts, refs = refs[:p_out], refs[p_out:]
    scr, psem = refs[:n_scr], refs[n_scr:]
    psem = tuple((lambda i, s=s: s.at[i]) for s in psem)
    return ins, outs, scr, (pins, pouts, psem)


def _grid_edge(grid, last):
    cond = None
    for ax, n in enumerate(grid):
        c = pl.program_id(ax) == (n - 1 if last else 0)
        cond = c if cond is None else cond & c
    return cond


def _plan_start(plan, pctx, grid):
    if plan is not None:
        pins, pouts, psem = pctx
        pl.when(_grid_edge(grid, False))(lambda: plan.start(pins, pouts, *psem))


def _plan_wait(plan, pctx, grid):
    if plan is not None:
        pins, pouts, psem = pctx
        pl.when(_grid_edge(grid, True))(lambda: plan.wait(pins, pouts, *psem))


def _run_plan(plan, *, name):
    def body(*refs):
        _, _, _, (pins, pouts, psem) = _split_refs(refs, 0, 0, 0, plan)
        plan.start(pins, pouts, *psem)
        plan.wait(pins, pouts, *psem)

    p_in, p_ospec, p_oshape, p_scr, p_alias = _plan_io(plan, 0, 0)
    return pl.pallas_call(body, name=name, in_specs=[_ANY] * len(p_in), out_specs=p_ospec, out_shape=p_oshape,
                          scratch_shapes=p_scr, input_output_aliases=p_alias)(*p_in)


def _fold8(x):
    return x.reshape(x.shape[0] // 8, 8, x.shape[1]).sum(axis=0)


def _ln_stats(r):
    mu = jnp.mean(r, -1, keepdims=True)
    xc = r - mu
    rstd = lax.rsqrt(jnp.mean(xc * xc, -1, keepdims=True) + EPS)
    return xc * rstd, rstd


def _sigmoid(x):
    return jax.nn.sigmoid(x)


def _gelu(x):
    return 0.5 * x * (1.0 + jnp.tanh(GELU_C * (x + GELU_A * x * x * x)))


def _gelu_grad(x):
    t = jnp.tanh(GELU_C * (x + GELU_A * x * x * x))
    return 0.5 * (1.0 + t) + 0.5 * x * (1.0 - t * t) * GELU_C * (1.0 + 3.0 * GELU_A * x * x)


MM_ROWS = 1024
DW_TOKENS = 4096


def _matmul(a, b, *, name, M, N, K, ta=False, tb=False, out_dtype=F32, tm=MM_ROWS, tn=1024, tk=1024,
            a_spec=None, b_spec=None, b_merge=None, out_shape=None, o_spec=None, into=None,
            a_sq=False, mul=None, add=None, add_scale=1.0, n_slots=False, plan=None):
    assert not n_slots or (K // min(tk, K) == 1 and add is None)
    tm, tn, tk = min(tm, M), min(tn, N), min(tk, K)
    assert M % tm == 0 and N % tn == 0 and K % tk == 0
    grid = (M // tm, N // tn, K // tk)
    nk = grid[2]
    if a_spec is None:
        a_spec = pl.BlockSpec((tk, tm), lambda i, j, k: (k, i)) if ta else pl.BlockSpec((tm, tk), lambda i, j, k: (i, k))
    if b_spec is None:
        b_spec = pl.BlockSpec((tn, tk), lambda i, j, k: (j, k)) if tb else pl.BlockSpec((tk, tn), lambda i, j, k: (k, j))
    if o_spec is None:
        o_spec = pl.BlockSpec((tm, tn), lambda i, j, k: (i, j))
        out_shape = jax.ShapeDtypeStruct((M, N), out_dtype)
    e_spec = pl.BlockSpec((tm, tn), lambda i, j, k: (i, j))
    dims = (((0 if ta else 1,), (1 if tb else 0,)), ((), ()))
    extra = [e for e in (mul, add, into) if e is not None]
    n_in = 2 + len(extra)

    def body(*refs):
        ins, outs, scr, pctx = _split_refs(refs, n_in, 1, 1 if nk > 1 else 0, plan)
        a_ref, b_ref = ins[0], ins[1]
        rest = list(ins[2:])
        mul_ref = rest.pop(0) if mul is not None else None
        add_ref = rest.pop(0) if add is not None else None
        o_ref = outs[0]
        _plan_start(plan, pctx, grid)
        av = a_ref[...].astype(BF16)
        if a_sq:
            av = av * av
        bv = b_ref[...]
        if b_merge is not None:
            bv = bv.reshape(b_merge)
        if n_slots:
            for s in range(bv.shape[0]):
                r = lax.dot_general(av, bv[s], dims, preferred_element_type=F32)
                w = r.shape[1]
                if mul_ref is not None:
                    r = r * (2.0 * mul_ref[:, s * w:(s + 1) * w].astype(F32))
                if o_ref.ndim == 3:
                    o_ref[s] = r.astype(o_ref.dtype)
                else:
                    o_ref[:, s * w:(s + 1) * w] = r.astype(o_ref.dtype)
            _plan_wait(plan, pctx, grid)
            return
        if bv.ndim == 3:
            w = av.shape[-1] // (1 if av.ndim == 3 else bv.shape[0])
            a_parts = [av[s] if av.ndim == 3 else av[:, s * w:(s + 1) * w] for s in range(bv.shape[0])]
            p = sum(lax.dot_general(a_parts[s], bv[s], dims, preferred_element_type=F32) for s in range(bv.shape[0]))
        else:
            p = lax.dot_general(av, bv, dims, preferred_element_type=F32)

        def finish(r):
            if mul_ref is not None:
                r = r * (2.0 * mul_ref[...].astype(F32))
            if add_ref is not None:
                r = r + add_scale * add_ref[...]
            o_ref[...] = r.astype(o_ref.dtype)

        if nk == 1:
            finish(p)
        else:
            acc_ref = scr[0]
            k = pl.program_id(2)

            @pl.when(k == 0)
            def _():
                acc_ref[...] = p

            @pl.when(k > 0)
            def _():
                acc_ref[...] += p

            @pl.when(k == nk - 1)
            def _():
                finish(acc_ref[...])

        _plan_wait(plan, pctx, grid)

    p_in, p_ospec, p_oshape, p_scr, p_alias = _plan_io(plan, n_in, 1)
    aliases = dict(p_alias)
    if into is not None:
        aliases[n_in - 1] = 0
    return pl.pallas_call(
        body, name=name, grid=grid,
        in_specs=[a_spec, b_spec] + [e_spec] * (len(extra) - (into is not None)) + [_ANY] * (into is not None)
        + [_ANY] * len(p_in),
        out_specs=[o_spec] + p_ospec, out_shape=[out_shape] + p_oshape,
        scratch_shapes=([pltpu.VMEM((tm, tn), F32)] if nk > 1 else []) + p_scr,
        input_output_aliases=aliases, compiler_params=_params(48, 3),
    )(*_hbm(a, b, *extra), *p_in)


def _rows4_spec(rowblk, n_axes):
    return pl.BlockSpec((4, 256, D), lambda *_: (0, rowblk, 0))


def _residual(h_ref, prev_refs):
    if not prev_refs:
        return h_ref[...]
    xhat, _ = _ln_stats(h_ref[...])
    return xhat * prev_refs[0][...] + prev_refs[1][...]


def _proj_ln(a_b, w, h_prev, g, b, *, name, prev_ln=(), w_rowblk=None, plan=None):
    T = a_b.shape[0]
    tm = min(MM_ROWS, T)
    grid = (T // tm,)
    row = pl.BlockSpec((tm, D), lambda i: (i, 0))
    vec = pl.BlockSpec((1, D), lambda i: (0, 0))
    w_spec = pl.BlockSpec((D, D), lambda i: (0, 0)) if w_rowblk is None else _rows4_spec(w_rowblk, 1)
    n_in = 5 + len(prev_ln)

    def body(*refs):
        ins, (r_ref, hb_ref), _, pctx = _split_refs(refs, n_in, 2, 0, plan)
        a_ref, w_ref, h_ref, g_ref, b_ref = ins[:5]
        _plan_start(plan, pctx, grid)
        mix = jnp.dot(a_ref[...], w_ref[...].reshape(D, D), preferred_element_type=F32)
        r = ALPHA * _residual(h_ref, ins[5:]) + mix
        xhat, _ = _ln_stats(r)
        r_ref[...] = r
        hb_ref[...] = (xhat * g_ref[...] + b_ref[...]).astype(BF16)
        _plan_wait(plan, pctx, grid)

    p_in, p_ospec, p_oshape, p_scr, p_alias = _plan_io(plan, n_in, 2)
    return pl.pallas_call(
        body, name=name, grid=grid,
        in_specs=[row, w_spec, row, vec, vec] + [vec] * len(prev_ln) + [_ANY] * len(p_in),
        out_specs=[row, row] + p_ospec,
        out_shape=[jax.ShapeDtypeStruct((T, D), F32), jax.ShapeDtypeStruct((T, D), BF16)] + p_oshape,
        scratch_shapes=p_scr, input_output_aliases=p_alias, compiler_params=_params(40, 1),
    )(*_hbm(a_b, w, h_prev, g, b, *prev_ln), *p_in)


def _ffn_ln(h_b, wbuf, h, g, b, *, name, prev_ln=(), plan=None):
    T = h_b.shape[0]
    slots = 2
    tm, tf = min(ROW_BLOCK, T), slots * 1024
    nf = 4 // slots
    F = nf * tf
    grid = (T // tm, nf)
    row = pl.BlockSpec((tm, D), lambda i, j: (i, 0))
    vec = pl.BlockSpec((1, D), lambda i, j: (0, 0))
    n_in = 6 + len(prev_ln)

    def body(*refs):
        ins, (ra_ref, r_ref, hbo_ref), (acc_ref,), pctx = _split_refs(refs, n_in, 3, 1, plan)
        hb_ref, w1_ref, w2_ref, h_ref, g_ref, b_ref = ins[:6]
        _plan_start(plan, pctx, grid)
        j = pl.program_id(1)
        hb = hb_ref[...]
        p = None
        for s in range(slots):
            ra = jnp.maximum(jnp.dot(hb, w1_ref[s], preferred_element_type=F32), 0.0)
            ra_ref[:, s * 1024:(s + 1) * 1024] = ra.astype(BF16)
            ps = jnp.dot((ra * ra).astype(BF16), w2_ref[s], preferred_element_type=F32)
            p = ps if p is None else p + ps

        @pl.when(j == 0)
        def _():
            acc_ref[...] = p

        @pl.when(j > 0)
        def _():
            acc_ref[...] += p

        @pl.when(j == nf - 1)
        def _():
            r = ALPHA * _residual(h_ref, ins[6:]) + acc_ref[...]
            xhat, _ = _ln_stats(r)
            r_ref[...] = r
            hbo_ref[...] = (xhat * g_ref[...] + b_ref[...]).astype(BF16)

        _plan_wait(plan, pctx, grid)

    p_in, p_ospec, p_oshape, p_scr, p_alias = _plan_io(plan, n_in, 3)
    return pl.pallas_call(
        body, name=name, grid=grid,
        in_specs=[row, pl.BlockSpec((slots, D, D), lambda i, j: (j, 0, 0)),
                  pl.BlockSpec((slots, D, D), lambda i, j: (j, 1, 0)), row, vec, vec] + [vec] * len(prev_ln)
        + [_ANY] * len(p_in),
        out_specs=[pl.BlockSpec((tm, tf), lambda i, j: (i, j)), row, row] + p_ospec,
        out_shape=[jax.ShapeDtypeStruct((T, F), BF16), jax.ShapeDtypeStruct((T, D), F32),
                   jax.ShapeDtypeStruct((T, D), BF16)] + p_oshape,
        scratch_shapes=[pltpu.VMEM((tm, D), F32)] + p_scr,
        input_output_aliases=p_alias, compiler_params=_params(56, 2),
    )(*_hbm(h_b, wbuf, wbuf, h, g, b, *prev_ln), *p_in)


def _ln_bwd(dy, r, g, *, name, loss_head=()):
    T = r.shape[0]
    tm = min(ROW_BLOCK, T)
    row = pl.BlockSpec((tm, D), lambda i: (i, 0))
    vec = pl.BlockSpec((1, D), lambda i: (0, 0))
    acc = pl.BlockSpec((8, D), lambda i: (0, 0))
    operands, in_specs = ([r, g, *loss_head], [row, vec, vec, row]) if loss_head else ([r, g, dy], [row, vec, row])
    n_in = len(operands)

    def body(*refs):
        r_ref, g_ref = refs[:2]
        dr_ref, drb_ref, dg_ref, db_ref = refs[n_in:n_in + 4]

        @pl.when(pl.program_id(0) == 0)
        def _():
            for ref in refs[n_in + 2:]:
                ref[...] = jnp.zeros_like(ref)

        xhat, rstd = _ln_stats(r_ref[...])
        if loss_head:
            err = xhat * g_ref[...] + refs[2][...] - refs[3][...]
            refs[n_in + 4][...] += _fold8(err * err)
            dy_ = err * (1.0 / D)
        else:
            dy_ = refs[2][...]
        dxh = dy_ * g_ref[...]
        m1 = jnp.mean(dxh, -1, keepdims=True)
        m2 = jnp.mean(dxh * xhat, -1, keepdims=True)
        dr = rstd * (dxh - m1 - xhat * m2)
        dr_ref[...] = dr
        drb_ref[...] = dr.astype(BF16)
        dg_ref[...] += _fold8(dy_ * xhat)
        db_ref[...] += _fold8(dy_)

    n_acc = 3 if loss_head else 2
    return pl.pallas_call(
        body, name=name, grid=(T // tm,), in_specs=in_specs, out_specs=[row, row] + [acc] * n_acc,
        out_shape=[jax.ShapeDtypeStruct((T, D), F32), jax.ShapeDtypeStruct((T, D), BF16)]
        + [jax.ShapeDtypeStruct((8, D), F32)] * n_acc,
        compiler_params=_params(40, 1),
    )(*_hbm(*operands))


def _rope(x, c, s1, s2):
    return x * c + pltpu.roll(x, 112, 1) * s1 + pltpu.roll(x, 16, 1) * s2


def _rope_t(dy, c, s1, s2):
    return dy * c + pltpu.roll(dy * s1, 16, 1) + pltpu.roll(dy * s2, 112, 1)


def _rms(x, g):
    rstd = lax.rsqrt(jnp.mean(x * x, -1, keepdims=True) + EPS)
    xhat = x * rstd
    return xhat * g, xhat, rstd


def _mla_prep(z0, gq, gkv, wq, wk, wv, rc, rs1, rs2):
    T = z0.shape[0]
    tm = min(ROW_BLOCK, T)
    HW = HEADS * 128

    def body(cq_ref, ckv_ref, kr_ref, gq_ref, gkv_ref, wq_ref, wk_ref, wv_ref, c_ref, s1_ref, s2_ref,
             q_ref, k_ref, v_ref):
        nq = _rms(cq_ref[...], gq_ref[...])[0].astype(BF16)
        nkv = _rms(ckv_ref[...], gkv_ref[...])[0].astype(BF16)
        q = jnp.dot(nq, wq_ref[...], preferred_element_type=F32)
        k = jnp.dot(nkv, wk_ref[...], preferred_element_type=F32)
        v = jnp.dot(nkv, wv_ref[...], preferred_element_type=F32)
        c, s1, s2 = c_ref[...], s1_ref[...], s2_ref[...]
        kr = _rope(pltpu.roll(kr_ref[...], 64, 1), c, s1, s2)
        for h in range(HEADS):
            sl = slice(h * 128, (h + 1) * 128)
            q_ref[:, sl] = (_rope(q[:, sl], c, s1, s2) * QK_SCALE).astype(BF16)
            k_ref[:, sl] = (k[:, sl] + kr).astype(BF16)
        v_ref[...] = v.astype(BF16)

    full = lambda shape: pl.BlockSpec(shape, lambda i: (0, 0))
    tab = pl.BlockSpec((tm, 128), lambda i: (i, 0))
    return pl.pallas_call(
        body, name="mla_prep", grid=(T // tm,),
        in_specs=[pl.BlockSpec((tm, 256), lambda i: (i, 0)), pl.BlockSpec((tm, 256), lambda i: (i, 1)),
                  pl.BlockSpec((tm, 128), lambda i: (i, 12)), full((1, 256)), full((1, 256)),
                  full((256, HW)), full((256, HW)), full((256, 512)), tab, tab, tab],
        out_specs=[pl.BlockSpec((tm, HW), lambda i: (i, 0)), pl.BlockSpec((tm, HW), lambda i: (i, 0)),
                   pl.BlockSpec((tm, 512), lambda i: (i, 0))],
        out_shape=[jax.ShapeDtypeStruct((T, HW), BF16), jax.ShapeDtypeStruct((T, HW), BF16),
                   jax.ShapeDtypeStruct((T, 512), BF16)],
        compiler_params=_params(40, 1),
    )(z0, z0, z0, gq, gkv, wq, wk, wv, rc, rs1, rs2)


def _flash_fwd(q, k, v, plan=None):
    T = q.shape[0]
    bq = min(2 * ROW_BLOCK, T)
    nq = T // bq
    pairs = [(i, j) for i in range(nq) for j in range(i + 1)]
    imap, jmap = (jnp.array(m, jnp.int32) for m in zip(*pairs))
    grid = (4, len(pairs))

    def body(imap_ref, jmap_ref, *refs):
        (q_ref, k_ref, v_ref), (o_ref, lse_ref), (m_sc, acc_sc), pctx = _split_refs(refs, 3, 2, 2, plan)
        _plan_start(plan, pctx, grid)
        i, j = imap_ref[pl.program_id(1)], jmap_ref[pl.program_id(1)]
        first = lax.broadcasted_iota(jnp.int32, (bq, 128), 1) < 64

        @pl.when(j == 0)
        def _():
            m_sc[...] = jnp.full_like(m_sc, -jnp.inf)
            acc_sc[...] = jnp.zeros_like(acc_sc)

        def step(masked):
            vp = v_ref[...]
            for h in range(2):
                sl = slice(h * 128, (h + 1) * 128)
                s = lax.dot_general(q_ref[:, sl], k_ref[:, sl], NT_DIMS, preferred_element_type=F32)
                if masked:
                    rows = lax.broadcasted_iota(jnp.int32, (bq, bq), 0)
                    cols = lax.broadcasted_iota(jnp.int32, (bq, bq), 1)
                    s = jnp.where(cols <= rows, s, -jnp.inf)
                m_prev = m_sc[h, :, 0:1]
                m_new = jnp.maximum(m_prev, jnp.max(s, axis=1, keepdims=True))
                alpha = jnp.exp(m_prev - m_new)
                p = jnp.exp(s - m_new).astype(BF16)
                vh = jnp.where(first if h == 0 else jnp.logical_not(first), vp, jnp.ones_like(vp))
                acc_sc[h] = acc_sc[h] * alpha + jnp.dot(p, vh, preferred_element_type=F32)
                m_sc[h] = jnp.broadcast_to(m_new, (bq, 128))

        @pl.when(j < i)
        def _():
            step(False)

        @pl.when(j == i)
        def _():
            step(True)
            a0, a1 = acc_sc[0], acc_sc[1]
            l0, l1 = pltpu.roll(a0, 64, 1), pltpu.roll(a1, 64, 1)
            o_ref[...] = jnp.where(first, a0 / l0, a1 / l1).astype(BF16)
            lse_ref[...] = jnp.where(first, m_sc[0] + jnp.log(l0), m_sc[1] + jnp.log(l1))

        _plan_wait(plan, pctx, grid)

    qi = lambda hp, t, im, jm: (im[t], hp)
    kj = lambda hp, t, im, jm: (jm[t], hp)
    p_in, p_ospec, p_oshape, p_scr, p_alias = _plan_io(plan, 2 + 3, 2)
    return pl.pallas_call(
        body, name="flash_fwd",
        out_shape=[jax.ShapeDtypeStruct((T, 512), BF16), jax.ShapeDtypeStruct((T, 512), F32)] + p_oshape,
        grid_spec=pltpu.PrefetchScalarGridSpec(
            num_scalar_prefetch=2, grid=grid,
            in_specs=[pl.BlockSpec((bq, 256), qi), pl.BlockSpec((bq, 256), kj), pl.BlockSpec((bq, 128), kj)]
            + [_ANY] * len(p_in),
            out_specs=[pl.BlockSpec((bq, 128), qi), pl.BlockSpec((bq, 128), qi)] + p_ospec,
            scratch_shapes=[pltpu.VMEM((2, bq, 128), F32), pltpu.VMEM((2, bq, 128), F32)] + p_scr),
        input_output_aliases=p_alias, compiler_params=_params(56, 2),
    )(imap, jmap, *_hbm(q, k, v), *p_in)


def _attn_delta(dmix, o):
    T = o.shape[0]
    tm = min(ROW_BLOCK, T)
    blk = pl.BlockSpec((tm, 512), lambda i: (i, 0))

    def body(do_ref, o_ref, delta_ref, dob_ref):
        first = lax.broadcasted_iota(jnp.int32, (tm, 128), 1) < 64
        for hp in range(4):
            sl = slice(hp * 128, (hp + 1) * 128)
            prod = do_ref[:, sl] * o_ref[:, sl].astype(F32)
            d0 = jnp.sum(jnp.where(first, prod, 0.0), axis=1, keepdims=True)
            d1 = jnp.sum(jnp.where(first, 0.0, prod), axis=1, keepdims=True)
            delta_ref[:, sl] = jnp.where(first, d0, d1)
        dob_ref[...] = do_ref[...].astype(BF16)

    return pl.pallas_call(
        body, name="attn_delta", grid=(T // tm,), in_specs=[blk, blk], out_specs=[blk, blk],
        out_shape=[jax.ShapeDtypeStruct((T, 512), F32), jax.ShapeDtypeStruct((T, 512), BF16)],
        compiler_params=_params(32, 1),
    )(dmix, o)


def _flash_bwd(q, k, v, do_b, lse, delta, plan=None):
    T = q.shape[0]
    bq = min(2 * ROW_BLOCK, T)
    nq = T // bq
    pairs = [(i, j) for j in range(nq) for i in range(j, nq)]
    imap, jmap = (jnp.array(m, jnp.int32) for m in zip(*pairs))
    grid = (4, len(pairs))

    def body(imap_ref, jmap_ref, *refs):
        ((q_ref, k_ref, v_ref, do_ref, lse_ref, dl_ref), (dq_hbm, dk_ref, dv_ref), (dq_sc, dk_sc, dv_sc, sem),
         pctx) = _split_refs(refs, 6, 3, 4, plan)
        _plan_start(plan, pctx, grid)
        hp = pl.program_id(0)
        i, j = imap_ref[pl.program_id(1)], jmap_ref[pl.program_id(1)]
        first = lax.broadcasted_iota(jnp.int32, (bq, 128), 1) < 64

        @pl.when((j == 0) & (i == 0))
        def _():
            dq_sc[...] = jnp.zeros_like(dq_sc)

        @pl.when(i == j)
        def _():
            dk_sc[...] = jnp.zeros_like(dk_sc)
            dv_sc[...] = jnp.zeros_like(dv_sc)

        def tile(r0, nr, nc, masked):
            rs, cs = slice(r0, r0 + nr), slice(0, nc)
            vp = v_ref[cs, :]
            do = do_ref[rs, :]
            lanes = first[rs, :]
            for h in range(2):
                sl = slice(h * 128, (h + 1) * 128)
                qh, kh = q_ref[rs, sl], k_ref[cs, sl]
                s = lax.dot_general(qh, kh, NT_DIMS, preferred_element_type=F32)
                p = jnp.exp(s - lse_ref[rs, h * 64:h * 64 + 1])
                if masked:
                    rows = r0 + lax.broadcasted_iota(jnp.int32, (nr, nc), 0)
                    cols = lax.broadcasted_iota(jnp.int32, (nr, nc), 1)
                    p = jnp.where(cols <= rows, p, 0.0)
                do_h = jnp.where(lanes if h == 0 else jnp.logical_not(lanes), do, jnp.zeros_like(do))
                dv_sc[cs, :] += lax.dot_general(p.astype(BF16), do_h, TN_DIMS, preferred_element_type=F32)
                dp = lax.dot_general(do_h, vp, NT_DIMS, preferred_element_type=F32)
                ds = (p * (dp - dl_ref[rs, h * 64:h * 64 + 1])).astype(BF16)
                dq_sc[i, rs, sl] += jnp.dot(ds, kh, preferred_element_type=F32)
                dk_sc[cs, sl] += lax.dot_general(ds, qh, TN_DIMS, preferred_element_type=F32)

        @pl.when(i > j)
        def _():
            tile(0, bq, bq, False)

        @pl.when(i == j)
        def _():
            tile(0, bq // 2, bq // 2, True)
            tile(bq // 2, bq // 2, bq, True)

        @pl.when(i == nq - 1)
        def _():
            dk_ref[...] = dk_sc[...]
            dv_ref[...] = dv_sc[...]

        @pl.when((j == nq - 1) & (i == nq - 1))
        def _():
            cp = pltpu.make_async_copy(dq_sc, dq_hbm.at[hp], sem)
            cp.start()
            cp.wait()

        _plan_wait(plan, pctx, grid)

    qi = lambda hp, t, im, jm: (im[t], hp)
    kj = lambda hp, t, im, jm: (jm[t], hp)
    p_in, p_ospec, p_oshape, p_scr, p_alias = _plan_io(plan, 2 + 6, 3)
    return pl.pallas_call(
        body, name="flash_bwd",
        out_shape=[jax.ShapeDtypeStruct((4, nq, bq, 256), F32), jax.ShapeDtypeStruct((T, 1024), F32),
                   jax.ShapeDtypeStruct((T, 512), F32)] + p_oshape,
        grid_spec=pltpu.PrefetchScalarGridSpec(
            num_scalar_prefetch=2, grid=grid,
            in_specs=[pl.BlockSpec((bq, 256), qi), pl.BlockSpec((bq, 256), kj), pl.BlockSpec((bq, 128), kj),
                      pl.BlockSpec((bq, 128), qi), pl.BlockSpec((bq, 128), qi), pl.BlockSpec((bq, 128), qi)]
            + [_ANY] * len(p_in),
            out_specs=[_ANY, pl.BlockSpec((bq, 256), kj), pl.BlockSpec((bq, 128), kj)] + p_ospec,
            scratch_shapes=[pltpu.VMEM((nq, bq, 256), F32), pltpu.VMEM((bq, 256), F32), pltpu.VMEM((bq, 128), F32),
                            pltpu.SemaphoreType.DMA] + p_scr),
        input_output_aliases=p_alias, compiler_params=_params(56, 2),
    )(imap, jmap, *_hbm(q, k, v, do_b, lse, delta), *p_in)


def _mla_bwd(z0, dq4, dk, dv, gq, gkv, wq, wk, wv, rc, rs1, rs2, plan=None):
    T = z0.shape[0]
    tm = min(ROW_BLOCK, T)
    HW = HEADS * 128
    grid = (T // tm,)
    dq4 = dq4.reshape(4, T, 256)

    def body(*refs):
        ((cq_ref, ckv_ref, dq_ref, dk_ref, dv_ref, gq_ref, gkv_ref, wq_ref, wk_ref, wv_ref, c_ref, s1_ref, s2_ref),
         (dc_ref, dkr_ref, dwq_ref, dwk_ref, dwv_ref, dgq_ref, dgkv_ref), _, pctx) = _split_refs(refs, 13, 7, 0, plan)
        _plan_start(plan, pctx, grid)

        @pl.when(pl.program_id(0) == 0)
        def _():
            for ref in (dwq_ref, dwk_ref, dwv_ref, dgq_ref, dgkv_ref):
                ref[...] = jnp.zeros_like(ref)

        c, s1, s2 = c_ref[...], s1_ref[...], s2_ref[...]
        lane = lax.broadcasted_iota(jnp.int32, (tm, 128), 1)
        nq, xq, rq = _rms(cq_ref[...], gq_ref[...])
        nkv, xkv, rkv = _rms(ckv_ref[...], gkv_ref[...])
        nq_b, nkv_b = nq.astype(BF16), nkv.astype(BF16)

        dq_parts, dk_parts = [], []
        dkr = jnp.zeros((tm, 128), F32)
        for h in range(HEADS):
            blk = dq_ref[h // 2, :, (h % 2) * 128:(h % 2 + 1) * 128] * QK_SCALE
            dq_parts.append(_rope_t(blk, c, s1, s2).astype(BF16))
            kb = dk_ref[:, h * 128:(h + 1) * 128]
            dk_parts.append(jnp.where(lane < NOPE, kb, 0.0).astype(BF16))
            dkr = dkr + kb
        dq_b = jnp.concatenate(dq_parts, axis=1)
        dk_b = jnp.concatenate(dk_parts, axis=1)
        dv_b = dv_ref[...].astype(BF16)

        dwq_ref[...] += lax.dot_general(nq_b, dq_b, TN_DIMS, preferred_element_type=F32)
        dwk_ref[...] += lax.dot_general(nkv_b, dk_b, TN_DIMS, preferred_element_type=F32)
        dwv_ref[...] += lax.dot_general(nkv_b, dv_b, TN_DIMS, preferred_element_type=F32)
        dnq = lax.dot_general(dq_b, wq_ref[...], NT_DIMS, preferred_element_type=F32)
        dnkv = (lax.dot_general(dk_b, wk_ref[...], NT_DIMS, preferred_element_type=F32)
                + lax.dot_general(dv_b, wv_ref[...], NT_DIMS, preferred_element_type=F32))

        def rms_bwd(dn, xhat, rstd, g):
            dxh = dn * g
            return rstd * (dxh - xhat * jnp.mean(dxh * xhat, -1, keepdims=True))

        dc_ref[:, :256] = rms_bwd(dnq, xq, rq, gq_ref[...]).astype(BF16)
        dc_ref[:, 256:] = rms_bwd(dnkv, xkv, rkv, gkv_ref[...]).astype(BF16)
        dgq_ref[...] += _fold8(dnq * xq)
        dgkv_ref[...] += _fold8(dnkv * xkv)
        dkr = pltpu.roll(_rope_t(dkr, c, s1, s2), 64, 1)
        dkr_ref[...] = jnp.where(lane < ROPE, dkr, 0.0).astype(BF16)
        _plan_wait(plan, pctx, grid)

    full = lambda shape: pl.BlockSpec(shape, lambda i: (0,) * len(shape))
    tab = pl.BlockSpec((tm, 128), lambda i: (i, 0))
    p_in, p_ospec, p_oshape, p_scr, p_alias = _plan_io(plan, 13, 7)
    return pl.pallas_call(
        body, name="mla_bwd", grid=grid,
        in_specs=[pl.BlockSpec((tm, 256), lambda i: (i, 0)), pl.BlockSpec((tm, 256), lambda i: (i, 1)),
                  pl.BlockSpec((4, tm, 256), lambda i: (0, i, 0)),
                  pl.BlockSpec((tm, HW), lambda i: (i, 0)), pl.BlockSpec((tm, 512), lambda i: (i, 0)),
                  full((1, 256)), full((1, 256)), full((256, HW)), full((256, HW)), full((256, 512)), tab, tab, tab]
        + [_ANY] * len(p_in),
        out_specs=[pl.BlockSpec((tm, 512), lambda i: (i, 0)), tab, full((256, HW)), full((256, HW)),
                   full((256, 512)), full((8, 256)), full((8, 256))] + p_ospec,
        out_shape=[jax.ShapeDtypeStruct((T, 512), BF16), jax.ShapeDtypeStruct((T, 128), BF16),
                   jax.ShapeDtypeStruct((256, HW), F32), jax.ShapeDtypeStruct((256, HW), F32),
                   jax.ShapeDtypeStruct((256, 512), F32), jax.ShapeDtypeStruct((8, 256), F32),
                   jax.ShapeDtypeStruct((8, 256), F32)] + p_oshape,
        scratch_shapes=p_scr, input_output_aliases=p_alias, compiler_params=_params(48, 1),
    )(*_hbm(z0, z0, dq4, dk, dv, gq, gkv, wq, wk, wv, rc, rs1, rs2), *p_in)


def _sgu_fwd(z0, a_out, ln_g, ln_b, w, b_t):
    T = z0.shape[0]
    tm = min(ROW_BLOCK, T)
    W = SGU_G * SGU_C

    def body(u_ref, v_ref, a_ref, g_ref, b_ref, w_ref, bt_ref, o_ref):
        o_ref[:, :W] = a_ref[...]
        ug = _gelu(u_ref[...])
        xhat, _ = _ln_stats(_gelu(v_ref[...]))
        vn = (xhat * g_ref[...] + b_ref[...]).astype(BF16)
        tril = lax.broadcasted_iota(jnp.int32, (SGU_C, SGU_C), 0) >= lax.broadcasted_iota(jnp.int32, (SGU_C, SGU_C), 1)
        for g in range(SGU_G):
            cs = slice(g * SGU_C, (g + 1) * SGU_C)
            wg = jnp.where(tril, w_ref[g], 0.0).astype(BF16)
            bcol = bt_ref[:, g:g + 1]
            for c in range(tm // SGU_C):
                rs = slice(c * SGU_C, (c + 1) * SGU_C)
                mixed = jnp.dot(wg, vn[rs, cs], preferred_element_type=F32) + bcol
                o_ref[rs, W + g * SGU_C:W + (g + 1) * SGU_C] = (ug[rs, cs] * mixed).astype(BF16)

    full = lambda shape: pl.BlockSpec(shape, lambda i: (0,) * len(shape))
    return pl.pallas_call(
        body, name="sgu_fwd", grid=(T // tm,),
        in_specs=[pl.BlockSpec((tm, W), lambda i: (i, 1)), pl.BlockSpec((tm, W), lambda i: (i, 2)),
                  pl.BlockSpec((tm, W), lambda i: (i, 0)),
                  full((1, W)), full((1, W)), full((SGU_G, SGU_C, SGU_C)), full((SGU_C, SGU_G))],
        out_specs=pl.BlockSpec((tm, 2 * W), lambda i: (i, 0)),
        out_shape=jax.ShapeDtypeStruct((T, 2 * W), BF16),
        compiler_params=_params(32, 1),
    )(z0, z0, a_out, ln_g, ln_b, w, b_t)


def _sgu_bwd(z0, dmix, dc, dkr, ln_g, ln_b, w, b_t, plan=None):
    T = z0.shape[0]
    tm = min(ROW_BLOCK, T)
    W = SGU_G * SGU_C
    grid = (T // tm,)

    def body(*refs):
        ((u_ref, v_ref, do_ref, dc_ref, dkr_ref, g_ref, b_ref, w_ref, bt_ref),
         (dz_ref, dw_ref, db_ref, dlg_ref, dlb_ref), _, pctx) = _split_refs(refs, 9, 5, 0, plan)
        _plan_start(plan, pctx, grid)

        @pl.when(pl.program_id(0) == 0)
        def _():
            for ref in (dw_ref, db_ref, dlg_ref, dlb_ref):
                ref[...] = jnp.zeros_like(ref)

        dz_ref[:, :W] = dc_ref[...]
        dz_ref[:, 3 * W:] = dkr_ref[...]

        u, v, dout = u_ref[...], v_ref[...], do_ref[...]
        ug = _gelu(u)
        xhat, rstd = _ln_stats(_gelu(v))
        vn = (xhat * g_ref[...] + b_ref[...]).astype(BF16)
        dmixed = dout * ug
        dmixed_b = dmixed.astype(BF16)
        tril = lax.broadcasted_iota(jnp.int32, (SGU_C, SGU_C), 0) >= lax.broadcasted_iota(jnp.int32, (SGU_C, SGU_C), 1)
        lane = lax.broadcasted_iota(jnp.int32, (SGU_C, SGU_C), 1)
        dvn_cols = []
        for g in range(SGU_G):
            cs = slice(g * SGU_C, (g + 1) * SGU_C)
            wg = jnp.where(tril, w_ref[g], 0.0).astype(BF16)
            bcol = bt_ref[:, g:g + 1]
            dw_g = jnp.zeros((SGU_C, SGU_C), F32)
            db_g = jnp.zeros((SGU_C, 1), F32)
            dvn_rows = []
            for c in range(tm // SGU_C):
                rs = slice(c * SGU_C, (c + 1) * SGU_C)
                mixed = jnp.dot(wg, vn[rs, cs], preferred_element_type=F32) + bcol
                dz_ref[rs, W + g * SGU_C:W + (g + 1) * SGU_C] = (dout[rs, cs] * mixed * _gelu_grad(u[rs, cs])).astype(BF16)
                dm = dmixed_b[rs, cs]
                dw_g = dw_g + lax.dot_general(dm, vn[rs, cs], NT_DIMS, preferred_element_type=F32)
                db_g = db_g + jnp.sum(dmixed[rs, cs], axis=1, keepdims=True)
                dvn_rows.append(lax.dot_general(wg, dm, TN_DIMS, preferred_element_type=F32))
            dw_ref[g] += jnp.where(tril, dw_g, 0.0)
            db_ref[...] += jnp.where(lane == g, db_g, 0.0)
            dvn_cols.append(jnp.concatenate(dvn_rows, axis=0))
        dvn = jnp.concatenate(dvn_cols, axis=1)
        dxh = dvn * g_ref[...]
        m1 = jnp.mean(dxh, -1, keepdims=True)
        m2 = jnp.mean(dxh * xhat, -1, keepdims=True)
        dvg = rstd * (dxh - m1 - xhat * m2)
        dz_ref[:, 2 * W:3 * W] = (dvg * _gelu_grad(v)).astype(BF16)
        dlg_ref[...] += _fold8(dvn * xhat)
        dlb_ref[...] += _fold8(dvn)
        _plan_wait(plan, pctx, grid)

    full = lambda shape: pl.BlockSpec(shape, lambda i: (0,) * len(shape))
    p_in, p_ospec, p_oshape, p_scr, p_alias = _plan_io(plan, 9, 5)
    return pl.pallas_call(
        body, name="sgu_bwd", grid=grid,
        in_specs=[pl.BlockSpec((tm, W), lambda i: (i, 1)), pl.BlockSpec((tm, W), lambda i: (i, 2)),
                  pl.BlockSpec((tm, W), lambda i: (i, 1)), pl.BlockSpec((tm, W), lambda i: (i, 0)),
                  pl.BlockSpec((tm, 128), lambda i: (i, 0)),
                  full((1, W)), full((1, W)), full((SGU_G, SGU_C, SGU_C)), full((SGU_C, SGU_G))] + [_ANY] * len(p_in),
        out_specs=[pl.BlockSpec((tm, 3 * W + 128), lambda i: (i, 0)), full((SGU_G, SGU_C, SGU_C)),
                   full((SGU_C, SGU_C)), full((8, W)), full((8, W))] + p_ospec,
        out_shape=[jax.ShapeDtypeStruct((T, 3 * W + 128), BF16), jax.ShapeDtypeStruct((SGU_G, SGU_C, SGU_C), F32),
                   jax.ShapeDtypeStruct((SGU_C, SGU_C), F32), jax.ShapeDtypeStruct((8, W), F32),
                   jax.ShapeDtypeStruct((8, W), F32)] + p_oshape,
        scratch_shapes=p_scr, input_output_aliases=p_alias, compiler_params=_params(40, 1),
    )(z0, z0, dmix, dc, dkr, ln_g, ln_b, w, b_t, *p_in)


def _hg_lower_bound(lb_ref):
    a0, a1 = lb_ref[0:1, :], lb_ref[1:2, :]
    m = jnp.maximum(a0, a1)
    e0, e1 = jnp.exp(a0 - m), jnp.exp(a1 - m)
    return e1 / (e0 + e1)


def _running_sum(x, reverse=False):
    n = x.shape[0]
    row = lax.broadcasted_iota(jnp.int32, x.shape, 0)
    s = 1
    while s < n:
        if reverse:
            x = x + jnp.where(row < n - s, pltpu.roll(x, n - s, 0), 0.0)
        else:
            x = x + jnp.where(row >= s, pltpu.roll(x, s, 0), 0.0)
        s *= 2
    return x


def _hg_chunk(qc, fc, lb):
    C = HG_CHUNK
    rows = lax.broadcasted_iota(jnp.int32, (C, C), 0)
    cols = lax.broadcasted_iota(jnp.int32, (C, C), 1)
    rowid = lax.broadcasted_iota(jnp.int32, (C, 128), 0)
    sq, sg = _sigmoid(qc), _sigmoid(fc)
    qf = qc * sq
    gate = lb + (1.0 - lb) * sg
    kk = 1.0 - gate
    lg = jnp.log(gate)
    bcum = _running_sum(lg)
    b_mid = jnp.sum(jnp.where(rowid < C // 2, lg, 0.0), axis=0, keepdims=True)
    b_last = jnp.sum(lg, axis=0, keepdims=True)
    eq, ek, e, eh = jnp.exp(bcum - b_mid), jnp.exp(b_mid - bcum), jnp.exp(bcum), jnp.exp(b_last - bcum)
    qt, kt, qe, khat = qf * eq, kk * ek, qf * e, kk * eh
    a = lax.dot_general(qt.astype(BF16), kt.astype(BF16), NT_DIMS, preferred_element_type=F32)
    a = jnp.where(rows >= cols, a, 0.0)
    return dict(sq=sq, sg=sg, gate=gate, kk=kk, eq=eq, ek=ek, e=e, eh=eh, qt=qt, kt=kt, qe=qe, khat=khat, a=a,
                e_last=jnp.exp(b_last), tril=rows >= cols, rowid=rowid)


def _hgrn_fwd(z4, hg_lb, gnorm):
    T = z4.shape[1]
    tb = min(ROW_BLOCK, T)
    C = HG_CHUNK
    ncb = tb // C
    HPB = HG_HEADS_PER_STEP

    def body(q_ref, f_ref, i_ref, g_ref, lb_ref, gn_ref, y_ref, o_ref, st_ref, st_sc):
        @pl.when(pl.program_id(1) == 0)
        def _():
            st_sc[...] = jnp.zeros_like(st_sc)

        def chunk(c, carry):
            rs = pl.ds(pl.multiple_of(c * C, C), C)
            for hh in range(HPB):
                hs = slice(hh * 128, (hh + 1) * 128)
                lb = _hg_lower_bound(lb_ref.at[:, hs])
                v_b = i_ref[rs, hs].astype(BF16)
                gc = g_ref[rs, hs]
                x = _hg_chunk(q_ref[rs, hs], f_ref[rs, hs], lb)
                st = st_sc[hh]
                st_ref[hh, c] = st
                o = (jnp.dot(x["a"].astype(BF16), v_b, preferred_element_type=F32)
                     + lax.dot_general(x["qe"].astype(BF16), st.astype(BF16), NT_DIMS, preferred_element_type=F32))
                st_sc[hh] = st * x["e_last"] + lax.dot_general(v_b, x["khat"].astype(BF16), TN_DIMS,
                                                               preferred_element_type=F32)
                o_ref[rs, hs] = o
                n = o * lax.rsqrt(jnp.mean(o * o, -1, keepdims=True) + EPS)
                y_ref[rs, hs] = (n * gn_ref[:, hs] * (gc * _sigmoid(gc))).astype(BF16)
            return carry

        lax.fori_loop(0, ncb, chunk, 0, unroll=4)

    W = 128 * HPB
    zb = lambda k: pl.BlockSpec((None, tb, W), lambda h, t: (k, t, h))
    out = pl.BlockSpec((tb, W), lambda h, t: (t, h))
    return pl.pallas_call(
        body, name="hgrn_fwd", grid=(HEADS // HPB, T // tb),
        in_specs=[zb(0), zb(1), zb(2), zb(3), pl.BlockSpec((2, W), lambda h, t: (0, h)),
                  pl.BlockSpec((1, W), lambda h, t: (0, h))],
        out_specs=[out, out, pl.BlockSpec((HPB, ncb, 128, 128), lambda h, t: (h, t, 0, 0))],
        out_shape=[jax.ShapeDtypeStruct((T, D), BF16), jax.ShapeDtypeStruct((T, D), F32),
                   jax.ShapeDtypeStruct((HEADS, T // C, 128, 128), F32)],
        scratch_shapes=[pltpu.VMEM((HPB, 128, 128), F32)],
        compiler_params=_params(48, 2),
    )(*_hbm(z4, z4, z4, z4, hg_lb, gnorm))


def _hgrn_bwd(z4, o_raw, dy, states, hg_lb, gnorm):
    T = z4.shape[1]
    tb = min(ROW_BLOCK, T)
    C = HG_CHUNK
    ncb = tb // C
    nt = T // tb
    HPB = HG_HEADS_PER_STEP

    def body(q_ref, f_ref, i_ref, g_ref, o_ref, dy_ref, st_ref, lb_ref, gn_ref, dz_ref, dlb_ref, dgn_ref, dst_sc):
        @pl.when(pl.program_id(1) == 0)
        def _():
            dst_sc[...] = jnp.zeros_like(dst_sc)
            dlb_ref[...] = jnp.zeros_like(dlb_ref)
            dgn_ref[...] = jnp.zeros_like(dgn_ref)

        def chunk(cc, carry):
            for hh in range(HPB):
                one_head(ncb - 1 - cc, hh, slice(hh * 128, (hh + 1) * 128))
            return carry

        def one_head(c, hh, hs):
            rs = pl.ds(pl.multiple_of(c * C, C), C)
            lb = _hg_lower_bound(lb_ref.at[:, hs])
            gn = gn_ref[:, hs]
            qc, gc = q_ref[rs, hs], g_ref[rs, hs]
            v_b = i_ref[rs, hs].astype(BF16)
            x = _hg_chunk(qc, f_ref[rs, hs], lb)
            st, dst = st_ref[hh, c], dst_sc[hh]
            st_b, dst_b = st.astype(BF16), dst.astype(BF16)
            o, dyc = o_ref[rs, hs], dy_ref[rs, hs]
            sgg = _sigmoid(gc)
            sil = gc * sgg
            rstd = lax.rsqrt(jnp.mean(o * o, -1, keepdims=True) + EPS)
            n = o * rstd
            dgn_ref[:, hs] += _fold8(dyc * n * sil)
            dn = dyc * gn * sil
            do = rstd * (dn - n * jnp.mean(dn * n, -1, keepdims=True))
            dg = dyc * n * gn * (sgg * (1.0 + gc * (1.0 - sgg)))
            do_b = do.astype(BF16)
            da = jnp.where(x["tril"], lax.dot_general(do_b, v_b, NT_DIMS, preferred_element_type=F32), 0.0).astype(BF16)
            qt_b, kt_b, qe_b, khat_b = (x[n_].astype(BF16) for n_ in ("qt", "kt", "qe", "khat"))
            dv = (lax.dot_general(x["a"].astype(BF16), do_b, TN_DIMS, preferred_element_type=F32)
                  + lax.dot_general(khat_b, dst_b, NT_DIMS, preferred_element_type=F32))
            dqt = jnp.dot(da, kt_b, preferred_element_type=F32)
            dqe = jnp.dot(do_b, st_b, preferred_element_type=F32)
            dkt = lax.dot_general(da, qt_b, TN_DIMS, preferred_element_type=F32)
            dkhat = jnp.dot(v_b, dst_b, preferred_element_type=F32)
            dst_sc[hh] = lax.dot_general(do_b, qe_b, TN_DIMS, preferred_element_type=F32) + dst * x["e_last"]
            de_last = jnp.sum(st * dst, axis=0, keepdims=True)
            dqf = dqt * x["eq"] + dqe * x["e"]
            dkk = dkt * x["ek"] + dkhat * x["eh"]
            dkh_kh = dkhat * x["khat"]
            db = dqt * qt_b.astype(F32) - dkt * kt_b.astype(F32) + dqe * x["qe"] - dkh_kh
            db_last = jnp.sum(dkh_kh, axis=0, keepdims=True) + de_last * x["e_last"]
            db = db + jnp.where(x["rowid"] == C - 1, db_last, 0.0)
            dlg = _running_sum(db, reverse=True)
            dgate = dlg / x["gate"] - dkk
            sg, sq = x["sg"], x["sq"]
            dlb_ref[:, hs] += _fold8(dgate * (1.0 - sg)) * (lb * (1.0 - lb))
            dz_ref[0, rs, hs] = (dqf * (sq * (1.0 + qc * (1.0 - sq)))).astype(BF16)
            dz_ref[1, rs, hs] = (dgate * (1.0 - lb) * sg * (1.0 - sg)).astype(BF16)
            dz_ref[2, rs, hs] = dv.astype(BF16)
            dz_ref[3, rs, hs] = dg.astype(BF16)

        lax.fori_loop(0, ncb, chunk, 0, unroll=4)

    W = 128 * HPB
    zb = lambda k: pl.BlockSpec((None, tb, W), lambda h, t: (k, nt - 1 - t, h))
    blk = pl.BlockSpec((tb, W), lambda h, t: (nt - 1 - t, h))
    acc = pl.BlockSpec((8, W), lambda h, t: (0, h))
    return pl.pallas_call(
        body, name="hgrn_bwd", grid=(HEADS // HPB, nt),
        in_specs=[zb(0), zb(1), zb(2), zb(3), blk, blk,
                  pl.BlockSpec((HPB, ncb, 128, 128), lambda h, t: (h, nt - 1 - t, 0, 0)),
                  pl.BlockSpec((2, W), lambda h, t: (0, h)), pl.BlockSpec((1, W), lambda h, t: (0, h))],
        out_specs=[pl.BlockSpec((4, tb, W), lambda h, t: (0, nt - 1 - t, h)), acc, acc],
        out_shape=[jax.ShapeDtypeStruct((4, T, D), BF16), jax.ShapeDtypeStruct((8, D), F32),
                   jax.ShapeDtypeStruct((8, D), F32)],
        scratch_shapes=[pltpu.VMEM((HPB, 128, 128), F32)],
        compiler_params=_params(48, 2),
    )(*_hbm(z4, z4, z4, z4, o_raw, dy, states, hg_lb, gnorm))


def _adamw(w, g, m, v, *, name, with_g=False):
    R, L = w.shape
    tr = R if R <= 512 else 512
    assert R % tr == 0
    blk = pl.BlockSpec((tr, L), lambda i: (i, 0))
    c1, c2 = 1.0 - B1 ** STEP, 1.0 - B2 ** STEP
    n_out = 4 if with_g else 3

    def body(w_ref, g_ref, m_ref, v_ref, *o_refs):
        d_ref, mo_ref, vo_ref = o_refs[-3:]
        g_ = g_ref[...]
        m_ = B1 * m_ref[...] + (1.0 - B1) * g_
        v_ = B2 * v_ref[...] + (1.0 - B2) * (g_ * g_)
        if with_g:
            o_refs[0][...] = g_
        d_ref[...] = -LR * ((m_ / c1) / (jnp.sqrt(v_ / c2) + ADAM_EPS) + WD * w_ref[...])
        mo_ref[...] = m_
        vo_ref[...] = v_

    sds = jax.ShapeDtypeStruct((R, L), F32)
    return pl.pallas_call(
        body, name=name, grid=(R // tr,), in_specs=[blk] * 4, out_specs=[blk] * n_out, out_shape=[sds] * n_out,
        compiler_params=_params(32, 1),
    )(w, g, m, v)


def _adamw_rows(w, m, v, gbufs, row0, *, name, plan=None):
    L, R, C = w.shape
    tr = 256
    assert R % tr == 0 and row0 % tr == 0 and len(gbufs) == L
    grid = (L, R // tr)
    blk = pl.BlockSpec((None, tr, C), lambda l, i: (l, i, 0))
    gblks = [pl.BlockSpec((tr, C), lambda l, i, k=k: (row0 // tr + jnp.where(l == k, i, 0), 0)) for k in range(L)]
    c1, c2 = 1.0 - B1 ** STEP, 1.0 - B2 ** STEP

    def body(*refs):
        ins, (go_ref, d_ref, mo_ref, vo_ref), _, pctx = _split_refs(refs, 3 + L, 4, 0, plan)
        w_ref, m_ref, v_ref = ins[:3]
        g_refs = ins[3:]
        _plan_start(plan, pctx, grid)
        g_ = g_refs[0][...]
        for l in range(1, L):
            g_ = jnp.where(pl.program_id(0) == l, g_refs[l][...], g_)
        m_ = B1 * m_ref[...] + (1.0 - B1) * g_
        v_ = B2 * v_ref[...] + (1.0 - B2) * (g_ * g_)
        go_ref[...] = g_
        d_ref[...] = -LR * ((m_ / c1) / (jnp.sqrt(v_ / c2) + ADAM_EPS) + WD * w_ref[...])
        mo_ref[...] = m_
        vo_ref[...] = v_
        _plan_wait(plan, pctx, grid)

    sds = jax.ShapeDtypeStruct((L, R, C), F32)
    p_in, p_ospec, p_oshape, p_scr, p_alias = _plan_io(plan, 3 + L, 4)
    return pl.pallas_call(
        body, name=name, grid=grid, in_specs=[blk] * 3 + gblks + [_ANY] * len(p_in),
        out_specs=[blk] * 4 + p_ospec, out_shape=[sds] * 4 + p_oshape, scratch_shapes=p_scr,
        input_output_aliases=p_alias, compiler_params=_params(32, 2),
    )(*_hbm(w, m, v, *gbufs), *p_in)


def _add_pairs(g, theirs, ids, *, name):
    n, R, L = theirs.shape
    tr = math.gcd(R, 128)
    nb = R // tr

    def body(ids_ref, a_ref, b_ref, o_ref):
        o_ref[...] = (a_ref[...].astype(F32) + b_ref[...].astype(F32)).astype(BF16)

    blk = pl.BlockSpec((n, tr, L), lambda i, ids: (0, i, 0))
    return pl.pallas_call(
        body, name=name, out_shape=jax.ShapeDtypeStruct((n, R, L), BF16),
        grid_spec=pltpu.PrefetchScalarGridSpec(
            num_scalar_prefetch=1, grid=(nb,),
            in_specs=[pl.BlockSpec((n, tr, L), lambda i, ids: (0, ids[1] * nb + i, 0)), blk], out_specs=blk),
        compiler_params=_params(16, 1),
    )(ids, g, theirs)


def _sum_chips(pair, parts, ids, *, name):
    _, R, L = parts.shape
    tr = math.gcd(R, 128)

    def body(ids_ref, o_ref, r_ref, out_ref):
        out_ref[...] = ((o_ref[...].astype(F32) + r_ref[0].astype(F32)) + r_ref[1].astype(F32)) + r_ref[2].astype(F32)

    return pl.pallas_call(
        body, name=name, out_shape=jax.ShapeDtypeStruct((2, R, L), F32),
        grid_spec=pltpu.PrefetchScalarGridSpec(
            num_scalar_prefetch=1, grid=(R // tr,),
            in_specs=[pl.BlockSpec((None, tr, L), lambda i, ids: (ids[0], i, 0)),
                      pl.BlockSpec((3, tr, L), lambda i, ids: (0, i, 0))],
            out_specs=pl.BlockSpec((None, tr, L), lambda i, ids: (ids[1], i, 0))),
        compiler_params=_params(32, 1),
    )(ids, pair, parts)


def _mesh_ids():
    x, y, c = _mesh_pos()
    return jnp.stack([2 * x + y, c]).astype(jnp.int32)


def _place_shard(rows, ids, *, name):
    R, L = rows.shape
    tr = 128

    def body(ids_ref, in_ref, out_ref):
        out_ref[...] = in_ref[...].astype(BF16)

    return pl.pallas_call(
        body, name=name, out_shape=jax.ShapeDtypeStruct((4, R, L), BF16),
        grid_spec=pltpu.PrefetchScalarGridSpec(
            num_scalar_prefetch=1, grid=(R // tr,), in_specs=[pl.BlockSpec((tr, L), lambda i, ids: (i, 0))],
            out_specs=pl.BlockSpec((None, tr, L), lambda i, ids: (ids[0], i, 0))),
        compiler_params=_params(16, 1),
    )(ids, rows)


IN_E_SHARD, IN_E_LAYOUT = 392, 1664


def _in_e_runs():
    runs = []
    for lo, hi, dst in ((0, 512, 0), (512, 544, 1536), (544, 1568, 512)):
        while lo < hi:
            j = lo // IN_E_SHARD
            wd = min(hi, IN_E_SHARD * (j + 1)) - lo
            runs.append((j, lo - IN_E_SHARD * j, dst, wd))
            lo, dst = lo + wd, dst + wd
    return runs


def _in_e_to_layout(shards):
    tr = 256

    def body(s_ref, o_ref, t_ref):
        t_ref[...] = jnp.zeros_like(t_ref)
        for j in range(4):
            s = s_ref[j].astype(F32)
            for _, c, dst, wd in (r for r in _in_e_runs() if r[0] == j):
                t_ref[:, dst:dst + wd] = s[:, c:c + wd]
        o_ref[...] = t_ref[...].astype(BF16)

    return pl.pallas_call(
        body, name="in_e_to_layout", out_shape=jax.ShapeDtypeStruct((D, IN_E_LAYOUT), BF16), grid=(D // tr,),
        in_specs=[pl.BlockSpec((4, tr, IN_E_SHARD), lambda i: (0, i, 0))],
        out_specs=pl.BlockSpec((tr, IN_E_LAYOUT), lambda i: (i, 0)),
        scratch_shapes=[pltpu.VMEM((tr, IN_E_LAYOUT), F32)], compiler_params=_params(16, 1),
    )(shards)


def _in_e_from_layout(g):
    tr = 256

    def body(g_ref, o_ref, t_ref):
        g_ = g_ref[...]
        for j, c, src, wd in _in_e_runs():
            t_ref[j, :, c:c + wd] = g_[:, src:src + wd]
        o_ref[...] = t_ref[...].astype(BF16)

    return pl.pallas_call(
        body, name="in_e_from_layout", out_shape=jax.ShapeDtypeStruct((4, D, IN_E_SHARD), BF16), grid=(D // tr,),
        in_specs=[pl.BlockSpec((tr, IN_E_LAYOUT), lambda i: (i, 0))],
        out_specs=pl.BlockSpec((4, tr, IN_E_SHARD), lambda i: (0, i, 0)),
        scratch_shapes=[pltpu.VMEM((4, tr, IN_E_SHARD), F32)], compiler_params=_params(16, 1),
    )(g)


def _place_weights(pieces, buffer_rows, ids, *, plan=None):
    tr = 256
    steps, s = [], 0
    for arr, layer, buf, row0 in pieces:
        nblk = arr.shape[1] // tr
        steps.append((s, nblk))
        s += nblk
    total = s
    buf_start = [min(st for (st, _), p in zip(steps, pieces) if p[2] == k) for k in range(len(buffer_rows))]
    grid = (total,)
    n_in = len(pieces)

    def body(*refs):
        ins, outs, _, pctx = _split_refs(refs[1:], n_in, len(buffer_rows), 0, plan)
        _plan_start(plan, pctx, grid)
        i = pl.program_id(0)
        for (st, nblk), (_, _, buf, _), ref in zip(steps, pieces, ins):
            @pl.when((i >= st) & (i < st + nblk))
            def _(ref=ref, buf=buf):
                outs[buf][...] = ref[...].astype(BF16)
        _plan_wait(plan, pctx, grid)

    in_specs = [pl.BlockSpec((None, tr, D), lambda i, ids, layer=layer, st=st, nblk=nblk:
                             (layer, jnp.clip(i - st, 0, nblk - 1), 0))
                for (st, nblk), (_, layer, _, _) in zip(steps, pieces)]
    out_specs = [pl.BlockSpec((None, tr, D), lambda i, ids, st=st, nb=rows // tr: (ids[0], jnp.clip(i - st, 0, nb - 1), 0))
                 for st, rows in zip(buf_start, buffer_rows)]
    p_in, p_ospec, p_oshape, p_scr, p_alias = _plan_io(plan, 1 + n_in, len(buffer_rows))
    return pl.pallas_call(
        body, name="place_weights",
        out_shape=[jax.ShapeDtypeStruct((4, rows, D), BF16) for rows in buffer_rows] + p_oshape,
        grid_spec=pltpu.PrefetchScalarGridSpec(
            num_scalar_prefetch=1, grid=grid, in_specs=in_specs + [_ANY] * len(p_in), out_specs=out_specs + p_ospec,
            scratch_shapes=p_scr),
        input_output_aliases=p_alias, compiler_params=_params(16, 1),
    )(ids, *[p[0] for p in pieces], *p_in)


def _remote(src, dst, send_sem, recv_sem, to):
    return pltpu.make_async_remote_copy(src_ref=src, dst_ref=dst, send_sem=send_sem, recv_sem=recv_sem,
                                        device_id=to, device_id_type=MESH_IDS)


def _rows(ref, lead, start, size):
    return ref.at[tuple(pl.ds(0, n) for n in ref.shape[:lead]) + (pl.ds(start, size),)]


def _other_chips():
    x, y, _ = _mesh_pos()
    return [(1 - x, y), (x, 1 - y), (1 - x, 1 - y)]


def _plan_gather_ici(bufs):
    n = len(bufs)

    def copies(outs, send, recv):
        x, y, c = _mesh_pos()
        res = []
        for b in range(n):
            half = bufs[b].shape[1] // 2
            mine = _rows(outs[b].at[2 * x + y], 0, c * half, half)
            for j, (cx, cy) in enumerate(_other_chips()):
                res.append((_remote(mine, mine, send(3 * b + j), recv(3 * b + j), (cx, cy, c)),
                            _remote(mine, _rows(outs[b].at[2 * cx + cy], 0, c * half, half),
                                    send(3 * b + j), recv(3 * b + j), (x, y, c))))
        return res

    def start(ins, outs, send, recv, loc):
        for out_cp, _ in copies(outs, send, recv):
            out_cp.start()

    def wait(ins, outs, send, recv, loc):
        for out_cp, in_cp in copies(outs, send, recv):
            in_cp.wait_recv()
            out_cp.wait_send()

    outs = [jax.ShapeDtypeStruct(b.shape, b.dtype) for b in bufs]
    return _Plan(bufs, outs, 3 * n, 0, start, wait, aliases={b: b for b in range(n)})


def _plan_gather_forward(bufs):
    n = len(bufs)

    def copies(outs, send, recv):
        x, y, c = _mesh_pos()
        res = []
        for b in range(n):
            half = bufs[b].shape[1] // 2
            for j, (cx, cy) in enumerate(_other_chips()):
                slot = outs[b].at[2 * cx + cy]
                res.append((_remote(_rows(slot, 0, c * half, half), _rows(slot, 0, c * half, half),
                                    send(3 * b + j), recv(3 * b + j), (x, y, 1 - c)),
                            _remote(_rows(slot, 0, c * half, half), _rows(slot, 0, (1 - c) * half, half),
                                    send(3 * b + j), recv(3 * b + j), (x, y, c))))
        return res

    def start(ins, outs, send, recv, loc):
        for out_cp, _ in copies(outs, send, recv):
            out_cp.start()

    def wait(ins, outs, send, recv, loc):
        for out_cp, in_cp in copies(outs, send, recv):
            in_cp.wait_recv()
            out_cp.wait_send()

    outs = [jax.ShapeDtypeStruct(b.shape, b.dtype) for b in bufs]
    return _Plan(bufs, outs, 3 * n, 0, start, wait, aliases={b: b for b in range(n)})


def _plan_pair_swap(g):
    half = g.shape[1] // 2

    def copy(ins, outs, send, recv, loc):
        x, y, c = _mesh_pos()
        return _remote(_rows(ins[0], 1, (1 - c) * half, half), outs[0], send(0), recv(0), (x, y, 1 - c))

    return _Plan([g], [jax.ShapeDtypeStruct((4, half, g.shape[2]), g.dtype)], 1, 0,
                 lambda *a: copy(*a).start(), lambda *a: copy(*a).wait())


def _plan_pair_gather(buf):
    def copies(ins, outs, send, recv, loc):
        x, y, c = _mesh_pos()
        return (_remote(outs[0].at[c], outs[0].at[c], send(0), recv(0), (x, y, 1 - c)),
                _remote(outs[0].at[c], outs[0].at[1 - c], send(0), recv(0), (x, y, c)))

    def wait(*a):
        out_cp, in_cp = copies(*a)
        in_cp.wait_recv()
        out_cp.wait_send()

    return _Plan([buf], [jax.ShapeDtypeStruct(buf.shape, buf.dtype)], 1, 0, lambda *a: copies(*a)[0].start(), wait,
                 aliases={0: 0})


def _plan_chip_scatter(p):
    def copies(ins, outs, send, recv, loc):
        _, _, c = _mesh_pos()
        return [_remote(ins[0].at[2 * cx + cy], outs[0].at[j], send(j), recv(j), (cx, cy, c))
                for j, (cx, cy) in enumerate(_other_chips())]

    def start(*a):
        for cp in copies(*a):
            cp.start()

    def wait(*a):
        for cp in copies(*a):
            cp.wait()

    return _Plan([p], [jax.ShapeDtypeStruct((3,) + p.shape[1:], p.dtype)], 3, 0, start, wait)


def _plan_exchange_all(vec):
    def copies(ins, outs, send, recv, loc):
        x, y, c = _mesh_pos()
        return [_remote(ins[0], outs[0].at[r - 1], send(r - 1), recv(r - 1), (x ^ (r >> 2), y ^ ((r >> 1) & 1), c ^ (r & 1)))
                for r in range(1, 8)]

    def start(*a):
        for cp in copies(*a):
            cp.start()

    def wait(*a):
        for cp in copies(*a):
            cp.wait()

    return _Plan([vec], [jax.ShapeDtypeStruct((7,) + vec.shape, vec.dtype)], 7, 0, start, wait)


SMALL_LAYOUT = {
    "mla_gq": (0, 1, 256, (1, 256)), "mla_gkv": (1, 1, 256, (1, 256)), "sgu_ln_g": (2, 1, 512, (1, 512)),
    "sgu_ln_b": (3, 1, 512, (1, 512)), "sgu_w": (4, 64, 1024, (64, 1024)), "sgu_b": (68, 1, 512, (1, 512)),
    "hg_lb": (69, 2, 1024, (2, 1024)), "hg_gnorm": (71, 1, 1024, (1, 256)), "ln1_g": (72, 2, 1024, (2, 1024)),
    "ln1_b": (74, 2, 1024, (2, 1024)), "ln2_g": (76, 2, 1024, (2, 1024)), "ln2_b": (78, 2, 1024, (2, 1024)),
}


def _small_pack(dgq, dgkv, dslg, dslb, dsw, dsb, dlb, dgn, ln_parts, sq_err):
    flat_ln = [p for pair in ln_parts for p in pair]

    def body(*refs):
        gq_ref, gkv_ref, slg_ref, slb_ref, sw_ref, sb_ref, lb_ref, gn_ref = refs[:8]
        ln_refs, err_ref, out_ref, t_sc = refs[8:16], refs[16], refs[17], refs[18]
        s8 = lambda ref: jnp.sum(ref[...], axis=0, keepdims=True)
        out_ref[...] = jnp.zeros_like(out_ref)
        out_ref[0:1, 0:256] = s8(gq_ref)
        out_ref[1:2, 0:256] = s8(gkv_ref)
        out_ref[2:3, 0:512] = s8(slg_ref)
        out_ref[3:4, 0:512] = s8(slb_ref)
        out_ref[4:68, :] = sw_ref[...]
        t_sc[...] = sb_ref[...].T
        for g in range(SGU_G):
            out_ref[68:69, g * SGU_C:(g + 1) * SGU_C] = t_sc[g:g + 1, :]
        d_lb1 = s8(lb_ref)
        out_ref[69:70, :] = -d_lb1
        out_ref[70:71, :] = d_lb1
        out_ref[71:72, :] = s8(gn_ref)
        for k, ref in enumerate(ln_refs):
            out_ref[72 + k:73 + k, :] = s8(ref)
        out_ref[0:1, 1023:1024] = jnp.sum(s8(err_ref), axis=1, keepdims=True) * (0.5 / D)

    vm = pl.BlockSpec(memory_space=pltpu.VMEM)
    return pl.pallas_call(
        body, name="small_grad_pack", in_specs=[vm] * 17, out_specs=vm,
        out_shape=jax.ShapeDtypeStruct((SMALL_ROWS, 1024), F32), scratch_shapes=[pltpu.VMEM((SGU_C, SGU_C), F32)],
        compiler_params=_params(16),
    )(dgq, dgkv, dslg, dslb, dsw.reshape(64, 1024), dsb, dlb, dgn, *flat_ln, sq_err)


def _small_update(vec, others, ids, w, m, v):
    names = list(SMALL_LAYOUT)
    n = len(names)
    c1, c2 = 1.0 - B1 ** STEP, 1.0 - B2 ** STEP
    have_others = others is not None

    def body(*refs):
        ids_ref, v_ref = refs[0], refs[1]
        k = 2 + have_others
        w_refs, m_refs, v_refs = refs[k:k + n], refs[k + n:k + 2 * n], refs[k + 2 * n:k + 3 * n]
        outs = refs[k + 3 * n:]
        row0_ref, tot_sc = outs[0], outs[-1]
        total = v_ref[...]
        if have_others:
            me = 2 * ids_ref[0] + ids_ref[1]
            total = None
            for d in range(8):
                rel = d ^ me
                term = jnp.where(rel == 0, v_ref[...], refs[2][jnp.maximum(rel - 1, 0)])
                total = term if total is None else total + term
        tot_sc[...] = total
        row0_ref[...] = tot_sc[0:1, :]
        for i, name in enumerate(names):
            r0, nr, width, _ = SMALL_LAYOUT[name]
            if name == "hg_gnorm":
                g_ = tot_sc[r0:r0 + 1, 0:256]
                for chip in range(1, 4):
                    g_ = jnp.where(ids_ref[0] == chip, tot_sc[r0:r0 + 1, chip * 256:(chip + 1) * 256], g_)
            else:
                g_ = tot_sc[r0:r0 + nr, 0:width]
            m_ = B1 * m_refs[i][...] + (1.0 - B1) * g_
            v_ = B2 * v_refs[i][...] + (1.0 - B2) * (g_ * g_)
            go, do, mo, vo = outs[1 + 4 * i:5 + 4 * i]
            go[...] = g_
            do[...] = -LR * ((m_ / c1) / (jnp.sqrt(v_ / c2) + ADAM_EPS) + WD * w_refs[i][...])
            mo[...] = m_
            vo[...] = v_

    full = lambda shape: pl.BlockSpec(shape, lambda i, ids, nd=len(shape): (0,) * nd)
    kshapes = [SMALL_LAYOUT[name][3] for name in names]
    operands = [vec] + ([others] if have_others else []) + [d[name] for d in (w, m, v) for name in names]
    out_shapes = [jax.ShapeDtypeStruct((1, 1024), F32)] + [jax.ShapeDtypeStruct(s, F32) for s in kshapes for _ in range(4)]
    res = pl.pallas_call(
        body, name="small_update", out_shape=out_shapes,
        grid_spec=pltpu.PrefetchScalarGridSpec(
            num_scalar_prefetch=1, grid=(1,), in_specs=[full(o.shape) for o in operands],
            out_specs=[full(s.shape) for s in out_shapes],
            scratch_shapes=[pltpu.VMEM((SMALL_ROWS, 1024), F32)]),
        compiler_params=_params(32, 1),
    )(ids, *operands)
    return res[0], {name: tuple(res[1 + 4 * i:5 + 4 * i]) for i, name in enumerate(names)}


ROWS_L1, ROWS_L0, ROWS_ODD_W = 3328, 2048, 384
ODD_PARTS = (("w_out_e", (256, 1024)), ("w_in_e", (1024, 392)), ("w_qb", (256, 192)), ("w_kvb", (256, 256)))
ODD_W_PARTS = tuple(p for p in ODD_PARTS if p[0] != "w_in_e")


def _odd_rows(parts, dtype, layout, total, gnorm=None):
    rows = [parts[n].reshape(-1, 1024).astype(dtype) for n, _ in layout]
    used = sum(r.shape[0] for r in rows)
    if gnorm is not None:
        bits = lax.bitcast_convert_type(gnorm.reshape(-1), BF16).reshape(1, 512)
        rows.append(jnp.pad(bits, ((0, 15), (0, 512))))
        used += 16
    if total > used:
        rows.append(jnp.zeros((total - used, 1024), dtype))
    return jnp.concatenate(rows, axis=0)


def _odd_unrows(buf, layout, with_gnorm=False):
    out, off = {}, 0
    for n, shape in layout:
        nr = math.prod(shape) // 1024
        out[n] = buf[off:off + nr].reshape(shape)
        off += nr
    if with_gnorm:
        out["hg_gnorm"] = lax.bitcast_convert_type(buf[off, :512].reshape(256, 2), F32).reshape(1, 256)
    return out


def _rope_tables(positions):
    half = ROPE // 2
    inv_freq = ROPE_BASE ** (-jnp.arange(half, dtype=F32) / half)
    per_row = 128 // half
    ang = jnp.repeat(positions.astype(F32).reshape(-1, per_row), half, axis=1) * jnp.tile(inv_freq, per_row)
    cos, sin = jnp.cos(ang).reshape(-1, half), jnp.sin(ang).reshape(-1, half)
    T = cos.shape[0]
    one, z16, z32 = jnp.ones((T, NOPE), F32), jnp.zeros((T, half), F32), jnp.zeros((T, 32), F32)
    z64 = jnp.zeros((T, NOPE), F32)
    c = jnp.concatenate([one, cos, cos, z32], axis=1)
    s1 = jnp.concatenate([z64, -sin, z16, z32], axis=1)
    s2 = jnp.concatenate([z64, z16, sin, z32], axis=1)
    return c, s1, s2


def _local_step(x, positions, tgt, odd, bufs, P, exchange):
    T = x.shape[0]
    row = lambda a: a.reshape(1, -1)
    rc, rs1, rs2 = _rope_tables(positions)
    blk = lambda f: pl.BlockSpec((None, D, D), f)

    w_in = _in_e_to_layout(odd["w_in_e"])
    wq = jnp.pad(odd["w_qb"].reshape(256, HEADS, NOPE + ROPE), ((0, 0), (0, 0), (0, 32))).reshape(256, HEADS * 128)
    kvb = odd["w_kvb"].reshape(256, HEADS, NOPE + VDIM)
    wk = jnp.pad(kvb[:, :, :NOPE], ((0, 0), (0, 0), (0, 64))).reshape(256, HEADS * 128)
    wv = kvb[:, :, NOPE:].reshape(256, HEADS * VDIM)
    w_out_e = odd["w_out_e"]
    sgu_w = P["sgu_w"][0]
    sgu_bt = P["sgu_b"][0].T
    gq, gkv = P["mla_gq"], P["mla_gkv"]
    gnorm = P["hg_gnorm"]

    z0 = _matmul(x, w_in, name="in_proj_e", M=T, N=1664, K=D, tn=1664)[0]
    q, k, v = _mla_prep(z0, gq, gkv, wq, wk, wv, rc, rs1, rs2)
    if exchange:
        ids = _mesh_ids()
        placed = list(bufs)
        a_out, lse, wga, wgb = _flash_fwd(q, k, v, plan=_plan_gather_ici(placed[:2]))
    else:
        a_out, lse = _flash_fwd(q, k, v)
        wga, wgb, wgc = bufs
    mix0 = _sgu_fwd(z0, a_out, P["sgu_ln_g"], P["sgu_ln_b"], sgu_w, sgu_bt)
    res = _proj_ln(mix0, w_out_e, x, row(P["ln1_g"][0]), row(P["ln1_b"][0]), name="out_proj_ln_e",
                   plan=_plan_gather_forward([wga, wgb]) if exchange else None)
    r1, h1b = res[:2]
    if exchange:
        wga, wgb = res[2:]
    ln = lambda name, l: (row(P[name + "_g"][l]), row(P[name + "_b"][l]))
    res = _ffn_ln(h1b, wga, r1, *ln("ln2", 0), name="ffn_ln_0", prev_ln=ln("ln1", 0),
                  plan=_plan_gather_ici(placed[2:]) if exchange else None)
    ra0, r2, h2b = res[:3]
    z4 = _matmul(h2b, wgb, name="in_proj_o", M=T, N=4 * D, K=D, tn=2 * D, n_slots=True,
                 b_spec=pl.BlockSpec((2, D, D), lambda i, j, k: (j, 0, 0)),
                 out_shape=jax.ShapeDtypeStruct((4, T, D), F32),
                 o_spec=pl.BlockSpec((2, min(MM_ROWS, T), D), lambda i, j, k: (j, i, 0)))[0]
    y1, o_raw, states = _hgrn_fwd(z4, P["hg_lb"], gnorm)
    res2 = _proj_ln(y1, wgb, r2, *ln("ln1", 1), name="out_proj_ln_o", prev_ln=ln("ln2", 0), w_rowblk=4,
                    plan=_plan_gather_forward([res[3]]) if exchange else None)
    r3, h3b = res2[:2]
    if exchange:
        wgc = res2[2]
    ra1, r4, _ = _ffn_ln(h3b, wgc, r3, *ln("ln2", 1), name="ffn_ln_1", prev_ln=ln("ln1", 1))

    ln1_g, ln1_b, ln2_g, ln2_b = [None, None], [None, None], [None, None], [None, None]
    sq_err_parts = []

    def ffn_bwd(l, dh, r_out, ra, h_mid_b, g2, wg, rows, plan=None, loss_head=()):
        dr, dr_b, dg, db, *sq_err = _ln_bwd(dh, r_out, row(g2), name=f"ln2_bwd_{l}", loss_head=loss_head)
        sq_err_parts.extend(sq_err)
        ln2_g[l], ln2_b[l] = dg, db
        da, *extra = _matmul(dr_b, wg, tb=True, mul=ra, out_dtype=BF16, name=f"ffn_da_{l}", M=T, N=4 * D, K=D, tn=2 * D,
                             b_spec=pl.BlockSpec((2, D, D), lambda i, j, k: (j, 1, 0)), n_slots=True, plan=plan)
        gbuf = _matmul(ra, dr_b, ta=True, a_sq=True, name=f"ffn_dw2_{l}", M=4 * D, N=D, K=T, tm=1024, tk=DW_TOKENS // 2,
                       out_shape=jax.ShapeDtypeStruct((4, rows, D), BF16), o_spec=blk(lambda i, j, k: (i, 1, 0)))[0]
        gbuf = _matmul(h_mid_b, da, ta=True, name=f"ffn_dw1_{l}", M=D, N=4 * D, K=T, tm=1024, tk=DW_TOKENS, into=gbuf,
                       out_shape=jax.ShapeDtypeStruct((4, rows, D), BF16), o_spec=blk(lambda i, j, k: (j, 0, 0)))[0]
        dh_mid = _matmul(da, wg, tb=True, add=dr, add_scale=ALPHA, name=f"ffn_dh_{l}", M=T, N=D, K=4 * D, tk=2 * D,
                         b_spec=pl.BlockSpec((2, D, D), lambda i, j, k: (k, 0, 0)))[0]
        return dh_mid, gbuf, extra

    dh3, g1, _ = ffn_bwd(1, None, r4, ra1, h3b, P["ln2_g"][1], wgc, ROWS_L1, loss_head=(row(P["ln2_b"][1]), tgt))
    loss_parts = sq_err_parts[0]
    dr3, dr3_b, dg, db = _ln_bwd(dh3, r3, row(P["ln1_g"][1]), name="ln1_bwd_1")
    ln1_g[1], ln1_b[1] = dg, db
    g1_sds = jax.ShapeDtypeStruct((4, ROWS_L1, D), BF16)
    g1 = _matmul(y1, dr3_b, ta=True, name="dw_out_o", M=D, N=D, K=T, tm=256, tk=DW_TOKENS, into=g1, out_shape=g1_sds,
                 o_spec=pl.BlockSpec((None, 256, D), lambda i, j, k: (i, 12, 0)))[0]
    dmix1 = _matmul(dr3_b, wgb, tb=True, name="dmix_o", M=T, N=D, K=D, b_spec=_rows4_spec(4, 3), b_merge=(D, D))[0]
    dz4, dlb, dgn = _hgrn_bwd(z4, o_raw, dmix1, states, P["hg_lb"], gnorm)
    g1 = _matmul(h2b, dz4, ta=True, name="dw_in_o", M=D, N=4 * D, K=T, tm=1024, tk=DW_TOKENS, into=g1, out_shape=g1_sds,
                 b_spec=pl.BlockSpec((None, min(DW_TOKENS, T), D), lambda i, j, k: (j, k, 0)),
                 o_spec=blk(lambda i, j, k: (j, 2, 0)))[0]
    dh2 = _matmul(dz4, wgb, tb=True, add=dr3, add_scale=ALPHA, name="dh_in_o", M=T, N=D, K=4 * D, tk=2 * D,
                  a_spec=pl.BlockSpec((2, min(MM_ROWS, T), D), lambda i, j, k: (k, i, 0)),
                  b_spec=pl.BlockSpec((2, D, D), lambda i, j, k: (k, 0, 0)))[0]

    dh1, g0, swapped1 = ffn_bwd(0, dh2, r2, ra0, h1b, P["ln2_g"][0], wga, ROWS_L0,
                                plan=_plan_pair_swap(g1) if exchange else None)
    dr1, dr1_b, dg, db = _ln_bwd(dh1, r1, row(P["ln1_g"][0]), name="ln1_bwd_0")
    ln1_g[0], ln1_b[0] = dg, db
    godd = {"w_out_e": _matmul(mix0, dr1_b, ta=True, name="dw_out_e", M=D, N=D, K=T, tm=1024, tk=DW_TOKENS)[0]}
    dmix0, *swapped0 = _matmul(dr1_b, w_out_e, tb=True, name="dmix_e", M=T, N=D, K=D,
                               plan=_plan_pair_swap(g0) if exchange else None)
    delta, do_b = _attn_delta(dmix0, a_out)
    if exchange:
        pair1 = _add_pairs(g1, swapped1[0], ids, name="grad_pair_add_1")
        pair0 = _add_pairs(g0, swapped0[0], ids, name="grad_pair_add_0")
        dq4, dk, dv, parts0, parts1 = _flash_bwd(
            q, k, v, do_b, lse, delta, plan=_join_plans([_plan_chip_scatter(pair0), _plan_chip_scatter(pair1)]))
        half0 = _sum_chips(pair0, parts0, ids, name="grad_chip_sum_0")
        half1 = _sum_chips(pair1, parts1, ids, name="grad_chip_sum_1")
        dc, dkr, dwq, dwk, dwv, dgq, dgkv, g0, g1 = _mla_bwd(
            z0, dq4, dk, dv, gq, gkv, wq, wk, wv, rc, rs1, rs2,
            plan=_join_plans([_plan_pair_gather(half0), _plan_pair_gather(half1)]))
        g0, g1 = g0.reshape(ROWS_L0, D), g1.reshape(ROWS_L1, D)
    else:
        dq4, dk, dv = _flash_bwd(q, k, v, do_b, lse, delta)
        dc, dkr, dwq, dwk, dwv, dgq, dgkv = _mla_bwd(z0, dq4, dk, dv, gq, gkv, wq, wk, wv, rc, rs1, rs2)
    godd["w_qb"] = dwq.reshape(256, HEADS, 128)[:, :, :NOPE + ROPE].reshape(256, HEADS * (NOPE + ROPE))
    godd["w_kvb"] = jnp.concatenate([dwk.reshape(256, HEADS, 128)[:, :, :NOPE], dwv.reshape(256, HEADS, VDIM)],
                                    axis=2).reshape(256, HEADS * (NOPE + VDIM))
    swap_b = None
    if exchange:
        by_chip = [_odd_rows({"w_out_e": jnp.split(godd["w_out_e"], 4, axis=0)[j],
                              **{n: jnp.split(godd[n], 4, axis=1)[j] for n in ("w_qb", "w_kvb")}}, BF16,
                             ODD_W_PARTS, ROWS_ODD_W)
                   for j in range(4)]
        odd_b = jnp.stack(by_chip)
        swap_b = _plan_pair_swap(odd_b)
    dz0, dsw, dsb, dslg, dslb, *theirs_b = _sgu_bwd(z0, dmix0, dc, dkr, P["sgu_ln_g"], P["sgu_ln_b"], sgu_w, sgu_bt,
                                                    plan=swap_b)
    small_vec = _small_pack(dgq, dgkv, dslg, dslb, dsw, dsb, dlb, dgn, [ln1_g, ln1_b, ln2_g, ln2_b], loss_parts)
    plan_in = None
    if exchange:
        pair_b = _add_pairs(odd_b, theirs_b[0], ids, name="odd_pair_add_1")
        plan_in = _join_plans([_plan_exchange_all(small_vec), _plan_chip_scatter(pair_b)])
    dw_in, *carried = _matmul(x, dz0, ta=True, name="dw_in_e", M=D, N=1664, K=T, tm=1024, tn=1664, tk=DW_TOKENS // 4,
                              plan=plan_in)
    odd_a = godd["w_in_e"] = _in_e_from_layout(dw_in)
    plan_x = None
    if exchange:
        small_others, parts_b = carried
        theirs_a = _run_plan(_plan_pair_swap(odd_a), name="odd_pair_swap")[0]
        pair_a = _add_pairs(odd_a, theirs_a, ids, name="odd_pair_add_0")
        plan_x = _plan_chip_scatter(pair_a)
    grad_x, *parts_a = _matmul(dz0, w_in, tb=True, add=dr1, add_scale=ALPHA, name="dx", M=T, N=D, K=1664, tk=1664,
                               plan=plan_x)
    if exchange:
        godd = ([pair_a, pair_b], [parts_a[0], parts_b])
        return grad_x, g0, g1, godd, small_vec, small_others
    return grad_x, g0, g1, godd, small_vec, None


WEIGHTS = ['w_in_e', 'mla_gq', 'mla_gkv', 'w_qb', 'w_kvb', 'sgu_ln_g', 'sgu_ln_b', 'sgu_w', 'sgu_b', 'w_out_e',
           'w_in_o', 'hg_lb', 'hg_gnorm', 'w_out_o', 'ln1_g', 'ln1_b', 'w_ff1', 'w_ff2', 'ln2_g', 'ln2_b']


def kernel(x, positions, w_in_e, mla_gq, mla_gkv, w_qb, w_kvb, sgu_ln_g, sgu_ln_b, sgu_w, sgu_b, w_out_e, w_in_o, hg_lb, hg_gnorm, w_out_o, ln1_g, ln1_b, w_ff1, w_ff2, ln2_g, ln2_b, loss_target, m_w_in_e, m_mla_gq, m_mla_gkv, m_w_qb, m_w_kvb, m_sgu_ln_g, m_sgu_ln_b, m_sgu_w, m_sgu_b, m_w_out_e, m_w_in_o, m_hg_lb, m_hg_gnorm, m_w_out_o, m_ln1_g, m_ln1_b, m_w_ff1, m_w_ff2, m_ln2_g, m_ln2_b, v_w_in_e, v_mla_gq, v_mla_gkv, v_w_qb, v_w_kvb, v_sgu_ln_g, v_sgu_ln_b, v_sgu_w, v_sgu_b, v_w_out_e, v_w_in_o, v_hg_lb, v_hg_gnorm, v_w_out_o, v_ln1_g, v_ln1_b, v_w_ff1, v_w_ff2, v_ln2_g, v_ln2_b):
    args = dict(locals())
    w = {n: args[n] for n in WEIGHTS}
    m = {n: args["m_" + n] for n in WEIGHTS}
    v = {n: args["v_" + n] for n in WEIGHTS}
    cx, cy, cc = _mesh_pos()
    chip = 2 * cx + cy

    odd_shard = _odd_rows({"w_out_e": w_out_e[0], "w_qb": w_qb[0], "w_kvb": w_kvb[0]}, BF16, ODD_W_PARTS, ROWS_ODD_W,
                          gnorm=hg_gnorm)
    ids = _mesh_ids()
    placed = [_place_shard(w_in_e[0], ids, name="place_shard_in_e"), _place_shard(odd_shard, ids, name="place_shard_odd")]
    pieces = [(w_ff1, 0, 0, 0), (w_ff2, 0, 0, 1024), (w_in_o, 0, 1, 0), (w_out_o, 0, 1, 1024),
              (w_ff1, 1, 2, 0), (w_ff2, 1, 2, 1024)]
    *big_bufs, odd_a, odd_b = _place_weights(pieces, (2048, 1280, 2048), ids, plan=_plan_gather_ici(placed))
    gathered = _run_plan(_plan_gather_forward([odd_a, odd_b]), name="odd_gather_forward")
    per_chip = [_odd_unrows(gathered[1][j], ODD_W_PARTS, with_gnorm=True) for j in range(4)]
    odd = {"w_out_e": jnp.concatenate([p["w_out_e"] for p in per_chip], axis=0),
           "w_in_e": gathered[0]}
    for n in ("w_qb", "w_kvb"):
        odd[n] = jnp.concatenate([p[n] for p in per_chip], axis=1)
    small = {n: w[n] for n in SMALL_LAYOUT if n != "hg_gnorm"}
    small["hg_gnorm"] = jnp.concatenate([p["hg_gnorm"] for p in per_chip], axis=1)
    grad_x, g_l0, g_l1, godd, small_vec, small_others = _local_step(
        x[0], positions[0], loss_target[0], odd, big_bufs, small, True)

    sums = [_sum_chips(pair, parts, ids, name=f"odd_chip_sum_{k}") for k, (pair, parts) in enumerate(zip(*godd))]
    g_in_e, g_rest = _run_plan(_join_plans([_plan_pair_gather(s) for s in sums]), name="odd_pair_gather")
    g_odd = _odd_unrows(g_rest.reshape(ROWS_ODD_W, 1024), ODD_W_PARTS)
    g_odd["w_in_e"] = g_in_e.reshape(D, 392)

    to_kernel = lambda d: {n: d[n].reshape(SMALL_LAYOUT[n][3]) for n in SMALL_LAYOUT}
    first_row, small_out = _small_update(small_vec, small_others, ids, to_kernel(w), to_kernel(m), to_kernel(v))
    loss = first_row[0, 1023]
    grads, delta, new_m, new_v = {}, {}, {}, {}
    for n, res in small_out.items():
        grads[n], delta[n], new_m[n], new_v[n] = (r.reshape(w[n].shape) for r in res)

    for n, bufs_, row0 in (("w_ff1", [g_l0, g_l1], 0), ("w_ff2", [g_l0, g_l1], 1024), ("w_in_o", [g_l1], 2048),
                           ("w_out_o", [g_l1], 3072)):
        grads[n], delta[n], new_m[n], new_v[n] = _adamw_rows(w[n], m[n], v[n], bufs_, row0, name=f"adamw_{n}")
    for n, _ in ODD_PARTS:
        if n == "w_in_e":
            res = _adamw(w[n][0].T, g_odd[n].T, m[n][0].T, v[n][0].T, name=f"adamw_{n}", with_g=True)
            grads[n], delta[n], new_m[n], new_v[n] = (r.T[None] for r in res)
            continue
        grads[n] = g_odd[n][None]
        d_, m_, v_ = _adamw(w[n][0], g_odd[n], m[n][0], v[n][0], name=f"adamw_{n}")
        delta[n], new_m[n], new_v[n] = d_[None], m_[None], v_[None]

    return (loss, grad_x[None], *[grads[n] for n in WEIGHTS], *[delta[n] for n in WEIGHTS],
            *[new_m[n] for n in WEIGHTS], *[new_v[n] for n in WEIGHTS])
```

```python
import math

import jax
import jax.numpy as jnp
from jax import lax
from jax.experimental import pallas as pl
from jax.experimental.pallas import tpu as pltpu

F32 = jnp.float32
BF16 = jnp.bfloat16
MESH_IDS = pl.DeviceIdType.MESH

D = 1024
DEPTH = 2
HEADS = 8
NOPE, ROPE, VDIM = 64, 32, 64
QK_SCALE = (NOPE + ROPE) ** -0.5
ROPE_BASE = 10000.0
SGU_G, SGU_C = 4, 128
HG_CHUNK = 64
HG_HEADS_PER_STEP = 8
ALPHA = (2 * DEPTH) ** 0.25
EPS = 1e-5
LR, B1, B2, ADAM_EPS, WD, STEP = 0.001, 0.9, 0.999, 1e-08, 0.01, 10
GELU_C = math.sqrt(2.0 / math.pi)
GELU_A = 0.044715
MB = 1024 * 1024
ROW_BLOCK = 512
SMALL_ROWS = 80

NT_DIMS = (((1,), (1,)), ((), ()))
TN_DIMS = (((0,), (0,)), ((), ()))


def _params(vmem_mb, n_axes=0):
    kw = dict(vmem_limit_bytes=vmem_mb * MB)
    if n_axes:
        kw["dimension_semantics"] = ("arbitrary",) * n_axes
    return pltpu.CompilerParams(**kw)


_ANY = pl.BlockSpec(memory_space=pltpu.HBM)


def _mesh_pos():
    return lax.axis_index("x"), lax.axis_index("y"), lax.axis_index("c")


def _hbm(*arrays):
    return tuple(pltpu.with_memory_space_constraint(a, pltpu.HBM) if a.size >= 2 ** 18 else a for a in arrays)


class _Plan:
    def __init__(self, ins, outs, n_remote, n_local, start, wait, aliases=None):
        self.ins, self.outs, self.n_remote, self.n_local = list(ins), list(outs), n_remote, n_local
        self.start, self.wait, self.aliases = start, wait, dict(aliases or {})


def _join_plans(plans):
    ins, outs, aliases, parts = [], [], {}, []
    nr = nl = 0
    for p in plans:
        parts.append((p, len(ins), len(outs), nr, nl))
        aliases.update({len(ins) + i: len(outs) + o for i, o in p.aliases.items()})
        ins += p.ins
        outs += p.outs
        nr += p.n_remote
        nl += p.n_local

    def run(which):
        def go(in_refs, out_refs, send, recv, loc):
            for p, i0, o0, r0, l0 in parts:
                getattr(p, which)(in_refs[i0:i0 + len(p.ins)], out_refs[o0:o0 + len(p.outs)],
                                  lambda i, r0=r0: send(r0 + i), lambda i, r0=r0: recv(r0 + i),
                                  lambda i, l0=l0: loc(l0 + i))
        return go

    return _Plan(ins, outs, nr, nl, run("start"), run("wait"), aliases)


def _plan_io(plan, n_in, n_out):
    if plan is None:
        return [], [], [], [], {}
    sems = [pltpu.SemaphoreType.DMA((max(plan.n_remote, 1),)), pltpu.SemaphoreType.DMA((max(plan.n_remote, 1),)),
            pltpu.SemaphoreType.DMA((max(plan.n_local, 1),))]
    aliases = {n_in + i: n_out + o for i, o in plan.aliases.items()}
    return plan.ins, [_ANY] * len(plan.outs), plan.outs, sems, aliases


def _split_refs(refs, n_in, n_out, n_scr, plan):
    p_in, p_out = (len(plan.ins), len(plan.outs)) if plan is not None else (0, 0)
    refs = list(refs)
    ins, refs = refs[:n_in], refs[n_in:]
    pins, refs = refs[:p_in], refs[p_in:]
    outs, refs = refs[:n_out], refs[n_out:]
    pouts, refs = refs[:p_out], refs[p_out:]
    scr, psem = refs[:n_scr], refs[n_scr:]
    psem = tuple((lambda i, s=s: s.at[i]) for s in psem)
    return ins, outs, scr, (pins, pouts, psem)


def _grid_edge(grid, last):
    cond = None
    for ax, n in enumerate(grid):
        c = pl.program_id(ax) == (n - 1 if last else 0)
        cond = c if cond is None else cond & c
    return cond


def _plan_start(plan, pctx, grid):
    if plan is not None:
        pins, pouts, psem = pctx
        pl.when(_grid_edge(grid, False))(lambda: plan.start(pins, pouts, *psem))


def _plan_wait(plan, pctx, grid):
    if plan is not None:
        pins, pouts, psem = pctx
        pl.when(_grid_edge(grid, True))(lambda: plan.wait(pins, pouts, *psem))


def _run_plan(plan, *, name):
    def body(*refs):
        _, _, _, (pins, pouts, psem) = _split_refs(refs, 0, 0, 0, plan)
        plan.start(pins, pouts, *psem)
        plan.wait(pins, pouts, *psem)

    p_in, p_ospec, p_oshape, p_scr, p_alias = _plan_io(plan, 0, 0)
    return pl.pallas_call(body, name=name, in_specs=[_ANY] * len(p_in), out_specs=p_ospec, out_shape=p_oshape,
                          scratch_shapes=p_scr, input_output_aliases=p_alias)(*p_in)


def _fold8(x):
    return x.reshape(x.shape[0] // 8, 8, x.shape[1]).sum(axis=0)


def _ln_stats(r):
    mu = jnp.mean(r, -1, keepdims=True)
    xc = r - mu
    rstd = lax.rsqrt(jnp.mean(xc * xc, -1, keepdims=True) + EPS)
    return xc * rstd, rstd


def _sigmoid(x):
    return jax.nn.sigmoid(x)


def _gelu(x):
    return 0.5 * x * (1.0 + jnp.tanh(GELU_C * (x + GELU_A * x * x * x)))


def _gelu_grad(x):
    t = jnp.tanh(GELU_C * (x + GELU_A * x * x * x))
    return 0.5 * (1.0 + t) + 0.5 * x * (1.0 - t * t) * GELU_C * (1.0 + 3.0 * GELU_A * x * x)


MM_ROWS = 1024
DW_TOKENS = 4096


def _matmul(a, b, *, name, M, N, K, ta=False, tb=False, out_dtype=F32, tm=MM_ROWS, tn=1024, tk=1024,
            a_spec=None, b_spec=None, b_merge=None, out_shape=None, o_spec=None, into=None,
            a_sq=False, mul=None, add=None, add_scale=1.0, n_slots=False, plan=None):
    assert not n_slots or (K // min(tk, K) == 1 and add is None)
    tm, tn, tk = min(tm, M), min(tn, N), min(tk, K)
    assert M % tm == 0 and N % tn == 0 and K % tk == 0
    grid = (M // tm, N // tn, K // tk)
    nk = grid[2]
    if a_spec is None:
        a_spec = pl.BlockSpec((tk, tm), lambda i, j, k: (k, i)) if ta else pl.BlockSpec((tm, tk), lambda i, j, k: (i, k))
    if b_spec is None:
        b_spec = pl.BlockSpec((tn, tk), lambda i, j, k: (j, k)) if tb else pl.BlockSpec((tk, tn), lambda i, j, k: (k, j))
    if o_spec is None:
        o_spec = pl.BlockSpec((tm, tn), lambda i, j, k: (i, j))
        out_shape = jax.ShapeDtypeStruct((M, N), out_dtype)
    e_spec = pl.BlockSpec((tm, tn), lambda i, j, k: (i, j))
    dims = (((0 if ta else 1,), (1 if tb else 0,)), ((), ()))
    extra = [e for e in (mul, add, into) if e is not None]
    n_in = 2 + len(extra)

    def body(*refs):
        ins, outs, scr, pctx = _split_refs(refs, n_in, 1, 1 if nk > 1 else 0, plan)
        a_ref, b_ref = ins[0], ins[1]
        rest = list(ins[2:])
        mul_ref = rest.pop(0) if mul is not None else None
        add_ref = rest.pop(0) if add is not None else None
        o_ref = outs[0]
        _plan_start(plan, pctx, grid)
        av = a_ref[...].astype(BF16)
        if a_sq:
            av = av * av
        bv = b_ref[...]
        if b_merge is not None:
            bv = bv.reshape(b_merge)
        if n_slots:
            for s in range(bv.shape[0]):
                r = lax.dot_general(av, bv[s], dims, preferred_element_type=F32)
                w = r.shape[1]
                if mul_ref is not None:
                    r = r * (2.0 * mul_ref[:, s * w:(s + 1) * w].astype(F32))
                if o_ref.ndim == 3:
                    o_ref[s] = r.astype(o_ref.dtype)
                else:
                    o_ref[:, s * w:(s + 1) * w] = r.astype(o_ref.dtype)
            _plan_wait(plan, pctx, grid)
            return
        if bv.ndim == 3:
            w = av.shape[-1] // (1 if av.ndim == 3 else bv.shape[0])
            a_parts = [av[s] if av.ndim == 3 else av[:, s * w:(s + 1) * w] for s in range(bv.shape[0])]
            p = sum(lax.dot_general(a_parts[s], bv[s], dims, preferred_element_type=F32) for s in range(bv.shape[0]))
        else:
            p = lax.dot_general(av, bv, dims, preferred_element_type=F32)

        def finish(r):
            if mul_ref is not None:
                r = r * (2.0 * mul_ref[...].astype(F32))
            if add_ref is not None:
                r = r + add_scale * add_ref[...]
            o_ref[...] = r.astype(o_ref.dtype)

        if nk == 1:
            finish(p)
        else:
            acc_ref = scr[0]
            k = pl.program_id(2)

            @pl.when(k == 0)
            def _():
                acc_ref[...] = p

            @pl.when(k > 0)
            def _():
                acc_ref[...] += p

            @pl.when(k == nk - 1)
            def _():
                finish(acc_ref[...])

        _plan_wait(plan, pctx, grid)

    p_in, p_ospec, p_oshape, p_scr, p_alias = _plan_io(plan, n_in, 1)
    aliases = dict(p_alias)
    if into is not None:
        aliases[n_in - 1] = 0
    return pl.pallas_call(
        body, name=name, grid=grid,
        in_specs=[a_spec, b_spec] + [e_spec] * (len(extra) - (into is not None)) + [_ANY] * (into is not None)
        + [_ANY] * len(p_in),
        out_specs=[o_spec] + p_ospec, out_shape=[out_shape] + p_oshape,
        scratch_shapes=([pltpu.VMEM((tm, tn), F32)] if nk > 1 else []) + p_scr,
        input_output_aliases=aliases, compiler_params=_params(48, 3),
    )(*_hbm(a, b, *extra), *p_in)


def _rows4_spec(rowblk, n_axes):
    return pl.BlockSpec((4, 256, D), lambda *_: (0, rowblk, 0))


def _residual(h_ref, prev_refs):
    if not prev_refs:
        return h_ref[...]
    xhat, _ = _ln_stats(h_ref[...])
    return xhat * prev_refs[0][...] + prev_refs[1][...]


def _proj_ln(a_b, w, h_prev, g, b, *, name, prev_ln=(), w_rowblk=None, plan=None):
    T = a_b.shape[0]
    tm = min(MM_ROWS, T)
    grid = (T // tm,)
    row = pl.BlockSpec((tm, D), lambda i: (i, 0))
    vec = pl.BlockSpec((1, D), lambda i: (0, 0))
    w_spec = pl.BlockSpec((D, D), lambda i: (0, 0)) if w_rowblk is None else _rows4_spec(w_rowblk, 1)
    n_in = 5 + len(prev_ln)

    def body(*refs):
        ins, (r_ref, hb_ref), _, pctx = _split_refs(refs, n_in, 2, 0, plan)
        a_ref, w_ref, h_ref, g_ref, b_ref = ins[:5]
        _plan_start(plan, pctx, grid)
        mix = jnp.dot(a_ref[...], w_ref[...].reshape(D, D), preferred_element_type=F32)
        r = ALPHA * _residual(h_ref, ins[5:]) + mix
        xhat, _ = _ln_stats(r)
        r_ref[...] = r
        hb_ref[...] = (xhat * g_ref[...] + b_ref[...]).astype(BF16)
        _plan_wait(plan, pctx, grid)

    p_in, p_ospec, p_oshape, p_scr, p_alias = _plan_io(plan, n_in, 2)
    return pl.pallas_call(
        body, name=name, grid=grid,
        in_specs=[row, w_spec, row, vec, vec] + [vec] * len(prev_ln) + [_ANY] * len(p_in),
        out_specs=[row, row] + p_ospec,
        out_shape=[jax.ShapeDtypeStruct((T, D), F32), jax.ShapeDtypeStruct((T, D), BF16)] + p_oshape,
        scratch_shapes=p_scr, input_output_aliases=p_alias, compiler_params=_params(40, 1),
    )(*_hbm(a_b, w, h_prev, g, b, *prev_ln), *p_in)


def _ffn_ln(h_b, wbuf, h, g, b, *, name, prev_ln=(), plan=None):
    T = h_b.shape[0]
    slots = 2
    tm, tf = min(ROW_BLOCK, T), slots * 1024
    nf = 4 // slots
    F = nf * tf
    grid = (T // tm, nf)
    row = pl.BlockSpec((tm, D), lambda i, j: (i, 0))
    vec = pl.BlockSpec((1, D), lambda i, j: (0, 0))
    n_in = 6 + len(prev_ln)

    def body(*refs):
        ins, (ra_ref, r_ref, hbo_ref), (acc_ref,), pctx = _split_refs(refs, n_in, 3, 1, plan)
        hb_ref, w1_ref, w2_ref, h_ref, g_ref, b_ref = ins[:6]
        _plan_start(plan, pctx, grid)
        j = pl.program_id(1)
        hb = hb_ref[...]
        p = None
        for s in range(slots):
            ra = jnp.maximum(jnp.dot(hb, w1_ref[s], preferred_element_type=F32), 0.0)
            ra_ref[:, s * 1024:(s + 1) * 1024] = ra.astype(BF16)
            ps = jnp.dot((ra * ra).astype(BF16), w2_ref[s], preferred_element_type=F32)
            p = ps if p is None else p + ps

        @pl.when(j == 0)
        def _():
            acc_ref[...] = p

        @pl.when(j > 0)
        def _():
            acc_ref[...] += p

        @pl.when(j == nf - 1)
        def _():
            r = ALPHA * _residual(h_ref, ins[6:]) + acc_ref[...]
            xhat, _ = _ln_stats(r)
            r_ref[...] = r
            hbo_ref[...] = (xhat * g_ref[...] + b_ref[...]).astype(BF16)

        _plan_wait(plan, pctx, grid)

    p_in, p_ospec, p_oshape, p_scr, p_alias = _plan_io(plan, n_in, 3)
    return pl.pallas_call(
        body, name=name, grid=grid,
        in_specs=[row, pl.BlockSpec((slots, D, D), lambda i, j: (j, 0, 0)),
                  pl.BlockSpec((slots, D, D), lambda i, j: (j, 1, 0)), row, vec, vec] + [vec] * len(prev_ln)
        + [_ANY] * len(p_in),
        out_specs=[pl.BlockSpec((tm, tf), lambda i, j: (i, j)), row, row] + p_ospec,
        out_shape=[jax.ShapeDtypeStruct((T, F), BF16), jax.ShapeDtypeStruct((T, D), F32),
                   jax.ShapeDtypeStruct((T, D), BF16)] + p_oshape,
        scratch_shapes=[pltpu.VMEM((tm, D), F32)] + p_scr,
        input_output_aliases=p_alias, compiler_params=_params(56, 2),
    )(*_hbm(h_b, wbuf, wbuf, h, g, b, *prev_ln), *p_in)


def _ln_bwd(dy, r, g, *, name, loss_head=()):
    T = r.shape[0]
    tm = min(ROW_BLOCK, T)
    row = pl.BlockSpec((tm, D), lambda i: (i, 0))
    vec = pl.BlockSpec((1, D), lambda i: (0, 0))
    acc = pl.BlockSpec((8, D), lambda i: (0, 0))
    operands, in_specs = ([r, g, *loss_head], [row, vec, vec, row]) if loss_head else ([r, g, dy], [row, vec, row])
    n_in = len(operands)

    def body(*refs):
        r_ref, g_ref = refs[:2]
        dr_ref, drb_ref, dg_ref, db_ref = refs[n_in:n_in + 4]

        @pl.when(pl.program_id(0) == 0)
        def _():
            for ref in refs[n_in + 2:]:
                ref[...] = jnp.zeros_like(ref)

        xhat, rstd = _ln_stats(r_ref[...])
        if loss_head:
            err = xhat * g_ref[...] + refs[2][...] - refs[3][...]
            refs[n_in + 4][...] += _fold8(err * err)
            dy_ = err * (1.0 / D)
        else:
            dy_ = refs[2][...]
        dxh = dy_ * g_ref[...]
        m1 = jnp.mean(dxh, -1, keepdims=True)
        m2 = jnp.mean(dxh * xhat, -1, keepdims=True)
        dr = rstd * (dxh - m1 - xhat * m2)
        dr_ref[...] = dr
        drb_ref[...] = dr.astype(BF16)
        dg_ref[...] += _fold8(dy_ * xhat)
        db_ref[...] += _fold8(dy_)

    n_acc = 3 if loss_head else 2
    return pl.pallas_call(
        body, name=name, grid=(T // tm,), in_specs=in_specs, out_specs=[row, row] + [acc] * n_acc,
        out_shape=[jax.ShapeDtypeStruct((T, D), F32), jax.ShapeDtypeStruct((T, D), BF16)]
        + [jax.ShapeDtypeStruct((8, D), F32)] * n_acc,
        compiler_params=_params(40, 1),
    )(*_hbm(*operands))


def _rope(x, c, s1, s2):
    return x * c + pltpu.roll(x, 112, 1) * s1 + pltpu.roll(x, 16, 1) * s2


def _rope_t(dy, c, s1, s2):
    return dy * c + pltpu.roll(dy * s1, 16, 1) + pltpu.roll(dy * s2, 112, 1)


def _rms(x, g):
    rstd = lax.rsqrt(jnp.mean(x * x, -1, keepdims=True) + EPS)
    xhat = x * rstd
    return xhat * g, xhat, rstd


def _mla_prep(z0, gq, gkv, wq, wk, wv, rc, rs1, rs2):
    T = z0.shape[0]
    tm = min(ROW_BLOCK, T)
    HW = HEADS * 128

    def body(cq_ref, ckv_ref, kr_ref, gq_ref, gkv_ref, wq_ref, wk_ref, wv_ref, c_ref, s1_ref, s2_ref,
             q_ref, k_ref, v_ref):
        nq = _rms(cq_ref[...], gq_ref[...])[0].astype(BF16)
        nkv = _rms(ckv_ref[...], gkv_ref[...])[0].astype(BF16)
        q = jnp.dot(nq, wq_ref[...], preferred_element_type=F32)
        k = jnp.dot(nkv, wk_ref[...], preferred_element_type=F32)
        v = jnp.dot(nkv, wv_ref[...], preferred_element_type=F32)
        c, s1, s2 = c_ref[...], s1_ref[...], s2_ref[...]
        kr = _rope(pltpu.roll(kr_ref[...], 64, 1), c, s1, s2)
        for h in range(HEADS):
            sl = slice(h * 128, (h + 1) * 128)
            q_ref[:, sl] = (_rope(q[:, sl], c, s1, s2) * QK_SCALE).astype(BF16)
            k_ref[:, sl] = (k[:, sl] + kr).astype(BF16)
        v_ref[...] = v.astype(BF16)

    full = lambda shape: pl.BlockSpec(shape, lambda i: (0, 0))
    tab = pl.BlockSpec((tm, 128), lambda i: (i, 0))
    return pl.pallas_call(
        body, name="mla_prep", grid=(T // tm,),
        in_specs=[pl.BlockSpec((tm, 256), lambda i: (i, 0)), pl.BlockSpec((tm, 256), lambda i: (i, 1)),
                  pl.BlockSpec((tm, 128), lambda i: (i, 12)), full((1, 256)), full((1, 256)),
                  full((256, HW)), full((256, HW)), full((256, 512)), tab, tab, tab],
        out_specs=[pl.BlockSpec((tm, HW), lambda i: (i, 0)), pl.BlockSpec((tm, HW), lambda i: (i, 0)),
                   pl.BlockSpec((tm, 512), lambda i: (i, 0))],
        out_shape=[jax.ShapeDtypeStruct((T, HW), BF16), jax.ShapeDtypeStruct((T, HW), BF16),
                   jax.ShapeDtypeStruct((T, 512), BF16)],
        compiler_params=_params(40, 1),
    )(*_hbm(z0, z0, z0, gq, gkv, wq, wk, wv, rc, rs1, rs2))


def _flash_fwd(q, k, v, plan=None):
    T = q.shape[0]
    bq = min(2 * ROW_BLOCK, T)
    nq = T // bq
    pairs = [(i, j) for i in range(nq) for j in range(i + 1)]
    imap, jmap = (jnp.array(m, jnp.int32) for m in zip(*pairs))
    grid = (4, len(pairs))

    def body(imap_ref, jmap_ref, *refs):
        (q_ref, k_ref, v_ref), (o_ref, lse_ref), (m_sc, acc_sc), pctx = _split_refs(refs, 3, 2, 2, plan)
        _plan_start(plan, pctx, grid)
        i, j = imap_ref[pl.program_id(1)], jmap_ref[pl.program_id(1)]
        first = lax.broadcasted_iota(jnp.int32, (bq, 128), 1) < 64

        @pl.when(j == 0)
        def _():
            m_sc[...] = jnp.full_like(m_sc, -jnp.inf)
            acc_sc[...] = jnp.zeros_like(acc_sc)

        def step(masked):
            vp = v_ref[...]
            for h in range(2):
                sl = slice(h * 128, (h + 1) * 128)
                s = lax.dot_general(q_ref[:, sl], k_ref[:, sl], NT_DIMS, preferred_element_type=F32)
                if masked:
                    rows = lax.broadcasted_iota(jnp.int32, (bq, bq), 0)
                    cols = lax.broadcasted_iota(jnp.int32, (bq, bq), 1)
                    s = jnp.where(cols <= rows, s, -jnp.inf)
                m_prev = m_sc[h, :, 0:1]
                m_new = jnp.maximum(m_prev, jnp.max(s, axis=1, keepdims=True))
                alpha = jnp.exp(m_prev - m_new)
                p = jnp.exp(s - m_new).astype(BF16)
                vh = jnp.where(first if h == 0 else jnp.logical_not(first), vp, jnp.ones_like(vp))
                acc_sc[h] = acc_sc[h] * alpha + jnp.dot(p, vh, preferred_element_type=F32)
                m_sc[h] = jnp.broadcast_to(m_new, (bq, 128))

        @pl.when(j < i)
        def _():
            step(False)

        @pl.when(j == i)
        def _():
            step(True)
            a0, a1 = acc_sc[0], acc_sc[1]
            l0, l1 = pltpu.roll(a0, 64, 1), pltpu.roll(a1, 64, 1)
            o_ref[...] = jnp.where(first, a0 / l0, a1 / l1).astype(BF16)
            lse_ref[...] = jnp.where(first, m_sc[0] + jnp.log(l0), m_sc[1] + jnp.log(l1))

        _plan_wait(plan, pctx, grid)

    qi = lambda hp, t, im, jm: (im[t], hp)
    kj = lambda hp, t, im, jm: (jm[t], hp)
    p_in, p_ospec, p_oshape, p_scr, p_alias = _plan_io(plan, 2 + 3, 2)
    return pl.pallas_call(
        body, name="flash_fwd",
        out_shape=[jax.ShapeDtypeStruct((T, 512), BF16), jax.ShapeDtypeStruct((T, 512), F32)] + p_oshape,
        grid_spec=pltpu.PrefetchScalarGridSpec(
            num_scalar_prefetch=2, grid=grid,
            in_specs=[pl.BlockSpec((bq, 256), qi), pl.BlockSpec((bq, 256), kj), pl.BlockSpec((bq, 128), kj)]
            + [_ANY] * len(p_in),
            out_specs=[pl.BlockSpec((bq, 128), qi), pl.BlockSpec((bq, 128), qi)] + p_ospec,
            scratch_shapes=[pltpu.VMEM((2, bq, 128), F32), pltpu.VMEM((2, bq, 128), F32)] + p_scr),
        input_output_aliases=p_alias, compiler_params=_params(56, 2),
    )(imap, jmap, *_hbm(q, k, v), *p_in)


def _attn_delta(dmix, o):
    T = o.shape[0]
    tm = min(ROW_BLOCK, T)
    blk = pl.BlockSpec((tm, 512), lambda i: (i, 0))

    def body(do_ref, o_ref, delta_ref, dob_ref):
        first = lax.broadcasted_iota(jnp.int32, (tm, 128), 1) < 64
        for hp in range(4):
            sl = slice(hp * 128, (hp + 1) * 128)
            prod = do_ref[:, sl] * o_ref[:, sl].astype(F32)
            d0 = jnp.sum(jnp.where(first, prod, 0.0), axis=1, keepdims=True)
            d1 = jnp.sum(jnp.where(first, 0.0, prod), axis=1, keepdims=True)
            delta_ref[:, sl] = jnp.where(first, d0, d1)
        dob_ref[...] = do_ref[...].astype(BF16)

    return pl.pallas_call(
        body, name="attn_delta", grid=(T // tm,), in_specs=[blk, blk], out_specs=[blk, blk],
        out_shape=[jax.ShapeDtypeStruct((T, 512), F32), jax.ShapeDtypeStruct((T, 512), BF16)],
        compiler_params=_params(32, 1),
    )(*_hbm(dmix, o))


def _flash_bwd(q, k, v, do_b, lse, delta, plan=None):
    T = q.shape[0]
    bq = min(2 * ROW_BLOCK, T)
    nq = T // bq
    pairs = [(i, j) for j in range(nq) for i in range(j, nq)]
    imap, jmap = (jnp.array(m, jnp.int32) for m in zip(*pairs))
    grid = (4, len(pairs))

    def body(imap_ref, jmap_ref, *refs):
        ((q_ref, k_ref, v_ref, do_ref, lse_ref, dl_ref), (dq_hbm, dk_ref, dv_ref), (dq_sc, dk_sc, dv_sc, sem),
         pctx) = _split_refs(refs, 6, 3, 4, plan)
        _plan_start(plan, pctx, grid)
        hp = pl.program_id(0)
        i, j = imap_ref[pl.program_id(1)], jmap_ref[pl.program_id(1)]
        first = lax.broadcasted_iota(jnp.int32, (bq, 128), 1) < 64

        @pl.when((j == 0) & (i == 0))
        def _():
            dq_sc[...] = jnp.zeros_like(dq_sc)

        @pl.when(i == j)
        def _():
            dk_sc[...] = jnp.zeros_like(dk_sc)
            dv_sc[...] = jnp.zeros_like(dv_sc)

        def tile(r0, nr, nc, masked):
            rs, cs = slice(r0, r0 + nr), slice(0, nc)
            vp = v_ref[cs, :]
            do = do_ref[rs, :]
            lanes = first[rs, :]
            for h in range(2):
                sl = slice(h * 128, (h + 1) * 128)
                qh, kh = q_ref[rs, sl], k_ref[cs, sl]
                s = lax.dot_general(qh, kh, NT_DIMS, preferred_element_type=F32)
                p = jnp.exp(s - lse_ref[rs, h * 64:h * 64 + 1])
                if masked:
                    rows = r0 + lax.broadcasted_iota(jnp.int32, (nr, nc), 0)
                    cols = lax.broadcasted_iota(jnp.int32, (nr, nc), 1)
                    p = jnp.where(cols <= rows, p, 0.0)
                do_h = jnp.where(lanes if h == 0 else jnp.logical_not(lanes), do, jnp.zeros_like(do))
                dv_sc[cs, :] += lax.dot_general(p.astype(BF16), do_h, TN_DIMS, preferred_element_type=F32)
                dp = lax.dot_general(do_h, vp, NT_DIMS, preferred_element_type=F32)
                ds = (p * (dp - dl_ref[rs, h * 64:h * 64 + 1])).astype(BF16)
                dq_sc[i, rs, sl] += jnp.dot(ds, kh, preferred_element_type=F32)
                dk_sc[cs, sl] += lax.dot_general(ds, qh, TN_DIMS, preferred_element_type=F32)

        @pl.when(i > j)
        def _():
            tile(0, bq, bq, False)

        @pl.when(i == j)
        def _():
            tile(0, bq // 2, bq // 2, True)
            tile(bq // 2, bq // 2, bq, True)

        @pl.when(i == nq - 1)
        def _():
            dk_ref[...] = dk_sc[...]
            dv_ref[...] = dv_sc[...]

        @pl.when((j == nq - 1) & (i == nq - 1))
        def _():
            cp = pltpu.make_async_copy(dq_sc, dq_hbm.at[hp], sem)
            cp.start()
            cp.wait()

        _plan_wait(plan, pctx, grid)

    qi = lambda hp, t, im, jm: (im[t], hp)
    kj = lambda hp, t, im, jm: (jm[t], hp)
    p_in, p_ospec, p_oshape, p_scr, p_alias = _plan_io(plan, 2 + 6, 3)
    return pl.pallas_call(
        body, name="flash_bwd",
        out_shape=[jax.ShapeDtypeStruct((4, nq, bq, 256), F32), jax.ShapeDtypeStruct((T, 1024), F32),
                   jax.ShapeDtypeStruct((T, 512), F32)] + p_oshape,
        grid_spec=pltpu.PrefetchScalarGridSpec(
            num_scalar_prefetch=2, grid=grid,
            in_specs=[pl.BlockSpec((bq, 256), qi), pl.BlockSpec((bq, 256), kj), pl.BlockSpec((bq, 128), kj),
                      pl.BlockSpec((bq, 128), qi), pl.BlockSpec((bq, 128), qi), pl.BlockSpec((bq, 128), qi)]
            + [_ANY] * len(p_in),
            out_specs=[_ANY, pl.BlockSpec((bq, 256), kj), pl.BlockSpec((bq, 128), kj)] + p_ospec,
            scratch_shapes=[pltpu.VMEM((nq, bq, 256), F32), pltpu.VMEM((bq, 256), F32), pltpu.VMEM((bq, 128), F32),
                            pltpu.SemaphoreType.DMA] + p_scr),
        input_output_aliases=p_alias, compiler_params=_params(56, 2),
    )(imap, jmap, *_hbm(q, k, v, do_b, lse, delta), *p_in)


def _mla_bwd(z0, dq4, dk, dv, gq, gkv, wq, wk, wv, rc, rs1, rs2, plan=None):
    T = z0.shape[0]
    tm = min(ROW_BLOCK, T)
    HW = HEADS * 128
    grid = (T // tm,)
    dq4 = dq4.reshape(4, T, 256)

    def body(*refs):
        ((cq_ref, ckv_ref, dq_ref, dk_ref, dv_ref, gq_ref, gkv_ref, wq_ref, wk_ref, wv_ref, c_ref, s1_ref, s2_ref),
         (dc_ref, dkr_ref, dwq_ref, dwk_ref, dwv_ref, dgq_ref, dgkv_ref), _, pctx) = _split_refs(refs, 13, 7, 0, plan)
        _plan_start(plan, pctx, grid)

        @pl.when(pl.program_id(0) == 0)
        def _():
            for ref in (dwq_ref, dwk_ref, dwv_ref, dgq_ref, dgkv_ref):
                ref[...] = jnp.zeros_like(ref)

        c, s1, s2 = c_ref[...], s1_ref[...], s2_ref[...]
        lane = lax.broadcasted_iota(jnp.int32, (tm, 128), 1)
        nq, xq, rq = _rms(cq_ref[...], gq_ref[...])
        nkv, xkv, rkv = _rms(ckv_ref[...], gkv_ref[...])
        nq_b, nkv_b = nq.astype(BF16), nkv.astype(BF16)

        dq_parts, dk_parts = [], []
        dkr = jnp.zeros((tm, 128), F32)
        for h in range(HEADS):
            blk = dq_ref[h // 2, :, (h % 2) * 128:(h % 2 + 1) * 128] * QK_SCALE
            dq_parts.append(_rope_t(blk, c, s1, s2).astype(BF16))
            kb = dk_ref[:, h * 128:(h + 1) * 128]
            dk_parts.append(jnp.where(lane < NOPE, kb, 0.0).astype(BF16))
            dkr = dkr + kb
        dq_b = jnp.concatenate(dq_parts, axis=1)
        dk_b = jnp.concatenate(dk_parts, axis=1)
        dv_b = dv_ref[...].astype(BF16)

        dwq_ref[...] += lax.dot_general(nq_b, dq_b, TN_DIMS, preferred_element_type=F32)
        dwk_ref[...] += lax.dot_general(nkv_b, dk_b, TN_DIMS, preferred_element_type=F32)
        dwv_ref[...] += lax.dot_general(nkv_b, dv_b, TN_DIMS, preferred_element_type=F32)
        dnq = lax.dot_general(dq_b, wq_ref[...], NT_DIMS, preferred_element_type=F32)
        dnkv = (lax.dot_general(dk_b, wk_ref[...], NT_DIMS, preferred_element_type=F32)
                + lax.dot_general(dv_b, wv_ref[...], NT_DIMS, preferred_element_type=F32))

        def rms_bwd(dn, xhat, rstd, g):
            dxh = dn * g
            return rstd * (dxh - xhat * jnp.mean(dxh * xhat, -1, keepdims=True))

        dc_ref[:, :256] = rms_bwd(dnq, xq, rq, gq_ref[...]).astype(BF16)
        dc_ref[:, 256:] = rms_bwd(dnkv, xkv, rkv, gkv_ref[...]).astype(BF16)
        dgq_ref[...] += _fold8(dnq * xq)
        dgkv_ref[...] += _fold8(dnkv * xkv)
        dkr = pltpu.roll(_rope_t(dkr, c, s1, s2), 64, 1)
        dkr_ref[...] = jnp.where(lane < ROPE, dkr, 0.0).astype(BF16)
        _plan_wait(plan, pctx, grid)

    full = lambda shape: pl.BlockSpec(shape, lambda i: (0,) * len(shape))
    tab = pl.BlockSpec((tm, 128), lambda i: (i, 0))
    p_in, p_ospec, p_oshape, p_scr, p_alias = _plan_io(plan, 13, 7)
    return pl.pallas_call(
        body, name="mla_bwd", grid=grid,
        in_specs=[pl.BlockSpec((tm, 256), lambda i: (i, 0)), pl.BlockSpec((tm, 256), lambda i: (i, 1)),
                  pl.BlockSpec((4, tm, 256), lambda i: (0, i, 0)),
                  pl.BlockSpec((tm, HW), lambda i: (i, 0)), pl.BlockSpec((tm, 512), lambda i: (i, 0)),
                  full((1, 256)), full((1, 256)), full((256, HW)), full((256, HW)), full((256, 512)), tab, tab, tab]
        + [_ANY] * len(p_in),
        out_specs=[pl.BlockSpec((tm, 512), lambda i: (i, 0)), tab, full((256, HW)), full((256, HW)),
                   full((256, 512)), full((8, 256)), full((8, 256))] + p_ospec,
        out_shape=[jax.ShapeDtypeStruct((T, 512), BF16), jax.ShapeDtypeStruct((T, 128), BF16),
                   jax.ShapeDtypeStruct((256, HW), F32), jax.ShapeDtypeStruct((256, HW), F32),
                   jax.ShapeDtypeStruct((256, 512), F32), jax.ShapeDtypeStruct((8, 256), F32),
                   jax.ShapeDtypeStruct((8, 256), F32)] + p_oshape,
        scratch_shapes=p_scr, input_output_aliases=p_alias, compiler_params=_params(48, 1),
    )(*_hbm(z0, z0, dq4, dk, dv, gq, gkv, wq, wk, wv, rc, rs1, rs2), *p_in)


def _sgu_fwd(z0, a_out, ln_g, ln_b, w, b_t):
    T = z0.shape[0]
    tm = min(ROW_BLOCK, T)
    W = SGU_G * SGU_C

    def body(u_ref, v_ref, a_ref, g_ref, b_ref, w_ref, bt_ref, o_ref):
        o_ref[:, :W] = a_ref[...]
        ug = _gelu(u_ref[...])
        xhat, _ = _ln_stats(_gelu(v_ref[...]))
        vn = (xhat * g_ref[...] + b_ref[...]).astype(BF16)
        tril = lax.broadcasted_iota(jnp.int32, (SGU_C, SGU_C), 0) >= lax.broadcasted_iota(jnp.int32, (SGU_C, SGU_C), 1)
        for g in range(SGU_G):
            cs = slice(g * SGU_C, (g + 1) * SGU_C)
            wg = jnp.where(tril, w_ref[g], 0.0).astype(BF16)
            bcol = bt_ref[:, g:g + 1]
            for c in range(tm // SGU_C):
                rs = slice(c * SGU_C, (c + 1) * SGU_C)
                mixed = jnp.dot(wg, vn[rs, cs], preferred_element_type=F32) + bcol
                o_ref[rs, W + g * SGU_C:W + (g + 1) * SGU_C] = (ug[rs, cs] * mixed).astype(BF16)

    full = lambda shape: pl.BlockSpec(shape, lambda i: (0,) * len(shape))
    return pl.pallas_call(
        body, name="sgu_fwd", grid=(T // tm,),
        in_specs=[pl.BlockSpec((tm, W), lambda i: (i, 1)), pl.BlockSpec((tm, W), lambda i: (i, 2)),
                  pl.BlockSpec((tm, W), lambda i: (i, 0)),
                  full((1, W)), full((1, W)), full((SGU_G, SGU_C, SGU_C)), full((SGU_C, SGU_G))],
        out_specs=pl.BlockSpec((tm, 2 * W), lambda i: (i, 0)),
        out_shape=jax.ShapeDtypeStruct((T, 2 * W), BF16),
        compiler_params=_params(32, 1),
    )(*_hbm(z0, z0, a_out, ln_g, ln_b, w, b_t))


def _sgu_bwd(z0, dmix, dc, dkr, ln_g, ln_b, w, b_t, plan=None):
    T = z0.shape[0]
    tm = min(ROW_BLOCK, T)
    W = SGU_G * SGU_C
    grid = (T // tm,)

    def body(*refs):
        ((u_ref, v_ref, do_ref, dc_ref, dkr_ref, g_ref, b_ref, w_ref, bt_ref),
         (dz_ref, dw_ref, db_ref, dlg_ref, dlb_ref), _, pctx) = _split_refs(refs, 9, 5, 0, plan)
        _plan_start(plan, pctx, grid)

        @pl.when(pl.program_id(0) == 0)
        def _():
            for ref in (dw_ref, db_ref, dlg_ref, dlb_ref):
                ref[...] = jnp.zeros_like(ref)

        dz_ref[:, :W] = dc_ref[...]
        dz_ref[:, 3 * W:] = dkr_ref[...]

        u, v, dout = u_ref[...], v_ref[...], do_ref[...]
        ug = _gelu(u)
        xhat, rstd = _ln_stats(_gelu(v))
        vn = (xhat * g_ref[...] + b_ref[...]).astype(BF16)
        dmixed = dout * ug
        dmixed_b = dmixed.astype(BF16)
        tril = lax.broadcasted_iota(jnp.int32, (SGU_C, SGU_C), 0) >= lax.broadcasted_iota(jnp.int32, (SGU_C, SGU_C), 1)
        lane = lax.broadcasted_iota(jnp.int32, (SGU_C, SGU_C), 1)
        dvn_cols = []
        for g in range(SGU_G):
            cs = slice(g * SGU_C, (g + 1) * SGU_C)
            wg = jnp.where(tril, w_ref[g], 0.0).astype(BF16)
            bcol = bt_ref[:, g:g + 1]
            dw_g = jnp.zeros((SGU_C, SGU_C), F32)
            db_g = jnp.zeros((SGU_C, 1), F32)
            dvn_rows = []
            for c in range(tm // SGU_C):
                rs = slice(c * SGU_C, (c + 1) * SGU_C)
                mixed = jnp.dot(wg, vn[rs, cs], preferred_element_type=F32) + bcol
                dz_ref[rs, W + g * SGU_C:W + (g + 1) * SGU_C] = (dout[rs, cs] * mixed * _gelu_grad(u[rs, cs])).astype(BF16)
                dm = dmixed_b[rs, cs]
                dw_g = dw_g + lax.dot_general(dm, vn[rs, cs], NT_DIMS, preferred_element_type=F32)
                db_g = db_g + jnp.sum(dmixed[rs, cs], axis=1, keepdims=True)
                dvn_rows.append(lax.dot_general(wg, dm, TN_DIMS, preferred_element_type=F32))
            dw_ref[g] += jnp.where(tril, dw_g, 0.0)
            db_ref[...] += jnp.where(lane == g, db_g, 0.0)
            dvn_cols.append(jnp.concatenate(dvn_rows, axis=0))
        dvn = jnp.concatenate(dvn_cols, axis=1)
        dxh = dvn * g_ref[...]
        m1 = jnp.mean(dxh, -1, keepdims=True)
        m2 = jnp.mean(dxh * xhat, -1, keepdims=True)
        dvg = rstd * (dxh - m1 - xhat * m2)
        dz_ref[:, 2 * W:3 * W] = (dvg * _gelu_grad(v)).astype(BF16)
        dlg_ref[...] += _fold8(dvn * xhat)
        dlb_ref[...] += _fold8(dvn)
        _plan_wait(plan, pctx, grid)

    full = lambda shape: pl.BlockSpec(shape, lambda i: (0,) * len(shape))
    p_in, p_ospec, p_oshape, p_scr, p_alias = _plan_io(plan, 9, 5)
    return pl.pallas_call(
        body, name="sgu_bwd", grid=grid,
        in_specs=[pl.BlockSpec((tm, W), lambda i: (i, 1)), pl.BlockSpec((tm, W), lambda i: (i, 2)),
                  pl.BlockSpec((tm, W), lambda i: (i, 1)), pl.BlockSpec((tm, W), lambda i: (i, 0)),
                  pl.BlockSpec((tm, 128), lambda i: (i, 0)),
                  full((1, W)), full((1, W)), full((SGU_G, SGU_C, SGU_C)), full((SGU_C, SGU_G))] + [_ANY] * len(p_in),
        out_specs=[pl.BlockSpec((tm, 3 * W + 128), lambda i: (i, 0)), full((SGU_G, SGU_C, SGU_C)),
                   full((SGU_C, SGU_C)), full((8, W)), full((8, W))] + p_ospec,
        out_shape=[jax.ShapeDtypeStruct((T, 3 * W + 128), BF16), jax.ShapeDtypeStruct((SGU_G, SGU_C, SGU_C), F32),
                   jax.ShapeDtypeStruct((SGU_C, SGU_C), F32), jax.ShapeDtypeStruct((8, W), F32),
                   jax.ShapeDtypeStruct((8, W), F32)] + p_oshape,
        scratch_shapes=p_scr, input_output_aliases=p_alias, compiler_params=_params(40, 1),
    )(*_hbm(z0, z0, dmix, dc, dkr, ln_g, ln_b, w, b_t), *p_in)


def _hg_lower_bound(lb_ref):
    a0, a1 = lb_ref[0:1, :], lb_ref[1:2, :]
    m = jnp.maximum(a0, a1)
    e0, e1 = jnp.exp(a0 - m), jnp.exp(a1 - m)
    return e1 / (e0 + e1)


def _running_sum(x, reverse=False):
    n = x.shape[0]
    row = lax.broadcasted_iota(jnp.int32, x.shape, 0)
    s = 1
    while s < n:
        if reverse:
            x = x + jnp.where(row < n - s, pltpu.roll(x, n - s, 0), 0.0)
        else:
            x = x + jnp.where(row >= s, pltpu.roll(x, s, 0), 0.0)
        s *= 2
    return x


def _hg_chunk(qc, fc, lb):
    C = HG_CHUNK
    rows = lax.broadcasted_iota(jnp.int32, (C, C), 0)
    cols = lax.broadcasted_iota(jnp.int32, (C, C), 1)
    rowid = lax.broadcasted_iota(jnp.int32, (C, 128), 0)
    sq, sg = _sigmoid(qc), _sigmoid(fc)
    qf = qc * sq
    gate = lb + (1.0 - lb) * sg
    kk = 1.0 - gate
    lg = jnp.log(gate)
    bcum = _running_sum(lg)
    b_mid = jnp.sum(jnp.where(rowid < C // 2, lg, 0.0), axis=0, keepdims=True)
    b_last = jnp.sum(lg, axis=0, keepdims=True)
    eq, ek, e, eh = jnp.exp(bcum - b_mid), jnp.exp(b_mid - bcum), jnp.exp(bcum), jnp.exp(b_last - bcum)
    qt, kt, qe, khat = qf * eq, kk * ek, qf * e, kk * eh
    a = lax.dot_general(qt.astype(BF16), kt.astype(BF16), NT_DIMS, preferred_element_type=F32)
    a = jnp.where(rows >= cols, a, 0.0)
    return dict(sq=sq, sg=sg, gate=gate, kk=kk, eq=eq, ek=ek, e=e, eh=eh, qt=qt, kt=kt, qe=qe, khat=khat, a=a,
                e_last=jnp.exp(b_last), tril=rows >= cols, rowid=rowid)


def _hgrn_fwd(z4, hg_lb, gnorm):
    T = z4.shape[1]
    tb = min(ROW_BLOCK, T)
    C = HG_CHUNK
    ncb = tb // C
    HPB = HG_HEADS_PER_STEP

    def body(q_ref, f_ref, i_ref, g_ref, lb_ref, gn_ref, y_ref, o_ref, st_ref, st_sc):
        @pl.when(pl.program_id(1) == 0)
        def _():
            st_sc[...] = jnp.zeros_like(st_sc)

        def chunk(c, carry):
            rs = pl.ds(pl.multiple_of(c * C, C), C)
            for hh in range(HPB):
                hs = slice(hh * 128, (hh + 1) * 128)
                lb = _hg_lower_bound(lb_ref.at[:, hs])
                v_b = i_ref[rs, hs].astype(BF16)
                gc = g_ref[rs, hs]
                x = _hg_chunk(q_ref[rs, hs], f_ref[rs, hs], lb)
                st = st_sc[hh]
                st_ref[hh, c] = st
                o = (jnp.dot(x["a"].astype(BF16), v_b, preferred_element_type=F32)
                     + lax.dot_general(x["qe"].astype(BF16), st.astype(BF16), NT_DIMS, preferred_element_type=F32))
                st_sc[hh] = st * x["e_last"] + lax.dot_general(v_b, x["khat"].astype(BF16), TN_DIMS,
                                                               preferred_element_type=F32)
                o_ref[rs, hs] = o
                n = o * lax.rsqrt(jnp.mean(o * o, -1, keepdims=True) + EPS)
                y_ref[rs, hs] = (n * gn_ref[:, hs] * (gc * _sigmoid(gc))).astype(BF16)
            return carry

        lax.fori_loop(0, ncb, chunk, 0, unroll=4)

    W = 128 * HPB
    zb = lambda k: pl.BlockSpec((None, tb, W), lambda h, t: (k, t, h))
    out = pl.BlockSpec((tb, W), lambda h, t: (t, h))
    return pl.pallas_call(
        body, name="hgrn_fwd", grid=(HEADS // HPB, T // tb),
        in_specs=[zb(0), zb(1), zb(2), zb(3), pl.BlockSpec((2, W), lambda h, t: (0, h)),
                  pl.BlockSpec((1, W), lambda h, t: (0, h))],
        out_specs=[out, out, pl.BlockSpec((HPB, ncb, 128, 128), lambda h, t: (h, t, 0, 0))],
        out_shape=[jax.ShapeDtypeStruct((T, D), BF16), jax.ShapeDtypeStruct((T, D), F32),
                   jax.ShapeDtypeStruct((HEADS, T // C, 128, 128), F32)],
        scratch_shapes=[pltpu.VMEM((HPB, 128, 128), F32)],
        compiler_params=_params(48, 2),
    )(*_hbm(z4, z4, z4, z4, hg_lb, gnorm))


def _hgrn_bwd(z4, o_raw, dy, states, hg_lb, gnorm):
    T = z4.shape[1]
    tb = min(ROW_BLOCK, T)
    C = HG_CHUNK
    ncb = tb // C
    nt = T // tb
    HPB = HG_HEADS_PER_STEP

    def body(q_ref, f_ref, i_ref, g_ref, o_ref, dy_ref, st_ref, lb_ref, gn_ref, dz_ref, dlb_ref, dgn_ref, dst_sc):
        @pl.when(pl.program_id(1) == 0)
        def _():
            dst_sc[...] = jnp.zeros_like(dst_sc)
            dlb_ref[...] = jnp.zeros_like(dlb_ref)
            dgn_ref[...] = jnp.zeros_like(dgn_ref)

        def chunk(cc, carry):
            for hh in range(HPB):
                one_head(ncb - 1 - cc, hh, slice(hh * 128, (hh + 1) * 128))
            return carry

        def one_head(c, hh, hs):
            rs = pl.ds(pl.multiple_of(c * C, C), C)
            lb = _hg_lower_bound(lb_ref.at[:, hs])
            gn = gn_ref[:, hs]
            qc, gc = q_ref[rs, hs], g_ref[rs, hs]
            v_b = i_ref[rs, hs].astype(BF16)
            x = _hg_chunk(qc, f_ref[rs, hs], lb)
            st, dst = st_ref[hh, c], dst_sc[hh]
            st_b, dst_b = st.astype(BF16), dst.astype(BF16)
            o, dyc = o_ref[rs, hs], dy_ref[rs, hs]
            sgg = _sigmoid(gc)
            sil = gc * sgg
            rstd = lax.rsqrt(jnp.mean(o * o, -1, keepdims=True) + EPS)
            n = o * rstd
            dgn_ref[:, hs] += _fold8(dyc * n * sil)
            dn = dyc * gn * sil
            do = rstd * (dn - n * jnp.mean(dn * n, -1, keepdims=True))
            dg = dyc * n * gn * (sgg * (1.0 + gc * (1.0 - sgg)))
            do_b = do.astype(BF16)
            da = jnp.where(x["tril"], lax.dot_general(do_b, v_b, NT_DIMS, preferred_element_type=F32), 0.0).astype(BF16)
            qt_b, kt_b, qe_b, khat_b = (x[n_].astype(BF16) for n_ in ("qt", "kt", "qe", "khat"))
            dv = (lax.dot_general(x["a"].astype(BF16), do_b, TN_DIMS, preferred_element_type=F32)
                  + lax.dot_general(khat_b, dst_b, NT_DIMS, preferred_element_type=F32))
            dqt = jnp.dot(da, kt_b, preferred_element_type=F32)
            dqe = jnp.dot(do_b, st_b, preferred_element_type=F32)
            dkt = lax.dot_general(da, qt_b, TN_DIMS, preferred_element_type=F32)
            dkhat = jnp.dot(v_b, dst_b, preferred_element_type=F32)
            dst_sc[hh] = lax.dot_general(do_b, qe_b, TN_DIMS, preferred_element_type=F32) + dst * x["e_last"]
            de_last = jnp.sum(st * dst, axis=0, keepdims=True)
            dqf = dqt * x["eq"] + dqe * x["e"]
            dkk = dkt * x["ek"] + dkhat * x["eh"]
            dkh_kh = dkhat * x["khat"]
            db = dqt * qt_b.astype(F32) - dkt * kt_b.astype(F32) + dqe * x["qe"] - dkh_kh
            db_last = jnp.sum(dkh_kh, axis=0, keepdims=True) + de_last * x["e_last"]
            db = db + jnp.where(x["rowid"] == C - 1, db_last, 0.0)
            dlg = _running_sum(db, reverse=True)
            dgate = dlg / x["gate"] - dkk
            sg, sq = x["sg"], x["sq"]
            dlb_ref[:, hs] += _fold8(dgate * (1.0 - sg)) * (lb * (1.0 - lb))
            dz_ref[0, rs, hs] = (dqf * (sq * (1.0 + qc * (1.0 - sq)))).astype(BF16)
            dz_ref[1, rs, hs] = (dgate * (1.0 - lb) * sg * (1.0 - sg)).astype(BF16)
            dz_ref[2, rs, hs] = dv.astype(BF16)
            dz_ref[3, rs, hs] = dg.astype(BF16)

        lax.fori_loop(0, ncb, chunk, 0, unroll=4)

    W = 128 * HPB
    zb = lambda k: pl.BlockSpec((None, tb, W), lambda h, t: (k, nt - 1 - t, h))
    blk = pl.BlockSpec((tb, W), lambda h, t: (nt - 1 - t, h))
    acc = pl.BlockSpec((8, W), lambda h, t: (0, h))
    return pl.pallas_call(
        body, name="hgrn_bwd", grid=(HEADS // HPB, nt),
        in_specs=[zb(0), zb(1), zb(2), zb(3), blk, blk,
                  pl.BlockSpec((HPB, ncb, 128, 128), lambda h, t: (h, nt - 1 - t, 0, 0)),
                  pl.BlockSpec((2, W), lambda h, t: (0, h)), pl.BlockSpec((1, W), lambda h, t: (0, h))],
        out_specs=[pl.BlockSpec((4, tb, W), lambda h, t: (0, nt - 1 - t, h)), acc, acc],
        out_shape=[jax.ShapeDtypeStruct((4, T, D), BF16), jax.ShapeDtypeStruct((8, D), F32),
                   jax.ShapeDtypeStruct((8, D), F32)],
        scratch_shapes=[pltpu.VMEM((HPB, 128, 128), F32)],
        compiler_params=_params(48, 2),
    )(*_hbm(z4, z4, z4, z4, o_raw, dy, states, hg_lb, gnorm))


def _adamw(w, g, m, v, *, name, with_g=False):
    R, L = w.shape
    tr = R if R <= 512 else 512
    assert R % tr == 0
    blk = pl.BlockSpec((tr, L), lambda i: (i, 0))
    c1, c2 = 1.0 - B1 ** STEP, 1.0 - B2 ** STEP
    n_out = 4 if with_g else 3

    def body(w_ref, g_ref, m_ref, v_ref, *o_refs):
        d_ref, mo_ref, vo_ref = o_refs[-3:]
        g_ = g_ref[...]
        m_ = B1 * m_ref[...] + (1.0 - B1) * g_
        v_ = B2 * v_ref[...] + (1.0 - B2) * (g_ * g_)
        if with_g:
            o_refs[0][...] = g_
        d_ref[...] = -LR * ((m_ / c1) / (jnp.sqrt(v_ / c2) + ADAM_EPS) + WD * w_ref[...])
        mo_ref[...] = m_
        vo_ref[...] = v_

    sds = jax.ShapeDtypeStruct((R, L), F32)
    return pl.pallas_call(
        body, name=name, grid=(R // tr,), in_specs=[blk] * 4, out_specs=[blk] * n_out, out_shape=[sds] * n_out,
        compiler_params=_params(32, 1),
    )(*_hbm(w, g, m, v))


def _adamw_rows(w, m, v, gbufs, row0, *, name, plan=None):
    L, R, C = w.shape
    tr = 256
    assert R % tr == 0 and row0 % tr == 0 and len(gbufs) == L
    grid = (L, R // tr)
    blk = pl.BlockSpec((None, tr, C), lambda l, i: (l, i, 0))
    gblks = [pl.BlockSpec((tr, C), lambda l, i, k=k: (row0 // tr + jnp.where(l == k, i, 0), 0)) for k in range(L)]
    c1, c2 = 1.0 - B1 ** STEP, 1.0 - B2 ** STEP

    def body(*refs):
        ins, (go_ref, d_ref, mo_ref, vo_ref), _, pctx = _split_refs(refs, 3 + L, 4, 0, plan)
        w_ref, m_ref, v_ref = ins[:3]
        g_refs = ins[3:]
        _plan_start(plan, pctx, grid)
        g_ = g_refs[0][...]
        for l in range(1, L):
            g_ = jnp.where(pl.program_id(0) == l, g_refs[l][...], g_)
        m_ = B1 * m_ref[...] + (1.0 - B1) * g_
        v_ = B2 * v_ref[...] + (1.0 - B2) * (g_ * g_)
        go_ref[...] = g_
        d_ref[...] = -LR * ((m_ / c1) / (jnp.sqrt(v_ / c2) + ADAM_EPS) + WD * w_ref[...])
        mo_ref[...] = m_
        vo_ref[...] = v_
        _plan_wait(plan, pctx, grid)

    sds = jax.ShapeDtypeStruct((L, R, C), F32)
    p_in, p_ospec, p_oshape, p_scr, p_alias = _plan_io(plan, 3 + L, 4)
    return pl.pallas_call(
        body, name=name, grid=grid, in_specs=[blk] * 3 + gblks + [_ANY] * len(p_in),
        out_specs=[blk] * 4 + p_ospec, out_shape=[sds] * 4 + p_oshape, scratch_shapes=p_scr,
        input_output_aliases=p_alias, compiler_params=_params(32, 2),
    )(*_hbm(w, m, v, *gbufs), *p_in)


def _add_pairs(g, theirs, ids, *, name):
    n, R, L = theirs.shape
    tr = math.gcd(R, 128)
    nb = R // tr

    def body(ids_ref, a_ref, b_ref, o_ref):
        o_ref[...] = (a_ref[...].astype(F32) + b_ref[...].astype(F32)).astype(BF16)

    blk = pl.BlockSpec((n, tr, L), lambda i, ids: (0, i, 0))
    return pl.pallas_call(
        body, name=name, out_shape=jax.ShapeDtypeStruct((n, R, L), BF16),
        grid_spec=pltpu.PrefetchScalarGridSpec(
            num_scalar_prefetch=1, grid=(nb,),
            in_specs=[pl.BlockSpec((n, tr, L), lambda i, ids: (0, ids[1] * nb + i, 0)), blk], out_specs=blk),
        compiler_params=_params(16, 1),
    )(ids, *_hbm(g, theirs))


def _sum_chips(pair, parts, ids, *, name):
    _, R, L = parts.shape
    tr = math.gcd(R, 128)

    def body(ids_ref, o_ref, r_ref, out_ref):
        out_ref[...] = ((o_ref[...].astype(F32) + r_ref[0].astype(F32)) + r_ref[1].astype(F32)) + r_ref[2].astype(F32)

    return pl.pallas_call(
        body, name=name, out_shape=jax.ShapeDtypeStruct((2, R, L), F32),
        grid_spec=pltpu.PrefetchScalarGridSpec(
            num_scalar_prefetch=1, grid=(R // tr,),
            in_specs=[pl.BlockSpec((None, tr, L), lambda i, ids: (ids[0], i, 0)),
                      pl.BlockSpec((3, tr, L), lambda i, ids: (0, i, 0))],
            out_specs=pl.BlockSpec((None, tr, L), lambda i, ids: (ids[1], i, 0))),
        compiler_params=_params(32, 1),
    )(ids, *_hbm(pair, parts))


def _mesh_ids():
    x, y, c = _mesh_pos()
    return jnp.stack([2 * x + y, c]).astype(jnp.int32)


def _place_shard(rows, ids, *, name):
    R, L = rows.shape
    tr = 128

    def body(ids_ref, in_ref, out_ref):
        out_ref[...] = in_ref[...].astype(BF16)

    return pl.pallas_call(
        body, name=name, out_shape=jax.ShapeDtypeStruct((4, R, L), BF16),
        grid_spec=pltpu.PrefetchScalarGridSpec(
            num_scalar_prefetch=1, grid=(R // tr,), in_specs=[pl.BlockSpec((tr, L), lambda i, ids: (i, 0))],
            out_specs=pl.BlockSpec((None, tr, L), lambda i, ids: (ids[0], i, 0))),
        compiler_params=_params(16, 1),
    )(ids, *_hbm(rows))


IN_E_SHARD, IN_E_LAYOUT = 392, 1664


def _in_e_runs():
    runs = []
    for lo, hi, dst in ((0, 512, 0), (512, 544, 1536), (544, 1568, 512)):
        while lo < hi:
            j = lo // IN_E_SHARD
            wd = min(hi, IN_E_SHARD * (j + 1)) - lo
            runs.append((j, lo - IN_E_SHARD * j, dst, wd))
            lo, dst = lo + wd, dst + wd
    return runs


def _in_e_to_layout(shards):
    tr = 256

    def body(s_ref, o_ref, t_ref):
        t_ref[...] = jnp.zeros_like(t_ref)
        for j in range(4):
            s = s_ref[j].astype(F32)
            for _, c, dst, wd in (r for r in _in_e_runs() if r[0] == j):
                t_ref[:, dst:dst + wd] = s[:, c:c + wd]
        o_ref[...] = t_ref[...].astype(BF16)

    return pl.pallas_call(
        body, name="in_e_to_layout", out_shape=jax.ShapeDtypeStruct((D, IN_E_LAYOUT), BF16), grid=(D // tr,),
        in_specs=[pl.BlockSpec((4, tr, IN_E_SHARD), lambda i: (0, i, 0))],
        out_specs=pl.BlockSpec((tr, IN_E_LAYOUT), lambda i: (i, 0)),
        scratch_shapes=[pltpu.VMEM((tr, IN_E_LAYOUT), F32)], compiler_params=_params(16, 1),
    )(*_hbm(shards))


def _in_e_from_layout(g):
    tr = 256

    def body(g_ref, o_ref, t_ref):
        g_ = g_ref[...]
        for j, c, src, wd in _in_e_runs():
            t_ref[j, :, c:c + wd] = g_[:, src:src + wd]
        o_ref[...] = t_ref[...].astype(BF16)

    return pl.pallas_call(
        body, name="in_e_from_layout", out_shape=jax.ShapeDtypeStruct((4, D, IN_E_SHARD), BF16), grid=(D // tr,),
        in_specs=[pl.BlockSpec((tr, IN_E_LAYOUT), lambda i: (i, 0))],
        out_specs=pl.BlockSpec((4, tr, IN_E_SHARD), lambda i: (0, i, 0)),
        scratch_shapes=[pltpu.VMEM((4, tr, IN_E_SHARD), F32)], compiler_params=_params(16, 1),
    )(*_hbm(g))


def _place_weights(pieces, buffer_rows, ids, *, plan=None):
    tr = 256
    steps, s = [], 0
    for arr, layer, buf, row0 in pieces:
        nblk = arr.shape[1] // tr
        steps.append((s, nblk))
        s += nblk
    total = s
    buf_start = [min(st for (st, _), p in zip(steps, pieces) if p[2] == k) for k in range(len(buffer_rows))]
    grid = (total,)
    n_in = len(pieces)

    def body(*refs):
        ins, outs, _, pctx = _split_refs(refs[1:], n_in, len(buffer_rows), 0, plan)
        _plan_start(plan, pctx, grid)
        i = pl.program_id(0)
        for (st, nblk), (_, _, buf, _), ref in zip(steps, pieces, ins):
            @pl.when((i >= st) & (i < st + nblk))
            def _(ref=ref, buf=buf):
                outs[buf][...] = ref[...].astype(BF16)
        _plan_wait(plan, pctx, grid)

    in_specs = [pl.BlockSpec((None, tr, D), lambda i, ids, layer=layer, st=st, nblk=nblk:
                             (layer, jnp.clip(i - st, 0, nblk - 1), 0))
                for (st, nblk), (_, layer, _, _) in zip(steps, pieces)]
    out_specs = [pl.BlockSpec((None, tr, D), lambda i, ids, st=st, nb=rows // tr: (ids[0], jnp.clip(i - st, 0, nb - 1), 0))
                 for st, rows in zip(buf_start, buffer_rows)]
    p_in, p_ospec, p_oshape, p_scr, p_alias = _plan_io(plan, 1 + n_in, len(buffer_rows))
    return pl.pallas_call(
        body, name="place_weights",
        out_shape=[jax.ShapeDtypeStruct((4, rows, D), BF16) for rows in buffer_rows] + p_oshape,
        grid_spec=pltpu.PrefetchScalarGridSpec(
            num_scalar_prefetch=1, grid=grid, in_specs=in_specs + [_ANY] * len(p_in), out_specs=out_specs + p_ospec,
            scratch_shapes=p_scr),
        input_output_aliases=p_alias, compiler_params=_params(16, 1),
    )(ids, *_hbm(*[p[0] for p in pieces]), *p_in)


def _remote(src, dst, send_sem, recv_sem, to):
    return pltpu.make_async_remote_copy(src_ref=src, dst_ref=dst, send_sem=send_sem, recv_sem=recv_sem,
                                        device_id=to, device_id_type=MESH_IDS)


def _rows(ref, lead, start, size):
    return ref.at[tuple(pl.ds(0, n) for n in ref.shape[:lead]) + (pl.ds(start, size),)]


def _other_chips():
    x, y, _ = _mesh_pos()
    return [(1 - x, y), (x, 1 - y), (1 - x, 1 - y)]


def _plan_gather_ici(bufs):
    n = len(bufs)

    def copies(outs, send, recv):
        x, y, c = _mesh_pos()
        res = []
        for b in range(n):
            half = bufs[b].shape[1] // 2
            mine = _rows(outs[b].at[2 * x + y], 0, c * half, half)
            for j, (cx, cy) in enumerate(_other_chips()):
                res.append((_remote(mine, mine, send(3 * b + j), recv(3 * b + j), (cx, cy, c)),
                            _remote(mine, _rows(outs[b].at[2 * cx + cy], 0, c * half, half),
                                    send(3 * b + j), recv(3 * b + j), (x, y, c))))
        return res

    def start(ins, outs, send, recv, loc):
        for out_cp, _ in copies(outs, send, recv):
            out_cp.start()

    def wait(ins, outs, send, recv, loc):
        for out_cp, in_cp in copies(outs, send, recv):
            in_cp.wait_recv()
            out_cp.wait_send()

    outs = [jax.ShapeDtypeStruct(b.shape, b.dtype) for b in bufs]
    return _Plan(bufs, outs, 3 * n, 0, start, wait, aliases={b: b for b in range(n)})


def _plan_gather_forward(bufs):
    n = len(bufs)

    def copies(outs, send, recv):
        x, y, c = _mesh_pos()
        res = []
        for b in range(n):
            half = bufs[b].shape[1] // 2
            for j, (cx, cy) in enumerate(_other_chips()):
                slot = outs[b].at[2 * cx + cy]
                res.append((_remote(_rows(slot, 0, c * half, half), _rows(slot, 0, c * half, half),
                                    send(3 * b + j), recv(3 * b + j), (x, y, 1 - c)),
                            _remote(_rows(slot, 0, c * half, half), _rows(slot, 0, (1 - c) * half, half),
                                    send(3 * b + j), recv(3 * b + j), (x, y, c))))
        return res

    def start(ins, outs, send, recv, loc):
        for out_cp, _ in copies(outs, send, recv):
            out_cp.start()

    def wait(ins, outs, send, recv, loc):
        for out_cp, in_cp in copies(outs, send, recv):
            in_cp.wait_recv()
            out_cp.wait_send()

    outs = [jax.ShapeDtypeStruct(b.shape, b.dtype) for b in bufs]
    return _Plan(bufs, outs, 3 * n, 0, start, wait, aliases={b: b for b in range(n)})


def _plan_pair_swap(g):
    half = g.shape[1] // 2

    def copy(ins, outs, send, recv, loc):
        x, y, c = _mesh_pos()
        return _remote(_rows(ins[0], 1, (1 - c) * half, half), outs[0], send(0), recv(0), (x, y, 1 - c))

    return _Plan([g], [jax.ShapeDtypeStruct((4, half, g.shape[2]), g.dtype)], 1, 0,
                 lambda *a: copy(*a).start(), lambda *a: copy(*a).wait())


def _plan_pair_gather(buf):
    def copies(ins, outs, send, recv, loc):
        x, y, c = _mesh_pos()
        return (_remote(outs[0].at[c], outs[0].at[c], send(0), recv(0), (x, y, 1 - c)),
                _remote(outs[0].at[c], outs[0].at[1 - c], send(0), recv(0), (x, y, c)))

    def wait(*a):
        out_cp, in_cp = copies(*a)
        in_cp.wait_recv()
        out_cp.wait_send()

    return _Plan([buf], [jax.ShapeDtypeStruct(buf.shape, buf.dtype)], 1, 0, lambda *a: copies(*a)[0].start(), wait,
                 aliases={0: 0})


def _plan_chip_scatter(p):
    def copies(ins, outs, send, recv, loc):
        _, _, c = _mesh_pos()
        return [_remote(ins[0].at[2 * cx + cy], outs[0].at[j], send(j), recv(j), (cx, cy, c))
                for j, (cx, cy) in enumerate(_other_chips())]

    def start(*a):
        for cp in copies(*a):
            cp.start()

    def wait(*a):
        for cp in copies(*a):
            cp.wait()

    return _Plan([p], [jax.ShapeDtypeStruct((3,) + p.shape[1:], p.dtype)], 3, 0, start, wait)


def _plan_exchange_all(vec):
    def copies(ins, outs, send, recv, loc):
        x, y, c = _mesh_pos()
        return [_remote(ins[0], outs[0].at[r - 1], send(r - 1), recv(r - 1), (x ^ (r >> 2), y ^ ((r >> 1) & 1), c ^ (r & 1)))
                for r in range(1, 8)]

    def start(*a):
        for cp in copies(*a):
            cp.start()

    def wait(*a):
        for cp in copies(*a):
            cp.wait()

    return _Plan([vec], [jax.ShapeDtypeStruct((7,) + vec.shape, vec.dtype)], 7, 0, start, wait)


SMALL_LAYOUT = {
    "mla_gq": (0, 1, 256, (1, 256)), "mla_gkv": (1, 1, 256, (1, 256)), "sgu_ln_g": (2, 1, 512, (1, 512)),
    "sgu_ln_b": (3, 1, 512, (1, 512)), "sgu_w": (4, 64, 1024, (64, 1024)), "sgu_b": (68, 1, 512, (1, 512)),
    "hg_lb": (69, 2, 1024, (2, 1024)), "hg_gnorm": (71, 1, 1024, (1, 256)), "ln1_g": (72, 2, 1024, (2, 1024)),
    "ln1_b": (74, 2, 1024, (2, 1024)), "ln2_g": (76, 2, 1024, (2, 1024)), "ln2_b": (78, 2, 1024, (2, 1024)),
}


def _small_pack(dgq, dgkv, dslg, dslb, dsw, dsb, dlb, dgn, ln_parts, sq_err):
    flat_ln = [p for pair in ln_parts for p in pair]

    def body(*refs):
        gq_ref, gkv_ref, slg_ref, slb_ref, sw_ref, sb_ref, lb_ref, gn_ref = refs[:8]
        ln_refs, err_ref, out_ref, t_sc = refs[8:16], refs[16], refs[17], refs[18]
        s8 = lambda ref: jnp.sum(ref[...], axis=0, keepdims=True)
        out_ref[...] = jnp.zeros_like(out_ref)
        out_ref[0:1, 0:256] = s8(gq_ref)
        out_ref[1:2, 0:256] = s8(gkv_ref)
        out_ref[2:3, 0:512] = s8(slg_ref)
        out_ref[3:4, 0:512] = s8(slb_ref)
        out_ref[4:68, :] = sw_ref[...]
        t_sc[...] = sb_ref[...].T
        for g in range(SGU_G):
            out_ref[68:69, g * SGU_C:(g + 1) * SGU_C] = t_sc[g:g + 1, :]
        d_lb1 = s8(lb_ref)
        out_ref[69:70, :] = -d_lb1
        out_ref[70:71, :] = d_lb1
        out_ref[71:72, :] = s8(gn_ref)
        for k, ref in enumerate(ln_refs):
            out_ref[72 + k:73 + k, :] = s8(ref)
        out_ref[0:1, 1023:1024] = jnp.sum(s8(err_ref), axis=1, keepdims=True) * (0.5 / D)

    vm = pl.BlockSpec(memory_space=pltpu.VMEM)
    return pl.pallas_call(
        body, name="small_grad_pack", in_specs=[vm] * 17, out_specs=vm,
        out_shape=jax.ShapeDtypeStruct((SMALL_ROWS, 1024), F32), scratch_shapes=[pltpu.VMEM((SGU_C, SGU_C), F32)],
        compiler_params=_params(16),
    )(dgq, dgkv, dslg, dslb, dsw.reshape(64, 1024), dsb, dlb, dgn, *flat_ln, sq_err)


def _small_update(vec, others, ids, w, m, v):
    names = list(SMALL_LAYOUT)
    n = len(names)
    c1, c2 = 1.0 - B1 ** STEP, 1.0 - B2 ** STEP
    have_others = others is not None

    def body(*refs):
        ids_ref, v_ref = refs[0], refs[1]
        k = 2 + have_others
        w_refs, m_refs, v_refs = refs[k:k + n], refs[k + n:k + 2 * n], refs[k + 2 * n:k + 3 * n]
        outs = refs[k + 3 * n:]
        row0_ref, tot_sc = outs[0], outs[-1]
        total = v_ref[...]
        if have_others:
            me = 2 * ids_ref[0] + ids_ref[1]
            total = None
            for d in range(8):
                rel = d ^ me
                term = jnp.where(rel == 0, v_ref[...], refs[2][jnp.maximum(rel - 1, 0)])
                total = term if total is None else total + term
        tot_sc[...] = total
        row0_ref[...] = tot_sc[0:1, :]
        for i, name in enumerate(names):
            r0, nr, width, _ = SMALL_LAYOUT[name]
            if name == "hg_gnorm":
                g_ = tot_sc[r0:r0 + 1, 0:256]
                for chip in range(1, 4):
                    g_ = jnp.where(ids_ref[0] == chip, tot_sc[r0:r0 + 1, chip * 256:(chip + 1) * 256], g_)
            else:
                g_ = tot_sc[r0:r0 + nr, 0:width]
            m_ = B1 * m_refs[i][...] + (1.0 - B1) * g_
            v_ = B2 * v_refs[i][...] + (1.0 - B2) * (g_ * g_)
            go, do, mo, vo = outs[1 + 4 * i:5 + 4 * i]
            go[...] = g_
            do[...] = -LR * ((m_ / c1) / (jnp.sqrt(v_ / c2) + ADAM_EPS) + WD * w_refs[i][...])
            mo[...] = m_
            vo[...] = v_

    full = lambda shape: pl.BlockSpec(shape, lambda i, ids, nd=len(shape): (0,) * nd)
    kshapes = [SMALL_LAYOUT[name][3] for name in names]
    operands = [vec] + ([others] if have_others else []) + [d[name] for d in (w, m, v) for name in names]
    out_shapes = [jax.ShapeDtypeStruct((1, 1024), F32)] + [jax.ShapeDtypeStruct(s, F32) for s in kshapes for _ in range(4)]
    res = pl.pallas_call(
        body, name="small_update", out_shape=out_shapes,
        grid_spec=pltpu.PrefetchScalarGridSpec(
            num_scalar_prefetch=1, grid=(1,), in_specs=[full(o.shape) for o in operands],
            out_specs=[full(s.shape) for s in out_shapes],
            scratch_shapes=[pltpu.VMEM((SMALL_ROWS, 1024), F32)]),
        compiler_params=_params(32, 1),
    )(ids, *operands)
    return res[0], {name: tuple(res[1 + 4 * i:5 + 4 * i]) for i, name in enumerate(names)}


ROWS_L1, ROWS_L0, ROWS_ODD_W = 3328, 2048, 384
ODD_PARTS = (("w_out_e", (256, 1024)), ("w_in_e", (1024, 392)), ("w_qb", (256, 192)), ("w_kvb", (256, 256)))
ODD_W_PARTS = tuple(p for p in ODD_PARTS if p[0] != "w_in_e")


def _odd_rows(parts, dtype, layout, total, gnorm=None):
    rows = [parts[n].reshape(-1, 1024).astype(dtype) for n, _ in layout]
    used = sum(r.shape[0] for r in rows)
    if gnorm is not None:
        bits = lax.bitcast_convert_type(gnorm.reshape(-1), BF16).reshape(1, 512)
        rows.append(jnp.pad(bits, ((0, 15), (0, 512))))
        used += 16
    if total > used:
        rows.append(jnp.zeros((total - used, 1024), dtype))
    return jnp.concatenate(rows, axis=0)


def _odd_unrows(buf, layout, with_gnorm=False):
    out, off = {}, 0
    for n, shape in layout:
        nr = math.prod(shape) // 1024
        out[n] = buf[off:off + nr].reshape(shape)
        off += nr
    if with_gnorm:
        out["hg_gnorm"] = lax.bitcast_convert_type(buf[off, :512].reshape(256, 2), F32).reshape(1, 256)
    return out


def _rope_tables(positions):
    half = ROPE // 2
    inv_freq = ROPE_BASE ** (-jnp.arange(half, dtype=F32) / half)
    per_row = 128 // half
    ang = jnp.repeat(positions.astype(F32).reshape(-1, per_row), half, axis=1) * jnp.tile(inv_freq, per_row)
    cos, sin = jnp.cos(ang).reshape(-1, half), jnp.sin(ang).reshape(-1, half)
    T = cos.shape[0]
    one, z16, z32 = jnp.ones((T, NOPE), F32), jnp.zeros((T, half), F32), jnp.zeros((T, 32), F32)
    z64 = jnp.zeros((T, NOPE), F32)
    c = jnp.concatenate([one, cos, cos, z32], axis=1)
    s1 = jnp.concatenate([z64, -sin, z16, z32], axis=1)
    s2 = jnp.concatenate([z64, z16, sin, z32], axis=1)
    return c, s1, s2


def _local_step(x, positions, tgt, odd, bufs, P, exchange):
    T = x.shape[0]
    row = lambda a: a.reshape(1, -1)
    rc, rs1, rs2 = _rope_tables(positions)
    blk = lambda f: pl.BlockSpec((None, D, D), f)

    w_in = _in_e_to_layout(odd["w_in_e"])
    wq = jnp.pad(odd["w_qb"].reshape(256, HEADS, NOPE + ROPE), ((0, 0), (0, 0), (0, 32))).reshape(256, HEADS * 128)
    kvb = odd["w_kvb"].reshape(256, HEADS, NOPE + VDIM)
    wk = jnp.pad(kvb[:, :, :NOPE], ((0, 0), (0, 0), (0, 64))).reshape(256, HEADS * 128)
    wv = kvb[:, :, NOPE:].reshape(256, HEADS * VDIM)
    w_out_e = odd["w_out_e"]
    sgu_w = P["sgu_w"][0]
    sgu_bt = P["sgu_b"][0].T
    gq, gkv = P["mla_gq"], P["mla_gkv"]
    gnorm = P["hg_gnorm"]

    z0 = _matmul(x, w_in, name="in_proj_e", M=T, N=1664, K=D, tn=1664)[0]
    q, k, v = _mla_prep(z0, gq, gkv, wq, wk, wv, rc, rs1, rs2)
    if exchange:
        ids = _mesh_ids()
        placed = list(bufs)
        a_out, lse, wga, wgb = _flash_fwd(q, k, v, plan=_plan_gather_ici(placed[:2]))
    else:
        a_out, lse = _flash_fwd(q, k, v)
        wga, wgb, wgc = bufs
    mix0 = _sgu_fwd(z0, a_out, P["sgu_ln_g"], P["sgu_ln_b"], sgu_w, sgu_bt)
    res = _proj_ln(mix0, w_out_e, x, row(P["ln1_g"][0]), row(P["ln1_b"][0]), name="out_proj_ln_e",
                   plan=_plan_gather_forward([wga, wgb]) if exchange else None)
    r1, h1b = res[:2]
    if exchange:
        wga, wgb = res[2:]
    ln = lambda name, l: (row(P[name + "_g"][l]), row(P[name + "_b"][l]))
    res = _ffn_ln(h1b, wga, r1, *ln("ln2", 0), name="ffn_ln_0", prev_ln=ln("ln1", 0),
                  plan=_plan_gather_ici(placed[2:]) if exchange else None)
    ra0, r2, h2b = res[:3]
    z4 = _matmul(h2b, wgb, name="in_proj_o", M=T, N=4 * D, K=D, tn=2 * D, n_slots=True,
                 b_spec=pl.BlockSpec((2, D, D), lambda i, j, k: (j, 0, 0)),
                 out_shape=jax.ShapeDtypeStruct((4, T, D), F32),
                 o_spec=pl.BlockSpec((2, min(MM_ROWS, T), D), lambda i, j, k: (j, i, 0)))[0]
    y1, o_raw, states = _hgrn_fwd(z4, P["hg_lb"], gnorm)
    res2 = _proj_ln(y1, wgb, r2, *ln("ln1", 1), name="out_proj_ln_o", prev_ln=ln("ln2", 0), w_rowblk=4,
                    plan=_plan_gather_forward([res[3]]) if exchange else None)
    r3, h3b = res2[:2]
    if exchange:
        wgc = res2[2]
    ra1, r4, _ = _ffn_ln(h3b, wgc, r3, *ln("ln2", 1), name="ffn_ln_1", prev_ln=ln("ln1", 1))

    ln1_g, ln1_b, ln2_g, ln2_b = [None, None], [None, None], [None, None], [None, None]
    sq_err_parts = []

    def ffn_bwd(l, dh, r_out, ra, h_mid_b, g2, wg, rows, plan=None, loss_head=()):
        dr, dr_b, dg, db, *sq_err = _ln_bwd(dh, r_out, row(g2), name=f"ln2_bwd_{l}", loss_head=loss_head)
        sq_err_parts.extend(sq_err)
        ln2_g[l], ln2_b[l] = dg, db
        da, *extra = _matmul(dr_b, wg, tb=True, mul=ra, out_dtype=BF16, name=f"ffn_da_{l}", M=T, N=4 * D, K=D, tn=2 * D,
                             b_spec=pl.BlockSpec((2, D, D), lambda i, j, k: (j, 1, 0)), n_slots=True, plan=plan)
        gbuf = _matmul(ra, dr_b, ta=True, a_sq=True, name=f"ffn_dw2_{l}", M=4 * D, N=D, K=T, tm=1024, tk=DW_TOKENS // 2,
                       out_shape=jax.ShapeDtypeStruct((4, rows, D), BF16), o_spec=blk(lambda i, j, k: (i, 1, 0)))[0]
        gbuf = _matmul(h_mid_b, da, ta=True, name=f"ffn_dw1_{l}", M=D, N=4 * D, K=T, tm=1024, tk=DW_TOKENS, into=gbuf,
                       out_shape=jax.ShapeDtypeStruct((4, rows, D), BF16), o_spec=blk(lambda i, j, k: (j, 0, 0)))[0]
        dh_mid = _matmul(da, wg, tb=True, add=dr, add_scale=ALPHA, name=f"ffn_dh_{l}", M=T, N=D, K=4 * D, tk=2 * D,
                         b_spec=pl.BlockSpec((2, D, D), lambda i, j, k: (k, 0, 0)))[0]
        return dh_mid, gbuf, extra

    dh3, g1, _ = ffn_bwd(1, None, r4, ra1, h3b, P["ln2_g"][1], wgc, ROWS_L1, loss_head=(row(P["ln2_b"][1]), tgt))
    loss_parts = sq_err_parts[0]
    dr3, dr3_b, dg, db = _ln_bwd(dh3, r3, row(P["ln1_g"][1]), name="ln1_bwd_1")
    ln1_g[1], ln1_b[1] = dg, db
    g1_sds = jax.ShapeDtypeStruct((4, ROWS_L1, D), BF16)
    g1 = _matmul(y1, dr3_b, ta=True, name="dw_out_o", M=D, N=D, K=T, tm=256, tk=DW_TOKENS, into=g1, out_shape=g1_sds,
                 o_spec=pl.BlockSpec((None, 256, D), lambda i, j, k: (i, 12, 0)))[0]
    dmix1 = _matmul(dr3_b, wgb, tb=True, name="dmix_o", M=T, N=D, K=D, b_spec=_rows4_spec(4, 3), b_merge=(D, D))[0]
    dz4, dlb, dgn = _hgrn_bwd(z4, o_raw, dmix1, states, P["hg_lb"], gnorm)
    g1 = _matmul(h2b, dz4, ta=True, name="dw_in_o", M=D, N=4 * D, K=T, tm=1024, tk=DW_TOKENS, into=g1, out_shape=g1_sds,
                 b_spec=pl.BlockSpec((None, min(DW_TOKENS, T), D), lambda i, j, k: (j, k, 0)),
                 o_spec=blk(lambda i, j, k: (j, 2, 0)))[0]
    dh2 = _matmul(dz4, wgb, tb=True, add=dr3, add_scale=ALPHA, name="dh_in_o", M=T, N=D, K=4 * D, tk=2 * D,
                  a_spec=pl.BlockSpec((2, min(MM_ROWS, T), D), lambda i, j, k: (k, i, 0)),
                  b_spec=pl.BlockSpec((2, D, D), lambda i, j, k: (k, 0, 0)))[0]

    dh1, g0, swapped1 = ffn_bwd(0, dh2, r2, ra0, h1b, P["ln2_g"][0], wga, ROWS_L0,
                                plan=_plan_pair_swap(g1) if exchange else None)
    dr1, dr1_b, dg, db = _ln_bwd(dh1, r1, row(P["ln1_g"][0]), name="ln1_bwd_0")
    ln1_g[0], ln1_b[0] = dg, db
    godd = {"w_out_e": _matmul(mix0, dr1_b, ta=True, name="dw_out_e", M=D, N=D, K=T, tm=1024, tk=DW_TOKENS)[0]}
    dmix0, *swapped0 = _matmul(dr1_b, w_out_e, tb=True, name="dmix_e", M=T, N=D, K=D,
                               plan=_plan_pair_swap(g0) if exchange else None)
    delta, do_b = _attn_delta(dmix0, a_out)
    if exchange:
        pair1 = _add_pairs(g1, swapped1[0], ids, name="grad_pair_add_1")
        pair0 = _add_pairs(g0, swapped0[0], ids, name="grad_pair_add_0")
        dq4, dk, dv, parts0, parts1 = _flash_bwd(
            q, k, v, do_b, lse, delta, plan=_join_plans([_plan_chip_scatter(pair0), _plan_chip_scatter(pair1)]))
        half0 = _sum_chips(pair0, parts0, ids, name="grad_chip_sum_0")
        half1 = _sum_chips(pair1, parts1, ids, name="grad_chip_sum_1")
        dc, dkr, dwq, dwk, dwv, dgq, dgkv, g0, g1 = _mla_bwd(
            z0, dq4, dk, dv, gq, gkv, wq, wk, wv, rc, rs1, rs2,
            plan=_join_plans([_plan_pair_gather(half0), _plan_pair_gather(half1)]))
        g0, g1 = g0.reshape(ROWS_L0, D), g1.reshape(ROWS_L1, D)
    else:
        dq4, dk, dv = _flash_bwd(q, k, v, do_b, lse, delta)
        dc, dkr, dwq, dwk, dwv, dgq, dgkv = _mla_bwd(z0, dq4, dk, dv, gq, gkv, wq, wk, wv, rc, rs1, rs2)
    godd["w_qb"] = dwq.reshape(256, HEADS, 128)[:, :, :NOPE + ROPE].reshape(256, HEADS * (NOPE + ROPE))
    godd["w_kvb"] = jnp.concatenate([dwk.reshape(256, HEADS, 128)[:, :, :NOPE], dwv.reshape(256, HEADS, VDIM)],
                                    axis=2).reshape(256, HEADS * (NOPE + VDIM))
    swap_b = None
    if exchange:
        by_chip = [_odd_rows({"w_out_e": jnp.split(godd["w_out_e"], 4, axis=0)[j],
                              **{n: jnp.split(godd[n], 4, axis=1)[j] for n in ("w_qb", "w_kvb")}}, BF16,
                             ODD_W_PARTS, ROWS_ODD_W)
                   for j in range(4)]
        odd_b = jnp.stack(by_chip)
        swap_b = _plan_pair_swap(odd_b)
    dz0, dsw, dsb, dslg, dslb, *theirs_b = _sgu_bwd(z0, dmix0, dc, dkr, P["sgu_ln_g"], P["sgu_ln_b"], sgu_w, sgu_bt,
                                                    plan=swap_b)
    small_vec = _small_pack(dgq, dgkv, dslg, dslb, dsw, dsb, dlb, dgn, [ln1_g, ln1_b, ln2_g, ln2_b], loss_parts)
    plan_in = None
    if exchange:
        pair_b = _add_pairs(odd_b, theirs_b[0], ids, name="odd_pair_add_1")
        plan_in = _join_plans([_plan_exchange_all(small_vec), _plan_chip_scatter(pair_b)])
    dw_in, *carried = _matmul(x, dz0, ta=True, name="dw_in_e", M=D, N=1664, K=T, tm=1024, tn=1664, tk=DW_TOKENS // 4,
                              plan=plan_in)
    odd_a = godd["w_in_e"] = _in_e_from_layout(dw_in)
    plan_x = None
    if exchange:
        small_others, parts_b = carried
        theirs_a = _run_plan(_plan_pair_swap(odd_a), name="odd_pair_swap")[0]
        pair_a = _add_pairs(odd_a, theirs_a, ids, name="odd_pair_add_0")
        plan_x = _plan_chip_scatter(pair_a)
    grad_x, *parts_a = _matmul(dz0, w_in, tb=True, add=dr1, add_scale=ALPHA, name="dx", M=T, N=D, K=1664, tk=1664,
                               plan=plan_x)
    if exchange:
        godd = ([pair_a, pair_b], [parts_a[0], parts_b])
        return grad_x, g0, g1, godd, small_vec, small_others
    return grad_x, g0, g1, godd, small_vec, None


WEIGHTS = ['w_in_e', 'mla_gq', 'mla_gkv', 'w_qb', 'w_kvb', 'sgu_ln_g', 'sgu_ln_b', 'sgu_w', 'sgu_b', 'w_out_e',
           'w_in_o', 'hg_lb', 'hg_gnorm', 'w_out_o', 'ln1_g', 'ln1_b', 'w_ff1', 'w_ff2', 'ln2_g', 'ln2_b']


def kernel(x, positions, w_in_e, mla_gq, mla_gkv, w_qb, w_kvb, sgu_ln_g, sgu_ln_b, sgu_w, sgu_b, w_out_e, w_in_o, hg_lb, hg_gnorm, w_out_o, ln1_g, ln1_b, w_ff1, w_ff2, ln2_g, ln2_b, loss_target, m_w_in_e, m_mla_gq, m_mla_gkv, m_w_qb, m_w_kvb, m_sgu_ln_g, m_sgu_ln_b, m_sgu_w, m_sgu_b, m_w_out_e, m_w_in_o, m_hg_lb, m_hg_gnorm, m_w_out_o, m_ln1_g, m_ln1_b, m_w_ff1, m_w_ff2, m_ln2_g, m_ln2_b, v_w_in_e, v_mla_gq, v_mla_gkv, v_w_qb, v_w_kvb, v_sgu_ln_g, v_sgu_ln_b, v_sgu_w, v_sgu_b, v_w_out_e, v_w_in_o, v_hg_lb, v_hg_gnorm, v_w_out_o, v_ln1_g, v_ln1_b, v_w_ff1, v_w_ff2, v_ln2_g, v_ln2_b):
    args = dict(locals())
    w = {n: args[n] for n in WEIGHTS}
    m = {n: args["m_" + n] for n in WEIGHTS}
    v = {n: args["v_" + n] for n in WEIGHTS}
    cx, cy, cc = _mesh_pos()
    chip = 2 * cx + cy

    odd_shard = _odd_rows({"w_out_e": w_out_e[0], "w_qb": w_qb[0], "w_kvb": w_kvb[0]}, BF16, ODD_W_PARTS, ROWS_ODD_W,
                          gnorm=hg_gnorm)
    ids = _mesh_ids()
    placed = [_place_shard(w_in_e[0], ids, name="place_shard_in_e"), _place_shard(odd_shard, ids, name="place_shard_odd")]
    pieces = [(w_ff1, 0, 0, 0), (w_ff2, 0, 0, 1024), (w_in_o, 0, 1, 0), (w_out_o, 0, 1, 1024),
              (w_ff1, 1, 2, 0), (w_ff2, 1, 2, 1024)]
    *big_bufs, odd_a, odd_b = _place_weights(pieces, (2048, 1280, 2048), ids, plan=_plan_gather_ici(placed))
    gathered = _run_plan(_plan_gather_forward([odd_a, odd_b]), name="odd_gather_forward")
    per_chip = [_odd_unrows(gathered[1][j], ODD_W_PARTS, with_gnorm=True) for j in range(4)]
    odd = {"w_out_e": jnp.concatenate([p["w_out_e"] for p in per_chip], axis=0),
           "w_in_e": gathered[0]}
    for n in ("w_qb", "w_kvb"):
        odd[n] = jnp.concatenate([p[n] for p in per_chip], axis=1)
    small = {n: w[n] for n in SMALL_LAYOUT if n != "hg_gnorm"}
    small["hg_gnorm"] = jnp.concatenate([p["hg_gnorm"] for p in per_chip], axis=1)
    grad_x, g_l0, g_l1, godd, small_vec, small_others = _local_step(
        x[0], positions[0], loss_target[0], odd, big_bufs, small, True)

    sums = [_sum_chips(pair, parts, ids, name=f"odd_chip_sum_{k}") for k, (pair, parts) in enumerate(zip(*godd))]
    g_in_e, g_rest = _run_plan(_join_plans([_plan_pair_gather(s) for s in sums]), name="odd_pair_gather")
    g_odd = _odd_unrows(g_rest.reshape(ROWS_ODD_W, 1024), ODD_W_PARTS)
    g_odd["w_in_e"] = g_in_e.reshape(D, 392)

    to_kernel = lambda d: {n: d[n].reshape(SMALL_LAYOUT[n][3]) for n in SMALL_LAYOUT}
    first_row, small_out = _small_update(small_vec, small_others, ids, to_kernel(w), to_kernel(m), to_kernel(v))
    loss = first_row[0, 1023]
    grads, delta, new_m, new_v = {}, {}, {}, {}
    for n, res in small_out.items():
        grads[n], delta[n], new_m[n], new_v[n] = (r.reshape(w[n].shape) for r in res)

    for n, bufs_, row0 in (("w_ff1", [g_l0, g_l1], 0), ("w_ff2", [g_l0, g_l1], 1024), ("w_in_o", [g_l1], 2048),
                           ("w_out_o", [g_l1], 3072)):
        grads[n], delta[n], new_m[n], new_v[n] = _adamw_rows(w[n], m[n], v[n], bufs_, row0, name=f"adamw_{n}")
    for n, _ in ODD_PARTS:
        if n == "w_in_e":
            res = _adamw(w[n][0].T, g_odd[n].T, m[n][0].T, v[n][0].T, name=f"adamw_{n}", with_g=True)
            grads[n], delta[n], new_m[n], new_v[n] = (r.T[None] for r in res)
            continue
        grads[n] = g_odd[n][None]
        d_, m_, v_ = _adamw(w[n][0], g_odd[n], m[n][0], v[n][0], name=f"adamw_{n}")
        delta[n], new_m[n], new_v[n] = d_[None], m_[None], v_[None]

    return (loss, grad_x[None], *[grads[n] for n in WEIGHTS], *[delta[n] for n in WEIGHTS],
            *[new_m[n] for n in WEIGHTS], *[new_v[n] for n in WEIGHTS])
```

```python
import math

import jax
import jax.numpy as jnp
from jax import lax
from jax.experimental import pallas as pl
from jax.experimental.pallas import tpu as pltpu

F32 = jnp.float32
BF16 = jnp.bfloat16
MESH_IDS = pl.DeviceIdType.MESH

D = 1024
DEPTH = 2
HEADS = 8
NOPE, ROPE, VDIM = 64, 32, 64
QK_SCALE = (NOPE + ROPE) ** -0.5
ROPE_BASE = 10000.0
SGU_G, SGU_C = 4, 128
HG_CHUNK = 64
HG_HEADS_PER_STEP = 8
ALPHA = (2 * DEPTH) ** 0.25
EPS = 1e-5
LR, B1, B2, ADAM_EPS, WD, STEP = 0.001, 0.9, 0.999, 1e-08, 0.01, 10
GELU_C = math.sqrt(2.0 / math.pi)
GELU_A = 0.044715
MB = 1024 * 1024
ROW_BLOCK = 512
SMALL_ROWS = 80

NT_DIMS = (((1,), (1,)), ((), ()))
TN_DIMS = (((0,), (0,)), ((), ()))


def _params(vmem_mb, n_axes=0):
    kw = dict(vmem_limit_bytes=vmem_mb * MB)
    if n_axes:
        kw["dimension_semantics"] = ("arbitrary",) * n_axes
    return pltpu.CompilerParams(**kw)


_ANY = pl.BlockSpec(memory_space=pltpu.HBM)


def _mesh_pos():
    return lax.axis_index("x"), lax.axis_index("y"), lax.axis_index("c")


def _hbm(*arrays):
    return tuple(pltpu.with_memory_space_constraint(a, pltpu.HBM) if a.size >= 2 ** 18 else a for a in arrays)


class _Plan:
    def __init__(self, ins, outs, n_remote, n_local, start, wait, aliases=None):
        self.ins, self.outs, self.n_remote, self.n_local = list(ins), list(outs), n_remote, n_local
        self.start, self.wait, self.aliases = start, wait, dict(aliases or {})


def _join_plans(plans):
    ins, outs, aliases, parts = [], [], {}, []
    nr = nl = 0
    for p in plans:
        parts.append((p, len(ins), len(outs), nr, nl))
        aliases.update({len(ins) + i: len(outs) + o for i, o in p.aliases.items()})
        ins += p.ins
        outs += p.outs
        nr += p.n_remote
        nl += p.n_local

    def run(which):
        def go(in_refs, out_refs, send, recv, loc):
            for p, i0, o0, r0, l0 in parts:
                getattr(p, which)(in_refs[i0:i0 + len(p.ins)], out_refs[o0:o0 + len(p.outs)],
                                  lambda i, r0=r0: send(r0 + i), lambda i, r0=r0: recv(r0 + i),
                                  lambda i, l0=l0: loc(l0 + i))
        return go

    return _Plan(ins, outs, nr, nl, run("start"), run("wait"), aliases)


def _plan_io(plan, n_in, n_out):
    if plan is None:
        return [], [], [], [], {}
    sems = [pltpu.SemaphoreType.DMA((max(plan.n_remote, 1),)), pltpu.SemaphoreType.DMA((max(plan.n_remote, 1),)),
            pltpu.SemaphoreType.DMA((max(plan.n_local, 1),))]
    aliases = {n_in + i: n_out + o for i, o in plan.aliases.items()}
    return plan.ins, [_ANY] * len(plan.outs), plan.outs, sems, aliases


def _split_refs(refs, n_in, n_out, n_scr, plan):
    p_in, p_out = (len(plan.ins), len(plan.outs)) if plan is not None else (0, 0)
    refs = list(refs)
    ins, refs = refs[:n_in], refs[n_in:]
    pins, refs = refs[:p_in], refs[p_in:]
    outs, refs = refs[:n_out], refs[n_out:]
    pouts, refs = refs[:p_out], refs[p_out:]
    scr, psem = refs[:n_scr], refs[n_scr:]
    psem = tuple((lambda i, s=s: s.at[i]) for s in psem)
    return ins, outs, scr, (pins, pouts, psem)


def _grid_edge(grid, last):
    cond = None
    for ax, n in enumerate(grid):
        c = pl.program_id(ax) == (n - 1 if last else 0)
        cond = c if cond is None else cond & c
    return cond


def _plan_start(plan, pctx, grid):
    if plan is not None:
        pins, pouts, psem = pctx
        pl.when(_grid_edge(grid, False))(lambda: plan.start(pins, pouts, *psem))


def _plan_wait(plan, pctx, grid):
    if plan is not None:
        pins, pouts, psem = pctx
        pl.when(_grid_edge(grid, True))(lambda: plan.wait(pins, pouts, *psem))


def _run_plan(plan, *, name):
    def body(*refs):
        _, _, _, (pins, pouts, psem) = _split_refs(refs, 0, 0, 0, plan)
        plan.start(pins, pouts, *psem)
        plan.wait(pins, pouts, *psem)

    p_in, p_ospec, p_oshape, p_scr, p_alias = _plan_io(plan, 0, 0)
    return pl.pallas_call(body, name=name, in_specs=[_ANY] * len(p_in), out_specs=p_ospec, out_shape=p_oshape,
                          scratch_shapes=p_scr, input_output_aliases=p_alias)(*p_in)


def _fold8(x):
    return x.reshape(x.shape[0] // 8, 8, x.shape[1]).sum(axis=0)


def _ln_stats(r):
    mu = jnp.mean(r, -1, keepdims=True)
    xc = r - mu
    rstd = lax.rsqrt(jnp.mean(xc * xc, -1, keepdims=True) + EPS)
    return xc * rstd, rstd


def _sigmoid(x):
    return jax.nn.sigmoid(x)


def _gelu(x):
    return 0.5 * x * (1.0 + jnp.tanh(GELU_C * (x + GELU_A * x * x * x)))


def _gelu_grad(x):
    t = jnp.tanh(GELU_C * (x + GELU_A * x * x * x))
    return 0.5 * (1.0 + t) + 0.5 * x * (1.0 - t * t) * GELU_C * (1.0 + 3.0 * GELU_A * x * x)


MM_ROWS = 1024
DW_TOKENS = 4096


def _matmul(a, b, *, name, M, N, K, ta=False, tb=False, out_dtype=F32, tm=MM_ROWS, tn=1024, tk=1024,
            a_spec=None, b_spec=None, b_merge=None, out_shape=None, o_spec=None, into=None,
            a_sq=False, mul=None, add=None, add_scale=1.0, n_slots=False, plan=None):
    assert not n_slots or (K // min(tk, K) == 1 and add is None)
    tm, tn, tk = min(tm, M), min(tn, N), min(tk, K)
    assert M % tm == 0 and N % tn == 0 and K % tk == 0
    grid = (M // tm, N // tn, K // tk)
    nk = grid[2]
    if a_spec is None:
        a_spec = pl.BlockSpec((tk, tm), lambda i, j, k: (k, i)) if ta else pl.BlockSpec((tm, tk), lambda i, j, k: (i, k))
    if b_spec is None:
        b_spec = pl.BlockSpec((tn, tk), lambda i, j, k: (j, k)) if tb else pl.BlockSpec((tk, tn), lambda i, j, k: (k, j))
    if o_spec is None:
        o_spec = pl.BlockSpec((tm, tn), lambda i, j, k: (i, j))
        out_shape = jax.ShapeDtypeStruct((M, N), out_dtype)
    e_spec = pl.BlockSpec((tm, tn), lambda i, j, k: (i, j))
    dims = (((0 if ta else 1,), (1 if tb else 0,)), ((), ()))
    extra = [e for e in (mul, add, into) if e is not None]
    n_in = 2 + len(extra)

    def body(*refs):
        ins, outs, scr, pctx = _split_refs(refs, n_in, 1, 1 if nk > 1 else 0, plan)
        a_ref, b_ref = ins[0], ins[1]
        rest = list(ins[2:])
        mul_ref = rest.pop(0) if mul is not None else None
        add_ref = rest.pop(0) if add is not None else None
        o_ref = outs[0]
        _plan_start(plan, pctx, grid)
        av = a_ref[...].astype(BF16)
        if a_sq:
            av = av * av
        bv = b_ref[...]
        if b_merge is not None:
            bv = bv.reshape(b_merge)
        if n_slots:
            for s in range(bv.shape[0]):
                r = lax.dot_general(av, bv[s], dims, preferred_element_type=F32)
                w = r.shape[1]
                if mul_ref is not None:
                    r = r * (2.0 * mul_ref[:, s * w:(s + 1) * w].astype(F32))
                if o_ref.ndim == 3:
                    o_ref[s] = r.astype(o_ref.dtype)
                else:
                    o_ref[:, s * w:(s + 1) * w] = r.astype(o_ref.dtype)
            _plan_wait(plan, pctx, grid)
            return
        if bv.ndim == 3:
            w = av.shape[-1] // (1 if av.ndim == 3 else bv.shape[0])
            a_parts = [av[s] if av.ndim == 3 else av[:, s * w:(s + 1) * w] for s in range(bv.shape[0])]
            p = sum(lax.dot_general(a_parts[s], bv[s], dims, preferred_element_type=F32) for s in range(bv.shape[0]))
        else:
            p = lax.dot_general(av, bv, dims, preferred_element_type=F32)

        def finish(r):
            if mul_ref is not None:
                r = r * (2.0 * mul_ref[...].astype(F32))
            if add_ref is not None:
                r = r + add_scale * add_ref[...]
            o_ref[...] = r.astype(o_ref.dtype)

        if nk == 1:
            finish(p)
        else:
            acc_ref = scr[0]
            k = pl.program_id(2)

            @pl.when(k == 0)
            def _():
                acc_ref[...] = p

            @pl.when(k > 0)
            def _():
                acc_ref[...] += p

            @pl.when(k == nk - 1)
            def _():
                finish(acc_ref[...])

        _plan_wait(plan, pctx, grid)

    p_in, p_ospec, p_oshape, p_scr, p_alias = _plan_io(plan, n_in, 1)
    aliases = dict(p_alias)
    if into is not None:
        aliases[n_in - 1] = 0
    return pl.pallas_call(
        body, name=name, grid=grid,
        in_specs=[a_spec, b_spec] + [e_spec] * (len(extra) - (into is not None)) + [_ANY] * (into is not None)
        + [_ANY] * len(p_in),
        out_specs=[o_spec] + p_ospec, out_shape=[out_shape] + p_oshape,
        scratch_shapes=([pltpu.VMEM((tm, tn), F32)] if nk > 1 else []) + p_scr,
        input_output_aliases=aliases, compiler_params=_params(48, 3),
    )(*_hbm(a, b, *extra), *p_in)


def _rows4_spec(rowblk, n_axes):
    return pl.BlockSpec((4, 256, D), lambda *_: (0, rowblk, 0))


def _residual(h_ref, prev_refs):
    if not prev_refs:
        return h_ref[...]
    xhat, _ = _ln_stats(h_ref[...])
    return xhat * prev_refs[0][...] + prev_refs[1][...]


def _proj_ln(a_b, w, h_prev, g, b, *, name, prev_ln=(), w_rowblk=None, plan=None):
    T = a_b.shape[0]
    tm = min(MM_ROWS, T)
    grid = (T // tm,)
    row = pl.BlockSpec((tm, D), lambda i: (i, 0))
    vec = pl.BlockSpec((1, D), lambda i: (0, 0))
    w_spec = pl.BlockSpec((D, D), lambda i: (0, 0)) if w_rowblk is None else _rows4_spec(w_rowblk, 1)
    n_in = 5 + len(prev_ln)

    def body(*refs):
        ins, (r_ref, hb_ref), _, pctx = _split_refs(refs, n_in, 2, 0, plan)
        a_ref, w_ref, h_ref, g_ref, b_ref = ins[:5]
        _plan_start(plan, pctx, grid)
        mix = jnp.dot(a_ref[...], w_ref[...].reshape(D, D), preferred_element_type=F32)
        r = ALPHA * _residual(h_ref, ins[5:]) + mix
        xhat, _ = _ln_stats(r)
        r_ref[...] = r
        hb_ref[...] = (xhat * g_ref[...] + b_ref[...]).astype(BF16)
        _plan_wait(plan, pctx, grid)

    p_in, p_ospec, p_oshape, p_scr, p_alias = _plan_io(plan, n_in, 2)
    return pl.pallas_call(
        body, name=name, grid=grid,
        in_specs=[row, w_spec, row, vec, vec] + [vec] * len(prev_ln) + [_ANY] * len(p_in),
        out_specs=[row, row] + p_ospec,
        out_shape=[jax.ShapeDtypeStruct((T, D), F32), jax.ShapeDtypeStruct((T, D), BF16)] + p_oshape,
        scratch_shapes=p_scr, input_output_aliases=p_alias, compiler_params=_params(40, 1),
    )(*_hbm(a_b, w, h_prev, g, b, *prev_ln), *p_in)


def _ffn_ln(h_b, wbuf, h, g, b, *, name, prev_ln=(), plan=None):
    T = h_b.shape[0]
    slots = 2
    tm, tf = min(ROW_BLOCK, T), slots * 1024
    nf = 4 // slots
    F = nf * tf
    grid = (T // tm, nf)
    row = pl.BlockSpec((tm, D), lambda i, j: (i, 0))
    vec = pl.BlockSpec((1, D), lambda i, j: (0, 0))
    n_in = 6 + len(prev_ln)

    def body(*refs):
        ins, (ra_ref, r_ref, hbo_ref), (acc_ref,), pctx = _split_refs(refs, n_in, 3, 1, plan)
        hb_ref, w1_ref, w2_ref, h_ref, g_ref, b_ref = ins[:6]
        _plan_start(plan, pctx, grid)
        j = pl.program_id(1)
        hb = hb_ref[...]
        p = None
        for s in range(slots):
            ra = jnp.maximum(jnp.dot(hb, w1_ref[s], preferred_element_type=F32), 0.0)
            ra_ref[:, s * 1024:(s + 1) * 1024] = ra.astype(BF16)
            ps = jnp.dot((ra * ra).astype(BF16), w2_ref[s], preferred_element_type=F32)
            p = ps if p is None else p + ps

        @pl.when(j == 0)
        def _():
            acc_ref[...] = p

        @pl.when(j > 0)
        def _():
            acc_ref[...] += p

        @pl.when(j == nf - 1)
        def _():
            r = ALPHA * _residual(h_ref, ins[6:]) + acc_ref[...]
            xhat, _ = _ln_stats(r)
            r_ref[...] = r
            hbo_ref[...] = (xhat * g_ref[...] + b_ref[...]).astype(BF16)

        _plan_wait(plan, pctx, grid)

    p_in, p_ospec, p_oshape, p_scr, p_alias = _plan_io(plan, n_in, 3)
    return pl.pallas_call(
        body, name=name, grid=grid,
        in_specs=[row, pl.BlockSpec((slots, D, D), lambda i, j: (j, 0, 0)),
                  pl.BlockSpec((slots, D, D), lambda i, j: (j, 1, 0)), row, vec, vec] + [vec] * len(prev_ln)
        + [_ANY] * len(p_in),
        out_specs=[pl.BlockSpec((tm, tf), lambda i, j: (i, j)), row, row] + p_ospec,
        out_shape=[jax.ShapeDtypeStruct((T, F), BF16), jax.ShapeDtypeStruct((T, D), F32),
                   jax.ShapeDtypeStruct((T, D), BF16)] + p_oshape,
        scratch_shapes=[pltpu.VMEM((tm, D), F32)] + p_scr,
        input_output_aliases=p_alias, compiler_params=_params(56, 2),
    )(*_hbm(h_b, wbuf, wbuf, h, g, b, *prev_ln), *p_in)


def _ln_bwd(dy, r, g, *, name, loss_head=()):
    T = r.shape[0]
    tm = min(ROW_BLOCK, T)
    row = pl.BlockSpec((tm, D), lambda i: (i, 0))
    vec = pl.BlockSpec((1, D), lambda i: (0, 0))
    acc = pl.BlockSpec((8, D), lambda i: (0, 0))
    operands, in_specs = ([r, g, *loss_head], [row, vec, vec, row]) if loss_head else ([r, g, dy], [row, vec, row])
    n_in = len(operands)

    def body(*refs):
        r_ref, g_ref = refs[:2]
        dr_ref, drb_ref, dg_ref, db_ref = refs[n_in:n_in + 4]

        @pl.when(pl.program_id(0) == 0)
        def _():
            for ref in refs[n_in + 2:]:
                ref[...] = jnp.zeros_like(ref)

        xhat, rstd = _ln_stats(r_ref[...])
        if loss_head:
            err = xhat * g_ref[...] + refs[2][...] - refs[3][...]
            refs[n_in + 4][...] += _fold8(err * err)
            dy_ = err * (1.0 / D)
        else:
            dy_ = refs[2][...]
        dxh = dy_ * g_ref[...]
        m1 = jnp.mean(dxh, -1, keepdims=True)
        m2 = jnp.mean(dxh * xhat, -1, keepdims=True)
        dr = rstd * (dxh - m1 - xhat * m2)
        dr_ref[...] = dr
        drb_ref[...] = dr.astype(BF16)
        dg_ref[...] += _fold8(dy_ * xhat)
        db_ref[...] += _fold8(dy_)

    n_acc = 3 if loss_head else 2
    return pl.pallas_call(
        body, name=name, grid=(T // tm,), in_specs=in_specs, out_specs=[row, row] + [acc] * n_acc,
        out_shape=[jax.ShapeDtypeStruct((T, D), F32), jax.ShapeDtypeStruct((T, D), BF16)]
        + [jax.ShapeDtypeStruct((8, D), F32)] * n_acc,
        compiler_params=_params(40, 1),
    )(*_hbm(*operands))


def _rope(x, c, s1, s2):
    return x * c + pltpu.roll(x, 112, 1) * s1 + pltpu.roll(x, 16, 1) * s2


def _rope_t(dy, c, s1, s2):
    return dy * c + pltpu.roll(dy * s1, 16, 1) + pltpu.roll(dy * s2, 112, 1)


def _rms(x, g):
    rstd = lax.rsqrt(jnp.mean(x * x, -1, keepdims=True) + EPS)
    xhat = x * rstd
    return xhat * g, xhat, rstd


def _mla_prep(z0, gq, gkv, wq, wk, wv, rc, rs1, rs2):
    T = z0.shape[0]
    tm = min(ROW_BLOCK, T)
    HW = HEADS * 128

    def body(cq_ref, ckv_ref, kr_ref, gq_ref, gkv_ref, wq_ref, wk_ref, wv_ref, c_ref, s1_ref, s2_ref,
             q_ref, k_ref, v_ref):
        nq = _rms(cq_ref[...], gq_ref[...])[0].astype(BF16)
        nkv = _rms(ckv_ref[...], gkv_ref[...])[0].astype(BF16)
        q = jnp.dot(nq, wq_ref[...], preferred_element_type=F32)
        k = jnp.dot(nkv, wk_ref[...], preferred_element_type=F32)
        v = jnp.dot(nkv, wv_ref[...], preferred_element_type=F32)
        c, s1, s2 = c_ref[...], s1_ref[...], s2_ref[...]
        kr = _rope(pltpu.roll(kr_ref[...], 64, 1), c, s1, s2)
        for h in range(HEADS):
            sl = slice(h * 128, (h + 1) * 128)
            q_ref[:, sl] = (_rope(q[:, sl], c, s1, s2) * QK_SCALE).astype(BF16)
            k_ref[:, sl] = (k[:, sl] + kr).astype(BF16)
        v_ref[...] = v.astype(BF16)

    full = lambda shape: pl.BlockSpec(shape, lambda i: (0, 0))
    tab = pl.BlockSpec((tm, 128), lambda i: (i, 0))
    return pl.pallas_call(
        body, name="mla_prep", grid=(T // tm,),
        in_specs=[pl.BlockSpec((tm, 256), lambda i: (i, 0)), pl.BlockSpec((tm, 256), lambda i: (i, 1)),
                  pl.BlockSpec((tm, 128), lambda i: (i, 12)), full((1, 256)), full((1, 256)),
                  full((256, HW)), full((256, HW)), full((256, 512)), tab, tab, tab],
        out_specs=[pl.BlockSpec((tm, HW), lambda i: (i, 0)), pl.BlockSpec((tm, HW), lambda i: (i, 0)),
                   pl.BlockSpec((tm, 512), lambda i: (i, 0))],
        out_shape=[jax.ShapeDtypeStruct((T, HW), BF16), jax.ShapeDtypeStruct((T, HW), BF16),
                   jax.ShapeDtypeStruct((T, 512), BF16)],
        compiler_params=_params(40, 1),
    )(*_hbm(z0, z0, z0, gq, gkv, wq, wk, wv, rc, rs1, rs2))


def _flash_fwd(q, k, v, plan=None):
    T = q.shape[0]
    bq = min(2 * ROW_BLOCK, T)
    nq = T // bq
    pairs = [(i, j) for i in range(nq) for j in range(i + 1)]
    imap, jmap = (jnp.array(m, jnp.int32) for m in zip(*pairs))
    grid = (4, len(pairs))

    def body(imap_ref, jmap_ref, *refs):
        (q_ref, k_ref, v_ref), (o_ref, lse_ref), (m_sc, acc_sc), pctx = _split_refs(refs, 3, 2, 2, plan)
        _plan_start(plan, pctx, grid)
        i, j = imap_ref[pl.program_id(1)], jmap_ref[pl.program_id(1)]
        first = lax.broadcasted_iota(jnp.int32, (bq, 128), 1) < 64

        @pl.when(j == 0)
        def _():
            m_sc[...] = jnp.full_like(m_sc, -jnp.inf)
            acc_sc[...] = jnp.zeros_like(acc_sc)

        def step(masked):
            vp = v_ref[...]
            for h in range(2):
                sl = slice(h * 128, (h + 1) * 128)
                s = lax.dot_general(q_ref[:, sl], k_ref[:, sl], NT_DIMS, preferred_element_type=F32)
                if masked:
                    rows = lax.broadcasted_iota(jnp.int32, (bq, bq), 0)
                    cols = lax.broadcasted_iota(jnp.int32, (bq, bq), 1)
                    s = jnp.where(cols <= rows, s, -jnp.inf)
                m_prev = m_sc[h, :, 0:1]
                m_new = jnp.maximum(m_prev, jnp.max(s, axis=1, keepdims=True))
                alpha = jnp.exp(m_prev - m_new)
                p = jnp.exp(s - m_new).astype(BF16)
                vh = jnp.where(first if h == 0 else jnp.logical_not(first), vp, jnp.ones_like(vp))
                acc_sc[h] = acc_sc[h] * alpha + jnp.dot(p, vh, preferred_element_type=F32)
                m_sc[h] = jnp.broadcast_to(m_new, (bq, 128))

        @pl.when(j < i)
        def _():
            step(False)

        @pl.when(j == i)
        def _():
            step(True)
            a0, a1 = acc_sc[0], acc_sc[1]
            l0, l1 = pltpu.roll(a0, 64, 1), pltpu.roll(a1, 64, 1)
            o_ref[...] = jnp.where(first, a0 / l0, a1 / l1).astype(BF16)
            lse_ref[...] = jnp.where(first, m_sc[0] + jnp.log(l0), m_sc[1] + jnp.log(l1))

        _plan_wait(plan, pctx, grid)

    qi = lambda hp, t, im, jm: (im[t], hp)
    kj = lambda hp, t, im, jm: (jm[t], hp)
    p_in, p_ospec, p_oshape, p_scr, p_alias = _plan_io(plan, 2 + 3, 2)
    return pl.pallas_call(
        body, name="flash_fwd",
        out_shape=[jax.ShapeDtypeStruct((T, 512), BF16), jax.ShapeDtypeStruct((T, 512), F32)] + p_oshape,
        grid_spec=pltpu.PrefetchScalarGridSpec(
            num_scalar_prefetch=2, grid=grid,
            in_specs=[pl.BlockSpec((bq, 256), qi), pl.BlockSpec((bq, 256), kj), pl.BlockSpec((bq, 128), kj)]
            + [_ANY] * len(p_in),
            out_specs=[pl.BlockSpec((bq, 128), qi), pl.BlockSpec((bq, 128), qi)] + p_ospec,
            scratch_shapes=[pltpu.VMEM((2, bq, 128), F32), pltpu.VMEM((2, bq, 128), F32)] + p_scr),
        input_output_aliases=p_alias, compiler_params=_params(56, 2),
    )(imap, jmap, *_hbm(q, k, v), *p_in)


def _attn_delta(dmix, o):
    T = o.shape[0]
    tm = min(ROW_BLOCK, T)
    blk = pl.BlockSpec((tm, 512), lambda i: (i, 0))

    def body(do_ref, o_ref, delta_ref, dob_ref):
        first = lax.broadcasted_iota(jnp.int32, (tm, 128), 1) < 64
        for hp in range(4):
            sl = slice(hp * 128, (hp + 1) * 128)
            prod = do_ref[:, sl] * o_ref[:, sl].astype(F32)
            d0 = jnp.sum(jnp.where(first, prod, 0.0), axis=1, keepdims=True)
            d1 = jnp.sum(jnp.where(first, 0.0, prod), axis=1, keepdims=True)
            delta_ref[:, sl] = jnp.where(first, d0, d1)
        dob_ref[...] = do_ref[...].astype(BF16)

    return pl.pallas_call(
        body, name="attn_delta", grid=(T // tm,), in_specs=[blk, blk], out_specs=[blk, blk],
        out_shape=[jax.ShapeDtypeStruct((T, 512), F32), jax.ShapeDtypeStruct((T, 512), BF16)],
        compiler_params=_params(32, 1),
    )(*_hbm(dmix, o))


def _flash_bwd(q, k, v, do_b, lse, delta, plan=None):
    T = q.shape[0]
    bq = min(2 * ROW_BLOCK, T)
    nq = T // bq
    pairs = [(i, j) for j in range(nq) for i in range(j, nq)]
    imap, jmap = (jnp.array(m, jnp.int32) for m in zip(*pairs))
    grid = (4, len(pairs))

    def body(imap_ref, jmap_ref, *refs):
        ((q_ref, k_ref, v_ref, do_ref, lse_ref, dl_ref), (dq_hbm, dk_ref, dv_ref), (dq_sc, dk_sc, dv_sc, sem),
         pctx) = _split_refs(refs, 6, 3, 4, plan)
        _plan_start(plan, pctx, grid)
        hp = pl.program_id(0)
        i, j = imap_ref[pl.program_id(1)], jmap_ref[pl.program_id(1)]
        first = lax.broadcasted_iota(jnp.int32, (bq, 128), 1) < 64

        @pl.when((j == 0) & (i == 0))
        def _():
            dq_sc[...] = jnp.zeros_like(dq_sc)

        @pl.when(i == j)
        def _():
            dk_sc[...] = jnp.zeros_like(dk_sc)
            dv_sc[...] = jnp.zeros_like(dv_sc)

        def tile(r0, nr, nc, masked):
            rs, cs = slice(r0, r0 + nr), slice(0, nc)
            vp = v_ref[cs, :]
            do = do_ref[rs, :]
            lanes = first[rs, :]
            for h in range(2):
                sl = slice(h * 128, (h + 1) * 128)
                qh, kh = q_ref[rs, sl], k_ref[cs, sl]
                s = lax.dot_general(qh, kh, NT_DIMS, preferred_element_type=F32)
                p = jnp.exp(s - lse_ref[rs, h * 64:h * 64 + 1])
                if masked:
                    rows = r0 + lax.broadcasted_iota(jnp.int32, (nr, nc), 0)
                    cols = lax.broadcasted_iota(jnp.int32, (nr, nc), 1)
                    p = jnp.where(cols <= rows, p, 0.0)
                do_h = jnp.where(lanes if h == 0 else jnp.logical_not(lanes), do, jnp.zeros_like(do))
                dv_sc[cs, :] += lax.dot_general(p.astype(BF16), do_h, TN_DIMS, preferred_element_type=F32)
                dp = lax.dot_general(do_h, vp, NT_DIMS, preferred_element_type=F32)
                ds = (p * (dp - dl_ref[rs, h * 64:h * 64 + 1])).astype(BF16)
                dq_sc[i, rs, sl] += jnp.dot(ds, kh, preferred_element_type=F32)
                dk_sc[cs, sl] += lax.dot_general(ds, qh, TN_DIMS, preferred_element_type=F32)

        @pl.when(i > j)
        def _():
            tile(0, bq, bq, False)

        @pl.when(i == j)
        def _():
            tile(0, bq // 2, bq // 2, True)
            tile(bq // 2, bq // 2, bq, True)

        @pl.when(i == nq - 1)
        def _():
            dk_ref[...] = dk_sc[...]
            dv_ref[...] = dv_sc[...]

        @pl.when((j == nq - 1) & (i == nq - 1))
        def _():
            cp = pltpu.make_async_copy(dq_sc, dq_hbm.at[hp], sem)
            cp.start()
            cp.wait()

        _plan_wait(plan, pctx, grid)

    qi = lambda hp, t, im, jm: (im[t], hp)
    kj = lambda hp, t, im, jm: (jm[t], hp)
    p_in, p_ospec, p_oshape, p_scr, p_alias = _plan_io(plan, 2 + 6, 3)
    return pl.pallas_call(
        body, name="flash_bwd",
        out_shape=[jax.ShapeDtypeStruct((4, nq, bq, 256), F32), jax.ShapeDtypeStruct((T, 1024), F32),
                   jax.ShapeDtypeStruct((T, 512), F32)] + p_oshape,
        grid_spec=pltpu.PrefetchScalarGridSpec(
            num_scalar_prefetch=2, grid=grid,
            in_specs=[pl.BlockSpec((bq, 256), qi), pl.BlockSpec((bq, 256), kj), pl.BlockSpec((bq, 128), kj),
                      pl.BlockSpec((bq, 128), qi), pl.BlockSpec((bq, 128), qi), pl.BlockSpec((bq, 128), qi)]
            + [_ANY] * len(p_in),
            out_specs=[_ANY, pl.BlockSpec((bq, 256), kj), pl.BlockSpec((bq, 128), kj)] + p_ospec,
            scratch_shapes=[pltpu.VMEM((nq, bq, 256), F32), pltpu.VMEM((bq, 256), F32), pltpu.VMEM((bq, 128), F32),
                            pltpu.SemaphoreType.DMA] + p_scr),
        input_output_aliases=p_alias, compiler_params=_params(56, 2),
    )(imap, jmap, *_hbm(q, k, v, do_b, lse, delta), *p_in)


def _mla_bwd(z0, dq4, dk, dv, gq, gkv, wq, wk, wv, rc, rs1, rs2, plan=None):
    T = z0.shape[0]
    tm = min(ROW_BLOCK, T)
    HW = HEADS * 128
    grid = (T // tm,)
    dq4 = dq4.reshape(4, T, 256)

    def body(*refs):
        ((cq_ref, ckv_ref, dq_ref, dk_ref, dv_ref, gq_ref, gkv_ref, wq_ref, wk_ref, wv_ref, c_ref, s1_ref, s2_ref),
         (dc_ref, dkr_ref, dwq_ref, dwk_ref, dwv_ref, dgq_ref, dgkv_ref), _, pctx) = _split_refs(refs, 13, 7, 0, plan)
        _plan_start(plan, pctx, grid)

        @pl.when(pl.program_id(0) == 0)
        def _():
            for ref in (dwq_ref, dwk_ref, dwv_ref, dgq_ref, dgkv_ref):
                ref[...] = jnp.zeros_like(ref)

        c, s1, s2 = c_ref[...], s1_ref[...], s2_ref[...]
        lane = lax.broadcasted_iota(jnp.int32, (tm, 128), 1)
        nq, xq, rq = _rms(cq_ref[...], gq_ref[...])
        nkv, xkv, rkv = _rms(ckv_ref[...], gkv_ref[...])
        nq_b, nkv_b = nq.astype(BF16), nkv.astype(BF16)

        dq_parts, dk_parts = [], []
        dkr = jnp.zeros((tm, 128), F32)
        for h in range(HEADS):
            blk = dq_ref[h // 2, :, (h % 2) * 128:(h % 2 + 1) * 128] * QK_SCALE
            dq_parts.append(_rope_t(blk, c, s1, s2).astype(BF16))
            kb = dk_ref[:, h * 128:(h + 1) * 128]
            dk_parts.append(jnp.where(lane < NOPE, kb, 0.0).astype(BF16))
            dkr = dkr + kb
        dq_b = jnp.concatenate(dq_parts, axis=1)
        dk_b = jnp.concatenate(dk_parts, axis=1)
        dv_b = dv_ref[...].astype(BF16)

        dwq_ref[...] += lax.dot_general(nq_b, dq_b, TN_DIMS, preferred_element_type=F32)
        dwk_ref[...] += lax.dot_general(nkv_b, dk_b, TN_DIMS, preferred_element_type=F32)
        dwv_ref[...] += lax.dot_general(nkv_b, dv_b, TN_DIMS, preferred_element_type=F32)
        dnq = lax.dot_general(dq_b, wq_ref[...], NT_DIMS, preferred_element_type=F32)
        dnkv = (lax.dot_general(dk_b, wk_ref[...], NT_DIMS, preferred_element_type=F32)
                + lax.dot_general(dv_b, wv_ref[...], NT_DIMS, preferred_element_type=F32))

        def rms_bwd(dn, xhat, rstd, g):
            dxh = dn * g
            return rstd * (dxh - xhat * jnp.mean(dxh * xhat, -1, keepdims=True))

        dc_ref[:, :256] = rms_bwd(dnq, xq, rq, gq_ref[...]).astype(BF16)
        dc_ref[:, 256:] = rms_bwd(dnkv, xkv, rkv, gkv_ref[...]).astype(BF16)
        dgq_ref[...] += _fold8(dnq * xq)
        dgkv_ref[...] += _fold8(dnkv * xkv)
        dkr = pltpu.roll(_rope_t(dkr, c, s1, s2), 64, 1)
        dkr_ref[...] = jnp.where(lane < ROPE, dkr, 0.0).astype(BF16)
        _plan_wait(plan, pctx, grid)

    full = lambda shape: pl.BlockSpec(shape, lambda i: (0,) * len(shape))
    tab = pl.BlockSpec((tm, 128), lambda i: (i, 0))
    p_in, p_ospec, p_oshape, p_scr, p_alias = _plan_io(plan, 13, 7)
    return pl.pallas_call(
        body, name="mla_bwd", grid=grid,
        in_specs=[pl.BlockSpec((tm, 256), lambda i: (i, 0)), pl.BlockSpec((tm, 256), lambda i: (i, 1)),
                  pl.BlockSpec((4, tm, 256), lambda i: (0, i, 0)),
                  pl.BlockSpec((tm, HW), lambda i: (i, 0)), pl.BlockSpec((tm, 512), lambda i: (i, 0)),
                  full((1, 256)), full((1, 256)), full((256, HW)), full((256, HW)), full((256, 512)), tab, tab, tab]
        + [_ANY] * len(p_in),
        out_specs=[pl.BlockSpec((tm, 512), lambda i: (i, 0)), tab, full((256, HW)), full((256, HW)),
                   full((256, 512)), full((8, 256)), full((8, 256))] + p_ospec,
        out_shape=[jax.ShapeDtypeStruct((T, 512), BF16), jax.ShapeDtypeStruct((T, 128), BF16),
                   jax.ShapeDtypeStruct((256, HW), F32), jax.ShapeDtypeStruct((256, HW), F32),
                   jax.ShapeDtypeStruct((256, 512), F32), jax.ShapeDtypeStruct((8, 256), F32),
                   jax.ShapeDtypeStruct((8, 256), F32)] + p_oshape,
        scratch_shapes=p_scr, input_output_aliases=p_alias, compiler_params=_params(48, 1),
    )(*_hbm(z0, z0, dq4, dk, dv, gq, gkv, wq, wk, wv, rc, rs1, rs2), *p_in)


def _sgu_fwd(z0, a_out, ln_g, ln_b, w, b_t):
    T = z0.shape[0]
    tm = min(ROW_BLOCK, T)
    W = SGU_G * SGU_C

    def body(u_ref, v_ref, a_ref, g_ref, b_ref, w_ref, bt_ref, o_ref):
        o_ref[:, :W] = a_ref[...]
        ug = _gelu(u_ref[...])
        xhat, _ = _ln_stats(_gelu(v_ref[...]))
        vn = (xhat * g_ref[...] + b_ref[...]).astype(BF16)
        tril = lax.broadcasted_iota(jnp.int32, (SGU_C, SGU_C), 0) >= lax.broadcasted_iota(jnp.int32, (SGU_C, SGU_C), 1)
        for g in range(SGU_G):
            cs = slice(g * SGU_C, (g + 1) * SGU_C)
            wg = jnp.where(tril, w_ref[g], 0.0).astype(BF16)
            bcol = bt_ref[:, g:g + 1]
            for c in range(tm // SGU_C):
                rs = slice(c * SGU_C, (c + 1) * SGU_C)
                mixed = jnp.dot(wg, vn[rs, cs], preferred_element_type=F32) + bcol
                o_ref[rs, W + g * SGU_C:W + (g + 1) * SGU_C] = (ug[rs, cs] * mixed).astype(BF16)

    full = lambda shape: pl.BlockSpec(shape, lambda i: (0,) * len(shape))
    return pl.pallas_call(
        body, name="sgu_fwd", grid=(T // tm,),
        in_specs=[pl.BlockSpec((tm, W), lambda i: (i, 1)), pl.BlockSpec((tm, W), lambda i: (i, 2)),
                  pl.BlockSpec((tm, W), lambda i: (i, 0)),
                  full((1, W)), full((1, W)), full((SGU_G, SGU_C, SGU_C)), full((SGU_C, SGU_G))],
        out_specs=pl.BlockSpec((tm, 2 * W), lambda i: (i, 0)),
        out_shape=jax.ShapeDtypeStruct((T, 2 * W), BF16),
        compiler_params=_params(32, 1),
    )(*_hbm(z0, z0, a_out, ln_g, ln_b, w, b_t))


def _sgu_bwd(z0, dmix, dc, dkr, ln_g, ln_b, w, b_t, plan=None):
    T = z0.shape[0]
    tm = min(ROW_BLOCK, T)
    W = SGU_G * SGU_C
    grid = (T // tm,)

    def body(*refs):
        ((u_ref, v_ref, do_ref, dc_ref, dkr_ref, g_ref, b_ref, w_ref, bt_ref),
         (dz_ref, dw_ref, db_ref, dlg_ref, dlb_ref), _, pctx) = _split_refs(refs, 9, 5, 0, plan)
        _plan_start(plan, pctx, grid)

        @pl.when(pl.program_id(0) == 0)
        def _():
            for ref in (dw_ref, db_ref, dlg_ref, dlb_ref):
                ref[...] = jnp.zeros_like(ref)

        dz_ref[:, :W] = dc_ref[...]
        dz_ref[:, 3 * W:] = dkr_ref[...]

        u, v, dout = u_ref[...], v_ref[...], do_ref[...]
        ug = _gelu(u)
        xhat, rstd = _ln_stats(_gelu(v))
        vn = (xhat * g_ref[...] + b_ref[...]).astype(BF16)
        dmixed = dout * ug
        dmixed_b = dmixed.astype(BF16)
        tril = lax.broadcasted_iota(jnp.int32, (SGU_C, SGU_C), 0) >= lax.broadcasted_iota(jnp.int32, (SGU_C, SGU_C), 1)
        lane = lax.broadcasted_iota(jnp.int32, (SGU_C, SGU_C), 1)
        dvn_cols = []
        for g in range(SGU_G):
            cs = slice(g * SGU_C, (g + 1) * SGU_C)
            wg = jnp.where(tril, w_ref[g], 0.0).astype(BF16)
            bcol = bt_ref[:, g:g + 1]
            dw_g = jnp.zeros((SGU_C, SGU_C), F32)
            db_g = jnp.zeros((SGU_C, 1), F32)
            dvn_rows = []
            for c in range(tm // SGU_C):
                rs = slice(c * SGU_C, (c + 1) * SGU_C)
                mixed = jnp.dot(wg, vn[rs, cs], preferred_element_type=F32) + bcol
                dz_ref[rs, W + g * SGU_C:W + (g + 1) * SGU_C] = (dout[rs, cs] * mixed * _gelu_grad(u[rs, cs])).astype(BF16)
                dm = dmixed_b[rs, cs]
                dw_g = dw_g + lax.dot_general(dm, vn[rs, cs], NT_DIMS, preferred_element_type=F32)
                db_g = db_g + jnp.sum(dmixed[rs, cs], axis=1, keepdims=True)
                dvn_rows.append(lax.dot_general(wg, dm, TN_DIMS, preferred_element_type=F32))
            dw_ref[g] += jnp.where(tril, dw_g, 0.0)
            db_ref[...] += jnp.where(lane == g, db_g, 0.0)
            dvn_cols.append(jnp.concatenate(dvn_rows, axis=0))
        dvn = jnp.concatenate(dvn_cols, axis=1)
        dxh = dvn * g_ref[...]
        m1 = jnp.mean(dxh, -1, keepdims=True)
        m2 = jnp.mean(dxh * xhat, -1, keepdims=True)
        dvg = rstd * (dxh - m1 - xhat * m2)
        dz_ref[:, 2 * W:3 * W] = (dvg * _gelu_grad(v)).astype(BF16)
        dlg_ref[...] += _fold8(dvn * xhat)
        dlb_ref[...] += _fold8(dvn)
        _plan_wait(plan, pctx, grid)

    full = lambda shape: pl.BlockSpec(shape, lambda i: (0,) * len(shape))
    p_in, p_ospec, p_oshape, p_scr, p_alias = _plan_io(plan, 9, 5)
    return pl.pallas_call(
        body, name="sgu_bwd", grid=grid,
        in_specs=[pl.BlockSpec((tm, W), lambda i: (i, 1)), pl.BlockSpec((tm, W), lambda i: (i, 2)),
                  pl.BlockSpec((tm, W), lambda i: (i, 1)), pl.BlockSpec((tm, W), lambda i: (i, 0)),
                  pl.BlockSpec((tm, 128), lambda i: (i, 0)),
                  full((1, W)), full((1, W)), full((SGU_G, SGU_C, SGU_C)), full((SGU_C, SGU_G))] + [_ANY] * len(p_in),
        out_specs=[pl.BlockSpec((tm, 3 * W + 128), lambda i: (i, 0)), full((SGU_G, SGU_C, SGU_C)),
                   full((SGU_C, SGU_C)), full((8, W)), full((8, W))] + p_ospec,
        out_shape=[jax.ShapeDtypeStruct((T, 3 * W + 128), BF16), jax.ShapeDtypeStruct((SGU_G, SGU_C, SGU_C), F32),
                   jax.ShapeDtypeStruct((SGU_C, SGU_C), F32), jax.ShapeDtypeStruct((8, W), F32),
                   jax.ShapeDtypeStruct((8, W), F32)] + p_oshape,
        scratch_shapes=p_scr, input_output_aliases=p_alias, compiler_params=_params(40, 1),
    )(*_hbm(z0, z0, dmix, dc, dkr, ln_g, ln_b, w, b_t), *p_in)


def _hg_lower_bound(lb_ref):
    a0, a1 = lb_ref[0:1, :], lb_ref[1:2, :]
    m = jnp.maximum(a0, a1)
    e0, e1 = jnp.exp(a0 - m), jnp.exp(a1 - m)
    return e1 / (e0 + e1)


def _running_sum(x, reverse=False):
    n = x.shape[0]
    row = lax.broadcasted_iota(jnp.int32, x.shape, 0)
    s = 1
    while s < n:
        if reverse:
            x = x + jnp.where(row < n - s, pltpu.roll(x, n - s, 0), 0.0)
        else:
            x = x + jnp.where(row >= s, pltpu.roll(x, s, 0), 0.0)
        s *= 2
    return x


def _hg_chunk(qc, fc, lb):
    C = HG_CHUNK
    rows = lax.broadcasted_iota(jnp.int32, (C, C), 0)
    cols = lax.broadcasted_iota(jnp.int32, (C, C), 1)
    rowid = lax.broadcasted_iota(jnp.int32, (C, 128), 0)
    sq, sg = _sigmoid(qc), _sigmoid(fc)
    qf = qc * sq
    gate = lb + (1.0 - lb) * sg
    kk = 1.0 - gate
    lg = jnp.log(gate)
    bcum = _running_sum(lg)
    b_mid = jnp.sum(jnp.where(rowid < C // 2, lg, 0.0), axis=0, keepdims=True)
    b_last = jnp.sum(lg, axis=0, keepdims=True)
    eq, ek, e, eh = jnp.exp(bcum - b_mid), jnp.exp(b_mid - bcum), jnp.exp(bcum), jnp.exp(b_last - bcum)
    qt, kt, qe, khat = qf * eq, kk * ek, qf * e, kk * eh
    a = lax.dot_general(qt.astype(BF16), kt.astype(BF16), NT_DIMS, preferred_element_type=F32)
    a = jnp.where(rows >= cols, a, 0.0)
    return dict(sq=sq, sg=sg, gate=gate, kk=kk, eq=eq, ek=ek, e=e, eh=eh, qt=qt, kt=kt, qe=qe, khat=khat, a=a,
                e_last=jnp.exp(b_last), tril=rows >= cols, rowid=rowid)


def _hgrn_fwd(z4, hg_lb, gnorm):
    T = z4.shape[1]
    tb = min(ROW_BLOCK, T)
    C = HG_CHUNK
    ncb = tb // C
    HPB = HG_HEADS_PER_STEP

    def body(q_ref, f_ref, i_ref, g_ref, lb_ref, gn_ref, y_ref, o_ref, st_ref, st_sc):
        @pl.when(pl.program_id(1) == 0)
        def _():
            st_sc[...] = jnp.zeros_like(st_sc)

        def chunk(c, carry):
            rs = pl.ds(pl.multiple_of(c * C, C), C)
            for hh in range(HPB):
                hs = slice(hh * 128, (hh + 1) * 128)
                lb = _hg_lower_bound(lb_ref.at[:, hs])
                v_b = i_ref[rs, hs].astype(BF16)
                gc = g_ref[rs, hs]
                x = _hg_chunk(q_ref[rs, hs], f_ref[rs, hs], lb)
                st = st_sc[hh]
                st_ref[hh, c] = st
                o = (jnp.dot(x["a"].astype(BF16), v_b, preferred_element_type=F32)
                     + lax.dot_general(x["qe"].astype(BF16), st.astype(BF16), NT_DIMS, preferred_element_type=F32))
                st_sc[hh] = st * x["e_last"] + lax.dot_general(v_b, x["khat"].astype(BF16), TN_DIMS,
                                                               preferred_element_type=F32)
                o_ref[rs, hs] = o
                n = o * lax.rsqrt(jnp.mean(o * o, -1, keepdims=True) + EPS)
                y_ref[rs, hs] = (n * gn_ref[:, hs] * (gc * _sigmoid(gc))).astype(BF16)
            return carry

        lax.fori_loop(0, ncb, chunk, 0, unroll=4)

    W = 128 * HPB
    zb = lambda k: pl.BlockSpec((None, tb, W), lambda h, t: (k, t, h))
    out = pl.BlockSpec((tb, W), lambda h, t: (t, h))
    return pl.pallas_call(
        body, name="hgrn_fwd", grid=(HEADS // HPB, T // tb),
        in_specs=[zb(0), zb(1), zb(2), zb(3), pl.BlockSpec((2, W), lambda h, t: (0, h)),
                  pl.BlockSpec((1, W), lambda h, t: (0, h))],
        out_specs=[out, out, pl.BlockSpec((HPB, ncb, 128, 128), lambda h, t: (h, t, 0, 0))],
        out_shape=[jax.ShapeDtypeStruct((T, D), BF16), jax.ShapeDtypeStruct((T, D), F32),
                   jax.ShapeDtypeStruct((HEADS, T // C, 128, 128), F32)],
        scratch_shapes=[pltpu.VMEM((HPB, 128, 128), F32)],
        compiler_params=_params(48, 2),
    )(*_hbm(z4, z4, z4, z4, hg_lb, gnorm))


def _hgrn_bwd(z4, o_raw, dy, states, hg_lb, gnorm):
    T = z4.shape[1]
    tb = min(ROW_BLOCK, T)
    C = HG_CHUNK
    ncb = tb // C
    nt = T // tb
    HPB = HG_HEADS_PER_STEP

    def body(q_ref, f_ref, i_ref, g_ref, o_ref, dy_ref, st_ref, lb_ref, gn_ref, dz_ref, dlb_ref, dgn_ref, dst_sc):
        @pl.when(pl.program_id(1) == 0)
        def _():
            dst_sc[...] = jnp.zeros_like(dst_sc)
            dlb_ref[...] = jnp.zeros_like(dlb_ref)
            dgn_ref[...] = jnp.zeros_like(dgn_ref)

        def chunk(cc, carry):
            for hh in range(HPB):
                one_head(ncb - 1 - cc, hh, slice(hh * 128, (hh + 1) * 128))
            return carry

        def one_head(c, hh, hs):
            rs = pl.ds(pl.multiple_of(c * C, C), C)
            lb = _hg_lower_bound(lb_ref.at[:, hs])
            gn = gn_ref[:, hs]
            qc, gc = q_ref[rs, hs], g_ref[rs, hs]
            v_b = i_ref[rs, hs].astype(BF16)
            x = _hg_chunk(qc, f_ref[rs, hs], lb)
            st, dst = st_ref[hh, c], dst_sc[hh]
            st_b, dst_b = st.astype(BF16), dst.astype(BF16)
            o, dyc = o_ref[rs, hs], dy_ref[rs, hs]
            sgg = _sigmoid(gc)
            sil = gc * sgg
            rstd = lax.rsqrt(jnp.mean(o * o, -1, keepdims=True) + EPS)
            n = o * rstd
            dgn_ref[:, hs] += _fold8(dyc * n * sil)
            dn = dyc * gn * sil
            do = rstd * (dn - n * jnp.mean(dn * n, -1, keepdims=True))
            dg = dyc * n * gn * (sgg * (1.0 + gc * (1.0 - sgg)))
            do_b = do.astype(BF16)
            da = jnp.where(x["tril"], lax.dot_general(do_b, v_b, NT_DIMS, preferred_element_type=F32), 0.0).astype(BF16)
            qt_b, kt_b, qe_b, khat_b = (x[n_].astype(BF16) for n_ in ("qt", "kt", "qe", "khat"))
            dv = (lax.dot_general(x["a"].astype(BF16), do_b, TN_DIMS, preferred_element_type=F32)
                  + lax.dot_general(khat_b, dst_b, NT_DIMS, preferred_element_type=F32))
            dqt = jnp.dot(da, kt_b, preferred_element_type=F32)
            dqe = jnp.dot(do_b, st_b, preferred_element_type=F32)
            dkt = lax.dot_general(da, qt_b, TN_DIMS, preferred_element_type=F32)
            dkhat = jnp.dot(v_b, dst_b, preferred_element_type=F32)
            dst_sc[hh] = lax.dot_general(do_b, qe_b, TN_DIMS, preferred_element_type=F32) + dst * x["e_last"]
            de_last = jnp.sum(st * dst, axis=0, keepdims=True)
            dqf = dqt * x["eq"] + dqe * x["e"]
            dkk = dkt * x["ek"] + dkhat * x["eh"]
            dkh_kh = dkhat * x["khat"]
            db = dqt * qt_b.astype(F32) - dkt * kt_b.astype(F32) + dqe * x["qe"] - dkh_kh
            db_last = jnp.sum(dkh_kh, axis=0, keepdims=True) + de_last * x["e_last"]
            db = db + jnp.where(x["rowid"] == C - 1, db_last, 0.0)
            dlg = _running_sum(db, reverse=True)
            dgate = dlg / x["gate"] - dkk
            sg, sq = x["sg"], x["sq"]
            dlb_ref[:, hs] += _fold8(dgate * (1.0 - sg)) * (lb * (1.0 - lb))
            dz_ref[0, rs, hs] = (dqf * (sq * (1.0 + qc * (1.0 - sq)))).astype(BF16)
            dz_ref[1, rs, hs] = (dgate * (1.0 - lb) * sg * (1.0 - sg)).astype(BF16)
            dz_ref[2, rs, hs] = dv.astype(BF16)
            dz_ref[3, rs, hs] = dg.astype(BF16)

        lax.fori_loop(0, ncb, chunk, 0, unroll=4)

    W = 128 * HPB
    zb = lambda k: pl.BlockSpec((None, tb, W), lambda h, t: (k, nt - 1 - t, h))
    blk = pl.BlockSpec((tb, W), lambda h, t: (nt - 1 - t, h))
    acc = pl.BlockSpec((8, W), lambda h, t: (0, h))
    return pl.pallas_call(
        body, name="hgrn_bwd", grid=(HEADS // HPB, nt),
        in_specs=[zb(0), zb(1), zb(2), zb(3), blk, blk,
                  pl.BlockSpec((HPB, ncb, 128, 128), lambda h, t: (h, nt - 1 - t, 0, 0)),
                  pl.BlockSpec((2, W), lambda h, t: (0, h)), pl.BlockSpec((1, W), lambda h, t: (0, h))],
        out_specs=[pl.BlockSpec((4, tb, W), lambda h, t: (0, nt - 1 - t, h)), acc, acc],
        out_shape=[jax.ShapeDtypeStruct((4, T, D), BF16), jax.ShapeDtypeStruct((8, D), F32),
                   jax.ShapeDtypeStruct((8, D), F32)],
        scratch_shapes=[pltpu.VMEM((HPB, 128, 128), F32)],
        compiler_params=_params(48, 2),
    )(*_hbm(z4, z4, z4, z4, o_raw, dy, states, hg_lb, gnorm))


def _adamw(w, g, m, v, *, name, with_g=False):
    R, L = w.shape
    tr = R if R <= 512 else 512
    assert R % tr == 0
    blk = pl.BlockSpec((tr, L), lambda i: (i, 0))
    c1, c2 = 1.0 - B1 ** STEP, 1.0 - B2 ** STEP
    n_out = 4 if with_g else 3

    def body(w_ref, g_ref, m_ref, v_ref, *o_refs):
        d_ref, mo_ref, vo_ref = o_refs[-3:]
        g_ = g_ref[...]
        m_ = B1 * m_ref[...] + (1.0 - B1) * g_
        v_ = B2 * v_ref[...] + (1.0 - B2) * (g_ * g_)
        if with_g:
            o_refs[0][...] = g_
        d_ref[...] = -LR * ((m_ / c1) / (jnp.sqrt(v_ / c2) + ADAM_EPS) + WD * w_ref[...])
        mo_ref[...] = m_
        vo_ref[...] = v_

    sds = jax.ShapeDtypeStruct((R, L), F32)
    return pl.pallas_call(
        body, name=name, grid=(R // tr,), in_specs=[blk] * 4, out_specs=[blk] * n_out, out_shape=[sds] * n_out,
        compiler_params=_params(32, 1),
    )(*_hbm(w, g, m, v))


def _adamw_rows(w, m, v, gbufs, row0, *, name, plan=None):
    L, R, C = w.shape
    tr = 256
    assert R % tr == 0 and row0 % tr == 0 and len(gbufs) == L
    grid = (L, R // tr)
    blk = pl.BlockSpec((None, tr, C), lambda l, i: (l, i, 0))
    gblks = [pl.BlockSpec((tr, C), lambda l, i, k=k: (row0 // tr + jnp.where(l == k, i, 0), 0)) for k in range(L)]
    c1, c2 = 1.0 - B1 ** STEP, 1.0 - B2 ** STEP

    def body(*refs):
        ins, (go_ref, d_ref, mo_ref, vo_ref), _, pctx = _split_refs(refs, 3 + L, 4, 0, plan)
        w_ref, m_ref, v_ref = ins[:3]
        g_refs = ins[3:]
        _plan_start(plan, pctx, grid)
        g_ = g_refs[0][...]
        for l in range(1, L):
            g_ = jnp.where(pl.program_id(0) == l, g_refs[l][...], g_)
        m_ = B1 * m_ref[...] + (1.0 - B1) * g_
        v_ = B2 * v_ref[...] + (1.0 - B2) * (g_ * g_)
        go_ref[...] = g_
        d_ref[...] = -LR * ((m_ / c1) / (jnp.sqrt(v_ / c2) + ADAM_EPS) + WD * w_ref[...])
        mo_ref[...] = m_
        vo_ref[...] = v_
        _plan_wait(plan, pctx, grid)

    sds = jax.ShapeDtypeStruct((L, R, C), F32)
    p_in, p_ospec, p_oshape, p_scr, p_alias = _plan_io(plan, 3 + L, 4)
    return pl.pallas_call(
        body, name=name, grid=grid, in_specs=[blk] * 3 + gblks + [_ANY] * len(p_in),
        out_specs=[blk] * 4 + p_ospec, out_shape=[sds] * 4 + p_oshape, scratch_shapes=p_scr,
        input_output_aliases=p_alias, compiler_params=_params(32, 2),
    )(*_hbm(w, m, v, *gbufs), *p_in)


def _add_pairs(g, theirs, ids, *, name):
    n, R, L = theirs.shape
    tr = math.gcd(R, 128)
    nb = R // tr

    def body(ids_ref, a_ref, b_ref, o_ref):
        o_ref[...] = (a_ref[...].astype(F32) + b_ref[...].astype(F32)).astype(BF16)

    blk = pl.BlockSpec((n, tr, L), lambda i, ids: (0, i, 0))
    return pl.pallas_call(
        body, name=name, out_shape=jax.ShapeDtypeStruct((n, R, L), BF16),
        grid_spec=pltpu.PrefetchScalarGridSpec(
            num_scalar_prefetch=1, grid=(nb,),
            in_specs=[pl.BlockSpec((n, tr, L), lambda i, ids: (0, ids[1] * nb + i, 0)), blk], out_specs=blk),
        compiler_params=_params(16, 1),
    )(ids, *_hbm(g, theirs))


def _sum_chips(pair, parts, ids, *, name):
    _, R, L = parts.shape
    tr = math.gcd(R, 128)

    def body(ids_ref, o_ref, r_ref, out_ref):
        out_ref[...] = ((o_ref[...].astype(F32) + r_ref[0].astype(F32)) + r_ref[1].astype(F32)) + r_ref[2].astype(F32)

    return pl.pallas_call(
        body, name=name, out_shape=jax.ShapeDtypeStruct((2, R, L), F32),
        grid_spec=pltpu.PrefetchScalarGridSpec(
            num_scalar_prefetch=1, grid=(R // tr,),
            in_specs=[pl.BlockSpec((None, tr, L), lambda i, ids: (ids[0], i, 0)),
                      pl.BlockSpec((3, tr, L), lambda i, ids: (0, i, 0))],
            out_specs=pl.BlockSpec((None, tr, L), lambda i, ids: (ids[1], i, 0))),
        compiler_params=_params(32, 1),
    )(ids, *_hbm(pair, parts))


def _mesh_ids():
    x, y, c = _mesh_pos()
    return jnp.stack([2 * x + y, c]).astype(jnp.int32)


def _place_shard(rows, ids, *, name):
    R, L = rows.shape
    tr = 128

    def body(ids_ref, in_ref, out_ref):
        out_ref[...] = in_ref[...].astype(BF16)

    return pl.pallas_call(
        body, name=name, out_shape=jax.ShapeDtypeStruct((4, R, L), BF16),
        grid_spec=pltpu.PrefetchScalarGridSpec(
            num_scalar_prefetch=1, grid=(R // tr,), in_specs=[pl.BlockSpec((tr, L), lambda i, ids: (i, 0))],
            out_specs=pl.BlockSpec((None, tr, L), lambda i, ids: (ids[0], i, 0))),
        compiler_params=_params(16, 1),
    )(ids, *_hbm(rows))


IN_E_SHARD, IN_E_LAYOUT = 392, 1664


def _in_e_runs():
    runs = []
    for lo, hi, dst in ((0, 512, 0), (512, 544, 1536), (544, 1568, 512)):
        while lo < hi:
            j = lo // IN_E_SHARD
            wd = min(hi, IN_E_SHARD * (j + 1)) - lo
            runs.append((j, lo - IN_E_SHARD * j, dst, wd))
            lo, dst = lo + wd, dst + wd
    return runs


def _in_e_to_layout(shards):
    tr = 256

    def body(s_ref, o_ref, t_ref):
        t_ref[...] = jnp.zeros_like(t_ref)
        for j in range(4):
            s = s_ref[j].astype(F32)
            for _, c, dst, wd in (r for r in _in_e_runs() if r[0] == j):
                t_ref[:, dst:dst + wd] = s[:, c:c + wd]
        o_ref[...] = t_ref[...].astype(BF16)

    return pl.pallas_call(
        body, name="in_e_to_layout", out_shape=jax.ShapeDtypeStruct((D, IN_E_LAYOUT), BF16), grid=(D // tr,),
        in_specs=[pl.BlockSpec((4, tr, IN_E_SHARD), lambda i: (0, i, 0))],
        out_specs=pl.BlockSpec((tr, IN_E_LAYOUT), lambda i: (i, 0)),
        scratch_shapes=[pltpu.VMEM((tr, IN_E_LAYOUT), F32)], compiler_params=_params(16, 1),
    )(*_hbm(shards))


def _in_e_from_layout(g):
    tr = 256

    def body(g_ref, o_ref, t_ref):
        g_ = g_ref[...]
        for j, c, src, wd in _in_e_runs():
            t_ref[j, :, c:c + wd] = g_[:, src:src + wd]
        o_ref[...] = t_ref[...].astype(BF16)

    return pl.pallas_call(
        body, name="in_e_from_layout", out_shape=jax.ShapeDtypeStruct((4, D, IN_E_SHARD), BF16), grid=(D // tr,),
        in_specs=[pl.BlockSpec((tr, IN_E_LAYOUT), lambda i: (i, 0))],
        out_specs=pl.BlockSpec((4, tr, IN_E_SHARD), lambda i: (0, i, 0)),
        scratch_shapes=[pltpu.VMEM((4, tr, IN_E_SHARD), F32)], compiler_params=_params(16, 1),
    )(*_hbm(g))


def _place_weights(pieces, buffer_rows, ids, *, plan=None):
    tr = 256
    steps, s = [], 0
    for arr, layer, buf, row0 in pieces:
        nblk = arr.shape[1] // tr
        steps.append((s, nblk))
        s += nblk
    total = s
    buf_start = [min(st for (st, _), p in zip(steps, pieces) if p[2] == k) for k in range(len(buffer_rows))]
    grid = (total,)
    n_in = len(pieces)

    def body(*refs):
        ins, outs, _, pctx = _split_refs(refs[1:], n_in, len(buffer_rows), 0, plan)
        _plan_start(plan, pctx, grid)
        i = pl.program_id(0)
        for (st, nblk), (_, _, buf, _), ref in zip(steps, pieces, ins):
            @pl.when((i >= st) & (i < st + nblk))
            def _(ref=ref, buf=buf):
                outs[buf][...] = ref[...].astype(BF16)
        _plan_wait(plan, pctx, grid)

    in_specs = [pl.BlockSpec((None, tr, D), lambda i, ids, layer=layer, st=st, nblk=nblk:
                             (layer, jnp.clip(i - st, 0, nblk - 1), 0))
                for (st, nblk), (_, layer, _, _) in zip(steps, pieces)]
    out_specs = [pl.BlockSpec((None, tr, D), lambda i, ids, st=st, nb=rows // tr: (ids[0], jnp.clip(i - st, 0, nb - 1), 0))
                 for st, rows in zip(buf_start, buffer_rows)]
    p_in, p_ospec, p_oshape, p_scr, p_alias = _plan_io(plan, 1 + n_in, len(buffer_rows))
    return pl.pallas_call(
        body, name="place_weights",
        out_shape=[jax.ShapeDtypeStruct((4, rows, D), BF16) for rows in buffer_rows] + p_oshape,
        grid_spec=pltpu.PrefetchScalarGridSpec(
            num_scalar_prefetch=1, grid=grid, in_specs=in_specs + [_ANY] * len(p_in), out_specs=out_specs + p_ospec,
            scratch_shapes=p_scr),
        input_output_aliases=p_alias, compiler_params=_params(16, 1),
    )(ids, *_hbm(*[p[0] for p in pieces]), *p_in)


def _remote(src, dst, send_sem, recv_sem, to):
    return pltpu.make_async_remote_copy(src_ref=src, dst_ref=dst, send_sem=send_sem, recv_sem=recv_sem,
                                        device_id=to, device_id_type=MESH_IDS)


def _rows(ref, lead, start, size):
    return ref.at[tuple(pl.ds(0, n) for n in ref.shape[:lead]) + (pl.ds(start, size),)]


def _other_chips():
    x, y, _ = _mesh_pos()
    return [(1 - x, y), (x, 1 - y), (1 - x, 1 - y)]


def _plan_gather_ici(bufs):
    n = len(bufs)

    def copies(outs, send, recv):
        x, y, c = _mesh_pos()
        res = []
        for b in range(n):
            half = bufs[b].shape[1] // 2
            mine = _rows(outs[b].at[2 * x + y], 0, c * half, half)
            for j, (cx, cy) in enumerate(_other_chips()):
                res.append((_remote(mine, mine, send(3 * b + j), recv(3 * b + j), (cx, cy, c)),
                            _remote(mine, _rows(outs[b].at[2 * cx + cy], 0, c * half, half),
                                    send(3 * b + j), recv(3 * b + j), (x, y, c))))
        return res

    def start(ins, outs, send, recv, loc):
        for out_cp, _ in copies(outs, send, recv):
            out_cp.start()

    def wait(ins, outs, send, recv, loc):
        for out_cp, in_cp in copies(outs, send, recv):
            in_cp.wait_recv()
            out_cp.wait_send()

    outs = [jax.ShapeDtypeStruct(b.shape, b.dtype) for b in bufs]
    return _Plan(bufs, outs, 3 * n, 0, start, wait, aliases={b: b for b in range(n)})


def _plan_gather_forward(bufs):
    n = len(bufs)

    def copies(outs, send, recv):
        x, y, c = _mesh_pos()
        res = []
        for b in range(n):
            half = bufs[b].shape[1] // 2
            for j, (cx, cy) in enumerate(_other_chips()):
                slot = outs[b].at[2 * cx + cy]
                res.append((_remote(_rows(slot, 0, c * half, half), _rows(slot, 0, c * half, half),
                                    send(3 * b + j), recv(3 * b + j), (x, y, 1 - c)),
                            _remote(_rows(slot, 0, c * half, half), _rows(slot, 0, (1 - c) * half, half),
                                    send(3 * b + j), recv(3 * b + j), (x, y, c))))
        return res

    def start(ins, outs, send, recv, loc):
        for out_cp, _ in copies(outs, send, recv):
            out_cp.start()

    def wait(ins, outs, send, recv, loc):
        for out_cp, in_cp in copies(outs, send, recv):
            in_cp.wait_recv()
            out_cp.wait_send()

    outs = [jax.ShapeDtypeStruct(b.shape, b.dtype) for b in bufs]
    return _Plan(bufs, outs, 3 * n, 0, start, wait, aliases={b: b for b in range(n)})


def _plan_pair_swap(g):
    half = g.shape[1] // 2

    def copy(ins, outs, send, recv, loc):
        x, y, c = _mesh_pos()
        return _remote(_rows(ins[0], 1, (1 - c) * half, half), outs[0], send(0), recv(0), (x, y, 1 - c))

    return _Plan([g], [jax.ShapeDtypeStruct((4, half, g.shape[2]), g.dtype)], 1, 0,
                 lambda *a: copy(*a).start(), lambda *a: copy(*a).wait())


def _plan_pair_gather(buf):
    def copies(ins, outs, send, recv, loc):
        x, y, c = _mesh_pos()
        return (_remote(outs[0].at[c], outs[0].at[c], send(0), recv(0), (x, y, 1 - c)),
                _remote(outs[0].at[c], outs[0].at[1 - c], send(0), recv(0), (x, y, c)))

    def wait(*a):
        out_cp, in_cp = copies(*a)
        in_cp.wait_recv()
        out_cp.wait_send()

    return _Plan([buf], [jax.ShapeDtypeStruct(buf.shape, buf.dtype)], 1, 0, lambda *a: copies(*a)[0].start(), wait,
                 aliases={0: 0})


def _plan_chip_scatter(p):
    def copies(ins, outs, send, recv, loc):
        _, _, c = _mesh_pos()
        return [_remote(ins[0].at[2 * cx + cy], outs[0].at[j], send(j), recv(j), (cx, cy, c))
                for j, (cx, cy) in enumerate(_other_chips())]

    def start(*a):
        for cp in copies(*a):
            cp.start()

    def wait(*a):
        for cp in copies(*a):
            cp.wait()

    return _Plan([p], [jax.ShapeDtypeStruct((3,) + p.shape[1:], p.dtype)], 3, 0, start, wait)


def _plan_exchange_all(vec):
    def copies(ins, outs, send, recv, loc):
        x, y, c = _mesh_pos()
        return [_remote(ins[0], outs[0].at[r - 1], send(r - 1), recv(r - 1), (x ^ (r >> 2), y ^ ((r >> 1) & 1), c ^ (r & 1)))
                for r in range(1, 8)]

    def start(*a):
        for cp in copies(*a):
            cp.start()

    def wait(*a):
        for cp in copies(*a):
            cp.wait()

    return _Plan([vec], [jax.ShapeDtypeStruct((7,) + vec.shape, vec.dtype)], 7, 0, start, wait)


SMALL_LAYOUT = {
    "mla_gq": (0, 1, 256, (1, 256)), "mla_gkv": (1, 1, 256, (1, 256)), "sgu_ln_g": (2, 1, 512, (1, 512)),
    "sgu_ln_b": (3, 1, 512, (1, 512)), "sgu_w": (4, 64, 1024, (64, 1024)), "sgu_b": (68, 1, 512, (1, 512)),
    "hg_lb": (69, 2, 1024, (2, 1024)), "hg_gnorm": (71, 1, 1024, (1, 256)), "ln1_g": (72, 2, 1024, (2, 1024)),
    "ln1_b": (74, 2, 1024, (2, 1024)), "ln2_g": (76, 2, 1024, (2, 1024)), "ln2_b": (78, 2, 1024, (2, 1024)),
}


def _small_pack(dgq, dgkv, dslg, dslb, dsw, dsb, dlb, dgn, ln_parts, sq_err):
    flat_ln = [p for pair in ln_parts for p in pair]

    def body(*refs):
        gq_ref, gkv_ref, slg_ref, slb_ref, sw_ref, sb_ref, lb_ref, gn_ref = refs[:8]
        ln_refs, err_ref, out_ref, t_sc = refs[8:16], refs[16], refs[17], refs[18]
        s8 = lambda ref: jnp.sum(ref[...], axis=0, keepdims=True)
        out_ref[...] = jnp.zeros_like(out_ref)
        out_ref[0:1, 0:256] = s8(gq_ref)
        out_ref[1:2, 0:256] = s8(gkv_ref)
        out_ref[2:3, 0:512] = s8(slg_ref)
        out_ref[3:4, 0:512] = s8(slb_ref)
        out_ref[4:68, :] = sw_ref[...]
        t_sc[...] = sb_ref[...].T
        for g in range(SGU_G):
            out_ref[68:69, g * SGU_C:(g + 1) * SGU_C] = t_sc[g:g + 1, :]
        d_lb1 = s8(lb_ref)
        out_ref[69:70, :] = -d_lb1
        out_ref[70:71, :] = d_lb1
        out_ref[71:72, :] = s8(gn_ref)
        for k, ref in enumerate(ln_refs):
            out_ref[72 + k:73 + k, :] = s8(ref)
        out_ref[0:1, 1023:1024] = jnp.sum(s8(err_ref), axis=1, keepdims=True) * (0.5 / D)

    vm = pl.BlockSpec(memory_space=pltpu.VMEM)
    return pl.pallas_call(
        body, name="small_grad_pack", in_specs=[vm] * 17, out_specs=vm,
        out_shape=jax.ShapeDtypeStruct((SMALL_ROWS, 1024), F32), scratch_shapes=[pltpu.VMEM((SGU_C, SGU_C), F32)],
        compiler_params=_params(16),
    )(dgq, dgkv, dslg, dslb, dsw.reshape(64, 1024), dsb, dlb, dgn, *flat_ln, sq_err)


def _small_update(vec, others, ids, w, m, v):
    names = list(SMALL_LAYOUT)
    n = len(names)
    c1, c2 = 1.0 - B1 ** STEP, 1.0 - B2 ** STEP
    have_others = others is not None

    def body(*refs):
        ids_ref, v_ref = refs[0], refs[1]
        k = 2 + have_others
        w_refs, m_refs, v_refs = refs[k:k + n], refs[k + n:k + 2 * n], refs[k + 2 * n:k + 3 * n]
        outs = refs[k + 3 * n:]
        row0_ref, tot_sc = outs[0], outs[-1]
        total = v_ref[...]
        if have_others:
            me = 2 * ids_ref[0] + ids_ref[1]
            total = None
            for d in range(8):
                rel = d ^ me
                term = jnp.where(rel == 0, v_ref[...], refs[2][jnp.maximum(rel - 1, 0)])
                total = term if total is None else total + term
        tot_sc[...] = total
        row0_ref[...] = tot_sc[0:1, :]
        for i, name in enumerate(names):
            r0, nr, width, _ = SMALL_LAYOUT[name]
            if name == "hg_gnorm":
                g_ = tot_sc[r0:r0 + 1, 0:256]
                for chip in range(1, 4):
                    g_ = jnp.where(ids_ref[0] == chip, tot_sc[r0:r0 + 1, chip * 256:(chip + 1) * 256], g_)
            else:
                g_ = tot_sc[r0:r0 + nr, 0:width]
            m_ = B1 * m_refs[i][...] + (1.0 - B1) * g_
            v_ = B2 * v_refs[i][...] + (1.0 - B2) * (g_ * g_)
            go, do, mo, vo = outs[1 + 4 * i:5 + 4 * i]
            go[...] = g_
            do[...] = -LR * ((m_ / c1) / (jnp.sqrt(v_ / c2) + ADAM_EPS) + WD * w_refs[i][...])
            mo[...] = m_
            vo[...] = v_

    full = lambda shape: pl.BlockSpec(shape, lambda i, ids, nd=len(shape): (0,) * nd)
    kshapes = [SMALL_LAYOUT[name][3] for name in names]
    operands = [vec] + ([others] if have_others else []) + [d[name] for d in (w, m, v) for name in names]
    out_shapes = [jax.ShapeDtypeStruct((1, 1024), F32)] + [jax.ShapeDtypeStruct(s, F32) for s in kshapes for _ in range(4)]
    res = pl.pallas_call(
        body, name="small_update", out_shape=out_shapes,
        grid_spec=pltpu.PrefetchScalarGridSpec(
            num_scalar_prefetch=1, grid=(1,), in_specs=[full(o.shape) for o in operands],
            out_specs=[full(s.shape) for s in out_shapes],
            scratch_shapes=[pltpu.VMEM((SMALL_ROWS, 1024), F32)]),
        compiler_params=_params(32, 1),
    )(ids, *operands)
    return res[0], {name: tuple(res[1 + 4 * i:5 + 4 * i]) for i, name in enumerate(names)}


ROWS_L1, ROWS_L0, ROWS_ODD_W = 3328, 2048, 384
ODD_PARTS = (("w_out_e", (256, 1024)), ("w_in_e", (1024, 392)), ("w_qb", (256, 192)), ("w_kvb", (256, 256)))
ODD_W_PARTS = tuple(p for p in ODD_PARTS if p[0] != "w_in_e")


def _odd_rows(parts, dtype, layout, total, gnorm=None):
    rows = [parts[n].reshape(-1, 1024).astype(dtype) for n, _ in layout]
    used = sum(r.shape[0] for r in rows)
    if gnorm is not None:
        bits = lax.bitcast_convert_type(gnorm.reshape(-1), BF16).reshape(1, 512)
        rows.append(jnp.pad(bits, ((0, 15), (0, 512))))
        used += 16
    if total > used:
        rows.append(jnp.zeros((total - used, 1024), dtype))
    return jnp.concatenate(rows, axis=0)


def _odd_unrows(buf, layout, with_gnorm=False):
    out, off = {}, 0
    for n, shape in layout:
        nr = math.prod(shape) // 1024
        out[n] = buf[off:off + nr].reshape(shape)
        off += nr
    if with_gnorm:
        out["hg_gnorm"] = lax.bitcast_convert_type(buf[off, :512].reshape(256, 2), F32).reshape(1, 256)
    return out


def _rope_tables(positions):
    half = ROPE // 2
    inv_freq = ROPE_BASE ** (-jnp.arange(half, dtype=F32) / half)
    per_row = 128 // half
    ang = jnp.repeat(positions.astype(F32).reshape(-1, per_row), half, axis=1) * jnp.tile(inv_freq, per_row)
    cos, sin = jnp.cos(ang).reshape(-1, half), jnp.sin(ang).reshape(-1, half)
    T = cos.shape[0]
    one, z16, z32 = jnp.ones((T, NOPE), F32), jnp.zeros((T, half), F32), jnp.zeros((T, 32), F32)
    z64 = jnp.zeros((T, NOPE), F32)
    c = jnp.concatenate([one, cos, cos, z32], axis=1)
    s1 = jnp.concatenate([z64, -sin, z16, z32], axis=1)
    s2 = jnp.concatenate([z64, z16, sin, z32], axis=1)
    return c, s1, s2


def _local_step(x, positions, tgt, odd, bufs, P, exchange):
    T = x.shape[0]
    row = lambda a: a.reshape(1, -1)
    rc, rs1, rs2 = _rope_tables(positions)
    blk = lambda f: pl.BlockSpec((None, D, D), f)

    w_in = _in_e_to_layout(odd["w_in_e"])
    wq = jnp.pad(odd["w_qb"].reshape(256, HEADS, NOPE + ROPE), ((0, 0), (0, 0), (0, 32))).reshape(256, HEADS * 128)
    kvb = odd["w_kvb"].reshape(256, HEADS, NOPE + VDIM)
    wk = jnp.pad(kvb[:, :, :NOPE], ((0, 0), (0, 0), (0, 64))).reshape(256, HEADS * 128)
    wv = kvb[:, :, NOPE:].reshape(256, HEADS * VDIM)
    w_out_e = odd["w_out_e"]
    sgu_w = P["sgu_w"][0]
    sgu_bt = P["sgu_b"][0].T
    gq, gkv = P["mla_gq"], P["mla_gkv"]
    gnorm = P["hg_gnorm"]

    z0 = _matmul(x, w_in, name="in_proj_e", M=T, N=1664, K=D, tn=1664)[0]
    q, k, v = _mla_prep(z0, gq, gkv, wq, wk, wv, rc, rs1, rs2)
    if exchange:
        ids = _mesh_ids()
        placed = list(bufs)
        a_out, lse, wga, wgb = _flash_fwd(q, k, v, plan=_plan_gather_ici(placed[:2]))
    else:
        a_out, lse = _flash_fwd(q, k, v)
        wga, wgb, wgc = bufs
    mix0 = _sgu_fwd(z0, a_out, P["sgu_ln_g"], P["sgu_ln_b"], sgu_w, sgu_bt)
    res = _proj_ln(mix0, w_out_e, x, row(P["ln1_g"][0]), row(P["ln1_b"][0]), name="out_proj_ln_e",
                   plan=_plan_gather_forward([wga, wgb]) if exchange else None)
    r1, h1b = res[:2]
    if exchange:
        wga, wgb = res[2:]
    ln = lambda name, l: (row(P[name + "_g"][l]), row(P[name + "_b"][l]))
    res = _ffn_ln(h1b, wga, r1, *ln("ln2", 0), name="ffn_ln_0", prev_ln=ln("ln1", 0),
                  plan=_plan_gather_ici(placed[2:]) if exchange else None)
    ra0, r2, h2b = res[:3]
    z4 = _matmul(h2b, wgb, name="in_proj_o", M=T, N=4 * D, K=D, tn=2 * D, n_slots=True,
                 b_spec=pl.BlockSpec((2, D, D), lambda i, j, k: (j, 0, 0)),
                 out_shape=jax.ShapeDtypeStruct((4, T, D), F32),
                 o_spec=pl.BlockSpec((2, min(MM_ROWS, T), D), lambda i, j, k: (j, i, 0)))[0]
    y1, o_raw, states = _hgrn_fwd(z4, P["hg_lb"], gnorm)
    res2 = _proj_ln(y1, wgb, r2, *ln("ln1", 1), name="out_proj_ln_o", prev_ln=ln("ln2", 0), w_rowblk=4,
                    plan=_plan_gather_forward([res[3]]) if exchange else None)
    r3, h3b = res2[:2]
    if exchange:
        wgc = res2[2]
    ra1, r4, _ = _ffn_ln(h3b, wgc, r3, *ln("ln2", 1), name="ffn_ln_1", prev_ln=ln("ln1", 1))

    ln1_g, ln1_b, ln2_g, ln2_b = [None, None], [None, None], [None, None], [None, None]
    sq_err_parts = []

    def ffn_bwd(l, dh, r_out, ra, h_mid_b, g2, wg, rows, plan=None, loss_head=()):
        dr, dr_b, dg, db, *sq_err = _ln_bwd(dh, r_out, row(g2), name=f"ln2_bwd_{l}", loss_head=loss_head)
        sq_err_parts.extend(sq_err)
        ln2_g[l], ln2_b[l] = dg, db
        da, *extra = _matmul(dr_b, wg, tb=True, mul=ra, out_dtype=BF16, name=f"ffn_da_{l}", M=T, N=4 * D, K=D, tn=2 * D,
                             b_spec=pl.BlockSpec((2, D, D), lambda i, j, k: (j, 1, 0)), n_slots=True, plan=plan)
        gbuf = _matmul(ra, dr_b, ta=True, a_sq=True, name=f"ffn_dw2_{l}", M=4 * D, N=D, K=T, tm=1024, tk=DW_TOKENS // 2,
                       out_shape=jax.ShapeDtypeStruct((4, rows, D), BF16), o_spec=blk(lambda i, j, k: (i, 1, 0)))[0]
        gbuf = _matmul(h_mid_b, da, ta=True, name=f"ffn_dw1_{l}", M=D, N=4 * D, K=T, tm=1024, tk=DW_TOKENS, into=gbuf,
                       out_shape=jax.ShapeDtypeStruct((4, rows, D), BF16), o_spec=blk(lambda i, j, k: (j, 0, 0)))[0]
        dh_mid = _matmul(da, wg, tb=True, add=dr, add_scale=ALPHA, name=f"ffn_dh_{l}", M=T, N=D, K=4 * D, tk=2 * D,
                         b_spec=pl.BlockSpec((2, D, D), lambda i, j, k: (k, 0, 0)))[0]
        return dh_mid, gbuf, extra

    dh3, g1, _ = ffn_bwd(1, None, r4, ra1, h3b, P["ln2_g"][1], wgc, ROWS_L1, loss_head=(row(P["ln2_b"][1]), tgt))
    loss_parts = sq_err_parts[0]
    dr3, dr3_b, dg, db = _ln_bwd(dh3, r3, row(P["ln1_g"][1]), name="ln1_bwd_1")
    ln1_g[1], ln1_b[1] = dg, db
    g1_sds = jax.ShapeDtypeStruct((4, ROWS_L1, D), BF16)
    g1 = _matmul(y1, dr3_b, ta=True, name="dw_out_o", M=D, N=D, K=T, tm=256, tk=DW_TOKENS, into=g1, out_shape=g1_sds,
                 o_spec=pl.BlockSpec((None, 256, D), lambda i, j, k: (i, 12, 0)))[0]
    dmix1 = _matmul(dr3_b, wgb, tb=True, name="dmix_o", M=T, N=D, K=D, b_spec=_rows4_spec(4, 3), b_merge=(D, D))[0]
    dz4, dlb, dgn = _hgrn_bwd(z4, o_raw, dmix1, states, P["hg_lb"], gnorm)
    g1 = _matmul(h2b, dz4, ta=True, name="dw_in_o", M=D, N=4 * D, K=T, tm=1024, tk=DW_TOKENS, into=g1, out_shape=g1_sds,
                 b_spec=pl.BlockSpec((None, min(DW_TOKENS, T), D), lambda i, j, k: (j, k, 0)),
                 o_spec=blk(lambda i, j, k: (j, 2, 0)))[0]
    dh2 = _matmul(dz4, wgb, tb=True, add=dr3, add_scale=ALPHA, name="dh_in_o", M=T, N=D, K=4 * D, tk=2 * D,
                  a_spec=pl.BlockSpec((2, min(MM_ROWS, T), D), lambda i, j, k: (k, i, 0)),
                  b_spec=pl.BlockSpec((2, D, D), lambda i, j, k: (k, 0, 0)))[0]

    dh1, g0, swapped1 = ffn_bwd(0, dh2, r2, ra0, h1b, P["ln2_g"][0], wga, ROWS_L0,
                                plan=_plan_pair_swap(g1) if exchange else None)
    dr1, dr1_b, dg, db = _ln_bwd(dh1, r1, row(P["ln1_g"][0]), name="ln1_bwd_0")
    ln1_g[0], ln1_b[0] = dg, db
    godd = {"w_out_e": _matmul(mix0, dr1_b, ta=True, name="dw_out_e", M=D, N=D, K=T, tm=1024, tk=DW_TOKENS)[0]}
    dmix0, *swapped0 = _matmul(dr1_b, w_out_e, tb=True, name="dmix_e", M=T, N=D, K=D,
                               plan=_plan_pair_swap(g0) if exchange else None)
    delta, do_b = _attn_delta(dmix0, a_out)
    if exchange:
        pair1 = _add_pairs(g1, swapped1[0], ids, name="grad_pair_add_1")
        pair0 = _add_pairs(g0, swapped0[0], ids, name="grad_pair_add_0")
        dq4, dk, dv, parts0, parts1 = _flash_bwd(
            q, k, v, do_b, lse, delta, plan=_join_plans([_plan_chip_scatter(pair0), _plan_chip_scatter(pair1)]))
        half0 = _sum_chips(pair0, parts0, ids, name="grad_chip_sum_0")
        half1 = _sum_chips(pair1, parts1, ids, name="grad_chip_sum_1")
        dc, dkr, dwq, dwk, dwv, dgq, dgkv, g0, g1 = _mla_bwd(
            z0, dq4, dk, dv, gq, gkv, wq, wk, wv, rc, rs1, rs2,
            plan=_join_plans([_plan_pair_gather(half0), _plan_pair_gather(half1)]))
        g0, g1 = g0.reshape(ROWS_L0, D), g1.reshape(ROWS_L1, D)
    else:
        dq4, dk, dv = _flash_bwd(q, k, v, do_b, lse, delta)
        dc, dkr, dwq, dwk, dwv, dgq, dgkv = _mla_bwd(z0, dq4, dk, dv, gq, gkv, wq, wk, wv, rc, rs1, rs2)
    godd["w_qb"] = dwq.reshape(256, HEADS, 128)[:, :, :NOPE + ROPE].reshape(256, HEADS * (NOPE + ROPE))
    godd["w_kvb"] = jnp.concatenate([dwk.reshape(256, HEADS, 128)[:, :, :NOPE], dwv.reshape(256, HEADS, VDIM)],
                                    axis=2).reshape(256, HEADS * (NOPE + VDIM))
    swap_b = None
    if exchange:
        by_chip = [_odd_rows({"w_out_e": jnp.split(godd["w_out_e"], 4, axis=0)[j],
                              **{n: jnp.split(godd[n], 4, axis=1)[j] for n in ("w_qb", "w_kvb")}}, BF16,
                             ODD_W_PARTS, ROWS_ODD_W)
                   for j in range(4)]
        odd_b = jnp.stack(by_chip)
        swap_b = _plan_pair_swap(odd_b)
    dz0, dsw, dsb, dslg, dslb, *theirs_b = _sgu_bwd(z0, dmix0, dc, dkr, P["sgu_ln_g"], P["sgu_ln_b"], sgu_w, sgu_bt,
                                                    plan=swap_b)
    small_vec = _small_pack(dgq, dgkv, dslg, dslb, dsw, dsb, dlb, dgn, [ln1_g, ln1_b, ln2_g, ln2_b], loss_parts)
    plan_in = None
    if exchange:
        pair_b = _add_pairs(odd_b, theirs_b[0], ids, name="odd_pair_add_1")
        plan_in = _join_plans([_plan_exchange_all(small_vec), _plan_chip_scatter(pair_b)])
    grad_x, = _matmul(dz0, w_in, tb=True, add=dr1, add_scale=ALPHA, name="dx", M=T, N=D, K=1664, tk=1664)
    dw_in, *carried = _matmul(x, dz0, ta=True, name="dw_in_e", M=D, N=1664, K=T, tm=1024, tn=1664, tk=DW_TOKENS // 4,
                              plan=plan_in)
    odd_a = godd["w_in_e"] = _in_e_from_layout(dw_in)
    if exchange:
        small_others, parts_b = carried
        theirs_a = _run_plan(_plan_pair_swap(odd_a), name="odd_pair_swap")[0]
        pair_a = _add_pairs(odd_a, theirs_a, ids, name="odd_pair_add_0")
        parts_a = _run_plan(_plan_chip_scatter(pair_a), name="odd_chip_scatter")
        godd = ([pair_a, pair_b], [parts_a[0], parts_b])
        return grad_x, g0, g1, godd, small_vec, small_others
    return grad_x, g0, g1, godd, small_vec, None


WEIGHTS = ['w_in_e', 'mla_gq', 'mla_gkv', 'w_qb', 'w_kvb', 'sgu_ln_g', 'sgu_ln_b', 'sgu_w', 'sgu_b', 'w_out_e',
           'w_in_o', 'hg_lb', 'hg_gnorm', 'w_out_o', 'ln1_g', 'ln1_b', 'w_ff1', 'w_ff2', 'ln2_g', 'ln2_b']


def kernel(x, positions, w_in_e, mla_gq, mla_gkv, w_qb, w_kvb, sgu_ln_g, sgu_ln_b, sgu_w, sgu_b, w_out_e, w_in_o, hg_lb, hg_gnorm, w_out_o, ln1_g, ln1_b, w_ff1, w_ff2, ln2_g, ln2_b, loss_target, m_w_in_e, m_mla_gq, m_mla_gkv, m_w_qb, m_w_kvb, m_sgu_ln_g, m_sgu_ln_b, m_sgu_w, m_sgu_b, m_w_out_e, m_w_in_o, m_hg_lb, m_hg_gnorm, m_w_out_o, m_ln1_g, m_ln1_b, m_w_ff1, m_w_ff2, m_ln2_g, m_ln2_b, v_w_in_e, v_mla_gq, v_mla_gkv, v_w_qb, v_w_kvb, v_sgu_ln_g, v_sgu_ln_b, v_sgu_w, v_sgu_b, v_w_out_e, v_w_in_o, v_hg_lb, v_hg_gnorm, v_w_out_o, v_ln1_g, v_ln1_b, v_w_ff1, v_w_ff2, v_ln2_g, v_ln2_b):
    args = dict(locals())
    w = {n: args[n] for n in WEIGHTS}
    m = {n: args["m_" + n] for n in WEIGHTS}
    v = {n: args["v_" + n] for n in WEIGHTS}
    cx, cy, cc = _mesh_pos()
    chip = 2 * cx + cy

    odd_shard = _odd_rows({"w_out_e": w_out_e[0], "w_qb": w_qb[0], "w_kvb": w_kvb[0]}, BF16, ODD_W_PARTS, ROWS_ODD_W,
                          gnorm=hg_gnorm)
    ids = _mesh_ids()
    placed = [_place_shard(w_in_e[0], ids, name="place_shard_in_e"), _place_shard(odd_shard, ids, name="place_shard_odd")]
    pieces = [(w_ff1, 0, 0, 0), (w_ff2, 0, 0, 1024), (w_in_o, 0, 1, 0), (w_out_o, 0, 1, 1024),
              (w_ff1, 1, 2, 0), (w_ff2, 1, 2, 1024)]
    *big_bufs, odd_a, odd_b = _place_weights(pieces, (2048, 1280, 2048), ids, plan=_plan_gather_ici(placed))
    gathered = _run_plan(_plan_gather_forward([odd_a, odd_b]), name="odd_gather_forward")
    per_chip = [_odd_unrows(gathered[1][j], ODD_W_PARTS, with_gnorm=True) for j in range(4)]
    odd = {"w_out_e": jnp.concatenate([p["w_out_e"] for p in per_chip], axis=0),
           "w_in_e": gathered[0]}
    for n in ("w_qb", "w_kvb"):
        odd[n] = jnp.concatenate([p[n] for p in per_chip], axis=1)
    small = {n: w[n] for n in SMALL_LAYOUT if n != "hg_gnorm"}
    small["hg_gnorm"] = jnp.concatenate([p["hg_gnorm"] for p in per_chip], axis=1)
    grad_x, g_l0, g_l1, godd, small_vec, small_others = _local_step(
        x[0], positions[0], loss_target[0], odd, big_bufs, small, True)

    sums = [_sum_chips(pair, parts, ids, name=f"odd_chip_sum_{k}") for k, (pair, parts) in enumerate(zip(*godd))]
    g_in_e, g_rest = _run_plan(_join_plans([_plan_pair_gather(s) for s in sums]), name="odd_pair_gather")
    g_odd = _odd_unrows(g_rest.reshape(ROWS_ODD_W, 1024), ODD_W_PARTS)
    g_odd["w_in_e"] = g_in_e.reshape(D, 392)

    to_kernel = lambda d: {n: d[n].reshape(SMALL_LAYOUT[n][3]) for n in SMALL_LAYOUT}
    first_row, small_out = _small_update(small_vec, small_others, ids, to_kernel(w), to_kernel(m), to_kernel(v))
    loss = first_row[0, 1023]
    grads, delta, new_m, new_v = {}, {}, {}, {}
    for n, res in small_out.items():
        grads[n], delta[n], new_m[n], new_v[n] = (r.reshape(w[n].shape) for r in res)

    for n, bufs_, row0 in (("w_ff1", [g_l0, g_l1], 0), ("w_ff2", [g_l0, g_l1], 1024), ("w_in_o", [g_l1], 2048),
                           ("w_out_o", [g_l1], 3072)):
        grads[n], delta[n], new_m[n], new_v[n] = _adamw_rows(w[n], m[n], v[n], bufs_, row0, name=f"adamw_{n}")
    for n, _ in ODD_PARTS:
        if n == "w_in_e":
            res = _adamw(w[n][0].T, g_odd[n].T, m[n][0].T, v[n][0].T, name=f"adamw_{n}", with_g=True)
            grads[n], delta[n], new_m[n], new_v[n] = (r.T[None] for r in res)
            continue
        grads[n] = g_odd[n][None]
        d_, m_, v_ = _adamw(w[n][0], g_odd[n], m[n][0], v[n][0], name=f"adamw_{n}")
        delta[n], new_m[n], new_v[n] = d_[None], m_[None], v_[None]

    return (loss, grad_x[None], *[grads[n] for n in WEIGHTS], *[delta[n] for n in WEIGHTS],
            *[new_m[n] for n in WEIGHTS], *[new_v[n] for n in WEIGHTS])
```

```python
import math

import jax
import jax.numpy as jnp
from jax import lax
from jax.experimental import pallas as pl
from jax.experimental.pallas import tpu as pltpu

F32 = jnp.float32
BF16 = jnp.bfloat16
MESH_IDS = pl.DeviceIdType.MESH

D = 1024
DEPTH = 2
HEADS = 8
NOPE, ROPE, VDIM = 64, 32, 64
QK_SCALE = (NOPE + ROPE) ** -0.5
ROPE_BASE = 10000.0
SGU_G, SGU_C = 4, 128
HG_CHUNK = 64
HG_HEADS_PER_STEP = 8
ALPHA = (2 * DEPTH) ** 0.25
EPS = 1e-5
LR, B1, B2, ADAM_EPS, WD, STEP = 0.001, 0.9, 0.999, 1e-08, 0.01, 10
GELU_C = math.sqrt(2.0 / math.pi)
GELU_A = 0.044715
MB = 1024 * 1024
ROW_BLOCK = 512
SMALL_ROWS = 80

NT_DIMS = (((1,), (1,)), ((), ()))
TN_DIMS = (((0,), (0,)), ((), ()))


def _params(vmem_mb, n_axes=0):
    kw = dict(vmem_limit_bytes=vmem_mb * MB)
    if n_axes:
        kw["dimension_semantics"] = ("arbitrary",) * n_axes
    return pltpu.CompilerParams(**kw)


_ANY = pl.BlockSpec(memory_space=pltpu.HBM)


def _mesh_pos():
    return lax.axis_index("x"), lax.axis_index("y"), lax.axis_index("c")


def _hbm(*arrays):
    return tuple(pltpu.with_memory_space_constraint(a, pltpu.HBM) if a.size >= 2 ** 18 else a for a in arrays)


class _Plan:
    def __init__(self, ins, outs, n_remote, n_local, start, wait, aliases=None):
        self.ins, self.outs, self.n_remote, self.n_local = list(ins), list(outs), n_remote, n_local
        self.start, self.wait, self.aliases = start, wait, dict(aliases or {})


def _join_plans(plans):
    ins, outs, aliases, parts = [], [], {}, []
    nr = nl = 0
    for p in plans:
        parts.append((p, len(ins), len(outs), nr, nl))
        aliases.update({len(ins) + i: len(outs) + o for i, o in p.aliases.items()})
        ins += p.ins
        outs += p.outs
        nr += p.n_remote
        nl += p.n_local

    def run(which):
        def go(in_refs, out_refs, send, recv, loc):
            for p, i0, o0, r0, l0 in parts:
                getattr(p, which)(in_refs[i0:i0 + len(p.ins)], out_refs[o0:o0 + len(p.outs)],
                                  lambda i, r0=r0: send(r0 + i), lambda i, r0=r0: recv(r0 + i),
                                  lambda i, l0=l0: loc(l0 + i))
        return go

    return _Plan(ins, outs, nr, nl, run("start"), run("wait"), aliases)


def _plan_io(plan, n_in, n_out):
    if plan is None:
        return [], [], [], [], {}
    sems = [pltpu.SemaphoreType.DMA((max(plan.n_remote, 1),)), pltpu.SemaphoreType.DMA((max(plan.n_remote, 1),)),
            pltpu.SemaphoreType.DMA((max(plan.n_local, 1),))]
    aliases = {n_in + i: n_out + o for i, o in plan.aliases.items()}
    return plan.ins, [_ANY] * len(plan.outs), plan.outs, sems, aliases


def _split_refs(refs, n_in, n_out, n_scr, plan):
    p_in, p_out = (len(plan.ins), len(plan.outs)) if plan is not None else (0, 0)
    refs = list(refs)
    ins, refs = refs[:n_in], refs[n_in:]
    pins, refs = refs[:p_in], refs[p_in:]
    outs, refs = refs[:n_out], refs[n_out:]
    pouts, refs = refs[:p_out], refs[p_out:]
    scr, psem = refs[:n_scr], refs[n_scr:]
    psem = tuple((lambda i, s=s: s.at[i]) for s in psem)
    return ins, outs, scr, (pins, pouts, psem)


def _grid_edge(grid, last):
    cond = None
    for ax, n in enumerate(grid):
        c = pl.program_id(ax) == (n - 1 if last else 0)
        cond = c if cond is None else cond & c
    return cond


def _plan_start(plan, pctx, grid):
    if plan is not None:
        pins, pouts, psem = pctx
        pl.when(_grid_edge(grid, False))(lambda: plan.start(pins, pouts, *psem))


def _plan_wait(plan, pctx, grid):
    if plan is not None:
        pins, pouts, psem = pctx
        pl.when(_grid_edge(grid, True))(lambda: plan.wait(pins, pouts, *psem))


def _run_plan(plan, *, name):
    def body(*refs):
        _, _, _, (pins, pouts, psem) = _split_refs(refs, 0, 0, 0, plan)
        plan.start(pins, pouts, *psem)
        plan.wait(pins, pouts, *psem)

    p_in, p_ospec, p_oshape, p_scr, p_alias = _plan_io(plan, 0, 0)
    return pl.pallas_call(body, name=name, in_specs=[_ANY] * len(p_in), out_specs=p_ospec, out_shape=p_oshape,
                          scratch_shapes=p_scr, input_output_aliases=p_alias)(*p_in)


def _fold8(x):
    return x.reshape(x.shape[0] // 8, 8, x.shape[1]).sum(axis=0)


def _ln_stats(r):
    mu = jnp.mean(r, -1, keepdims=True)
    xc = r - mu
    rstd = lax.rsqrt(jnp.mean(xc * xc, -1, keepdims=True) + EPS)
    return xc * rstd, rstd


def _sigmoid(x):
    return jax.nn.sigmoid(x)


def _gelu(x):
    return 0.5 * x * (1.0 + jnp.tanh(GELU_C * (x + GELU_A * x * x * x)))


def _gelu_grad(x):
    t = jnp.tanh(GELU_C * (x + GELU_A * x * x * x))
    return 0.5 * (1.0 + t) + 0.5 * x * (1.0 - t * t) * GELU_C * (1.0 + 3.0 * GELU_A * x * x)


MM_ROWS = 1024
DW_TOKENS = 4096


def _matmul(a, b, *, name, M, N, K, ta=False, tb=False, out_dtype=F32, tm=MM_ROWS, tn=1024, tk=1024,
            a_spec=None, b_spec=None, b_merge=None, out_shape=None, o_spec=None, into=None,
            a_sq=False, mul=None, add=None, add_scale=1.0, n_slots=False, plan=None):
    assert not n_slots or (K // min(tk, K) == 1 and add is None)
    tm, tn, tk = min(tm, M), min(tn, N), min(tk, K)
    assert M % tm == 0 and N % tn == 0 and K % tk == 0
    grid = (M // tm, N // tn, K // tk)
    nk = grid[2]
    if a_spec is None:
        a_spec = pl.BlockSpec((tk, tm), lambda i, j, k: (k, i)) if ta else pl.BlockSpec((tm, tk), lambda i, j, k: (i, k))
    if b_spec is None:
        b_spec = pl.BlockSpec((tn, tk), lambda i, j, k: (j, k)) if tb else pl.BlockSpec((tk, tn), lambda i, j, k: (k, j))
    if o_spec is None:
        o_spec = pl.BlockSpec((tm, tn), lambda i, j, k: (i, j))
        out_shape = jax.ShapeDtypeStruct((M, N), out_dtype)
    e_spec = pl.BlockSpec((tm, tn), lambda i, j, k: (i, j))
    dims = (((0 if ta else 1,), (1 if tb else 0,)), ((), ()))
    extra = [e for e in (mul, add, into) if e is not None]
    n_in = 2 + len(extra)

    def body(*refs):
        ins, outs, scr, pctx = _split_refs(refs, n_in, 1, 1 if nk > 1 else 0, plan)
        a_ref, b_ref = ins[0], ins[1]
        rest = list(ins[2:])
        mul_ref = rest.pop(0) if mul is not None else None
        add_ref = rest.pop(0) if add is not None else None
        o_ref = outs[0]
        _plan_start(plan, pctx, grid)
        av = a_ref[...].astype(BF16)
        if a_sq:
            av = av * av
        bv = b_ref[...]
        if b_merge is not None:
            bv = bv.reshape(b_merge)
        if n_slots:
            for s in range(bv.shape[0]):
                r = lax.dot_general(av, bv[s], dims, preferred_element_type=F32)
                w = r.shape[1]
                if mul_ref is not None:
                    r = r * (2.0 * mul_ref[:, s * w:(s + 1) * w].astype(F32))
                if o_ref.ndim == 3:
                    o_ref[s] = r.astype(o_ref.dtype)
                else:
                    o_ref[:, s * w:(s + 1) * w] = r.astype(o_ref.dtype)
            _plan_wait(plan, pctx, grid)
            return
        if bv.ndim == 3:
            w = av.shape[-1] // (1 if av.ndim == 3 else bv.shape[0])
            a_parts = [av[s] if av.ndim == 3 else av[:, s * w:(s + 1) * w] for s in range(bv.shape[0])]
            p = sum(lax.dot_general(a_parts[s], bv[s], dims, preferred_element_type=F32) for s in range(bv.shape[0]))
        else:
            p = lax.dot_general(av, bv, dims, preferred_element_type=F32)

        def finish(r):
            if mul_ref is not None:
                r = r * (2.0 * mul_ref[...].astype(F32))
            if add_ref is not None:
                r = r + add_scale * add_ref[...]
            o_ref[...] = r.astype(o_ref.dtype)

        if nk == 1:
            finish(p)
        else:
            acc_ref = scr[0]
            k = pl.program_id(2)

            @pl.when(k == 0)
            def _():
                acc_ref[...] = p

            @pl.when(k > 0)
            def _():
                acc_ref[...] += p

            @pl.when(k == nk - 1)
            def _():
                finish(acc_ref[...])

        _plan_wait(plan, pctx, grid)

    p_in, p_ospec, p_oshape, p_scr, p_alias = _plan_io(plan, n_in, 1)
    aliases = dict(p_alias)
    if into is not None:
        aliases[n_in - 1] = 0
    return pl.pallas_call(
        body, name=name, grid=grid,
        in_specs=[a_spec, b_spec] + [e_spec] * (len(extra) - (into is not None)) + [_ANY] * (into is not None)
        + [_ANY] * len(p_in),
        out_specs=[o_spec] + p_ospec, out_shape=[out_shape] + p_oshape,
        scratch_shapes=([pltpu.VMEM((tm, tn), F32)] if nk > 1 else []) + p_scr,
        input_output_aliases=aliases, compiler_params=_params(48, 3),
    )(*_hbm(a, b, *extra), *p_in)


def _rows4_spec(rowblk, n_axes):
    return pl.BlockSpec((4, 256, D), lambda *_: (0, rowblk, 0))


def _residual(h_ref, prev_refs):
    if not prev_refs:
        return h_ref[...]
    xhat, _ = _ln_stats(h_ref[...])
    return xhat * prev_refs[0][...] + prev_refs[1][...]


def _proj_ln(a_b, w, h_prev, g, b, *, name, prev_ln=(), w_rowblk=None, plan=None):
    T = a_b.shape[0]
    tm = min(MM_ROWS, T)
    grid = (T // tm,)
    row = pl.BlockSpec((tm, D), lambda i: (i, 0))
    vec = pl.BlockSpec((1, D), lambda i: (0, 0))
    w_spec = pl.BlockSpec((D, D), lambda i: (0, 0)) if w_rowblk is None else _rows4_spec(w_rowblk, 1)
    n_in = 5 + len(prev_ln)

    def body(*refs):
        ins, (r_ref, hb_ref), _, pctx = _split_refs(refs, n_in, 2, 0, plan)
        a_ref, w_ref, h_ref, g_ref, b_ref = ins[:5]
        _plan_start(plan, pctx, grid)
        mix = jnp.dot(a_ref[...], w_ref[...].reshape(D, D), preferred_element_type=F32)
        r = ALPHA * _residual(h_ref, ins[5:]) + mix
        xhat, _ = _ln_stats(r)
        r_ref[...] = r
        hb_ref[...] = (xhat * g_ref[...] + b_ref[...]).astype(BF16)
        _plan_wait(plan, pctx, grid)

    p_in, p_ospec, p_oshape, p_scr, p_alias = _plan_io(plan, n_in, 2)
    return pl.pallas_call(
        body, name=name, grid=grid,
        in_specs=[row, w_spec, row, vec, vec] + [vec] * len(prev_ln) + [_ANY] * len(p_in),
        out_specs=[row, row] + p_ospec,
        out_shape=[jax.ShapeDtypeStruct((T, D), F32), jax.ShapeDtypeStruct((T, D), BF16)] + p_oshape,
        scratch_shapes=p_scr, input_output_aliases=p_alias, compiler_params=_params(40, 1),
    )(*_hbm(a_b, w, h_prev, g, b, *prev_ln), *p_in)


def _ffn_ln(h_b, wbuf, h, g, b, *, name, prev_ln=(), plan=None):
    T = h_b.shape[0]
    slots = 2
    tm, tf = min(ROW_BLOCK, T), slots * 1024
    nf = 4 // slots
    F = nf * tf
    grid = (T // tm, nf)
    row = pl.BlockSpec((tm, D), lambda i, j: (i, 0))
    vec = pl.BlockSpec((1, D), lambda i, j: (0, 0))
    n_in = 6 + len(prev_ln)

    def body(*refs):
        ins, (ra_ref, r_ref, hbo_ref), (acc_ref,), pctx = _split_refs(refs, n_in, 3, 1, plan)
        hb_ref, w1_ref, w2_ref, h_ref, g_ref, b_ref = ins[:6]
        _plan_start(plan, pctx, grid)
        j = pl.program_id(1)
        hb = hb_ref[...]
        p = None
        for s in range(slots):
            ra = jnp.maximum(jnp.dot(hb, w1_ref[s], preferred_element_type=F32), 0.0)
            ra_ref[:, s * 1024:(s + 1) * 1024] = ra.astype(BF16)
            ps = jnp.dot((ra * ra).astype(BF16), w2_ref[s], preferred_element_type=F32)
            p = ps if p is None else p + ps

        @pl.when(j == 0)
        def _():
            acc_ref[...] = p

        @pl.when(j > 0)
        def _():
            acc_ref[...] += p

        @pl.when(j == nf - 1)
        def _():
            r = ALPHA * _residual(h_ref, ins[6:]) + acc_ref[...]
            xhat, _ = _ln_stats(r)
            r_ref[...] = r
            hbo_ref[...] = (xhat * g_ref[...] + b_ref[...]).astype(BF16)

        _plan_wait(plan, pctx, grid)

    p_in, p_ospec, p_oshape, p_scr, p_alias = _plan_io(plan, n_in, 3)
    return pl.pallas_call(
        body, name=name, grid=grid,
        in_specs=[row, pl.BlockSpec((slots, D, D), lambda i, j: (j, 0, 0)),
                  pl.BlockSpec((slots, D, D), lambda i, j: (j, 1, 0)), row, vec, vec] + [vec] * len(prev_ln)
        + [_ANY] * len(p_in),
        out_specs=[pl.BlockSpec((tm, tf), lambda i, j: (i, j)), row, row] + p_ospec,
        out_shape=[jax.ShapeDtypeStruct((T, F), BF16), jax.ShapeDtypeStruct((T, D), F32),
                   jax.ShapeDtypeStruct((T, D), BF16)] + p_oshape,
        scratch_shapes=[pltpu.VMEM((tm, D), F32)] + p_scr,
        input_output_aliases=p_alias, compiler_params=_params(56, 2),
    )(*_hbm(h_b, wbuf, wbuf, h, g, b, *prev_ln), *p_in)


def _ln_bwd(dy, r, g, *, name, loss_head=()):
    T = r.shape[0]
    tm = min(ROW_BLOCK, T)
    row = pl.BlockSpec((tm, D), lambda i: (i, 0))
    vec = pl.BlockSpec((1, D), lambda i: (0, 0))
    acc = pl.BlockSpec((8, D), lambda i: (0, 0))
    operands, in_specs = ([r, g, *loss_head], [row, vec, vec, row]) if loss_head else ([r, g, dy], [row, vec, row])
    n_in = len(operands)

    def body(*refs):
        r_ref, g_ref = refs[:2]
        dr_ref, drb_ref, dg_ref, db_ref = refs[n_in:n_in + 4]

        @pl.when(pl.program_id(0) == 0)
        def _():
            for ref in refs[n_in + 2:]:
                ref[...] = jnp.zeros_like(ref)

        xhat, rstd = _ln_stats(r_ref[...])
        if loss_head:
            err = xhat * g_ref[...] + refs[2][...] - refs[3][...]
            refs[n_in + 4][...] += _fold8(err * err)
            dy_ = err * (1.0 / D)
        else:
            dy_ = refs[2][...]
        dxh = dy_ * g_ref[...]
        m1 = jnp.mean(dxh, -1, keepdims=True)
        m2 = jnp.mean(dxh * xhat, -1, keepdims=True)
        dr = rstd * (dxh - m1 - xhat * m2)
        dr_ref[...] = dr
        drb_ref[...] = dr.astype(BF16)
        dg_ref[...] += _fold8(dy_ * xhat)
        db_ref[...] += _fold8(dy_)

    n_acc = 3 if loss_head else 2
    return pl.pallas_call(
        body, name=name, grid=(T // tm,), in_specs=in_specs, out_specs=[row, row] + [acc] * n_acc,
        out_shape=[jax.ShapeDtypeStruct((T, D), F32), jax.ShapeDtypeStruct((T, D), BF16)]
        + [jax.ShapeDtypeStruct((8, D), F32)] * n_acc,
        compiler_params=_params(40, 1),
    )(*_hbm(*operands))


def _rope(x, c, s1, s2):
    return x * c + pltpu.roll(x, 112, 1) * s1 + pltpu.roll(x, 16, 1) * s2


def _rope_t(dy, c, s1, s2):
    return dy * c + pltpu.roll(dy * s1, 16, 1) + pltpu.roll(dy * s2, 112, 1)


def _rms(x, g):
    rstd = lax.rsqrt(jnp.mean(x * x, -1, keepdims=True) + EPS)
    xhat = x * rstd
    return xhat * g, xhat, rstd


def _mla_prep(z0, gq, gkv, wq, wk, wv, rc, rs1, rs2):
    T = z0.shape[0]
    tm = min(ROW_BLOCK, T)
    HW = HEADS * 128

    def body(cq_ref, ckv_ref, kr_ref, gq_ref, gkv_ref, wq_ref, wk_ref, wv_ref, c_ref, s1_ref, s2_ref,
             q_ref, k_ref, v_ref):
        nq = _rms(cq_ref[...], gq_ref[...])[0].astype(BF16)
        nkv = _rms(ckv_ref[...], gkv_ref[...])[0].astype(BF16)
        q = jnp.dot(nq, wq_ref[...], preferred_element_type=F32)
        k = jnp.dot(nkv, wk_ref[...], preferred_element_type=F32)
        v = jnp.dot(nkv, wv_ref[...], preferred_element_type=F32)
        c, s1, s2 = c_ref[...], s1_ref[...], s2_ref[...]
        kr = _rope(pltpu.roll(kr_ref[...], 64, 1), c, s1, s2)
        for h in range(HEADS):
            sl = slice(h * 128, (h + 1) * 128)
            q_ref[:, sl] = (_rope(q[:, sl], c, s1, s2) * QK_SCALE).astype(BF16)
            k_ref[:, sl] = (k[:, sl] + kr).astype(BF16)
        v_ref[...] = v.astype(BF16)

    full = lambda shape: pl.BlockSpec(shape, lambda i: (0, 0))
    tab = pl.BlockSpec((tm, 128), lambda i: (i, 0))
    return pl.pallas_call(
        body, name="mla_prep", grid=(T // tm,),
        in_specs=[pl.BlockSpec((tm, 256), lambda i: (i, 0)), pl.BlockSpec((tm, 256), lambda i: (i, 1)),
                  pl.BlockSpec((tm, 128), lambda i: (i, 12)), full((1, 256)), full((1, 256)),
                  full((256, HW)), full((256, HW)), full((256, 512)), tab, tab, tab],
        out_specs=[pl.BlockSpec((tm, HW), lambda i: (i, 0)), pl.BlockSpec((tm, HW), lambda i: (i, 0)),
                   pl.BlockSpec((tm, 512), lambda i: (i, 0))],
        out_shape=[jax.ShapeDtypeStruct((T, HW), BF16), jax.ShapeDtypeStruct((T, HW), BF16),
                   jax.ShapeDtypeStruct((T, 512), BF16)],
        compiler_params=_params(40, 1),
    )(*_hbm(z0, z0, z0, gq, gkv, wq, wk, wv, rc, rs1, rs2))


def _flash_fwd(q, k, v, plan=None):
    T = q.shape[0]
    bq = min(2 * ROW_BLOCK, T)
    nq = T // bq
    pairs = [(i, j) for i in range(nq) for j in range(i + 1)]
    imap, jmap = (jnp.array(m, jnp.int32) for m in zip(*pairs))
    grid = (4, len(pairs))

    def body(imap_ref, jmap_ref, *refs):
        (q_ref, k_ref, v_ref), (o_ref, lse_ref), (m_sc, acc_sc), pctx = _split_refs(refs, 3, 2, 2, plan)
        _plan_start(plan, pctx, grid)
        i, j = imap_ref[pl.program_id(1)], jmap_ref[pl.program_id(1)]
        first = lax.broadcasted_iota(jnp.int32, (bq, 128), 1) < 64

        @pl.when(j == 0)
        def _():
            m_sc[...] = jnp.full_like(m_sc, -jnp.inf)
            acc_sc[...] = jnp.zeros_like(acc_sc)

        def step(masked):
            vp = v_ref[...]
            for h in range(2):
                sl = slice(h * 128, (h + 1) * 128)
                s = lax.dot_general(q_ref[:, sl], k_ref[:, sl], NT_DIMS, preferred_element_type=F32)
                if masked:
                    rows = lax.broadcasted_iota(jnp.int32, (bq, bq), 0)
                    cols = lax.broadcasted_iota(jnp.int32, (bq, bq), 1)
                    s = jnp.where(cols <= rows, s, -jnp.inf)
                m_prev = m_sc[h, :, 0:1]
                m_new = jnp.maximum(m_prev, jnp.max(s, axis=1, keepdims=True))
                alpha = jnp.exp(m_prev - m_new)
                p = jnp.exp(s - m_new).astype(BF16)
                vh = jnp.where(first if h == 0 else jnp.logical_not(first), vp, jnp.ones_like(vp))
                acc_sc[h] = acc_sc[h] * alpha + jnp.dot(p, vh, preferred_element_type=F32)
                m_sc[h] = jnp.broadcast_to(m_new, (bq, 128))

        @pl.when(j < i)
        def _():
            step(False)

        @pl.when(j == i)
        def _():
            step(True)
            a0, a1 = acc_sc[0], acc_sc[1]
            l0, l1 = pltpu.roll(a0, 64, 1), pltpu.roll(a1, 64, 1)
            o_ref[...] = jnp.where(first, a0 / l0, a1 / l1).astype(BF16)
            lse_ref[...] = jnp.where(first, m_sc[0] + jnp.log(l0), m_sc[1] + jnp.log(l1))

        _plan_wait(plan, pctx, grid)

    qi = lambda hp, t, im, jm: (im[t], hp)
    kj = lambda hp, t, im, jm: (jm[t], hp)
    p_in, p_ospec, p_oshape, p_scr, p_alias = _plan_io(plan, 2 + 3, 2)
    return pl.pallas_call(
        body, name="flash_fwd",
        out_shape=[jax.ShapeDtypeStruct((T, 512), BF16), jax.ShapeDtypeStruct((T, 512), F32)] + p_oshape,
        grid_spec=pltpu.PrefetchScalarGridSpec(
            num_scalar_prefetch=2, grid=grid,
            in_specs=[pl.BlockSpec((bq, 256), qi), pl.BlockSpec((bq, 256), kj), pl.BlockSpec((bq, 128), kj)]
            + [_ANY] * len(p_in),
            out_specs=[pl.BlockSpec((bq, 128), qi), pl.BlockSpec((bq, 128), qi)] + p_ospec,
            scratch_shapes=[pltpu.VMEM((2, bq, 128), F32), pltpu.VMEM((2, bq, 128), F32)] + p_scr),
        input_output_aliases=p_alias, compiler_params=_params(56, 2),
    )(imap, jmap, *_hbm(q, k, v), *p_in)


def _attn_delta(dmix, o):
    T = o.shape[0]
    tm = min(ROW_BLOCK, T)
    blk = pl.BlockSpec((tm, 512), lambda i: (i, 0))

    def body(do_ref, o_ref, delta_ref, dob_ref):
        first = lax.broadcasted_iota(jnp.int32, (tm, 128), 1) < 64
        for hp in range(4):
            sl = slice(hp * 128, (hp + 1) * 128)
            prod = do_ref[:, sl] * o_ref[:, sl].astype(F32)
            d0 = jnp.sum(jnp.where(first, prod, 0.0), axis=1, keepdims=True)
            d1 = jnp.sum(jnp.where(first, 0.0, prod), axis=1, keepdims=True)
            delta_ref[:, sl] = jnp.where(first, d0, d1)
        dob_ref[...] = do_ref[...].astype(BF16)

    return pl.pallas_call(
        body, name="attn_delta", grid=(T // tm,), in_specs=[blk, blk], out_specs=[blk, blk],
        out_shape=[jax.ShapeDtypeStruct((T, 512), F32), jax.ShapeDtypeStruct((T, 512), BF16)],
        compiler_params=_params(32, 1),
    )(*_hbm(dmix, o))


def _flash_bwd(q, k, v, do_b, lse, delta, plan=None):
    T = q.shape[0]
    bq = min(2 * ROW_BLOCK, T)
    nq = T // bq
    pairs = [(i, j) for j in range(nq) for i in range(j, nq)]
    imap, jmap = (jnp.array(m, jnp.int32) for m in zip(*pairs))
    grid = (4, len(pairs))

    def body(imap_ref, jmap_ref, *refs):
        ((q_ref, k_ref, v_ref, do_ref, lse_ref, dl_ref), (dq_hbm, dk_ref, dv_ref), (dq_sc, dk_sc, dv_sc, sem),
         pctx) = _split_refs(refs, 6, 3, 4, plan)
        _plan_start(plan, pctx, grid)
        hp = pl.program_id(0)
        i, j = imap_ref[pl.program_id(1)], jmap_ref[pl.program_id(1)]
        first = lax.broadcasted_iota(jnp.int32, (bq, 128), 1) < 64

        @pl.when((j == 0) & (i == 0))
        def _():
            dq_sc[...] = jnp.zeros_like(dq_sc)

        @pl.when(i == j)
        def _():
            dk_sc[...] = jnp.zeros_like(dk_sc)
            dv_sc[...] = jnp.zeros_like(dv_sc)

        def tile(r0, nr, nc, masked):
            rs, cs = slice(r0, r0 + nr), slice(0, nc)
            vp = v_ref[cs, :]
            do = do_ref[rs, :]
            lanes = first[rs, :]
            for h in range(2):
                sl = slice(h * 128, (h + 1) * 128)
                qh, kh = q_ref[rs, sl], k_ref[cs, sl]
                s = lax.dot_general(qh, kh, NT_DIMS, preferred_element_type=F32)
                p = jnp.exp(s - lse_ref[rs, h * 64:h * 64 + 1])
                if masked:
                    rows = r0 + lax.broadcasted_iota(jnp.int32, (nr, nc), 0)
                    cols = lax.broadcasted_iota(jnp.int32, (nr, nc), 1)
                    p = jnp.where(cols <= rows, p, 0.0)
                do_h = jnp.where(lanes if h == 0 else jnp.logical_not(lanes), do, jnp.zeros_like(do))
                dv_sc[cs, :] += lax.dot_general(p.astype(BF16), do_h, TN_DIMS, preferred_element_type=F32)
                dp = lax.dot_general(do_h, vp, NT_DIMS, preferred_element_type=F32)
                ds = (p * (dp - dl_ref[rs, h * 64:h * 64 + 1])).astype(BF16)
                dq_sc[i, rs, sl] += jnp.dot(ds, kh, preferred_element_type=F32)
                dk_sc[cs, sl] += lax.dot_general(ds, qh, TN_DIMS, preferred_element_type=F32)

        @pl.when(i > j)
        def _():
            tile(0, bq, bq, False)

        @pl.when(i == j)
        def _():
            tile(0, bq // 2, bq // 2, True)
            tile(bq // 2, bq // 2, bq, True)

        @pl.when(i == nq - 1)
        def _():
            dk_ref[...] = dk_sc[...]
            dv_ref[...] = dv_sc[...]

        @pl.when((j == nq - 1) & (i == nq - 1))
        def _():
            cp = pltpu.make_async_copy(dq_sc, dq_hbm.at[hp], sem)
            cp.start()
            cp.wait()

        _plan_wait(plan, pctx, grid)

    qi = lambda hp, t, im, jm: (im[t], hp)
    kj = lambda hp, t, im, jm: (jm[t], hp)
    p_in, p_ospec, p_oshape, p_scr, p_alias = _plan_io(plan, 2 + 6, 3)
    return pl.pallas_call(
        body, name="flash_bwd",
        out_shape=[jax.ShapeDtypeStruct((4, nq, bq, 256), F32), jax.ShapeDtypeStruct((T, 1024), F32),
                   jax.ShapeDtypeStruct((T, 512), F32)] + p_oshape,
        grid_spec=pltpu.PrefetchScalarGridSpec(
            num_scalar_prefetch=2, grid=grid,
            in_specs=[pl.BlockSpec((bq, 256), qi), pl.BlockSpec((bq, 256), kj), pl.BlockSpec((bq, 128), kj),
                      pl.BlockSpec((bq, 128), qi), pl.BlockSpec((bq, 128), qi), pl.BlockSpec((bq, 128), qi)]
            + [_ANY] * len(p_in),
            out_specs=[_ANY, pl.BlockSpec((bq, 256), kj), pl.BlockSpec((bq, 128), kj)] + p_ospec,
            scratch_shapes=[pltpu.VMEM((nq, bq, 256), F32), pltpu.VMEM((bq, 256), F32), pltpu.VMEM((bq, 128), F32),
                            pltpu.SemaphoreType.DMA] + p_scr),
        input_output_aliases=p_alias, compiler_params=_params(56, 2),
    )(imap, jmap, *_hbm(q, k, v, do_b, lse, delta), *p_in)


def _mla_bwd(z0, dq4, dk, dv, gq, gkv, wq, wk, wv, rc, rs1, rs2, plan=None):
    T = z0.shape[0]
    tm = min(ROW_BLOCK, T)
    HW = HEADS * 128
    grid = (T // tm,)
    dq4 = dq4.reshape(4, T, 256)

    def body(*refs):
        ((cq_ref, ckv_ref, dq_ref, dk_ref, dv_ref, gq_ref, gkv_ref, wq_ref, wk_ref, wv_ref, c_ref, s1_ref, s2_ref),
         (dc_ref, dkr_ref, dwq_ref, dwk_ref, dwv_ref, dgq_ref, dgkv_ref), _, pctx) = _split_refs(refs, 13, 7, 0, plan)
        _plan_start(plan, pctx, grid)

        @pl.when(pl.program_id(0) == 0)
        def _():
            for ref in (dwq_ref, dwk_ref, dwv_ref, dgq_ref, dgkv_ref):
                ref[...] = jnp.zeros_like(ref)

        c, s1, s2 = c_ref[...], s1_ref[...], s2_ref[...]
        lane = lax.broadcasted_iota(jnp.int32, (tm, 128), 1)
        nq, xq, rq = _rms(cq_ref[...], gq_ref[...])
        nkv, xkv, rkv = _rms(ckv_ref[...], gkv_ref[...])
        nq_b, nkv_b = nq.astype(BF16), nkv.astype(BF16)

        dq_parts, dk_parts = [], []
        dkr = jnp.zeros((tm, 128), F32)
        for h in range(HEADS):
            blk = dq_ref[h // 2, :, (h % 2) * 128:(h % 2 + 1) * 128] * QK_SCALE
            dq_parts.append(_rope_t(blk, c, s1, s2).astype(BF16))
            kb = dk_ref[:, h * 128:(h + 1) * 128]
            dk_parts.append(jnp.where(lane < NOPE, kb, 0.0).astype(BF16))
            dkr = dkr + kb
        dq_b = jnp.concatenate(dq_parts, axis=1)
        dk_b = jnp.concatenate(dk_parts, axis=1)
        dv_b = dv_ref[...].astype(BF16)

        dwq_ref[...] += lax.dot_general(nq_b, dq_b, TN_DIMS, preferred_element_type=F32)
        dwk_ref[...] += lax.dot_general(nkv_b, dk_b, TN_DIMS, preferred_element_type=F32)
        dwv_ref[...] += lax.dot_general(nkv_b, dv_b, TN_DIMS, preferred_element_type=F32)
        dnq = lax.dot_general(dq_b, wq_ref[...], NT_DIMS, preferred_element_type=F32)
        dnkv = (lax.dot_general(dk_b, wk_ref[...], NT_DIMS, preferred_element_type=F32)
                + lax.dot_general(dv_b, wv_ref[...], NT_DIMS, preferred_element_type=F32))

        def rms_bwd(dn, xhat, rstd, g):
            dxh = dn * g
            return rstd * (dxh - xhat * jnp.mean(dxh * xhat, -1, keepdims=True))

        dc_ref[:, :256] = rms_bwd(dnq, xq, rq, gq_ref[...]).astype(BF16)
        dc_ref[:, 256:] = rms_bwd(dnkv, xkv, rkv, gkv_ref[...]).astype(BF16)
        dgq_ref[...] += _fold8(dnq * xq)
        dgkv_ref[...] += _fold8(dnkv * xkv)
        dkr = pltpu.roll(_rope_t(dkr, c, s1, s2), 64, 1)
        dkr_ref[...] = jnp.where(lane < ROPE, dkr, 0.0).astype(BF16)
        _plan_wait(plan, pctx, grid)

    full = lambda shape: pl.BlockSpec(shape, lambda i: (0,) * len(shape))
    tab = pl.BlockSpec((tm, 128), lambda i: (i, 0))
    p_in, p_ospec, p_oshape, p_scr, p_alias = _plan_io(plan, 13, 7)
    return pl.pallas_call(
        body, name="mla_bwd", grid=grid,
        in_specs=[pl.BlockSpec((tm, 256), lambda i: (i, 0)), pl.BlockSpec((tm, 256), lambda i: (i, 1)),
                  pl.BlockSpec((4, tm, 256), lambda i: (0, i, 0)),
                  pl.BlockSpec((tm, HW), lambda i: (i, 0)), pl.BlockSpec((tm, 512), lambda i: (i, 0)),
                  full((1, 256)), full((1, 256)), full((256, HW)), full((256, HW)), full((256, 512)), tab, tab, tab]
        + [_ANY] * len(p_in),
        out_specs=[pl.BlockSpec((tm, 512), lambda i: (i, 0)), tab, full((256, HW)), full((256, HW)),
                   full((256, 512)), full((8, 256)), full((8, 256))] + p_ospec,
        out_shape=[jax.ShapeDtypeStruct((T, 512), BF16), jax.ShapeDtypeStruct((T, 128), BF16),
                   jax.ShapeDtypeStruct((256, HW), F32), jax.ShapeDtypeStruct((256, HW), F32),
                   jax.ShapeDtypeStruct((256, 512), F32), jax.ShapeDtypeStruct((8, 256), F32),
                   jax.ShapeDtypeStruct((8, 256), F32)] + p_oshape,
        scratch_shapes=p_scr, input_output_aliases=p_alias, compiler_params=_params(48, 1),
    )(*_hbm(z0, z0, dq4, dk, dv, gq, gkv, wq, wk, wv, rc, rs1, rs2), *p_in)


def _sgu_fwd(z0, a_out, ln_g, ln_b, w, b_t):
    T = z0.shape[0]
    tm = min(ROW_BLOCK, T)
    W = SGU_G * SGU_C

    def body(u_ref, v_ref, a_ref, g_ref, b_ref, w_ref, bt_ref, o_ref):
        o_ref[:, :W] = a_ref[...]
        ug = _gelu(u_ref[...])
        xhat, _ = _ln_stats(_gelu(v_ref[...]))
        vn = (xhat * g_ref[...] + b_ref[...]).astype(BF16)
        tril = lax.broadcasted_iota(jnp.int32, (SGU_C, SGU_C), 0) >= lax.broadcasted_iota(jnp.int32, (SGU_C, SGU_C), 1)
        for g in range(SGU_G):
            cs = slice(g * SGU_C, (g + 1) * SGU_C)
            wg = jnp.where(tril, w_ref[g], 0.0).astype(BF16)
            bcol = bt_ref[:, g:g + 1]
            for c in range(tm // SGU_C):
                rs = slice(c * SGU_C, (c + 1) * SGU_C)
                mixed = jnp.dot(wg, vn[rs, cs], preferred_element_type=F32) + bcol
                o_ref[rs, W + g * SGU_C:W + (g + 1) * SGU_C] = (ug[rs, cs] * mixed).astype(BF16)

    full = lambda shape: pl.BlockSpec(shape, lambda i: (0,) * len(shape))
    return pl.pallas_call(
        body, name="sgu_fwd", grid=(T // tm,),
        in_specs=[pl.BlockSpec((tm, W), lambda i: (i, 1)), pl.BlockSpec((tm, W), lambda i: (i, 2)),
                  pl.BlockSpec((tm, W), lambda i: (i, 0)),
                  full((1, W)), full((1, W)), full((SGU_G, SGU_C, SGU_C)), full((SGU_C, SGU_G))],
        out_specs=pl.BlockSpec((tm, 2 * W), lambda i: (i, 0)),
        out_shape=jax.ShapeDtypeStruct((T, 2 * W), BF16),
        compiler_params=_params(32, 1),
    )(*_hbm(z0, z0, a_out, ln_g, ln_b, w, b_t))


def _sgu_bwd(z0, dmix, dc, dkr, ln_g, ln_b, w, b_t, plan=None):
    T = z0.shape[0]
    tm = min(ROW_BLOCK, T)
    W = SGU_G * SGU_C
    grid = (T // tm,)

    def body(*refs):
        ((u_ref, v_ref, do_ref, dc_ref, dkr_ref, g_ref, b_ref, w_ref, bt_ref),
         (dz_ref, dw_ref, db_ref, dlg_ref, dlb_ref), _, pctx) = _split_refs(refs, 9, 5, 0, plan)
        _plan_start(plan, pctx, grid)

        @pl.when(pl.program_id(0) == 0)
        def _():
            for ref in (dw_ref, db_ref, dlg_ref, dlb_ref):
                ref[...] = jnp.zeros_like(ref)

        dz_ref[:, :W] = dc_ref[...]
        dz_ref[:, 3 * W:] = dkr_ref[...]

        u, v, dout = u_ref[...], v_ref[...], do_ref[...]
        ug = _gelu(u)
        xhat, rstd = _ln_stats(_gelu(v))
        vn = (xhat * g_ref[...] + b_ref[...]).astype(BF16)
        dmixed = dout * ug
        dmixed_b = dmixed.astype(BF16)
        tril = lax.broadcasted_iota(jnp.int32, (SGU_C, SGU_C), 0) >= lax.broadcasted_iota(jnp.int32, (SGU_C, SGU_C), 1)
        lane = lax.broadcasted_iota(jnp.int32, (SGU_C, SGU_C), 1)
        dvn_cols = []
        for g in range(SGU_G):
            cs = slice(g * SGU_C, (g + 1) * SGU_C)
            wg = jnp.where(tril, w_ref[g], 0.0).astype(BF16)
            bcol = bt_ref[:, g:g + 1]
            dw_g = jnp.zeros((SGU_C, SGU_C), F32)
            db_g = jnp.zeros((SGU_C, 1), F32)
            dvn_rows = []
            for c in range(tm // SGU_C):
                rs = slice(c * SGU_C, (c + 1) * SGU_C)
                mixed = jnp.dot(wg, vn[rs, cs], preferred_element_type=F32) + bcol
                dz_ref[rs, W + g * SGU_C:W + (g + 1) * SGU_C] = (dout[rs, cs] * mixed * _gelu_grad(u[rs, cs])).astype(BF16)
                dm = dmixed_b[rs, cs]
                dw_g = dw_g + lax.dot_general(dm, vn[rs, cs], NT_DIMS, preferred_element_type=F32)
                db_g = db_g + jnp.sum(dmixed[rs, cs], axis=1, keepdims=True)
                dvn_rows.append(lax.dot_general(wg, dm, TN_DIMS, preferred_element_type=F32))
            dw_ref[g] += jnp.where(tril, dw_g, 0.0)
            db_ref[...] += jnp.where(lane == g, db_g, 0.0)
            dvn_cols.append(jnp.concatenate(dvn_rows, axis=0))
        dvn = jnp.concatenate(dvn_cols, axis=1)
        dxh = dvn * g_ref[...]
        m1 = jnp.mean(dxh, -1, keepdims=True)
        m2 = jnp.mean(dxh * xhat, -1, keepdims=True)
        dvg = rstd * (dxh - m1 - xhat * m2)
        dz_ref[:, 2 * W:3 * W] = (dvg * _gelu_grad(v)).astype(BF16)
        dlg_ref[...] += _fold8(dvn * xhat)
        dlb_ref[...] += _fold8(dvn)
        _plan_wait(plan, pctx, grid)

    full = lambda shape: pl.BlockSpec(shape, lambda i: (0,) * len(shape))
    p_in, p_ospec, p_oshape, p_scr, p_alias = _plan_io(plan, 9, 5)
    return pl.pallas_call(
        body, name="sgu_bwd", grid=grid,
        in_specs=[pl.BlockSpec((tm, W), lambda i: (i, 1)), pl.BlockSpec((tm, W), lambda i: (i, 2)),
                  pl.BlockSpec((tm, W), lambda i: (i, 1)), pl.BlockSpec((tm, W), lambda i: (i, 0)),
                  pl.BlockSpec((tm, 128), lambda i: (i, 0)),
                  full((1, W)), full((1, W)), full((SGU_G, SGU_C, SGU_C)), full((SGU_C, SGU_G))] + [_ANY] * len(p_in),
        out_specs=[pl.BlockSpec((tm, 3 * W + 128), lambda i: (i, 0)), full((SGU_G, SGU_C, SGU_C)),
                   full((SGU_C, SGU_C)), full((8, W)), full((8, W))] + p_ospec,
        out_shape=[jax.ShapeDtypeStruct((T, 3 * W + 128), BF16), jax.ShapeDtypeStruct((SGU_G, SGU_C, SGU_C), F32),
                   jax.ShapeDtypeStruct((SGU_C, SGU_C), F32), jax.ShapeDtypeStruct((8, W), F32),
                   jax.ShapeDtypeStruct((8, W), F32)] + p_oshape,
        scratch_shapes=p_scr, input_output_aliases=p_alias, compiler_params=_params(40, 1),
    )(*_hbm(z0, z0, dmix, dc, dkr, ln_g, ln_b, w, b_t), *p_in)


def _hg_lower_bound(lb_ref):
    a0, a1 = lb_ref[0:1, :], lb_ref[1:2, :]
    m = jnp.maximum(a0, a1)
    e0, e1 = jnp.exp(a0 - m), jnp.exp(a1 - m)
    return e1 / (e0 + e1)


def _running_sum(x, reverse=False):
    n = x.shape[0]
    row = lax.broadcasted_iota(jnp.int32, x.shape, 0)
    s = 1
    while s < n:
        if reverse:
            x = x + jnp.where(row < n - s, pltpu.roll(x, n - s, 0), 0.0)
        else:
            x = x + jnp.where(row >= s, pltpu.roll(x, s, 0), 0.0)
        s *= 2
    return x


def _hg_chunk(qc, fc, lb):
    C = HG_CHUNK
    rows = lax.broadcasted_iota(jnp.int32, (C, C), 0)
    cols = lax.broadcasted_iota(jnp.int32, (C, C), 1)
    rowid = lax.broadcasted_iota(jnp.int32, (C, 128), 0)
    sq, sg = _sigmoid(qc), _sigmoid(fc)
    qf = qc * sq
    gate = lb + (1.0 - lb) * sg
    kk = 1.0 - gate
    lg = jnp.log(gate)
    bcum = _running_sum(lg)
    b_mid = jnp.sum(jnp.where(rowid < C // 2, lg, 0.0), axis=0, keepdims=True)
    b_last = jnp.sum(lg, axis=0, keepdims=True)
    eq, ek, e, eh = jnp.exp(bcum - b_mid), jnp.exp(b_mid - bcum), jnp.exp(bcum), jnp.exp(b_last - bcum)
    qt, kt, qe, khat = qf * eq, kk * ek, qf * e, kk * eh
    a = lax.dot_general(qt.astype(BF16), kt.astype(BF16), NT_DIMS, preferred_element_type=F32)
    a = jnp.where(rows >= cols, a, 0.0)
    return dict(sq=sq, sg=sg, gate=gate, kk=kk, eq=eq, ek=ek, e=e, eh=eh, qt=qt, kt=kt, qe=qe, khat=khat, a=a,
                e_last=jnp.exp(b_last), tril=rows >= cols, rowid=rowid)


def _hgrn_fwd(z4, hg_lb, gnorm):
    T = z4.shape[1]
    tb = min(ROW_BLOCK, T)
    C = HG_CHUNK
    ncb = tb // C
    HPB = HG_HEADS_PER_STEP

    def body(q_ref, f_ref, i_ref, g_ref, lb_ref, gn_ref, y_ref, o_ref, st_ref, st_sc):
        @pl.when(pl.program_id(1) == 0)
        def _():
            st_sc[...] = jnp.zeros_like(st_sc)

        def chunk(c, carry):
            rs = pl.ds(pl.multiple_of(c * C, C), C)
            for hh in range(HPB):
                hs = slice(hh * 128, (hh + 1) * 128)
                lb = _hg_lower_bound(lb_ref.at[:, hs])
                v_b = i_ref[rs, hs].astype(BF16)
                gc = g_ref[rs, hs]
                x = _hg_chunk(q_ref[rs, hs], f_ref[rs, hs], lb)
                st = st_sc[hh]
                st_ref[hh, c] = st
                o = (jnp.dot(x["a"].astype(BF16), v_b, preferred_element_type=F32)
                     + lax.dot_general(x["qe"].astype(BF16), st.astype(BF16), NT_DIMS, preferred_element_type=F32))
                st_sc[hh] = st * x["e_last"] + lax.dot_general(v_b, x["khat"].astype(BF16), TN_DIMS,
                                                               preferred_element_type=F32)
                o_ref[rs, hs] = o
                n = o * lax.rsqrt(jnp.mean(o * o, -1, keepdims=True) + EPS)
                y_ref[rs, hs] = (n * gn_ref[:, hs] * (gc * _sigmoid(gc))).astype(BF16)
            return carry

        lax.fori_loop(0, ncb, chunk, 0, unroll=True)

    W = 128 * HPB
    zb = lambda k: pl.BlockSpec((None, tb, W), lambda h, t: (k, t, h))
    out = pl.BlockSpec((tb, W), lambda h, t: (t, h))
    return pl.pallas_call(
        body, name="hgrn_fwd", grid=(HEADS // HPB, T // tb),
        in_specs=[zb(0), zb(1), zb(2), zb(3), pl.BlockSpec((2, W), lambda h, t: (0, h)),
                  pl.BlockSpec((1, W), lambda h, t: (0, h))],
        out_specs=[out, out, pl.BlockSpec((HPB, ncb, 128, 128), lambda h, t: (h, t, 0, 0))],
        out_shape=[jax.ShapeDtypeStruct((T, D), BF16), jax.ShapeDtypeStruct((T, D), F32),
                   jax.ShapeDtypeStruct((HEADS, T // C, 128, 128), F32)],
        scratch_shapes=[pltpu.VMEM((HPB, 128, 128), F32)],
        compiler_params=_params(48, 2),
    )(*_hbm(z4, z4, z4, z4, hg_lb, gnorm))


def _hgrn_bwd(z4, o_raw, dy, states, hg_lb, gnorm):
    T = z4.shape[1]
    tb = min(ROW_BLOCK, T)
    C = HG_CHUNK
    ncb = tb // C
    nt = T // tb
    HPB = HG_HEADS_PER_STEP

    def body(q_ref, f_ref, i_ref, g_ref, o_ref, dy_ref, st_ref, lb_ref, gn_ref, dz_ref, dlb_ref, dgn_ref, dst_sc):
        @pl.when(pl.program_id(1) == 0)
        def _():
            dst_sc[...] = jnp.zeros_like(dst_sc)
            dlb_ref[...] = jnp.zeros_like(dlb_ref)
            dgn_ref[...] = jnp.zeros_like(dgn_ref)

        def chunk(cc, carry):
            for hh in range(HPB):
                one_head(ncb - 1 - cc, hh, slice(hh * 128, (hh + 1) * 128))
            return carry

        def one_head(c, hh, hs):
            rs = pl.ds(pl.multiple_of(c * C, C), C)
            lb = _hg_lower_bound(lb_ref.at[:, hs])
            gn = gn_ref[:, hs]
            qc, gc = q_ref[rs, hs], g_ref[rs, hs]
            v_b = i_ref[rs, hs].astype(BF16)
            x = _hg_chunk(qc, f_ref[rs, hs], lb)
            st, dst = st_ref[hh, c], dst_sc[hh]
            st_b, dst_b = st.astype(BF16), dst.astype(BF16)
            o, dyc = o_ref[rs, hs], dy_ref[rs, hs]
            sgg = _sigmoid(gc)
            sil = gc * sgg
            rstd = lax.rsqrt(jnp.mean(o * o, -1, keepdims=True) + EPS)
            n = o * rstd
            dgn_ref[:, hs] += _fold8(dyc * n * sil)
            dn = dyc * gn * sil
            do = rstd * (dn - n * jnp.mean(dn * n, -1, keepdims=True))
            dg = dyc * n * gn * (sgg * (1.0 + gc * (1.0 - sgg)))
            do_b = do.astype(BF16)
            da = jnp.where(x["tril"], lax.dot_general(do_b, v_b, NT_DIMS, preferred_element_type=F32), 0.0).astype(BF16)
            qt_b, kt_b, qe_b, khat_b = (x[n_].astype(BF16) for n_ in ("qt", "kt", "qe", "khat"))
            dv = (lax.dot_general(x["a"].astype(BF16), do_b, TN_DIMS, preferred_element_type=F32)
                  + lax.dot_general(khat_b, dst_b, NT_DIMS, preferred_element_type=F32))
            dqt = jnp.dot(da, kt_b, preferred_element_type=F32)
            dqe = jnp.dot(do_b, st_b, preferred_element_type=F32)
            dkt = lax.dot_general(da, qt_b, TN_DIMS, preferred_element_type=F32)
            dkhat = jnp.dot(v_b, dst_b, preferred_element_type=F32)
            dst_sc[hh] = lax.dot_general(do_b, qe_b, TN_DIMS, preferred_element_type=F32) + dst * x["e_last"]
            de_last = jnp.sum(st * dst, axis=0, keepdims=True)
            dqf = dqt * x["eq"] + dqe * x["e"]
            dkk = dkt * x["ek"] + dkhat * x["eh"]
            dkh_kh = dkhat * x["khat"]
            db = dqt * qt_b.astype(F32) - dkt * kt_b.astype(F32) + dqe * x["qe"] - dkh_kh
            db_last = jnp.sum(dkh_kh, axis=0, keepdims=True) + de_last * x["e_last"]
            db = db + jnp.where(x["rowid"] == C - 1, db_last, 0.0)
            dlg = _running_sum(db, reverse=True)
            dgate = dlg / x["gate"] - dkk
            sg, sq = x["sg"], x["sq"]
            dlb_ref[:, hs] += _fold8(dgate * (1.0 - sg)) * (lb * (1.0 - lb))
            dz_ref[0, rs, hs] = (dqf * (sq * (1.0 + qc * (1.0 - sq)))).astype(BF16)
            dz_ref[1, rs, hs] = (dgate * (1.0 - lb) * sg * (1.0 - sg)).astype(BF16)
            dz_ref[2, rs, hs] = dv.astype(BF16)
            dz_ref[3, rs, hs] = dg.astype(BF16)

        lax.fori_loop(0, ncb, chunk, 0, unroll=4)

    W = 128 * HPB
    zb = lambda k: pl.BlockSpec((None, tb, W), lambda h, t: (k, nt - 1 - t, h))
    blk = pl.BlockSpec((tb, W), lambda h, t: (nt - 1 - t, h))
    acc = pl.BlockSpec((8, W), lambda h, t: (0, h))
    return pl.pallas_call(
        body, name="hgrn_bwd", grid=(HEADS // HPB, nt),
        in_specs=[zb(0), zb(1), zb(2), zb(3), blk, blk,
                  pl.BlockSpec((HPB, ncb, 128, 128), lambda h, t: (h, nt - 1 - t, 0, 0)),
                  pl.BlockSpec((2, W), lambda h, t: (0, h)), pl.BlockSpec((1, W), lambda h, t: (0, h))],
        out_specs=[pl.BlockSpec((4, tb, W), lambda h, t: (0, nt - 1 - t, h)), acc, acc],
        out_shape=[jax.ShapeDtypeStruct((4, T, D), BF16), jax.ShapeDtypeStruct((8, D), F32),
                   jax.ShapeDtypeStruct((8, D), F32)],
        scratch_shapes=[pltpu.VMEM((HPB, 128, 128), F32)],
        compiler_params=_params(48, 2),
    )(*_hbm(z4, z4, z4, z4, o_raw, dy, states, hg_lb, gnorm))


def _adamw(w, g, m, v, *, name, with_g=False):
    R, L = w.shape
    tr = R if R <= 512 else 512
    assert R % tr == 0
    blk = pl.BlockSpec((tr, L), lambda i: (i, 0))
    c1, c2 = 1.0 - B1 ** STEP, 1.0 - B2 ** STEP
    n_out = 4 if with_g else 3

    def body(w_ref, g_ref, m_ref, v_ref, *o_refs):
        d_ref, mo_ref, vo_ref = o_refs[-3:]
        g_ = g_ref[...]
        m_ = B1 * m_ref[...] + (1.0 - B1) * g_
        v_ = B2 * v_ref[...] + (1.0 - B2) * (g_ * g_)
        if with_g:
            o_refs[0][...] = g_
        d_ref[...] = -LR * ((m_ / c1) / (jnp.sqrt(v_ / c2) + ADAM_EPS) + WD * w_ref[...])
        mo_ref[...] = m_
        vo_ref[...] = v_

    sds = jax.ShapeDtypeStruct((R, L), F32)
    return pl.pallas_call(
        body, name=name, grid=(R // tr,), in_specs=[blk] * 4, out_specs=[blk] * n_out, out_shape=[sds] * n_out,
        compiler_params=_params(32, 1),
    )(*_hbm(w, g, m, v))


def _adamw_rows(w, m, v, gbufs, row0, *, name, plan=None):
    L, R, C = w.shape
    tr = 256
    assert R % tr == 0 and row0 % tr == 0 and len(gbufs) == L
    grid = (L, R // tr)
    blk = pl.BlockSpec((None, tr, C), lambda l, i: (l, i, 0))
    gblks = [pl.BlockSpec((tr, C), lambda l, i, k=k: (row0 // tr + jnp.where(l == k, i, 0), 0)) for k in range(L)]
    c1, c2 = 1.0 - B1 ** STEP, 1.0 - B2 ** STEP

    def body(*refs):
        ins, (go_ref, d_ref, mo_ref, vo_ref), _, pctx = _split_refs(refs, 3 + L, 4, 0, plan)
        w_ref, m_ref, v_ref = ins[:3]
        g_refs = ins[3:]
        _plan_start(plan, pctx, grid)
        g_ = g_refs[0][...]
        for l in range(1, L):
            g_ = jnp.where(pl.program_id(0) == l, g_refs[l][...], g_)
        m_ = B1 * m_ref[...] + (1.0 - B1) * g_
        v_ = B2 * v_ref[...] + (1.0 - B2) * (g_ * g_)
        go_ref[...] = g_
        d_ref[...] = -LR * ((m_ / c1) / (jnp.sqrt(v_ / c2) + ADAM_EPS) + WD * w_ref[...])
        mo_ref[...] = m_
        vo_ref[...] = v_
        _plan_wait(plan, pctx, grid)

    sds = jax.ShapeDtypeStruct((L, R, C), F32)
    p_in, p_ospec, p_oshape, p_scr, p_alias = _plan_io(plan, 3 + L, 4)
    return pl.pallas_call(
        body, name=name, grid=grid, in_specs=[blk] * 3 + gblks + [_ANY] * len(p_in),
        out_specs=[blk] * 4 + p_ospec, out_shape=[sds] * 4 + p_oshape, scratch_shapes=p_scr,
        input_output_aliases=p_alias, compiler_params=_params(32, 2),
    )(*_hbm(w, m, v, *gbufs), *p_in)


def _add_pairs(g, theirs, ids, *, name):
    n, R, L = theirs.shape
    tr = math.gcd(R, 128)
    nb = R // tr

    def body(ids_ref, a_ref, b_ref, o_ref):
        o_ref[...] = (a_ref[...].astype(F32) + b_ref[...].astype(F32)).astype(BF16)

    blk = pl.BlockSpec((n, tr, L), lambda i, ids: (0, i, 0))
    return pl.pallas_call(
        body, name=name, out_shape=jax.ShapeDtypeStruct((n, R, L), BF16),
        grid_spec=pltpu.PrefetchScalarGridSpec(
            num_scalar_prefetch=1, grid=(nb,),
            in_specs=[pl.BlockSpec((n, tr, L), lambda i, ids: (0, ids[1] * nb + i, 0)), blk], out_specs=blk),
        compiler_params=_params(16, 1),
    )(ids, *_hbm(g, theirs))


def _sum_chips(pair, parts, ids, *, name):
    _, R, L = parts.shape
    tr = math.gcd(R, 128)

    def body(ids_ref, o_ref, r_ref, out_ref):
        out_ref[...] = ((o_ref[...].astype(F32) + r_ref[0].astype(F32)) + r_ref[1].astype(F32)) + r_ref[2].astype(F32)

    return pl.pallas_call(
        body, name=name, out_shape=jax.ShapeDtypeStruct((2, R, L), F32),
        grid_spec=pltpu.PrefetchScalarGridSpec(
            num_scalar_prefetch=1, grid=(R // tr,),
            in_specs=[pl.BlockSpec((None, tr, L), lambda i, ids: (ids[0], i, 0)),
                      pl.BlockSpec((3, tr, L), lambda i, ids: (0, i, 0))],
            out_specs=pl.BlockSpec((None, tr, L), lambda i, ids: (ids[1], i, 0))),
        compiler_params=_params(32, 1),
    )(ids, *_hbm(pair, parts))


def _mesh_ids():
    x, y, c = _mesh_pos()
    return jnp.stack([2 * x + y, c]).astype(jnp.int32)


def _place_shard(rows, ids, *, name):
    R, L = rows.shape
    tr = 128

    def body(ids_ref, in_ref, out_ref):
        out_ref[...] = in_ref[...].astype(BF16)

    return pl.pallas_call(
        body, name=name, out_shape=jax.ShapeDtypeStruct((4, R, L), BF16),
        grid_spec=pltpu.PrefetchScalarGridSpec(
            num_scalar_prefetch=1, grid=(R // tr,), in_specs=[pl.BlockSpec((tr, L), lambda i, ids: (i, 0))],
            out_specs=pl.BlockSpec((None, tr, L), lambda i, ids: (ids[0], i, 0))),
        compiler_params=_params(16, 1),
    )(ids, *_hbm(rows))


IN_E_SHARD, IN_E_LAYOUT = 392, 1664


def _in_e_runs():
    runs = []
    for lo, hi, dst in ((0, 512, 0), (512, 544, 1536), (544, 1568, 512)):
        while lo < hi:
            j = lo // IN_E_SHARD
            wd = min(hi, IN_E_SHARD * (j + 1)) - lo
            runs.append((j, lo - IN_E_SHARD * j, dst, wd))
            lo, dst = lo + wd, dst + wd
    return runs


def _in_e_to_layout(shards):
    tr = 256

    def body(s_ref, o_ref, t_ref):
        t_ref[...] = jnp.zeros_like(t_ref)
        for j in range(4):
            s = s_ref[j].astype(F32)
            for _, c, dst, wd in (r for r in _in_e_runs() if r[0] == j):
                t_ref[:, dst:dst + wd] = s[:, c:c + wd]
        o_ref[...] = t_ref[...].astype(BF16)

    return pl.pallas_call(
        body, name="in_e_to_layout", out_shape=jax.ShapeDtypeStruct((D, IN_E_LAYOUT), BF16), grid=(D // tr,),
        in_specs=[pl.BlockSpec((4, tr, IN_E_SHARD), lambda i: (0, i, 0))],
        out_specs=pl.BlockSpec((tr, IN_E_LAYOUT), lambda i: (i, 0)),
        scratch_shapes=[pltpu.VMEM((tr, IN_E_LAYOUT), F32)], compiler_params=_params(16, 1),
    )(*_hbm(shards))


def _in_e_from_layout(g):
    tr = 256

    def body(g_ref, o_ref, t_ref):
        g_ = g_ref[...]
        for j, c, src, wd in _in_e_runs():
            t_ref[j, :, c:c + wd] = g_[:, src:src + wd]
        o_ref[...] = t_ref[...].astype(BF16)

    return pl.pallas_call(
        body, name="in_e_from_layout", out_shape=jax.ShapeDtypeStruct((4, D, IN_E_SHARD), BF16), grid=(D // tr,),
        in_specs=[pl.BlockSpec((tr, IN_E_LAYOUT), lambda i: (i, 0))],
        out_specs=pl.BlockSpec((4, tr, IN_E_SHARD), lambda i: (0, i, 0)),
        scratch_shapes=[pltpu.VMEM((4, tr, IN_E_SHARD), F32)], compiler_params=_params(16, 1),
    )(*_hbm(g))


def _place_weights(pieces, buffer_rows, ids, *, plan=None):
    tr = 256
    steps, s = [], 0
    for arr, layer, buf, row0 in pieces:
        nblk = arr.shape[1] // tr
        steps.append((s, nblk))
        s += nblk
    total = s
    buf_start = [min(st for (st, _), p in zip(steps, pieces) if p[2] == k) for k in range(len(buffer_rows))]
    grid = (total,)
    n_in = len(pieces)

    def body(*refs):
        ins, outs, _, pctx = _split_refs(refs[1:], n_in, len(buffer_rows), 0, plan)
        _plan_start(plan, pctx, grid)
        i = pl.program_id(0)
        for (st, nblk), (_, _, buf, _), ref in zip(steps, pieces, ins):
            @pl.when((i >= st) & (i < st + nblk))
            def _(ref=ref, buf=buf):
                outs[buf][...] = ref[...].astype(BF16)
        _plan_wait(plan, pctx, grid)

    in_specs = [pl.BlockSpec((None, tr, D), lambda i, ids, layer=layer, st=st, nblk=nblk:
                             (layer, jnp.clip(i - st, 0, nblk - 1), 0))
                for (st, nblk), (_, layer, _, _) in zip(steps, pieces)]
    out_specs = [pl.BlockSpec((None, tr, D), lambda i, ids, st=st, nb=rows // tr: (ids[0], jnp.clip(i - st, 0, nb - 1), 0))
                 for st, rows in zip(buf_start, buffer_rows)]
    p_in, p_ospec, p_oshape, p_scr, p_alias = _plan_io(plan, 1 + n_in, len(buffer_rows))
    return pl.pallas_call(
        body, name="place_weights",
        out_shape=[jax.ShapeDtypeStruct((4, rows, D), BF16) for rows in buffer_rows] + p_oshape,
        grid_spec=pltpu.PrefetchScalarGridSpec(
            num_scalar_prefetch=1, grid=grid, in_specs=in_specs + [_ANY] * len(p_in), out_specs=out_specs + p_ospec,
            scratch_shapes=p_scr),
        input_output_aliases=p_alias, compiler_params=_params(16, 1),
    )(ids, *_hbm(*[p[0] for p in pieces]), *p_in)


def _remote(src, dst, send_sem, recv_sem, to):
    return pltpu.make_async_remote_copy(src_ref=src, dst_ref=dst, send_sem=send_sem, recv_sem=recv_sem,
                                        device_id=to, device_id_type=MESH_IDS)


def _rows(ref, lead, start, size):
    return ref.at[tuple(pl.ds(0, n) for n in ref.shape[:lead]) + (pl.ds(start, size),)]


def _other_chips():
    x, y, _ = _mesh_pos()
    return [(1 - x, y), (x, 1 - y), (1 - x, 1 - y)]


def _plan_gather_ici(bufs):
    n = len(bufs)

    def copies(outs, send, recv):
        x, y, c = _mesh_pos()
        res = []
        for b in range(n):
            half = bufs[b].shape[1] // 2
            mine = _rows(outs[b].at[2 * x + y], 0, c * half, half)
            for j, (cx, cy) in enumerate(_other_chips()):
                res.append((_remote(mine, mine, send(3 * b + j), recv(3 * b + j), (cx, cy, c)),
                            _remote(mine, _rows(outs[b].at[2 * cx + cy], 0, c * half, half),
                                    send(3 * b + j), recv(3 * b + j), (x, y, c))))
        return res

    def start(ins, outs, send, recv, loc):
        for out_cp, _ in copies(outs, send, recv):
            out_cp.start()

    def wait(ins, outs, send, recv, loc):
        for out_cp, in_cp in copies(outs, send, recv):
            in_cp.wait_recv()
            out_cp.wait_send()

    outs = [jax.ShapeDtypeStruct(b.shape, b.dtype) for b in bufs]
    return _Plan(bufs, outs, 3 * n, 0, start, wait, aliases={b: b for b in range(n)})


def _plan_gather_forward(bufs):
    n = len(bufs)

    def copies(outs, send, recv):
        x, y, c = _mesh_pos()
        res = []
        for b in range(n):
            half = bufs[b].shape[1] // 2
            for j, (cx, cy) in enumerate(_other_chips()):
                slot = outs[b].at[2 * cx + cy]
                res.append((_remote(_rows(slot, 0, c * half, half), _rows(slot, 0, c * half, half),
                                    send(3 * b + j), recv(3 * b + j), (x, y, 1 - c)),
                            _remote(_rows(slot, 0, c * half, half), _rows(slot, 0, (1 - c) * half, half),
                                    send(3 * b + j), recv(3 * b + j), (x, y, c))))
        return res

    def start(ins, outs, send, recv, loc):
        for out_cp, _ in copies(outs, send, recv):
            out_cp.start()

    def wait(ins, outs, send, recv, loc):
        for out_cp, in_cp in copies(outs, send, recv):
            in_cp.wait_recv()
            out_cp.wait_send()

    outs = [jax.ShapeDtypeStruct(b.shape, b.dtype) for b in bufs]
    return _Plan(bufs, outs, 3 * n, 0, start, wait, aliases={b: b for b in range(n)})


def _plan_pair_swap(g):
    half = g.shape[1] // 2

    def copy(ins, outs, send, recv, loc):
        x, y, c = _mesh_pos()
        return _remote(_rows(ins[0], 1, (1 - c) * half, half), outs[0], send(0), recv(0), (x, y, 1 - c))

    return _Plan([g], [jax.ShapeDtypeStruct((4, half, g.shape[2]), g.dtype)], 1, 0,
                 lambda *a: copy(*a).start(), lambda *a: copy(*a).wait())


def _plan_pair_gather(buf):
    def copies(ins, outs, send, recv, loc):
        x, y, c = _mesh_pos()
        return (_remote(outs[0].at[c], outs[0].at[c], send(0), recv(0), (x, y, 1 - c)),
                _remote(outs[0].at[c], outs[0].at[1 - c], send(0), recv(0), (x, y, c)))

    def wait(*a):
        out_cp, in_cp = copies(*a)
        in_cp.wait_recv()
        out_cp.wait_send()

    return _Plan([buf], [jax.ShapeDtypeStruct(buf.shape, buf.dtype)], 1, 0, lambda *a: copies(*a)[0].start(), wait,
                 aliases={0: 0})


def _plan_chip_scatter(p):
    def copies(ins, outs, send, recv, loc):
        _, _, c = _mesh_pos()
        return [_remote(ins[0].at[2 * cx + cy], outs[0].at[j], send(j), recv(j), (cx, cy, c))
                for j, (cx, cy) in enumerate(_other_chips())]

    def start(*a):
        for cp in copies(*a):
            cp.start()

    def wait(*a):
        for cp in copies(*a):
            cp.wait()

    return _Plan([p], [jax.ShapeDtypeStruct((3,) + p.shape[1:], p.dtype)], 3, 0, start, wait)


def _plan_exchange_all(vec):
    def copies(ins, outs, send, recv, loc):
        x, y, c = _mesh_pos()
        return [_remote(ins[0], outs[0].at[r - 1], send(r - 1), recv(r - 1), (x ^ (r >> 2), y ^ ((r >> 1) & 1), c ^ (r & 1)))
                for r in range(1, 8)]

    def start(*a):
        for cp in copies(*a):
            cp.start()

    def wait(*a):
        for cp in copies(*a):
            cp.wait()

    return _Plan([vec], [jax.ShapeDtypeStruct((7,) + vec.shape, vec.dtype)], 7, 0, start, wait)


SMALL_LAYOUT = {
    "mla_gq": (0, 1, 256, (1, 256)), "mla_gkv": (1, 1, 256, (1, 256)), "sgu_ln_g": (2, 1, 512, (1, 512)),
    "sgu_ln_b": (3, 1, 512, (1, 512)), "sgu_w": (4, 64, 1024, (64, 1024)), "sgu_b": (68, 1, 512, (1, 512)),
    "hg_lb": (69, 2, 1024, (2, 1024)), "hg_gnorm": (71, 1, 1024, (1, 256)), "ln1_g": (72, 2, 1024, (2, 1024)),
    "ln1_b": (74, 2, 1024, (2, 1024)), "ln2_g": (76, 2, 1024, (2, 1024)), "ln2_b": (78, 2, 1024, (2, 1024)),
}


def _small_pack(dgq, dgkv, dslg, dslb, dsw, dsb, dlb, dgn, ln_parts, sq_err):
    flat_ln = [p for pair in ln_parts for p in pair]

    def body(*refs):
        gq_ref, gkv_ref, slg_ref, slb_ref, sw_ref, sb_ref, lb_ref, gn_ref = refs[:8]
        ln_refs, err_ref, out_ref, t_sc = refs[8:16], refs[16], refs[17], refs[18]
        s8 = lambda ref: jnp.sum(ref[...], axis=0, keepdims=True)
        out_ref[...] = jnp.zeros_like(out_ref)
        out_ref[0:1, 0:256] = s8(gq_ref)
        out_ref[1:2, 0:256] = s8(gkv_ref)
        out_ref[2:3, 0:512] = s8(slg_ref)
        out_ref[3:4, 0:512] = s8(slb_ref)
        out_ref[4:68, :] = sw_ref[...]
        t_sc[...] = sb_ref[...].T
        for g in range(SGU_G):
            out_ref[68:69, g * SGU_C:(g + 1) * SGU_C] = t_sc[g:g + 1, :]
        d_lb1 = s8(lb_ref)
        out_ref[69:70, :] = -d_lb1
        out_ref[70:71, :] = d_lb1
        out_ref[71:72, :] = s8(gn_ref)
        for k, ref in enumerate(ln_refs):
            out_ref[72 + k:73 + k, :] = s8(ref)
        out_ref[0:1, 1023:1024] = jnp.sum(s8(err_ref), axis=1, keepdims=True) * (0.5 / D)

    vm = pl.BlockSpec(memory_space=pltpu.VMEM)
    return pl.pallas_call(
        body, name="small_grad_pack", in_specs=[vm] * 17, out_specs=vm,
        out_shape=jax.ShapeDtypeStruct((SMALL_ROWS, 1024), F32), scratch_shapes=[pltpu.VMEM((SGU_C, SGU_C), F32)],
        compiler_params=_params(16),
    )(dgq, dgkv, dslg, dslb, dsw.reshape(64, 1024), dsb, dlb, dgn, *flat_ln, sq_err)


def _small_update(vec, others, ids, w, m, v):
    names = list(SMALL_LAYOUT)
    n = len(names)
    c1, c2 = 1.0 - B1 ** STEP, 1.0 - B2 ** STEP
    have_others = others is not None

    def body(*refs):
        ids_ref, v_ref = refs[0], refs[1]
        k = 2 + have_others
        w_refs, m_refs, v_refs = refs[k:k + n], refs[k + n:k + 2 * n], refs[k + 2 * n:k + 3 * n]
        outs = refs[k + 3 * n:]
        row0_ref, tot_sc = outs[0], outs[-1]
        total = v_ref[...]
        if have_others:
            me = 2 * ids_ref[0] + ids_ref[1]
            total = None
            for d in range(8):
                rel = d ^ me
                term = jnp.where(rel == 0, v_ref[...], refs[2][jnp.maximum(rel - 1, 0)])
                total = term if total is None else total + term
        tot_sc[...] = total
        row0_ref[...] = tot_sc[0:1, :]
        for i, name in enumerate(names):
            r0, nr, width, _ = SMALL_LAYOUT[name]
            if name == "hg_gnorm":
                g_ = tot_sc[r0:r0 + 1, 0:256]
                for chip in range(1, 4):
                    g_ = jnp.where(ids_ref[0] == chip, tot_sc[r0:r0 + 1, chip * 256:(chip + 1) * 256], g_)
            else:
                g_ = tot_sc[r0:r0 + nr, 0:width]
            m_ = B1 * m_refs[i][...] + (1.0 - B1) * g_
            v_ = B2 * v_refs[i][...] + (1.0 - B2) * (g_ * g_)
            go, do, mo, vo = outs[1 + 4 * i:5 + 4 * i]
            go[...] = g_
            do[...] = -LR * ((m_ / c1) / (jnp.sqrt(v_ / c2) + ADAM_EPS) + WD * w_refs[i][...])
            mo[...] = m_
            vo[...] = v_

    full = lambda shape: pl.BlockSpec(shape, lambda i, ids, nd=len(shape): (0,) * nd)
    kshapes = [SMALL_LAYOUT[name][3] for name in names]
    operands = [vec] + ([others] if have_others else []) + [d[name] for d in (w, m, v) for name in names]
    out_shapes = [jax.ShapeDtypeStruct((1, 1024), F32)] + [jax.ShapeDtypeStruct(s, F32) for s in kshapes for _ in range(4)]
    res = pl.pallas_call(
        body, name="small_update", out_shape=out_shapes,
        grid_spec=pltpu.PrefetchScalarGridSpec(
            num_scalar_prefetch=1, grid=(1,), in_specs=[full(o.shape) for o in operands],
            out_specs=[full(s.shape) for s in out_shapes],
            scratch_shapes=[pltpu.VMEM((SMALL_ROWS, 1024), F32)]),
        compiler_params=_params(32, 1),
    )(ids, *operands)
    return res[0], {name: tuple(res[1 + 4 * i:5 + 4 * i]) for i, name in enumerate(names)}


ROWS_L1, ROWS_L0, ROWS_ODD_W = 3328, 2048, 384
ODD_PARTS = (("w_out_e", (256, 1024)), ("w_in_e", (1024, 392)), ("w_qb", (256, 192)), ("w_kvb", (256, 256)))
ODD_W_PARTS = tuple(p for p in ODD_PARTS if p[0] != "w_in_e")


def _odd_rows(parts, dtype, layout, total, gnorm=None):
    rows = [parts[n].reshape(-1, 1024).astype(dtype) for n, _ in layout]
    used = sum(r.shape[0] for r in rows)
    if gnorm is not None:
        bits = lax.bitcast_convert_type(gnorm.reshape(-1), BF16).reshape(1, 512)
        rows.append(jnp.pad(bits, ((0, 15), (0, 512))))
        used += 16
    if total > used:
        rows.append(jnp.zeros((total - used, 1024), dtype))
    return jnp.concatenate(rows, axis=0)


def _odd_unrows(buf, layout, with_gnorm=False):
    out, off = {}, 0
    for n, shape in layout:
        nr = math.prod(shape) // 1024
        out[n] = buf[off:off + nr].reshape(shape)
        off += nr
    if with_gnorm:
        out["hg_gnorm"] = lax.bitcast_convert_type(buf[off, :512].reshape(256, 2), F32).reshape(1, 256)
    return out


def _rope_tables(positions):
    half = ROPE // 2
    inv_freq = ROPE_BASE ** (-jnp.arange(half, dtype=F32) / half)
    per_row = 128 // half
    ang = jnp.repeat(positions.astype(F32).reshape(-1, per_row), half, axis=1) * jnp.tile(inv_freq, per_row)
    cos, sin = jnp.cos(ang).reshape(-1, half), jnp.sin(ang).reshape(-1, half)
    T = cos.shape[0]
    one, z16, z32 = jnp.ones((T, NOPE), F32), jnp.zeros((T, half), F32), jnp.zeros((T, 32), F32)
    z64 = jnp.zeros((T, NOPE), F32)
    c = jnp.concatenate([one, cos, cos, z32], axis=1)
    s1 = jnp.concatenate([z64, -sin, z16, z32], axis=1)
    s2 = jnp.concatenate([z64, z16, sin, z32], axis=1)
    return c, s1, s2


def _local_step(x, positions, tgt, odd, bufs, P, exchange):
    T = x.shape[0]
    row = lambda a: a.reshape(1, -1)
    rc, rs1, rs2 = _rope_tables(positions)
    blk = lambda f: pl.BlockSpec((None, D, D), f)

    w_in = _in_e_to_layout(odd["w_in_e"])
    wq = jnp.pad(odd["w_qb"].reshape(256, HEADS, NOPE + ROPE), ((0, 0), (0, 0), (0, 32))).reshape(256, HEADS * 128)
    kvb = odd["w_kvb"].reshape(256, HEADS, NOPE + VDIM)
    wk = jnp.pad(kvb[:, :, :NOPE], ((0, 0), (0, 0), (0, 64))).reshape(256, HEADS * 128)
    wv = kvb[:, :, NOPE:].reshape(256, HEADS * VDIM)
    w_out_e = odd["w_out_e"]
    sgu_w = P["sgu_w"][0]
    sgu_bt = P["sgu_b"][0].T
    gq, gkv = P["mla_gq"], P["mla_gkv"]
    gnorm = P["hg_gnorm"]

    z0 = _matmul(x, w_in, name="in_proj_e", M=T, N=1664, K=D, tn=1664)[0]
    q, k, v = _mla_prep(z0, gq, gkv, wq, wk, wv, rc, rs1, rs2)
    if exchange:
        ids = _mesh_ids()
        placed = list(bufs)
        a_out, lse, wga, wgb = _flash_fwd(q, k, v, plan=_plan_gather_ici(placed[:2]))
    else:
        a_out, lse = _flash_fwd(q, k, v)
        wga, wgb, wgc = bufs
    mix0 = _sgu_fwd(z0, a_out, P["sgu_ln_g"], P["sgu_ln_b"], sgu_w, sgu_bt)
    res = _proj_ln(mix0, w_out_e, x, row(P["ln1_g"][0]), row(P["ln1_b"][0]), name="out_proj_ln_e",
                   plan=_plan_gather_forward([wga, wgb]) if exchange else None)
    r1, h1b = res[:2]
    if exchange:
        wga, wgb = res[2:]
    ln = lambda name, l: (row(P[name + "_g"][l]), row(P[name + "_b"][l]))
    res = _ffn_ln(h1b, wga, r1, *ln("ln2", 0), name="ffn_ln_0", prev_ln=ln("ln1", 0),
                  plan=_plan_gather_ici(placed[2:]) if exchange else None)
    ra0, r2, h2b = res[:3]
    z4 = _matmul(h2b, wgb, name="in_proj_o", M=T, N=4 * D, K=D, tn=2 * D, n_slots=True,
                 b_spec=pl.BlockSpec((2, D, D), lambda i, j, k: (j, 0, 0)),
                 out_shape=jax.ShapeDtypeStruct((4, T, D), F32),
                 o_spec=pl.BlockSpec((2, min(MM_ROWS, T), D), lambda i, j, k: (j, i, 0)))[0]
    y1, o_raw, states = _hgrn_fwd(z4, P["hg_lb"], gnorm)
    res2 = _proj_ln(y1, wgb, r2, *ln("ln1", 1), name="out_proj_ln_o", prev_ln=ln("ln2", 0), w_rowblk=4,
                    plan=_plan_gather_forward([res[3]]) if exchange else None)
    r3, h3b = res2[:2]
    if exchange:
        wgc = res2[2]
    ra1, r4, _ = _ffn_ln(h3b, wgc, r3, *ln("ln2", 1), name="ffn_ln_1", prev_ln=ln("ln1", 1))

    ln1_g, ln1_b, ln2_g, ln2_b = [None, None], [None, None], [None, None], [None, None]
    sq_err_parts = []

    def ffn_bwd(l, dh, r_out, ra, h_mid_b, g2, wg, rows, plan=None, loss_head=()):
        dr, dr_b, dg, db, *sq_err = _ln_bwd(dh, r_out, row(g2), name=f"ln2_bwd_{l}", loss_head=loss_head)
        sq_err_parts.extend(sq_err)
        ln2_g[l], ln2_b[l] = dg, db
        da, *extra = _matmul(dr_b, wg, tb=True, mul=ra, out_dtype=BF16, name=f"ffn_da_{l}", M=T, N=4 * D, K=D, tn=2 * D,
                             b_spec=pl.BlockSpec((2, D, D), lambda i, j, k: (j, 1, 0)), n_slots=True, plan=plan)
        gbuf = _matmul(ra, dr_b, ta=True, a_sq=True, name=f"ffn_dw2_{l}", M=4 * D, N=D, K=T, tm=1024, tk=DW_TOKENS // 2,
                       out_shape=jax.ShapeDtypeStruct((4, rows, D), BF16), o_spec=blk(lambda i, j, k: (i, 1, 0)))[0]
        gbuf = _matmul(h_mid_b, da, ta=True, name=f"ffn_dw1_{l}", M=D, N=4 * D, K=T, tm=1024, tk=DW_TOKENS, into=gbuf,
                       out_shape=jax.ShapeDtypeStruct((4, rows, D), BF16), o_spec=blk(lambda i, j, k: (j, 0, 0)))[0]
        dh_mid = _matmul(da, wg, tb=True, add=dr, add_scale=ALPHA, name=f"ffn_dh_{l}", M=T, N=D, K=4 * D, tk=2 * D,
                         b_spec=pl.BlockSpec((2, D, D), lambda i, j, k: (k, 0, 0)))[0]
        return dh_mid, gbuf, extra

    dh3, g1, _ = ffn_bwd(1, None, r4, ra1, h3b, P["ln2_g"][1], wgc, ROWS_L1, loss_head=(row(P["ln2_b"][1]), tgt))
    loss_parts = sq_err_parts[0]
    dr3, dr3_b, dg, db = _ln_bwd(dh3, r3, row(P["ln1_g"][1]), name="ln1_bwd_1")
    ln1_g[1], ln1_b[1] = dg, db
    g1_sds = jax.ShapeDtypeStruct((4, ROWS_L1, D), BF16)
    g1 = _matmul(y1, dr3_b, ta=True, name="dw_out_o", M=D, N=D, K=T, tm=256, tk=DW_TOKENS, into=g1, out_shape=g1_sds,
                 o_spec=pl.BlockSpec((None, 256, D), lambda i, j, k: (i, 12, 0)))[0]
    dmix1 = _matmul(dr3_b, wgb, tb=True, name="dmix_o", M=T, N=D, K=D, b_spec=_rows4_spec(4, 3), b_merge=(D, D))[0]
    dz4, dlb, dgn = _hgrn_bwd(z4, o_raw, dmix1, states, P["hg_lb"], gnorm)
    g1 = _matmul(h2b, dz4, ta=True, name="dw_in_o", M=D, N=4 * D, K=T, tm=1024, tk=DW_TOKENS, into=g1, out_shape=g1_sds,
                 b_spec=pl.BlockSpec((None, min(DW_TOKENS, T), D), lambda i, j, k: (j, k, 0)),
                 o_spec=blk(lambda i, j, k: (j, 2, 0)))[0]
    dh2 = _matmul(dz4, wgb, tb=True, add=dr3, add_scale=ALPHA, name="dh_in_o", M=T, N=D, K=4 * D, tk=2 * D,
                  a_spec=pl.BlockSpec((2, min(MM_ROWS, T), D), lambda i, j, k: (k, i, 0)),
                  b_spec=pl.BlockSpec((2, D, D), lambda i, j, k: (k, 0, 0)))[0]

    dh1, g0, swapped1 = ffn_bwd(0, dh2, r2, ra0, h1b, P["ln2_g"][0], wga, ROWS_L0,
                                plan=_plan_pair_swap(g1) if exchange else None)
    dr1, dr1_b, dg, db = _ln_bwd(dh1, r1, row(P["ln1_g"][0]), name="ln1_bwd_0")
    ln1_g[0], ln1_b[0] = dg, db
    godd = {"w_out_e": _matmul(mix0, dr1_b, ta=True, name="dw_out_e", M=D, N=D, K=T, tm=1024, tk=DW_TOKENS)[0]}
    dmix0, *swapped0 = _matmul(dr1_b, w_out_e, tb=True, name="dmix_e", M=T, N=D, K=D,
                               plan=_plan_pair_swap(g0) if exchange else None)
    delta, do_b = _attn_delta(dmix0, a_out)
    if exchange:
        pair1 = _add_pairs(g1, swapped1[0], ids, name="grad_pair_add_1")
        pair0 = _add_pairs(g0, swapped0[0], ids, name="grad_pair_add_0")
        dq4, dk, dv, parts0, parts1 = _flash_bwd(
            q, k, v, do_b, lse, delta, plan=_join_plans([_plan_chip_scatter(pair0), _plan_chip_scatter(pair1)]))
        half0 = _sum_chips(pair0, parts0, ids, name="grad_chip_sum_0")
        half1 = _sum_chips(pair1, parts1, ids, name="grad_chip_sum_1")
        dc, dkr, dwq, dwk, dwv, dgq, dgkv, g0, g1 = _mla_bwd(
            z0, dq4, dk, dv, gq, gkv, wq, wk, wv, rc, rs1, rs2,
            plan=_join_plans([_plan_pair_gather(half0), _plan_pair_gather(half1)]))
        g0, g1 = g0.reshape(ROWS_L0, D), g1.reshape(ROWS_L1, D)
    else:
        dq4, dk, dv = _flash_bwd(q, k, v, do_b, lse, delta)
        dc, dkr, dwq, dwk, dwv, dgq, dgkv = _mla_bwd(z0, dq4, dk, dv, gq, gkv, wq, wk, wv, rc, rs1, rs2)
    godd["w_qb"] = dwq.reshape(256, HEADS, 128)[:, :, :NOPE + ROPE].reshape(256, HEADS * (NOPE + ROPE))
    godd["w_kvb"] = jnp.concatenate([dwk.reshape(256, HEADS, 128)[:, :, :NOPE], dwv.reshape(256, HEADS, VDIM)],
                                    axis=2).reshape(256, HEADS * (NOPE + VDIM))
    swap_b = None
    if exchange:
        by_chip = [_odd_rows({"w_out_e": jnp.split(godd["w_out_e"], 4, axis=0)[j],
                              **{n: jnp.split(godd[n], 4, axis=1)[j] for n in ("w_qb", "w_kvb")}}, BF16,
                             ODD_W_PARTS, ROWS_ODD_W)
                   for j in range(4)]
        odd_b = jnp.stack(by_chip)
        swap_b = _plan_pair_swap(odd_b)
    dz0, dsw, dsb, dslg, dslb, *theirs_b = _sgu_bwd(z0, dmix0, dc, dkr, P["sgu_ln_g"], P["sgu_ln_b"], sgu_w, sgu_bt,
                                                    plan=swap_b)
    small_vec = _small_pack(dgq, dgkv, dslg, dslb, dsw, dsb, dlb, dgn, [ln1_g, ln1_b, ln2_g, ln2_b], loss_parts)
    plan_in = None
    if exchange:
        pair_b = _add_pairs(odd_b, theirs_b[0], ids, name="odd_pair_add_1")
        plan_in = _join_plans([_plan_exchange_all(small_vec), _plan_chip_scatter(pair_b)])
    dw_in, *carried = _matmul(x, dz0, ta=True, name="dw_in_e", M=D, N=1664, K=T, tm=1024, tn=1664, tk=DW_TOKENS // 4,
                              plan=plan_in)
    odd_a = godd["w_in_e"] = _in_e_from_layout(dw_in)
    plan_x = None
    if exchange:
        small_others, parts_b = carried
        theirs_a = _run_plan(_plan_pair_swap(odd_a), name="odd_pair_swap")[0]
        pair_a = _add_pairs(odd_a, theirs_a, ids, name="odd_pair_add_0")
        plan_x = _plan_chip_scatter(pair_a)
    grad_x, *parts_a = _matmul(dz0, w_in, tb=True, add=dr1, add_scale=ALPHA, name="dx", M=T, N=D, K=1664, tk=1664,
                               plan=plan_x)
    if exchange:
        godd = ([pair_a, pair_b], [parts_a[0], parts_b])
        return grad_x, g0, g1, godd, small_vec, small_others
    return grad_x, g0, g1, godd, small_vec, None


WEIGHTS = ['w_in_e', 'mla_gq', 'mla_gkv', 'w_qb', 'w_kvb', 'sgu_ln_g', 'sgu_ln_b', 'sgu_w', 'sgu_b', 'w_out_e',
           'w_in_o', 'hg_lb', 'hg_gnorm', 'w_out_o', 'ln1_g', 'ln1_b', 'w_ff1', 'w_ff2', 'ln2_g', 'ln2_b']


def kernel(x, positions, w_in_e, mla_gq, mla_gkv, w_qb, w_kvb, sgu_ln_g, sgu_ln_b, sgu_w, sgu_b, w_out_e, w_in_o, hg_lb, hg_gnorm, w_out_o, ln1_g, ln1_b, w_ff1, w_ff2, ln2_g, ln2_b, loss_target, m_w_in_e, m_mla_gq, m_mla_gkv, m_w_qb, m_w_kvb, m_sgu_ln_g, m_sgu_ln_b, m_sgu_w, m_sgu_b, m_w_out_e, m_w_in_o, m_hg_lb, m_hg_gnorm, m_w_out_o, m_ln1_g, m_ln1_b, m_w_ff1, m_w_ff2, m_ln2_g, m_ln2_b, v_w_in_e, v_mla_gq, v_mla_gkv, v_w_qb, v_w_kvb, v_sgu_ln_g, v_sgu_ln_b, v_sgu_w, v_sgu_b, v_w_out_e, v_w_in_o, v_hg_lb, v_hg_gnorm, v_w_out_o, v_ln1_g, v_ln1_b, v_w_ff1, v_w_ff2, v_ln2_g, v_ln2_b):
    args = dict(locals())
    w = {n: args[n] for n in WEIGHTS}
    m = {n: args["m_" + n] for n in WEIGHTS}
    v = {n: args["v_" + n] for n in WEIGHTS}
    cx, cy, cc = _mesh_pos()
    chip = 2 * cx + cy

    odd_shard = _odd_rows({"w_out_e": w_out_e[0], "w_qb": w_qb[0], "w_kvb": w_kvb[0]}, BF16, ODD_W_PARTS, ROWS_ODD_W,
                          gnorm=hg_gnorm)
    ids = _mesh_ids()
    placed = [_place_shard(w_in_e[0], ids, name="place_shard_in_e"), _place_shard(odd_shard, ids, name="place_shard_odd")]
    pieces = [(w_ff1, 0, 0, 0), (w_ff2, 0, 0, 1024), (w_in_o, 0, 1, 0), (w_out_o, 0, 1, 1024),
              (w_ff1, 1, 2, 0), (w_ff2, 1, 2, 1024)]
    *big_bufs, odd_a, odd_b = _place_weights(pieces, (2048, 1280, 2048), ids, plan=_plan_gather_ici(placed))
    gathered = _run_plan(_plan_gather_forward([odd_a, odd_b]), name="odd_gather_forward")
    per_chip = [_odd_unrows(gathered[1][j], ODD_W_PARTS, with_gnorm=True) for j in range(4)]
    odd = {"w_out_e": jnp.concatenate([p["w_out_e"] for p in per_chip], axis=0),
           "w_in_e": gathered[0]}
    for n in ("w_qb", "w_kvb"):
        odd[n] = jnp.concatenate([p[n] for p in per_chip], axis=1)
    small = {n: w[n] for n in SMALL_LAYOUT if n != "hg_gnorm"}
    small["hg_gnorm"] = jnp.concatenate([p["hg_gnorm"] for p in per_chip], axis=1)
    grad_x, g_l0, g_l1, godd, small_vec, small_others = _local_step(
        x[0], positions[0], loss_target[0], odd, big_bufs, small, True)

    sums = [_sum_chips(pair, parts, ids, name=f"odd_chip_sum_{k}") for k, (pair, parts) in enumerate(zip(*godd))]
    g_in_e, g_rest = _run_plan(_join_plans([_plan_pair_gather(s) for s in sums]), name="odd_pair_gather")
    g_odd = _odd_unrows(g_rest.reshape(ROWS_ODD_W, 1024), ODD_W_PARTS)
    g_odd["w_in_e"] = g_in_e.reshape(D, 392)

    to_kernel = lambda d: {n: d[n].reshape(SMALL_LAYOUT[n][3]) for n in SMALL_LAYOUT}
    first_row, small_out = _small_update(small_vec, small_others, ids, to_kernel(w), to_kernel(m), to_kernel(v))
    loss = first_row[0, 1023]
    grads, delta, new_m, new_v = {}, {}, {}, {}
    for n, res in small_out.items():
        grads[n], delta[n], new_m[n], new_v[n] = (r.reshape(w[n].shape) for r in res)

    for n, bufs_, row0 in (("w_ff1", [g_l0, g_l1], 0), ("w_ff2", [g_l0, g_l1], 1024), ("w_in_o", [g_l1], 2048),
                           ("w_out_o", [g_l1], 3072)):
        grads[n], delta[n], new_m[n], new_v[n] = _adamw_rows(w[n], m[n], v[n], bufs_, row0, name=f"adamw_{n}")
    for n, _ in ODD_PARTS:
        if n == "w_in_e":
            res = _adamw(w[n][0].T, g_odd[n].T, m[n][0].T, v[n][0].T, name=f"adamw_{n}", with_g=True)
            grads[n], delta[n], new_m[n], new_v[n] = (r.T[None] for r in res)
            continue
        grads[n] = g_odd[n][None]
        d_, m_, v_ = _adamw(w[n][0], g_odd[n], m[n][0], v[n][0], name=f"adamw_{n}")
        delta[n], new_m[n], new_v[n] = d_[None], m_[None], v_[None]

    return (loss, grad_x[None], *[grads[n] for n in WEIGHTS], *[delta[n] for n in WEIGHTS],
            *[new_m[n] for n in WEIGHTS], *[new_v[n] for n in WEIGHTS])
```

```python
import math

import jax
import jax.numpy as jnp
from jax import lax
from jax.experimental import pallas as pl
from jax.experimental.pallas import tpu as pltpu

F32 = jnp.float32
BF16 = jnp.bfloat16
MESH_IDS = pl.DeviceIdType.MESH

D = 1024
DEPTH = 2
HEADS = 8
NOPE, ROPE, VDIM = 64, 32, 64
QK_SCALE = (NOPE + ROPE) ** -0.5
ROPE_BASE = 10000.0
SGU_G, SGU_C = 4, 128
HG_CHUNK = 64
HG_HEADS_PER_STEP = 8
ALPHA = (2 * DEPTH) ** 0.25
EPS = 1e-5
LR, B1, B2, ADAM_EPS, WD, STEP = 0.001, 0.9, 0.999, 1e-08, 0.01, 10
GELU_C = math.sqrt(2.0 / math.pi)
GELU_A = 0.044715
MB = 1024 * 1024
ROW_BLOCK = 512
SMALL_ROWS = 80

NT_DIMS = (((1,), (1,)), ((), ()))
TN_DIMS = (((0,), (0,)), ((), ()))


def _params(vmem_mb, n_axes=0):
    kw = dict(vmem_limit_bytes=vmem_mb * MB)
    if n_axes:
        kw["dimension_semantics"] = ("arbitrary",) * n_axes
    return pltpu.CompilerParams(**kw)


_ANY = pl.BlockSpec(memory_space=pltpu.HBM)


def _mesh_pos():
    return lax.axis_index("x"), lax.axis_index("y"), lax.axis_index("c")


def _hbm(*arrays):
    return tuple(pltpu.with_memory_space_constraint(a, pltpu.HBM) if a.size >= 2 ** 18 else a for a in arrays)


class _Plan:
    def __init__(self, ins, outs, n_remote, n_local, start, wait, aliases=None):
        self.ins, self.outs, self.n_remote, self.n_local = list(ins), list(outs), n_remote, n_local
        self.start, self.wait, self.aliases = start, wait, dict(aliases or {})


def _join_plans(plans):
    ins, outs, aliases, parts = [], [], {}, []
    nr = nl = 0
    for p in plans:
        parts.append((p, len(ins), len(outs), nr, nl))
        aliases.update({len(ins) + i: len(outs) + o for i, o in p.aliases.items()})
        ins += p.ins
        outs += p.outs
        nr += p.n_remote
        nl += p.n_local

    def run(which):
        def go(in_refs, out_refs, send, recv, loc):
            for p, i0, o0, r0, l0 in parts:
                getattr(p, which)(in_refs[i0:i0 + len(p.ins)], out_refs[o0:o0 + len(p.outs)],
                                  lambda i, r0=r0: send(r0 + i), lambda i, r0=r0: recv(r0 + i),
                                  lambda i, l0=l0: loc(l0 + i))
        return go

    return _Plan(ins, outs, nr, nl, run("start"), run("wait"), aliases)


def _plan_io(plan, n_in, n_out):
    if plan is None:
        return [], [], [], [], {}
    sems = [pltpu.SemaphoreType.DMA((max(plan.n_remote, 1),)), pltpu.SemaphoreType.DMA((max(plan.n_remote, 1),)),
            pltpu.SemaphoreType.DMA((max(plan.n_local, 1),))]
    aliases = {n_in + i: n_out + o for i, o in plan.aliases.items()}
    return plan.ins, [_ANY] * len(plan.outs), plan.outs, sems, aliases


def _split_refs(refs, n_in, n_out, n_scr, plan):
    p_in, p_out = (len(plan.ins), len(plan.outs)) if plan is not None else (0, 0)
    refs = list(refs)
    ins, refs = refs[:n_in], refs[n_in:]
    pins, refs = refs[:p_in], refs[p_in:]
    outs, refs = refs[:n_out], refs[n_out:]
    pouts, refs = refs[:p_out], refs[p_out:]
    scr, psem = refs[:n_scr], refs[n_scr:]
    psem = tuple((lambda i, s=s: s.at[i]) for s in psem)
    return ins, outs, scr, (pins, pouts, psem)


def _grid_edge(grid, last):
    cond = None
    for ax, n in enumerate(grid):
        c = pl.program_id(ax) == (n - 1 if last else 0)
        cond = c if cond is None else cond & c
    return cond


def _plan_start(plan, pctx, grid):
    if plan is not None:
        pins, pouts, psem = pctx
        pl.when(_grid_edge(grid, False))(lambda: plan.start(pins, pouts, *psem))


def _plan_wait(plan, pctx, grid):
    if plan is not None:
        pins, pouts, psem = pctx
        pl.when(_grid_edge(grid, True))(lambda: plan.wait(pins, pouts, *psem))


def _run_plan(plan, *, name):
    def body(*refs):
        _, _, _, (pins, pouts, psem) = _split_refs(refs, 0, 0, 0, plan)
        plan.start(pins, pouts, *psem)
        plan.wait(pins, pouts, *psem)

    p_in, p_ospec, p_oshape, p_scr, p_alias = _plan_io(plan, 0, 0)
    return pl.pallas_call(body, name=name, in_specs=[_ANY] * len(p_in), out_specs=p_ospec, out_shape=p_oshape,
                          scratch_shapes=p_scr, input_output_aliases=p_alias)(*p_in)


def _fold8(x):
    return x.reshape(x.shape[0] // 8, 8, x.shape[1]).sum(axis=0)


def _ln_stats(r):
    mu = jnp.mean(r, -1, keepdims=True)
    xc = r - mu
    rstd = lax.rsqrt(jnp.mean(xc * xc, -1, keepdims=True) + EPS)
    return xc * rstd, rstd


def _sigmoid(x):
    return jax.nn.sigmoid(x)


def _gelu(x):
    return 0.5 * x * (1.0 + jnp.tanh(GELU_C * (x + GELU_A * x * x * x)))


def _gelu_grad(x):
    t = jnp.tanh(GELU_C * (x + GELU_A * x * x * x))
    return 0.5 * (1.0 + t) + 0.5 * x * (1.0 - t * t) * GELU_C * (1.0 + 3.0 * GELU_A * x * x)


MM_ROWS = 1024
DW_TOKENS = 4096


def _matmul(a, b, *, name, M, N, K, ta=False, tb=False, out_dtype=F32, tm=MM_ROWS, tn=1024, tk=1024,
            a_spec=None, b_spec=None, b_merge=None, out_shape=None, o_spec=None, into=None,
            a_sq=False, mul=None, add=None, add_scale=1.0, n_slots=False, plan=None):
    assert not n_slots or (K // min(tk, K) == 1 and add is None)
    tm, tn, tk = min(tm, M), min(tn, N), min(tk, K)
    assert M % tm == 0 and N % tn == 0 and K % tk == 0
    grid = (M // tm, N // tn, K // tk)
    nk = grid[2]
    if a_spec is None:
        a_spec = pl.BlockSpec((tk, tm), lambda i, j, k: (k, i)) if ta else pl.BlockSpec((tm, tk), lambda i, j, k: (i, k))
    if b_spec is None:
        b_spec = pl.BlockSpec((tn, tk), lambda i, j, k: (j, k)) if tb else pl.BlockSpec((tk, tn), lambda i, j, k: (k, j))
    if o_spec is None:
        o_spec = pl.BlockSpec((tm, tn), lambda i, j, k: (i, j))
        out_shape = jax.ShapeDtypeStruct((M, N), out_dtype)
    e_spec = pl.BlockSpec((tm, tn), lambda i, j, k: (i, j))
    dims = (((0 if ta else 1,), (1 if tb else 0,)), ((), ()))
    extra = [e for e in (mul, add, into) if e is not None]
    n_in = 2 + len(extra)

    def body(*refs):
        ins, outs, scr, pctx = _split_refs(refs, n_in, 1, 1 if nk > 1 else 0, plan)
        a_ref, b_ref = ins[0], ins[1]
        rest = list(ins[2:])
        mul_ref = rest.pop(0) if mul is not None else None
        add_ref = rest.pop(0) if add is not None else None
        o_ref = outs[0]
        _plan_start(plan, pctx, grid)
        av = a_ref[...].astype(BF16)
        if a_sq:
            av = av * av
        bv = b_ref[...]
        if b_merge is not None:
            bv = bv.reshape(b_merge)
        if n_slots:
            for s in range(bv.shape[0]):
                r = lax.dot_general(av, bv[s], dims, preferred_element_type=F32)
                w = r.shape[1]
                if mul_ref is not None:
                    r = r * (2.0 * mul_ref[:, s * w:(s + 1) * w].astype(F32))
                if o_ref.ndim == 3:
                    o_ref[s] = r.astype(o_ref.dtype)
                else:
                    o_ref[:, s * w:(s + 1) * w] = r.astype(o_ref.dtype)
            _plan_wait(plan, pctx, grid)
            return
        if bv.ndim == 3:
            w = av.shape[-1] // (1 if av.ndim == 3 else bv.shape[0])
            a_parts = [av[s] if av.ndim == 3 else av[:, s * w:(s + 1) * w] for s in range(bv.shape[0])]
            p = sum(lax.dot_general(a_parts[s], bv[s], dims, preferred_element_type=F32) for s in range(bv.shape[0]))
        else:
            p = lax.dot_general(av, bv, dims, preferred_element_type=F32)

        def finish(r):
            if mul_ref is not None:
                r = r * (2.0 * mul_ref[...].astype(F32))
            if add_ref is not None:
                r = r + add_scale * add_ref[...]
            o_ref[...] = r.astype(o_ref.dtype)

        if nk == 1:
            finish(p)
        else:
            acc_ref = scr[0]
            k = pl.program_id(2)

            @pl.when(k == 0)
            def _():
                acc_ref[...] = p

            @pl.when(k > 0)
            def _():
                acc_ref[...] += p

            @pl.when(k == nk - 1)
            def _():
                finish(acc_ref[...])

        _plan_wait(plan, pctx, grid)

    p_in, p_ospec, p_oshape, p_scr, p_alias = _plan_io(plan, n_in, 1)
    aliases = dict(p_alias)
    if into is not None:
        aliases[n_in - 1] = 0
    return pl.pallas_call(
        body, name=name, grid=grid,
        in_specs=[a_spec, b_spec] + [e_spec] * (len(extra) - (into is not None)) + [_ANY] * (into is not None)
        + [_ANY] * len(p_in),
        out_specs=[o_spec] + p_ospec, out_shape=[out_shape] + p_oshape,
        scratch_shapes=([pltpu.VMEM((tm, tn), F32)] if nk > 1 else []) + p_scr,
        input_output_aliases=aliases, compiler_params=_params(48, 3),
    )(*_hbm(a, b, *extra), *p_in)


def _rows4_spec(rowblk, n_axes):
    return pl.BlockSpec((4, 256, D), lambda *_: (0, rowblk, 0))


def _residual(h_ref, prev_refs):
    if not prev_refs:
        return h_ref[...]
    xhat, _ = _ln_stats(h_ref[...])
    return xhat * prev_refs[0][...] + prev_refs[1][...]


def _proj_ln(a_b, w, h_prev, g, b, *, name, prev_ln=(), w_rowblk=None, plan=None):
    T = a_b.shape[0]
    tm = min(MM_ROWS, T)
    grid = (T // tm,)
    row = pl.BlockSpec((tm, D), lambda i: (i, 0))
    vec = pl.BlockSpec((1, D), lambda i: (0, 0))
    w_spec = pl.BlockSpec((D, D), lambda i: (0, 0)) if w_rowblk is None else _rows4_spec(w_rowblk, 1)
    n_in = 5 + len(prev_ln)

    def body(*refs):
        ins, (r_ref, hb_ref), _, pctx = _split_refs(refs, n_in, 2, 0, plan)
        a_ref, w_ref, h_ref, g_ref, b_ref = ins[:5]
        _plan_start(plan, pctx, grid)
        mix = jnp.dot(a_ref[...], w_ref[...].reshape(D, D), preferred_element_type=F32)
        r = ALPHA * _residual(h_ref, ins[5:]) + mix
        xhat, _ = _ln_stats(r)
        r_ref[...] = r
        hb_ref[...] = (xhat * g_ref[...] + b_ref[...]).astype(BF16)
        _plan_wait(plan, pctx, grid)

    p_in, p_ospec, p_oshape, p_scr, p_alias = _plan_io(plan, n_in, 2)
    return pl.pallas_call(
        body, name=name, grid=grid,
        in_specs=[row, w_spec, row, vec, vec] + [vec] * len(prev_ln) + [_ANY] * len(p_in),
        out_specs=[row, row] + p_ospec,
        out_shape=[jax.ShapeDtypeStruct((T, D), F32), jax.ShapeDtypeStruct((T, D), BF16)] + p_oshape,
        scratch_shapes=p_scr, input_output_aliases=p_alias, compiler_params=_params(40, 1),
    )(*_hbm(a_b, w, h_prev, g, b, *prev_ln), *p_in)


def _ffn_ln(h_b, wbuf, h, g, b, *, name, prev_ln=(), plan=None):
    T = h_b.shape[0]
    slots = 2
    tm, tf = min(ROW_BLOCK, T), slots * 1024
    nf = 4 // slots
    F = nf * tf
    grid = (T // tm, nf)
    row = pl.BlockSpec((tm, D), lambda i, j: (i, 0))
    vec = pl.BlockSpec((1, D), lambda i, j: (0, 0))
    n_in = 6 + len(prev_ln)

    def body(*refs):
        ins, (ra_ref, r_ref, hbo_ref), (acc_ref,), pctx = _split_refs(refs, n_in, 3, 1, plan)
        hb_ref, w1_ref, w2_ref, h_ref, g_ref, b_ref = ins[:6]
        _plan_start(plan, pctx, grid)
        j = pl.program_id(1)
        hb = hb_ref[...]
        p = None
        for s in range(slots):
            ra = jnp.maximum(jnp.dot(hb, w1_ref[s], preferred_element_type=F32), 0.0)
            ra_ref[:, s * 1024:(s + 1) * 1024] = ra.astype(BF16)
            ps = jnp.dot((ra * ra).astype(BF16), w2_ref[s], preferred_element_type=F32)
            p = ps if p is None else p + ps

        @pl.when(j == 0)
        def _():
            acc_ref[...] = p

        @pl.when(j > 0)
        def _():
            acc_ref[...] += p

        @pl.when(j == nf - 1)
        def _():
            r = ALPHA * _residual(h_ref, ins[6:]) + acc_ref[...]
            xhat, _ = _ln_stats(r)
            r_ref[...] = r
            hbo_ref[...] = (xhat * g_ref[...] + b_ref[...]).astype(BF16)

        _plan_wait(plan, pctx, grid)

    p_in, p_ospec, p_oshape, p_scr, p_alias = _plan_io(plan, n_in, 3)
    return pl.pallas_call(
        body, name=name, grid=grid,
        in_specs=[row, pl.BlockSpec((slots, D, D), lambda i, j: (j, 0, 0)),
                  pl.BlockSpec((slots, D, D), lambda i, j: (j, 1, 0)), row, vec, vec] + [vec] * len(prev_ln)
        + [_ANY] * len(p_in),
        out_specs=[pl.BlockSpec((tm, tf), lambda i, j: (i, j)), row, row] + p_ospec,
        out_shape=[jax.ShapeDtypeStruct((T, F), BF16), jax.ShapeDtypeStruct((T, D), F32),
                   jax.ShapeDtypeStruct((T, D), BF16)] + p_oshape,
        scratch_shapes=[pltpu.VMEM((tm, D), F32)] + p_scr,
        input_output_aliases=p_alias, compiler_params=_params(56, 2),
    )(*_hbm(h_b, wbuf, wbuf, h, g, b, *prev_ln), *p_in)


def _ln_bwd(dy, r, g, *, name, loss_head=()):
    T = r.shape[0]
    tm = min(ROW_BLOCK, T)
    row = pl.BlockSpec((tm, D), lambda i: (i, 0))
    vec = pl.BlockSpec((1, D), lambda i: (0, 0))
    acc = pl.BlockSpec((8, D), lambda i: (0, 0))
    operands, in_specs = ([r, g, *loss_head], [row, vec, vec, row]) if loss_head else ([r, g, dy], [row, vec, row])
    n_in = len(operands)

    def body(*refs):
        r_ref, g_ref = refs[:2]
        dr_ref, drb_ref, dg_ref, db_ref = refs[n_in:n_in + 4]

        @pl.when(pl.program_id(0) == 0)
        def _():
            for ref in refs[n_in + 2:]:
                ref[...] = jnp.zeros_like(ref)

        xhat, rstd = _ln_stats(r_ref[...])
        if loss_head:
            err = xhat * g_ref[...] + refs[2][...] - refs[3][...]
            refs[n_in + 4][...] += _fold8(err * err)
            dy_ = err * (1.0 / D)
        else:
            dy_ = refs[2][...]
        dxh = dy_ * g_ref[...]
        m1 = jnp.mean(dxh, -1, keepdims=True)
        m2 = jnp.mean(dxh * xhat, -1, keepdims=True)
        dr = rstd * (dxh - m1 - xhat * m2)
        dr_ref[...] = dr
        drb_ref[...] = dr.astype(BF16)
        dg_ref[...] += _fold8(dy_ * xhat)
        db_ref[...] += _fold8(dy_)

    n_acc = 3 if loss_head else 2
    return pl.pallas_call(
        body, name=name, grid=(T // tm,), in_specs=in_specs, out_specs=[row, row] + [acc] * n_acc,
        out_shape=[jax.ShapeDtypeStruct((T, D), F32), jax.ShapeDtypeStruct((T, D), BF16)]
        + [jax.ShapeDtypeStruct((8, D), F32)] * n_acc,
        compiler_params=_params(40, 1),
    )(*_hbm(*operands))


def _rope(x, c, s1, s2):
    return x * c + pltpu.roll(x, 112, 1) * s1 + pltpu.roll(x, 16, 1) * s2


def _rope_t(dy, c, s1, s2):
    return dy * c + pltpu.roll(dy * s1, 16, 1) + pltpu.roll(dy * s2, 112, 1)


def _rms(x, g):
    rstd = lax.rsqrt(jnp.mean(x * x, -1, keepdims=True) + EPS)
    xhat = x * rstd
    return xhat * g, xhat, rstd


def _mla_prep(z0, gq, gkv, wq, wk, wv, rc, rs1, rs2):
    T = z0.shape[0]
    tm = min(ROW_BLOCK, T)
    HW = HEADS * 128

    def body(cq_ref, ckv_ref, kr_ref, gq_ref, gkv_ref, wq_ref, wk_ref, wv_ref, c_ref, s1_ref, s2_ref,
             q_ref, k_ref, v_ref):
        nq = _rms(cq_ref[...], gq_ref[...])[0].astype(BF16)
        nkv = _rms(ckv_ref[...], gkv_ref[...])[0].astype(BF16)
        q = jnp.dot(nq, wq_ref[...], preferred_element_type=F32)
        k = jnp.dot(nkv, wk_ref[...], preferred_element_type=F32)
        v = jnp.dot(nkv, wv_ref[...], preferred_element_type=F32)
        c, s1, s2 = c_ref[...], s1_ref[...], s2_ref[...]
        kr = _rope(pltpu.roll(kr_ref[...], 64, 1), c, s1, s2)
        for h in range(HEADS):
            sl = slice(h * 128, (h + 1) * 128)
            q_ref[:, sl] = (_rope(q[:, sl], c, s1, s2) * QK_SCALE).astype(BF16)
            k_ref[:, sl] = (k[:, sl] + kr).astype(BF16)
        v_ref[...] = v.astype(BF16)

    full = lambda shape: pl.BlockSpec(shape, lambda i: (0, 0))
    tab = pl.BlockSpec((tm, 128), lambda i: (i, 0))
    return pl.pallas_call(
        body, name="mla_prep", grid=(T // tm,),
        in_specs=[pl.BlockSpec((tm, 256), lambda i: (i, 0)), pl.BlockSpec((tm, 256), lambda i: (i, 1)),
                  pl.BlockSpec((tm, 128), lambda i: (i, 12)), full((1, 256)), full((1, 256)),
                  full((256, HW)), full((256, HW)), full((256, 512)), tab, tab, tab],
        out_specs=[pl.BlockSpec((tm, HW), lambda i: (i, 0)), pl.BlockSpec((tm, HW), lambda i: (i, 0)),
                   pl.BlockSpec((tm, 512), lambda i: (i, 0))],
        out_shape=[jax.ShapeDtypeStruct((T, HW), BF16), jax.ShapeDtypeStruct((T, HW), BF16),
                   jax.ShapeDtypeStruct((T, 512), BF16)],
        compiler_params=_params(40, 1),
    )(*_hbm(z0, z0, z0, gq, gkv, wq, wk, wv, rc, rs1, rs2))


def _flash_fwd(q, k, v, plan=None):
    T = q.shape[0]
    bq = min(2 * ROW_BLOCK, T)
    nq = T // bq
    pairs = [(i, j) for i in range(nq) for j in range(i + 1)]
    imap, jmap = (jnp.array(m, jnp.int32) for m in zip(*pairs))
    grid = (4, len(pairs))

    def body(imap_ref, jmap_ref, *refs):
        (q_ref, k_ref, v_ref), (o_ref, lse_ref), (m_sc, acc_sc), pctx = _split_refs(refs, 3, 2, 2, plan)
        _plan_start(plan, pctx, grid)
        i, j = imap_ref[pl.program_id(1)], jmap_ref[pl.program_id(1)]
        first = lax.broadcasted_iota(jnp.int32, (bq, 128), 1) < 64

        @pl.when(j == 0)
        def _():
            m_sc[...] = jnp.full_like(m_sc, -jnp.inf)
            acc_sc[...] = jnp.zeros_like(acc_sc)

        def step(masked):
            vp = v_ref[...]
            for h in range(2):
                sl = slice(h * 128, (h + 1) * 128)
                s = lax.dot_general(q_ref[:, sl], k_ref[:, sl], NT_DIMS, preferred_element_type=F32)
                if masked:
                    rows = lax.broadcasted_iota(jnp.int32, (bq, bq), 0)
                    cols = lax.broadcasted_iota(jnp.int32, (bq, bq), 1)
                    s = jnp.where(cols <= rows, s, -jnp.inf)
                m_prev = m_sc[h, :, 0:1]
                m_new = jnp.maximum(m_prev, jnp.max(s, axis=1, keepdims=True))
                alpha = jnp.exp(m_prev - m_new)
                p = jnp.exp(s - m_new).astype(BF16)
                vh = jnp.where(first if h == 0 else jnp.logical_not(first), vp, jnp.ones_like(vp))
                acc_sc[h] = acc_sc[h] * alpha + jnp.dot(p, vh, preferred_element_type=F32)
                m_sc[h] = jnp.broadcast_to(m_new, (bq, 128))

        @pl.when(j < i)
        def _():
            step(False)

        @pl.when(j == i)
        def _():
            step(True)
            a0, a1 = acc_sc[0], acc_sc[1]
            l0, l1 = pltpu.roll(a0, 64, 1), pltpu.roll(a1, 64, 1)
            o_ref[...] = jnp.where(first, a0 / l0, a1 / l1).astype(BF16)
            lse_ref[...] = jnp.where(first, m_sc[0] + jnp.log(l0), m_sc[1] + jnp.log(l1))

        _plan_wait(plan, pctx, grid)

    qi = lambda hp, t, im, jm: (im[t], hp)
    kj = lambda hp, t, im, jm: (jm[t], hp)
    p_in, p_ospec, p_oshape, p_scr, p_alias = _plan_io(plan, 2 + 3, 2)
    return pl.pallas_call(
        body, name="flash_fwd",
        out_shape=[jax.ShapeDtypeStruct((T, 512), BF16), jax.ShapeDtypeStruct((T, 512), F32)] + p_oshape,
        grid_spec=pltpu.PrefetchScalarGridSpec(
            num_scalar_prefetch=2, grid=grid,
            in_specs=[pl.BlockSpec((bq, 256), qi), pl.BlockSpec((bq, 256), kj), pl.BlockSpec((bq, 128), kj)]
            + [_ANY] * len(p_in),
            out_specs=[pl.BlockSpec((bq, 128), qi), pl.BlockSpec((bq, 128), qi)] + p_ospec,
            scratch_shapes=[pltpu.VMEM((2, bq, 128), F32), pltpu.VMEM((2, bq, 128), F32)] + p_scr),
        input_output_aliases=p_alias, compiler_params=_params(56, 2),
    )(imap, jmap, *_hbm(q, k, v), *p_in)


def _attn_delta(dmix, o):
    T = o.shape[0]
    tm = min(ROW_BLOCK, T)
    blk = pl.BlockSpec((tm, 512), lambda i: (i, 0))

    def body(do_ref, o_ref, delta_ref, dob_ref):
        first = lax.broadcasted_iota(jnp.int32, (tm, 128), 1) < 64
        for hp in range(4):
            sl = slice(hp * 128, (hp + 1) * 128)
            prod = do_ref[:, sl] * o_ref[:, sl].astype(F32)
            d0 = jnp.sum(jnp.where(first, prod, 0.0), axis=1, keepdims=True)
            d1 = jnp.sum(jnp.where(first, 0.0, prod), axis=1, keepdims=True)
            delta_ref[:, sl] = jnp.where(first, d0, d1)
        dob_ref[...] = do_ref[...].astype(BF16)

    return pl.pallas_call(
        body, name="attn_delta", grid=(T // tm,), in_specs=[blk, blk], out_specs=[blk, blk],
        out_shape=[jax.ShapeDtypeStruct((T, 512), F32), jax.ShapeDtypeStruct((T, 512), BF16)],
        compiler_params=_params(32, 1),
    )(*_hbm(dmix, o))


def _flash_bwd(q, k, v, do_b, lse, delta, plan=None):
    T = q.shape[0]
    bq = min(2 * ROW_BLOCK, T)
    nq = T // bq
    pairs = [(i, j) for j in range(nq) for i in range(j, nq)]
    imap, jmap = (jnp.array(m, jnp.int32) for m in zip(*pairs))
    grid = (4, len(pairs))

    def body(imap_ref, jmap_ref, *refs):
        ((q_ref, k_ref, v_ref, do_ref, lse_ref, dl_ref), (dq_hbm, dk_ref, dv_ref), (dq_sc, dk_sc, dv_sc, sem),
         pctx) = _split_refs(refs, 6, 3, 4, plan)
        _plan_start(plan, pctx, grid)
        hp = pl.program_id(0)
        i, j = imap_ref[pl.program_id(1)], jmap_ref[pl.program_id(1)]
        first = lax.broadcasted_iota(jnp.int32, (bq, 128), 1) < 64

        @pl.when((j == 0) & (i == 0))
        def _():
            dq_sc[...] = jnp.zeros_like(dq_sc)

        @pl.when(i == j)
        def _():
            dk_sc[...] = jnp.zeros_like(dk_sc)
            dv_sc[...] = jnp.zeros_like(dv_sc)

        def tile(r0, nr, nc, masked):
            rs, cs = slice(r0, r0 + nr), slice(0, nc)
            vp = v_ref[cs, :]
            do = do_ref[rs, :]
            lanes = first[rs, :]
            for h in range(2):
                sl = slice(h * 128, (h + 1) * 128)
                qh, kh = q_ref[rs, sl], k_ref[cs, sl]
                s = lax.dot_general(qh, kh, NT_DIMS, preferred_element_type=F32)
                p = jnp.exp(s - lse_ref[rs, h * 64:h * 64 + 1])
                if masked:
                    rows = r0 + lax.broadcasted_iota(jnp.int32, (nr, nc), 0)
                    cols = lax.broadcasted_iota(jnp.int32, (nr, nc), 1)
                    p = jnp.where(cols <= rows, p, 0.0)
                do_h = jnp.where(lanes if h == 0 else jnp.logical_not(lanes), do, jnp.zeros_like(do))
                dv_sc[cs, :] += lax.dot_general(p.astype(BF16), do_h, TN_DIMS, preferred_element_type=F32)
                dp = lax.dot_general(do_h, vp, NT_DIMS, preferred_element_type=F32)
                ds = (p * (dp - dl_ref[rs, h * 64:h * 64 + 1])).astype(BF16)
                dq_sc[i, rs, sl] += jnp.dot(ds, kh, preferred_element_type=F32)
                dk_sc[cs, sl] += lax.dot_general(ds, qh, TN_DIMS, preferred_element_type=F32)

        @pl.when(i > j)
        def _():
            tile(0, bq, bq, False)

        @pl.when(i == j)
        def _():
            tile(0, bq // 2, bq // 2, True)
            tile(bq // 2, bq // 2, bq, True)

        @pl.when(i == nq - 1)
        def _():
            dk_ref[...] = dk_sc[...]
            dv_ref[...] = dv_sc[...]

        @pl.when((j == nq - 1) & (i == nq - 1))
        def _():
            cp = pltpu.make_async_copy(dq_sc, dq_hbm.at[hp], sem)
            cp.start()
            cp.wait()

        _plan_wait(plan, pctx, grid)

    qi = lambda hp, t, im, jm: (im[t], hp)
    kj = lambda hp, t, im, jm: (jm[t], hp)
    p_in, p_ospec, p_oshape, p_scr, p_alias = _plan_io(plan, 2 + 6, 3)
    return pl.pallas_call(
        body, name="flash_bwd",
        out_shape=[jax.ShapeDtypeStruct((4, nq, bq, 256), F32), jax.ShapeDtypeStruct((T, 1024), F32),
                   jax.ShapeDtypeStruct((T, 512), F32)] + p_oshape,
        grid_spec=pltpu.PrefetchScalarGridSpec(
            num_scalar_prefetch=2, grid=grid,
            in_specs=[pl.BlockSpec((bq, 256), qi), pl.BlockSpec((bq, 256), kj), pl.BlockSpec((bq, 128), kj),
                      pl.BlockSpec((bq, 128), qi), pl.BlockSpec((bq, 128), qi), pl.BlockSpec((bq, 128), qi)]
            + [_ANY] * len(p_in),
            out_specs=[_ANY, pl.BlockSpec((bq, 256), kj), pl.BlockSpec((bq, 128), kj)] + p_ospec,
            scratch_shapes=[pltpu.VMEM((nq, bq, 256), F32), pltpu.VMEM((bq, 256), F32), pltpu.VMEM((bq, 128), F32),
                            pltpu.SemaphoreType.DMA] + p_scr),
        input_output_aliases=p_alias, compiler_params=_params(56, 2),
    )(imap, jmap, *_hbm(q, k, v, do_b, lse, delta), *p_in)


def _mla_bwd(z0, dq4, dk, dv, gq, gkv, wq, wk, wv, rc, rs1, rs2, plan=None):
    T = z0.shape[0]
    tm = min(ROW_BLOCK, T)
    HW = HEADS * 128
    grid = (T // tm,)
    dq4 = dq4.reshape(4, T, 256)

    def body(*refs):
        ((cq_ref, ckv_ref, dq_ref, dk_ref, dv_ref, gq_ref, gkv_ref, wq_ref, wk_ref, wv_ref, c_ref, s1_ref, s2_ref),
         (dc_ref, dkr_ref, dwq_ref, dwk_ref, dwv_ref, dgq_ref, dgkv_ref), _, pctx) = _split_refs(refs, 13, 7, 0, plan)
        _plan_start(plan, pctx, grid)

        @pl.when(pl.program_id(0) == 0)
        def _():
            for ref in (dwq_ref, dwk_ref, dwv_ref, dgq_ref, dgkv_ref):
                ref[...] = jnp.zeros_like(ref)

        c, s1, s2 = c_ref[...], s1_ref[...], s2_ref[...]
        lane = lax.broadcasted_iota(jnp.int32, (tm, 128), 1)
        nq, xq, rq = _rms(cq_ref[...], gq_ref[...])
        nkv, xkv, rkv = _rms(ckv_ref[...], gkv_ref[...])
        nq_b, nkv_b = nq.astype(BF16), nkv.astype(BF16)

        dq_parts, dk_parts = [], []
        dkr = jnp.zeros((tm, 128), F32)
        for h in range(HEADS):
            blk = dq_ref[h // 2, :, (h % 2) * 128:(h % 2 + 1) * 128] * QK_SCALE
            dq_parts.append(_rope_t(blk, c, s1, s2).astype(BF16))
            kb = dk_ref[:, h * 128:(h + 1) * 128]
            dk_parts.append(jnp.where(lane < NOPE, kb, 0.0).astype(BF16))
            dkr = dkr + kb
        dq_b = jnp.concatenate(dq_parts, axis=1)
        dk_b = jnp.concatenate(dk_parts, axis=1)
        dv_b = dv_ref[...].astype(BF16)

        dwq_ref[...] += lax.dot_general(nq_b, dq_b, TN_DIMS, preferred_element_type=F32)
        dwk_ref[...] += lax.dot_general(nkv_b, dk_b, TN_DIMS, preferred_element_type=F32)
        dwv_ref[...] += lax.dot_general(nkv_b, dv_b, TN_DIMS, preferred_element_type=F32)
        dnq = lax.dot_general(dq_b, wq_ref[...], NT_DIMS, preferred_element_type=F32)
        dnkv = (lax.dot_general(dk_b, wk_ref[...], NT_DIMS, preferred_element_type=F32)
                + lax.dot_general(dv_b, wv_ref[...], NT_DIMS, preferred_element_type=F32))

        def rms_bwd(dn, xhat, rstd, g):
            dxh = dn * g
            return rstd * (dxh - xhat * jnp.mean(dxh * xhat, -1, keepdims=True))

        dc_ref[:, :256] = rms_bwd(dnq, xq, rq, gq_ref[...]).astype(BF16)
        dc_ref[:, 256:] = rms_bwd(dnkv, xkv, rkv, gkv_ref[...]).astype(BF16)
        dgq_ref[...] += _fold8(dnq * xq)
        dgkv_ref[...] += _fold8(dnkv * xkv)
        dkr = pltpu.roll(_rope_t(dkr, c, s1, s2), 64, 1)
        dkr_ref[...] = jnp.where(lane < ROPE, dkr, 0.0).astype(BF16)
        _plan_wait(plan, pctx, grid)

    full = lambda shape: pl.BlockSpec(shape, lambda i: (0,) * len(shape))
    tab = pl.BlockSpec((tm, 128), lambda i: (i, 0))
    p_in, p_ospec, p_oshape, p_scr, p_alias = _plan_io(plan, 13, 7)
    return pl.pallas_call(
        body, name="mla_bwd", grid=grid,
        in_specs=[pl.BlockSpec((tm, 256), lambda i: (i, 0)), pl.BlockSpec((tm, 256), lambda i: (i, 1)),
                  pl.BlockSpec((4, tm, 256), lambda i: (0, i, 0)),
                  pl.BlockSpec((tm, HW), lambda i: (i, 0)), pl.BlockSpec((tm, 512), lambda i: (i, 0)),
                  full((1, 256)), full((1, 256)), full((256, HW)), full((256, HW)), full((256, 512)), tab, tab, tab]
        + [_ANY] * len(p_in),
        out_specs=[pl.BlockSpec((tm, 512), lambda i: (i, 0)), tab, full((256, HW)), full((256, HW)),
                   full((256, 512)), full((8, 256)), full((8, 256))] + p_ospec,
        out_shape=[jax.ShapeDtypeStruct((T, 512), BF16), jax.ShapeDtypeStruct((T, 128), BF16),
                   jax.ShapeDtypeStruct((256, HW), F32), jax.ShapeDtypeStruct((256, HW), F32),
                   jax.ShapeDtypeStruct((256, 512), F32), jax.ShapeDtypeStruct((8, 256), F32),
                   jax.ShapeDtypeStruct((8, 256), F32)] + p_oshape,
        scratch_shapes=p_scr, input_output_aliases=p_alias, compiler_params=_params(48, 1),
    )(*_hbm(z0, z0, dq4, dk, dv, gq, gkv, wq, wk, wv, rc, rs1, rs2), *p_in)


def _sgu_fwd(z0, a_out, ln_g, ln_b, w, b_t):
    T = z0.shape[0]
    tm = min(ROW_BLOCK, T)
    W = SGU_G * SGU_C

    def body(u_ref, v_ref, a_ref, g_ref, b_ref, w_ref, bt_ref, o_ref):
        o_ref[:, :W] = a_ref[...]
        ug = _gelu(u_ref[...])
        xhat, _ = _ln_stats(_gelu(v_ref[...]))
        vn = (xhat * g_ref[...] + b_ref[...]).astype(BF16)
        tril = lax.broadcasted_iota(jnp.int32, (SGU_C, SGU_C), 0) >= lax.broadcasted_iota(jnp.int32, (SGU_C, SGU_C), 1)
        for g in range(SGU_G):
            cs = slice(g * SGU_C, (g + 1) * SGU_C)
            wg = jnp.where(tril, w_ref[g], 0.0).astype(BF16)
            bcol = bt_ref[:, g:g + 1]
            for c in range(tm // SGU_C):
                rs = slice(c * SGU_C, (c + 1) * SGU_C)
                mixed = jnp.dot(wg, vn[rs, cs], preferred_element_type=F32) + bcol
                o_ref[rs, W + g * SGU_C:W + (g + 1) * SGU_C] = (ug[rs, cs] * mixed).astype(BF16)

    full = lambda shape: pl.BlockSpec(shape, lambda i: (0,) * len(shape))
    return pl.pallas_call(
        body, name="sgu_fwd", grid=(T // tm,),
        in_specs=[pl.BlockSpec((tm, W), lambda i: (i, 1)), pl.BlockSpec((tm, W), lambda i: (i, 2)),
                  pl.BlockSpec((tm, W), lambda i: (i, 0)),
                  full((1, W)), full((1, W)), full((SGU_G, SGU_C, SGU_C)), full((SGU_C, SGU_G))],
        out_specs=pl.BlockSpec((tm, 2 * W), lambda i: (i, 0)),
        out_shape=jax.ShapeDtypeStruct((T, 2 * W), BF16),
        compiler_params=_params(32, 1),
    )(*_hbm(z0, z0, a_out, ln_g, ln_b, w, b_t))


def _sgu_bwd(z0, dmix, dc, dkr, ln_g, ln_b, w, b_t, plan=None):
    T = z0.shape[0]
    tm = min(ROW_BLOCK, T)
    W = SGU_G * SGU_C
    grid = (T // tm,)

    def body(*refs):
        ((u_ref, v_ref, do_ref, dc_ref, dkr_ref, g_ref, b_ref, w_ref, bt_ref),
         (dz_ref, dw_ref, db_ref, dlg_ref, dlb_ref), _, pctx) = _split_refs(refs, 9, 5, 0, plan)
        _plan_start(plan, pctx, grid)

        @pl.when(pl.program_id(0) == 0)
        def _():
            for ref in (dw_ref, db_ref, dlg_ref, dlb_ref):
                ref[...] = jnp.zeros_like(ref)

        dz_ref[:, :W] = dc_ref[...]
        dz_ref[:, 3 * W:] = dkr_ref[...]

        u, v, dout = u_ref[...], v_ref[...], do_ref[...]
        ug = _gelu(u)
        xhat, rstd = _ln_stats(_gelu(v))
        vn = (xhat * g_ref[...] + b_ref[...]).astype(BF16)
        dmixed = dout * ug
        dmixed_b = dmixed.astype(BF16)
        tril = lax.broadcasted_iota(jnp.int32, (SGU_C, SGU_C), 0) >= lax.broadcasted_iota(jnp.int32, (SGU_C, SGU_C), 1)
        lane = lax.broadcasted_iota(jnp.int32, (SGU_C, SGU_C), 1)
        dvn_cols = []
        for g in range(SGU_G):
            cs = slice(g * SGU_C, (g + 1) * SGU_C)
            wg = jnp.where(tril, w_ref[g], 0.0).astype(BF16)
            bcol = bt_ref[:, g:g + 1]
            dw_g = jnp.zeros((SGU_C, SGU_C), F32)
            db_g = jnp.zeros((SGU_C, 1), F32)
            dvn_rows = []
            for c in range(tm // SGU_C):
                rs = slice(c * SGU_C, (c + 1) * SGU_C)
                mixed = jnp.dot(wg, vn[rs, cs], preferred_element_type=F32) + bcol
                dz_ref[rs, W + g * SGU_C:W + (g + 1) * SGU_C] = (dout[rs, cs] * mixed * _gelu_grad(u[rs, cs])).astype(BF16)
                dm = dmixed_b[rs, cs]
                dw_g = dw_g + lax.dot_general(dm, vn[rs, cs], NT_DIMS, preferred_element_type=F32)
                db_g = db_g + jnp.sum(dmixed[rs, cs], axis=1, keepdims=True)
                dvn_rows.append(lax.dot_general(wg, dm, TN_DIMS, preferred_element_type=F32))
            dw_ref[g] += jnp.where(tril, dw_g, 0.0)
            db_ref[...] += jnp.where(lane == g, db_g, 0.0)
            dvn_cols.append(jnp.concatenate(dvn_rows, axis=0))
        dvn = jnp.concatenate(dvn_cols, axis=1)
        dxh = dvn * g_ref[...]
        m1 = jnp.mean(dxh, -1, keepdims=True)
        m2 = jnp.mean(dxh * xhat, -1, keepdims=True)
        dvg = rstd * (dxh - m1 - xhat * m2)
        dz_ref[:, 2 * W:3 * W] = (dvg * _gelu_grad(v)).astype(BF16)
        dlg_ref[...] += _fold8(dvn * xhat)
        dlb_ref[...] += _fold8(dvn)
        _plan_wait(plan, pctx, grid)

    full = lambda shape: pl.BlockSpec(shape, lambda i: (0,) * len(shape))
    p_in, p_ospec, p_oshape, p_scr, p_alias = _plan_io(plan, 9, 5)
    return pl.pallas_call(
        body, name="sgu_bwd", grid=grid,
        in_specs=[pl.BlockSpec((tm, W), lambda i: (i, 1)), pl.BlockSpec((tm, W), lambda i: (i, 2)),
                  pl.BlockSpec((tm, W), lambda i: (i, 1)), pl.BlockSpec((tm, W), lambda i: (i, 0)),
                  pl.BlockSpec((tm, 128), lambda i: (i, 0)),
                  full((1, W)), full((1, W)), full((SGU_G, SGU_C, SGU_C)), full((SGU_C, SGU_G))] + [_ANY] * len(p_in),
        out_specs=[pl.BlockSpec((tm, 3 * W + 128), lambda i: (i, 0)), full((SGU_G, SGU_C, SGU_C)),
                   full((SGU_C, SGU_C)), full((8, W)), full((8, W))] + p_ospec,
        out_shape=[jax.ShapeDtypeStruct((T, 3 * W + 128), BF16), jax.ShapeDtypeStruct((SGU_G, SGU_C, SGU_C), F32),
                   jax.ShapeDtypeStruct((SGU_C, SGU_C), F32), jax.ShapeDtypeStruct((8, W), F32),
                   jax.ShapeDtypeStruct((8, W), F32)] + p_oshape,
        scratch_shapes=p_scr, input_output_aliases=p_alias, compiler_params=_params(40, 1),
    )(*_hbm(z0, z0, dmix, dc, dkr, ln_g, ln_b, w, b_t), *p_in)


def _hg_lower_bound(lb_ref):
    a0, a1 = lb_ref[0:1, :], lb_ref[1:2, :]
    m = jnp.maximum(a0, a1)
    e0, e1 = jnp.exp(a0 - m), jnp.exp(a1 - m)
    return e1 / (e0 + e1)


def _running_sum(x, reverse=False):
    n = x.shape[0]
    row = lax.broadcasted_iota(jnp.int32, x.shape, 0)
    s = 1
    while s < n:
        if reverse:
            x = x + jnp.where(row < n - s, pltpu.roll(x, n - s, 0), 0.0)
        else:
            x = x + jnp.where(row >= s, pltpu.roll(x, s, 0), 0.0)
        s *= 2
    return x


def _hg_chunk(qc, fc, lb):
    C = HG_CHUNK
    rows = lax.broadcasted_iota(jnp.int32, (C, C), 0)
    cols = lax.broadcasted_iota(jnp.int32, (C, C), 1)
    rowid = lax.broadcasted_iota(jnp.int32, (C, 128), 0)
    sq, sg = _sigmoid(qc), _sigmoid(fc)
    qf = qc * sq
    gate = lb + (1.0 - lb) * sg
    kk = 1.0 - gate
    lg = jnp.log(gate)
    bcum = _running_sum(lg)
    b_mid = jnp.sum(jnp.where(rowid < C // 2, lg, 0.0), axis=0, keepdims=True)
    b_last = jnp.sum(lg, axis=0, keepdims=True)
    eq, ek, e, eh = jnp.exp(bcum - b_mid), jnp.exp(b_mid - bcum), jnp.exp(bcum), jnp.exp(b_last - bcum)
    qt, kt, qe, khat = qf * eq, kk * ek, qf * e, kk * eh
    a = lax.dot_general(qt.astype(BF16), kt.astype(BF16), NT_DIMS, preferred_element_type=F32)
    a = jnp.where(rows >= cols, a, 0.0)
    return dict(sq=sq, sg=sg, gate=gate, kk=kk, eq=eq, ek=ek, e=e, eh=eh, qt=qt, kt=kt, qe=qe, khat=khat, a=a,
                e_last=jnp.exp(b_last), tril=rows >= cols, rowid=rowid)


def _hgrn_fwd(z4, hg_lb, gnorm):
    T = z4.shape[1]
    tb = min(ROW_BLOCK, T)
    C = HG_CHUNK
    ncb = tb // C
    HPB = HG_HEADS_PER_STEP

    def body(q_ref, f_ref, i_ref, g_ref, lb_ref, gn_ref, y_ref, o_ref, st_ref, st_sc):
        @pl.when(pl.program_id(1) == 0)
        def _():
            st_sc[...] = jnp.zeros_like(st_sc)

        def chunk(c, carry):
            rs = pl.ds(pl.multiple_of(c * C, C), C)
            for hh in range(HPB):
                hs = slice(hh * 128, (hh + 1) * 128)
                lb = _hg_lower_bound(lb_ref.at[:, hs])
                v_b = i_ref[rs, hs].astype(BF16)
                gc = g_ref[rs, hs]
                x = _hg_chunk(q_ref[rs, hs], f_ref[rs, hs], lb)
                st = st_sc[hh]
                st_ref[hh, c] = st
                o = (jnp.dot(x["a"].astype(BF16), v_b, preferred_element_type=F32)
                     + lax.dot_general(x["qe"].astype(BF16), st.astype(BF16), NT_DIMS, preferred_element_type=F32))
                st_sc[hh] = st * x["e_last"] + lax.dot_general(v_b, x["khat"].astype(BF16), TN_DIMS,
                                                               preferred_element_type=F32)
                o_ref[rs, hs] = o
                n = o * lax.rsqrt(jnp.mean(o * o, -1, keepdims=True) + EPS)
                y_ref[rs, hs] = (n * gn_ref[:, hs] * (gc * _sigmoid(gc))).astype(BF16)
            return carry

        lax.fori_loop(0, ncb, chunk, 0, unroll=4)

    W = 128 * HPB
    zb = lambda k: pl.BlockSpec((None, tb, W), lambda h, t: (k, t, h))
    out = pl.BlockSpec((tb, W), lambda h, t: (t, h))
    return pl.pallas_call(
        body, name="hgrn_fwd", grid=(HEADS // HPB, T // tb),
        in_specs=[zb(0), zb(1), zb(2), zb(3), pl.BlockSpec((2, W), lambda h, t: (0, h)),
                  pl.BlockSpec((1, W), lambda h, t: (0, h))],
        out_specs=[out, out, pl.BlockSpec((HPB, ncb, 128, 128), lambda h, t: (h, t, 0, 0))],
        out_shape=[jax.ShapeDtypeStruct((T, D), BF16), jax.ShapeDtypeStruct((T, D), F32),
                   jax.ShapeDtypeStruct((HEADS, T // C, 128, 128), F32)],
        scratch_shapes=[pltpu.VMEM((HPB, 128, 128), F32)],
        compiler_params=_params(48, 2),
    )(*_hbm(z4, z4, z4, z4, hg_lb, gnorm))


def _hgrn_bwd(z4, o_raw, dy, states, hg_lb, gnorm):
    T = z4.shape[1]
    tb = min(ROW_BLOCK, T)
    C = HG_CHUNK
    ncb = tb // C
    nt = T // tb
    HPB = HG_HEADS_PER_STEP

    def body(q_ref, f_ref, i_ref, g_ref, o_ref, dy_ref, st_ref, lb_ref, gn_ref, dz_ref, dlb_ref, dgn_ref, dst_sc):
        @pl.when(pl.program_id(1) == 0)
        def _():
            dst_sc[...] = jnp.zeros_like(dst_sc)
            dlb_ref[...] = jnp.zeros_like(dlb_ref)
            dgn_ref[...] = jnp.zeros_like(dgn_ref)

        def chunk(cc, carry):
            for hh in range(HPB):
                one_head(ncb - 1 - cc, hh, slice(hh * 128, (hh + 1) * 128))
            return carry

        def one_head(c, hh, hs):
            rs = pl.ds(pl.multiple_of(c * C, C), C)
            lb = _hg_lower_bound(lb_ref.at[:, hs])
            gn = gn_ref[:, hs]
            qc, gc = q_ref[rs, hs], g_ref[rs, hs]
            v_b = i_ref[rs, hs].astype(BF16)
            x = _hg_chunk(qc, f_ref[rs, hs], lb)
            st, dst = st_ref[hh, c], dst_sc[hh]
            st_b, dst_b = st.astype(BF16), dst.astype(BF16)
            o, dyc = o_ref[rs, hs], dy_ref[rs, hs]
            sgg = _sigmoid(gc)
            sil = gc * sgg
            rstd = lax.rsqrt(jnp.mean(o * o, -1, keepdims=True) + EPS)
            n = o * rstd
            dgn_ref[:, hs] += _fold8(dyc * n * sil)
            dn = dyc * gn * sil
            do = rstd * (dn - n * jnp.mean(dn * n, -1, keepdims=True))
            dg = dyc * n * gn * (sgg * (1.0 + gc * (1.0 - sgg)))
            do_b = do.astype(BF16)
            da = jnp.where(x["tril"], lax.dot_general(do_b, v_b, NT_DIMS, preferred_element_type=F32), 0.0).astype(BF16)
            qt_b, kt_b, qe_b, khat_b = (x[n_].astype(BF16) for n_ in ("qt", "kt", "qe", "khat"))
            dv = (lax.dot_general(x["a"].astype(BF16), do_b, TN_DIMS, preferred_element_type=F32)
                  + lax.dot_general(khat_b, dst_b, NT_DIMS, preferred_element_type=F32))
            dqt = jnp.dot(da, kt_b, preferred_element_type=F32)
            dqe = jnp.dot(do_b, st_b, preferred_element_type=F32)
            dkt = lax.dot_general(da, qt_b, TN_DIMS, preferred_element_type=F32)
            dkhat = jnp.dot(v_b, dst_b, preferred_element_type=F32)
            dst_sc[hh] = lax.dot_general(do_b, qe_b, TN_DIMS, preferred_element_type=F32) + dst * x["e_last"]
            de_last = jnp.sum(st * dst, axis=0, keepdims=True)
            dqf = dqt * x["eq"] + dqe * x["e"]
            dkk = dkt * x["ek"] + dkhat * x["eh"]
            dkh_kh = dkhat * x["khat"]
            db = dqt * qt_b.astype(F32) - dkt * kt_b.astype(F32) + dqe * x["qe"] - dkh_kh
            db_last = jnp.sum(dkh_kh, axis=0, keepdims=True) + de_last * x["e_last"]
            db = db + jnp.where(x["rowid"] == C - 1, db_last, 0.0)
            dlg = _running_sum(db, reverse=True)
            dgate = dlg / x["gate"] - dkk
            sg, sq = x["sg"], x["sq"]
            dlb_ref[:, hs] += _fold8(dgate * (1.0 - sg)) * (lb * (1.0 - lb))
            dz_ref[0, rs, hs] = (dqf * (sq * (1.0 + qc * (1.0 - sq)))).astype(BF16)
            dz_ref[1, rs, hs] = (dgate * (1.0 - lb) * sg * (1.0 - sg)).astype(BF16)
            dz_ref[2, rs, hs] = dv.astype(BF16)
            dz_ref[3, rs, hs] = dg.astype(BF16)

        lax.fori_loop(0, ncb, chunk, 0, unroll=4)

    W = 128 * HPB
    zb = lambda k: pl.BlockSpec((None, tb, W), lambda h, t: (k, nt - 1 - t, h))
    blk = pl.BlockSpec((tb, W), lambda h, t: (nt - 1 - t, h))
    acc = pl.BlockSpec((8, W), lambda h, t: (0, h))
    return pl.pallas_call(
        body, name="hgrn_bwd", grid=(HEADS // HPB, nt),
        in_specs=[zb(0), zb(1), zb(2), zb(3), blk, blk,
                  pl.BlockSpec((HPB, ncb, 128, 128), lambda h, t: (h, nt - 1 - t, 0, 0)),
                  pl.BlockSpec((2, W), lambda h, t: (0, h)), pl.BlockSpec((1, W), lambda h, t: (0, h))],
        out_specs=[pl.BlockSpec((4, tb, W), lambda h, t: (0, nt - 1 - t, h)), acc, acc],
        out_shape=[jax.ShapeDtypeStruct((4, T, D), BF16), jax.ShapeDtypeStruct((8, D), F32),
                   jax.ShapeDtypeStruct((8, D), F32)],
        scratch_shapes=[pltpu.VMEM((HPB, 128, 128), F32)],
        compiler_params=_params(48, 2),
    )(*_hbm(z4, z4, z4, z4, o_raw, dy, states, hg_lb, gnorm))


def _adamw(w, g, m, v, *, name, with_g=False):
    R, L = w.shape
    tr = R if R <= 512 else 512
    assert R % tr == 0
    blk = pl.BlockSpec((tr, L), lambda i: (i, 0))
    c1, c2 = 1.0 - B1 ** STEP, 1.0 - B2 ** STEP
    n_out = 4 if with_g else 3

    def body(w_ref, g_ref, m_ref, v_ref, *o_refs):
        d_ref, mo_ref, vo_ref = o_refs[-3:]
        g_ = g_ref[...]
        m_ = B1 * m_ref[...] + (1.0 - B1) * g_
        v_ = B2 * v_ref[...] + (1.0 - B2) * (g_ * g_)
        if with_g:
            o_refs[0][...] = g_
        d_ref[...] = -LR * ((m_ / c1) / (jnp.sqrt(v_ / c2) + ADAM_EPS) + WD * w_ref[...])
        mo_ref[...] = m_
        vo_ref[...] = v_

    sds = jax.ShapeDtypeStruct((R, L), F32)
    return pl.pallas_call(
        body, name=name, grid=(R // tr,), in_specs=[blk] * 4, out_specs=[blk] * n_out, out_shape=[sds] * n_out,
        compiler_params=_params(32, 1),
    )(*_hbm(w, g, m, v))


def _adamw_rows(w, m, v, gbufs, row0, *, name):
    L, R, C = w.shape
    tr = min(R, 256)
    n_in, n_out = 3, 2
    assert R % tr == 0 and len(gbufs) == L
    steps = [(l, i * tr) for l in range(L) for i in range(R // tr)]
    c1, c2 = 1.0 - B1 ** STEP, 1.0 - B2 ** STEP

    def body(*refs):
        w_ref, m_ref, v_ref = refs[:3]
        g_refs, out_refs = refs[3:3 + L], refs[3 + L:7 + L]
        ibuf, obuf, isem, osem = refs[7 + L:]

        def reads(s):
            l, r0 = steps[s]
            srcs = [w_ref.at[l, pl.ds(r0, tr)], m_ref.at[l, pl.ds(r0, tr)], v_ref.at[l, pl.ds(r0, tr)],
                    g_refs[l].at[pl.ds(row0 + r0, tr)]]
            return [pltpu.make_async_copy(src, ibuf.at[s % n_in, k], isem.at[s % n_in, k]) for k, src in enumerate(srcs)]

        def writes(s):
            l, r0 = steps[s]
            return [pltpu.make_async_copy(obuf.at[s % n_out, k], o_ref.at[l, pl.ds(r0, tr)], osem.at[s % n_out, k])
                    for k, o_ref in enumerate(out_refs)]

        for s in range(min(n_in, len(steps))):
            for c in reads(s):
                c.start()
        for s in range(len(steps)):
            for c in reads(s):
                c.wait()
            if s >= n_out:
                for c in writes(s - n_out):
                    c.wait()
            i_, o_ = s % n_in, s % n_out
            g_ = ibuf[i_, 3]
            m_ = B1 * ibuf[i_, 1] + (1.0 - B1) * g_
            v_ = B2 * ibuf[i_, 2] + (1.0 - B2) * (g_ * g_)
            obuf[o_, 0] = g_
            obuf[o_, 1] = -LR * ((m_ / c1) / (jnp.sqrt(v_ / c2) + ADAM_EPS) + WD * ibuf[i_, 0])
            obuf[o_, 2] = m_
            obuf[o_, 3] = v_
            for c in writes(s):
                c.start()
            if s + n_in < len(steps):
                for c in reads(s + n_in):
                    c.start()
        for s in range(max(0, len(steps) - n_out), len(steps)):
            for c in writes(s):
                c.wait()

    sds = jax.ShapeDtypeStruct((L, R, C), F32)
    return pl.pallas_call(
        body, name=name, in_specs=[_ANY] * (3 + L), out_specs=[_ANY] * 4, out_shape=[sds] * 4,
        scratch_shapes=[pltpu.VMEM((n_in, 4, tr, C), F32), pltpu.VMEM((n_out, 4, tr, C), F32),
                        pltpu.SemaphoreType.DMA((n_in, 4)), pltpu.SemaphoreType.DMA((n_out, 4))],
        compiler_params=_params(40),
    )(*_hbm(w, m, v, *gbufs))


def _add_pairs(g, theirs, ids, *, name):
    n, R, L = theirs.shape
    tr = math.gcd(R, 128)
    nb = R // tr

    def body(ids_ref, a_ref, b_ref, o_ref):
        o_ref[...] = (a_ref[...].astype(F32) + b_ref[...].astype(F32)).astype(BF16)

    blk = pl.BlockSpec((n, tr, L), lambda i, ids: (0, i, 0))
    return pl.pallas_call(
        body, name=name, out_shape=jax.ShapeDtypeStruct((n, R, L), BF16),
        grid_spec=pltpu.PrefetchScalarGridSpec(
            num_scalar_prefetch=1, grid=(nb,),
            in_specs=[pl.BlockSpec((n, tr, L), lambda i, ids: (0, ids[1] * nb + i, 0)), blk], out_specs=blk),
        compiler_params=_params(16, 1),
    )(ids, *_hbm(g, theirs))


def _sum_chips(pair, parts, ids, *, name):
    _, R, L = parts.shape
    tr = math.gcd(R, 128)

    def body(ids_ref, o_ref, r_ref, out_ref):
        out_ref[...] = ((o_ref[...].astype(F32) + r_ref[0].astype(F32)) + r_ref[1].astype(F32)) + r_ref[2].astype(F32)

    return pl.pallas_call(
        body, name=name, out_shape=jax.ShapeDtypeStruct((2, R, L), F32),
        grid_spec=pltpu.PrefetchScalarGridSpec(
            num_scalar_prefetch=1, grid=(R // tr,),
            in_specs=[pl.BlockSpec((None, tr, L), lambda i, ids: (ids[0], i, 0)),
                      pl.BlockSpec((3, tr, L), lambda i, ids: (0, i, 0))],
            out_specs=pl.BlockSpec((None, tr, L), lambda i, ids: (ids[1], i, 0))),
        compiler_params=_params(32, 1),
    )(ids, *_hbm(pair, parts))


def _mesh_ids():
    x, y, c = _mesh_pos()
    return jnp.stack([2 * x + y, c]).astype(jnp.int32)


def _place_shard(rows, ids, *, name):
    R, L = rows.shape
    tr = 128

    def body(ids_ref, in_ref, out_ref):
        out_ref[...] = in_ref[...].astype(BF16)

    return pl.pallas_call(
        body, name=name, out_shape=jax.ShapeDtypeStruct((4, R, L), BF16),
        grid_spec=pltpu.PrefetchScalarGridSpec(
            num_scalar_prefetch=1, grid=(R // tr,), in_specs=[pl.BlockSpec((tr, L), lambda i, ids: (i, 0))],
            out_specs=pl.BlockSpec((None, tr, L), lambda i, ids: (ids[0], i, 0))),
        compiler_params=_params(16, 1),
    )(ids, *_hbm(rows))


IN_E_SHARD, IN_E_LAYOUT = 392, 1664


def _in_e_runs():
    runs = []
    for lo, hi, dst in ((0, 512, 0), (512, 544, 1536), (544, 1568, 512)):
        while lo < hi:
            j = lo // IN_E_SHARD
            wd = min(hi, IN_E_SHARD * (j + 1)) - lo
            runs.append((j, lo - IN_E_SHARD * j, dst, wd))
            lo, dst = lo + wd, dst + wd
    return runs


def _in_e_to_layout(shards):
    tr = 256

    def body(s_ref, o_ref, t_ref):
        t_ref[...] = jnp.zeros_like(t_ref)
        for j in range(4):
            s = s_ref[j].astype(F32)
            for _, c, dst, wd in (r for r in _in_e_runs() if r[0] == j):
                t_ref[:, dst:dst + wd] = s[:, c:c + wd]
        o_ref[...] = t_ref[...].astype(BF16)

    return pl.pallas_call(
        body, name="in_e_to_layout", out_shape=jax.ShapeDtypeStruct((D, IN_E_LAYOUT), BF16), grid=(D // tr,),
        in_specs=[pl.BlockSpec((4, tr, IN_E_SHARD), lambda i: (0, i, 0))],
        out_specs=pl.BlockSpec((tr, IN_E_LAYOUT), lambda i: (i, 0)),
        scratch_shapes=[pltpu.VMEM((tr, IN_E_LAYOUT), F32)], compiler_params=_params(16, 1),
    )(*_hbm(shards))


def _in_e_from_layout(g):
    tr = 256

    def body(g_ref, o_ref, t_ref):
        g_ = g_ref[...]
        for j, c, src, wd in _in_e_runs():
            t_ref[j, :, c:c + wd] = g_[:, src:src + wd]
        o_ref[...] = t_ref[...].astype(BF16)

    return pl.pallas_call(
        body, name="in_e_from_layout", out_shape=jax.ShapeDtypeStruct((4, D, IN_E_SHARD), BF16), grid=(D // tr,),
        in_specs=[pl.BlockSpec((tr, IN_E_LAYOUT), lambda i: (i, 0))],
        out_specs=pl.BlockSpec((4, tr, IN_E_SHARD), lambda i: (0, i, 0)),
        scratch_shapes=[pltpu.VMEM((4, tr, IN_E_SHARD), F32)], compiler_params=_params(16, 1),
    )(*_hbm(g))


def _place_weights(pieces, buffer_rows, ids, *, plan=None):
    tr = 256
    steps, s = [], 0
    for arr, layer, buf, row0 in pieces:
        nblk = arr.shape[1] // tr
        steps.append((s, nblk))
        s += nblk
    total = s
    buf_start = [min(st for (st, _), p in zip(steps, pieces) if p[2] == k) for k in range(len(buffer_rows))]
    grid = (total,)
    n_in = len(pieces)

    def body(*refs):
        ins, outs, _, pctx = _split_refs(refs[1:], n_in, len(buffer_rows), 0, plan)
        _plan_start(plan, pctx, grid)
        i = pl.program_id(0)
        for (st, nblk), (_, _, buf, _), ref in zip(steps, pieces, ins):
            @pl.when((i >= st) & (i < st + nblk))
            def _(ref=ref, buf=buf):
                outs[buf][...] = ref[...].astype(BF16)
        _plan_wait(plan, pctx, grid)

    in_specs = [pl.BlockSpec((None, tr, D), lambda i, ids, layer=layer, st=st, nblk=nblk:
                             (layer, jnp.clip(i - st, 0, nblk - 1), 0))
                for (st, nblk), (_, layer, _, _) in zip(steps, pieces)]
    out_specs = [pl.BlockSpec((None, tr, D), lambda i, ids, st=st, nb=rows // tr: (ids[0], jnp.clip(i - st, 0, nb - 1), 0))
                 for st, rows in zip(buf_start, buffer_rows)]
    p_in, p_ospec, p_oshape, p_scr, p_alias = _plan_io(plan, 1 + n_in, len(buffer_rows))
    return pl.pallas_call(
        body, name="place_weights",
        out_shape=[jax.ShapeDtypeStruct((4, rows, D), BF16) for rows in buffer_rows] + p_oshape,
        grid_spec=pltpu.PrefetchScalarGridSpec(
            num_scalar_prefetch=1, grid=grid, in_specs=in_specs + [_ANY] * len(p_in), out_specs=out_specs + p_ospec,
            scratch_shapes=p_scr),
        input_output_aliases=p_alias, compiler_params=_params(16, 1),
    )(ids, *_hbm(*[p[0] for p in pieces]), *p_in)


def _remote(src, dst, send_sem, recv_sem, to):
    return pltpu.make_async_remote_copy(src_ref=src, dst_ref=dst, send_sem=send_sem, recv_sem=recv_sem,
                                        device_id=to, device_id_type=MESH_IDS)


def _rows(ref, lead, start, size):
    return ref.at[tuple(pl.ds(0, n) for n in ref.shape[:lead]) + (pl.ds(start, size),)]


def _other_chips():
    x, y, _ = _mesh_pos()
    return [(1 - x, y), (x, 1 - y), (1 - x, 1 - y)]


def _plan_gather_ici(bufs):
    n = len(bufs)

    def copies(outs, send, recv):
        x, y, c = _mesh_pos()
        res = []
        for b in range(n):
            half = bufs[b].shape[1] // 2
            mine = _rows(outs[b].at[2 * x + y], 0, c * half, half)
            for j, (cx, cy) in enumerate(_other_chips()):
                res.append((_remote(mine, mine, send(3 * b + j), recv(3 * b + j), (cx, cy, c)),
                            _remote(mine, _rows(outs[b].at[2 * cx + cy], 0, c * half, half),
                                    send(3 * b + j), recv(3 * b + j), (x, y, c))))
        return res

    def start(ins, outs, send, recv, loc):
        for out_cp, _ in copies(outs, send, recv):
            out_cp.start()

    def wait(ins, outs, send, recv, loc):
        for out_cp, in_cp in copies(outs, send, recv):
            in_cp.wait_recv()
            out_cp.wait_send()

    outs = [jax.ShapeDtypeStruct(b.shape, b.dtype) for b in bufs]
    return _Plan(bufs, outs, 3 * n, 0, start, wait, aliases={b: b for b in range(n)})


def _plan_gather_forward(bufs):
    n = len(bufs)

    def copies(outs, send, recv):
        x, y, c = _mesh_pos()
        res = []
        for b in range(n):
            half = bufs[b].shape[1] // 2
            for j, (cx, cy) in enumerate(_other_chips()):
                slot = outs[b].at[2 * cx + cy]
                res.append((_remote(_rows(slot, 0, c * half, half), _rows(slot, 0, c * half, half),
                                    send(3 * b + j), recv(3 * b + j), (x, y, 1 - c)),
                            _remote(_rows(slot, 0, c * half, half), _rows(slot, 0, (1 - c) * half, half),
                                    send(3 * b + j), recv(3 * b + j), (x, y, c))))
        return res

    def start(ins, outs, send, recv, loc):
        for out_cp, _ in copies(outs, send, recv):
            out_cp.start()

    def wait(ins, outs, send, recv, loc):
        for out_cp, in_cp in copies(outs, send, recv):
            in_cp.wait_recv()
            out_cp.wait_send()

    outs = [jax.ShapeDtypeStruct(b.shape, b.dtype) for b in bufs]
    return _Plan(bufs, outs, 3 * n, 0, start, wait, aliases={b: b for b in range(n)})


def _plan_pair_swap(g):
    half = g.shape[1] // 2

    def copy(ins, outs, send, recv, loc):
        x, y, c = _mesh_pos()
        return _remote(_rows(ins[0], 1, (1 - c) * half, half), outs[0], send(0), recv(0), (x, y, 1 - c))

    return _Plan([g], [jax.ShapeDtypeStruct((4, half, g.shape[2]), g.dtype)], 1, 0,
                 lambda *a: copy(*a).start(), lambda *a: copy(*a).wait())


def _plan_pair_gather(buf):
    def copies(ins, outs, send, recv, loc):
        x, y, c = _mesh_pos()
        return (_remote(outs[0].at[c], outs[0].at[c], send(0), recv(0), (x, y, 1 - c)),
                _remote(outs[0].at[c], outs[0].at[1 - c], send(0), recv(0), (x, y, c)))

    def wait(*a):
        out_cp, in_cp = copies(*a)
        in_cp.wait_recv()
        out_cp.wait_send()

    return _Plan([buf], [jax.ShapeDtypeStruct(buf.shape, buf.dtype)], 1, 0, lambda *a: copies(*a)[0].start(), wait,
                 aliases={0: 0})


def _plan_chip_scatter(p):
    def copies(ins, outs, send, recv, loc):
        _, _, c = _mesh_pos()
        return [_remote(ins[0].at[2 * cx + cy], outs[0].at[j], send(j), recv(j), (cx, cy, c))
                for j, (cx, cy) in enumerate(_other_chips())]

    def start(*a):
        for cp in copies(*a):
            cp.start()

    def wait(*a):
        for cp in copies(*a):
            cp.wait()

    return _Plan([p], [jax.ShapeDtypeStruct((3,) + p.shape[1:], p.dtype)], 3, 0, start, wait)


def _plan_exchange_all(vec):
    def copies(ins, outs, send, recv, loc):
        x, y, c = _mesh_pos()
        return [_remote(ins[0], outs[0].at[r - 1], send(r - 1), recv(r - 1), (x ^ (r >> 2), y ^ ((r >> 1) & 1), c ^ (r & 1)))
                for r in range(1, 8)]

    def start(*a):
        for cp in copies(*a):
            cp.start()

    def wait(*a):
        for cp in copies(*a):
            cp.wait()

    return _Plan([vec], [jax.ShapeDtypeStruct((7,) + vec.shape, vec.dtype)], 7, 0, start, wait)


SMALL_LAYOUT = {
    "mla_gq": (0, 1, 256, (1, 256)), "mla_gkv": (1, 1, 256, (1, 256)), "sgu_ln_g": (2, 1, 512, (1, 512)),
    "sgu_ln_b": (3, 1, 512, (1, 512)), "sgu_w": (4, 64, 1024, (64, 1024)), "sgu_b": (68, 1, 512, (1, 512)),
    "hg_lb": (69, 2, 1024, (2, 1024)), "hg_gnorm": (71, 1, 1024, (1, 256)), "ln1_g": (72, 2, 1024, (2, 1024)),
    "ln1_b": (74, 2, 1024, (2, 1024)), "ln2_g": (76, 2, 1024, (2, 1024)), "ln2_b": (78, 2, 1024, (2, 1024)),
}


def _small_pack(dgq, dgkv, dslg, dslb, dsw, dsb, dlb, dgn, ln_parts, sq_err):
    flat_ln = [p for pair in ln_parts for p in pair]

    def body(*refs):
        gq_ref, gkv_ref, slg_ref, slb_ref, sw_ref, sb_ref, lb_ref, gn_ref = refs[:8]
        ln_refs, err_ref, out_ref, t_sc = refs[8:16], refs[16], refs[17], refs[18]
        s8 = lambda ref: jnp.sum(ref[...], axis=0, keepdims=True)
        out_ref[...] = jnp.zeros_like(out_ref)
        out_ref[0:1, 0:256] = s8(gq_ref)
        out_ref[1:2, 0:256] = s8(gkv_ref)
        out_ref[2:3, 0:512] = s8(slg_ref)
        out_ref[3:4, 0:512] = s8(slb_ref)
        out_ref[4:68, :] = sw_ref[...]
        t_sc[...] = sb_ref[...].T
        for g in range(SGU_G):
            out_ref[68:69, g * SGU_C:(g + 1) * SGU_C] = t_sc[g:g + 1, :]
        d_lb1 = s8(lb_ref)
        out_ref[69:70, :] = -d_lb1
        out_ref[70:71, :] = d_lb1
        out_ref[71:72, :] = s8(gn_ref)
        for k, ref in enumerate(ln_refs):
            out_ref[72 + k:73 + k, :] = s8(ref)
        out_ref[0:1, 1023:1024] = jnp.sum(s8(err_ref), axis=1, keepdims=True) * (0.5 / D)

    vm = pl.BlockSpec(memory_space=pltpu.VMEM)
    return pl.pallas_call(
        body, name="small_grad_pack", in_specs=[vm] * 17, out_specs=vm,
        out_shape=jax.ShapeDtypeStruct((SMALL_ROWS, 1024), F32), scratch_shapes=[pltpu.VMEM((SGU_C, SGU_C), F32)],
        compiler_params=_params(16),
    )(dgq, dgkv, dslg, dslb, dsw.reshape(64, 1024), dsb, dlb, dgn, *flat_ln, sq_err)


def _small_update(vec, others, ids, w, m, v):
    names = list(SMALL_LAYOUT)
    n = len(names)
    c1, c2 = 1.0 - B1 ** STEP, 1.0 - B2 ** STEP
    have_others = others is not None

    def body(*refs):
        ids_ref, v_ref = refs[0], refs[1]
        k = 2 + have_others
        w_refs, m_refs, v_refs = refs[k:k + n], refs[k + n:k + 2 * n], refs[k + 2 * n:k + 3 * n]
        outs = refs[k + 3 * n:]
        row0_ref, tot_sc = outs[0], outs[-1]
        total = v_ref[...]
        if have_others:
            me = 2 * ids_ref[0] + ids_ref[1]
            total = None
            for d in range(8):
                rel = d ^ me
                term = jnp.where(rel == 0, v_ref[...], refs[2][jnp.maximum(rel - 1, 0)])
                total = term if total is None else total + term
        tot_sc[...] = total
        row0_ref[...] = tot_sc[0:1, :]
        for i, name in enumerate(names):
            r0, nr, width, _ = SMALL_LAYOUT[name]
            if name == "hg_gnorm":
                g_ = tot_sc[r0:r0 + 1, 0:256]
                for chip in range(1, 4):
                    g_ = jnp.where(ids_ref[0] == chip, tot_sc[r0:r0 + 1, chip * 256:(chip + 1) * 256], g_)
            else:
                g_ = tot_sc[r0:r0 + nr, 0:width]
            m_ = B1 * m_refs[i][...] + (1.0 - B1) * g_
            v_ = B2 * v_refs[i][...] + (1.0 - B2) * (g_ * g_)
            go, do, mo, vo = outs[1 + 4 * i:5 + 4 * i]
            go[...] = g_
            do[...] = -LR * ((m_ / c1) / (jnp.sqrt(v_ / c2) + ADAM_EPS) + WD * w_refs[i][...])
            mo[...] = m_
            vo[...] = v_

    full = lambda shape: pl.BlockSpec(shape, lambda i, ids, nd=len(shape): (0,) * nd)
    kshapes = [SMALL_LAYOUT[name][3] for name in names]
    operands = [vec] + ([others] if have_others else []) + [d[name] for d in (w, m, v) for name in names]
    out_shapes = [jax.ShapeDtypeStruct((1, 1024), F32)] + [jax.ShapeDtypeStruct(s, F32) for s in kshapes for _ in range(4)]
    res = pl.pallas_call(
        body, name="small_update", out_shape=out_shapes,
        grid_spec=pltpu.PrefetchScalarGridSpec(
            num_scalar_prefetch=1, grid=(1,), in_specs=[full(o.shape) for o in operands],
            out_specs=[full(s.shape) for s in out_shapes],
            scratch_shapes=[pltpu.VMEM((SMALL_ROWS, 1024), F32)]),
        compiler_params=_params(32, 1),
    )(ids, *operands)
    return res[0], {name: tuple(res[1 + 4 * i:5 + 4 * i]) for i, name in enumerate(names)}


ROWS_L1, ROWS_L0, ROWS_ODD_W = 3328, 2048, 384
ODD_PARTS = (("w_out_e", (256, 1024)), ("w_in_e", (1024, 392)), ("w_qb", (256, 192)), ("w_kvb", (256, 256)))
ODD_W_PARTS = tuple(p for p in ODD_PARTS if p[0] != "w_in_e")


def _odd_rows(parts, dtype, layout, total, gnorm=None):
    rows = [parts[n].reshape(-1, 1024).astype(dtype) for n, _ in layout]
    used = sum(r.shape[0] for r in rows)
    if gnorm is not None:
        bits = lax.bitcast_convert_type(gnorm.reshape(-1), BF16).reshape(1, 512)
        rows.append(jnp.pad(bits, ((0, 15), (0, 512))))
        used += 16
    if total > used:
        rows.append(jnp.zeros((total - used, 1024), dtype))
    return jnp.concatenate(rows, axis=0)


def _odd_unrows(buf, layout, with_gnorm=False):
    out, off = {}, 0
    for n, shape in layout:
        nr = math.prod(shape) // 1024
        out[n] = buf[off:off + nr].reshape(shape)
        off += nr
    if with_gnorm:
        out["hg_gnorm"] = lax.bitcast_convert_type(buf[off, :512].reshape(256, 2), F32).reshape(1, 256)
    return out


def _rope_tables(positions):
    half = ROPE // 2
    inv_freq = ROPE_BASE ** (-jnp.arange(half, dtype=F32) / half)
    per_row = 128 // half
    ang = jnp.repeat(positions.astype(F32).reshape(-1, per_row), half, axis=1) * jnp.tile(inv_freq, per_row)
    cos, sin = jnp.cos(ang).reshape(-1, half), jnp.sin(ang).reshape(-1, half)
    T = cos.shape[0]
    one, z16, z32 = jnp.ones((T, NOPE), F32), jnp.zeros((T, half), F32), jnp.zeros((T, 32), F32)
    z64 = jnp.zeros((T, NOPE), F32)
    c = jnp.concatenate([one, cos, cos, z32], axis=1)
    s1 = jnp.concatenate([z64, -sin, z16, z32], axis=1)
    s2 = jnp.concatenate([z64, z16, sin, z32], axis=1)
    return c, s1, s2


def _local_step(x, positions, tgt, odd, bufs, P, exchange):
    T = x.shape[0]
    row = lambda a: a.reshape(1, -1)
    rc, rs1, rs2 = _rope_tables(positions)
    blk = lambda f: pl.BlockSpec((None, D, D), f)

    w_in = _in_e_to_layout(odd["w_in_e"])
    wq = jnp.pad(odd["w_qb"].reshape(256, HEADS, NOPE + ROPE), ((0, 0), (0, 0), (0, 32))).reshape(256, HEADS * 128)
    kvb = odd["w_kvb"].reshape(256, HEADS, NOPE + VDIM)
    wk = jnp.pad(kvb[:, :, :NOPE], ((0, 0), (0, 0), (0, 64))).reshape(256, HEADS * 128)
    wv = kvb[:, :, NOPE:].reshape(256, HEADS * VDIM)
    w_out_e = odd["w_out_e"]
    sgu_w = P["sgu_w"][0]
    sgu_bt = P["sgu_b"][0].T
    gq, gkv = P["mla_gq"], P["mla_gkv"]
    gnorm = P["hg_gnorm"]

    z0 = _matmul(x, w_in, name="in_proj_e", M=T, N=1664, K=D, tn=1664)[0]
    q, k, v = _mla_prep(z0, gq, gkv, wq, wk, wv, rc, rs1, rs2)
    if exchange:
        ids = _mesh_ids()
        placed = list(bufs)
        a_out, lse, wga, wgb = _flash_fwd(q, k, v, plan=_plan_gather_ici(placed[:2]))
    else:
        a_out, lse = _flash_fwd(q, k, v)
        wga, wgb, wgc = bufs
    mix0 = _sgu_fwd(z0, a_out, P["sgu_ln_g"], P["sgu_ln_b"], sgu_w, sgu_bt)
    res = _proj_ln(mix0, w_out_e, x, row(P["ln1_g"][0]), row(P["ln1_b"][0]), name="out_proj_ln_e",
                   plan=_plan_gather_forward([wga, wgb]) if exchange else None)
    r1, h1b = res[:2]
    if exchange:
        wga, wgb = res[2:]
    ln = lambda name, l: (row(P[name + "_g"][l]), row(P[name + "_b"][l]))
    res = _ffn_ln(h1b, wga, r1, *ln("ln2", 0), name="ffn_ln_0", prev_ln=ln("ln1", 0),
                  plan=_plan_gather_ici(placed[2:]) if exchange else None)
    ra0, r2, h2b = res[:3]
    z4 = _matmul(h2b, wgb, name="in_proj_o", M=T, N=4 * D, K=D, tn=2 * D, n_slots=True,
                 b_spec=pl.BlockSpec((2, D, D), lambda i, j, k: (j, 0, 0)),
                 out_shape=jax.ShapeDtypeStruct((4, T, D), F32),
                 o_spec=pl.BlockSpec((2, min(MM_ROWS, T), D), lambda i, j, k: (j, i, 0)))[0]
    y1, o_raw, states = _hgrn_fwd(z4, P["hg_lb"], gnorm)
    res2 = _proj_ln(y1, wgb, r2, *ln("ln1", 1), name="out_proj_ln_o", prev_ln=ln("ln2", 0), w_rowblk=4,
                    plan=_plan_gather_forward([res[3]]) if exchange else None)
    r3, h3b = res2[:2]
    if exchange:
        wgc = res2[2]
    ra1, r4, _ = _ffn_ln(h3b, wgc, r3, *ln("ln2", 1), name="ffn_ln_1", prev_ln=ln("ln1", 1))

    ln1_g, ln1_b, ln2_g, ln2_b = [None, None], [None, None], [None, None], [None, None]
    sq_err_parts = []

    def ffn_bwd(l, dh, r_out, ra, h_mid_b, g2, wg, rows, plan=None, loss_head=()):
        dr, dr_b, dg, db, *sq_err = _ln_bwd(dh, r_out, row(g2), name=f"ln2_bwd_{l}", loss_head=loss_head)
        sq_err_parts.extend(sq_err)
        ln2_g[l], ln2_b[l] = dg, db
        da, *extra = _matmul(dr_b, wg, tb=True, mul=ra, out_dtype=BF16, name=f"ffn_da_{l}", M=T, N=4 * D, K=D, tn=2 * D,
                             b_spec=pl.BlockSpec((2, D, D), lambda i, j, k: (j, 1, 0)), n_slots=True, plan=plan)
        gbuf = _matmul(ra, dr_b, ta=True, a_sq=True, name=f"ffn_dw2_{l}", M=4 * D, N=D, K=T, tm=1024, tk=DW_TOKENS // 2,
                       out_shape=jax.ShapeDtypeStruct((4, rows, D), BF16), o_spec=blk(lambda i, j, k: (i, 1, 0)))[0]
        gbuf = _matmul(h_mid_b, da, ta=True, name=f"ffn_dw1_{l}", M=D, N=4 * D, K=T, tm=1024, tk=DW_TOKENS, into=gbuf,
                       out_shape=jax.ShapeDtypeStruct((4, rows, D), BF16), o_spec=blk(lambda i, j, k: (j, 0, 0)))[0]
        dh_mid = _matmul(da, wg, tb=True, add=dr, add_scale=ALPHA, name=f"ffn_dh_{l}", M=T, N=D, K=4 * D, tk=2 * D,
                         b_spec=pl.BlockSpec((2, D, D), lambda i, j, k: (k, 0, 0)))[0]
        return dh_mid, gbuf, extra

    dh3, g1, _ = ffn_bwd(1, None, r4, ra1, h3b, P["ln2_g"][1], wgc, ROWS_L1, loss_head=(row(P["ln2_b"][1]), tgt))
    loss_parts = sq_err_parts[0]
    dr3, dr3_b, dg, db = _ln_bwd(dh3, r3, row(P["ln1_g"][1]), name="ln1_bwd_1")
    ln1_g[1], ln1_b[1] = dg, db
    g1_sds = jax.ShapeDtypeStruct((4, ROWS_L1, D), BF16)
    g1 = _matmul(y1, dr3_b, ta=True, name="dw_out_o", M=D, N=D, K=T, tm=256, tk=DW_TOKENS, into=g1, out_shape=g1_sds,
                 o_spec=pl.BlockSpec((None, 256, D), lambda i, j, k: (i, 12, 0)))[0]
    dmix1 = _matmul(dr3_b, wgb, tb=True, name="dmix_o", M=T, N=D, K=D, b_spec=_rows4_spec(4, 3), b_merge=(D, D))[0]
    dz4, dlb, dgn = _hgrn_bwd(z4, o_raw, dmix1, states, P["hg_lb"], gnorm)
    g1 = _matmul(h2b, dz4, ta=True, name="dw_in_o", M=D, N=4 * D, K=T, tm=1024, tk=DW_TOKENS, into=g1, out_shape=g1_sds,
                 b_spec=pl.BlockSpec((None, min(DW_TOKENS, T), D), lambda i, j, k: (j, k, 0)),
                 o_spec=blk(lambda i, j, k: (j, 2, 0)))[0]
    dh2 = _matmul(dz4, wgb, tb=True, add=dr3, add_scale=ALPHA, name="dh_in_o", M=T, N=D, K=4 * D, tk=2 * D,
                  a_spec=pl.BlockSpec((2, min(MM_ROWS, T), D), lambda i, j, k: (k, i, 0)),
                  b_spec=pl.BlockSpec((2, D, D), lambda i, j, k: (k, 0, 0)))[0]

    dh1, g0, swapped1 = ffn_bwd(0, dh2, r2, ra0, h1b, P["ln2_g"][0], wga, ROWS_L0,
                                plan=_plan_pair_swap(g1) if exchange else None)
    dr1, dr1_b, dg, db = _ln_bwd(dh1, r1, row(P["ln1_g"][0]), name="ln1_bwd_0")
    ln1_g[0], ln1_b[0] = dg, db
    godd = {"w_out_e": _matmul(mix0, dr1_b, ta=True, name="dw_out_e", M=D, N=D, K=T, tm=1024, tk=DW_TOKENS)[0]}
    dmix0, *swapped0 = _matmul(dr1_b, w_out_e, tb=True, name="dmix_e", M=T, N=D, K=D,
                               plan=_plan_pair_swap(g0) if exchange else None)
    delta, do_b = _attn_delta(dmix0, a_out)
    if exchange:
        pair1 = _add_pairs(g1, swapped1[0], ids, name="grad_pair_add_1")
        pair0 = _add_pairs(g0, swapped0[0], ids, name="grad_pair_add_0")
        dq4, dk, dv, parts0, parts1 = _flash_bwd(
            q, k, v, do_b, lse, delta, plan=_join_plans([_plan_chip_scatter(pair0), _plan_chip_scatter(pair1)]))
        half0 = _sum_chips(pair0, parts0, ids, name="grad_chip_sum_0")
        half1 = _sum_chips(pair1, parts1, ids, name="grad_chip_sum_1")
        dc, dkr, dwq, dwk, dwv, dgq, dgkv, g0, g1 = _mla_bwd(
            z0, dq4, dk, dv, gq, gkv, wq, wk, wv, rc, rs1, rs2,
            plan=_join_plans([_plan_pair_gather(half0), _plan_pair_gather(half1)]))
        g0, g1 = g0.reshape(ROWS_L0, D), g1.reshape(ROWS_L1, D)
    else:
        dq4, dk, dv = _flash_bwd(q, k, v, do_b, lse, delta)
        dc, dkr, dwq, dwk, dwv, dgq, dgkv = _mla_bwd(z0, dq4, dk, dv, gq, gkv, wq, wk, wv, rc, rs1, rs2)
    godd["w_qb"] = dwq.reshape(256, HEADS, 128)[:, :, :NOPE + ROPE].reshape(256, HEADS * (NOPE + ROPE))
    godd["w_kvb"] = jnp.concatenate([dwk.reshape(256, HEADS, 128)[:, :, :NOPE], dwv.reshape(256, HEADS, VDIM)],
                                    axis=2).reshape(256, HEADS * (NOPE + VDIM))
    swap_b = None
    if exchange:
        by_chip = [_odd_rows({"w_out_e": jnp.split(godd["w_out_e"], 4, axis=0)[j],
                              **{n: jnp.split(godd[n], 4, axis=1)[j] for n in ("w_qb", "w_kvb")}}, BF16,
                             ODD_W_PARTS, ROWS_ODD_W)
                   for j in range(4)]
        odd_b = jnp.stack(by_chip)
        swap_b = _plan_pair_swap(odd_b)
    dz0, dsw, dsb, dslg, dslb, *theirs_b = _sgu_bwd(z0, dmix0, dc, dkr, P["sgu_ln_g"], P["sgu_ln_b"], sgu_w, sgu_bt,
                                                    plan=swap_b)
    small_vec = _small_pack(dgq, dgkv, dslg, dslb, dsw, dsb, dlb, dgn, [ln1_g, ln1_b, ln2_g, ln2_b], loss_parts)
    plan_in = None
    if exchange:
        pair_b = _add_pairs(odd_b, theirs_b[0], ids, name="odd_pair_add_1")
        plan_in = _join_plans([_plan_exchange_all(small_vec), _plan_chip_scatter(pair_b)])
    dw_in, *carried = _matmul(x, dz0, ta=True, name="dw_in_e", M=D, N=1664, K=T, tm=1024, tn=1664, tk=DW_TOKENS // 4,
                              plan=plan_in)
    odd_a = godd["w_in_e"] = _in_e_from_layout(dw_in)
    plan_x = None
    if exchange:
        small_others, parts_b = carried
        theirs_a = _run_plan(_plan_pair_swap(odd_a), name="odd_pair_swap")[0]
        pair_a = _add_pairs(odd_a, theirs_a, ids, name="odd_pair_add_0")
        plan_x = _plan_chip_scatter(pair_a)
    grad_x, *parts_a = _matmul(dz0, w_in, tb=True, add=dr1, add_scale=ALPHA, name="dx", M=T, N=D, K=1664, tk=1664,
                               plan=plan_x)
    if exchange:
        godd = ([pair_a, pair_b], [parts_a[0], parts_b])
        return grad_x, g0, g1, godd, small_vec, small_others
    return grad_x, g0, g1, godd, small_vec, None


WEIGHTS = ['w_in_e', 'mla_gq', 'mla_gkv', 'w_qb', 'w_kvb', 'sgu_ln_g', 'sgu_ln_b', 'sgu_w', 'sgu_b', 'w_out_e',
           'w_in_o', 'hg_lb', 'hg_gnorm', 'w_out_o', 'ln1_g', 'ln1_b', 'w_ff1', 'w_ff2', 'ln2_g', 'ln2_b']


def kernel(x, positions, w_in_e, mla_gq, mla_gkv, w_qb, w_kvb, sgu_ln_g, sgu_ln_b, sgu_w, sgu_b, w_out_e, w_in_o, hg_lb, hg_gnorm, w_out_o, ln1_g, ln1_b, w_ff1, w_ff2, ln2_g, ln2_b, loss_target, m_w_in_e, m_mla_gq, m_mla_gkv, m_w_qb, m_w_kvb, m_sgu_ln_g, m_sgu_ln_b, m_sgu_w, m_sgu_b, m_w_out_e, m_w_in_o, m_hg_lb, m_hg_gnorm, m_w_out_o, m_ln1_g, m_ln1_b, m_w_ff1, m_w_ff2, m_ln2_g, m_ln2_b, v_w_in_e, v_mla_gq, v_mla_gkv, v_w_qb, v_w_kvb, v_sgu_ln_g, v_sgu_ln_b, v_sgu_w, v_sgu_b, v_w_out_e, v_w_in_o, v_hg_lb, v_hg_gnorm, v_w_out_o, v_ln1_g, v_ln1_b, v_w_ff1, v_w_ff2, v_ln2_g, v_ln2_b):
    args = dict(locals())
    w = {n: args[n] for n in WEIGHTS}
    m = {n: args["m_" + n] for n in WEIGHTS}
    v = {n: args["v_" + n] for n in WEIGHTS}
    cx, cy, cc = _mesh_pos()
    chip = 2 * cx + cy

    odd_shard = _odd_rows({"w_out_e": w_out_e[0], "w_qb": w_qb[0], "w_kvb": w_kvb[0]}, BF16, ODD_W_PARTS, ROWS_ODD_W,
                          gnorm=hg_gnorm)
    ids = _mesh_ids()
    placed = [_place_shard(w_in_e[0], ids, name="place_shard_in_e"), _place_shard(odd_shard, ids, name="place_shard_odd")]
    pieces = [(w_ff1, 0, 0, 0), (w_ff2, 0, 0, 1024), (w_in_o, 0, 1, 0), (w_out_o, 0, 1, 1024),
              (w_ff1, 1, 2, 0), (w_ff2, 1, 2, 1024)]
    *big_bufs, odd_a, odd_b = _place_weights(pieces, (2048, 1280, 2048), ids, plan=_plan_gather_ici(placed))
    gathered = _run_plan(_plan_gather_forward([odd_a, odd_b]), name="odd_gather_forward")
    per_chip = [_odd_unrows(gathered[1][j], ODD_W_PARTS, with_gnorm=True) for j in range(4)]
    odd = {"w_out_e": jnp.concatenate([p["w_out_e"] for p in per_chip], axis=0),
           "w_in_e": gathered[0]}
    for n in ("w_qb", "w_kvb"):
        odd[n] = jnp.concatenate([p[n] for p in per_chip], axis=1)
    small = {n: w[n] for n in SMALL_LAYOUT if n != "hg_gnorm"}
    small["hg_gnorm"] = jnp.concatenate([p["hg_gnorm"] for p in per_chip], axis=1)
    grad_x, g_l0, g_l1, godd, small_vec, small_others = _local_step(
        x[0], positions[0], loss_target[0], odd, big_bufs, small, True)

    sums = [_sum_chips(pair, parts, ids, name=f"odd_chip_sum_{k}") for k, (pair, parts) in enumerate(zip(*godd))]
    g_in_e, g_rest = _run_plan(_join_plans([_plan_pair_gather(s) for s in sums]), name="odd_pair_gather")
    g_odd = _odd_unrows(g_rest.reshape(ROWS_ODD_W, 1024), ODD_W_PARTS)
    g_odd["w_in_e"] = g_in_e.reshape(D, 392)

    to_kernel = lambda d: {n: d[n].reshape(SMALL_LAYOUT[n][3]) for n in SMALL_LAYOUT}
    first_row, small_out = _small_update(small_vec, small_others, ids, to_kernel(w), to_kernel(m), to_kernel(v))
    loss = first_row[0, 1023]
    grads, delta, new_m, new_v = {}, {}, {}, {}
    for n, res in small_out.items():
        grads[n], delta[n], new_m[n], new_v[n] = (r.reshape(w[n].shape) for r in res)

    for n, bufs_, row0 in (("w_ff1", [g_l0, g_l1], 0), ("w_ff2", [g_l0, g_l1], 1024), ("w_in_o", [g_l1], 2048),
                           ("w_out_o", [g_l1], 3072)):
        grads[n], delta[n], new_m[n], new_v[n] = _adamw_rows(w[n], m[n], v[n], bufs_, row0, name=f"adamw_{n}")
    for n, _ in ODD_PARTS:
        if n == "w_in_e":
            res = _adamw(w[n][0].T, g_odd[n].T, m[n][0].T, v[n][0].T, name=f"adamw_{n}", with_g=True)
            grads[n], delta[n], new_m[n], new_v[n] = (r.T[None] for r in res)
            continue
        grads[n] = g_odd[n][None]
        d_, m_, v_ = _adamw(w[n][0], g_odd[n], m[n][0], v[n][0], name=f"adamw_{n}")
        delta[n], new_m[n], new_v[n] = d_[None], m_[None], v_[None]

    return (loss, grad_x[None], *[grads[n] for n in WEIGHTS], *[delta[n] for n in WEIGHTS],
            *[new_m[n] for n in WEIGHTS], *[new_v[n] for n in WEIGHTS])
```

```python
import math

import jax
import jax.numpy as jnp
from jax import lax
from jax.experimental import pallas as pl
from jax.experimental.pallas import tpu as pltpu

F32 = jnp.float32
BF16 = jnp.bfloat16
MESH_IDS = pl.DeviceIdType.MESH

D = 1024
DEPTH = 2
HEADS = 8
NOPE, ROPE, VDIM = 64, 32, 64
QK_SCALE = (NOPE + ROPE) ** -0.5
ROPE_BASE = 10000.0
SGU_G, SGU_C = 4, 128
HG_CHUNK = 64
HG_HEADS_PER_STEP = 8
ALPHA = (2 * DEPTH) ** 0.25
EPS = 1e-5
LR, B1, B2, ADAM_EPS, WD, STEP = 0.001, 0.9, 0.999, 1e-08, 0.01, 10
GELU_C = math.sqrt(2.0 / math.pi)
GELU_A = 0.044715
MB = 1024 * 1024
ROW_BLOCK = 512
SMALL_ROWS = 80

NT_DIMS = (((1,), (1,)), ((), ()))
TN_DIMS = (((0,), (0,)), ((), ()))


def _params(vmem_mb, n_axes=0):
    kw = dict(vmem_limit_bytes=vmem_mb * MB)
    if n_axes:
        kw["dimension_semantics"] = ("arbitrary",) * n_axes
    return pltpu.CompilerParams(**kw)


_ANY = pl.BlockSpec(memory_space=pltpu.HBM)


def _mesh_pos():
    return lax.axis_index("x"), lax.axis_index("y"), lax.axis_index("c")


def _hbm(*arrays):
    return tuple(pltpu.with_memory_space_constraint(a, pltpu.HBM) if a.size >= 2 ** 18 else a for a in arrays)


class _Plan:
    def __init__(self, ins, outs, n_remote, n_local, start, wait, aliases=None):
        self.ins, self.outs, self.n_remote, self.n_local = list(ins), list(outs), n_remote, n_local
        self.start, self.wait, self.aliases = start, wait, dict(aliases or {})


def _join_plans(plans):
    ins, outs, aliases, parts = [], [], {}, []
    nr = nl = 0
    for p in plans:
        parts.append((p, len(ins), len(outs), nr, nl))
        aliases.update({len(ins) + i: len(outs) + o for i, o in p.aliases.items()})
        ins += p.ins
        outs += p.outs
        nr += p.n_remote
        nl += p.n_local

    def run(which):
        def go(in_refs, out_refs, send, recv, loc):
            for p, i0, o0, r0, l0 in parts:
                getattr(p, which)(in_refs[i0:i0 + len(p.ins)], out_refs[o0:o0 + len(p.outs)],
                                  lambda i, r0=r0: send(r0 + i), lambda i, r0=r0: recv(r0 + i),
                                  lambda i, l0=l0: loc(l0 + i))
        return go

    return _Plan(ins, outs, nr, nl, run("start"), run("wait"), aliases)


def _plan_io(plan, n_in, n_out):
    if plan is None:
        return [], [], [], [], {}
    sems = [pltpu.SemaphoreType.DMA((max(plan.n_remote, 1),)), pltpu.SemaphoreType.DMA((max(plan.n_remote, 1),)),
            pltpu.SemaphoreType.DMA((max(plan.n_local, 1),))]
    aliases = {n_in + i: n_out + o for i, o in plan.aliases.items()}
    return plan.ins, [_ANY] * len(plan.outs), plan.outs, sems, aliases


def _split_refs(refs, n_in, n_out, n_scr, plan):
    p_in, p_out = (len(plan.ins), len(plan.outs)) if plan is not None else (0, 0)
    refs = list(refs)
    ins, refs = refs[:n_in], refs[n_in:]
    pins, refs = refs[:p_in], refs[p_in:]
    outs, refs = refs[:n_out], refs[n_out:]
    pouts, refs = refs[:p_out], refs[p_out:]
    scr, psem = refs[:n_scr], refs[n_scr:]
    psem = tuple((lambda i, s=s: s.at[i]) for s in psem)
    return ins, outs, scr, (pins, pouts, psem)


def _grid_edge(grid, last):
    cond = None
    for ax, n in enumerate(grid):
        c = pl.program_id(ax) == (n - 1 if last else 0)
        cond = c if cond is None else cond & c
    return cond


def _plan_start(plan, pctx, grid):
    if plan is not None:
        pins, pouts, psem = pctx
        pl.when(_grid_edge(grid, False))(lambda: plan.start(pins, pouts, *psem))


def _plan_wait(plan, pctx, grid):
    if plan is not None:
        pins, pouts, psem = pctx
        pl.when(_grid_edge(grid, True))(lambda: plan.wait(pins, pouts, *psem))


def _run_plan(plan, *, name):
    def body(*refs):
        _, _, _, (pins, pouts, psem) = _split_refs(refs, 0, 0, 0, plan)
        plan.start(pins, pouts, *psem)
        plan.wait(pins, pouts, *psem)

    p_in, p_ospec, p_oshape, p_scr, p_alias = _plan_io(plan, 0, 0)
    return pl.pallas_call(body, name=name, in_specs=[_ANY] * len(p_in), out_specs=p_ospec, out_shape=p_oshape,
                          scratch_shapes=p_scr, input_output_aliases=p_alias)(*p_in)


def _fold8(x):
    return x.reshape(x.shape[0] // 8, 8, x.shape[1]).sum(axis=0)


def _ln_stats(r):
    mu = jnp.mean(r, -1, keepdims=True)
    xc = r - mu
    rstd = lax.rsqrt(jnp.mean(xc * xc, -1, keepdims=True) + EPS)
    return xc * rstd, rstd


def _sigmoid(x):
    return jax.nn.sigmoid(x)


def _gelu(x):
    return 0.5 * x * (1.0 + jnp.tanh(GELU_C * (x + GELU_A * x * x * x)))


def _gelu_grad(x):
    t = jnp.tanh(GELU_C * (x + GELU_A * x * x * x))
    return 0.5 * (1.0 + t) + 0.5 * x * (1.0 - t * t) * GELU_C * (1.0 + 3.0 * GELU_A * x * x)


MM_ROWS = 1024
DW_TOKENS = 4096


def _matmul(a, b, *, name, M, N, K, ta=False, tb=False, out_dtype=F32, tm=MM_ROWS, tn=1024, tk=1024,
            a_spec=None, b_spec=None, b_merge=None, out_shape=None, o_spec=None, into=None,
            a_sq=False, mul=None, add=None, add_scale=1.0, n_slots=False, plan=None):
    assert not n_slots or (K // min(tk, K) == 1 and add is None)
    tm, tn, tk = min(tm, M), min(tn, N), min(tk, K)
    assert M % tm == 0 and N % tn == 0 and K % tk == 0
    grid = (M // tm, N // tn, K // tk)
    nk = grid[2]
    if a_spec is None:
        a_spec = pl.BlockSpec((tk, tm), lambda i, j, k: (k, i)) if ta else pl.BlockSpec((tm, tk), lambda i, j, k: (i, k))
    if b_spec is None:
        b_spec = pl.BlockSpec((tn, tk), lambda i, j, k: (j, k)) if tb else pl.BlockSpec((tk, tn), lambda i, j, k: (k, j))
    if o_spec is None:
        o_spec = pl.BlockSpec((tm, tn), lambda i, j, k: (i, j))
        out_shape = jax.ShapeDtypeStruct((M, N), out_dtype)
    e_spec = pl.BlockSpec((tm, tn), lambda i, j, k: (i, j))
    dims = (((0 if ta else 1,), (1 if tb else 0,)), ((), ()))
    extra = [e for e in (mul, add, into) if e is not None]
    n_in = 2 + len(extra)

    def body(*refs):
        ins, outs, scr, pctx = _split_refs(refs, n_in, 1, 1 if nk > 1 else 0, plan)
        a_ref, b_ref = ins[0], ins[1]
        rest = list(ins[2:])
        mul_ref = rest.pop(0) if mul is not None else None
        add_ref = rest.pop(0) if add is not None else None
        o_ref = outs[0]
        _plan_start(plan, pctx, grid)
        av = a_ref[...].astype(BF16)
        if a_sq:
            av = av * av
        bv = b_ref[...]
        if b_merge is not None:
            bv = bv.reshape(b_merge)
        if n_slots:
            for s in range(bv.shape[0]):
                r = lax.dot_general(av, bv[s], dims, preferred_element_type=F32)
                w = r.shape[1]
                if mul_ref is not None:
                    r = r * (2.0 * mul_ref[:, s * w:(s + 1) * w].astype(F32))
                if o_ref.ndim == 3:
                    o_ref[s] = r.astype(o_ref.dtype)
                else:
                    o_ref[:, s * w:(s + 1) * w] = r.astype(o_ref.dtype)
            _plan_wait(plan, pctx, grid)
            return
        if bv.ndim == 3:
            w = av.shape[-1] // (1 if av.ndim == 3 else bv.shape[0])
            a_parts = [av[s] if av.ndim == 3 else av[:, s * w:(s + 1) * w] for s in range(bv.shape[0])]
            p = sum(lax.dot_general(a_parts[s], bv[s], dims, preferred_element_type=F32) for s in range(bv.shape[0]))
        else:
            p = lax.dot_general(av, bv, dims, preferred_element_type=F32)

        def finish(r):
            if mul_ref is not None:
                r = r * (2.0 * mul_ref[...].astype(F32))
            if add_ref is not None:
                r = r + add_scale * add_ref[...]
            o_ref[...] = r.astype(o_ref.dtype)

        if nk == 1:
            finish(p)
        else:
            acc_ref = scr[0]
            k = pl.program_id(2)

            @pl.when(k == 0)
            def _():
                acc_ref[...] = p

            @pl.when(k > 0)
            def _():
                acc_ref[...] += p

            @pl.when(k == nk - 1)
            def _():
                finish(acc_ref[...])

        _plan_wait(plan, pctx, grid)

    p_in, p_ospec, p_oshape, p_scr, p_alias = _plan_io(plan, n_in, 1)
    aliases = dict(p_alias)
    if into is not None:
        aliases[n_in - 1] = 0
    return pl.pallas_call(
        body, name=name, grid=grid,
        in_specs=[a_spec, b_spec] + [e_spec] * (len(extra) - (into is not None)) + [_ANY] * (into is not None)
        + [_ANY] * len(p_in),
        out_specs=[o_spec] + p_ospec, out_shape=[out_shape] + p_oshape,
        scratch_shapes=([pltpu.VMEM((tm, tn), F32)] if nk > 1 else []) + p_scr,
        input_output_aliases=aliases, compiler_params=_params(48, 3),
    )(*_hbm(a, b, *extra), *p_in)


def _rows4_spec(rowblk, n_axes):
    return pl.BlockSpec((4, 256, D), lambda *_: (0, rowblk, 0))


def _residual(h_ref, prev_refs):
    if not prev_refs:
        return h_ref[...]
    xhat, _ = _ln_stats(h_ref[...])
    return xhat * prev_refs[0][...] + prev_refs[1][...]


def _proj_ln(a_b, w, h_prev, g, b, *, name, prev_ln=(), w_rowblk=None, plan=None):
    T = a_b.shape[0]
    tm = min(MM_ROWS, T)
    grid = (T // tm,)
    row = pl.BlockSpec((tm, D), lambda i: (i, 0))
    vec = pl.BlockSpec((1, D), lambda i: (0, 0))
    w_spec = pl.BlockSpec((D, D), lambda i: (0, 0)) if w_rowblk is None else _rows4_spec(w_rowblk, 1)
    n_in = 5 + len(prev_ln)

    def body(*refs):
        ins, (r_ref, hb_ref), _, pctx = _split_refs(refs, n_in, 2, 0, plan)
        a_ref, w_ref, h_ref, g_ref, b_ref = ins[:5]
        _plan_start(plan, pctx, grid)
        mix = jnp.dot(a_ref[...], w_ref[...].reshape(D, D), preferred_element_type=F32)
        r = ALPHA * _residual(h_ref, ins[5:]) + mix
        xhat, _ = _ln_stats(r)
        r_ref[...] = r
        hb_ref[...] = (xhat * g_ref[...] + b_ref[...]).astype(BF16)
        _plan_wait(plan, pctx, grid)

    p_in, p_ospec, p_oshape, p_scr, p_alias = _plan_io(plan, n_in, 2)
    return pl.pallas_call(
        body, name=name, grid=grid,
        in_specs=[row, w_spec, row, vec, vec] + [vec] * len(prev_ln) + [_ANY] * len(p_in),
        out_specs=[row, row] + p_ospec,
        out_shape=[jax.ShapeDtypeStruct((T, D), F32), jax.ShapeDtypeStruct((T, D), BF16)] + p_oshape,
        scratch_shapes=p_scr, input_output_aliases=p_alias, compiler_params=_params(40, 1),
    )(*_hbm(a_b, w, h_prev, g, b, *prev_ln), *p_in)


def _ffn_ln(h_b, wbuf, h, g, b, *, name, prev_ln=(), plan=None):
    T = h_b.shape[0]
    slots = 2
    tm, tf = min(ROW_BLOCK, T), slots * 1024
    nf = 4 // slots
    F = nf * tf
    grid = (T // tm, nf)
    row = pl.BlockSpec((tm, D), lambda i, j: (i, 0))
    vec = pl.BlockSpec((1, D), lambda i, j: (0, 0))
    n_in = 6 + len(prev_ln)

    def body(*refs):
        ins, (ra_ref, r_ref, hbo_ref), (acc_ref,), pctx = _split_refs(refs, n_in, 3, 1, plan)
        hb_ref, w1_ref, w2_ref, h_ref, g_ref, b_ref = ins[:6]
        _plan_start(plan, pctx, grid)
        j = pl.program_id(1)
        hb = hb_ref[...]
        p = None
        for s in range(slots):
            ra = jnp.maximum(jnp.dot(hb, w1_ref[s], preferred_element_type=F32), 0.0)
            ra_ref[:, s * 1024:(s + 1) * 1024] = ra.astype(BF16)
            ps = jnp.dot((ra * ra).astype(BF16), w2_ref[s], preferred_element_type=F32)
            p = ps if p is None else p + ps

        @pl.when(j == 0)
        def _():
            acc_ref[...] = p

        @pl.when(j > 0)
        def _():
            acc_ref[...] += p

        @pl.when(j == nf - 1)
        def _():
            r = ALPHA * _residual(h_ref, ins[6:]) + acc_ref[...]
            xhat, _ = _ln_stats(r)
            r_ref[...] = r
            hbo_ref[...] = (xhat * g_ref[...] + b_ref[...]).astype(BF16)

        _plan_wait(plan, pctx, grid)

    p_in, p_ospec, p_oshape, p_scr, p_alias = _plan_io(plan, n_in, 3)
    return pl.pallas_call(
        body, name=name, grid=grid,
        in_specs=[row, pl.BlockSpec((slots, D, D), lambda i, j: (j, 0, 0)),
                  pl.BlockSpec((slots, D, D), lambda i, j: (j, 1, 0)), row, vec, vec] + [vec] * len(prev_ln)
        + [_ANY] * len(p_in),
        out_specs=[pl.BlockSpec((tm, tf), lambda i, j: (i, j)), row, row] + p_ospec,
        out_shape=[jax.ShapeDtypeStruct((T, F), BF16), jax.ShapeDtypeStruct((T, D), F32),
                   jax.ShapeDtypeStruct((T, D), BF16)] + p_oshape,
        scratch_shapes=[pltpu.VMEM((tm, D), F32)] + p_scr,
        input_output_aliases=p_alias, compiler_params=_params(56, 2),
    )(*_hbm(h_b, wbuf, wbuf, h, g, b, *prev_ln), *p_in)


def _ln_bwd(dy, r, g, *, name, loss_head=()):
    T = r.shape[0]
    tm = min(ROW_BLOCK, T)
    row = pl.BlockSpec((tm, D), lambda i: (i, 0))
    vec = pl.BlockSpec((1, D), lambda i: (0, 0))
    acc = pl.BlockSpec((8, D), lambda i: (0, 0))
    operands, in_specs = ([r, g, *loss_head], [row, vec, vec, row]) if loss_head else ([r, g, dy], [row, vec, row])
    n_in = len(operands)

    def body(*refs):
        r_ref, g_ref = refs[:2]
        dr_ref, drb_ref, dg_ref, db_ref = refs[n_in:n_in + 4]

        @pl.when(pl.program_id(0) == 0)
        def _():
            for ref in refs[n_in + 2:]:
                ref[...] = jnp.zeros_like(ref)

        xhat, rstd = _ln_stats(r_ref[...])
        if loss_head:
            err = xhat * g_ref[...] + refs[2][...] - refs[3][...]
            refs[n_in + 4][...] += _fold8(err * err)
            dy_ = err * (1.0 / D)
        else:
            dy_ = refs[2][...]
        dxh = dy_ * g_ref[...]
        m1 = jnp.mean(dxh, -1, keepdims=True)
        m2 = jnp.mean(dxh * xhat, -1, keepdims=True)
        dr = rstd * (dxh - m1 - xhat * m2)
        dr_ref[...] = dr
        drb_ref[...] = dr.astype(BF16)
        dg_ref[...] += _fold8(dy_ * xhat)
        db_ref[...] += _fold8(dy_)

    n_acc = 3 if loss_head else 2
    return pl.pallas_call(
        body, name=name, grid=(T // tm,), in_specs=in_specs, out_specs=[row, row] + [acc] * n_acc,
        out_shape=[jax.ShapeDtypeStruct((T, D), F32), jax.ShapeDtypeStruct((T, D), BF16)]
        + [jax.ShapeDtypeStruct((8, D), F32)] * n_acc,
        compiler_params=_params(40, 1),
    )(*_hbm(*operands))


def _rope(x, c, s1, s2):
    return x * c + pltpu.roll(x, 112, 1) * s1 + pltpu.roll(x, 16, 1) * s2


def _rope_t(dy, c, s1, s2):
    return dy * c + pltpu.roll(dy * s1, 16, 1) + pltpu.roll(dy * s2, 112, 1)


def _rms(x, g):
    rstd = lax.rsqrt(jnp.mean(x * x, -1, keepdims=True) + EPS)
    xhat = x * rstd
    return xhat * g, xhat, rstd


def _mla_prep(z0, gq, gkv, wq, wk, wv, rc, rs1, rs2):
    T = z0.shape[0]
    tm = min(ROW_BLOCK, T)
    HW = HEADS * 128

    def body(cq_ref, ckv_ref, kr_ref, gq_ref, gkv_ref, wq_ref, wk_ref, wv_ref, c_ref, s1_ref, s2_ref,
             q_ref, k_ref, v_ref):
        nq = _rms(cq_ref[...], gq_ref[...])[0].astype(BF16)
        nkv = _rms(ckv_ref[...], gkv_ref[...])[0].astype(BF16)
        q = jnp.dot(nq, wq_ref[...], preferred_element_type=F32)
        k = jnp.dot(nkv, wk_ref[...], preferred_element_type=F32)
        v = jnp.dot(nkv, wv_ref[...], preferred_element_type=F32)
        c, s1, s2 = c_ref[...], s1_ref[...], s2_ref[...]
        kr = _rope(pltpu.roll(kr_ref[...], 64, 1), c, s1, s2)
        for h in range(HEADS):
            sl = slice(h * 128, (h + 1) * 128)
            q_ref[:, sl] = (_rope(q[:, sl], c, s1, s2) * QK_SCALE).astype(BF16)
            k_ref[:, sl] = (k[:, sl] + kr).astype(BF16)
        v_ref[...] = v.astype(BF16)

    full = lambda shape: pl.BlockSpec(shape, lambda i: (0, 0))
    tab = pl.BlockSpec((tm, 128), lambda i: (i, 0))
    return pl.pallas_call(
        body, name="mla_prep", grid=(T // tm,),
        in_specs=[pl.BlockSpec((tm, 256), lambda i: (i, 0)), pl.BlockSpec((tm, 256), lambda i: (i, 1)),
                  pl.BlockSpec((tm, 128), lambda i: (i, 12)), full((1, 256)), full((1, 256)),
                  full((256, HW)), full((256, HW)), full((256, 512)), tab, tab, tab],
        out_specs=[pl.BlockSpec((tm, HW), lambda i: (i, 0)), pl.BlockSpec((tm, HW), lambda i: (i, 0)),
                   pl.BlockSpec((tm, 512), lambda i: (i, 0))],
        out_shape=[jax.ShapeDtypeStruct((T, HW), BF16), jax.ShapeDtypeStruct((T, HW), BF16),
                   jax.ShapeDtypeStruct((T, 512), BF16)],
        compiler_params=_params(40, 1),
    )(*_hbm(z0, z0, z0, gq, gkv, wq, wk, wv, rc, rs1, rs2))


def _flash_fwd(q, k, v, plan=None):
    T = q.shape[0]
    bq = min(2 * ROW_BLOCK, T)
    nq = T // bq
    pairs = [(i, j) for i in range(nq) for j in range(i + 1)]
    imap, jmap = (jnp.array(m, jnp.int32) for m in zip(*pairs))
    grid = (4, len(pairs))

    def body(imap_ref, jmap_ref, *refs):
        (q_ref, k_ref, v_ref), (o_ref, lse_ref), (m_sc, acc_sc), pctx = _split_refs(refs, 3, 2, 2, plan)
        _plan_start(plan, pctx, grid)
        i, j = imap_ref[pl.program_id(1)], jmap_ref[pl.program_id(1)]
        first = lax.broadcasted_iota(jnp.int32, (bq, 128), 1) < 64

        @pl.when(j == 0)
        def _():
            m_sc[...] = jnp.full_like(m_sc, -jnp.inf)
            acc_sc[...] = jnp.zeros_like(acc_sc)

        def step(masked):
            vp = v_ref[...]
            for h in range(2):
                sl = slice(h * 128, (h + 1) * 128)
                s = lax.dot_general(q_ref[:, sl], k_ref[:, sl], NT_DIMS, preferred_element_type=F32)
                if masked:
                    rows = lax.broadcasted_iota(jnp.int32, (bq, bq), 0)
                    cols = lax.broadcasted_iota(jnp.int32, (bq, bq), 1)
                    s = jnp.where(cols <= rows, s, -jnp.inf)
                m_prev = m_sc[h, :, 0:1]
                m_new = jnp.maximum(m_prev, jnp.max(s, axis=1, keepdims=True))
                alpha = jnp.exp(m_prev - m_new)
                p = jnp.exp(s - m_new).astype(BF16)
                vh = jnp.where(first if h == 0 else jnp.logical_not(first), vp, jnp.ones_like(vp))
                acc_sc[h] = acc_sc[h] * alpha + jnp.dot(p, vh, preferred_element_type=F32)
                m_sc[h] = jnp.broadcast_to(m_new, (bq, 128))

        @pl.when(j < i)
        def _():
            step(False)

        @pl.when(j == i)
        def _():
            step(True)
            a0, a1 = acc_sc[0], acc_sc[1]
            l0, l1 = pltpu.roll(a0, 64, 1), pltpu.roll(a1, 64, 1)
            o_ref[...] = jnp.where(first, a0 / l0, a1 / l1).astype(BF16)
            lse_ref[...] = jnp.where(first, m_sc[0] + jnp.log(l0), m_sc[1] + jnp.log(l1))

        _plan_wait(plan, pctx, grid)

    qi = lambda hp, t, im, jm: (im[t], hp)
    kj = lambda hp, t, im, jm: (jm[t], hp)
    p_in, p_ospec, p_oshape, p_scr, p_alias = _plan_io(plan, 2 + 3, 2)
    return pl.pallas_call(
        body, name="flash_fwd",
        out_shape=[jax.ShapeDtypeStruct((T, 512), BF16), jax.ShapeDtypeStruct((T, 512), F32)] + p_oshape,
        grid_spec=pltpu.PrefetchScalarGridSpec(
            num_scalar_prefetch=2, grid=grid,
            in_specs=[pl.BlockSpec((bq, 256), qi), pl.BlockSpec((bq, 256), kj), pl.BlockSpec((bq, 128), kj)]
            + [_ANY] * len(p_in),
            out_specs=[pl.BlockSpec((bq, 128), qi), pl.BlockSpec((bq, 128), qi)] + p_ospec,
            scratch_shapes=[pltpu.VMEM((2, bq, 128), F32), pltpu.VMEM((2, bq, 128), F32)] + p_scr),
        input_output_aliases=p_alias, compiler_params=_params(56, 2),
    )(imap, jmap, *_hbm(q, k, v), *p_in)


def _attn_delta(dmix, o):
    T = o.shape[0]
    tm = min(ROW_BLOCK, T)
    blk = pl.BlockSpec((tm, 512), lambda i: (i, 0))

    def body(do_ref, o_ref, delta_ref, dob_ref):
        first = lax.broadcasted_iota(jnp.int32, (tm, 128), 1) < 64
        for hp in range(4):
            sl = slice(hp * 128, (hp + 1) * 128)
            prod = do_ref[:, sl] * o_ref[:, sl].astype(F32)
            d0 = jnp.sum(jnp.where(first, prod, 0.0), axis=1, keepdims=True)
            d1 = jnp.sum(jnp.where(first, 0.0, prod), axis=1, keepdims=True)
            delta_ref[:, sl] = jnp.where(first, d0, d1)
        dob_ref[...] = do_ref[...].astype(BF16)

    return pl.pallas_call(
        body, name="attn_delta", grid=(T // tm,), in_specs=[blk, blk], out_specs=[blk, blk],
        out_shape=[jax.ShapeDtypeStruct((T, 512), F32), jax.ShapeDtypeStruct((T, 512), BF16)],
        compiler_params=_params(32, 1),
    )(*_hbm(dmix, o))


def _flash_bwd(q, k, v, do_b, lse, delta, plan=None):
    T = q.shape[0]
    bq = min(2 * ROW_BLOCK, T)
    nq = T // bq
    pairs = [(i, j) for j in range(nq) for i in range(j, nq)]
    imap, jmap = (jnp.array(m, jnp.int32) for m in zip(*pairs))
    grid = (4, len(pairs))

    def body(imap_ref, jmap_ref, *refs):
        ((q_ref, k_ref, v_ref, do_ref, lse_ref, dl_ref), (dq_hbm, dk_ref, dv_ref), (dq_sc, dk_sc, dv_sc, sem),
         pctx) = _split_refs(refs, 6, 3, 4, plan)
        _plan_start(plan, pctx, grid)
        hp = pl.program_id(0)
        i, j = imap_ref[pl.program_id(1)], jmap_ref[pl.program_id(1)]
        first = lax.broadcasted_iota(jnp.int32, (bq, 128), 1) < 64

        @pl.when((j == 0) & (i == 0))
        def _():
            dq_sc[...] = jnp.zeros_like(dq_sc)

        @pl.when(i == j)
        def _():
            dk_sc[...] = jnp.zeros_like(dk_sc)
            dv_sc[...] = jnp.zeros_like(dv_sc)

        def tile(r0, nr, nc, masked):
            rs, cs = slice(r0, r0 + nr), slice(0, nc)
            vp = v_ref[cs, :]
            do = do_ref[rs, :]
            lanes = first[rs, :]
            for h in range(2):
                sl = slice(h * 128, (h + 1) * 128)
                qh, kh = q_ref[rs, sl], k_ref[cs, sl]
                s = lax.dot_general(qh, kh, NT_DIMS, preferred_element_type=F32)
                p = jnp.exp(s - lse_ref[rs, h * 64:h * 64 + 1])
                if masked:
                    rows = r0 + lax.broadcasted_iota(jnp.int32, (nr, nc), 0)
                    cols = lax.broadcasted_iota(jnp.int32, (nr, nc), 1)
                    p = jnp.where(cols <= rows, p, 0.0)
                do_h = jnp.where(lanes if h == 0 else jnp.logical_not(lanes), do, jnp.zeros_like(do))
                dv_sc[cs, :] += lax.dot_general(p.astype(BF16), do_h, TN_DIMS, preferred_element_type=F32)
                dp = lax.dot_general(do_h, vp, NT_DIMS, preferred_element_type=F32)
                ds = (p * (dp - dl_ref[rs, h * 64:h * 64 + 1])).astype(BF16)
                dq_sc[i, rs, sl] += jnp.dot(ds, kh, preferred_element_type=F32)
                dk_sc[cs, sl] += lax.dot_general(ds, qh, TN_DIMS, preferred_element_type=F32)

        @pl.when(i > j)
        def _():
            tile(0, bq, bq, False)

        @pl.when(i == j)
        def _():
            tile(0, bq // 2, bq // 2, True)
            tile(bq // 2, bq // 2, bq, True)

        @pl.when(i == nq - 1)
        def _():
            dk_ref[...] = dk_sc[...]
            dv_ref[...] = dv_sc[...]

        @pl.when((j == nq - 1) & (i == nq - 1))
        def _():
            cp = pltpu.make_async_copy(dq_sc, dq_hbm.at[hp], sem)
            cp.start()
            cp.wait()

        _plan_wait(plan, pctx, grid)

    qi = lambda hp, t, im, jm: (im[t], hp)
    kj = lambda hp, t, im, jm: (jm[t], hp)
    p_in, p_ospec, p_oshape, p_scr, p_alias = _plan_io(plan, 2 + 6, 3)
    return pl.pallas_call(
        body, name="flash_bwd",
        out_shape=[jax.ShapeDtypeStruct((4, nq, bq, 256), F32), jax.ShapeDtypeStruct((T, 1024), F32),
                   jax.ShapeDtypeStruct((T, 512), F32)] + p_oshape,
        grid_spec=pltpu.PrefetchScalarGridSpec(
            num_scalar_prefetch=2, grid=grid,
            in_specs=[pl.BlockSpec((bq, 256), qi), pl.BlockSpec((bq, 256), kj), pl.BlockSpec((bq, 128), kj),
                      pl.BlockSpec((bq, 128), qi), pl.BlockSpec((bq, 128), qi), pl.BlockSpec((bq, 128), qi)]
            + [_ANY] * len(p_in),
            out_specs=[_ANY, pl.BlockSpec((bq, 256), kj), pl.BlockSpec((bq, 128), kj)] + p_ospec,
            scratch_shapes=[pltpu.VMEM((nq, bq, 256), F32), pltpu.VMEM((bq, 256), F32), pltpu.VMEM((bq, 128), F32),
                            pltpu.SemaphoreType.DMA] + p_scr),
        input_output_aliases=p_alias, compiler_params=_params(56, 2),
    )(imap, jmap, *_hbm(q, k, v, do_b, lse, delta), *p_in)


def _mla_bwd(z0, dq4, dk, dv, gq, gkv, wq, wk, wv, rc, rs1, rs2, plan=None):
    T = z0.shape[0]
    tm = min(ROW_BLOCK, T)
    HW = HEADS * 128
    grid = (T // tm,)
    dq4 = dq4.reshape(4, T, 256)

    def body(*refs):
        ((cq_ref, ckv_ref, dq_ref, dk_ref, dv_ref, gq_ref, gkv_ref, wq_ref, wk_ref, wv_ref, c_ref, s1_ref, s2_ref),
         (dc_ref, dkr_ref, dwq_ref, dwk_ref, dwv_ref, dgq_ref, dgkv_ref), _, pctx) = _split_refs(refs, 13, 7, 0, plan)
        _plan_start(plan, pctx, grid)

        @pl.when(pl.program_id(0) == 0)
        def _():
            for ref in (dwq_ref, dwk_ref, dwv_ref, dgq_ref, dgkv_ref):
                ref[...] = jnp.zeros_like(ref)

        c, s1, s2 = c_ref[...], s1_ref[...], s2_ref[...]
        lane = lax.broadcasted_iota(jnp.int32, (tm, 128), 1)
        nq, xq, rq = _rms(cq_ref[...], gq_ref[...])
        nkv, xkv, rkv = _rms(ckv_ref[...], gkv_ref[...])
        nq_b, nkv_b = nq.astype(BF16), nkv.astype(BF16)

        dq_parts, dk_parts = [], []
        dkr = jnp.zeros((tm, 128), F32)
        for h in range(HEADS):
            blk = dq_ref[h // 2, :, (h % 2) * 128:(h % 2 + 1) * 128] * QK_SCALE
            dq_parts.append(_rope_t(blk, c, s1, s2).astype(BF16))
            kb = dk_ref[:, h * 128:(h + 1) * 128]
            dk_parts.append(jnp.where(lane < NOPE, kb, 0.0).astype(BF16))
            dkr = dkr + kb
        dq_b = jnp.concatenate(dq_parts, axis=1)
        dk_b = jnp.concatenate(dk_parts, axis=1)
        dv_b = dv_ref[...].astype(BF16)

        dwq_ref[...] += lax.dot_general(nq_b, dq_b, TN_DIMS, preferred_element_type=F32)
        dwk_ref[...] += lax.dot_general(nkv_b, dk_b, TN_DIMS, preferred_element_type=F32)
        dwv_ref[...] += lax.dot_general(nkv_b, dv_b, TN_DIMS, preferred_element_type=F32)
        dnq = lax.dot_general(dq_b, wq_ref[...], NT_DIMS, preferred_element_type=F32)
        dnkv = (lax.dot_general(dk_b, wk_ref[...], NT_DIMS, preferred_element_type=F32)
                + lax.dot_general(dv_b, wv_ref[...], NT_DIMS, preferred_element_type=F32))

        def rms_bwd(dn, xhat, rstd, g):
            dxh = dn * g
            return rstd * (dxh - xhat * jnp.mean(dxh * xhat, -1, keepdims=True))

        dc_ref[:, :256] = rms_bwd(dnq, xq, rq, gq_ref[...]).astype(BF16)
        dc_ref[:, 256:] = rms_bwd(dnkv, xkv, rkv, gkv_ref[...]).astype(BF16)
        dgq_ref[...] += _fold8(dnq * xq)
        dgkv_ref[...] += _fold8(dnkv * xkv)
        dkr = pltpu.roll(_rope_t(dkr, c, s1, s2), 64, 1)
        dkr_ref[...] = jnp.where(lane < ROPE, dkr, 0.0).astype(BF16)
        _plan_wait(plan, pctx, grid)

    full = lambda shape: pl.BlockSpec(shape, lambda i: (0,) * len(shape))
    tab = pl.BlockSpec((tm, 128), lambda i: (i, 0))
    p_in, p_ospec, p_oshape, p_scr, p_alias = _plan_io(plan, 13, 7)
    return pl.pallas_call(
        body, name="mla_bwd", grid=grid,
        in_specs=[pl.BlockSpec((tm, 256), lambda i: (i, 0)), pl.BlockSpec((tm, 256), lambda i: (i, 1)),
                  pl.BlockSpec((4, tm, 256), lambda i: (0, i, 0)),
                  pl.BlockSpec((tm, HW), lambda i: (i, 0)), pl.BlockSpec((tm, 512), lambda i: (i, 0)),
                  full((1, 256)), full((1, 256)), full((256, HW)), full((256, HW)), full((256, 512)), tab, tab, tab]
        + [_ANY] * len(p_in),
        out_specs=[pl.BlockSpec((tm, 512), lambda i: (i, 0)), tab, full((256, HW)), full((256, HW)),
                   full((256, 512)), full((8, 256)), full((8, 256))] + p_ospec,
        out_shape=[jax.ShapeDtypeStruct((T, 512), BF16), jax.ShapeDtypeStruct((T, 128), BF16),
                   jax.ShapeDtypeStruct((256, HW), F32), jax.ShapeDtypeStruct((256, HW), F32),
                   jax.ShapeDtypeStruct((256, 512), F32), jax.ShapeDtypeStruct((8, 256), F32),
                   jax.ShapeDtypeStruct((8, 256), F32)] + p_oshape,
        scratch_shapes=p_scr, input_output_aliases=p_alias, compiler_params=_params(48, 1),
    )(*_hbm(z0, z0, dq4, dk, dv, gq, gkv, wq, wk, wv, rc, rs1, rs2), *p_in)


def _sgu_fwd(z0, a_out, ln_g, ln_b, w, b_t):
    T = z0.shape[0]
    tm = min(ROW_BLOCK, T)
    W = SGU_G * SGU_C

    def body(u_ref, v_ref, a_ref, g_ref, b_ref, w_ref, bt_ref, o_ref):
        o_ref[:, :W] = a_ref[...]
        ug = _gelu(u_ref[...])
        xhat, _ = _ln_stats(_gelu(v_ref[...]))
        vn = (xhat * g_ref[...] + b_ref[...]).astype(BF16)
        tril = lax.broadcasted_iota(jnp.int32, (SGU_C, SGU_C), 0) >= lax.broadcasted_iota(jnp.int32, (SGU_C, SGU_C), 1)
        for g in range(SGU_G):
            cs = slice(g * SGU_C, (g + 1) * SGU_C)
            wg = jnp.where(tril, w_ref[g], 0.0).astype(BF16)
            bcol = bt_ref[:, g:g + 1]
            for c in range(tm // SGU_C):
                rs = slice(c * SGU_C, (c + 1) * SGU_C)
                mixed = jnp.dot(wg, vn[rs, cs], preferred_element_type=F32) + bcol
                o_ref[rs, W + g * SGU_C:W + (g + 1) * SGU_C] = (ug[rs, cs] * mixed).astype(BF16)

    full = lambda shape: pl.BlockSpec(shape, lambda i: (0,) * len(shape))
    return pl.pallas_call(
        body, name="sgu_fwd", grid=(T // tm,),
        in_specs=[pl.BlockSpec((tm, W), lambda i: (i, 1)), pl.BlockSpec((tm, W), lambda i: (i, 2)),
                  pl.BlockSpec((tm, W), lambda i: (i, 0)),
                  full((1, W)), full((1, W)), full((SGU_G, SGU_C, SGU_C)), full((SGU_C, SGU_G))],
        out_specs=pl.BlockSpec((tm, 2 * W), lambda i: (i, 0)),
        out_shape=jax.ShapeDtypeStruct((T, 2 * W), BF16),
        compiler_params=_params(32, 1),
    )(*_hbm(z0, z0, a_out, ln_g, ln_b, w, b_t))


def _sgu_bwd(z0, dmix, dc, dkr, ln_g, ln_b, w, b_t, plan=None):
    T = z0.shape[0]
    tm = min(ROW_BLOCK, T)
    W = SGU_G * SGU_C
    grid = (T // tm,)

    def body(*refs):
        ((u_ref, v_ref, do_ref, dc_ref, dkr_ref, g_ref, b_ref, w_ref, bt_ref),
         (dz_ref, dw_ref, db_ref, dlg_ref, dlb_ref), _, pctx) = _split_refs(refs, 9, 5, 0, plan)
        _plan_start(plan, pctx, grid)

        @pl.when(pl.program_id(0) == 0)
        def _():
            for ref in (dw_ref, db_ref, dlg_ref, dlb_ref):
                ref[...] = jnp.zeros_like(ref)

        dz_ref[:, :W] = dc_ref[...]
        dz_ref[:, 3 * W:] = dkr_ref[...]

        u, v, dout = u_ref[...], v_ref[...], do_ref[...]
        ug = _gelu(u)
        xhat, rstd = _ln_stats(_gelu(v))
        vn = (xhat * g_ref[...] + b_ref[...]).astype(BF16)
        dmixed = dout * ug
        dmixed_b = dmixed.astype(BF16)
        tril = lax.broadcasted_iota(jnp.int32, (SGU_C, SGU_C), 0) >= lax.broadcasted_iota(jnp.int32, (SGU_C, SGU_C), 1)
        lane = lax.broadcasted_iota(jnp.int32, (SGU_C, SGU_C), 1)
        dvn_cols = []
        for g in range(SGU_G):
            cs = slice(g * SGU_C, (g + 1) * SGU_C)
            wg = jnp.where(tril, w_ref[g], 0.0).astype(BF16)
            bcol = bt_ref[:, g:g + 1]
            dw_g = jnp.zeros((SGU_C, SGU_C), F32)
            db_g = jnp.zeros((SGU_C, 1), F32)
            dvn_rows = []
            for c in range(tm // SGU_C):
                rs = slice(c * SGU_C, (c + 1) * SGU_C)
                mixed = jnp.dot(wg, vn[rs, cs], preferred_element_type=F32) + bcol
                dz_ref[rs, W + g * SGU_C:W + (g + 1) * SGU_C] = (dout[rs, cs] * mixed * _gelu_grad(u[rs, cs])).astype(BF16)
                dm = dmixed_b[rs, cs]
                dw_g = dw_g + lax.dot_general(dm, vn[rs, cs], NT_DIMS, preferred_element_type=F32)
                db_g = db_g + jnp.sum(dmixed[rs, cs], axis=1, keepdims=True)
                dvn_rows.append(lax.dot_general(wg, dm, TN_DIMS, preferred_element_type=F32))
            dw_ref[g] += jnp.where(tril, dw_g, 0.0)
            db_ref[...] += jnp.where(lane == g, db_g, 0.0)
            dvn_cols.append(jnp.concatenate(dvn_rows, axis=0))
        dvn = jnp.concatenate(dvn_cols, axis=1)
        dxh = dvn * g_ref[...]
        m1 = jnp.mean(dxh, -1, keepdims=True)
        m2 = jnp.mean(dxh * xhat, -1, keepdims=True)
        dvg = rstd * (dxh - m1 - xhat * m2)
        dz_ref[:, 2 * W:3 * W] = (dvg * _gelu_grad(v)).astype(BF16)
        dlg_ref[...] += _fold8(dvn * xhat)
        dlb_ref[...] += _fold8(dvn)
        _plan_wait(plan, pctx, grid)

    full = lambda shape: pl.BlockSpec(shape, lambda i: (0,) * len(shape))
    p_in, p_ospec, p_oshape, p_scr, p_alias = _plan_io(plan, 9, 5)
    return pl.pallas_call(
        body, name="sgu_bwd", grid=grid,
        in_specs=[pl.BlockSpec((tm, W), lambda i: (i, 1)), pl.BlockSpec((tm, W), lambda i: (i, 2)),
                  pl.BlockSpec((tm, W), lambda i: (i, 1)), pl.BlockSpec((tm, W), lambda i: (i, 0)),
                  pl.BlockSpec((tm, 128), lambda i: (i, 0)),
                  full((1, W)), full((1, W)), full((SGU_G, SGU_C, SGU_C)), full((SGU_C, SGU_G))] + [_ANY] * len(p_in),
        out_specs=[pl.BlockSpec((tm, 3 * W + 128), lambda i: (i, 0)), full((SGU_G, SGU_C, SGU_C)),
                   full((SGU_C, SGU_C)), full((8, W)), full((8, W))] + p_ospec,
        out_shape=[jax.ShapeDtypeStruct((T, 3 * W + 128), BF16), jax.ShapeDtypeStruct((SGU_G, SGU_C, SGU_C), F32),
                   jax.ShapeDtypeStruct((SGU_C, SGU_C), F32), jax.ShapeDtypeStruct((8, W), F32),
                   jax.ShapeDtypeStruct((8, W), F32)] + p_oshape,
        scratch_shapes=p_scr, input_output_aliases=p_alias, compiler_params=_params(40, 1),
    )(*_hbm(z0, z0, dmix, dc, dkr, ln_g, ln_b, w, b_t), *p_in)


def _hg_lower_bound(lb_ref):
    a0, a1 = lb_ref[0:1, :], lb_ref[1:2, :]
    m = jnp.maximum(a0, a1)
    e0, e1 = jnp.exp(a0 - m), jnp.exp(a1 - m)
    return e1 / (e0 + e1)


def _running_sum(x, reverse=False):
    n = x.shape[0]
    row = lax.broadcasted_iota(jnp.int32, x.shape, 0)
    s = 1
    while s < n:
        if reverse:
            x = x + jnp.where(row < n - s, pltpu.roll(x, n - s, 0), 0.0)
        else:
            x = x + jnp.where(row >= s, pltpu.roll(x, s, 0), 0.0)
        s *= 2
    return x


def _hg_chunk(qc, fc, lb):
    C = HG_CHUNK
    rows = lax.broadcasted_iota(jnp.int32, (C, C), 0)
    cols = lax.broadcasted_iota(jnp.int32, (C, C), 1)
    rowid = lax.broadcasted_iota(jnp.int32, (C, 128), 0)
    sq, sg = _sigmoid(qc), _sigmoid(fc)
    qf = qc * sq
    gate = lb + (1.0 - lb) * sg
    kk = 1.0 - gate
    lg = jnp.log(gate)
    bcum = _running_sum(lg)
    b_mid = jnp.sum(jnp.where(rowid < C // 2, lg, 0.0), axis=0, keepdims=True)
    b_last = jnp.sum(lg, axis=0, keepdims=True)
    eq, ek, e, eh = jnp.exp(bcum - b_mid), jnp.exp(b_mid - bcum), jnp.exp(bcum), jnp.exp(b_last - bcum)
    qt, kt, qe, khat = qf * eq, kk * ek, qf * e, kk * eh
    a = lax.dot_general(qt.astype(BF16), kt.astype(BF16), NT_DIMS, preferred_element_type=F32)
    a = jnp.where(rows >= cols, a, 0.0)
    return dict(sq=sq, sg=sg, gate=gate, kk=kk, eq=eq, ek=ek, e=e, eh=eh, qt=qt, kt=kt, qe=qe, khat=khat, a=a,
                e_last=jnp.exp(b_last), tril=rows >= cols, rowid=rowid)


def _hgrn_fwd(z4, hg_lb, gnorm):
    T = z4.shape[1]
    tb = min(ROW_BLOCK, T)
    C = HG_CHUNK
    ncb = tb // C
    HPB = HG_HEADS_PER_STEP

    def body(q_ref, f_ref, i_ref, g_ref, lb_ref, gn_ref, y_ref, o_ref, st_ref, st_sc):
        @pl.when(pl.program_id(1) == 0)
        def _():
            st_sc[...] = jnp.zeros_like(st_sc)

        def chunk(c, carry):
            rs = pl.ds(pl.multiple_of(c * C, C), C)
            for hh in range(HPB):
                hs = slice(hh * 128, (hh + 1) * 128)
                lb = _hg_lower_bound(lb_ref.at[:, hs])
                v_b = i_ref[rs, hs].astype(BF16)
                gc = g_ref[rs, hs]
                x = _hg_chunk(q_ref[rs, hs], f_ref[rs, hs], lb)
                st = st_sc[hh]
                st_ref[hh, c] = st
                o = (jnp.dot(x["a"].astype(BF16), v_b, preferred_element_type=F32)
                     + lax.dot_general(x["qe"].astype(BF16), st.astype(BF16), NT_DIMS, preferred_element_type=F32))
                st_sc[hh] = st * x["e_last"] + lax.dot_general(v_b, x["khat"].astype(BF16), TN_DIMS,
                                                               preferred_element_type=F32)
                o_ref[rs, hs] = o
                n = o * lax.rsqrt(jnp.mean(o * o, -1, keepdims=True) + EPS)
                y_ref[rs, hs] = (n * gn_ref[:, hs] * (gc * _sigmoid(gc))).astype(BF16)
            return carry

        lax.fori_loop(0, ncb, chunk, 0, unroll=4)

    W = 128 * HPB
    zb = lambda k: pl.BlockSpec((None, tb, W), lambda h, t: (k, t, h))
    out = pl.BlockSpec((tb, W), lambda h, t: (t, h))
    return pl.pallas_call(
        body, name="hgrn_fwd", grid=(HEADS // HPB, T // tb),
        in_specs=[zb(0), zb(1), zb(2), zb(3), pl.BlockSpec((2, W), lambda h, t: (0, h)),
                  pl.BlockSpec((1, W), lambda h, t: (0, h))],
        out_specs=[out, out, pl.BlockSpec((HPB, ncb, 128, 128), lambda h, t: (h, t, 0, 0))],
        out_shape=[jax.ShapeDtypeStruct((T, D), BF16), jax.ShapeDtypeStruct((T, D), F32),
                   jax.ShapeDtypeStruct((HEADS, T // C, 128, 128), F32)],
        scratch_shapes=[pltpu.VMEM((HPB, 128, 128), F32)],
        compiler_params=_params(48, 2),
    )(*_hbm(z4, z4, z4, z4, hg_lb, gnorm))


def _hgrn_bwd(z4, o_raw, dy, states, hg_lb, gnorm):
    T = z4.shape[1]
    tb = min(ROW_BLOCK, T)
    C = HG_CHUNK
    ncb = tb // C
    nt = T // tb
    HPB = HG_HEADS_PER_STEP

    def body(q_ref, f_ref, i_ref, g_ref, o_ref, dy_ref, st_ref, lb_ref, gn_ref, dz_ref, dlb_ref, dgn_ref, dst_sc):
        @pl.when(pl.program_id(1) == 0)
        def _():
            dst_sc[...] = jnp.zeros_like(dst_sc)
            dlb_ref[...] = jnp.zeros_like(dlb_ref)
            dgn_ref[...] = jnp.zeros_like(dgn_ref)

        def chunk(cc, carry):
            for hh in range(HPB):
                one_head(ncb - 1 - cc, hh, slice(hh * 128, (hh + 1) * 128))
            return carry

        def one_head(c, hh, hs):
            rs = pl.ds(pl.multiple_of(c * C, C), C)
            lb = _hg_lower_bound(lb_ref.at[:, hs])
            gn = gn_ref[:, hs]
            qc, gc = q_ref[rs, hs], g_ref[rs, hs]
            v_b = i_ref[rs, hs].astype(BF16)
            x = _hg_chunk(qc, f_ref[rs, hs], lb)
            st, dst = st_ref[hh, c], dst_sc[hh]
            st_b, dst_b = st.astype(BF16), dst.astype(BF16)
            o, dyc = o_ref[rs, hs], dy_ref[rs, hs]
            sgg = _sigmoid(gc)
            sil = gc * sgg
            rstd = lax.rsqrt(jnp.mean(o * o, -1, keepdims=True) + EPS)
            n = o * rstd
            dgn_ref[:, hs] += _fold8(dyc * n * sil)
            dn = dyc * gn * sil
            do = rstd * (dn - n * jnp.mean(dn * n, -1, keepdims=True))
            dg = dyc * n * gn * (sgg * (1.0 + gc * (1.0 - sgg)))
            do_b = do.astype(BF16)
            da = jnp.where(x["tril"], lax.dot_general(do_b, v_b, NT_DIMS, preferred_element_type=F32), 0.0).astype(BF16)
            qt_b, kt_b, qe_b, khat_b = (x[n_].astype(BF16) for n_ in ("qt", "kt", "qe", "khat"))
            dv = (lax.dot_general(x["a"].astype(BF16), do_b, TN_DIMS, preferred_element_type=F32)
                  + lax.dot_general(khat_b, dst_b, NT_DIMS, preferred_element_type=F32))
            dqt = jnp.dot(da, kt_b, preferred_element_type=F32)
            dqe = jnp.dot(do_b, st_b, preferred_element_type=F32)
            dkt = lax.dot_general(da, qt_b, TN_DIMS, preferred_element_type=F32)
            dkhat = jnp.dot(v_b, dst_b, preferred_element_type=F32)
            dst_sc[hh] = lax.dot_general(do_b, qe_b, TN_DIMS, preferred_element_type=F32) + dst * x["e_last"]
            de_last = jnp.sum(st * dst, axis=0, keepdims=True)
            dqf = dqt * x["eq"] + dqe * x["e"]
            dkk = dkt * x["ek"] + dkhat * x["eh"]
            dkh_kh = dkhat * x["khat"]
            db = dqt * qt_b.astype(F32) - dkt * kt_b.astype(F32) + dqe * x["qe"] - dkh_kh
            db_last = jnp.sum(dkh_kh, axis=0, keepdims=True) + de_last * x["e_last"]
            db = db + jnp.where(x["rowid"] == C - 1, db_last, 0.0)
            dlg = _running_sum(db, reverse=True)
            dgate = dlg / x["gate"] - dkk
            sg, sq = x["sg"], x["sq"]
            dlb_ref[:, hs] += _fold8(dgate * (1.0 - sg)) * (lb * (1.0 - lb))
            dz_ref[0, rs, hs] = (dqf * (sq * (1.0 + qc * (1.0 - sq)))).astype(BF16)
            dz_ref[1, rs, hs] = (dgate * (1.0 - lb) * sg * (1.0 - sg)).astype(BF16)
            dz_ref[2, rs, hs] = dv.astype(BF16)
            dz_ref[3, rs, hs] = dg.astype(BF16)

        lax.fori_loop(0, ncb, chunk, 0, unroll=4)

    W = 128 * HPB
    zb = lambda k: pl.BlockSpec((None, tb, W), lambda h, t: (k, nt - 1 - t, h))
    blk = pl.BlockSpec((tb, W), lambda h, t: (nt - 1 - t, h))
    acc = pl.BlockSpec((8, W), lambda h, t: (0, h))
    return pl.pallas_call(
        body, name="hgrn_bwd", grid=(HEADS // HPB, nt),
        in_specs=[zb(0), zb(1), zb(2), zb(3), blk, blk,
                  pl.BlockSpec((HPB, ncb, 128, 128), lambda h, t: (h, nt - 1 - t, 0, 0)),
                  pl.BlockSpec((2, W), lambda h, t: (0, h)), pl.BlockSpec((1, W), lambda h, t: (0, h))],
        out_specs=[pl.BlockSpec((4, tb, W), lambda h, t: (0, nt - 1 - t, h)), acc, acc],
        out_shape=[jax.ShapeDtypeStruct((4, T, D), BF16), jax.ShapeDtypeStruct((8, D), F32),
                   jax.ShapeDtypeStruct((8, D), F32)],
        scratch_shapes=[pltpu.VMEM((HPB, 128, 128), F32)],
        compiler_params=_params(48, 2),
    )(*_hbm(z4, z4, z4, z4, o_raw, dy, states, hg_lb, gnorm))


def _adamw(w, g, m, v, *, name, with_g=False):
    R, L = w.shape
    tr = R if R <= 512 else 512
    assert R % tr == 0
    blk = pl.BlockSpec((tr, L), lambda i: (i, 0))
    c1, c2 = 1.0 - B1 ** STEP, 1.0 - B2 ** STEP
    n_out = 4 if with_g else 3

    def body(w_ref, g_ref, m_ref, v_ref, *o_refs):
        d_ref, mo_ref, vo_ref = o_refs[-3:]
        g_ = g_ref[...]
        m_ = B1 * m_ref[...] + (1.0 - B1) * g_
        v_ = B2 * v_ref[...] + (1.0 - B2) * (g_ * g_)
        if with_g:
            o_refs[0][...] = g_
        d_ref[...] = -LR * ((m_ / c1) / (jnp.sqrt(v_ / c2) + ADAM_EPS) + WD * w_ref[...])
        mo_ref[...] = m_
        vo_ref[...] = v_

    sds = jax.ShapeDtypeStruct((R, L), F32)
    return pl.pallas_call(
        body, name=name, grid=(R // tr,), in_specs=[blk] * 4, out_specs=[blk] * n_out, out_shape=[sds] * n_out,
        compiler_params=_params(32, 1),
    )(*_hbm(w, g, m, v))


def _adamw_rows(w, m, v, gbufs, row0, *, name):
    L, R, C = w.shape
    tr = 256 if R > 256 and R % 256 == 0 else max(t for t in range(8, 129, 8) if R % t == 0)
    n_in, n_out = 3, 2
    assert R % tr == 0 and len(gbufs) == L
    steps = [(l, i * tr) for l in range(L) for i in range(R // tr)]
    c1, c2 = 1.0 - B1 ** STEP, 1.0 - B2 ** STEP

    def body(*refs):
        w_ref, m_ref, v_ref = refs[:3]
        g_refs, out_refs = refs[3:3 + L], refs[3 + L:7 + L]
        ibuf, obuf, isem, osem = refs[7 + L:]

        def reads(s):
            l, r0 = steps[s]
            srcs = [w_ref.at[l, pl.ds(r0, tr)], m_ref.at[l, pl.ds(r0, tr)], v_ref.at[l, pl.ds(r0, tr)],
                    g_refs[l].at[pl.ds(row0 + r0, tr)]]
            return [pltpu.make_async_copy(src, ibuf.at[s % n_in, k], isem.at[s % n_in, k]) for k, src in enumerate(srcs)]

        def writes(s):
            l, r0 = steps[s]
            return [pltpu.make_async_copy(obuf.at[s % n_out, k], o_ref.at[l, pl.ds(r0, tr)], osem.at[s % n_out, k])
                    for k, o_ref in enumerate(out_refs)]

        for s in range(min(n_in, len(steps))):
            for c in reads(s):
                c.start()
        for s in range(len(steps)):
            for c in reads(s):
                c.wait()
            if s >= n_out:
                for c in writes(s - n_out):
                    c.wait()
            i_, o_ = s % n_in, s % n_out
            g_ = ibuf[i_, 3]
            m_ = B1 * ibuf[i_, 1] + (1.0 - B1) * g_
            v_ = B2 * ibuf[i_, 2] + (1.0 - B2) * (g_ * g_)
            obuf[o_, 0] = g_
            obuf[o_, 1] = -LR * ((m_ / c1) / (jnp.sqrt(v_ / c2) + ADAM_EPS) + WD * ibuf[i_, 0])
            obuf[o_, 2] = m_
            obuf[o_, 3] = v_
            for c in writes(s):
                c.start()
            if s + n_in < len(steps):
                for c in reads(s + n_in):
                    c.start()
        for s in range(max(0, len(steps) - n_out), len(steps)):
            for c in writes(s):
                c.wait()

    sds = jax.ShapeDtypeStruct((L, R, C), F32)
    return pl.pallas_call(
        body, name=name, in_specs=[_ANY] * (3 + L), out_specs=[_ANY] * 4, out_shape=[sds] * 4,
        scratch_shapes=[pltpu.VMEM((n_in, 4, tr, C), F32), pltpu.VMEM((n_out, 4, tr, C), F32),
                        pltpu.SemaphoreType.DMA((n_in, 4)), pltpu.SemaphoreType.DMA((n_out, 4))],
        compiler_params=_params(40),
    )(*_hbm(w, m, v, *gbufs))


def _add_pairs(g, theirs, ids, *, name):
    n, R, L = theirs.shape
    tr = math.gcd(R, 128)
    nb = R // tr

    def body(ids_ref, a_ref, b_ref, o_ref):
        o_ref[...] = (a_ref[...].astype(F32) + b_ref[...].astype(F32)).astype(BF16)

    blk = pl.BlockSpec((n, tr, L), lambda i, ids: (0, i, 0))
    return pl.pallas_call(
        body, name=name, out_shape=jax.ShapeDtypeStruct((n, R, L), BF16),
        grid_spec=pltpu.PrefetchScalarGridSpec(
            num_scalar_prefetch=1, grid=(nb,),
            in_specs=[pl.BlockSpec((n, tr, L), lambda i, ids: (0, ids[1] * nb + i, 0)), blk], out_specs=blk),
        compiler_params=_params(16, 1),
    )(ids, *_hbm(g, theirs))


def _sum_chips(pair, parts, ids, *, name):
    _, R, L = parts.shape
    tr = math.gcd(R, 128)

    def body(ids_ref, o_ref, r_ref, out_ref):
        out_ref[...] = ((o_ref[...].astype(F32) + r_ref[0].astype(F32)) + r_ref[1].astype(F32)) + r_ref[2].astype(F32)

    return pl.pallas_call(
        body, name=name, out_shape=jax.ShapeDtypeStruct((2, R, L), F32),
        grid_spec=pltpu.PrefetchScalarGridSpec(
            num_scalar_prefetch=1, grid=(R // tr,),
            in_specs=[pl.BlockSpec((None, tr, L), lambda i, ids: (ids[0], i, 0)),
                      pl.BlockSpec((3, tr, L), lambda i, ids: (0, i, 0))],
            out_specs=pl.BlockSpec((None, tr, L), lambda i, ids: (ids[1], i, 0))),
        compiler_params=_params(32, 1),
    )(ids, *_hbm(pair, parts))


def _mesh_ids():
    x, y, c = _mesh_pos()
    return jnp.stack([2 * x + y, c]).astype(jnp.int32)


def _place_shard(rows, ids, *, name):
    R, L = rows.shape
    tr = 128

    def body(ids_ref, in_ref, out_ref):
        out_ref[...] = in_ref[...].astype(BF16)

    return pl.pallas_call(
        body, name=name, out_shape=jax.ShapeDtypeStruct((4, R, L), BF16),
        grid_spec=pltpu.PrefetchScalarGridSpec(
            num_scalar_prefetch=1, grid=(R // tr,), in_specs=[pl.BlockSpec((tr, L), lambda i, ids: (i, 0))],
            out_specs=pl.BlockSpec((None, tr, L), lambda i, ids: (ids[0], i, 0))),
        compiler_params=_params(16, 1),
    )(ids, *_hbm(rows))


IN_E_SHARD, IN_E_LAYOUT = 392, 1664


def _in_e_runs():
    runs = []
    for lo, hi, dst in ((0, 512, 0), (512, 544, 1536), (544, 1568, 512)):
        while lo < hi:
            j = lo // IN_E_SHARD
            wd = min(hi, IN_E_SHARD * (j + 1)) - lo
            runs.append((j, lo - IN_E_SHARD * j, dst, wd))
            lo, dst = lo + wd, dst + wd
    return runs


def _in_e_to_layout(shards):
    tr = 256

    def body(s_ref, o_ref, t_ref):
        t_ref[...] = jnp.zeros_like(t_ref)
        for j in range(4):
            s = s_ref[j].astype(F32)
            for _, c, dst, wd in (r for r in _in_e_runs() if r[0] == j):
                t_ref[:, dst:dst + wd] = s[:, c:c + wd]
        o_ref[...] = t_ref[...].astype(BF16)

    return pl.pallas_call(
        body, name="in_e_to_layout", out_shape=jax.ShapeDtypeStruct((D, IN_E_LAYOUT), BF16), grid=(D // tr,),
        in_specs=[pl.BlockSpec((4, tr, IN_E_SHARD), lambda i: (0, i, 0))],
        out_specs=pl.BlockSpec((tr, IN_E_LAYOUT), lambda i: (i, 0)),
        scratch_shapes=[pltpu.VMEM((tr, IN_E_LAYOUT), F32)], compiler_params=_params(16, 1),
    )(*_hbm(shards))


def _in_e_from_layout(g):
    tr = 256

    def body(g_ref, o_ref, t_ref):
        g_ = g_ref[...]
        for j, c, src, wd in _in_e_runs():
            t_ref[j, :, c:c + wd] = g_[:, src:src + wd]
        o_ref[...] = t_ref[...].astype(BF16)

    return pl.pallas_call(
        body, name="in_e_from_layout", out_shape=jax.ShapeDtypeStruct((4, D, IN_E_SHARD), BF16), grid=(D // tr,),
        in_specs=[pl.BlockSpec((tr, IN_E_LAYOUT), lambda i: (i, 0))],
        out_specs=pl.BlockSpec((4, tr, IN_E_SHARD), lambda i: (0, i, 0)),
        scratch_shapes=[pltpu.VMEM((4, tr, IN_E_SHARD), F32)], compiler_params=_params(16, 1),
    )(*_hbm(g))


def _place_weights(pieces, buffer_rows, ids, *, plan=None):
    tr = 256
    steps, s = [], 0
    for arr, layer, buf, row0 in pieces:
        nblk = arr.shape[1] // tr
        steps.append((s, nblk))
        s += nblk
    total = s
    buf_start = [min(st for (st, _), p in zip(steps, pieces) if p[2] == k) for k in range(len(buffer_rows))]
    grid = (total,)
    n_in = len(pieces)

    def body(*refs):
        ins, outs, _, pctx = _split_refs(refs[1:], n_in, len(buffer_rows), 0, plan)
        _plan_start(plan, pctx, grid)
        i = pl.program_id(0)
        for (st, nblk), (_, _, buf, _), ref in zip(steps, pieces, ins):
            @pl.when((i >= st) & (i < st + nblk))
            def _(ref=ref, buf=buf):
                outs[buf][...] = ref[...].astype(BF16)
        _plan_wait(plan, pctx, grid)

    in_specs = [pl.BlockSpec((None, tr, D), lambda i, ids, layer=layer, st=st, nblk=nblk:
                             (layer, jnp.clip(i - st, 0, nblk - 1), 0))
                for (st, nblk), (_, layer, _, _) in zip(steps, pieces)]
    out_specs = [pl.BlockSpec((None, tr, D), lambda i, ids, st=st, nb=rows // tr: (ids[0], jnp.clip(i - st, 0, nb - 1), 0))
                 for st, rows in zip(buf_start, buffer_rows)]
    p_in, p_ospec, p_oshape, p_scr, p_alias = _plan_io(plan, 1 + n_in, len(buffer_rows))
    return pl.pallas_call(
        body, name="place_weights",
        out_shape=[jax.ShapeDtypeStruct((4, rows, D), BF16) for rows in buffer_rows] + p_oshape,
        grid_spec=pltpu.PrefetchScalarGridSpec(
            num_scalar_prefetch=1, grid=grid, in_specs=in_specs + [_ANY] * len(p_in), out_specs=out_specs + p_ospec,
            scratch_shapes=p_scr),
        input_output_aliases=p_alias, compiler_params=_params(16, 1),
    )(ids, *_hbm(*[p[0] for p in pieces]), *p_in)


def _remote(src, dst, send_sem, recv_sem, to):
    return pltpu.make_async_remote_copy(src_ref=src, dst_ref=dst, send_sem=send_sem, recv_sem=recv_sem,
                                        device_id=to, device_id_type=MESH_IDS)


def _rows(ref, lead, start, size):
    return ref.at[tuple(pl.ds(0, n) for n in ref.shape[:lead]) + (pl.ds(start, size),)]


def _other_chips():
    x, y, _ = _mesh_pos()
    return [(1 - x, y), (x, 1 - y), (1 - x, 1 - y)]


def _plan_gather_ici(bufs):
    n = len(bufs)

    def copies(outs, send, recv):
        x, y, c = _mesh_pos()
        res = []
        for b in range(n):
            half = bufs[b].shape[1] // 2
            mine = _rows(outs[b].at[2 * x + y], 0, c * half, half)
            for j, (cx, cy) in enumerate(_other_chips()):
                res.append((_remote(mine, mine, send(3 * b + j), recv(3 * b + j), (cx, cy, c)),
                            _remote(mine, _rows(outs[b].at[2 * cx + cy], 0, c * half, half),
                                    send(3 * b + j), recv(3 * b + j), (x, y, c))))
        return res

    def start(ins, outs, send, recv, loc):
        for out_cp, _ in copies(outs, send, recv):
            out_cp.start()

    def wait(ins, outs, send, recv, loc):
        for out_cp, in_cp in copies(outs, send, recv):
            in_cp.wait_recv()
            out_cp.wait_send()

    outs = [jax.ShapeDtypeStruct(b.shape, b.dtype) for b in bufs]
    return _Plan(bufs, outs, 3 * n, 0, start, wait, aliases={b: b for b in range(n)})


def _plan_gather_forward(bufs):
    n = len(bufs)

    def copies(outs, send, recv):
        x, y, c = _mesh_pos()
        res = []
        for b in range(n):
            half = bufs[b].shape[1] // 2
            for j, (cx, cy) in enumerate(_other_chips()):
                slot = outs[b].at[2 * cx + cy]
                res.append((_remote(_rows(slot, 0, c * half, half), _rows(slot, 0, c * half, half),
                                    send(3 * b + j), recv(3 * b + j), (x, y, 1 - c)),
                            _remote(_rows(slot, 0, c * half, half), _rows(slot, 0, (1 - c) * half, half),
                                    send(3 * b + j), recv(3 * b + j), (x, y, c))))
        return res

    def start(ins, outs, send, recv, loc):
        for out_cp, _ in copies(outs, send, recv):
            out_cp.start()

    def wait(ins, outs, send, recv, loc):
        for out_cp, in_cp in copies(outs, send, recv):
            in_cp.wait_recv()
            out_cp.wait_send()

    outs = [jax.ShapeDtypeStruct(b.shape, b.dtype) for b in bufs]
    return _Plan(bufs, outs, 3 * n, 0, start, wait, aliases={b: b for b in range(n)})


def _plan_pair_swap(g):
    half = g.shape[1] // 2

    def copy(ins, outs, send, recv, loc):
        x, y, c = _mesh_pos()
        return _remote(_rows(ins[0], 1, (1 - c) * half, half), outs[0], send(0), recv(0), (x, y, 1 - c))

    return _Plan([g], [jax.ShapeDtypeStruct((4, half, g.shape[2]), g.dtype)], 1, 0,
                 lambda *a: copy(*a).start(), lambda *a: copy(*a).wait())


def _plan_pair_gather(buf):
    def copies(ins, outs, send, recv, loc):
        x, y, c = _mesh_pos()
        return (_remote(outs[0].at[c], outs[0].at[c], send(0), recv(0), (x, y, 1 - c)),
                _remote(outs[0].at[c], outs[0].at[1 - c], send(0), recv(0), (x, y, c)))

    def wait(*a):
        out_cp, in_cp = copies(*a)
        in_cp.wait_recv()
        out_cp.wait_send()

    return _Plan([buf], [jax.ShapeDtypeStruct(buf.shape, buf.dtype)], 1, 0, lambda *a: copies(*a)[0].start(), wait,
                 aliases={0: 0})


def _plan_chip_scatter(p):
    def copies(ins, outs, send, recv, loc):
        _, _, c = _mesh_pos()
        return [_remote(ins[0].at[2 * cx + cy], outs[0].at[j], send(j), recv(j), (cx, cy, c))
                for j, (cx, cy) in enumerate(_other_chips())]

    def start(*a):
        for cp in copies(*a):
            cp.start()

    def wait(*a):
        for cp in copies(*a):
            cp.wait()

    return _Plan([p], [jax.ShapeDtypeStruct((3,) + p.shape[1:], p.dtype)], 3, 0, start, wait)


def _plan_exchange_all(vec):
    def copies(ins, outs, send, recv, loc):
        x, y, c = _mesh_pos()
        return [_remote(ins[0], outs[0].at[r - 1], send(r - 1), recv(r - 1), (x ^ (r >> 2), y ^ ((r >> 1) & 1), c ^ (r & 1)))
                for r in range(1, 8)]

    def start(*a):
        for cp in copies(*a):
            cp.start()

    def wait(*a):
        for cp in copies(*a):
            cp.wait()

    return _Plan([vec], [jax.ShapeDtypeStruct((7,) + vec.shape, vec.dtype)], 7, 0, start, wait)


SMALL_LAYOUT = {
    "mla_gq": (0, 1, 256, (1, 256)), "mla_gkv": (1, 1, 256, (1, 256)), "sgu_ln_g": (2, 1, 512, (1, 512)),
    "sgu_ln_b": (3, 1, 512, (1, 512)), "sgu_w": (4, 64, 1024, (64, 1024)), "sgu_b": (68, 1, 512, (1, 512)),
    "hg_lb": (69, 2, 1024, (2, 1024)), "hg_gnorm": (71, 1, 1024, (1, 256)), "ln1_g": (72, 2, 1024, (2, 1024)),
    "ln1_b": (74, 2, 1024, (2, 1024)), "ln2_g": (76, 2, 1024, (2, 1024)), "ln2_b": (78, 2, 1024, (2, 1024)),
}


def _small_pack(dgq, dgkv, dslg, dslb, dsw, dsb, dlb, dgn, ln_parts, sq_err):
    flat_ln = [p for pair in ln_parts for p in pair]

    def body(*refs):
        gq_ref, gkv_ref, slg_ref, slb_ref, sw_ref, sb_ref, lb_ref, gn_ref = refs[:8]
        ln_refs, err_ref, out_ref, t_sc = refs[8:16], refs[16], refs[17], refs[18]
        s8 = lambda ref: jnp.sum(ref[...], axis=0, keepdims=True)
        out_ref[...] = jnp.zeros_like(out_ref)
        out_ref[0:1, 0:256] = s8(gq_ref)
        out_ref[1:2, 0:256] = s8(gkv_ref)
        out_ref[2:3, 0:512] = s8(slg_ref)
        out_ref[3:4, 0:512] = s8(slb_ref)
        out_ref[4:68, :] = sw_ref[...]
        t_sc[...] = sb_ref[...].T
        for g in range(SGU_G):
            out_ref[68:69, g * SGU_C:(g + 1) * SGU_C] = t_sc[g:g + 1, :]
        d_lb1 = s8(lb_ref)
        out_ref[69:70, :] = -d_lb1
        out_ref[70:71, :] = d_lb1
        out_ref[71:72, :] = s8(gn_ref)
        for k, ref in enumerate(ln_refs):
            out_ref[72 + k:73 + k, :] = s8(ref)
        out_ref[0:1, 1023:1024] = jnp.sum(s8(err_ref), axis=1, keepdims=True) * (0.5 / D)

    vm = pl.BlockSpec(memory_space=pltpu.VMEM)
    return pl.pallas_call(
        body, name="small_grad_pack", in_specs=[vm] * 17, out_specs=vm,
        out_shape=jax.ShapeDtypeStruct((SMALL_ROWS, 1024), F32), scratch_shapes=[pltpu.VMEM((SGU_C, SGU_C), F32)],
        compiler_params=_params(16),
    )(dgq, dgkv, dslg, dslb, dsw.reshape(64, 1024), dsb, dlb, dgn, *flat_ln, sq_err)


def _small_update(vec, others, ids, w, m, v):
    names = list(SMALL_LAYOUT)
    n = len(names)
    c1, c2 = 1.0 - B1 ** STEP, 1.0 - B2 ** STEP
    have_others = others is not None

    def body(*refs):
        ids_ref, v_ref = refs[0], refs[1]
        k = 2 + have_others
        w_refs, m_refs, v_refs = refs[k:k + n], refs[k + n:k + 2 * n], refs[k + 2 * n:k + 3 * n]
        outs = refs[k + 3 * n:]
        row0_ref, tot_sc = outs[0], outs[-1]
        total = v_ref[...]
        if have_others:
            me = 2 * ids_ref[0] + ids_ref[1]
            total = None
            for d in range(8):
                rel = d ^ me
                term = jnp.where(rel == 0, v_ref[...], refs[2][jnp.maximum(rel - 1, 0)])
                total = term if total is None else total + term
        tot_sc[...] = total
        row0_ref[...] = tot_sc[0:1, :]
        for i, name in enumerate(names):
            r0, nr, width, _ = SMALL_LAYOUT[name]
            if name == "hg_gnorm":
                g_ = tot_sc[r0:r0 + 1, 0:256]
                for chip in range(1, 4):
                    g_ = jnp.where(ids_ref[0] == chip, tot_sc[r0:r0 + 1, chip * 256:(chip + 1) * 256], g_)
            else:
                g_ = tot_sc[r0:r0 + nr, 0:width]
            m_ = B1 * m_refs[i][...] + (1.0 - B1) * g_
            v_ = B2 * v_refs[i][...] + (1.0 - B2) * (g_ * g_)
            go, do, mo, vo = outs[1 + 4 * i:5 + 4 * i]
            go[...] = g_
            do[...] = -LR * ((m_ / c1) / (jnp.sqrt(v_ / c2) + ADAM_EPS) + WD * w_refs[i][...])
            mo[...] = m_
            vo[...] = v_

    full = lambda shape: pl.BlockSpec(shape, lambda i, ids, nd=len(shape): (0,) * nd)
    kshapes = [SMALL_LAYOUT[name][3] for name in names]
    operands = [vec] + ([others] if have_others else []) + [d[name] for d in (w, m, v) for name in names]
    out_shapes = [jax.ShapeDtypeStruct((1, 1024), F32)] + [jax.ShapeDtypeStruct(s, F32) for s in kshapes for _ in range(4)]
    res = pl.pallas_call(
        body, name="small_update", out_shape=out_shapes,
        grid_spec=pltpu.PrefetchScalarGridSpec(
            num_scalar_prefetch=1, grid=(1,), in_specs=[full(o.shape) for o in operands],
            out_specs=[full(s.shape) for s in out_shapes],
            scratch_shapes=[pltpu.VMEM((SMALL_ROWS, 1024), F32)]),
        compiler_params=_params(32, 1),
    )(ids, *operands)
    return res[0], {name: tuple(res[1 + 4 * i:5 + 4 * i]) for i, name in enumerate(names)}


ROWS_L1, ROWS_L0, ROWS_ODD_W = 3328, 2048, 384
ODD_PARTS = (("w_out_e", (256, 1024)), ("w_in_e", (1024, 392)), ("w_qb", (256, 192)), ("w_kvb", (256, 256)))
ODD_W_PARTS = tuple(p for p in ODD_PARTS if p[0] != "w_in_e")


def _odd_rows(parts, dtype, layout, total, gnorm=None):
    rows = [parts[n].reshape(-1, 1024).astype(dtype) for n, _ in layout]
    used = sum(r.shape[0] for r in rows)
    if gnorm is not None:
        bits = lax.bitcast_convert_type(gnorm.reshape(-1), BF16).reshape(1, 512)
        rows.append(jnp.pad(bits, ((0, 15), (0, 512))))
        used += 16
    if total > used:
        rows.append(jnp.zeros((total - used, 1024), dtype))
    return jnp.concatenate(rows, axis=0)


def _odd_unrows(buf, layout, with_gnorm=False):
    out, off = {}, 0
    for n, shape in layout:
        nr = math.prod(shape) // 1024
        out[n] = buf[off:off + nr].reshape(shape)
        off += nr
    if with_gnorm:
        out["hg_gnorm"] = lax.bitcast_convert_type(buf[off, :512].reshape(256, 2), F32).reshape(1, 256)
    return out


def _rope_tables(positions):
    half = ROPE // 2
    inv_freq = ROPE_BASE ** (-jnp.arange(half, dtype=F32) / half)
    per_row = 128 // half
    ang = jnp.repeat(positions.astype(F32).reshape(-1, per_row), half, axis=1) * jnp.tile(inv_freq, per_row)
    cos, sin = jnp.cos(ang).reshape(-1, half), jnp.sin(ang).reshape(-1, half)
    T = cos.shape[0]
    one, z16, z32 = jnp.ones((T, NOPE), F32), jnp.zeros((T, half), F32), jnp.zeros((T, 32), F32)
    z64 = jnp.zeros((T, NOPE), F32)
    c = jnp.concatenate([one, cos, cos, z32], axis=1)
    s1 = jnp.concatenate([z64, -sin, z16, z32], axis=1)
    s2 = jnp.concatenate([z64, z16, sin, z32], axis=1)
    return c, s1, s2


def _local_step(x, positions, tgt, odd, bufs, P, exchange):
    T = x.shape[0]
    row = lambda a: a.reshape(1, -1)
    rc, rs1, rs2 = _rope_tables(positions)
    blk = lambda f: pl.BlockSpec((None, D, D), f)

    w_in = _in_e_to_layout(odd["w_in_e"])
    wq = jnp.pad(odd["w_qb"].reshape(256, HEADS, NOPE + ROPE), ((0, 0), (0, 0), (0, 32))).reshape(256, HEADS * 128)
    kvb = odd["w_kvb"].reshape(256, HEADS, NOPE + VDIM)
    wk = jnp.pad(kvb[:, :, :NOPE], ((0, 0), (0, 0), (0, 64))).reshape(256, HEADS * 128)
    wv = kvb[:, :, NOPE:].reshape(256, HEADS * VDIM)
    w_out_e = odd["w_out_e"]
    sgu_w = P["sgu_w"][0]
    sgu_bt = P["sgu_b"][0].T
    gq, gkv = P["mla_gq"], P["mla_gkv"]
    gnorm = P["hg_gnorm"]

    z0 = _matmul(x, w_in, name="in_proj_e", M=T, N=1664, K=D, tn=1664)[0]
    q, k, v = _mla_prep(z0, gq, gkv, wq, wk, wv, rc, rs1, rs2)
    if exchange:
        ids = _mesh_ids()
        placed = list(bufs)
        a_out, lse, wga, wgb = _flash_fwd(q, k, v, plan=_plan_gather_ici(placed[:2]))
    else:
        a_out, lse = _flash_fwd(q, k, v)
        wga, wgb, wgc = bufs
    mix0 = _sgu_fwd(z0, a_out, P["sgu_ln_g"], P["sgu_ln_b"], sgu_w, sgu_bt)
    res = _proj_ln(mix0, w_out_e, x, row(P["ln1_g"][0]), row(P["ln1_b"][0]), name="out_proj_ln_e",
                   plan=_plan_gather_forward([wga, wgb]) if exchange else None)
    r1, h1b = res[:2]
    if exchange:
        wga, wgb = res[2:]
    ln = lambda name, l: (row(P[name + "_g"][l]), row(P[name + "_b"][l]))
    res = _ffn_ln(h1b, wga, r1, *ln("ln2", 0), name="ffn_ln_0", prev_ln=ln("ln1", 0),
                  plan=_plan_gather_ici(placed[2:]) if exchange else None)
    ra0, r2, h2b = res[:3]
    z4 = _matmul(h2b, wgb, name="in_proj_o", M=T, N=4 * D, K=D, tn=2 * D, n_slots=True,
                 b_spec=pl.BlockSpec((2, D, D), lambda i, j, k: (j, 0, 0)),
                 out_shape=jax.ShapeDtypeStruct((4, T, D), F32),
                 o_spec=pl.BlockSpec((2, min(MM_ROWS, T), D), lambda i, j, k: (j, i, 0)))[0]
    y1, o_raw, states = _hgrn_fwd(z4, P["hg_lb"], gnorm)
    res2 = _proj_ln(y1, wgb, r2, *ln("ln1", 1), name="out_proj_ln_o", prev_ln=ln("ln2", 0), w_rowblk=4,
                    plan=_plan_gather_forward([res[3]]) if exchange else None)
    r3, h3b = res2[:2]
    if exchange:
        wgc = res2[2]
    ra1, r4, _ = _ffn_ln(h3b, wgc, r3, *ln("ln2", 1), name="ffn_ln_1", prev_ln=ln("ln1", 1))

    ln1_g, ln1_b, ln2_g, ln2_b = [None, None], [None, None], [None, None], [None, None]
    sq_err_parts = []

    def ffn_bwd(l, dh, r_out, ra, h_mid_b, g2, wg, rows, plan=None, loss_head=()):
        dr, dr_b, dg, db, *sq_err = _ln_bwd(dh, r_out, row(g2), name=f"ln2_bwd_{l}", loss_head=loss_head)
        sq_err_parts.extend(sq_err)
        ln2_g[l], ln2_b[l] = dg, db
        da, *extra = _matmul(dr_b, wg, tb=True, mul=ra, out_dtype=BF16, name=f"ffn_da_{l}", M=T, N=4 * D, K=D, tn=2 * D,
                             b_spec=pl.BlockSpec((2, D, D), lambda i, j, k: (j, 1, 0)), n_slots=True, plan=plan)
        gbuf = _matmul(ra, dr_b, ta=True, a_sq=True, name=f"ffn_dw2_{l}", M=4 * D, N=D, K=T, tm=1024, tk=DW_TOKENS // 2,
                       out_shape=jax.ShapeDtypeStruct((4, rows, D), BF16), o_spec=blk(lambda i, j, k: (i, 1, 0)))[0]
        gbuf = _matmul(h_mid_b, da, ta=True, name=f"ffn_dw1_{l}", M=D, N=4 * D, K=T, tm=1024, tk=DW_TOKENS, into=gbuf,
                       out_shape=jax.ShapeDtypeStruct((4, rows, D), BF16), o_spec=blk(lambda i, j, k: (j, 0, 0)))[0]
        dh_mid = _matmul(da, wg, tb=True, add=dr, add_scale=ALPHA, name=f"ffn_dh_{l}", M=T, N=D, K=4 * D, tk=2 * D,
                         b_spec=pl.BlockSpec((2, D, D), lambda i, j, k: (k, 0, 0)))[0]
        return dh_mid, gbuf, extra

    dh3, g1, _ = ffn_bwd(1, None, r4, ra1, h3b, P["ln2_g"][1], wgc, ROWS_L1, loss_head=(row(P["ln2_b"][1]), tgt))
    loss_parts = sq_err_parts[0]
    dr3, dr3_b, dg, db = _ln_bwd(dh3, r3, row(P["ln1_g"][1]), name="ln1_bwd_1")
    ln1_g[1], ln1_b[1] = dg, db
    g1_sds = jax.ShapeDtypeStruct((4, ROWS_L1, D), BF16)
    g1 = _matmul(y1, dr3_b, ta=True, name="dw_out_o", M=D, N=D, K=T, tm=256, tk=DW_TOKENS, into=g1, out_shape=g1_sds,
                 o_spec=pl.BlockSpec((None, 256, D), lambda i, j, k: (i, 12, 0)))[0]
    dmix1 = _matmul(dr3_b, wgb, tb=True, name="dmix_o", M=T, N=D, K=D, b_spec=_rows4_spec(4, 3), b_merge=(D, D))[0]
    dz4, dlb, dgn = _hgrn_bwd(z4, o_raw, dmix1, states, P["hg_lb"], gnorm)
    g1 = _matmul(h2b, dz4, ta=True, name="dw_in_o", M=D, N=4 * D, K=T, tm=1024, tk=DW_TOKENS, into=g1, out_shape=g1_sds,
                 b_spec=pl.BlockSpec((None, min(DW_TOKENS, T), D), lambda i, j, k: (j, k, 0)),
                 o_spec=blk(lambda i, j, k: (j, 2, 0)))[0]
    dh2 = _matmul(dz4, wgb, tb=True, add=dr3, add_scale=ALPHA, name="dh_in_o", M=T, N=D, K=4 * D, tk=2 * D,
                  a_spec=pl.BlockSpec((2, min(MM_ROWS, T), D), lambda i, j, k: (k, i, 0)),
                  b_spec=pl.BlockSpec((2, D, D), lambda i, j, k: (k, 0, 0)))[0]

    dh1, g0, swapped1 = ffn_bwd(0, dh2, r2, ra0, h1b, P["ln2_g"][0], wga, ROWS_L0,
                                plan=_plan_pair_swap(g1) if exchange else None)
    dr1, dr1_b, dg, db = _ln_bwd(dh1, r1, row(P["ln1_g"][0]), name="ln1_bwd_0")
    ln1_g[0], ln1_b[0] = dg, db
    godd = {"w_out_e": _matmul(mix0, dr1_b, ta=True, name="dw_out_e", M=D, N=D, K=T, tm=1024, tk=DW_TOKENS)[0]}
    dmix0, *swapped0 = _matmul(dr1_b, w_out_e, tb=True, name="dmix_e", M=T, N=D, K=D,
                               plan=_plan_pair_swap(g0) if exchange else None)
    delta, do_b = _attn_delta(dmix0, a_out)
    if exchange:
        pair1 = _add_pairs(g1, swapped1[0], ids, name="grad_pair_add_1")
        pair0 = _add_pairs(g0, swapped0[0], ids, name="grad_pair_add_0")
        dq4, dk, dv, parts0, parts1 = _flash_bwd(
            q, k, v, do_b, lse, delta, plan=_join_plans([_plan_chip_scatter(pair0), _plan_chip_scatter(pair1)]))
        half0 = _sum_chips(pair0, parts0, ids, name="grad_chip_sum_0")
        half1 = _sum_chips(pair1, parts1, ids, name="grad_chip_sum_1")
        dc, dkr, dwq, dwk, dwv, dgq, dgkv, g0, g1 = _mla_bwd(
            z0, dq4, dk, dv, gq, gkv, wq, wk, wv, rc, rs1, rs2,
            plan=_join_plans([_plan_pair_gather(half0), _plan_pair_gather(half1)]))
        g0, g1 = g0.reshape(ROWS_L0, D), g1.reshape(ROWS_L1, D)
    else:
        dq4, dk, dv = _flash_bwd(q, k, v, do_b, lse, delta)
        dc, dkr, dwq, dwk, dwv, dgq, dgkv = _mla_bwd(z0, dq4, dk, dv, gq, gkv, wq, wk, wv, rc, rs1, rs2)
    godd["w_qb"] = dwq.reshape(256, HEADS, 128)[:, :, :NOPE + ROPE].reshape(256, HEADS * (NOPE + ROPE))
    godd["w_kvb"] = jnp.concatenate([dwk.reshape(256, HEADS, 128)[:, :, :NOPE], dwv.reshape(256, HEADS, VDIM)],
                                    axis=2).reshape(256, HEADS * (NOPE + VDIM))
    swap_b = None
    if exchange:
        by_chip = [_odd_rows({"w_out_e": jnp.split(godd["w_out_e"], 4, axis=0)[j],
                              **{n: jnp.split(godd[n], 4, axis=1)[j] for n in ("w_qb", "w_kvb")}}, BF16,
                             ODD_W_PARTS, ROWS_ODD_W)
                   for j in range(4)]
        odd_b = jnp.stack(by_chip)
        swap_b = _plan_pair_swap(odd_b)
    dz0, dsw, dsb, dslg, dslb, *theirs_b = _sgu_bwd(z0, dmix0, dc, dkr, P["sgu_ln_g"], P["sgu_ln_b"], sgu_w, sgu_bt,
                                                    plan=swap_b)
    small_vec = _small_pack(dgq, dgkv, dslg, dslb, dsw, dsb, dlb, dgn, [ln1_g, ln1_b, ln2_g, ln2_b], loss_parts)
    plan_in = None
    if exchange:
        pair_b = _add_pairs(odd_b, theirs_b[0], ids, name="odd_pair_add_1")
        plan_in = _join_plans([_plan_exchange_all(small_vec), _plan_chip_scatter(pair_b)])
    dw_in, *carried = _matmul(x, dz0, ta=True, name="dw_in_e", M=D, N=1664, K=T, tm=1024, tn=1664, tk=DW_TOKENS // 4,
                              plan=plan_in)
    odd_a = godd["w_in_e"] = _in_e_from_layout(dw_in)
    plan_x = None
    if exchange:
        small_others, parts_b = carried
        theirs_a = _run_plan(_plan_pair_swap(odd_a), name="odd_pair_swap")[0]
        pair_a = _add_pairs(odd_a, theirs_a, ids, name="odd_pair_add_0")
        plan_x = _plan_chip_scatter(pair_a)
    grad_x, *parts_a = _matmul(dz0, w_in, tb=True, add=dr1, add_scale=ALPHA, name="dx", M=T, N=D, K=1664, tk=1664,
                               plan=plan_x)
    if exchange:
        godd = ([pair_a, pair_b], [parts_a[0], parts_b])
        return grad_x, g0, g1, godd, small_vec, small_others
    return grad_x, g0, g1, godd, small_vec, None


WEIGHTS = ['w_in_e', 'mla_gq', 'mla_gkv', 'w_qb', 'w_kvb', 'sgu_ln_g', 'sgu_ln_b', 'sgu_w', 'sgu_b', 'w_out_e',
           'w_in_o', 'hg_lb', 'hg_gnorm', 'w_out_o', 'ln1_g', 'ln1_b', 'w_ff1', 'w_ff2', 'ln2_g', 'ln2_b']


def kernel(x, positions, w_in_e, mla_gq, mla_gkv, w_qb, w_kvb, sgu_ln_g, sgu_ln_b, sgu_w, sgu_b, w_out_e, w_in_o, hg_lb, hg_gnorm, w_out_o, ln1_g, ln1_b, w_ff1, w_ff2, ln2_g, ln2_b, loss_target, m_w_in_e, m_mla_gq, m_mla_gkv, m_w_qb, m_w_kvb, m_sgu_ln_g, m_sgu_ln_b, m_sgu_w, m_sgu_b, m_w_out_e, m_w_in_o, m_hg_lb, m_hg_gnorm, m_w_out_o, m_ln1_g, m_ln1_b, m_w_ff1, m_w_ff2, m_ln2_g, m_ln2_b, v_w_in_e, v_mla_gq, v_mla_gkv, v_w_qb, v_w_kvb, v_sgu_ln_g, v_sgu_ln_b, v_sgu_w, v_sgu_b, v_w_out_e, v_w_in_o, v_hg_lb, v_hg_gnorm, v_w_out_o, v_ln1_g, v_ln1_b, v_w_ff1, v_w_ff2, v_ln2_g, v_ln2_b):
    args = dict(locals())
    w = {n: args[n] for n in WEIGHTS}
    m = {n: args["m_" + n] for n in WEIGHTS}
    v = {n: args["v_" + n] for n in WEIGHTS}
    cx, cy, cc = _mesh_pos()
    chip = 2 * cx + cy

    odd_shard = _odd_rows({"w_out_e": w_out_e[0], "w_qb": w_qb[0], "w_kvb": w_kvb[0]}, BF16, ODD_W_PARTS, ROWS_ODD_W,
                          gnorm=hg_gnorm)
    ids = _mesh_ids()
    placed = [_place_shard(w_in_e[0], ids, name="place_shard_in_e"), _place_shard(odd_shard, ids, name="place_shard_odd")]
    pieces = [(w_ff1, 0, 0, 0), (w_ff2, 0, 0, 1024), (w_in_o, 0, 1, 0), (w_out_o, 0, 1, 1024),
              (w_ff1, 1, 2, 0), (w_ff2, 1, 2, 1024)]
    *big_bufs, odd_a, odd_b = _place_weights(pieces, (2048, 1280, 2048), ids, plan=_plan_gather_ici(placed))
    gathered = _run_plan(_plan_gather_forward([odd_a, odd_b]), name="odd_gather_forward")
    per_chip = [_odd_unrows(gathered[1][j], ODD_W_PARTS, with_gnorm=True) for j in range(4)]
    odd = {"w_out_e": jnp.concatenate([p["w_out_e"] for p in per_chip], axis=0),
           "w_in_e": gathered[0]}
    for n in ("w_qb", "w_kvb"):
        odd[n] = jnp.concatenate([p[n] for p in per_chip], axis=1)
    small = {n: w[n] for n in SMALL_LAYOUT if n != "hg_gnorm"}
    small["hg_gnorm"] = jnp.concatenate([p["hg_gnorm"] for p in per_chip], axis=1)
    grad_x, g_l0, g_l1, godd, small_vec, small_others = _local_step(
        x[0], positions[0], loss_target[0], odd, big_bufs, small, True)

    sums = [_sum_chips(pair, parts, ids, name=f"odd_chip_sum_{k}") for k, (pair, parts) in enumerate(zip(*godd))]
    g_in_e, g_rest = _run_plan(_join_plans([_plan_pair_gather(s) for s in sums]), name="odd_pair_gather")
    g_odd = _odd_unrows(g_rest.reshape(ROWS_ODD_W, 1024), ODD_W_PARTS)
    g_odd["w_in_e"] = g_in_e.reshape(D, 392)

    to_kernel = lambda d: {n: d[n].reshape(SMALL_LAYOUT[n][3]) for n in SMALL_LAYOUT}
    first_row, small_out = _small_update(small_vec, small_others, ids, to_kernel(w), to_kernel(m), to_kernel(v))
    loss = first_row[0, 1023]
    grads, delta, new_m, new_v = {}, {}, {}, {}
    for n, res in small_out.items():
        grads[n], delta[n], new_m[n], new_v[n] = (r.reshape(w[n].shape) for r in res)

    for n, bufs_, row0 in (("w_ff1", [g_l0, g_l1], 0), ("w_ff2", [g_l0, g_l1], 1024), ("w_in_o", [g_l1], 2048),
                           ("w_out_o", [g_l1], 3072)):
        grads[n], delta[n], new_m[n], new_v[n] = _adamw_rows(w[n], m[n], v[n], bufs_, row0, name=f"adamw_{n}")
    for n, _ in ODD_PARTS:
        if n == "w_in_e":
            res = _adamw_rows(w[n][0].T[None], m[n][0].T[None], v[n][0].T[None], [g_odd[n].T], 0, name=f"adamw_{n}")
            grads[n], delta[n], new_m[n], new_v[n] = (r[0].T[None] for r in res)
            continue
        if n == "w_out_e":
            grads[n], delta[n], new_m[n], new_v[n] = _adamw_rows(w[n], m[n], v[n], [g_odd[n]], 0, name=f"adamw_{n}")
            continue
        grads[n] = g_odd[n][None]
        d_, m_, v_ = _adamw(w[n][0], g_odd[n], m[n][0], v[n][0], name=f"adamw_{n}")
        delta[n], new_m[n], new_v[n] = d_[None], m_[None], v_[None]

    return (loss, grad_x[None], *[grads[n] for n in WEIGHTS], *[delta[n] for n in WEIGHTS],
            *[new_m[n] for n in WEIGHTS], *[new_v[n] for n in WEIGHTS])
```
